```python
import jax, jax.numpy as jnp
from jax import lax
import numpy as np

D_MODEL = 2048
BATCH = 8
SEQ = 2048
DEPTH = 1

CHUNK = 64
N_PAST_CHUNKS = 8
BAND = (N_PAST_CHUNKS + 1) * CHUNK
ATTN_WIDTH = D_MODEL // 2
ATTN_HEAD_DIM = 64
ATTN_HEADS = ATTN_WIDTH // ATTN_HEAD_DIM
MAX_REL = 4 * CHUNK
REC_WIDTH = D_MODEL - ATTN_WIDTH
REC_HEAD_DIM = 128
REC_HEADS = REC_WIDTH // REC_HEAD_DIM
MIX_WIDTH = ATTN_WIDTH + REC_WIDTH
IN_PROJ_WIDTH = 3 * ATTN_WIDTH + 4 * REC_WIDTH
D_FF = ((8 * D_MODEL // 3 + 255) // 256) * 256
ALPHA = (2 * DEPTH) ** 0.25
BETA = (8 * DEPTH) ** -0.25
EPS = 1e-5
N_MOD = 6

kernel_name = "hybrid_chunkattn_hgrn2_deepnorm_adaln"


def _layernorm(x, g=None, b=None):
    xf = x.astype(jnp.float32)
    mu = jnp.mean(xf, axis=-1, keepdims=True)
    var = jnp.mean(jnp.square(xf - mu), axis=-1, keepdims=True)
    y = (xf - mu) * lax.rsqrt(var + EPS)
    if g is not None:
        y = y * g.astype(jnp.float32) + b.astype(jnp.float32)
    return y.astype(x.dtype)


def _rmsnorm(x, g):
    xf = x.astype(jnp.float32)
    y = xf * lax.rsqrt(jnp.mean(jnp.square(xf), axis=-1, keepdims=True) + EPS)
    return y * g.astype(jnp.float32)


def _chunk_attention(q, k, v, rel_bias):
    B, T, H, Dh = q.shape
    n_chunks = T // CHUNK
    pad = N_PAST_CHUNKS * CHUNK
    k_pad = jnp.pad(k, ((0, 0), (pad, 0), (0, 0), (0, 0)))
    v_pad = jnp.pad(v, ((0, 0), (pad, 0), (0, 0), (0, 0)))
    rel = jnp.arange(CHUNK)[:, None] + pad - jnp.arange(BAND)[None, :]
    idx = jnp.clip(rel, -MAX_REL, MAX_REL) + MAX_REL
    bias = rel_bias[:, idx].astype(jnp.float32)
    q_chunks = q.reshape(B, n_chunks, CHUNK, H, Dh).transpose(1, 0, 2, 3, 4)
    scale = Dh ** -0.5
    band_pos = jnp.arange(BAND)

    def one_chunk(args):
        n, qc = args
        kb = lax.dynamic_slice_in_dim(k_pad, n * CHUNK, BAND, axis=1)
        vb = lax.dynamic_slice_in_dim(v_pad, n * CHUNK, BAND, axis=1)
        s = jnp.einsum('bthd,bjhd->bhtj', qc, kb).astype(jnp.float32) * scale + bias
        valid = band_pos >= (N_PAST_CHUNKS - n) * CHUNK
        s = jnp.where(valid, s, -jnp.inf)
        p = jax.nn.softmax(s, axis=-1).astype(vb.dtype)
        return jnp.einsum('bhtj,bjhd->bthd', p, vb)

    out = lax.map(one_chunk, (jnp.arange(n_chunks), q_chunks))
    return out.transpose(1, 0, 2, 3, 4).reshape(B, T, H, Dh)


def _hgrn2(q, f_logit, i, lower_bound):
    B, T, H, Dk = q.shape
    lb = lower_bound.reshape(H, Dk).astype(jnp.float32)
    f = lb + (1.0 - lb) * jax.nn.sigmoid(f_logit.astype(jnp.float32))
    log_f = jnp.log(f)
    k = 1.0 - f
    q = jax.nn.silu(q.astype(jnp.float32))
    i = i.astype(jnp.float32)
    Dv = i.shape[-1]
    n_chunks = T // CHUNK

    def to_chunks(a):
        return a.reshape(B, n_chunks, CHUNK, H, a.shape[-1]).transpose(1, 0, 3, 2, 4)

    causal = jnp.tril(jnp.ones((CHUNK, CHUNK), dtype=bool))[:, :, None]

    def step(S, inp):
        qc, kc, ic, gc = inp
        b = jnp.cumsum(gc, axis=2)
        diff = b[:, :, :, None, :] - b[:, :, None, :, :]
        decay = jnp.exp(jnp.where(causal, diff, -jnp.inf))
        scores = jnp.einsum('bhtd,bhtsd,bhsd->bhts', qc, decay, kc)
        o = (jnp.einsum('bhts,bhse->bhte', scores, ic)
             + jnp.einsum('bhtd,bhde->bhte', qc * jnp.exp(b), S))
        b_last = b[:, :, -1:, :]
        S = (jnp.exp(b_last[:, :, 0, :, None]) * S
             + jnp.einsum('bhsd,bhse->bhde', kc * jnp.exp(b_last - b), ic))
        return S, o

    S0 = jnp.zeros((B, H, Dk, Dv), jnp.float32)
    _, o = lax.scan(step, S0, (to_chunks(q), to_chunks(k), to_chunks(i), to_chunks(log_f)))
    return o.transpose(1, 0, 3, 2, 4).reshape(B, T, H, Dv)


def _token_mixer(h, w_in, rel_bias, attn_gain, lower_bound, gnorm_gain, w_o):
    B, T, _ = h.shape
    proj = h @ w_in
    splits = [ATTN_WIDTH, 2 * ATTN_WIDTH, 3 * ATTN_WIDTH,
              3 * ATTN_WIDTH + REC_WIDTH, 3 * ATTN_WIDTH + 2 * REC_WIDTH,
              3 * ATTN_WIDTH + 3 * REC_WIDTH]
    q_a, k_a, v_a, q_b, f_b, i_b, g_b = jnp.split(proj, splits, axis=-1)
    heads_a = lambda a: a.reshape(B, T, ATTN_HEADS, ATTN_HEAD_DIM)
    heads_b = lambda a: a.reshape(B, T, REC_HEADS, REC_HEAD_DIM)
    o_a = _chunk_attention(heads_a(q_a), heads_a(k_a), heads_a(v_a), rel_bias)
    o_a = _rmsnorm(o_a, attn_gain.reshape(ATTN_HEADS, ATTN_HEAD_DIM)).reshape(B, T, ATTN_WIDTH)
    o_b = _hgrn2(heads_b(q_b), heads_b(f_b), heads_b(i_b), lower_bound)
    o_b = _rmsnorm(o_b, gnorm_gain).reshape(B, T, REC_WIDTH)
    o_b = o_b * jax.nn.silu(g_b.astype(jnp.float32))
    out = jnp.concatenate([o_a, o_b], axis=-1).astype(h.dtype)
    return out @ w_o


def _swiglu(h, w_ffn_in, w_ffn_out):
    gate, up = jnp.split(h @ w_ffn_in, 2, axis=-1)
    return (jax.nn.silu(gate) * up) @ w_ffn_out


def _fwd_setup_inputs(seed: int = 0) -> dict:
    key = jax.random.key(seed)
    ks = jax.random.split(key, 20)
    f32 = jnp.float32
    nrm = lambda k, shape, s: jax.random.normal(k, shape, f32) * s
    return {
        "x": nrm(ks[0], (BATCH, SEQ, D_MODEL), 1.0),
        "c": nrm(ks[1], (BATCH, D_MODEL), 1.0),
        "w_ada": nrm(ks[2], (DEPTH, D_MODEL, N_MOD * D_MODEL), 0.5 * D_MODEL ** -0.5),
        "b_ada": nrm(ks[3], (DEPTH, N_MOD * D_MODEL), 0.01),
        "w_in": nrm(ks[4], (DEPTH, D_MODEL, IN_PROJ_WIDTH), D_MODEL ** -0.5),
        "rel_bias": nrm(ks[5], (DEPTH, ATTN_HEADS, 2 * MAX_REL + 1), 0.1),
        "attn_norm_g": 1.0 + nrm(ks[6], (DEPTH, ATTN_WIDTH), 0.02),
        "lb_logits": nrm(ks[7], (DEPTH + 1, REC_WIDTH), 0.1),
        "gnorm_g": 1.0 + nrm(ks[8], (DEPTH, REC_HEAD_DIM), 0.02),
        "w_o": nrm(ks[9], (DEPTH, MIX_WIDTH, D_MODEL), BETA * MIX_WIDTH ** -0.5),
        "ln1_g": 1.0 + nrm(ks[10], (DEPTH, D_MODEL), 0.02),
        "ln1_b": nrm(ks[11], (DEPTH, D_MODEL), 0.01),
        "w_ffn_in": nrm(ks[12], (DEPTH, D_MODEL, 2 * D_FF), D_MODEL ** -0.5),
        "w_ffn_out": nrm(ks[13], (DEPTH, D_FF, D_MODEL), BETA * D_FF ** -0.5),
        "ln2_g": 1.0 + nrm(ks[14], (DEPTH, D_MODEL), 0.02),
        "ln2_b": nrm(ks[15], (DEPTH, D_MODEL), 0.01),
    }


def _fwd_reference(x, c, w_ada, b_ada, w_in, rel_bias, attn_norm_g, lb_logits, gnorm_g, w_o,
              ln1_g, ln1_b, w_ffn_in, w_ffn_out, ln2_g, ln2_b):
    lower_bounds = jnp.cumsum(jax.nn.softmax(lb_logits.astype(jnp.float32), axis=0), axis=0)
    c_act = jax.nn.silu(c)
    for layer in range(DEPTH):
        mod = c_act @ w_ada[layer] + b_ada[layer]
        shift1, scale1, gate1, shift2, scale2, gate2 = [m[:, None, :] for m in jnp.split(mod, N_MOD, axis=-1)]
        h = _layernorm(x) * (1.0 + scale1) + shift1
        mix = _token_mixer(h, w_in[layer], rel_bias[layer], attn_norm_g[layer],
                           lower_bounds[layer], gnorm_g[layer], w_o[layer])
        x = _layernorm(ALPHA * x + gate1 * mix, ln1_g[layer], ln1_b[layer])
        h = _layernorm(x) * (1.0 + scale2) + shift2
        x = _layernorm(ALPHA * x + gate2 * _swiglu(h, w_ffn_in[layer], w_ffn_out[layer]),
                       ln2_g[layer], ln2_b[layer])
    return x


import jax as _jax
import jax.numpy as _jnp

TWIN_FORMAT = 'train_step'
FWD_PARAMS = ['x', 'c', 'w_ada', 'b_ada', 'w_in', 'rel_bias', 'attn_norm_g', 'lb_logits', 'gnorm_g', 'w_o', 'ln1_g', 'ln1_b', 'w_ffn_in', 'w_ffn_out', 'ln2_g', 'ln2_b']
TWIN_WEIGHTS = ['w_ada', 'b_ada', 'w_in', 'rel_bias', 'attn_norm_g', 'lb_logits', 'gnorm_g', 'w_o', 'ln1_g', 'ln1_b', 'w_ffn_in', 'w_ffn_out', 'ln2_g', 'ln2_b']
TWIN_DIFF_INPUT = 'x'
TWIN_INPUTS = ['x', 'c', 'w_ada', 'b_ada', 'w_in', 'rel_bias', 'attn_norm_g', 'lb_logits', 'gnorm_g', 'w_o', 'ln1_g', 'ln1_b', 'w_ffn_in', 'w_ffn_out', 'ln2_g', 'ln2_b', 'loss_target', 'm_w_ada', 'm_b_ada', 'm_w_in', 'm_rel_bias', 'm_attn_norm_g', 'm_lb_logits', 'm_gnorm_g', 'm_w_o', 'm_ln1_g', 'm_ln1_b', 'm_w_ffn_in', 'm_w_ffn_out', 'm_ln2_g', 'm_ln2_b', 'v_w_ada', 'v_b_ada', 'v_w_in', 'v_rel_bias', 'v_attn_norm_g', 'v_lb_logits', 'v_gnorm_g', 'v_w_o', 'v_ln1_g', 'v_ln1_b', 'v_w_ffn_in', 'v_w_ffn_out', 'v_ln2_g', 'v_ln2_b']
TWIN_OUTPUTS = ['loss', 'grad_x', 'grad_w_ada', 'grad_b_ada', 'grad_w_in', 'grad_rel_bias', 'grad_attn_norm_g', 'grad_lb_logits', 'grad_gnorm_g', 'grad_w_o', 'grad_ln1_g', 'grad_ln1_b', 'grad_w_ffn_in', 'grad_w_ffn_out', 'grad_ln2_g', 'grad_ln2_b', 'delta_w_ada', 'delta_b_ada', 'delta_w_in', 'delta_rel_bias', 'delta_attn_norm_g', 'delta_lb_logits', 'delta_gnorm_g', 'delta_w_o', 'delta_ln1_g', 'delta_ln1_b', 'delta_w_ffn_in', 'delta_w_ffn_out', 'delta_ln2_g', 'delta_ln2_b', 'new_m_w_ada', 'new_m_b_ada', 'new_m_w_in', 'new_m_rel_bias', 'new_m_attn_norm_g', 'new_m_lb_logits', 'new_m_gnorm_g', 'new_m_w_o', 'new_m_ln1_g', 'new_m_ln1_b', 'new_m_w_ffn_in', 'new_m_w_ffn_out', 'new_m_ln2_g', 'new_m_ln2_b', 'new_v_w_ada', 'new_v_b_ada', 'new_v_w_in', 'new_v_rel_bias', 'new_v_attn_norm_g', 'new_v_lb_logits', 'new_v_gnorm_g', 'new_v_w_o', 'new_v_ln1_g', 'new_v_ln1_b', 'new_v_w_ffn_in', 'new_v_w_ffn_out', 'new_v_ln2_g', 'new_v_ln2_b']
TWIN_LEAF_KINDS = {'loss': 'loss', 'grad_x': 'grad_x', 'grad_w_ada': 'grad_w', 'grad_b_ada': 'grad_w', 'grad_w_in': 'grad_w', 'grad_rel_bias': 'grad_w', 'grad_attn_norm_g': 'grad_w', 'grad_lb_logits': 'grad_w', 'grad_gnorm_g': 'grad_w', 'grad_w_o': 'grad_w', 'grad_ln1_g': 'grad_w', 'grad_ln1_b': 'grad_w', 'grad_w_ffn_in': 'grad_w', 'grad_w_ffn_out': 'grad_w', 'grad_ln2_g': 'grad_w', 'grad_ln2_b': 'grad_w', 'delta_w_ada': 'delta_w', 'delta_b_ada': 'delta_w', 'delta_w_in': 'delta_w', 'delta_rel_bias': 'delta_w', 'delta_attn_norm_g': 'delta_w', 'delta_lb_logits': 'delta_w', 'delta_gnorm_g': 'delta_w', 'delta_w_o': 'delta_w', 'delta_ln1_g': 'delta_w', 'delta_ln1_b': 'delta_w', 'delta_w_ffn_in': 'delta_w', 'delta_w_ffn_out': 'delta_w', 'delta_ln2_g': 'delta_w', 'delta_ln2_b': 'delta_w', 'new_m_w_ada': 'new_m', 'new_m_b_ada': 'new_m', 'new_m_w_in': 'new_m', 'new_m_rel_bias': 'new_m', 'new_m_attn_norm_g': 'new_m', 'new_m_lb_logits': 'new_m', 'new_m_gnorm_g': 'new_m', 'new_m_w_o': 'new_m', 'new_m_ln1_g': 'new_m', 'new_m_ln1_b': 'new_m', 'new_m_w_ffn_in': 'new_m', 'new_m_w_ffn_out': 'new_m', 'new_m_ln2_g': 'new_m', 'new_m_ln2_b': 'new_m', 'new_v_w_ada': 'new_v', 'new_v_b_ada': 'new_v', 'new_v_w_in': 'new_v', 'new_v_rel_bias': 'new_v', 'new_v_attn_norm_g': 'new_v', 'new_v_lb_logits': 'new_v', 'new_v_gnorm_g': 'new_v', 'new_v_w_o': 'new_v', 'new_v_ln1_g': 'new_v', 'new_v_ln1_b': 'new_v', 'new_v_w_ffn_in': 'new_v', 'new_v_w_ffn_out': 'new_v', 'new_v_ln2_g': 'new_v', 'new_v_ln2_b': 'new_v'}


def _forward(args):
    return _fwd_reference(*[args[k] for k in FWD_PARAMS])


def _output_shape():
    out = _jax.eval_shape(lambda: _forward(_fwd_setup_inputs(0)))
    return out.shape, out.dtype

N_MICROBATCH = 1
ADAM_LR = 0.001
ADAM_B1 = 0.9
ADAM_B2 = 0.999
ADAM_EPS = 1e-08
ADAM_WD = 0.01
ADAM_STEP = 10
PER_EXAMPLE_BATCH_AXIS = {'x': 0, 'c': 0, 'loss_target': 0}
SHARED_INPUTS = []
_WEIGHT_DTYPES = {'w_ada': _jnp.float32, 'b_ada': _jnp.float32, 'w_in': _jnp.float32, 'rel_bias': _jnp.float32, 'attn_norm_g': _jnp.float32, 'lb_logits': _jnp.float32, 'gnorm_g': _jnp.float32, 'w_o': _jnp.float32, 'ln1_g': _jnp.float32, 'ln1_b': _jnp.float32, 'w_ffn_in': _jnp.float32, 'w_ffn_out': _jnp.float32, 'ln2_g': _jnp.float32, 'ln2_b': _jnp.float32}
MOMENT_SCALE = {'w_ada': 1.140574e-02, 'b_ada': 1.879042e-02, 'w_in': 5.505261e-03, 'rel_bias': 1.444267e-03, 'attn_norm_g': 1.008920e-02, 'lb_logits': 4.618047e-04, 'gnorm_g': 2.011170e-02, 'w_o': 1.457766e-02, 'ln1_g': 2.902163e-01, 'ln1_b': 8.655859e-02, 'w_ffn_in': 4.159217e-03, 'w_ffn_out': 1.140842e-02, 'ln2_g': 7.998986e+00, 'ln2_b': 3.978661e-01}


def _to_microbatches(a, axis):
    t = _jnp.moveaxis(a, axis, 0)
    t = t.reshape((N_MICROBATCH, t.shape[0] // N_MICROBATCH) + t.shape[1:])
    return _jnp.moveaxis(t, 1, axis + 1)


def setup_inputs(seed: int = 0) -> dict:
    inp = _fwd_setup_inputs(seed)
    key = _jax.random.fold_in(_jax.random.key(seed), 7919)
    shape, _ = _output_shape()
    out = dict(inp)
    out["loss_target"] = _jax.random.normal(_jax.random.fold_in(key, 0), shape, _jnp.float32)
    for i, name in enumerate(TWIN_WEIGHTS):
        w = inp[name].astype(_jnp.float32)
        if MOMENT_SCALE is None:
            s = _jnp.sqrt(_jnp.mean(_jnp.square(w)) + 1e-30)
        else:
            s = MOMENT_SCALE[name]
        km, kv = _jax.random.split(_jax.random.fold_in(key, i + 1))
        out[name] = w
        out["m_" + name] = s * _jax.random.normal(km, w.shape, _jnp.float32)
        out["v_" + name] = (s * s) * _jax.random.uniform(kv, w.shape, _jnp.float32, 0.5, 1.5)
    if N_MICROBATCH > 1:
        for name, axis in PER_EXAMPLE_BATCH_AXIS.items():
            out[name] = _to_microbatches(out[name], axis)
    return {'x': out['x'], 'c': out['c'], 'w_ada': out['w_ada'], 'b_ada': out['b_ada'], 'w_in': out['w_in'], 'rel_bias': out['rel_bias'], 'attn_norm_g': out['attn_norm_g'], 'lb_logits': out['lb_logits'], 'gnorm_g': out['gnorm_g'], 'w_o': out['w_o'], 'ln1_g': out['ln1_g'], 'ln1_b': out['ln1_b'], 'w_ffn_in': out['w_ffn_in'], 'w_ffn_out': out['w_ffn_out'], 'ln2_g': out['ln2_g'], 'ln2_b': out['ln2_b'], 'loss_target': out['loss_target'], 'm_w_ada': out['m_w_ada'], 'm_b_ada': out['m_b_ada'], 'm_w_in': out['m_w_in'], 'm_rel_bias': out['m_rel_bias'], 'm_attn_norm_g': out['m_attn_norm_g'], 'm_lb_logits': out['m_lb_logits'], 'm_gnorm_g': out['m_gnorm_g'], 'm_w_o': out['m_w_o'], 'm_ln1_g': out['m_ln1_g'], 'm_ln1_b': out['m_ln1_b'], 'm_w_ffn_in': out['m_w_ffn_in'], 'm_w_ffn_out': out['m_w_ffn_out'], 'm_ln2_g': out['m_ln2_g'], 'm_ln2_b': out['m_ln2_b'], 'v_w_ada': out['v_w_ada'], 'v_b_ada': out['v_b_ada'], 'v_w_in': out['v_w_in'], 'v_rel_bias': out['v_rel_bias'], 'v_attn_norm_g': out['v_attn_norm_g'], 'v_lb_logits': out['v_lb_logits'], 'v_gnorm_g': out['v_gnorm_g'], 'v_w_o': out['v_w_o'], 'v_ln1_g': out['v_ln1_g'], 'v_ln1_b': out['v_ln1_b'], 'v_w_ffn_in': out['v_w_ffn_in'], 'v_w_ffn_out': out['v_w_ffn_out'], 'v_ln2_g': out['v_ln2_g'], 'v_ln2_b': out['v_ln2_b']}


def _loss(weights, diff, rest, loss_target):
    with _jax.named_scope("forward"):
        args = {**rest, TWIN_DIFF_INPUT: diff, **{k: w.astype(_WEIGHT_DTYPES[k]) for k, w in weights.items()}}
        y = _forward(args)
    with _jax.named_scope("loss_head"):
        err = _jnp.square(y.astype(_jnp.float32) - loss_target)
        return 0.5 * _jnp.sum(_jnp.mean(err, axis=-1)) if err.ndim else 0.5 * err


def _adamw(w, g, m, v):
    m = ADAM_B1 * m + (1.0 - ADAM_B1) * g
    v = ADAM_B2 * v + (1.0 - ADAM_B2) * _jnp.square(g)
    m_hat = m / (1.0 - ADAM_B1 ** ADAM_STEP)
    v_hat = v / (1.0 - ADAM_B2 ** ADAM_STEP)
    delta = -ADAM_LR * (m_hat / (_jnp.sqrt(v_hat) + ADAM_EPS) + ADAM_WD * w)
    return delta, m, v


def reference(x, c, w_ada, b_ada, w_in, rel_bias, attn_norm_g, lb_logits, gnorm_g, w_o, ln1_g, ln1_b, w_ffn_in, w_ffn_out, ln2_g, ln2_b, loss_target, m_w_ada, m_b_ada, m_w_in, m_rel_bias, m_attn_norm_g, m_lb_logits, m_gnorm_g, m_w_o, m_ln1_g, m_ln1_b, m_w_ffn_in, m_w_ffn_out, m_ln2_g, m_ln2_b, v_w_ada, v_b_ada, v_w_in, v_rel_bias, v_attn_norm_g, v_lb_logits, v_gnorm_g, v_w_o, v_ln1_g, v_ln1_b, v_w_ffn_in, v_w_ffn_out, v_ln2_g, v_ln2_b):
    given = dict(x=x, c=c, w_ada=w_ada, b_ada=b_ada, w_in=w_in, rel_bias=rel_bias, attn_norm_g=attn_norm_g, lb_logits=lb_logits, gnorm_g=gnorm_g, w_o=w_o, ln1_g=ln1_g, ln1_b=ln1_b, w_ffn_in=w_ffn_in, w_ffn_out=w_ffn_out, ln2_g=ln2_g, ln2_b=ln2_b, loss_target=loss_target, m_w_ada=m_w_ada, m_b_ada=m_b_ada, m_w_in=m_w_in, m_rel_bias=m_rel_bias, m_attn_norm_g=m_attn_norm_g, m_lb_logits=m_lb_logits, m_gnorm_g=m_gnorm_g, m_w_o=m_w_o, m_ln1_g=m_ln1_g, m_ln1_b=m_ln1_b, m_w_ffn_in=m_w_ffn_in, m_w_ffn_out=m_w_ffn_out, m_ln2_g=m_ln2_g, m_ln2_b=m_ln2_b, v_w_ada=v_w_ada, v_b_ada=v_b_ada, v_w_in=v_w_in, v_rel_bias=v_rel_bias, v_attn_norm_g=v_attn_norm_g, v_lb_logits=v_lb_logits, v_gnorm_g=v_gnorm_g, v_w_o=v_w_o, v_ln1_g=v_ln1_g, v_ln1_b=v_ln1_b, v_w_ffn_in=v_w_ffn_in, v_w_ffn_out=v_w_ffn_out, v_ln2_g=v_ln2_g, v_ln2_b=v_ln2_b)
    weights = {n: given[n] for n in TWIN_WEIGHTS}
    shared = {n: given[n] for n in SHARED_INPUTS}
    per_example = {n: given[n] for n in ['x', 'c']}
    grad_fn = _jax.value_and_grad(_loss, argnums=(0, 1))

    def one_microbatch(ex, loss_target):
        ex = dict(ex)
        diff = ex.pop(TWIN_DIFF_INPUT)
        return grad_fn(weights, diff, {**shared, **ex}, loss_target)

    if N_MICROBATCH == 1:
        loss, (grad_w, grad_x) = one_microbatch(per_example, given["loss_target"])
    else:
        def body(carry, xs):
            loss_sum, grad_sum = carry
            l_k, (gw_k, gx_k) = one_microbatch(xs[0], xs[1])
            with _jax.named_scope("update"):
                return (loss_sum + l_k, _jax.tree.map(_jnp.add, grad_sum, gw_k)), gx_k

        init = (_jnp.zeros((), _jnp.float32), _jax.tree.map(_jnp.zeros_like, weights))
        (loss, grad_w), grad_x = _jax.lax.scan(body, init, (per_example, given["loss_target"]))
    with _jax.named_scope("update"):
        delta_w, new_m, new_v = {}, {}, {}
        for n in TWIN_WEIGHTS:
            delta_w[n], new_m[n], new_v[n] = _adamw(weights[n], grad_w[n], given["m_" + n], given["v_" + n])
    return (loss, grad_x, *[grad_w[n] for n in TWIN_WEIGHTS], *[delta_w[n] for n in TWIN_WEIGHTS],
            *[new_m[n] for n in TWIN_WEIGHTS], *[new_v[n] for n in TWIN_WEIGHTS])
```

```python
import jax
import jax.numpy as jnp
from jax import lax
from jax.experimental import pallas as pl
from jax.experimental.pallas import tpu as pltpu

F32 = jnp.float32
BF16 = jnp.bfloat16
MESH = pl.DeviceIdType.MESH
HIGHEST = lax.Precision.HIGHEST

CHUNK = 64
N_PAST = 8
KPAD = (N_PAST + 1) * CHUNK
BAND = (N_PAST + 2) * CHUNK
HD_A = 64
HD_B = 128
SUB = 16
MAX_REL = 256
EPS = 1e-5
ALPHA = 2.0 ** 0.25
LR, B1, B2, ADAM_EPS, WD, STEP = 1e-3, 0.9, 0.999, 1e-8, 0.01, 10
N_CHIPS = 4
N_DEV = 8
NEG = -1e30
TILE_BYTES = 3 << 19

NN = ((1,), (0,))
NT = ((1,), (1,))
TN = ((0,), (0,))


def _dot(a, b, dims=NN, precision=None):
    return lax.dot_general(a, b, (dims, ((), ())), preferred_element_type=F32, precision=precision)


def _params(sem=None, vmem_mb=None, **kw):
    return pltpu.CompilerParams(dimension_semantics=sem,
                                vmem_limit_bytes=None if vmem_mb is None else vmem_mb << 20, **kw)


def _row_tile(rows, cols):
    for cand in (512, 256, 128, 64, 32, 16, 8):
        if rows % cand == 0 and cand * cols * 4 <= TILE_BYTES:
            return cand
    raise ValueError((rows, cols))


def _place():
    return lax.axis_index("x"), lax.axis_index("y"), lax.axis_index("c")


def _flip(v, bit):
    return 1 - v if bit else v


def _mm(a, b, *, grid, a_spec, b_spec, o_spec, o_shape, o_dtype, dims, acc_shape, name, vmem_mb=48):
    nk = grid[2]

    def body(a_ref, b_ref, o_ref, *scratch):
        part = _dot(a_ref[...], b_ref[...], dims)
        if nk == 1:
            o_ref[...] = part.astype(o_ref.dtype)
            return
        acc_ref, = scratch
        k = pl.program_id(2)

        @pl.when(k == 0)
        def _():
            acc_ref[...] = part

        @pl.when(k > 0)
        def _():
            acc_ref[...] += part

        @pl.when(k == nk - 1)
        def _():
            o_ref[...] = acc_ref[...].astype(o_ref.dtype)

    return pl.pallas_call(
        body, name=name, grid=grid, in_specs=[a_spec, b_spec], out_specs=o_spec,
        out_shape=jax.ShapeDtypeStruct(o_shape, o_dtype),
        scratch_shapes=[] if nk == 1 else [pltpu.VMEM(acc_shape, F32)],
        compiler_params=_params(("parallel", "parallel", "arbitrary"), vmem_mb),
    )(a, b)


def _mm_nn(a, w, *, tm, tn, tk, name):
    T, K = a.shape
    Q, _, Ns = w.shape
    nbs = Ns // tn
    tm = min(tm, T)
    return _mm(a, w, grid=(T // tm, Q * nbs, K // tk),
               a_spec=pl.BlockSpec((tm, tk), lambda i, j, k: (i, k)),
               b_spec=pl.BlockSpec((None, tk, tn), lambda i, j, k: (j // nbs, k, j % nbs)),
               o_spec=pl.BlockSpec((tm, tn), lambda i, j, k: (i, j)),
               o_shape=(T, Q * Ns), o_dtype=F32, dims=NN, acc_shape=(tm, tn), name=name)


def _mm_nt(g, w, *, tm, to, tn, name):
    T = g.shape[0]
    Q, K, Ns = w.shape
    nbs = Ns // tn
    tm = min(tm, T)
    return _mm(g, w, grid=(T // tm, K // to, Q * nbs),
               a_spec=pl.BlockSpec((tm, tn), lambda i, j, n: (i, n)),
               b_spec=pl.BlockSpec((None, to, tn), lambda i, j, n: (n // nbs, j, n % nbs)),
               o_spec=pl.BlockSpec((tm, to), lambda i, j, n: (i, j)),
               o_shape=(T, K), o_dtype=F32, dims=NT, acc_shape=(tm, to), name=name)


def _mm_tn(a, g, *, q, tk, tn, tt, name):
    T, K = a.shape
    Ns = g.shape[1] // q
    nbs = Ns // tn
    return _mm(a, g, grid=(K // tk, q * nbs, T // tt),
               a_spec=pl.BlockSpec((tt, tk), lambda i, j, t: (t, i)),
               b_spec=pl.BlockSpec((tt, tn), lambda i, j, t: (t, j)),
               o_spec=pl.BlockSpec((None, tk, tn), lambda i, j, t: (j // nbs, i, j % nbs)),
               o_shape=(q, K, Ns), o_dtype=F32, dims=TN, acc_shape=(tk, tn), name=name)


def _ln(u):
    mu = jnp.mean(u, axis=-1, keepdims=True)
    d = u - mu
    r = lax.rsqrt(jnp.mean(d * d, axis=-1, keepdims=True) + EPS)
    return d * r, r


def _ln_bwd(dy, un, r):
    return r * (dy - jnp.mean(dy, axis=-1, keepdims=True) - un * jnp.mean(dy * un, axis=-1, keepdims=True))


def _colsum(v):
    return jnp.sum(v, axis=0, keepdims=True)


def _rowwise(name, fn, bigs, vecs, out_dtypes, n_acc, tm=128):
    T, D = bigs[0].shape
    nb, nv, no = len(bigs), len(vecs), len(out_dtypes)

    def body(*refs):
        outs, accs = fn([r[...] for r in refs[:nb]], [r[...] for r in refs[nb:nb + nv]])
        for r, o in zip(refs[nb + nv:nb + nv + no], outs):
            r[...] = o.astype(r.dtype)
        if n_acc:
            acc_ref = refs[nb + nv + no]

            @pl.when(pl.program_id(0) == 0)
            def _():
                acc_ref[...] = jnp.zeros_like(acc_ref)

            for row, a in enumerate(accs):
                acc_ref[row:row + 1, :] += a

    big_spec = pl.BlockSpec((tm, D), lambda i: (i, 0))
    vec_spec = pl.BlockSpec((1, D), lambda i: (0, 0))
    out_shape = [jax.ShapeDtypeStruct((T, D), dt) for dt in out_dtypes]
    out_specs = [big_spec] * no
    if n_acc:
        out_shape.append(jax.ShapeDtypeStruct((8, D), F32))
        out_specs.append(pl.BlockSpec((8, D), lambda i: (0, 0)))
    return pl.pallas_call(
        body, name=name, grid=(T // tm,), in_specs=[big_spec] * nb + [vec_spec] * nv,
        out_specs=out_specs, out_shape=out_shape,
        compiler_params=_params(("arbitrary",), 48),
    )(*bigs, *vecs)


def _pre_mixer(x, scale1, shift1):
    def fn(b, v):
        xn, _ = _ln(b[0])
        return [xn * (1.0 + v[0]) + v[1]], []
    return _rowwise("pre_mixer", fn, [x], [scale1, shift1], [BF16], 0)[0]


def _post_mixer(mix, x, gate1, g1, b1, scale2, shift2):
    def fn(b, v):
        un1, _ = _ln(ALPHA * b[1] + v[0] * b[0])
        x1 = un1 * v[1] + v[2]
        xn1, _ = _ln(x1)
        return [x1, xn1 * (1.0 + v[3]) + v[4]], []
    return _rowwise("post_mixer", fn, [mix, x], [gate1, g1, b1, scale2, shift2], [F32, BF16], 0)


def _loss_head(f, x1, tgt, gate2, g2, b2):
    def fn(b, v):
        ff, xx, tt = b
        d_model = ff.shape[-1]
        un2, r2 = _ln(ALPHA * xx + v[0] * ff)
        err = un2 * v[1] + v[2] - tt
        dy = err * (1.0 / d_model)
        du2 = _ln_bwd(dy * v[1], un2, r2)
        return [du2, du2 * v[0]], [_colsum(dy * un2), _colsum(dy), _colsum(du2 * ff), _colsum(err * err)]
    return _rowwise("loss_head", fn, [f, x1, tgt], [gate2, g2, b2], [F32, BF16], 4)


def _mid_bwd(dh2, du2, x1, mix, x, gate1, g1, scale2):
    def fn(b, v):
        dh, du, xx1, mm, xx = b
        xn1, r1n = _ln(xx1)
        dx1 = ALPHA * du + _ln_bwd(dh * (1.0 + v[2]), xn1, r1n)
        un1, r1 = _ln(ALPHA * xx + v[0] * mm)
        du1 = _ln_bwd(dx1 * v[1], un1, r1)
        return [du1, du1 * v[0]], [_colsum(dh * xn1), _colsum(dh), _colsum(dx1 * un1), _colsum(dx1),
                                   _colsum(du1 * mm)]
    return _rowwise("mid_bwd", fn, [dh2, du2, x1, mix, x], [gate1, g1, scale2], [F32, BF16], 5)


def _first_bwd(dh1, du1, x, scale1):
    def fn(b, v):
        dh, du, xx = b
        xn, r0 = _ln(xx)
        return [ALPHA * du + _ln_bwd(dh * (1.0 + v[0]), xn, r0)], [_colsum(dh * xn), _colsum(dh)]
    return _rowwise("first_bwd", fn, [dh1, du1, x], [scale1], [F32], 2)


def _swiglu_fwd(ff, tm=256, tf=512):
    T, F2 = ff.shape
    F = F2 // 2
    nf = F // tf

    def body(g_ref, u_ref, a_ref):
        g = g_ref[...]
        a_ref[...] = (g * jax.nn.sigmoid(g) * u_ref[...]).astype(a_ref.dtype)

    return pl.pallas_call(
        body, name="swiglu_fwd", grid=(T // tm, nf),
        in_specs=[pl.BlockSpec((tm, tf), lambda i, j: (i, j)), pl.BlockSpec((tm, tf), lambda i, j: (i, j + nf))],
        out_specs=pl.BlockSpec((tm, tf), lambda i, j: (i, j)),
        out_shape=jax.ShapeDtypeStruct((T, F), BF16),
        compiler_params=_params(("parallel", "parallel")),
    )(ff, ff)


def _swiglu_bwd(da, ff, tm=256, tf=512):
    T, F = da.shape
    nf = F // tf

    def body(da_ref, g_ref, u_ref, dg_ref, du_ref):
        g = g_ref[...]
        d = da_ref[...]
        s = jax.nn.sigmoid(g)
        du_ref[...] = (d * g * s).astype(du_ref.dtype)
        dg_ref[...] = (d * u_ref[...] * s * (1.0 + g * (1.0 - s))).astype(dg_ref.dtype)

    blk = pl.BlockSpec((tm, tf), lambda i, j: (i, j))
    return pl.pallas_call(
        body, name="swiglu_bwd", grid=(T // tm, nf),
        in_specs=[blk, blk, pl.BlockSpec((tm, tf), lambda i, j: (i, j + nf))],
        out_specs=[blk, blk],
        out_shape=[jax.ShapeDtypeStruct((T, F), BF16)] * 2,
        compiler_params=_params(("parallel", "parallel")),
    )(da, ff, ff)


def _attn_probs(q_ref, kp_ref, bias_ref, n):
    start = pl.multiple_of(n * CHUNK, CHUNK)
    kb = kp_ref[pl.ds(start, BAND), :]
    s = _dot(q_ref[...], kb, NT) * (HD_A ** -0.5) + bias_ref[...]
    jpos = lax.broadcasted_iota(jnp.int32, s.shape, 1)
    first_valid = jnp.maximum(CHUNK, (N_PAST + 1 - n) * CHUNK)
    s = jnp.where(jpos >= first_valid, s, NEG)
    p = jnp.exp(s - jnp.max(s, axis=-1, keepdims=True))
    return p / jnp.sum(p, axis=-1, keepdims=True), start


def _attn_fwd(qh, kp, vp, bias, gain):
    H, T, _ = qh.shape
    nC = T // CHUNK

    def body(q_ref, kp_ref, vp_ref, bias_ref, gain_ref, o_ref):
        p, start = _attn_probs(q_ref, kp_ref, bias_ref, pl.program_id(1))
        o = _dot(p.astype(BF16), vp_ref[pl.ds(start, BAND), :])
        rr = lax.rsqrt(jnp.mean(o * o, axis=-1, keepdims=True) + EPS)
        o_ref[...] = (o * rr * gain_ref[...]).astype(o_ref.dtype)

    per_chunk = pl.BlockSpec((None, CHUNK, HD_A), lambda h, n: (h, n, 0))
    per_head = lambda shape: pl.BlockSpec((None,) + shape, lambda h, n: (h, 0, 0))
    return pl.pallas_call(
        body, name="attn_fwd", grid=(H, nC),
        in_specs=[per_chunk, per_head((KPAD + T, HD_A)), per_head((KPAD + T, HD_A)), per_head((CHUNK, BAND)),
                  per_head((1, HD_A))],
        out_specs=per_chunk, out_shape=jax.ShapeDtypeStruct((H, T, HD_A), BF16),
        compiler_params=_params(("parallel", "arbitrary"), 40),
    )(qh, kp, vp, bias, gain)


def _attn_bwd(qh, kp, vp, bias, gain, d_on):
    H, T, _ = qh.shape
    nC = T // CHUNK
    scale = HD_A ** -0.5

    def body(q_ref, kp_ref, vp_ref, bias_ref, gain_ref, don_ref, dq_ref, dk_ref, dv_ref, dbias_ref, dgain_ref):
        n = pl.program_id(1)

        @pl.when(n == 0)
        def _():
            dk_ref[...] = jnp.zeros_like(dk_ref)
            dv_ref[...] = jnp.zeros_like(dv_ref)
            dbias_ref[...] = jnp.zeros_like(dbias_ref)
            dgain_ref[...] = jnp.zeros_like(dgain_ref)

        p, start = _attn_probs(q_ref, kp_ref, bias_ref, n)
        band = pl.ds(start, BAND)
        pb = p.astype(BF16)
        vb = vp_ref[band, :]
        o = _dot(pb, vb)
        rr = lax.rsqrt(jnp.mean(o * o, axis=-1, keepdims=True) + EPS)
        on = o * rr
        d_on = don_ref[...]
        dgain_ref[...] += _colsum(d_on * on)
        dyo = d_on * gain_ref[...]
        do = rr * (dyo - on * jnp.mean(dyo * on, axis=-1, keepdims=True))
        dob = do.astype(BF16)
        dp = _dot(dob, vb, NT)
        ds = p * (dp - jnp.sum(do * o, axis=-1, keepdims=True))
        dbias_ref[...] += ds
        dsb = ds.astype(BF16)
        dq_ref[...] = (_dot(dsb, kp_ref[band, :]) * scale).astype(dq_ref.dtype)
        dk_ref[band, :] += _dot(dsb, q_ref[...], TN) * scale
        dv_ref[band, :] += _dot(pb, dob, TN)

    per_chunk = pl.BlockSpec((None, CHUNK, HD_A), lambda h, n: (h, n, 0))
    per_head = lambda shape: pl.BlockSpec((None,) + shape, lambda h, n: (h, 0, 0))
    return pl.pallas_call(
        body, name="attn_bwd", grid=(H, nC),
        in_specs=[per_chunk, per_head((KPAD + T, HD_A)), per_head((KPAD + T, HD_A)), per_head((CHUNK, BAND)),
                  per_head((1, HD_A)), per_chunk],
        out_specs=[per_chunk, per_head((KPAD + T, HD_A)), per_head((KPAD + T, HD_A)), per_head((CHUNK, BAND)),
                   per_head((1, HD_A))],
        out_shape=[jax.ShapeDtypeStruct((H, T, HD_A), BF16), jax.ShapeDtypeStruct((H, KPAD + T, HD_A), F32),
                   jax.ShapeDtypeStruct((H, KPAD + T, HD_A), F32), jax.ShapeDtypeStruct((H, CHUNK, BAND), F32),
                   jax.ShapeDtypeStruct((H, 1, HD_A), F32)],
        compiler_params=_params(("parallel", "arbitrary"), 40),
    )(qh, kp, vp, bias, gain, d_on)


def _bias_band(rel_bias):
    rel = jnp.arange(CHUNK)[:, None] + KPAD - jnp.arange(BAND)[None, :]
    return rel_bias[:, jnp.clip(rel, -MAX_REL, MAX_REL) + MAX_REL]


def _bias_band_grad(dbias):
    H = dbias.shape[0]
    width = BAND + CHUNK - 1
    skew = jnp.pad(dbias[:, ::-1, :], ((0, 0), (0, 0), (0, CHUNK)))
    skew = skew.reshape(H, CHUNK * (width + 1))[:, :CHUNK * width].reshape(H, CHUNK, width)
    diag = jnp.sum(skew, axis=1)
    n_far = KPAD + CHUNK - 1 - MAX_REL + 1
    far = jnp.sum(diag[:, :n_far], axis=1, keepdims=True)
    near = diag[:, n_far:][:, ::-1]
    zeros = jnp.zeros((H, MAX_REL - (CHUNK - 1)), F32)
    return jnp.concatenate([zeros, near, far], axis=1)


def _tri(n, lower):
    r = lax.broadcasted_iota(jnp.int32, (n, n), 0)
    c = lax.broadcasted_iota(jnp.int32, (n, n), 1)
    return jnp.where((c <= r) if lower else (c >= r), 1.0, 0.0).astype(F32)


def _hgrn_gates(zq_ref, zf_ref, lbl_ref, q_s, k_s, b_s):
    lb = jax.nn.sigmoid(lbl_ref[0:1, :] - lbl_ref[1:2, :])
    zq = zq_ref[...]
    sig = jax.nn.sigmoid(zf_ref[...])
    f = lb + (1.0 - lb) * sig
    sq = jax.nn.sigmoid(zq)
    q_s[...] = zq * sq
    k_s[...] = 1.0 - f
    b_s[...] = _dot(_tri(CHUNK, True), jnp.log(f), precision=HIGHEST)
    return lb, sig, f, sq


def _sub_rows(i):
    return pl.ds(i * SUB, SUB)


def _row_mask(s):
    return lax.broadcasted_iota(jnp.int32, (SUB, HD_B), 0) >= s


def _decay_from(b_sub, b_row, s):
    return jnp.where(_row_mask(s), jnp.exp(jnp.minimum(b_sub - b_row, 0.0)), 0.0)


def _hgrn_fwd(proj, lb_logits, gnorm_g):
    T = proj.shape[0]
    nC = T // CHUNK
    W = lb_logits.shape[1]
    H = W // HD_B
    col0 = (proj.shape[1] - 4 * W) // HD_B

    def body(zq_ref, zf_ref, xi_ref, zg_ref, lbl_ref, gn_ref, mix_ref, o_ref, stall_ref, st_ref, q_s, k_s, b_s, acc_s):
        @pl.when(pl.program_id(1) == 0)
        def _():
            st_ref[...] = jnp.zeros_like(st_ref)

        _hgrn_gates(zq_ref, zf_ref, lbl_ref, q_s, k_s, b_s)
        q, k, b = q_s[...], k_s[...], b_s[...]
        st = st_ref[...]
        stall_ref[...] = st
        b_last = b_s[CHUNK - 1:CHUNK, :]
        acc_s[...] = _dot((q * jnp.exp(b)).astype(BF16), st.astype(BF16), NT)
        for i in range(CHUNK // SUB):
            rows = _sub_rows(i)
            q_i, b_i = q_s[rows, :], b_s[rows, :]
            acc = jnp.zeros((SUB, HD_B), F32)
            if i:
                past = pl.ds(0, i * SUB)
                b_ref = b_s[i * SUB - 1:i * SUB, :]
                qs = (q_i * jnp.exp(b_i - b_ref)).astype(BF16)
                ks = (k_s[past, :] * jnp.exp(b_ref - b_s[past, :])).astype(BF16)
                acc += _dot(_dot(qs, ks, NT).astype(BF16), xi_ref[past, :].astype(BF16))
            for s in range(SUB):
                row = pl.ds(i * SUB + s, 1)
                w = q_i * _decay_from(b_i, b_s[row, :], s)
                acc += jnp.sum(w * k_s[row, :], axis=-1, keepdims=True) * xi_ref[row, :]
            acc_s[rows, :] += acc
        o = acc_s[...]
        kd = (k * jnp.exp(b_last - b)).astype(BF16)
        st_ref[...] = st * jnp.exp(b_last) + _dot(xi_ref[...].astype(BF16), kd, TN)
        o_ref[...] = o
        zg = zg_ref[...]
        rr = lax.rsqrt(jnp.mean(o * o, axis=-1, keepdims=True) + EPS)
        mix_ref[...] = (o * rr * gn_ref[...] * (zg * jax.nn.sigmoid(zg))).astype(mix_ref.dtype)

    col = lambda part: pl.BlockSpec((CHUNK, HD_B), lambda h, n: (n, col0 + part * H + h))
    out_blk = pl.BlockSpec((CHUNK, HD_B), lambda h, n: (n, h))
    tile = pltpu.VMEM((CHUNK, HD_B), F32)
    return pl.pallas_call(
        body, name="hgrn_fwd", grid=(H, nC),
        in_specs=[col(0), col(1), col(2), col(3), pl.BlockSpec((2, HD_B), lambda h, n: (0, h)),
                  pl.BlockSpec((1, HD_B), lambda h, n: (0, 0))],
        out_specs=[out_blk, out_blk, pl.BlockSpec((None, None, HD_B, HD_B), lambda h, n: (h, n, 0, 0))],
        out_shape=[jax.ShapeDtypeStruct((T, W), BF16), jax.ShapeDtypeStruct((T, W), F32),
                   jax.ShapeDtypeStruct((H, nC, HD_B, HD_B), F32)],
        scratch_shapes=[pltpu.VMEM((HD_B, HD_B), F32), tile, tile, tile, tile],
        compiler_params=_params(("parallel", "arbitrary")),
    )(proj, proj, proj, proj, lb_logits, gnorm_g)


def _hgrn_bwd(proj, lb_logits, gnorm_g, o_b, st_all, dmixin):
    T = proj.shape[0]
    nC = T // CHUNK
    W = lb_logits.shape[1]
    H = W // HD_B
    col0 = (proj.shape[1] - 4 * W) // HD_B
    dcol0 = (dmixin.shape[1] - W) // HD_B

    def body(zq_ref, zf_ref, xi_ref, zg_ref, lbl_ref, gn_ref, o_ref, st_ref, dout_ref,
             dzq_ref, dzf_ref, dxi_ref, dzg_ref, dl0_ref, dgn_ref, dst_ref, q_s, k_s, b_s, do_s, dq_s, dk_s, di_s):
        h, n = pl.program_id(0), pl.program_id(1)

        @pl.when(n == 0)
        def _():
            dst_ref[...] = jnp.zeros_like(dst_ref)
            dl0_ref[...] = jnp.zeros_like(dl0_ref)

        @pl.when((n == 0) & (h == 0))
        def _():
            dgn_ref[...] = jnp.zeros_like(dgn_ref)

        lb, sig, f, sq = _hgrn_gates(zq_ref, zf_ref, lbl_ref, q_s, k_s, b_s)
        q, k, b = q_s[...], k_s[...], b_s[...]
        zg, o, dout = zg_ref[...], o_ref[...], dout_ref[...]
        sg = jax.nn.sigmoid(zg)
        rr = lax.rsqrt(jnp.mean(o * o, axis=-1, keepdims=True) + EPS)
        on = o * rr
        gn = gn_ref[...]
        dzg_ref[...] = (dout * on * gn * sg * (1.0 + zg * (1.0 - sg))).astype(dzg_ref.dtype)
        d_on = dout * zg * sg
        dgn_ref[...] += _colsum(d_on * on)
        d_on = d_on * gn
        do = rr * (d_on - on * jnp.mean(d_on * on, axis=-1, keepdims=True))
        do_s[...] = do
        dob = do.astype(BF16)
        st, dst = st_ref[...], dst_ref[...]
        b_last = b_s[CHUNK - 1:CHUNK, :]
        eb, e_last, k_dec = jnp.exp(b), jnp.exp(b_last), jnp.exp(b_last - b)
        qt, kd = q * eb, k * k_dec
        dstb = dst.astype(BF16)
        xib = xi_ref[...].astype(BF16)
        d_kd = _dot(xib, dstb)
        dq_s[...] = _dot(dob, st.astype(BF16)) * eb
        dk_s[...] = d_kd * k_dec
        di_s[...] = _dot(kd.astype(BF16), dstb, NT)
        d_b_last = e_last * _colsum(st * dst) + _colsum(d_kd * kd)
        dst_ref[...] = _dot(dob, qt.astype(BF16), TN) + dst * e_last
        for i in range(CHUNK // SUB):
            rows = _sub_rows(i)
            q_i, b_i, do_i = q_s[rows, :], b_s[rows, :], do_s[rows, :]
            dq_i = jnp.zeros((SUB, HD_B), F32)
            if i:
                past = pl.ds(0, i * SUB)
                b_ref = b_s[i * SUB - 1:i * SUB, :]
                e_q, e_k = jnp.exp(b_i - b_ref), jnp.exp(b_ref - b_s[past, :])
                qs, ks = (q_i * e_q).astype(BF16), (k_s[past, :] * e_k).astype(BF16)
                xi_p, do_b = xi_ref[past, :].astype(BF16), do_i.astype(BF16)
                di_s[past, :] += _dot(_dot(ks, qs, NT).astype(BF16), do_b)
                dq_i += _dot(_dot(do_b, xi_p, NT).astype(BF16), ks) * e_q
                dk_s[past, :] += _dot(_dot(xi_p, do_b, NT).astype(BF16), qs) * e_k
            for s in range(SUB):
                row = pl.ds(i * SUB + s, 1)
                k_row, i_row = k_s[row, :], xi_ref[row, :]
                e = _decay_from(b_i, b_s[row, :], s)
                w = q_i * e
                a_col = jnp.sum(w * k_row, axis=-1, keepdims=True)
                da_col = jnp.sum(do_i * i_row, axis=-1, keepdims=True)
                di_s[row, :] += _colsum(a_col * do_i)
                dq_i += da_col * e * k_row
                dk_s[row, :] += _colsum(da_col * w)
            dq_s[rows, :] += dq_i
        dq, dk = dq_s[...], dk_s[...]
        db = q * dq - k * dk
        is_last = lax.broadcasted_iota(jnp.int32, (CHUNK, HD_B), 0) == CHUNK - 1
        db = db + jnp.where(is_last, d_b_last, 0.0)
        df = _dot(_tri(CHUNK, False), db, precision=HIGHEST) / f - dk
        dzf_ref[...] = (df * (1.0 - lb) * sig * (1.0 - sig)).astype(dzf_ref.dtype)
        dl0_ref[...] += _colsum(df * (1.0 - sig)) * (lb * (1.0 - lb))
        zq = zq_ref[...]
        dzq_ref[...] = (dq * sq * (1.0 + zq * (1.0 - sq))).astype(dzq_ref.dtype)
        dxi_ref[...] = di_s[...].astype(dxi_ref.dtype)

    rev = lambda n: nC - 1 - n
    col = lambda part: pl.BlockSpec((CHUNK, HD_B), lambda h, n: (rev(n), col0 + part * H + h))
    blk = pl.BlockSpec((CHUNK, HD_B), lambda h, n: (rev(n), h))
    tile = pltpu.VMEM((CHUNK, HD_B), F32)
    out_big = jax.ShapeDtypeStruct((T, W), BF16)
    return pl.pallas_call(
        body, name="hgrn_bwd", grid=(H, nC),
        in_specs=[col(0), col(1), col(2), col(3), pl.BlockSpec((2, HD_B), lambda h, n: (0, h)),
                  pl.BlockSpec((1, HD_B), lambda h, n: (0, 0)), blk,
                  pl.BlockSpec((None, None, HD_B, HD_B), lambda h, n: (h, rev(n), 0, 0)),
                  pl.BlockSpec((CHUNK, HD_B), lambda h, n: (rev(n), dcol0 + h))],
        out_specs=[blk, blk, blk, blk, pl.BlockSpec((1, HD_B), lambda h, n: (0, h)),
                   pl.BlockSpec((1, HD_B), lambda h, n: (0, 0))],
        out_shape=[out_big, out_big, out_big, out_big, jax.ShapeDtypeStruct((1, W), F32),
                   jax.ShapeDtypeStruct((1, HD_B), F32)],
        scratch_shapes=[pltpu.VMEM((HD_B, HD_B), F32)] + [tile] * 7,
        compiler_params=_params(("arbitrary", "arbitrary")),
    )(proj, proj, proj, proj, lb_logits, gnorm_g, o_b, st_all, dmixin)


def _adamw_math(g, w, m, v):
    m = B1 * m + (1.0 - B1) * g
    v = B2 * v + (1.0 - B2) * (g * g)
    m_hat = m / (1.0 - B1 ** STEP)
    v_hat = v / (1.0 - B2 ** STEP)
    return -LR * (m_hat / (jnp.sqrt(v_hat) + ADAM_EPS) + WD * w), m, v


def _adamw(g, w, m, v, name):
    R, C = g.shape
    tr = _row_tile(R, C)

    def body(g_ref, w_ref, m_ref, v_ref, d_ref, mo_ref, vo_ref):
        d_ref[...], mo_ref[...], vo_ref[...] = _adamw_math(g_ref[...], w_ref[...], m_ref[...], v_ref[...])

    blk = pl.BlockSpec((tr, C), lambda i: (i, 0))
    return pl.pallas_call(
        body, name=name, grid=(R // tr,), in_specs=[blk] * 4, out_specs=[blk] * 3,
        out_shape=[jax.ShapeDtypeStruct((R, C), F32)] * 3, compiler_params=_params(("parallel",), 40),
    )(g, w, m, v)


def _sum_pair(g_full, from_sibling, c_arr, name):
    Q, K, Ns = g_full.shape
    kh = K // 2
    tr = _row_tile(kh, Ns)
    nh = kh // tr

    def body(c_ref, a_ref, b_ref, o_ref):
        o_ref[...] = a_ref[...] + b_ref[...]

    return pl.pallas_call(
        body, name=name,
        grid_spec=pltpu.PrefetchScalarGridSpec(
            num_scalar_prefetch=1, grid=(Q, nh),
            in_specs=[pl.BlockSpec((None, tr, Ns), lambda q, i, c: (q, c[0] * nh + i, 0)),
                      pl.BlockSpec((None, tr, Ns), lambda q, i, c: (q, i, 0))],
            out_specs=pl.BlockSpec((None, tr, Ns), lambda q, i, c: (q, i, 0))),
        out_shape=jax.ShapeDtypeStruct((Q, kh, Ns), F32), compiler_params=_params(("parallel", "parallel")),
    )(c_arr, g_full, from_sibling)


def _sum_chips(pair_sum, from_chips, chip_arr, name):
    Q, kh, Ns = pair_sum.shape
    tr = _row_tile(kh, Ns)

    def body(chip_ref, a_ref, b0_ref, b1_ref, b2_ref, o_ref):
        o_ref[...] = ((a_ref[...] + b0_ref[...]) + b1_ref[...]) + b2_ref[...]

    recv = lambda k: pl.BlockSpec((None, tr, Ns), lambda i, ch: (k, i, 0))
    return pl.pallas_call(
        body, name=name,
        grid_spec=pltpu.PrefetchScalarGridSpec(
            num_scalar_prefetch=1, grid=(kh // tr,),
            in_specs=[pl.BlockSpec((None, tr, Ns), lambda i, ch: (ch[0], i, 0)), recv(0), recv(1), recv(2)],
            out_specs=pl.BlockSpec((tr, Ns), lambda i, ch: (i, 0))),
        out_shape=jax.ShapeDtypeStruct((kh, Ns), F32), compiler_params=_params(("parallel",)),
    )(chip_arr, pair_sum, from_chips, from_chips, from_chips)


ANY = pl.BlockSpec(memory_space=pl.ANY)
CHIP_FLIPS = ((1, 0), (0, 1), (1, 1))


def _gather_small(v, name):
    R, L = v.shape

    def body(v_ref, out_ref, send_sems, recv_sems):
        x, y, c = _place()
        me = 4 * x + 2 * y + c
        out_ref[me] = v_ref[...]
        peers = [(_flip(x, k >> 2 & 1), _flip(y, k >> 1 & 1), _flip(c, k & 1)) for k in range(1, N_DEV)]

        def copy(k, row, to):
            return pltpu.make_async_remote_copy(src_ref=v_ref, dst_ref=out_ref.at[row], send_sem=send_sems.at[k],
                                                recv_sem=recv_sems.at[k], device_id=to, device_id_type=MESH)

        sends = [copy(k, me, peer) for k, peer in enumerate(peers)]
        for cp in sends:
            cp.start()
        for k, (px, py, pc) in enumerate(peers):
            copy(k, 4 * px + 2 * py + pc, (x, y, c)).wait_recv()
        for cp in sends:
            cp.wait_send()

    vmem = pl.BlockSpec(memory_space=pltpu.VMEM)
    return pl.pallas_call(
        body, name=name, in_specs=[vmem], out_specs=vmem, out_shape=jax.ShapeDtypeStruct((N_DEV, R, L), F32),
        scratch_shapes=[pltpu.SemaphoreType.DMA((N_DEV - 1,)), pltpu.SemaphoreType.DMA((N_DEV - 1,))],
    )(v)


def _gather_weight(shard, name):
    K, Ns = shard.shape
    kh = K // 2

    def body(w_ref, out_ref, send_sems, recv_sems, local_sem):
        x, y, c = _place()
        chips = [(_flip(x, fx), _flip(y, fy)) for fx, fy in CHIP_FLIPS]

        def half(chip, which):
            return out_ref.at[2 * chip[0] + chip[1], pl.ds(which * kh, kh), :]

        def copy(k, dst, to, src=None):
            return pltpu.make_async_remote_copy(src_ref=dst if src is None else src, dst_ref=dst,
                                                send_sem=send_sems.at[k], recv_sem=recv_sems.at[k],
                                                device_id=to, device_id_type=MESH)

        mine = pltpu.make_async_copy(w_ref, out_ref.at[2 * x + y], local_sem)
        mine.start()
        first = [copy(j, half((x, y), c), (*chip, c), src=w_ref.at[pl.ds(c * kh, kh), :])
                 for j, chip in enumerate(chips)]
        for cp in first:
            cp.start()
        passed = [copy(3 + j, half(chip, c), (x, y, 1 - c)) for j, chip in enumerate(chips)]
        for j, chip in enumerate(chips):
            copy(j, half(chip, c), (x, y, c)).wait_recv()
            passed[j].start()
        for j, chip in enumerate(chips):
            copy(3 + j, half(chip, 1 - c), (x, y, c)).wait_recv()
        for cp in first + passed:
            cp.wait_send()
        mine.wait()

    return pl.pallas_call(
        body, name=name, in_specs=[ANY], out_specs=ANY,
        out_shape=jax.ShapeDtypeStruct((N_CHIPS, K, Ns), shard.dtype),
        scratch_shapes=[pltpu.SemaphoreType.DMA((6,)), pltpu.SemaphoreType.DMA((6,)), pltpu.SemaphoreType.DMA],
    )(shard)


def _send_pair(g_full, name):
    Q, K, Ns = g_full.shape
    kh = K // 2

    def body(g_ref, got_ref, send_sem, recv_sem):
        x, y, c = _place()
        cp = pltpu.make_async_remote_copy(src_ref=g_ref.at[:, pl.ds((1 - c) * kh, kh), :], dst_ref=got_ref,
                                          send_sem=send_sem, recv_sem=recv_sem, device_id=(x, y, 1 - c),
                                          device_id_type=MESH)
        cp.start()
        cp.wait()

    return pl.pallas_call(
        body, name=name, in_specs=[ANY], out_specs=ANY, out_shape=jax.ShapeDtypeStruct((Q, kh, Ns), g_full.dtype),
        scratch_shapes=[pltpu.SemaphoreType.DMA, pltpu.SemaphoreType.DMA],
    )(g_full)


def _send_chips(pair_sum, name):
    Q, kh, Ns = pair_sum.shape

    def body(p_ref, got_ref, send_sems, recv_sems):
        x, y, c = _place()
        sends = []
        for j, (fx, fy) in enumerate(CHIP_FLIPS):
            px, py = _flip(x, fx), _flip(y, fy)
            sends.append(pltpu.make_async_remote_copy(
                src_ref=p_ref.at[2 * px + py], dst_ref=got_ref.at[j], send_sem=send_sems.at[j],
                recv_sem=recv_sems.at[j], device_id=(px, py, c), device_id_type=MESH))
        for cp in sends:
            cp.start()
        for cp in sends:
            cp.wait_recv()
        for cp in sends:
            cp.wait_send()

    return pl.pallas_call(
        body, name=name, in_specs=[ANY], out_specs=ANY,
        out_shape=jax.ShapeDtypeStruct((Q - 1, kh, Ns), pair_sum.dtype),
        scratch_shapes=[pltpu.SemaphoreType.DMA((3,)), pltpu.SemaphoreType.DMA((3,))],
    )(pair_sum)


def _share_halves(half, name):
    kh, Ns = half.shape

    def body(h_ref, out_ref, send_sem, recv_sem, local_sem):
        x, y, c = _place()
        mine = pltpu.make_async_copy(h_ref, out_ref.at[pl.ds(c * kh, kh), :], local_sem)
        mine.start()
        cp = pltpu.make_async_remote_copy(src_ref=h_ref, dst_ref=out_ref.at[pl.ds(c * kh, kh), :], send_sem=send_sem,
                                          recv_sem=recv_sem, device_id=(x, y, 1 - c), device_id_type=MESH)
        cp.start()
        pltpu.make_async_remote_copy(src_ref=h_ref, dst_ref=out_ref.at[pl.ds((1 - c) * kh, kh), :], send_sem=send_sem,
                                     recv_sem=recv_sem, device_id=(x, y, 1 - c), device_id_type=MESH).wait_recv()
        cp.wait_send()
        mine.wait()

    return pl.pallas_call(
        body, name=name, in_specs=[ANY], out_specs=ANY, out_shape=jax.ShapeDtypeStruct((2 * kh, Ns), half.dtype),
        scratch_shapes=[pltpu.SemaphoreType.DMA, pltpu.SemaphoreType.DMA, pltpu.SemaphoreType.DMA],
    )(half)


def _reduce_weight_grad(g_full, c_arr, chip_arr, tag):
    pair = _sum_pair(g_full, _send_pair(g_full, tag + "_send_pair"), c_arr, tag + "_sum_pair")
    half = _sum_chips(pair, _send_chips(pair, tag + "_send_chips"), chip_arr, tag + "_sum_chips")
    return _share_halves(half, tag + "_share")


def _silu(v):
    return v * jax.nn.sigmoid(v)


def _ada_fwd(c_all, w_ada, tn=512):
    M, D = c_all.shape
    Ns = w_ada.shape[1]

    def body(c_ref, w_ref, o_ref):
        o_ref[...] = _dot(_silu(c_ref[...]).astype(BF16), w_ref[...].astype(BF16))

    return pl.pallas_call(
        body, name="ada_fwd", grid=(Ns // tn,),
        in_specs=[pl.BlockSpec((M, D), lambda j: (0, 0)), pl.BlockSpec((D, tn), lambda j: (0, j))],
        out_specs=pl.BlockSpec((M, tn), lambda j: (0, j)), out_shape=jax.ShapeDtypeStruct((M, Ns), F32),
        compiler_params=_params(("parallel",)),
    )(c_all, w_ada)


def _ada_bwd(c_all, dmod, w, m, v, tk=256, tn=1536):
    M, D = c_all.shape
    Ns = dmod.shape[1]

    def body(c_ref, d_ref, w_ref, m_ref, v_ref, g_ref, dl_ref, mo_ref, vo_ref):
        g = _dot(_silu(c_ref[...]).astype(BF16), d_ref[...].astype(BF16), TN)
        g_ref[...] = g
        dl_ref[...], mo_ref[...], vo_ref[...] = _adamw_math(g, w_ref[...], m_ref[...], v_ref[...])

    blk = pl.BlockSpec((tk, tn), lambda i, j: (i, j))
    return pl.pallas_call(
        body, name="ada_bwd", grid=(D // tk, Ns // tn),
        in_specs=[pl.BlockSpec((M, tk), lambda i, j: (0, i)), pl.BlockSpec((M, tn), lambda i, j: (0, j)), blk, blk, blk],
        out_specs=[blk] * 4, out_shape=[jax.ShapeDtypeStruct((D, Ns), F32)] * 4,
        compiler_params=_params(("parallel", "parallel"), 40),
    )(c_all, dmod, w, m, v)


def _small_update(g_all, w, m, v):
    R, L = w.shape

    def body(g_ref, w_ref, m_ref, v_ref, go_ref, d_ref, mo_ref, vo_ref):
        g = g_ref[0]
        for d in range(1, N_DEV):
            g = g + g_ref[d]
        go_ref[...] = g
        d_ref[...], mo_ref[...], vo_ref[...] = _adamw_math(g, w_ref[...], m_ref[...], v_ref[...])

    return pl.pallas_call(body, name="small_update", out_shape=[jax.ShapeDtypeStruct((R, L), F32)] * 4)(g_all, w, m, v)


def _pack(parts, rows):
    flat = jnp.concatenate([p.reshape(-1) for p in parts])
    return jnp.pad(flat, (0, rows * 128 - flat.shape[0])).reshape(rows, 128)


def _unpack(packed, shapes):
    flat, out, at = packed.reshape(-1), [], 0
    for shp in shapes:
        size = 1
        for d in shp:
            size *= d
        out.append(flat[at:at + size].reshape(shp))
        at += size
    return out


def _heads(a, pad):
    T = a.shape[0]
    h = a.reshape(T, -1, HD_A).transpose(1, 0, 2).astype(BF16)
    return jnp.pad(h, ((0, 0), (pad, 0), (0, 0))) if pad else h


def _unheads(a):
    return a.transpose(1, 0, 2).reshape(a.shape[1], -1)


def _layer(x, tgt, mod, w_in, w_o, w_ffn_in, w_ffn_out, rel_bias, attn_norm_g, lb_logits, gnorm_g, ln1_g, ln1_b,
           ln2_g, ln2_b):
    T, D = x.shape
    aw = attn_norm_g.shape[1]
    shift1, scale1, gate1, shift2, scale2, gate2 = [mod[i:i + 1] for i in range(6)]
    w_o3 = w_o.reshape(1, -1, w_o.shape[2])
    w_out3 = w_ffn_out.reshape(1, -1, w_ffn_out.shape[2])
    d_ff = w_out3.shape[1]
    h1 = _pre_mixer(x, scale1, shift1)
    proj = _mm_nn(h1, w_in, tm=1024, tn=256, tk=D, name="proj")
    qh = _heads(proj[:, :aw], 0)
    kp = _heads(proj[:, aw:2 * aw], KPAD)
    vp = _heads(proj[:, 2 * aw:3 * aw], KPAD)
    bias = _bias_band(rel_bias)
    gain = attn_norm_g.reshape(-1, 1, HD_A)
    mix_a = _unheads(_attn_fwd(qh, kp, vp, bias, gain))
    mix_b, o_b, st_all = _hgrn_fwd(proj, lb_logits, gnorm_g)
    mixin = jnp.concatenate([mix_a, mix_b], axis=1)
    mix = _mm_nn(mixin, w_o3, tm=1024, tn=512, tk=D, name="mix_out")
    x1, h2 = _post_mixer(mix, x, gate1, ln1_g, ln1_b, scale2, shift2)
    ff = _mm_nn(h2, w_ffn_in, tm=1024, tn=256, tk=D, name="ffn_in")
    act = _swiglu_fwd(ff)
    f = _mm_nn(act, w_out3, tm=1024, tn=1024, tk=d_ff // 4, name="ffn_out")
    du2, df, acc2 = _loss_head(f, x1, tgt, gate2, ln2_g, ln2_b)
    loss = (0.5 / D) * jnp.sum(acc2[3])
    da = _mm_nt(df, w_out3, tm=1024, to=512, tn=D, name="d_act")
    g_ffn_out = _mm_tn(act, df, q=1, tk=512, tn=1024, tt=T, name="g_ffn_out")
    dff = jnp.concatenate(_swiglu_bwd(da, ff), axis=1)
    dh2 = _mm_nt(dff, w_ffn_in, tm=1024, to=1024, tn=w_ffn_in.shape[2], name="d_h2")
    g_ffn_in = _mm_tn(h2, dff, q=N_CHIPS, tk=512, tn=w_ffn_in.shape[2] // 2, tt=T, name="g_ffn_in")
    du1, dmix, acc1 = _mid_bwd(dh2, du2, x1, mix, x, gate1, ln1_g, scale2)
    dmixin = _mm_nt(dmix, w_o3, tm=1024, to=512, tn=D, name="d_mixin")
    g_o = _mm_tn(mixin, dmix, q=1, tk=512, tn=1024, tt=T, name="g_o")
    d_on = dmixin[:, :aw].reshape(T, -1, HD_A).transpose(1, 0, 2)
    dq, dk, dv, dbias, dgain = _attn_bwd(qh, kp, vp, bias, gain, d_on)
    dzq, dzf, dxi, dzg, dl0, dgn = _hgrn_bwd(proj, lb_logits, gnorm_g, o_b, st_all, dmixin)
    dproj = jnp.concatenate([_unheads(dq), _unheads(dk[:, KPAD:]).astype(BF16), _unheads(dv[:, KPAD:]).astype(BF16),
                             dzq, dzf, dxi, dzg], axis=1)
    dh1 = _mm_nt(dproj, w_in, tm=1024, to=1024, tn=w_in.shape[2], name="d_h1")
    g_in = _mm_tn(h1, dproj, q=N_CHIPS, tk=512, tn=w_in.shape[2] // 2, tt=T, name="g_in")
    grad_x, acc0 = _first_bwd(dh1, du1, x, scale1)
    dmod = jnp.concatenate([acc0[1:2], acc0[0:1], acc1[4:5], acc1[1:2], acc1[0:1], acc2[2:3]], axis=0)
    small = dict(rel_bias=_bias_band_grad(dbias), attn_norm_g=dgain.reshape(1, -1),
                 lb_logits=jnp.concatenate([dl0, -dl0], axis=0), gnorm_g=dgn,
                 ln1_g=acc1[2:3], ln1_b=acc1[3:4], ln2_g=acc2[0:1], ln2_b=acc2[1:2])
    return loss, grad_x, (g_in, g_o.reshape(N_CHIPS, -1, D), g_ffn_in, g_ffn_out.reshape(N_CHIPS, -1, D)), dmod, small


SMALL = ("rel_bias", "attn_norm_g", "lb_logits", "gnorm_g", "ln1_g", "ln1_b", "ln2_g", "ln2_b")
SMALL_ROWS = 256


def kernel(x, c, w_ada, b_ada, w_in, rel_bias, attn_norm_g, lb_logits, gnorm_g, w_o, ln1_g, ln1_b, w_ffn_in, w_ffn_out, ln2_g, ln2_b, loss_target, m_w_ada, m_b_ada, m_w_in, m_rel_bias, m_attn_norm_g, m_lb_logits, m_gnorm_g, m_w_o, m_ln1_g, m_ln1_b, m_w_ffn_in, m_w_ffn_out, m_ln2_g, m_ln2_b, v_w_ada, v_b_ada, v_w_in, v_rel_bias, v_attn_norm_g, v_lb_logits, v_gnorm_g, v_w_o, v_ln1_g, v_ln1_b, v_w_ffn_in, v_w_ffn_out, v_ln2_g, v_ln2_b):
    mx, my, mc = _place()
    me = 4 * mx + 2 * my + mc
    chip = 2 * mx + my
    c_arr, chip_arr = mc.reshape(1).astype(jnp.int32), chip.reshape(1).astype(jnp.int32)
    D = x.shape[2]
    ns_ada = w_ada.shape[2]

    big = dict(w_in=(w_in, m_w_in, v_w_in), w_o=(w_o, m_w_o, v_w_o), w_ffn_in=(w_ffn_in, m_w_ffn_in, v_w_ffn_in),
               w_ffn_out=(w_ffn_out, m_w_ffn_out, v_w_ffn_out))
    full = {n: _gather_weight(t[0][0].astype(BF16), "gather_" + n) for n, t in big.items()}

    c_all = _gather_small(c.reshape(D // 128, 128), "gather_c").reshape(N_DEV, D)
    c_all = jnp.pad(c_all, ((0, 16 - N_DEV), (0, 0)))
    mod_cols = _ada_fwd(c_all, w_ada[0])[:N_DEV]
    mod_all = _gather_small(mod_cols.reshape(-1, 128), "gather_mod").reshape(N_DEV, N_DEV, ns_ada)
    mod = lax.dynamic_index_in_dim(mod_all[::2], me, axis=1, keepdims=False)
    mod = (mod.reshape(1, -1) + b_ada).reshape(6, D)

    loss, grad_x, g_big, dmod, g_small = _layer(
        x[0], loss_target[0], mod, full["w_in"], full["w_o"], full["w_ffn_in"], full["w_ffn_out"], rel_bias[0],
        attn_norm_g, lb_logits, gnorm_g, ln1_g, ln1_b, ln2_g, ln2_b)
    loss = lax.psum(loss, ("x", "y", "c"))

    grads, deltas, new_m, new_v = {}, {}, {}, {}
    for (n, (w, m, v)), g_full in zip(big.items(), g_big):
        g = _reduce_weight_grad(g_full, c_arr, chip_arr, "rs_" + n)
        d, mo, vo = _adamw(g, w[0], m[0], v[0], "adamw_" + n)
        grads[n], deltas[n], new_m[n], new_v[n] = g[None], d[None], mo[None], vo[None]

    small_in = dict(rel_bias=(rel_bias, m_rel_bias, v_rel_bias), attn_norm_g=(attn_norm_g, m_attn_norm_g, v_attn_norm_g),
                    lb_logits=(lb_logits, m_lb_logits, v_lb_logits), gnorm_g=(gnorm_g, m_gnorm_g, v_gnorm_g),
                    ln1_g=(ln1_g, m_ln1_g, v_ln1_g), ln1_b=(ln1_b, m_ln1_b, v_ln1_b), ln2_g=(ln2_g, m_ln2_g, v_ln2_g),
                    ln2_b=(ln2_b, m_ln2_b, v_ln2_b))
    g_all = _gather_small(_pack([dmod] + [g_small[n] for n in SMALL], SMALL_ROWS), "gather_small")
    packed = [_pack([t] + [small_in[n][i] for n in SMALL], SMALL_ROWS)
              for i, t in enumerate((b_ada, m_b_ada, v_b_ada))]
    shapes = [b_ada.shape] + [small_in[n][0].shape for n in SMALL]
    outs = [_unpack(o, shapes) for o in _small_update(g_all, *packed)]
    for i, n in enumerate(("b_ada",) + SMALL):
        grads[n], deltas[n], new_m[n], new_v[n] = outs[0][i], outs[1][i], outs[2][i], outs[3][i]

    dmod_all = g_all[:, :6 * D // 128].reshape(N_DEV, 6 * D)
    dmod_cols = lax.dynamic_slice_in_dim(dmod_all, chip * ns_ada, ns_ada, axis=1)
    dmod_cols = jnp.pad(dmod_cols, ((0, 16 - N_DEV), (0, 0)))
    g, d, mo, vo = _ada_bwd(c_all, dmod_cols, w_ada[0], m_w_ada[0], v_w_ada[0])
    grads["w_ada"], deltas["w_ada"], new_m["w_ada"], new_v["w_ada"] = g[None], d[None], mo[None], vo[None]

    order = ("w_ada", "b_ada", "w_in", "rel_bias", "attn_norm_g", "lb_logits", "gnorm_g", "w_o", "ln1_g", "ln1_b",
             "w_ffn_in", "w_ffn_out", "ln2_g", "ln2_b")
    return (loss, grad_x[None], *[grads[n] for n in order], *[deltas[n] for n in order],
            *[new_m[n] for n in order], *[new_v[n] for n in order])
```

```python
import numpy as np
import jax
import jax.numpy as jnp
from jax import lax
from jax.experimental import pallas as pl
from jax.experimental.pallas import tpu as pltpu

F32 = jnp.float32
BF16 = jnp.bfloat16
MESH = pl.DeviceIdType.MESH
HIGHEST = lax.Precision.HIGHEST

CHUNK = 64
N_PAST = 8
QG = 4
QROWS = QG * CHUNK
KPAD = N_PAST * CHUNK
UNION = (QG + N_PAST) * CHUNK
HD_A = 64
HD_B = 128
SUB = 16
MAX_REL = 256
EPS = 1e-5
ALPHA = 2.0 ** 0.25
LR, B1, B2, ADAM_EPS, WD, STEP = 1e-3, 0.9, 0.999, 1e-8, 0.01, 10
N_CHIPS = 4
N_DEV = 8
NEG = -1e30
TILE_BYTES = 3 << 19

NN = ((1,), (0,))
NT = ((1,), (1,))
TN = ((0,), (0,))


def _dot(a, b, dims=NN, precision=None):
    return lax.dot_general(a, b, (dims, ((), ())), preferred_element_type=F32, precision=precision)


def _params(sem=None, vmem_mb=None, **kw):
    return pltpu.CompilerParams(dimension_semantics=sem,
                                vmem_limit_bytes=None if vmem_mb is None else vmem_mb << 20, **kw)


def _row_tile(rows, cols):
    for cand in (512, 256, 128, 64, 32, 16, 8):
        if rows % cand == 0 and cand * cols * 4 <= TILE_BYTES:
            return cand
    raise ValueError((rows, cols))


def _place():
    return lax.axis_index("x"), lax.axis_index("y"), lax.axis_index("c")


def _flip(v, bit):
    return 1 - v if bit else v


def _mm(a, b, *, grid, a_spec, b_spec, o_spec, o_shape, o_dtype, dims, acc_shape, name, vmem_mb=48):
    nk = grid[2]

    def body(a_ref, b_ref, o_ref, *scratch):
        part = _dot(a_ref[...], b_ref[...], dims)
        if nk == 1:
            o_ref[...] = part.astype(o_ref.dtype)
            return
        acc_ref, = scratch
        k = pl.program_id(2)

        @pl.when(k == 0)
        def _():
            acc_ref[...] = part

        @pl.when(k > 0)
        def _():
            acc_ref[...] += part

        @pl.when(k == nk - 1)
        def _():
            o_ref[...] = acc_ref[...].astype(o_ref.dtype)

    return pl.pallas_call(
        body, name=name, grid=grid, in_specs=[a_spec, b_spec], out_specs=o_spec,
        out_shape=jax.ShapeDtypeStruct(o_shape, o_dtype),
        scratch_shapes=[] if nk == 1 else [pltpu.VMEM(acc_shape, F32)],
        compiler_params=_params(("parallel", "parallel", "arbitrary"), vmem_mb),
    )(a, b)


def _mm_nn(a, w, *, tm, tn, tk, name):
    T, K = a.shape
    Q, _, Ns = w.shape
    nbs = Ns // tn
    tm = min(tm, T)
    return _mm(a, w, grid=(T // tm, Q * nbs, K // tk),
               a_spec=pl.BlockSpec((tm, tk), lambda i, j, k: (i, k)),
               b_spec=pl.BlockSpec((None, tk, tn), lambda i, j, k: (j // nbs, k, j % nbs)),
               o_spec=pl.BlockSpec((tm, tn), lambda i, j, k: (i, j)),
               o_shape=(T, Q * Ns), o_dtype=F32, dims=NN, acc_shape=(tm, tn), name=name)


def _mm_nt(g, w, *, tm, to, tn, name):
    T = g.shape[0]
    Q, K, Ns = w.shape
    nbs = Ns // tn
    tm = min(tm, T)
    return _mm(g, w, grid=(T // tm, K // to, Q * nbs),
               a_spec=pl.BlockSpec((tm, tn), lambda i, j, n: (i, n)),
               b_spec=pl.BlockSpec((None, to, tn), lambda i, j, n: (n // nbs, j, n % nbs)),
               o_spec=pl.BlockSpec((tm, to), lambda i, j, n: (i, j)),
               o_shape=(T, K), o_dtype=F32, dims=NT, acc_shape=(tm, to), name=name)


def _mm_tn(a, g, *, q, tk, tn, tt, name):
    T, K = a.shape
    Ns = g.shape[1] // q
    nbs = Ns // tn
    return _mm(a, g, grid=(K // tk, q * nbs, T // tt),
               a_spec=pl.BlockSpec((tt, tk), lambda i, j, t: (t, i)),
               b_spec=pl.BlockSpec((tt, tn), lambda i, j, t: (t, j)),
               o_spec=pl.BlockSpec((None, tk, tn), lambda i, j, t: (j // nbs, i, j % nbs)),
               o_shape=(q, K, Ns), o_dtype=BF16, dims=TN, acc_shape=(tk, tn), name=name)


def _ln(u):
    mu = jnp.mean(u, axis=-1, keepdims=True)
    d = u - mu
    r = lax.rsqrt(jnp.mean(d * d, axis=-1, keepdims=True) + EPS)
    return d * r, r


def _ln_bwd(dy, un, r):
    return r * (dy - jnp.mean(dy, axis=-1, keepdims=True) - un * jnp.mean(dy * un, axis=-1, keepdims=True))


def _colsum(v):
    return jnp.sum(v, axis=0, keepdims=True)


def _rowwise(name, fn, bigs, vecs, out_dtypes, n_acc, tm=128):
    T, D = bigs[0].shape
    nb, nv, no = len(bigs), len(vecs), len(out_dtypes)

    def body(*refs):
        outs, accs = fn([r[...] for r in refs[:nb]], [r[...] for r in refs[nb:nb + nv]])
        for r, o in zip(refs[nb + nv:nb + nv + no], outs):
            r[...] = o.astype(r.dtype)
        if n_acc:
            acc_ref = refs[nb + nv + no]

            @pl.when(pl.program_id(0) == 0)
            def _():
                acc_ref[...] = jnp.zeros_like(acc_ref)

            for row, a in enumerate(accs):
                acc_ref[row:row + 1, :] += a

    big_spec = pl.BlockSpec((tm, D), lambda i: (i, 0))
    vec_spec = pl.BlockSpec((1, D), lambda i: (0, 0))
    out_shape = [jax.ShapeDtypeStruct((T, D), dt) for dt in out_dtypes]
    out_specs = [big_spec] * no
    if n_acc:
        out_shape.append(jax.ShapeDtypeStruct((8, D), F32))
        out_specs.append(pl.BlockSpec((8, D), lambda i: (0, 0)))
    return pl.pallas_call(
        body, name=name, grid=(T // tm,), in_specs=[big_spec] * nb + [vec_spec] * nv,
        out_specs=out_specs, out_shape=out_shape,
        compiler_params=_params(("arbitrary",), 48),
    )(*bigs, *vecs)


def _pre_mixer(x, scale1, shift1):
    def fn(b, v):
        xn, _ = _ln(b[0])
        return [xn * (1.0 + v[0]) + v[1]], []
    return _rowwise("pre_mixer", fn, [x], [scale1, shift1], [BF16], 0)[0]


def _post_mixer(mix, x, gate1, g1, b1, scale2, shift2):
    def fn(b, v):
        un1, _ = _ln(ALPHA * b[1] + v[0] * b[0])
        x1 = un1 * v[1] + v[2]
        xn1, _ = _ln(x1)
        return [x1, xn1 * (1.0 + v[3]) + v[4]], []
    return _rowwise("post_mixer", fn, [mix, x], [gate1, g1, b1, scale2, shift2], [F32, BF16], 0)


def _loss_head(f, x1, tgt, gate2, g2, b2):
    def fn(b, v):
        ff, xx, tt = b
        d_model = ff.shape[-1]
        un2, r2 = _ln(ALPHA * xx + v[0] * ff)
        err = un2 * v[1] + v[2] - tt
        dy = err * (1.0 / d_model)
        du2 = _ln_bwd(dy * v[1], un2, r2)
        return [du2, du2 * v[0]], [_colsum(dy * un2), _colsum(dy), _colsum(du2 * ff), _colsum(err * err)]
    return _rowwise("loss_head", fn, [f, x1, tgt], [gate2, g2, b2], [F32, BF16], 4)


def _mid_bwd(dh2, du2, x1, mix, x, gate1, g1, scale2):
    def fn(b, v):
        dh, du, xx1, mm, xx = b
        xn1, r1n = _ln(xx1)
        dx1 = ALPHA * du + _ln_bwd(dh * (1.0 + v[2]), xn1, r1n)
        un1, r1 = _ln(ALPHA * xx + v[0] * mm)
        du1 = _ln_bwd(dx1 * v[1], un1, r1)
        return [du1, du1 * v[0]], [_colsum(dh * xn1), _colsum(dh), _colsum(dx1 * un1), _colsum(dx1),
                                   _colsum(du1 * mm)]
    return _rowwise("mid_bwd", fn, [dh2, du2, x1, mix, x], [gate1, g1, scale2], [F32, BF16], 5)


def _first_bwd(dh1, du1, x, scale1):
    def fn(b, v):
        dh, du, xx = b
        xn, r0 = _ln(xx)
        return [ALPHA * du + _ln_bwd(dh * (1.0 + v[0]), xn, r0)], [_colsum(dh * xn), _colsum(dh)]
    return _rowwise("first_bwd", fn, [dh1, du1, x], [scale1], [F32], 2)


def _swiglu_fwd(ff, tm=256, tf=512):
    T, F2 = ff.shape
    F = F2 // 2
    nf = F // tf

    def body(g_ref, u_ref, a_ref):
        g = g_ref[...]
        a_ref[...] = (g * jax.nn.sigmoid(g) * u_ref[...]).astype(a_ref.dtype)

    return pl.pallas_call(
        body, name="swiglu_fwd", grid=(T // tm, nf),
        in_specs=[pl.BlockSpec((tm, tf), lambda i, j: (i, j)), pl.BlockSpec((tm, tf), lambda i, j: (i, j + nf))],
        out_specs=pl.BlockSpec((tm, tf), lambda i, j: (i, j)),
        out_shape=jax.ShapeDtypeStruct((T, F), BF16),
        compiler_params=_params(("parallel", "parallel")),
    )(ff, ff)


def _swiglu_bwd(da, ff, tm=256, tf=512):
    T, F = da.shape
    nf = F // tf

    def body(da_ref, g_ref, u_ref, dg_ref, du_ref):
        g = g_ref[...]
        d = da_ref[...]
        s = jax.nn.sigmoid(g)
        du_ref[...] = (d * g * s).astype(du_ref.dtype)
        dg_ref[...] = (d * u_ref[...] * s * (1.0 + g * (1.0 - s))).astype(dg_ref.dtype)

    blk = pl.BlockSpec((tm, tf), lambda i, j: (i, j))
    return pl.pallas_call(
        body, name="swiglu_bwd", grid=(T // tm, nf),
        in_specs=[blk, blk, pl.BlockSpec((tm, tf), lambda i, j: (i, j + nf))],
        out_specs=[blk, blk],
        out_shape=[jax.ShapeDtypeStruct((T, F), BF16)] * 2,
        compiler_params=_params(("parallel", "parallel")),
    )(da, ff, ff)


def _attn_probs(q_ref, kp_ref, bias_ref, step):
    start = pl.multiple_of(step * QROWS, QROWS)
    kb = kp_ref[pl.ds(start, UNION), :]
    s = _dot(q_ref[...], kb, NT) * (HD_A ** -0.5) + bias_ref[...]
    col = lax.broadcasted_iota(jnp.int32, s.shape, 1)
    s = jnp.where(col + start >= KPAD, s, NEG)
    p = jnp.exp(s - jnp.max(s, axis=-1, keepdims=True))
    return p / jnp.sum(p, axis=-1, keepdims=True), start


def _attn_specs(T):
    per_step = pl.BlockSpec((None, QROWS, HD_A), lambda h, n: (h, n, 0))
    per_head = lambda shape: pl.BlockSpec((None,) + shape, lambda h, n: (h, 0, 0))
    return per_step, per_head((KPAD + T, HD_A)), per_head((QROWS, UNION)), per_head((1, HD_A))


def _attn_fwd(qh, kp, vp, bias, gain):
    H, T, _ = qh.shape

    def body(q_ref, kp_ref, vp_ref, bias_ref, gain_ref, o_ref):
        p, start = _attn_probs(q_ref, kp_ref, bias_ref, pl.program_id(1))
        o = _dot(p.astype(BF16), vp_ref[pl.ds(start, UNION), :])
        rr = lax.rsqrt(jnp.mean(o * o, axis=-1, keepdims=True) + EPS)
        o_ref[...] = (o * rr * gain_ref[...]).astype(o_ref.dtype)

    per_step, keys, table, vec = _attn_specs(T)
    return pl.pallas_call(
        body, name="attn_fwd", grid=(H, T // QROWS), in_specs=[per_step, keys, keys, table, vec],
        out_specs=per_step, out_shape=jax.ShapeDtypeStruct((H, T, HD_A), BF16),
        compiler_params=_params(("parallel", "arbitrary"), 40),
    )(qh, kp, vp, bias, gain)


def _attn_bwd(qh, kp, vp, bias, gain, d_on):
    H, T, _ = qh.shape
    scale = HD_A ** -0.5

    def body(q_ref, kp_ref, vp_ref, bias_ref, gain_ref, don_ref, dq_ref, dk_ref, dv_ref, dbias_ref, dgain_ref):
        n = pl.program_id(1)

        @pl.when(n == 0)
        def _():
            dk_ref[...] = jnp.zeros_like(dk_ref)
            dv_ref[...] = jnp.zeros_like(dv_ref)
            dbias_ref[...] = jnp.zeros_like(dbias_ref)
            dgain_ref[...] = jnp.zeros_like(dgain_ref)

        p, start = _attn_probs(q_ref, kp_ref, bias_ref, n)
        keys = pl.ds(start, UNION)
        pb = p.astype(BF16)
        vb = vp_ref[keys, :]
        o = _dot(pb, vb)
        rr = lax.rsqrt(jnp.mean(o * o, axis=-1, keepdims=True) + EPS)
        on = o * rr
        d_on = don_ref[...]
        dgain_ref[...] += _colsum(d_on * on)
        dyo = d_on * gain_ref[...]
        do = rr * (dyo - on * jnp.mean(dyo * on, axis=-1, keepdims=True))
        dob = do.astype(BF16)
        dp = _dot(dob, vb, NT)
        ds = p * (dp - jnp.sum(do * o, axis=-1, keepdims=True))
        dbias_ref[...] += ds
        dsb = ds.astype(BF16)
        dq_ref[...] = (_dot(dsb, kp_ref[keys, :]) * scale).astype(dq_ref.dtype)
        dk_ref[keys, :] += _dot(dsb, q_ref[...], TN) * scale
        dv_ref[keys, :] += _dot(pb, dob, TN)

    per_step, keys, table, vec = _attn_specs(T)
    return pl.pallas_call(
        body, name="attn_bwd", grid=(H, T // QROWS), in_specs=[per_step, keys, keys, table, vec, per_step],
        out_specs=[per_step, keys, keys, table, vec],
        out_shape=[jax.ShapeDtypeStruct((H, T, HD_A), BF16), jax.ShapeDtypeStruct((H, KPAD + T, HD_A), F32),
                   jax.ShapeDtypeStruct((H, KPAD + T, HD_A), F32), jax.ShapeDtypeStruct((H, QROWS, UNION), F32),
                   jax.ShapeDtypeStruct((H, 1, HD_A), F32)],
        compiler_params=_params(("parallel", "arbitrary"), 40),
    )(qh, kp, vp, bias, gain, d_on)


N_DIAG = QROWS + UNION - 1


def _bias_table(rel_bias):
    H = rel_bias.shape[0]
    idx = np.clip(UNION - 1 - np.arange(N_DIAG), -MAX_REL, MAX_REL) + MAX_REL
    diag = rel_bias[:, idx]
    flat = jnp.broadcast_to(diag[:, None, :], (H, QROWS, N_DIAG)).reshape(H, QROWS * N_DIAG)
    toep = jnp.pad(flat, ((0, 0), (0, QROWS))).reshape(H, QROWS, N_DIAG + 1)
    toep = toep[:, ::-1, :UNION]
    g = np.arange(QROWS)[:, None] // CHUNK
    m = np.arange(UNION)[None, :] // CHUNK
    return jnp.where((m >= g) & (m <= g + N_PAST), toep, NEG)


def _bias_table_grad(dbias):
    H = dbias.shape[0]
    skew = jnp.pad(dbias[:, ::-1, :], ((0, 0), (0, 0), (0, N_DIAG + 1 - UNION)))
    skew = skew.reshape(H, QROWS * (N_DIAG + 1))[:, :QROWS * N_DIAG].reshape(H, QROWS, N_DIAG)
    diag = jnp.sum(skew, axis=1)
    n_far = UNION - MAX_REL
    far = jnp.sum(diag[:, :n_far], axis=1, keepdims=True)
    near = diag[:, n_far:][:, ::-1]
    zeros = jnp.zeros((H, MAX_REL - (QROWS - 1)), F32)
    return jnp.concatenate([zeros, near, far], axis=1)


def _tri(n, lower):
    r = lax.broadcasted_iota(jnp.int32, (n, n), 0)
    c = lax.broadcasted_iota(jnp.int32, (n, n), 1)
    return jnp.where((c <= r) if lower else (c >= r), 1.0, 0.0).astype(F32)


def _hgrn_gates(zq_ref, zf_ref, lbl_ref, q_s, k_s, b_s):
    lb = jax.nn.sigmoid(lbl_ref[0:1, :] - lbl_ref[1:2, :])
    zq = zq_ref[...]
    sig = jax.nn.sigmoid(zf_ref[...])
    f = lb + (1.0 - lb) * sig
    sq = jax.nn.sigmoid(zq)
    q_s[...] = zq * sq
    k_s[...] = 1.0 - f
    b_s[...] = _dot(_tri(CHUNK, True), jnp.log(f), precision=HIGHEST)
    return lb, sig, f, sq


def _sub_rows(i):
    return pl.ds(i * SUB, SUB)


def _row_mask(s):
    return lax.broadcasted_iota(jnp.int32, (SUB, HD_B), 0) >= s


def _decay_from(b_sub, b_row, s):
    return jnp.where(_row_mask(s), jnp.exp(jnp.minimum(b_sub - b_row, 0.0)), 0.0)


def _hgrn_fwd(proj, lb_logits, gnorm_g):
    T = proj.shape[0]
    nC = T // CHUNK
    W = lb_logits.shape[1]
    H = W // HD_B
    col0 = (proj.shape[1] - 4 * W) // HD_B

    def body(zq_ref, zf_ref, xi_ref, zg_ref, lbl_ref, gn_ref, mix_ref, o_ref, stall_ref, st_ref, q_s, k_s, b_s, acc_s):
        @pl.when(pl.program_id(1) == 0)
        def _():
            st_ref[...] = jnp.zeros_like(st_ref)

        _hgrn_gates(zq_ref, zf_ref, lbl_ref, q_s, k_s, b_s)
        q, k, b = q_s[...], k_s[...], b_s[...]
        st = st_ref[...]
        stall_ref[...] = st
        b_last = b_s[CHUNK - 1:CHUNK, :]
        acc_s[...] = _dot((q * jnp.exp(b)).astype(BF16), st.astype(BF16), NT)
        for i in range(CHUNK // SUB):
            rows = _sub_rows(i)
            q_i, b_i = q_s[rows, :], b_s[rows, :]
            acc = jnp.zeros((SUB, HD_B), F32)
            if i:
                past = pl.ds(0, i * SUB)
                b_ref = b_s[i * SUB - 1:i * SUB, :]
                qs = (q_i * jnp.exp(b_i - b_ref)).astype(BF16)
                ks = (k_s[past, :] * jnp.exp(b_ref - b_s[past, :])).astype(BF16)
                acc += _dot(_dot(qs, ks, NT).astype(BF16), xi_ref[past, :].astype(BF16))
            for s in range(SUB):
                row = pl.ds(i * SUB + s, 1)
                w = q_i * _decay_from(b_i, b_s[row, :], s)
                acc += jnp.sum(w * k_s[row, :], axis=-1, keepdims=True) * xi_ref[row, :]
            acc_s[rows, :] += acc
        o = acc_s[...]
        kd = (k * jnp.exp(b_last - b)).astype(BF16)
        st_ref[...] = st * jnp.exp(b_last) + _dot(xi_ref[...].astype(BF16), kd, TN)
        o_ref[...] = o
        zg = zg_ref[...]
        rr = lax.rsqrt(jnp.mean(o * o, axis=-1, keepdims=True) + EPS)
        mix_ref[...] = (o * rr * gn_ref[...] * (zg * jax.nn.sigmoid(zg))).astype(mix_ref.dtype)

    col = lambda part: pl.BlockSpec((CHUNK, HD_B), lambda h, n: (n, col0 + part * H + h))
    out_blk = pl.BlockSpec((CHUNK, HD_B), lambda h, n: (n, h))
    tile = pltpu.VMEM((CHUNK, HD_B), F32)
    return pl.pallas_call(
        body, name="hgrn_fwd", grid=(H, nC),
        in_specs=[col(0), col(1), col(2), col(3), pl.BlockSpec((2, HD_B), lambda h, n: (0, h)),
                  pl.BlockSpec((1, HD_B), lambda h, n: (0, 0))],
        out_specs=[out_blk, out_blk, pl.BlockSpec((None, None, HD_B, HD_B), lambda h, n: (h, n, 0, 0))],
        out_shape=[jax.ShapeDtypeStruct((T, W), BF16), jax.ShapeDtypeStruct((T, W), F32),
                   jax.ShapeDtypeStruct((H, nC, HD_B, HD_B), F32)],
        scratch_shapes=[pltpu.VMEM((HD_B, HD_B), F32), tile, tile, tile, tile],
        compiler_params=_params(("parallel", "arbitrary")),
    )(proj, proj, proj, proj, lb_logits, gnorm_g)


def _hgrn_bwd(proj, lb_logits, gnorm_g, o_b, st_all, dmixin):
    T = proj.shape[0]
    nC = T // CHUNK
    W = lb_logits.shape[1]
    H = W // HD_B
    col0 = (proj.shape[1] - 4 * W) // HD_B
    dcol0 = (dmixin.shape[1] - W) // HD_B

    def body(zq_ref, zf_ref, xi_ref, zg_ref, lbl_ref, gn_ref, o_ref, st_ref, dout_ref,
             dzq_ref, dzf_ref, dxi_ref, dzg_ref, dl0_ref, dgn_ref, dst_ref, q_s, k_s, b_s, do_s, dq_s, dk_s, di_s):
        h, n = pl.program_id(0), pl.program_id(1)

        @pl.when(n == 0)
        def _():
            dst_ref[...] = jnp.zeros_like(dst_ref)
            dl0_ref[...] = jnp.zeros_like(dl0_ref)

        @pl.when((n == 0) & (h == 0))
        def _():
            dgn_ref[...] = jnp.zeros_like(dgn_ref)

        lb, sig, f, sq = _hgrn_gates(zq_ref, zf_ref, lbl_ref, q_s, k_s, b_s)
        q, k, b = q_s[...], k_s[...], b_s[...]
        zg, o, dout = zg_ref[...], o_ref[...], dout_ref[...]
        sg = jax.nn.sigmoid(zg)
        rr = lax.rsqrt(jnp.mean(o * o, axis=-1, keepdims=True) + EPS)
        on = o * rr
        gn = gn_ref[...]
        dzg_ref[...] = (dout * on * gn * sg * (1.0 + zg * (1.0 - sg))).astype(dzg_ref.dtype)
        d_on = dout * zg * sg
        dgn_ref[...] += _colsum(d_on * on)
        d_on = d_on * gn
        do = rr * (d_on - on * jnp.mean(d_on * on, axis=-1, keepdims=True))
        do_s[...] = do
        dob = do.astype(BF16)
        st, dst = st_ref[...], dst_ref[...]
        b_last = b_s[CHUNK - 1:CHUNK, :]
        eb, e_last, k_dec = jnp.exp(b), jnp.exp(b_last), jnp.exp(b_last - b)
        qt, kd = q * eb, k * k_dec
        dstb = dst.astype(BF16)
        xib = xi_ref[...].astype(BF16)
        d_kd = _dot(xib, dstb)
        dq_s[...] = _dot(dob, st.astype(BF16)) * eb
        dk_s[...] = d_kd * k_dec
        di_s[...] = _dot(kd.astype(BF16), dstb, NT)
        d_b_last = e_last * _colsum(st * dst) + _colsum(d_kd * kd)
        dst_ref[...] = _dot(dob, qt.astype(BF16), TN) + dst * e_last
        for i in range(CHUNK // SUB):
            rows = _sub_rows(i)
            q_i, b_i, do_i = q_s[rows, :], b_s[rows, :], do_s[rows, :]
            dq_i = jnp.zeros((SUB, HD_B), F32)
            if i:
                past = pl.ds(0, i * SUB)
                b_ref = b_s[i * SUB - 1:i * SUB, :]
                e_q, e_k = jnp.exp(b_i - b_ref), jnp.exp(b_ref - b_s[past, :])
                qs, ks = (q_i * e_q).astype(BF16), (k_s[past, :] * e_k).astype(BF16)
                xi_p, do_b = xi_ref[past, :].astype(BF16), do_i.astype(BF16)
                di_s[past, :] += _dot(_dot(ks, qs, NT).astype(BF16), do_b)
                dq_i += _dot(_dot(do_b, xi_p, NT).astype(BF16), ks) * e_q
                dk_s[past, :] += _dot(_dot(xi_p, do_b, NT).astype(BF16), qs) * e_k
            for s in range(SUB):
                row = pl.ds(i * SUB + s, 1)
                k_row, i_row = k_s[row, :], xi_ref[row, :]
                e = _decay_from(b_i, b_s[row, :], s)
                w = q_i * e
                a_col = jnp.sum(w * k_row, axis=-1, keepdims=True)
                da_col = jnp.sum(do_i * i_row, axis=-1, keepdims=True)
                di_s[row, :] += _colsum(a_col * do_i)
                dq_i += da_col * e * k_row
                dk_s[row, :] += _colsum(da_col * w)
            dq_s[rows, :] += dq_i
        dq, dk = dq_s[...], dk_s[...]
        db = q * dq - k * dk
        is_last = lax.broadcasted_iota(jnp.int32, (CHUNK, HD_B), 0) == CHUNK - 1
        db = db + jnp.where(is_last, d_b_last, 0.0)
        df = _dot(_tri(CHUNK, False), db, precision=HIGHEST) / f - dk
        dzf_ref[...] = (df * (1.0 - lb) * sig * (1.0 - sig)).astype(dzf_ref.dtype)
        dl0_ref[...] += _colsum(df * (1.0 - sig)) * (lb * (1.0 - lb))
        zq = zq_ref[...]
        dzq_ref[...] = (dq * sq * (1.0 + zq * (1.0 - sq))).astype(dzq_ref.dtype)
        dxi_ref[...] = di_s[...].astype(dxi_ref.dtype)

    rev = lambda n: nC - 1 - n
    col = lambda part: pl.BlockSpec((CHUNK, HD_B), lambda h, n: (rev(n), col0 + part * H + h))
    blk = pl.BlockSpec((CHUNK, HD_B), lambda h, n: (rev(n), h))
    tile = pltpu.VMEM((CHUNK, HD_B), F32)
    out_big = jax.ShapeDtypeStruct((T, W), BF16)
    return pl.pallas_call(
        body, name="hgrn_bwd", grid=(H, nC),
        in_specs=[col(0), col(1), col(2), col(3), pl.BlockSpec((2, HD_B), lambda h, n: (0, h)),
                  pl.BlockSpec((1, HD_B), lambda h, n: (0, 0)), blk,
                  pl.BlockSpec((None, None, HD_B, HD_B), lambda h, n: (h, rev(n), 0, 0)),
                  pl.BlockSpec((CHUNK, HD_B), lambda h, n: (rev(n), dcol0 + h))],
        out_specs=[blk, blk, blk, blk, pl.BlockSpec((1, HD_B), lambda h, n: (0, h)),
                   pl.BlockSpec((1, HD_B), lambda h, n: (0, 0))],
        out_shape=[out_big, out_big, out_big, out_big, jax.ShapeDtypeStruct((1, W), F32),
                   jax.ShapeDtypeStruct((1, HD_B), F32)],
        scratch_shapes=[pltpu.VMEM((HD_B, HD_B), F32)] + [tile] * 7,
        compiler_params=_params(("arbitrary", "arbitrary")),
    )(proj, proj, proj, proj, lb_logits, gnorm_g, o_b, st_all, dmixin)


def _adamw_math(g, w, m, v):
    m = B1 * m + (1.0 - B1) * g
    v = B2 * v + (1.0 - B2) * (g * g)
    m_hat = m / (1.0 - B1 ** STEP)
    v_hat = v / (1.0 - B2 ** STEP)
    return -LR * (m_hat / (jnp.sqrt(v_hat) + ADAM_EPS) + WD * w), m, v


def _adamw(g, w, m, v, name):
    R, C = g.shape
    tr = _row_tile(R, C)

    def body(g_ref, w_ref, m_ref, v_ref, d_ref, mo_ref, vo_ref):
        d_ref[...], mo_ref[...], vo_ref[...] = _adamw_math(g_ref[...], w_ref[...], m_ref[...], v_ref[...])

    blk = pl.BlockSpec((tr, C), lambda i: (i, 0))
    return pl.pallas_call(
        body, name=name, grid=(R // tr,), in_specs=[blk] * 4, out_specs=[blk] * 3,
        out_shape=[jax.ShapeDtypeStruct((R, C), F32)] * 3, compiler_params=_params(("parallel",), 40),
    )(g, w, m, v)


def _sum_pair(g_full, from_sibling, sel, name):
    Q, K, Ns = g_full.shape
    kh = K // 2
    tr = _row_tile(kh, Ns)
    nh = kh // tr

    def body(sel_ref, a_ref, b_ref, o_ref):
        o_ref[...] = (a_ref[...].astype(F32) + b_ref[...].astype(F32)).astype(o_ref.dtype)

    return pl.pallas_call(
        body, name=name,
        grid_spec=pltpu.PrefetchScalarGridSpec(
            num_scalar_prefetch=1, grid=(Q, nh),
            in_specs=[pl.BlockSpec((None, tr, Ns), lambda q, i, sel: (q, sel[1] * nh + i, 0)),
                      pl.BlockSpec((None, tr, Ns), lambda q, i, sel: (q, i, 0))],
            out_specs=pl.BlockSpec((None, tr, Ns), lambda q, i, sel: (q, i, 0))),
        out_shape=jax.ShapeDtypeStruct((Q, kh, Ns), BF16), compiler_params=_params(("parallel", "parallel")),
    )(sel, g_full, from_sibling)


def _sum_chips(pair_sum, from_chips, sel, name):
    Q, kh, Ns = pair_sum.shape
    tr = _row_tile(kh, Ns)
    nh = kh // tr

    def body(sel_ref, a_ref, b0_ref, b1_ref, b2_ref, o_ref):
        up = lambda r: r[...].astype(F32)
        o_ref[...] = ((up(a_ref) + up(b0_ref)) + up(b1_ref)) + up(b2_ref)

    recv = lambda k: pl.BlockSpec((None, tr, Ns), lambda i, sel: (k, i, 0))
    return pl.pallas_call(
        body, name=name,
        grid_spec=pltpu.PrefetchScalarGridSpec(
            num_scalar_prefetch=1, grid=(nh,),
            in_specs=[pl.BlockSpec((None, tr, Ns), lambda i, sel: (sel[0], i, 0)), recv(0), recv(1), recv(2)],
            out_specs=pl.BlockSpec((tr, Ns), lambda i, sel: (sel[1] * nh + i, 0))),
        out_shape=jax.ShapeDtypeStruct((2 * kh, Ns), F32), compiler_params=_params(("parallel",)),
    )(sel, pair_sum, from_chips, from_chips, from_chips)


ANY = pl.BlockSpec(memory_space=pl.ANY)
CHIP_FLIPS = ((1, 0), (0, 1), (1, 1))


def _gather_small(v, name):
    R, L = v.shape

    def body(v_ref, out_ref, send_sems, recv_sems):
        x, y, c = _place()
        me = 4 * x + 2 * y + c
        out_ref[me] = v_ref[...]
        peers = [(_flip(x, k >> 2 & 1), _flip(y, k >> 1 & 1), _flip(c, k & 1)) for k in range(1, N_DEV)]

        def copy(k, row, to):
            return pltpu.make_async_remote_copy(src_ref=v_ref, dst_ref=out_ref.at[row], send_sem=send_sems.at[k],
                                                recv_sem=recv_sems.at[k], device_id=to, device_id_type=MESH)

        sends = [copy(k, me, peer) for k, peer in enumerate(peers)]
        for cp in sends:
            cp.start()
        for k, (px, py, pc) in enumerate(peers):
            copy(k, 4 * px + 2 * py + pc, (x, y, c)).wait_recv()
        for cp in sends:
            cp.wait_send()

    vmem = pl.BlockSpec(memory_space=pltpu.VMEM)
    return pl.pallas_call(
        body, name=name, in_specs=[vmem], out_specs=vmem, out_shape=jax.ShapeDtypeStruct((N_DEV, R, L), F32),
        scratch_shapes=[pltpu.SemaphoreType.DMA((N_DEV - 1,)), pltpu.SemaphoreType.DMA((N_DEV - 1,))],
    )(v)


def _gather_weight(shard, name):
    K, Ns = shard.shape
    kh = K // 2

    def body(w_ref, out_ref, send_sems, recv_sems):
        x, y, c = _place()
        chips = [(_flip(x, fx), _flip(y, fy)) for fx, fy in CHIP_FLIPS]

        def half(chip, which):
            return out_ref.at[2 * chip[0] + chip[1], pl.ds(which * kh, kh), :]

        def copy(k, dst, to, src=None):
            return pltpu.make_async_remote_copy(src_ref=dst if src is None else src, dst_ref=dst,
                                                send_sem=send_sems.at[k], recv_sem=recv_sems.at[k],
                                                device_id=to, device_id_type=MESH)

        first = [copy(j, half((x, y), c), (*chip, c), src=w_ref.at[pl.ds(c * kh, kh), :])
                 for j, chip in enumerate(chips)]
        for cp in first:
            cp.start()
        passed = [copy(3 + j, half(chip, c), (x, y, 1 - c)) for j, chip in enumerate(chips)]
        for j, chip in enumerate(chips):
            copy(j, half(chip, c), (x, y, c)).wait_recv()
            passed[j].start()
        for j, chip in enumerate(chips):
            copy(3 + j, half(chip, 1 - c), (x, y, c)).wait_recv()
        for cp in first + passed:
            cp.wait_send()

    return pl.pallas_call(
        body, name=name, in_specs=[ANY], out_specs=ANY,
        out_shape=jax.ShapeDtypeStruct((N_CHIPS, K, Ns), shard.dtype),
        scratch_shapes=[pltpu.SemaphoreType.DMA((6,)), pltpu.SemaphoreType.DMA((6,))],
    )(shard)


def _send_pair(g_full, name):
    Q, K, Ns = g_full.shape
    kh = K // 2

    def body(g_ref, got_ref, send_sem, recv_sem):
        x, y, c = _place()
        cp = pltpu.make_async_remote_copy(src_ref=g_ref.at[:, pl.ds((1 - c) * kh, kh), :], dst_ref=got_ref,
                                          send_sem=send_sem, recv_sem=recv_sem, device_id=(x, y, 1 - c),
                                          device_id_type=MESH)
        cp.start()
        cp.wait()

    return pl.pallas_call(
        body, name=name, in_specs=[ANY], out_specs=ANY, out_shape=jax.ShapeDtypeStruct((Q, kh, Ns), g_full.dtype),
        scratch_shapes=[pltpu.SemaphoreType.DMA, pltpu.SemaphoreType.DMA],
    )(g_full)


def _send_chips(pair_sum, name):
    Q, kh, Ns = pair_sum.shape

    def body(p_ref, got_ref, send_sems, recv_sems):
        x, y, c = _place()
        sends = []
        for j, (fx, fy) in enumerate(CHIP_FLIPS):
            px, py = _flip(x, fx), _flip(y, fy)
            sends.append(pltpu.make_async_remote_copy(
                src_ref=p_ref.at[2 * px + py], dst_ref=got_ref.at[j], send_sem=send_sems.at[j],
                recv_sem=recv_sems.at[j], device_id=(px, py, c), device_id_type=MESH))
        for cp in sends:
            cp.start()
        for cp in sends:
            cp.wait_recv()
        for cp in sends:
            cp.wait_send()

    return pl.pallas_call(
        body, name=name, in_specs=[ANY], out_specs=ANY,
        out_shape=jax.ShapeDtypeStruct((Q - 1, kh, Ns), pair_sum.dtype),
        scratch_shapes=[pltpu.SemaphoreType.DMA((3,)), pltpu.SemaphoreType.DMA((3,))],
    )(pair_sum)


def _share_halves(block, name):
    K, Ns = block.shape
    kh = K // 2

    def body(_, out_ref, send_sem, recv_sem):
        x, y, c = _place()
        mine, theirs = out_ref.at[pl.ds(c * kh, kh), :], out_ref.at[pl.ds((1 - c) * kh, kh), :]
        cp = pltpu.make_async_remote_copy(src_ref=mine, dst_ref=mine, send_sem=send_sem, recv_sem=recv_sem,
                                          device_id=(x, y, 1 - c), device_id_type=MESH)
        cp.start()
        pltpu.make_async_remote_copy(src_ref=theirs, dst_ref=theirs, send_sem=send_sem, recv_sem=recv_sem,
                                     device_id=(x, y, 1 - c), device_id_type=MESH).wait_recv()
        cp.wait_send()

    return pl.pallas_call(
        body, name=name, in_specs=[ANY], out_specs=ANY, out_shape=jax.ShapeDtypeStruct((K, Ns), block.dtype),
        input_output_aliases={0: 0},
        scratch_shapes=[pltpu.SemaphoreType.DMA, pltpu.SemaphoreType.DMA],
    )(block)


def _reduce_weight_grad(g_full, sel, tag):
    pair = _sum_pair(g_full, _send_pair(g_full, tag + "_send_pair"), sel, tag + "_sum_pair")
    block = _sum_chips(pair, _send_chips(pair, tag + "_send_chips"), sel, tag + "_sum_chips")
    return _share_halves(block, tag + "_share")


def _silu(v):
    return v * jax.nn.sigmoid(v)


def _ada_fwd(c_all, w_ada, tn=512):
    M, D = c_all.shape
    Ns = w_ada.shape[1]

    def body(c_ref, w_ref, o_ref):
        o_ref[...] = _dot(_silu(c_ref[...]).astype(BF16), w_ref[...].astype(BF16))

    return pl.pallas_call(
        body, name="ada_fwd", grid=(Ns // tn,),
        in_specs=[pl.BlockSpec((M, D), lambda j: (0, 0)), pl.BlockSpec((D, tn), lambda j: (0, j))],
        out_specs=pl.BlockSpec((M, tn), lambda j: (0, j)), out_shape=jax.ShapeDtypeStruct((M, Ns), F32),
        compiler_params=_params(("parallel",)),
    )(c_all, w_ada)


def _ada_bwd(c_all, dmod, w, m, v, tk=256, tn=1536):
    M, D = c_all.shape
    Ns = dmod.shape[1]

    def body(c_ref, d_ref, w_ref, m_ref, v_ref, g_ref, dl_ref, mo_ref, vo_ref):
        g = _dot(_silu(c_ref[...]).astype(BF16), d_ref[...].astype(BF16), TN)
        g_ref[...] = g
        dl_ref[...], mo_ref[...], vo_ref[...] = _adamw_math(g, w_ref[...], m_ref[...], v_ref[...])

    blk = pl.BlockSpec((tk, tn), lambda i, j: (i, j))
    return pl.pallas_call(
        body, name="ada_bwd", grid=(D // tk, Ns // tn),
        in_specs=[pl.BlockSpec((M, tk), lambda i, j: (0, i)), pl.BlockSpec((M, tn), lambda i, j: (0, j)), blk, blk, blk],
        out_specs=[blk] * 4, out_shape=[jax.ShapeDtypeStruct((D, Ns), F32)] * 4,
        compiler_params=_params(("parallel", "parallel"), 40),
    )(c_all, dmod, w, m, v)


def _small_update(g_all, w, m, v):
    R, L = w.shape

    def body(g_ref, w_ref, m_ref, v_ref, go_ref, d_ref, mo_ref, vo_ref):
        g = g_ref[0]
        for d in range(1, N_DEV):
            g = g + g_ref[d]
        go_ref[...] = g
        d_ref[...], mo_ref[...], vo_ref[...] = _adamw_math(g, w_ref[...], m_ref[...], v_ref[...])

    return pl.pallas_call(body, name="small_update", out_shape=[jax.ShapeDtypeStruct((R, L), F32)] * 4)(g_all, w, m, v)


def _pack(parts, rows):
    flat = jnp.concatenate([p.reshape(-1) for p in parts])
    return jnp.pad(flat, (0, rows * 128 - flat.shape[0])).reshape(rows, 128)


def _unpack(packed, shapes):
    flat, out, at = packed.reshape(-1), [], 0
    for shp in shapes:
        size = 1
        for d in shp:
            size *= d
        out.append(flat[at:at + size].reshape(shp))
        at += size
    return out


def _heads(a, pad):
    T = a.shape[0]
    h = a.reshape(T, -1, HD_A).transpose(1, 0, 2).astype(BF16)
    return jnp.pad(h, ((0, 0), (pad, 0), (0, 0))) if pad else h


def _unheads(a):
    return a.transpose(1, 0, 2).reshape(a.shape[1], -1)


def _layer(x, tgt, mod, w_in, w_o, w_ffn_in, w_ffn_out, rel_bias, attn_norm_g, lb_logits, gnorm_g, ln1_g, ln1_b,
           ln2_g, ln2_b):
    T, D = x.shape
    aw = attn_norm_g.shape[1]
    shift1, scale1, gate1, shift2, scale2, gate2 = [mod[i:i + 1] for i in range(6)]
    w_o3 = w_o.reshape(1, -1, w_o.shape[2])
    w_out3 = w_ffn_out.reshape(1, -1, w_ffn_out.shape[2])
    d_ff = w_out3.shape[1]
    h1 = _pre_mixer(x, scale1, shift1)
    proj = _mm_nn(h1, w_in, tm=1024, tn=256, tk=D, name="proj")
    qh = _heads(proj[:, :aw], 0)
    kp = _heads(proj[:, aw:2 * aw], KPAD)
    vp = _heads(proj[:, 2 * aw:3 * aw], KPAD)
    bias = _bias_table(rel_bias)
    gain = attn_norm_g.reshape(-1, 1, HD_A)
    mix_a = _unheads(_attn_fwd(qh, kp, vp, bias, gain))
    mix_b, o_b, st_all = _hgrn_fwd(proj, lb_logits, gnorm_g)
    mixin = jnp.concatenate([mix_a, mix_b], axis=1)
    mix = _mm_nn(mixin, w_o3, tm=1024, tn=512, tk=D, name="mix_out")
    x1, h2 = _post_mixer(mix, x, gate1, ln1_g, ln1_b, scale2, shift2)
    ff = _mm_nn(h2, w_ffn_in, tm=1024, tn=256, tk=D, name="ffn_in")
    act = _swiglu_fwd(ff)
    f = _mm_nn(act, w_out3, tm=1024, tn=1024, tk=d_ff // 4, name="ffn_out")
    du2, df, acc2 = _loss_head(f, x1, tgt, gate2, ln2_g, ln2_b)
    loss = (0.5 / D) * jnp.sum(acc2[3])
    da = _mm_nt(df, w_out3, tm=1024, to=512, tn=D, name="d_act")
    g_ffn_out = _mm_tn(act, df, q=1, tk=512, tn=1024, tt=T, name="g_ffn_out")
    dff = jnp.concatenate(_swiglu_bwd(da, ff), axis=1)
    dh2 = _mm_nt(dff, w_ffn_in, tm=1024, to=1024, tn=w_ffn_in.shape[2], name="d_h2")
    g_ffn_in = _mm_tn(h2, dff, q=N_CHIPS, tk=512, tn=w_ffn_in.shape[2] // 2, tt=T, name="g_ffn_in")
    du1, dmix, acc1 = _mid_bwd(dh2, du2, x1, mix, x, gate1, ln1_g, scale2)
    dmixin = _mm_nt(dmix, w_o3, tm=1024, to=512, tn=D, name="d_mixin")
    g_o = _mm_tn(mixin, dmix, q=1, tk=512, tn=1024, tt=T, name="g_o")
    d_on = dmixin[:, :aw].reshape(T, -1, HD_A).transpose(1, 0, 2)
    dq, dk, dv, dbias, dgain = _attn_bwd(qh, kp, vp, bias, gain, d_on)
    dzq, dzf, dxi, dzg, dl0, dgn = _hgrn_bwd(proj, lb_logits, gnorm_g, o_b, st_all, dmixin)
    dproj = jnp.concatenate([_unheads(dq), _unheads(dk[:, KPAD:]).astype(BF16), _unheads(dv[:, KPAD:]).astype(BF16),
                             dzq, dzf, dxi, dzg], axis=1)
    dh1 = _mm_nt(dproj, w_in, tm=1024, to=1024, tn=w_in.shape[2], name="d_h1")
    g_in = _mm_tn(h1, dproj, q=N_CHIPS, tk=512, tn=w_in.shape[2] // 2, tt=T, name="g_in")
    grad_x, acc0 = _first_bwd(dh1, du1, x, scale1)
    dmod = jnp.concatenate([acc0[1:2], acc0[0:1], acc1[4:5], acc1[1:2], acc1[0:1], acc2[2:3]], axis=0)
    small = dict(rel_bias=_bias_table_grad(dbias), attn_norm_g=dgain.reshape(1, -1),
                 lb_logits=jnp.concatenate([dl0, -dl0], axis=0), gnorm_g=dgn,
                 ln1_g=acc1[2:3], ln1_b=acc1[3:4], ln2_g=acc2[0:1], ln2_b=acc2[1:2])
    return loss, grad_x, (g_in, g_o.reshape(N_CHIPS, -1, D), g_ffn_in, g_ffn_out.reshape(N_CHIPS, -1, D)), dmod, small


SMALL = ("rel_bias", "attn_norm_g", "lb_logits", "gnorm_g", "ln1_g", "ln1_b", "ln2_g", "ln2_b")
SMALL_ROWS = 256


def kernel(x, c, w_ada, b_ada, w_in, rel_bias, attn_norm_g, lb_logits, gnorm_g, w_o, ln1_g, ln1_b, w_ffn_in, w_ffn_out, ln2_g, ln2_b, loss_target, m_w_ada, m_b_ada, m_w_in, m_rel_bias, m_attn_norm_g, m_lb_logits, m_gnorm_g, m_w_o, m_ln1_g, m_ln1_b, m_w_ffn_in, m_w_ffn_out, m_ln2_g, m_ln2_b, v_w_ada, v_b_ada, v_w_in, v_rel_bias, v_attn_norm_g, v_lb_logits, v_gnorm_g, v_w_o, v_ln1_g, v_ln1_b, v_w_ffn_in, v_w_ffn_out, v_ln2_g, v_ln2_b):
    mx, my, mc = _place()
    me = 4 * mx + 2 * my + mc
    chip = 2 * mx + my
    sel = jnp.stack([chip, mc]).astype(jnp.int32)
    D = x.shape[2]
    ns_ada = w_ada.shape[2]

    big = dict(w_in=(w_in, m_w_in, v_w_in), w_o=(w_o, m_w_o, v_w_o), w_ffn_in=(w_ffn_in, m_w_ffn_in, v_w_ffn_in),
               w_ffn_out=(w_ffn_out, m_w_ffn_out, v_w_ffn_out))
    full = {}
    for n, t in big.items():
        shard = t[0][0].astype(BF16)
        full[n] = lax.dynamic_update_index_in_dim(_gather_weight(shard, "gather_" + n), shard, chip, 0)

    c_all = _gather_small(c.reshape(D // 128, 128), "gather_c").reshape(N_DEV, D)
    c_all = jnp.pad(c_all, ((0, 16 - N_DEV), (0, 0)))
    mod_cols = _ada_fwd(c_all, w_ada[0])[:N_DEV]
    mod_all = _gather_small(mod_cols.reshape(-1, 128), "gather_mod").reshape(N_DEV, N_DEV, ns_ada)
    mod = lax.dynamic_index_in_dim(mod_all[::2], me, axis=1, keepdims=False)
    mod = (mod.reshape(1, -1) + b_ada).reshape(6, D)

    loss, grad_x, g_big, dmod, g_small = _layer(
        x[0], loss_target[0], mod, full["w_in"], full["w_o"], full["w_ffn_in"], full["w_ffn_out"], rel_bias[0],
        attn_norm_g, lb_logits, gnorm_g, ln1_g, ln1_b, ln2_g, ln2_b)
    loss = lax.psum(loss, ("x", "y", "c"))

    grads, deltas, new_m, new_v = {}, {}, {}, {}
    for (n, (w, m, v)), g_full in zip(big.items(), g_big):
        g = _reduce_weight_grad(g_full, sel, "rs_" + n)
        d, mo, vo = _adamw(g, w[0], m[0], v[0], "adamw_" + n)
        grads[n], deltas[n], new_m[n], new_v[n] = g[None], d[None], mo[None], vo[None]

    small_in = dict(rel_bias=(rel_bias, m_rel_bias, v_rel_bias), attn_norm_g=(attn_norm_g, m_attn_norm_g, v_attn_norm_g),
                    lb_logits=(lb_logits, m_lb_logits, v_lb_logits), gnorm_g=(gnorm_g, m_gnorm_g, v_gnorm_g),
                    ln1_g=(ln1_g, m_ln1_g, v_ln1_g), ln1_b=(ln1_b, m_ln1_b, v_ln1_b), ln2_g=(ln2_g, m_ln2_g, v_ln2_g),
                    ln2_b=(ln2_b, m_ln2_b, v_ln2_b))
    g_all = _gather_small(_pack([dmod] + [g_small[n] for n in SMALL], SMALL_ROWS), "gather_small")
    packed = [_pack([t] + [small_in[n][i] for n in SMALL], SMALL_ROWS)
              for i, t in enumerate((b_ada, m_b_ada, v_b_ada))]
    shapes = [b_ada.shape] + [small_in[n][0].shape for n in SMALL]
    outs = [_unpack(o, shapes) for o in _small_update(g_all, *packed)]
    for i, n in enumerate(("b_ada",) + SMALL):
        grads[n], deltas[n], new_m[n], new_v[n] = outs[0][i], outs[1][i], outs[2][i], outs[3][i]

    dmod_all = g_all[:, :6 * D // 128].reshape(N_DEV, 6 * D)
    dmod_cols = lax.dynamic_slice_in_dim(dmod_all, chip * ns_ada, ns_ada, axis=1)
    dmod_cols = jnp.pad(dmod_cols, ((0, 16 - N_DEV), (0, 0)))
    g, d, mo, vo = _ada_bwd(c_all, dmod_cols, w_ada[0], m_w_ada[0], v_w_ada[0])
    grads["w_ada"], deltas["w_ada"], new_m["w_ada"], new_v["w_ada"] = g[None], d[None], mo[None], vo[None]

    order = ("w_ada", "b_ada", "w_in", "rel_bias", "attn_norm_g", "lb_logits", "gnorm_g", "w_o", "ln1_g", "ln1_b",
             "w_ffn_in", "w_ffn_out", "ln2_g", "ln2_b")
    return (loss, grad_x[None], *[grads[n] for n in order], *[deltas[n] for n in order],
            *[new_m[n] for n in order], *[new_v[n] for n in order])
```

```python
import numpy as np
import jax
import jax.numpy as jnp
from jax import lax
from jax.experimental import pallas as pl
from jax.experimental.pallas import tpu as pltpu

F32 = jnp.float32
BF16 = jnp.bfloat16
MESH = pl.DeviceIdType.MESH
HIGHEST = lax.Precision.HIGHEST

CHUNK = 64
N_PAST = 8
QG = 4
QROWS = QG * CHUNK
KPAD = N_PAST * CHUNK
UNION = (QG + N_PAST) * CHUNK
HD_A = 64
HD_B = 128
SUB = 16
MAX_REL = 256
EPS = 1e-5
ALPHA = 2.0 ** 0.25
LR, B1, B2, ADAM_EPS, WD, STEP = 1e-3, 0.9, 0.999, 1e-8, 0.01, 10
N_CHIPS = 4
N_DEV = 8
NEG = -1e30
TILE_BYTES = 3 << 19

NN = ((1,), (0,))
NT = ((1,), (1,))
TN = ((0,), (0,))


def _dot(a, b, dims=NN, precision=None):
    return lax.dot_general(a, b, (dims, ((), ())), preferred_element_type=F32, precision=precision)


def _params(sem=None, vmem_mb=None, **kw):
    return pltpu.CompilerParams(dimension_semantics=sem,
                                vmem_limit_bytes=None if vmem_mb is None else vmem_mb << 20, **kw)


def _row_tile(rows, cols):
    for cand in (512, 256, 128, 64, 32, 16, 8):
        if rows % cand == 0 and cand * cols * 4 <= TILE_BYTES:
            return cand
    raise ValueError((rows, cols))


def _place():
    return lax.axis_index("x"), lax.axis_index("y"), lax.axis_index("c")


def _flip(v, bit):
    return 1 - v if bit else v


ANY = pl.BlockSpec(memory_space=pl.ANY)
CHIP_FLIPS = ((1, 0), (0, 1), (1, 1))


class _Rider:
    def __init__(self, operands, out_shape, n_sems, start, finish):
        self.operands, self.out_shape, self.n_sems, self.start, self.finish = operands, out_shape, n_sems, start, finish


def _call(body, rider, *, name, grid, in_specs, out_specs, out_shape, scratch_shapes, compiler_params, operands):
    if rider is None:
        outs = pl.pallas_call(body, name=name, grid=grid, in_specs=in_specs, out_specs=out_specs, out_shape=out_shape,
                              scratch_shapes=scratch_shapes, compiler_params=compiler_params)(*operands)
        return list(outs), []
    n_in, n_out, n_sc = len(in_specs), len(out_specs), len(scratch_shapes)
    r_in, r_out = len(rider.operands), len(rider.out_shape)

    def carried(*refs):
        refs = list(refs)
        cuts = [n_in, r_in, n_out, r_out, n_sc]
        ins, r_ins, outs, r_outs, scratch = [[refs.pop(0) for _ in range(n)] for n in cuts]
        first, last = None, None
        for axis, size in enumerate(grid):
            i = pl.program_id(axis)
            first = (i == 0) if first is None else first & (i == 0)
            last = (i == size - 1) if last is None else last & (i == size - 1)

        @pl.when(first)
        def _():
            rider.start(r_ins, r_outs, *refs)

        body(*ins, *outs, *scratch)

        @pl.when(last)
        def _():
            rider.finish(r_ins, r_outs, *refs)

    sems = [pltpu.SemaphoreType.DMA((rider.n_sems,)), pltpu.SemaphoreType.DMA((rider.n_sems,))]
    outs = pl.pallas_call(carried, name=name, grid=grid, in_specs=list(in_specs) + [ANY] * r_in,
                          out_specs=list(out_specs) + [ANY] * r_out, out_shape=list(out_shape) + rider.out_shape,
                          scratch_shapes=list(scratch_shapes) + sems, compiler_params=compiler_params,
                          )(*operands, *rider.operands)
    return list(outs[:n_out]), list(outs[n_out:])


def _alone(rider, name):
    def body(*refs):
        ins, outs, sems = refs[:len(rider.operands)], refs[len(rider.operands):-2], refs[-2:]
        rider.start(ins, outs, *sems)
        rider.finish(ins, outs, *sems)

    return pl.pallas_call(
        body, name=name, in_specs=[ANY] * len(rider.operands), out_specs=[ANY] * len(rider.out_shape),
        out_shape=rider.out_shape,
        scratch_shapes=[pltpu.SemaphoreType.DMA((rider.n_sems,)), pltpu.SemaphoreType.DMA((rider.n_sems,))],
    )(*rider.operands)


def _gather_rider(shard):
    K, Ns = shard.shape
    kh = K // 2

    def copies(w_ref, out_ref, send_sems, recv_sems):
        x, y, c = _place()
        chips = [(_flip(x, fx), _flip(y, fy)) for fx, fy in CHIP_FLIPS]

        def half(chip, which):
            return out_ref.at[2 * chip[0] + chip[1], pl.ds(which * kh, kh), :]

        def copy(k, dst, to, src=None):
            return pltpu.make_async_remote_copy(src_ref=dst if src is None else src, dst_ref=dst,
                                                send_sem=send_sems.at[k], recv_sem=recv_sems.at[k],
                                                device_id=to, device_id_type=MESH)

        def first():
            return [copy(j, half((x, y), c), (*chip, c), src=w_ref.at[pl.ds(c * kh, kh), :])
                    for j, chip in enumerate(chips)]

        def onward():
            return [copy(3 + j, half(chip, c), (x, y, 1 - c)) for j, chip in enumerate(chips)]

        def arriving(base, which):
            return [copy(base + j, half(chip, which), (x, y, c)) for j, chip in enumerate(chips)]

        return first, onward, arriving

    def start(ins, outs, send_sems, recv_sems):
        for cp in copies(ins[0], outs[0], send_sems, recv_sems)[0]():
            cp.start()

    def finish(ins, outs, send_sems, recv_sems):
        x, y, c = _place()
        first, onward, arriving = copies(ins[0], outs[0], send_sems, recv_sems)
        passed = onward()
        for arrived, cp in zip(arriving(0, c), passed):
            arrived.wait_recv()
            cp.start()
        for arrived in arriving(3, 1 - c):
            arrived.wait_recv()
        for cp in first() + passed:
            cp.wait_send()

    return _Rider([shard], [jax.ShapeDtypeStruct((N_CHIPS, K, Ns), shard.dtype)], 6, start, finish)


def _chips_rider(pair_sum):
    Q, kh, Ns = pair_sum.shape

    def copies(p_ref, got_ref, send_sems, recv_sems):
        x, y, c = _place()
        out = []
        for j, (fx, fy) in enumerate(CHIP_FLIPS):
            px, py = _flip(x, fx), _flip(y, fy)
            out.append(pltpu.make_async_remote_copy(
                src_ref=p_ref.at[2 * px + py], dst_ref=got_ref.at[j], send_sem=send_sems.at[j],
                recv_sem=recv_sems.at[j], device_id=(px, py, c), device_id_type=MESH))
        return out

    def start(ins, outs, send_sems, recv_sems):
        for cp in copies(ins[0], outs[0], send_sems, recv_sems):
            cp.start()

    def finish(ins, outs, send_sems, recv_sems):
        sends = copies(ins[0], outs[0], send_sems, recv_sems)
        for cp in sends:
            cp.wait_recv()
        for cp in sends:
            cp.wait_send()

    return _Rider([pair_sum], [jax.ShapeDtypeStruct((Q - 1, kh, Ns), pair_sum.dtype)], 3, start, finish)


def _mm(a, b, *, grid, a_spec, b_spec, o_spec, o_shape, o_dtype, dims, acc_shape, name, rider=None, vmem_mb=48):
    nk = grid[2]

    def body(a_ref, b_ref, o_ref, *scratch):
        part = _dot(a_ref[...], b_ref[...], dims)
        if nk == 1:
            o_ref[...] = part.astype(o_ref.dtype)
            return
        acc_ref, = scratch
        k = pl.program_id(2)

        @pl.when(k == 0)
        def _():
            acc_ref[...] = part

        @pl.when(k > 0)
        def _():
            acc_ref[...] += part

        @pl.when(k == nk - 1)
        def _():
            o_ref[...] = acc_ref[...].astype(o_ref.dtype)

    (out,), rode = _call(
        body, rider, name=name, grid=grid, in_specs=[a_spec, b_spec], out_specs=[o_spec],
        out_shape=[jax.ShapeDtypeStruct(o_shape, o_dtype)],
        scratch_shapes=[] if nk == 1 else [pltpu.VMEM(acc_shape, F32)],
        compiler_params=_params(("parallel", "parallel", "arbitrary") if rider is None else ("arbitrary",) * 3, vmem_mb),
        operands=(a, b))
    return out if rider is None else (out, rode)


def _mm_nn(a, w, *, tm, tn, tk, name, rider=None):
    T, K = a.shape
    Q, _, Ns = w.shape
    nbs = Ns // tn
    tm = min(tm, T)
    return _mm(a, w, grid=(T // tm, Q * nbs, K // tk),
               a_spec=pl.BlockSpec((tm, tk), lambda i, j, k: (i, k)),
               b_spec=pl.BlockSpec((None, tk, tn), lambda i, j, k: (j // nbs, k, j % nbs)),
               o_spec=pl.BlockSpec((tm, tn), lambda i, j, k: (i, j)),
               o_shape=(T, Q * Ns), o_dtype=F32, dims=NN, acc_shape=(tm, tn), name=name, rider=rider)


def _mm_nt(g, w, *, tm, to, tn, name, rider=None):
    T = g.shape[0]
    Q, K, Ns = w.shape
    nbs = Ns // tn
    tm = min(tm, T)
    return _mm(g, w, grid=(T // tm, K // to, Q * nbs),
               a_spec=pl.BlockSpec((tm, tn), lambda i, j, n: (i, n)),
               b_spec=pl.BlockSpec((None, to, tn), lambda i, j, n: (n // nbs, j, n % nbs)),
               o_spec=pl.BlockSpec((tm, to), lambda i, j, n: (i, j)),
               o_shape=(T, K), o_dtype=F32, dims=NT, acc_shape=(tm, to), name=name, rider=rider)


def _mm_tn(a, g, *, q, tk, tn, tt, name):
    T, K = a.shape
    Ns = g.shape[1] // q
    nbs = Ns // tn
    return _mm(a, g, grid=(K // tk, q * nbs, T // tt),
               a_spec=pl.BlockSpec((tt, tk), lambda i, j, t: (t, i)),
               b_spec=pl.BlockSpec((tt, tn), lambda i, j, t: (t, j)),
               o_spec=pl.BlockSpec((None, tk, tn), lambda i, j, t: (j // nbs, i, j % nbs)),
               o_shape=(q, K, Ns), o_dtype=BF16, dims=TN, acc_shape=(tk, tn), name=name)


def _ln(u):
    mu = jnp.mean(u, axis=-1, keepdims=True)
    d = u - mu
    r = lax.rsqrt(jnp.mean(d * d, axis=-1, keepdims=True) + EPS)
    return d * r, r


def _ln_bwd(dy, un, r):
    return r * (dy - jnp.mean(dy, axis=-1, keepdims=True) - un * jnp.mean(dy * un, axis=-1, keepdims=True))


def _colsum(v):
    return jnp.sum(v, axis=0, keepdims=True)


def _rowwise(name, fn, bigs, vecs, out_dtypes, n_acc, tm=128):
    T, D = bigs[0].shape
    nb, nv, no = len(bigs), len(vecs), len(out_dtypes)

    def body(*refs):
        outs, accs = fn([r[...] for r in refs[:nb]], [r[...] for r in refs[nb:nb + nv]])
        for r, o in zip(refs[nb + nv:nb + nv + no], outs):
            r[...] = o.astype(r.dtype)
        if n_acc:
            acc_ref = refs[nb + nv + no]

            @pl.when(pl.program_id(0) == 0)
            def _():
                acc_ref[...] = jnp.zeros_like(acc_ref)

            for row, a in enumerate(accs):
                acc_ref[row:row + 1, :] += a

    big_spec = pl.BlockSpec((tm, D), lambda i: (i, 0))
    vec_spec = pl.BlockSpec((1, D), lambda i: (0, 0))
    out_shape = [jax.ShapeDtypeStruct((T, D), dt) for dt in out_dtypes]
    out_specs = [big_spec] * no
    if n_acc:
        out_shape.append(jax.ShapeDtypeStruct((8, D), F32))
        out_specs.append(pl.BlockSpec((8, D), lambda i: (0, 0)))
    return pl.pallas_call(
        body, name=name, grid=(T // tm,), in_specs=[big_spec] * nb + [vec_spec] * nv,
        out_specs=out_specs, out_shape=out_shape,
        compiler_params=_params(("arbitrary",), 48),
    )(*bigs, *vecs)


def _pre_mixer(x, scale1, shift1):
    def fn(b, v):
        xn, _ = _ln(b[0])
        return [xn * (1.0 + v[0]) + v[1]], []
    return _rowwise("pre_mixer", fn, [x], [scale1, shift1], [BF16], 0)[0]


def _post_mixer(mix, x, gate1, g1, b1, scale2, shift2):
    def fn(b, v):
        un1, _ = _ln(ALPHA * b[1] + v[0] * b[0])
        x1 = un1 * v[1] + v[2]
        xn1, _ = _ln(x1)
        return [x1, xn1 * (1.0 + v[3]) + v[4]], []
    return _rowwise("post_mixer", fn, [mix, x], [gate1, g1, b1, scale2, shift2], [F32, BF16], 0)


def _loss_head(f, x1, tgt, gate2, g2, b2):
    def fn(b, v):
        ff, xx, tt = b
        d_model = ff.shape[-1]
        un2, r2 = _ln(ALPHA * xx + v[0] * ff)
        err = un2 * v[1] + v[2] - tt
        dy = err * (1.0 / d_model)
        du2 = _ln_bwd(dy * v[1], un2, r2)
        return [du2, du2 * v[0]], [_colsum(dy * un2), _colsum(dy), _colsum(du2 * ff), _colsum(err * err)]
    return _rowwise("loss_head", fn, [f, x1, tgt], [gate2, g2, b2], [F32, BF16], 4)


def _mid_bwd(dh2, du2, x1, mix, x, gate1, g1, scale2):
    def fn(b, v):
        dh, du, xx1, mm, xx = b
        xn1, r1n = _ln(xx1)
        dx1 = ALPHA * du + _ln_bwd(dh * (1.0 + v[2]), xn1, r1n)
        un1, r1 = _ln(ALPHA * xx + v[0] * mm)
        du1 = _ln_bwd(dx1 * v[1], un1, r1)
        return [du1, du1 * v[0]], [_colsum(dh * xn1), _colsum(dh), _colsum(dx1 * un1), _colsum(dx1),
                                   _colsum(du1 * mm)]
    return _rowwise("mid_bwd", fn, [dh2, du2, x1, mix, x], [gate1, g1, scale2], [F32, BF16], 5)


def _first_bwd(dh1, du1, x, scale1):
    def fn(b, v):
        dh, du, xx = b
        xn, r0 = _ln(xx)
        return [ALPHA * du + _ln_bwd(dh * (1.0 + v[0]), xn, r0)], [_colsum(dh * xn), _colsum(dh)]
    return _rowwise("first_bwd", fn, [dh1, du1, x], [scale1], [F32], 2)


def _swiglu_fwd(ff, tm=256, tf=512):
    T, F2 = ff.shape
    F = F2 // 2
    nf = F // tf

    def body(g_ref, u_ref, a_ref):
        g = g_ref[...]
        a_ref[...] = (g * jax.nn.sigmoid(g) * u_ref[...]).astype(a_ref.dtype)

    return pl.pallas_call(
        body, name="swiglu_fwd", grid=(T // tm, nf),
        in_specs=[pl.BlockSpec((tm, tf), lambda i, j: (i, j)), pl.BlockSpec((tm, tf), lambda i, j: (i, j + nf))],
        out_specs=pl.BlockSpec((tm, tf), lambda i, j: (i, j)),
        out_shape=jax.ShapeDtypeStruct((T, F), BF16),
        compiler_params=_params(("parallel", "parallel")),
    )(ff, ff)


def _swiglu_bwd(da, ff, tm=256, tf=512):
    T, F = da.shape
    nf = F // tf

    def body(da_ref, g_ref, u_ref, dg_ref, du_ref):
        g = g_ref[...]
        d = da_ref[...]
        s = jax.nn.sigmoid(g)
        du_ref[...] = (d * g * s).astype(du_ref.dtype)
        dg_ref[...] = (d * u_ref[...] * s * (1.0 + g * (1.0 - s))).astype(dg_ref.dtype)

    blk = pl.BlockSpec((tm, tf), lambda i, j: (i, j))
    return pl.pallas_call(
        body, name="swiglu_bwd", grid=(T // tm, nf),
        in_specs=[blk, blk, pl.BlockSpec((tm, tf), lambda i, j: (i, j + nf))],
        out_specs=[blk, blk],
        out_shape=[jax.ShapeDtypeStruct((T, F), BF16)] * 2,
        compiler_params=_params(("parallel", "parallel")),
    )(da, ff, ff)


def _attn_probs(q_ref, kp_ref, bias_ref, step):
    start = pl.multiple_of(step * QROWS, QROWS)
    kb = kp_ref[pl.ds(start, UNION), :]
    s = _dot(q_ref[...], kb, NT) * (HD_A ** -0.5) + bias_ref[...]
    col = lax.broadcasted_iota(jnp.int32, s.shape, 1)
    s = jnp.where(col + start >= KPAD, s, NEG)
    p = jnp.exp(s - jnp.max(s, axis=-1, keepdims=True))
    return p / jnp.sum(p, axis=-1, keepdims=True), start


def _attn_specs(T):
    per_step = pl.BlockSpec((None, QROWS, HD_A), lambda h, n: (h, n, 0))
    per_head = lambda shape: pl.BlockSpec((None,) + shape, lambda h, n: (h, 0, 0))
    return per_step, per_head((KPAD + T, HD_A)), per_head((QROWS, UNION)), per_head((1, HD_A))


def _attn_fwd(qh, kp, vp, bias, gain):
    H, T, _ = qh.shape

    def body(q_ref, kp_ref, vp_ref, bias_ref, gain_ref, o_ref):
        p, start = _attn_probs(q_ref, kp_ref, bias_ref, pl.program_id(1))
        o = _dot(p.astype(BF16), vp_ref[pl.ds(start, UNION), :])
        rr = lax.rsqrt(jnp.mean(o * o, axis=-1, keepdims=True) + EPS)
        o_ref[...] = (o * rr * gain_ref[...]).astype(o_ref.dtype)

    per_step, keys, table, vec = _attn_specs(T)
    return pl.pallas_call(
        body, name="attn_fwd", grid=(H, T // QROWS), in_specs=[per_step, keys, keys, table, vec],
        out_specs=per_step, out_shape=jax.ShapeDtypeStruct((H, T, HD_A), BF16),
        compiler_params=_params(("parallel", "arbitrary"), 40),
    )(qh, kp, vp, bias, gain)


def _attn_bwd(qh, kp, vp, bias, gain, d_on, rider=None):
    H, T, _ = qh.shape
    scale = HD_A ** -0.5

    def body(q_ref, kp_ref, vp_ref, bias_ref, gain_ref, don_ref, dq_ref, dk_ref, dv_ref, dbias_ref, dgain_ref):
        n = pl.program_id(1)

        @pl.when(n == 0)
        def _():
            dk_ref[...] = jnp.zeros_like(dk_ref)
            dv_ref[...] = jnp.zeros_like(dv_ref)
            dbias_ref[...] = jnp.zeros_like(dbias_ref)
            dgain_ref[...] = jnp.zeros_like(dgain_ref)

        p, start = _attn_probs(q_ref, kp_ref, bias_ref, n)
        keys = pl.ds(start, UNION)
        pb = p.astype(BF16)
        vb = vp_ref[keys, :]
        o = _dot(pb, vb)
        rr = lax.rsqrt(jnp.mean(o * o, axis=-1, keepdims=True) + EPS)
        on = o * rr
        d_on = don_ref[...]
        dgain_ref[...] += _colsum(d_on * on)
        dyo = d_on * gain_ref[...]
        do = rr * (dyo - on * jnp.mean(dyo * on, axis=-1, keepdims=True))
        dob = do.astype(BF16)
        dp = _dot(dob, vb, NT)
        ds = p * (dp - jnp.sum(do * o, axis=-1, keepdims=True))
        dbias_ref[...] += ds
        dsb = ds.astype(BF16)
        dq_ref[...] = (_dot(dsb, kp_ref[keys, :]) * scale).astype(dq_ref.dtype)
        dk_ref[keys, :] += _dot(dsb, q_ref[...], TN) * scale
        dv_ref[keys, :] += _dot(pb, dob, TN)

    per_step, keys, table, vec = _attn_specs(T)
    return _call(
        body, rider, name="attn_bwd", grid=(H, T // QROWS), in_specs=[per_step, keys, keys, table, vec, per_step],
        out_specs=[per_step, keys, keys, table, vec],
        out_shape=[jax.ShapeDtypeStruct((H, T, HD_A), BF16), jax.ShapeDtypeStruct((H, KPAD + T, HD_A), F32),
                   jax.ShapeDtypeStruct((H, KPAD + T, HD_A), F32), jax.ShapeDtypeStruct((H, QROWS, UNION), F32),
                   jax.ShapeDtypeStruct((H, 1, HD_A), F32)],
        scratch_shapes=[], compiler_params=_params(("arbitrary", "arbitrary"), 40),
        operands=(qh, kp, vp, bias, gain, d_on))


N_DIAG = QROWS + UNION - 1


def _bias_table(rel_bias):
    H = rel_bias.shape[0]
    idx = np.clip(UNION - 1 - np.arange(N_DIAG), -MAX_REL, MAX_REL) + MAX_REL
    rolled = rel_bias[:, idx[(np.arange(N_DIAG) + QROWS - 1) % N_DIAG]]
    flat = jnp.broadcast_to(rolled[:, None, :], (H, QROWS, N_DIAG)).reshape(H, QROWS * N_DIAG)
    toep = flat[:, :QROWS * (N_DIAG - 1)].reshape(H, QROWS, N_DIAG - 1)[:, :, :UNION]
    g = np.arange(QROWS)[:, None] // CHUNK
    m = np.arange(UNION)[None, :] // CHUNK
    return jnp.where((m >= g) & (m <= g + N_PAST), toep, NEG)


def _bias_table_grad(dbias):
    H = dbias.shape[0]
    skew = jnp.pad(dbias, ((0, 0), (0, 0), (QROWS - 1, 0))).reshape(H, QROWS * N_DIAG)
    skew = jnp.pad(skew, ((0, 0), (0, QROWS))).reshape(H, QROWS, N_DIAG + 1)
    diag = jnp.sum(skew, axis=1)[:, :N_DIAG]
    n_far = UNION - MAX_REL
    far = jnp.sum(diag[:, :n_far], axis=1, keepdims=True)
    near = diag[:, n_far:][:, ::-1]
    zeros = jnp.zeros((H, MAX_REL - (QROWS - 1)), F32)
    return jnp.concatenate([zeros, near, far], axis=1)


def _tri(n, lower):
    r = lax.broadcasted_iota(jnp.int32, (n, n), 0)
    c = lax.broadcasted_iota(jnp.int32, (n, n), 1)
    return jnp.where((c <= r) if lower else (c >= r), 1.0, 0.0).astype(F32)


def _hgrn_gates(zq_ref, zf_ref, lbl_ref, q_s, k_s, b_s):
    lb = jax.nn.sigmoid(lbl_ref[0:1, :] - lbl_ref[1:2, :])
    zq = zq_ref[...]
    sig = jax.nn.sigmoid(zf_ref[...])
    f = lb + (1.0 - lb) * sig
    sq = jax.nn.sigmoid(zq)
    q_s[...] = zq * sq
    k_s[...] = 1.0 - f
    b_s[...] = _dot(_tri(CHUNK, True), jnp.log(f), precision=HIGHEST)
    return lb, sig, f, sq


def _sub_rows(i):
    return pl.ds(i * SUB, SUB)


def _row_mask(s):
    return lax.broadcasted_iota(jnp.int32, (SUB, HD_B), 0) >= s


def _decay_from(b_sub, b_row, s):
    return jnp.where(_row_mask(s), jnp.exp(jnp.minimum(b_sub - b_row, 0.0)), 0.0)


def _hgrn_fwd(proj, lb_logits, gnorm_g, rider=None):
    T = proj.shape[0]
    nC = T // CHUNK
    W = lb_logits.shape[1]
    H = W // HD_B
    col0 = (proj.shape[1] - 4 * W) // HD_B

    def body(zq_ref, zf_ref, xi_ref, zg_ref, lbl_ref, gn_ref, mix_ref, o_ref, stall_ref, st_ref, q_s, k_s, b_s, acc_s):
        @pl.when(pl.program_id(1) == 0)
        def _():
            st_ref[...] = jnp.zeros_like(st_ref)

        _hgrn_gates(zq_ref, zf_ref, lbl_ref, q_s, k_s, b_s)
        q, k, b = q_s[...], k_s[...], b_s[...]
        st = st_ref[...]
        stall_ref[...] = st
        b_last = b_s[CHUNK - 1:CHUNK, :]
        acc_s[...] = _dot((q * jnp.exp(b)).astype(BF16), st.astype(BF16), NT)
        for i in range(CHUNK // SUB):
            rows = _sub_rows(i)
            q_i, b_i = q_s[rows, :], b_s[rows, :]
            acc = jnp.zeros((SUB, HD_B), F32)
            if i:
                past = pl.ds(0, i * SUB)
                b_ref = b_s[i * SUB - 1:i * SUB, :]
                qs = (q_i * jnp.exp(b_i - b_ref)).astype(BF16)
                ks = (k_s[past, :] * jnp.exp(b_ref - b_s[past, :])).astype(BF16)
                acc += _dot(_dot(qs, ks, NT).astype(BF16), xi_ref[past, :].astype(BF16))
            for s in range(SUB):
                row = pl.ds(i * SUB + s, 1)
                w = q_i * _decay_from(b_i, b_s[row, :], s)
                acc += jnp.sum(w * k_s[row, :], axis=-1, keepdims=True) * xi_ref[row, :]
            acc_s[rows, :] += acc
        o = acc_s[...]
        kd = (k * jnp.exp(b_last - b)).astype(BF16)
        st_ref[...] = st * jnp.exp(b_last) + _dot(xi_ref[...].astype(BF16), kd, TN)
        o_ref[...] = o
        zg = zg_ref[...]
        rr = lax.rsqrt(jnp.mean(o * o, axis=-1, keepdims=True) + EPS)
        mix_ref[...] = (o * rr * gn_ref[...] * (zg * jax.nn.sigmoid(zg))).astype(mix_ref.dtype)

    col = lambda part: pl.BlockSpec((CHUNK, HD_B), lambda h, n: (n, col0 + part * H + h))
    out_blk = pl.BlockSpec((CHUNK, HD_B), lambda h, n: (n, h))
    tile = pltpu.VMEM((CHUNK, HD_B), F32)
    return _call(
        body, rider, name="hgrn_fwd", grid=(H, nC),
        in_specs=[col(0), col(1), col(2), col(3), pl.BlockSpec((2, HD_B), lambda h, n: (0, h)),
                  pl.BlockSpec((1, HD_B), lambda h, n: (0, 0))],
        out_specs=[out_blk, out_blk, pl.BlockSpec((None, None, HD_B, HD_B), lambda h, n: (h, n, 0, 0))],
        out_shape=[jax.ShapeDtypeStruct((T, W), BF16), jax.ShapeDtypeStruct((T, W), F32),
                   jax.ShapeDtypeStruct((H, nC, HD_B, HD_B), F32)],
        scratch_shapes=[pltpu.VMEM((HD_B, HD_B), F32), tile, tile, tile, tile],
        compiler_params=_params(("arbitrary", "arbitrary")),
        operands=(proj, proj, proj, proj, lb_logits, gnorm_g))


def _hgrn_bwd(proj, lb_logits, gnorm_g, o_b, st_all, dmixin, rider=None):
    T = proj.shape[0]
    nC = T // CHUNK
    W = lb_logits.shape[1]
    H = W // HD_B
    col0 = (proj.shape[1] - 4 * W) // HD_B
    dcol0 = (dmixin.shape[1] - W) // HD_B

    def body(zq_ref, zf_ref, xi_ref, zg_ref, lbl_ref, gn_ref, o_ref, st_ref, dout_ref,
             dzq_ref, dzf_ref, dxi_ref, dzg_ref, dl0_ref, dgn_ref, dst_ref, q_s, k_s, b_s, do_s, dq_s, dk_s, di_s):
        h, n = pl.program_id(0), pl.program_id(1)

        @pl.when(n == 0)
        def _():
            dst_ref[...] = jnp.zeros_like(dst_ref)
            dl0_ref[...] = jnp.zeros_like(dl0_ref)

        @pl.when((n == 0) & (h == 0))
        def _():
            dgn_ref[...] = jnp.zeros_like(dgn_ref)

        lb, sig, f, sq = _hgrn_gates(zq_ref, zf_ref, lbl_ref, q_s, k_s, b_s)
        q, k, b = q_s[...], k_s[...], b_s[...]
        zg, o, dout = zg_ref[...], o_ref[...], dout_ref[...]
        sg = jax.nn.sigmoid(zg)
        rr = lax.rsqrt(jnp.mean(o * o, axis=-1, keepdims=True) + EPS)
        on = o * rr
        gn = gn_ref[...]
        dzg_ref[...] = (dout * on * gn * sg * (1.0 + zg * (1.0 - sg))).astype(dzg_ref.dtype)
        d_on = dout * zg * sg
        dgn_ref[...] += _colsum(d_on * on)
        d_on = d_on * gn
        do = rr * (d_on - on * jnp.mean(d_on * on, axis=-1, keepdims=True))
        do_s[...] = do
        dob = do.astype(BF16)
        st, dst = st_ref[...], dst_ref[...]
        b_last = b_s[CHUNK - 1:CHUNK, :]
        eb, e_last, k_dec = jnp.exp(b), jnp.exp(b_last), jnp.exp(b_last - b)
        qt, kd = q * eb, k * k_dec
        dstb = dst.astype(BF16)
        xib = xi_ref[...].astype(BF16)
        d_kd = _dot(xib, dstb)
        dq_s[...] = _dot(dob, st.astype(BF16)) * eb
        dk_s[...] = d_kd * k_dec
        di_s[...] = _dot(kd.astype(BF16), dstb, NT)
        d_b_last = e_last * _colsum(st * dst) + _colsum(d_kd * kd)
        dst_ref[...] = _dot(dob, qt.astype(BF16), TN) + dst * e_last
        for i in range(CHUNK // SUB):
            rows = _sub_rows(i)
            q_i, b_i, do_i = q_s[rows, :], b_s[rows, :], do_s[rows, :]
            dq_i = jnp.zeros((SUB, HD_B), F32)
            if i:
                past = pl.ds(0, i * SUB)
                b_ref = b_s[i * SUB - 1:i * SUB, :]
                e_q, e_k = jnp.exp(b_i - b_ref), jnp.exp(b_ref - b_s[past, :])
                qs, ks = (q_i * e_q).astype(BF16), (k_s[past, :] * e_k).astype(BF16)
                xi_p, do_b = xi_ref[past, :].astype(BF16), do_i.astype(BF16)
                di_s[past, :] += _dot(_dot(ks, qs, NT).astype(BF16), do_b)
                dq_i += _dot(_dot(do_b, xi_p, NT).astype(BF16), ks) * e_q
                dk_s[past, :] += _dot(_dot(xi_p, do_b, NT).astype(BF16), qs) * e_k
            for s in range(SUB):
                row = pl.ds(i * SUB + s, 1)
                k_row, i_row = k_s[row, :], xi_ref[row, :]
                e = _decay_from(b_i, b_s[row, :], s)
                w = q_i * e
                a_col = jnp.sum(w * k_row, axis=-1, keepdims=True)
                da_col = jnp.sum(do_i * i_row, axis=-1, keepdims=True)
                di_s[row, :] += _colsum(a_col * do_i)
                dq_i += da_col * e * k_row
                dk_s[row, :] += _colsum(da_col * w)
            dq_s[rows, :] += dq_i
        dq, dk = dq_s[...], dk_s[...]
        db = q * dq - k * dk
        is_last = lax.broadcasted_iota(jnp.int32, (CHUNK, HD_B), 0) == CHUNK - 1
        db = db + jnp.where(is_last, d_b_last, 0.0)
        df = _dot(_tri(CHUNK, False), db, precision=HIGHEST) / f - dk
        dzf_ref[...] = (df * (1.0 - lb) * sig * (1.0 - sig)).astype(dzf_ref.dtype)
        dl0_ref[...] += _colsum(df * (1.0 - sig)) * (lb * (1.0 - lb))
        zq = zq_ref[...]
        dzq_ref[...] = (dq * sq * (1.0 + zq * (1.0 - sq))).astype(dzq_ref.dtype)
        dxi_ref[...] = di_s[...].astype(dxi_ref.dtype)

    rev = lambda n: nC - 1 - n
    col = lambda part: pl.BlockSpec((CHUNK, HD_B), lambda h, n: (rev(n), col0 + part * H + h))
    blk = pl.BlockSpec((CHUNK, HD_B), lambda h, n: (rev(n), h))
    tile = pltpu.VMEM((CHUNK, HD_B), F32)
    out_big = jax.ShapeDtypeStruct((T, W), BF16)
    return _call(
        body, rider, name="hgrn_bwd", grid=(H, nC),
        in_specs=[col(0), col(1), col(2), col(3), pl.BlockSpec((2, HD_B), lambda h, n: (0, h)),
                  pl.BlockSpec((1, HD_B), lambda h, n: (0, 0)), blk,
                  pl.BlockSpec((None, None, HD_B, HD_B), lambda h, n: (h, rev(n), 0, 0)),
                  pl.BlockSpec((CHUNK, HD_B), lambda h, n: (rev(n), dcol0 + h))],
        out_specs=[blk, blk, blk, blk, pl.BlockSpec((1, HD_B), lambda h, n: (0, h)),
                   pl.BlockSpec((1, HD_B), lambda h, n: (0, 0))],
        out_shape=[out_big, out_big, out_big, out_big, jax.ShapeDtypeStruct((1, W), F32),
                   jax.ShapeDtypeStruct((1, HD_B), F32)],
        scratch_shapes=[pltpu.VMEM((HD_B, HD_B), F32)] + [tile] * 7,
        compiler_params=_params(("arbitrary", "arbitrary")),
        operands=(proj, proj, proj, proj, lb_logits, gnorm_g, o_b, st_all, dmixin))


def _adamw_math(g, w, m, v):
    m = B1 * m + (1.0 - B1) * g
    v = B2 * v + (1.0 - B2) * (g * g)
    m_hat = m / (1.0 - B1 ** STEP)
    v_hat = v / (1.0 - B2 ** STEP)
    return -LR * (m_hat / (jnp.sqrt(v_hat) + ADAM_EPS) + WD * w), m, v


def _adamw(g, w, m, v, name):
    R, C = g.shape
    tr = _row_tile(R, C)

    def body(g_ref, w_ref, m_ref, v_ref, d_ref, mo_ref, vo_ref):
        d_ref[...], mo_ref[...], vo_ref[...] = _adamw_math(g_ref[...], w_ref[...], m_ref[...], v_ref[...])

    blk = pl.BlockSpec((tr, C), lambda i: (i, 0))
    return pl.pallas_call(
        body, name=name, grid=(R // tr,), in_specs=[blk] * 4, out_specs=[blk] * 3,
        out_shape=[jax.ShapeDtypeStruct((R, C), F32)] * 3, compiler_params=_params(("parallel",), 40),
    )(g, w, m, v)


def _sum_pair(g_full, from_sibling, sel, name):
    Q, K, Ns = g_full.shape
    kh = K // 2
    tr = _row_tile(kh, Ns)
    nh = kh // tr

    def body(sel_ref, a_ref, b_ref, o_ref):
        o_ref[...] = (a_ref[...].astype(F32) + b_ref[...].astype(F32)).astype(o_ref.dtype)

    return pl.pallas_call(
        body, name=name,
        grid_spec=pltpu.PrefetchScalarGridSpec(
            num_scalar_prefetch=1, grid=(Q, nh),
            in_specs=[pl.BlockSpec((None, tr, Ns), lambda q, i, sel: (q, sel[1] * nh + i, 0)),
                      pl.BlockSpec((None, tr, Ns), lambda q, i, sel: (q, i, 0))],
            out_specs=pl.BlockSpec((None, tr, Ns), lambda q, i, sel: (q, i, 0))),
        out_shape=jax.ShapeDtypeStruct((Q, kh, Ns), BF16), compiler_params=_params(("parallel", "parallel")),
    )(sel, g_full, from_sibling)


def _sum_chips(pair_sum, from_chips, sel, name):
    Q, kh, Ns = pair_sum.shape
    tr = _row_tile(kh, Ns)
    nh = kh // tr

    def body(sel_ref, a_ref, b0_ref, b1_ref, b2_ref, o_ref):
        up = lambda r: r[...].astype(F32)
        o_ref[...] = ((up(a_ref) + up(b0_ref)) + up(b1_ref)) + up(b2_ref)

    recv = lambda k: pl.BlockSpec((None, tr, Ns), lambda i, sel: (k, i, 0))
    return pl.pallas_call(
        body, name=name,
        grid_spec=pltpu.PrefetchScalarGridSpec(
            num_scalar_prefetch=1, grid=(nh,),
            in_specs=[pl.BlockSpec((None, tr, Ns), lambda i, sel: (sel[0], i, 0)), recv(0), recv(1), recv(2)],
            out_specs=pl.BlockSpec((tr, Ns), lambda i, sel: (sel[1] * nh + i, 0))),
        out_shape=jax.ShapeDtypeStruct((2 * kh, Ns), F32), compiler_params=_params(("parallel",)),
    )(sel, pair_sum, from_chips, from_chips, from_chips)


def _gather_small(v, name):
    R, L = v.shape

    def body(v_ref, out_ref, send_sems, recv_sems):
        x, y, c = _place()
        me = 4 * x + 2 * y + c
        out_ref[me] = v_ref[...]
        peers = [(_flip(x, k >> 2 & 1), _flip(y, k >> 1 & 1), _flip(c, k & 1)) for k in range(1, N_DEV)]

        def copy(k, row, to):
            return pltpu.make_async_remote_copy(src_ref=v_ref, dst_ref=out_ref.at[row], send_sem=send_sems.at[k],
                                                recv_sem=recv_sems.at[k], device_id=to, device_id_type=MESH)

        sends = [copy(k, me, peer) for k, peer in enumerate(peers)]
        for cp in sends:
            cp.start()
        for k, (px, py, pc) in enumerate(peers):
            copy(k, 4 * px + 2 * py + pc, (x, y, c)).wait_recv()
        for cp in sends:
            cp.wait_send()

    vmem = pl.BlockSpec(memory_space=pltpu.VMEM)
    return pl.pallas_call(
        body, name=name, in_specs=[vmem], out_specs=vmem, out_shape=jax.ShapeDtypeStruct((N_DEV, R, L), F32),
        scratch_shapes=[pltpu.SemaphoreType.DMA((N_DEV - 1,)), pltpu.SemaphoreType.DMA((N_DEV - 1,))],
    )(v)


def _send_pair(g_full, name):
    Q, K, Ns = g_full.shape
    kh = K // 2

    def body(g_ref, got_ref, send_sem, recv_sem):
        x, y, c = _place()
        cp = pltpu.make_async_remote_copy(src_ref=g_ref.at[:, pl.ds((1 - c) * kh, kh), :], dst_ref=got_ref,
                                          send_sem=send_sem, recv_sem=recv_sem, device_id=(x, y, 1 - c),
                                          device_id_type=MESH)
        cp.start()
        cp.wait()

    return pl.pallas_call(
        body, name=name, in_specs=[ANY], out_specs=ANY, out_shape=jax.ShapeDtypeStruct((Q, kh, Ns), g_full.dtype),
        scratch_shapes=[pltpu.SemaphoreType.DMA, pltpu.SemaphoreType.DMA],
    )(g_full)


def _share_halves(block, name):
    K, Ns = block.shape
    kh = K // 2

    def body(_, out_ref, send_sem, recv_sem):
        x, y, c = _place()
        mine, theirs = out_ref.at[pl.ds(c * kh, kh), :], out_ref.at[pl.ds((1 - c) * kh, kh), :]
        cp = pltpu.make_async_remote_copy(src_ref=mine, dst_ref=mine, send_sem=send_sem, recv_sem=recv_sem,
                                          device_id=(x, y, 1 - c), device_id_type=MESH)
        cp.start()
        pltpu.make_async_remote_copy(src_ref=theirs, dst_ref=theirs, send_sem=send_sem, recv_sem=recv_sem,
                                     device_id=(x, y, 1 - c), device_id_type=MESH).wait_recv()
        cp.wait_send()

    return pl.pallas_call(
        body, name=name, in_specs=[ANY], out_specs=ANY, out_shape=jax.ShapeDtypeStruct((K, Ns), block.dtype),
        input_output_aliases={0: 0},
        scratch_shapes=[pltpu.SemaphoreType.DMA, pltpu.SemaphoreType.DMA],
    )(block)


def _silu(v):
    return v * jax.nn.sigmoid(v)


def _ada_fwd(c_all, w_ada, tn=512):
    M, D = c_all.shape
    Ns = w_ada.shape[1]

    def body(c_ref, w_ref, o_ref):
        o_ref[...] = _dot(_silu(c_ref[...]).astype(BF16), w_ref[...].astype(BF16))

    return pl.pallas_call(
        body, name="ada_fwd", grid=(Ns // tn,),
        in_specs=[pl.BlockSpec((M, D), lambda j: (0, 0)), pl.BlockSpec((D, tn), lambda j: (0, j))],
        out_specs=pl.BlockSpec((M, tn), lambda j: (0, j)), out_shape=jax.ShapeDtypeStruct((M, Ns), F32),
        compiler_params=_params(("parallel",)),
    )(c_all, w_ada)


def _ada_bwd(c_all, dmod, w, m, v, tk=256, tn=1536):
    M, D = c_all.shape
    Ns = dmod.shape[1]

    def body(c_ref, d_ref, w_ref, m_ref, v_ref, g_ref, dl_ref, mo_ref, vo_ref):
        g = _dot(_silu(c_ref[...]).astype(BF16), d_ref[...].astype(BF16), TN)
        g_ref[...] = g
        dl_ref[...], mo_ref[...], vo_ref[...] = _adamw_math(g, w_ref[...], m_ref[...], v_ref[...])

    blk = pl.BlockSpec((tk, tn), lambda i, j: (i, j))
    return pl.pallas_call(
        body, name="ada_bwd", grid=(D // tk, Ns // tn),
        in_specs=[pl.BlockSpec((M, tk), lambda i, j: (0, i)), pl.BlockSpec((M, tn), lambda i, j: (0, j)), blk, blk, blk],
        out_specs=[blk] * 4, out_shape=[jax.ShapeDtypeStruct((D, Ns), F32)] * 4,
        compiler_params=_params(("parallel", "parallel"), 40),
    )(c_all, dmod, w, m, v)


def _small_update(g_all, w, m, v):
    R, L = w.shape

    def body(g_ref, w_ref, m_ref, v_ref, go_ref, d_ref, mo_ref, vo_ref):
        g = g_ref[0]
        for d in range(1, N_DEV):
            g = g + g_ref[d]
        go_ref[...] = g
        d_ref[...], mo_ref[...], vo_ref[...] = _adamw_math(g, w_ref[...], m_ref[...], v_ref[...])

    return pl.pallas_call(body, name="small_update", out_shape=[jax.ShapeDtypeStruct((R, L), F32)] * 4)(g_all, w, m, v)


def _pack(parts, rows):
    flat = jnp.concatenate([p.reshape(-1) for p in parts])
    return jnp.pad(flat, (0, rows * 128 - flat.shape[0])).reshape(rows, 128)


def _unpack(packed, shapes):
    flat, out, at = packed.reshape(-1), [], 0
    for shp in shapes:
        size = 1
        for d in shp:
            size *= d
        out.append(flat[at:at + size].reshape(shp))
        at += size
    return out


def _heads(a, pad):
    T = a.shape[0]
    h = a.reshape(T, -1, HD_A).transpose(1, 0, 2).astype(BF16)
    return jnp.pad(h, ((0, 0), (pad, 0), (0, 0))) if pad else h


def _unheads(a):
    return a.transpose(1, 0, 2).reshape(a.shape[1], -1)


def _layer(x, tgt, mod, wts, rel_bias, attn_norm_g, lb_logits, gnorm_g, ln1_g, ln1_b, ln2_g, ln2_b, place=None):
    T, D = x.shape
    aw = attn_norm_g.shape[1]
    shift1, scale1, gate1, shift2, scale2, gate2 = [mod[i:i + 1] for i in range(6)]

    def gather(n):
        return None if place is None else _gather_rider(wts[n])

    def gathered(n, rode):
        return wts[n] if place is None else lax.dynamic_update_index_in_dim(rode[0], wts[n], place[0], 0)

    def pair_sum(n, g):
        g = g.reshape(N_CHIPS, -1, g.shape[2])
        return g if place is None else _sum_pair(g, _send_pair(g, n + "_send_pair"), place[1], n + "_sum_pair")

    def to_chips(p):
        return None if place is None else _chips_rider(p)

    def reduced(n, p, rode):
        return p if place is None else _share_halves(_sum_chips(p, rode[0], place[1], n + "_sum_chips"), n + "_share")

    def carrying(mm, *args, rider, **kw):
        return mm(*args, rider=rider, **kw) if rider is not None else (mm(*args, **kw), None)

    w_in = gathered("w_in", None if place is None else _alone(gather("w_in"), "gather_w_in"))
    h1 = _pre_mixer(x, scale1, shift1)
    proj, rode = carrying(_mm_nn, h1, w_in, tm=1024, tn=256, tk=D, name="proj", rider=gather("w_o"))
    w_o3 = gathered("w_o", rode).reshape(1, D, D)
    qh = _heads(proj[:, :aw], 0)
    kp = _heads(proj[:, aw:2 * aw], KPAD)
    vp = _heads(proj[:, 2 * aw:3 * aw], KPAD)
    bias = _bias_table(rel_bias)
    gain = attn_norm_g.reshape(-1, 1, HD_A)
    mix_a = _unheads(_attn_fwd(qh, kp, vp, bias, gain))
    (mix_b, o_b, st_all), rode = _hgrn_fwd(proj, lb_logits, gnorm_g, rider=gather("w_ffn_in"))
    w_ffn_in = gathered("w_ffn_in", rode)
    mixin = jnp.concatenate([mix_a, mix_b], axis=1)
    mix = _mm_nn(mixin, w_o3, tm=1024, tn=512, tk=D, name="mix_out")
    x1, h2 = _post_mixer(mix, x, gate1, ln1_g, ln1_b, scale2, shift2)
    ff, rode = carrying(_mm_nn, h2, w_ffn_in, tm=1024, tn=256, tk=D, name="ffn_in", rider=gather("w_ffn_out"))
    w_out3 = gathered("w_ffn_out", rode)
    w_out3 = w_out3.reshape(1, -1, w_out3.shape[2])
    d_ff = w_out3.shape[1]
    act = _swiglu_fwd(ff)
    f = _mm_nn(act, w_out3, tm=1024, tn=1024, tk=d_ff // 4, name="ffn_out")
    du2, df, acc2 = _loss_head(f, x1, tgt, gate2, ln2_g, ln2_b)
    loss = (0.5 / D) * jnp.sum(acc2[3])
    da = _mm_nt(df, w_out3, tm=1024, to=512, tn=D, name="d_act")
    p_out = pair_sum("w_ffn_out", _mm_tn(act, df, q=1, tk=512, tn=1024, tt=T, name="g_ffn_out"))
    dff = jnp.concatenate(_swiglu_bwd(da, ff), axis=1)
    dh2, rode = carrying(_mm_nt, dff, w_ffn_in, tm=1024, to=1024, tn=w_ffn_in.shape[2], name="d_h2",
                         rider=to_chips(p_out))
    g_ffn_out = reduced("w_ffn_out", p_out, rode)
    p_fin = pair_sum("w_ffn_in", _mm_tn(h2, dff, q=N_CHIPS, tk=512, tn=w_ffn_in.shape[2] // 2, tt=T, name="g_ffn_in"))
    du1, dmix, acc1 = _mid_bwd(dh2, du2, x1, mix, x, gate1, ln1_g, scale2)
    dmixin = _mm_nt(dmix, w_o3, tm=1024, to=512, tn=D, name="d_mixin")
    p_o = pair_sum("w_o", _mm_tn(mixin, dmix, q=1, tk=512, tn=1024, tt=T, name="g_o"))
    d_on = dmixin[:, :aw].reshape(T, -1, HD_A).transpose(1, 0, 2)
    (dq, dk, dv, dbias, dgain), rode = _attn_bwd(qh, kp, vp, bias, gain, d_on, rider=to_chips(p_o))
    g_o = reduced("w_o", p_o, rode)
    (dzq, dzf, dxi, dzg, dl0, dgn), rode = _hgrn_bwd(proj, lb_logits, gnorm_g, o_b, st_all, dmixin, rider=to_chips(p_fin))
    g_ffn_in = reduced("w_ffn_in", p_fin, rode)
    dproj = jnp.concatenate([_unheads(dq), _unheads(dk[:, KPAD:]).astype(BF16), _unheads(dv[:, KPAD:]).astype(BF16),
                             dzq, dzf, dxi, dzg], axis=1)
    p_in = pair_sum("w_in", _mm_tn(h1, dproj, q=N_CHIPS, tk=512, tn=w_in.shape[2] // 2, tt=T, name="g_in"))
    dh1, rode = carrying(_mm_nt, dproj, w_in, tm=1024, to=1024, tn=w_in.shape[2], name="d_h1", rider=to_chips(p_in))
    g_in = reduced("w_in", p_in, rode)
    grad_x, acc0 = _first_bwd(dh1, du1, x, scale1)
    dmod = jnp.concatenate([acc0[1:2], acc0[0:1], acc1[4:5], acc1[1:2], acc1[0:1], acc2[2:3]], axis=0)
    small = dict(rel_bias=_bias_table_grad(dbias), attn_norm_g=dgain.reshape(1, -1),
                 lb_logits=jnp.concatenate([dl0, -dl0], axis=0), gnorm_g=dgn,
                 ln1_g=acc1[2:3], ln1_b=acc1[3:4], ln2_g=acc2[0:1], ln2_b=acc2[1:2])
    return loss, grad_x, dict(w_in=g_in, w_o=g_o, w_ffn_in=g_ffn_in, w_ffn_out=g_ffn_out), dmod, small


SMALL = ("rel_bias", "attn_norm_g", "lb_logits", "gnorm_g", "ln1_g", "ln1_b", "ln2_g", "ln2_b")
SMALL_ROWS = 256


def kernel(x, c, w_ada, b_ada, w_in, rel_bias, attn_norm_g, lb_logits, gnorm_g, w_o, ln1_g, ln1_b, w_ffn_in, w_ffn_out, ln2_g, ln2_b, loss_target, m_w_ada, m_b_ada, m_w_in, m_rel_bias, m_attn_norm_g, m_lb_logits, m_gnorm_g, m_w_o, m_ln1_g, m_ln1_b, m_w_ffn_in, m_w_ffn_out, m_ln2_g, m_ln2_b, v_w_ada, v_b_ada, v_w_in, v_rel_bias, v_attn_norm_g, v_lb_logits, v_gnorm_g, v_w_o, v_ln1_g, v_ln1_b, v_w_ffn_in, v_w_ffn_out, v_ln2_g, v_ln2_b):
    mx, my, mc = _place()
    me = 4 * mx + 2 * my + mc
    chip = 2 * mx + my
    sel = jnp.stack([chip, mc]).astype(jnp.int32)
    D = x.shape[2]
    ns_ada = w_ada.shape[2]

    big = dict(w_in=(w_in, m_w_in, v_w_in), w_o=(w_o, m_w_o, v_w_o), w_ffn_in=(w_ffn_in, m_w_ffn_in, v_w_ffn_in),
               w_ffn_out=(w_ffn_out, m_w_ffn_out, v_w_ffn_out))
    shards = {n: t[0][0].astype(BF16) for n, t in big.items()}

    c_all = _gather_small(c.reshape(D // 128, 128), "gather_c").reshape(N_DEV, D)
    c_all = jnp.pad(c_all, ((0, 16 - N_DEV), (0, 0)))
    mod_cols = _ada_fwd(c_all, w_ada[0])[:N_DEV]
    mod_all = _gather_small(mod_cols.reshape(-1, 128), "gather_mod").reshape(N_DEV, N_DEV, ns_ada)
    mod = lax.dynamic_index_in_dim(mod_all[::2], me, axis=1, keepdims=False)
    mod = (mod.reshape(1, -1) + b_ada).reshape(6, D)

    loss, grad_x, g_big, dmod, g_small = _layer(
        x[0], loss_target[0], mod, shards, rel_bias[0], attn_norm_g, lb_logits, gnorm_g, ln1_g, ln1_b, ln2_g, ln2_b,
        place=(chip, sel))
    loss = lax.psum(loss, ("x", "y", "c"))

    grads, deltas, new_m, new_v = {}, {}, {}, {}
    for n, (w, m, v) in big.items():
        g = g_big[n]
        d, mo, vo = _adamw(g, w[0], m[0], v[0], "adamw_" + n)
        grads[n], deltas[n], new_m[n], new_v[n] = g[None], d[None], mo[None], vo[None]

    small_in = dict(rel_bias=(rel_bias, m_rel_bias, v_rel_bias), attn_norm_g=(attn_norm_g, m_attn_norm_g, v_attn_norm_g),
                    lb_logits=(lb_logits, m_lb_logits, v_lb_logits), gnorm_g=(gnorm_g, m_gnorm_g, v_gnorm_g),
                    ln1_g=(ln1_g, m_ln1_g, v_ln1_g), ln1_b=(ln1_b, m_ln1_b, v_ln1_b), ln2_g=(ln2_g, m_ln2_g, v_ln2_g),
                    ln2_b=(ln2_b, m_ln2_b, v_ln2_b))
    g_all = _gather_small(_pack([dmod] + [g_small[n] for n in SMALL], SMALL_ROWS), "gather_small")
    packed = [_pack([t] + [small_in[n][i] for n in SMALL], SMALL_ROWS)
              for i, t in enumerate((b_ada, m_b_ada, v_b_ada))]
    shapes = [b_ada.shape] + [small_in[n][0].shape for n in SMALL]
    outs = [_unpack(o, shapes) for o in _small_update(g_all, *packed)]
    for i, n in enumerate(("b_ada",) + SMALL):
        grads[n], deltas[n], new_m[n], new_v[n] = outs[0][i], outs[1][i], outs[2][i], outs[3][i]

    dmod_all = g_all[:, :6 * D // 128].reshape(N_DEV, 6 * D)
    dmod_cols = lax.dynamic_slice_in_dim(dmod_all, chip * ns_ada, ns_ada, axis=1)
    dmod_cols = jnp.pad(dmod_cols, ((0, 16 - N_DEV), (0, 0)))
    g, d, mo, vo = _ada_bwd(c_all, dmod_cols, w_ada[0], m_w_ada[0], v_w_ada[0])
    grads["w_ada"], deltas["w_ada"], new_m["w_ada"], new_v["w_ada"] = g[None], d[None], mo[None], vo[None]

    order = ("w_ada", "b_ada", "w_in", "rel_bias", "attn_norm_g", "lb_logits", "gnorm_g", "w_o", "ln1_g", "ln1_b",
             "w_ffn_in", "w_ffn_out", "ln2_g", "ln2_b")
    return (loss, grad_x[None], *[grads[n] for n in order], *[deltas[n] for n in order],
            *[new_m[n] for n in order], *[new_v[n] for n in order])
```

```python
import numpy as np
import jax
import jax.numpy as jnp
from jax import lax
from jax.experimental import pallas as pl
from jax.experimental.pallas import tpu as pltpu

F32 = jnp.float32
BF16 = jnp.bfloat16
MESH = pl.DeviceIdType.MESH
HIGHEST = lax.Precision.HIGHEST

CHUNK = 64
N_PAST = 8
QG = 4
QROWS = QG * CHUNK
KPAD = N_PAST * CHUNK
UNION = (QG + N_PAST) * CHUNK
HD_A = 64
HD_B = 128
SUB = 16
HGRN_HEADS = 4
MAX_REL = 256
EPS = 1e-5
ALPHA = 2.0 ** 0.25
LR, B1, B2, ADAM_EPS, WD, STEP = 1e-3, 0.9, 0.999, 1e-8, 0.01, 10
N_CHIPS = 4
N_DEV = 8
NEG = -1e30
TILE_BYTES = 3 << 19

NN = ((1,), (0,))
NT = ((1,), (1,))
TN = ((0,), (0,))


def _dot(a, b, dims=NN, precision=None):
    return lax.dot_general(a, b, (dims, ((), ())), preferred_element_type=F32, precision=precision)


def _params(sem=None, vmem_mb=None, **kw):
    return pltpu.CompilerParams(dimension_semantics=sem,
                                vmem_limit_bytes=None if vmem_mb is None else vmem_mb << 20, **kw)


def _row_tile(rows, cols):
    for cand in (512, 256, 128, 64, 32, 16, 8):
        if rows % cand == 0 and cand * cols * 4 <= TILE_BYTES:
            return cand
    raise ValueError((rows, cols))


def _place():
    return lax.axis_index("x"), lax.axis_index("y"), lax.axis_index("c")


def _flip(v, bit):
    return 1 - v if bit else v


ANY = pl.BlockSpec(memory_space=pl.ANY)
CHIP_FLIPS = ((1, 0), (0, 1), (1, 1))


class _Rider:
    def __init__(self, operands, out_shape, n_sems, start, finish, aliases=None):
        self.operands, self.out_shape, self.n_sems, self.start, self.finish = operands, out_shape, n_sems, start, finish
        self.aliases = aliases or {}


def _call(body, rider, *, name, grid, in_specs, out_specs, out_shape, scratch_shapes, compiler_params, operands):
    if rider is None:
        outs = pl.pallas_call(body, name=name, grid=grid, in_specs=in_specs, out_specs=out_specs, out_shape=out_shape,
                              scratch_shapes=scratch_shapes, compiler_params=compiler_params)(*operands)
        return list(outs), []
    n_in, n_out, n_sc = len(in_specs), len(out_specs), len(scratch_shapes)
    r_in, r_out = len(rider.operands), len(rider.out_shape)

    def carried(*refs):
        refs = list(refs)
        cuts = [n_in, r_in, n_out, r_out, n_sc]
        ins, r_ins, outs, r_outs, scratch = [[refs.pop(0) for _ in range(n)] for n in cuts]
        first, last = None, None
        for axis, size in enumerate(grid):
            i = pl.program_id(axis)
            first = (i == 0) if first is None else first & (i == 0)
            last = (i == size - 1) if last is None else last & (i == size - 1)

        @pl.when(first)
        def _():
            rider.start(r_ins, r_outs, *refs)

        body(*ins, *outs, *scratch)

        @pl.when(last)
        def _():
            rider.finish(r_ins, r_outs, *refs)

    sems = [pltpu.SemaphoreType.DMA((rider.n_sems,)), pltpu.SemaphoreType.DMA((rider.n_sems,))]
    outs = pl.pallas_call(carried, name=name, grid=grid, in_specs=list(in_specs) + [ANY] * r_in,
                          out_specs=list(out_specs) + [ANY] * r_out, out_shape=list(out_shape) + rider.out_shape,
                          scratch_shapes=list(scratch_shapes) + sems, compiler_params=compiler_params,
                          input_output_aliases={n_in + i: n_out + o for i, o in rider.aliases.items()},
                          )(*operands, *rider.operands)
    return list(outs[:n_out]), list(outs[n_out:])


def _alone(rider, name):
    def body(*refs):
        ins, outs, sems = refs[:len(rider.operands)], refs[len(rider.operands):-2], refs[-2:]
        rider.start(ins, outs, *sems)
        rider.finish(ins, outs, *sems)

    return pl.pallas_call(
        body, name=name, in_specs=[ANY] * len(rider.operands), out_specs=[ANY] * len(rider.out_shape),
        out_shape=rider.out_shape,
        scratch_shapes=[pltpu.SemaphoreType.DMA((rider.n_sems,)), pltpu.SemaphoreType.DMA((rider.n_sems,))],
    )(*rider.operands)


def _gather_rider(shard):
    K, Ns = shard.shape
    kh = K // 2

    def copies(w_ref, out_ref, send_sems, recv_sems):
        x, y, c = _place()
        chips = [(_flip(x, fx), _flip(y, fy)) for fx, fy in CHIP_FLIPS]

        def half(chip, which):
            return out_ref.at[2 * chip[0] + chip[1], pl.ds(which * kh, kh), :]

        def copy(k, dst, to, src=None):
            return pltpu.make_async_remote_copy(src_ref=dst if src is None else src, dst_ref=dst,
                                                send_sem=send_sems.at[k], recv_sem=recv_sems.at[k],
                                                device_id=to, device_id_type=MESH)

        def first():
            return [copy(j, half((x, y), c), (*chip, c), src=w_ref.at[pl.ds(c * kh, kh), :])
                    for j, chip in enumerate(chips)]

        def onward():
            return [copy(3 + j, half(chip, c), (x, y, 1 - c)) for j, chip in enumerate(chips)]

        def arriving(base, which):
            return [copy(base + j, half(chip, which), (x, y, c)) for j, chip in enumerate(chips)]

        return first, onward, arriving

    def start(ins, outs, send_sems, recv_sems):
        for cp in copies(ins[0], outs[0], send_sems, recv_sems)[0]():
            cp.start()

    def finish(ins, outs, send_sems, recv_sems):
        x, y, c = _place()
        first, onward, arriving = copies(ins[0], outs[0], send_sems, recv_sems)
        passed = onward()
        for arrived, cp in zip(arriving(0, c), passed):
            arrived.wait_recv()
            cp.start()
        for arrived in arriving(3, 1 - c):
            arrived.wait_recv()
        for cp in first() + passed:
            cp.wait_send()

    return _Rider([shard], [jax.ShapeDtypeStruct((N_CHIPS, K, Ns), shard.dtype)], 6, start, finish)


def _chips_rider(pair_sum, rows=None, into=None):
    Q, kh, Ns = pair_sum.shape
    first_row, n_rows = rows or (0, kh)

    def copies(p_ref, got_ref, send_sems, recv_sems):
        x, y, c = _place()
        part = pl.ds(first_row, n_rows)
        out = []
        for j, (fx, fy) in enumerate(CHIP_FLIPS):
            px, py = _flip(x, fx), _flip(y, fy)
            out.append(pltpu.make_async_remote_copy(
                src_ref=p_ref.at[2 * px + py, part, :], dst_ref=got_ref.at[j, part, :], send_sem=send_sems.at[j],
                recv_sem=recv_sems.at[j], device_id=(px, py, c), device_id_type=MESH))
        return out

    def start(ins, outs, send_sems, recv_sems):
        for cp in copies(ins[0], outs[0], send_sems, recv_sems):
            cp.start()

    def finish(ins, outs, send_sems, recv_sems):
        sends = copies(ins[0], outs[0], send_sems, recv_sems)
        for cp in sends:
            cp.wait_recv()
        for cp in sends:
            cp.wait_send()

    got = jax.ShapeDtypeStruct((Q - 1, kh, Ns), pair_sum.dtype)
    if into is None:
        return _Rider([pair_sum], [got], 3, start, finish)
    return _Rider([pair_sum, into], [got], 3, start, finish, aliases={1: 0})


def _mm(a, b, *, grid, a_spec, b_spec, o_spec, o_shape, o_dtype, dims, acc_shape, name, rider=None, vmem_mb=48):
    nk = grid[2]

    def body(a_ref, b_ref, o_ref, *scratch):
        part = _dot(a_ref[...], b_ref[...], dims)
        if nk == 1:
            o_ref[...] = part.astype(o_ref.dtype)
            return
        acc_ref, = scratch
        k = pl.program_id(2)

        @pl.when(k == 0)
        def _():
            acc_ref[...] = part

        @pl.when(k > 0)
        def _():
            acc_ref[...] += part

        @pl.when(k == nk - 1)
        def _():
            o_ref[...] = acc_ref[...].astype(o_ref.dtype)

    (out,), rode = _call(
        body, rider, name=name, grid=grid, in_specs=[a_spec, b_spec], out_specs=[o_spec],
        out_shape=[jax.ShapeDtypeStruct(o_shape, o_dtype)],
        scratch_shapes=[] if nk == 1 else [pltpu.VMEM(acc_shape, F32)],
        compiler_params=_params(("parallel", "parallel", "arbitrary") if rider is None else ("arbitrary",) * 3, vmem_mb),
        operands=(a, b))
    return out if rider is None else (out, rode)


def _mm_nn(a, w, *, tm, tn, tk, name, rider=None, cols=None, o_dtype=F32):
    T, K = a.shape
    Q, _, Ns = w.shape
    nbs = Ns // tn
    tm = min(tm, T)
    j0, j1 = cols or (0, Q * nbs)
    return _mm(a, w, grid=(T // tm, j1 - j0, K // tk),
               a_spec=pl.BlockSpec((tm, tk), lambda i, j, k: (i, k)),
               b_spec=pl.BlockSpec((None, tk, tn), lambda i, j, k: ((j + j0) // nbs, k, (j + j0) % nbs)),
               o_spec=pl.BlockSpec((tm, tn), lambda i, j, k: (i, j)),
               o_shape=(T, (j1 - j0) * tn), o_dtype=o_dtype, dims=NN, acc_shape=(tm, tn), name=name, rider=rider)


def _mm_nt(g, w, *, tm, to, tn, name, rider=None):
    T = g.shape[0]
    Q, K, Ns = w.shape
    nbs = Ns // tn
    tm = min(tm, T)
    return _mm(g, w, grid=(T // tm, K // to, Q * nbs),
               a_spec=pl.BlockSpec((tm, tn), lambda i, j, n: (i, n)),
               b_spec=pl.BlockSpec((None, to, tn), lambda i, j, n: (n // nbs, j, n % nbs)),
               o_spec=pl.BlockSpec((tm, to), lambda i, j, n: (i, j)),
               o_shape=(T, K), o_dtype=F32, dims=NT, acc_shape=(tm, to), name=name, rider=rider)


def _mm_tn(a, g, *, q, tk, tn, tt, name):
    T, K = a.shape
    Ns = g.shape[1] // q
    nbs = Ns // tn
    return _mm(a, g, grid=(K // tk, q * nbs, T // tt),
               a_spec=pl.BlockSpec((tt, tk), lambda i, j, t: (t, i)),
               b_spec=pl.BlockSpec((tt, tn), lambda i, j, t: (t, j)),
               o_spec=pl.BlockSpec((None, tk, tn), lambda i, j, t: (j // nbs, i, j % nbs)),
               o_shape=(q, K, Ns), o_dtype=BF16, dims=TN, acc_shape=(tk, tn), name=name)


def _ln(u):
    mu = jnp.mean(u, axis=-1, keepdims=True)
    d = u - mu
    r = lax.rsqrt(jnp.mean(d * d, axis=-1, keepdims=True) + EPS)
    return d * r, r


def _ln_bwd(dy, un, r):
    return r * (dy - jnp.mean(dy, axis=-1, keepdims=True) - un * jnp.mean(dy * un, axis=-1, keepdims=True))


def _colsum(v):
    return jnp.sum(v, axis=0, keepdims=True)


def _rowwise(name, fn, bigs, vecs, out_dtypes, n_acc, tm=128):
    T, D = bigs[0].shape
    nb, nv, no = len(bigs), len(vecs), len(out_dtypes)

    def body(*refs):
        outs, accs = fn([r[...] for r in refs[:nb]], [r[...] for r in refs[nb:nb + nv]])
        for r, o in zip(refs[nb + nv:nb + nv + no], outs):
            r[...] = o.astype(r.dtype)
        if n_acc:
            acc_ref = refs[nb + nv + no]

            @pl.when(pl.program_id(0) == 0)
            def _():
                acc_ref[...] = jnp.zeros_like(acc_ref)

            for row, a in enumerate(accs):
                acc_ref[row:row + 1, :] += a

    big_spec = pl.BlockSpec((tm, D), lambda i: (i, 0))
    vec_spec = pl.BlockSpec((1, D), lambda i: (0, 0))
    out_shape = [jax.ShapeDtypeStruct((T, D), dt) for dt in out_dtypes]
    out_specs = [big_spec] * no
    if n_acc:
        out_shape.append(jax.ShapeDtypeStruct((8, D), F32))
        out_specs.append(pl.BlockSpec((8, D), lambda i: (0, 0)))
    return pl.pallas_call(
        body, name=name, grid=(T // tm,), in_specs=[big_spec] * nb + [vec_spec] * nv,
        out_specs=out_specs, out_shape=out_shape,
        compiler_params=_params(("arbitrary",), 48),
    )(*bigs, *vecs)


def _pre_mixer(x, scale1, shift1):
    def fn(b, v):
        xn, _ = _ln(b[0])
        return [xn * (1.0 + v[0]) + v[1]], []
    return _rowwise("pre_mixer", fn, [x], [scale1, shift1], [BF16], 0)[0]


def _post_mixer(mix, x, gate1, g1, b1, scale2, shift2):
    def fn(b, v):
        un1, _ = _ln(ALPHA * b[1] + v[0] * b[0])
        x1 = un1 * v[1] + v[2]
        xn1, _ = _ln(x1)
        return [x1, xn1 * (1.0 + v[3]) + v[4]], []
    return _rowwise("post_mixer", fn, [mix, x], [gate1, g1, b1, scale2, shift2], [F32, BF16], 0)


def _loss_head(f, x1, tgt, gate2, g2, b2):
    def fn(b, v):
        ff, xx, tt = b
        d_model = ff.shape[-1]
        un2, r2 = _ln(ALPHA * xx + v[0] * ff)
        err = un2 * v[1] + v[2] - tt
        dy = err * (1.0 / d_model)
        du2 = _ln_bwd(dy * v[1], un2, r2)
        return [du2, du2 * v[0]], [_colsum(dy * un2), _colsum(dy), _colsum(du2 * ff), _colsum(err * err)]
    return _rowwise("loss_head", fn, [f, x1, tgt], [gate2, g2, b2], [F32, BF16], 4)


def _mid_bwd(dh2, du2, x1, mix, x, gate1, g1, scale2):
    def fn(b, v):
        dh, du, xx1, mm, xx = b
        xn1, r1n = _ln(xx1)
        dx1 = ALPHA * du + _ln_bwd(dh * (1.0 + v[2]), xn1, r1n)
        un1, r1 = _ln(ALPHA * xx + v[0] * mm)
        du1 = _ln_bwd(dx1 * v[1], un1, r1)
        return [du1, du1 * v[0]], [_colsum(dh * xn1), _colsum(dh), _colsum(dx1 * un1), _colsum(dx1),
                                   _colsum(du1 * mm)]
    return _rowwise("mid_bwd", fn, [dh2, du2, x1, mix, x], [gate1, g1, scale2], [F32, BF16], 5)


def _first_bwd(dh1, du1, x, scale1):
    def fn(b, v):
        dh, du, xx = b
        xn, r0 = _ln(xx)
        return [ALPHA * du + _ln_bwd(dh * (1.0 + v[0]), xn, r0)], [_colsum(dh * xn), _colsum(dh)]
    return _rowwise("first_bwd", fn, [dh1, du1, x], [scale1], [F32], 2)


def _swiglu_fwd(ff, tm=256, tf=512):
    T, F2 = ff.shape
    F = F2 // 2
    nf = F // tf

    def body(g_ref, u_ref, a_ref):
        g = g_ref[...]
        a_ref[...] = (g * jax.nn.sigmoid(g) * u_ref[...]).astype(a_ref.dtype)

    return pl.pallas_call(
        body, name="swiglu_fwd", grid=(T // tm, nf),
        in_specs=[pl.BlockSpec((tm, tf), lambda i, j: (i, j)), pl.BlockSpec((tm, tf), lambda i, j: (i, j + nf))],
        out_specs=pl.BlockSpec((tm, tf), lambda i, j: (i, j)),
        out_shape=jax.ShapeDtypeStruct((T, F), BF16),
        compiler_params=_params(("parallel", "parallel")),
    )(ff, ff)


def _swiglu_bwd(da, ff, tm=256, tf=512):
    T, F = da.shape
    nf = F // tf

    def body(da_ref, g_ref, u_ref, dg_ref, du_ref):
        g = g_ref[...]
        d = da_ref[...]
        s = jax.nn.sigmoid(g)
        du_ref[...] = (d * g * s).astype(du_ref.dtype)
        dg_ref[...] = (d * u_ref[...] * s * (1.0 + g * (1.0 - s))).astype(dg_ref.dtype)

    blk = pl.BlockSpec((tm, tf), lambda i, j: (i, j))
    return pl.pallas_call(
        body, name="swiglu_bwd", grid=(T // tm, nf),
        in_specs=[blk, blk, pl.BlockSpec((tm, tf), lambda i, j: (i, j + nf))],
        out_specs=[blk, blk],
        out_shape=[jax.ShapeDtypeStruct((T, F), BF16)] * 2,
        compiler_params=_params(("parallel", "parallel")),
    )(da, ff, ff)


PAIR = 2


def _attn_probs(q_ref, k_ref, bias_ref, e, step):
    start = pl.multiple_of(step * QROWS, QROWS)
    lanes = pl.ds(e * HD_A, HD_A)
    s = _dot(q_ref[:, lanes], k_ref[pl.ds(start, UNION), lanes], NT) * (HD_A ** -0.5) + bias_ref[e]
    col = lax.broadcasted_iota(jnp.int32, s.shape, 1)
    s = jnp.where(col + start >= KPAD, s, NEG)
    p = jnp.exp(s - jnp.max(s, axis=-1, keepdims=True))
    return p / jnp.sum(p, axis=-1, keepdims=True), start


def _attn_specs(T, n_pairs):
    wide = PAIR * HD_A
    per_step = pl.BlockSpec((QROWS, wide), lambda hp, n: (n, hp))
    keys = pl.BlockSpec((KPAD + T, wide), lambda hp, n: (0, hp))
    values = pl.BlockSpec((KPAD + T, wide), lambda hp, n: (0, n_pairs + hp))
    table = pl.BlockSpec((PAIR, QROWS, UNION), lambda hp, n: (hp, 0, 0))
    vec = pl.BlockSpec((1, wide), lambda hp, n: (0, hp))
    return per_step, keys, values, table, vec


def _attn_fwd(qkv, kv, bias, gain):
    T = qkv.shape[0]
    W = gain.shape[1]
    n_pairs = W // (PAIR * HD_A)

    def body(q_ref, k_ref, v_ref, bias_ref, gain_ref, o_ref):
        for e in range(PAIR):
            lanes = pl.ds(e * HD_A, HD_A)
            p, start = _attn_probs(q_ref, k_ref, bias_ref, e, pl.program_id(1))
            o = _dot(p.astype(BF16), v_ref[pl.ds(start, UNION), lanes])
            rr = lax.rsqrt(jnp.mean(o * o, axis=-1, keepdims=True) + EPS)
            o_ref[:, lanes] = (o * rr * gain_ref[:, lanes]).astype(o_ref.dtype)

    per_step, keys, values, table, vec = _attn_specs(T, n_pairs)
    return pl.pallas_call(
        body, name="attn_fwd", grid=(n_pairs, T // QROWS), in_specs=[per_step, keys, values, table, vec],
        out_specs=per_step, out_shape=jax.ShapeDtypeStruct((T, W), BF16),
        compiler_params=_params(("parallel", "arbitrary"), 40),
    )(qkv, kv, kv, bias, gain)


def _attn_bwd(qkv, kv, bias, gain, dmixin, rider=None):
    T = qkv.shape[0]
    W = gain.shape[1]
    n_pairs = W // (PAIR * HD_A)
    scale = HD_A ** -0.5

    def body(q_ref, k_ref, v_ref, bias_ref, gain_ref, don_ref, dq_ref, dk_ref, dv_ref, dbias_ref, dgain_ref):
        n = pl.program_id(1)

        @pl.when(n == 0)
        def _():
            dk_ref[...] = jnp.zeros_like(dk_ref)
            dv_ref[...] = jnp.zeros_like(dv_ref)
            dbias_ref[...] = jnp.zeros_like(dbias_ref)
            dgain_ref[...] = jnp.zeros_like(dgain_ref)

        for e in range(PAIR):
            lanes = pl.ds(e * HD_A, HD_A)
            p, start = _attn_probs(q_ref, k_ref, bias_ref, e, n)
            keys = pl.ds(start, UNION)
            pb = p.astype(BF16)
            vb = v_ref[keys, lanes]
            o = _dot(pb, vb)
            rr = lax.rsqrt(jnp.mean(o * o, axis=-1, keepdims=True) + EPS)
            on = o * rr
            d_on = don_ref[:, lanes]
            dgain_ref[:, lanes] += _colsum(d_on * on)
            dyo = d_on * gain_ref[:, lanes]
            do = rr * (dyo - on * jnp.mean(dyo * on, axis=-1, keepdims=True))
            dob = do.astype(BF16)
            dp = _dot(dob, vb, NT)
            ds = p * (dp - jnp.sum(do * o, axis=-1, keepdims=True))
            dbias_ref[e] += ds
            dsb = ds.astype(BF16)
            dq_ref[:, lanes] = (_dot(dsb, k_ref[keys, lanes]) * scale).astype(dq_ref.dtype)
            dk_ref[keys, lanes] += _dot(dsb, q_ref[:, lanes], TN) * scale
            dv_ref[keys, lanes] += _dot(pb, dob, TN)

    per_step, keys, values, table, vec = _attn_specs(T, n_pairs)
    H = n_pairs * PAIR
    return _call(
        body, rider, name="attn_bwd", grid=(n_pairs, T // QROWS),
        in_specs=[per_step, keys, values, table, vec, per_step],
        out_specs=[per_step, keys, keys, table, vec],
        out_shape=[jax.ShapeDtypeStruct((T, W), BF16), jax.ShapeDtypeStruct((KPAD + T, W), F32),
                   jax.ShapeDtypeStruct((KPAD + T, W), F32), jax.ShapeDtypeStruct((H, QROWS, UNION), F32),
                   jax.ShapeDtypeStruct((1, W), F32)],
        scratch_shapes=[], compiler_params=_params(("arbitrary", "arbitrary"), 40),
        operands=(qkv, kv, kv, bias, gain, dmixin))


N_DIAG = QROWS + UNION - 1


def _bias_table(rel_bias):
    H = rel_bias.shape[0]
    idx = np.clip(UNION - 1 - np.arange(N_DIAG), -MAX_REL, MAX_REL) + MAX_REL
    rolled = rel_bias[:, idx[(np.arange(N_DIAG) + QROWS - 1) % N_DIAG]]
    flat = jnp.broadcast_to(rolled[:, None, :], (H, QROWS, N_DIAG)).reshape(H, QROWS * N_DIAG)
    toep = flat[:, :QROWS * (N_DIAG - 1)].reshape(H, QROWS, N_DIAG - 1)[:, :, :UNION]
    g = np.arange(QROWS)[:, None] // CHUNK
    m = np.arange(UNION)[None, :] // CHUNK
    return jnp.where((m >= g) & (m <= g + N_PAST), toep, NEG)


def _bias_table_grad(dbias):
    H = dbias.shape[0]
    skew = jnp.pad(dbias, ((0, 0), (0, 0), (QROWS - 1, 0))).reshape(H, QROWS * N_DIAG)
    skew = jnp.pad(skew, ((0, 0), (0, QROWS))).reshape(H, QROWS, N_DIAG + 1)
    diag = jnp.sum(skew, axis=1)[:, :N_DIAG]
    n_far = UNION - MAX_REL
    far = jnp.sum(diag[:, :n_far], axis=1, keepdims=True)
    near = diag[:, n_far:][:, ::-1]
    zeros = jnp.zeros((H, MAX_REL - (QROWS - 1)), F32)
    return jnp.concatenate([zeros, near, far], axis=1)


def _tri(n, lower):
    r = lax.broadcasted_iota(jnp.int32, (n, n), 0)
    c = lax.broadcasted_iota(jnp.int32, (n, n), 1)
    return jnp.where((c <= r) if lower else (c >= r), 1.0, 0.0).astype(F32)


def _hgrn_gates(zq_ref, zf_ref, lbl_ref, q_s, k_s, b_s):
    lb = jax.nn.sigmoid(lbl_ref[0:1, :] - lbl_ref[1:2, :])
    zq = zq_ref[...]
    sig = jax.nn.sigmoid(zf_ref[...])
    f = lb + (1.0 - lb) * sig
    sq = jax.nn.sigmoid(zq)
    q_s[...] = zq * sq
    k_s[...] = 1.0 - f
    b_s[...] = _dot(_tri(CHUNK, True), jnp.log(f), precision=HIGHEST)
    return lb, sig, f, sq


def _sub_rows(i):
    return pl.ds(i * SUB, SUB)


def _row_mask(s):
    return lax.broadcasted_iota(jnp.int32, (SUB, HD_B), 0) >= s


def _decay_from(b_sub, b_row, s):
    return jnp.where(_row_mask(s), jnp.exp(jnp.minimum(b_sub - b_row, 0.0)), 0.0)


def _hgrn_fwd(proj, lb_logits, gnorm_g, rider=None):
    T = proj.shape[0]
    nC = T // CHUNK
    W = lb_logits.shape[1]
    G = W // HD_B // HGRN_HEADS
    col0 = (proj.shape[1] - 4 * W) // (HD_B * HGRN_HEADS)
    wide = HGRN_HEADS * HD_B

    def body(*refs):
        @pl.when(pl.program_id(1) == 0)
        def _():
            refs[9][...] = jnp.zeros_like(refs[9])

        for h in range(HGRN_HEADS):
            lanes = pl.ds(h * HD_B, HD_B)
            one_head(*[r.at[:, lanes] for r in refs[:5]], refs[5], *[r.at[:, lanes] for r in refs[6:8]],
                     *[r.at[h] for r in refs[8:]])

    def one_head(zq_ref, zf_ref, xi_ref, zg_ref, lbl_ref, gn_ref, mix_ref, o_ref, stall_ref, st_ref, q_s, k_s, b_s, acc_s):
        _hgrn_gates(zq_ref, zf_ref, lbl_ref, q_s, k_s, b_s)
        q, k, b = q_s[...], k_s[...], b_s[...]
        st = st_ref[...]
        stall_ref[...] = st
        b_last = b_s[CHUNK - 1:CHUNK, :]
        acc_s[...] = _dot((q * jnp.exp(b)).astype(BF16), st.astype(BF16), NT)
        for i in range(CHUNK // SUB):
            rows = _sub_rows(i)
            q_i, b_i = q_s[rows, :], b_s[rows, :]
            acc = jnp.zeros((SUB, HD_B), F32)
            if i:
                past = pl.ds(0, i * SUB)
                b_ref = b_s[i * SUB - 1:i * SUB, :]
                qs = (q_i * jnp.exp(b_i - b_ref)).astype(BF16)
                ks = (k_s[past, :] * jnp.exp(b_ref - b_s[past, :])).astype(BF16)
                acc += _dot(_dot(qs, ks, NT).astype(BF16), xi_ref[past, :].astype(BF16))
            for s in range(SUB):
                row = pl.ds(i * SUB + s, 1)
                w = q_i * _decay_from(b_i, b_s[row, :], s)
                acc += jnp.sum(w * k_s[row, :], axis=-1, keepdims=True) * xi_ref[row, :]
            acc_s[rows, :] += acc
        o = acc_s[...]
        kd = (k * jnp.exp(b_last - b)).astype(BF16)
        st_ref[...] = st * jnp.exp(b_last) + _dot(xi_ref[...].astype(BF16), kd, TN)
        o_ref[...] = o
        zg = zg_ref[...]
        rr = lax.rsqrt(jnp.mean(o * o, axis=-1, keepdims=True) + EPS)
        mix_ref[...] = (o * rr * gn_ref[...] * (zg * jax.nn.sigmoid(zg))).astype(mix_ref.dtype)

    col = lambda part: pl.BlockSpec((CHUNK, wide), lambda g, n: (n, col0 + part * G + g))
    out_blk = pl.BlockSpec((CHUNK, wide), lambda g, n: (n, g))
    tile = pltpu.VMEM((HGRN_HEADS, CHUNK, HD_B), F32)
    return _call(
        body, rider, name="hgrn_fwd", grid=(G, nC),
        in_specs=[col(0), col(1), col(2), col(3), pl.BlockSpec((2, wide), lambda g, n: (0, g)),
                  pl.BlockSpec((1, HD_B), lambda g, n: (0, 0))],
        out_specs=[out_blk, out_blk, pl.BlockSpec((HGRN_HEADS, None, HD_B, HD_B), lambda g, n: (g, n, 0, 0))],
        out_shape=[jax.ShapeDtypeStruct((T, W), BF16), jax.ShapeDtypeStruct((T, W), F32),
                   jax.ShapeDtypeStruct((G * HGRN_HEADS, nC, HD_B, HD_B), F32)],
        scratch_shapes=[pltpu.VMEM((HGRN_HEADS, HD_B, HD_B), F32), tile, tile, tile, tile],
        compiler_params=_params(("arbitrary", "arbitrary")),
        operands=(proj, proj, proj, proj, lb_logits, gnorm_g))


def _hgrn_bwd(proj, lb_logits, gnorm_g, o_b, st_all, dmixin, rider=None):
    T = proj.shape[0]
    nC = T // CHUNK
    W = lb_logits.shape[1]
    G = W // HD_B // HGRN_HEADS
    wide = HGRN_HEADS * HD_B
    col0 = (proj.shape[1] - 4 * W) // wide
    dcol0 = (dmixin.shape[1] - W) // wide

    def body(*refs):
        g, n = pl.program_id(0), pl.program_id(1)
        dl0_ref, dgn_ref, dst_ref = refs[13:16]

        @pl.when(n == 0)
        def _():
            dst_ref[...] = jnp.zeros_like(dst_ref)
            dl0_ref[...] = jnp.zeros_like(dl0_ref)

        @pl.when((n == 0) & (g == 0))
        def _():
            dgn_ref[...] = jnp.zeros_like(dgn_ref)

        for h in range(HGRN_HEADS):
            lanes = pl.ds(h * HD_B, HD_B)
            cut = lambda r: r.at[:, lanes]
            one_head(*[cut(r) for r in refs[:5]], refs[5], cut(refs[6]), refs[7].at[h], cut(refs[8]),
                     *[cut(r) for r in refs[9:14]], dgn_ref, *[r.at[h] for r in refs[15:]])

    def one_head(zq_ref, zf_ref, xi_ref, zg_ref, lbl_ref, gn_ref, o_ref, st_ref, dout_ref,
                 dzq_ref, dzf_ref, dxi_ref, dzg_ref, dl0_ref, dgn_ref, dst_ref, q_s, k_s, b_s, do_s, dq_s, dk_s, di_s):
        lb, sig, f, sq = _hgrn_gates(zq_ref, zf_ref, lbl_ref, q_s, k_s, b_s)
        q, k, b = q_s[...], k_s[...], b_s[...]
        zg, o, dout = zg_ref[...], o_ref[...], dout_ref[...]
        sg = jax.nn.sigmoid(zg)
        rr = lax.rsqrt(jnp.mean(o * o, axis=-1, keepdims=True) + EPS)
        on = o * rr
        gn = gn_ref[...]
        dzg_ref[...] = (dout * on * gn * sg * (1.0 + zg * (1.0 - sg))).astype(dzg_ref.dtype)
        d_on = dout * zg * sg
        dgn_ref[...] += _colsum(d_on * on)
        d_on = d_on * gn
        do = rr * (d_on - on * jnp.mean(d_on * on, axis=-1, keepdims=True))
        do_s[...] = do
        dob = do.astype(BF16)
        st, dst = st_ref[...], dst_ref[...]
        b_last = b_s[CHUNK - 1:CHUNK, :]
        eb, e_last, k_dec = jnp.exp(b), jnp.exp(b_last), jnp.exp(b_last - b)
        qt, kd = q * eb, k * k_dec
        dstb = dst.astype(BF16)
        xib = xi_ref[...].astype(BF16)
        d_kd = _dot(xib, dstb)
        dq_s[...] = _dot(dob, st.astype(BF16)) * eb
        dk_s[...] = d_kd * k_dec
        di_s[...] = _dot(kd.astype(BF16), dstb, NT)
        d_b_last = e_last * _colsum(st * dst) + _colsum(d_kd * kd)
        dst_ref[...] = _dot(dob, qt.astype(BF16), TN) + dst * e_last
        for i in range(CHUNK // SUB):
            rows = _sub_rows(i)
            q_i, b_i, do_i = q_s[rows, :], b_s[rows, :], do_s[rows, :]
            dq_i = jnp.zeros((SUB, HD_B), F32)
            if i:
                past = pl.ds(0, i * SUB)
                b_ref = b_s[i * SUB - 1:i * SUB, :]
                e_q, e_k = jnp.exp(b_i - b_ref), jnp.exp(b_ref - b_s[past, :])
                qs, ks = (q_i * e_q).astype(BF16), (k_s[past, :] * e_k).astype(BF16)
                xi_p, do_b = xi_ref[past, :].astype(BF16), do_i.astype(BF16)
                di_s[past, :] += _dot(_dot(ks, qs, NT).astype(BF16), do_b)
                dq_i += _dot(_dot(do_b, xi_p, NT).astype(BF16), ks) * e_q
                dk_s[past, :] += _dot(_dot(xi_p, do_b, NT).astype(BF16), qs) * e_k
            for s in range(SUB):
                row = pl.ds(i * SUB + s, 1)
                k_row, i_row = k_s[row, :], xi_ref[row, :]
                e = _decay_from(b_i, b_s[row, :], s)
                w = q_i * e
                a_col = jnp.sum(w * k_row, axis=-1, keepdims=True)
                da_col = jnp.sum(do_i * i_row, axis=-1, keepdims=True)
                di_s[row, :] += _colsum(a_col * do_i)
                dq_i += da_col * e * k_row
                dk_s[row, :] += _colsum(da_col * w)
            dq_s[rows, :] += dq_i
        dq, dk = dq_s[...], dk_s[...]
        db = q * dq - k * dk
        is_last = lax.broadcasted_iota(jnp.int32, (CHUNK, HD_B), 0) == CHUNK - 1
        db = db + jnp.where(is_last, d_b_last, 0.0)
        df = _dot(_tri(CHUNK, False), db, precision=HIGHEST) / f - dk
        dzf_ref[...] = (df * (1.0 - lb) * sig * (1.0 - sig)).astype(dzf_ref.dtype)
        dl0_ref[...] += _colsum(df * (1.0 - sig)) * (lb * (1.0 - lb))
        zq = zq_ref[...]
        dzq_ref[...] = (dq * sq * (1.0 + zq * (1.0 - sq))).astype(dzq_ref.dtype)
        dxi_ref[...] = di_s[...].astype(dxi_ref.dtype)

    rev = lambda n: nC - 1 - n
    col = lambda part: pl.BlockSpec((CHUNK, wide), lambda g, n: (rev(n), col0 + part * G + g))
    blk = pl.BlockSpec((CHUNK, wide), lambda g, n: (rev(n), g))
    tile = pltpu.VMEM((HGRN_HEADS, CHUNK, HD_B), F32)
    out_big = jax.ShapeDtypeStruct((T, W), BF16)
    return _call(
        body, rider, name="hgrn_bwd", grid=(G, nC),
        in_specs=[col(0), col(1), col(2), col(3), pl.BlockSpec((2, wide), lambda g, n: (0, g)),
                  pl.BlockSpec((1, HD_B), lambda g, n: (0, 0)), blk,
                  pl.BlockSpec((HGRN_HEADS, None, HD_B, HD_B), lambda g, n: (g, rev(n), 0, 0)),
                  pl.BlockSpec((CHUNK, wide), lambda g, n: (rev(n), dcol0 + g))],
        out_specs=[blk, blk, blk, blk, pl.BlockSpec((1, wide), lambda g, n: (0, g)),
                   pl.BlockSpec((1, HD_B), lambda g, n: (0, 0))],
        out_shape=[out_big, out_big, out_big, out_big, jax.ShapeDtypeStruct((1, W), F32),
                   jax.ShapeDtypeStruct((1, HD_B), F32)],
        scratch_shapes=[pltpu.VMEM((HGRN_HEADS, HD_B, HD_B), F32)] + [tile] * 7,
        compiler_params=_params(("arbitrary", "arbitrary")),
        operands=(proj, proj, proj, proj, lb_logits, gnorm_g, o_b, st_all, dmixin))


def _adamw_math(g, w, m, v):
    m = B1 * m + (1.0 - B1) * g
    v = B2 * v + (1.0 - B2) * (g * g)
    m_hat = m / (1.0 - B1 ** STEP)
    v_hat = v / (1.0 - B2 ** STEP)
    return -LR * (m_hat / (jnp.sqrt(v_hat) + ADAM_EPS) + WD * w), m, v


def _adamw(g, w, m, v, name, rider=None):
    R, C = g.shape
    tr = _row_tile(R, C)

    def body(g_ref, w_ref, m_ref, v_ref, d_ref, mo_ref, vo_ref):
        d_ref[...], mo_ref[...], vo_ref[...] = _adamw_math(g_ref[...], w_ref[...], m_ref[...], v_ref[...])

    blk = pl.BlockSpec((tr, C), lambda i: (i, 0))
    return _call(
        body, rider, name=name, grid=(R // tr,), in_specs=[blk] * 4, out_specs=[blk] * 3,
        out_shape=[jax.ShapeDtypeStruct((R, C), F32)] * 3, scratch_shapes=[],
        compiler_params=_params(("arbitrary",), 40), operands=(g, w, m, v))


def _sum_pair(g_full, from_sibling, sel, name):
    Q, K, Ns = g_full.shape
    kh = K // 2
    tr = _row_tile(kh, Ns)
    nh = kh // tr

    def body(sel_ref, a_ref, b_ref, o_ref):
        o_ref[...] = (a_ref[...].astype(F32) + b_ref[...].astype(F32)).astype(o_ref.dtype)

    return pl.pallas_call(
        body, name=name,
        grid_spec=pltpu.PrefetchScalarGridSpec(
            num_scalar_prefetch=1, grid=(Q, nh),
            in_specs=[pl.BlockSpec((None, tr, Ns), lambda q, i, sel: (q, sel[1] * nh + i, 0)),
                      pl.BlockSpec((None, tr, Ns), lambda q, i, sel: (q, i, 0))],
            out_specs=pl.BlockSpec((None, tr, Ns), lambda q, i, sel: (q, i, 0))),
        out_shape=jax.ShapeDtypeStruct((Q, kh, Ns), BF16), compiler_params=_params(("parallel", "parallel")),
    )(sel, g_full, from_sibling)


def _sum_chips(pair_sum, from_chips, sel, name):
    Q, kh, Ns = pair_sum.shape
    tr = _row_tile(kh, Ns)
    nh = kh // tr

    def body(sel_ref, a_ref, b0_ref, b1_ref, b2_ref, o_ref):
        up = lambda r: r[...].astype(F32)
        o_ref[...] = ((up(a_ref) + up(b0_ref)) + up(b1_ref)) + up(b2_ref)

    recv = lambda k: pl.BlockSpec((None, tr, Ns), lambda i, sel: (k, i, 0))
    return pl.pallas_call(
        body, name=name,
        grid_spec=pltpu.PrefetchScalarGridSpec(
            num_scalar_prefetch=1, grid=(nh,),
            in_specs=[pl.BlockSpec((None, tr, Ns), lambda i, sel: (sel[0], i, 0)), recv(0), recv(1), recv(2)],
            out_specs=pl.BlockSpec((tr, Ns), lambda i, sel: (sel[1] * nh + i, 0))),
        out_shape=jax.ShapeDtypeStruct((2 * kh, Ns), F32), compiler_params=_params(("parallel",)),
    )(sel, pair_sum, from_chips, from_chips, from_chips)


def _gather_small(v, name):
    R, L = v.shape

    def body(v_ref, out_ref, send_sems, recv_sems):
        x, y, c = _place()
        me = 4 * x + 2 * y + c
        out_ref[me] = v_ref[...]
        peers = [(_flip(x, k >> 2 & 1), _flip(y, k >> 1 & 1), _flip(c, k & 1)) for k in range(1, N_DEV)]

        def copy(k, row, to):
            return pltpu.make_async_remote_copy(src_ref=v_ref, dst_ref=out_ref.at[row], send_sem=send_sems.at[k],
                                                recv_sem=recv_sems.at[k], device_id=to, device_id_type=MESH)

        sends = [copy(k, me, peer) for k, peer in enumerate(peers)]
        for cp in sends:
            cp.start()
        for k, (px, py, pc) in enumerate(peers):
            copy(k, 4 * px + 2 * py + pc, (x, y, c)).wait_recv()
        for cp in sends:
            cp.wait_send()

    vmem = pl.BlockSpec(memory_space=pltpu.VMEM)
    return pl.pallas_call(
        body, name=name, in_specs=[vmem], out_specs=vmem, out_shape=jax.ShapeDtypeStruct((N_DEV, R, L), F32),
        scratch_shapes=[pltpu.SemaphoreType.DMA((N_DEV - 1,)), pltpu.SemaphoreType.DMA((N_DEV - 1,))],
    )(v)


def _send_pair(g_full, name):
    Q, K, Ns = g_full.shape
    kh = K // 2

    def body(g_ref, got_ref, send_sem, recv_sem):
        x, y, c = _place()
        cp = pltpu.make_async_remote_copy(src_ref=g_ref.at[:, pl.ds((1 - c) * kh, kh), :], dst_ref=got_ref,
                                          send_sem=send_sem, recv_sem=recv_sem, device_id=(x, y, 1 - c),
                                          device_id_type=MESH)
        cp.start()
        cp.wait()

    return pl.pallas_call(
        body, name=name, in_specs=[ANY], out_specs=ANY, out_shape=jax.ShapeDtypeStruct((Q, kh, Ns), g_full.dtype),
        scratch_shapes=[pltpu.SemaphoreType.DMA, pltpu.SemaphoreType.DMA],
    )(g_full)


def _share_halves(block, name):
    K, Ns = block.shape
    kh = K // 2

    def body(_, out_ref, send_sem, recv_sem):
        x, y, c = _place()
        mine, theirs = out_ref.at[pl.ds(c * kh, kh), :], out_ref.at[pl.ds((1 - c) * kh, kh), :]
        cp = pltpu.make_async_remote_copy(src_ref=mine, dst_ref=mine, send_sem=send_sem, recv_sem=recv_sem,
                                          device_id=(x, y, 1 - c), device_id_type=MESH)
        cp.start()
        pltpu.make_async_remote_copy(src_ref=theirs, dst_ref=theirs, send_sem=send_sem, recv_sem=recv_sem,
                                     device_id=(x, y, 1 - c), device_id_type=MESH).wait_recv()
        cp.wait_send()

    return pl.pallas_call(
        body, name=name, in_specs=[ANY], out_specs=ANY, out_shape=jax.ShapeDtypeStruct((K, Ns), block.dtype),
        input_output_aliases={0: 0},
        scratch_shapes=[pltpu.SemaphoreType.DMA, pltpu.SemaphoreType.DMA],
    )(block)


def _silu(v):
    return v * jax.nn.sigmoid(v)


def _ada_fwd(c_all, w_ada, tn=512):
    M, D = c_all.shape
    Ns = w_ada.shape[1]

    def body(c_ref, w_ref, o_ref):
        o_ref[...] = _dot(_silu(c_ref[...]).astype(BF16), w_ref[...].astype(BF16))

    return pl.pallas_call(
        body, name="ada_fwd", grid=(Ns // tn,),
        in_specs=[pl.BlockSpec((M, D), lambda j: (0, 0)), pl.BlockSpec((D, tn), lambda j: (0, j))],
        out_specs=pl.BlockSpec((M, tn), lambda j: (0, j)), out_shape=jax.ShapeDtypeStruct((M, Ns), F32),
        compiler_params=_params(("parallel",)),
    )(c_all, w_ada)


def _ada_bwd(c_all, dmod, w, m, v, rider=None, tk=256, tn=1536):
    M, D = c_all.shape
    Ns = dmod.shape[1]

    def body(c_ref, d_ref, w_ref, m_ref, v_ref, g_ref, dl_ref, mo_ref, vo_ref):
        g = _dot(_silu(c_ref[...]).astype(BF16), d_ref[...].astype(BF16), TN)
        g_ref[...] = g
        dl_ref[...], mo_ref[...], vo_ref[...] = _adamw_math(g, w_ref[...], m_ref[...], v_ref[...])

    blk = pl.BlockSpec((tk, tn), lambda i, j: (i, j))
    return _call(
        body, rider, name="ada_bwd", grid=(D // tk, Ns // tn),
        in_specs=[pl.BlockSpec((M, tk), lambda i, j: (0, i)), pl.BlockSpec((M, tn), lambda i, j: (0, j)), blk, blk, blk],
        out_specs=[blk] * 4, out_shape=[jax.ShapeDtypeStruct((D, Ns), F32)] * 4, scratch_shapes=[],
        compiler_params=_params(("arbitrary", "arbitrary"), 40), operands=(c_all, dmod, w, m, v))


def _small_update(g_all, w, m, v):
    R, L = w.shape

    def body(g_ref, w_ref, m_ref, v_ref, go_ref, d_ref, mo_ref, vo_ref):
        g = g_ref[0]
        for d in range(1, N_DEV):
            g = g + g_ref[d]
        go_ref[...] = g
        d_ref[...], mo_ref[...], vo_ref[...] = _adamw_math(g, w_ref[...], m_ref[...], v_ref[...])

    return pl.pallas_call(body, name="small_update", out_shape=[jax.ShapeDtypeStruct((R, L), F32)] * 4)(g_all, w, m, v)


def _pack(parts, rows):
    flat = jnp.concatenate([p.reshape(-1) for p in parts])
    return jnp.pad(flat, (0, rows * 128 - flat.shape[0])).reshape(rows, 128)


def _unpack(packed, shapes):
    flat, out, at = packed.reshape(-1), [], 0
    for shp in shapes:
        size = 1
        for d in shp:
            size *= d
        out.append(flat[at:at + size].reshape(shp))
        at += size
    return out


def _layer(x, tgt, mod, wts, rel_bias, attn_norm_g, lb_logits, gnorm_g, ln1_g, ln1_b, ln2_g, ln2_b, place=None):
    T, D = x.shape
    aw = attn_norm_g.shape[1]
    shift1, scale1, gate1, shift2, scale2, gate2 = [mod[i:i + 1] for i in range(6)]

    def gather(n):
        return None if place is None else _gather_rider(wts[n])

    def gathered(n, rode):
        return wts[n] if place is None else lax.dynamic_update_index_in_dim(rode[0], wts[n], place[0], 0)

    def pair_sum(n, g):
        g = g.reshape(N_CHIPS, -1, g.shape[2])
        return g if place is None else _sum_pair(g, _send_pair(g, n + "_send_pair"), place[1], n + "_sum_pair")

    def to_chips(p):
        return None if place is None else _chips_rider(p)

    def reduced(n, p, rode):
        return p if place is None else _share_halves(_sum_chips(p, rode[0], place[1], n + "_sum_chips"), n + "_share")

    def carrying(mm, *args, rider, **kw):
        return mm(*args, rider=rider, **kw) if rider is not None else (mm(*args, **kw), None)

    w_in = gathered("w_in", None if place is None else _alone(gather("w_in"), "gather_w_in"))
    h1 = _pre_mixer(x, scale1, shift1)
    n_qkv = 3 * aw // 256
    qkv, rode = carrying(_mm_nn, h1, w_in, tm=1024, tn=256, tk=D, name="proj_qkv", cols=(0, n_qkv), o_dtype=BF16,
                         rider=gather("w_o"))
    w_o3 = gathered("w_o", rode).reshape(1, D, D)
    proj = _mm_nn(h1, w_in, tm=1024, tn=256, tk=D, name="proj_rec", cols=(n_qkv, N_CHIPS * w_in.shape[2] // 256))
    kv = jnp.pad(qkv[:, aw:], ((KPAD, 0), (0, 0)))
    bias = _bias_table(rel_bias)
    mix_a = _attn_fwd(qkv, kv, bias, attn_norm_g)
    (mix_b, o_b, st_all), rode = _hgrn_fwd(proj, lb_logits, gnorm_g, rider=gather("w_ffn_in"))
    w_ffn_in = gathered("w_ffn_in", rode)
    mixin = jnp.concatenate([mix_a, mix_b], axis=1)
    mix = _mm_nn(mixin, w_o3, tm=1024, tn=512, tk=D, name="mix_out")
    x1, h2 = _post_mixer(mix, x, gate1, ln1_g, ln1_b, scale2, shift2)
    ff, rode = carrying(_mm_nn, h2, w_ffn_in, tm=1024, tn=256, tk=D, name="ffn_in", rider=gather("w_ffn_out"))
    w_out3 = gathered("w_ffn_out", rode)
    w_out3 = w_out3.reshape(1, -1, w_out3.shape[2])
    d_ff = w_out3.shape[1]
    act = _swiglu_fwd(ff)
    f = _mm_nn(act, w_out3, tm=1024, tn=1024, tk=d_ff // 4, name="ffn_out")
    du2, df, acc2 = _loss_head(f, x1, tgt, gate2, ln2_g, ln2_b)
    loss = (0.5 / D) * jnp.sum(acc2[3])
    da = _mm_nt(df, w_out3, tm=1024, to=512, tn=D, name="d_act")
    p_out = pair_sum("w_ffn_out", _mm_tn(act, df, q=1, tk=512, tn=1024, tt=T, name="g_ffn_out"))
    dff = jnp.concatenate(_swiglu_bwd(da, ff), axis=1)
    dh2, rode = carrying(_mm_nt, dff, w_ffn_in, tm=1024, to=1024, tn=w_ffn_in.shape[2], name="d_h2",
                         rider=to_chips(p_out))
    g_ffn_out = reduced("w_ffn_out", p_out, rode)
    p_fin = pair_sum("w_ffn_in", _mm_tn(h2, dff, q=N_CHIPS, tk=512, tn=w_ffn_in.shape[2] // 2, tt=T, name="g_ffn_in"))
    du1, dmix, acc1 = _mid_bwd(dh2, du2, x1, mix, x, gate1, ln1_g, scale2)
    dmixin = _mm_nt(dmix, w_o3, tm=1024, to=512, tn=D, name="d_mixin")
    p_o = pair_sum("w_o", _mm_tn(mixin, dmix, q=1, tk=512, tn=1024, tt=T, name="g_o"))
    (dq, dk, dv, dbias, dgain), rode = _attn_bwd(qkv, kv, bias, attn_norm_g, dmixin, rider=to_chips(p_o))
    g_o = reduced("w_o", p_o, rode)
    (dzq, dzf, dxi, dzg, dl0, dgn), rode = _hgrn_bwd(proj, lb_logits, gnorm_g, o_b, st_all, dmixin, rider=to_chips(p_fin))
    g_ffn_in = reduced("w_ffn_in", p_fin, rode)
    dproj = jnp.concatenate([dq, dk[KPAD:].astype(BF16), dv[KPAD:].astype(BF16), dzq, dzf, dxi, dzg], axis=1)
    p_in = pair_sum("w_in", _mm_tn(h1, dproj, q=N_CHIPS, tk=512, tn=w_in.shape[2] // 2, tt=T, name="g_in"))
    first_half = None if place is None else _chips_rider(p_in, rows=(0, p_in.shape[1] // 2))
    dh1, rode = carrying(_mm_nt, dproj, w_in, tm=1024, to=1024, tn=w_in.shape[2], name="d_h1", rider=first_half)
    grad_x, acc0 = _first_bwd(dh1, du1, x, scale1)
    dmod = jnp.concatenate([acc0[1:2], acc0[0:1], acc1[4:5], acc1[1:2], acc1[0:1], acc2[2:3]], axis=0)
    small = dict(rel_bias=_bias_table_grad(dbias), attn_norm_g=dgain,
                 lb_logits=jnp.concatenate([dl0, -dl0], axis=0), gnorm_g=dgn,
                 ln1_g=acc1[2:3], ln1_b=acc1[3:4], ln2_g=acc2[0:1], ln2_b=acc2[1:2])
    g_in = p_in if place is None else (p_in, rode[0])
    return loss, grad_x, dict(w_in=g_in, w_o=g_o, w_ffn_in=g_ffn_in, w_ffn_out=g_ffn_out), dmod, small


SMALL = ("rel_bias", "attn_norm_g", "lb_logits", "gnorm_g", "ln1_g", "ln1_b", "ln2_g", "ln2_b")
SMALL_ROWS = 256


def kernel(x, c, w_ada, b_ada, w_in, rel_bias, attn_norm_g, lb_logits, gnorm_g, w_o, ln1_g, ln1_b, w_ffn_in, w_ffn_out, ln2_g, ln2_b, loss_target, m_w_ada, m_b_ada, m_w_in, m_rel_bias, m_attn_norm_g, m_lb_logits, m_gnorm_g, m_w_o, m_ln1_g, m_ln1_b, m_w_ffn_in, m_w_ffn_out, m_ln2_g, m_ln2_b, v_w_ada, v_b_ada, v_w_in, v_rel_bias, v_attn_norm_g, v_lb_logits, v_gnorm_g, v_w_o, v_ln1_g, v_ln1_b, v_w_ffn_in, v_w_ffn_out, v_ln2_g, v_ln2_b):
    mx, my, mc = _place()
    me = 4 * mx + 2 * my + mc
    chip = 2 * mx + my
    sel = jnp.stack([chip, mc]).astype(jnp.int32)
    D = x.shape[2]
    ns_ada = w_ada.shape[2]

    big = dict(w_in=(w_in, m_w_in, v_w_in), w_o=(w_o, m_w_o, v_w_o), w_ffn_in=(w_ffn_in, m_w_ffn_in, v_w_ffn_in),
               w_ffn_out=(w_ffn_out, m_w_ffn_out, v_w_ffn_out))
    shards = {n: t[0][0].astype(BF16) for n, t in big.items()}

    c_all = _gather_small(c.reshape(D // 128, 128), "gather_c").reshape(N_DEV, D)
    c_all = jnp.pad(c_all, ((0, 16 - N_DEV), (0, 0)))
    mod_cols = _ada_fwd(c_all, w_ada[0])[:N_DEV]
    mod_all = _gather_small(mod_cols.reshape(-1, 128), "gather_mod").reshape(N_DEV, N_DEV, ns_ada)
    mod = lax.dynamic_index_in_dim(mod_all[::2], me, axis=1, keepdims=False)
    mod = (mod.reshape(1, -1) + b_ada).reshape(6, D)

    loss, grad_x, g_big, dmod, g_small = _layer(
        x[0], loss_target[0], mod, shards, rel_bias[0], attn_norm_g, lb_logits, gnorm_g, ln1_g, ln1_b, ln2_g, ln2_b,
        place=(chip, sel))
    loss = lax.psum(loss, ("x", "y", "c"))

    grads, deltas, new_m, new_v = {}, {}, {}, {}
    p_in, got_in = g_big["w_in"]
    quarter = p_in.shape[1] // 4

    def update(n, g, rider=None):
        w, m, v = big[n]
        (d, mo, vo), rode = _adamw(g, w[0], m[0], v[0], "adamw_" + n, rider=rider)
        grads[n], deltas[n], new_m[n], new_v[n] = g[None], d[None], mo[None], vo[None]
        return rode

    got_in, = update("w_ffn_in", g_big["w_ffn_in"], _chips_rider(p_in, rows=(2 * quarter, quarter), into=got_in))
    update("w_ffn_out", g_big["w_ffn_out"])
    update("w_o", g_big["w_o"])

    small_in = dict(rel_bias=(rel_bias, m_rel_bias, v_rel_bias), attn_norm_g=(attn_norm_g, m_attn_norm_g, v_attn_norm_g),
                    lb_logits=(lb_logits, m_lb_logits, v_lb_logits), gnorm_g=(gnorm_g, m_gnorm_g, v_gnorm_g),
                    ln1_g=(ln1_g, m_ln1_g, v_ln1_g), ln1_b=(ln1_b, m_ln1_b, v_ln1_b), ln2_g=(ln2_g, m_ln2_g, v_ln2_g),
                    ln2_b=(ln2_b, m_ln2_b, v_ln2_b))
    g_all = _gather_small(_pack([dmod] + [g_small[n] for n in SMALL], SMALL_ROWS), "gather_small")
    packed = [_pack([t] + [small_in[n][i] for n in SMALL], SMALL_ROWS)
              for i, t in enumerate((b_ada, m_b_ada, v_b_ada))]
    shapes = [b_ada.shape] + [small_in[n][0].shape for n in SMALL]
    outs = [_unpack(o, shapes) for o in _small_update(g_all, *packed)]
    for i, n in enumerate(("b_ada",) + SMALL):
        grads[n], deltas[n], new_m[n], new_v[n] = outs[0][i], outs[1][i], outs[2][i], outs[3][i]

    dmod_all = g_all[:, :6 * D // 128].reshape(N_DEV, 6 * D)
    dmod_cols = lax.dynamic_slice_in_dim(dmod_all, chip * ns_ada, ns_ada, axis=1)
    dmod_cols = jnp.pad(dmod_cols, ((0, 16 - N_DEV), (0, 0)))
    (g, d, mo, vo), (got_in,) = _ada_bwd(c_all, dmod_cols, w_ada[0], m_w_ada[0], v_w_ada[0],
                                         rider=_chips_rider(p_in, rows=(3 * quarter, quarter), into=got_in))
    grads["w_ada"], deltas["w_ada"], new_m["w_ada"], new_v["w_ada"] = g[None], d[None], mo[None], vo[None]
    update("w_in", _share_halves(_sum_chips(p_in, got_in, sel, "w_in_sum_chips"), "w_in_share"))

    order = ("w_ada", "b_ada", "w_in", "rel_bias", "attn_norm_g", "lb_logits", "gnorm_g", "w_o", "ln1_g", "ln1_b",
             "w_ffn_in", "w_ffn_out", "ln2_g", "ln2_b")
    return (loss, grad_x[None], *[grads[n] for n in order], *[deltas[n] for n in order],
            *[new_m[n] for n in order], *[new_v[n] for n in order])
```

```python
import numpy as np
import jax
import jax.numpy as jnp
from jax import lax
from jax.experimental import pallas as pl
from jax.experimental.pallas import tpu as pltpu

F32 = jnp.float32
BF16 = jnp.bfloat16
MESH = pl.DeviceIdType.MESH
HIGHEST = lax.Precision.HIGHEST

CHUNK = 64
N_PAST = 8
QG = 4
QROWS = QG * CHUNK
KPAD = N_PAST * CHUNK
UNION = (QG + N_PAST) * CHUNK
HD_A = 64
HD_B = 128
SUB = 16
HGRN_HEADS = 4
MAX_REL = 256
EPS = 1e-5
ALPHA = 2.0 ** 0.25
LR, B1, B2, ADAM_EPS, WD, STEP = 1e-3, 0.9, 0.999, 1e-8, 0.01, 10
N_CHIPS = 4
N_DEV = 8
NEG = -1e30
TILE_BYTES = 3 << 19

NN = ((1,), (0,))
NT = ((1,), (1,))
TN = ((0,), (0,))


def _dot(a, b, dims=NN, precision=None):
    return lax.dot_general(a, b, (dims, ((), ())), preferred_element_type=F32, precision=precision)


def _params(sem=None, vmem_mb=None, **kw):
    return pltpu.CompilerParams(dimension_semantics=sem,
                                vmem_limit_bytes=None if vmem_mb is None else vmem_mb << 20, **kw)


def _row_tile(rows, cols):
    for cand in (512, 256, 128, 64, 32, 16, 8):
        if rows % cand == 0 and cand * cols * 4 <= TILE_BYTES:
            return cand
    raise ValueError((rows, cols))


def _place():
    return lax.axis_index("x"), lax.axis_index("y"), lax.axis_index("c")


def _flip(v, bit):
    return 1 - v if bit else v


ANY = pl.BlockSpec(memory_space=pl.ANY)
CHIP_FLIPS = ((1, 0), (0, 1), (1, 1))


class _Rider:
    def __init__(self, operands, out_shape, n_sems, start, finish, aliases=None):
        self.operands, self.out_shape, self.n_sems, self.start, self.finish = operands, out_shape, n_sems, start, finish
        self.aliases = aliases or {}


def _call(body, rider, *, name, grid, in_specs, out_specs, out_shape, scratch_shapes, compiler_params, operands):
    if rider is None:
        outs = pl.pallas_call(body, name=name, grid=grid, in_specs=in_specs, out_specs=out_specs, out_shape=out_shape,
                              scratch_shapes=scratch_shapes, compiler_params=compiler_params)(*operands)
        return list(outs), []
    n_in, n_out, n_sc = len(in_specs), len(out_specs), len(scratch_shapes)
    r_in, r_out = len(rider.operands), len(rider.out_shape)

    def carried(*refs):
        refs = list(refs)
        cuts = [n_in, r_in, n_out, r_out, n_sc]
        ins, r_ins, outs, r_outs, scratch = [[refs.pop(0) for _ in range(n)] for n in cuts]
        first, last = None, None
        for axis, size in enumerate(grid):
            i = pl.program_id(axis)
            first = (i == 0) if first is None else first & (i == 0)
            last = (i == size - 1) if last is None else last & (i == size - 1)

        @pl.when(first)
        def _():
            rider.start(r_ins, r_outs, *refs)

        body(*ins, *outs, *scratch)

        @pl.when(last)
        def _():
            rider.finish(r_ins, r_outs, *refs)

    sems = [pltpu.SemaphoreType.DMA((rider.n_sems,)), pltpu.SemaphoreType.DMA((rider.n_sems,))]
    outs = pl.pallas_call(carried, name=name, grid=grid, in_specs=list(in_specs) + [ANY] * r_in,
                          out_specs=list(out_specs) + [ANY] * r_out, out_shape=list(out_shape) + rider.out_shape,
                          scratch_shapes=list(scratch_shapes) + sems, compiler_params=compiler_params,
                          input_output_aliases={n_in + i: n_out + o for i, o in rider.aliases.items()},
                          )(*operands, *rider.operands)
    return list(outs[:n_out]), list(outs[n_out:])


def _alone(rider, name):
    def body(*refs):
        ins, outs, sems = refs[:len(rider.operands)], refs[len(rider.operands):-2], refs[-2:]
        rider.start(ins, outs, *sems)
        rider.finish(ins, outs, *sems)

    return pl.pallas_call(
        body, name=name, in_specs=[ANY] * len(rider.operands), out_specs=[ANY] * len(rider.out_shape),
        out_shape=rider.out_shape,
        scratch_shapes=[pltpu.SemaphoreType.DMA((rider.n_sems,)), pltpu.SemaphoreType.DMA((rider.n_sems,))],
    )(*rider.operands)


class _Sems:
    def __init__(self, sems, base):
        self.sems, self.base = sems, base

    @property
    def at(self):
        return self

    def __getitem__(self, k):
        return self.sems.at[self.base + k]


def _join(*riders):
    riders = [r for r in riders if r is not None]
    if len(riders) < 2:
        return riders[0] if riders else None

    def parts(ins, outs, send_sems, recv_sems):
        i = o = s = 0
        for r in riders:
            ni, no = len(r.operands), len(r.out_shape)
            yield r, ins[i:i + ni], outs[o:o + no], _Sems(send_sems, s), _Sems(recv_sems, s)
            i, o, s = i + ni, o + no, s + r.n_sems

    def start(*refs):
        for r, *args in parts(*refs):
            r.start(*args)

    def finish(*refs):
        for r, *args in parts(*refs):
            r.finish(*args)

    aliases, i, o = {}, 0, 0
    for r in riders:
        aliases.update({i + a: o + b for a, b in r.aliases.items()})
        i, o = i + len(r.operands), o + len(r.out_shape)
    return _Rider([a for r in riders for a in r.operands], [s for r in riders for s in r.out_shape],
                  sum(r.n_sems for r in riders), start, finish, aliases)


def _gather_rider(shard, rows=None, into=None):
    K, Ns = shard.shape
    kh = K // 2
    first_row, n_rows = rows or (0, kh)

    def copies(w_ref, out_ref, send_sems, recv_sems):
        x, y, c = _place()
        chips = [(_flip(x, fx), _flip(y, fy)) for fx, fy in CHIP_FLIPS]

        def half(chip, which):
            return out_ref.at[2 * chip[0] + chip[1], pl.ds(which * kh + first_row, n_rows), :]

        def copy(k, dst, to, src=None):
            return pltpu.make_async_remote_copy(src_ref=dst if src is None else src, dst_ref=dst,
                                                send_sem=send_sems.at[k], recv_sem=recv_sems.at[k],
                                                device_id=to, device_id_type=MESH)

        def first():
            return [copy(j, half((x, y), c), (*chip, c), src=w_ref.at[pl.ds(c * kh + first_row, n_rows), :])
                    for j, chip in enumerate(chips)]

        def onward():
            return [copy(3 + j, half(chip, c), (x, y, 1 - c)) for j, chip in enumerate(chips)]

        def arriving(base, which):
            return [copy(base + j, half(chip, which), (x, y, c)) for j, chip in enumerate(chips)]

        return first, onward, arriving

    def start(ins, outs, send_sems, recv_sems):
        for cp in copies(ins[0], outs[0], send_sems, recv_sems)[0]():
            cp.start()

    def finish(ins, outs, send_sems, recv_sems):
        x, y, c = _place()
        first, onward, arriving = copies(ins[0], outs[0], send_sems, recv_sems)
        passed = onward()
        for arrived, cp in zip(arriving(0, c), passed):
            arrived.wait_recv()
            cp.start()
        for arrived in arriving(3, 1 - c):
            arrived.wait_recv()
        for cp in first() + passed:
            cp.wait_send()

    full = jax.ShapeDtypeStruct((N_CHIPS, K, Ns), shard.dtype)
    if into is None:
        return _Rider([shard], [full], 6, start, finish)
    return _Rider([shard, into], [full], 6, start, finish, aliases={1: 0})


def _pair_rider(g_full):
    Q, K, Ns = g_full.shape
    kh = K // 2

    def copy(g_ref, got_ref, send_sems, recv_sems):
        x, y, c = _place()
        return pltpu.make_async_remote_copy(src_ref=g_ref.at[:, pl.ds((1 - c) * kh, kh), :], dst_ref=got_ref,
                                            send_sem=send_sems.at[0], recv_sem=recv_sems.at[0],
                                            device_id=(x, y, 1 - c), device_id_type=MESH)

    def start(ins, outs, send_sems, recv_sems):
        copy(ins[0], outs[0], send_sems, recv_sems).start()

    def finish(ins, outs, send_sems, recv_sems):
        copy(ins[0], outs[0], send_sems, recv_sems).wait()

    return _Rider([g_full], [jax.ShapeDtypeStruct((Q, kh, Ns), g_full.dtype)], 1, start, finish)


def _chips_rider(pair_sum, rows=None, into=None):
    Q, kh, Ns = pair_sum.shape
    first_row, n_rows = rows or (0, kh)

    def copies(p_ref, got_ref, send_sems, recv_sems):
        x, y, c = _place()
        part = pl.ds(first_row, n_rows)
        out = []
        for j, (fx, fy) in enumerate(CHIP_FLIPS):
            px, py = _flip(x, fx), _flip(y, fy)
            out.append(pltpu.make_async_remote_copy(
                src_ref=p_ref.at[2 * px + py, part, :], dst_ref=got_ref.at[j, part, :], send_sem=send_sems.at[j],
                recv_sem=recv_sems.at[j], device_id=(px, py, c), device_id_type=MESH))
        return out

    def start(ins, outs, send_sems, recv_sems):
        for cp in copies(ins[0], outs[0], send_sems, recv_sems):
            cp.start()

    def finish(ins, outs, send_sems, recv_sems):
        sends = copies(ins[0], outs[0], send_sems, recv_sems)
        for cp in sends:
            cp.wait_recv()
        for cp in sends:
            cp.wait_send()

    got = jax.ShapeDtypeStruct((Q - 1, kh, Ns), pair_sum.dtype)
    if into is None:
        return _Rider([pair_sum], [got], 3, start, finish)
    return _Rider([pair_sum, into], [got], 3, start, finish, aliases={1: 0})


def _mm(a, b, *, grid, a_spec, b_spec, o_spec, o_shape, o_dtype, dims, acc_shape, name, rider=None, vmem_mb=48):
    nk = grid[2]

    def body(a_ref, b_ref, o_ref, *scratch):
        part = _dot(a_ref[...], b_ref[...], dims)
        if nk == 1:
            o_ref[...] = part.astype(o_ref.dtype)
            return
        acc_ref, = scratch
        k = pl.program_id(2)

        @pl.when(k == 0)
        def _():
            acc_ref[...] = part

        @pl.when(k > 0)
        def _():
            acc_ref[...] += part

        @pl.when(k == nk - 1)
        def _():
            o_ref[...] = acc_ref[...].astype(o_ref.dtype)

    (out,), rode = _call(
        body, rider, name=name, grid=grid, in_specs=[a_spec, b_spec], out_specs=[o_spec],
        out_shape=[jax.ShapeDtypeStruct(o_shape, o_dtype)],
        scratch_shapes=[] if nk == 1 else [pltpu.VMEM(acc_shape, F32)],
        compiler_params=_params(("parallel", "parallel", "arbitrary") if rider is None else ("arbitrary",) * 3, vmem_mb),
        operands=(a, b))
    return out if rider is None else (out, rode)


def _mm_nn(a, w, *, tm, tn, tk, name, rider=None, cols=None, o_dtype=F32):
    T, K = a.shape
    Q, _, Ns = w.shape
    nbs = Ns // tn
    tm = min(tm, T)
    j0, j1 = cols or (0, Q * nbs)
    return _mm(a, w, grid=(T // tm, j1 - j0, K // tk),
               a_spec=pl.BlockSpec((tm, tk), lambda i, j, k: (i, k)),
               b_spec=pl.BlockSpec((None, tk, tn), lambda i, j, k: ((j + j0) // nbs, k, (j + j0) % nbs)),
               o_spec=pl.BlockSpec((tm, tn), lambda i, j, k: (i, j)),
               o_shape=(T, (j1 - j0) * tn), o_dtype=o_dtype, dims=NN, acc_shape=(tm, tn), name=name, rider=rider)


def _mm_nt(g, w, *, tm, to, tn, name, rider=None):
    T = g.shape[0]
    Q, K, Ns = w.shape
    nbs = Ns // tn
    tm = min(tm, T)
    return _mm(g, w, grid=(T // tm, K // to, Q * nbs),
               a_spec=pl.BlockSpec((tm, tn), lambda i, j, n: (i, n)),
               b_spec=pl.BlockSpec((None, to, tn), lambda i, j, n: (n // nbs, j, n % nbs)),
               o_spec=pl.BlockSpec((tm, to), lambda i, j, n: (i, j)),
               o_shape=(T, K), o_dtype=F32, dims=NT, acc_shape=(tm, to), name=name, rider=rider)


def _mm_tn(a, g, *, q, tk, tn, tt, name):
    T, K = a.shape
    Ns = g.shape[1] // q
    nbs = Ns // tn
    return _mm(a, g, grid=(K // tk, q * nbs, T // tt),
               a_spec=pl.BlockSpec((tt, tk), lambda i, j, t: (t, i)),
               b_spec=pl.BlockSpec((tt, tn), lambda i, j, t: (t, j)),
               o_spec=pl.BlockSpec((None, tk, tn), lambda i, j, t: (j // nbs, i, j % nbs)),
               o_shape=(q, K, Ns), o_dtype=BF16, dims=TN, acc_shape=(tk, tn), name=name)


def _ln(u):
    mu = jnp.mean(u, axis=-1, keepdims=True)
    d = u - mu
    r = lax.rsqrt(jnp.mean(d * d, axis=-1, keepdims=True) + EPS)
    return d * r, r


def _ln_bwd(dy, un, r):
    return r * (dy - jnp.mean(dy, axis=-1, keepdims=True) - un * jnp.mean(dy * un, axis=-1, keepdims=True))


def _colsum(v):
    return jnp.sum(v, axis=0, keepdims=True)


def _rowwise(name, fn, bigs, vecs, out_dtypes, n_acc, tm=128, rider=None):
    T, D = bigs[0].shape
    nb, nv, no = len(bigs), len(vecs), len(out_dtypes)

    def body(*refs):
        outs, accs = fn([r[...] for r in refs[:nb]], [r[...] for r in refs[nb:nb + nv]])
        for r, o in zip(refs[nb + nv:nb + nv + no], outs):
            r[...] = o.astype(r.dtype)
        if n_acc:
            acc_ref = refs[nb + nv + no]

            @pl.when(pl.program_id(0) == 0)
            def _():
                acc_ref[...] = jnp.zeros_like(acc_ref)

            for row, a in enumerate(accs):
                acc_ref[row:row + 1, :] += a

    big_spec = pl.BlockSpec((tm, D), lambda i: (i, 0))
    vec_spec = pl.BlockSpec((1, D), lambda i: (0, 0))
    out_shape = [jax.ShapeDtypeStruct((T, D), dt) for dt in out_dtypes]
    out_specs = [big_spec] * no
    if n_acc:
        out_shape.append(jax.ShapeDtypeStruct((8, D), F32))
        out_specs.append(pl.BlockSpec((8, D), lambda i: (0, 0)))
    outs, rode = _call(
        body, rider, name=name, grid=(T // tm,), in_specs=[big_spec] * nb + [vec_spec] * nv,
        out_specs=out_specs, out_shape=out_shape, scratch_shapes=[],
        compiler_params=_params(("arbitrary",), 48), operands=(*bigs, *vecs))
    return outs if rider is None else (outs, rode)


def _pre_mixer(x, scale1, shift1):
    def fn(b, v):
        xn, _ = _ln(b[0])
        return [xn * (1.0 + v[0]) + v[1]], []
    return _rowwise("pre_mixer", fn, [x], [scale1, shift1], [BF16], 0)[0]


def _post_mixer(mix, x, gate1, g1, b1, scale2, shift2):
    def fn(b, v):
        un1, _ = _ln(ALPHA * b[1] + v[0] * b[0])
        x1 = un1 * v[1] + v[2]
        xn1, _ = _ln(x1)
        return [x1, xn1 * (1.0 + v[3]) + v[4]], []
    return _rowwise("post_mixer", fn, [mix, x], [gate1, g1, b1, scale2, shift2], [F32, BF16], 0)


def _loss_head(f, x1, tgt, gate2, g2, b2):
    def fn(b, v):
        ff, xx, tt = b
        d_model = ff.shape[-1]
        un2, r2 = _ln(ALPHA * xx + v[0] * ff)
        err = un2 * v[1] + v[2] - tt
        dy = err * (1.0 / d_model)
        du2 = _ln_bwd(dy * v[1], un2, r2)
        return [du2, du2 * v[0]], [_colsum(dy * un2), _colsum(dy), _colsum(du2 * ff), _colsum(err * err)]
    return _rowwise("loss_head", fn, [f, x1, tgt], [gate2, g2, b2], [F32, BF16], 4)


def _mid_bwd(dh2, du2, x1, mix, x, gate1, g1, scale2, rider=None):
    def fn(b, v):
        dh, du, xx1, mm, xx = b
        xn1, r1n = _ln(xx1)
        dx1 = ALPHA * du + _ln_bwd(dh * (1.0 + v[2]), xn1, r1n)
        un1, r1 = _ln(ALPHA * xx + v[0] * mm)
        du1 = _ln_bwd(dx1 * v[1], un1, r1)
        return [du1, du1 * v[0]], [_colsum(dh * xn1), _colsum(dh), _colsum(dx1 * un1), _colsum(dx1),
                                   _colsum(du1 * mm)]
    return _rowwise("mid_bwd", fn, [dh2, du2, x1, mix, x], [gate1, g1, scale2], [F32, BF16], 5, rider=rider)


def _first_bwd(dh1, du1, x, scale1):
    def fn(b, v):
        dh, du, xx = b
        xn, r0 = _ln(xx)
        return [ALPHA * du + _ln_bwd(dh * (1.0 + v[0]), xn, r0)], [_colsum(dh * xn), _colsum(dh)]
    return _rowwise("first_bwd", fn, [dh1, du1, x], [scale1], [F32], 2)


def _swiglu_fwd(ff, tm=256, tf=512):
    T, F2 = ff.shape
    F = F2 // 2
    nf = F // tf

    def body(g_ref, u_ref, a_ref):
        g = g_ref[...]
        a_ref[...] = (g * jax.nn.sigmoid(g) * u_ref[...]).astype(a_ref.dtype)

    return pl.pallas_call(
        body, name="swiglu_fwd", grid=(T // tm, nf),
        in_specs=[pl.BlockSpec((tm, tf), lambda i, j: (i, j)), pl.BlockSpec((tm, tf), lambda i, j: (i, j + nf))],
        out_specs=pl.BlockSpec((tm, tf), lambda i, j: (i, j)),
        out_shape=jax.ShapeDtypeStruct((T, F), BF16),
        compiler_params=_params(("parallel", "parallel")),
    )(ff, ff)


def _swiglu_bwd(da, ff, rider=None, tm=256, tf=512):
    T, F = da.shape
    nf = F // tf

    def body(da_ref, g_ref, u_ref, dg_ref, du_ref):
        g = g_ref[...]
        d = da_ref[...]
        s = jax.nn.sigmoid(g)
        du_ref[...] = (d * g * s).astype(du_ref.dtype)
        dg_ref[...] = (d * u_ref[...] * s * (1.0 + g * (1.0 - s))).astype(dg_ref.dtype)

    blk = pl.BlockSpec((tm, tf), lambda i, j: (i, j))
    return _call(
        body, rider, name="swiglu_bwd", grid=(T // tm, nf),
        in_specs=[blk, blk, pl.BlockSpec((tm, tf), lambda i, j: (i, j + nf))],
        out_specs=[blk, blk], out_shape=[jax.ShapeDtypeStruct((T, F), BF16)] * 2, scratch_shapes=[],
        compiler_params=_params(("arbitrary", "arbitrary")), operands=(da, ff, ff))


PAIR = 2


def _attn_probs(q_ref, k_ref, bias_ref, e, step):
    start = pl.multiple_of(step * QROWS, QROWS)
    lanes = pl.ds(e * HD_A, HD_A)
    s = _dot(q_ref[:, lanes], k_ref[pl.ds(start, UNION), lanes], NT) * (HD_A ** -0.5) + bias_ref[e]
    col = lax.broadcasted_iota(jnp.int32, s.shape, 1)
    s = jnp.where(col + start >= KPAD, s, NEG)
    p = jnp.exp(s - jnp.max(s, axis=-1, keepdims=True))
    return p / jnp.sum(p, axis=-1, keepdims=True), start


def _attn_specs(T, n_pairs):
    wide = PAIR * HD_A
    per_step = pl.BlockSpec((QROWS, wide), lambda hp, n: (n, hp))
    keys = pl.BlockSpec((KPAD + T, wide), lambda hp, n: (0, hp))
    values = pl.BlockSpec((KPAD + T, wide), lambda hp, n: (0, n_pairs + hp))
    table = pl.BlockSpec((PAIR, QROWS, UNION), lambda hp, n: (hp, 0, 0))
    vec = pl.BlockSpec((1, wide), lambda hp, n: (0, hp))
    return per_step, keys, values, table, vec


def _attn_fwd(qkv, kv, bias, gain, rider=None):
    T = qkv.shape[0]
    W = gain.shape[1]
    n_pairs = W // (PAIR * HD_A)

    def body(q_ref, k_ref, v_ref, bias_ref, gain_ref, o_ref):
        for e in range(PAIR):
            lanes = pl.ds(e * HD_A, HD_A)
            p, start = _attn_probs(q_ref, k_ref, bias_ref, e, pl.program_id(1))
            o = _dot(p.astype(BF16), v_ref[pl.ds(start, UNION), lanes])
            rr = lax.rsqrt(jnp.mean(o * o, axis=-1, keepdims=True) + EPS)
            o_ref[:, lanes] = (o * rr * gain_ref[:, lanes]).astype(o_ref.dtype)

    per_step, keys, values, table, vec = _attn_specs(T, n_pairs)
    return _call(
        body, rider, name="attn_fwd", grid=(n_pairs, T // QROWS), in_specs=[per_step, keys, values, table, vec],
        out_specs=[per_step], out_shape=[jax.ShapeDtypeStruct((T, W), BF16)], scratch_shapes=[],
        compiler_params=_params(("arbitrary", "arbitrary"), 40), operands=(qkv, kv, kv, bias, gain))


def _attn_bwd(qkv, kv, bias, gain, dmixin, rider=None):
    T = qkv.shape[0]
    W = gain.shape[1]
    n_pairs = W // (PAIR * HD_A)
    scale = HD_A ** -0.5

    def body(q_ref, k_ref, v_ref, bias_ref, gain_ref, don_ref, dq_ref, dk_ref, dv_ref, dbias_ref, dgain_ref):
        n = pl.program_id(1)

        @pl.when(n == 0)
        def _():
            dk_ref[...] = jnp.zeros_like(dk_ref)
            dv_ref[...] = jnp.zeros_like(dv_ref)
            dbias_ref[...] = jnp.zeros_like(dbias_ref)
            dgain_ref[...] = jnp.zeros_like(dgain_ref)

        for e in range(PAIR):
            lanes = pl.ds(e * HD_A, HD_A)
            p, start = _attn_probs(q_ref, k_ref, bias_ref, e, n)
            keys = pl.ds(start, UNION)
            pb = p.astype(BF16)
            vb = v_ref[keys, lanes]
            o = _dot(pb, vb)
            rr = lax.rsqrt(jnp.mean(o * o, axis=-1, keepdims=True) + EPS)
            on = o * rr
            d_on = don_ref[:, lanes]
            dgain_ref[:, lanes] += _colsum(d_on * on)
            dyo = d_on * gain_ref[:, lanes]
            do = rr * (dyo - on * jnp.mean(dyo * on, axis=-1, keepdims=True))
            dob = do.astype(BF16)
            dp = _dot(dob, vb, NT)
            ds = p * (dp - jnp.sum(do * o, axis=-1, keepdims=True))
            dbias_ref[e] += ds
            dsb = ds.astype(BF16)
            dq_ref[:, lanes] = (_dot(dsb, k_ref[keys, lanes]) * scale).astype(dq_ref.dtype)
            dk_ref[keys, lanes] += _dot(dsb, q_ref[:, lanes], TN) * scale
            dv_ref[keys, lanes] += _dot(pb, dob, TN)

    per_step, keys, values, table, vec = _attn_specs(T, n_pairs)
    H = n_pairs * PAIR
    return _call(
        body, rider, name="attn_bwd", grid=(n_pairs, T // QROWS),
        in_specs=[per_step, keys, values, table, vec, per_step],
        out_specs=[per_step, keys, keys, table, vec],
        out_shape=[jax.ShapeDtypeStruct((T, W), BF16), jax.ShapeDtypeStruct((KPAD + T, W), F32),
                   jax.ShapeDtypeStruct((KPAD + T, W), F32), jax.ShapeDtypeStruct((H, QROWS, UNION), F32),
                   jax.ShapeDtypeStruct((1, W), F32)],
        scratch_shapes=[], compiler_params=_params(("arbitrary", "arbitrary"), 40),
        operands=(qkv, kv, kv, bias, gain, dmixin))


N_DIAG = QROWS + UNION - 1


def _bias_table(rel_bias):
    H = rel_bias.shape[0]
    idx = np.clip(UNION - 1 - np.arange(N_DIAG), -MAX_REL, MAX_REL) + MAX_REL
    rolled = rel_bias[:, idx[(np.arange(N_DIAG) + QROWS - 1) % N_DIAG]]
    flat = jnp.broadcast_to(rolled[:, None, :], (H, QROWS, N_DIAG)).reshape(H, QROWS * N_DIAG)
    toep = flat[:, :QROWS * (N_DIAG - 1)].reshape(H, QROWS, N_DIAG - 1)[:, :, :UNION]
    g = np.arange(QROWS)[:, None] // CHUNK
    m = np.arange(UNION)[None, :] // CHUNK
    return jnp.where((m >= g) & (m <= g + N_PAST), toep, NEG)


def _bias_table_grad(dbias):
    H = dbias.shape[0]
    skew = jnp.pad(dbias, ((0, 0), (0, 0), (QROWS - 1, 0))).reshape(H, QROWS * N_DIAG)
    skew = jnp.pad(skew, ((0, 0), (0, QROWS))).reshape(H, QROWS, N_DIAG + 1)
    diag = jnp.sum(skew, axis=1)[:, :N_DIAG]
    n_far = UNION - MAX_REL
    far = jnp.sum(diag[:, :n_far], axis=1, keepdims=True)
    near = diag[:, n_far:][:, ::-1]
    zeros = jnp.zeros((H, MAX_REL - (QROWS - 1)), F32)
    return jnp.concatenate([zeros, near, far], axis=1)


def _tri(n, lower):
    r = lax.broadcasted_iota(jnp.int32, (n, n), 0)
    c = lax.broadcasted_iota(jnp.int32, (n, n), 1)
    return jnp.where((c <= r) if lower else (c >= r), 1.0, 0.0).astype(F32)


def _hgrn_gates(zq_ref, zf_ref, lbl_ref, q_s, k_s, b_s):
    lb = jax.nn.sigmoid(lbl_ref[0:1, :] - lbl_ref[1:2, :])
    zq = zq_ref[...]
    sig = jax.nn.sigmoid(zf_ref[...])
    f = lb + (1.0 - lb) * sig
    sq = jax.nn.sigmoid(zq)
    q_s[...] = zq * sq
    k_s[...] = 1.0 - f
    b_s[...] = _dot(_tri(CHUNK, True), jnp.log(f), precision=HIGHEST)
    return lb, sig, f, sq


def _sub_rows(i):
    return pl.ds(i * SUB, SUB)


def _row_mask(s):
    return lax.broadcasted_iota(jnp.int32, (SUB, HD_B), 0) >= s


def _decay_from(b_sub, b_row, s):
    return jnp.where(_row_mask(s), jnp.exp(jnp.minimum(b_sub - b_row, 0.0)), 0.0)


def _hgrn_fwd(proj, lb_logits, gnorm_g, rider=None):
    T = proj.shape[0]
    nC = T // CHUNK
    W = lb_logits.shape[1]
    G = W // HD_B // HGRN_HEADS
    col0 = (proj.shape[1] - 4 * W) // (HD_B * HGRN_HEADS)
    wide = HGRN_HEADS * HD_B

    def body(*refs):
        @pl.when(pl.program_id(1) == 0)
        def _():
            refs[9][...] = jnp.zeros_like(refs[9])

        for h in range(HGRN_HEADS):
            lanes = pl.ds(h * HD_B, HD_B)
            one_head(*[r.at[:, lanes] for r in refs[:5]], refs[5], *[r.at[:, lanes] for r in refs[6:8]],
                     *[r.at[h] for r in refs[8:]])

    def one_head(zq_ref, zf_ref, xi_ref, zg_ref, lbl_ref, gn_ref, mix_ref, o_ref, stall_ref, st_ref, q_s, k_s, b_s, acc_s):
        _hgrn_gates(zq_ref, zf_ref, lbl_ref, q_s, k_s, b_s)
        q, k, b = q_s[...], k_s[...], b_s[...]
        st = st_ref[...]
        stall_ref[...] = st
        b_last = b_s[CHUNK - 1:CHUNK, :]
        acc_s[...] = _dot((q * jnp.exp(b)).astype(BF16), st.astype(BF16), NT)
        for i in range(CHUNK // SUB):
            rows = _sub_rows(i)
            q_i, b_i = q_s[rows, :], b_s[rows, :]
            acc = jnp.zeros((SUB, HD_B), F32)
            if i:
                past = pl.ds(0, i * SUB)
                b_ref = b_s[i * SUB - 1:i * SUB, :]
                qs = (q_i * jnp.exp(b_i - b_ref)).astype(BF16)
                ks = (k_s[past, :] * jnp.exp(b_ref - b_s[past, :])).astype(BF16)
                acc += _dot(_dot(qs, ks, NT).astype(BF16), xi_ref[past, :].astype(BF16))
            for s in range(SUB):
                row = pl.ds(i * SUB + s, 1)
                w = q_i * _decay_from(b_i, b_s[row, :], s)
                acc += jnp.sum(w * k_s[row, :], axis=-1, keepdims=True) * xi_ref[row, :]
            acc_s[rows, :] += acc
        o = acc_s[...]
        kd = (k * jnp.exp(b_last - b)).astype(BF16)
        st_ref[...] = st * jnp.exp(b_last) + _dot(xi_ref[...].astype(BF16), kd, TN)
        o_ref[...] = o
        zg = zg_ref[...]
        rr = lax.rsqrt(jnp.mean(o * o, axis=-1, keepdims=True) + EPS)
        mix_ref[...] = (o * rr * gn_ref[...] * (zg * jax.nn.sigmoid(zg))).astype(mix_ref.dtype)

    col = lambda part: pl.BlockSpec((CHUNK, wide), lambda g, n: (n, col0 + part * G + g))
    out_blk = pl.BlockSpec((CHUNK, wide), lambda g, n: (n, g))
    tile = pltpu.VMEM((HGRN_HEADS, CHUNK, HD_B), F32)
    return _call(
        body, rider, name="hgrn_fwd", grid=(G, nC),
        in_specs=[col(0), col(1), col(2), col(3), pl.BlockSpec((2, wide), lambda g, n: (0, g)),
                  pl.BlockSpec((1, HD_B), lambda g, n: (0, 0))],
        out_specs=[out_blk, out_blk, pl.BlockSpec((HGRN_HEADS, None, HD_B, HD_B), lambda g, n: (g, n, 0, 0))],
        out_shape=[jax.ShapeDtypeStruct((T, W), BF16), jax.ShapeDtypeStruct((T, W), F32),
                   jax.ShapeDtypeStruct((G * HGRN_HEADS, nC, HD_B, HD_B), F32)],
        scratch_shapes=[pltpu.VMEM((HGRN_HEADS, HD_B, HD_B), F32), tile, tile, tile, tile],
        compiler_params=_params(("arbitrary", "arbitrary")),
        operands=(proj, proj, proj, proj, lb_logits, gnorm_g))


def _hgrn_bwd(proj, lb_logits, gnorm_g, o_b, st_all, dmixin, rider=None):
    T = proj.shape[0]
    nC = T // CHUNK
    W = lb_logits.shape[1]
    G = W // HD_B // HGRN_HEADS
    wide = HGRN_HEADS * HD_B
    col0 = (proj.shape[1] - 4 * W) // wide
    dcol0 = (dmixin.shape[1] - W) // wide

    def body(*refs):
        g, n = pl.program_id(0), pl.program_id(1)
        dl0_ref, dgn_ref, dst_ref = refs[13:16]

        @pl.when(n == 0)
        def _():
            dst_ref[...] = jnp.zeros_like(dst_ref)
            dl0_ref[...] = jnp.zeros_like(dl0_ref)

        @pl.when((n == 0) & (g == 0))
        def _():
            dgn_ref[...] = jnp.zeros_like(dgn_ref)

        for h in range(HGRN_HEADS):
            lanes = pl.ds(h * HD_B, HD_B)
            cut = lambda r: r.at[:, lanes]
            one_head(*[cut(r) for r in refs[:5]], refs[5], cut(refs[6]), refs[7].at[h], cut(refs[8]),
                     *[cut(r) for r in refs[9:14]], dgn_ref, *[r.at[h] for r in refs[15:]])

    def one_head(zq_ref, zf_ref, xi_ref, zg_ref, lbl_ref, gn_ref, o_ref, st_ref, dout_ref,
                 dzq_ref, dzf_ref, dxi_ref, dzg_ref, dl0_ref, dgn_ref, dst_ref, q_s, k_s, b_s, do_s, dq_s, dk_s, di_s):
        lb, sig, f, sq = _hgrn_gates(zq_ref, zf_ref, lbl_ref, q_s, k_s, b_s)
        q, k, b = q_s[...], k_s[...], b_s[...]
        zg, o, dout = zg_ref[...], o_ref[...], dout_ref[...]
        sg = jax.nn.sigmoid(zg)
        rr = lax.rsqrt(jnp.mean(o * o, axis=-1, keepdims=True) + EPS)
        on = o * rr
        gn = gn_ref[...]
        dzg_ref[...] = (dout * on * gn * sg * (1.0 + zg * (1.0 - sg))).astype(dzg_ref.dtype)
        d_on = dout * zg * sg
        dgn_ref[...] += _colsum(d_on * on)
        d_on = d_on * gn
        do = rr * (d_on - on * jnp.mean(d_on * on, axis=-1, keepdims=True))
        do_s[...] = do
        dob = do.astype(BF16)
        st, dst = st_ref[...], dst_ref[...]
        b_last = b_s[CHUNK - 1:CHUNK, :]
        eb, e_last, k_dec = jnp.exp(b), jnp.exp(b_last), jnp.exp(b_last - b)
        qt, kd = q * eb, k * k_dec
        dstb = dst.astype(BF16)
        xib = xi_ref[...].astype(BF16)
        d_kd = _dot(xib, dstb)
        dq_s[...] = _dot(dob, st.astype(BF16)) * eb
        dk_s[...] = d_kd * k_dec
        di_s[...] = _dot(kd.astype(BF16), dstb, NT)
        d_b_last = e_last * _colsum(st * dst) + _colsum(d_kd * kd)
        dst_ref[...] = _dot(dob, qt.astype(BF16), TN) + dst * e_last
        for i in range(CHUNK // SUB):
            rows = _sub_rows(i)
            q_i, b_i, do_i = q_s[rows, :], b_s[rows, :], do_s[rows, :]
            dq_i = jnp.zeros((SUB, HD_B), F32)
            if i:
                past = pl.ds(0, i * SUB)
                b_ref = b_s[i * SUB - 1:i * SUB, :]
                e_q, e_k = jnp.exp(b_i - b_ref), jnp.exp(b_ref - b_s[past, :])
                qs, ks = (q_i * e_q).astype(BF16), (k_s[past, :] * e_k).astype(BF16)
                xi_p, do_b = xi_ref[past, :].astype(BF16), do_i.astype(BF16)
                di_s[past, :] += _dot(_dot(ks, qs, NT).astype(BF16), do_b)
                dq_i += _dot(_dot(do_b, xi_p, NT).astype(BF16), ks) * e_q
                dk_s[past, :] += _dot(_dot(xi_p, do_b, NT).astype(BF16), qs) * e_k
            for s in range(SUB):
                row = pl.ds(i * SUB + s, 1)
                k_row, i_row = k_s[row, :], xi_ref[row, :]
                e = _decay_from(b_i, b_s[row, :], s)
                w = q_i * e
                a_col = jnp.sum(w * k_row, axis=-1, keepdims=True)
                da_col = jnp.sum(do_i * i_row, axis=-1, keepdims=True)
                di_s[row, :] += _colsum(a_col * do_i)
                dq_i += da_col * e * k_row
                dk_s[row, :] += _colsum(da_col * w)
            dq_s[rows, :] += dq_i
        dq, dk = dq_s[...], dk_s[...]
        db = q * dq - k * dk
        is_last = lax.broadcasted_iota(jnp.int32, (CHUNK, HD_B), 0) == CHUNK - 1
        db = db + jnp.where(is_last, d_b_last, 0.0)
        df = _dot(_tri(CHUNK, False), db, precision=HIGHEST) / f - dk
        dzf_ref[...] = (df * (1.0 - lb) * sig * (1.0 - sig)).astype(dzf_ref.dtype)
        dl0_ref[...] += _colsum(df * (1.0 - sig)) * (lb * (1.0 - lb))
        zq = zq_ref[...]
        dzq_ref[...] = (dq * sq * (1.0 + zq * (1.0 - sq))).astype(dzq_ref.dtype)
        dxi_ref[...] = di_s[...].astype(dxi_ref.dtype)

    rev = lambda n: nC - 1 - n
    col = lambda part: pl.BlockSpec((CHUNK, wide), lambda g, n: (rev(n), col0 + part * G + g))
    blk = pl.BlockSpec((CHUNK, wide), lambda g, n: (rev(n), g))
    tile = pltpu.VMEM((HGRN_HEADS, CHUNK, HD_B), F32)
    out_big = jax.ShapeDtypeStruct((T, W), BF16)
    return _call(
        body, rider, name="hgrn_bwd", grid=(G, nC),
        in_specs=[col(0), col(1), col(2), col(3), pl.BlockSpec((2, wide), lambda g, n: (0, g)),
                  pl.BlockSpec((1, HD_B), lambda g, n: (0, 0)), blk,
                  pl.BlockSpec((HGRN_HEADS, None, HD_B, HD_B), lambda g, n: (g, rev(n), 0, 0)),
                  pl.BlockSpec((CHUNK, wide), lambda g, n: (rev(n), dcol0 + g))],
        out_specs=[blk, blk, blk, blk, pl.BlockSpec((1, wide), lambda g, n: (0, g)),
                   pl.BlockSpec((1, HD_B), lambda g, n: (0, 0))],
        out_shape=[out_big, out_big, out_big, out_big, jax.ShapeDtypeStruct((1, W), F32),
                   jax.ShapeDtypeStruct((1, HD_B), F32)],
        scratch_shapes=[pltpu.VMEM((HGRN_HEADS, HD_B, HD_B), F32)] + [tile] * 7,
        compiler_params=_params(("arbitrary", "arbitrary")),
        operands=(proj, proj, proj, proj, lb_logits, gnorm_g, o_b, st_all, dmixin))


def _adamw_math(g, w, m, v):
    m = B1 * m + (1.0 - B1) * g
    v = B2 * v + (1.0 - B2) * (g * g)
    m_hat = m / (1.0 - B1 ** STEP)
    v_hat = v / (1.0 - B2 ** STEP)
    return -LR * (m_hat / (jnp.sqrt(v_hat) + ADAM_EPS) + WD * w), m, v


def _adamw(g, w, m, v, name, rider=None):
    R, C = g.shape
    tr = _row_tile(R, C)

    def body(g_ref, w_ref, m_ref, v_ref, d_ref, mo_ref, vo_ref):
        d_ref[...], mo_ref[...], vo_ref[...] = _adamw_math(g_ref[...], w_ref[...], m_ref[...], v_ref[...])

    blk = pl.BlockSpec((tr, C), lambda i: (i, 0))
    return _call(
        body, rider, name=name, grid=(R // tr,), in_specs=[blk] * 4, out_specs=[blk] * 3,
        out_shape=[jax.ShapeDtypeStruct((R, C), F32)] * 3, scratch_shapes=[],
        compiler_params=_params(("arbitrary",), 40), operands=(g, w, m, v))


def _sum_pair(g_full, from_sibling, sel, name):
    Q, K, Ns = g_full.shape
    kh = K // 2
    tr = _row_tile(kh, Ns)
    nh = kh // tr

    def body(sel_ref, a_ref, b_ref, o_ref):
        o_ref[...] = (a_ref[...].astype(F32) + b_ref[...].astype(F32)).astype(o_ref.dtype)

    return pl.pallas_call(
        body, name=name,
        grid_spec=pltpu.PrefetchScalarGridSpec(
            num_scalar_prefetch=1, grid=(Q, nh),
            in_specs=[pl.BlockSpec((None, tr, Ns), lambda q, i, sel: (q, sel[1] * nh + i, 0)),
                      pl.BlockSpec((None, tr, Ns), lambda q, i, sel: (q, i, 0))],
            out_specs=pl.BlockSpec((None, tr, Ns), lambda q, i, sel: (q, i, 0))),
        out_shape=jax.ShapeDtypeStruct((Q, kh, Ns), BF16), compiler_params=_params(("parallel", "parallel")),
    )(sel, g_full, from_sibling)


def _sum_chips(pair_sum, from_chips, sel, name):
    Q, kh, Ns = pair_sum.shape
    tr = _row_tile(kh, Ns)
    nh = kh // tr

    def body(sel_ref, a_ref, b0_ref, b1_ref, b2_ref, o_ref):
        up = lambda r: r[...].astype(F32)
        o_ref[...] = ((up(a_ref) + up(b0_ref)) + up(b1_ref)) + up(b2_ref)

    recv = lambda k: pl.BlockSpec((None, tr, Ns), lambda i, sel: (k, i, 0))
    return pl.pallas_call(
        body, name=name,
        grid_spec=pltpu.PrefetchScalarGridSpec(
            num_scalar_prefetch=1, grid=(nh,),
            in_specs=[pl.BlockSpec((None, tr, Ns), lambda i, sel: (sel[0], i, 0)), recv(0), recv(1), recv(2)],
            out_specs=pl.BlockSpec((tr, Ns), lambda i, sel: (sel[1] * nh + i, 0))),
        out_shape=jax.ShapeDtypeStruct((2 * kh, Ns), F32), compiler_params=_params(("parallel",)),
    )(sel, pair_sum, from_chips, from_chips, from_chips)


def _gather_small(v, name):
    R, L = v.shape

    def body(v_ref, out_ref, send_sems, recv_sems):
        x, y, c = _place()
        me = 4 * x + 2 * y + c
        out_ref[me] = v_ref[...]
        peers = [(_flip(x, k >> 2 & 1), _flip(y, k >> 1 & 1), _flip(c, k & 1)) for k in range(1, N_DEV)]

        def copy(k, row, to):
            return pltpu.make_async_remote_copy(src_ref=v_ref, dst_ref=out_ref.at[row], send_sem=send_sems.at[k],
                                                recv_sem=recv_sems.at[k], device_id=to, device_id_type=MESH)

        sends = [copy(k, me, peer) for k, peer in enumerate(peers)]
        for cp in sends:
            cp.start()
        for k, (px, py, pc) in enumerate(peers):
            copy(k, 4 * px + 2 * py + pc, (x, y, c)).wait_recv()
        for cp in sends:
            cp.wait_send()

    vmem = pl.BlockSpec(memory_space=pltpu.VMEM)
    return pl.pallas_call(
        body, name=name, in_specs=[vmem], out_specs=vmem, out_shape=jax.ShapeDtypeStruct((N_DEV, R, L), F32),
        scratch_shapes=[pltpu.SemaphoreType.DMA((N_DEV - 1,)), pltpu.SemaphoreType.DMA((N_DEV - 1,))],
    )(v)


def _share_halves(block, name):
    K, Ns = block.shape
    kh = K // 2

    def body(_, out_ref, send_sem, recv_sem):
        x, y, c = _place()
        mine, theirs = out_ref.at[pl.ds(c * kh, kh), :], out_ref.at[pl.ds((1 - c) * kh, kh), :]
        cp = pltpu.make_async_remote_copy(src_ref=mine, dst_ref=mine, send_sem=send_sem, recv_sem=recv_sem,
                                          device_id=(x, y, 1 - c), device_id_type=MESH)
        cp.start()
        pltpu.make_async_remote_copy(src_ref=theirs, dst_ref=theirs, send_sem=send_sem, recv_sem=recv_sem,
                                     device_id=(x, y, 1 - c), device_id_type=MESH).wait_recv()
        cp.wait_send()

    return pl.pallas_call(
        body, name=name, in_specs=[ANY], out_specs=ANY, out_shape=jax.ShapeDtypeStruct((K, Ns), block.dtype),
        input_output_aliases={0: 0},
        scratch_shapes=[pltpu.SemaphoreType.DMA, pltpu.SemaphoreType.DMA],
    )(block)


def _silu(v):
    return v * jax.nn.sigmoid(v)


def _ada_fwd(c_all, w_ada, tn=512):
    M, D = c_all.shape
    Ns = w_ada.shape[1]

    def body(c_ref, w_ref, o_ref):
        o_ref[...] = _dot(_silu(c_ref[...]).astype(BF16), w_ref[...].astype(BF16))

    return pl.pallas_call(
        body, name="ada_fwd", grid=(Ns // tn,),
        in_specs=[pl.BlockSpec((M, D), lambda j: (0, 0)), pl.BlockSpec((D, tn), lambda j: (0, j))],
        out_specs=pl.BlockSpec((M, tn), lambda j: (0, j)), out_shape=jax.ShapeDtypeStruct((M, Ns), F32),
        compiler_params=_params(("parallel",)),
    )(c_all, w_ada)


def _ada_bwd(c_all, dmod, w, m, v, rider=None, tk=256, tn=1536):
    M, D = c_all.shape
    Ns = dmod.shape[1]

    def body(c_ref, d_ref, w_ref, m_ref, v_ref, g_ref, dl_ref, mo_ref, vo_ref):
        g = _dot(_silu(c_ref[...]).astype(BF16), d_ref[...].astype(BF16), TN)
        g_ref[...] = g
        dl_ref[...], mo_ref[...], vo_ref[...] = _adamw_math(g, w_ref[...], m_ref[...], v_ref[...])

    blk = pl.BlockSpec((tk, tn), lambda i, j: (i, j))
    return _call(
        body, rider, name="ada_bwd", grid=(D // tk, Ns // tn),
        in_specs=[pl.BlockSpec((M, tk), lambda i, j: (0, i)), pl.BlockSpec((M, tn), lambda i, j: (0, j)), blk, blk, blk],
        out_specs=[blk] * 4, out_shape=[jax.ShapeDtypeStruct((D, Ns), F32)] * 4, scratch_shapes=[],
        compiler_params=_params(("arbitrary", "arbitrary"), 40), operands=(c_all, dmod, w, m, v))


def _small_update(g_all, w, m, v):
    R, L = w.shape

    def body(g_ref, w_ref, m_ref, v_ref, go_ref, d_ref, mo_ref, vo_ref):
        g = g_ref[0]
        for d in range(1, N_DEV):
            g = g + g_ref[d]
        go_ref[...] = g
        d_ref[...], mo_ref[...], vo_ref[...] = _adamw_math(g, w_ref[...], m_ref[...], v_ref[...])

    return pl.pallas_call(body, name="small_update", out_shape=[jax.ShapeDtypeStruct((R, L), F32)] * 4)(g_all, w, m, v)


def _pack(parts, rows):
    flat = jnp.concatenate([p.reshape(-1) for p in parts])
    return jnp.pad(flat, (0, rows * 128 - flat.shape[0])).reshape(rows, 128)


def _unpack(packed, shapes):
    flat, out, at = packed.reshape(-1), [], 0
    for shp in shapes:
        size = 1
        for d in shp:
            size *= d
        out.append(flat[at:at + size].reshape(shp))
        at += size
    return out


def _layer(x, tgt, mod, wts, rel_bias, attn_norm_g, lb_logits, gnorm_g, ln1_g, ln1_b, ln2_g, ln2_b, place=None):
    T, D = x.shape
    aw = attn_norm_g.shape[1]
    shift1, scale1, gate1, shift2, scale2, gate2 = [mod[i:i + 1] for i in range(6)]

    def gather(n, rows=None, into=None):
        return None if place is None else _gather_rider(wts[n], rows, None if into is None else into[0])

    def gathered(n, rode):
        return wts[n] if place is None else lax.dynamic_update_index_in_dim(rode[0], wts[n], place[0], 0)

    def blocks(g):
        return g.reshape(N_CHIPS, -1, g.shape[2])

    def to_sibling(g):
        return None if place is None else _pair_rider(g)

    def pair_sum(n, g, rode=None):
        if place is None:
            return g
        rode = _alone(_pair_rider(g), n + "_send_pair") if rode is None else rode
        return _sum_pair(g, rode[0], place[1], n + "_sum_pair")

    def to_chips(p, rows=None, into=None):
        return None if place is None else _chips_rider(p, rows, None if into is None else into[0])

    def reduced(n, p, rode):
        return p if place is None else _share_halves(_sum_chips(p, rode[0], place[1], n + "_sum_chips"), n + "_share")

    def carrying(mm, *args, rider, **kw):
        return mm(*args, rider=rider, **kw) if rider is not None else (mm(*args, **kw), None)

    w_in = gathered("w_in", None if place is None else _alone(gather("w_in"), "gather_w_in"))
    h1 = _pre_mixer(x, scale1, shift1)
    n_qkv = 3 * aw // 256
    qkv, rode = carrying(_mm_nn, h1, w_in, tm=1024, tn=256, tk=D, name="proj_qkv", cols=(0, n_qkv), o_dtype=BF16,
                         rider=gather("w_o"))
    w_o3 = gathered("w_o", rode).reshape(1, D, D)
    proj = _mm_nn(h1, w_in, tm=1024, tn=256, tk=D, name="proj_rec", cols=(n_qkv, N_CHIPS * w_in.shape[2] // 256))
    kv = jnp.pad(qkv[:, aw:], ((KPAD, 0), (0, 0)))
    bias = _bias_table(rel_bias)
    cut = 7 * (wts["w_ffn_in"].shape[-2] // 2) // 16
    (mix_a,), rode = _attn_fwd(qkv, kv, bias, attn_norm_g, rider=gather("w_ffn_in", (0, cut)))
    (mix_b, o_b, st_all), rode = _hgrn_fwd(
        proj, lb_logits, gnorm_g, rider=gather("w_ffn_in", (cut, wts["w_ffn_in"].shape[-2] // 2 - cut), rode))
    w_ffn_in = gathered("w_ffn_in", rode)
    mixin = jnp.concatenate([mix_a, mix_b], axis=1)
    mix = _mm_nn(mixin, w_o3, tm=1024, tn=512, tk=D, name="mix_out")
    x1, h2 = _post_mixer(mix, x, gate1, ln1_g, ln1_b, scale2, shift2)
    ff, rode = carrying(_mm_nn, h2, w_ffn_in, tm=1024, tn=256, tk=D, name="ffn_in", rider=gather("w_ffn_out"))
    w_out3 = gathered("w_ffn_out", rode)
    w_out3 = w_out3.reshape(1, -1, w_out3.shape[2])
    d_ff = w_out3.shape[1]
    act = _swiglu_fwd(ff)
    f = _mm_nn(act, w_out3, tm=1024, tn=1024, tk=d_ff // 4, name="ffn_out")
    du2, df, acc2 = _loss_head(f, x1, tgt, gate2, ln2_g, ln2_b)
    loss = (0.5 / D) * jnp.sum(acc2[3])
    da = _mm_nt(df, w_out3, tm=1024, to=512, tn=D, name="d_act")
    g = blocks(_mm_tn(act, df, q=1, tk=512, tn=1024, tt=T, name="g_ffn_out"))
    d_gate_up, rode = _swiglu_bwd(da, ff, rider=to_sibling(g))
    p_out = pair_sum("w_ffn_out", g, rode)
    dff = jnp.concatenate(d_gate_up, axis=1)
    dh2, rode = carrying(_mm_nt, dff, w_ffn_in, tm=1024, to=1024, tn=w_ffn_in.shape[2], name="d_h2",
                         rider=to_chips(p_out))
    g_ffn_out = reduced("w_ffn_out", p_out, rode)
    g = _mm_tn(h2, dff, q=N_CHIPS, tk=512, tn=w_ffn_in.shape[2] // 2, tt=T, name="g_ffn_in")
    if place is None:
        (du1, dmix, acc1), p_fin = _mid_bwd(dh2, du2, x1, mix, x, gate1, ln1_g, scale2), g
    else:
        (du1, dmix, acc1), rode = _mid_bwd(dh2, du2, x1, mix, x, gate1, ln1_g, scale2, rider=to_sibling(g))
        p_fin = pair_sum("w_ffn_in", g, rode)
    dmixin = _mm_nt(dmix, w_o3, tm=1024, to=512, tn=D, name="d_mixin")
    p_o = pair_sum("w_o", blocks(_mm_tn(mixin, dmix, q=1, tk=512, tn=1024, tt=T, name="g_o")))
    cut = 19 * p_fin.shape[1] // 32
    (dq, dk, dv, dbias, dgain), rode = _attn_bwd(qkv, kv, bias, attn_norm_g, dmixin, rider=to_chips(p_fin, (0, cut)))
    (dzq, dzf, dxi, dzg, dl0, dgn), rode = _hgrn_bwd(
        proj, lb_logits, gnorm_g, o_b, st_all, dmixin,
        rider=_join(to_chips(p_fin, (cut, p_fin.shape[1] - cut), rode), to_chips(p_o)))
    g_ffn_in = reduced("w_ffn_in", p_fin, rode[:1])
    g_o = reduced("w_o", p_o, rode[1:])
    dproj = jnp.concatenate([dq, dk[KPAD:].astype(BF16), dv[KPAD:].astype(BF16), dzq, dzf, dxi, dzg], axis=1)
    p_in = pair_sum("w_in", _mm_tn(h1, dproj, q=N_CHIPS, tk=512, tn=w_in.shape[2] // 2, tt=T, name="g_in"))
    dh1, rode = carrying(_mm_nt, dproj, w_in, tm=1024, to=1024, tn=w_in.shape[2], name="d_h1", rider=to_chips(p_in))
    g_in = reduced("w_in", p_in, rode)
    grad_x, acc0 = _first_bwd(dh1, du1, x, scale1)
    dmod = jnp.concatenate([acc0[1:2], acc0[0:1], acc1[4:5], acc1[1:2], acc1[0:1], acc2[2:3]], axis=0)
    small = dict(rel_bias=_bias_table_grad(dbias), attn_norm_g=dgain,
                 lb_logits=jnp.concatenate([dl0, -dl0], axis=0), gnorm_g=dgn,
                 ln1_g=acc1[2:3], ln1_b=acc1[3:4], ln2_g=acc2[0:1], ln2_b=acc2[1:2])
    return loss, grad_x, dict(w_in=g_in, w_o=g_o, w_ffn_in=g_ffn_in, w_ffn_out=g_ffn_out), dmod, small


SMALL = ("rel_bias", "attn_norm_g", "lb_logits", "gnorm_g", "ln1_g", "ln1_b", "ln2_g", "ln2_b")
SMALL_ROWS = 256


def kernel(x, c, w_ada, b_ada, w_in, rel_bias, attn_norm_g, lb_logits, gnorm_g, w_o, ln1_g, ln1_b, w_ffn_in, w_ffn_out, ln2_g, ln2_b, loss_target, m_w_ada, m_b_ada, m_w_in, m_rel_bias, m_attn_norm_g, m_lb_logits, m_gnorm_g, m_w_o, m_ln1_g, m_ln1_b, m_w_ffn_in, m_w_ffn_out, m_ln2_g, m_ln2_b, v_w_ada, v_b_ada, v_w_in, v_rel_bias, v_attn_norm_g, v_lb_logits, v_gnorm_g, v_w_o, v_ln1_g, v_ln1_b, v_w_ffn_in, v_w_ffn_out, v_ln2_g, v_ln2_b):
    mx, my, mc = _place()
    me = 4 * mx + 2 * my + mc
    chip = 2 * mx + my
    sel = jnp.stack([chip, mc]).astype(jnp.int32)
    D = x.shape[2]
    ns_ada = w_ada.shape[2]

    big = dict(w_in=(w_in, m_w_in, v_w_in), w_o=(w_o, m_w_o, v_w_o), w_ffn_in=(w_ffn_in, m_w_ffn_in, v_w_ffn_in),
               w_ffn_out=(w_ffn_out, m_w_ffn_out, v_w_ffn_out))
    shards = {n: t[0][0].astype(BF16) for n, t in big.items()}

    c_all = _gather_small(c.reshape(D // 128, 128), "gather_c").reshape(N_DEV, D)
    c_all = jnp.pad(c_all, ((0, 16 - N_DEV), (0, 0)))
    mod_cols = _ada_fwd(c_all, w_ada[0])[:N_DEV]
    mod_all = _gather_small(mod_cols.reshape(-1, 128), "gather_mod").reshape(N_DEV, N_DEV, ns_ada)
    mod = lax.dynamic_index_in_dim(mod_all[::2], me, axis=1, keepdims=False)
    mod = (mod.reshape(1, -1) + b_ada).reshape(6, D)

    loss, grad_x, g_big, dmod, g_small = _layer(
        x[0], loss_target[0], mod, shards, rel_bias[0], attn_norm_g, lb_logits, gnorm_g, ln1_g, ln1_b, ln2_g, ln2_b,
        place=(chip, sel))
    loss = lax.psum(loss, ("x", "y", "c"))

    grads, deltas, new_m, new_v = {}, {}, {}, {}
    for n, (w, m, v) in big.items():
        g = g_big[n]
        (d, mo, vo), _ = _adamw(g, w[0], m[0], v[0], "adamw_" + n)
        grads[n], deltas[n], new_m[n], new_v[n] = g[None], d[None], mo[None], vo[None]

    small_in = dict(rel_bias=(rel_bias, m_rel_bias, v_rel_bias), attn_norm_g=(attn_norm_g, m_attn_norm_g, v_attn_norm_g),
                    lb_logits=(lb_logits, m_lb_logits, v_lb_logits), gnorm_g=(gnorm_g, m_gnorm_g, v_gnorm_g),
                    ln1_g=(ln1_g, m_ln1_g, v_ln1_g), ln1_b=(ln1_b, m_ln1_b, v_ln1_b), ln2_g=(ln2_g, m_ln2_g, v_ln2_g),
                    ln2_b=(ln2_b, m_ln2_b, v_ln2_b))
    g_all = _gather_small(_pack([dmod] + [g_small[n] for n in SMALL], SMALL_ROWS), "gather_small")
    packed = [_pack([t] + [small_in[n][i] for n in SMALL], SMALL_ROWS)
              for i, t in enumerate((b_ada, m_b_ada, v_b_ada))]
    shapes = [b_ada.shape] + [small_in[n][0].shape for n in SMALL]
    outs = [_unpack(o, shapes) for o in _small_update(g_all, *packed)]
    for i, n in enumerate(("b_ada",) + SMALL):
        grads[n], deltas[n], new_m[n], new_v[n] = outs[0][i], outs[1][i], outs[2][i], outs[3][i]

    dmod_all = g_all[:, :6 * D // 128].reshape(N_DEV, 6 * D)
    dmod_cols = lax.dynamic_slice_in_dim(dmod_all, chip * ns_ada, ns_ada, axis=1)
    dmod_cols = jnp.pad(dmod_cols, ((0, 16 - N_DEV), (0, 0)))
    (g, d, mo, vo), _ = _ada_bwd(c_all, dmod_cols, w_ada[0], m_w_ada[0], v_w_ada[0])
    grads["w_ada"], deltas["w_ada"], new_m["w_ada"], new_v["w_ada"] = g[None], d[None], mo[None], vo[None]

    order = ("w_ada", "b_ada", "w_in", "rel_bias", "attn_norm_g", "lb_logits", "gnorm_g", "w_o", "ln1_g", "ln1_b",
             "w_ffn_in", "w_ffn_out", "ln2_g", "ln2_b")
    return (loss, grad_x[None], *[grads[n] for n in order], *[deltas[n] for n in order],
            *[new_m[n] for n in order], *[new_v[n] for n in order])
```

```python
import numpy as np
import jax
import jax.numpy as jnp
from jax import lax
from jax.experimental import pallas as pl
from jax.experimental.pallas import tpu as pltpu

F32 = jnp.float32
BF16 = jnp.bfloat16
MESH = pl.DeviceIdType.MESH
HIGHEST = lax.Precision.HIGHEST

CHUNK = 64
N_PAST = 8
QG = 4
QROWS = QG * CHUNK
KPAD = N_PAST * CHUNK
UNION = (QG + N_PAST) * CHUNK
BAND = (N_PAST + 1) * CHUNK
HD_A = 64
HD_B = 128
SUB = 16
HGRN_HEADS = 4
MAX_REL = 256
EPS = 1e-5
ALPHA = 2.0 ** 0.25
LR, B1, B2, ADAM_EPS, WD, STEP = 1e-3, 0.9, 0.999, 1e-8, 0.01, 10
N_CHIPS = 4
N_DEV = 8
NEG = -1e30
TILE_BYTES = 3 << 19

NN = ((1,), (0,))
NT = ((1,), (1,))
TN = ((0,), (0,))


def _dot(a, b, dims=NN, precision=None):
    return lax.dot_general(a, b, (dims, ((), ())), preferred_element_type=F32, precision=precision)


def _params(sem=None, vmem_mb=None, **kw):
    return pltpu.CompilerParams(dimension_semantics=sem,
                                vmem_limit_bytes=None if vmem_mb is None else vmem_mb << 20, **kw)


def _row_tile(rows, cols):
    for cand in (512, 256, 128, 64, 32, 16, 8):
        if rows % cand == 0 and cand * cols * 4 <= TILE_BYTES:
            return cand
    raise ValueError((rows, cols))


def _place():
    return lax.axis_index("x"), lax.axis_index("y"), lax.axis_index("c")


def _flip(v, bit):
    return 1 - v if bit else v


ANY = pl.BlockSpec(memory_space=pl.ANY)
CHIP_FLIPS = ((1, 0), (0, 1), (1, 1))


class _Rider:
    def __init__(self, operands, out_shape, n_sems, start, finish, aliases=None):
        self.operands, self.out_shape, self.n_sems, self.start, self.finish = operands, out_shape, n_sems, start, finish
        self.aliases = aliases or {}


def _call(body, rider, *, name, grid, in_specs, out_specs, out_shape, scratch_shapes, compiler_params, operands):
    if rider is None:
        outs = pl.pallas_call(body, name=name, grid=grid, in_specs=in_specs, out_specs=out_specs, out_shape=out_shape,
                              scratch_shapes=scratch_shapes, compiler_params=compiler_params)(*operands)
        return list(outs), []
    n_in, n_out, n_sc = len(in_specs), len(out_specs), len(scratch_shapes)
    r_in, r_out = len(rider.operands), len(rider.out_shape)

    def carried(*refs):
        refs = list(refs)
        cuts = [n_in, r_in, n_out, r_out, n_sc]
        ins, r_ins, outs, r_outs, scratch = [[refs.pop(0) for _ in range(n)] for n in cuts]
        first, last = None, None
        for axis, size in enumerate(grid):
            i = pl.program_id(axis)
            first = (i == 0) if first is None else first & (i == 0)
            last = (i == size - 1) if last is None else last & (i == size - 1)

        @pl.when(first)
        def _():
            rider.start(r_ins, r_outs, *refs)

        body(*ins, *outs, *scratch)

        @pl.when(last)
        def _():
            rider.finish(r_ins, r_outs, *refs)

    sems = [pltpu.SemaphoreType.DMA((rider.n_sems,)), pltpu.SemaphoreType.DMA((rider.n_sems,))]
    outs = pl.pallas_call(carried, name=name, grid=grid, in_specs=list(in_specs) + [ANY] * r_in,
                          out_specs=list(out_specs) + [ANY] * r_out, out_shape=list(out_shape) + rider.out_shape,
                          scratch_shapes=list(scratch_shapes) + sems, compiler_params=compiler_params,
                          input_output_aliases={n_in + i: n_out + o for i, o in rider.aliases.items()},
                          )(*operands, *rider.operands)
    return list(outs[:n_out]), list(outs[n_out:])


def _alone(rider, name):
    def body(*refs):
        ins, outs, sems = refs[:len(rider.operands)], refs[len(rider.operands):-2], refs[-2:]
        rider.start(ins, outs, *sems)
        rider.finish(ins, outs, *sems)

    return pl.pallas_call(
        body, name=name, in_specs=[ANY] * len(rider.operands), out_specs=[ANY] * len(rider.out_shape),
        out_shape=rider.out_shape,
        scratch_shapes=[pltpu.SemaphoreType.DMA((rider.n_sems,)), pltpu.SemaphoreType.DMA((rider.n_sems,))],
    )(*rider.operands)


class _Sems:
    def __init__(self, sems, base):
        self.sems, self.base = sems, base

    @property
    def at(self):
        return self

    def __getitem__(self, k):
        return self.sems.at[self.base + k]


def _join(*riders):
    riders = [r for r in riders if r is not None]
    if len(riders) < 2:
        return riders[0] if riders else None

    def parts(ins, outs, send_sems, recv_sems):
        i = o = s = 0
        for r in riders:
            ni, no = len(r.operands), len(r.out_shape)
            yield r, ins[i:i + ni], outs[o:o + no], _Sems(send_sems, s), _Sems(recv_sems, s)
            i, o, s = i + ni, o + no, s + r.n_sems

    def start(*refs):
        for r, *args in parts(*refs):
            r.start(*args)

    def finish(*refs):
        for r, *args in parts(*refs):
            r.finish(*args)

    aliases, i, o = {}, 0, 0
    for r in riders:
        aliases.update({i + a: o + b for a, b in r.aliases.items()})
        i, o = i + len(r.operands), o + len(r.out_shape)
    return _Rider([a for r in riders for a in r.operands], [s for r in riders for s in r.out_shape],
                  sum(r.n_sems for r in riders), start, finish, aliases)


def _gather_rider(shard, rows=None, into=None):
    K, Ns = shard.shape
    kh = K // 2
    first_row, n_rows = rows or (0, kh)

    def copies(w_ref, out_ref, send_sems, recv_sems):
        x, y, c = _place()
        chips = [(_flip(x, fx), _flip(y, fy)) for fx, fy in CHIP_FLIPS]

        def half(chip, which):
            return out_ref.at[2 * chip[0] + chip[1], pl.ds(which * kh + first_row, n_rows), :]

        def copy(k, dst, to, src=None):
            return pltpu.make_async_remote_copy(src_ref=dst if src is None else src, dst_ref=dst,
                                                send_sem=send_sems.at[k], recv_sem=recv_sems.at[k],
                                                device_id=to, device_id_type=MESH)

        def first():
            return [copy(j, half((x, y), c), (*chip, c), src=w_ref.at[pl.ds(c * kh + first_row, n_rows), :])
                    for j, chip in enumerate(chips)]

        def onward():
            return [copy(3 + j, half(chip, c), (x, y, 1 - c)) for j, chip in enumerate(chips)]

        def arriving(base, which):
            return [copy(base + j, half(chip, which), (x, y, c)) for j, chip in enumerate(chips)]

        return first, onward, arriving

    def start(ins, outs, send_sems, recv_sems):
        for cp in copies(ins[0], outs[0], send_sems, recv_sems)[0]():
            cp.start()

    def finish(ins, outs, send_sems, recv_sems):
        x, y, c = _place()
        first, onward, arriving = copies(ins[0], outs[0], send_sems, recv_sems)
        passed = onward()
        for arrived, cp in zip(arriving(0, c), passed):
            arrived.wait_recv()
            cp.start()
        for arrived in arriving(3, 1 - c):
            arrived.wait_recv()
        for cp in first() + passed:
            cp.wait_send()

    full = jax.ShapeDtypeStruct((N_CHIPS, K, Ns), shard.dtype)
    if into is None:
        return _Rider([shard], [full], 6, start, finish)
    return _Rider([shard, into], [full], 6, start, finish, aliases={1: 0})


def _pair_rider(g_full):
    Q, K, Ns = g_full.shape
    kh = K // 2

    def copy(g_ref, got_ref, send_sems, recv_sems):
        x, y, c = _place()
        return pltpu.make_async_remote_copy(src_ref=g_ref.at[:, pl.ds((1 - c) * kh, kh), :], dst_ref=got_ref,
                                            send_sem=send_sems.at[0], recv_sem=recv_sems.at[0],
                                            device_id=(x, y, 1 - c), device_id_type=MESH)

    def start(ins, outs, send_sems, recv_sems):
        copy(ins[0], outs[0], send_sems, recv_sems).start()

    def finish(ins, outs, send_sems, recv_sems):
        copy(ins[0], outs[0], send_sems, recv_sems).wait()

    return _Rider([g_full], [jax.ShapeDtypeStruct((Q, kh, Ns), g_full.dtype)], 1, start, finish)


def _chips_rider(pair_sum, rows=None, into=None):
    Q, kh, Ns = pair_sum.shape
    first_row, n_rows = rows or (0, kh)

    def copies(p_ref, got_ref, send_sems, recv_sems):
        x, y, c = _place()
        part = pl.ds(first_row, n_rows)
        out = []
        for j, (fx, fy) in enumerate(CHIP_FLIPS):
            px, py = _flip(x, fx), _flip(y, fy)
            out.append(pltpu.make_async_remote_copy(
                src_ref=p_ref.at[2 * px + py, part, :], dst_ref=got_ref.at[j, part, :], send_sem=send_sems.at[j],
                recv_sem=recv_sems.at[j], device_id=(px, py, c), device_id_type=MESH))
        return out

    def start(ins, outs, send_sems, recv_sems):
        for cp in copies(ins[0], outs[0], send_sems, recv_sems):
            cp.start()

    def finish(ins, outs, send_sems, recv_sems):
        sends = copies(ins[0], outs[0], send_sems, recv_sems)
        for cp in sends:
            cp.wait_recv()
        for cp in sends:
            cp.wait_send()

    got = jax.ShapeDtypeStruct((Q - 1, kh, Ns), pair_sum.dtype)
    if into is None:
        return _Rider([pair_sum], [got], 3, start, finish)
    return _Rider([pair_sum, into], [got], 3, start, finish, aliases={1: 0})


def _mm(a, b, *, grid, a_spec, b_spec, o_spec, o_shape, o_dtype, dims, acc_shape, name, rider=None, vmem_mb=48):
    nk = grid[2]

    def body(a_ref, b_ref, o_ref, *scratch):
        part = _dot(a_ref[...], b_ref[...], dims)
        if nk == 1:
            o_ref[...] = part.astype(o_ref.dtype)
            return
        acc_ref, = scratch
        k = pl.program_id(2)

        @pl.when(k == 0)
        def _():
            acc_ref[...] = part

        @pl.when(k > 0)
        def _():
            acc_ref[...] += part

        @pl.when(k == nk - 1)
        def _():
            o_ref[...] = acc_ref[...].astype(o_ref.dtype)

    (out,), rode = _call(
        body, rider, name=name, grid=grid, in_specs=[a_spec, b_spec], out_specs=[o_spec],
        out_shape=[jax.ShapeDtypeStruct(o_shape, o_dtype)],
        scratch_shapes=[] if nk == 1 else [pltpu.VMEM(acc_shape, F32)],
        compiler_params=_params(("parallel", "parallel", "arbitrary") if rider is None else ("arbitrary",) * 3, vmem_mb),
        operands=(a, b))
    return out if rider is None else (out, rode)


def _mm_nn(a, w, *, tm, tn, tk, name, rider=None, cols=None, o_dtype=F32):
    T, K = a.shape
    Q, _, Ns = w.shape
    nbs = Ns // tn
    tm = min(tm, T)
    j0, j1 = cols or (0, Q * nbs)
    return _mm(a, w, grid=(T // tm, j1 - j0, K // tk),
               a_spec=pl.BlockSpec((tm, tk), lambda i, j, k: (i, k)),
               b_spec=pl.BlockSpec((None, tk, tn), lambda i, j, k: ((j + j0) // nbs, k, (j + j0) % nbs)),
               o_spec=pl.BlockSpec((tm, tn), lambda i, j, k: (i, j)),
               o_shape=(T, (j1 - j0) * tn), o_dtype=o_dtype, dims=NN, acc_shape=(tm, tn), name=name, rider=rider)


def _mm_nt(g, w, *, tm, to, tn, name, rider=None):
    T = g.shape[0]
    Q, K, Ns = w.shape
    nbs = Ns // tn
    tm = min(tm, T)
    return _mm(g, w, grid=(T // tm, K // to, Q * nbs),
               a_spec=pl.BlockSpec((tm, tn), lambda i, j, n: (i, n)),
               b_spec=pl.BlockSpec((None, to, tn), lambda i, j, n: (n // nbs, j, n % nbs)),
               o_spec=pl.BlockSpec((tm, to), lambda i, j, n: (i, j)),
               o_shape=(T, K), o_dtype=F32, dims=NT, acc_shape=(tm, to), name=name, rider=rider)


def _mm_tn(a, g, *, q, tk, tn, tt, name):
    T, K = a.shape
    Ns = g.shape[1] // q
    nbs = Ns // tn
    return _mm(a, g, grid=(K // tk, q * nbs, T // tt),
               a_spec=pl.BlockSpec((tt, tk), lambda i, j, t: (t, i)),
               b_spec=pl.BlockSpec((tt, tn), lambda i, j, t: (t, j)),
               o_spec=pl.BlockSpec((None, tk, tn), lambda i, j, t: (j // nbs, i, j % nbs)),
               o_shape=(q, K, Ns), o_dtype=BF16, dims=TN, acc_shape=(tk, tn), name=name)


def _ln(u):
    mu = jnp.mean(u, axis=-1, keepdims=True)
    d = u - mu
    r = lax.rsqrt(jnp.mean(d * d, axis=-1, keepdims=True) + EPS)
    return d * r, r


def _ln_bwd(dy, un, r):
    return r * (dy - jnp.mean(dy, axis=-1, keepdims=True) - un * jnp.mean(dy * un, axis=-1, keepdims=True))


def _colsum(v):
    return jnp.sum(v, axis=0, keepdims=True)


def _rowwise(name, fn, bigs, vecs, out_dtypes, n_acc, tm=128, rider=None):
    T, D = bigs[0].shape
    nb, nv, no = len(bigs), len(vecs), len(out_dtypes)

    def body(*refs):
        outs, accs = fn([r[...] for r in refs[:nb]], [r[...] for r in refs[nb:nb + nv]])
        for r, o in zip(refs[nb + nv:nb + nv + no], outs):
            r[...] = o.astype(r.dtype)
        if n_acc:
            acc_ref = refs[nb + nv + no]

            @pl.when(pl.program_id(0) == 0)
            def _():
                acc_ref[...] = jnp.zeros_like(acc_ref)

            for row, a in enumerate(accs):
                acc_ref[row:row + 1, :] += a

    big_spec = pl.BlockSpec((tm, D), lambda i: (i, 0))
    vec_spec = pl.BlockSpec((1, D), lambda i: (0, 0))
    out_shape = [jax.ShapeDtypeStruct((T, D), dt) for dt in out_dtypes]
    out_specs = [big_spec] * no
    if n_acc:
        out_shape.append(jax.ShapeDtypeStruct((8, D), F32))
        out_specs.append(pl.BlockSpec((8, D), lambda i: (0, 0)))
    outs, rode = _call(
        body, rider, name=name, grid=(T // tm,), in_specs=[big_spec] * nb + [vec_spec] * nv,
        out_specs=out_specs, out_shape=out_shape, scratch_shapes=[],
        compiler_params=_params(("arbitrary",), 48), operands=(*bigs, *vecs))
    return outs if rider is None else (outs, rode)


def _pre_mixer(x, scale1, shift1):
    def fn(b, v):
        xn, _ = _ln(b[0])
        return [xn * (1.0 + v[0]) + v[1]], []
    return _rowwise("pre_mixer", fn, [x], [scale1, shift1], [BF16], 0)[0]


def _post_mixer(mix, x, gate1, g1, b1, scale2, shift2):
    def fn(b, v):
        un1, _ = _ln(ALPHA * b[1] + v[0] * b[0])
        x1 = un1 * v[1] + v[2]
        xn1, _ = _ln(x1)
        return [x1, xn1 * (1.0 + v[3]) + v[4]], []
    return _rowwise("post_mixer", fn, [mix, x], [gate1, g1, b1, scale2, shift2], [F32, BF16], 0)


def _loss_head(f, x1, tgt, gate2, g2, b2):
    def fn(b, v):
        ff, xx, tt = b
        d_model = ff.shape[-1]
        un2, r2 = _ln(ALPHA * xx + v[0] * ff)
        err = un2 * v[1] + v[2] - tt
        dy = err * (1.0 / d_model)
        du2 = _ln_bwd(dy * v[1], un2, r2)
        return [du2, du2 * v[0]], [_colsum(dy * un2), _colsum(dy), _colsum(du2 * ff), _colsum(err * err)]
    return _rowwise("loss_head", fn, [f, x1, tgt], [gate2, g2, b2], [F32, BF16], 4)


def _mid_bwd(dh2, du2, x1, mix, x, gate1, g1, scale2, rider=None):
    def fn(b, v):
        dh, du, xx1, mm, xx = b
        xn1, r1n = _ln(xx1)
        dx1 = ALPHA * du + _ln_bwd(dh * (1.0 + v[2]), xn1, r1n)
        un1, r1 = _ln(ALPHA * xx + v[0] * mm)
        du1 = _ln_bwd(dx1 * v[1], un1, r1)
        return [du1, du1 * v[0]], [_colsum(dh * xn1), _colsum(dh), _colsum(dx1 * un1), _colsum(dx1),
                                   _colsum(du1 * mm)]
    return _rowwise("mid_bwd", fn, [dh2, du2, x1, mix, x], [gate1, g1, scale2], [F32, BF16], 5, rider=rider)


def _first_bwd(dh1, du1, x, scale1):
    def fn(b, v):
        dh, du, xx = b
        xn, r0 = _ln(xx)
        return [ALPHA * du + _ln_bwd(dh * (1.0 + v[0]), xn, r0)], [_colsum(dh * xn), _colsum(dh)]
    return _rowwise("first_bwd", fn, [dh1, du1, x], [scale1], [F32], 2)


def _ffn_in_swiglu(h2, w, *, tm, tn, rider=None):
    T, K = h2.shape
    Q, _, Ns = w.shape
    nbs = Ns // tn
    half = Q * nbs // 2
    tm = min(tm, T)

    def body(a_ref, wg_ref, wu_ref, g_ref, u_ref, act_ref):
        a = a_ref[...]
        g, u = _dot(a, wg_ref[...]), _dot(a, wu_ref[...])
        g_ref[...] = g.astype(g_ref.dtype)
        u_ref[...] = u.astype(u_ref.dtype)
        act_ref[...] = (g * jax.nn.sigmoid(g) * u).astype(act_ref.dtype)

    cols = lambda first: pl.BlockSpec((None, K, tn), lambda i, j: ((j + first) // nbs, 0, (j + first) % nbs))
    blk = pl.BlockSpec((tm, tn), lambda i, j: (i, j))
    return _call(
        body, rider, name="ffn_in", grid=(T // tm, half),
        in_specs=[pl.BlockSpec((tm, K), lambda i, j: (i, 0)), cols(0), cols(half)], out_specs=[blk] * 3,
        out_shape=[jax.ShapeDtypeStruct((T, half * tn), BF16)] * 3, scratch_shapes=[],
        compiler_params=_params(("arbitrary", "arbitrary"), 48), operands=(h2, w, w))


def _d_act_swiglu(df, w, gate, up, *, tm, to, rider=None):
    T, N = df.shape
    F = w.shape[1]
    tm = min(tm, T)

    def body(df_ref, w_ref, g_ref, u_ref, dg_ref, du_ref):
        d = _dot(df_ref[...], w_ref[...], NT)
        g = g_ref[...].astype(F32)
        s = jax.nn.sigmoid(g)
        du_ref[...] = (d * g * s).astype(du_ref.dtype)
        dg_ref[...] = (d * u_ref[...].astype(F32) * s * (1.0 + g * (1.0 - s))).astype(dg_ref.dtype)

    blk = pl.BlockSpec((tm, to), lambda i, j: (i, j))
    return _call(
        body, rider, name="d_act", grid=(T // tm, F // to),
        in_specs=[pl.BlockSpec((tm, N), lambda i, j: (i, 0)), pl.BlockSpec((None, to, N), lambda i, j: (0, j, 0)), blk, blk],
        out_specs=[blk, blk], out_shape=[jax.ShapeDtypeStruct((T, F), BF16)] * 2, scratch_shapes=[],
        compiler_params=_params(("arbitrary", "arbitrary"), 48), operands=(df, w, gate, up))


PAIR = 2


def _fill_table(table_ref, band_ref):
    table_ref[...] = jnp.full(table_ref.shape, NEG, F32)
    for e in range(PAIR):
        for g in range(QG):
            table_ref[e, g * CHUNK:(g + 1) * CHUNK, g * CHUNK:g * CHUNK + BAND] = band_ref[e]


def _attn_probs(q_ref, k_ref, bias_ref, e, step):
    start = pl.multiple_of(step * QROWS, QROWS)
    lanes = pl.ds(e * HD_A, HD_A)
    s = _dot(q_ref[:, lanes], k_ref[pl.ds(start, UNION), lanes], NT) * (HD_A ** -0.5) + bias_ref[e]
    col = lax.broadcasted_iota(jnp.int32, s.shape, 1)
    s = jnp.where(col + start >= KPAD, s, NEG)
    p = jnp.exp(s - jnp.max(s, axis=-1, keepdims=True))
    return p / jnp.sum(p, axis=-1, keepdims=True), start


def _attn_specs(T, n_pairs):
    wide = PAIR * HD_A
    per_step = pl.BlockSpec((QROWS, wide), lambda hp, n: (n, hp))
    keys = pl.BlockSpec((KPAD + T, wide), lambda hp, n: (0, hp))
    values = pl.BlockSpec((KPAD + T, wide), lambda hp, n: (0, n_pairs + hp))
    table = pl.BlockSpec((PAIR, CHUNK, BAND), lambda hp, n: (hp, 0, 0))
    vec = pl.BlockSpec((1, wide), lambda hp, n: (0, hp))
    return per_step, keys, values, table, vec


def _attn_fwd(qkv, kv, bias, gain, rider=None):
    T = qkv.shape[0]
    W = gain.shape[1]
    n_pairs = W // (PAIR * HD_A)

    def body(q_ref, k_ref, v_ref, band_ref, gain_ref, o_ref, table_ref):
        @pl.when(pl.program_id(1) == 0)
        def _():
            _fill_table(table_ref, band_ref)

        for e in range(PAIR):
            lanes = pl.ds(e * HD_A, HD_A)
            p, start = _attn_probs(q_ref, k_ref, table_ref, e, pl.program_id(1))
            o = _dot(p.astype(BF16), v_ref[pl.ds(start, UNION), lanes])
            rr = lax.rsqrt(jnp.mean(o * o, axis=-1, keepdims=True) + EPS)
            o_ref[:, lanes] = (o * rr * gain_ref[:, lanes]).astype(o_ref.dtype)

    per_step, keys, values, table, vec = _attn_specs(T, n_pairs)
    return _call(
        body, rider, name="attn_fwd", grid=(n_pairs, T // QROWS), in_specs=[per_step, keys, values, table, vec],
        out_specs=[per_step], out_shape=[jax.ShapeDtypeStruct((T, W), BF16)],
        scratch_shapes=[pltpu.VMEM((PAIR, QROWS, UNION), F32)],
        compiler_params=_params(("arbitrary", "arbitrary"), 40), operands=(qkv, kv, kv, bias, gain))


def _attn_bwd(qkv, kv, bias, gain, dmixin, rider=None):
    T = qkv.shape[0]
    W = gain.shape[1]
    n_pairs = W // (PAIR * HD_A)
    scale = HD_A ** -0.5

    def body(q_ref, k_ref, v_ref, band_ref, gain_ref, don_ref, dq_ref, dk_ref, dv_ref, dband_ref, dgain_ref,
             table_ref, dtable_ref):
        n = pl.program_id(1)

        @pl.when(n == 0)
        def _():
            _fill_table(table_ref, band_ref)
            dk_ref[...] = jnp.zeros_like(dk_ref)
            dv_ref[...] = jnp.zeros_like(dv_ref)
            dtable_ref[...] = jnp.zeros_like(dtable_ref)
            dgain_ref[...] = jnp.zeros_like(dgain_ref)

        for e in range(PAIR):
            lanes = pl.ds(e * HD_A, HD_A)
            p, start = _attn_probs(q_ref, k_ref, table_ref, e, n)
            keys = pl.ds(start, UNION)
            pb = p.astype(BF16)
            vb = v_ref[keys, lanes]
            o = _dot(pb, vb)
            rr = lax.rsqrt(jnp.mean(o * o, axis=-1, keepdims=True) + EPS)
            on = o * rr
            d_on = don_ref[:, lanes]
            dgain_ref[:, lanes] += _colsum(d_on * on)
            dyo = d_on * gain_ref[:, lanes]
            do = rr * (dyo - on * jnp.mean(dyo * on, axis=-1, keepdims=True))
            dob = do.astype(BF16)
            dp = _dot(dob, vb, NT)
            ds = p * (dp - jnp.sum(do * o, axis=-1, keepdims=True))
            dtable_ref[e] += ds
            dsb = ds.astype(BF16)
            dq_ref[:, lanes] = (_dot(dsb, k_ref[keys, lanes]) * scale).astype(dq_ref.dtype)
            dk_ref[keys, lanes] += _dot(dsb, q_ref[:, lanes], TN) * scale
            dv_ref[keys, lanes] += _dot(pb, dob, TN)

        @pl.when(n == T // QROWS - 1)
        def _():
            for e in range(PAIR):
                dband_ref[e] = sum(dtable_ref[e, g * CHUNK:(g + 1) * CHUNK, g * CHUNK:g * CHUNK + BAND]
                                   for g in range(QG))

    per_step, keys, values, table, vec = _attn_specs(T, n_pairs)
    H = n_pairs * PAIR
    return _call(
        body, rider, name="attn_bwd", grid=(n_pairs, T // QROWS),
        in_specs=[per_step, keys, values, table, vec, per_step],
        out_specs=[per_step, keys, keys, table, vec],
        out_shape=[jax.ShapeDtypeStruct((T, W), BF16), jax.ShapeDtypeStruct((KPAD + T, W), F32),
                   jax.ShapeDtypeStruct((KPAD + T, W), F32), jax.ShapeDtypeStruct((H, CHUNK, BAND), F32),
                   jax.ShapeDtypeStruct((1, W), F32)],
        scratch_shapes=[pltpu.VMEM((PAIR, QROWS, UNION), F32)] * 2,
        compiler_params=_params(("arbitrary", "arbitrary"), 40),
        operands=(qkv, kv, kv, bias, gain, dmixin))


N_DIAG = CHUNK + BAND - 1


def _bias_band(rel_bias):
    H = rel_bias.shape[0]
    idx = np.clip(BAND - 1 - np.arange(N_DIAG), -MAX_REL, MAX_REL) + MAX_REL
    rolled = rel_bias[:, idx[(np.arange(N_DIAG) + CHUNK - 1) % N_DIAG]]
    flat = jnp.broadcast_to(rolled[:, None, :], (H, CHUNK, N_DIAG)).reshape(H, CHUNK * N_DIAG)
    return flat[:, :CHUNK * (N_DIAG - 1)].reshape(H, CHUNK, N_DIAG - 1)[:, :, :BAND]


def _bias_band_grad(dband):
    H = dband.shape[0]
    skew = jnp.pad(dband, ((0, 0), (0, 0), (CHUNK - 1, 0))).reshape(H, CHUNK * N_DIAG)
    skew = jnp.pad(skew, ((0, 0), (0, CHUNK))).reshape(H, CHUNK, N_DIAG + 1)
    diag = jnp.sum(skew, axis=1)[:, :N_DIAG]
    n_far = BAND - MAX_REL
    far = jnp.sum(diag[:, :n_far], axis=1, keepdims=True)
    near = diag[:, n_far:][:, ::-1]
    zeros = jnp.zeros((H, MAX_REL - (CHUNK - 1)), F32)
    return jnp.concatenate([zeros, near, far], axis=1)


def _tri(n, lower):
    r = lax.broadcasted_iota(jnp.int32, (n, n), 0)
    c = lax.broadcasted_iota(jnp.int32, (n, n), 1)
    return jnp.where((c <= r) if lower else (c >= r), 1.0, 0.0).astype(F32)


def _hgrn_gates(zq_ref, zf_ref, lbl_ref, q_s, k_s, b_s):
    lb = jax.nn.sigmoid(lbl_ref[0:1, :] - lbl_ref[1:2, :])
    zq = zq_ref[...]
    sig = jax.nn.sigmoid(zf_ref[...])
    f = lb + (1.0 - lb) * sig
    sq = jax.nn.sigmoid(zq)
    q_s[...] = zq * sq
    k_s[...] = 1.0 - f
    b_s[...] = _dot(_tri(CHUNK, True), jnp.log(f), precision=HIGHEST)
    return lb, sig, f, sq


def _sub_rows(i):
    return pl.ds(i * SUB, SUB)


def _row_mask(s):
    return lax.broadcasted_iota(jnp.int32, (SUB, HD_B), 0) >= s


def _decay_from(b_sub, b_row, s):
    return jnp.where(_row_mask(s), jnp.exp(jnp.minimum(b_sub - b_row, 0.0)), 0.0)


def _hgrn_fwd(proj, lb_logits, gnorm_g, rider=None):
    T = proj.shape[0]
    nC = T // CHUNK
    W = lb_logits.shape[1]
    G = W // HD_B // HGRN_HEADS
    col0 = (proj.shape[1] - 4 * W) // (HD_B * HGRN_HEADS)
    wide = HGRN_HEADS * HD_B

    def body(*refs):
        @pl.when(pl.program_id(1) == 0)
        def _():
            refs[9][...] = jnp.zeros_like(refs[9])

        for h in range(HGRN_HEADS):
            lanes = pl.ds(h * HD_B, HD_B)
            one_head(*[r.at[:, lanes] for r in refs[:5]], refs[5], *[r.at[:, lanes] for r in refs[6:8]],
                     *[r.at[h] for r in refs[8:]])

    def one_head(zq_ref, zf_ref, xi_ref, zg_ref, lbl_ref, gn_ref, mix_ref, o_ref, stall_ref, st_ref, q_s, k_s, b_s, acc_s):
        _hgrn_gates(zq_ref, zf_ref, lbl_ref, q_s, k_s, b_s)
        q, k, b = q_s[...], k_s[...], b_s[...]
        st = st_ref[...]
        stall_ref[...] = st
        b_last = b_s[CHUNK - 1:CHUNK, :]
        acc_s[...] = _dot((q * jnp.exp(b)).astype(BF16), st.astype(BF16), NT)
        for i in range(CHUNK // SUB):
            rows = _sub_rows(i)
            q_i, b_i = q_s[rows, :], b_s[rows, :]
            acc = jnp.zeros((SUB, HD_B), F32)
            if i:
                past = pl.ds(0, i * SUB)
                b_ref = b_s[i * SUB - 1:i * SUB, :]
                qs = (q_i * jnp.exp(b_i - b_ref)).astype(BF16)
                ks = (k_s[past, :] * jnp.exp(b_ref - b_s[past, :])).astype(BF16)
                acc += _dot(_dot(qs, ks, NT).astype(BF16), xi_ref[past, :].astype(BF16))
            for s in range(SUB):
                row = pl.ds(i * SUB + s, 1)
                w = q_i * _decay_from(b_i, b_s[row, :], s)
                acc += jnp.sum(w * k_s[row, :], axis=-1, keepdims=True) * xi_ref[row, :]
            acc_s[rows, :] += acc
        o = acc_s[...]
        kd = (k * jnp.exp(b_last - b)).astype(BF16)
        st_ref[...] = st * jnp.exp(b_last) + _dot(xi_ref[...].astype(BF16), kd, TN)
        o_ref[...] = o
        zg = zg_ref[...]
        rr = lax.rsqrt(jnp.mean(o * o, axis=-1, keepdims=True) + EPS)
        mix_ref[...] = (o * rr * gn_ref[...] * (zg * jax.nn.sigmoid(zg))).astype(mix_ref.dtype)

    col = lambda part: pl.BlockSpec((CHUNK, wide), lambda g, n: (n, col0 + part * G + g))
    out_blk = pl.BlockSpec((CHUNK, wide), lambda g, n: (n, g))
    tile = pltpu.VMEM((HGRN_HEADS, CHUNK, HD_B), F32)
    return _call(
        body, rider, name="hgrn_fwd", grid=(G, nC),
        in_specs=[col(0), col(1), col(2), col(3), pl.BlockSpec((2, wide), lambda g, n: (0, g)),
                  pl.BlockSpec((1, HD_B), lambda g, n: (0, 0))],
        out_specs=[out_blk, out_blk, pl.BlockSpec((HGRN_HEADS, None, HD_B, HD_B), lambda g, n: (g, n, 0, 0))],
        out_shape=[jax.ShapeDtypeStruct((T, W), BF16), jax.ShapeDtypeStruct((T, W), F32),
                   jax.ShapeDtypeStruct((G * HGRN_HEADS, nC, HD_B, HD_B), F32)],
        scratch_shapes=[pltpu.VMEM((HGRN_HEADS, HD_B, HD_B), F32), tile, tile, tile, tile],
        compiler_params=_params(("arbitrary", "arbitrary")),
        operands=(proj, proj, proj, proj, lb_logits, gnorm_g))


def _hgrn_bwd(proj, lb_logits, gnorm_g, o_b, st_all, dmixin, rider=None):
    T = proj.shape[0]
    nC = T // CHUNK
    W = lb_logits.shape[1]
    G = W // HD_B // HGRN_HEADS
    wide = HGRN_HEADS * HD_B
    col0 = (proj.shape[1] - 4 * W) // wide
    dcol0 = (dmixin.shape[1] - W) // wide

    def body(*refs):
        g, n = pl.program_id(0), pl.program_id(1)
        dl0_ref, dgn_ref, dst_ref = refs[13:16]

        @pl.when(n == 0)
        def _():
            dst_ref[...] = jnp.zeros_like(dst_ref)
            dl0_ref[...] = jnp.zeros_like(dl0_ref)

        @pl.when((n == 0) & (g == 0))
        def _():
            dgn_ref[...] = jnp.zeros_like(dgn_ref)

        for h in range(HGRN_HEADS):
            lanes = pl.ds(h * HD_B, HD_B)
            cut = lambda r: r.at[:, lanes]
            one_head(*[cut(r) for r in refs[:5]], refs[5], cut(refs[6]), refs[7].at[h], cut(refs[8]),
                     *[cut(r) for r in refs[9:14]], dgn_ref, *[r.at[h] for r in refs[15:]])

    def one_head(zq_ref, zf_ref, xi_ref, zg_ref, lbl_ref, gn_ref, o_ref, st_ref, dout_ref,
                 dzq_ref, dzf_ref, dxi_ref, dzg_ref, dl0_ref, dgn_ref, dst_ref, q_s, k_s, b_s, do_s, dq_s, dk_s, di_s):
        lb, sig, f, sq = _hgrn_gates(zq_ref, zf_ref, lbl_ref, q_s, k_s, b_s)
        q, k, b = q_s[...], k_s[...], b_s[...]
        zg, o, dout = zg_ref[...], o_ref[...], dout_ref[...]
        sg = jax.nn.sigmoid(zg)
        rr = lax.rsqrt(jnp.mean(o * o, axis=-1, keepdims=True) + EPS)
        on = o * rr
        gn = gn_ref[...]
        dzg_ref[...] = (dout * on * gn * sg * (1.0 + zg * (1.0 - sg))).astype(dzg_ref.dtype)
        d_on = dout * zg * sg
        dgn_ref[...] += _colsum(d_on * on)
        d_on = d_on * gn
        do = rr * (d_on - on * jnp.mean(d_on * on, axis=-1, keepdims=True))
        do_s[...] = do
        dob = do.astype(BF16)
        st, dst = st_ref[...], dst_ref[...]
        b_last = b_s[CHUNK - 1:CHUNK, :]
        eb, e_last, k_dec = jnp.exp(b), jnp.exp(b_last), jnp.exp(b_last - b)
        qt, kd = q * eb, k * k_dec
        dstb = dst.astype(BF16)
        xib = xi_ref[...].astype(BF16)
        d_kd = _dot(xib, dstb)
        dq_s[...] = _dot(dob, st.astype(BF16)) * eb
        dk_s[...] = d_kd * k_dec
        di_s[...] = _dot(kd.astype(BF16), dstb, NT)
        d_b_last = e_last * _colsum(st * dst) + _colsum(d_kd * kd)
        dst_ref[...] = _dot(dob, qt.astype(BF16), TN) + dst * e_last
        for i in range(CHUNK // SUB):
            rows = _sub_rows(i)
            q_i, b_i, do_i = q_s[rows, :], b_s[rows, :], do_s[rows, :]
            dq_i = jnp.zeros((SUB, HD_B), F32)
            if i:
                past = pl.ds(0, i * SUB)
                b_ref = b_s[i * SUB - 1:i * SUB, :]
                e_q, e_k = jnp.exp(b_i - b_ref), jnp.exp(b_ref - b_s[past, :])
                qs, ks = (q_i * e_q).astype(BF16), (k_s[past, :] * e_k).astype(BF16)
                xi_p, do_b = xi_ref[past, :].astype(BF16), do_i.astype(BF16)
                di_s[past, :] += _dot(_dot(ks, qs, NT).astype(BF16), do_b)
                dq_i += _dot(_dot(do_b, xi_p, NT).astype(BF16), ks) * e_q
                dk_s[past, :] += _dot(_dot(xi_p, do_b, NT).astype(BF16), qs) * e_k
            for s in range(SUB):
                row = pl.ds(i * SUB + s, 1)
                k_row, i_row = k_s[row, :], xi_ref[row, :]
                e = _decay_from(b_i, b_s[row, :], s)
                w = q_i * e
                a_col = jnp.sum(w * k_row, axis=-1, keepdims=True)
                da_col = jnp.sum(do_i * i_row, axis=-1, keepdims=True)
                di_s[row, :] += _colsum(a_col * do_i)
                dq_i += da_col * e * k_row
                dk_s[row, :] += _colsum(da_col * w)
            dq_s[rows, :] += dq_i
        dq, dk = dq_s[...], dk_s[...]
        db = q * dq - k * dk
        is_last = lax.broadcasted_iota(jnp.int32, (CHUNK, HD_B), 0) == CHUNK - 1
        db = db + jnp.where(is_last, d_b_last, 0.0)
        df = _dot(_tri(CHUNK, False), db, precision=HIGHEST) / f - dk
        dzf_ref[...] = (df * (1.0 - lb) * sig * (1.0 - sig)).astype(dzf_ref.dtype)
        dl0_ref[...] += _colsum(df * (1.0 - sig)) * (lb * (1.0 - lb))
        zq = zq_ref[...]
        dzq_ref[...] = (dq * sq * (1.0 + zq * (1.0 - sq))).astype(dzq_ref.dtype)
        dxi_ref[...] = di_s[...].astype(dxi_ref.dtype)

    rev = lambda n: nC - 1 - n
    col = lambda part: pl.BlockSpec((CHUNK, wide), lambda g, n: (rev(n), col0 + part * G + g))
    blk = pl.BlockSpec((CHUNK, wide), lambda g, n: (rev(n), g))
    tile = pltpu.VMEM((HGRN_HEADS, CHUNK, HD_B), F32)
    out_big = jax.ShapeDtypeStruct((T, W), BF16)
    return _call(
        body, rider, name="hgrn_bwd", grid=(G, nC),
        in_specs=[col(0), col(1), col(2), col(3), pl.BlockSpec((2, wide), lambda g, n: (0, g)),
                  pl.BlockSpec((1, HD_B), lambda g, n: (0, 0)), blk,
                  pl.BlockSpec((HGRN_HEADS, None, HD_B, HD_B), lambda g, n: (g, rev(n), 0, 0)),
                  pl.BlockSpec((CHUNK, wide), lambda g, n: (rev(n), dcol0 + g))],
        out_specs=[blk, blk, blk, blk, pl.BlockSpec((1, wide), lambda g, n: (0, g)),
                   pl.BlockSpec((1, HD_B), lambda g, n: (0, 0))],
        out_shape=[out_big, out_big, out_big, out_big, jax.ShapeDtypeStruct((1, W), F32),
                   jax.ShapeDtypeStruct((1, HD_B), F32)],
        scratch_shapes=[pltpu.VMEM((HGRN_HEADS, HD_B, HD_B), F32)] + [tile] * 7,
        compiler_params=_params(("arbitrary", "arbitrary")),
        operands=(proj, proj, proj, proj, lb_logits, gnorm_g, o_b, st_all, dmixin))


def _adamw_math(g, w, m, v):
    m = B1 * m + (1.0 - B1) * g
    v = B2 * v + (1.0 - B2) * (g * g)
    m_hat = m / (1.0 - B1 ** STEP)
    v_hat = v / (1.0 - B2 ** STEP)
    return -LR * (m_hat / (jnp.sqrt(v_hat) + ADAM_EPS) + WD * w), m, v


def _adamw(g, w, m, v, name):
    R, C = g.shape
    tr = _row_tile(R, C)

    def body(g_ref, w_ref, m_ref, v_ref, d_ref, mo_ref, vo_ref):
        d_ref[...], mo_ref[...], vo_ref[...] = _adamw_math(g_ref[...], w_ref[...], m_ref[...], v_ref[...])

    blk = pl.BlockSpec((tr, C), lambda i: (i, 0))
    return pl.pallas_call(
        body, name=name, grid=(R // tr,), in_specs=[blk] * 4, out_specs=[blk] * 3,
        out_shape=[jax.ShapeDtypeStruct((R, C), F32)] * 3, compiler_params=_params(("parallel",), 40),
    )(g, w, m, v)


def _sum_pair(g_full, from_sibling, sel, name):
    Q, K, Ns = g_full.shape
    kh = K // 2
    tr = _row_tile(kh, Ns)
    nh = kh // tr

    def body(sel_ref, a_ref, b_ref, o_ref):
        o_ref[...] = (a_ref[...].astype(F32) + b_ref[...].astype(F32)).astype(o_ref.dtype)

    return pl.pallas_call(
        body, name=name,
        grid_spec=pltpu.PrefetchScalarGridSpec(
            num_scalar_prefetch=1, grid=(Q, nh),
            in_specs=[pl.BlockSpec((None, tr, Ns), lambda q, i, sel: (q, sel[1] * nh + i, 0)),
                      pl.BlockSpec((None, tr, Ns), lambda q, i, sel: (q, i, 0))],
            out_specs=pl.BlockSpec((None, tr, Ns), lambda q, i, sel: (q, i, 0))),
        out_shape=jax.ShapeDtypeStruct((Q, kh, Ns), BF16), compiler_params=_params(("parallel", "parallel")),
    )(sel, g_full, from_sibling)


def _sum_chips(pair_sum, from_chips, sel, name):
    Q, kh, Ns = pair_sum.shape
    tr = _row_tile(kh, Ns)
    nh = kh // tr

    def body(sel_ref, a_ref, b0_ref, b1_ref, b2_ref, o_ref):
        up = lambda r: r[...].astype(F32)
        o_ref[...] = ((up(a_ref) + up(b0_ref)) + up(b1_ref)) + up(b2_ref)

    recv = lambda k: pl.BlockSpec((None, tr, Ns), lambda i, sel: (k, i, 0))
    return pl.pallas_call(
        body, name=name,
        grid_spec=pltpu.PrefetchScalarGridSpec(
            num_scalar_prefetch=1, grid=(nh,),
            in_specs=[pl.BlockSpec((None, tr, Ns), lambda i, sel: (sel[0], i, 0)), recv(0), recv(1), recv(2)],
            out_specs=pl.BlockSpec((tr, Ns), lambda i, sel: (sel[1] * nh + i, 0))),
        out_shape=jax.ShapeDtypeStruct((2 * kh, Ns), F32), compiler_params=_params(("parallel",)),
    )(sel, pair_sum, from_chips, from_chips, from_chips)


def _gather_small(v, name):
    R, L = v.shape

    def body(v_ref, out_ref, send_sems, recv_sems):
        x, y, c = _place()
        me = 4 * x + 2 * y + c
        out_ref[me] = v_ref[...]
        peers = [(_flip(x, k >> 2 & 1), _flip(y, k >> 1 & 1), _flip(c, k & 1)) for k in range(1, N_DEV)]

        def copy(k, row, to):
            return pltpu.make_async_remote_copy(src_ref=v_ref, dst_ref=out_ref.at[row], send_sem=send_sems.at[k],
                                                recv_sem=recv_sems.at[k], device_id=to, device_id_type=MESH)

        sends = [copy(k, me, peer) for k, peer in enumerate(peers)]
        for cp in sends:
            cp.start()
        for k, (px, py, pc) in enumerate(peers):
            copy(k, 4 * px + 2 * py + pc, (x, y, c)).wait_recv()
        for cp in sends:
            cp.wait_send()

    vmem = pl.BlockSpec(memory_space=pltpu.VMEM)
    return pl.pallas_call(
        body, name=name, in_specs=[vmem], out_specs=vmem, out_shape=jax.ShapeDtypeStruct((N_DEV, R, L), F32),
        scratch_shapes=[pltpu.SemaphoreType.DMA((N_DEV - 1,)), pltpu.SemaphoreType.DMA((N_DEV - 1,))],
    )(v)


def _share_halves(block, name):
    K, Ns = block.shape
    kh = K // 2

    def body(_, out_ref, send_sem, recv_sem):
        x, y, c = _place()
        mine, theirs = out_ref.at[pl.ds(c * kh, kh), :], out_ref.at[pl.ds((1 - c) * kh, kh), :]
        cp = pltpu.make_async_remote_copy(src_ref=mine, dst_ref=mine, send_sem=send_sem, recv_sem=recv_sem,
                                          device_id=(x, y, 1 - c), device_id_type=MESH)
        cp.start()
        pltpu.make_async_remote_copy(src_ref=theirs, dst_ref=theirs, send_sem=send_sem, recv_sem=recv_sem,
                                     device_id=(x, y, 1 - c), device_id_type=MESH).wait_recv()
        cp.wait_send()

    return pl.pallas_call(
        body, name=name, in_specs=[ANY], out_specs=ANY, out_shape=jax.ShapeDtypeStruct((K, Ns), block.dtype),
        input_output_aliases={0: 0},
        scratch_shapes=[pltpu.SemaphoreType.DMA, pltpu.SemaphoreType.DMA],
    )(block)


def _silu(v):
    return v * jax.nn.sigmoid(v)


def _ada_fwd(c_all, w_ada, tn=512):
    M, D = c_all.shape
    Ns = w_ada.shape[1]

    def body(c_ref, w_ref, o_ref):
        o_ref[...] = _dot(_silu(c_ref[...]).astype(BF16), w_ref[...].astype(BF16))

    return pl.pallas_call(
        body, name="ada_fwd", grid=(Ns // tn,),
        in_specs=[pl.BlockSpec((M, D), lambda j: (0, 0)), pl.BlockSpec((D, tn), lambda j: (0, j))],
        out_specs=pl.BlockSpec((M, tn), lambda j: (0, j)), out_shape=jax.ShapeDtypeStruct((M, Ns), F32),
        compiler_params=_params(("parallel",)),
    )(c_all, w_ada)


def _ada_bwd(c_all, dmod, w, m, v, tk=256, tn=1536):
    M, D = c_all.shape
    Ns = dmod.shape[1]

    def body(c_ref, d_ref, w_ref, m_ref, v_ref, g_ref, dl_ref, mo_ref, vo_ref):
        g = _dot(_silu(c_ref[...]).astype(BF16), d_ref[...].astype(BF16), TN)
        g_ref[...] = g
        dl_ref[...], mo_ref[...], vo_ref[...] = _adamw_math(g, w_ref[...], m_ref[...], v_ref[...])

    blk = pl.BlockSpec((tk, tn), lambda i, j: (i, j))
    return pl.pallas_call(
        body, name="ada_bwd", grid=(D // tk, Ns // tn),
        in_specs=[pl.BlockSpec((M, tk), lambda i, j: (0, i)), pl.BlockSpec((M, tn), lambda i, j: (0, j)), blk, blk, blk],
        out_specs=[blk] * 4, out_shape=[jax.ShapeDtypeStruct((D, Ns), F32)] * 4,
        compiler_params=_params(("parallel", "parallel"), 40),
    )(c_all, dmod, w, m, v)


def _small_update(g_all, w, m, v):
    R, L = w.shape

    def body(g_ref, w_ref, m_ref, v_ref, go_ref, d_ref, mo_ref, vo_ref):
        g = g_ref[0]
        for d in range(1, N_DEV):
            g = g + g_ref[d]
        go_ref[...] = g
        d_ref[...], mo_ref[...], vo_ref[...] = _adamw_math(g, w_ref[...], m_ref[...], v_ref[...])

    return pl.pallas_call(body, name="small_update", out_shape=[jax.ShapeDtypeStruct((R, L), F32)] * 4)(g_all, w, m, v)


def _pack(parts, rows):
    flat = jnp.concatenate([p.reshape(-1) for p in parts])
    return jnp.pad(flat, (0, rows * 128 - flat.shape[0])).reshape(rows, 128)


def _unpack(packed, shapes):
    flat, out, at = packed.reshape(-1), [], 0
    for shp in shapes:
        size = 1
        for d in shp:
            size *= d
        out.append(flat[at:at + size].reshape(shp))
        at += size
    return out


def _layer(x, tgt, mod, wts, rel_bias, attn_norm_g, lb_logits, gnorm_g, ln1_g, ln1_b, ln2_g, ln2_b, place=None):
    T, D = x.shape
    aw = attn_norm_g.shape[1]
    shift1, scale1, gate1, shift2, scale2, gate2 = [mod[i:i + 1] for i in range(6)]

    def gather(n, rows=None, into=None):
        return None if place is None else _gather_rider(wts[n], rows, None if into is None else into[0])

    def gathered(n, rode):
        return wts[n] if place is None else lax.dynamic_update_index_in_dim(rode[0], wts[n], place[0], 0)

    def blocks(g):
        return g.reshape(N_CHIPS, -1, g.shape[2])

    def to_sibling(g):
        return None if place is None else _pair_rider(g)

    def pair_sum(n, g, rode=None):
        if place is None:
            return g
        rode = _alone(_pair_rider(g), n + "_send_pair") if rode is None else rode
        return _sum_pair(g, rode[0], place[1], n + "_sum_pair")

    def to_chips(p, rows=None, into=None):
        return None if place is None else _chips_rider(p, rows, None if into is None else into[0])

    def reduced(n, p, rode):
        return p if place is None else _share_halves(_sum_chips(p, rode[0], place[1], n + "_sum_chips"), n + "_share")

    def carrying(mm, *args, rider, **kw):
        return mm(*args, rider=rider, **kw) if rider is not None else (mm(*args, **kw), None)

    w_in = gathered("w_in", None if place is None else _alone(gather("w_in"), "gather_w_in"))
    h1 = _pre_mixer(x, scale1, shift1)
    n_qkv = 3 * aw // 256
    qkv, rode = carrying(_mm_nn, h1, w_in, tm=1024, tn=256, tk=D, name="proj_qkv", cols=(0, n_qkv), o_dtype=BF16,
                         rider=gather("w_o"))
    w_o3 = gathered("w_o", rode).reshape(1, D, D)
    proj = _mm_nn(h1, w_in, tm=1024, tn=256, tk=D, name="proj_rec", cols=(n_qkv, N_CHIPS * w_in.shape[2] // 256))
    kv = jnp.pad(qkv[:, aw:], ((KPAD, 0), (0, 0)))
    bias = _bias_band(rel_bias)
    cut = 7 * (wts["w_ffn_in"].shape[-2] // 2) // 16
    (mix_a,), rode = _attn_fwd(qkv, kv, bias, attn_norm_g, rider=gather("w_ffn_in", (0, cut)))
    (mix_b, o_b, st_all), rode = _hgrn_fwd(
        proj, lb_logits, gnorm_g, rider=gather("w_ffn_in", (cut, wts["w_ffn_in"].shape[-2] // 2 - cut), rode))
    w_ffn_in = gathered("w_ffn_in", rode)
    mixin = jnp.concatenate([mix_a, mix_b], axis=1)
    mix = _mm_nn(mixin, w_o3, tm=1024, tn=512, tk=D, name="mix_out")
    x1, h2 = _post_mixer(mix, x, gate1, ln1_g, ln1_b, scale2, shift2)
    (gate, up, act), rode = _ffn_in_swiglu(h2, w_ffn_in, tm=1024, tn=256, rider=gather("w_ffn_out"))
    w_out3 = gathered("w_ffn_out", rode)
    w_out3 = w_out3.reshape(1, -1, w_out3.shape[2])
    d_ff = w_out3.shape[1]
    f = _mm_nn(act, w_out3, tm=1024, tn=1024, tk=d_ff // 4, name="ffn_out")
    du2, df, acc2 = _loss_head(f, x1, tgt, gate2, ln2_g, ln2_b)
    loss = (0.5 / D) * jnp.sum(acc2[3])
    g = blocks(_mm_tn(act, df, q=1, tk=512, tn=1024, tt=T, name="g_ffn_out"))
    d_gate_up, rode = _d_act_swiglu(df, w_out3, gate, up, tm=1024, to=512, rider=to_sibling(g))
    p_out = pair_sum("w_ffn_out", g, rode)
    dff = jnp.concatenate(d_gate_up, axis=1)
    dh2, rode = carrying(_mm_nt, dff, w_ffn_in, tm=1024, to=1024, tn=w_ffn_in.shape[2], name="d_h2",
                         rider=to_chips(p_out))
    g_ffn_out = reduced("w_ffn_out", p_out, rode)
    g = _mm_tn(h2, dff, q=N_CHIPS, tk=512, tn=w_ffn_in.shape[2] // 2, tt=T, name="g_ffn_in")
    if place is None:
        (du1, dmix, acc1), p_fin = _mid_bwd(dh2, du2, x1, mix, x, gate1, ln1_g, scale2), g
    else:
        (du1, dmix, acc1), rode = _mid_bwd(dh2, du2, x1, mix, x, gate1, ln1_g, scale2, rider=to_sibling(g))
        p_fin = pair_sum("w_ffn_in", g, rode)
    dmixin = _mm_nt(dmix, w_o3, tm=1024, to=512, tn=D, name="d_mixin")
    p_o = pair_sum("w_o", blocks(_mm_tn(mixin, dmix, q=1, tk=512, tn=1024, tt=T, name="g_o")))
    cut = 19 * p_fin.shape[1] // 32
    (dq, dk, dv, dbias, dgain), rode = _attn_bwd(qkv, kv, bias, attn_norm_g, dmixin, rider=to_chips(p_fin, (0, cut)))
    (dzq, dzf, dxi, dzg, dl0, dgn), rode = _hgrn_bwd(
        proj, lb_logits, gnorm_g, o_b, st_all, dmixin,
        rider=_join(to_chips(p_fin, (cut, p_fin.shape[1] - cut), rode), to_chips(p_o)))
    g_ffn_in = reduced("w_ffn_in", p_fin, rode[:1])
    g_o = reduced("w_o", p_o, rode[1:])
    dproj = jnp.concatenate([dq, dk[KPAD:].astype(BF16), dv[KPAD:].astype(BF16), dzq, dzf, dxi, dzg], axis=1)
    p_in = pair_sum("w_in", _mm_tn(h1, dproj, q=N_CHIPS, tk=512, tn=w_in.shape[2] // 2, tt=T, name="g_in"))
    dh1, rode = carrying(_mm_nt, dproj, w_in, tm=1024, to=1024, tn=w_in.shape[2], name="d_h1", rider=to_chips(p_in))
    g_in = reduced("w_in", p_in, rode)
    grad_x, acc0 = _first_bwd(dh1, du1, x, scale1)
    dmod = jnp.concatenate([acc0[1:2], acc0[0:1], acc1[4:5], acc1[1:2], acc1[0:1], acc2[2:3]], axis=0)
    small = dict(rel_bias=_bias_band_grad(dbias), attn_norm_g=dgain,
                 lb_logits=jnp.concatenate([dl0, -dl0], axis=0), gnorm_g=dgn,
                 ln1_g=acc1[2:3], ln1_b=acc1[3:4], ln2_g=acc2[0:1], ln2_b=acc2[1:2])
    return loss, grad_x, dict(w_in=g_in, w_o=g_o, w_ffn_in=g_ffn_in, w_ffn_out=g_ffn_out), dmod, small


SMALL = ("rel_bias", "attn_norm_g", "lb_logits", "gnorm_g", "ln1_g", "ln1_b", "ln2_g", "ln2_b")
SMALL_ROWS = 256


def kernel(x, c, w_ada, b_ada, w_in, rel_bias, attn_norm_g, lb_logits, gnorm_g, w_o, ln1_g, ln1_b, w_ffn_in, w_ffn_out, ln2_g, ln2_b, loss_target, m_w_ada, m_b_ada, m_w_in, m_rel_bias, m_attn_norm_g, m_lb_logits, m_gnorm_g, m_w_o, m_ln1_g, m_ln1_b, m_w_ffn_in, m_w_ffn_out, m_ln2_g, m_ln2_b, v_w_ada, v_b_ada, v_w_in, v_rel_bias, v_attn_norm_g, v_lb_logits, v_gnorm_g, v_w_o, v_ln1_g, v_ln1_b, v_w_ffn_in, v_w_ffn_out, v_ln2_g, v_ln2_b):
    mx, my, mc = _place()
    me = 4 * mx + 2 * my + mc
    chip = 2 * mx + my
    sel = jnp.stack([chip, mc]).astype(jnp.int32)
    D = x.shape[2]
    ns_ada = w_ada.shape[2]

    big = dict(w_in=(w_in, m_w_in, v_w_in), w_o=(w_o, m_w_o, v_w_o), w_ffn_in=(w_ffn_in, m_w_ffn_in, v_w_ffn_in),
               w_ffn_out=(w_ffn_out, m_w_ffn_out, v_w_ffn_out))
    shards = {n: t[0][0].astype(BF16) for n, t in big.items()}

    c_all = _gather_small(c.reshape(D // 128, 128), "gather_c").reshape(N_DEV, D)
    c_all = jnp.pad(c_all, ((0, 16 - N_DEV), (0, 0)))
    mod_cols = _ada_fwd(c_all, w_ada[0])[:N_DEV]
    mod_all = _gather_small(mod_cols.reshape(-1, 128), "gather_mod").reshape(N_DEV, N_DEV, ns_ada)
    mod = lax.dynamic_index_in_dim(mod_all[::2], me, axis=1, keepdims=False)
    mod = (mod.reshape(1, -1) + b_ada).reshape(6, D)

    loss, grad_x, g_big, dmod, g_small = _layer(
        x[0], loss_target[0], mod, shards, rel_bias[0], attn_norm_g, lb_logits, gnorm_g, ln1_g, ln1_b, ln2_g, ln2_b,
        place=(chip, sel))
    loss = lax.psum(loss, ("x", "y", "c"))

    grads, deltas, new_m, new_v = {}, {}, {}, {}
    for n, (w, m, v) in big.items():
        g = g_big[n]
        d, mo, vo = _adamw(g, w[0], m[0], v[0], "adamw_" + n)
        grads[n], deltas[n], new_m[n], new_v[n] = g[None], d[None], mo[None], vo[None]

    small_in = dict(rel_bias=(rel_bias, m_rel_bias, v_rel_bias), attn_norm_g=(attn_norm_g, m_attn_norm_g, v_attn_norm_g),
                    lb_logits=(lb_logits, m_lb_logits, v_lb_logits), gnorm_g=(gnorm_g, m_gnorm_g, v_gnorm_g),
                    ln1_g=(ln1_g, m_ln1_g, v_ln1_g), ln1_b=(ln1_b, m_ln1_b, v_ln1_b), ln2_g=(ln2_g, m_ln2_g, v_ln2_g),
                    ln2_b=(ln2_b, m_ln2_b, v_ln2_b))
    g_all = _gather_small(_pack([dmod] + [g_small[n] for n in SMALL], SMALL_ROWS), "gather_small")
    packed = [_pack([t] + [small_in[n][i] for n in SMALL], SMALL_ROWS)
              for i, t in enumerate((b_ada, m_b_ada, v_b_ada))]
    shapes = [b_ada.shape] + [small_in[n][0].shape for n in SMALL]
    outs = [_unpack(o, shapes) for o in _small_update(g_all, *packed)]
    for i, n in enumerate(("b_ada",) + SMALL):
        grads[n], deltas[n], new_m[n], new_v[n] = outs[0][i], outs[1][i], outs[2][i], outs[3][i]

    dmod_all = g_all[:, :6 * D // 128].reshape(N_DEV, 6 * D)
    dmod_cols = lax.dynamic_slice_in_dim(dmod_all, chip * ns_ada, ns_ada, axis=1)
    dmod_cols = jnp.pad(dmod_cols, ((0, 16 - N_DEV), (0, 0)))
    g, d, mo, vo = _ada_bwd(c_all, dmod_cols, w_ada[0], m_w_ada[0], v_w_ada[0])
    grads["w_ada"], deltas["w_ada"], new_m["w_ada"], new_v["w_ada"] = g[None], d[None], mo[None], vo[None]

    order = ("w_ada", "b_ada", "w_in", "rel_bias", "attn_norm_g", "lb_logits", "gnorm_g", "w_o", "ln1_g", "ln1_b",
             "w_ffn_in", "w_ffn_out", "ln2_g", "ln2_b")
    return (loss, grad_x[None], *[grads[n] for n in order], *[deltas[n] for n in order],
            *[new_m[n] for n in order], *[new_v[n] for n in order])
```

```python
import numpy as np
import jax
import jax.numpy as jnp
from jax import lax
from jax.experimental import pallas as pl
from jax.experimental.pallas import tpu as pltpu

F32 = jnp.float32
BF16 = jnp.bfloat16
MESH = pl.DeviceIdType.MESH
HIGHEST = lax.Precision.HIGHEST

CHUNK = 64
N_PAST = 8
QG = 4
QROWS = QG * CHUNK
KPAD = N_PAST * CHUNK
UNION = (QG + N_PAST) * CHUNK
BAND = (N_PAST + 1) * CHUNK
HD_A = 64
HD_B = 128
SUB = 16
HGRN_HEADS = 4
MAX_REL = 256
EPS = 1e-5
ALPHA = 2.0 ** 0.25
LR, B1, B2, ADAM_EPS, WD, STEP = 1e-3, 0.9, 0.999, 1e-8, 0.01, 10
N_CHIPS = 4
N_DEV = 8
NEG = -1e30
TILE_BYTES = 3 << 19

NN = ((1,), (0,))
NT = ((1,), (1,))
TN = ((0,), (0,))


def _dot(a, b, dims=NN, precision=None):
    return lax.dot_general(a, b, (dims, ((), ())), preferred_element_type=F32, precision=precision)


def _params(sem=None, vmem_mb=None, **kw):
    return pltpu.CompilerParams(dimension_semantics=sem,
                                vmem_limit_bytes=None if vmem_mb is None else vmem_mb << 20, **kw)


def _row_tile(rows, cols):
    for cand in (512, 256, 128, 64, 32, 16, 8):
        if rows % cand == 0 and cand * cols * 4 <= TILE_BYTES:
            return cand
    raise ValueError((rows, cols))


def _place():
    return lax.axis_index("x"), lax.axis_index("y"), lax.axis_index("c")


def _flip(v, bit):
    return 1 - v if bit else v


ANY = pl.BlockSpec(memory_space=pl.ANY)
CHIP_FLIPS = ((1, 0), (0, 1), (1, 1))


class _Rider:
    def __init__(self, operands, out_shape, n_sems, start, finish, aliases=None):
        self.operands, self.out_shape, self.n_sems, self.start, self.finish = operands, out_shape, n_sems, start, finish
        self.aliases = aliases or {}


def _call(body, rider, *, name, grid, in_specs, out_specs, out_shape, scratch_shapes, compiler_params, operands):
    if rider is None:
        outs = pl.pallas_call(body, name=name, grid=grid, in_specs=in_specs, out_specs=out_specs, out_shape=out_shape,
                              scratch_shapes=scratch_shapes, compiler_params=compiler_params)(*operands)
        return list(outs), []
    n_in, n_out, n_sc = len(in_specs), len(out_specs), len(scratch_shapes)
    r_in, r_out = len(rider.operands), len(rider.out_shape)

    def carried(*refs):
        refs = list(refs)
        cuts = [n_in, r_in, n_out, r_out, n_sc]
        ins, r_ins, outs, r_outs, scratch = [[refs.pop(0) for _ in range(n)] for n in cuts]
        first, last = None, None
        for axis, size in enumerate(grid):
            i = pl.program_id(axis)
            first = (i == 0) if first is None else first & (i == 0)
            last = (i == size - 1) if last is None else last & (i == size - 1)

        @pl.when(first)
        def _():
            rider.start(r_ins, r_outs, *refs)

        body(*ins, *outs, *scratch)

        @pl.when(last)
        def _():
            rider.finish(r_ins, r_outs, *refs)

    sems = [pltpu.SemaphoreType.DMA((rider.n_sems,)), pltpu.SemaphoreType.DMA((rider.n_sems,))]
    outs = pl.pallas_call(carried, name=name, grid=grid, in_specs=list(in_specs) + [ANY] * r_in,
                          out_specs=list(out_specs) + [ANY] * r_out, out_shape=list(out_shape) + rider.out_shape,
                          scratch_shapes=list(scratch_shapes) + sems, compiler_params=compiler_params,
                          input_output_aliases={n_in + i: n_out + o for i, o in rider.aliases.items()},
                          )(*operands, *rider.operands)
    return list(outs[:n_out]), list(outs[n_out:])


def _alone(rider, name):
    def body(*refs):
        ins, outs, sems = refs[:len(rider.operands)], refs[len(rider.operands):-2], refs[-2:]
        rider.start(ins, outs, *sems)
        rider.finish(ins, outs, *sems)

    return pl.pallas_call(
        body, name=name, in_specs=[ANY] * len(rider.operands), out_specs=[ANY] * len(rider.out_shape),
        out_shape=rider.out_shape,
        scratch_shapes=[pltpu.SemaphoreType.DMA((rider.n_sems,)), pltpu.SemaphoreType.DMA((rider.n_sems,))],
    )(*rider.operands)


class _Sems:
    def __init__(self, sems, base):
        self.sems, self.base = sems, base

    @property
    def at(self):
        return self

    def __getitem__(self, k):
        return self.sems.at[self.base + k]


def _join(*riders):
    riders = [r for r in riders if r is not None]
    if len(riders) < 2:
        return riders[0] if riders else None

    def parts(ins, outs, send_sems, recv_sems):
        i = o = s = 0
        for r in riders:
            ni, no = len(r.operands), len(r.out_shape)
            yield r, ins[i:i + ni], outs[o:o + no], _Sems(send_sems, s), _Sems(recv_sems, s)
            i, o, s = i + ni, o + no, s + r.n_sems

    def start(*refs):
        for r, *args in parts(*refs):
            r.start(*args)

    def finish(*refs):
        for r, *args in parts(*refs):
            r.finish(*args)

    aliases, i, o = {}, 0, 0
    for r in riders:
        aliases.update({i + a: o + b for a, b in r.aliases.items()})
        i, o = i + len(r.operands), o + len(r.out_shape)
    return _Rider([a for r in riders for a in r.operands], [s for r in riders for s in r.out_shape],
                  sum(r.n_sems for r in riders), start, finish, aliases)


def _gather_rider(shard, rows=None, into=None):
    K, Ns = shard.shape
    kh = K // 2
    first_row, n_rows = rows or (0, kh)

    def copies(w_ref, out_ref, send_sems, recv_sems):
        x, y, c = _place()
        chips = [(_flip(x, fx), _flip(y, fy)) for fx, fy in CHIP_FLIPS]

        def half(chip, which):
            return out_ref.at[2 * chip[0] + chip[1], pl.ds(which * kh + first_row, n_rows), :]

        def copy(k, dst, to, src=None):
            return pltpu.make_async_remote_copy(src_ref=dst if src is None else src, dst_ref=dst,
                                                send_sem=send_sems.at[k], recv_sem=recv_sems.at[k],
                                                device_id=to, device_id_type=MESH)

        def first():
            return [copy(j, half((x, y), c), (*chip, c), src=w_ref.at[pl.ds(c * kh + first_row, n_rows), :])
                    for j, chip in enumerate(chips)]

        def onward():
            return [copy(3 + j, half(chip, c), (x, y, 1 - c)) for j, chip in enumerate(chips)]

        def arriving(base, which):
            return [copy(base + j, half(chip, which), (x, y, c)) for j, chip in enumerate(chips)]

        return first, onward, arriving

    def start(ins, outs, send_sems, recv_sems):
        for cp in copies(ins[0], outs[0], send_sems, recv_sems)[0]():
            cp.start()

    def finish(ins, outs, send_sems, recv_sems):
        x, y, c = _place()
        first, onward, arriving = copies(ins[0], outs[0], send_sems, recv_sems)
        passed = onward()
        for arrived, cp in zip(arriving(0, c), passed):
            arrived.wait_recv()
            cp.start()
        for arrived in arriving(3, 1 - c):
            arrived.wait_recv()
        for cp in first() + passed:
            cp.wait_send()

    full = jax.ShapeDtypeStruct((N_CHIPS, K, Ns), shard.dtype)
    if into is None:
        return _Rider([shard], [full], 6, start, finish)
    return _Rider([shard, into], [full], 6, start, finish, aliases={1: 0})


def _pair_rider(g_full):
    Q, K, Ns = g_full.shape
    kh = K // 2

    def copy(g_ref, got_ref, send_sems, recv_sems):
        x, y, c = _place()
        return pltpu.make_async_remote_copy(src_ref=g_ref.at[:, pl.ds((1 - c) * kh, kh), :], dst_ref=got_ref,
                                            send_sem=send_sems.at[0], recv_sem=recv_sems.at[0],
                                            device_id=(x, y, 1 - c), device_id_type=MESH)

    def start(ins, outs, send_sems, recv_sems):
        copy(ins[0], outs[0], send_sems, recv_sems).start()

    def finish(ins, outs, send_sems, recv_sems):
        copy(ins[0], outs[0], send_sems, recv_sems).wait()

    return _Rider([g_full], [jax.ShapeDtypeStruct((Q, kh, Ns), g_full.dtype)], 1, start, finish)


def _chips_rider(pair_sum, rows=None, into=None):
    Q, kh, Ns = pair_sum.shape
    first_row, n_rows = rows or (0, kh)

    def copies(p_ref, got_ref, send_sems, recv_sems):
        x, y, c = _place()
        part = pl.ds(first_row, n_rows)
        out = []
        for j, (fx, fy) in enumerate(CHIP_FLIPS):
            px, py = _flip(x, fx), _flip(y, fy)
            out.append(pltpu.make_async_remote_copy(
                src_ref=p_ref.at[2 * px + py, part, :], dst_ref=got_ref.at[j, part, :], send_sem=send_sems.at[j],
                recv_sem=recv_sems.at[j], device_id=(px, py, c), device_id_type=MESH))
        return out

    def start(ins, outs, send_sems, recv_sems):
        for cp in copies(ins[0], outs[0], send_sems, recv_sems):
            cp.start()

    def finish(ins, outs, send_sems, recv_sems):
        sends = copies(ins[0], outs[0], send_sems, recv_sems)
        for cp in sends:
            cp.wait_recv()
        for cp in sends:
            cp.wait_send()

    got = jax.ShapeDtypeStruct((Q - 1, kh, Ns), pair_sum.dtype)
    if into is None:
        return _Rider([pair_sum], [got], 3, start, finish)
    return _Rider([pair_sum, into], [got], 3, start, finish, aliases={1: 0})


def _mm(a, b, *, grid, a_spec, b_spec, o_spec, o_shape, o_dtype, dims, acc_shape, name, rider=None, vmem_mb=48):
    nk = grid[2]

    def body(a_ref, b_ref, o_ref, *scratch):
        part = _dot(a_ref[...], b_ref[...], dims)
        if nk == 1:
            o_ref[...] = part.astype(o_ref.dtype)
            return
        acc_ref, = scratch
        k = pl.program_id(2)

        @pl.when(k == 0)
        def _():
            acc_ref[...] = part

        @pl.when(k > 0)
        def _():
            acc_ref[...] += part

        @pl.when(k == nk - 1)
        def _():
            o_ref[...] = acc_ref[...].astype(o_ref.dtype)

    (out,), rode = _call(
        body, rider, name=name, grid=grid, in_specs=[a_spec, b_spec], out_specs=[o_spec],
        out_shape=[jax.ShapeDtypeStruct(o_shape, o_dtype)],
        scratch_shapes=[] if nk == 1 else [pltpu.VMEM(acc_shape, F32)],
        compiler_params=_params(("parallel", "parallel", "arbitrary") if rider is None else ("arbitrary",) * 3, vmem_mb),
        operands=(a, b))
    return out if rider is None else (out, rode)


def _mm_nn(a, w, *, tm, tn, tk, name, rider=None, cols=None, o_dtype=F32):
    T, K = a.shape
    Q, _, Ns = w.shape
    nbs = Ns // tn
    tm = min(tm, T)
    j0, j1 = cols or (0, Q * nbs)
    return _mm(a, w, grid=(T // tm, j1 - j0, K // tk),
               a_spec=pl.BlockSpec((tm, tk), lambda i, j, k: (i, k)),
               b_spec=pl.BlockSpec((None, tk, tn), lambda i, j, k: ((j + j0) // nbs, k, (j + j0) % nbs)),
               o_spec=pl.BlockSpec((tm, tn), lambda i, j, k: (i, j)),
               o_shape=(T, (j1 - j0) * tn), o_dtype=o_dtype, dims=NN, acc_shape=(tm, tn), name=name, rider=rider)


def _mm_nt(g, w, *, tm, to, tn, name, rider=None):
    T = g.shape[0]
    Q, K, Ns = w.shape
    nbs = Ns // tn
    tm = min(tm, T)
    return _mm(g, w, grid=(T // tm, K // to, Q * nbs),
               a_spec=pl.BlockSpec((tm, tn), lambda i, j, n: (i, n)),
               b_spec=pl.BlockSpec((None, to, tn), lambda i, j, n: (n // nbs, j, n % nbs)),
               o_spec=pl.BlockSpec((tm, to), lambda i, j, n: (i, j)),
               o_shape=(T, K), o_dtype=F32, dims=NT, acc_shape=(tm, to), name=name, rider=rider)


def _mm_tn(a, g, *, q, tk, tn, tt, name, rider=None):
    T, K = a.shape
    Ns = g.shape[1] // q
    nbs = Ns // tn
    return _mm(a, g, grid=(K // tk, q * nbs, T // tt),
               a_spec=pl.BlockSpec((tt, tk), lambda i, j, t: (t, i)),
               b_spec=pl.BlockSpec((tt, tn), lambda i, j, t: (t, j)),
               o_spec=pl.BlockSpec((None, tk, tn), lambda i, j, t: (j // nbs, i, j % nbs)),
               o_shape=(q, K, Ns), o_dtype=BF16, dims=TN, acc_shape=(tk, tn), name=name, rider=rider)


def _ln(u):
    mu = jnp.mean(u, axis=-1, keepdims=True)
    d = u - mu
    r = lax.rsqrt(jnp.mean(d * d, axis=-1, keepdims=True) + EPS)
    return d * r, r


def _ln_bwd(dy, un, r):
    return r * (dy - jnp.mean(dy, axis=-1, keepdims=True) - un * jnp.mean(dy * un, axis=-1, keepdims=True))


def _colsum(v):
    return jnp.sum(v, axis=0, keepdims=True)


def _rowwise(name, fn, bigs, vecs, out_dtypes, n_acc, tm=128, rider=None):
    T, D = bigs[0].shape
    nb, nv, no = len(bigs), len(vecs), len(out_dtypes)

    def body(*refs):
        outs, accs = fn([r[...] for r in refs[:nb]], [r[...] for r in refs[nb:nb + nv]])
        for r, o in zip(refs[nb + nv:nb + nv + no], outs):
            r[...] = o.astype(r.dtype)
        if n_acc:
            acc_ref = refs[nb + nv + no]

            @pl.when(pl.program_id(0) == 0)
            def _():
                acc_ref[...] = jnp.zeros_like(acc_ref)

            for row, a in enumerate(accs):
                acc_ref[row:row + 1, :] += a

    big_spec = pl.BlockSpec((tm, D), lambda i: (i, 0))
    vec_spec = pl.BlockSpec((1, D), lambda i: (0, 0))
    out_shape = [jax.ShapeDtypeStruct((T, D), dt) for dt in out_dtypes]
    out_specs = [big_spec] * no
    if n_acc:
        out_shape.append(jax.ShapeDtypeStruct((8, D), F32))
        out_specs.append(pl.BlockSpec((8, D), lambda i: (0, 0)))
    outs, rode = _call(
        body, rider, name=name, grid=(T // tm,), in_specs=[big_spec] * nb + [vec_spec] * nv,
        out_specs=out_specs, out_shape=out_shape, scratch_shapes=[],
        compiler_params=_params(("arbitrary",), 48), operands=(*bigs, *vecs))
    return outs if rider is None else (outs, rode)


def _pre_mixer(x, scale1, shift1):
    def fn(b, v):
        xn, _ = _ln(b[0])
        return [xn * (1.0 + v[0]) + v[1]], []
    return _rowwise("pre_mixer", fn, [x], [scale1, shift1], [BF16], 0)[0]


def _post_mixer(mix, x, gate1, g1, b1, scale2, shift2, rider=None):
    def fn(b, v):
        un1, _ = _ln(ALPHA * b[1] + v[0] * b[0])
        x1 = un1 * v[1] + v[2]
        xn1, _ = _ln(x1)
        return [x1, xn1 * (1.0 + v[3]) + v[4]], []
    return _rowwise("post_mixer", fn, [mix, x], [gate1, g1, b1, scale2, shift2], [F32, BF16], 0, rider=rider)


def _loss_head(f, x1, tgt, gate2, g2, b2):
    def fn(b, v):
        ff, xx, tt = b
        d_model = ff.shape[-1]
        un2, r2 = _ln(ALPHA * xx + v[0] * ff)
        err = un2 * v[1] + v[2] - tt
        dy = err * (1.0 / d_model)
        du2 = _ln_bwd(dy * v[1], un2, r2)
        return [du2, du2 * v[0]], [_colsum(dy * un2), _colsum(dy), _colsum(du2 * ff), _colsum(err * err)]
    return _rowwise("loss_head", fn, [f, x1, tgt], [gate2, g2, b2], [F32, BF16], 4)


def _mid_bwd(dh2, du2, x1, mix, x, gate1, g1, scale2, rider=None):
    def fn(b, v):
        dh, du, xx1, mm, xx = b
        xn1, r1n = _ln(xx1)
        dx1 = ALPHA * du + _ln_bwd(dh * (1.0 + v[2]), xn1, r1n)
        un1, r1 = _ln(ALPHA * xx + v[0] * mm)
        du1 = _ln_bwd(dx1 * v[1], un1, r1)
        return [du1, du1 * v[0]], [_colsum(dh * xn1), _colsum(dh), _colsum(dx1 * un1), _colsum(dx1),
                                   _colsum(du1 * mm)]
    return _rowwise("mid_bwd", fn, [dh2, du2, x1, mix, x], [gate1, g1, scale2], [F32, BF16], 5, rider=rider)


def _first_bwd(dh1, du1, x, scale1):
    def fn(b, v):
        dh, du, xx = b
        xn, r0 = _ln(xx)
        return [ALPHA * du + _ln_bwd(dh * (1.0 + v[0]), xn, r0)], [_colsum(dh * xn), _colsum(dh)]
    return _rowwise("first_bwd", fn, [dh1, du1, x], [scale1], [F32], 2)


def _ffn_in_swiglu(h2, w, *, tm, tn, rider=None):
    T, K = h2.shape
    Q, _, Ns = w.shape
    nbs = Ns // tn
    half = Q * nbs // 2
    tm = min(tm, T)

    def body(a_ref, wg_ref, wu_ref, g_ref, u_ref, act_ref):
        a = a_ref[...]
        g, u = _dot(a, wg_ref[...]), _dot(a, wu_ref[...])
        g_ref[...] = g.astype(g_ref.dtype)
        u_ref[...] = u.astype(u_ref.dtype)
        act_ref[...] = (g * jax.nn.sigmoid(g) * u).astype(act_ref.dtype)

    cols = lambda first: pl.BlockSpec((None, K, tn), lambda i, j: ((j + first) // nbs, 0, (j + first) % nbs))
    blk = pl.BlockSpec((tm, tn), lambda i, j: (i, j))
    return _call(
        body, rider, name="ffn_in", grid=(T // tm, half),
        in_specs=[pl.BlockSpec((tm, K), lambda i, j: (i, 0)), cols(0), cols(half)], out_specs=[blk] * 3,
        out_shape=[jax.ShapeDtypeStruct((T, half * tn), BF16)] * 3, scratch_shapes=[],
        compiler_params=_params(("arbitrary", "arbitrary"), 48), operands=(h2, w, w))


def _d_act_swiglu(df, w, gate, up, *, tm, to, rider=None):
    T, N = df.shape
    F = w.shape[1]
    tm = min(tm, T)

    def body(df_ref, w_ref, g_ref, u_ref, dg_ref, du_ref):
        d = _dot(df_ref[...], w_ref[...], NT)
        g = g_ref[...].astype(F32)
        s = jax.nn.sigmoid(g)
        du_ref[...] = (d * g * s).astype(du_ref.dtype)
        dg_ref[...] = (d * u_ref[...].astype(F32) * s * (1.0 + g * (1.0 - s))).astype(dg_ref.dtype)

    blk = pl.BlockSpec((tm, to), lambda i, j: (i, j))
    return _call(
        body, rider, name="d_act", grid=(T // tm, F // to),
        in_specs=[pl.BlockSpec((tm, N), lambda i, j: (i, 0)), pl.BlockSpec((None, to, N), lambda i, j: (0, j, 0)), blk, blk],
        out_specs=[blk, blk], out_shape=[jax.ShapeDtypeStruct((T, F), BF16)] * 2, scratch_shapes=[],
        compiler_params=_params(("arbitrary", "arbitrary"), 48), operands=(df, w, gate, up))


PAIR = 2


def _fill_table(table_ref, band_ref):
    table_ref[...] = jnp.full(table_ref.shape, NEG, F32)
    for e in range(PAIR):
        for g in range(QG):
            table_ref[e, g * CHUNK:(g + 1) * CHUNK, g * CHUNK:g * CHUNK + BAND] = band_ref[e]


def _attn_probs(q_ref, k_ref, bias_ref, e, step):
    start = pl.multiple_of(step * QROWS, QROWS)
    lanes = pl.ds(e * HD_A, HD_A)
    s = _dot(q_ref[:, lanes], k_ref[pl.ds(start, UNION), lanes], NT) * (HD_A ** -0.5) + bias_ref[e]
    col = lax.broadcasted_iota(jnp.int32, s.shape, 1)
    s = jnp.where(col + start >= KPAD, s, NEG)
    p = jnp.exp(s - jnp.max(s, axis=-1, keepdims=True))
    return p / jnp.sum(p, axis=-1, keepdims=True), start


def _attn_specs(T, n_pairs):
    wide = PAIR * HD_A
    per_step = pl.BlockSpec((QROWS, wide), lambda hp, n: (n, hp))
    keys = pl.BlockSpec((KPAD + T, wide), lambda hp, n: (0, hp))
    values = pl.BlockSpec((KPAD + T, wide), lambda hp, n: (0, n_pairs + hp))
    table = pl.BlockSpec((PAIR, CHUNK, BAND), lambda hp, n: (hp, 0, 0))
    vec = pl.BlockSpec((1, wide), lambda hp, n: (0, hp))
    return per_step, keys, values, table, vec


def _probs_spec():
    return pl.BlockSpec((PAIR, QROWS, UNION), lambda hp, n: (hp, n, 0))


def _attn_fwd(qkv, kv, bias, gain, rider=None):
    T = qkv.shape[0]
    W = gain.shape[1]
    n_pairs = W // (PAIR * HD_A)

    def body(q_ref, k_ref, v_ref, band_ref, gain_ref, o_ref, p_ref, table_ref):
        @pl.when(pl.program_id(1) == 0)
        def _():
            _fill_table(table_ref, band_ref)

        for e in range(PAIR):
            lanes = pl.ds(e * HD_A, HD_A)
            p, start = _attn_probs(q_ref, k_ref, table_ref, e, pl.program_id(1))
            p_ref[e] = p.astype(p_ref.dtype)
            o = _dot(p_ref[e], v_ref[pl.ds(start, UNION), lanes])
            rr = lax.rsqrt(jnp.mean(o * o, axis=-1, keepdims=True) + EPS)
            o_ref[:, lanes] = (o * rr * gain_ref[:, lanes]).astype(o_ref.dtype)

    per_step, keys, values, table, vec = _attn_specs(T, n_pairs)
    return _call(
        body, rider, name="attn_fwd", grid=(n_pairs, T // QROWS), in_specs=[per_step, keys, values, table, vec],
        out_specs=[per_step, _probs_spec()],
        out_shape=[jax.ShapeDtypeStruct((T, W), BF16), jax.ShapeDtypeStruct((n_pairs * PAIR, T, UNION), BF16)],
        scratch_shapes=[pltpu.VMEM((PAIR, QROWS, UNION), F32)],
        compiler_params=_params(("arbitrary", "arbitrary"), 40), operands=(qkv, kv, kv, bias, gain))


def _attn_bwd(qkv, kv, probs, gain, dmixin, rider=None):
    T = qkv.shape[0]
    W = gain.shape[1]
    n_pairs = W // (PAIR * HD_A)
    scale = HD_A ** -0.5

    def body(q_ref, k_ref, v_ref, p_ref, gain_ref, don_ref, dq_ref, dk_ref, dv_ref, dband_ref, dgain_ref, dtable_ref):
        n = pl.program_id(1)

        @pl.when(n == 0)
        def _():
            dk_ref[...] = jnp.zeros_like(dk_ref)
            dv_ref[...] = jnp.zeros_like(dv_ref)
            dtable_ref[...] = jnp.zeros_like(dtable_ref)
            dgain_ref[...] = jnp.zeros_like(dgain_ref)

        for e in range(PAIR):
            lanes = pl.ds(e * HD_A, HD_A)
            keys = pl.ds(pl.multiple_of(n * QROWS, QROWS), UNION)
            pb = p_ref[e]
            p = pb.astype(F32)
            vb = v_ref[keys, lanes]
            o = _dot(pb, vb)
            rr = lax.rsqrt(jnp.mean(o * o, axis=-1, keepdims=True) + EPS)
            on = o * rr
            d_on = don_ref[:, lanes]
            dgain_ref[:, lanes] += _colsum(d_on * on)
            dyo = d_on * gain_ref[:, lanes]
            do = rr * (dyo - on * jnp.mean(dyo * on, axis=-1, keepdims=True))
            dob = do.astype(BF16)
            dp = _dot(dob, vb, NT)
            ds = p * (dp - jnp.sum(do * o, axis=-1, keepdims=True))
            dtable_ref[e] += ds
            dsb = ds.astype(BF16)
            dq_ref[:, lanes] = (_dot(dsb, k_ref[keys, lanes]) * scale).astype(dq_ref.dtype)
            dk_ref[keys, lanes] += _dot(dsb, q_ref[:, lanes], TN) * scale
            dv_ref[keys, lanes] += _dot(pb, dob, TN)

        @pl.when(n == T // QROWS - 1)
        def _():
            for e in range(PAIR):
                dband_ref[e] = sum(dtable_ref[e, g * CHUNK:(g + 1) * CHUNK, g * CHUNK:g * CHUNK + BAND]
                                   for g in range(QG))

    per_step, keys, values, table, vec = _attn_specs(T, n_pairs)
    H = n_pairs * PAIR
    return _call(
        body, rider, name="attn_bwd", grid=(n_pairs, T // QROWS),
        in_specs=[per_step, keys, values, _probs_spec(), vec, per_step],
        out_specs=[per_step, keys, keys, table, vec],
        out_shape=[jax.ShapeDtypeStruct((T, W), BF16), jax.ShapeDtypeStruct((KPAD + T, W), F32),
                   jax.ShapeDtypeStruct((KPAD + T, W), F32), jax.ShapeDtypeStruct((H, CHUNK, BAND), F32),
                   jax.ShapeDtypeStruct((1, W), F32)],
        scratch_shapes=[pltpu.VMEM((PAIR, QROWS, UNION), F32)],
        compiler_params=_params(("arbitrary", "arbitrary"), 40),
        operands=(qkv, kv, kv, probs, gain, dmixin))


N_DIAG = CHUNK + BAND - 1


def _bias_band(rel_bias):
    H = rel_bias.shape[0]
    idx = np.clip(BAND - 1 - np.arange(N_DIAG), -MAX_REL, MAX_REL) + MAX_REL
    rolled = rel_bias[:, idx[(np.arange(N_DIAG) + CHUNK - 1) % N_DIAG]]
    flat = jnp.broadcast_to(rolled[:, None, :], (H, CHUNK, N_DIAG)).reshape(H, CHUNK * N_DIAG)
    return flat[:, :CHUNK * (N_DIAG - 1)].reshape(H, CHUNK, N_DIAG - 1)[:, :, :BAND]


def _bias_band_grad(dband):
    H = dband.shape[0]
    skew = jnp.pad(dband, ((0, 0), (0, 0), (CHUNK - 1, 0))).reshape(H, CHUNK * N_DIAG)
    skew = jnp.pad(skew, ((0, 0), (0, CHUNK))).reshape(H, CHUNK, N_DIAG + 1)
    diag = jnp.sum(skew, axis=1)[:, :N_DIAG]
    n_far = BAND - MAX_REL
    far = jnp.sum(diag[:, :n_far], axis=1, keepdims=True)
    near = diag[:, n_far:][:, ::-1]
    zeros = jnp.zeros((H, MAX_REL - (CHUNK - 1)), F32)
    return jnp.concatenate([zeros, near, far], axis=1)


def _tri(n, lower):
    r = lax.broadcasted_iota(jnp.int32, (n, n), 0)
    c = lax.broadcasted_iota(jnp.int32, (n, n), 1)
    return jnp.where((c <= r) if lower else (c >= r), 1.0, 0.0).astype(F32)


def _hgrn_gates(zq_ref, zf_ref, lbl_ref, q_s, k_s, b_s):
    lb = jax.nn.sigmoid(lbl_ref[0:1, :] - lbl_ref[1:2, :])
    zq = zq_ref[...]
    sig = jax.nn.sigmoid(zf_ref[...])
    f = lb + (1.0 - lb) * sig
    sq = jax.nn.sigmoid(zq)
    q_s[...] = zq * sq
    k_s[...] = 1.0 - f
    b_s[...] = _dot(_tri(CHUNK, True), jnp.log(f), precision=HIGHEST)
    return lb, sig, f, sq


def _sub_rows(i):
    return pl.ds(i * SUB, SUB)


def _row_mask(s):
    return lax.broadcasted_iota(jnp.int32, (SUB, HD_B), 0) >= s


def _decay_from(b_sub, b_row, s):
    return jnp.where(_row_mask(s), jnp.exp(jnp.minimum(b_sub - b_row, 0.0)), 0.0)


def _hgrn_fwd(proj, lb_logits, gnorm_g, rider=None):
    T = proj.shape[0]
    nC = T // CHUNK
    W = lb_logits.shape[1]
    G = W // HD_B // HGRN_HEADS
    col0 = (proj.shape[1] - 4 * W) // (HD_B * HGRN_HEADS)
    wide = HGRN_HEADS * HD_B

    def body(*refs):
        @pl.when(pl.program_id(1) == 0)
        def _():
            refs[9][...] = jnp.zeros_like(refs[9])

        for h in range(HGRN_HEADS):
            lanes = pl.ds(h * HD_B, HD_B)
            one_head(*[r.at[:, lanes] for r in refs[:5]], refs[5], *[r.at[:, lanes] for r in refs[6:8]],
                     *[r.at[h] for r in refs[8:]])

    def one_head(zq_ref, zf_ref, xi_ref, zg_ref, lbl_ref, gn_ref, mix_ref, o_ref, stall_ref, st_ref, q_s, k_s, b_s, acc_s):
        _hgrn_gates(zq_ref, zf_ref, lbl_ref, q_s, k_s, b_s)
        q, k, b = q_s[...], k_s[...], b_s[...]
        st = st_ref[...]
        stall_ref[...] = st
        b_last = b_s[CHUNK - 1:CHUNK, :]
        acc_s[...] = _dot((q * jnp.exp(b)).astype(BF16), st.astype(BF16), NT)
        for i in range(CHUNK // SUB):
            rows = _sub_rows(i)
            q_i, b_i = q_s[rows, :], b_s[rows, :]
            acc = jnp.zeros((SUB, HD_B), F32)
            if i:
                past = pl.ds(0, i * SUB)
                b_ref = b_s[i * SUB - 1:i * SUB, :]
                qs = (q_i * jnp.exp(b_i - b_ref)).astype(BF16)
                ks = (k_s[past, :] * jnp.exp(b_ref - b_s[past, :])).astype(BF16)
                acc += _dot(_dot(qs, ks, NT).astype(BF16), xi_ref[past, :].astype(BF16))
            for s in range(SUB):
                row = pl.ds(i * SUB + s, 1)
                w = q_i * _decay_from(b_i, b_s[row, :], s)
                acc += jnp.sum(w * k_s[row, :], axis=-1, keepdims=True) * xi_ref[row, :]
            acc_s[rows, :] += acc
        o = acc_s[...]
        kd = (k * jnp.exp(b_last - b)).astype(BF16)
        st_ref[...] = st * jnp.exp(b_last) + _dot(xi_ref[...].astype(BF16), kd, TN)
        o_ref[...] = o
        zg = zg_ref[...]
        rr = lax.rsqrt(jnp.mean(o * o, axis=-1, keepdims=True) + EPS)
        mix_ref[...] = (o * rr * gn_ref[...] * (zg * jax.nn.sigmoid(zg))).astype(mix_ref.dtype)

    col = lambda part: pl.BlockSpec((CHUNK, wide), lambda g, n: (n, col0 + part * G + g))
    out_blk = pl.BlockSpec((CHUNK, wide), lambda g, n: (n, g))
    tile = pltpu.VMEM((HGRN_HEADS, CHUNK, HD_B), F32)
    return _call(
        body, rider, name="hgrn_fwd", grid=(G, nC),
        in_specs=[col(0), col(1), col(2), col(3), pl.BlockSpec((2, wide), lambda g, n: (0, g)),
                  pl.BlockSpec((1, HD_B), lambda g, n: (0, 0))],
        out_specs=[out_blk, out_blk, pl.BlockSpec((HGRN_HEADS, None, HD_B, HD_B), lambda g, n: (g, n, 0, 0))],
        out_shape=[jax.ShapeDtypeStruct((T, W), BF16), jax.ShapeDtypeStruct((T, W), F32),
                   jax.ShapeDtypeStruct((G * HGRN_HEADS, nC, HD_B, HD_B), F32)],
        scratch_shapes=[pltpu.VMEM((HGRN_HEADS, HD_B, HD_B), F32), tile, tile, tile, tile],
        compiler_params=_params(("arbitrary", "arbitrary")),
        operands=(proj, proj, proj, proj, lb_logits, gnorm_g))


def _hgrn_bwd(proj, lb_logits, gnorm_g, o_b, st_all, dmixin, rider=None):
    T = proj.shape[0]
    nC = T // CHUNK
    W = lb_logits.shape[1]
    G = W // HD_B // HGRN_HEADS
    wide = HGRN_HEADS * HD_B
    col0 = (proj.shape[1] - 4 * W) // wide
    dcol0 = (dmixin.shape[1] - W) // wide

    def body(*refs):
        g, n = pl.program_id(0), pl.program_id(1)
        dl0_ref, dgn_ref, dst_ref = refs[13:16]

        @pl.when(n == 0)
        def _():
            dst_ref[...] = jnp.zeros_like(dst_ref)
            dl0_ref[...] = jnp.zeros_like(dl0_ref)

        @pl.when((n == 0) & (g == 0))
        def _():
            dgn_ref[...] = jnp.zeros_like(dgn_ref)

        for h in range(HGRN_HEADS):
            lanes = pl.ds(h * HD_B, HD_B)
            cut = lambda r: r.at[:, lanes]
            one_head(*[cut(r) for r in refs[:5]], refs[5], cut(refs[6]), refs[7].at[h], cut(refs[8]),
                     *[cut(r) for r in refs[9:14]], dgn_ref, *[r.at[h] for r in refs[15:]])

    def one_head(zq_ref, zf_ref, xi_ref, zg_ref, lbl_ref, gn_ref, o_ref, st_ref, dout_ref,
                 dzq_ref, dzf_ref, dxi_ref, dzg_ref, dl0_ref, dgn_ref, dst_ref, q_s, k_s, b_s, do_s, dq_s, dk_s, di_s):
        lb, sig, f, sq = _hgrn_gates(zq_ref, zf_ref, lbl_ref, q_s, k_s, b_s)
        q, k, b = q_s[...], k_s[...], b_s[...]
        zg, o, dout = zg_ref[...], o_ref[...], dout_ref[...]
        sg = jax.nn.sigmoid(zg)
        rr = lax.rsqrt(jnp.mean(o * o, axis=-1, keepdims=True) + EPS)
        on = o * rr
        gn = gn_ref[...]
        dzg_ref[...] = (dout * on * gn * sg * (1.0 + zg * (1.0 - sg))).astype(dzg_ref.dtype)
        d_on = dout * zg * sg
        dgn_ref[...] += _colsum(d_on * on)
        d_on = d_on * gn
        do = rr * (d_on - on * jnp.mean(d_on * on, axis=-1, keepdims=True))
        do_s[...] = do
        dob = do.astype(BF16)
        st, dst = st_ref[...], dst_ref[...]
        b_last = b_s[CHUNK - 1:CHUNK, :]
        eb, e_last, k_dec = jnp.exp(b), jnp.exp(b_last), jnp.exp(b_last - b)
        qt, kd = q * eb, k * k_dec
        dstb = dst.astype(BF16)
        xib = xi_ref[...].astype(BF16)
        d_kd = _dot(xib, dstb)
        dq_s[...] = _dot(dob, st.astype(BF16)) * eb
        dk_s[...] = d_kd * k_dec
        di_s[...] = _dot(kd.astype(BF16), dstb, NT)
        d_b_last = e_last * _colsum(st * dst) + _colsum(d_kd * kd)
        dst_ref[...] = _dot(dob, qt.astype(BF16), TN) + dst * e_last
        for i in range(CHUNK // SUB):
            rows = _sub_rows(i)
            q_i, b_i, do_i = q_s[rows, :], b_s[rows, :], do_s[rows, :]
            dq_i = jnp.zeros((SUB, HD_B), F32)
            if i:
                past = pl.ds(0, i * SUB)
                b_ref = b_s[i * SUB - 1:i * SUB, :]
                e_q, e_k = jnp.exp(b_i - b_ref), jnp.exp(b_ref - b_s[past, :])
                qs, ks = (q_i * e_q).astype(BF16), (k_s[past, :] * e_k).astype(BF16)
                xi_p, do_b = xi_ref[past, :].astype(BF16), do_i.astype(BF16)
                di_s[past, :] += _dot(_dot(ks, qs, NT).astype(BF16), do_b)
                dq_i += _dot(_dot(do_b, xi_p, NT).astype(BF16), ks) * e_q
                dk_s[past, :] += _dot(_dot(xi_p, do_b, NT).astype(BF16), qs) * e_k
            for s in range(SUB):
                row = pl.ds(i * SUB + s, 1)
                k_row, i_row = k_s[row, :], xi_ref[row, :]
                e = _decay_from(b_i, b_s[row, :], s)
                w = q_i * e
                a_col = jnp.sum(w * k_row, axis=-1, keepdims=True)
                da_col = jnp.sum(do_i * i_row, axis=-1, keepdims=True)
                di_s[row, :] += _colsum(a_col * do_i)
                dq_i += da_col * e * k_row
                dk_s[row, :] += _colsum(da_col * w)
            dq_s[rows, :] += dq_i
        dq, dk = dq_s[...], dk_s[...]
        db = q * dq - k * dk
        is_last = lax.broadcasted_iota(jnp.int32, (CHUNK, HD_B), 0) == CHUNK - 1
        db = db + jnp.where(is_last, d_b_last, 0.0)
        df = _dot(_tri(CHUNK, False), db, precision=HIGHEST) / f - dk
        dzf_ref[...] = (df * (1.0 - lb) * sig * (1.0 - sig)).astype(dzf_ref.dtype)
        dl0_ref[...] += _colsum(df * (1.0 - sig)) * (lb * (1.0 - lb))
        zq = zq_ref[...]
        dzq_ref[...] = (dq * sq * (1.0 + zq * (1.0 - sq))).astype(dzq_ref.dtype)
        dxi_ref[...] = di_s[...].astype(dxi_ref.dtype)

    rev = lambda n: nC - 1 - n
    col = lambda part: pl.BlockSpec((CHUNK, wide), lambda g, n: (rev(n), col0 + part * G + g))
    blk = pl.BlockSpec((CHUNK, wide), lambda g, n: (rev(n), g))
    tile = pltpu.VMEM((HGRN_HEADS, CHUNK, HD_B), F32)
    out_big = jax.ShapeDtypeStruct((T, W), BF16)
    return _call(
        body, rider, name="hgrn_bwd", grid=(G, nC),
        in_specs=[col(0), col(1), col(2), col(3), pl.BlockSpec((2, wide), lambda g, n: (0, g)),
                  pl.BlockSpec((1, HD_B), lambda g, n: (0, 0)), blk,
                  pl.BlockSpec((HGRN_HEADS, None, HD_B, HD_B), lambda g, n: (g, rev(n), 0, 0)),
                  pl.BlockSpec((CHUNK, wide), lambda g, n: (rev(n), dcol0 + g))],
        out_specs=[blk, blk, blk, blk, pl.BlockSpec((1, wide), lambda g, n: (0, g)),
                   pl.BlockSpec((1, HD_B), lambda g, n: (0, 0))],
        out_shape=[out_big, out_big, out_big, out_big, jax.ShapeDtypeStruct((1, W), F32),
                   jax.ShapeDtypeStruct((1, HD_B), F32)],
        scratch_shapes=[pltpu.VMEM((HGRN_HEADS, HD_B, HD_B), F32)] + [tile] * 7,
        compiler_params=_params(("arbitrary", "arbitrary")),
        operands=(proj, proj, proj, proj, lb_logits, gnorm_g, o_b, st_all, dmixin))


def _adamw_math(g, w, m, v):
    m = B1 * m + (1.0 - B1) * g
    v = B2 * v + (1.0 - B2) * (g * g)
    m_hat = m / (1.0 - B1 ** STEP)
    v_hat = v / (1.0 - B2 ** STEP)
    return -LR * (m_hat / (jnp.sqrt(v_hat) + ADAM_EPS) + WD * w), m, v


def _adamw(g, w, m, v, name):
    R, C = g.shape
    tr = _row_tile(R, C)

    def body(g_ref, w_ref, m_ref, v_ref, go_ref, d_ref, mo_ref, vo_ref):
        g = g_ref[...]
        go_ref[...] = g
        d_ref[...], mo_ref[...], vo_ref[...] = _adamw_math(g, w_ref[...], m_ref[...], v_ref[...])

    blk = pl.BlockSpec((tr, C), lambda i: (i, 0))
    return pl.pallas_call(
        body, name=name, grid=(R // tr,), in_specs=[blk] * 4, out_specs=[blk] * 4,
        out_shape=[jax.ShapeDtypeStruct((R, C), F32)] * 4, compiler_params=_params(("parallel",), 40),
    )(g, w, m, v)


def _sum_pair(g_full, from_sibling, sel, name):
    Q, K, Ns = g_full.shape
    kh = K // 2
    tr = _row_tile(kh, Ns)
    nh = kh // tr

    def body(sel_ref, a_ref, b_ref, o_ref):
        o_ref[...] = (a_ref[...].astype(F32) + b_ref[...].astype(F32)).astype(o_ref.dtype)

    return pl.pallas_call(
        body, name=name,
        grid_spec=pltpu.PrefetchScalarGridSpec(
            num_scalar_prefetch=1, grid=(Q, nh),
            in_specs=[pl.BlockSpec((None, tr, Ns), lambda q, i, sel: (q, sel[1] * nh + i, 0)),
                      pl.BlockSpec((None, tr, Ns), lambda q, i, sel: (q, i, 0))],
            out_specs=pl.BlockSpec((None, tr, Ns), lambda q, i, sel: (q, i, 0))),
        out_shape=jax.ShapeDtypeStruct((Q, kh, Ns), BF16), compiler_params=_params(("parallel", "parallel")),
    )(sel, g_full, from_sibling)


def _sum_chips(pair_sum, from_chips, sel, name):
    Q, kh, Ns = pair_sum.shape
    tr = _row_tile(kh, Ns)
    nh = kh // tr

    def body(sel_ref, a_ref, b0_ref, b1_ref, b2_ref, o_ref):
        up = lambda r: r[...].astype(F32)
        o_ref[...] = ((up(a_ref) + up(b0_ref)) + up(b1_ref)) + up(b2_ref)

    recv = lambda k: pl.BlockSpec((None, tr, Ns), lambda i, sel: (k, i, 0))
    return pl.pallas_call(
        body, name=name,
        grid_spec=pltpu.PrefetchScalarGridSpec(
            num_scalar_prefetch=1, grid=(nh,),
            in_specs=[pl.BlockSpec((None, tr, Ns), lambda i, sel: (sel[0], i, 0)), recv(0), recv(1), recv(2)],
            out_specs=pl.BlockSpec((tr, Ns), lambda i, sel: (sel[1] * nh + i, 0))),
        out_shape=jax.ShapeDtypeStruct((2 * kh, Ns), F32), compiler_params=_params(("parallel",)),
    )(sel, pair_sum, from_chips, from_chips, from_chips)


def _gather_small(v, name):
    R, L = v.shape

    def body(v_ref, out_ref, send_sems, recv_sems):
        x, y, c = _place()
        me = 4 * x + 2 * y + c
        out_ref[me] = v_ref[...]
        peers = [(_flip(x, k >> 2 & 1), _flip(y, k >> 1 & 1), _flip(c, k & 1)) for k in range(1, N_DEV)]

        def copy(k, row, to):
            return pltpu.make_async_remote_copy(src_ref=v_ref, dst_ref=out_ref.at[row], send_sem=send_sems.at[k],
                                                recv_sem=recv_sems.at[k], device_id=to, device_id_type=MESH)

        sends = [copy(k, me, peer) for k, peer in enumerate(peers)]
        for cp in sends:
            cp.start()
        for k, (px, py, pc) in enumerate(peers):
            copy(k, 4 * px + 2 * py + pc, (x, y, c)).wait_recv()
        for cp in sends:
            cp.wait_send()

    vmem = pl.BlockSpec(memory_space=pltpu.VMEM)
    return pl.pallas_call(
        body, name=name, in_specs=[vmem], out_specs=vmem, out_shape=jax.ShapeDtypeStruct((N_DEV, R, L), F32),
        scratch_shapes=[pltpu.SemaphoreType.DMA((N_DEV - 1,)), pltpu.SemaphoreType.DMA((N_DEV - 1,))],
    )(v)


def _share_halves(block, name):
    K, Ns = block.shape
    kh = K // 2

    def body(_, out_ref, send_sem, recv_sem):
        x, y, c = _place()
        mine, theirs = out_ref.at[pl.ds(c * kh, kh), :], out_ref.at[pl.ds((1 - c) * kh, kh), :]
        cp = pltpu.make_async_remote_copy(src_ref=mine, dst_ref=mine, send_sem=send_sem, recv_sem=recv_sem,
                                          device_id=(x, y, 1 - c), device_id_type=MESH)
        cp.start()
        pltpu.make_async_remote_copy(src_ref=theirs, dst_ref=theirs, send_sem=send_sem, recv_sem=recv_sem,
                                     device_id=(x, y, 1 - c), device_id_type=MESH).wait_recv()
        cp.wait_send()

    return pl.pallas_call(
        body, name=name, in_specs=[ANY], out_specs=ANY, out_shape=jax.ShapeDtypeStruct((K, Ns), block.dtype),
        input_output_aliases={0: 0},
        scratch_shapes=[pltpu.SemaphoreType.DMA, pltpu.SemaphoreType.DMA],
    )(block)


def _silu(v):
    return v * jax.nn.sigmoid(v)


def _ada_fwd(c_all, w_ada, tn=512):
    M, D = c_all.shape
    Ns = w_ada.shape[1]

    def body(c_ref, w_ref, o_ref):
        o_ref[...] = _dot(_silu(c_ref[...]).astype(BF16), w_ref[...].astype(BF16))

    return pl.pallas_call(
        body, name="ada_fwd", grid=(Ns // tn,),
        in_specs=[pl.BlockSpec((M, D), lambda j: (0, 0)), pl.BlockSpec((D, tn), lambda j: (0, j))],
        out_specs=pl.BlockSpec((M, tn), lambda j: (0, j)), out_shape=jax.ShapeDtypeStruct((M, Ns), F32),
        compiler_params=_params(("parallel",)),
    )(c_all, w_ada)


def _ada_bwd(c_all, dmod, w, m, v, tk=256, tn=1536):
    M, D = c_all.shape
    Ns = dmod.shape[1]

    def body(c_ref, d_ref, w_ref, m_ref, v_ref, g_ref, dl_ref, mo_ref, vo_ref):
        g = _dot(_silu(c_ref[...]).astype(BF16), d_ref[...].astype(BF16), TN)
        g_ref[...] = g
        dl_ref[...], mo_ref[...], vo_ref[...] = _adamw_math(g, w_ref[...], m_ref[...], v_ref[...])

    blk = pl.BlockSpec((tk, tn), lambda i, j: (i, j))
    return pl.pallas_call(
        body, name="ada_bwd", grid=(D // tk, Ns // tn),
        in_specs=[pl.BlockSpec((M, tk), lambda i, j: (0, i)), pl.BlockSpec((M, tn), lambda i, j: (0, j)), blk, blk, blk],
        out_specs=[blk] * 4, out_shape=[jax.ShapeDtypeStruct((D, Ns), F32)] * 4,
        compiler_params=_params(("parallel", "parallel"), 40),
    )(c_all, dmod, w, m, v)


def _small_update(g_all, w, m, v):
    R, L = w.shape

    def body(g_ref, w_ref, m_ref, v_ref, go_ref, d_ref, mo_ref, vo_ref):
        g = g_ref[0]
        for d in range(1, N_DEV):
            g = g + g_ref[d]
        go_ref[...] = g
        d_ref[...], mo_ref[...], vo_ref[...] = _adamw_math(g, w_ref[...], m_ref[...], v_ref[...])

    return pl.pallas_call(body, name="small_update", out_shape=[jax.ShapeDtypeStruct((R, L), F32)] * 4)(g_all, w, m, v)


def _pack(parts, rows):
    flat = jnp.concatenate([p.reshape(-1) for p in parts])
    return jnp.pad(flat, (0, rows * 128 - flat.shape[0])).reshape(rows, 128)


def _unpack(packed, shapes):
    flat, out, at = packed.reshape(-1), [], 0
    for shp in shapes:
        size = 1
        for d in shp:
            size *= d
        out.append(flat[at:at + size].reshape(shp))
        at += size
    return out


def _layer(x, tgt, mod, wts, rel_bias, attn_norm_g, lb_logits, gnorm_g, ln1_g, ln1_b, ln2_g, ln2_b, place=None):
    T, D = x.shape
    aw = attn_norm_g.shape[1]
    shift1, scale1, gate1, shift2, scale2, gate2 = [mod[i:i + 1] for i in range(6)]

    def gather(n, rows=None, into=None):
        return None if place is None else _gather_rider(wts[n], rows, None if into is None else into[0])

    def gathered(n, rode):
        return wts[n] if place is None else lax.dynamic_update_index_in_dim(rode[0], wts[n], place[0], 0)

    def blocks(g):
        return g.reshape(N_CHIPS, -1, g.shape[2])

    def to_sibling(g):
        return None if place is None else _pair_rider(g)

    def pair_sum(n, g, rode=None):
        if place is None:
            return g
        rode = _alone(_pair_rider(g), n + "_send_pair") if rode is None else rode
        return _sum_pair(g, rode[0], place[1], n + "_sum_pair")

    def to_chips(p, rows=None, into=None):
        return None if place is None else _chips_rider(p, rows, None if into is None else into[0])

    def reduced(n, p, rode):
        return p if place is None else _share_halves(_sum_chips(p, rode[0], place[1], n + "_sum_chips"), n + "_share")

    def carrying(mm, *args, rider, **kw):
        return mm(*args, rider=rider, **kw) if rider is not None else (mm(*args, **kw), None)

    w_in = gathered("w_in", None if place is None else _alone(gather("w_in"), "gather_w_in"))
    h1 = _pre_mixer(x, scale1, shift1)
    n_qkv = 3 * aw // 256
    kh_o, kh_f = wts["w_o"].shape[-2] // 2, wts["w_ffn_in"].shape[-2] // 2
    o_cut, f_cuts = 3 * kh_o // 8, (25 * kh_f // 64, 57 * kh_f // 64)
    qkv, rode = carrying(_mm_nn, h1, w_in, tm=1024, tn=256, tk=D, name="proj_qkv", cols=(0, n_qkv), o_dtype=BF16,
                         rider=gather("w_o", (0, o_cut)))
    proj, rode = carrying(_mm_nn, h1, w_in, tm=1024, tn=256, tk=D, name="proj_rec",
                          cols=(n_qkv, N_CHIPS * w_in.shape[2] // 256), rider=gather("w_o", (o_cut, kh_o - o_cut), rode))
    w_o3 = gathered("w_o", rode).reshape(1, D, D)
    kv = jnp.pad(qkv[:, aw:], ((KPAD, 0), (0, 0)))
    bias = _bias_band(rel_bias)
    (mix_a, probs), rode = _attn_fwd(qkv, kv, bias, attn_norm_g, rider=gather("w_ffn_in", (0, f_cuts[0])))
    (mix_b, o_b, st_all), rode = _hgrn_fwd(
        proj, lb_logits, gnorm_g, rider=gather("w_ffn_in", (f_cuts[0], f_cuts[1] - f_cuts[0]), rode))
    mixin = jnp.concatenate([mix_a, mix_b], axis=1)
    mix = _mm_nn(mixin, w_o3, tm=1024, tn=512, tk=D, name="mix_out")
    if place is None:
        x1, h2 = _post_mixer(mix, x, gate1, ln1_g, ln1_b, scale2, shift2)
    else:
        (x1, h2), rode = _post_mixer(mix, x, gate1, ln1_g, ln1_b, scale2, shift2,
                                     rider=gather("w_ffn_in", (f_cuts[1], kh_f - f_cuts[1]), rode))
    w_ffn_in = gathered("w_ffn_in", rode)
    (gate, up, act), rode = _ffn_in_swiglu(h2, w_ffn_in, tm=1024, tn=256, rider=gather("w_ffn_out"))
    w_out3 = gathered("w_ffn_out", rode)
    w_out3 = w_out3.reshape(1, -1, w_out3.shape[2])
    d_ff = w_out3.shape[1]
    f = _mm_nn(act, w_out3, tm=1024, tn=1024, tk=d_ff // 4, name="ffn_out")
    du2, df, acc2 = _loss_head(f, x1, tgt, gate2, ln2_g, ln2_b)
    loss = (0.5 / D) * jnp.sum(acc2[3])
    g = blocks(_mm_tn(act, df, q=1, tk=512, tn=1024, tt=T, name="g_ffn_out"))
    d_gate_up, rode = _d_act_swiglu(df, w_out3, gate, up, tm=1024, to=512, rider=to_sibling(g))
    p_out = pair_sum("w_ffn_out", g, rode)
    dff = jnp.concatenate(d_gate_up, axis=1)
    cut = 21 * p_out.shape[1] // 44
    dh2, rode = carrying(_mm_nt, dff, w_ffn_in, tm=1024, to=1024, tn=w_ffn_in.shape[2], name="d_h2",
                         rider=to_chips(p_out, (0, cut)))
    g, rode = carrying(_mm_tn, h2, dff, q=N_CHIPS, tk=512, tn=w_ffn_in.shape[2] // 2, tt=T, name="g_ffn_in",
                       rider=to_chips(p_out, (cut, p_out.shape[1] - cut), rode))
    g_ffn_out = reduced("w_ffn_out", p_out, rode)
    if place is None:
        (du1, dmix, acc1), p_fin = _mid_bwd(dh2, du2, x1, mix, x, gate1, ln1_g, scale2), g
    else:
        (du1, dmix, acc1), rode = _mid_bwd(dh2, du2, x1, mix, x, gate1, ln1_g, scale2, rider=to_sibling(g))
        p_fin = pair_sum("w_ffn_in", g, rode)
    dmixin = _mm_nt(dmix, w_o3, tm=1024, to=512, tn=D, name="d_mixin")
    p_o = pair_sum("w_o", blocks(_mm_tn(mixin, dmix, q=1, tk=512, tn=1024, tt=T, name="g_o")))
    cut = p_fin.shape[1] // 2
    (dq, dk, dv, dbias, dgain), rode = _attn_bwd(qkv, kv, probs, attn_norm_g, dmixin, rider=to_chips(p_fin, (0, cut)))
    (dzq, dzf, dxi, dzg, dl0, dgn), rode = _hgrn_bwd(
        proj, lb_logits, gnorm_g, o_b, st_all, dmixin,
        rider=_join(to_chips(p_fin, (cut, p_fin.shape[1] - cut), rode), to_chips(p_o)))
    g_ffn_in = reduced("w_ffn_in", p_fin, rode[:1])
    g_o = reduced("w_o", p_o, rode[1:])
    dproj = jnp.concatenate([dq, dk[KPAD:].astype(BF16), dv[KPAD:].astype(BF16), dzq, dzf, dxi, dzg], axis=1)
    p_in = pair_sum("w_in", _mm_tn(h1, dproj, q=N_CHIPS, tk=512, tn=w_in.shape[2] // 2, tt=T, name="g_in"))
    dh1, rode = carrying(_mm_nt, dproj, w_in, tm=1024, to=1024, tn=w_in.shape[2], name="d_h1", rider=to_chips(p_in))
    g_in = reduced("w_in", p_in, rode)
    grad_x, acc0 = _first_bwd(dh1, du1, x, scale1)
    dmod = jnp.concatenate([acc0[1:2], acc0[0:1], acc1[4:5], acc1[1:2], acc1[0:1], acc2[2:3]], axis=0)
    small = dict(rel_bias=_bias_band_grad(dbias), attn_norm_g=dgain,
                 lb_logits=jnp.concatenate([dl0, -dl0], axis=0), gnorm_g=dgn,
                 ln1_g=acc1[2:3], ln1_b=acc1[3:4], ln2_g=acc2[0:1], ln2_b=acc2[1:2])
    return loss, grad_x, dict(w_in=g_in, w_o=g_o, w_ffn_in=g_ffn_in, w_ffn_out=g_ffn_out), dmod, small


SMALL = ("rel_bias", "attn_norm_g", "lb_logits", "gnorm_g", "ln1_g", "ln1_b", "ln2_g", "ln2_b")
SMALL_ROWS = 256


def kernel(x, c, w_ada, b_ada, w_in, rel_bias, attn_norm_g, lb_logits, gnorm_g, w_o, ln1_g, ln1_b, w_ffn_in, w_ffn_out, ln2_g, ln2_b, loss_target, m_w_ada, m_b_ada, m_w_in, m_rel_bias, m_attn_norm_g, m_lb_logits, m_gnorm_g, m_w_o, m_ln1_g, m_ln1_b, m_w_ffn_in, m_w_ffn_out, m_ln2_g, m_ln2_b, v_w_ada, v_b_ada, v_w_in, v_rel_bias, v_attn_norm_g, v_lb_logits, v_gnorm_g, v_w_o, v_ln1_g, v_ln1_b, v_w_ffn_in, v_w_ffn_out, v_ln2_g, v_ln2_b):
    mx, my, mc = _place()
    me = 4 * mx + 2 * my + mc
    chip = 2 * mx + my
    sel = jnp.stack([chip, mc]).astype(jnp.int32)
    D = x.shape[2]
    ns_ada = w_ada.shape[2]

    big = dict(w_in=(w_in, m_w_in, v_w_in), w_o=(w_o, m_w_o, v_w_o), w_ffn_in=(w_ffn_in, m_w_ffn_in, v_w_ffn_in),
               w_ffn_out=(w_ffn_out, m_w_ffn_out, v_w_ffn_out))
    shards = {n: t[0][0].astype(BF16) for n, t in big.items()}

    c_all = _gather_small(c.reshape(D // 128, 128), "gather_c").reshape(N_DEV, D)
    c_all = jnp.pad(c_all, ((0, 16 - N_DEV), (0, 0)))
    mod_cols = _ada_fwd(c_all, w_ada[0])[:N_DEV]
    mod_all = _gather_small(mod_cols.reshape(-1, 128), "gather_mod").reshape(N_DEV, N_DEV, ns_ada)
    mod = lax.dynamic_index_in_dim(mod_all[::2], me, axis=1, keepdims=False)
    mod = (mod.reshape(1, -1) + b_ada).reshape(6, D)

    loss, grad_x, g_big, dmod, g_small = _layer(
        x[0], loss_target[0], mod, shards, rel_bias[0], attn_norm_g, lb_logits, gnorm_g, ln1_g, ln1_b, ln2_g, ln2_b,
        place=(chip, sel))
    loss = lax.psum(loss, ("x", "y", "c"))

    grads, deltas, new_m, new_v = {}, {}, {}, {}
    for n, (w, m, v) in big.items():
        g, d, mo, vo = _adamw(g_big[n], w[0], m[0], v[0], "adamw_" + n)
        grads[n], deltas[n], new_m[n], new_v[n] = g[None], d[None], mo[None], vo[None]

    small_in = dict(rel_bias=(rel_bias, m_rel_bias, v_rel_bias), attn_norm_g=(attn_norm_g, m_attn_norm_g, v_attn_norm_g),
                    lb_logits=(lb_logits, m_lb_logits, v_lb_logits), gnorm_g=(gnorm_g, m_gnorm_g, v_gnorm_g),
                    ln1_g=(ln1_g, m_ln1_g, v_ln1_g), ln1_b=(ln1_b, m_ln1_b, v_ln1_b), ln2_g=(ln2_g, m_ln2_g, v_ln2_g),
                    ln2_b=(ln2_b, m_ln2_b, v_ln2_b))
    g_all = _gather_small(_pack([dmod] + [g_small[n] for n in SMALL], SMALL_ROWS), "gather_small")
    packed = [_pack([t] + [small_in[n][i] for n in SMALL], SMALL_ROWS)
              for i, t in enumerate((b_ada, m_b_ada, v_b_ada))]
    shapes = [b_ada.shape] + [small_in[n][0].shape for n in SMALL]
    outs = [_unpack(o, shapes) for o in _small_update(g_all, *packed)]
    for i, n in enumerate(("b_ada",) + SMALL):
        grads[n], deltas[n], new_m[n], new_v[n] = outs[0][i], outs[1][i], outs[2][i], outs[3][i]

    dmod_all = g_all[:, :6 * D // 128].reshape(N_DEV, 6 * D)
    dmod_cols = lax.dynamic_slice_in_dim(dmod_all, chip * ns_ada, ns_ada, axis=1)
    dmod_cols = jnp.pad(dmod_cols, ((0, 16 - N_DEV), (0, 0)))
    g, d, mo, vo = _ada_bwd(c_all, dmod_cols, w_ada[0], m_w_ada[0], v_w_ada[0])
    grads["w_ada"], deltas["w_ada"], new_m["w_ada"], new_v["w_ada"] = g[None], d[None], mo[None], vo[None]

    order = ("w_ada", "b_ada", "w_in", "rel_bias", "attn_norm_g", "lb_logits", "gnorm_g", "w_o", "ln1_g", "ln1_b",
             "w_ffn_in", "w_ffn_out", "ln2_g", "ln2_b")
    return (loss, grad_x[None], *[grads[n] for n in order], *[deltas[n] for n in order],
            *[new_m[n] for n in order], *[new_v[n] for n in order])
```

```python
import numpy as np
import jax
import jax.numpy as jnp
from jax import lax
from jax.experimental import pallas as pl
from jax.experimental.pallas import tpu as pltpu

F32 = jnp.float32
BF16 = jnp.bfloat16
MESH = pl.DeviceIdType.MESH
HIGHEST = lax.Precision.HIGHEST

CHUNK = 64
N_PAST = 8
QG = 4
QROWS = QG * CHUNK
KPAD = N_PAST * CHUNK
UNION = (QG + N_PAST) * CHUNK
BAND = (N_PAST + 1) * CHUNK
HD_A = 64
HD_B = 128
SUB = 16
HGRN_HEADS = 4
MAX_REL = 256
EPS = 1e-5
ALPHA = 2.0 ** 0.25
LR, B1, B2, ADAM_EPS, WD, STEP = 1e-3, 0.9, 0.999, 1e-8, 0.01, 10
N_CHIPS = 4
N_DEV = 8
NEG = -1e30
TILE_BYTES = 3 << 19

NN = ((1,), (0,))
NT = ((1,), (1,))
TN = ((0,), (0,))


def _dot(a, b, dims=NN, precision=None):
    return lax.dot_general(a, b, (dims, ((), ())), preferred_element_type=F32, precision=precision)


def _params(sem=None, vmem_mb=None, **kw):
    return pltpu.CompilerParams(dimension_semantics=sem,
                                vmem_limit_bytes=None if vmem_mb is None else vmem_mb << 20, **kw)


def _row_tile(rows, cols):
    for cand in (512, 256, 128, 64, 32, 16, 8):
        if rows % cand == 0 and cand * cols * 4 <= TILE_BYTES:
            return cand
    raise ValueError((rows, cols))


def _place():
    return lax.axis_index("x"), lax.axis_index("y"), lax.axis_index("c")


def _flip(v, bit):
    return 1 - v if bit else v


ANY = pl.BlockSpec(memory_space=pl.ANY)
CHIP_FLIPS = ((1, 0), (0, 1), (1, 1))


class _Rider:
    def __init__(self, operands, out_shape, n_sems, start, finish, aliases=None):
        self.operands, self.out_shape, self.n_sems, self.start, self.finish = operands, out_shape, n_sems, start, finish
        self.aliases = aliases or {}


def _call(body, rider, *, name, grid, in_specs, out_specs, out_shape, scratch_shapes, compiler_params, operands):
    if rider is None:
        outs = pl.pallas_call(body, name=name, grid=grid, in_specs=in_specs, out_specs=out_specs, out_shape=out_shape,
                              scratch_shapes=scratch_shapes, compiler_params=compiler_params)(*operands)
        return list(outs), []
    n_in, n_out, n_sc = len(in_specs), len(out_specs), len(scratch_shapes)
    r_in, r_out = len(rider.operands), len(rider.out_shape)

    def carried(*refs):
        refs = list(refs)
        cuts = [n_in, r_in, n_out, r_out, n_sc]
        ins, r_ins, outs, r_outs, scratch = [[refs.pop(0) for _ in range(n)] for n in cuts]
        first, last = None, None
        for axis, size in enumerate(grid):
            i = pl.program_id(axis)
            first = (i == 0) if first is None else first & (i == 0)
            last = (i == size - 1) if last is None else last & (i == size - 1)

        @pl.when(first)
        def _():
            rider.start(r_ins, r_outs, *refs)

        body(*ins, *outs, *scratch)

        @pl.when(last)
        def _():
            rider.finish(r_ins, r_outs, *refs)

    sems = [pltpu.SemaphoreType.DMA((rider.n_sems,)), pltpu.SemaphoreType.DMA((rider.n_sems,))]
    outs = pl.pallas_call(carried, name=name, grid=grid, in_specs=list(in_specs) + [ANY] * r_in,
                          out_specs=list(out_specs) + [ANY] * r_out, out_shape=list(out_shape) + rider.out_shape,
                          scratch_shapes=list(scratch_shapes) + sems, compiler_params=compiler_params,
                          input_output_aliases={n_in + i: n_out + o for i, o in rider.aliases.items()},
                          )(*operands, *rider.operands)
    return list(outs[:n_out]), list(outs[n_out:])


def _alone(rider, name):
    def body(*refs):
        ins, outs, sems = refs[:len(rider.operands)], refs[len(rider.operands):-2], refs[-2:]
        rider.start(ins, outs, *sems)
        rider.finish(ins, outs, *sems)

    return pl.pallas_call(
        body, name=name, in_specs=[ANY] * len(rider.operands), out_specs=[ANY] * len(rider.out_shape),
        out_shape=rider.out_shape, input_output_aliases=rider.aliases,
        scratch_shapes=[pltpu.SemaphoreType.DMA((rider.n_sems,)), pltpu.SemaphoreType.DMA((rider.n_sems,))],
    )(*rider.operands)


class _Sems:
    def __init__(self, sems, base):
        self.sems, self.base = sems, base

    @property
    def at(self):
        return self

    def __getitem__(self, k):
        return self.sems.at[self.base + k]


def _join(*riders):
    riders = [r for r in riders if r is not None]
    if len(riders) < 2:
        return riders[0] if riders else None

    def parts(ins, outs, send_sems, recv_sems):
        i = o = s = 0
        for r in riders:
            ni, no = len(r.operands), len(r.out_shape)
            yield r, ins[i:i + ni], outs[o:o + no], _Sems(send_sems, s), _Sems(recv_sems, s)
            i, o, s = i + ni, o + no, s + r.n_sems

    def start(*refs):
        for r, *args in parts(*refs):
            r.start(*args)

    def finish(*refs):
        for r, *args in parts(*refs):
            r.finish(*args)

    aliases, i, o = {}, 0, 0
    for r in riders:
        aliases.update({i + a: o + b for a, b in r.aliases.items()})
        i, o = i + len(r.operands), o + len(r.out_shape)
    return _Rider([a for r in riders for a in r.operands], [s for r in riders for s in r.out_shape],
                  sum(r.n_sems for r in riders), start, finish, aliases)


def _gather_rider(shard, rows=None, into=None):
    K, Ns = shard.shape
    kh = K // 2
    first_row, n_rows = rows or (0, kh)

    def copies(w_ref, out_ref, send_sems, recv_sems):
        x, y, c = _place()
        chips = [(_flip(x, fx), _flip(y, fy)) for fx, fy in CHIP_FLIPS]

        def half(chip, which):
            return out_ref.at[2 * chip[0] + chip[1], pl.ds(which * kh + first_row, n_rows), :]

        def copy(k, dst, to, src=None):
            return pltpu.make_async_remote_copy(src_ref=dst if src is None else src, dst_ref=dst,
                                                send_sem=send_sems.at[k], recv_sem=recv_sems.at[k],
                                                device_id=to, device_id_type=MESH)

        def first():
            return [copy(j, half((x, y), c), (*chip, c), src=w_ref.at[pl.ds(c * kh + first_row, n_rows), :])
                    for j, chip in enumerate(chips)]

        def onward():
            return [copy(3 + j, half(chip, c), (x, y, 1 - c)) for j, chip in enumerate(chips)]

        def arriving(base, which):
            return [copy(base + j, half(chip, which), (x, y, c)) for j, chip in enumerate(chips)]

        return first, onward, arriving

    def start(ins, outs, send_sems, recv_sems):
        for cp in copies(ins[0], outs[0], send_sems, recv_sems)[0]():
            cp.start()

    def finish(ins, outs, send_sems, recv_sems):
        x, y, c = _place()
        first, onward, arriving = copies(ins[0], outs[0], send_sems, recv_sems)
        passed = onward()
        for arrived, cp in zip(arriving(0, c), passed):
            arrived.wait_recv()
            cp.start()
        for arrived in arriving(3, 1 - c):
            arrived.wait_recv()
        for cp in first() + passed:
            cp.wait_send()

    full = jax.ShapeDtypeStruct((N_CHIPS, K, Ns), shard.dtype)
    if into is None:
        return _Rider([shard], [full], 6, start, finish)
    return _Rider([shard, into], [full], 6, start, finish, aliases={1: 0})


def _pair_rider(g_full):
    Q, K, Ns = g_full.shape
    kh = K // 2

    def copy(g_ref, got_ref, send_sems, recv_sems):
        x, y, c = _place()
        return pltpu.make_async_remote_copy(src_ref=g_ref.at[:, pl.ds((1 - c) * kh, kh), :], dst_ref=got_ref,
                                            send_sem=send_sems.at[0], recv_sem=recv_sems.at[0],
                                            device_id=(x, y, 1 - c), device_id_type=MESH)

    def start(ins, outs, send_sems, recv_sems):
        copy(ins[0], outs[0], send_sems, recv_sems).start()

    def finish(ins, outs, send_sems, recv_sems):
        copy(ins[0], outs[0], send_sems, recv_sems).wait()

    return _Rider([g_full], [jax.ShapeDtypeStruct((Q, kh, Ns), g_full.dtype)], 1, start, finish)


def _share_rider(block):
    K, Ns = block.shape
    kh = K // 2

    def halves(out_ref):
        x, y, c = _place()
        return out_ref.at[pl.ds(c * kh, kh), :], out_ref.at[pl.ds((1 - c) * kh, kh), :], (x, y, 1 - c)

    def start(ins, outs, send_sems, recv_sems):
        mine, _, sibling = halves(outs[0])
        pltpu.make_async_remote_copy(src_ref=mine, dst_ref=mine, send_sem=send_sems.at[0], recv_sem=recv_sems.at[0],
                                     device_id=sibling, device_id_type=MESH).start()

    def finish(ins, outs, send_sems, recv_sems):
        mine, theirs, sibling = halves(outs[0])
        pltpu.make_async_remote_copy(src_ref=theirs, dst_ref=theirs, send_sem=send_sems.at[0], recv_sem=recv_sems.at[0],
                                     device_id=sibling, device_id_type=MESH).wait_recv()
        pltpu.make_async_remote_copy(src_ref=mine, dst_ref=mine, send_sem=send_sems.at[0], recv_sem=recv_sems.at[0],
                                     device_id=sibling, device_id_type=MESH).wait_send()

    return _Rider([block], [jax.ShapeDtypeStruct((K, Ns), block.dtype)], 1, start, finish, aliases={0: 0})


def _chips_rider(pair_sum, rows=None, into=None):
    Q, kh, Ns = pair_sum.shape
    first_row, n_rows = rows or (0, kh)

    def copies(p_ref, got_ref, send_sems, recv_sems):
        x, y, c = _place()
        part = pl.ds(first_row, n_rows)
        out = []
        for j, (fx, fy) in enumerate(CHIP_FLIPS):
            px, py = _flip(x, fx), _flip(y, fy)
            out.append(pltpu.make_async_remote_copy(
                src_ref=p_ref.at[2 * px + py, part, :], dst_ref=got_ref.at[j, part, :], send_sem=send_sems.at[j],
                recv_sem=recv_sems.at[j], device_id=(px, py, c), device_id_type=MESH))
        return out

    def start(ins, outs, send_sems, recv_sems):
        for cp in copies(ins[0], outs[0], send_sems, recv_sems):
            cp.start()

    def finish(ins, outs, send_sems, recv_sems):
        sends = copies(ins[0], outs[0], send_sems, recv_sems)
        for cp in sends:
            cp.wait_recv()
        for cp in sends:
            cp.wait_send()

    got = jax.ShapeDtypeStruct((Q - 1, kh, Ns), pair_sum.dtype)
    if into is None:
        return _Rider([pair_sum], [got], 3, start, finish)
    return _Rider([pair_sum, into], [got], 3, start, finish, aliases={1: 0})


def _mm(a, b, *, grid, a_spec, b_spec, o_spec, o_shape, o_dtype, dims, acc_shape, name, rider=None, zero_rows=0,
        vmem_mb=48):
    nk = grid[2]

    def body(a_ref, b_ref, o_ref, *scratch):
        if zero_rows:
            @pl.when(pl.program_id(0) < zero_rows)
            def _():
                o_ref[...] = jnp.zeros_like(o_ref)

            @pl.when(pl.program_id(0) >= zero_rows)
            def _():
                o_ref[...] = _dot(a_ref[...], b_ref[...], dims).astype(o_ref.dtype)
            return
        part = _dot(a_ref[...], b_ref[...], dims)
        if nk == 1:
            o_ref[...] = part.astype(o_ref.dtype)
            return
        acc_ref, = scratch
        k = pl.program_id(2)

        @pl.when(k == 0)
        def _():
            acc_ref[...] = part

        @pl.when(k > 0)
        def _():
            acc_ref[...] += part

        @pl.when(k == nk - 1)
        def _():
            o_ref[...] = acc_ref[...].astype(o_ref.dtype)

    (out,), rode = _call(
        body, rider, name=name, grid=grid, in_specs=[a_spec, b_spec], out_specs=[o_spec],
        out_shape=[jax.ShapeDtypeStruct(o_shape, o_dtype)],
        scratch_shapes=[] if nk == 1 else [pltpu.VMEM(acc_shape, F32)],
        compiler_params=_params(("parallel", "parallel", "arbitrary") if rider is None else ("arbitrary",) * 3, vmem_mb),
        operands=(a, b))
    return out if rider is None else (out, rode)


def _mm_nn(a, w, *, tm, tn, tk, name, rider=None, cols=None, o_dtype=F32, pad_rows=0):
    T, K = a.shape
    Q, _, Ns = w.shape
    nbs = Ns // tn
    tm = min(tm, T)
    j0, j1 = cols or (0, Q * nbs)
    lead = pad_rows // tm
    return _mm(a, w, grid=(lead + T // tm, j1 - j0, K // tk),
               a_spec=pl.BlockSpec((tm, tk), lambda i, j, k: (jnp.maximum(i - lead, 0), k)),
               b_spec=pl.BlockSpec((None, tk, tn), lambda i, j, k: ((j + j0) // nbs, k, (j + j0) % nbs)),
               o_spec=pl.BlockSpec((tm, tn), lambda i, j, k: (i, j)),
               o_shape=(pad_rows + T, (j1 - j0) * tn), o_dtype=o_dtype, dims=NN, acc_shape=(tm, tn), name=name,
               rider=rider, zero_rows=lead)


def _mm_nt(g, w, *, tm, to, tn, name, rider=None):
    T = g.shape[0]
    Q, K, Ns = w.shape
    nbs = Ns // tn
    tm = min(tm, T)
    return _mm(g, w, grid=(T // tm, K // to, Q * nbs),
               a_spec=pl.BlockSpec((tm, tn), lambda i, j, n: (i, n)),
               b_spec=pl.BlockSpec((None, to, tn), lambda i, j, n: (n // nbs, j, n % nbs)),
               o_spec=pl.BlockSpec((tm, to), lambda i, j, n: (i, j)),
               o_shape=(T, K), o_dtype=F32, dims=NT, acc_shape=(tm, to), name=name, rider=rider)


def _mm_tn(a, g, *, q, tk, tn, tt, name, rider=None):
    T, K = a.shape
    Ns = g.shape[1] // q
    nbs = Ns // tn
    return _mm(a, g, grid=(K // tk, q * nbs, T // tt),
               a_spec=pl.BlockSpec((tt, tk), lambda i, j, t: (t, i)),
               b_spec=pl.BlockSpec((tt, tn), lambda i, j, t: (t, j)),
               o_spec=pl.BlockSpec((None, tk, tn), lambda i, j, t: (j // nbs, i, j % nbs)),
               o_shape=(q, K, Ns), o_dtype=BF16, dims=TN, acc_shape=(tk, tn), name=name, rider=rider)


def _ln(u):
    mu = jnp.mean(u, axis=-1, keepdims=True)
    d = u - mu
    r = lax.rsqrt(jnp.mean(d * d, axis=-1, keepdims=True) + EPS)
    return d * r, r


def _ln_bwd(dy, un, r):
    return r * (dy - jnp.mean(dy, axis=-1, keepdims=True) - un * jnp.mean(dy * un, axis=-1, keepdims=True))


def _colsum(v):
    return jnp.sum(v, axis=0, keepdims=True)


def _rowwise(name, fn, bigs, vecs, out_dtypes, n_acc, tm=128, rider=None):
    T, D = bigs[0].shape
    nb, nv, no = len(bigs), len(vecs), len(out_dtypes)

    def body(*refs):
        outs, accs = fn([r[...] for r in refs[:nb]], [r[...] for r in refs[nb:nb + nv]])
        for r, o in zip(refs[nb + nv:nb + nv + no], outs):
            r[...] = o.astype(r.dtype)
        if n_acc:
            acc_ref = refs[nb + nv + no]

            @pl.when(pl.program_id(0) == 0)
            def _():
                acc_ref[...] = jnp.zeros_like(acc_ref)

            for row, a in enumerate(accs):
                acc_ref[row:row + 1, :] += a

    big_spec = pl.BlockSpec((tm, D), lambda i: (i, 0))
    vec_spec = pl.BlockSpec((1, D), lambda i: (0, 0))
    out_shape = [jax.ShapeDtypeStruct((T, D), dt) for dt in out_dtypes]
    out_specs = [big_spec] * no
    if n_acc:
        out_shape.append(jax.ShapeDtypeStruct((8, D), F32))
        out_specs.append(pl.BlockSpec((8, D), lambda i: (0, 0)))
    outs, rode = _call(
        body, rider, name=name, grid=(T // tm,), in_specs=[big_spec] * nb + [vec_spec] * nv,
        out_specs=out_specs, out_shape=out_shape, scratch_shapes=[],
        compiler_params=_params(("arbitrary",), 48), operands=(*bigs, *vecs))
    return outs if rider is None else (outs, rode)


def _pre_mixer(x, scale1, shift1):
    def fn(b, v):
        xn, _ = _ln(b[0])
        return [xn * (1.0 + v[0]) + v[1]], []
    return _rowwise("pre_mixer", fn, [x], [scale1, shift1], [BF16], 0)[0]


def _post_mixer(mix, x, gate1, g1, b1, scale2, shift2, rider=None):
    def fn(b, v):
        un1, _ = _ln(ALPHA * b[1] + v[0] * b[0])
        x1 = un1 * v[1] + v[2]
        xn1, _ = _ln(x1)
        return [x1, xn1 * (1.0 + v[3]) + v[4]], []
    return _rowwise("post_mixer", fn, [mix, x], [gate1, g1, b1, scale2, shift2], [F32, BF16], 0, rider=rider)


def _loss_head(f, x1, tgt, gate2, g2, b2):
    def fn(b, v):
        ff, xx, tt = b
        d_model = ff.shape[-1]
        un2, r2 = _ln(ALPHA * xx + v[0] * ff)
        err = un2 * v[1] + v[2] - tt
        dy = err * (1.0 / d_model)
        du2 = _ln_bwd(dy * v[1], un2, r2)
        return [du2, du2 * v[0]], [_colsum(dy * un2), _colsum(dy), _colsum(du2 * ff), _colsum(err * err)]
    return _rowwise("loss_head", fn, [f, x1, tgt], [gate2, g2, b2], [F32, BF16], 4)


def _mid_bwd(dh2, du2, x1, mix, x, gate1, g1, scale2, rider=None):
    def fn(b, v):
        dh, du, xx1, mm, xx = b
        xn1, r1n = _ln(xx1)
        dx1 = ALPHA * du + _ln_bwd(dh * (1.0 + v[2]), xn1, r1n)
        un1, r1 = _ln(ALPHA * xx + v[0] * mm)
        du1 = _ln_bwd(dx1 * v[1], un1, r1)
        return [du1, du1 * v[0]], [_colsum(dh * xn1), _colsum(dh), _colsum(dx1 * un1), _colsum(dx1),
                                   _colsum(du1 * mm)]
    return _rowwise("mid_bwd", fn, [dh2, du2, x1, mix, x], [gate1, g1, scale2], [F32, BF16], 5, rider=rider)


def _first_bwd(dh1, du1, x, scale1):
    def fn(b, v):
        dh, du, xx = b
        xn, r0 = _ln(xx)
        return [ALPHA * du + _ln_bwd(dh * (1.0 + v[0]), xn, r0)], [_colsum(dh * xn), _colsum(dh)]
    return _rowwise("first_bwd", fn, [dh1, du1, x], [scale1], [F32], 2)


def _ffn_in_swiglu(h2, w, *, tm, tn, rider=None):
    T, K = h2.shape
    Q, _, Ns = w.shape
    nbs = Ns // tn
    half = Q * nbs // 2
    tm = min(tm, T)

    def body(a_ref, wg_ref, wu_ref, g_ref, u_ref, act_ref):
        a = a_ref[...]
        g, u = _dot(a, wg_ref[...]), _dot(a, wu_ref[...])
        g_ref[...] = g.astype(g_ref.dtype)
        u_ref[...] = u.astype(u_ref.dtype)
        act_ref[...] = (g * jax.nn.sigmoid(g) * u).astype(act_ref.dtype)

    cols = lambda first: pl.BlockSpec((None, K, tn), lambda i, j: ((j + first) // nbs, 0, (j + first) % nbs))
    blk = pl.BlockSpec((tm, tn), lambda i, j: (i, j))
    return _call(
        body, rider, name="ffn_in", grid=(T // tm, half),
        in_specs=[pl.BlockSpec((tm, K), lambda i, j: (i, 0)), cols(0), cols(half)], out_specs=[blk] * 3,
        out_shape=[jax.ShapeDtypeStruct((T, half * tn), BF16)] * 3, scratch_shapes=[],
        compiler_params=_params(("arbitrary", "arbitrary"), 48), operands=(h2, w, w))


def _d_act_swiglu(df, w, gate, up, *, tm, to, rider=None):
    T, N = df.shape
    F = w.shape[1]
    tm = min(tm, T)

    def body(df_ref, w_ref, g_ref, u_ref, dg_ref, du_ref):
        d = _dot(df_ref[...], w_ref[...], NT)
        g = g_ref[...].astype(F32)
        s = jax.nn.sigmoid(g)
        du_ref[...] = (d * g * s).astype(du_ref.dtype)
        dg_ref[...] = (d * u_ref[...].astype(F32) * s * (1.0 + g * (1.0 - s))).astype(dg_ref.dtype)

    blk = pl.BlockSpec((tm, to), lambda i, j: (i, j))
    return _call(
        body, rider, name="d_act", grid=(T // tm, F // to),
        in_specs=[pl.BlockSpec((tm, N), lambda i, j: (i, 0)), pl.BlockSpec((None, to, N), lambda i, j: (0, j, 0)), blk, blk],
        out_specs=[blk, blk], out_shape=[jax.ShapeDtypeStruct((T, F), BF16)] * 2, scratch_shapes=[],
        compiler_params=_params(("arbitrary", "arbitrary"), 48), operands=(df, w, gate, up))


PAIR = 2


def _fill_table(table_ref, band_ref):
    table_ref[...] = jnp.full(table_ref.shape, NEG, F32)
    for e in range(PAIR):
        for g in range(QG):
            table_ref[e, g * CHUNK:(g + 1) * CHUNK, g * CHUNK:g * CHUNK + BAND] = band_ref[e]


def _attn_probs(q_ref, k_ref, bias_ref, e, step):
    start = pl.multiple_of(step * QROWS, QROWS)
    lanes = pl.ds(e * HD_A, HD_A)
    s = _dot(q_ref[:, lanes], k_ref[pl.ds(start, UNION), lanes], NT) * (HD_A ** -0.5) + bias_ref[e]
    col = lax.broadcasted_iota(jnp.int32, s.shape, 1)
    s = jnp.where(col + start >= KPAD, s, NEG)
    p = jnp.exp(s - jnp.max(s, axis=-1, keepdims=True))
    return p / jnp.sum(p, axis=-1, keepdims=True), start


def _attn_specs(T, n_pairs):
    wide = PAIR * HD_A
    per_step = pl.BlockSpec((QROWS, wide), lambda hp, n: (n, hp))
    queries = pl.BlockSpec((QROWS, wide), lambda hp, n: (n + KPAD // QROWS, hp))
    keys = pl.BlockSpec((KPAD + T, wide), lambda hp, n: (0, n_pairs + hp))
    values = pl.BlockSpec((KPAD + T, wide), lambda hp, n: (0, 2 * n_pairs + hp))
    grads = pl.BlockSpec((KPAD + T, wide), lambda hp, n: (0, hp))
    table = pl.BlockSpec((PAIR, CHUNK, BAND), lambda hp, n: (hp, 0, 0))
    vec = pl.BlockSpec((1, wide), lambda hp, n: (0, hp))
    return per_step, queries, keys, values, grads, table, vec


def _probs_spec():
    return pl.BlockSpec((PAIR, QROWS, UNION), lambda hp, n: (hp, n, 0))


def _attn_fwd(qkv, bias, gain, rider=None):
    T = qkv.shape[0] - KPAD
    W = gain.shape[1]
    n_pairs = W // (PAIR * HD_A)

    def body(q_ref, k_ref, v_ref, band_ref, gain_ref, o_ref, p_ref, table_ref):
        @pl.when(pl.program_id(1) == 0)
        def _():
            _fill_table(table_ref, band_ref)

        for e in range(PAIR):
            lanes = pl.ds(e * HD_A, HD_A)
            p, start = _attn_probs(q_ref, k_ref, table_ref, e, pl.program_id(1))
            p_ref[e] = p.astype(p_ref.dtype)
            o = _dot(p_ref[e], v_ref[pl.ds(start, UNION), lanes])
            rr = lax.rsqrt(jnp.mean(o * o, axis=-1, keepdims=True) + EPS)
            o_ref[:, lanes] = (o * rr * gain_ref[:, lanes]).astype(o_ref.dtype)

    per_step, queries, keys, values, _, table, vec = _attn_specs(T, n_pairs)
    return _call(
        body, rider, name="attn_fwd", grid=(n_pairs, T // QROWS), in_specs=[queries, keys, values, table, vec],
        out_specs=[per_step, _probs_spec()],
        out_shape=[jax.ShapeDtypeStruct((T, W), BF16), jax.ShapeDtypeStruct((n_pairs * PAIR, T, UNION), BF16)],
        scratch_shapes=[pltpu.VMEM((PAIR, QROWS, UNION), F32)],
        compiler_params=_params(("arbitrary", "arbitrary"), 40), operands=(qkv, qkv, qkv, bias, gain))


def _attn_bwd(qkv, probs, gain, dmixin, rider=None):
    T = qkv.shape[0] - KPAD
    W = gain.shape[1]
    n_pairs = W // (PAIR * HD_A)
    scale = HD_A ** -0.5

    def body(q_ref, k_ref, v_ref, p_ref, gain_ref, don_ref, dq_ref, dk_ref, dv_ref, dband_ref, dgain_ref, dtable_ref):
        n = pl.program_id(1)

        @pl.when(n == 0)
        def _():
            dk_ref[...] = jnp.zeros_like(dk_ref)
            dv_ref[...] = jnp.zeros_like(dv_ref)
            dtable_ref[...] = jnp.zeros_like(dtable_ref)
            dgain_ref[...] = jnp.zeros_like(dgain_ref)

        for e in range(PAIR):
            lanes = pl.ds(e * HD_A, HD_A)
            keys = pl.ds(pl.multiple_of(n * QROWS, QROWS), UNION)
            pb = p_ref[e]
            p = pb.astype(F32)
            vb = v_ref[keys, lanes]
            o = _dot(pb, vb)
            rr = lax.rsqrt(jnp.mean(o * o, axis=-1, keepdims=True) + EPS)
            on = o * rr
            d_on = don_ref[:, lanes]
            dgain_ref[:, lanes] += _colsum(d_on * on)
            dyo = d_on * gain_ref[:, lanes]
            do = rr * (dyo - on * jnp.mean(dyo * on, axis=-1, keepdims=True))
            dob = do.astype(BF16)
            dp = _dot(dob, vb, NT)
            ds = p * (dp - jnp.sum(do * o, axis=-1, keepdims=True))
            dtable_ref[e] += ds
            dsb = ds.astype(BF16)
            dq_ref[:, lanes] = (_dot(dsb, k_ref[keys, lanes]) * scale).astype(dq_ref.dtype)
            dk_ref[keys, lanes] += _dot(dsb, q_ref[:, lanes], TN) * scale
            dv_ref[keys, lanes] += _dot(pb, dob, TN)

        @pl.when(n == T // QROWS - 1)
        def _():
            for e in range(PAIR):
                dband_ref[e] = sum(dtable_ref[e, g * CHUNK:(g + 1) * CHUNK, g * CHUNK:g * CHUNK + BAND]
                                   for g in range(QG))

    per_step, queries, keys, values, grads, table, vec = _attn_specs(T, n_pairs)
    H = n_pairs * PAIR
    return _call(
        body, rider, name="attn_bwd", grid=(n_pairs, T // QROWS),
        in_specs=[queries, keys, values, _probs_spec(), vec, per_step],
        out_specs=[per_step, grads, grads, table, vec],
        out_shape=[jax.ShapeDtypeStruct((T, W), BF16), jax.ShapeDtypeStruct((KPAD + T, W), F32),
                   jax.ShapeDtypeStruct((KPAD + T, W), F32), jax.ShapeDtypeStruct((H, CHUNK, BAND), F32),
                   jax.ShapeDtypeStruct((1, W), F32)],
        scratch_shapes=[pltpu.VMEM((PAIR, QROWS, UNION), F32)],
        compiler_params=_params(("arbitrary", "arbitrary"), 40),
        operands=(qkv, qkv, qkv, probs, gain, dmixin))


N_DIAG = CHUNK + BAND - 1


def _bias_band(rel_bias):
    H = rel_bias.shape[0]
    idx = np.clip(BAND - 1 - np.arange(N_DIAG), -MAX_REL, MAX_REL) + MAX_REL
    rolled = rel_bias[:, idx[(np.arange(N_DIAG) + CHUNK - 1) % N_DIAG]]
    flat = jnp.broadcast_to(rolled[:, None, :], (H, CHUNK, N_DIAG)).reshape(H, CHUNK * N_DIAG)
    return flat[:, :CHUNK * (N_DIAG - 1)].reshape(H, CHUNK, N_DIAG - 1)[:, :, :BAND]


def _bias_band_grad(dband):
    H = dband.shape[0]
    skew = jnp.pad(dband, ((0, 0), (0, 0), (CHUNK - 1, 0))).reshape(H, CHUNK * N_DIAG)
    skew = jnp.pad(skew, ((0, 0), (0, CHUNK))).reshape(H, CHUNK, N_DIAG + 1)
    diag = jnp.sum(skew, axis=1)[:, :N_DIAG]
    n_far = BAND - MAX_REL
    far = jnp.sum(diag[:, :n_far], axis=1, keepdims=True)
    near = diag[:, n_far:][:, ::-1]
    zeros = jnp.zeros((H, MAX_REL - (CHUNK - 1)), F32)
    return jnp.concatenate([zeros, near, far], axis=1)


def _tri(n, lower):
    r = lax.broadcasted_iota(jnp.int32, (n, n), 0)
    c = lax.broadcasted_iota(jnp.int32, (n, n), 1)
    return jnp.where((c <= r) if lower else (c >= r), 1.0, 0.0).astype(F32)


def _hgrn_gates(zq_ref, zf_ref, lbl_ref, q_s, k_s, b_s):
    lb = jax.nn.sigmoid(lbl_ref[0:1, :] - lbl_ref[1:2, :])
    zq = zq_ref[...]
    sig = jax.nn.sigmoid(zf_ref[...])
    f = lb + (1.0 - lb) * sig
    sq = jax.nn.sigmoid(zq)
    q_s[...] = zq * sq
    k_s[...] = 1.0 - f
    b_s[...] = _dot(_tri(CHUNK, True), jnp.log(f), precision=HIGHEST)
    return lb, sig, f, sq


def _sub_rows(i):
    return pl.ds(i * SUB, SUB)


def _row_mask(s):
    return lax.broadcasted_iota(jnp.int32, (SUB, HD_B), 0) >= s


def _decay_from(b_sub, b_row, s):
    return jnp.where(_row_mask(s), jnp.exp(jnp.minimum(b_sub - b_row, 0.0)), 0.0)


def _hgrn_fwd(proj, lb_logits, gnorm_g, rider=None):
    T = proj.shape[0]
    nC = T // CHUNK
    W = lb_logits.shape[1]
    G = W // HD_B // HGRN_HEADS
    col0 = (proj.shape[1] - 4 * W) // (HD_B * HGRN_HEADS)
    wide = HGRN_HEADS * HD_B

    def body(*refs):
        @pl.when(pl.program_id(1) == 0)
        def _():
            refs[9][...] = jnp.zeros_like(refs[9])

        for h in range(HGRN_HEADS):
            lanes = pl.ds(h * HD_B, HD_B)
            one_head(*[r.at[:, lanes] for r in refs[:5]], refs[5], *[r.at[:, lanes] for r in refs[6:8]],
                     *[r.at[h] for r in refs[8:]])

    def one_head(zq_ref, zf_ref, xi_ref, zg_ref, lbl_ref, gn_ref, mix_ref, o_ref, stall_ref, st_ref, q_s, k_s, b_s, acc_s):
        _hgrn_gates(zq_ref, zf_ref, lbl_ref, q_s, k_s, b_s)
        q, k, b = q_s[...], k_s[...], b_s[...]
        st = st_ref[...]
        stall_ref[...] = st
        b_last = b_s[CHUNK - 1:CHUNK, :]
        acc_s[...] = _dot((q * jnp.exp(b)).astype(BF16), st.astype(BF16), NT)
        for i in range(CHUNK // SUB):
            rows = _sub_rows(i)
            q_i, b_i = q_s[rows, :], b_s[rows, :]
            acc = jnp.zeros((SUB, HD_B), F32)
            if i:
                past = pl.ds(0, i * SUB)
                b_ref = b_s[i * SUB - 1:i * SUB, :]
                qs = (q_i * jnp.exp(b_i - b_ref)).astype(BF16)
                ks = (k_s[past, :] * jnp.exp(b_ref - b_s[past, :])).astype(BF16)
                acc += _dot(_dot(qs, ks, NT).astype(BF16), xi_ref[past, :].astype(BF16))
            for s in range(SUB):
                row = pl.ds(i * SUB + s, 1)
                w = q_i * _decay_from(b_i, b_s[row, :], s)
                acc += jnp.sum(w * k_s[row, :], axis=-1, keepdims=True) * xi_ref[row, :]
            acc_s[rows, :] += acc
        o = acc_s[...]
        kd = (k * jnp.exp(b_last - b)).astype(BF16)
        st_ref[...] = st * jnp.exp(b_last) + _dot(xi_ref[...].astype(BF16), kd, TN)
        o_ref[...] = o
        zg = zg_ref[...]
        rr = lax.rsqrt(jnp.mean(o * o, axis=-1, keepdims=True) + EPS)
        mix_ref[...] = (o * rr * gn_ref[...] * (zg * jax.nn.sigmoid(zg))).astype(mix_ref.dtype)

    col = lambda part: pl.BlockSpec((CHUNK, wide), lambda g, n: (n, col0 + part * G + g))
    out_blk = pl.BlockSpec((CHUNK, wide), lambda g, n: (n, g))
    tile = pltpu.VMEM((HGRN_HEADS, CHUNK, HD_B), F32)
    return _call(
        body, rider, name="hgrn_fwd", grid=(G, nC),
        in_specs=[col(0), col(1), col(2), col(3), pl.BlockSpec((2, wide), lambda g, n: (0, g)),
                  pl.BlockSpec((1, HD_B), lambda g, n: (0, 0))],
        out_specs=[out_blk, out_blk, pl.BlockSpec((HGRN_HEADS, None, HD_B, HD_B), lambda g, n: (g, n, 0, 0))],
        out_shape=[jax.ShapeDtypeStruct((T, W), BF16), jax.ShapeDtypeStruct((T, W), F32),
                   jax.ShapeDtypeStruct((G * HGRN_HEADS, nC, HD_B, HD_B), F32)],
        scratch_shapes=[pltpu.VMEM((HGRN_HEADS, HD_B, HD_B), F32), tile, tile, tile, tile],
        compiler_params=_params(("arbitrary", "arbitrary")),
        operands=(proj, proj, proj, proj, lb_logits, gnorm_g))


def _hgrn_bwd(proj, lb_logits, gnorm_g, o_b, st_all, dmixin, rider=None):
    T = proj.shape[0]
    nC = T // CHUNK
    W = lb_logits.shape[1]
    G = W // HD_B // HGRN_HEADS
    wide = HGRN_HEADS * HD_B
    col0 = (proj.shape[1] - 4 * W) // wide
    dcol0 = (dmixin.shape[1] - W) // wide

    def body(*refs):
        g, n = pl.program_id(0), pl.program_id(1)
        dl0_ref, dgn_ref, dst_ref = refs[13:16]

        @pl.when(n == 0)
        def _():
            dst_ref[...] = jnp.zeros_like(dst_ref)
            dl0_ref[...] = jnp.zeros_like(dl0_ref)

        @pl.when((n == 0) & (g == 0))
        def _():
            dgn_ref[...] = jnp.zeros_like(dgn_ref)

        for h in range(HGRN_HEADS):
            lanes = pl.ds(h * HD_B, HD_B)
            cut = lambda r: r.at[:, lanes]
            one_head(*[cut(r) for r in refs[:5]], refs[5], cut(refs[6]), refs[7].at[h], cut(refs[8]),
                     *[cut(r) for r in refs[9:14]], dgn_ref, *[r.at[h] for r in refs[15:]])

    def one_head(zq_ref, zf_ref, xi_ref, zg_ref, lbl_ref, gn_ref, o_ref, st_ref, dout_ref,
                 dzq_ref, dzf_ref, dxi_ref, dzg_ref, dl0_ref, dgn_ref, dst_ref, q_s, k_s, b_s, do_s, dq_s, dk_s, di_s):
        lb, sig, f, sq = _hgrn_gates(zq_ref, zf_ref, lbl_ref, q_s, k_s, b_s)
        q, k, b = q_s[...], k_s[...], b_s[...]
        zg, o, dout = zg_ref[...], o_ref[...], dout_ref[...]
        sg = jax.nn.sigmoid(zg)
        rr = lax.rsqrt(jnp.mean(o * o, axis=-1, keepdims=True) + EPS)
        on = o * rr
        gn = gn_ref[...]
        dzg_ref[...] = (dout * on * gn * sg * (1.0 + zg * (1.0 - sg))).astype(dzg_ref.dtype)
        d_on = dout * zg * sg
        dgn_ref[...] += _colsum(d_on * on)
        d_on = d_on * gn
        do = rr * (d_on - on * jnp.mean(d_on * on, axis=-1, keepdims=True))
        do_s[...] = do
        dob = do.astype(BF16)
        st, dst = st_ref[...], dst_ref[...]
        b_last = b_s[CHUNK - 1:CHUNK, :]
        eb, e_last, k_dec = jnp.exp(b), jnp.exp(b_last), jnp.exp(b_last - b)
        qt, kd = q * eb, k * k_dec
        dstb = dst.astype(BF16)
        xib = xi_ref[...].astype(BF16)
        d_kd = _dot(xib, dstb)
        dq_s[...] = _dot(dob, st.astype(BF16)) * eb
        dk_s[...] = d_kd * k_dec
        di_s[...] = _dot(kd.astype(BF16), dstb, NT)
        d_b_last = e_last * _colsum(st * dst) + _colsum(d_kd * kd)
        dst_ref[...] = _dot(dob, qt.astype(BF16), TN) + dst * e_last
        for i in range(CHUNK // SUB):
            rows = _sub_rows(i)
            q_i, b_i, do_i = q_s[rows, :], b_s[rows, :], do_s[rows, :]
            dq_i = jnp.zeros((SUB, HD_B), F32)
            if i:
                past = pl.ds(0, i * SUB)
                b_ref = b_s[i * SUB - 1:i * SUB, :]
                e_q, e_k = jnp.exp(b_i - b_ref), jnp.exp(b_ref - b_s[past, :])
                qs, ks = (q_i * e_q).astype(BF16), (k_s[past, :] * e_k).astype(BF16)
                xi_p, do_b = xi_ref[past, :].astype(BF16), do_i.astype(BF16)
                di_s[past, :] += _dot(_dot(ks, qs, NT).astype(BF16), do_b)
                dq_i += _dot(_dot(do_b, xi_p, NT).astype(BF16), ks) * e_q
                dk_s[past, :] += _dot(_dot(xi_p, do_b, NT).astype(BF16), qs) * e_k
            for s in range(SUB):
                row = pl.ds(i * SUB + s, 1)
                k_row, i_row = k_s[row, :], xi_ref[row, :]
                e = _decay_from(b_i, b_s[row, :], s)
                w = q_i * e
                a_col = jnp.sum(w * k_row, axis=-1, keepdims=True)
                da_col = jnp.sum(do_i * i_row, axis=-1, keepdims=True)
                di_s[row, :] += _colsum(a_col * do_i)
                dq_i += da_col * e * k_row
                dk_s[row, :] += _colsum(da_col * w)
            dq_s[rows, :] += dq_i
        dq, dk = dq_s[...], dk_s[...]
        db = q * dq - k * dk
        is_last = lax.broadcasted_iota(jnp.int32, (CHUNK, HD_B), 0) == CHUNK - 1
        db = db + jnp.where(is_last, d_b_last, 0.0)
        df = _dot(_tri(CHUNK, False), db, precision=HIGHEST) / f - dk
        dzf_ref[...] = (df * (1.0 - lb) * sig * (1.0 - sig)).astype(dzf_ref.dtype)
        dl0_ref[...] += _colsum(df * (1.0 - sig)) * (lb * (1.0 - lb))
        zq = zq_ref[...]
        dzq_ref[...] = (dq * sq * (1.0 + zq * (1.0 - sq))).astype(dzq_ref.dtype)
        dxi_ref[...] = di_s[...].astype(dxi_ref.dtype)

    rev = lambda n: nC - 1 - n
    col = lambda part: pl.BlockSpec((CHUNK, wide), lambda g, n: (rev(n), col0 + part * G + g))
    blk = pl.BlockSpec((CHUNK, wide), lambda g, n: (rev(n), g))
    tile = pltpu.VMEM((HGRN_HEADS, CHUNK, HD_B), F32)
    out_big = jax.ShapeDtypeStruct((T, W), BF16)
    return _call(
        body, rider, name="hgrn_bwd", grid=(G, nC),
        in_specs=[col(0), col(1), col(2), col(3), pl.BlockSpec((2, wide), lambda g, n: (0, g)),
                  pl.BlockSpec((1, HD_B), lambda g, n: (0, 0)), blk,
                  pl.BlockSpec((HGRN_HEADS, None, HD_B, HD_B), lambda g, n: (g, rev(n), 0, 0)),
                  pl.BlockSpec((CHUNK, wide), lambda g, n: (rev(n), dcol0 + g))],
        out_specs=[blk, blk, blk, blk, pl.BlockSpec((1, wide), lambda g, n: (0, g)),
                   pl.BlockSpec((1, HD_B), lambda g, n: (0, 0))],
        out_shape=[out_big, out_big, out_big, out_big, jax.ShapeDtypeStruct((1, W), F32),
                   jax.ShapeDtypeStruct((1, HD_B), F32)],
        scratch_shapes=[pltpu.VMEM((HGRN_HEADS, HD_B, HD_B), F32)] + [tile] * 7,
        compiler_params=_params(("arbitrary", "arbitrary")),
        operands=(proj, proj, proj, proj, lb_logits, gnorm_g, o_b, st_all, dmixin))


def _adamw_math(g, w, m, v):
    m = B1 * m + (1.0 - B1) * g
    v = B2 * v + (1.0 - B2) * (g * g)
    m_hat = m / (1.0 - B1 ** STEP)
    v_hat = v / (1.0 - B2 ** STEP)
    return -LR * (m_hat / (jnp.sqrt(v_hat) + ADAM_EPS) + WD * w), m, v


def _adamw(g, w, m, v, name):
    R, C = g.shape
    tr = _row_tile(R, C)

    def body(g_ref, w_ref, m_ref, v_ref, go_ref, d_ref, mo_ref, vo_ref):
        g = g_ref[...]
        go_ref[...] = g
        d_ref[...], mo_ref[...], vo_ref[...] = _adamw_math(g, w_ref[...], m_ref[...], v_ref[...])

    blk = pl.BlockSpec((tr, C), lambda i: (i, 0))
    return pl.pallas_call(
        body, name=name, grid=(R // tr,), in_specs=[blk] * 4, out_specs=[blk] * 4,
        out_shape=[jax.ShapeDtypeStruct((R, C), F32)] * 4, compiler_params=_params(("parallel",), 40),
    )(g, w, m, v)


def _sum_pair(g_full, from_sibling, sel, name):
    Q, K, Ns = g_full.shape
    kh = K // 2
    tr = _row_tile(kh, Ns)
    nh = kh // tr

    def body(sel_ref, a_ref, b_ref, o_ref):
        o_ref[...] = (a_ref[...].astype(F32) + b_ref[...].astype(F32)).astype(o_ref.dtype)

    return pl.pallas_call(
        body, name=name,
        grid_spec=pltpu.PrefetchScalarGridSpec(
            num_scalar_prefetch=1, grid=(Q, nh),
            in_specs=[pl.BlockSpec((None, tr, Ns), lambda q, i, sel: (q, sel[1] * nh + i, 0)),
                      pl.BlockSpec((None, tr, Ns), lambda q, i, sel: (q, i, 0))],
            out_specs=pl.BlockSpec((None, tr, Ns), lambda q, i, sel: (q, i, 0))),
        out_shape=jax.ShapeDtypeStruct((Q, kh, Ns), BF16), compiler_params=_params(("parallel", "parallel")),
    )(sel, g_full, from_sibling)


def _sum_chips(pair_sum, from_chips, sel, name):
    Q, kh, Ns = pair_sum.shape
    tr = _row_tile(kh, Ns)
    nh = kh // tr

    def body(sel_ref, a_ref, b0_ref, b1_ref, b2_ref, o_ref):
        up = lambda r: r[...].astype(F32)
        o_ref[...] = ((up(a_ref) + up(b0_ref)) + up(b1_ref)) + up(b2_ref)

    recv = lambda k: pl.BlockSpec((None, tr, Ns), lambda i, sel: (k, i, 0))
    return pl.pallas_call(
        body, name=name,
        grid_spec=pltpu.PrefetchScalarGridSpec(
            num_scalar_prefetch=1, grid=(nh,),
            in_specs=[pl.BlockSpec((None, tr, Ns), lambda i, sel: (sel[0], i, 0)), recv(0), recv(1), recv(2)],
            out_specs=pl.BlockSpec((tr, Ns), lambda i, sel: (sel[1] * nh + i, 0))),
        out_shape=jax.ShapeDtypeStruct((2 * kh, Ns), F32), compiler_params=_params(("parallel",)),
    )(sel, pair_sum, from_chips, from_chips, from_chips)


def _gather_small(v, name):
    R, L = v.shape

    def body(v_ref, out_ref, send_sems, recv_sems):
        x, y, c = _place()
        me = 4 * x + 2 * y + c
        out_ref[me] = v_ref[...]
        peers = [(_flip(x, k >> 2 & 1), _flip(y, k >> 1 & 1), _flip(c, k & 1)) for k in range(1, N_DEV)]

        def copy(k, row, to):
            return pltpu.make_async_remote_copy(src_ref=v_ref, dst_ref=out_ref.at[row], send_sem=send_sems.at[k],
                                                recv_sem=recv_sems.at[k], device_id=to, device_id_type=MESH)

        sends = [copy(k, me, peer) for k, peer in enumerate(peers)]
        for cp in sends:
            cp.start()
        for k, (px, py, pc) in enumerate(peers):
            copy(k, 4 * px + 2 * py + pc, (x, y, c)).wait_recv()
        for cp in sends:
            cp.wait_send()

    vmem = pl.BlockSpec(memory_space=pltpu.VMEM)
    return pl.pallas_call(
        body, name=name, in_specs=[vmem], out_specs=vmem, out_shape=jax.ShapeDtypeStruct((N_DEV, R, L), F32),
        scratch_shapes=[pltpu.SemaphoreType.DMA((N_DEV - 1,)), pltpu.SemaphoreType.DMA((N_DEV - 1,))],
    )(v)


def _silu(v):
    return v * jax.nn.sigmoid(v)


def _ada_fwd(c_all, w_ada, tn=512):
    M, D = c_all.shape
    Ns = w_ada.shape[1]

    def body(c_ref, w_ref, o_ref):
        o_ref[...] = _dot(_silu(c_ref[...]).astype(BF16), w_ref[...].astype(BF16))

    return pl.pallas_call(
        body, name="ada_fwd", grid=(Ns // tn,),
        in_specs=[pl.BlockSpec((M, D), lambda j: (0, 0)), pl.BlockSpec((D, tn), lambda j: (0, j))],
        out_specs=pl.BlockSpec((M, tn), lambda j: (0, j)), out_shape=jax.ShapeDtypeStruct((M, Ns), F32),
        compiler_params=_params(("parallel",)),
    )(c_all, w_ada)


def _ada_bwd(c_all, dmod, w, m, v, tk=256, tn=1536):
    M, D = c_all.shape
    Ns = dmod.shape[1]

    def body(c_ref, d_ref, w_ref, m_ref, v_ref, g_ref, dl_ref, mo_ref, vo_ref):
        g = _dot(_silu(c_ref[...]).astype(BF16), d_ref[...].astype(BF16), TN)
        g_ref[...] = g
        dl_ref[...], mo_ref[...], vo_ref[...] = _adamw_math(g, w_ref[...], m_ref[...], v_ref[...])

    blk = pl.BlockSpec((tk, tn), lambda i, j: (i, j))
    return pl.pallas_call(
        body, name="ada_bwd", grid=(D // tk, Ns // tn),
        in_specs=[pl.BlockSpec((M, tk), lambda i, j: (0, i)), pl.BlockSpec((M, tn), lambda i, j: (0, j)), blk, blk, blk],
        out_specs=[blk] * 4, out_shape=[jax.ShapeDtypeStruct((D, Ns), F32)] * 4,
        compiler_params=_params(("parallel", "parallel"), 40),
    )(c_all, dmod, w, m, v)


def _small_update(g_all, w, m, v):
    R, L = w.shape

    def body(g_ref, w_ref, m_ref, v_ref, go_ref, d_ref, mo_ref, vo_ref):
        g = g_ref[0]
        for d in range(1, N_DEV):
            g = g + g_ref[d]
        go_ref[...] = g
        d_ref[...], mo_ref[...], vo_ref[...] = _adamw_math(g, w_ref[...], m_ref[...], v_ref[...])

    return pl.pallas_call(body, name="small_update", out_shape=[jax.ShapeDtypeStruct((R, L), F32)] * 4)(g_all, w, m, v)


def _pack(parts, rows):
    flat = jnp.concatenate([p.reshape(-1) for p in parts])
    return jnp.pad(flat, (0, rows * 128 - flat.shape[0])).reshape(rows, 128)


def _unpack(packed, shapes):
    flat, out, at = packed.reshape(-1), [], 0
    for shp in shapes:
        size = 1
        for d in shp:
            size *= d
        out.append(flat[at:at + size].reshape(shp))
        at += size
    return out


def _layer(x, tgt, mod, wts, rel_bias, attn_norm_g, lb_logits, gnorm_g, ln1_g, ln1_b, ln2_g, ln2_b, place=None):
    T, D = x.shape
    aw = attn_norm_g.shape[1]
    shift1, scale1, gate1, shift2, scale2, gate2 = [mod[i:i + 1] for i in range(6)]

    def gather(n, rows=None, into=None):
        return None if place is None else _gather_rider(wts[n], rows, None if into is None else into[0])

    def gathered(n, rode):
        return wts[n] if place is None else lax.dynamic_update_index_in_dim(rode[0], wts[n], place[0], 0)

    def blocks(g):
        return g.reshape(N_CHIPS, -1, g.shape[2])

    def to_sibling(g):
        return None if place is None else _pair_rider(g)

    def pair_sum(n, g, rode=None):
        if place is None:
            return g
        rode = _alone(_pair_rider(g), n + "_send_pair") if rode is None else rode
        return _sum_pair(g, rode[0], place[1], n + "_sum_pair")

    def to_chips(p, rows=None, into=None):
        return None if place is None else _chips_rider(p, rows, None if into is None else into[0])

    def summed(n, p, rode):
        return p if place is None else _sum_chips(p, rode[0], place[1], n + "_sum_chips")

    def to_both(block):
        return None if place is None else _share_rider(block)

    def carrying(mm, *args, rider, **kw):
        return mm(*args, rider=rider, **kw) if rider is not None else (mm(*args, **kw), None)

    w_in = gathered("w_in", None if place is None else _alone(gather("w_in"), "gather_w_in"))
    h1 = _pre_mixer(x, scale1, shift1)
    n_qkv = 3 * aw // 256
    kh_o, kh_f = wts["w_o"].shape[-2] // 2, wts["w_ffn_in"].shape[-2] // 2
    o_cut, f_cuts = 3 * kh_o // 8, (25 * kh_f // 64, 57 * kh_f // 64)
    qkv, rode = carrying(_mm_nn, h1, w_in, tm=KPAD, tn=256, tk=D, name="proj_qkv", cols=(0, n_qkv), o_dtype=BF16,
                         pad_rows=KPAD, rider=gather("w_o", (0, o_cut)))
    proj, rode = carrying(_mm_nn, h1, w_in, tm=1024, tn=256, tk=D, name="proj_rec",
                          cols=(n_qkv, N_CHIPS * w_in.shape[2] // 256), rider=gather("w_o", (o_cut, kh_o - o_cut), rode))
    w_o3 = gathered("w_o", rode).reshape(1, D, D)
    bias = _bias_band(rel_bias)
    (mix_a, probs), rode = _attn_fwd(qkv, bias, attn_norm_g, rider=gather("w_ffn_in", (0, f_cuts[0])))
    (mix_b, o_b, st_all), rode = _hgrn_fwd(
        proj, lb_logits, gnorm_g, rider=gather("w_ffn_in", (f_cuts[0], f_cuts[1] - f_cuts[0]), rode))
    mixin = jnp.concatenate([mix_a, mix_b], axis=1)
    mix = _mm_nn(mixin, w_o3, tm=1024, tn=512, tk=D, name="mix_out")
    if place is None:
        x1, h2 = _post_mixer(mix, x, gate1, ln1_g, ln1_b, scale2, shift2)
    else:
        (x1, h2), rode = _post_mixer(mix, x, gate1, ln1_g, ln1_b, scale2, shift2,
                                     rider=gather("w_ffn_in", (f_cuts[1], kh_f - f_cuts[1]), rode))
    w_ffn_in = gathered("w_ffn_in", rode)
    (gate, up, act), rode = _ffn_in_swiglu(h2, w_ffn_in, tm=1024, tn=256, rider=gather("w_ffn_out"))
    w_out3 = gathered("w_ffn_out", rode)
    w_out3 = w_out3.reshape(1, -1, w_out3.shape[2])
    d_ff = w_out3.shape[1]
    f = _mm_nn(act, w_out3, tm=1024, tn=1024, tk=d_ff // 4, name="ffn_out")
    du2, df, acc2 = _loss_head(f, x1, tgt, gate2, ln2_g, ln2_b)
    loss = (0.5 / D) * jnp.sum(acc2[3])
    g = blocks(_mm_tn(act, df, q=1, tk=512, tn=1024, tt=T, name="g_ffn_out"))
    d_gate_up, rode = _d_act_swiglu(df, w_out3, gate, up, tm=1024, to=512, rider=to_sibling(g))
    p_out = pair_sum("w_ffn_out", g, rode)
    dff = jnp.concatenate(d_gate_up, axis=1)
    cut = 21 * p_out.shape[1] // 44
    dh2, rode = carrying(_mm_nt, dff, w_ffn_in, tm=1024, to=1024, tn=w_ffn_in.shape[2], name="d_h2",
                         rider=to_chips(p_out, (0, cut)))
    g, rode = carrying(_mm_tn, h2, dff, q=N_CHIPS, tk=512, tn=w_ffn_in.shape[2] // 2, tt=T, name="g_ffn_in",
                       rider=to_chips(p_out, (cut, p_out.shape[1] - cut), rode))
    g_ffn_out = summed("w_ffn_out", p_out, rode)
    if place is None:
        (du1, dmix, acc1), p_fin = _mid_bwd(dh2, du2, x1, mix, x, gate1, ln1_g, scale2), g
    else:
        (du1, dmix, acc1), rode = _mid_bwd(dh2, du2, x1, mix, x, gate1, ln1_g, scale2,
                                           rider=_join(to_sibling(g), to_both(g_ffn_out)))
        p_fin, g_ffn_out = pair_sum("w_ffn_in", g, rode[:1]), rode[1]
    g = blocks(_mm_tn(mixin, dmix, q=1, tk=512, tn=1024, tt=T, name="g_o"))
    dmixin, rode = carrying(_mm_nt, dmix, w_o3, tm=1024, to=512, tn=D, name="d_mixin", rider=to_sibling(g))
    p_o = pair_sum("w_o", g, rode)
    cut = p_fin.shape[1] // 2
    (dq, dk, dv, dbias, dgain), rode = _attn_bwd(qkv, probs, attn_norm_g, dmixin, rider=to_chips(p_fin, (0, cut)))
    (dzq, dzf, dxi, dzg, dl0, dgn), rode = _hgrn_bwd(
        proj, lb_logits, gnorm_g, o_b, st_all, dmixin,
        rider=_join(to_chips(p_fin, (cut, p_fin.shape[1] - cut), rode), to_chips(p_o)))
    g_ffn_in, g_o = summed("w_ffn_in", p_fin, rode[:1]), summed("w_o", p_o, rode[1:])
    dproj = jnp.concatenate([dq, dk[KPAD:].astype(BF16), dv[KPAD:].astype(BF16), dzq, dzf, dxi, dzg], axis=1)
    g, rode = carrying(_mm_tn, h1, dproj, q=N_CHIPS, tk=512, tn=w_in.shape[2] // 2, tt=T, name="g_in",
                       rider=_join(to_both(g_ffn_in), to_both(g_o)))
    if place is not None:
        g_ffn_in, g_o = rode
    p_in = pair_sum("w_in", g)
    dh1, rode = carrying(_mm_nt, dproj, w_in, tm=1024, to=1024, tn=w_in.shape[2], name="d_h1", rider=to_chips(p_in))
    g_in = summed("w_in", p_in, rode)
    if place is not None:
        g_in, = _alone(to_both(g_in), "w_in_share")
    grad_x, acc0 = _first_bwd(dh1, du1, x, scale1)
    dmod = jnp.concatenate([acc0[1:2], acc0[0:1], acc1[4:5], acc1[1:2], acc1[0:1], acc2[2:3]], axis=0)
    small = dict(rel_bias=_bias_band_grad(dbias), attn_norm_g=dgain,
                 lb_logits=jnp.concatenate([dl0, -dl0], axis=0), gnorm_g=dgn,
                 ln1_g=acc1[2:3], ln1_b=acc1[3:4], ln2_g=acc2[0:1], ln2_b=acc2[1:2])
    return loss, grad_x, dict(w_in=g_in, w_o=g_o, w_ffn_in=g_ffn_in, w_ffn_out=g_ffn_out), dmod, small


SMALL = ("rel_bias", "attn_norm_g", "lb_logits", "gnorm_g", "ln1_g", "ln1_b", "ln2_g", "ln2_b")
SMALL_ROWS = 256


def kernel(x, c, w_ada, b_ada, w_in, rel_bias, attn_norm_g, lb_logits, gnorm_g, w_o, ln1_g, ln1_b, w_ffn_in, w_ffn_out, ln2_g, ln2_b, loss_target, m_w_ada, m_b_ada, m_w_in, m_rel_bias, m_attn_norm_g, m_lb_logits, m_gnorm_g, m_w_o, m_ln1_g, m_ln1_b, m_w_ffn_in, m_w_ffn_out, m_ln2_g, m_ln2_b, v_w_ada, v_b_ada, v_w_in, v_rel_bias, v_attn_norm_g, v_lb_logits, v_gnorm_g, v_w_o, v_ln1_g, v_ln1_b, v_w_ffn_in, v_w_ffn_out, v_ln2_g, v_ln2_b):
    mx, my, mc = _place()
    me = 4 * mx + 2 * my + mc
    chip = 2 * mx + my
    sel = jnp.stack([chip, mc]).astype(jnp.int32)
    D = x.shape[2]
    ns_ada = w_ada.shape[2]

    big = dict(w_in=(w_in, m_w_in, v_w_in), w_o=(w_o, m_w_o, v_w_o), w_ffn_in=(w_ffn_in, m_w_ffn_in, v_w_ffn_in),
               w_ffn_out=(w_ffn_out, m_w_ffn_out, v_w_ffn_out))
    shards = {n: t[0][0].astype(BF16) for n, t in big.items()}

    c_all = _gather_small(c.reshape(D // 128, 128), "gather_c").reshape(N_DEV, D)
    c_all = jnp.pad(c_all, ((0, 16 - N_DEV), (0, 0)))
    mod_cols = _ada_fwd(c_all, w_ada[0])[:N_DEV]
    mod_all = _gather_small(mod_cols.reshape(-1, 128), "gather_mod").reshape(N_DEV, N_DEV, ns_ada)
    mod = lax.dynamic_index_in_dim(mod_all[::2], me, axis=1, keepdims=False)
    mod = (mod.reshape(1, -1) + b_ada).reshape(6, D)

    loss, grad_x, g_big, dmod, g_small = _layer(
        x[0], loss_target[0], mod, shards, rel_bias[0], attn_norm_g, lb_logits, gnorm_g, ln1_g, ln1_b, ln2_g, ln2_b,
        place=(chip, sel))

    grads, deltas, new_m, new_v = {}, {}, {}, {}
    for n, (w, m, v) in big.items():
        g, d, mo, vo = _adamw(g_big[n], w[0], m[0], v[0], "adamw_" + n)
        grads[n], deltas[n], new_m[n], new_v[n] = g[None], d[None], mo[None], vo[None]

    small_in = dict(rel_bias=(rel_bias, m_rel_bias, v_rel_bias), attn_norm_g=(attn_norm_g, m_attn_norm_g, v_attn_norm_g),
                    lb_logits=(lb_logits, m_lb_logits, v_lb_logits), gnorm_g=(gnorm_g, m_gnorm_g, v_gnorm_g),
                    ln1_g=(ln1_g, m_ln1_g, v_ln1_g), ln1_b=(ln1_b, m_ln1_b, v_ln1_b), ln2_g=(ln2_g, m_ln2_g, v_ln2_g),
                    ln2_b=(ln2_b, m_ln2_b, v_ln2_b))
    g_all = _gather_small(_pack([dmod] + [g_small[n] for n in SMALL] + [loss], SMALL_ROWS), "gather_small")
    packed = [_pack([t] + [small_in[n][i] for n in SMALL] + [jnp.zeros((), F32)], SMALL_ROWS)
              for i, t in enumerate((b_ada, m_b_ada, v_b_ada))]
    shapes = [b_ada.shape] + [small_in[n][0].shape for n in SMALL] + [()]
    outs = [_unpack(o, shapes) for o in _small_update(g_all, *packed)]
    loss = outs[0][-1]
    for i, n in enumerate(("b_ada",) + SMALL):
        grads[n], deltas[n], new_m[n], new_v[n] = outs[0][i], outs[1][i], outs[2][i], outs[3][i]

    dmod_all = g_all[:, :6 * D // 128].reshape(N_DEV, 6 * D)
    dmod_cols = lax.dynamic_slice_in_dim(dmod_all, chip * ns_ada, ns_ada, axis=1)
    dmod_cols = jnp.pad(dmod_cols, ((0, 16 - N_DEV), (0, 0)))
    g, d, mo, vo = _ada_bwd(c_all, dmod_cols, w_ada[0], m_w_ada[0], v_w_ada[0])
    grads["w_ada"], deltas["w_ada"], new_m["w_ada"], new_v["w_ada"] = g[None], d[None], mo[None], vo[None]

    order = ("w_ada", "b_ada", "w_in", "rel_bias", "attn_norm_g", "lb_logits", "gnorm_g", "w_o", "ln1_g", "ln1_b",
             "w_ffn_in", "w_ffn_out", "ln2_g", "ln2_b")
    return (loss, grad_x[None], *[grads[n] for n in order], *[deltas[n] for n in order],
            *[new_m[n] for n in order], *[new_v[n] for n in order])
```

```python
import numpy as np
import jax
import jax.numpy as jnp
from jax import lax
from jax.experimental import pallas as pl
from jax.experimental.pallas import tpu as pltpu

F32 = jnp.float32
BF16 = jnp.bfloat16
MESH = pl.DeviceIdType.MESH
HIGHEST = lax.Precision.HIGHEST

CHUNK = 64
N_PAST = 8
QG = 4
QROWS = QG * CHUNK
KPAD = N_PAST * CHUNK
ZPAD = 2 * KPAD
UNION = (QG + N_PAST) * CHUNK
BAND = (N_PAST + 1) * CHUNK
HD_A = 64
HD_B = 128
SUB = 16
HGRN_HEADS = 4
MAX_REL = 256
EPS = 1e-5
ALPHA = 2.0 ** 0.25
LR, B1, B2, ADAM_EPS, WD, STEP = 1e-3, 0.9, 0.999, 1e-8, 0.01, 10
N_CHIPS = 4
N_DEV = 8
NEG = -1e30
TILE_BYTES = 3 << 19

NN = ((1,), (0,))
NT = ((1,), (1,))
TN = ((0,), (0,))


def _dot(a, b, dims=NN, precision=None):
    return lax.dot_general(a, b, (dims, ((), ())), preferred_element_type=F32, precision=precision)


def _params(sem=None, vmem_mb=None, **kw):
    return pltpu.CompilerParams(dimension_semantics=sem,
                                vmem_limit_bytes=None if vmem_mb is None else vmem_mb << 20, **kw)


def _row_tile(rows, cols):
    for cand in (512, 256, 128, 64, 32, 16, 8):
        if rows % cand == 0 and cand * cols * 4 <= TILE_BYTES:
            return cand
    raise ValueError((rows, cols))


def _place():
    return lax.axis_index("x"), lax.axis_index("y"), lax.axis_index("c")


def _flip(v, bit):
    return 1 - v if bit else v


ANY = pl.BlockSpec(memory_space=pl.ANY)
CHIP_FLIPS = ((1, 0), (0, 1), (1, 1))


class _Rider:
    def __init__(self, operands, out_shape, n_sems, start, finish, aliases=None):
        self.operands, self.out_shape, self.n_sems, self.start, self.finish = operands, out_shape, n_sems, start, finish
        self.aliases = aliases or {}


def _call(body, rider, *, name, grid, in_specs, out_specs, out_shape, scratch_shapes, compiler_params, operands):
    if rider is None:
        outs = pl.pallas_call(body, name=name, grid=grid, in_specs=in_specs, out_specs=out_specs, out_shape=out_shape,
                              scratch_shapes=scratch_shapes, compiler_params=compiler_params)(*operands)
        return list(outs), []
    n_in, n_out, n_sc = len(in_specs), len(out_specs), len(scratch_shapes)
    r_in, r_out = len(rider.operands), len(rider.out_shape)

    def carried(*refs):
        refs = list(refs)
        cuts = [n_in, r_in, n_out, r_out, n_sc]
        ins, r_ins, outs, r_outs, scratch = [[refs.pop(0) for _ in range(n)] for n in cuts]
        first, last = None, None
        for axis, size in enumerate(grid):
            i = pl.program_id(axis)
            first = (i == 0) if first is None else first & (i == 0)
            last = (i == size - 1) if last is None else last & (i == size - 1)

        @pl.when(first)
        def _():
            rider.start(r_ins, r_outs, *refs)

        body(*ins, *outs, *scratch)

        @pl.when(last)
        def _():
            rider.finish(r_ins, r_outs, *refs)

    sems = [pltpu.SemaphoreType.DMA((rider.n_sems,)), pltpu.SemaphoreType.DMA((rider.n_sems,))]
    outs = pl.pallas_call(carried, name=name, grid=grid, in_specs=list(in_specs) + [ANY] * r_in,
                          out_specs=list(out_specs) + [ANY] * r_out, out_shape=list(out_shape) + rider.out_shape,
                          scratch_shapes=list(scratch_shapes) + sems, compiler_params=compiler_params,
                          input_output_aliases={n_in + i: n_out + o for i, o in rider.aliases.items()},
                          )(*operands, *rider.operands)
    return list(outs[:n_out]), list(outs[n_out:])


def _alone(rider, name):
    def body(*refs):
        ins, outs, sems = refs[:len(rider.operands)], refs[len(rider.operands):-2], refs[-2:]
        rider.start(ins, outs, *sems)
        rider.finish(ins, outs, *sems)

    return pl.pallas_call(
        body, name=name, in_specs=[ANY] * len(rider.operands), out_specs=[ANY] * len(rider.out_shape),
        out_shape=rider.out_shape, input_output_aliases=rider.aliases,
        scratch_shapes=[pltpu.SemaphoreType.DMA((rider.n_sems,)), pltpu.SemaphoreType.DMA((rider.n_sems,))],
    )(*rider.operands)


class _Sems:
    def __init__(self, sems, base):
        self.sems, self.base = sems, base

    @property
    def at(self):
        return self

    def __getitem__(self, k):
        return self.sems.at[self.base + k]


def _join(*riders):
    riders = [r for r in riders if r is not None]
    if len(riders) < 2:
        return riders[0] if riders else None

    def parts(ins, outs, send_sems, recv_sems):
        i = o = s = 0
        for r in riders:
            ni, no = len(r.operands), len(r.out_shape)
            yield r, ins[i:i + ni], outs[o:o + no], _Sems(send_sems, s), _Sems(recv_sems, s)
            i, o, s = i + ni, o + no, s + r.n_sems

    def start(*refs):
        for r, *args in parts(*refs):
            r.start(*args)

    def finish(*refs):
        for r, *args in parts(*refs):
            r.finish(*args)

    aliases, i, o = {}, 0, 0
    for r in riders:
        aliases.update({i + a: o + b for a, b in r.aliases.items()})
        i, o = i + len(r.operands), o + len(r.out_shape)
    return _Rider([a for r in riders for a in r.operands], [s for r in riders for s in r.out_shape],
                  sum(r.n_sems for r in riders), start, finish, aliases)


def _gather_rider(shard, rows=None, into=None):
    K, Ns = shard.shape
    kh = K // 2
    first_row, n_rows = rows or (0, kh)

    def copies(w_ref, out_ref, send_sems, recv_sems):
        x, y, c = _place()
        chips = [(_flip(x, fx), _flip(y, fy)) for fx, fy in CHIP_FLIPS]

        def half(chip, which):
            return out_ref.at[2 * chip[0] + chip[1], pl.ds(which * kh + first_row, n_rows), :]

        def copy(k, dst, to, src=None):
            return pltpu.make_async_remote_copy(src_ref=dst if src is None else src, dst_ref=dst,
                                                send_sem=send_sems.at[k], recv_sem=recv_sems.at[k],
                                                device_id=to, device_id_type=MESH)

        def first():
            return [copy(j, half((x, y), c), (*chip, c), src=w_ref.at[pl.ds(c * kh + first_row, n_rows), :])
                    for j, chip in enumerate(chips)]

        def onward():
            return [copy(3 + j, half(chip, c), (x, y, 1 - c)) for j, chip in enumerate(chips)]

        def arriving(base, which):
            return [copy(base + j, half(chip, which), (x, y, c)) for j, chip in enumerate(chips)]

        return first, onward, arriving

    def start(ins, outs, send_sems, recv_sems):
        for cp in copies(ins[0], outs[0], send_sems, recv_sems)[0]():
            cp.start()

    def finish(ins, outs, send_sems, recv_sems):
        x, y, c = _place()
        first, onward, arriving = copies(ins[0], outs[0], send_sems, recv_sems)
        passed = onward()
        for arrived, cp in zip(arriving(0, c), passed):
            arrived.wait_recv()
            cp.start()
        for arrived in arriving(3, 1 - c):
            arrived.wait_recv()
        for cp in first() + passed:
            cp.wait_send()

    full = jax.ShapeDtypeStruct((N_CHIPS, K, Ns), shard.dtype)
    if into is None:
        return _Rider([shard], [full], 6, start, finish)
    return _Rider([shard, into], [full], 6, start, finish, aliases={1: 0})


def _pair_rider(g_full):
    Q, K, Ns = g_full.shape
    kh = K // 2

    def copy(g_ref, got_ref, send_sems, recv_sems):
        x, y, c = _place()
        return pltpu.make_async_remote_copy(src_ref=g_ref.at[:, pl.ds((1 - c) * kh, kh), :], dst_ref=got_ref,
                                            send_sem=send_sems.at[0], recv_sem=recv_sems.at[0],
                                            device_id=(x, y, 1 - c), device_id_type=MESH)

    def start(ins, outs, send_sems, recv_sems):
        copy(ins[0], outs[0], send_sems, recv_sems).start()

    def finish(ins, outs, send_sems, recv_sems):
        copy(ins[0], outs[0], send_sems, recv_sems).wait()

    return _Rider([g_full], [jax.ShapeDtypeStruct((Q, kh, Ns), g_full.dtype)], 1, start, finish)


def _share_rider(block):
    K, Ns = block.shape
    kh = K // 2

    def halves(out_ref):
        x, y, c = _place()
        return out_ref.at[pl.ds(c * kh, kh), :], out_ref.at[pl.ds((1 - c) * kh, kh), :], (x, y, 1 - c)

    def start(ins, outs, send_sems, recv_sems):
        mine, _, sibling = halves(outs[0])
        pltpu.make_async_remote_copy(src_ref=mine, dst_ref=mine, send_sem=send_sems.at[0], recv_sem=recv_sems.at[0],
                                     device_id=sibling, device_id_type=MESH).start()

    def finish(ins, outs, send_sems, recv_sems):
        mine, theirs, sibling = halves(outs[0])
        pltpu.make_async_remote_copy(src_ref=theirs, dst_ref=theirs, send_sem=send_sems.at[0], recv_sem=recv_sems.at[0],
                                     device_id=sibling, device_id_type=MESH).wait_recv()
        pltpu.make_async_remote_copy(src_ref=mine, dst_ref=mine, send_sem=send_sems.at[0], recv_sem=recv_sems.at[0],
                                     device_id=sibling, device_id_type=MESH).wait_send()

    return _Rider([block], [jax.ShapeDtypeStruct((K, Ns), block.dtype)], 1, start, finish, aliases={0: 0})


def _chips_rider(pair_sum, rows=None, into=None):
    Q, kh, Ns = pair_sum.shape
    first_row, n_rows = rows or (0, kh)

    def copies(p_ref, got_ref, send_sems, recv_sems):
        x, y, c = _place()
        part = pl.ds(first_row, n_rows)
        out = []
        for j, (fx, fy) in enumerate(CHIP_FLIPS):
            px, py = _flip(x, fx), _flip(y, fy)
            out.append(pltpu.make_async_remote_copy(
                src_ref=p_ref.at[2 * px + py, part, :], dst_ref=got_ref.at[j, part, :], send_sem=send_sems.at[j],
                recv_sem=recv_sems.at[j], device_id=(px, py, c), device_id_type=MESH))
        return out

    def start(ins, outs, send_sems, recv_sems):
        for cp in copies(ins[0], outs[0], send_sems, recv_sems):
            cp.start()

    def finish(ins, outs, send_sems, recv_sems):
        sends = copies(ins[0], outs[0], send_sems, recv_sems)
        for cp in sends:
            cp.wait_recv()
        for cp in sends:
            cp.wait_send()

    got = jax.ShapeDtypeStruct((Q - 1, kh, Ns), pair_sum.dtype)
    if into is None:
        return _Rider([pair_sum], [got], 3, start, finish)
    return _Rider([pair_sum, into], [got], 3, start, finish, aliases={1: 0})


def _mm(a, b, *, grid, a_spec, b_spec, o_spec, o_shape, o_dtype, dims, acc_shape, name, rider=None, zero_rows=0,
        vmem_mb=48):
    nk = grid[2]

    def body(a_ref, b_ref, o_ref, *scratch):
        if zero_rows:
            @pl.when(pl.program_id(0) < zero_rows)
            def _():
                o_ref[...] = jnp.zeros_like(o_ref)

            @pl.when(pl.program_id(0) >= zero_rows)
            def _():
                o_ref[...] = _dot(a_ref[...], b_ref[...], dims).astype(o_ref.dtype)
            return
        part = _dot(a_ref[...], b_ref[...], dims)
        if nk == 1:
            o_ref[...] = part.astype(o_ref.dtype)
            return
        acc_ref, = scratch
        k = pl.program_id(2)

        @pl.when(k == 0)
        def _():
            acc_ref[...] = part

        @pl.when(k > 0)
        def _():
            acc_ref[...] += part

        @pl.when(k == nk - 1)
        def _():
            o_ref[...] = acc_ref[...].astype(o_ref.dtype)

    (out,), rode = _call(
        body, rider, name=name, grid=grid, in_specs=[a_spec, b_spec], out_specs=[o_spec],
        out_shape=[jax.ShapeDtypeStruct(o_shape, o_dtype)],
        scratch_shapes=[] if nk == 1 else [pltpu.VMEM(acc_shape, F32)],
        compiler_params=_params(("parallel", "parallel", "arbitrary") if rider is None else ("arbitrary",) * 3, vmem_mb),
        operands=(a, b))
    return out if rider is None else (out, rode)


def _mm_nn(a, w, *, tm, tn, tk, name, rider=None, cols=None, o_dtype=F32, pad_rows=0):
    T, K = a.shape
    Q, _, Ns = w.shape
    nbs = Ns // tn
    tm = min(tm, T)
    j0, j1 = cols or (0, Q * nbs)
    lead = pad_rows // tm
    return _mm(a, w, grid=(lead + T // tm, j1 - j0, K // tk),
               a_spec=pl.BlockSpec((tm, tk), lambda i, j, k: (jnp.maximum(i - lead, 0), k)),
               b_spec=pl.BlockSpec((None, tk, tn), lambda i, j, k: ((j + j0) // nbs, k, (j + j0) % nbs)),
               o_spec=pl.BlockSpec((tm, tn), lambda i, j, k: (i, j)),
               o_shape=(pad_rows + T, (j1 - j0) * tn), o_dtype=o_dtype, dims=NN, acc_shape=(tm, tn), name=name,
               rider=rider, zero_rows=lead)


def _mm_nt(g, w, *, tm, to, tn, name, rider=None):
    T = g.shape[0]
    Q, K, Ns = w.shape
    nbs = Ns // tn
    tm = min(tm, T)
    return _mm(g, w, grid=(T // tm, K // to, Q * nbs),
               a_spec=pl.BlockSpec((tm, tn), lambda i, j, n: (i, n)),
               b_spec=pl.BlockSpec((None, to, tn), lambda i, j, n: (n // nbs, j, n % nbs)),
               o_spec=pl.BlockSpec((tm, to), lambda i, j, n: (i, j)),
               o_shape=(T, K), o_dtype=F32, dims=NT, acc_shape=(tm, to), name=name, rider=rider)


def _mm_tn(a, g, *, q, tk, tn, tt, name, rider=None, every=(1, 0)):
    T, K = a.shape
    Ns = g.shape[1] // q
    nbs = Ns // tn
    step, first = every
    return _mm(a, g, grid=(K // tk // step, q * nbs, T // tt),
               a_spec=pl.BlockSpec((tt, tk), lambda i, j, t: (t, step * i + first)),
               b_spec=pl.BlockSpec((tt, tn), lambda i, j, t: (t, j)),
               o_spec=pl.BlockSpec((None, tk, tn), lambda i, j, t: (j // nbs, i, j % nbs)),
               o_shape=(q, K // step, Ns), o_dtype=BF16, dims=TN, acc_shape=(tk, tn), name=name, rider=rider)


def _ln(u):
    mu = jnp.mean(u, axis=-1, keepdims=True)
    d = u - mu
    r = lax.rsqrt(jnp.mean(d * d, axis=-1, keepdims=True) + EPS)
    return d * r, r


def _ln_bwd(dy, un, r):
    return r * (dy - jnp.mean(dy, axis=-1, keepdims=True) - un * jnp.mean(dy * un, axis=-1, keepdims=True))


def _colsum(v):
    return jnp.sum(v, axis=0, keepdims=True)


def _rowwise(name, fn, bigs, vecs, out_dtypes, n_acc, tm=128, rider=None):
    T, D = bigs[0].shape
    nb, nv, no = len(bigs), len(vecs), len(out_dtypes)

    def body(*refs):
        outs, accs = fn([r[...] for r in refs[:nb]], [r[...] for r in refs[nb:nb + nv]])
        for r, o in zip(refs[nb + nv:nb + nv + no], outs):
            r[...] = o.astype(r.dtype)
        if n_acc:
            acc_ref = refs[nb + nv + no]

            @pl.when(pl.program_id(0) == 0)
            def _():
                acc_ref[...] = jnp.zeros_like(acc_ref)

            for row, a in enumerate(accs):
                acc_ref[row:row + 1, :] += a

    big_spec = pl.BlockSpec((tm, D), lambda i: (i, 0))
    vec_spec = pl.BlockSpec((1, D), lambda i: (0, 0))
    out_shape = [jax.ShapeDtypeStruct((T, D), dt) for dt in out_dtypes]
    out_specs = [big_spec] * no
    if n_acc:
        out_shape.append(jax.ShapeDtypeStruct((8, D), F32))
        out_specs.append(pl.BlockSpec((8, D), lambda i: (0, 0)))
    outs, rode = _call(
        body, rider, name=name, grid=(T // tm,), in_specs=[big_spec] * nb + [vec_spec] * nv,
        out_specs=out_specs, out_shape=out_shape, scratch_shapes=[],
        compiler_params=_params(("arbitrary",), 48), operands=(*bigs, *vecs))
    return outs if rider is None else (outs, rode)


def _pre_mixer(x, scale1, shift1):
    def fn(b, v):
        xn, _ = _ln(b[0])
        return [xn * (1.0 + v[0]) + v[1]], []
    return _rowwise("pre_mixer", fn, [x], [scale1, shift1], [BF16], 0)[0]


def _post_mixer(mix, x, gate1, g1, b1, scale2, shift2, rider=None):
    def fn(b, v):
        un1, _ = _ln(ALPHA * b[1] + v[0] * b[0])
        x1 = un1 * v[1] + v[2]
        xn1, _ = _ln(x1)
        return [x1, xn1 * (1.0 + v[3]) + v[4]], []
    return _rowwise("post_mixer", fn, [mix, x], [gate1, g1, b1, scale2, shift2], [F32, BF16], 0, rider=rider)


def _loss_head(f, x1, tgt, gate2, g2, b2):
    def fn(b, v):
        ff, xx, tt = b
        d_model = ff.shape[-1]
        un2, r2 = _ln(ALPHA * xx + v[0] * ff)
        err = un2 * v[1] + v[2] - tt
        dy = err * (1.0 / d_model)
        du2 = _ln_bwd(dy * v[1], un2, r2)
        return [du2, du2 * v[0]], [_colsum(dy * un2), _colsum(dy), _colsum(du2 * ff), _colsum(err * err)]
    return _rowwise("loss_head", fn, [f, x1, tgt], [gate2, g2, b2], [F32, BF16], 4)


def _mid_bwd(dh2, du2, x1, mix, x, gate1, g1, scale2, rider=None):
    def fn(b, v):
        dh, du, xx1, mm, xx = b
        xn1, r1n = _ln(xx1)
        dx1 = ALPHA * du + _ln_bwd(dh * (1.0 + v[2]), xn1, r1n)
        un1, r1 = _ln(ALPHA * xx + v[0] * mm)
        du1 = _ln_bwd(dx1 * v[1], un1, r1)
        return [du1, du1 * v[0]], [_colsum(dh * xn1), _colsum(dh), _colsum(dx1 * un1), _colsum(dx1),
                                   _colsum(du1 * mm)]
    return _rowwise("mid_bwd", fn, [dh2, du2, x1, mix, x], [gate1, g1, scale2], [F32, BF16], 5, rider=rider)


def _first_bwd(dh1, du1, x, scale1):
    def fn(b, v):
        dh, du, xx = b
        xn, r0 = _ln(xx)
        return [ALPHA * du + _ln_bwd(dh * (1.0 + v[0]), xn, r0)], [_colsum(dh * xn), _colsum(dh)]
    return _rowwise("first_bwd", fn, [dh1, du1, x], [scale1], [F32], 2)


def _ffn_in_swiglu(h2, w, *, tm, tn, rider=None):
    T, K = h2.shape
    Q, _, Ns = w.shape
    nbs = Ns // tn
    half = Q * nbs // 2
    tm = min(tm, T)

    def body(a_ref, wg_ref, wu_ref, g_ref, u_ref, act_ref):
        a = a_ref[...]
        g, u = _dot(a, wg_ref[...]), _dot(a, wu_ref[...])
        g_ref[...] = g.astype(g_ref.dtype)
        u_ref[...] = u.astype(u_ref.dtype)
        act_ref[...] = (g * jax.nn.sigmoid(g) * u).astype(act_ref.dtype)

    cols = lambda first: pl.BlockSpec((None, K, tn), lambda i, j: ((j + first) // nbs, 0, (j + first) % nbs))
    blk = pl.BlockSpec((tm, tn), lambda i, j: (i, j))
    return _call(
        body, rider, name="ffn_in", grid=(T // tm, half),
        in_specs=[pl.BlockSpec((tm, K), lambda i, j: (i, 0)), cols(0), cols(half)], out_specs=[blk] * 3,
        out_shape=[jax.ShapeDtypeStruct((T, half * tn), BF16)] * 3, scratch_shapes=[],
        compiler_params=_params(("arbitrary", "arbitrary"), 48), operands=(h2, w, w))


def _d_act_swiglu(df, w, gate, up, *, tm, to, rider=None):
    T, N = df.shape
    F = w.shape[1]
    tm = min(tm, T)

    def body(df_ref, w_ref, g_ref, u_ref, dg_ref, du_ref):
        d = _dot(df_ref[...], w_ref[...], NT)
        g = g_ref[...].astype(F32)
        s = jax.nn.sigmoid(g)
        du_ref[...] = (d * g * s).astype(du_ref.dtype)
        dg_ref[...] = (d * u_ref[...].astype(F32) * s * (1.0 + g * (1.0 - s))).astype(dg_ref.dtype)

    blk = pl.BlockSpec((tm, to), lambda i, j: (i, j))
    return _call(
        body, rider, name="d_act", grid=(T // tm, F // to),
        in_specs=[pl.BlockSpec((tm, N), lambda i, j: (i, 0)), pl.BlockSpec((None, to, N), lambda i, j: (0, j, 0)), blk, blk],
        out_specs=[blk, blk], out_shape=[jax.ShapeDtypeStruct((T, F), BF16)] * 2, scratch_shapes=[],
        compiler_params=_params(("arbitrary", "arbitrary"), 48), operands=(df, w, gate, up))


PAIR = 2


def _fill_table(table_ref, band_ref):
    table_ref[...] = jnp.full(table_ref.shape, NEG, F32)
    for e in range(PAIR):
        for g in range(QG):
            table_ref[e, g * CHUNK:(g + 1) * CHUNK, g * CHUNK:g * CHUNK + BAND] = band_ref[e]


def _attn_probs(q_ref, k_ref, bias_ref, e, step):
    start = pl.multiple_of(step * QROWS, QROWS)
    lanes = pl.ds(e * HD_A, HD_A)
    s = _dot(q_ref[:, lanes], k_ref[pl.ds(start + ZPAD - KPAD, UNION), lanes], NT) * (HD_A ** -0.5) + bias_ref[e]
    col = lax.broadcasted_iota(jnp.int32, s.shape, 1)
    s = jnp.where(col + start >= KPAD, s, NEG)
    p = jnp.exp(s - jnp.max(s, axis=-1, keepdims=True))
    return p / jnp.sum(p, axis=-1, keepdims=True), start


def _attn_specs(T, n_pairs):
    wide = PAIR * HD_A
    per_step = pl.BlockSpec((QROWS, wide), lambda hp, n: (n, hp))
    queries = pl.BlockSpec((QROWS, wide), lambda hp, n: (n + ZPAD // QROWS, hp))
    keys = pl.BlockSpec((ZPAD + T, wide), lambda hp, n: (0, n_pairs + hp))
    values = pl.BlockSpec((ZPAD + T, wide), lambda hp, n: (0, 2 * n_pairs + hp))
    grads = pl.BlockSpec((KPAD + T, wide), lambda hp, n: (0, hp))
    table = pl.BlockSpec((PAIR, CHUNK, BAND), lambda hp, n: (hp, 0, 0))
    vec = pl.BlockSpec((1, wide), lambda hp, n: (0, hp))
    return per_step, queries, keys, values, grads, table, vec


def _probs_spec():
    return pl.BlockSpec((PAIR, QROWS, UNION), lambda hp, n: (hp, n, 0))


def _attn_fwd(qkv, bias, gain, rider=None):
    T = qkv.shape[0] - ZPAD
    W = gain.shape[1]
    n_pairs = W // (PAIR * HD_A)

    def body(q_ref, k_ref, v_ref, band_ref, gain_ref, o_ref, p_ref, table_ref):
        @pl.when(pl.program_id(1) == 0)
        def _():
            _fill_table(table_ref, band_ref)

        for e in range(PAIR):
            lanes = pl.ds(e * HD_A, HD_A)
            p, start = _attn_probs(q_ref, k_ref, table_ref, e, pl.program_id(1))
            p_ref[e] = p.astype(p_ref.dtype)
            o = _dot(p_ref[e], v_ref[pl.ds(start + ZPAD - KPAD, UNION), lanes])
            rr = lax.rsqrt(jnp.mean(o * o, axis=-1, keepdims=True) + EPS)
            o_ref[:, lanes] = (o * rr * gain_ref[:, lanes]).astype(o_ref.dtype)

    per_step, queries, keys, values, _, table, vec = _attn_specs(T, n_pairs)
    return _call(
        body, rider, name="attn_fwd", grid=(n_pairs, T // QROWS), in_specs=[queries, keys, values, table, vec],
        out_specs=[per_step, _probs_spec()],
        out_shape=[jax.ShapeDtypeStruct((T, W), BF16), jax.ShapeDtypeStruct((n_pairs * PAIR, T, UNION), BF16)],
        scratch_shapes=[pltpu.VMEM((PAIR, QROWS, UNION), F32)],
        compiler_params=_params(("arbitrary", "arbitrary"), 40), operands=(qkv, qkv, qkv, bias, gain))


def _attn_bwd(qkv, probs, gain, dmixin, rider=None):
    T = qkv.shape[0] - ZPAD
    W = gain.shape[1]
    n_pairs = W // (PAIR * HD_A)
    scale = HD_A ** -0.5

    def body(q_ref, k_ref, v_ref, p_ref, gain_ref, don_ref, dq_ref, dk_ref, dv_ref, dband_ref, dgain_ref, dtable_ref):
        n = pl.program_id(1)

        @pl.when(n == 0)
        def _():
            dk_ref[...] = jnp.zeros_like(dk_ref)
            dv_ref[...] = jnp.zeros_like(dv_ref)
            dtable_ref[...] = jnp.zeros_like(dtable_ref)
            dgain_ref[...] = jnp.zeros_like(dgain_ref)

        for e in range(PAIR):
            lanes = pl.ds(e * HD_A, HD_A)
            start = pl.multiple_of(n * QROWS, QROWS)
            keys, in_qkv = pl.ds(start, UNION), pl.ds(start + ZPAD - KPAD, UNION)
            pb = p_ref[e]
            p = pb.astype(F32)
            vb = v_ref[in_qkv, lanes]
            o = _dot(pb, vb)
            rr = lax.rsqrt(jnp.mean(o * o, axis=-1, keepdims=True) + EPS)
            on = o * rr
            d_on = don_ref[:, lanes]
            dgain_ref[:, lanes] += _colsum(d_on * on)
            dyo = d_on * gain_ref[:, lanes]
            do = rr * (dyo - on * jnp.mean(dyo * on, axis=-1, keepdims=True))
            dob = do.astype(BF16)
            dp = _dot(dob, vb, NT)
            ds = p * (dp - jnp.sum(do * o, axis=-1, keepdims=True))
            dtable_ref[e] += ds
            dsb = ds.astype(BF16)
            dq_ref[:, lanes] = (_dot(dsb, k_ref[in_qkv, lanes]) * scale).astype(dq_ref.dtype)
            dk_ref[keys, lanes] += _dot(dsb, q_ref[:, lanes], TN) * scale
            dv_ref[keys, lanes] += _dot(pb, dob, TN)

        @pl.when(n == T // QROWS - 1)
        def _():
            for e in range(PAIR):
                dband_ref[e] = sum(dtable_ref[e, g * CHUNK:(g + 1) * CHUNK, g * CHUNK:g * CHUNK + BAND]
                                   for g in range(QG))

    per_step, queries, keys, values, grads, table, vec = _attn_specs(T, n_pairs)
    H = n_pairs * PAIR
    return _call(
        body, rider, name="attn_bwd", grid=(n_pairs, T // QROWS),
        in_specs=[queries, keys, values, _probs_spec(), vec, per_step],
        out_specs=[per_step, grads, grads, table, vec],
        out_shape=[jax.ShapeDtypeStruct((T, W), BF16), jax.ShapeDtypeStruct((KPAD + T, W), F32),
                   jax.ShapeDtypeStruct((KPAD + T, W), F32), jax.ShapeDtypeStruct((H, CHUNK, BAND), F32),
                   jax.ShapeDtypeStruct((1, W), F32)],
        scratch_shapes=[pltpu.VMEM((PAIR, QROWS, UNION), F32)],
        compiler_params=_params(("arbitrary", "arbitrary"), 40),
        operands=(qkv, qkv, qkv, probs, gain, dmixin))


N_DIAG = CHUNK + BAND - 1


def _bias_band(rel_bias):
    H = rel_bias.shape[0]
    idx = np.clip(BAND - 1 - np.arange(N_DIAG), -MAX_REL, MAX_REL) + MAX_REL
    rolled = rel_bias[:, idx[(np.arange(N_DIAG) + CHUNK - 1) % N_DIAG]]
    flat = jnp.broadcast_to(rolled[:, None, :], (H, CHUNK, N_DIAG)).reshape(H, CHUNK * N_DIAG)
    return flat[:, :CHUNK * (N_DIAG - 1)].reshape(H, CHUNK, N_DIAG - 1)[:, :, :BAND]


def _bias_band_grad(dband):
    H = dband.shape[0]
    skew = jnp.pad(dband, ((0, 0), (0, 0), (CHUNK - 1, 0))).reshape(H, CHUNK * N_DIAG)
    skew = jnp.pad(skew, ((0, 0), (0, CHUNK))).reshape(H, CHUNK, N_DIAG + 1)
    diag = jnp.sum(skew, axis=1)[:, :N_DIAG]
    n_far = BAND - MAX_REL
    far = jnp.sum(diag[:, :n_far], axis=1, keepdims=True)
    near = diag[:, n_far:][:, ::-1]
    zeros = jnp.zeros((H, MAX_REL - (CHUNK - 1)), F32)
    return jnp.concatenate([zeros, near, far], axis=1)


def _tri(n, lower):
    r = lax.broadcasted_iota(jnp.int32, (n, n), 0)
    c = lax.broadcasted_iota(jnp.int32, (n, n), 1)
    return jnp.where((c <= r) if lower else (c >= r), 1.0, 0.0).astype(F32)


def _hgrn_gates(zq_ref, zf_ref, lbl_ref, q_s, k_s, b_s):
    lb = jax.nn.sigmoid(lbl_ref[0:1, :] - lbl_ref[1:2, :])
    zq = zq_ref[...]
    sig = jax.nn.sigmoid(zf_ref[...])
    f = lb + (1.0 - lb) * sig
    sq = jax.nn.sigmoid(zq)
    q_s[...] = zq * sq
    k_s[...] = 1.0 - f
    b_s[...] = _dot(_tri(CHUNK, True), jnp.log(f), precision=HIGHEST)
    return lb, sig, f, sq


def _sub_rows(i):
    return pl.ds(i * SUB, SUB)


def _row_mask(s):
    return lax.broadcasted_iota(jnp.int32, (SUB, HD_B), 0) >= s


def _decay_from(b_sub, b_row, s):
    return jnp.where(_row_mask(s), jnp.exp(jnp.minimum(b_sub - b_row, 0.0)), 0.0)


def _hgrn_fwd(proj, lb_logits, gnorm_g, rider=None):
    T = proj.shape[0]
    nC = T // CHUNK
    W = lb_logits.shape[1]
    G = W // HD_B // HGRN_HEADS
    col0 = (proj.shape[1] - 4 * W) // (HD_B * HGRN_HEADS)
    wide = HGRN_HEADS * HD_B

    def body(*refs):
        @pl.when(pl.program_id(1) == 0)
        def _():
            refs[9][...] = jnp.zeros_like(refs[9])

        for h in range(HGRN_HEADS):
            lanes = pl.ds(h * HD_B, HD_B)
            one_head(*[r.at[:, lanes] for r in refs[:5]], refs[5], *[r.at[:, lanes] for r in refs[6:8]],
                     *[r.at[h] for r in refs[8:]])

    def one_head(zq_ref, zf_ref, xi_ref, zg_ref, lbl_ref, gn_ref, mix_ref, o_ref, stall_ref, st_ref, q_s, k_s, b_s, acc_s):
        _hgrn_gates(zq_ref, zf_ref, lbl_ref, q_s, k_s, b_s)
        q, k, b = q_s[...], k_s[...], b_s[...]
        st = st_ref[...]
        stall_ref[...] = st
        b_last = b_s[CHUNK - 1:CHUNK, :]
        acc_s[...] = _dot((q * jnp.exp(b)).astype(BF16), st.astype(BF16), NT)
        for i in range(CHUNK // SUB):
            rows = _sub_rows(i)
            q_i, b_i = q_s[rows, :], b_s[rows, :]
            acc = jnp.zeros((SUB, HD_B), F32)
            if i:
                past = pl.ds(0, i * SUB)
                b_ref = b_s[i * SUB - 1:i * SUB, :]
                qs = (q_i * jnp.exp(b_i - b_ref)).astype(BF16)
                ks = (k_s[past, :] * jnp.exp(b_ref - b_s[past, :])).astype(BF16)
                acc += _dot(_dot(qs, ks, NT).astype(BF16), xi_ref[past, :].astype(BF16))
            for s in range(SUB):
                row = pl.ds(i * SUB + s, 1)
                w = q_i * _decay_from(b_i, b_s[row, :], s)
                acc += jnp.sum(w * k_s[row, :], axis=-1, keepdims=True) * xi_ref[row, :]
            acc_s[rows, :] += acc
        o = acc_s[...]
        kd = (k * jnp.exp(b_last - b)).astype(BF16)
        st_ref[...] = st * jnp.exp(b_last) + _dot(xi_ref[...].astype(BF16), kd, TN)
        o_ref[...] = o
        zg = zg_ref[...]
        rr = lax.rsqrt(jnp.mean(o * o, axis=-1, keepdims=True) + EPS)
        mix_ref[...] = (o * rr * gn_ref[...] * (zg * jax.nn.sigmoid(zg))).astype(mix_ref.dtype)

    col = lambda part: pl.BlockSpec((CHUNK, wide), lambda g, n: (n, col0 + part * G + g))
    out_blk = pl.BlockSpec((CHUNK, wide), lambda g, n: (n, g))
    tile = pltpu.VMEM((HGRN_HEADS, CHUNK, HD_B), F32)
    return _call(
        body, rider, name="hgrn_fwd", grid=(G, nC),
        in_specs=[col(0), col(1), col(2), col(3), pl.BlockSpec((2, wide), lambda g, n: (0, g)),
                  pl.BlockSpec((1, HD_B), lambda g, n: (0, 0))],
        out_specs=[out_blk, out_blk, pl.BlockSpec((HGRN_HEADS, None, HD_B, HD_B), lambda g, n: (g, n, 0, 0))],
        out_shape=[jax.ShapeDtypeStruct((T, W), BF16), jax.ShapeDtypeStruct((T, W), F32),
                   jax.ShapeDtypeStruct((G * HGRN_HEADS, nC, HD_B, HD_B), F32)],
        scratch_shapes=[pltpu.VMEM((HGRN_HEADS, HD_B, HD_B), F32), tile, tile, tile, tile],
        compiler_params=_params(("arbitrary", "arbitrary")),
        operands=(proj, proj, proj, proj, lb_logits, gnorm_g))


def _hgrn_bwd(proj, lb_logits, gnorm_g, o_b, st_all, dmixin, rider=None):
    T = proj.shape[0]
    nC = T // CHUNK
    W = lb_logits.shape[1]
    G = W // HD_B // HGRN_HEADS
    wide = HGRN_HEADS * HD_B
    col0 = (proj.shape[1] - 4 * W) // wide
    dcol0 = (dmixin.shape[1] - W) // wide

    def body(*refs):
        g, n = pl.program_id(0), pl.program_id(1)
        dl0_ref, dgn_ref, dst_ref = refs[13:16]

        @pl.when(n == 0)
        def _():
            dst_ref[...] = jnp.zeros_like(dst_ref)
            dl0_ref[...] = jnp.zeros_like(dl0_ref)

        @pl.when((n == 0) & (g == 0))
        def _():
            dgn_ref[...] = jnp.zeros_like(dgn_ref)

        for h in range(HGRN_HEADS):
            lanes = pl.ds(h * HD_B, HD_B)
            cut = lambda r: r.at[:, lanes]
            one_head(*[cut(r) for r in refs[:5]], refs[5], cut(refs[6]), refs[7].at[h], cut(refs[8]),
                     *[cut(r) for r in refs[9:14]], dgn_ref, *[r.at[h] for r in refs[15:]])

    def one_head(zq_ref, zf_ref, xi_ref, zg_ref, lbl_ref, gn_ref, o_ref, st_ref, dout_ref,
                 dzq_ref, dzf_ref, dxi_ref, dzg_ref, dl0_ref, dgn_ref, dst_ref, q_s, k_s, b_s, do_s, dq_s, dk_s, di_s):
        lb, sig, f, sq = _hgrn_gates(zq_ref, zf_ref, lbl_ref, q_s, k_s, b_s)
        q, k, b = q_s[...], k_s[...], b_s[...]
        zg, o, dout = zg_ref[...], o_ref[...], dout_ref[...]
        sg = jax.nn.sigmoid(zg)
        rr = lax.rsqrt(jnp.mean(o * o, axis=-1, keepdims=True) + EPS)
        on = o * rr
        gn = gn_ref[...]
        dzg_ref[...] = (dout * on * gn * sg * (1.0 + zg * (1.0 - sg))).astype(dzg_ref.dtype)
        d_on = dout * zg * sg
        dgn_ref[...] += _colsum(d_on * on)
        d_on = d_on * gn
        do = rr * (d_on - on * jnp.mean(d_on * on, axis=-1, keepdims=True))
        do_s[...] = do
        dob = do.astype(BF16)
        st, dst = st_ref[...], dst_ref[...]
        b_last = b_s[CHUNK - 1:CHUNK, :]
        eb, e_last, k_dec = jnp.exp(b), jnp.exp(b_last), jnp.exp(b_last - b)
        qt, kd = q * eb, k * k_dec
        dstb = dst.astype(BF16)
        xib = xi_ref[...].astype(BF16)
        d_kd = _dot(xib, dstb)
        dq_s[...] = _dot(dob, st.astype(BF16)) * eb
        dk_s[...] = d_kd * k_dec
        di_s[...] = _dot(kd.astype(BF16), dstb, NT)
        d_b_last = e_last * _colsum(st * dst) + _colsum(d_kd * kd)
        dst_ref[...] = _dot(dob, qt.astype(BF16), TN) + dst * e_last
        for i in range(CHUNK // SUB):
            rows = _sub_rows(i)
            q_i, b_i, do_i = q_s[rows, :], b_s[rows, :], do_s[rows, :]
            dq_i = jnp.zeros((SUB, HD_B), F32)
            if i:
                past = pl.ds(0, i * SUB)
                b_ref = b_s[i * SUB - 1:i * SUB, :]
                e_q, e_k = jnp.exp(b_i - b_ref), jnp.exp(b_ref - b_s[past, :])
                qs, ks = (q_i * e_q).astype(BF16), (k_s[past, :] * e_k).astype(BF16)
                xi_p, do_b = xi_ref[past, :].astype(BF16), do_i.astype(BF16)
                di_s[past, :] += _dot(_dot(ks, qs, NT).astype(BF16), do_b)
                dq_i += _dot(_dot(do_b, xi_p, NT).astype(BF16), ks) * e_q
                dk_s[past, :] += _dot(_dot(xi_p, do_b, NT).astype(BF16), qs) * e_k
            for s in range(SUB):
                row = pl.ds(i * SUB + s, 1)
                k_row, i_row = k_s[row, :], xi_ref[row, :]
                e = _decay_from(b_i, b_s[row, :], s)
                w = q_i * e
                a_col = jnp.sum(w * k_row, axis=-1, keepdims=True)
                da_col = jnp.sum(do_i * i_row, axis=-1, keepdims=True)
                di_s[row, :] += _colsum(a_col * do_i)
                dq_i += da_col * e * k_row
                dk_s[row, :] += _colsum(da_col * w)
            dq_s[rows, :] += dq_i
        dq, dk = dq_s[...], dk_s[...]
        db = q * dq - k * dk
        is_last = lax.broadcasted_iota(jnp.int32, (CHUNK, HD_B), 0) == CHUNK - 1
        db = db + jnp.where(is_last, d_b_last, 0.0)
        df = _dot(_tri(CHUNK, False), db, precision=HIGHEST) / f - dk
        dzf_ref[...] = (df * (1.0 - lb) * sig * (1.0 - sig)).astype(dzf_ref.dtype)
        dl0_ref[...] += _colsum(df * (1.0 - sig)) * (lb * (1.0 - lb))
        zq = zq_ref[...]
        dzq_ref[...] = (dq * sq * (1.0 + zq * (1.0 - sq))).astype(dzq_ref.dtype)
        dxi_ref[...] = di_s[...].astype(dxi_ref.dtype)

    rev = lambda n: nC - 1 - n
    col = lambda part: pl.BlockSpec((CHUNK, wide), lambda g, n: (rev(n), col0 + part * G + g))
    blk = pl.BlockSpec((CHUNK, wide), lambda g, n: (rev(n), g))
    tile = pltpu.VMEM((HGRN_HEADS, CHUNK, HD_B), F32)
    out_big = jax.ShapeDtypeStruct((T, W), BF16)
    return _call(
        body, rider, name="hgrn_bwd", grid=(G, nC),
        in_specs=[col(0), col(1), col(2), col(3), pl.BlockSpec((2, wide), lambda g, n: (0, g)),
                  pl.BlockSpec((1, HD_B), lambda g, n: (0, 0)), blk,
                  pl.BlockSpec((HGRN_HEADS, None, HD_B, HD_B), lambda g, n: (g, rev(n), 0, 0)),
                  pl.BlockSpec((CHUNK, wide), lambda g, n: (rev(n), dcol0 + g))],
        out_specs=[blk, blk, blk, blk, pl.BlockSpec((1, wide), lambda g, n: (0, g)),
                   pl.BlockSpec((1, HD_B), lambda g, n: (0, 0))],
        out_shape=[out_big, out_big, out_big, out_big, jax.ShapeDtypeStruct((1, W), F32),
                   jax.ShapeDtypeStruct((1, HD_B), F32)],
        scratch_shapes=[pltpu.VMEM((HGRN_HEADS, HD_B, HD_B), F32)] + [tile] * 7,
        compiler_params=_params(("arbitrary", "arbitrary")),
        operands=(proj, proj, proj, proj, lb_logits, gnorm_g, o_b, st_all, dmixin))


def _adamw_math(g, w, m, v):
    m = B1 * m + (1.0 - B1) * g
    v = B2 * v + (1.0 - B2) * (g * g)
    m_hat = m / (1.0 - B1 ** STEP)
    v_hat = v / (1.0 - B2 ** STEP)
    return -LR * (m_hat / (jnp.sqrt(v_hat) + ADAM_EPS) + WD * w), m, v


def _adamw(g, w, m, v, name):
    R, C = g.shape
    tr = _row_tile(R, C)

    def body(g_ref, w_ref, m_ref, v_ref, go_ref, d_ref, mo_ref, vo_ref):
        g = g_ref[...]
        go_ref[...] = g
        d_ref[...], mo_ref[...], vo_ref[...] = _adamw_math(g, w_ref[...], m_ref[...], v_ref[...])

    blk = pl.BlockSpec((tr, C), lambda i: (i, 0))
    return pl.pallas_call(
        body, name=name, grid=(R // tr,), in_specs=[blk] * 4, out_specs=[blk] * 4,
        out_shape=[jax.ShapeDtypeStruct((R, C), F32)] * 4, compiler_params=_params(("parallel",), 40),
    )(g, w, m, v)


def _sum_pair(g_full, from_sibling, sel, name):
    Q, K, Ns = g_full.shape
    kh = K // 2
    tr = _row_tile(kh, Ns)
    nh = kh // tr

    def body(sel_ref, a_ref, b_ref, o_ref):
        o_ref[...] = (a_ref[...].astype(F32) + b_ref[...].astype(F32)).astype(o_ref.dtype)

    return pl.pallas_call(
        body, name=name,
        grid_spec=pltpu.PrefetchScalarGridSpec(
            num_scalar_prefetch=1, grid=(Q, nh),
            in_specs=[pl.BlockSpec((None, tr, Ns), lambda q, i, sel: (q, sel[1] * nh + i, 0)),
                      pl.BlockSpec((None, tr, Ns), lambda q, i, sel: (q, i, 0))],
            out_specs=pl.BlockSpec((None, tr, Ns), lambda q, i, sel: (q, i, 0))),
        out_shape=jax.ShapeDtypeStruct((Q, kh, Ns), BF16), compiler_params=_params(("parallel", "parallel")),
    )(sel, g_full, from_sibling)


def _sum_chips(pair_sum, from_chips, sel, name):
    Q, kh, Ns = pair_sum.shape
    tr = _row_tile(kh, Ns)
    nh = kh // tr

    def body(sel_ref, a_ref, b0_ref, b1_ref, b2_ref, o_ref):
        up = lambda r: r[...].astype(F32)
        o_ref[...] = ((up(a_ref) + up(b0_ref)) + up(b1_ref)) + up(b2_ref)

    recv = lambda k: pl.BlockSpec((None, tr, Ns), lambda i, sel: (k, i, 0))
    return pl.pallas_call(
        body, name=name,
        grid_spec=pltpu.PrefetchScalarGridSpec(
            num_scalar_prefetch=1, grid=(nh,),
            in_specs=[pl.BlockSpec((None, tr, Ns), lambda i, sel: (sel[0], i, 0)), recv(0), recv(1), recv(2)],
            out_specs=pl.BlockSpec((tr, Ns), lambda i, sel: (sel[1] * nh + i, 0))),
        out_shape=jax.ShapeDtypeStruct((2 * kh, Ns), F32), compiler_params=_params(("parallel",)),
    )(sel, pair_sum, from_chips, from_chips, from_chips)


def _gather_small(v, name):
    R, L = v.shape

    def body(v_ref, out_ref, send_sems, recv_sems):
        x, y, c = _place()
        me = 4 * x + 2 * y + c
        out_ref[me] = v_ref[...]
        peers = [(_flip(x, k >> 2 & 1), _flip(y, k >> 1 & 1), _flip(c, k & 1)) for k in range(1, N_DEV)]

        def copy(k, row, to):
            return pltpu.make_async_remote_copy(src_ref=v_ref, dst_ref=out_ref.at[row], send_sem=send_sems.at[k],
                                                recv_sem=recv_sems.at[k], device_id=to, device_id_type=MESH)

        sends = [copy(k, me, peer) for k, peer in enumerate(peers)]
        for cp in sends:
            cp.start()
        for k, (px, py, pc) in enumerate(peers):
            copy(k, 4 * px + 2 * py + pc, (x, y, c)).wait_recv()
        for cp in sends:
            cp.wait_send()

    vmem = pl.BlockSpec(memory_space=pltpu.VMEM)
    return pl.pallas_call(
        body, name=name, in_specs=[vmem], out_specs=vmem, out_shape=jax.ShapeDtypeStruct((N_DEV, R, L), F32),
        scratch_shapes=[pltpu.SemaphoreType.DMA((N_DEV - 1,)), pltpu.SemaphoreType.DMA((N_DEV - 1,))],
    )(v)


def _silu(v):
    return v * jax.nn.sigmoid(v)


def _ada_fwd(c_all, w_ada, tn=512):
    M, D = c_all.shape
    Ns = w_ada.shape[1]

    def body(c_ref, w_ref, o_ref):
        o_ref[...] = _dot(_silu(c_ref[...]).astype(BF16), w_ref[...].astype(BF16))

    return pl.pallas_call(
        body, name="ada_fwd", grid=(Ns // tn,),
        in_specs=[pl.BlockSpec((M, D), lambda j: (0, 0)), pl.BlockSpec((D, tn), lambda j: (0, j))],
        out_specs=pl.BlockSpec((M, tn), lambda j: (0, j)), out_shape=jax.ShapeDtypeStruct((M, Ns), F32),
        compiler_params=_params(("parallel",)),
    )(c_all, w_ada)


def _ada_bwd(c_all, dmod, w, m, v, tk=256, tn=1536):
    M, D = c_all.shape
    Ns = dmod.shape[1]

    def body(c_ref, d_ref, w_ref, m_ref, v_ref, g_ref, dl_ref, mo_ref, vo_ref):
        g = _dot(_silu(c_ref[...]).astype(BF16), d_ref[...].astype(BF16), TN)
        g_ref[...] = g
        dl_ref[...], mo_ref[...], vo_ref[...] = _adamw_math(g, w_ref[...], m_ref[...], v_ref[...])

    blk = pl.BlockSpec((tk, tn), lambda i, j: (i, j))
    return pl.pallas_call(
        body, name="ada_bwd", grid=(D // tk, Ns // tn),
        in_specs=[pl.BlockSpec((M, tk), lambda i, j: (0, i)), pl.BlockSpec((M, tn), lambda i, j: (0, j)), blk, blk, blk],
        out_specs=[blk] * 4, out_shape=[jax.ShapeDtypeStruct((D, Ns), F32)] * 4,
        compiler_params=_params(("parallel", "parallel"), 40),
    )(c_all, dmod, w, m, v)


def _small_update(g_all, w, m, v):
    R, L = w.shape

    def body(g_ref, w_ref, m_ref, v_ref, go_ref, d_ref, mo_ref, vo_ref):
        g = g_ref[0]
        for d in range(1, N_DEV):
            g = g + g_ref[d]
        go_ref[...] = g
        d_ref[...], mo_ref[...], vo_ref[...] = _adamw_math(g, w_ref[...], m_ref[...], v_ref[...])

    return pl.pallas_call(body, name="small_update", out_shape=[jax.ShapeDtypeStruct((R, L), F32)] * 4)(g_all, w, m, v)


def _pack(parts, rows):
    flat = jnp.concatenate([p.reshape(-1) for p in parts])
    return jnp.pad(flat, (0, rows * 128 - flat.shape[0])).reshape(rows, 128)


def _unpack(packed, shapes):
    flat, out, at = packed.reshape(-1), [], 0
    for shp in shapes:
        size = 1
        for d in shp:
            size *= d
        out.append(flat[at:at + size].reshape(shp))
        at += size
    return out


def _layer(x, tgt, mod, wts, rel_bias, attn_norm_g, lb_logits, gnorm_g, ln1_g, ln1_b, ln2_g, ln2_b, place=None):
    T, D = x.shape
    aw = attn_norm_g.shape[1]
    shift1, scale1, gate1, shift2, scale2, gate2 = [mod[i:i + 1] for i in range(6)]

    def gather(n, rows=None, into=None):
        return None if place is None else _gather_rider(wts[n], rows, None if into is None else into[0])

    def gathered(n, rode):
        return wts[n] if place is None else lax.dynamic_update_index_in_dim(rode[0], wts[n], place[0], 0)

    def blocks(g):
        return g.reshape(N_CHIPS, -1, g.shape[2])

    def to_sibling(g):
        return None if place is None else _pair_rider(g)

    def pair_sum(n, g, rode=None):
        if place is None:
            return g
        rode = _alone(_pair_rider(g), n + "_send_pair") if rode is None else rode
        return _sum_pair(g, rode[0], place[1], n + "_sum_pair")

    def to_chips(p, rows=None, into=None):
        return None if place is None else _chips_rider(p, rows, None if into is None else into[0])

    def summed(n, p, rode):
        return p if place is None else _sum_chips(p, rode[0], place[1], n + "_sum_chips")

    def to_both(block):
        return None if place is None else _share_rider(block)

    def carrying(mm, *args, rider, **kw):
        return mm(*args, rider=rider, **kw) if rider is not None else (mm(*args, **kw), None)

    w_in = gathered("w_in", None if place is None else _alone(gather("w_in"), "gather_w_in"))
    h1 = _pre_mixer(x, scale1, shift1)
    n_qkv = 3 * aw // 256
    kh_o, kh_f = wts["w_o"].shape[-2] // 2, wts["w_ffn_in"].shape[-2] // 2
    o_cut, f_cuts = 3 * kh_o // 8, (25 * kh_f // 64, 57 * kh_f // 64)
    qkv, rode = carrying(_mm_nn, h1, w_in, tm=ZPAD, tn=256, tk=D, name="proj_qkv", cols=(0, n_qkv), o_dtype=BF16,
                         pad_rows=ZPAD, rider=gather("w_o", (0, o_cut)))
    proj, rode = carrying(_mm_nn, h1, w_in, tm=2048, tn=256, tk=D, name="proj_rec",
                          cols=(n_qkv, N_CHIPS * w_in.shape[2] // 256), rider=gather("w_o", (o_cut, kh_o - o_cut), rode))
    w_o3 = gathered("w_o", rode).reshape(1, D, D)
    bias = _bias_band(rel_bias)
    (mix_a, probs), rode = _attn_fwd(qkv, bias, attn_norm_g, rider=gather("w_ffn_in", (0, f_cuts[0])))
    (mix_b, o_b, st_all), rode = _hgrn_fwd(
        proj, lb_logits, gnorm_g, rider=gather("w_ffn_in", (f_cuts[0], f_cuts[1] - f_cuts[0]), rode))
    mixin = jnp.concatenate([mix_a, mix_b], axis=1)
    mix = _mm_nn(mixin, w_o3, tm=1024, tn=512, tk=D, name="mix_out")
    if place is None:
        x1, h2 = _post_mixer(mix, x, gate1, ln1_g, ln1_b, scale2, shift2)
    else:
        (x1, h2), rode = _post_mixer(mix, x, gate1, ln1_g, ln1_b, scale2, shift2,
                                     rider=gather("w_ffn_in", (f_cuts[1], kh_f - f_cuts[1]), rode))
    w_ffn_in = gathered("w_ffn_in", rode)
    (gate, up, act), rode = _ffn_in_swiglu(h2, w_ffn_in, tm=2048, tn=256, rider=gather("w_ffn_out"))
    w_out3 = gathered("w_ffn_out", rode)
    w_out3 = w_out3.reshape(1, -1, w_out3.shape[2])
    d_ff = w_out3.shape[1]
    f = _mm_nn(act, w_out3, tm=1024, tn=1024, tk=d_ff // 4, name="ffn_out")
    du2, df, acc2 = _loss_head(f, x1, tgt, gate2, ln2_g, ln2_b)
    loss = (0.5 / D) * jnp.sum(acc2[3])
    g = blocks(_mm_tn(act, df, q=1, tk=512, tn=1024, tt=T, name="g_ffn_out"))
    d_gate_up, rode = _d_act_swiglu(df, w_out3, gate, up, tm=1024, to=512, rider=to_sibling(g))
    p_out = pair_sum("w_ffn_out", g, rode)
    dff = jnp.concatenate(d_gate_up, axis=1)
    cut = 21 * p_out.shape[1] // 44
    dh2, rode = carrying(_mm_nt, dff, w_ffn_in, tm=1024, to=1024, tn=w_ffn_in.shape[2], name="d_h2",
                         rider=to_chips(p_out, (0, cut)))
    g, rode = carrying(_mm_tn, h2, dff, q=N_CHIPS, tk=512, tn=w_ffn_in.shape[2] // 2, tt=T, name="g_ffn_in",
                       rider=to_chips(p_out, (cut, p_out.shape[1] - cut), rode))
    g_ffn_out = summed("w_ffn_out", p_out, rode)
    if place is None:
        (du1, dmix, acc1), p_fin = _mid_bwd(dh2, du2, x1, mix, x, gate1, ln1_g, scale2), g
    else:
        (du1, dmix, acc1), rode = _mid_bwd(dh2, du2, x1, mix, x, gate1, ln1_g, scale2,
                                           rider=_join(to_sibling(g), to_both(g_ffn_out)))
        p_fin, g_ffn_out = pair_sum("w_ffn_in", g, rode[:1]), rode[1]
    g = blocks(_mm_tn(mixin, dmix, q=1, tk=512, tn=1024, tt=T, name="g_o"))
    dmixin, rode = carrying(_mm_nt, dmix, w_o3, tm=1024, to=512, tn=D, name="d_mixin", rider=to_sibling(g))
    p_o = pair_sum("w_o", g, rode)
    cut = p_fin.shape[1] // 2
    (dq, dk, dv, dbias, dgain), rode = _attn_bwd(qkv, probs, attn_norm_g, dmixin, rider=to_chips(p_fin, (0, cut)))
    (dzq, dzf, dxi, dzg, dl0, dgn), rode = _hgrn_bwd(
        proj, lb_logits, gnorm_g, o_b, st_all, dmixin,
        rider=_join(to_chips(p_fin, (cut, p_fin.shape[1] - cut), rode), to_chips(p_o)))
    g_ffn_in, g_o = summed("w_ffn_in", p_fin, rode[:1]), summed("w_o", p_o, rode[1:])
    dproj = jnp.concatenate([dq, dk[KPAD:].astype(BF16), dv[KPAD:].astype(BF16), dzq, dzf, dxi, dzg], axis=1)
    tk = w_in.shape[1] // 4
    g_in = lambda part, rider: carrying(_mm_tn, h1, dproj, q=N_CHIPS, tk=tk, tn=w_in.shape[2] // 2, tt=T,
                                        name="g_in_%d" % part, every=(2, part), rider=rider)
    g, rode = g_in(0, _join(to_both(g_ffn_in), to_both(g_o)))
    if place is not None:
        g_ffn_in, g_o = rode
    p_a = pair_sum("w_in_0", g)
    g, rode = g_in(1, to_chips(p_a))
    r_a, p_b = summed("w_in_0", p_a, rode), pair_sum("w_in_1", g)
    dh1, rode = carrying(_mm_nt, dproj, w_in, tm=1024, to=1024, tn=w_in.shape[2], name="d_h1",
                         rider=_join(to_chips(p_b), to_both(r_a)))
    r_b = summed("w_in_1", p_b, None if place is None else rode[:1])
    if place is not None:
        r_a = rode[1]
        r_b, = _alone(to_both(r_b), "w_in_1_share")
    rows = lambda r, quarter: r[..., quarter * tk:(quarter + 1) * tk, :]
    g_in = jnp.concatenate([rows(r_a, 0), rows(r_b, 0), rows(r_a, 1), rows(r_b, 1)], axis=-2)
    grad_x, acc0 = _first_bwd(dh1, du1, x, scale1)
    dmod = jnp.concatenate([acc0[1:2], acc0[0:1], acc1[4:5], acc1[1:2], acc1[0:1], acc2[2:3]], axis=0)
    small = dict(rel_bias=_bias_band_grad(dbias), attn_norm_g=dgain,
                 lb_logits=jnp.concatenate([dl0, -dl0], axis=0), gnorm_g=dgn,
                 ln1_g=acc1[2:3], ln1_b=acc1[3:4], ln2_g=acc2[0:1], ln2_b=acc2[1:2])
    return loss, grad_x, dict(w_in=g_in, w_o=g_o, w_ffn_in=g_ffn_in, w_ffn_out=g_ffn_out), dmod, small


SMALL = ("rel_bias", "attn_norm_g", "lb_logits", "gnorm_g", "ln1_g", "ln1_b", "ln2_g", "ln2_b")
SMALL_ROWS = 256


def kernel(x, c, w_ada, b_ada, w_in, rel_bias, attn_norm_g, lb_logits, gnorm_g, w_o, ln1_g, ln1_b, w_ffn_in, w_ffn_out, ln2_g, ln2_b, loss_target, m_w_ada, m_b_ada, m_w_in, m_rel_bias, m_attn_norm_g, m_lb_logits, m_gnorm_g, m_w_o, m_ln1_g, m_ln1_b, m_w_ffn_in, m_w_ffn_out, m_ln2_g, m_ln2_b, v_w_ada, v_b_ada, v_w_in, v_rel_bias, v_attn_norm_g, v_lb_logits, v_gnorm_g, v_w_o, v_ln1_g, v_ln1_b, v_w_ffn_in, v_w_ffn_out, v_ln2_g, v_ln2_b):
    mx, my, mc = _place()
    me = 4 * mx + 2 * my + mc
    chip = 2 * mx + my
    sel = jnp.stack([chip, mc]).astype(jnp.int32)
    D = x.shape[2]
    ns_ada = w_ada.shape[2]

    big = dict(w_in=(w_in, m_w_in, v_w_in), w_o=(w_o, m_w_o, v_w_o), w_ffn_in=(w_ffn_in, m_w_ffn_in, v_w_ffn_in),
               w_ffn_out=(w_ffn_out, m_w_ffn_out, v_w_ffn_out))
    shards = {n: t[0][0].astype(BF16) for n, t in big.items()}

    c_all = _gather_small(c.reshape(D // 128, 128), "gather_c").reshape(N_DEV, D)
    c_all = jnp.pad(c_all, ((0, 16 - N_DEV), (0, 0)))
    mod_cols = _ada_fwd(c_all, w_ada[0])[:N_DEV]
    mod_all = _gather_small(mod_cols.reshape(-1, 128), "gather_mod").reshape(N_DEV, N_DEV, ns_ada)
    mod = lax.dynamic_index_in_dim(mod_all[::2], me, axis=1, keepdims=False)
    mod = (mod.reshape(1, -1) + b_ada).reshape(6, D)

    loss, grad_x, g_big, dmod, g_small = _layer(
        x[0], loss_target[0], mod, shards, rel_bias[0], attn_norm_g, lb_logits, gnorm_g, ln1_g, ln1_b, ln2_g, ln2_b,
        place=(chip, sel))

    grads, deltas, new_m, new_v = {}, {}, {}, {}
    for n, (w, m, v) in big.items():
        g, d, mo, vo = _adamw(g_big[n], w[0], m[0], v[0], "adamw_" + n)
        grads[n], deltas[n], new_m[n], new_v[n] = g[None], d[None], mo[None], vo[None]

    small_in = dict(rel_bias=(rel_bias, m_rel_bias, v_rel_bias), attn_norm_g=(attn_norm_g, m_attn_norm_g, v_attn_norm_g),
                    lb_logits=(lb_logits, m_lb_logits, v_lb_logits), gnorm_g=(gnorm_g, m_gnorm_g, v_gnorm_g),
                    ln1_g=(ln1_g, m_ln1_g, v_ln1_g), ln1_b=(ln1_b, m_ln1_b, v_ln1_b), ln2_g=(ln2_g, m_ln2_g, v_ln2_g),
                    ln2_b=(ln2_b, m_ln2_b, v_ln2_b))
    g_all = _gather_small(_pack([dmod] + [g_small[n] for n in SMALL] + [loss], SMALL_ROWS), "gather_small")
    packed = [_pack([t] + [small_in[n][i] for n in SMALL] + [jnp.zeros((), F32)], SMALL_ROWS)
              for i, t in enumerate((b_ada, m_b_ada, v_b_ada))]
    shapes = [b_ada.shape] + [small_in[n][0].shape for n in SMALL] + [()]
    outs = [_unpack(o, shapes) for o in _small_update(g_all, *packed)]
    loss = outs[0][-1]
    for i, n in enumerate(("b_ada",) + SMALL):
        grads[n], deltas[n], new_m[n], new_v[n] = outs[0][i], outs[1][i], outs[2][i], outs[3][i]

    dmod_all = g_all[:, :6 * D // 128].reshape(N_DEV, 6 * D)
    dmod_cols = lax.dynamic_slice_in_dim(dmod_all, chip * ns_ada, ns_ada, axis=1)
    dmod_cols = jnp.pad(dmod_cols, ((0, 16 - N_DEV), (0, 0)))
    g, d, mo, vo = _ada_bwd(c_all, dmod_cols, w_ada[0], m_w_ada[0], v_w_ada[0])
    grads["w_ada"], deltas["w_ada"], new_m["w_ada"], new_v["w_ada"] = g[None], d[None], mo[None], vo[None]

    order = ("w_ada", "b_ada", "w_in", "rel_bias", "attn_norm_g", "lb_logits", "gnorm_g", "w_o", "ln1_g", "ln1_b",
             "w_ffn_in", "w_ffn_out", "ln2_g", "ln2_b")
    return (loss, grad_x[None], *[grads[n] for n in order], *[deltas[n] for n in order],
            *[new_m[n] for n in order], *[new_v[n] for n in order])
```

```python
import numpy as np
import jax
import jax.numpy as jnp
from jax import lax
from jax.experimental import pallas as pl
from jax.experimental.pallas import tpu as pltpu

F32 = jnp.float32
BF16 = jnp.bfloat16
MESH = pl.DeviceIdType.MESH
HIGHEST = lax.Precision.HIGHEST

CHUNK = 64
N_PAST = 8
QG = 4
QROWS = QG * CHUNK
KPAD = N_PAST * CHUNK
ZPAD = 2 * KPAD
UNION = (QG + N_PAST) * CHUNK
BAND = (N_PAST + 1) * CHUNK
HD_A = 64
HD_B = 128
SUB = 16
HGRN_HEADS = 4
MAX_REL = 256
EPS = 1e-5
ALPHA = 2.0 ** 0.25
LR, B1, B2, ADAM_EPS, WD, STEP = 1e-3, 0.9, 0.999, 1e-8, 0.01, 10
N_CHIPS = 4
N_DEV = 8
NEG = -1e30
TILE_BYTES = 3 << 19

NN = ((1,), (0,))
NT = ((1,), (1,))
TN = ((0,), (0,))


def _dot(a, b, dims=NN, precision=None):
    return lax.dot_general(a, b, (dims, ((), ())), preferred_element_type=F32, precision=precision)


def _params(sem=None, vmem_mb=None, **kw):
    return pltpu.CompilerParams(dimension_semantics=sem,
                                vmem_limit_bytes=None if vmem_mb is None else vmem_mb << 20, **kw)


def _row_tile(rows, cols):
    for cand in (512, 256, 128, 64, 32, 16, 8):
        if rows % cand == 0 and cand * cols * 4 <= TILE_BYTES:
            return cand
    raise ValueError((rows, cols))


def _place():
    return lax.axis_index("x"), lax.axis_index("y"), lax.axis_index("c")


def _flip(v, bit):
    return 1 - v if bit else v


ANY = pl.BlockSpec(memory_space=pl.ANY)
CHIP_FLIPS = ((1, 0), (0, 1), (1, 1))


class _Rider:
    def __init__(self, operands, out_shape, n_sems, start, finish, aliases=None):
        self.operands, self.out_shape, self.n_sems, self.start, self.finish = operands, out_shape, n_sems, start, finish
        self.aliases = aliases or {}


def _call(body, rider, *, name, grid, in_specs, out_specs, out_shape, scratch_shapes, compiler_params, operands):
    if rider is None:
        outs = pl.pallas_call(body, name=name, grid=grid, in_specs=in_specs, out_specs=out_specs, out_shape=out_shape,
                              scratch_shapes=scratch_shapes, compiler_params=compiler_params)(*operands)
        return list(outs), []
    n_in, n_out, n_sc = len(in_specs), len(out_specs), len(scratch_shapes)
    r_in, r_out = len(rider.operands), len(rider.out_shape)

    def carried(*refs):
        refs = list(refs)
        cuts = [n_in, r_in, n_out, r_out, n_sc]
        ins, r_ins, outs, r_outs, scratch = [[refs.pop(0) for _ in range(n)] for n in cuts]
        first, last = None, None
        for axis, size in enumerate(grid):
            i = pl.program_id(axis)
            first = (i == 0) if first is None else first & (i == 0)
            last = (i == size - 1) if last is None else last & (i == size - 1)

        @pl.when(first)
        def _():
            rider.start(r_ins, r_outs, *refs)

        body(*ins, *outs, *scratch)

        @pl.when(last)
        def _():
            rider.finish(r_ins, r_outs, *refs)

    sems = [pltpu.SemaphoreType.DMA((rider.n_sems,)), pltpu.SemaphoreType.DMA((rider.n_sems,))]
    outs = pl.pallas_call(carried, name=name, grid=grid, in_specs=list(in_specs) + [ANY] * r_in,
                          out_specs=list(out_specs) + [ANY] * r_out, out_shape=list(out_shape) + rider.out_shape,
                          scratch_shapes=list(scratch_shapes) + sems, compiler_params=compiler_params,
                          input_output_aliases={n_in + i: n_out + o for i, o in rider.aliases.items()},
                          )(*operands, *rider.operands)
    return list(outs[:n_out]), list(outs[n_out:])


def _alone(rider, name):
    def body(*refs):
        ins, outs, sems = refs[:len(rider.operands)], refs[len(rider.operands):-2], refs[-2:]
        rider.start(ins, outs, *sems)
        rider.finish(ins, outs, *sems)

    return pl.pallas_call(
        body, name=name, in_specs=[ANY] * len(rider.operands), out_specs=[ANY] * len(rider.out_shape),
        out_shape=rider.out_shape, input_output_aliases=rider.aliases,
        scratch_shapes=[pltpu.SemaphoreType.DMA((rider.n_sems,)), pltpu.SemaphoreType.DMA((rider.n_sems,))],
    )(*rider.operands)


class _Sems:
    def __init__(self, sems, base):
        self.sems, self.base = sems, base

    @property
    def at(self):
        return self

    def __getitem__(self, k):
        return self.sems.at[self.base + k]


def _join(*riders):
    riders = [r for r in riders if r is not None]
    if len(riders) < 2:
        return riders[0] if riders else None

    def parts(ins, outs, send_sems, recv_sems):
        i = o = s = 0
        for r in riders:
            ni, no = len(r.operands), len(r.out_shape)
            yield r, ins[i:i + ni], outs[o:o + no], _Sems(send_sems, s), _Sems(recv_sems, s)
            i, o, s = i + ni, o + no, s + r.n_sems

    def start(*refs):
        for r, *args in parts(*refs):
            r.start(*args)

    def finish(*refs):
        for r, *args in parts(*refs):
            r.finish(*args)

    aliases, i, o = {}, 0, 0
    for r in riders:
        aliases.update({i + a: o + b for a, b in r.aliases.items()})
        i, o = i + len(r.operands), o + len(r.out_shape)
    return _Rider([a for r in riders for a in r.operands], [s for r in riders for s in r.out_shape],
                  sum(r.n_sems for r in riders), start, finish, aliases)


def _gather_rider(shard, rows=None, into=None):
    K, Ns = shard.shape
    kh = K // 2
    first_row, n_rows = rows or (0, kh)

    def copies(w_ref, out_ref, send_sems, recv_sems):
        x, y, c = _place()
        chips = [(_flip(x, fx), _flip(y, fy)) for fx, fy in CHIP_FLIPS]

        def half(chip, which):
            return out_ref.at[2 * chip[0] + chip[1], pl.ds(which * kh + first_row, n_rows), :]

        def copy(k, dst, to, src=None):
            return pltpu.make_async_remote_copy(src_ref=dst if src is None else src, dst_ref=dst,
                                                send_sem=send_sems.at[k], recv_sem=recv_sems.at[k],
                                                device_id=to, device_id_type=MESH)

        def first():
            return [copy(j, half((x, y), c), (*chip, c), src=w_ref.at[pl.ds(c * kh + first_row, n_rows), :])
                    for j, chip in enumerate(chips)]

        def onward():
            return [copy(3 + j, half(chip, c), (x, y, 1 - c)) for j, chip in enumerate(chips)]

        def arriving(base, which):
            return [copy(base + j, half(chip, which), (x, y, c)) for j, chip in enumerate(chips)]

        return first, onward, arriving

    def start(ins, outs, send_sems, recv_sems):
        for cp in copies(ins[0], outs[0], send_sems, recv_sems)[0]():
            cp.start()

    def finish(ins, outs, send_sems, recv_sems):
        x, y, c = _place()
        first, onward, arriving = copies(ins[0], outs[0], send_sems, recv_sems)
        passed = onward()
        for arrived, cp in zip(arriving(0, c), passed):
            arrived.wait_recv()
            cp.start()
        for arrived in arriving(3, 1 - c):
            arrived.wait_recv()
        for cp in first() + passed:
            cp.wait_send()

    full = jax.ShapeDtypeStruct((N_CHIPS, K, Ns), shard.dtype)
    if into is None:
        return _Rider([shard], [full], 6, start, finish)
    return _Rider([shard, into], [full], 6, start, finish, aliases={1: 0})


def _pair_rider(g_full):
    Q, K, Ns = g_full.shape
    kh = K // 2

    def copy(g_ref, got_ref, send_sems, recv_sems):
        x, y, c = _place()
        return pltpu.make_async_remote_copy(src_ref=g_ref.at[:, pl.ds((1 - c) * kh, kh), :], dst_ref=got_ref,
                                            send_sem=send_sems.at[0], recv_sem=recv_sems.at[0],
                                            device_id=(x, y, 1 - c), device_id_type=MESH)

    def start(ins, outs, send_sems, recv_sems):
        copy(ins[0], outs[0], send_sems, recv_sems).start()

    def finish(ins, outs, send_sems, recv_sems):
        copy(ins[0], outs[0], send_sems, recv_sems).wait()

    return _Rider([g_full], [jax.ShapeDtypeStruct((Q, kh, Ns), g_full.dtype)], 1, start, finish)


def _share_rider(block):
    K, Ns = block.shape
    kh = K // 2

    def halves(out_ref):
        x, y, c = _place()
        return out_ref.at[pl.ds(c * kh, kh), :], out_ref.at[pl.ds((1 - c) * kh, kh), :], (x, y, 1 - c)

    def start(ins, outs, send_sems, recv_sems):
        mine, _, sibling = halves(outs[0])
        pltpu.make_async_remote_copy(src_ref=mine, dst_ref=mine, send_sem=send_sems.at[0], recv_sem=recv_sems.at[0],
                                     device_id=sibling, device_id_type=MESH).start()

    def finish(ins, outs, send_sems, recv_sems):
        mine, theirs, sibling = halves(outs[0])
        pltpu.make_async_remote_copy(src_ref=theirs, dst_ref=theirs, send_sem=send_sems.at[0], recv_sem=recv_sems.at[0],
                                     device_id=sibling, device_id_type=MESH).wait_recv()
        pltpu.make_async_remote_copy(src_ref=mine, dst_ref=mine, send_sem=send_sems.at[0], recv_sem=recv_sems.at[0],
                                     device_id=sibling, device_id_type=MESH).wait_send()

    return _Rider([block], [jax.ShapeDtypeStruct((K, Ns), block.dtype)], 1, start, finish, aliases={0: 0})


def _chips_rider(pair_sum, rows=None, into=None):
    Q, kh, Ns = pair_sum.shape
    first_row, n_rows = rows or (0, kh)

    def copies(p_ref, got_ref, send_sems, recv_sems):
        x, y, c = _place()
        part = pl.ds(first_row, n_rows)
        out = []
        for j, (fx, fy) in enumerate(CHIP_FLIPS):
            px, py = _flip(x, fx), _flip(y, fy)
            out.append(pltpu.make_async_remote_copy(
                src_ref=p_ref.at[2 * px + py, part, :], dst_ref=got_ref.at[j, part, :], send_sem=send_sems.at[j],
                recv_sem=recv_sems.at[j], device_id=(px, py, c), device_id_type=MESH))
        return out

    def start(ins, outs, send_sems, recv_sems):
        for cp in copies(ins[0], outs[0], send_sems, recv_sems):
            cp.start()

    def finish(ins, outs, send_sems, recv_sems):
        sends = copies(ins[0], outs[0], send_sems, recv_sems)
        for cp in sends:
            cp.wait_recv()
        for cp in sends:
            cp.wait_send()

    got = jax.ShapeDtypeStruct((Q - 1, kh, Ns), pair_sum.dtype)
    if into is None:
        return _Rider([pair_sum], [got], 3, start, finish)
    return _Rider([pair_sum, into], [got], 3, start, finish, aliases={1: 0})


def _mm(a, b, *, grid, a_spec, b_spec, o_spec, o_shape, o_dtype, dims, acc_shape, name, rider=None, zero_rows=0,
        vmem_mb=48):
    nk = grid[2]

    def body(a_ref, b_ref, o_ref, *scratch):
        if zero_rows:
            @pl.when(pl.program_id(0) < zero_rows)
            def _():
                o_ref[...] = jnp.zeros_like(o_ref)

            @pl.when(pl.program_id(0) >= zero_rows)
            def _():
                o_ref[...] = _dot(a_ref[...], b_ref[...], dims).astype(o_ref.dtype)
            return
        part = _dot(a_ref[...], b_ref[...], dims)
        if nk == 1:
            o_ref[...] = part.astype(o_ref.dtype)
            return
        acc_ref, = scratch
        k = pl.program_id(2)

        @pl.when(k == 0)
        def _():
            acc_ref[...] = part

        @pl.when(k > 0)
        def _():
            acc_ref[...] += part

        @pl.when(k == nk - 1)
        def _():
            o_ref[...] = acc_ref[...].astype(o_ref.dtype)

    (out,), rode = _call(
        body, rider, name=name, grid=grid, in_specs=[a_spec, b_spec], out_specs=[o_spec],
        out_shape=[jax.ShapeDtypeStruct(o_shape, o_dtype)],
        scratch_shapes=[] if nk == 1 else [pltpu.VMEM(acc_shape, F32)],
        compiler_params=_params(("parallel", "parallel", "arbitrary") if rider is None else ("arbitrary",) * 3, vmem_mb),
        operands=(a, b))
    return out if rider is None else (out, rode)


def _mm_nn(a, w, *, tm, tn, tk, name, rider=None, cols=None, o_dtype=F32, pad_rows=0):
    T, K = a.shape
    Q, _, Ns = w.shape
    nbs = Ns // tn
    tm = min(tm, T)
    j0, j1 = cols or (0, Q * nbs)
    lead = pad_rows // tm
    return _mm(a, w, grid=(lead + T // tm, j1 - j0, K // tk),
               a_spec=pl.BlockSpec((tm, tk), lambda i, j, k: (jnp.maximum(i - lead, 0), k)),
               b_spec=pl.BlockSpec((None, tk, tn), lambda i, j, k: ((j + j0) // nbs, k, (j + j0) % nbs)),
               o_spec=pl.BlockSpec((tm, tn), lambda i, j, k: (i, j)),
               o_shape=(pad_rows + T, (j1 - j0) * tn), o_dtype=o_dtype, dims=NN, acc_shape=(tm, tn), name=name,
               rider=rider, zero_rows=lead)


def _mm_nt(g, w, *, tm, to, tn, name, rider=None):
    T = g.shape[0]
    Q, K, Ns = w.shape
    nbs = Ns // tn
    tm = min(tm, T)
    return _mm(g, w, grid=(T // tm, K // to, Q * nbs),
               a_spec=pl.BlockSpec((tm, tn), lambda i, j, n: (i, n)),
               b_spec=pl.BlockSpec((None, to, tn), lambda i, j, n: (n // nbs, j, n % nbs)),
               o_spec=pl.BlockSpec((tm, to), lambda i, j, n: (i, j)),
               o_shape=(T, K), o_dtype=F32, dims=NT, acc_shape=(tm, to), name=name, rider=rider)


def _mm_tn(a, g, *, q, tk, tn, tt, name, rider=None):
    T, K = a.shape
    Ns = g.shape[1] // q
    nbs = Ns // tn
    return _mm(a, g, grid=(K // tk, q * nbs, T // tt),
               a_spec=pl.BlockSpec((tt, tk), lambda i, j, t: (t, i)),
               b_spec=pl.BlockSpec((tt, tn), lambda i, j, t: (t, j)),
               o_spec=pl.BlockSpec((None, tk, tn), lambda i, j, t: (j // nbs, i, j % nbs)),
               o_shape=(q, K, Ns), o_dtype=BF16, dims=TN, acc_shape=(tk, tn), name=name, rider=rider)


def _ln(u):
    mu = jnp.mean(u, axis=-1, keepdims=True)
    d = u - mu
    r = lax.rsqrt(jnp.mean(d * d, axis=-1, keepdims=True) + EPS)
    return d * r, r


def _ln_bwd(dy, un, r):
    return r * (dy - jnp.mean(dy, axis=-1, keepdims=True) - un * jnp.mean(dy * un, axis=-1, keepdims=True))


def _colsum(v):
    return jnp.sum(v, axis=0, keepdims=True)


def _rowwise(name, fn, bigs, vecs, out_dtypes, n_acc, tm=128, rider=None):
    T, D = bigs[0].shape
    nb, nv, no = len(bigs), len(vecs), len(out_dtypes)

    def body(*refs):
        outs, accs = fn([r[...] for r in refs[:nb]], [r[...] for r in refs[nb:nb + nv]])
        for r, o in zip(refs[nb + nv:nb + nv + no], outs):
            r[...] = o.astype(r.dtype)
        if n_acc:
            acc_ref = refs[nb + nv + no]

            @pl.when(pl.program_id(0) == 0)
            def _():
                acc_ref[...] = jnp.zeros_like(acc_ref)

            for row, a in enumerate(accs):
                acc_ref[row:row + 1, :] += a

    big_spec = pl.BlockSpec((tm, D), lambda i: (i, 0))
    vec_spec = pl.BlockSpec((1, D), lambda i: (0, 0))
    out_shape = [jax.ShapeDtypeStruct((T, D), dt) for dt in out_dtypes]
    out_specs = [big_spec] * no
    if n_acc:
        out_shape.append(jax.ShapeDtypeStruct((8, D), F32))
        out_specs.append(pl.BlockSpec((8, D), lambda i: (0, 0)))
    outs, rode = _call(
        body, rider, name=name, grid=(T // tm,), in_specs=[big_spec] * nb + [vec_spec] * nv,
        out_specs=out_specs, out_shape=out_shape, scratch_shapes=[],
        compiler_params=_params(("arbitrary",), 48), operands=(*bigs, *vecs))
    return outs if rider is None else (outs, rode)


def _to_bf16(w, name, rider=None):
    R, C = w.shape
    tr = _row_tile(R, C)

    def body(w_ref, o_ref):
        o_ref[...] = w_ref[...].astype(o_ref.dtype)

    blk = pl.BlockSpec((tr, C), lambda i: (i, 0))
    return _call(body, rider, name=name, grid=(R // tr,), in_specs=[blk], out_specs=[blk],
                 out_shape=[jax.ShapeDtypeStruct((R, C), BF16)], scratch_shapes=[],
                 compiler_params=_params(("arbitrary",)), operands=(w,))


def _pre_mixer(x, scale1, shift1):
    def fn(b, v):
        xn, _ = _ln(b[0])
        return [xn * (1.0 + v[0]) + v[1]], []
    return _rowwise("pre_mixer", fn, [x], [scale1, shift1], [BF16], 0)[0]


def _post_mixer(mix, x, gate1, g1, b1, scale2, shift2, rider=None):
    def fn(b, v):
        un1, _ = _ln(ALPHA * b[1] + v[0] * b[0])
        x1 = un1 * v[1] + v[2]
        xn1, _ = _ln(x1)
        return [x1, xn1 * (1.0 + v[3]) + v[4]], []
    return _rowwise("post_mixer", fn, [mix, x], [gate1, g1, b1, scale2, shift2], [F32, BF16], 0, rider=rider)


def _loss_head(f, x1, tgt, gate2, g2, b2):
    def fn(b, v):
        ff, xx, tt = b
        d_model = ff.shape[-1]
        un2, r2 = _ln(ALPHA * xx + v[0] * ff)
        err = un2 * v[1] + v[2] - tt
        dy = err * (1.0 / d_model)
        du2 = _ln_bwd(dy * v[1], un2, r2)
        return [du2, du2 * v[0]], [_colsum(dy * un2), _colsum(dy), _colsum(du2 * ff), _colsum(err * err)]
    return _rowwise("loss_head", fn, [f, x1, tgt], [gate2, g2, b2], [F32, BF16], 4)


def _mid_bwd(dh2, du2, x1, mix, x, gate1, g1, scale2, rider=None):
    def fn(b, v):
        dh, du, xx1, mm, xx = b
        xn1, r1n = _ln(xx1)
        dx1 = ALPHA * du + _ln_bwd(dh * (1.0 + v[2]), xn1, r1n)
        un1, r1 = _ln(ALPHA * xx + v[0] * mm)
        du1 = _ln_bwd(dx1 * v[1], un1, r1)
        return [du1, du1 * v[0]], [_colsum(dh * xn1), _colsum(dh), _colsum(dx1 * un1), _colsum(dx1),
                                   _colsum(du1 * mm)]
    return _rowwise("mid_bwd", fn, [dh2, du2, x1, mix, x], [gate1, g1, scale2], [F32, BF16], 5, rider=rider)


def _first_bwd(dh1, du1, x, scale1, rider=None):
    def fn(b, v):
        dh, du, xx = b
        xn, r0 = _ln(xx)
        return [ALPHA * du + _ln_bwd(dh * (1.0 + v[0]), xn, r0)], [_colsum(dh * xn), _colsum(dh)]
    return _rowwise("first_bwd", fn, [dh1, du1, x], [scale1], [F32], 2, rider=rider)


def _ffn_in_swiglu(h2, w, *, tm, tn, rider=None):
    T, K = h2.shape
    Q, _, Ns = w.shape
    nbs = Ns // tn
    half = Q * nbs // 2
    tm = min(tm, T)

    def body(a_ref, wg_ref, wu_ref, g_ref, u_ref, act_ref):
        a = a_ref[...]
        g, u = _dot(a, wg_ref[...]), _dot(a, wu_ref[...])
        g_ref[...] = g.astype(g_ref.dtype)
        u_ref[...] = u.astype(u_ref.dtype)
        act_ref[...] = (g * jax.nn.sigmoid(g) * u).astype(act_ref.dtype)

    cols = lambda first: pl.BlockSpec((None, K, tn), lambda i, j: ((j + first) // nbs, 0, (j + first) % nbs))
    blk = pl.BlockSpec((tm, tn), lambda i, j: (i, j))
    return _call(
        body, rider, name="ffn_in", grid=(T // tm, half),
        in_specs=[pl.BlockSpec((tm, K), lambda i, j: (i, 0)), cols(0), cols(half)], out_specs=[blk] * 3,
        out_shape=[jax.ShapeDtypeStruct((T, half * tn), BF16)] * 3, scratch_shapes=[],
        compiler_params=_params(("arbitrary", "arbitrary"), 48), operands=(h2, w, w))


def _d_act_swiglu(df, w, gate, up, *, tm, to, rider=None):
    T, N = df.shape
    F = w.shape[1]
    tm = min(tm, T)

    def body(df_ref, w_ref, g_ref, u_ref, dg_ref, du_ref):
        d = _dot(df_ref[...], w_ref[...], NT)
        g = g_ref[...].astype(F32)
        s = jax.nn.sigmoid(g)
        du_ref[...] = (d * g * s).astype(du_ref.dtype)
        dg_ref[...] = (d * u_ref[...].astype(F32) * s * (1.0 + g * (1.0 - s))).astype(dg_ref.dtype)

    blk = pl.BlockSpec((tm, to), lambda i, j: (i, j))
    return _call(
        body, rider, name="d_act", grid=(T // tm, F // to),
        in_specs=[pl.BlockSpec((tm, N), lambda i, j: (i, 0)), pl.BlockSpec((None, to, N), lambda i, j: (0, j, 0)), blk, blk],
        out_specs=[blk, blk], out_shape=[jax.ShapeDtypeStruct((T, F), BF16)] * 2, scratch_shapes=[],
        compiler_params=_params(("arbitrary", "arbitrary"), 48), operands=(df, w, gate, up))


PAIR = 2


def _fill_table(table_ref, band_ref):
    table_ref[...] = jnp.full(table_ref.shape, NEG, F32)
    for e in range(PAIR):
        for g in range(QG):
            table_ref[e, g * CHUNK:(g + 1) * CHUNK, g * CHUNK:g * CHUNK + BAND] = band_ref[e]


def _attn_probs(q_ref, k_ref, bias_ref, e, step):
    start = pl.multiple_of(step * QROWS, QROWS)
    lanes = pl.ds(e * HD_A, HD_A)
    s = _dot(q_ref[:, lanes], k_ref[pl.ds(start + ZPAD - KPAD, UNION), lanes], NT) * (HD_A ** -0.5) + bias_ref[e]
    col = lax.broadcasted_iota(jnp.int32, s.shape, 1)
    s = jnp.where(col + start >= KPAD, s, NEG)
    p = jnp.exp(s - jnp.max(s, axis=-1, keepdims=True))
    return p / jnp.sum(p, axis=-1, keepdims=True), start


def _attn_specs(T, n_pairs):
    wide = PAIR * HD_A
    per_step = pl.BlockSpec((QROWS, wide), lambda hp, n: (n, hp))
    queries = pl.BlockSpec((QROWS, wide), lambda hp, n: (n + ZPAD // QROWS, hp))
    keys = pl.BlockSpec((ZPAD + T, wide), lambda hp, n: (0, n_pairs + hp))
    values = pl.BlockSpec((ZPAD + T, wide), lambda hp, n: (0, 2 * n_pairs + hp))
    grads = pl.BlockSpec((KPAD + T, wide), lambda hp, n: (0, hp))
    table = pl.BlockSpec((PAIR, CHUNK, BAND), lambda hp, n: (hp, 0, 0))
    vec = pl.BlockSpec((1, wide), lambda hp, n: (0, hp))
    return per_step, queries, keys, values, grads, table, vec


def _probs_spec():
    return pl.BlockSpec((PAIR, QROWS, UNION), lambda hp, n: (hp, n, 0))


def _attn_fwd(qkv, bias, gain, rider=None):
    T = qkv.shape[0] - ZPAD
    W = gain.shape[1]
    n_pairs = W // (PAIR * HD_A)

    def body(q_ref, k_ref, v_ref, band_ref, gain_ref, o_ref, p_ref, table_ref):
        @pl.when(pl.program_id(1) == 0)
        def _():
            _fill_table(table_ref, band_ref)

        for e in range(PAIR):
            lanes = pl.ds(e * HD_A, HD_A)
            p, start = _attn_probs(q_ref, k_ref, table_ref, e, pl.program_id(1))
            p_ref[e] = p.astype(p_ref.dtype)
            o = _dot(p_ref[e], v_ref[pl.ds(start + ZPAD - KPAD, UNION), lanes])
            rr = lax.rsqrt(jnp.mean(o * o, axis=-1, keepdims=True) + EPS)
            o_ref[:, lanes] = (o * rr * gain_ref[:, lanes]).astype(o_ref.dtype)

    per_step, queries, keys, values, _, table, vec = _attn_specs(T, n_pairs)
    return _call(
        body, rider, name="attn_fwd", grid=(n_pairs, T // QROWS), in_specs=[queries, keys, values, table, vec],
        out_specs=[per_step, _probs_spec()],
        out_shape=[jax.ShapeDtypeStruct((T, W), BF16), jax.ShapeDtypeStruct((n_pairs * PAIR, T, UNION), BF16)],
        scratch_shapes=[pltpu.VMEM((PAIR, QROWS, UNION), F32)],
        compiler_params=_params(("arbitrary", "arbitrary"), 40), operands=(qkv, qkv, qkv, bias, gain))


def _attn_bwd(qkv, probs, gain, dmixin, rider=None):
    T = qkv.shape[0] - ZPAD
    W = gain.shape[1]
    n_pairs = W // (PAIR * HD_A)
    scale = HD_A ** -0.5

    def body(q_ref, k_ref, v_ref, p_ref, gain_ref, don_ref, dq_ref, dk_ref, dv_ref, dband_ref, dgain_ref, dtable_ref):
        n = pl.program_id(1)

        @pl.when(n == 0)
        def _():
            dk_ref[...] = jnp.zeros_like(dk_ref)
            dv_ref[...] = jnp.zeros_like(dv_ref)
            dtable_ref[...] = jnp.zeros_like(dtable_ref)
            dgain_ref[...] = jnp.zeros_like(dgain_ref)

        for e in range(PAIR):
            lanes = pl.ds(e * HD_A, HD_A)
            start = pl.multiple_of(n * QROWS, QROWS)
            keys, in_qkv = pl.ds(start, UNION), pl.ds(start + ZPAD - KPAD, UNION)
            pb = p_ref[e]
            p = pb.astype(F32)
            vb = v_ref[in_qkv, lanes]
            o = _dot(pb, vb)
            rr = lax.rsqrt(jnp.mean(o * o, axis=-1, keepdims=True) + EPS)
            on = o * rr
            d_on = don_ref[:, lanes]
            dgain_ref[:, lanes] += _colsum(d_on * on)
            dyo = d_on * gain_ref[:, lanes]
            do = rr * (dyo - on * jnp.mean(dyo * on, axis=-1, keepdims=True))
            dob = do.astype(BF16)
            dp = _dot(dob, vb, NT)
            ds = p * (dp - jnp.sum(do * o, axis=-1, keepdims=True))
            dtable_ref[e] += ds
            dsb = ds.astype(BF16)
            dq_ref[:, lanes] = (_dot(dsb, k_ref[in_qkv, lanes]) * scale).astype(dq_ref.dtype)
            dk_ref[keys, lanes] += _dot(dsb, q_ref[:, lanes], TN) * scale
            dv_ref[keys, lanes] += _dot(pb, dob, TN)

        @pl.when(n == T // QROWS - 1)
        def _():
            for e in range(PAIR):
                dband_ref[e] = sum(dtable_ref[e, g * CHUNK:(g + 1) * CHUNK, g * CHUNK:g * CHUNK + BAND]
                                   for g in range(QG))

    per_step, queries, keys, values, grads, table, vec = _attn_specs(T, n_pairs)
    H = n_pairs * PAIR
    return _call(
        body, rider, name="attn_bwd", grid=(n_pairs, T // QROWS),
        in_specs=[queries, keys, values, _probs_spec(), vec, per_step],
        out_specs=[per_step, grads, grads, table, vec],
        out_shape=[jax.ShapeDtypeStruct((T, W), BF16), jax.ShapeDtypeStruct((KPAD + T, W), F32),
                   jax.ShapeDtypeStruct((KPAD + T, W), F32), jax.ShapeDtypeStruct((H, CHUNK, BAND), F32),
                   jax.ShapeDtypeStruct((1, W), F32)],
        scratch_shapes=[pltpu.VMEM((PAIR, QROWS, UNION), F32)],
        compiler_params=_params(("arbitrary", "arbitrary"), 40),
        operands=(qkv, qkv, qkv, probs, gain, dmixin))


N_DIAG = CHUNK + BAND - 1


def _bias_band(rel_bias):
    H = rel_bias.shape[0]
    idx = np.clip(BAND - 1 - np.arange(N_DIAG), -MAX_REL, MAX_REL) + MAX_REL
    rolled = rel_bias[:, idx[(np.arange(N_DIAG) + CHUNK - 1) % N_DIAG]]
    flat = jnp.broadcast_to(rolled[:, None, :], (H, CHUNK, N_DIAG)).reshape(H, CHUNK * N_DIAG)
    return flat[:, :CHUNK * (N_DIAG - 1)].reshape(H, CHUNK, N_DIAG - 1)[:, :, :BAND]


def _bias_band_grad(dband):
    H = dband.shape[0]
    skew = jnp.pad(dband, ((0, 0), (0, 0), (CHUNK - 1, 0))).reshape(H, CHUNK * N_DIAG)
    skew = jnp.pad(skew, ((0, 0), (0, CHUNK))).reshape(H, CHUNK, N_DIAG + 1)
    diag = jnp.sum(skew, axis=1)[:, :N_DIAG]
    n_far = BAND - MAX_REL
    far = jnp.sum(diag[:, :n_far], axis=1, keepdims=True)
    near = diag[:, n_far:][:, ::-1]
    zeros = jnp.zeros((H, MAX_REL - (CHUNK - 1)), F32)
    return jnp.concatenate([zeros, near, far], axis=1)


def _tri(n, lower):
    r = lax.broadcasted_iota(jnp.int32, (n, n), 0)
    c = lax.broadcasted_iota(jnp.int32, (n, n), 1)
    return jnp.where((c <= r) if lower else (c >= r), 1.0, 0.0).astype(F32)


def _hgrn_gates(zq_ref, zf_ref, lbl_ref, q_s, k_s, b_s):
    lb = jax.nn.sigmoid(lbl_ref[0:1, :] - lbl_ref[1:2, :])
    zq = zq_ref[...]
    sig = jax.nn.sigmoid(zf_ref[...])
    f = lb + (1.0 - lb) * sig
    sq = jax.nn.sigmoid(zq)
    q_s[...] = zq * sq
    k_s[...] = 1.0 - f
    b_s[...] = _dot(_tri(CHUNK, True), jnp.log(f), precision=HIGHEST)
    return lb, sig, f, sq


def _sub_rows(i):
    return pl.ds(i * SUB, SUB)


def _row_mask(s):
    return lax.broadcasted_iota(jnp.int32, (SUB, HD_B), 0) >= s


def _decay_from(b_sub, b_row, s):
    return jnp.where(_row_mask(s), jnp.exp(jnp.minimum(b_sub - b_row, 0.0)), 0.0)


def _hgrn_fwd(proj, lb_logits, gnorm_g, rider=None):
    T = proj.shape[0]
    nC = T // CHUNK
    W = lb_logits.shape[1]
    G = W // HD_B // HGRN_HEADS
    col0 = (proj.shape[1] - 4 * W) // (HD_B * HGRN_HEADS)
    wide = HGRN_HEADS * HD_B

    def body(*refs):
        @pl.when(pl.program_id(1) == 0)
        def _():
            refs[9][...] = jnp.zeros_like(refs[9])

        for h in range(HGRN_HEADS):
            lanes = pl.ds(h * HD_B, HD_B)
            one_head(*[r.at[:, lanes] for r in refs[:5]], refs[5], *[r.at[:, lanes] for r in refs[6:8]],
                     *[r.at[h] for r in refs[8:]])

    def one_head(zq_ref, zf_ref, xi_ref, zg_ref, lbl_ref, gn_ref, mix_ref, o_ref, stall_ref, st_ref, q_s, k_s, b_s, acc_s):
        _hgrn_gates(zq_ref, zf_ref, lbl_ref, q_s, k_s, b_s)
        q, k, b = q_s[...], k_s[...], b_s[...]
        st = st_ref[...]
        stall_ref[...] = st
        b_last = b_s[CHUNK - 1:CHUNK, :]
        acc_s[...] = _dot((q * jnp.exp(b)).astype(BF16), st.astype(BF16), NT)
        for i in range(CHUNK // SUB):
            rows = _sub_rows(i)
            q_i, b_i = q_s[rows, :], b_s[rows, :]
            acc = jnp.zeros((SUB, HD_B), F32)
            if i:
                past = pl.ds(0, i * SUB)
                b_ref = b_s[i * SUB - 1:i * SUB, :]
                qs = (q_i * jnp.exp(b_i - b_ref)).astype(BF16)
                ks = (k_s[past, :] * jnp.exp(b_ref - b_s[past, :])).astype(BF16)
                acc += _dot(_dot(qs, ks, NT).astype(BF16), xi_ref[past, :].astype(BF16))
            for s in range(SUB):
                row = pl.ds(i * SUB + s, 1)
                w = q_i * _decay_from(b_i, b_s[row, :], s)
                acc += jnp.sum(w * k_s[row, :], axis=-1, keepdims=True) * xi_ref[row, :]
            acc_s[rows, :] += acc
        o = acc_s[...]
        kd = (k * jnp.exp(b_last - b)).astype(BF16)
        st_ref[...] = st * jnp.exp(b_last) + _dot(xi_ref[...].astype(BF16), kd, TN)
        o_ref[...] = o
        zg = zg_ref[...]
        rr = lax.rsqrt(jnp.mean(o * o, axis=-1, keepdims=True) + EPS)
        mix_ref[...] = (o * rr * gn_ref[...] * (zg * jax.nn.sigmoid(zg))).astype(mix_ref.dtype)

    col = lambda part: pl.BlockSpec((CHUNK, wide), lambda g, n: (n, col0 + part * G + g))
    out_blk = pl.BlockSpec((CHUNK, wide), lambda g, n: (n, g))
    tile = pltpu.VMEM((HGRN_HEADS, CHUNK, HD_B), F32)
    return _call(
        body, rider, name="hgrn_fwd", grid=(G, nC),
        in_specs=[col(0), col(1), col(2), col(3), pl.BlockSpec((2, wide), lambda g, n: (0, g)),
                  pl.BlockSpec((1, HD_B), lambda g, n: (0, 0))],
        out_specs=[out_blk, out_blk, pl.BlockSpec((HGRN_HEADS, None, HD_B, HD_B), lambda g, n: (g, n, 0, 0))],
        out_shape=[jax.ShapeDtypeStruct((T, W), BF16), jax.ShapeDtypeStruct((T, W), F32),
                   jax.ShapeDtypeStruct((G * HGRN_HEADS, nC, HD_B, HD_B), F32)],
        scratch_shapes=[pltpu.VMEM((HGRN_HEADS, HD_B, HD_B), F32), tile, tile, tile, tile],
        compiler_params=_params(("arbitrary", "arbitrary")),
        operands=(proj, proj, proj, proj, lb_logits, gnorm_g))


def _hgrn_bwd(proj, lb_logits, gnorm_g, o_b, st_all, dmixin, rider=None):
    T = proj.shape[0]
    nC = T // CHUNK
    W = lb_logits.shape[1]
    G = W // HD_B // HGRN_HEADS
    wide = HGRN_HEADS * HD_B
    col0 = (proj.shape[1] - 4 * W) // wide
    dcol0 = (dmixin.shape[1] - W) // wide

    def body(*refs):
        g, n = pl.program_id(0), pl.program_id(1)
        dl0_ref, dgn_ref, dst_ref = refs[13:16]

        @pl.when(n == 0)
        def _():
            dst_ref[...] = jnp.zeros_like(dst_ref)
            dl0_ref[...] = jnp.zeros_like(dl0_ref)

        @pl.when((n == 0) & (g == 0))
        def _():
            dgn_ref[...] = jnp.zeros_like(dgn_ref)

        for h in range(HGRN_HEADS):
            lanes = pl.ds(h * HD_B, HD_B)
            cut = lambda r: r.at[:, lanes]
            one_head(*[cut(r) for r in refs[:5]], refs[5], cut(refs[6]), refs[7].at[h], cut(refs[8]),
                     *[cut(r) for r in refs[9:14]], dgn_ref, *[r.at[h] for r in refs[15:]])

    def one_head(zq_ref, zf_ref, xi_ref, zg_ref, lbl_ref, gn_ref, o_ref, st_ref, dout_ref,
                 dzq_ref, dzf_ref, dxi_ref, dzg_ref, dl0_ref, dgn_ref, dst_ref, q_s, k_s, b_s, do_s, dq_s, dk_s, di_s):
        lb, sig, f, sq = _hgrn_gates(zq_ref, zf_ref, lbl_ref, q_s, k_s, b_s)
        q, k, b = q_s[...], k_s[...], b_s[...]
        zg, o, dout = zg_ref[...], o_ref[...], dout_ref[...]
        sg = jax.nn.sigmoid(zg)
        rr = lax.rsqrt(jnp.mean(o * o, axis=-1, keepdims=True) + EPS)
        on = o * rr
        gn = gn_ref[...]
        dzg_ref[...] = (dout * on * gn * sg * (1.0 + zg * (1.0 - sg))).astype(dzg_ref.dtype)
        d_on = dout * zg * sg
        dgn_ref[...] += _colsum(d_on * on)
        d_on = d_on * gn
        do = rr * (d_on - on * jnp.mean(d_on * on, axis=-1, keepdims=True))
        do_s[...] = do
        dob = do.astype(BF16)
        st, dst = st_ref[...], dst_ref[...]
        b_last = b_s[CHUNK - 1:CHUNK, :]
        eb, e_last, k_dec = jnp.exp(b), jnp.exp(b_last), jnp.exp(b_last - b)
        qt, kd = q * eb, k * k_dec
        dstb = dst.astype(BF16)
        xib = xi_ref[...].astype(BF16)
        d_kd = _dot(xib, dstb)
        dq_s[...] = _dot(dob, st.astype(BF16)) * eb
        dk_s[...] = d_kd * k_dec
        di_s[...] = _dot(kd.astype(BF16), dstb, NT)
        d_b_last = e_last * _colsum(st * dst) + _colsum(d_kd * kd)
        dst_ref[...] = _dot(dob, qt.astype(BF16), TN) + dst * e_last
        for i in range(CHUNK // SUB):
            rows = _sub_rows(i)
            q_i, b_i, do_i = q_s[rows, :], b_s[rows, :], do_s[rows, :]
            dq_i = jnp.zeros((SUB, HD_B), F32)
            if i:
                past = pl.ds(0, i * SUB)
                b_ref = b_s[i * SUB - 1:i * SUB, :]
                e_q, e_k = jnp.exp(b_i - b_ref), jnp.exp(b_ref - b_s[past, :])
                qs, ks = (q_i * e_q).astype(BF16), (k_s[past, :] * e_k).astype(BF16)
                xi_p, do_b = xi_ref[past, :].astype(BF16), do_i.astype(BF16)
                di_s[past, :] += _dot(_dot(ks, qs, NT).astype(BF16), do_b)
                dq_i += _dot(_dot(do_b, xi_p, NT).astype(BF16), ks) * e_q
                dk_s[past, :] += _dot(_dot(xi_p, do_b, NT).astype(BF16), qs) * e_k
            for s in range(SUB):
                row = pl.ds(i * SUB + s, 1)
                k_row, i_row = k_s[row, :], xi_ref[row, :]
                e = _decay_from(b_i, b_s[row, :], s)
                w = q_i * e
                a_col = jnp.sum(w * k_row, axis=-1, keepdims=True)
                da_col = jnp.sum(do_i * i_row, axis=-1, keepdims=True)
                di_s[row, :] += _colsum(a_col * do_i)
                dq_i += da_col * e * k_row
                dk_s[row, :] += _colsum(da_col * w)
            dq_s[rows, :] += dq_i
        dq, dk = dq_s[...], dk_s[...]
        db = q * dq - k * dk
        is_last = lax.broadcasted_iota(jnp.int32, (CHUNK, HD_B), 0) == CHUNK - 1
        db = db + jnp.where(is_last, d_b_last, 0.0)
        df = _dot(_tri(CHUNK, False), db, precision=HIGHEST) / f - dk
        dzf_ref[...] = (df * (1.0 - lb) * sig * (1.0 - sig)).astype(dzf_ref.dtype)
        dl0_ref[...] += _colsum(df * (1.0 - sig)) * (lb * (1.0 - lb))
        zq = zq_ref[...]
        dzq_ref[...] = (dq * sq * (1.0 + zq * (1.0 - sq))).astype(dzq_ref.dtype)
        dxi_ref[...] = di_s[...].astype(dxi_ref.dtype)

    rev = lambda n: nC - 1 - n
    col = lambda part: pl.BlockSpec((CHUNK, wide), lambda g, n: (rev(n), col0 + part * G + g))
    blk = pl.BlockSpec((CHUNK, wide), lambda g, n: (rev(n), g))
    tile = pltpu.VMEM((HGRN_HEADS, CHUNK, HD_B), F32)
    out_big = jax.ShapeDtypeStruct((T, W), BF16)
    return _call(
        body, rider, name="hgrn_bwd", grid=(G, nC),
        in_specs=[col(0), col(1), col(2), col(3), pl.BlockSpec((2, wide), lambda g, n: (0, g)),
                  pl.BlockSpec((1, HD_B), lambda g, n: (0, 0)), blk,
                  pl.BlockSpec((HGRN_HEADS, None, HD_B, HD_B), lambda g, n: (g, rev(n), 0, 0)),
                  pl.BlockSpec((CHUNK, wide), lambda g, n: (rev(n), dcol0 + g))],
        out_specs=[blk, blk, blk, blk, pl.BlockSpec((1, wide), lambda g, n: (0, g)),
                   pl.BlockSpec((1, HD_B), lambda g, n: (0, 0))],
        out_shape=[out_big, out_big, out_big, out_big, jax.ShapeDtypeStruct((1, W), F32),
                   jax.ShapeDtypeStruct((1, HD_B), F32)],
        scratch_shapes=[pltpu.VMEM((HGRN_HEADS, HD_B, HD_B), F32)] + [tile] * 7,
        compiler_params=_params(("arbitrary", "arbitrary")),
        operands=(proj, proj, proj, proj, lb_logits, gnorm_g, o_b, st_all, dmixin))


def _adamw_math(g, w, m, v):
    m = B1 * m + (1.0 - B1) * g
    v = B2 * v + (1.0 - B2) * (g * g)
    m_hat = m / (1.0 - B1 ** STEP)
    v_hat = v / (1.0 - B2 ** STEP)
    return -LR * (m_hat / (jnp.sqrt(v_hat) + ADAM_EPS) + WD * w), m, v


def _adamw(g, w, m, v, name):
    R, C = g.shape
    tr = _row_tile(R, C)

    def body(g_ref, w_ref, m_ref, v_ref, go_ref, d_ref, mo_ref, vo_ref):
        g = g_ref[...]
        go_ref[...] = g
        d_ref[...], mo_ref[...], vo_ref[...] = _adamw_math(g, w_ref[...], m_ref[...], v_ref[...])

    blk = pl.BlockSpec((tr, C), lambda i: (i, 0))
    return pl.pallas_call(
        body, name=name, grid=(R // tr,), in_specs=[blk] * 4, out_specs=[blk] * 4,
        out_shape=[jax.ShapeDtypeStruct((R, C), F32)] * 4, compiler_params=_params(("parallel",), 40),
    )(g, w, m, v)


def _sum_pair(g_full, from_sibling, sel, name):
    Q, K, Ns = g_full.shape
    kh = K // 2
    tr = _row_tile(kh, Ns)
    nh = kh // tr

    def body(sel_ref, a_ref, b_ref, o_ref):
        o_ref[...] = (a_ref[...].astype(F32) + b_ref[...].astype(F32)).astype(o_ref.dtype)

    return pl.pallas_call(
        body, name=name,
        grid_spec=pltpu.PrefetchScalarGridSpec(
            num_scalar_prefetch=1, grid=(Q, nh),
            in_specs=[pl.BlockSpec((None, tr, Ns), lambda q, i, sel: (q, sel[1] * nh + i, 0)),
                      pl.BlockSpec((None, tr, Ns), lambda q, i, sel: (q, i, 0))],
            out_specs=pl.BlockSpec((None, tr, Ns), lambda q, i, sel: (q, i, 0))),
        out_shape=jax.ShapeDtypeStruct((Q, kh, Ns), BF16), compiler_params=_params(("parallel", "parallel")),
    )(sel, g_full, from_sibling)


def _sum_chips(pair_sum, from_chips, sel, name):
    Q, kh, Ns = pair_sum.shape
    tr = _row_tile(kh, Ns)
    nh = kh // tr

    def body(sel_ref, a_ref, b0_ref, b1_ref, b2_ref, o_ref):
        up = lambda r: r[...].astype(F32)
        o_ref[...] = ((up(a_ref) + up(b0_ref)) + up(b1_ref)) + up(b2_ref)

    recv = lambda k: pl.BlockSpec((None, tr, Ns), lambda i, sel: (k, i, 0))
    return pl.pallas_call(
        body, name=name,
        grid_spec=pltpu.PrefetchScalarGridSpec(
            num_scalar_prefetch=1, grid=(nh,),
            in_specs=[pl.BlockSpec((None, tr, Ns), lambda i, sel: (sel[0], i, 0)), recv(0), recv(1), recv(2)],
            out_specs=pl.BlockSpec((tr, Ns), lambda i, sel: (sel[1] * nh + i, 0))),
        out_shape=jax.ShapeDtypeStruct((2 * kh, Ns), F32), compiler_params=_params(("parallel",)),
    )(sel, pair_sum, from_chips, from_chips, from_chips)


def _gather_small(v, name):
    R, L = v.shape

    def body(v_ref, out_ref, send_sems, recv_sems):
        x, y, c = _place()
        me = 4 * x + 2 * y + c
        out_ref[me] = v_ref[...]
        peers = [(_flip(x, k >> 2 & 1), _flip(y, k >> 1 & 1), _flip(c, k & 1)) for k in range(1, N_DEV)]

        def copy(k, row, to):
            return pltpu.make_async_remote_copy(src_ref=v_ref, dst_ref=out_ref.at[row], send_sem=send_sems.at[k],
                                                recv_sem=recv_sems.at[k], device_id=to, device_id_type=MESH)

        sends = [copy(k, me, peer) for k, peer in enumerate(peers)]
        for cp in sends:
            cp.start()
        for k, (px, py, pc) in enumerate(peers):
            copy(k, 4 * px + 2 * py + pc, (x, y, c)).wait_recv()
        for cp in sends:
            cp.wait_send()

    vmem = pl.BlockSpec(memory_space=pltpu.VMEM)
    return pl.pallas_call(
        body, name=name, in_specs=[vmem], out_specs=vmem, out_shape=jax.ShapeDtypeStruct((N_DEV, R, L), F32),
        scratch_shapes=[pltpu.SemaphoreType.DMA((N_DEV - 1,)), pltpu.SemaphoreType.DMA((N_DEV - 1,))],
    )(v)


def _silu(v):
    return v * jax.nn.sigmoid(v)


def _ada_fwd(c_all, w_ada, tn=512):
    M, D = c_all.shape
    Ns = w_ada.shape[1]

    def body(c_ref, w_ref, o_ref):
        o_ref[...] = _dot(_silu(c_ref[...]).astype(BF16), w_ref[...].astype(BF16))

    return pl.pallas_call(
        body, name="ada_fwd", grid=(Ns // tn,),
        in_specs=[pl.BlockSpec((M, D), lambda j: (0, 0)), pl.BlockSpec((D, tn), lambda j: (0, j))],
        out_specs=pl.BlockSpec((M, tn), lambda j: (0, j)), out_shape=jax.ShapeDtypeStruct((M, Ns), F32),
        compiler_params=_params(("parallel",)),
    )(c_all, w_ada)


def _ada_bwd(c_all, dmod, w, m, v, tk=256, tn=1536):
    M, D = c_all.shape
    Ns = dmod.shape[1]

    def body(c_ref, d_ref, w_ref, m_ref, v_ref, g_ref, dl_ref, mo_ref, vo_ref):
        g = _dot(_silu(c_ref[...]).astype(BF16), d_ref[...].astype(BF16), TN)
        g_ref[...] = g
        dl_ref[...], mo_ref[...], vo_ref[...] = _adamw_math(g, w_ref[...], m_ref[...], v_ref[...])

    blk = pl.BlockSpec((tk, tn), lambda i, j: (i, j))
    return pl.pallas_call(
        body, name="ada_bwd", grid=(D // tk, Ns // tn),
        in_specs=[pl.BlockSpec((M, tk), lambda i, j: (0, i)), pl.BlockSpec((M, tn), lambda i, j: (0, j)), blk, blk, blk],
        out_specs=[blk] * 4, out_shape=[jax.ShapeDtypeStruct((D, Ns), F32)] * 4,
        compiler_params=_params(("parallel", "parallel"), 40),
    )(c_all, dmod, w, m, v)


def _small_update(g_all, w, m, v):
    R, L = w.shape

    def body(g_ref, w_ref, m_ref, v_ref, go_ref, d_ref, mo_ref, vo_ref):
        g = g_ref[0]
        for d in range(1, N_DEV):
            g = g + g_ref[d]
        go_ref[...] = g
        d_ref[...], mo_ref[...], vo_ref[...] = _adamw_math(g, w_ref[...], m_ref[...], v_ref[...])

    return pl.pallas_call(body, name="small_update", out_shape=[jax.ShapeDtypeStruct((R, L), F32)] * 4)(g_all, w, m, v)


def _pack(parts, rows):
    flat = jnp.concatenate([p.reshape(-1) for p in parts])
    return jnp.pad(flat, (0, rows * 128 - flat.shape[0])).reshape(rows, 128)


def _unpack(packed, shapes):
    flat, out, at = packed.reshape(-1), [], 0
    for shp in shapes:
        size = 1
        for d in shp:
            size *= d
        out.append(flat[at:at + size].reshape(shp))
        at += size
    return out


def _layer(x, tgt, mod, wts, rel_bias, attn_norm_g, lb_logits, gnorm_g, ln1_g, ln1_b, ln2_g, ln2_b, place=None):
    T, D = x.shape
    aw = attn_norm_g.shape[1]
    shift1, scale1, gate1, shift2, scale2, gate2 = [mod[i:i + 1] for i in range(6)]

    def gather(n, rows=None, into=None):
        return None if place is None else _gather_rider(wts[n], rows, None if into is None else into[0])

    def gathered(n, rode):
        return wts[n] if place is None else lax.dynamic_update_index_in_dim(rode[0], wts[n], place[0], 0)

    def blocks(g):
        return g.reshape(N_CHIPS, -1, g.shape[2])

    def to_sibling(g):
        return None if place is None else _pair_rider(g)

    def pair_sum(n, g, rode=None):
        if place is None:
            return g
        rode = _alone(_pair_rider(g), n + "_send_pair") if rode is None else rode
        return _sum_pair(g, rode[0], place[1], n + "_sum_pair")

    def to_chips(p, rows=None, into=None):
        return None if place is None else _chips_rider(p, rows, None if into is None else into[0])

    def summed(n, p, rode):
        return p if place is None else _sum_chips(p, rode[0], place[1], n + "_sum_chips")

    def to_both(block):
        return None if place is None else _share_rider(block)

    def carrying(mm, *args, rider, **kw):
        return mm(*args, rider=rider, **kw) if rider is not None else (mm(*args, **kw), None)

    w_in = gathered("w_in", None if place is None else [wts["w_in_gathered"]])
    h1 = _pre_mixer(x, scale1, shift1)
    n_qkv = 3 * aw // 256
    kh_o, kh_f = wts["w_o"].shape[-2] // 2, wts["w_ffn_in"].shape[-2] // 2
    o_cut, f_cuts = 3 * kh_o // 8, (25 * kh_f // 64, 57 * kh_f // 64)
    qkv, rode = carrying(_mm_nn, h1, w_in, tm=ZPAD, tn=256, tk=D, name="proj_qkv", cols=(0, n_qkv), o_dtype=BF16,
                         pad_rows=ZPAD, rider=gather("w_o", (0, o_cut)))
    proj, rode = carrying(_mm_nn, h1, w_in, tm=2048, tn=256, tk=D, name="proj_rec",
                          cols=(n_qkv, N_CHIPS * w_in.shape[2] // 256), rider=gather("w_o", (o_cut, kh_o - o_cut), rode))
    w_o3 = gathered("w_o", rode).reshape(1, D, D)
    bias = _bias_band(rel_bias)
    (mix_a, probs), rode = _attn_fwd(qkv, bias, attn_norm_g, rider=gather("w_ffn_in", (0, f_cuts[0])))
    (mix_b, o_b, st_all), rode = _hgrn_fwd(
        proj, lb_logits, gnorm_g, rider=gather("w_ffn_in", (f_cuts[0], f_cuts[1] - f_cuts[0]), rode))
    mixin = jnp.concatenate([mix_a, mix_b], axis=1)
    mix = _mm_nn(mixin, w_o3, tm=1024, tn=512, tk=D, name="mix_out")
    if place is None:
        x1, h2 = _post_mixer(mix, x, gate1, ln1_g, ln1_b, scale2, shift2)
    else:
        (x1, h2), rode = _post_mixer(mix, x, gate1, ln1_g, ln1_b, scale2, shift2,
                                     rider=gather("w_ffn_in", (f_cuts[1], kh_f - f_cuts[1]), rode))
    w_ffn_in = gathered("w_ffn_in", rode)
    (gate, up, act), rode = _ffn_in_swiglu(h2, w_ffn_in, tm=2048, tn=256, rider=gather("w_ffn_out"))
    w_out3 = gathered("w_ffn_out", rode)
    w_out3 = w_out3.reshape(1, -1, w_out3.shape[2])
    d_ff = w_out3.shape[1]
    f = _mm_nn(act, w_out3, tm=1024, tn=1024, tk=d_ff // 4, name="ffn_out")
    du2, df, acc2 = _loss_head(f, x1, tgt, gate2, ln2_g, ln2_b)
    loss = (0.5 / D) * jnp.sum(acc2[3])
    g = blocks(_mm_tn(act, df, q=1, tk=512, tn=1024, tt=T, name="g_ffn_out"))
    d_gate_up, rode = _d_act_swiglu(df, w_out3, gate, up, tm=1024, to=512, rider=to_sibling(g))
    p_out = pair_sum("w_ffn_out", g, rode)
    dff = jnp.concatenate(d_gate_up, axis=1)
    cut = 21 * p_out.shape[1] // 44
    dh2, rode = carrying(_mm_nt, dff, w_ffn_in, tm=1024, to=1024, tn=w_ffn_in.shape[2], name="d_h2",
                         rider=to_chips(p_out, (0, cut)))
    g, rode = carrying(_mm_tn, h2, dff, q=N_CHIPS, tk=512, tn=w_ffn_in.shape[2] // 2, tt=T, name="g_ffn_in",
                       rider=to_chips(p_out, (cut, p_out.shape[1] - cut), rode))
    g_ffn_out = summed("w_ffn_out", p_out, rode)
    if place is None:
        (du1, dmix, acc1), p_fin = _mid_bwd(dh2, du2, x1, mix, x, gate1, ln1_g, scale2), g
    else:
        (du1, dmix, acc1), rode = _mid_bwd(dh2, du2, x1, mix, x, gate1, ln1_g, scale2,
                                           rider=_join(to_sibling(g), to_both(g_ffn_out)))
        p_fin, g_ffn_out = pair_sum("w_ffn_in", g, rode[:1]), rode[1]
    g = blocks(_mm_tn(mixin, dmix, q=1, tk=512, tn=1024, tt=T, name="g_o"))
    dmixin, rode = carrying(_mm_nt, dmix, w_o3, tm=1024, to=512, tn=D, name="d_mixin", rider=to_sibling(g))
    p_o = pair_sum("w_o", g, rode)
    cut = p_fin.shape[1] // 2
    (dq, dk, dv, dbias, dgain), rode = _attn_bwd(qkv, probs, attn_norm_g, dmixin, rider=to_chips(p_fin, (0, cut)))
    (dzq, dzf, dxi, dzg, dl0, dgn), rode = _hgrn_bwd(
        proj, lb_logits, gnorm_g, o_b, st_all, dmixin,
        rider=_join(to_chips(p_fin, (cut, p_fin.shape[1] - cut), rode), to_chips(p_o)))
    g_ffn_in, g_o = summed("w_ffn_in", p_fin, rode[:1]), summed("w_o", p_o, rode[1:])
    dproj = jnp.concatenate([dq, dk[KPAD:].astype(BF16), dv[KPAD:].astype(BF16), dzq, dzf, dxi, dzg], axis=1)
    g, rode = carrying(_mm_tn, h1, dproj, q=N_CHIPS, tk=512, tn=w_in.shape[2] // 2, tt=T, name="g_in",
                       rider=_join(to_both(g_ffn_in), to_both(g_o)))
    if place is not None:
        g_ffn_in, g_o = rode
    p_in = pair_sum("w_in", g)
    cut = 3 * p_in.shape[1] // 4
    dh1, rode = carrying(_mm_nt, dproj, w_in, tm=1024, to=1024, tn=w_in.shape[2], name="d_h1",
                         rider=to_chips(p_in, (0, cut)))
    if place is None:
        (grad_x, acc0), g_in = _first_bwd(dh1, du1, x, scale1), p_in
    else:
        (grad_x, acc0), rode = _first_bwd(dh1, du1, x, scale1, rider=to_chips(p_in, (cut, p_in.shape[1] - cut), rode))
        g_in, = _alone(to_both(summed("w_in", p_in, rode)), "w_in_share")
    dmod = jnp.concatenate([acc0[1:2], acc0[0:1], acc1[4:5], acc1[1:2], acc1[0:1], acc2[2:3]], axis=0)
    small = dict(rel_bias=_bias_band_grad(dbias), attn_norm_g=dgain,
                 lb_logits=jnp.concatenate([dl0, -dl0], axis=0), gnorm_g=dgn,
                 ln1_g=acc1[2:3], ln1_b=acc1[3:4], ln2_g=acc2[0:1], ln2_b=acc2[1:2])
    return loss, grad_x, dict(w_in=g_in, w_o=g_o, w_ffn_in=g_ffn_in, w_ffn_out=g_ffn_out), dmod, small


SMALL = ("rel_bias", "attn_norm_g", "lb_logits", "gnorm_g", "ln1_g", "ln1_b", "ln2_g", "ln2_b")
SMALL_ROWS = 256


def kernel(x, c, w_ada, b_ada, w_in, rel_bias, attn_norm_g, lb_logits, gnorm_g, w_o, ln1_g, ln1_b, w_ffn_in, w_ffn_out, ln2_g, ln2_b, loss_target, m_w_ada, m_b_ada, m_w_in, m_rel_bias, m_attn_norm_g, m_lb_logits, m_gnorm_g, m_w_o, m_ln1_g, m_ln1_b, m_w_ffn_in, m_w_ffn_out, m_ln2_g, m_ln2_b, v_w_ada, v_b_ada, v_w_in, v_rel_bias, v_attn_norm_g, v_lb_logits, v_gnorm_g, v_w_o, v_ln1_g, v_ln1_b, v_w_ffn_in, v_w_ffn_out, v_ln2_g, v_ln2_b):
    mx, my, mc = _place()
    me = 4 * mx + 2 * my + mc
    chip = 2 * mx + my
    sel = jnp.stack([chip, mc]).astype(jnp.int32)
    D = x.shape[2]
    ns_ada = w_ada.shape[2]

    big = dict(w_in=(w_in, m_w_in, v_w_in), w_o=(w_o, m_w_o, v_w_o), w_ffn_in=(w_ffn_in, m_w_ffn_in, v_w_ffn_in),
               w_ffn_out=(w_ffn_out, m_w_ffn_out, v_w_ffn_out))
    shards = dict(w_in=w_in[0].astype(BF16))
    kh, rode, at = shards["w_in"].shape[0] // 2, None, 0
    for n, part in (("w_ffn_in", 19), ("w_ffn_out", 9), ("w_o", 4)):
        rows = (at, part * kh // 32)
        (shards[n],), rode = _to_bf16(big[n][0][0], "cast_" + n,
                                      _gather_rider(shards["w_in"], rows, None if rode is None else rode[0]))
        at += rows[1]
    shards["w_in_gathered"] = rode[0]

    c_all = _gather_small(c.reshape(D // 128, 128), "gather_c").reshape(N_DEV, D)
    c_all = jnp.pad(c_all, ((0, 16 - N_DEV), (0, 0)))
    mod_cols = _ada_fwd(c_all, w_ada[0])[:N_DEV]
    mod_all = _gather_small(mod_cols.reshape(-1, 128), "gather_mod").reshape(N_DEV, N_DEV, ns_ada)
    mod = lax.dynamic_index_in_dim(mod_all[::2], me, axis=1, keepdims=False)
    mod = (mod.reshape(1, -1) + b_ada).reshape(6, D)

    loss, grad_x, g_big, dmod, g_small = _layer(
        x[0], loss_target[0], mod, shards, rel_bias[0], attn_norm_g, lb_logits, gnorm_g, ln1_g, ln1_b, ln2_g, ln2_b,
        place=(chip, sel))

    grads, deltas, new_m, new_v = {}, {}, {}, {}
    for n, (w, m, v) in big.items():
        g, d, mo, vo = _adamw(g_big[n], w[0], m[0], v[0], "adamw_" + n)
        grads[n], deltas[n], new_m[n], new_v[n] = g[None], d[None], mo[None], vo[None]

    small_in = dict(rel_bias=(rel_bias, m_rel_bias, v_rel_bias), attn_norm_g=(attn_norm_g, m_attn_norm_g, v_attn_norm_g),
                    lb_logits=(lb_logits, m_lb_logits, v_lb_logits), gnorm_g=(gnorm_g, m_gnorm_g, v_gnorm_g),
                    ln1_g=(ln1_g, m_ln1_g, v_ln1_g), ln1_b=(ln1_b, m_ln1_b, v_ln1_b), ln2_g=(ln2_g, m_ln2_g, v_ln2_g),
                    ln2_b=(ln2_b, m_ln2_b, v_ln2_b))
    g_all = _gather_small(_pack([dmod] + [g_small[n] for n in SMALL] + [loss], SMALL_ROWS), "gather_small")
    packed = [_pack([t] + [small_in[n][i] for n in SMALL] + [jnp.zeros((), F32)], SMALL_ROWS)
              for i, t in enumerate((b_ada, m_b_ada, v_b_ada))]
    shapes = [b_ada.shape] + [small_in[n][0].shape for n in SMALL] + [()]
    outs = [_unpack(o, shapes) for o in _small_update(g_all, *packed)]
    loss = outs[0][-1]
    for i, n in enumerate(("b_ada",) + SMALL):
        grads[n], deltas[n], new_m[n], new_v[n] = outs[0][i], outs[1][i], outs[2][i], outs[3][i]

    dmod_all = g_all[:, :6 * D // 128].reshape(N_DEV, 6 * D)
    dmod_cols = lax.dynamic_slice_in_dim(dmod_all, chip * ns_ada, ns_ada, axis=1)
    dmod_cols = jnp.pad(dmod_cols, ((0, 16 - N_DEV), (0, 0)))
    g, d, mo, vo = _ada_bwd(c_all, dmod_cols, w_ada[0], m_w_ada[0], v_w_ada[0])
    grads["w_ada"], deltas["w_ada"], new_m["w_ada"], new_v["w_ada"] = g[None], d[None], mo[None], vo[None]

    order = ("w_ada", "b_ada", "w_in", "rel_bias", "attn_norm_g", "lb_logits", "gnorm_g", "w_o", "ln1_g", "ln1_b",
             "w_ffn_in", "w_ffn_out", "ln2_g", "ln2_b")
    return (loss, grad_x[None], *[grads[n] for n in order], *[deltas[n] for n in order],
            *[new_m[n] for n in order], *[new_v[n] for n in order])
```

```python
import numpy as np
import jax
import jax.numpy as jnp
from jax import lax
from jax.experimental import pallas as pl
from jax.experimental.pallas import tpu as pltpu

F32 = jnp.float32
BF16 = jnp.bfloat16
MESH = pl.DeviceIdType.MESH
HIGHEST = lax.Precision.HIGHEST

CHUNK = 64
N_PAST = 8
QG = 4
QROWS = QG * CHUNK
KPAD = N_PAST * CHUNK
ZPAD = 2 * KPAD
UNION = (QG + N_PAST) * CHUNK
BAND = (N_PAST + 1) * CHUNK
HD_A = 64
HD_B = 128
SUB = 16
HGRN_HEADS = 8
MAX_REL = 256
EPS = 1e-5
ALPHA = 2.0 ** 0.25
LR, B1, B2, ADAM_EPS, WD, STEP = 1e-3, 0.9, 0.999, 1e-8, 0.01, 10
N_CHIPS = 4
N_DEV = 8
NEG = -1e30
TILE_BYTES = 3 << 19

NN = ((1,), (0,))
NT = ((1,), (1,))
TN = ((0,), (0,))


def _dot(a, b, dims=NN, precision=None):
    return lax.dot_general(a, b, (dims, ((), ())), preferred_element_type=F32, precision=precision)


def _params(sem=None, vmem_mb=None, **kw):
    return pltpu.CompilerParams(dimension_semantics=sem,
                                vmem_limit_bytes=None if vmem_mb is None else vmem_mb << 20, **kw)


def _row_tile(rows, cols):
    for cand in (512, 256, 128, 64, 32, 16, 8):
        if rows % cand == 0 and cand * cols * 4 <= TILE_BYTES:
            return cand
    raise ValueError((rows, cols))


def _place():
    return lax.axis_index("x"), lax.axis_index("y"), lax.axis_index("c")


def _flip(v, bit):
    return 1 - v if bit else v


ANY = pl.BlockSpec(memory_space=pl.ANY)
CHIP_FLIPS = ((1, 0), (0, 1), (1, 1))


class _Rider:
    def __init__(self, operands, out_shape, n_sems, start, finish, aliases=None):
        self.operands, self.out_shape, self.n_sems, self.start, self.finish = operands, out_shape, n_sems, start, finish
        self.aliases = aliases or {}


def _call(body, rider, *, name, grid, in_specs, out_specs, out_shape, scratch_shapes, compiler_params, operands):
    if rider is None:
        outs = pl.pallas_call(body, name=name, grid=grid, in_specs=in_specs, out_specs=out_specs, out_shape=out_shape,
                              scratch_shapes=scratch_shapes, compiler_params=compiler_params)(*operands)
        return list(outs), []
    n_in, n_out, n_sc = len(in_specs), len(out_specs), len(scratch_shapes)
    r_in, r_out = len(rider.operands), len(rider.out_shape)

    def carried(*refs):
        refs = list(refs)
        cuts = [n_in, r_in, n_out, r_out, n_sc]
        ins, r_ins, outs, r_outs, scratch = [[refs.pop(0) for _ in range(n)] for n in cuts]
        first, last = None, None
        for axis, size in enumerate(grid):
            i = pl.program_id(axis)
            first = (i == 0) if first is None else first & (i == 0)
            last = (i == size - 1) if last is None else last & (i == size - 1)

        @pl.when(first)
        def _():
            rider.start(r_ins, r_outs, *refs)

        body(*ins, *outs, *scratch)

        @pl.when(last)
        def _():
            rider.finish(r_ins, r_outs, *refs)

    sems = [pltpu.SemaphoreType.DMA((rider.n_sems,)), pltpu.SemaphoreType.DMA((rider.n_sems,))]
    outs = pl.pallas_call(carried, name=name, grid=grid, in_specs=list(in_specs) + [ANY] * r_in,
                          out_specs=list(out_specs) + [ANY] * r_out, out_shape=list(out_shape) + rider.out_shape,
                          scratch_shapes=list(scratch_shapes) + sems, compiler_params=compiler_params,
                          input_output_aliases={n_in + i: n_out + o for i, o in rider.aliases.items()},
                          )(*operands, *rider.operands)
    return list(outs[:n_out]), list(outs[n_out:])


def _alone(rider, name):
    def body(*refs):
        ins, outs, sems = refs[:len(rider.operands)], refs[len(rider.operands):-2], refs[-2:]
        rider.start(ins, outs, *sems)
        rider.finish(ins, outs, *sems)

    return pl.pallas_call(
        body, name=name, in_specs=[ANY] * len(rider.operands), out_specs=[ANY] * len(rider.out_shape),
        out_shape=rider.out_shape, input_output_aliases=rider.aliases,
        scratch_shapes=[pltpu.SemaphoreType.DMA((rider.n_sems,)), pltpu.SemaphoreType.DMA((rider.n_sems,))],
    )(*rider.operands)


class _Sems:
    def __init__(self, sems, base):
        self.sems, self.base = sems, base

    @property
    def at(self):
        return self

    def __getitem__(self, k):
        return self.sems.at[self.base + k]


def _join(*riders):
    riders = [r for r in riders if r is not None]
    if len(riders) < 2:
        return riders[0] if riders else None

    def parts(ins, outs, send_sems, recv_sems):
        i = o = s = 0
        for r in riders:
            ni, no = len(r.operands), len(r.out_shape)
            yield r, ins[i:i + ni], outs[o:o + no], _Sems(send_sems, s), _Sems(recv_sems, s)
            i, o, s = i + ni, o + no, s + r.n_sems

    def start(*refs):
        for r, *args in parts(*refs):
            r.start(*args)

    def finish(*refs):
        for r, *args in parts(*refs):
            r.finish(*args)

    aliases, i, o = {}, 0, 0
    for r in riders:
        aliases.update({i + a: o + b for a, b in r.aliases.items()})
        i, o = i + len(r.operands), o + len(r.out_shape)
    return _Rider([a for r in riders for a in r.operands], [s for r in riders for s in r.out_shape],
                  sum(r.n_sems for r in riders), start, finish, aliases)


def _gather_rider(shard, rows=None, into=None):
    K, Ns = shard.shape
    kh = K // 2
    first_row, n_rows = rows or (0, kh)

    def copies(w_ref, out_ref, send_sems, recv_sems):
        x, y, c = _place()
        chips = [(_flip(x, fx), _flip(y, fy)) for fx, fy in CHIP_FLIPS]

        def half(chip, which):
            return out_ref.at[2 * chip[0] + chip[1], pl.ds(which * kh + first_row, n_rows), :]

        def copy(k, dst, to, src=None):
            return pltpu.make_async_remote_copy(src_ref=dst if src is None else src, dst_ref=dst,
                                                send_sem=send_sems.at[k], recv_sem=recv_sems.at[k],
                                                device_id=to, device_id_type=MESH)

        def first():
            return [copy(j, half((x, y), c), (*chip, c), src=w_ref.at[pl.ds(c * kh + first_row, n_rows), :])
                    for j, chip in enumerate(chips)]

        def onward():
            return [copy(3 + j, half(chip, c), (x, y, 1 - c)) for j, chip in enumerate(chips)]

        def arriving(base, which):
            return [copy(base + j, half(chip, which), (x, y, c)) for j, chip in enumerate(chips)]

        return first, onward, arriving

    def start(ins, outs, send_sems, recv_sems):
        for cp in copies(ins[0], outs[0], send_sems, recv_sems)[0]():
            cp.start()

    def finish(ins, outs, send_sems, recv_sems):
        x, y, c = _place()
        first, onward, arriving = copies(ins[0], outs[0], send_sems, recv_sems)
        passed = onward()
        for arrived, cp in zip(arriving(0, c), passed):
            arrived.wait_recv()
            cp.start()
        for arrived in arriving(3, 1 - c):
            arrived.wait_recv()
        for cp in first() + passed:
            cp.wait_send()

    full = jax.ShapeDtypeStruct((N_CHIPS, K, Ns), shard.dtype)
    if into is None:
        return _Rider([shard], [full], 6, start, finish)
    return _Rider([shard, into], [full], 6, start, finish, aliases={1: 0})


def _pair_rider(g_full):
    Q, K, Ns = g_full.shape
    kh = K // 2

    def copy(g_ref, got_ref, send_sems, recv_sems):
        x, y, c = _place()
        return pltpu.make_async_remote_copy(src_ref=g_ref.at[:, pl.ds((1 - c) * kh, kh), :], dst_ref=got_ref,
                                            send_sem=send_sems.at[0], recv_sem=recv_sems.at[0],
                                            device_id=(x, y, 1 - c), device_id_type=MESH)

    def start(ins, outs, send_sems, recv_sems):
        copy(ins[0], outs[0], send_sems, recv_sems).start()

    def finish(ins, outs, send_sems, recv_sems):
        copy(ins[0], outs[0], send_sems, recv_sems).wait()

    return _Rider([g_full], [jax.ShapeDtypeStruct((Q, kh, Ns), g_full.dtype)], 1, start, finish)


def _share_rider(block):
    K, Ns = block.shape
    kh = K // 2

    def halves(out_ref):
        x, y, c = _place()
        return out_ref.at[pl.ds(c * kh, kh), :], out_ref.at[pl.ds((1 - c) * kh, kh), :], (x, y, 1 - c)

    def start(ins, outs, send_sems, recv_sems):
        mine, _, sibling = halves(outs[0])
        pltpu.make_async_remote_copy(src_ref=mine, dst_ref=mine, send_sem=send_sems.at[0], recv_sem=recv_sems.at[0],
                                     device_id=sibling, device_id_type=MESH).start()

    def finish(ins, outs, send_sems, recv_sems):
        mine, theirs, sibling = halves(outs[0])
        pltpu.make_async_remote_copy(src_ref=theirs, dst_ref=theirs, send_sem=send_sems.at[0], recv_sem=recv_sems.at[0],
                                     device_id=sibling, device_id_type=MESH).wait_recv()
        pltpu.make_async_remote_copy(src_ref=mine, dst_ref=mine, send_sem=send_sems.at[0], recv_sem=recv_sems.at[0],
                                     device_id=sibling, device_id_type=MESH).wait_send()

    return _Rider([block], [jax.ShapeDtypeStruct((K, Ns), block.dtype)], 1, start, finish, aliases={0: 0})


def _chips_rider(pair_sum, rows=None, into=None):
    Q, kh, Ns = pair_sum.shape
    first_row, n_rows = rows or (0, kh)

    def copies(p_ref, got_ref, send_sems, recv_sems):
        x, y, c = _place()
        part = pl.ds(first_row, n_rows)
        out = []
        for j, (fx, fy) in enumerate(CHIP_FLIPS):
            px, py = _flip(x, fx), _flip(y, fy)
            out.append(pltpu.make_async_remote_copy(
                src_ref=p_ref.at[2 * px + py, part, :], dst_ref=got_ref.at[j, part, :], send_sem=send_sems.at[j],
                recv_sem=recv_sems.at[j], device_id=(px, py, c), device_id_type=MESH))
        return out

    def start(ins, outs, send_sems, recv_sems):
        for cp in copies(ins[0], outs[0], send_sems, recv_sems):
            cp.start()

    def finish(ins, outs, send_sems, recv_sems):
        sends = copies(ins[0], outs[0], send_sems, recv_sems)
        for cp in sends:
            cp.wait_recv()
        for cp in sends:
            cp.wait_send()

    got = jax.ShapeDtypeStruct((Q - 1, kh, Ns), pair_sum.dtype)
    if into is None:
        return _Rider([pair_sum], [got], 3, start, finish)
    return _Rider([pair_sum, into], [got], 3, start, finish, aliases={1: 0})


def _mm(a, b, *, grid, a_spec, b_spec, o_spec, o_shape, o_dtype, dims, acc_shape, name, rider=None, zero_rows=0,
        vmem_mb=48):
    nk = grid[2]

    def body(a_ref, b_ref, o_ref, *scratch):
        if zero_rows:
            @pl.when(pl.program_id(0) < zero_rows)
            def _():
                o_ref[...] = jnp.zeros_like(o_ref)

            @pl.when(pl.program_id(0) >= zero_rows)
            def _():
                o_ref[...] = _dot(a_ref[...], b_ref[...], dims).astype(o_ref.dtype)
            return
        part = _dot(a_ref[...], b_ref[...], dims)
        if nk == 1:
            o_ref[...] = part.astype(o_ref.dtype)
            return
        acc_ref, = scratch
        k = pl.program_id(2)

        @pl.when(k == 0)
        def _():
            acc_ref[...] = part

        @pl.when(k > 0)
        def _():
            acc_ref[...] += part

        @pl.when(k == nk - 1)
        def _():
            o_ref[...] = acc_ref[...].astype(o_ref.dtype)

    (out,), rode = _call(
        body, rider, name=name, grid=grid, in_specs=[a_spec, b_spec], out_specs=[o_spec],
        out_shape=[jax.ShapeDtypeStruct(o_shape, o_dtype)],
        scratch_shapes=[] if nk == 1 else [pltpu.VMEM(acc_shape, F32)],
        compiler_params=_params(("parallel", "parallel", "arbitrary") if rider is None else ("arbitrary",) * 3, vmem_mb),
        operands=(a, b))
    return out if rider is None else (out, rode)


def _mm_nn(a, w, *, tm, tn, tk, name, rider=None, cols=None, o_dtype=F32, pad_rows=0):
    T, K = a.shape
    Q, _, Ns = w.shape
    nbs = Ns // tn
    tm = min(tm, T)
    j0, j1 = cols or (0, Q * nbs)
    lead = pad_rows // tm
    return _mm(a, w, grid=(lead + T // tm, j1 - j0, K // tk),
               a_spec=pl.BlockSpec((tm, tk), lambda i, j, k: (jnp.maximum(i - lead, 0), k)),
               b_spec=pl.BlockSpec((None, tk, tn), lambda i, j, k: ((j + j0) // nbs, k, (j + j0) % nbs)),
               o_spec=pl.BlockSpec((tm, tn), lambda i, j, k: (i, j)),
               o_shape=(pad_rows + T, (j1 - j0) * tn), o_dtype=o_dtype, dims=NN, acc_shape=(tm, tn), name=name,
               rider=rider, zero_rows=lead)


def _mm_nt(g, w, *, tm, to, tn, name, rider=None):
    T = g.shape[0]
    Q, K, Ns = w.shape
    nbs = Ns // tn
    tm = min(tm, T)
    return _mm(g, w, grid=(T // tm, K // to, Q * nbs),
               a_spec=pl.BlockSpec((tm, tn), lambda i, j, n: (i, n)),
               b_spec=pl.BlockSpec((None, to, tn), lambda i, j, n: (n // nbs, j, n % nbs)),
               o_spec=pl.BlockSpec((tm, to), lambda i, j, n: (i, j)),
               o_shape=(T, K), o_dtype=F32, dims=NT, acc_shape=(tm, to), name=name, rider=rider)


def _mm_tn(a, g, *, q, tk, tn, tt, name, rider=None):
    T, K = a.shape
    Ns = g.shape[1] // q
    nbs = Ns // tn
    return _mm(a, g, grid=(K // tk, q * nbs, T // tt),
               a_spec=pl.BlockSpec((tt, tk), lambda i, j, t: (t, i)),
               b_spec=pl.BlockSpec((tt, tn), lambda i, j, t: (t, j)),
               o_spec=pl.BlockSpec((None, tk, tn), lambda i, j, t: (j // nbs, i, j % nbs)),
               o_shape=(q, K, Ns), o_dtype=BF16, dims=TN, acc_shape=(tk, tn), name=name, rider=rider)


def _ln(u):
    mu = jnp.mean(u, axis=-1, keepdims=True)
    d = u - mu
    r = lax.rsqrt(jnp.mean(d * d, axis=-1, keepdims=True) + EPS)
    return d * r, r


def _ln_bwd(dy, un, r):
    return r * (dy - jnp.mean(dy, axis=-1, keepdims=True) - un * jnp.mean(dy * un, axis=-1, keepdims=True))


def _colsum(v):
    return jnp.sum(v, axis=0, keepdims=True)


def _rowwise(name, fn, bigs, vecs, out_dtypes, n_acc, tm=128, rider=None):
    T, D = bigs[0].shape
    nb, nv, no = len(bigs), len(vecs), len(out_dtypes)

    def body(*refs):
        outs, accs = fn([r[...] for r in refs[:nb]], [r[...] for r in refs[nb:nb + nv]])
        for r, o in zip(refs[nb + nv:nb + nv + no], outs):
            r[...] = o.astype(r.dtype)
        if n_acc:
            acc_ref = refs[nb + nv + no]

            @pl.when(pl.program_id(0) == 0)
            def _():
                acc_ref[...] = jnp.zeros_like(acc_ref)

            for row, a in enumerate(accs):
                acc_ref[row:row + 1, :] += a

    big_spec = pl.BlockSpec((tm, D), lambda i: (i, 0))
    vec_spec = pl.BlockSpec((1, D), lambda i: (0, 0))
    out_shape = [jax.ShapeDtypeStruct((T, D), dt) for dt in out_dtypes]
    out_specs = [big_spec] * no
    if n_acc:
        out_shape.append(jax.ShapeDtypeStruct((8, D), F32))
        out_specs.append(pl.BlockSpec((8, D), lambda i: (0, 0)))
    outs, rode = _call(
        body, rider, name=name, grid=(T // tm,), in_specs=[big_spec] * nb + [vec_spec] * nv,
        out_specs=out_specs, out_shape=out_shape, scratch_shapes=[],
        compiler_params=_params(("arbitrary",), 48), operands=(*bigs, *vecs))
    return outs if rider is None else (outs, rode)


def _to_bf16(w, name, rider=None):
    R, C = w.shape
    tr = _row_tile(R, C)

    def body(w_ref, o_ref):
        o_ref[...] = w_ref[...].astype(o_ref.dtype)

    blk = pl.BlockSpec((tr, C), lambda i: (i, 0))
    return _call(body, rider, name=name, grid=(R // tr,), in_specs=[blk], out_specs=[blk],
                 out_shape=[jax.ShapeDtypeStruct((R, C), BF16)], scratch_shapes=[],
                 compiler_params=_params(("arbitrary",)), operands=(w,))


def _pre_mixer(x, scale1, shift1):
    def fn(b, v):
        xn, _ = _ln(b[0])
        return [xn * (1.0 + v[0]) + v[1]], []
    return _rowwise("pre_mixer", fn, [x], [scale1, shift1], [BF16], 0)[0]


def _post_mixer(mix, x, gate1, g1, b1, scale2, shift2, rider=None):
    def fn(b, v):
        un1, _ = _ln(ALPHA * b[1] + v[0] * b[0])
        x1 = un1 * v[1] + v[2]
        xn1, _ = _ln(x1)
        return [x1, xn1 * (1.0 + v[3]) + v[4]], []
    return _rowwise("post_mixer", fn, [mix, x], [gate1, g1, b1, scale2, shift2], [F32, BF16], 0, rider=rider)


def _loss_head(f, x1, tgt, gate2, g2, b2):
    def fn(b, v):
        ff, xx, tt = b
        d_model = ff.shape[-1]
        un2, r2 = _ln(ALPHA * xx + v[0] * ff)
        err = un2 * v[1] + v[2] - tt
        dy = err * (1.0 / d_model)
        du2 = _ln_bwd(dy * v[1], un2, r2)
        return [du2, du2 * v[0]], [_colsum(dy * un2), _colsum(dy), _colsum(du2 * ff), _colsum(err * err)]
    return _rowwise("loss_head", fn, [f, x1, tgt], [gate2, g2, b2], [F32, BF16], 4)


def _mid_bwd(dh2, du2, x1, mix, x, gate1, g1, scale2, rider=None):
    def fn(b, v):
        dh, du, xx1, mm, xx = b
        xn1, r1n = _ln(xx1)
        dx1 = ALPHA * du + _ln_bwd(dh * (1.0 + v[2]), xn1, r1n)
        un1, r1 = _ln(ALPHA * xx + v[0] * mm)
        du1 = _ln_bwd(dx1 * v[1], un1, r1)
        return [du1, du1 * v[0]], [_colsum(dh * xn1), _colsum(dh), _colsum(dx1 * un1), _colsum(dx1),
                                   _colsum(du1 * mm)]
    return _rowwise("mid_bwd", fn, [dh2, du2, x1, mix, x], [gate1, g1, scale2], [F32, BF16], 5, rider=rider)


def _first_bwd(dh1, du1, x, scale1, rider=None):
    def fn(b, v):
        dh, du, xx = b
        xn, r0 = _ln(xx)
        return [ALPHA * du + _ln_bwd(dh * (1.0 + v[0]), xn, r0)], [_colsum(dh * xn), _colsum(dh)]
    return _rowwise("first_bwd", fn, [dh1, du1, x], [scale1], [F32], 2, rider=rider)


def _ffn_in_swiglu(h2, w, *, tm, tn, rider=None):
    T, K = h2.shape
    Q, _, Ns = w.shape
    nbs = Ns // tn
    half = Q * nbs // 2
    tm = min(tm, T)

    def body(a_ref, wg_ref, wu_ref, g_ref, u_ref, act_ref):
        a = a_ref[...]
        g, u = _dot(a, wg_ref[...]), _dot(a, wu_ref[...])
        g_ref[...] = g.astype(g_ref.dtype)
        u_ref[...] = u.astype(u_ref.dtype)
        act_ref[...] = (g * jax.nn.sigmoid(g) * u).astype(act_ref.dtype)

    cols = lambda first: pl.BlockSpec((None, K, tn), lambda i, j: ((j + first) // nbs, 0, (j + first) % nbs))
    blk = pl.BlockSpec((tm, tn), lambda i, j: (i, j))
    return _call(
        body, rider, name="ffn_in", grid=(T // tm, half),
        in_specs=[pl.BlockSpec((tm, K), lambda i, j: (i, 0)), cols(0), cols(half)], out_specs=[blk] * 3,
        out_shape=[jax.ShapeDtypeStruct((T, half * tn), BF16)] * 3, scratch_shapes=[],
        compiler_params=_params(("arbitrary", "arbitrary"), 48), operands=(h2, w, w))


def _d_act_swiglu(df, w, gate, up, *, tm, to, rider=None):
    T, N = df.shape
    F = w.shape[1]
    tm = min(tm, T)

    def body(df_ref, w_ref, g_ref, u_ref, dg_ref, du_ref):
        d = _dot(df_ref[...], w_ref[...], NT)
        g = g_ref[...].astype(F32)
        s = jax.nn.sigmoid(g)
        du_ref[...] = (d * g * s).astype(du_ref.dtype)
        dg_ref[...] = (d * u_ref[...].astype(F32) * s * (1.0 + g * (1.0 - s))).astype(dg_ref.dtype)

    blk = pl.BlockSpec((tm, to), lambda i, j: (i, j))
    return _call(
        body, rider, name="d_act", grid=(T // tm, F // to),
        in_specs=[pl.BlockSpec((tm, N), lambda i, j: (i, 0)), pl.BlockSpec((None, to, N), lambda i, j: (0, j, 0)), blk, blk],
        out_specs=[blk, blk], out_shape=[jax.ShapeDtypeStruct((T, F), BF16)] * 2, scratch_shapes=[],
        compiler_params=_params(("arbitrary", "arbitrary"), 48), operands=(df, w, gate, up))


PAIR = 2


def _fill_table(table_ref, band_ref):
    table_ref[...] = jnp.full(table_ref.shape, NEG, F32)
    for e in range(PAIR):
        for g in range(QG):
            table_ref[e, g * CHUNK:(g + 1) * CHUNK, g * CHUNK:g * CHUNK + BAND] = band_ref[e]


def _attn_probs(q_ref, k_ref, bias_ref, e, step):
    start = pl.multiple_of(step * QROWS, QROWS)
    lanes = pl.ds(e * HD_A, HD_A)
    s = _dot(q_ref[:, lanes], k_ref[pl.ds(start + ZPAD - KPAD, UNION), lanes], NT) * (HD_A ** -0.5) + bias_ref[e]
    col = lax.broadcasted_iota(jnp.int32, s.shape, 1)
    s = jnp.where(col + start >= KPAD, s, NEG)
    p = jnp.exp(s - jnp.max(s, axis=-1, keepdims=True))
    return p / jnp.sum(p, axis=-1, keepdims=True), start


def _attn_specs(T, n_pairs):
    wide = PAIR * HD_A
    per_step = pl.BlockSpec((QROWS, wide), lambda hp, n: (n, hp))
    queries = pl.BlockSpec((QROWS, wide), lambda hp, n: (n + ZPAD // QROWS, hp))
    keys = pl.BlockSpec((ZPAD + T, wide), lambda hp, n: (0, n_pairs + hp))
    values = pl.BlockSpec((ZPAD + T, wide), lambda hp, n: (0, 2 * n_pairs + hp))
    grads = pl.BlockSpec((KPAD + T, wide), lambda hp, n: (0, hp))
    table = pl.BlockSpec((PAIR, CHUNK, BAND), lambda hp, n: (hp, 0, 0))
    vec = pl.BlockSpec((1, wide), lambda hp, n: (0, hp))
    return per_step, queries, keys, values, grads, table, vec


def _probs_spec():
    return pl.BlockSpec((PAIR, QROWS, UNION), lambda hp, n: (hp, n, 0))


def _attn_fwd(qkv, bias, gain, rider=None):
    T = qkv.shape[0] - ZPAD
    W = gain.shape[1]
    n_pairs = W // (PAIR * HD_A)

    def body(q_ref, k_ref, v_ref, band_ref, gain_ref, o_ref, p_ref, table_ref):
        @pl.when(pl.program_id(1) == 0)
        def _():
            _fill_table(table_ref, band_ref)

        for e in range(PAIR):
            lanes = pl.ds(e * HD_A, HD_A)
            p, start = _attn_probs(q_ref, k_ref, table_ref, e, pl.program_id(1))
            p_ref[e] = p.astype(p_ref.dtype)
            o = _dot(p_ref[e], v_ref[pl.ds(start + ZPAD - KPAD, UNION), lanes])
            rr = lax.rsqrt(jnp.mean(o * o, axis=-1, keepdims=True) + EPS)
            o_ref[:, lanes] = (o * rr * gain_ref[:, lanes]).astype(o_ref.dtype)

    per_step, queries, keys, values, _, table, vec = _attn_specs(T, n_pairs)
    return _call(
        body, rider, name="attn_fwd", grid=(n_pairs, T // QROWS), in_specs=[queries, keys, values, table, vec],
        out_specs=[per_step, _probs_spec()],
        out_shape=[jax.ShapeDtypeStruct((T, W), BF16), jax.ShapeDtypeStruct((n_pairs * PAIR, T, UNION), BF16)],
        scratch_shapes=[pltpu.VMEM((PAIR, QROWS, UNION), F32)],
        compiler_params=_params(("arbitrary", "arbitrary"), 40), operands=(qkv, qkv, qkv, bias, gain))


def _attn_bwd(qkv, probs, gain, dmixin, rider=None):
    T = qkv.shape[0] - ZPAD
    W = gain.shape[1]
    n_pairs = W // (PAIR * HD_A)
    scale = HD_A ** -0.5

    def body(q_ref, k_ref, v_ref, p_ref, gain_ref, don_ref, dq_ref, dk_ref, dv_ref, dband_ref, dgain_ref, dtable_ref):
        n = pl.program_id(1)

        @pl.when(n == 0)
        def _():
            dk_ref[...] = jnp.zeros_like(dk_ref)
            dv_ref[...] = jnp.zeros_like(dv_ref)
            dtable_ref[...] = jnp.zeros_like(dtable_ref)
            dgain_ref[...] = jnp.zeros_like(dgain_ref)

        for e in range(PAIR):
            lanes = pl.ds(e * HD_A, HD_A)
            start = pl.multiple_of(n * QROWS, QROWS)
            keys, in_qkv = pl.ds(start, UNION), pl.ds(start + ZPAD - KPAD, UNION)
            pb = p_ref[e]
            p = pb.astype(F32)
            vb = v_ref[in_qkv, lanes]
            o = _dot(pb, vb)
            rr = lax.rsqrt(jnp.mean(o * o, axis=-1, keepdims=True) + EPS)
            on = o * rr
            d_on = don_ref[:, lanes]
            dgain_ref[:, lanes] += _colsum(d_on * on)
            dyo = d_on * gain_ref[:, lanes]
            do = rr * (dyo - on * jnp.mean(dyo * on, axis=-1, keepdims=True))
            dob = do.astype(BF16)
            dp = _dot(dob, vb, NT)
            ds = p * (dp - jnp.sum(do * o, axis=-1, keepdims=True))
            dtable_ref[e] += ds
            dsb = ds.astype(BF16)
            dq_ref[:, lanes] = (_dot(dsb, k_ref[in_qkv, lanes]) * scale).astype(dq_ref.dtype)
            dk_ref[keys, lanes] += _dot(dsb, q_ref[:, lanes], TN) * scale
            dv_ref[keys, lanes] += _dot(pb, dob, TN)

        @pl.when(n == T // QROWS - 1)
        def _():
            for e in range(PAIR):
                dband_ref[e] = sum(dtable_ref[e, g * CHUNK:(g + 1) * CHUNK, g * CHUNK:g * CHUNK + BAND]
                                   for g in range(QG))

    per_step, queries, keys, values, grads, table, vec = _attn_specs(T, n_pairs)
    H = n_pairs * PAIR
    return _call(
        body, rider, name="attn_bwd", grid=(n_pairs, T // QROWS),
        in_specs=[queries, keys, values, _probs_spec(), vec, per_step],
        out_specs=[per_step, grads, grads, table, vec],
        out_shape=[jax.ShapeDtypeStruct((T, W), BF16), jax.ShapeDtypeStruct((KPAD + T, W), F32),
                   jax.ShapeDtypeStruct((KPAD + T, W), F32), jax.ShapeDtypeStruct((H, CHUNK, BAND), F32),
                   jax.ShapeDtypeStruct((1, W), F32)],
        scratch_shapes=[pltpu.VMEM((PAIR, QROWS, UNION), F32)],
        compiler_params=_params(("arbitrary", "arbitrary"), 40),
        operands=(qkv, qkv, qkv, probs, gain, dmixin))


N_DIAG = CHUNK + BAND - 1


def _bias_band(rel_bias):
    H = rel_bias.shape[0]
    idx = np.clip(BAND - 1 - np.arange(N_DIAG), -MAX_REL, MAX_REL) + MAX_REL
    rolled = rel_bias[:, idx[(np.arange(N_DIAG) + CHUNK - 1) % N_DIAG]]
    flat = jnp.broadcast_to(rolled[:, None, :], (H, CHUNK, N_DIAG)).reshape(H, CHUNK * N_DIAG)
    return flat[:, :CHUNK * (N_DIAG - 1)].reshape(H, CHUNK, N_DIAG - 1)[:, :, :BAND]


def _bias_band_grad(dband):
    H = dband.shape[0]
    skew = jnp.pad(dband, ((0, 0), (0, 0), (CHUNK - 1, 0))).reshape(H, CHUNK * N_DIAG)
    skew = jnp.pad(skew, ((0, 0), (0, CHUNK))).reshape(H, CHUNK, N_DIAG + 1)
    diag = jnp.sum(skew, axis=1)[:, :N_DIAG]
    n_far = BAND - MAX_REL
    far = jnp.sum(diag[:, :n_far], axis=1, keepdims=True)
    near = diag[:, n_far:][:, ::-1]
    zeros = jnp.zeros((H, MAX_REL - (CHUNK - 1)), F32)
    return jnp.concatenate([zeros, near, far], axis=1)


def _tri(n, lower):
    r = lax.broadcasted_iota(jnp.int32, (n, n), 0)
    c = lax.broadcasted_iota(jnp.int32, (n, n), 1)
    return jnp.where((c <= r) if lower else (c >= r), 1.0, 0.0).astype(F32)


def _hgrn_gates(zq_ref, zf_ref, lbl_ref, q_s, k_s, b_s):
    lb = jax.nn.sigmoid(lbl_ref[0:1, :] - lbl_ref[1:2, :])
    zq = zq_ref[...]
    sig = jax.nn.sigmoid(zf_ref[...])
    f = lb + (1.0 - lb) * sig
    sq = jax.nn.sigmoid(zq)
    q_s[...] = zq * sq
    k_s[...] = 1.0 - f
    b_s[...] = _dot(_tri(CHUNK, True), jnp.log(f), precision=HIGHEST)
    return lb, sig, f, sq


def _sub_rows(i):
    return pl.ds(i * SUB, SUB)


def _row_mask(s):
    return lax.broadcasted_iota(jnp.int32, (SUB, HD_B), 0) >= s


def _decay_from(b_sub, b_row, s):
    return jnp.where(_row_mask(s), jnp.exp(jnp.minimum(b_sub - b_row, 0.0)), 0.0)


def _hgrn_fwd(proj, lb_logits, gnorm_g, rider=None):
    T = proj.shape[0]
    nC = T // CHUNK
    W = lb_logits.shape[1]
    G = W // HD_B // HGRN_HEADS
    col0 = (proj.shape[1] - 4 * W) // (HD_B * HGRN_HEADS)
    wide = HGRN_HEADS * HD_B

    def body(*refs):
        @pl.when(pl.program_id(1) == 0)
        def _():
            refs[9][...] = jnp.zeros_like(refs[9])

        for h in range(HGRN_HEADS):
            lanes = pl.ds(h * HD_B, HD_B)
            one_head(*[r.at[:, lanes] for r in refs[:5]], refs[5], *[r.at[:, lanes] for r in refs[6:8]],
                     *[r.at[h] for r in refs[8:]])

    def one_head(zq_ref, zf_ref, xi_ref, zg_ref, lbl_ref, gn_ref, mix_ref, o_ref, stall_ref, st_ref, q_s, k_s, b_s, acc_s):
        _hgrn_gates(zq_ref, zf_ref, lbl_ref, q_s, k_s, b_s)
        q, k, b = q_s[...], k_s[...], b_s[...]
        st = st_ref[...]
        stall_ref[...] = st
        b_last = b_s[CHUNK - 1:CHUNK, :]
        acc_s[...] = _dot((q * jnp.exp(b)).astype(BF16), st.astype(BF16), NT)
        for i in range(CHUNK // SUB):
            rows = _sub_rows(i)
            q_i, b_i = q_s[rows, :], b_s[rows, :]
            acc = jnp.zeros((SUB, HD_B), F32)
            if i:
                past = pl.ds(0, i * SUB)
                b_ref = b_s[i * SUB - 1:i * SUB, :]
                qs = (q_i * jnp.exp(b_i - b_ref)).astype(BF16)
                ks = (k_s[past, :] * jnp.exp(b_ref - b_s[past, :])).astype(BF16)
                acc += _dot(_dot(qs, ks, NT).astype(BF16), xi_ref[past, :].astype(BF16))
            for s in range(SUB):
                row = pl.ds(i * SUB + s, 1)
                w = q_i * _decay_from(b_i, b_s[row, :], s)
                acc += jnp.sum(w * k_s[row, :], axis=-1, keepdims=True) * xi_ref[row, :]
            acc_s[rows, :] += acc
        o = acc_s[...]
        kd = (k * jnp.exp(b_last - b)).astype(BF16)
        st_ref[...] = st * jnp.exp(b_last) + _dot(xi_ref[...].astype(BF16), kd, TN)
        o_ref[...] = o
        zg = zg_ref[...]
        rr = lax.rsqrt(jnp.mean(o * o, axis=-1, keepdims=True) + EPS)
        mix_ref[...] = (o * rr * gn_ref[...] * (zg * jax.nn.sigmoid(zg))).astype(mix_ref.dtype)

    col = lambda part: pl.BlockSpec((CHUNK, wide), lambda g, n: (n, col0 + part * G + g))
    out_blk = pl.BlockSpec((CHUNK, wide), lambda g, n: (n, g))
    tile = pltpu.VMEM((HGRN_HEADS, CHUNK, HD_B), F32)
    return _call(
        body, rider, name="hgrn_fwd", grid=(G, nC),
        in_specs=[col(0), col(1), col(2), col(3), pl.BlockSpec((2, wide), lambda g, n: (0, g)),
                  pl.BlockSpec((1, HD_B), lambda g, n: (0, 0))],
        out_specs=[out_blk, out_blk, pl.BlockSpec((HGRN_HEADS, None, HD_B, HD_B), lambda g, n: (g, n, 0, 0))],
        out_shape=[jax.ShapeDtypeStruct((T, W), BF16), jax.ShapeDtypeStruct((T, W), F32),
                   jax.ShapeDtypeStruct((G * HGRN_HEADS, nC, HD_B, HD_B), F32)],
        scratch_shapes=[pltpu.VMEM((HGRN_HEADS, HD_B, HD_B), F32), tile, tile, tile, tile],
        compiler_params=_params(("arbitrary", "arbitrary")),
        operands=(proj, proj, proj, proj, lb_logits, gnorm_g))


def _hgrn_bwd(proj, lb_logits, gnorm_g, o_b, st_all, dmixin, rider=None):
    T = proj.shape[0]
    nC = T // CHUNK
    W = lb_logits.shape[1]
    G = W // HD_B // HGRN_HEADS
    wide = HGRN_HEADS * HD_B
    col0 = (proj.shape[1] - 4 * W) // wide
    dcol0 = (dmixin.shape[1] - W) // wide

    def body(*refs):
        g, n = pl.program_id(0), pl.program_id(1)
        dl0_ref, dgn_ref, dst_ref = refs[13:16]

        @pl.when(n == 0)
        def _():
            dst_ref[...] = jnp.zeros_like(dst_ref)
            dl0_ref[...] = jnp.zeros_like(dl0_ref)

        @pl.when((n == 0) & (g == 0))
        def _():
            dgn_ref[...] = jnp.zeros_like(dgn_ref)

        for h in range(HGRN_HEADS):
            lanes = pl.ds(h * HD_B, HD_B)
            cut = lambda r: r.at[:, lanes]
            one_head(*[cut(r) for r in refs[:5]], refs[5], cut(refs[6]), refs[7].at[h], cut(refs[8]),
                     *[cut(r) for r in refs[9:14]], dgn_ref, *[r.at[h] for r in refs[15:]])

    def one_head(zq_ref, zf_ref, xi_ref, zg_ref, lbl_ref, gn_ref, o_ref, st_ref, dout_ref,
                 dzq_ref, dzf_ref, dxi_ref, dzg_ref, dl0_ref, dgn_ref, dst_ref, q_s, k_s, b_s, do_s, dq_s, dk_s, di_s):
        lb, sig, f, sq = _hgrn_gates(zq_ref, zf_ref, lbl_ref, q_s, k_s, b_s)
        q, k, b = q_s[...], k_s[...], b_s[...]
        zg, o, dout = zg_ref[...], o_ref[...], dout_ref[...]
        sg = jax.nn.sigmoid(zg)
        rr = lax.rsqrt(jnp.mean(o * o, axis=-1, keepdims=True) + EPS)
        on = o * rr
        gn = gn_ref[...]
        dzg_ref[...] = (dout * on * gn * sg * (1.0 + zg * (1.0 - sg))).astype(dzg_ref.dtype)
        d_on = dout * zg * sg
        dgn_ref[...] += _colsum(d_on * on)
        d_on = d_on * gn
        do = rr * (d_on - on * jnp.mean(d_on * on, axis=-1, keepdims=True))
        do_s[...] = do
        dob = do.astype(BF16)
        st, dst = st_ref[...], dst_ref[...]
        b_last = b_s[CHUNK - 1:CHUNK, :]
        eb, e_last, k_dec = jnp.exp(b), jnp.exp(b_last), jnp.exp(b_last - b)
        qt, kd = q * eb, k * k_dec
        dstb = dst.astype(BF16)
        xib = xi_ref[...].astype(BF16)
        d_kd = _dot(xib, dstb)
        dq_s[...] = _dot(dob, st.astype(BF16)) * eb
        dk_s[...] = d_kd * k_dec
        di_s[...] = _dot(kd.astype(BF16), dstb, NT)
        d_b_last = e_last * _colsum(st * dst) + _colsum(d_kd * kd)
        dst_ref[...] = _dot(dob, qt.astype(BF16), TN) + dst * e_last
        for i in range(CHUNK // SUB):
            rows = _sub_rows(i)
            q_i, b_i, do_i = q_s[rows, :], b_s[rows, :], do_s[rows, :]
            dq_i = jnp.zeros((SUB, HD_B), F32)
            if i:
                past = pl.ds(0, i * SUB)
                b_ref = b_s[i * SUB - 1:i * SUB, :]
                e_q, e_k = jnp.exp(b_i - b_ref), jnp.exp(b_ref - b_s[past, :])
                qs, ks = (q_i * e_q).astype(BF16), (k_s[past, :] * e_k).astype(BF16)
                xi_p, do_b = xi_ref[past, :].astype(BF16), do_i.astype(BF16)
                di_s[past, :] += _dot(_dot(ks, qs, NT).astype(BF16), do_b)
                dq_i += _dot(_dot(do_b, xi_p, NT).astype(BF16), ks) * e_q
                dk_s[past, :] += _dot(_dot(xi_p, do_b, NT).astype(BF16), qs) * e_k
            for s in range(SUB):
                row = pl.ds(i * SUB + s, 1)
                k_row, i_row = k_s[row, :], xi_ref[row, :]
                e = _decay_from(b_i, b_s[row, :], s)
                w = q_i * e
                a_col = jnp.sum(w * k_row, axis=-1, keepdims=True)
                da_col = jnp.sum(do_i * i_row, axis=-1, keepdims=True)
                di_s[row, :] += _colsum(a_col * do_i)
                dq_i += da_col * e * k_row
                dk_s[row, :] += _colsum(da_col * w)
            dq_s[rows, :] += dq_i
        dq, dk = dq_s[...], dk_s[...]
        db = q * dq - k * dk
        is_last = lax.broadcasted_iota(jnp.int32, (CHUNK, HD_B), 0) == CHUNK - 1
        db = db + jnp.where(is_last, d_b_last, 0.0)
        df = _dot(_tri(CHUNK, False), db, precision=HIGHEST) / f - dk
        dzf_ref[...] = (df * (1.0 - lb) * sig * (1.0 - sig)).astype(dzf_ref.dtype)
        dl0_ref[...] += _colsum(df * (1.0 - sig)) * (lb * (1.0 - lb))
        zq = zq_ref[...]
        dzq_ref[...] = (dq * sq * (1.0 + zq * (1.0 - sq))).astype(dzq_ref.dtype)
        dxi_ref[...] = di_s[...].astype(dxi_ref.dtype)

    rev = lambda n: nC - 1 - n
    col = lambda part: pl.BlockSpec((CHUNK, wide), lambda g, n: (rev(n), col0 + part * G + g))
    blk = pl.BlockSpec((CHUNK, wide), lambda g, n: (rev(n), g))
    tile = pltpu.VMEM((HGRN_HEADS, CHUNK, HD_B), F32)
    out_big = jax.ShapeDtypeStruct((T, W), BF16)
    return _call(
        body, rider, name="hgrn_bwd", grid=(G, nC),
        in_specs=[col(0), col(1), col(2), col(3), pl.BlockSpec((2, wide), lambda g, n: (0, g)),
                  pl.BlockSpec((1, HD_B), lambda g, n: (0, 0)), blk,
                  pl.BlockSpec((HGRN_HEADS, None, HD_B, HD_B), lambda g, n: (g, rev(n), 0, 0)),
                  pl.BlockSpec((CHUNK, wide), lambda g, n: (rev(n), dcol0 + g))],
        out_specs=[blk, blk, blk, blk, pl.BlockSpec((1, wide), lambda g, n: (0, g)),
                   pl.BlockSpec((1, HD_B), lambda g, n: (0, 0))],
        out_shape=[out_big, out_big, out_big, out_big, jax.ShapeDtypeStruct((1, W), F32),
                   jax.ShapeDtypeStruct((1, HD_B), F32)],
        scratch_shapes=[pltpu.VMEM((HGRN_HEADS, HD_B, HD_B), F32)] + [tile] * 7,
        compiler_params=_params(("arbitrary", "arbitrary")),
        operands=(proj, proj, proj, proj, lb_logits, gnorm_g, o_b, st_all, dmixin))


def _adamw_math(g, w, m, v):
    m = B1 * m + (1.0 - B1) * g
    v = B2 * v + (1.0 - B2) * (g * g)
    m_hat = m / (1.0 - B1 ** STEP)
    v_hat = v / (1.0 - B2 ** STEP)
    return -LR * (m_hat / (jnp.sqrt(v_hat) + ADAM_EPS) + WD * w), m, v


def _adamw(g, w, m, v, name):
    R, C = g.shape
    tr = _row_tile(R, C)

    def body(g_ref, w_ref, m_ref, v_ref, go_ref, d_ref, mo_ref, vo_ref):
        g = g_ref[...]
        go_ref[...] = g
        d_ref[...], mo_ref[...], vo_ref[...] = _adamw_math(g, w_ref[...], m_ref[...], v_ref[...])

    blk = pl.BlockSpec((tr, C), lambda i: (i, 0))
    return pl.pallas_call(
        body, name=name, grid=(R // tr,), in_specs=[blk] * 4, out_specs=[blk] * 4,
        out_shape=[jax.ShapeDtypeStruct((R, C), F32)] * 4, compiler_params=_params(("parallel",), 40),
    )(g, w, m, v)


def _sum_pair(g_full, from_sibling, sel, name):
    Q, K, Ns = g_full.shape
    kh = K // 2
    tr = _row_tile(kh, Ns)
    nh = kh // tr

    def body(sel_ref, a_ref, b_ref, o_ref):
        o_ref[...] = (a_ref[...].astype(F32) + b_ref[...].astype(F32)).astype(o_ref.dtype)

    return pl.pallas_call(
        body, name=name,
        grid_spec=pltpu.PrefetchScalarGridSpec(
            num_scalar_prefetch=1, grid=(Q, nh),
            in_specs=[pl.BlockSpec((None, tr, Ns), lambda q, i, sel: (q, sel[1] * nh + i, 0)),
                      pl.BlockSpec((None, tr, Ns), lambda q, i, sel: (q, i, 0))],
            out_specs=pl.BlockSpec((None, tr, Ns), lambda q, i, sel: (q, i, 0))),
        out_shape=jax.ShapeDtypeStruct((Q, kh, Ns), BF16), compiler_params=_params(("parallel", "parallel")),
    )(sel, g_full, from_sibling)


def _sum_chips(pair_sum, from_chips, sel, name):
    Q, kh, Ns = pair_sum.shape
    tr = _row_tile(kh, Ns)
    nh = kh // tr

    def body(sel_ref, a_ref, b0_ref, b1_ref, b2_ref, o_ref):
        up = lambda r: r[...].astype(F32)
        o_ref[...] = ((up(a_ref) + up(b0_ref)) + up(b1_ref)) + up(b2_ref)

    recv = lambda k: pl.BlockSpec((None, tr, Ns), lambda i, sel: (k, i, 0))
    return pl.pallas_call(
        body, name=name,
        grid_spec=pltpu.PrefetchScalarGridSpec(
            num_scalar_prefetch=1, grid=(nh,),
            in_specs=[pl.BlockSpec((None, tr, Ns), lambda i, sel: (sel[0], i, 0)), recv(0), recv(1), recv(2)],
            out_specs=pl.BlockSpec((tr, Ns), lambda i, sel: (sel[1] * nh + i, 0))),
        out_shape=jax.ShapeDtypeStruct((2 * kh, Ns), F32), compiler_params=_params(("parallel",)),
    )(sel, pair_sum, from_chips, from_chips, from_chips)


def _gather_small(v, name):
    R, L = v.shape

    def body(v_ref, out_ref, send_sems, recv_sems):
        x, y, c = _place()
        me = 4 * x + 2 * y + c
        out_ref[me] = v_ref[...]
        peers = [(_flip(x, k >> 2 & 1), _flip(y, k >> 1 & 1), _flip(c, k & 1)) for k in range(1, N_DEV)]

        def copy(k, row, to):
            return pltpu.make_async_remote_copy(src_ref=v_ref, dst_ref=out_ref.at[row], send_sem=send_sems.at[k],
                                                recv_sem=recv_sems.at[k], device_id=to, device_id_type=MESH)

        sends = [copy(k, me, peer) for k, peer in enumerate(peers)]
        for cp in sends:
            cp.start()
        for k, (px, py, pc) in enumerate(peers):
            copy(k, 4 * px + 2 * py + pc, (x, y, c)).wait_recv()
        for cp in sends:
            cp.wait_send()

    vmem = pl.BlockSpec(memory_space=pltpu.VMEM)
    return pl.pallas_call(
        body, name=name, in_specs=[vmem], out_specs=vmem, out_shape=jax.ShapeDtypeStruct((N_DEV, R, L), F32),
        scratch_shapes=[pltpu.SemaphoreType.DMA((N_DEV - 1,)), pltpu.SemaphoreType.DMA((N_DEV - 1,))],
    )(v)


def _silu(v):
    return v * jax.nn.sigmoid(v)


def _ada_fwd(c_all, w_ada, tn=512):
    M, D = c_all.shape
    Ns = w_ada.shape[1]

    def body(c_ref, w_ref, o_ref):
        o_ref[...] = _dot(_silu(c_ref[...]).astype(BF16), w_ref[...].astype(BF16))

    return pl.pallas_call(
        body, name="ada_fwd", grid=(Ns // tn,),
        in_specs=[pl.BlockSpec((M, D), lambda j: (0, 0)), pl.BlockSpec((D, tn), lambda j: (0, j))],
        out_specs=pl.BlockSpec((M, tn), lambda j: (0, j)), out_shape=jax.ShapeDtypeStruct((M, Ns), F32),
        compiler_params=_params(("parallel",)),
    )(c_all, w_ada)


def _ada_bwd(c_all, dmod, w, m, v, tk=256, tn=1536):
    M, D = c_all.shape
    Ns = dmod.shape[1]

    def body(c_ref, d_ref, w_ref, m_ref, v_ref, g_ref, dl_ref, mo_ref, vo_ref):
        g = _dot(_silu(c_ref[...]).astype(BF16), d_ref[...].astype(BF16), TN)
        g_ref[...] = g
        dl_ref[...], mo_ref[...], vo_ref[...] = _adamw_math(g, w_ref[...], m_ref[...], v_ref[...])

    blk = pl.BlockSpec((tk, tn), lambda i, j: (i, j))
    return pl.pallas_call(
        body, name="ada_bwd", grid=(D // tk, Ns // tn),
        in_specs=[pl.BlockSpec((M, tk), lambda i, j: (0, i)), pl.BlockSpec((M, tn), lambda i, j: (0, j)), blk, blk, blk],
        out_specs=[blk] * 4, out_shape=[jax.ShapeDtypeStruct((D, Ns), F32)] * 4,
        compiler_params=_params(("parallel", "parallel"), 40),
    )(c_all, dmod, w, m, v)


def _small_update(g_all, w, m, v):
    R, L = w.shape

    def body(g_ref, w_ref, m_ref, v_ref, go_ref, d_ref, mo_ref, vo_ref):
        g = g_ref[0]
        for d in range(1, N_DEV):
            g = g + g_ref[d]
        go_ref[...] = g
        d_ref[...], mo_ref[...], vo_ref[...] = _adamw_math(g, w_ref[...], m_ref[...], v_ref[...])

    return pl.pallas_call(body, name="small_update", out_shape=[jax.ShapeDtypeStruct((R, L), F32)] * 4)(g_all, w, m, v)


def _pack(parts, rows):
    flat = jnp.concatenate([p.reshape(-1) for p in parts])
    return jnp.pad(flat, (0, rows * 128 - flat.shape[0])).reshape(rows, 128)


def _unpack(packed, shapes):
    flat, out, at = packed.reshape(-1), [], 0
    for shp in shapes:
        size = 1
        for d in shp:
            size *= d
        out.append(flat[at:at + size].reshape(shp))
        at += size
    return out


def _layer(x, tgt, mod, wts, rel_bias, attn_norm_g, lb_logits, gnorm_g, ln1_g, ln1_b, ln2_g, ln2_b, place=None):
    T, D = x.shape
    aw = attn_norm_g.shape[1]
    shift1, scale1, gate1, shift2, scale2, gate2 = [mod[i:i + 1] for i in range(6)]

    def gather(n, rows=None, into=None):
        return None if place is None else _gather_rider(wts[n], rows, None if into is None else into[0])

    def gathered(n, rode):
        return wts[n] if place is None else lax.dynamic_update_index_in_dim(rode[0], wts[n], place[0], 0)

    def blocks(g):
        return g.reshape(N_CHIPS, -1, g.shape[2])

    def to_sibling(g):
        return None if place is None else _pair_rider(g)

    def pair_sum(n, g, rode=None):
        if place is None:
            return g
        rode = _alone(_pair_rider(g), n + "_send_pair") if rode is None else rode
        return _sum_pair(g, rode[0], place[1], n + "_sum_pair")

    def to_chips(p, rows=None, into=None):
        return None if place is None else _chips_rider(p, rows, None if into is None else into[0])

    def summed(n, p, rode):
        return p if place is None else _sum_chips(p, rode[0], place[1], n + "_sum_chips")

    def to_both(block):
        return None if place is None else _share_rider(block)

    def carrying(mm, *args, rider, **kw):
        return mm(*args, rider=rider, **kw) if rider is not None else (mm(*args, **kw), None)

    w_in = gathered("w_in", None if place is None else [wts["w_in_gathered"]])
    h1 = _pre_mixer(x, scale1, shift1)
    n_qkv = 3 * aw // 256
    kh_o, kh_f = wts["w_o"].shape[-2] // 2, wts["w_ffn_in"].shape[-2] // 2
    o_cut, f_cuts = 3 * kh_o // 8, (7 * kh_f // 16, 7 * kh_f // 8)
    qkv, rode = carrying(_mm_nn, h1, w_in, tm=ZPAD, tn=256, tk=D, name="proj_qkv", cols=(0, n_qkv), o_dtype=BF16,
                         pad_rows=ZPAD, rider=gather("w_o", (0, o_cut)))
    proj, rode = carrying(_mm_nn, h1, w_in, tm=2048, tn=256, tk=D, name="proj_rec",
                          cols=(n_qkv, N_CHIPS * w_in.shape[2] // 256), rider=gather("w_o", (o_cut, kh_o - o_cut), rode))
    w_o3 = gathered("w_o", rode).reshape(1, D, D)
    bias = _bias_band(rel_bias)
    (mix_a, probs), rode = _attn_fwd(qkv, bias, attn_norm_g, rider=gather("w_ffn_in", (0, f_cuts[0])))
    (mix_b, o_b, st_all), rode = _hgrn_fwd(
        proj, lb_logits, gnorm_g, rider=gather("w_ffn_in", (f_cuts[0], f_cuts[1] - f_cuts[0]), rode))
    mixin = jnp.concatenate([mix_a, mix_b], axis=1)
    mix = _mm_nn(mixin, w_o3, tm=1024, tn=512, tk=D, name="mix_out")
    if place is None:
        x1, h2 = _post_mixer(mix, x, gate1, ln1_g, ln1_b, scale2, shift2)
    else:
        (x1, h2), rode = _post_mixer(mix, x, gate1, ln1_g, ln1_b, scale2, shift2,
                                     rider=gather("w_ffn_in", (f_cuts[1], kh_f - f_cuts[1]), rode))
    w_ffn_in = gathered("w_ffn_in", rode)
    (gate, up, act), rode = _ffn_in_swiglu(h2, w_ffn_in, tm=2048, tn=256, rider=gather("w_ffn_out"))
    w_out3 = gathered("w_ffn_out", rode)
    w_out3 = w_out3.reshape(1, -1, w_out3.shape[2])
    d_ff = w_out3.shape[1]
    f = _mm_nn(act, w_out3, tm=1024, tn=512, tk=d_ff, name="ffn_out")
    du2, df, acc2 = _loss_head(f, x1, tgt, gate2, ln2_g, ln2_b)
    loss = (0.5 / D) * jnp.sum(acc2[3])
    g = blocks(_mm_tn(act, df, q=1, tk=512, tn=1024, tt=T, name="g_ffn_out"))
    d_gate_up, rode = _d_act_swiglu(df, w_out3, gate, up, tm=1024, to=512, rider=to_sibling(g))
    p_out = pair_sum("w_ffn_out", g, rode)
    dff = jnp.concatenate(d_gate_up, axis=1)
    cut = 21 * p_out.shape[1] // 44
    dh2, rode = carrying(_mm_nt, dff, w_ffn_in, tm=1024, to=1024, tn=w_ffn_in.shape[2], name="d_h2",
                         rider=to_chips(p_out, (0, cut)))
    g, rode = carrying(_mm_tn, h2, dff, q=N_CHIPS, tk=512, tn=w_ffn_in.shape[2] // 2, tt=T, name="g_ffn_in",
                       rider=to_chips(p_out, (cut, p_out.shape[1] - cut), rode))
    g_ffn_out = summed("w_ffn_out", p_out, rode)
    if place is None:
        (du1, dmix, acc1), p_fin = _mid_bwd(dh2, du2, x1, mix, x, gate1, ln1_g, scale2), g
    else:
        (du1, dmix, acc1), rode = _mid_bwd(dh2, du2, x1, mix, x, gate1, ln1_g, scale2,
                                           rider=_join(to_sibling(g), to_both(g_ffn_out)))
        p_fin, g_ffn_out = pair_sum("w_ffn_in", g, rode[:1]), rode[1]
    g = blocks(_mm_tn(mixin, dmix, q=1, tk=512, tn=1024, tt=T, name="g_o"))
    dmixin, rode = carrying(_mm_nt, dmix, w_o3, tm=1024, to=512, tn=D, name="d_mixin", rider=to_sibling(g))
    p_o = pair_sum("w_o", g, rode)
    cut = p_fin.shape[1] // 2
    (dq, dk, dv, dbias, dgain), rode = _attn_bwd(qkv, probs, attn_norm_g, dmixin, rider=to_chips(p_fin, (0, cut)))
    (dzq, dzf, dxi, dzg, dl0, dgn), rode = _hgrn_bwd(
        proj, lb_logits, gnorm_g, o_b, st_all, dmixin,
        rider=_join(to_chips(p_fin, (cut, p_fin.shape[1] - cut), rode), to_chips(p_o)))
    g_ffn_in, g_o = summed("w_ffn_in", p_fin, rode[:1]), summed("w_o", p_o, rode[1:])
    dproj = jnp.concatenate([dq, dk[KPAD:].astype(BF16), dv[KPAD:].astype(BF16), dzq, dzf, dxi, dzg], axis=1)
    g, rode = carrying(_mm_tn, h1, dproj, q=N_CHIPS, tk=512, tn=w_in.shape[2] // 2, tt=T, name="g_in",
                       rider=_join(to_both(g_ffn_in), to_both(g_o)))
    if place is not None:
        g_ffn_in, g_o = rode
    p_in = pair_sum("w_in", g)
    cut = 3 * p_in.shape[1] // 4
    dh1, rode = carrying(_mm_nt, dproj, w_in, tm=1024, to=1024, tn=w_in.shape[2], name="d_h1",
                         rider=to_chips(p_in, (0, cut)))
    if place is None:
        (grad_x, acc0), g_in = _first_bwd(dh1, du1, x, scale1), p_in
    else:
        (grad_x, acc0), rode = _first_bwd(dh1, du1, x, scale1, rider=to_chips(p_in, (cut, p_in.shape[1] - cut), rode))
        g_in, = _alone(to_both(summed("w_in", p_in, rode)), "w_in_share")
    dmod = jnp.concatenate([acc0[1:2], acc0[0:1], acc1[4:5], acc1[1:2], acc1[0:1], acc2[2:3]], axis=0)
    small = dict(rel_bias=_bias_band_grad(dbias), attn_norm_g=dgain,
                 lb_logits=jnp.concatenate([dl0, -dl0], axis=0), gnorm_g=dgn,
                 ln1_g=acc1[2:3], ln1_b=acc1[3:4], ln2_g=acc2[0:1], ln2_b=acc2[1:2])
    return loss, grad_x, dict(w_in=g_in, w_o=g_o, w_ffn_in=g_ffn_in, w_ffn_out=g_ffn_out), dmod, small


SMALL = ("rel_bias", "attn_norm_g", "lb_logits", "gnorm_g", "ln1_g", "ln1_b", "ln2_g", "ln2_b")
SMALL_ROWS = 256


def kernel(x, c, w_ada, b_ada, w_in, rel_bias, attn_norm_g, lb_logits, gnorm_g, w_o, ln1_g, ln1_b, w_ffn_in, w_ffn_out, ln2_g, ln2_b, loss_target, m_w_ada, m_b_ada, m_w_in, m_rel_bias, m_attn_norm_g, m_lb_logits, m_gnorm_g, m_w_o, m_ln1_g, m_ln1_b, m_w_ffn_in, m_w_ffn_out, m_ln2_g, m_ln2_b, v_w_ada, v_b_ada, v_w_in, v_rel_bias, v_attn_norm_g, v_lb_logits, v_gnorm_g, v_w_o, v_ln1_g, v_ln1_b, v_w_ffn_in, v_w_ffn_out, v_ln2_g, v_ln2_b):
    mx, my, mc = _place()
    me = 4 * mx + 2 * my + mc
    chip = 2 * mx + my
    sel = jnp.stack([chip, mc]).astype(jnp.int32)
    D = x.shape[2]
    ns_ada = w_ada.shape[2]

    big = dict(w_in=(w_in, m_w_in, v_w_in), w_o=(w_o, m_w_o, v_w_o), w_ffn_in=(w_ffn_in, m_w_ffn_in, v_w_ffn_in),
               w_ffn_out=(w_ffn_out, m_w_ffn_out, v_w_ffn_out))
    shards = dict(w_in=w_in[0].astype(BF16))
    kh, rode, at = shards["w_in"].shape[0] // 2, None, 0
    for n, part in (("w_ffn_in", 19), ("w_ffn_out", 9), ("w_o", 4)):
        rows = (at, part * kh // 32)
        (shards[n],), rode = _to_bf16(big[n][0][0], "cast_" + n,
                                      _gather_rider(shards["w_in"], rows, None if rode is None else rode[0]))
        at += rows[1]
    shards["w_in_gathered"] = rode[0]

    c_all = _gather_small(c.reshape(D // 128, 128), "gather_c").reshape(N_DEV, D)
    c_all = jnp.pad(c_all, ((0, 16 - N_DEV), (0, 0)))
    mod_cols = _ada_fwd(c_all, w_ada[0])[:N_DEV]
    mod_all = _gather_small(mod_cols.reshape(-1, 128), "gather_mod").reshape(N_DEV, N_DEV, ns_ada)
    mod = lax.dynamic_index_in_dim(mod_all[::2], me, axis=1, keepdims=False)
    mod = (mod.reshape(1, -1) + b_ada).reshape(6, D)

    loss, grad_x, g_big, dmod, g_small = _layer(
        x[0], loss_target[0], mod, shards, rel_bias[0], attn_norm_g, lb_logits, gnorm_g, ln1_g, ln1_b, ln2_g, ln2_b,
        place=(chip, sel))

    grads, deltas, new_m, new_v = {}, {}, {}, {}
    for n, (w, m, v) in big.items():
        g, d, mo, vo = _adamw(g_big[n], w[0], m[0], v[0], "adamw_" + n)
        grads[n], deltas[n], new_m[n], new_v[n] = g[None], d[None], mo[None], vo[None]

    small_in = dict(rel_bias=(rel_bias, m_rel_bias, v_rel_bias), attn_norm_g=(attn_norm_g, m_attn_norm_g, v_attn_norm_g),
                    lb_logits=(lb_logits, m_lb_logits, v_lb_logits), gnorm_g=(gnorm_g, m_gnorm_g, v_gnorm_g),
                    ln1_g=(ln1_g, m_ln1_g, v_ln1_g), ln1_b=(ln1_b, m_ln1_b, v_ln1_b), ln2_g=(ln2_g, m_ln2_g, v_ln2_g),
                    ln2_b=(ln2_b, m_ln2_b, v_ln2_b))
    g_all = _gather_small(_pack([dmod] + [g_small[n] for n in SMALL] + [loss], SMALL_ROWS), "gather_small")
    packed = [_pack([t] + [small_in[n][i] for n in SMALL] + [jnp.zeros((), F32)], SMALL_ROWS)
              for i, t in enumerate((b_ada, m_b_ada, v_b_ada))]
    shapes = [b_ada.shape] + [small_in[n][0].shape for n in SMALL] + [()]
    outs = [_unpack(o, shapes) for o in _small_update(g_all, *packed)]
    loss = outs[0][-1]
    for i, n in enumerate(("b_ada",) + SMALL):
        grads[n], deltas[n], new_m[n], new_v[n] = outs[0][i], outs[1][i], outs[2][i], outs[3][i]

    dmod_all = g_all[:, :6 * D // 128].reshape(N_DEV, 6 * D)
    dmod_cols = lax.dynamic_slice_in_dim(dmod_all, chip * ns_ada, ns_ada, axis=1)
    dmod_cols = jnp.pad(dmod_cols, ((0, 16 - N_DEV), (0, 0)))
    g, d, mo, vo = _ada_bwd(c_all, dmod_cols, w_ada[0], m_w_ada[0], v_w_ada[0])
    grads["w_ada"], deltas["w_ada"], new_m["w_ada"], new_v["w_ada"] = g[None], d[None], mo[None], vo[None]

    order = ("w_ada", "b_ada", "w_in", "rel_bias", "attn_norm_g", "lb_logits", "gnorm_g", "w_o", "ln1_g", "ln1_b",
             "w_ffn_in", "w_ffn_out", "ln2_g", "ln2_b")
    return (loss, grad_x[None], *[grads[n] for n in order], *[deltas[n] for n in order],
            *[new_m[n] for n in order], *[new_v[n] for n in order])
```

```python
import numpy as np
import jax
import jax.numpy as jnp
from jax import lax
from jax.experimental import pallas as pl
from jax.experimental.pallas import tpu as pltpu

F32 = jnp.float32
BF16 = jnp.bfloat16
MESH = pl.DeviceIdType.MESH
HIGHEST = lax.Precision.HIGHEST

CHUNK = 64
N_PAST = 8
QG = 4
QROWS = QG * CHUNK
KPAD = N_PAST * CHUNK
ZPAD = 2 * KPAD
UNION = (QG + N_PAST) * CHUNK
BAND = (N_PAST + 1) * CHUNK
HD_A = 64
HD_B = 128
SUB = 16
HGRN_HEADS = 8
MAX_REL = 256
EPS = 1e-5
ALPHA = 2.0 ** 0.25
LR, B1, B2, ADAM_EPS, WD, STEP = 1e-3, 0.9, 0.999, 1e-8, 0.01, 10
N_CHIPS = 4
N_DEV = 8
NEG = -1e30
TILE_BYTES = 3 << 19

NN = ((1,), (0,))
NT = ((1,), (1,))
TN = ((0,), (0,))


def _dot(a, b, dims=NN, precision=None):
    return lax.dot_general(a, b, (dims, ((), ())), preferred_element_type=F32, precision=precision)


def _params(sem=None, vmem_mb=None, **kw):
    return pltpu.CompilerParams(dimension_semantics=sem,
                                vmem_limit_bytes=None if vmem_mb is None else vmem_mb << 20, **kw)


def _row_tile(rows, cols):
    for cand in (512, 256, 128, 64, 32, 16, 8):
        if rows % cand == 0 and cand * cols * 4 <= TILE_BYTES:
            return cand
    raise ValueError((rows, cols))


def _place():
    return lax.axis_index("x"), lax.axis_index("y"), lax.axis_index("c")


def _flip(v, bit):
    return 1 - v if bit else v


ANY = pl.BlockSpec(memory_space=pl.ANY)
CHIP_FLIPS = ((1, 0), (0, 1), (1, 1))


class _Rider:
    def __init__(self, operands, out_shape, n_sems, start, finish, aliases=None):
        self.operands, self.out_shape, self.n_sems, self.start, self.finish = operands, out_shape, n_sems, start, finish
        self.aliases = aliases or {}


def _call(body, rider, *, name, grid, in_specs, out_specs, out_shape, scratch_shapes, compiler_params, operands):
    if rider is None:
        outs = pl.pallas_call(body, name=name, grid=grid, in_specs=in_specs, out_specs=out_specs, out_shape=out_shape,
                              scratch_shapes=scratch_shapes, compiler_params=compiler_params)(*operands)
        return list(outs), []
    n_in, n_out, n_sc = len(in_specs), len(out_specs), len(scratch_shapes)
    r_in, r_out = len(rider.operands), len(rider.out_shape)

    def carried(*refs):
        refs = list(refs)
        cuts = [n_in, r_in, n_out, r_out, n_sc]
        ins, r_ins, outs, r_outs, scratch = [[refs.pop(0) for _ in range(n)] for n in cuts]
        first, last = None, None
        for axis, size in enumerate(grid):
            i = pl.program_id(axis)
            first = (i == 0) if first is None else first & (i == 0)
            last = (i == size - 1) if last is None else last & (i == size - 1)

        @pl.when(first)
        def _():
            rider.start(r_ins, r_outs, *refs)

        body(*ins, *outs, *scratch)

        @pl.when(last)
        def _():
            rider.finish(r_ins, r_outs, *refs)

    sems = [pltpu.SemaphoreType.DMA((rider.n_sems,)), pltpu.SemaphoreType.DMA((rider.n_sems,))]
    outs = pl.pallas_call(carried, name=name, grid=grid, in_specs=list(in_specs) + [ANY] * r_in,
                          out_specs=list(out_specs) + [ANY] * r_out, out_shape=list(out_shape) + rider.out_shape,
                          scratch_shapes=list(scratch_shapes) + sems, compiler_params=compiler_params,
                          input_output_aliases={n_in + i: n_out + o for i, o in rider.aliases.items()},
                          )(*operands, *rider.operands)
    return list(outs[:n_out]), list(outs[n_out:])


def _alone(rider, name):
    def body(*refs):
        ins, outs, sems = refs[:len(rider.operands)], refs[len(rider.operands):-2], refs[-2:]
        rider.start(ins, outs, *sems)
        rider.finish(ins, outs, *sems)

    return pl.pallas_call(
        body, name=name, in_specs=[ANY] * len(rider.operands), out_specs=[ANY] * len(rider.out_shape),
        out_shape=rider.out_shape, input_output_aliases=rider.aliases,
        scratch_shapes=[pltpu.SemaphoreType.DMA((rider.n_sems,)), pltpu.SemaphoreType.DMA((rider.n_sems,))],
    )(*rider.operands)


class _Sems:
    def __init__(self, sems, base):
        self.sems, self.base = sems, base

    @property
    def at(self):
        return self

    def __getitem__(self, k):
        return self.sems.at[self.base + k]


def _join(*riders):
    riders = [r for r in riders if r is not None]
    if len(riders) < 2:
        return riders[0] if riders else None

    def parts(ins, outs, send_sems, recv_sems):
        i = o = s = 0
        for r in riders:
            ni, no = len(r.operands), len(r.out_shape)
            yield r, ins[i:i + ni], outs[o:o + no], _Sems(send_sems, s), _Sems(recv_sems, s)
            i, o, s = i + ni, o + no, s + r.n_sems

    def start(*refs):
        for r, *args in parts(*refs):
            r.start(*args)

    def finish(*refs):
        for r, *args in parts(*refs):
            r.finish(*args)

    aliases, i, o = {}, 0, 0
    for r in riders:
        aliases.update({i + a: o + b for a, b in r.aliases.items()})
        i, o = i + len(r.operands), o + len(r.out_shape)
    return _Rider([a for r in riders for a in r.operands], [s for r in riders for s in r.out_shape],
                  sum(r.n_sems for r in riders), start, finish, aliases)


def _gather_rider(shard, rows=None, into=None):
    K, Ns = shard.shape
    kh = K // 2
    first_row, n_rows = rows or (0, kh)

    def copies(w_ref, out_ref, send_sems, recv_sems):
        x, y, c = _place()
        chips = [(_flip(x, fx), _flip(y, fy)) for fx, fy in CHIP_FLIPS]

        def half(chip, which):
            return out_ref.at[2 * chip[0] + chip[1], pl.ds(which * kh + first_row, n_rows), :]

        def copy(k, dst, to, src=None):
            return pltpu.make_async_remote_copy(src_ref=dst if src is None else src, dst_ref=dst,
                                                send_sem=send_sems.at[k], recv_sem=recv_sems.at[k],
                                                device_id=to, device_id_type=MESH)

        def first():
            return [copy(j, half((x, y), c), (*chip, c), src=w_ref.at[pl.ds(c * kh + first_row, n_rows), :])
                    for j, chip in enumerate(chips)]

        def onward():
            return [copy(3 + j, half(chip, c), (x, y, 1 - c)) for j, chip in enumerate(chips)]

        def arriving(base, which):
            return [copy(base + j, half(chip, which), (x, y, c)) for j, chip in enumerate(chips)]

        return first, onward, arriving

    def start(ins, outs, send_sems, recv_sems):
        for cp in copies(ins[0], outs[0], send_sems, recv_sems)[0]():
            cp.start()

    def finish(ins, outs, send_sems, recv_sems):
        x, y, c = _place()
        first, onward, arriving = copies(ins[0], outs[0], send_sems, recv_sems)
        passed = onward()
        for arrived, cp in zip(arriving(0, c), passed):
            arrived.wait_recv()
            cp.start()
        for arrived in arriving(3, 1 - c):
            arrived.wait_recv()
        for cp in first() + passed:
            cp.wait_send()

    full = jax.ShapeDtypeStruct((N_CHIPS, K, Ns), shard.dtype)
    if into is None:
        return _Rider([shard], [full], 6, start, finish)
    return _Rider([shard, into], [full], 6, start, finish, aliases={1: 0})


def _pair_rider(g_full):
    Q, K, Ns = g_full.shape
    kh = K // 2

    def copy(g_ref, got_ref, send_sems, recv_sems):
        x, y, c = _place()
        return pltpu.make_async_remote_copy(src_ref=g_ref.at[:, pl.ds((1 - c) * kh, kh), :], dst_ref=got_ref,
                                            send_sem=send_sems.at[0], recv_sem=recv_sems.at[0],
                                            device_id=(x, y, 1 - c), device_id_type=MESH)

    def start(ins, outs, send_sems, recv_sems):
        copy(ins[0], outs[0], send_sems, recv_sems).start()

    def finish(ins, outs, send_sems, recv_sems):
        copy(ins[0], outs[0], send_sems, recv_sems).wait()

    return _Rider([g_full], [jax.ShapeDtypeStruct((Q, kh, Ns), g_full.dtype)], 1, start, finish)


def _acts_rider(a, b):
    T, K = a.shape
    kh = K // 2

    def copies(ins, outs, send_sems, recv_sems):
        x, y, c = _place()
        pair = [(ins[0].at[:, pl.ds((1 - c) * kh, kh)], outs[0]), (ins[1], outs[1])]
        return [pltpu.make_async_remote_copy(src_ref=src, dst_ref=dst, send_sem=send_sems.at[k], recv_sem=recv_sems.at[k],
                                             device_id=(x, y, 1 - c), device_id_type=MESH)
                for k, (src, dst) in enumerate(pair)]

    def start(*refs):
        for cp in copies(*refs):
            cp.start()

    def finish(*refs):
        for cp in copies(*refs):
            cp.wait()

    return _Rider([a, b], [jax.ShapeDtypeStruct((T, kh), a.dtype), jax.ShapeDtypeStruct(b.shape, b.dtype)], 2,
                  start, finish)


def _share_rider(block):
    K, Ns = block.shape
    kh = K // 2

    def halves(out_ref):
        x, y, c = _place()
        return out_ref.at[pl.ds(c * kh, kh), :], out_ref.at[pl.ds((1 - c) * kh, kh), :], (x, y, 1 - c)

    def start(ins, outs, send_sems, recv_sems):
        mine, _, sibling = halves(outs[0])
        pltpu.make_async_remote_copy(src_ref=mine, dst_ref=mine, send_sem=send_sems.at[0], recv_sem=recv_sems.at[0],
                                     device_id=sibling, device_id_type=MESH).start()

    def finish(ins, outs, send_sems, recv_sems):
        mine, theirs, sibling = halves(outs[0])
        pltpu.make_async_remote_copy(src_ref=theirs, dst_ref=theirs, send_sem=send_sems.at[0], recv_sem=recv_sems.at[0],
                                     device_id=sibling, device_id_type=MESH).wait_recv()
        pltpu.make_async_remote_copy(src_ref=mine, dst_ref=mine, send_sem=send_sems.at[0], recv_sem=recv_sems.at[0],
                                     device_id=sibling, device_id_type=MESH).wait_send()

    return _Rider([block], [jax.ShapeDtypeStruct((K, Ns), block.dtype)], 1, start, finish, aliases={0: 0})


def _chips_rider(pair_sum, rows=None, into=None):
    Q, kh, Ns = pair_sum.shape
    first_row, n_rows = rows or (0, kh)

    def copies(p_ref, got_ref, send_sems, recv_sems):
        x, y, c = _place()
        part = pl.ds(first_row, n_rows)
        out = []
        for j, (fx, fy) in enumerate(CHIP_FLIPS):
            px, py = _flip(x, fx), _flip(y, fy)
            out.append(pltpu.make_async_remote_copy(
                src_ref=p_ref.at[2 * px + py, part, :], dst_ref=got_ref.at[j, part, :], send_sem=send_sems.at[j],
                recv_sem=recv_sems.at[j], device_id=(px, py, c), device_id_type=MESH))
        return out

    def start(ins, outs, send_sems, recv_sems):
        for cp in copies(ins[0], outs[0], send_sems, recv_sems):
            cp.start()

    def finish(ins, outs, send_sems, recv_sems):
        sends = copies(ins[0], outs[0], send_sems, recv_sems)
        for cp in sends:
            cp.wait_recv()
        for cp in sends:
            cp.wait_send()

    got = jax.ShapeDtypeStruct((Q - 1, kh, Ns), pair_sum.dtype)
    if into is None:
        return _Rider([pair_sum], [got], 3, start, finish)
    return _Rider([pair_sum, into], [got], 3, start, finish, aliases={1: 0})


def _mm(a, b, *, grid, a_spec, b_spec, o_spec, o_shape, o_dtype, dims, acc_shape, name, rider=None, zero_rows=0,
        vmem_mb=48):
    nk = grid[2]

    def body(a_ref, b_ref, o_ref, *scratch):
        if zero_rows:
            @pl.when(pl.program_id(0) < zero_rows)
            def _():
                o_ref[...] = jnp.zeros_like(o_ref)

            @pl.when(pl.program_id(0) >= zero_rows)
            def _():
                o_ref[...] = _dot(a_ref[...], b_ref[...], dims).astype(o_ref.dtype)
            return
        part = _dot(a_ref[...], b_ref[...], dims)
        if nk == 1:
            o_ref[...] = part.astype(o_ref.dtype)
            return
        acc_ref, = scratch
        k = pl.program_id(2)

        @pl.when(k == 0)
        def _():
            acc_ref[...] = part

        @pl.when(k > 0)
        def _():
            acc_ref[...] += part

        @pl.when(k == nk - 1)
        def _():
            o_ref[...] = acc_ref[...].astype(o_ref.dtype)

    (out,), rode = _call(
        body, rider, name=name, grid=grid, in_specs=[a_spec, b_spec], out_specs=[o_spec],
        out_shape=[jax.ShapeDtypeStruct(o_shape, o_dtype)],
        scratch_shapes=[] if nk == 1 else [pltpu.VMEM(acc_shape, F32)],
        compiler_params=_params(("parallel", "parallel", "arbitrary") if rider is None else ("arbitrary",) * 3, vmem_mb),
        operands=(a, b))
    return out if rider is None else (out, rode)


def _mm_nn(a, w, *, tm, tn, tk, name, rider=None, cols=None, o_dtype=F32, pad_rows=0):
    T, K = a.shape
    Q, _, Ns = w.shape
    nbs = Ns // tn
    tm = min(tm, T)
    j0, j1 = cols or (0, Q * nbs)
    lead = pad_rows // tm
    return _mm(a, w, grid=(lead + T // tm, j1 - j0, K // tk),
               a_spec=pl.BlockSpec((tm, tk), lambda i, j, k: (jnp.maximum(i - lead, 0), k)),
               b_spec=pl.BlockSpec((None, tk, tn), lambda i, j, k: ((j + j0) // nbs, k, (j + j0) % nbs)),
               o_spec=pl.BlockSpec((tm, tn), lambda i, j, k: (i, j)),
               o_shape=(pad_rows + T, (j1 - j0) * tn), o_dtype=o_dtype, dims=NN, acc_shape=(tm, tn), name=name,
               rider=rider, zero_rows=lead)


def _mm_nt(g, w, *, tm, to, tn, name, rider=None):
    T = g.shape[0]
    Q, K, Ns = w.shape
    nbs = Ns // tn
    tm = min(tm, T)
    return _mm(g, w, grid=(T // tm, K // to, Q * nbs),
               a_spec=pl.BlockSpec((tm, tn), lambda i, j, n: (i, n)),
               b_spec=pl.BlockSpec((None, to, tn), lambda i, j, n: (n // nbs, j, n % nbs)),
               o_spec=pl.BlockSpec((tm, to), lambda i, j, n: (i, j)),
               o_shape=(T, K), o_dtype=F32, dims=NT, acc_shape=(tm, to), name=name, rider=rider)


def _mm_tn(a, g, *, q, tk, tn, tt, name, rider=None):
    T, K = a.shape
    Ns = g.shape[1] // q
    nbs = Ns // tn
    return _mm(a, g, grid=(K // tk, q * nbs, T // tt),
               a_spec=pl.BlockSpec((tt, tk), lambda i, j, t: (t, i)),
               b_spec=pl.BlockSpec((tt, tn), lambda i, j, t: (t, j)),
               o_spec=pl.BlockSpec((None, tk, tn), lambda i, j, t: (j // nbs, i, j % nbs)),
               o_shape=(q, K, Ns), o_dtype=BF16, dims=TN, acc_shape=(tk, tn), name=name, rider=rider)


def _mm_tn_add(a, g, part, *, tk, tn, name):
    T, K = a.shape
    Q, _, Ns = part.shape
    nbs = Ns // tn

    def body(a_ref, g_ref, p_ref, o_ref):
        o_ref[...] = (_dot(a_ref[...], g_ref[...], TN) + p_ref[...].astype(F32)).astype(o_ref.dtype)

    blk = pl.BlockSpec((None, tk, tn), lambda i, j: (j // nbs, i, j % nbs))
    return pl.pallas_call(
        body, name=name, grid=(K // tk, Q * nbs),
        in_specs=[pl.BlockSpec((T, tk), lambda i, j: (0, i)), pl.BlockSpec((T, tn), lambda i, j: (0, j)), blk],
        out_specs=blk, out_shape=jax.ShapeDtypeStruct((Q, K, Ns), BF16),
        compiler_params=_params(("parallel", "parallel"), 48),
    )(a, g, part)


def _ln(u):
    mu = jnp.mean(u, axis=-1, keepdims=True)
    d = u - mu
    r = lax.rsqrt(jnp.mean(d * d, axis=-1, keepdims=True) + EPS)
    return d * r, r


def _ln_bwd(dy, un, r):
    return r * (dy - jnp.mean(dy, axis=-1, keepdims=True) - un * jnp.mean(dy * un, axis=-1, keepdims=True))


def _colsum(v):
    return jnp.sum(v, axis=0, keepdims=True)


def _rowwise(name, fn, bigs, vecs, out_dtypes, n_acc, tm=128, rider=None):
    T, D = bigs[0].shape
    nb, nv, no = len(bigs), len(vecs), len(out_dtypes)

    def body(*refs):
        outs, accs = fn([r[...] for r in refs[:nb]], [r[...] for r in refs[nb:nb + nv]])
        for r, o in zip(refs[nb + nv:nb + nv + no], outs):
            r[...] = o.astype(r.dtype)
        if n_acc:
            acc_ref = refs[nb + nv + no]

            @pl.when(pl.program_id(0) == 0)
            def _():
                acc_ref[...] = jnp.zeros_like(acc_ref)

            for row, a in enumerate(accs):
                acc_ref[row:row + 1, :] += a

    big_spec = pl.BlockSpec((tm, D), lambda i: (i, 0))
    vec_spec = pl.BlockSpec((1, D), lambda i: (0, 0))
    out_shape = [jax.ShapeDtypeStruct((T, D), dt) for dt in out_dtypes]
    out_specs = [big_spec] * no
    if n_acc:
        out_shape.append(jax.ShapeDtypeStruct((8, D), F32))
        out_specs.append(pl.BlockSpec((8, D), lambda i: (0, 0)))
    outs, rode = _call(
        body, rider, name=name, grid=(T // tm,), in_specs=[big_spec] * nb + [vec_spec] * nv,
        out_specs=out_specs, out_shape=out_shape, scratch_shapes=[],
        compiler_params=_params(("arbitrary",), 48), operands=(*bigs, *vecs))
    return outs if rider is None else (outs, rode)


def _to_bf16(w, name, rider=None):
    R, C = w.shape
    tr = _row_tile(R, C)

    def body(w_ref, o_ref):
        o_ref[...] = w_ref[...].astype(o_ref.dtype)

    blk = pl.BlockSpec((tr, C), lambda i: (i, 0))
    return _call(body, rider, name=name, grid=(R // tr,), in_specs=[blk], out_specs=[blk],
                 out_shape=[jax.ShapeDtypeStruct((R, C), BF16)], scratch_shapes=[],
                 compiler_params=_params(("arbitrary",)), operands=(w,))


def _pre_mixer(x, scale1, shift1):
    def fn(b, v):
        xn, _ = _ln(b[0])
        return [xn * (1.0 + v[0]) + v[1]], []
    return _rowwise("pre_mixer", fn, [x], [scale1, shift1], [BF16], 0)[0]


def _post_mixer(mix, x, gate1, g1, b1, scale2, shift2, rider=None):
    def fn(b, v):
        un1, _ = _ln(ALPHA * b[1] + v[0] * b[0])
        x1 = un1 * v[1] + v[2]
        xn1, _ = _ln(x1)
        return [x1, xn1 * (1.0 + v[3]) + v[4]], []
    return _rowwise("post_mixer", fn, [mix, x], [gate1, g1, b1, scale2, shift2], [F32, BF16], 0, rider=rider)


def _loss_head(f, x1, tgt, gate2, g2, b2):
    def fn(b, v):
        ff, xx, tt = b
        d_model = ff.shape[-1]
        un2, r2 = _ln(ALPHA * xx + v[0] * ff)
        err = un2 * v[1] + v[2] - tt
        dy = err * (1.0 / d_model)
        du2 = _ln_bwd(dy * v[1], un2, r2)
        return [du2, du2 * v[0]], [_colsum(dy * un2), _colsum(dy), _colsum(du2 * ff), _colsum(err * err)]
    return _rowwise("loss_head", fn, [f, x1, tgt], [gate2, g2, b2], [F32, BF16], 4)


def _mid_bwd(dh2, du2, x1, mix, x, gate1, g1, scale2, rider=None):
    def fn(b, v):
        dh, du, xx1, mm, xx = b
        xn1, r1n = _ln(xx1)
        dx1 = ALPHA * du + _ln_bwd(dh * (1.0 + v[2]), xn1, r1n)
        un1, r1 = _ln(ALPHA * xx + v[0] * mm)
        du1 = _ln_bwd(dx1 * v[1], un1, r1)
        return [du1, du1 * v[0]], [_colsum(dh * xn1), _colsum(dh), _colsum(dx1 * un1), _colsum(dx1),
                                   _colsum(du1 * mm)]
    return _rowwise("mid_bwd", fn, [dh2, du2, x1, mix, x], [gate1, g1, scale2], [F32, BF16], 5, rider=rider)


def _first_bwd(dh1, du1, x, scale1, rider=None):
    def fn(b, v):
        dh, du, xx = b
        xn, r0 = _ln(xx)
        return [ALPHA * du + _ln_bwd(dh * (1.0 + v[0]), xn, r0)], [_colsum(dh * xn), _colsum(dh)]
    return _rowwise("first_bwd", fn, [dh1, du1, x], [scale1], [F32], 2, rider=rider)


def _ffn_in_swiglu(h2, w, *, tm, tn, rider=None):
    T, K = h2.shape
    Q, _, Ns = w.shape
    nbs = Ns // tn
    half = Q * nbs // 2
    tm = min(tm, T)

    def body(a_ref, wg_ref, wu_ref, g_ref, u_ref, act_ref):
        a = a_ref[...]
        g, u = _dot(a, wg_ref[...]), _dot(a, wu_ref[...])
        g_ref[...] = g.astype(g_ref.dtype)
        u_ref[...] = u.astype(u_ref.dtype)
        act_ref[...] = (g * jax.nn.sigmoid(g) * u).astype(act_ref.dtype)

    cols = lambda first: pl.BlockSpec((None, K, tn), lambda i, j: ((j + first) // nbs, 0, (j + first) % nbs))
    blk = pl.BlockSpec((tm, tn), lambda i, j: (i, j))
    return _call(
        body, rider, name="ffn_in", grid=(T // tm, half),
        in_specs=[pl.BlockSpec((tm, K), lambda i, j: (i, 0)), cols(0), cols(half)], out_specs=[blk] * 3,
        out_shape=[jax.ShapeDtypeStruct((T, half * tn), BF16)] * 3, scratch_shapes=[],
        compiler_params=_params(("arbitrary", "arbitrary"), 48), operands=(h2, w, w))


def _d_act_swiglu(df, w, gate, up, *, tm, to, rider=None):
    T, N = df.shape
    F = w.shape[1]
    tm = min(tm, T)

    def body(df_ref, w_ref, g_ref, u_ref, dg_ref, du_ref):
        d = _dot(df_ref[...], w_ref[...], NT)
        g = g_ref[...].astype(F32)
        s = jax.nn.sigmoid(g)
        du_ref[...] = (d * g * s).astype(du_ref.dtype)
        dg_ref[...] = (d * u_ref[...].astype(F32) * s * (1.0 + g * (1.0 - s))).astype(dg_ref.dtype)

    blk = pl.BlockSpec((tm, to), lambda i, j: (i, j))
    return _call(
        body, rider, name="d_act", grid=(T // tm, F // to),
        in_specs=[pl.BlockSpec((tm, N), lambda i, j: (i, 0)), pl.BlockSpec((None, to, N), lambda i, j: (0, j, 0)), blk, blk],
        out_specs=[blk, blk], out_shape=[jax.ShapeDtypeStruct((T, F), BF16)] * 2, scratch_shapes=[],
        compiler_params=_params(("arbitrary", "arbitrary"), 48), operands=(df, w, gate, up))


PAIR = 2


def _fill_table(table_ref, band_ref):
    table_ref[...] = jnp.full(table_ref.shape, NEG, F32)
    for e in range(PAIR):
        for g in range(QG):
            table_ref[e, g * CHUNK:(g + 1) * CHUNK, g * CHUNK:g * CHUNK + BAND] = band_ref[e]


def _attn_probs(q_ref, k_ref, bias_ref, e, step):
    start = pl.multiple_of(step * QROWS, QROWS)
    lanes = pl.ds(e * HD_A, HD_A)
    s = _dot(q_ref[:, lanes], k_ref[pl.ds(start + ZPAD - KPAD, UNION), lanes], NT) * (HD_A ** -0.5) + bias_ref[e]
    col = lax.broadcasted_iota(jnp.int32, s.shape, 1)
    s = jnp.where(col + start >= KPAD, s, NEG)
    p = jnp.exp(s - jnp.max(s, axis=-1, keepdims=True))
    return p / jnp.sum(p, axis=-1, keepdims=True), start


def _attn_specs(T, n_pairs):
    wide = PAIR * HD_A
    per_step = pl.BlockSpec((QROWS, wide), lambda hp, n: (n, hp))
    queries = pl.BlockSpec((QROWS, wide), lambda hp, n: (n + ZPAD // QROWS, hp))
    keys = pl.BlockSpec((ZPAD + T, wide), lambda hp, n: (0, n_pairs + hp))
    values = pl.BlockSpec((ZPAD + T, wide), lambda hp, n: (0, 2 * n_pairs + hp))
    grads = pl.BlockSpec((KPAD + T, wide), lambda hp, n: (0, hp))
    table = pl.BlockSpec((PAIR, CHUNK, BAND), lambda hp, n: (hp, 0, 0))
    vec = pl.BlockSpec((1, wide), lambda hp, n: (0, hp))
    return per_step, queries, keys, values, grads, table, vec


def _probs_spec():
    return pl.BlockSpec((PAIR, QROWS, UNION), lambda hp, n: (hp, n, 0))


def _attn_fwd(qkv, bias, gain, rider=None):
    T = qkv.shape[0] - ZPAD
    W = gain.shape[1]
    n_pairs = W // (PAIR * HD_A)

    def body(q_ref, k_ref, v_ref, band_ref, gain_ref, o_ref, p_ref, table_ref):
        @pl.when(pl.program_id(1) == 0)
        def _():
            _fill_table(table_ref, band_ref)

        for e in range(PAIR):
            lanes = pl.ds(e * HD_A, HD_A)
            p, start = _attn_probs(q_ref, k_ref, table_ref, e, pl.program_id(1))
            p_ref[e] = p.astype(p_ref.dtype)
            o = _dot(p_ref[e], v_ref[pl.ds(start + ZPAD - KPAD, UNION), lanes])
            rr = lax.rsqrt(jnp.mean(o * o, axis=-1, keepdims=True) + EPS)
            o_ref[:, lanes] = (o * rr * gain_ref[:, lanes]).astype(o_ref.dtype)

    per_step, queries, keys, values, _, table, vec = _attn_specs(T, n_pairs)
    return _call(
        body, rider, name="attn_fwd", grid=(n_pairs, T // QROWS), in_specs=[queries, keys, values, table, vec],
        out_specs=[per_step, _probs_spec()],
        out_shape=[jax.ShapeDtypeStruct((T, W), BF16), jax.ShapeDtypeStruct((n_pairs * PAIR, T, UNION), BF16)],
        scratch_shapes=[pltpu.VMEM((PAIR, QROWS, UNION), F32)],
        compiler_params=_params(("arbitrary", "arbitrary"), 40), operands=(qkv, qkv, qkv, bias, gain))


def _attn_bwd(qkv, probs, gain, dmixin, rider=None):
    T = qkv.shape[0] - ZPAD
    W = gain.shape[1]
    n_pairs = W // (PAIR * HD_A)
    scale = HD_A ** -0.5

    def body(q_ref, k_ref, v_ref, p_ref, gain_ref, don_ref, dq_ref, dk_ref, dv_ref, dband_ref, dgain_ref, dtable_ref):
        n = pl.program_id(1)

        @pl.when(n == 0)
        def _():
            dk_ref[...] = jnp.zeros_like(dk_ref)
            dv_ref[...] = jnp.zeros_like(dv_ref)
            dtable_ref[...] = jnp.zeros_like(dtable_ref)
            dgain_ref[...] = jnp.zeros_like(dgain_ref)

        for e in range(PAIR):
            lanes = pl.ds(e * HD_A, HD_A)
            start = pl.multiple_of(n * QROWS, QROWS)
            keys, in_qkv = pl.ds(start, UNION), pl.ds(start + ZPAD - KPAD, UNION)
            pb = p_ref[e]
            p = pb.astype(F32)
            vb = v_ref[in_qkv, lanes]
            o = _dot(pb, vb)
            rr = lax.rsqrt(jnp.mean(o * o, axis=-1, keepdims=True) + EPS)
            on = o * rr
            d_on = don_ref[:, lanes]
            dgain_ref[:, lanes] += _colsum(d_on * on)
            dyo = d_on * gain_ref[:, lanes]
            do = rr * (dyo - on * jnp.mean(dyo * on, axis=-1, keepdims=True))
            dob = do.astype(BF16)
            dp = _dot(dob, vb, NT)
            ds = p * (dp - jnp.sum(do * o, axis=-1, keepdims=True))
            dtable_ref[e] += ds
            dsb = ds.astype(BF16)
            dq_ref[:, lanes] = (_dot(dsb, k_ref[in_qkv, lanes]) * scale).astype(dq_ref.dtype)
            dk_ref[keys, lanes] += _dot(dsb, q_ref[:, lanes], TN) * scale
            dv_ref[keys, lanes] += _dot(pb, dob, TN)

        @pl.when(n == T // QROWS - 1)
        def _():
            for e in range(PAIR):
                dband_ref[e] = sum(dtable_ref[e, g * CHUNK:(g + 1) * CHUNK, g * CHUNK:g * CHUNK + BAND]
                                   for g in range(QG))

    per_step, queries, keys, values, grads, table, vec = _attn_specs(T, n_pairs)
    H = n_pairs * PAIR
    return _call(
        body, rider, name="attn_bwd", grid=(n_pairs, T // QROWS),
        in_specs=[queries, keys, values, _probs_spec(), vec, per_step],
        out_specs=[per_step, grads, grads, table, vec],
        out_shape=[jax.ShapeDtypeStruct((T, W), BF16), jax.ShapeDtypeStruct((KPAD + T, W), F32),
                   jax.ShapeDtypeStruct((KPAD + T, W), F32), jax.ShapeDtypeStruct((H, CHUNK, BAND), F32),
                   jax.ShapeDtypeStruct((1, W), F32)],
        scratch_shapes=[pltpu.VMEM((PAIR, QROWS, UNION), F32)],
        compiler_params=_params(("arbitrary", "arbitrary"), 40),
        operands=(qkv, qkv, qkv, probs, gain, dmixin))


N_DIAG = CHUNK + BAND - 1


def _bias_band(rel_bias):
    H = rel_bias.shape[0]
    idx = np.clip(BAND - 1 - np.arange(N_DIAG), -MAX_REL, MAX_REL) + MAX_REL
    rolled = rel_bias[:, idx[(np.arange(N_DIAG) + CHUNK - 1) % N_DIAG]]
    flat = jnp.broadcast_to(rolled[:, None, :], (H, CHUNK, N_DIAG)).reshape(H, CHUNK * N_DIAG)
    return flat[:, :CHUNK * (N_DIAG - 1)].reshape(H, CHUNK, N_DIAG - 1)[:, :, :BAND]


def _bias_band_grad(dband):
    H = dband.shape[0]
    skew = jnp.pad(dband, ((0, 0), (0, 0), (CHUNK - 1, 0))).reshape(H, CHUNK * N_DIAG)
    skew = jnp.pad(skew, ((0, 0), (0, CHUNK))).reshape(H, CHUNK, N_DIAG + 1)
    diag = jnp.sum(skew, axis=1)[:, :N_DIAG]
    n_far = BAND - MAX_REL
    far = jnp.sum(diag[:, :n_far], axis=1, keepdims=True)
    near = diag[:, n_far:][:, ::-1]
    zeros = jnp.zeros((H, MAX_REL - (CHUNK - 1)), F32)
    return jnp.concatenate([zeros, near, far], axis=1)


def _tri(n, lower):
    r = lax.broadcasted_iota(jnp.int32, (n, n), 0)
    c = lax.broadcasted_iota(jnp.int32, (n, n), 1)
    return jnp.where((c <= r) if lower else (c >= r), 1.0, 0.0).astype(F32)


def _hgrn_gates(zq_ref, zf_ref, lbl_ref, q_s, k_s, b_s):
    lb = jax.nn.sigmoid(lbl_ref[0:1, :] - lbl_ref[1:2, :])
    zq = zq_ref[...]
    sig = jax.nn.sigmoid(zf_ref[...])
    f = lb + (1.0 - lb) * sig
    sq = jax.nn.sigmoid(zq)
    q_s[...] = zq * sq
    k_s[...] = 1.0 - f
    b_s[...] = _dot(_tri(CHUNK, True), jnp.log(f), precision=HIGHEST)
    return lb, sig, f, sq


def _sub_rows(i):
    return pl.ds(i * SUB, SUB)


def _row_mask(s):
    return lax.broadcasted_iota(jnp.int32, (SUB, HD_B), 0) >= s


def _decay_from(b_sub, b_row, s):
    return jnp.where(_row_mask(s), jnp.exp(jnp.minimum(b_sub - b_row, 0.0)), 0.0)


def _hgrn_fwd(proj, lb_logits, gnorm_g, rider=None):
    T = proj.shape[0]
    nC = T // CHUNK
    W = lb_logits.shape[1]
    G = W // HD_B // HGRN_HEADS
    col0 = (proj.shape[1] - 4 * W) // (HD_B * HGRN_HEADS)
    wide = HGRN_HEADS * HD_B

    def body(*refs):
        @pl.when(pl.program_id(1) == 0)
        def _():
            refs[9][...] = jnp.zeros_like(refs[9])

        for h in range(HGRN_HEADS):
            lanes = pl.ds(h * HD_B, HD_B)
            one_head(*[r.at[:, lanes] for r in refs[:5]], refs[5], *[r.at[:, lanes] for r in refs[6:8]],
                     *[r.at[h] for r in refs[8:]])

    def one_head(zq_ref, zf_ref, xi_ref, zg_ref, lbl_ref, gn_ref, mix_ref, o_ref, stall_ref, st_ref, q_s, k_s, b_s, acc_s):
        _hgrn_gates(zq_ref, zf_ref, lbl_ref, q_s, k_s, b_s)
        q, k, b = q_s[...], k_s[...], b_s[...]
        st = st_ref[...]
        stall_ref[...] = st
        b_last = b_s[CHUNK - 1:CHUNK, :]
        acc_s[...] = _dot((q * jnp.exp(b)).astype(BF16), st.astype(BF16), NT)
        for i in range(CHUNK // SUB):
            rows = _sub_rows(i)
            q_i, b_i = q_s[rows, :], b_s[rows, :]
            acc = jnp.zeros((SUB, HD_B), F32)
            if i:
                past = pl.ds(0, i * SUB)
                b_ref = b_s[i * SUB - 1:i * SUB, :]
                qs = (q_i * jnp.exp(b_i - b_ref)).astype(BF16)
                ks = (k_s[past, :] * jnp.exp(b_ref - b_s[past, :])).astype(BF16)
                acc += _dot(_dot(qs, ks, NT).astype(BF16), xi_ref[past, :].astype(BF16))
            for s in range(SUB):
                row = pl.ds(i * SUB + s, 1)
                w = q_i * _decay_from(b_i, b_s[row, :], s)
                acc += jnp.sum(w * k_s[row, :], axis=-1, keepdims=True) * xi_ref[row, :]
            acc_s[rows, :] += acc
        o = acc_s[...]
        kd = (k * jnp.exp(b_last - b)).astype(BF16)
        st_ref[...] = st * jnp.exp(b_last) + _dot(xi_ref[...].astype(BF16), kd, TN)
        o_ref[...] = o
        zg = zg_ref[...]
        rr = lax.rsqrt(jnp.mean(o * o, axis=-1, keepdims=True) + EPS)
        mix_ref[...] = (o * rr * gn_ref[...] * (zg * jax.nn.sigmoid(zg))).astype(mix_ref.dtype)

    col = lambda part: pl.BlockSpec((CHUNK, wide), lambda g, n: (n, col0 + part * G + g))
    out_blk = pl.BlockSpec((CHUNK, wide), lambda g, n: (n, g))
    tile = pltpu.VMEM((HGRN_HEADS, CHUNK, HD_B), F32)
    return _call(
        body, rider, name="hgrn_fwd", grid=(G, nC),
        in_specs=[col(0), col(1), col(2), col(3), pl.BlockSpec((2, wide), lambda g, n: (0, g)),
                  pl.BlockSpec((1, HD_B), lambda g, n: (0, 0))],
        out_specs=[out_blk, out_blk, pl.BlockSpec((HGRN_HEADS, None, HD_B, HD_B), lambda g, n: (g, n, 0, 0))],
        out_shape=[jax.ShapeDtypeStruct((T, W), BF16), jax.ShapeDtypeStruct((T, W), F32),
                   jax.ShapeDtypeStruct((G * HGRN_HEADS, nC, HD_B, HD_B), F32)],
        scratch_shapes=[pltpu.VMEM((HGRN_HEADS, HD_B, HD_B), F32), tile, tile, tile, tile],
        compiler_params=_params(("arbitrary", "arbitrary")),
        operands=(proj, proj, proj, proj, lb_logits, gnorm_g))


def _hgrn_bwd(proj, lb_logits, gnorm_g, o_b, st_all, dmixin, rider=None):
    T = proj.shape[0]
    nC = T // CHUNK
    W = lb_logits.shape[1]
    G = W // HD_B // HGRN_HEADS
    wide = HGRN_HEADS * HD_B
    col0 = (proj.shape[1] - 4 * W) // wide
    dcol0 = (dmixin.shape[1] - W) // wide

    def body(*refs):
        g, n = pl.program_id(0), pl.program_id(1)
        dl0_ref, dgn_ref, dst_ref = refs[13:16]

        @pl.when(n == 0)
        def _():
            dst_ref[...] = jnp.zeros_like(dst_ref)
            dl0_ref[...] = jnp.zeros_like(dl0_ref)

        @pl.when((n == 0) & (g == 0))
        def _():
            dgn_ref[...] = jnp.zeros_like(dgn_ref)

        for h in range(HGRN_HEADS):
            lanes = pl.ds(h * HD_B, HD_B)
            cut = lambda r: r.at[:, lanes]
            one_head(*[cut(r) for r in refs[:5]], refs[5], cut(refs[6]), refs[7].at[h], cut(refs[8]),
                     *[cut(r) for r in refs[9:14]], dgn_ref, *[r.at[h] for r in refs[15:]])

    def one_head(zq_ref, zf_ref, xi_ref, zg_ref, lbl_ref, gn_ref, o_ref, st_ref, dout_ref,
                 dzq_ref, dzf_ref, dxi_ref, dzg_ref, dl0_ref, dgn_ref, dst_ref, q_s, k_s, b_s, do_s, dq_s, dk_s, di_s):
        lb, sig, f, sq = _hgrn_gates(zq_ref, zf_ref, lbl_ref, q_s, k_s, b_s)
        q, k, b = q_s[...], k_s[...], b_s[...]
        zg, o, dout = zg_ref[...], o_ref[...], dout_ref[...]
        sg = jax.nn.sigmoid(zg)
        rr = lax.rsqrt(jnp.mean(o * o, axis=-1, keepdims=True) + EPS)
        on = o * rr
        gn = gn_ref[...]
        dzg_ref[...] = (dout * on * gn * sg * (1.0 + zg * (1.0 - sg))).astype(dzg_ref.dtype)
        d_on = dout * zg * sg
        dgn_ref[...] += _colsum(d_on * on)
        d_on = d_on * gn
        do = rr * (d_on - on * jnp.mean(d_on * on, axis=-1, keepdims=True))
        do_s[...] = do
        dob = do.astype(BF16)
        st, dst = st_ref[...], dst_ref[...]
        b_last = b_s[CHUNK - 1:CHUNK, :]
        eb, e_last, k_dec = jnp.exp(b), jnp.exp(b_last), jnp.exp(b_last - b)
        qt, kd = q * eb, k * k_dec
        dstb = dst.astype(BF16)
        xib = xi_ref[...].astype(BF16)
        d_kd = _dot(xib, dstb)
        dq_s[...] = _dot(dob, st.astype(BF16)) * eb
        dk_s[...] = d_kd * k_dec
        di_s[...] = _dot(kd.astype(BF16), dstb, NT)
        d_b_last = e_last * _colsum(st * dst) + _colsum(d_kd * kd)
        dst_ref[...] = _dot(dob, qt.astype(BF16), TN) + dst * e_last
        for i in range(CHUNK // SUB):
            rows = _sub_rows(i)
            q_i, b_i, do_i = q_s[rows, :], b_s[rows, :], do_s[rows, :]
            dq_i = jnp.zeros((SUB, HD_B), F32)
            if i:
                past = pl.ds(0, i * SUB)
                b_ref = b_s[i * SUB - 1:i * SUB, :]
                e_q, e_k = jnp.exp(b_i - b_ref), jnp.exp(b_ref - b_s[past, :])
                qs, ks = (q_i * e_q).astype(BF16), (k_s[past, :] * e_k).astype(BF16)
                xi_p, do_b = xi_ref[past, :].astype(BF16), do_i.astype(BF16)
                di_s[past, :] += _dot(_dot(ks, qs, NT).astype(BF16), do_b)
                dq_i += _dot(_dot(do_b, xi_p, NT).astype(BF16), ks) * e_q
                dk_s[past, :] += _dot(_dot(xi_p, do_b, NT).astype(BF16), qs) * e_k
            for s in range(SUB):
                row = pl.ds(i * SUB + s, 1)
                k_row, i_row = k_s[row, :], xi_ref[row, :]
                e = _decay_from(b_i, b_s[row, :], s)
                w = q_i * e
                a_col = jnp.sum(w * k_row, axis=-1, keepdims=True)
                da_col = jnp.sum(do_i * i_row, axis=-1, keepdims=True)
                di_s[row, :] += _colsum(a_col * do_i)
                dq_i += da_col * e * k_row
                dk_s[row, :] += _colsum(da_col * w)
            dq_s[rows, :] += dq_i
        dq, dk = dq_s[...], dk_s[...]
        db = q * dq - k * dk
        is_last = lax.broadcasted_iota(jnp.int32, (CHUNK, HD_B), 0) == CHUNK - 1
        db = db + jnp.where(is_last, d_b_last, 0.0)
        df = _dot(_tri(CHUNK, False), db, precision=HIGHEST) / f - dk
        dzf_ref[...] = (df * (1.0 - lb) * sig * (1.0 - sig)).astype(dzf_ref.dtype)
        dl0_ref[...] += _colsum(df * (1.0 - sig)) * (lb * (1.0 - lb))
        zq = zq_ref[...]
        dzq_ref[...] = (dq * sq * (1.0 + zq * (1.0 - sq))).astype(dzq_ref.dtype)
        dxi_ref[...] = di_s[...].astype(dxi_ref.dtype)

    rev = lambda n: nC - 1 - n
    col = lambda part: pl.BlockSpec((CHUNK, wide), lambda g, n: (rev(n), col0 + part * G + g))
    blk = pl.BlockSpec((CHUNK, wide), lambda g, n: (rev(n), g))
    tile = pltpu.VMEM((HGRN_HEADS, CHUNK, HD_B), F32)
    out_big = jax.ShapeDtypeStruct((T, W), BF16)
    return _call(
        body, rider, name="hgrn_bwd", grid=(G, nC),
        in_specs=[col(0), col(1), col(2), col(3), pl.BlockSpec((2, wide), lambda g, n: (0, g)),
                  pl.BlockSpec((1, HD_B), lambda g, n: (0, 0)), blk,
                  pl.BlockSpec((HGRN_HEADS, None, HD_B, HD_B), lambda g, n: (g, rev(n), 0, 0)),
                  pl.BlockSpec((CHUNK, wide), lambda g, n: (rev(n), dcol0 + g))],
        out_specs=[blk, blk, blk, blk, pl.BlockSpec((1, wide), lambda g, n: (0, g)),
                   pl.BlockSpec((1, HD_B), lambda g, n: (0, 0))],
        out_shape=[out_big, out_big, out_big, out_big, jax.ShapeDtypeStruct((1, W), F32),
                   jax.ShapeDtypeStruct((1, HD_B), F32)],
        scratch_shapes=[pltpu.VMEM((HGRN_HEADS, HD_B, HD_B), F32)] + [tile] * 7,
        compiler_params=_params(("arbitrary", "arbitrary")),
        operands=(proj, proj, proj, proj, lb_logits, gnorm_g, o_b, st_all, dmixin))


def _adamw_math(g, w, m, v):
    m = B1 * m + (1.0 - B1) * g
    v = B2 * v + (1.0 - B2) * (g * g)
    m_hat = m / (1.0 - B1 ** STEP)
    v_hat = v / (1.0 - B2 ** STEP)
    return -LR * (m_hat / (jnp.sqrt(v_hat) + ADAM_EPS) + WD * w), m, v


def _adamw(g, w, m, v, name):
    R, C = g.shape
    tr = _row_tile(R, C)

    def body(g_ref, w_ref, m_ref, v_ref, go_ref, d_ref, mo_ref, vo_ref):
        g = g_ref[...]
        go_ref[...] = g
        d_ref[...], mo_ref[...], vo_ref[...] = _adamw_math(g, w_ref[...], m_ref[...], v_ref[...])

    blk = pl.BlockSpec((tr, C), lambda i: (i, 0))
    return pl.pallas_call(
        body, name=name, grid=(R // tr,), in_specs=[blk] * 4, out_specs=[blk] * 4,
        out_shape=[jax.ShapeDtypeStruct((R, C), F32)] * 4, compiler_params=_params(("parallel",), 40),
    )(g, w, m, v)


def _sum_pair(g_full, from_sibling, sel, name):
    Q, K, Ns = g_full.shape
    kh = K // 2
    tr = _row_tile(kh, Ns)
    nh = kh // tr

    def body(sel_ref, a_ref, b_ref, o_ref):
        o_ref[...] = (a_ref[...].astype(F32) + b_ref[...].astype(F32)).astype(o_ref.dtype)

    return pl.pallas_call(
        body, name=name,
        grid_spec=pltpu.PrefetchScalarGridSpec(
            num_scalar_prefetch=1, grid=(Q, nh),
            in_specs=[pl.BlockSpec((None, tr, Ns), lambda q, i, sel: (q, sel[1] * nh + i, 0)),
                      pl.BlockSpec((None, tr, Ns), lambda q, i, sel: (q, i, 0))],
            out_specs=pl.BlockSpec((None, tr, Ns), lambda q, i, sel: (q, i, 0))),
        out_shape=jax.ShapeDtypeStruct((Q, kh, Ns), BF16), compiler_params=_params(("parallel", "parallel")),
    )(sel, g_full, from_sibling)


def _sum_chips(pair_sum, from_chips, sel, name):
    Q, kh, Ns = pair_sum.shape
    tr = _row_tile(kh, Ns)
    nh = kh // tr

    def body(sel_ref, a_ref, b0_ref, b1_ref, b2_ref, o_ref):
        up = lambda r: r[...].astype(F32)
        o_ref[...] = ((up(a_ref) + up(b0_ref)) + up(b1_ref)) + up(b2_ref)

    recv = lambda k: pl.BlockSpec((None, tr, Ns), lambda i, sel: (k, i, 0))
    return pl.pallas_call(
        body, name=name,
        grid_spec=pltpu.PrefetchScalarGridSpec(
            num_scalar_prefetch=1, grid=(nh,),
            in_specs=[pl.BlockSpec((None, tr, Ns), lambda i, sel: (sel[0], i, 0)), recv(0), recv(1), recv(2)],
            out_specs=pl.BlockSpec((tr, Ns), lambda i, sel: (sel[1] * nh + i, 0))),
        out_shape=jax.ShapeDtypeStruct((2 * kh, Ns), F32), compiler_params=_params(("parallel",)),
    )(sel, pair_sum, from_chips, from_chips, from_chips)


def _gather_small(v, name):
    R, L = v.shape

    def body(v_ref, out_ref, send_sems, recv_sems):
        x, y, c = _place()
        me = 4 * x + 2 * y + c
        out_ref[me] = v_ref[...]
        peers = [(_flip(x, k >> 2 & 1), _flip(y, k >> 1 & 1), _flip(c, k & 1)) for k in range(1, N_DEV)]

        def copy(k, row, to):
            return pltpu.make_async_remote_copy(src_ref=v_ref, dst_ref=out_ref.at[row], send_sem=send_sems.at[k],
                                                recv_sem=recv_sems.at[k], device_id=to, device_id_type=MESH)

        sends = [copy(k, me, peer) for k, peer in enumerate(peers)]
        for cp in sends:
            cp.start()
        for k, (px, py, pc) in enumerate(peers):
            copy(k, 4 * px + 2 * py + pc, (x, y, c)).wait_recv()
        for cp in sends:
            cp.wait_send()

    vmem = pl.BlockSpec(memory_space=pltpu.VMEM)
    return pl.pallas_call(
        body, name=name, in_specs=[vmem], out_specs=vmem, out_shape=jax.ShapeDtypeStruct((N_DEV, R, L), F32),
        scratch_shapes=[pltpu.SemaphoreType.DMA((N_DEV - 1,)), pltpu.SemaphoreType.DMA((N_DEV - 1,))],
    )(v)


def _silu(v):
    return v * jax.nn.sigmoid(v)


def _ada_fwd(c_all, w_ada, tn=512):
    M, D = c_all.shape
    Ns = w_ada.shape[1]

    def body(c_ref, w_ref, o_ref):
        o_ref[...] = _dot(_silu(c_ref[...]).astype(BF16), w_ref[...].astype(BF16))

    return pl.pallas_call(
        body, name="ada_fwd", grid=(Ns // tn,),
        in_specs=[pl.BlockSpec((M, D), lambda j: (0, 0)), pl.BlockSpec((D, tn), lambda j: (0, j))],
        out_specs=pl.BlockSpec((M, tn), lambda j: (0, j)), out_shape=jax.ShapeDtypeStruct((M, Ns), F32),
        compiler_params=_params(("parallel",)),
    )(c_all, w_ada)


def _ada_bwd(c_all, dmod, w, m, v, tk=256, tn=1536):
    M, D = c_all.shape
    Ns = dmod.shape[1]

    def body(c_ref, d_ref, w_ref, m_ref, v_ref, g_ref, dl_ref, mo_ref, vo_ref):
        g = _dot(_silu(c_ref[...]).astype(BF16), d_ref[...].astype(BF16), TN)
        g_ref[...] = g
        dl_ref[...], mo_ref[...], vo_ref[...] = _adamw_math(g, w_ref[...], m_ref[...], v_ref[...])

    blk = pl.BlockSpec((tk, tn), lambda i, j: (i, j))
    return pl.pallas_call(
        body, name="ada_bwd", grid=(D // tk, Ns // tn),
        in_specs=[pl.BlockSpec((M, tk), lambda i, j: (0, i)), pl.BlockSpec((M, tn), lambda i, j: (0, j)), blk, blk, blk],
        out_specs=[blk] * 4, out_shape=[jax.ShapeDtypeStruct((D, Ns), F32)] * 4,
        compiler_params=_params(("parallel", "parallel"), 40),
    )(c_all, dmod, w, m, v)


def _small_update(g_all, w, m, v):
    R, L = w.shape

    def body(g_ref, w_ref, m_ref, v_ref, go_ref, d_ref, mo_ref, vo_ref):
        g = g_ref[0]
        for d in range(1, N_DEV):
            g = g + g_ref[d]
        go_ref[...] = g
        d_ref[...], mo_ref[...], vo_ref[...] = _adamw_math(g, w_ref[...], m_ref[...], v_ref[...])

    return pl.pallas_call(body, name="small_update", out_shape=[jax.ShapeDtypeStruct((R, L), F32)] * 4)(g_all, w, m, v)


def _pack(parts, rows):
    flat = jnp.concatenate([p.reshape(-1) for p in parts])
    return jnp.pad(flat, (0, rows * 128 - flat.shape[0])).reshape(rows, 128)


def _unpack(packed, shapes):
    flat, out, at = packed.reshape(-1), [], 0
    for shp in shapes:
        size = 1
        for d in shp:
            size *= d
        out.append(flat[at:at + size].reshape(shp))
        at += size
    return out


def _layer(x, tgt, mod, wts, rel_bias, attn_norm_g, lb_logits, gnorm_g, ln1_g, ln1_b, ln2_g, ln2_b, place=None):
    T, D = x.shape
    aw = attn_norm_g.shape[1]
    shift1, scale1, gate1, shift2, scale2, gate2 = [mod[i:i + 1] for i in range(6)]

    def gather(n, rows=None, into=None):
        return None if place is None else _gather_rider(wts[n], rows, None if into is None else into[0])

    def gathered(n, rode):
        return wts[n] if place is None else lax.dynamic_update_index_in_dim(rode[0], wts[n], place[0], 0)

    def blocks(g):
        return g.reshape(N_CHIPS, -1, g.shape[2])

    def to_sibling(g):
        return None if place is None else _pair_rider(g)

    def pair_sum(n, g, rode=None):
        if place is None:
            return g
        rode = _alone(_pair_rider(g), n + "_send_pair") if rode is None else rode
        return _sum_pair(g, rode[0], place[1], n + "_sum_pair")

    def to_chips(p, rows=None, into=None):
        return None if place is None else _chips_rider(p, rows, None if into is None else into[0])

    def summed(n, p, rode):
        return p if place is None else _sum_chips(p, rode[0], place[1], n + "_sum_chips")

    def to_both(block):
        return None if place is None else _share_rider(block)

    def carrying(mm, *args, rider, **kw):
        return mm(*args, rider=rider, **kw) if rider is not None else (mm(*args, **kw), None)

    def pair_grad(name, a, b, tn, rider):
        if place is None:
            return _mm_tn(a, b, q=N_CHIPS, tk=512, tn=tn, tt=T, name=name), None
        kh = a.shape[1] // 2
        mine = lax.dynamic_slice_in_dim(a, place[1][1] * kh, kh, axis=1)
        part, rode = _mm_tn(mine, b, q=N_CHIPS, tk=512, tn=tn, tt=T, name=name + "_own",
                            rider=_join(_acts_rider(a, b), rider))
        return _mm_tn_add(rode[0], rode[1], part, tk=512, tn=tn, name=name + "_sib"), rode[2:]

    w_in = gathered("w_in", None if place is None else [wts["w_in_gathered"]])
    h1 = _pre_mixer(x, scale1, shift1)
    n_qkv = 3 * aw // 256
    kh_o, kh_f = wts["w_o"].shape[-2] // 2, wts["w_ffn_in"].shape[-2] // 2
    o_cut, f_cuts = 3 * kh_o // 8, (7 * kh_f // 16, 7 * kh_f // 8)
    qkv, rode = carrying(_mm_nn, h1, w_in, tm=ZPAD, tn=256, tk=D, name="proj_qkv", cols=(0, n_qkv), o_dtype=BF16,
                         pad_rows=ZPAD, rider=gather("w_o", (0, o_cut)))
    proj, rode = carrying(_mm_nn, h1, w_in, tm=2048, tn=256, tk=D, name="proj_rec",
                          cols=(n_qkv, N_CHIPS * w_in.shape[2] // 256), rider=gather("w_o", (o_cut, kh_o - o_cut), rode))
    w_o3 = gathered("w_o", rode).reshape(1, D, D)
    bias = _bias_band(rel_bias)
    (mix_a, probs), rode = _attn_fwd(qkv, bias, attn_norm_g, rider=gather("w_ffn_in", (0, f_cuts[0])))
    (mix_b, o_b, st_all), rode = _hgrn_fwd(
        proj, lb_logits, gnorm_g, rider=gather("w_ffn_in", (f_cuts[0], f_cuts[1] - f_cuts[0]), rode))
    mixin = jnp.concatenate([mix_a, mix_b], axis=1)
    mix = _mm_nn(mixin, w_o3, tm=1024, tn=512, tk=D, name="mix_out")
    if place is None:
        x1, h2 = _post_mixer(mix, x, gate1, ln1_g, ln1_b, scale2, shift2)
    else:
        (x1, h2), rode = _post_mixer(mix, x, gate1, ln1_g, ln1_b, scale2, shift2,
                                     rider=gather("w_ffn_in", (f_cuts[1], kh_f - f_cuts[1]), rode))
    w_ffn_in = gathered("w_ffn_in", rode)
    (gate, up, act), rode = _ffn_in_swiglu(h2, w_ffn_in, tm=2048, tn=256, rider=gather("w_ffn_out"))
    w_out3 = gathered("w_ffn_out", rode)
    w_out3 = w_out3.reshape(1, -1, w_out3.shape[2])
    d_ff = w_out3.shape[1]
    f = _mm_nn(act, w_out3, tm=1024, tn=512, tk=d_ff, name="ffn_out")
    du2, df, acc2 = _loss_head(f, x1, tgt, gate2, ln2_g, ln2_b)
    loss = (0.5 / D) * jnp.sum(acc2[3])
    g = blocks(_mm_tn(act, df, q=1, tk=512, tn=1024, tt=T, name="g_ffn_out"))
    d_gate_up, rode = _d_act_swiglu(df, w_out3, gate, up, tm=1024, to=512, rider=to_sibling(g))
    p_out = pair_sum("w_ffn_out", g, rode)
    dff = jnp.concatenate(d_gate_up, axis=1)
    cut = 21 * p_out.shape[1] // 44
    dh2, rode = carrying(_mm_nt, dff, w_ffn_in, tm=1024, to=1024, tn=w_ffn_in.shape[2], name="d_h2",
                         rider=to_chips(p_out, (0, cut)))
    p_fin, rode = pair_grad("g_ffn_in", h2, dff, w_ffn_in.shape[2] // 2,
                            to_chips(p_out, (cut, p_out.shape[1] - cut), rode))
    g_ffn_out = summed("w_ffn_out", p_out, rode)
    if place is None:
        du1, dmix, acc1 = _mid_bwd(dh2, du2, x1, mix, x, gate1, ln1_g, scale2)
    else:
        (du1, dmix, acc1), (g_ffn_out,) = _mid_bwd(dh2, du2, x1, mix, x, gate1, ln1_g, scale2, rider=to_both(g_ffn_out))
    g = blocks(_mm_tn(mixin, dmix, q=1, tk=512, tn=1024, tt=T, name="g_o"))
    dmixin, rode = carrying(_mm_nt, dmix, w_o3, tm=1024, to=512, tn=D, name="d_mixin", rider=to_sibling(g))
    p_o = pair_sum("w_o", g, rode)
    cut = p_fin.shape[1] // 2
    (dq, dk, dv, dbias, dgain), rode = _attn_bwd(qkv, probs, attn_norm_g, dmixin, rider=to_chips(p_fin, (0, cut)))
    (dzq, dzf, dxi, dzg, dl0, dgn), rode = _hgrn_bwd(
        proj, lb_logits, gnorm_g, o_b, st_all, dmixin,
        rider=_join(to_chips(p_fin, (cut, p_fin.shape[1] - cut), rode), to_chips(p_o)))
    g_ffn_in, g_o = summed("w_ffn_in", p_fin, rode[:1]), summed("w_o", p_o, rode[1:])
    dproj = jnp.concatenate([dq, dk[KPAD:].astype(BF16), dv[KPAD:].astype(BF16), dzq, dzf, dxi, dzg], axis=1)
    p_in, rode = pair_grad("g_in", h1, dproj, w_in.shape[2] // 2, _join(to_both(g_ffn_in), to_both(g_o)))
    if place is not None:
        g_ffn_in, g_o = rode
    cut = 3 * p_in.shape[1] // 4
    dh1, rode = carrying(_mm_nt, dproj, w_in, tm=1024, to=1024, tn=w_in.shape[2], name="d_h1",
                         rider=to_chips(p_in, (0, cut)))
    if place is None:
        (grad_x, acc0), g_in = _first_bwd(dh1, du1, x, scale1), p_in
    else:
        (grad_x, acc0), rode = _first_bwd(dh1, du1, x, scale1, rider=to_chips(p_in, (cut, p_in.shape[1] - cut), rode))
        g_in, = _alone(to_both(summed("w_in", p_in, rode)), "w_in_share")
    dmod = jnp.concatenate([acc0[1:2], acc0[0:1], acc1[4:5], acc1[1:2], acc1[0:1], acc2[2:3]], axis=0)
    small = dict(rel_bias=_bias_band_grad(dbias), attn_norm_g=dgain,
                 lb_logits=jnp.concatenate([dl0, -dl0], axis=0), gnorm_g=dgn,
                 ln1_g=acc1[2:3], ln1_b=acc1[3:4], ln2_g=acc2[0:1], ln2_b=acc2[1:2])
    return loss, grad_x, dict(w_in=g_in, w_o=g_o, w_ffn_in=g_ffn_in, w_ffn_out=g_ffn_out), dmod, small


SMALL = ("rel_bias", "attn_norm_g", "lb_logits", "gnorm_g", "ln1_g", "ln1_b", "ln2_g", "ln2_b")
SMALL_ROWS = 256


def kernel(x, c, w_ada, b_ada, w_in, rel_bias, attn_norm_g, lb_logits, gnorm_g, w_o, ln1_g, ln1_b, w_ffn_in, w_ffn_out, ln2_g, ln2_b, loss_target, m_w_ada, m_b_ada, m_w_in, m_rel_bias, m_attn_norm_g, m_lb_logits, m_gnorm_g, m_w_o, m_ln1_g, m_ln1_b, m_w_ffn_in, m_w_ffn_out, m_ln2_g, m_ln2_b, v_w_ada, v_b_ada, v_w_in, v_rel_bias, v_attn_norm_g, v_lb_logits, v_gnorm_g, v_w_o, v_ln1_g, v_ln1_b, v_w_ffn_in, v_w_ffn_out, v_ln2_g, v_ln2_b):
    mx, my, mc = _place()
    me = 4 * mx + 2 * my + mc
    chip = 2 * mx + my
    sel = jnp.stack([chip, mc]).astype(jnp.int32)
    D = x.shape[2]
    ns_ada = w_ada.shape[2]

    big = dict(w_in=(w_in, m_w_in, v_w_in), w_o=(w_o, m_w_o, v_w_o), w_ffn_in=(w_ffn_in, m_w_ffn_in, v_w_ffn_in),
               w_ffn_out=(w_ffn_out, m_w_ffn_out, v_w_ffn_out))
    shards = dict(w_in=w_in[0].astype(BF16))
    kh, rode, at = shards["w_in"].shape[0] // 2, None, 0
    for n, part in (("w_ffn_in", 19), ("w_ffn_out", 9), ("w_o", 4)):
        rows = (at, part * kh // 32)
        (shards[n],), rode = _to_bf16(big[n][0][0], "cast_" + n,
                                      _gather_rider(shards["w_in"], rows, None if rode is None else rode[0]))
        at += rows[1]
    shards["w_in_gathered"] = rode[0]

    c_all = _gather_small(c.reshape(D // 128, 128), "gather_c").reshape(N_DEV, D)
    c_all = jnp.pad(c_all, ((0, 16 - N_DEV), (0, 0)))
    mod_cols = _ada_fwd(c_all, w_ada[0])[:N_DEV]
    mod_all = _gather_small(mod_cols.reshape(-1, 128), "gather_mod").reshape(N_DEV, N_DEV, ns_ada)
    mod = lax.dynamic_index_in_dim(mod_all[::2], me, axis=1, keepdims=False)
    mod = (mod.reshape(1, -1) + b_ada).reshape(6, D)

    loss, grad_x, g_big, dmod, g_small = _layer(
        x[0], loss_target[0], mod, shards, rel_bias[0], attn_norm_g, lb_logits, gnorm_g, ln1_g, ln1_b, ln2_g, ln2_b,
        place=(chip, sel))

    grads, deltas, new_m, new_v = {}, {}, {}, {}
    for n, (w, m, v) in big.items():
        g, d, mo, vo = _adamw(g_big[n], w[0], m[0], v[0], "adamw_" + n)
        grads[n], deltas[n], new_m[n], new_v[n] = g[None], d[None], mo[None], vo[None]

    small_in = dict(rel_bias=(rel_bias, m_rel_bias, v_rel_bias), attn_norm_g=(attn_norm_g, m_attn_norm_g, v_attn_norm_g),
                    lb_logits=(lb_logits, m_lb_logits, v_lb_logits), gnorm_g=(gnorm_g, m_gnorm_g, v_gnorm_g),
                    ln1_g=(ln1_g, m_ln1_g, v_ln1_g), ln1_b=(ln1_b, m_ln1_b, v_ln1_b), ln2_g=(ln2_g, m_ln2_g, v_ln2_g),
                    ln2_b=(ln2_b, m_ln2_b, v_ln2_b))
    g_all = _gather_small(_pack([dmod] + [g_small[n] for n in SMALL] + [loss], SMALL_ROWS), "gather_small")
    packed = [_pack([t] + [small_in[n][i] for n in SMALL] + [jnp.zeros((), F32)], SMALL_ROWS)
              for i, t in enumerate((b_ada, m_b_ada, v_b_ada))]
    shapes = [b_ada.shape] + [small_in[n][0].shape for n in SMALL] + [()]
    outs = [_unpack(o, shapes) for o in _small_update(g_all, *packed)]
    loss = outs[0][-1]
    for i, n in enumerate(("b_ada",) + SMALL):
        grads[n], deltas[n], new_m[n], new_v[n] = outs[0][i], outs[1][i], outs[2][i], outs[3][i]

    dmod_all = g_all[:, :6 * D // 128].reshape(N_DEV, 6 * D)
    dmod_cols = lax.dynamic_slice_in_dim(dmod_all, chip * ns_ada, ns_ada, axis=1)
    dmod_cols = jnp.pad(dmod_cols, ((0, 16 - N_DEV), (0, 0)))
    g, d, mo, vo = _ada_bwd(c_all, dmod_cols, w_ada[0], m_w_ada[0], v_w_ada[0])
    grads["w_ada"], deltas["w_ada"], new_m["w_ada"], new_v["w_ada"] = g[None], d[None], mo[None], vo[None]

    order = ("w_ada", "b_ada", "w_in", "rel_bias", "attn_norm_g", "lb_logits", "gnorm_g", "w_o", "ln1_g", "ln1_b",
             "w_ffn_in", "w_ffn_out", "ln2_g", "ln2_b")
    return (loss, grad_x[None], *[grads[n] for n in order], *[deltas[n] for n in order],
            *[new_m[n] for n in order], *[new_v[n] for n in order])
```

```python
import numpy as np
import jax
import jax.numpy as jnp
from jax import lax
from jax.experimental import pallas as pl
from jax.experimental.pallas import tpu as pltpu

F32 = jnp.float32
BF16 = jnp.bfloat16
MESH = pl.DeviceIdType.MESH
HIGHEST = lax.Precision.HIGHEST

CHUNK = 64
N_PAST = 8
QG = 4
QROWS = QG * CHUNK
KPAD = N_PAST * CHUNK
ZPAD = 2 * KPAD
UNION = (QG + N_PAST) * CHUNK
BAND = (N_PAST + 1) * CHUNK
HD_A = 64
HD_B = 128
SUB = 16
HGRN_HEADS = 8
MAX_REL = 256
EPS = 1e-5
ALPHA = 2.0 ** 0.25
LR, B1, B2, ADAM_EPS, WD, STEP = 1e-3, 0.9, 0.999, 1e-8, 0.01, 10
N_CHIPS = 4
N_DEV = 8
NEG = -1e30
TILE_BYTES = 3 << 19

NN = ((1,), (0,))
NT = ((1,), (1,))
TN = ((0,), (0,))


def _dot(a, b, dims=NN, precision=None):
    return lax.dot_general(a, b, (dims, ((), ())), preferred_element_type=F32, precision=precision)


def _params(sem=None, vmem_mb=None, **kw):
    return pltpu.CompilerParams(dimension_semantics=sem,
                                vmem_limit_bytes=None if vmem_mb is None else vmem_mb << 20, **kw)


def _row_tile(rows, cols):
    for cand in (512, 256, 128, 64, 32, 16, 8):
        if rows % cand == 0 and cand * cols * 4 <= TILE_BYTES:
            return cand
    raise ValueError((rows, cols))


def _place():
    return lax.axis_index("x"), lax.axis_index("y"), lax.axis_index("c")


def _flip(v, bit):
    return 1 - v if bit else v


ANY = pl.BlockSpec(memory_space=pl.ANY)
CHIP_FLIPS = ((1, 0), (0, 1), (1, 1))


class _Rider:
    def __init__(self, operands, out_shape, n_sems, start, finish, aliases=None):
        self.operands, self.out_shape, self.n_sems, self.start, self.finish = operands, out_shape, n_sems, start, finish
        self.aliases = aliases or {}


def _call(body, rider, *, name, grid, in_specs, out_specs, out_shape, scratch_shapes, compiler_params, operands):
    if rider is None:
        outs = pl.pallas_call(body, name=name, grid=grid, in_specs=in_specs, out_specs=out_specs, out_shape=out_shape,
                              scratch_shapes=scratch_shapes, compiler_params=compiler_params)(*operands)
        return list(outs), []
    n_in, n_out, n_sc = len(in_specs), len(out_specs), len(scratch_shapes)
    r_in, r_out = len(rider.operands), len(rider.out_shape)

    def carried(*refs):
        refs = list(refs)
        cuts = [n_in, r_in, n_out, r_out, n_sc]
        ins, r_ins, outs, r_outs, scratch = [[refs.pop(0) for _ in range(n)] for n in cuts]
        first, last = None, None
        for axis, size in enumerate(grid):
            i = pl.program_id(axis)
            first = (i == 0) if first is None else first & (i == 0)
            last = (i == size - 1) if last is None else last & (i == size - 1)

        @pl.when(first)
        def _():
            rider.start(r_ins, r_outs, *refs)

        body(*ins, *outs, *scratch)

        @pl.when(last)
        def _():
            rider.finish(r_ins, r_outs, *refs)

    sems = [pltpu.SemaphoreType.DMA((rider.n_sems,)), pltpu.SemaphoreType.DMA((rider.n_sems,))]
    outs = pl.pallas_call(carried, name=name, grid=grid, in_specs=list(in_specs) + [ANY] * r_in,
                          out_specs=list(out_specs) + [ANY] * r_out, out_shape=list(out_shape) + rider.out_shape,
                          scratch_shapes=list(scratch_shapes) + sems, compiler_params=compiler_params,
                          input_output_aliases={n_in + i: n_out + o for i, o in rider.aliases.items()},
                          )(*operands, *rider.operands)
    return list(outs[:n_out]), list(outs[n_out:])


def _alone(rider, name):
    def body(*refs):
        ins, outs, sems = refs[:len(rider.operands)], refs[len(rider.operands):-2], refs[-2:]
        rider.start(ins, outs, *sems)
        rider.finish(ins, outs, *sems)

    return pl.pallas_call(
        body, name=name, in_specs=[ANY] * len(rider.operands), out_specs=[ANY] * len(rider.out_shape),
        out_shape=rider.out_shape, input_output_aliases=rider.aliases,
        scratch_shapes=[pltpu.SemaphoreType.DMA((rider.n_sems,)), pltpu.SemaphoreType.DMA((rider.n_sems,))],
    )(*rider.operands)


class _Sems:
    def __init__(self, sems, base):
        self.sems, self.base = sems, base

    @property
    def at(self):
        return self

    def __getitem__(self, k):
        return self.sems.at[self.base + k]


def _join(*riders):
    riders = [r for r in riders if r is not None]
    if len(riders) < 2:
        return riders[0] if riders else None

    def parts(ins, outs, send_sems, recv_sems):
        i = o = s = 0
        for r in riders:
            ni, no = len(r.operands), len(r.out_shape)
            yield r, ins[i:i + ni], outs[o:o + no], _Sems(send_sems, s), _Sems(recv_sems, s)
            i, o, s = i + ni, o + no, s + r.n_sems

    def start(*refs):
        for r, *args in parts(*refs):
            r.start(*args)

    def finish(*refs):
        for r, *args in parts(*refs):
            r.finish(*args)

    aliases, i, o = {}, 0, 0
    for r in riders:
        aliases.update({i + a: o + b for a, b in r.aliases.items()})
        i, o = i + len(r.operands), o + len(r.out_shape)
    return _Rider([a for r in riders for a in r.operands], [s for r in riders for s in r.out_shape],
                  sum(r.n_sems for r in riders), start, finish, aliases)


def _gather_rider(shard, rows=None, into=None):
    K, Ns = shard.shape
    kh = K // 2
    first_row, n_rows = rows or (0, kh)

    def copies(w_ref, out_ref, send_sems, recv_sems):
        x, y, c = _place()
        chips = [(_flip(x, fx), _flip(y, fy)) for fx, fy in CHIP_FLIPS]

        def half(chip, which):
            return out_ref.at[2 * chip[0] + chip[1], pl.ds(which * kh + first_row, n_rows), :]

        def copy(k, dst, to, src=None):
            return pltpu.make_async_remote_copy(src_ref=dst if src is None else src, dst_ref=dst,
                                                send_sem=send_sems.at[k], recv_sem=recv_sems.at[k],
                                                device_id=to, device_id_type=MESH)

        def first():
            return [copy(j, half((x, y), c), (*chip, c), src=w_ref.at[pl.ds(c * kh + first_row, n_rows), :])
                    for j, chip in enumerate(chips)]

        def onward():
            return [copy(3 + j, half(chip, c), (x, y, 1 - c)) for j, chip in enumerate(chips)]

        def arriving(base, which):
            return [copy(base + j, half(chip, which), (x, y, c)) for j, chip in enumerate(chips)]

        return first, onward, arriving

    def start(ins, outs, send_sems, recv_sems):
        for cp in copies(ins[0], outs[0], send_sems, recv_sems)[0]():
            cp.start()

    def finish(ins, outs, send_sems, recv_sems):
        x, y, c = _place()
        first, onward, arriving = copies(ins[0], outs[0], send_sems, recv_sems)
        passed = onward()
        for arrived, cp in zip(arriving(0, c), passed):
            arrived.wait_recv()
            cp.start()
        for arrived in arriving(3, 1 - c):
            arrived.wait_recv()
        for cp in first() + passed:
            cp.wait_send()

    full = jax.ShapeDtypeStruct((N_CHIPS, K, Ns), shard.dtype)
    if into is None:
        return _Rider([shard], [full], 6, start, finish)
    return _Rider([shard, into], [full], 6, start, finish, aliases={1: 0})


def _pair_rider(g_full):
    Q, K, Ns = g_full.shape
    kh = K // 2

    def copy(g_ref, got_ref, send_sems, recv_sems):
        x, y, c = _place()
        return pltpu.make_async_remote_copy(src_ref=g_ref.at[:, pl.ds((1 - c) * kh, kh), :], dst_ref=got_ref,
                                            send_sem=send_sems.at[0], recv_sem=recv_sems.at[0],
                                            device_id=(x, y, 1 - c), device_id_type=MESH)

    def start(ins, outs, send_sems, recv_sems):
        copy(ins[0], outs[0], send_sems, recv_sems).start()

    def finish(ins, outs, send_sems, recv_sems):
        copy(ins[0], outs[0], send_sems, recv_sems).wait()

    return _Rider([g_full], [jax.ShapeDtypeStruct((Q, kh, Ns), g_full.dtype)], 1, start, finish)


def _acts_rider(a, b):
    T, K = a.shape
    kh = K // 2

    def copies(ins, outs, send_sems, recv_sems):
        x, y, c = _place()
        pair = [(ins[0].at[:, pl.ds((1 - c) * kh, kh)], outs[0]), (ins[1], outs[1])]
        return [pltpu.make_async_remote_copy(src_ref=src, dst_ref=dst, send_sem=send_sems.at[k], recv_sem=recv_sems.at[k],
                                             device_id=(x, y, 1 - c), device_id_type=MESH)
                for k, (src, dst) in enumerate(pair)]

    def start(*refs):
        for cp in copies(*refs):
            cp.start()

    def finish(*refs):
        for cp in copies(*refs):
            cp.wait()

    return _Rider([a, b], [jax.ShapeDtypeStruct((T, kh), a.dtype), jax.ShapeDtypeStruct(b.shape, b.dtype)], 2,
                  start, finish)


def _share_rider(block):
    K, Ns = block.shape
    kh = K // 2

    def halves(out_ref):
        x, y, c = _place()
        return out_ref.at[pl.ds(c * kh, kh), :], out_ref.at[pl.ds((1 - c) * kh, kh), :], (x, y, 1 - c)

    def start(ins, outs, send_sems, recv_sems):
        mine, _, sibling = halves(outs[0])
        pltpu.make_async_remote_copy(src_ref=mine, dst_ref=mine, send_sem=send_sems.at[0], recv_sem=recv_sems.at[0],
                                     device_id=sibling, device_id_type=MESH).start()

    def finish(ins, outs, send_sems, recv_sems):
        mine, theirs, sibling = halves(outs[0])
        pltpu.make_async_remote_copy(src_ref=theirs, dst_ref=theirs, send_sem=send_sems.at[0], recv_sem=recv_sems.at[0],
                                     device_id=sibling, device_id_type=MESH).wait_recv()
        pltpu.make_async_remote_copy(src_ref=mine, dst_ref=mine, send_sem=send_sems.at[0], recv_sem=recv_sems.at[0],
                                     device_id=sibling, device_id_type=MESH).wait_send()

    return _Rider([block], [jax.ShapeDtypeStruct((K, Ns), block.dtype)], 1, start, finish, aliases={0: 0})


def _chips_rider(pair_sum, rows=None, into=None):
    Q, kh, Ns = pair_sum.shape
    first_row, n_rows = rows or (0, kh)

    def copies(p_ref, got_ref, send_sems, recv_sems):
        x, y, c = _place()
        part = pl.ds(first_row, n_rows)
        out = []
        for j, (fx, fy) in enumerate(CHIP_FLIPS):
            px, py = _flip(x, fx), _flip(y, fy)
            out.append(pltpu.make_async_remote_copy(
                src_ref=p_ref.at[2 * px + py, part, :], dst_ref=got_ref.at[j, part, :], send_sem=send_sems.at[j],
                recv_sem=recv_sems.at[j], device_id=(px, py, c), device_id_type=MESH))
        return out

    def start(ins, outs, send_sems, recv_sems):
        for cp in copies(ins[0], outs[0], send_sems, recv_sems):
            cp.start()

    def finish(ins, outs, send_sems, recv_sems):
        sends = copies(ins[0], outs[0], send_sems, recv_sems)
        for cp in sends:
            cp.wait_recv()
        for cp in sends:
            cp.wait_send()

    got = jax.ShapeDtypeStruct((Q - 1, kh, Ns), pair_sum.dtype)
    if into is None:
        return _Rider([pair_sum], [got], 3, start, finish)
    return _Rider([pair_sum, into], [got], 3, start, finish, aliases={1: 0})


def _mm(a, b, *, grid, a_spec, b_spec, o_spec, o_shape, o_dtype, dims, acc_shape, name, rider=None, zero_rows=0,
        vmem_mb=48):
    nk = grid[2]

    def body(a_ref, b_ref, o_ref, *scratch):
        if zero_rows:
            @pl.when(pl.program_id(0) < zero_rows)
            def _():
                o_ref[...] = jnp.zeros_like(o_ref)

            @pl.when(pl.program_id(0) >= zero_rows)
            def _():
                o_ref[...] = _dot(a_ref[...], b_ref[...], dims).astype(o_ref.dtype)
            return
        part = _dot(a_ref[...], b_ref[...], dims)
        if nk == 1:
            o_ref[...] = part.astype(o_ref.dtype)
            return
        acc_ref, = scratch
        k = pl.program_id(2)

        @pl.when(k == 0)
        def _():
            acc_ref[...] = part

        @pl.when(k > 0)
        def _():
            acc_ref[...] += part

        @pl.when(k == nk - 1)
        def _():
            o_ref[...] = acc_ref[...].astype(o_ref.dtype)

    (out,), rode = _call(
        body, rider, name=name, grid=grid, in_specs=[a_spec, b_spec], out_specs=[o_spec],
        out_shape=[jax.ShapeDtypeStruct(o_shape, o_dtype)],
        scratch_shapes=[] if nk == 1 else [pltpu.VMEM(acc_shape, F32)],
        compiler_params=_params(("parallel", "parallel", "arbitrary") if rider is None else ("arbitrary",) * 3, vmem_mb),
        operands=(a, b))
    return out if rider is None else (out, rode)


def _mm_nn(a, w, *, tm, tn, tk, name, rider=None, cols=None, o_dtype=F32, pad_rows=0):
    T, K = a.shape
    Q, _, Ns = w.shape
    nbs = Ns // tn
    tm = min(tm, T)
    j0, j1 = cols or (0, Q * nbs)
    lead = pad_rows // tm
    return _mm(a, w, grid=(lead + T // tm, j1 - j0, K // tk),
               a_spec=pl.BlockSpec((tm, tk), lambda i, j, k: (jnp.maximum(i - lead, 0), k)),
               b_spec=pl.BlockSpec((None, tk, tn), lambda i, j, k: ((j + j0) // nbs, k, (j + j0) % nbs)),
               o_spec=pl.BlockSpec((tm, tn), lambda i, j, k: (i, j)),
               o_shape=(pad_rows + T, (j1 - j0) * tn), o_dtype=o_dtype, dims=NN, acc_shape=(tm, tn), name=name,
               rider=rider, zero_rows=lead)


def _mm_nt(g, w, *, tm, to, tn, name, rider=None):
    T = g.shape[0]
    Q, K, Ns = w.shape
    nbs = Ns // tn
    tm = min(tm, T)
    return _mm(g, w, grid=(T // tm, K // to, Q * nbs),
               a_spec=pl.BlockSpec((tm, tn), lambda i, j, n: (i, n)),
               b_spec=pl.BlockSpec((None, to, tn), lambda i, j, n: (n // nbs, j, n % nbs)),
               o_spec=pl.BlockSpec((tm, to), lambda i, j, n: (i, j)),
               o_shape=(T, K), o_dtype=F32, dims=NT, acc_shape=(tm, to), name=name, rider=rider)


def _mm_tn(a, g, *, q, tk, tn, tt, name, rider=None):
    T, K = a.shape
    Ns = g.shape[1] // q
    nbs = Ns // tn
    return _mm(a, g, grid=(K // tk, q * nbs, T // tt),
               a_spec=pl.BlockSpec((tt, tk), lambda i, j, t: (t, i)),
               b_spec=pl.BlockSpec((tt, tn), lambda i, j, t: (t, j)),
               o_spec=pl.BlockSpec((None, tk, tn), lambda i, j, t: (j // nbs, i, j % nbs)),
               o_shape=(q, K, Ns), o_dtype=BF16, dims=TN, acc_shape=(tk, tn), name=name, rider=rider)


def _mm_tn_add(a, g, part, *, tk, tn, name):
    T, K = a.shape
    Q, _, Ns = part.shape
    nbs = Ns // tn

    def body(a_ref, g_ref, p_ref, o_ref):
        o_ref[...] = (_dot(a_ref[...], g_ref[...], TN) + p_ref[...].astype(F32)).astype(o_ref.dtype)

    blk = pl.BlockSpec((None, tk, tn), lambda i, j: (j // nbs, i, j % nbs))
    return pl.pallas_call(
        body, name=name, grid=(K // tk, Q * nbs),
        in_specs=[pl.BlockSpec((T, tk), lambda i, j: (0, i)), pl.BlockSpec((T, tn), lambda i, j: (0, j)), blk],
        out_specs=blk, out_shape=jax.ShapeDtypeStruct((Q, K, Ns), BF16),
        compiler_params=_params(("parallel", "parallel"), 48),
    )(a, g, part)


def _ln(u):
    mu = jnp.mean(u, axis=-1, keepdims=True)
    d = u - mu
    r = lax.rsqrt(jnp.mean(d * d, axis=-1, keepdims=True) + EPS)
    return d * r, r


def _ln_bwd(dy, un, r):
    return r * (dy - jnp.mean(dy, axis=-1, keepdims=True) - un * jnp.mean(dy * un, axis=-1, keepdims=True))


def _colsum(v):
    return jnp.sum(v, axis=0, keepdims=True)


def _rowwise(name, fn, bigs, vecs, out_dtypes, n_acc, tm=128, rider=None):
    T, D = bigs[0].shape
    nb, nv, no = len(bigs), len(vecs), len(out_dtypes)

    def body(*refs):
        outs, accs = fn([r[...] for r in refs[:nb]], [r[...] for r in refs[nb:nb + nv]])
        for r, o in zip(refs[nb + nv:nb + nv + no], outs):
            r[...] = o.astype(r.dtype)
        if n_acc:
            acc_ref = refs[nb + nv + no]

            @pl.when(pl.program_id(0) == 0)
            def _():
                acc_ref[...] = jnp.zeros_like(acc_ref)

            for row, a in enumerate(accs):
                acc_ref[row:row + 1, :] += a

    big_spec = pl.BlockSpec((tm, D), lambda i: (i, 0))
    vec_spec = pl.BlockSpec((1, D), lambda i: (0, 0))
    out_shape = [jax.ShapeDtypeStruct((T, D), dt) for dt in out_dtypes]
    out_specs = [big_spec] * no
    if n_acc:
        out_shape.append(jax.ShapeDtypeStruct((8, D), F32))
        out_specs.append(pl.BlockSpec((8, D), lambda i: (0, 0)))
    outs, rode = _call(
        body, rider, name=name, grid=(T // tm,), in_specs=[big_spec] * nb + [vec_spec] * nv,
        out_specs=out_specs, out_shape=out_shape, scratch_shapes=[],
        compiler_params=_params(("arbitrary",), 48), operands=(*bigs, *vecs))
    return outs if rider is None else (outs, rode)


def _to_bf16(w, name, rider=None):
    R, C = w.shape
    tr = _row_tile(R, C)

    def body(w_ref, o_ref):
        o_ref[...] = w_ref[...].astype(o_ref.dtype)

    blk = pl.BlockSpec((tr, C), lambda i: (i, 0))
    return _call(body, rider, name=name, grid=(R // tr,), in_specs=[blk], out_specs=[blk],
                 out_shape=[jax.ShapeDtypeStruct((R, C), BF16)], scratch_shapes=[],
                 compiler_params=_params(("arbitrary",)), operands=(w,))


def _pre_mixer(x, scale1, shift1):
    def fn(b, v):
        xn, _ = _ln(b[0])
        return [xn * (1.0 + v[0]) + v[1]], []
    return _rowwise("pre_mixer", fn, [x], [scale1, shift1], [BF16], 0)[0]


def _post_mixer(mix, x, gate1, g1, b1, scale2, shift2, rider=None):
    def fn(b, v):
        un1, _ = _ln(ALPHA * b[1] + v[0] * b[0])
        x1 = un1 * v[1] + v[2]
        xn1, _ = _ln(x1)
        return [x1, xn1 * (1.0 + v[3]) + v[4]], []
    return _rowwise("post_mixer", fn, [mix, x], [gate1, g1, b1, scale2, shift2], [F32, BF16], 0, rider=rider)


def _loss_head(f, x1, tgt, gate2, g2, b2):
    def fn(b, v):
        ff, xx, tt = b
        d_model = ff.shape[-1]
        un2, r2 = _ln(ALPHA * xx + v[0] * ff)
        err = un2 * v[1] + v[2] - tt
        dy = err * (1.0 / d_model)
        du2 = _ln_bwd(dy * v[1], un2, r2)
        return [du2, du2 * v[0]], [_colsum(dy * un2), _colsum(dy), _colsum(du2 * ff), _colsum(err * err)]
    return _rowwise("loss_head", fn, [f, x1, tgt], [gate2, g2, b2], [F32, BF16], 4)


def _mid_bwd(dh2, du2, x1, mix, x, gate1, g1, scale2, rider=None):
    def fn(b, v):
        dh, du, xx1, mm, xx = b
        xn1, r1n = _ln(xx1)
        dx1 = ALPHA * du + _ln_bwd(dh * (1.0 + v[2]), xn1, r1n)
        un1, r1 = _ln(ALPHA * xx + v[0] * mm)
        du1 = _ln_bwd(dx1 * v[1], un1, r1)
        return [du1, du1 * v[0]], [_colsum(dh * xn1), _colsum(dh), _colsum(dx1 * un1), _colsum(dx1),
                                   _colsum(du1 * mm)]
    return _rowwise("mid_bwd", fn, [dh2, du2, x1, mix, x], [gate1, g1, scale2], [F32, BF16], 5, rider=rider)


def _first_bwd(dh1, du1, x, scale1, rider=None):
    def fn(b, v):
        dh, du, xx = b
        xn, r0 = _ln(xx)
        return [ALPHA * du + _ln_bwd(dh * (1.0 + v[0]), xn, r0)], [_colsum(dh * xn), _colsum(dh)]
    return _rowwise("first_bwd", fn, [dh1, du1, x], [scale1], [F32], 2, rider=rider)


def _ffn_in_swiglu(h2, w, *, tm, tn, rider=None):
    T, K = h2.shape
    Q, _, Ns = w.shape
    nbs = Ns // tn
    half = Q * nbs // 2
    tm = min(tm, T)

    def body(a_ref, wg_ref, wu_ref, g_ref, u_ref, act_ref):
        a = a_ref[...]
        g, u = _dot(a, wg_ref[...]), _dot(a, wu_ref[...])
        g_ref[...] = g.astype(g_ref.dtype)
        u_ref[...] = u.astype(u_ref.dtype)
        act_ref[...] = (g * jax.nn.sigmoid(g) * u).astype(act_ref.dtype)

    cols = lambda first: pl.BlockSpec((None, K, tn), lambda i, j: ((j + first) // nbs, 0, (j + first) % nbs))
    blk = pl.BlockSpec((tm, tn), lambda i, j: (i, j))
    return _call(
        body, rider, name="ffn_in", grid=(T // tm, half),
        in_specs=[pl.BlockSpec((tm, K), lambda i, j: (i, 0)), cols(0), cols(half)], out_specs=[blk] * 3,
        out_shape=[jax.ShapeDtypeStruct((T, half * tn), BF16)] * 3, scratch_shapes=[],
        compiler_params=_params(("arbitrary", "arbitrary"), 48), operands=(h2, w, w))


def _d_act_swiglu(df, w, gate, up, *, tm, to, rider=None):
    T, N = df.shape
    F = w.shape[1]
    tm = min(tm, T)

    def body(df_ref, w_ref, g_ref, u_ref, dg_ref, du_ref):
        d = _dot(df_ref[...], w_ref[...], NT)
        g = g_ref[...].astype(F32)
        s = jax.nn.sigmoid(g)
        du_ref[...] = (d * g * s).astype(du_ref.dtype)
        dg_ref[...] = (d * u_ref[...].astype(F32) * s * (1.0 + g * (1.0 - s))).astype(dg_ref.dtype)

    blk = pl.BlockSpec((tm, to), lambda i, j: (i, j))
    return _call(
        body, rider, name="d_act", grid=(T // tm, F // to),
        in_specs=[pl.BlockSpec((tm, N), lambda i, j: (i, 0)), pl.BlockSpec((None, to, N), lambda i, j: (0, j, 0)), blk, blk],
        out_specs=[blk, blk], out_shape=[jax.ShapeDtypeStruct((T, F), BF16)] * 2, scratch_shapes=[],
        compiler_params=_params(("arbitrary", "arbitrary"), 48), operands=(df, w, gate, up))


PAIR = 2


def _fill_table(table_ref, band_ref):
    table_ref[...] = jnp.full(table_ref.shape, NEG, F32)
    for e in range(PAIR):
        for g in range(QG):
            table_ref[e, g * CHUNK:(g + 1) * CHUNK, g * CHUNK:g * CHUNK + BAND] = band_ref[e]


def _attn_probs(q_ref, k_ref, bias_ref, e, step):
    start = pl.multiple_of(step * QROWS, QROWS)
    lanes = pl.ds(e * HD_A, HD_A)
    s = _dot(q_ref[:, lanes], k_ref[pl.ds(start + ZPAD - KPAD, UNION), lanes], NT) * (HD_A ** -0.5) + bias_ref[e]
    col = lax.broadcasted_iota(jnp.int32, s.shape, 1)
    s = jnp.where(col + start >= KPAD, s, NEG)
    p = jnp.exp(s - jnp.max(s, axis=-1, keepdims=True))
    return p / jnp.sum(p, axis=-1, keepdims=True), start


def _attn_specs(T, n_pairs):
    wide = PAIR * HD_A
    per_step = pl.BlockSpec((QROWS, wide), lambda hp, n: (n, hp))
    queries = pl.BlockSpec((QROWS, wide), lambda hp, n: (n + ZPAD // QROWS, hp))
    keys = pl.BlockSpec((ZPAD + T, wide), lambda hp, n: (0, n_pairs + hp))
    values = pl.BlockSpec((ZPAD + T, wide), lambda hp, n: (0, 2 * n_pairs + hp))
    grads = pl.BlockSpec((KPAD + T, wide), lambda hp, n: (0, hp))
    table = pl.BlockSpec((PAIR, CHUNK, BAND), lambda hp, n: (hp, 0, 0))
    vec = pl.BlockSpec((1, wide), lambda hp, n: (0, hp))
    return per_step, queries, keys, values, grads, table, vec


def _probs_spec():
    return pl.BlockSpec((PAIR, QROWS, UNION), lambda hp, n: (hp, n, 0))


def _attn_fwd(qkv, bias, gain, rider=None):
    T = qkv.shape[0] - ZPAD
    W = gain.shape[1]
    n_pairs = W // (PAIR * HD_A)

    def body(q_ref, k_ref, v_ref, band_ref, gain_ref, o_ref, p_ref, table_ref):
        @pl.when(pl.program_id(1) == 0)
        def _():
            _fill_table(table_ref, band_ref)

        for e in range(PAIR):
            lanes = pl.ds(e * HD_A, HD_A)
            p, start = _attn_probs(q_ref, k_ref, table_ref, e, pl.program_id(1))
            p_ref[e] = p.astype(p_ref.dtype)
            o = _dot(p_ref[e], v_ref[pl.ds(start + ZPAD - KPAD, UNION), lanes])
            rr = lax.rsqrt(jnp.mean(o * o, axis=-1, keepdims=True) + EPS)
            o_ref[:, lanes] = (o * rr * gain_ref[:, lanes]).astype(o_ref.dtype)

    per_step, queries, keys, values, _, table, vec = _attn_specs(T, n_pairs)
    return _call(
        body, rider, name="attn_fwd", grid=(n_pairs, T // QROWS), in_specs=[queries, keys, values, table, vec],
        out_specs=[per_step, _probs_spec()],
        out_shape=[jax.ShapeDtypeStruct((T, W), BF16), jax.ShapeDtypeStruct((n_pairs * PAIR, T, UNION), BF16)],
        scratch_shapes=[pltpu.VMEM((PAIR, QROWS, UNION), F32)],
        compiler_params=_params(("arbitrary", "arbitrary"), 40), operands=(qkv, qkv, qkv, bias, gain))


def _attn_bwd(qkv, probs, gain, dmixin, rider=None):
    T = qkv.shape[0] - ZPAD
    W = gain.shape[1]
    n_pairs = W // (PAIR * HD_A)
    scale = HD_A ** -0.5

    def body(q_ref, k_ref, v_ref, p_ref, gain_ref, don_ref, dq_ref, dk_ref, dv_ref, dband_ref, dgain_ref, dtable_ref):
        n = pl.program_id(1)

        @pl.when(n == 0)
        def _():
            dk_ref[...] = jnp.zeros_like(dk_ref)
            dv_ref[...] = jnp.zeros_like(dv_ref)
            dtable_ref[...] = jnp.zeros_like(dtable_ref)
            dgain_ref[...] = jnp.zeros_like(dgain_ref)

        for e in range(PAIR):
            lanes = pl.ds(e * HD_A, HD_A)
            start = pl.multiple_of(n * QROWS, QROWS)
            keys, in_qkv = pl.ds(start, UNION), pl.ds(start + ZPAD - KPAD, UNION)
            pb = p_ref[e]
            p = pb.astype(F32)
            vb = v_ref[in_qkv, lanes]
            o = _dot(pb, vb)
            rr = lax.rsqrt(jnp.mean(o * o, axis=-1, keepdims=True) + EPS)
            on = o * rr
            d_on = don_ref[:, lanes]
            dgain_ref[:, lanes] += _colsum(d_on * on)
            dyo = d_on * gain_ref[:, lanes]
            do = rr * (dyo - on * jnp.mean(dyo * on, axis=-1, keepdims=True))
            dob = do.astype(BF16)
            dp = _dot(dob, vb, NT)
            ds = p * (dp - jnp.sum(do * o, axis=-1, keepdims=True))
            dtable_ref[e] += ds
            dsb = ds.astype(BF16)
            dq_ref[:, lanes] = (_dot(dsb, k_ref[in_qkv, lanes]) * scale).astype(dq_ref.dtype)
            dk_ref[keys, lanes] += _dot(dsb, q_ref[:, lanes], TN) * scale
            dv_ref[keys, lanes] += _dot(pb, dob, TN)

        @pl.when(n == T // QROWS - 1)
        def _():
            for e in range(PAIR):
                dband_ref[e] = sum(dtable_ref[e, g * CHUNK:(g + 1) * CHUNK, g * CHUNK:g * CHUNK + BAND]
                                   for g in range(QG))

    per_step, queries, keys, values, grads, table, vec = _attn_specs(T, n_pairs)
    H = n_pairs * PAIR
    return _call(
        body, rider, name="attn_bwd", grid=(n_pairs, T // QROWS),
        in_specs=[queries, keys, values, _probs_spec(), vec, per_step],
        out_specs=[per_step, grads, grads, table, vec],
        out_shape=[jax.ShapeDtypeStruct((T, W), BF16), jax.ShapeDtypeStruct((KPAD + T, W), F32),
                   jax.ShapeDtypeStruct((KPAD + T, W), F32), jax.ShapeDtypeStruct((H, CHUNK, BAND), F32),
                   jax.ShapeDtypeStruct((1, W), F32)],
        scratch_shapes=[pltpu.VMEM((PAIR, QROWS, UNION), F32)],
        compiler_params=_params(("arbitrary", "arbitrary"), 40),
        operands=(qkv, qkv, qkv, probs, gain, dmixin))


N_DIAG = CHUNK + BAND - 1


def _bias_band(rel_bias):
    H = rel_bias.shape[0]
    idx = np.clip(BAND - 1 - np.arange(N_DIAG), -MAX_REL, MAX_REL) + MAX_REL
    rolled = rel_bias[:, idx[(np.arange(N_DIAG) + CHUNK - 1) % N_DIAG]]
    flat = jnp.broadcast_to(rolled[:, None, :], (H, CHUNK, N_DIAG)).reshape(H, CHUNK * N_DIAG)
    return flat[:, :CHUNK * (N_DIAG - 1)].reshape(H, CHUNK, N_DIAG - 1)[:, :, :BAND]


def _bias_band_grad(dband):
    H = dband.shape[0]
    skew = jnp.pad(dband, ((0, 0), (0, 0), (CHUNK - 1, 0))).reshape(H, CHUNK * N_DIAG)
    skew = jnp.pad(skew, ((0, 0), (0, CHUNK))).reshape(H, CHUNK, N_DIAG + 1)
    diag = jnp.sum(skew, axis=1)[:, :N_DIAG]
    n_far = BAND - MAX_REL
    far = jnp.sum(diag[:, :n_far], axis=1, keepdims=True)
    near = diag[:, n_far:][:, ::-1]
    zeros = jnp.zeros((H, MAX_REL - (CHUNK - 1)), F32)
    return jnp.concatenate([zeros, near, far], axis=1)


def _tri(n, lower):
    r = lax.broadcasted_iota(jnp.int32, (n, n), 0)
    c = lax.broadcasted_iota(jnp.int32, (n, n), 1)
    return jnp.where((c <= r) if lower else (c >= r), 1.0, 0.0).astype(F32)


def _hgrn_gates(zq_ref, zf_ref, lbl_ref, q_s, k_s, b_s):
    lb = jax.nn.sigmoid(lbl_ref[0:1, :] - lbl_ref[1:2, :])
    zq = zq_ref[...]
    sig = jax.nn.sigmoid(zf_ref[...])
    f = lb + (1.0 - lb) * sig
    sq = jax.nn.sigmoid(zq)
    q_s[...] = zq * sq
    k_s[...] = 1.0 - f
    b_s[...] = _dot(_tri(CHUNK, True), jnp.log(f), precision=HIGHEST)
    return lb, sig, f, sq


def _sub_rows(i):
    return pl.ds(i * SUB, SUB)


def _row_mask(s):
    return lax.broadcasted_iota(jnp.int32, (SUB, HD_B), 0) >= s


def _decay_from(b_sub, b_row, s):
    return jnp.where(_row_mask(s), jnp.exp(jnp.minimum(b_sub - b_row, 0.0)), 0.0)


def _hgrn_fwd(proj, lb_logits, gnorm_g, rider=None):
    T = proj.shape[0]
    nC = T // CHUNK
    W = lb_logits.shape[1]
    G = W // HD_B // HGRN_HEADS
    col0 = (proj.shape[1] - 4 * W) // (HD_B * HGRN_HEADS)
    wide = HGRN_HEADS * HD_B

    def body(*refs):
        @pl.when(pl.program_id(1) == 0)
        def _():
            refs[9][...] = jnp.zeros_like(refs[9])

        for h in range(HGRN_HEADS):
            lanes = pl.ds(h * HD_B, HD_B)
            one_head(*[r.at[:, lanes] for r in refs[:5]], refs[5], *[r.at[:, lanes] for r in refs[6:8]],
                     *[r.at[h] for r in refs[8:]])

    def one_head(zq_ref, zf_ref, xi_ref, zg_ref, lbl_ref, gn_ref, mix_ref, o_ref, stall_ref, st_ref, q_s, k_s, b_s, acc_s):
        _hgrn_gates(zq_ref, zf_ref, lbl_ref, q_s, k_s, b_s)
        q, k, b = q_s[...], k_s[...], b_s[...]
        st = st_ref[...]
        stall_ref[...] = st
        b_last = b_s[CHUNK - 1:CHUNK, :]
        acc_s[...] = _dot((q * jnp.exp(b)).astype(BF16), st.astype(BF16), NT)
        for i in range(CHUNK // SUB):
            rows = _sub_rows(i)
            q_i, b_i = q_s[rows, :], b_s[rows, :]
            acc = jnp.zeros((SUB, HD_B), F32)
            if i:
                past = pl.ds(0, i * SUB)
                b_ref = b_s[i * SUB - 1:i * SUB, :]
                qs = (q_i * jnp.exp(b_i - b_ref)).astype(BF16)
                ks = (k_s[past, :] * jnp.exp(b_ref - b_s[past, :])).astype(BF16)
                acc += _dot(_dot(qs, ks, NT).astype(BF16), xi_ref[past, :].astype(BF16))
            for s in range(SUB):
                row = pl.ds(i * SUB + s, 1)
                w = q_i * _decay_from(b_i, b_s[row, :], s)
                acc += jnp.sum(w * k_s[row, :], axis=-1, keepdims=True) * xi_ref[row, :]
            acc_s[rows, :] += acc
        o = acc_s[...]
        kd = (k * jnp.exp(b_last - b)).astype(BF16)
        st_ref[...] = st * jnp.exp(b_last) + _dot(xi_ref[...].astype(BF16), kd, TN)
        o_ref[...] = o
        zg = zg_ref[...]
        rr = lax.rsqrt(jnp.mean(o * o, axis=-1, keepdims=True) + EPS)
        mix_ref[...] = (o * rr * gn_ref[...] * (zg * jax.nn.sigmoid(zg))).astype(mix_ref.dtype)

    col = lambda part: pl.BlockSpec((CHUNK, wide), lambda g, n: (n, col0 + part * G + g))
    out_blk = pl.BlockSpec((CHUNK, wide), lambda g, n: (n, g))
    tile = pltpu.VMEM((HGRN_HEADS, CHUNK, HD_B), F32)
    return _call(
        body, rider, name="hgrn_fwd", grid=(G, nC),
        in_specs=[col(0), col(1), col(2), col(3), pl.BlockSpec((2, wide), lambda g, n: (0, g)),
                  pl.BlockSpec((1, HD_B), lambda g, n: (0, 0))],
        out_specs=[out_blk, out_blk, pl.BlockSpec((HGRN_HEADS, None, HD_B, HD_B), lambda g, n: (g, n, 0, 0))],
        out_shape=[jax.ShapeDtypeStruct((T, W), BF16), jax.ShapeDtypeStruct((T, W), F32),
                   jax.ShapeDtypeStruct((G * HGRN_HEADS, nC, HD_B, HD_B), F32)],
        scratch_shapes=[pltpu.VMEM((HGRN_HEADS, HD_B, HD_B), F32), tile, tile, tile, tile],
        compiler_params=_params(("arbitrary", "arbitrary")),
        operands=(proj, proj, proj, proj, lb_logits, gnorm_g))


def _hgrn_bwd(proj, lb_logits, gnorm_g, o_b, st_all, dmixin, rider=None):
    T = proj.shape[0]
    nC = T // CHUNK
    W = lb_logits.shape[1]
    G = W // HD_B // HGRN_HEADS
    wide = HGRN_HEADS * HD_B
    col0 = (proj.shape[1] - 4 * W) // wide
    dcol0 = (dmixin.shape[1] - W) // wide

    def body(*refs):
        g, n = pl.program_id(0), pl.program_id(1)
        dl0_ref, dgn_ref, dst_ref = refs[13:16]

        @pl.when(n == 0)
        def _():
            dst_ref[...] = jnp.zeros_like(dst_ref)
            dl0_ref[...] = jnp.zeros_like(dl0_ref)

        @pl.when((n == 0) & (g == 0))
        def _():
            dgn_ref[...] = jnp.zeros_like(dgn_ref)

        for h in range(HGRN_HEADS):
            lanes = pl.ds(h * HD_B, HD_B)
            cut = lambda r: r.at[:, lanes]
            one_head(*[cut(r) for r in refs[:5]], refs[5], cut(refs[6]), refs[7].at[h], cut(refs[8]),
                     *[cut(r) for r in refs[9:14]], dgn_ref, *[r.at[h] for r in refs[15:]])

    def one_head(zq_ref, zf_ref, xi_ref, zg_ref, lbl_ref, gn_ref, o_ref, st_ref, dout_ref,
                 dzq_ref, dzf_ref, dxi_ref, dzg_ref, dl0_ref, dgn_ref, dst_ref, q_s, k_s, b_s, do_s, dq_s, dk_s, di_s):
        lb, sig, f, sq = _hgrn_gates(zq_ref, zf_ref, lbl_ref, q_s, k_s, b_s)
        q, k, b = q_s[...], k_s[...], b_s[...]
        zg, o, dout = zg_ref[...], o_ref[...], dout_ref[...]
        sg = jax.nn.sigmoid(zg)
        rr = lax.rsqrt(jnp.mean(o * o, axis=-1, keepdims=True) + EPS)
        on = o * rr
        gn = gn_ref[...]
        dzg_ref[...] = (dout * on * gn * sg * (1.0 + zg * (1.0 - sg))).astype(dzg_ref.dtype)
        d_on = dout * zg * sg
        dgn_ref[...] += _colsum(d_on * on)
        d_on = d_on * gn
        do = rr * (d_on - on * jnp.mean(d_on * on, axis=-1, keepdims=True))
        do_s[...] = do
        dob = do.astype(BF16)
        st, dst = st_ref[...], dst_ref[...]
        b_last = b_s[CHUNK - 1:CHUNK, :]
        eb, e_last, k_dec = jnp.exp(b), jnp.exp(b_last), jnp.exp(b_last - b)
        qt, kd = q * eb, k * k_dec
        dstb = dst.astype(BF16)
        xib = xi_ref[...].astype(BF16)
        d_kd = _dot(xib, dstb)
        dq_s[...] = _dot(dob, st.astype(BF16)) * eb
        dk_s[...] = d_kd * k_dec
        di_s[...] = _dot(kd.astype(BF16), dstb, NT)
        d_b_last = e_last * _colsum(st * dst) + _colsum(d_kd * kd)
        dst_ref[...] = _dot(dob, qt.astype(BF16), TN) + dst * e_last
        for i in range(CHUNK // SUB):
            rows = _sub_rows(i)
            q_i, b_i, do_i = q_s[rows, :], b_s[rows, :], do_s[rows, :]
            dq_i = jnp.zeros((SUB, HD_B), F32)
            if i:
                past = pl.ds(0, i * SUB)
                b_ref = b_s[i * SUB - 1:i * SUB, :]
                e_q, e_k = jnp.exp(b_i - b_ref), jnp.exp(b_ref - b_s[past, :])
                qs, ks = (q_i * e_q).astype(BF16), (k_s[past, :] * e_k).astype(BF16)
                xi_p, do_b = xi_ref[past, :].astype(BF16), do_i.astype(BF16)
                di_s[past, :] += _dot(_dot(ks, qs, NT).astype(BF16), do_b)
                dq_i += _dot(_dot(do_b, xi_p, NT).astype(BF16), ks) * e_q
                dk_s[past, :] += _dot(_dot(xi_p, do_b, NT).astype(BF16), qs) * e_k
            for s in range(SUB):
                row = pl.ds(i * SUB + s, 1)
                k_row, i_row = k_s[row, :], xi_ref[row, :]
                e = _decay_from(b_i, b_s[row, :], s)
                w = q_i * e
                a_col = jnp.sum(w * k_row, axis=-1, keepdims=True)
                da_col = jnp.sum(do_i * i_row, axis=-1, keepdims=True)
                di_s[row, :] += _colsum(a_col * do_i)
                dq_i += da_col * e * k_row
                dk_s[row, :] += _colsum(da_col * w)
            dq_s[rows, :] += dq_i
        dq, dk = dq_s[...], dk_s[...]
        db = q * dq - k * dk
        is_last = lax.broadcasted_iota(jnp.int32, (CHUNK, HD_B), 0) == CHUNK - 1
        db = db + jnp.where(is_last, d_b_last, 0.0)
        df = _dot(_tri(CHUNK, False), db, precision=HIGHEST) / f - dk
        dzf_ref[...] = (df * (1.0 - lb) * sig * (1.0 - sig)).astype(dzf_ref.dtype)
        dl0_ref[...] += _colsum(df * (1.0 - sig)) * (lb * (1.0 - lb))
        zq = zq_ref[...]
        dzq_ref[...] = (dq * sq * (1.0 + zq * (1.0 - sq))).astype(dzq_ref.dtype)
        dxi_ref[...] = di_s[...].astype(dxi_ref.dtype)

    rev = lambda n: nC - 1 - n
    col = lambda part: pl.BlockSpec((CHUNK, wide), lambda g, n: (rev(n), col0 + part * G + g))
    blk = pl.BlockSpec((CHUNK, wide), lambda g, n: (rev(n), g))
    tile = pltpu.VMEM((HGRN_HEADS, CHUNK, HD_B), F32)
    out_big = jax.ShapeDtypeStruct((T, W), BF16)
    return _call(
        body, rider, name="hgrn_bwd", grid=(G, nC),
        in_specs=[col(0), col(1), col(2), col(3), pl.BlockSpec((2, wide), lambda g, n: (0, g)),
                  pl.BlockSpec((1, HD_B), lambda g, n: (0, 0)), blk,
                  pl.BlockSpec((HGRN_HEADS, None, HD_B, HD_B), lambda g, n: (g, rev(n), 0, 0)),
                  pl.BlockSpec((CHUNK, wide), lambda g, n: (rev(n), dcol0 + g))],
        out_specs=[blk, blk, blk, blk, pl.BlockSpec((1, wide), lambda g, n: (0, g)),
                   pl.BlockSpec((1, HD_B), lambda g, n: (0, 0))],
        out_shape=[out_big, out_big, out_big, out_big, jax.ShapeDtypeStruct((1, W), F32),
                   jax.ShapeDtypeStruct((1, HD_B), F32)],
        scratch_shapes=[pltpu.VMEM((HGRN_HEADS, HD_B, HD_B), F32)] + [tile] * 7,
        compiler_params=_params(("arbitrary", "arbitrary")),
        operands=(proj, proj, proj, proj, lb_logits, gnorm_g, o_b, st_all, dmixin))


def _adamw_math(g, w, m, v):
    m = B1 * m + (1.0 - B1) * g
    v = B2 * v + (1.0 - B2) * (g * g)
    m_hat = m / (1.0 - B1 ** STEP)
    v_hat = v / (1.0 - B2 ** STEP)
    return -LR * (m_hat / (jnp.sqrt(v_hat) + ADAM_EPS) + WD * w), m, v


def _adamw(g, w, m, v, name):
    R, C = g.shape
    tr = _row_tile(R, C)

    def body(g_ref, w_ref, m_ref, v_ref, go_ref, d_ref, mo_ref, vo_ref):
        g = g_ref[...]
        go_ref[...] = g
        d_ref[...], mo_ref[...], vo_ref[...] = _adamw_math(g, w_ref[...], m_ref[...], v_ref[...])

    blk = pl.BlockSpec((tr, C), lambda i: (i, 0))
    return pl.pallas_call(
        body, name=name, grid=(R // tr,), in_specs=[blk] * 4, out_specs=[blk] * 4,
        out_shape=[jax.ShapeDtypeStruct((R, C), F32)] * 4, compiler_params=_params(("parallel",), 40),
    )(g, w, m, v)


def _sum_pair(g_full, from_sibling, sel, name):
    Q, K, Ns = g_full.shape
    kh = K // 2
    tr = _row_tile(kh, Ns)
    nh = kh // tr

    def body(sel_ref, a_ref, b_ref, o_ref):
        o_ref[...] = (a_ref[...].astype(F32) + b_ref[...].astype(F32)).astype(o_ref.dtype)

    return pl.pallas_call(
        body, name=name,
        grid_spec=pltpu.PrefetchScalarGridSpec(
            num_scalar_prefetch=1, grid=(Q, nh),
            in_specs=[pl.BlockSpec((None, tr, Ns), lambda q, i, sel: (q, sel[1] * nh + i, 0)),
                      pl.BlockSpec((None, tr, Ns), lambda q, i, sel: (q, i, 0))],
            out_specs=pl.BlockSpec((None, tr, Ns), lambda q, i, sel: (q, i, 0))),
        out_shape=jax.ShapeDtypeStruct((Q, kh, Ns), BF16), compiler_params=_params(("parallel", "parallel")),
    )(sel, g_full, from_sibling)


def _sum_chips(pair_sum, from_chips, sel, name):
    Q, kh, Ns = pair_sum.shape
    tr = _row_tile(kh, Ns)
    nh = kh // tr

    def body(sel_ref, a_ref, b0_ref, b1_ref, b2_ref, o_ref):
        up = lambda r: r[...].astype(F32)
        o_ref[...] = ((up(a_ref) + up(b0_ref)) + up(b1_ref)) + up(b2_ref)

    recv = lambda k: pl.BlockSpec((None, tr, Ns), lambda i, sel: (k, i, 0))
    return pl.pallas_call(
        body, name=name,
        grid_spec=pltpu.PrefetchScalarGridSpec(
            num_scalar_prefetch=1, grid=(nh,),
            in_specs=[pl.BlockSpec((None, tr, Ns), lambda i, sel: (sel[0], i, 0)), recv(0), recv(1), recv(2)],
            out_specs=pl.BlockSpec((tr, Ns), lambda i, sel: (sel[1] * nh + i, 0))),
        out_shape=jax.ShapeDtypeStruct((2 * kh, Ns), F32), compiler_params=_params(("parallel",)),
    )(sel, pair_sum, from_chips, from_chips, from_chips)


def _gather_small(v, name):
    R, L = v.shape

    def body(v_ref, out_ref, send_sems, recv_sems):
        x, y, c = _place()
        me = 4 * x + 2 * y + c
        out_ref[me] = v_ref[...]
        peers = [(_flip(x, k >> 2 & 1), _flip(y, k >> 1 & 1), _flip(c, k & 1)) for k in range(1, N_DEV)]

        def copy(k, row, to):
            return pltpu.make_async_remote_copy(src_ref=v_ref, dst_ref=out_ref.at[row], send_sem=send_sems.at[k],
                                                recv_sem=recv_sems.at[k], device_id=to, device_id_type=MESH)

        sends = [copy(k, me, peer) for k, peer in enumerate(peers)]
        for cp in sends:
            cp.start()
        for k, (px, py, pc) in enumerate(peers):
            copy(k, 4 * px + 2 * py + pc, (x, y, c)).wait_recv()
        for cp in sends:
            cp.wait_send()

    vmem = pl.BlockSpec(memory_space=pltpu.VMEM)
    return pl.pallas_call(
        body, name=name, in_specs=[vmem], out_specs=vmem, out_shape=jax.ShapeDtypeStruct((N_DEV, R, L), F32),
        scratch_shapes=[pltpu.SemaphoreType.DMA((N_DEV - 1,)), pltpu.SemaphoreType.DMA((N_DEV - 1,))],
    )(v)


def _silu(v):
    return v * jax.nn.sigmoid(v)


def _ada_fwd(c_all, w_ada, tn=512):
    M, D = c_all.shape
    Ns = w_ada.shape[1]

    def body(c_ref, w_ref, o_ref):
        o_ref[...] = _dot(_silu(c_ref[...]).astype(BF16), w_ref[...].astype(BF16))

    return pl.pallas_call(
        body, name="ada_fwd", grid=(Ns // tn,),
        in_specs=[pl.BlockSpec((M, D), lambda j: (0, 0)), pl.BlockSpec((D, tn), lambda j: (0, j))],
        out_specs=pl.BlockSpec((M, tn), lambda j: (0, j)), out_shape=jax.ShapeDtypeStruct((M, Ns), F32),
        compiler_params=_params(("parallel",)),
    )(c_all, w_ada)


def _ada_bwd(c_all, dmod, w, m, v, tk=256, tn=1536):
    M, D = c_all.shape
    Ns = dmod.shape[1]

    def body(c_ref, d_ref, w_ref, m_ref, v_ref, g_ref, dl_ref, mo_ref, vo_ref):
        g = _dot(_silu(c_ref[...]).astype(BF16), d_ref[...].astype(BF16), TN)
        g_ref[...] = g
        dl_ref[...], mo_ref[...], vo_ref[...] = _adamw_math(g, w_ref[...], m_ref[...], v_ref[...])

    blk = pl.BlockSpec((tk, tn), lambda i, j: (i, j))
    return pl.pallas_call(
        body, name="ada_bwd", grid=(D // tk, Ns // tn),
        in_specs=[pl.BlockSpec((M, tk), lambda i, j: (0, i)), pl.BlockSpec((M, tn), lambda i, j: (0, j)), blk, blk, blk],
        out_specs=[blk] * 4, out_shape=[jax.ShapeDtypeStruct((D, Ns), F32)] * 4,
        compiler_params=_params(("parallel", "parallel"), 40),
    )(c_all, dmod, w, m, v)


def _small_update(g_all, w, m, v):
    R, L = w.shape

    def body(g_ref, w_ref, m_ref, v_ref, go_ref, d_ref, mo_ref, vo_ref):
        g = g_ref[0]
        for d in range(1, N_DEV):
            g = g + g_ref[d]
        go_ref[...] = g
        d_ref[...], mo_ref[...], vo_ref[...] = _adamw_math(g, w_ref[...], m_ref[...], v_ref[...])

    return pl.pallas_call(body, name="small_update", out_shape=[jax.ShapeDtypeStruct((R, L), F32)] * 4)(g_all, w, m, v)


def _pack(parts, rows):
    flat = jnp.concatenate([p.reshape(-1) for p in parts])
    return jnp.pad(flat, (0, rows * 128 - flat.shape[0])).reshape(rows, 128)


def _unpack(packed, shapes):
    flat, out, at = packed.reshape(-1), [], 0
    for shp in shapes:
        size = 1
        for d in shp:
            size *= d
        out.append(flat[at:at + size].reshape(shp))
        at += size
    return out


def _layer(x, tgt, mod, wts, rel_bias, attn_norm_g, lb_logits, gnorm_g, ln1_g, ln1_b, ln2_g, ln2_b, place=None):
    T, D = x.shape
    aw = attn_norm_g.shape[1]
    shift1, scale1, gate1, shift2, scale2, gate2 = [mod[i:i + 1] for i in range(6)]

    def gather(n, rows=None, into=None):
        return None if place is None else _gather_rider(wts[n], rows, None if into is None else into[0])

    def gathered(n, rode):
        return wts[n] if place is None else lax.dynamic_update_index_in_dim(rode[0], wts[n], place[0], 0)

    def blocks(g):
        return g.reshape(N_CHIPS, -1, g.shape[2])

    def to_sibling(g):
        return None if place is None else _pair_rider(g)

    def pair_sum(n, g, rode=None):
        if place is None:
            return g
        rode = _alone(_pair_rider(g), n + "_send_pair") if rode is None else rode
        return _sum_pair(g, rode[0], place[1], n + "_sum_pair")

    def to_chips(p, rows=None, into=None):
        return None if place is None else _chips_rider(p, rows, None if into is None else into[0])

    def summed(n, p, rode):
        return p if place is None else _sum_chips(p, rode[0], place[1], n + "_sum_chips")

    def to_both(block):
        return None if place is None else _share_rider(block)

    def carrying(mm, *args, rider, **kw):
        return mm(*args, rider=rider, **kw) if rider is not None else (mm(*args, **kw), None)

    def to_sibling_acts(a, b):
        return None if place is None else _acts_rider(a, b)

    def pair_grad(name, a, b, tn, rider, arrived=None):
        if place is None:
            return _mm_tn(a, b, q=N_CHIPS, tk=512, tn=tn, tt=T, name=name), None
        kh = a.shape[1] // 2
        mine = lax.dynamic_slice_in_dim(a, place[1][1] * kh, kh, axis=1)
        part, rode = carrying(_mm_tn, mine, b, q=N_CHIPS, tk=512, tn=tn, tt=T, name=name + "_own",
                              rider=_join(None if arrived else _acts_rider(a, b), rider))
        (a_sib, b_sib), rode = arrived or rode[:2], rode if arrived else rode[2:]
        return _mm_tn_add(a_sib, b_sib, part, tk=512, tn=tn, name=name + "_sib"), rode

    w_in = gathered("w_in", None if place is None else [wts["w_in_gathered"]])
    h1 = _pre_mixer(x, scale1, shift1)
    n_qkv = 3 * aw // 256
    kh_o, kh_f = wts["w_o"].shape[-2] // 2, wts["w_ffn_in"].shape[-2] // 2
    o_cut, f_cuts = 3 * kh_o // 8, (7 * kh_f // 16, 7 * kh_f // 8)
    qkv, rode = carrying(_mm_nn, h1, w_in, tm=ZPAD, tn=256, tk=D, name="proj_qkv", cols=(0, n_qkv), o_dtype=BF16,
                         pad_rows=ZPAD, rider=gather("w_o", (0, o_cut)))
    proj, rode = carrying(_mm_nn, h1, w_in, tm=2048, tn=256, tk=D, name="proj_rec",
                          cols=(n_qkv, N_CHIPS * w_in.shape[2] // 256), rider=gather("w_o", (o_cut, kh_o - o_cut), rode))
    w_o3 = gathered("w_o", rode).reshape(1, D, D)
    bias = _bias_band(rel_bias)
    (mix_a, probs), rode = _attn_fwd(qkv, bias, attn_norm_g, rider=gather("w_ffn_in", (0, f_cuts[0])))
    (mix_b, o_b, st_all), rode = _hgrn_fwd(
        proj, lb_logits, gnorm_g, rider=gather("w_ffn_in", (f_cuts[0], f_cuts[1] - f_cuts[0]), rode))
    mixin = jnp.concatenate([mix_a, mix_b], axis=1)
    mix = _mm_nn(mixin, w_o3, tm=1024, tn=512, tk=D, name="mix_out")
    if place is None:
        x1, h2 = _post_mixer(mix, x, gate1, ln1_g, ln1_b, scale2, shift2)
    else:
        (x1, h2), rode = _post_mixer(mix, x, gate1, ln1_g, ln1_b, scale2, shift2,
                                     rider=gather("w_ffn_in", (f_cuts[1], kh_f - f_cuts[1]), rode))
    w_ffn_in = gathered("w_ffn_in", rode)
    (gate, up, act), rode = _ffn_in_swiglu(h2, w_ffn_in, tm=2048, tn=256, rider=gather("w_ffn_out"))
    w_out3 = gathered("w_ffn_out", rode)
    w_out3 = w_out3.reshape(1, -1, w_out3.shape[2])
    d_ff = w_out3.shape[1]
    f = _mm_nn(act, w_out3, tm=1024, tn=512, tk=d_ff, name="ffn_out")
    du2, df, acc2 = _loss_head(f, x1, tgt, gate2, ln2_g, ln2_b)
    loss = (0.5 / D) * jnp.sum(acc2[3])
    g = blocks(_mm_tn(act, df, q=1, tk=512, tn=1024, tt=T, name="g_ffn_out"))
    d_gate_up, rode = _d_act_swiglu(df, w_out3, gate, up, tm=1024, to=512, rider=to_sibling(g))
    p_out = pair_sum("w_ffn_out", g, rode)
    dff = jnp.concatenate(d_gate_up, axis=1)
    cut = 21 * p_out.shape[1] // 44
    dh2, rode = carrying(_mm_nt, dff, w_ffn_in, tm=1024, to=1024, tn=w_ffn_in.shape[2], name="d_h2",
                         rider=_join(to_chips(p_out, (0, cut)), to_sibling_acts(h2, dff)))
    p_fin, rode = pair_grad("g_ffn_in", h2, dff, w_ffn_in.shape[2] // 2,
                            to_chips(p_out, (cut, p_out.shape[1] - cut), rode), arrived=rode and rode[1:])
    g_ffn_out = summed("w_ffn_out", p_out, rode)
    if place is None:
        du1, dmix, acc1 = _mid_bwd(dh2, du2, x1, mix, x, gate1, ln1_g, scale2)
    else:
        (du1, dmix, acc1), (g_ffn_out,) = _mid_bwd(dh2, du2, x1, mix, x, gate1, ln1_g, scale2, rider=to_both(g_ffn_out))
    g = blocks(_mm_tn(mixin, dmix, q=1, tk=512, tn=1024, tt=T, name="g_o"))
    dmixin, rode = carrying(_mm_nt, dmix, w_o3, tm=1024, to=512, tn=D, name="d_mixin", rider=to_sibling(g))
    p_o = pair_sum("w_o", g, rode)
    cut = p_fin.shape[1] // 2
    (dq, dk, dv, dbias, dgain), rode = _attn_bwd(qkv, probs, attn_norm_g, dmixin, rider=to_chips(p_fin, (0, cut)))
    (dzq, dzf, dxi, dzg, dl0, dgn), rode = _hgrn_bwd(
        proj, lb_logits, gnorm_g, o_b, st_all, dmixin,
        rider=_join(to_chips(p_fin, (cut, p_fin.shape[1] - cut), rode), to_chips(p_o)))
    g_ffn_in, g_o = summed("w_ffn_in", p_fin, rode[:1]), summed("w_o", p_o, rode[1:])
    dproj = jnp.concatenate([dq, dk[KPAD:].astype(BF16), dv[KPAD:].astype(BF16), dzq, dzf, dxi, dzg], axis=1)
    p_in, rode = pair_grad("g_in", h1, dproj, w_in.shape[2] // 2, _join(to_both(g_ffn_in), to_both(g_o)))
    if place is not None:
        g_ffn_in, g_o = rode
    cut = 3 * p_in.shape[1] // 4
    dh1, rode = carrying(_mm_nt, dproj, w_in, tm=1024, to=1024, tn=w_in.shape[2], name="d_h1",
                         rider=to_chips(p_in, (0, cut)))
    if place is None:
        (grad_x, acc0), g_in = _first_bwd(dh1, du1, x, scale1), p_in
    else:
        (grad_x, acc0), rode = _first_bwd(dh1, du1, x, scale1, rider=to_chips(p_in, (cut, p_in.shape[1] - cut), rode))
        g_in, = _alone(to_both(summed("w_in", p_in, rode)), "w_in_share")
    dmod = jnp.concatenate([acc0[1:2], acc0[0:1], acc1[4:5], acc1[1:2], acc1[0:1], acc2[2:3]], axis=0)
    small = dict(rel_bias=_bias_band_grad(dbias), attn_norm_g=dgain,
                 lb_logits=jnp.concatenate([dl0, -dl0], axis=0), gnorm_g=dgn,
                 ln1_g=acc1[2:3], ln1_b=acc1[3:4], ln2_g=acc2[0:1], ln2_b=acc2[1:2])
    return loss, grad_x, dict(w_in=g_in, w_o=g_o, w_ffn_in=g_ffn_in, w_ffn_out=g_ffn_out), dmod, small


SMALL = ("rel_bias", "attn_norm_g", "lb_logits", "gnorm_g", "ln1_g", "ln1_b", "ln2_g", "ln2_b")
SMALL_ROWS = 256


def kernel(x, c, w_ada, b_ada, w_in, rel_bias, attn_norm_g, lb_logits, gnorm_g, w_o, ln1_g, ln1_b, w_ffn_in, w_ffn_out, ln2_g, ln2_b, loss_target, m_w_ada, m_b_ada, m_w_in, m_rel_bias, m_attn_norm_g, m_lb_logits, m_gnorm_g, m_w_o, m_ln1_g, m_ln1_b, m_w_ffn_in, m_w_ffn_out, m_ln2_g, m_ln2_b, v_w_ada, v_b_ada, v_w_in, v_rel_bias, v_attn_norm_g, v_lb_logits, v_gnorm_g, v_w_o, v_ln1_g, v_ln1_b, v_w_ffn_in, v_w_ffn_out, v_ln2_g, v_ln2_b):
    mx, my, mc = _place()
    me = 4 * mx + 2 * my + mc
    chip = 2 * mx + my
    sel = jnp.stack([chip, mc]).astype(jnp.int32)
    D = x.shape[2]
    ns_ada = w_ada.shape[2]

    big = dict(w_in=(w_in, m_w_in, v_w_in), w_o=(w_o, m_w_o, v_w_o), w_ffn_in=(w_ffn_in, m_w_ffn_in, v_w_ffn_in),
               w_ffn_out=(w_ffn_out, m_w_ffn_out, v_w_ffn_out))
    shards = dict(w_in=w_in[0].astype(BF16))
    kh, rode, at = shards["w_in"].shape[0] // 2, None, 0
    for n, part in (("w_ffn_in", 19), ("w_ffn_out", 9), ("w_o", 4)):
        rows = (at, part * kh // 32)
        (shards[n],), rode = _to_bf16(big[n][0][0], "cast_" + n,
                                      _gather_rider(shards["w_in"], rows, None if rode is None else rode[0]))
        at += rows[1]
    shards["w_in_gathered"] = rode[0]

    c_all = _gather_small(c.reshape(D // 128, 128), "gather_c").reshape(N_DEV, D)
    c_all = jnp.pad(c_all, ((0, 16 - N_DEV), (0, 0)))
    mod_cols = _ada_fwd(c_all, w_ada[0])[:N_DEV]
    mod_all = _gather_small(mod_cols.reshape(-1, 128), "gather_mod").reshape(N_DEV, N_DEV, ns_ada)
    mod = lax.dynamic_index_in_dim(mod_all[::2], me, axis=1, keepdims=False)
    mod = (mod.reshape(1, -1) + b_ada).reshape(6, D)

    loss, grad_x, g_big, dmod, g_small = _layer(
        x[0], loss_target[0], mod, shards, rel_bias[0], attn_norm_g, lb_logits, gnorm_g, ln1_g, ln1_b, ln2_g, ln2_b,
        place=(chip, sel))

    grads, deltas, new_m, new_v = {}, {}, {}, {}
    for n, (w, m, v) in big.items():
        g, d, mo, vo = _adamw(g_big[n], w[0], m[0], v[0], "adamw_" + n)
        grads[n], deltas[n], new_m[n], new_v[n] = g[None], d[None], mo[None], vo[None]

    small_in = dict(rel_bias=(rel_bias, m_rel_bias, v_rel_bias), attn_norm_g=(attn_norm_g, m_attn_norm_g, v_attn_norm_g),
                    lb_logits=(lb_logits, m_lb_logits, v_lb_logits), gnorm_g=(gnorm_g, m_gnorm_g, v_gnorm_g),
                    ln1_g=(ln1_g, m_ln1_g, v_ln1_g), ln1_b=(ln1_b, m_ln1_b, v_ln1_b), ln2_g=(ln2_g, m_ln2_g, v_ln2_g),
                    ln2_b=(ln2_b, m_ln2_b, v_ln2_b))
    g_all = _gather_small(_pack([dmod] + [g_small[n] for n in SMALL] + [loss], SMALL_ROWS), "gather_small")
    packed = [_pack([t] + [small_in[n][i] for n in SMALL] + [jnp.zeros((), F32)], SMALL_ROWS)
              for i, t in enumerate((b_ada, m_b_ada, v_b_ada))]
    shapes = [b_ada.shape] + [small_in[n][0].shape for n in SMALL] + [()]
    outs = [_unpack(o, shapes) for o in _small_update(g_all, *packed)]
    loss = outs[0][-1]
    for i, n in enumerate(("b_ada",) + SMALL):
        grads[n], deltas[n], new_m[n], new_v[n] = outs[0][i], outs[1][i], outs[2][i], outs[3][i]

    dmod_all = g_all[:, :6 * D // 128].reshape(N_DEV, 6 * D)
    dmod_cols = lax.dynamic_slice_in_dim(dmod_all, chip * ns_ada, ns_ada, axis=1)
    dmod_cols = jnp.pad(dmod_cols, ((0, 16 - N_DEV), (0, 0)))
    g, d, mo, vo = _ada_bwd(c_all, dmod_cols, w_ada[0], m_w_ada[0], v_w_ada[0])
    grads["w_ada"], deltas["w_ada"], new_m["w_ada"], new_v["w_ada"] = g[None], d[None], mo[None], vo[None]

    order = ("w_ada", "b_ada", "w_in", "rel_bias", "attn_norm_g", "lb_logits", "gnorm_g", "w_o", "ln1_g", "ln1_b",
             "w_ffn_in", "w_ffn_out", "ln2_g", "ln2_b")
    return (loss, grad_x[None], *[grads[n] for n in order], *[deltas[n] for n in order],
            *[new_m[n] for n in order], *[new_v[n] for n in order])
```

```python
import numpy as np
import jax
import jax.numpy as jnp
from jax import lax
from jax.experimental import pallas as pl
from jax.experimental.pallas import tpu as pltpu

F32 = jnp.float32
BF16 = jnp.bfloat16
MESH = pl.DeviceIdType.MESH
HIGHEST = lax.Precision.HIGHEST

CHUNK = 64
N_PAST = 8
QG = 4
QROWS = QG * CHUNK
KPAD = N_PAST * CHUNK
ZPAD = 2 * KPAD
UNION = (QG + N_PAST) * CHUNK
BAND = (N_PAST + 1) * CHUNK
HD_A = 64
HD_B = 128
SUB = 16
HGRN_HEADS = 8
MAX_REL = 256
EPS = 1e-5
ALPHA = 2.0 ** 0.25
LR, B1, B2, ADAM_EPS, WD, STEP = 1e-3, 0.9, 0.999, 1e-8, 0.01, 10
N_CHIPS = 4
N_DEV = 8
NEG = -1e30
TILE_BYTES = 3 << 19

NN = ((1,), (0,))
NT = ((1,), (1,))
TN = ((0,), (0,))


def _dot(a, b, dims=NN, precision=None):
    return lax.dot_general(a, b, (dims, ((), ())), preferred_element_type=F32, precision=precision)


def _params(sem=None, vmem_mb=None, **kw):
    return pltpu.CompilerParams(dimension_semantics=sem,
                                vmem_limit_bytes=None if vmem_mb is None else vmem_mb << 20, **kw)


def _row_tile(rows, cols):
    for cand in (512, 256, 128, 64, 32, 16, 8):
        if rows % cand == 0 and cand * cols * 4 <= TILE_BYTES:
            return cand
    raise ValueError((rows, cols))


def _place():
    return lax.axis_index("x"), lax.axis_index("y"), lax.axis_index("c")


def _flip(v, bit):
    return 1 - v if bit else v


ANY = pl.BlockSpec(memory_space=pl.ANY)
CHIP_FLIPS = ((1, 0), (0, 1), (1, 1))


class _Rider:
    def __init__(self, operands, out_shape, n_sems, start, finish, aliases=None):
        self.operands, self.out_shape, self.n_sems, self.start, self.finish = operands, out_shape, n_sems, start, finish
        self.aliases = aliases or {}


def _call(body, rider, *, name, grid, in_specs, out_specs, out_shape, scratch_shapes, compiler_params, operands):
    if rider is None:
        outs = pl.pallas_call(body, name=name, grid=grid, in_specs=in_specs, out_specs=out_specs, out_shape=out_shape,
                              scratch_shapes=scratch_shapes, compiler_params=compiler_params)(*operands)
        return list(outs), []
    n_in, n_out, n_sc = len(in_specs), len(out_specs), len(scratch_shapes)
    r_in, r_out = len(rider.operands), len(rider.out_shape)

    def carried(*refs):
        refs = list(refs)
        cuts = [n_in, r_in, n_out, r_out, n_sc]
        ins, r_ins, outs, r_outs, scratch = [[refs.pop(0) for _ in range(n)] for n in cuts]
        first, last = None, None
        for axis, size in enumerate(grid):
            i = pl.program_id(axis)
            first = (i == 0) if first is None else first & (i == 0)
            last = (i == size - 1) if last is None else last & (i == size - 1)

        @pl.when(first)
        def _():
            rider.start(r_ins, r_outs, *refs)

        body(*ins, *outs, *scratch)

        @pl.when(last)
        def _():
            rider.finish(r_ins, r_outs, *refs)

    sems = [pltpu.SemaphoreType.DMA((rider.n_sems,)), pltpu.SemaphoreType.DMA((rider.n_sems,))]
    outs = pl.pallas_call(carried, name=name, grid=grid, in_specs=list(in_specs) + [ANY] * r_in,
                          out_specs=list(out_specs) + [ANY] * r_out, out_shape=list(out_shape) + rider.out_shape,
                          scratch_shapes=list(scratch_shapes) + sems, compiler_params=compiler_params,
                          input_output_aliases={n_in + i: n_out + o for i, o in rider.aliases.items()},
                          )(*operands, *rider.operands)
    return list(outs[:n_out]), list(outs[n_out:])


def _alone(rider, name):
    def body(*refs):
        ins, outs, sems = refs[:len(rider.operands)], refs[len(rider.operands):-2], refs[-2:]
        rider.start(ins, outs, *sems)
        rider.finish(ins, outs, *sems)

    return pl.pallas_call(
        body, name=name, in_specs=[ANY] * len(rider.operands), out_specs=[ANY] * len(rider.out_shape),
        out_shape=rider.out_shape, input_output_aliases=rider.aliases,
        scratch_shapes=[pltpu.SemaphoreType.DMA((rider.n_sems,)), pltpu.SemaphoreType.DMA((rider.n_sems,))],
    )(*rider.operands)


class _Sems:
    def __init__(self, sems, base):
        self.sems, self.base = sems, base

    @property
    def at(self):
        return self

    def __getitem__(self, k):
        return self.sems.at[self.base + k]


def _join(*riders):
    riders = [r for r in riders if r is not None]
    if len(riders) < 2:
        return riders[0] if riders else None

    def parts(ins, outs, send_sems, recv_sems):
        i = o = s = 0
        for r in riders:
            ni, no = len(r.operands), len(r.out_shape)
            yield r, ins[i:i + ni], outs[o:o + no], _Sems(send_sems, s), _Sems(recv_sems, s)
            i, o, s = i + ni, o + no, s + r.n_sems

    def start(*refs):
        for r, *args in parts(*refs):
            r.start(*args)

    def finish(*refs):
        for r, *args in parts(*refs):
            r.finish(*args)

    aliases, i, o = {}, 0, 0
    for r in riders:
        aliases.update({i + a: o + b for a, b in r.aliases.items()})
        i, o = i + len(r.operands), o + len(r.out_shape)
    return _Rider([a for r in riders for a in r.operands], [s for r in riders for s in r.out_shape],
                  sum(r.n_sems for r in riders), start, finish, aliases)


def _gather_rider(shard, rows=None, into=None, before=None, last=True):
    K, Ns = shard.shape
    kh = K // 2
    rows = rows or (0, kh)

    def copies(w_ref, out_ref, send_sems, recv_sems):
        x, y, c = _place()
        chips = [(_flip(x, fx), _flip(y, fy)) for fx, fy in CHIP_FLIPS]

        def half(chip, which, part):
            return out_ref.at[2 * chip[0] + chip[1], pl.ds(which * kh + part[0], part[1]), :]

        def copy(k, dst, to, src=None):
            return pltpu.make_async_remote_copy(src_ref=dst if src is None else src, dst_ref=dst,
                                                send_sem=send_sems.at[k], recv_sem=recv_sems.at[k],
                                                device_id=to, device_id_type=MESH)

        def first():
            return [copy(j, half((x, y), c, rows), (*chip, c), src=w_ref.at[pl.ds(c * kh + rows[0], rows[1]), :])
                    for j, chip in enumerate(chips)]

        def onward(base, part):
            return [copy(base + j, half(chip, c, part), (x, y, 1 - c)) for j, chip in enumerate(chips)]

        def arriving(base, which, part):
            return [copy(base + j, half(chip, which, part), (x, y, c)) for j, chip in enumerate(chips)]

        return c, first, onward, arriving

    def start(ins, outs, send_sems, recv_sems):
        _, first, onward, _ = copies(ins[0], outs[0], send_sems, recv_sems)
        for cp in first() + (onward(3, before) if before else []):
            cp.start()

    def finish(ins, outs, send_sems, recv_sems):
        c, first, onward, arriving = copies(ins[0], outs[0], send_sems, recv_sems)
        sent = first() + (onward(3, before) if before else [])
        passed = onward(6, rows) if last else [None] * 3
        for arrived, cp in zip(arriving(0, c, rows), passed):
            arrived.wait_recv()
            if last:
                cp.start()
        for arrived in (arriving(3, 1 - c, before) if before else []) + (arriving(6, 1 - c, rows) if last else []):
            arrived.wait_recv()
        for cp in sent + (passed if last else []):
            cp.wait_send()

    full = jax.ShapeDtypeStruct((N_CHIPS, K, Ns), shard.dtype)
    if into is None:
        return _Rider([shard], [full], 9, start, finish)
    return _Rider([shard, into], [full], 9, start, finish, aliases={1: 0})


def _pair_rider(g_full):
    Q, K, Ns = g_full.shape
    kh = K // 2

    def copy(g_ref, got_ref, send_sems, recv_sems):
        x, y, c = _place()
        return pltpu.make_async_remote_copy(src_ref=g_ref.at[:, pl.ds((1 - c) * kh, kh), :], dst_ref=got_ref,
                                            send_sem=send_sems.at[0], recv_sem=recv_sems.at[0],
                                            device_id=(x, y, 1 - c), device_id_type=MESH)

    def start(ins, outs, send_sems, recv_sems):
        copy(ins[0], outs[0], send_sems, recv_sems).start()

    def finish(ins, outs, send_sems, recv_sems):
        copy(ins[0], outs[0], send_sems, recv_sems).wait()

    return _Rider([g_full], [jax.ShapeDtypeStruct((Q, kh, Ns), g_full.dtype)], 1, start, finish)


def _acts_rider(a, b):
    T, K = a.shape
    kh = K // 2

    def copies(ins, outs, send_sems, recv_sems):
        x, y, c = _place()
        pair = [(ins[0].at[:, pl.ds((1 - c) * kh, kh)], outs[0]), (ins[1], outs[1])]
        return [pltpu.make_async_remote_copy(src_ref=src, dst_ref=dst, send_sem=send_sems.at[k], recv_sem=recv_sems.at[k],
                                             device_id=(x, y, 1 - c), device_id_type=MESH)
                for k, (src, dst) in enumerate(pair)]

    def start(*refs):
        for cp in copies(*refs):
            cp.start()

    def finish(*refs):
        for cp in copies(*refs):
            cp.wait()

    return _Rider([a, b], [jax.ShapeDtypeStruct((T, kh), a.dtype), jax.ShapeDtypeStruct(b.shape, b.dtype)], 2,
                  start, finish)


def _share_rider(block):
    K, Ns = block.shape
    kh = K // 2

    def halves(out_ref):
        x, y, c = _place()
        return out_ref.at[pl.ds(c * kh, kh), :], out_ref.at[pl.ds((1 - c) * kh, kh), :], (x, y, 1 - c)

    def start(ins, outs, send_sems, recv_sems):
        mine, _, sibling = halves(outs[0])
        pltpu.make_async_remote_copy(src_ref=mine, dst_ref=mine, send_sem=send_sems.at[0], recv_sem=recv_sems.at[0],
                                     device_id=sibling, device_id_type=MESH).start()

    def finish(ins, outs, send_sems, recv_sems):
        mine, theirs, sibling = halves(outs[0])
        pltpu.make_async_remote_copy(src_ref=theirs, dst_ref=theirs, send_sem=send_sems.at[0], recv_sem=recv_sems.at[0],
                                     device_id=sibling, device_id_type=MESH).wait_recv()
        pltpu.make_async_remote_copy(src_ref=mine, dst_ref=mine, send_sem=send_sems.at[0], recv_sem=recv_sems.at[0],
                                     device_id=sibling, device_id_type=MESH).wait_send()

    return _Rider([block], [jax.ShapeDtypeStruct((K, Ns), block.dtype)], 1, start, finish, aliases={0: 0})


def _chips_rider(pair_sum, rows=None, into=None):
    Q, kh, Ns = pair_sum.shape
    first_row, n_rows = rows or (0, kh)

    def copies(p_ref, got_ref, send_sems, recv_sems):
        x, y, c = _place()
        part = pl.ds(first_row, n_rows)
        out = []
        for j, (fx, fy) in enumerate(CHIP_FLIPS):
            px, py = _flip(x, fx), _flip(y, fy)
            out.append(pltpu.make_async_remote_copy(
                src_ref=p_ref.at[2 * px + py, part, :], dst_ref=got_ref.at[j, part, :], send_sem=send_sems.at[j],
                recv_sem=recv_sems.at[j], device_id=(px, py, c), device_id_type=MESH))
        return out

    def start(ins, outs, send_sems, recv_sems):
        for cp in copies(ins[0], outs[0], send_sems, recv_sems):
            cp.start()

    def finish(ins, outs, send_sems, recv_sems):
        sends = copies(ins[0], outs[0], send_sems, recv_sems)
        for cp in sends:
            cp.wait_recv()
        for cp in sends:
            cp.wait_send()

    got = jax.ShapeDtypeStruct((Q - 1, kh, Ns), pair_sum.dtype)
    if into is None:
        return _Rider([pair_sum], [got], 3, start, finish)
    return _Rider([pair_sum, into], [got], 3, start, finish, aliases={1: 0})


def _mm(a, b, *, grid, a_spec, b_spec, o_spec, o_shape, o_dtype, dims, acc_shape, name, rider=None, zero_rows=0,
        vmem_mb=48):
    nk = grid[2]

    def body(a_ref, b_ref, o_ref, *scratch):
        if zero_rows:
            @pl.when(pl.program_id(0) < zero_rows)
            def _():
                o_ref[...] = jnp.zeros_like(o_ref)

            @pl.when(pl.program_id(0) >= zero_rows)
            def _():
                o_ref[...] = _dot(a_ref[...], b_ref[...], dims).astype(o_ref.dtype)
            return
        part = _dot(a_ref[...], b_ref[...], dims)
        if nk == 1:
            o_ref[...] = part.astype(o_ref.dtype)
            return
        acc_ref, = scratch
        k = pl.program_id(2)

        @pl.when(k == 0)
        def _():
            acc_ref[...] = part

        @pl.when(k > 0)
        def _():
            acc_ref[...] += part

        @pl.when(k == nk - 1)
        def _():
            o_ref[...] = acc_ref[...].astype(o_ref.dtype)

    (out,), rode = _call(
        body, rider, name=name, grid=grid, in_specs=[a_spec, b_spec], out_specs=[o_spec],
        out_shape=[jax.ShapeDtypeStruct(o_shape, o_dtype)],
        scratch_shapes=[] if nk == 1 else [pltpu.VMEM(acc_shape, F32)],
        compiler_params=_params(("parallel", "parallel", "arbitrary") if rider is None else ("arbitrary",) * 3, vmem_mb),
        operands=(a, b))
    return out if rider is None else (out, rode)


def _mm_nn(a, w, *, tm, tn, tk, name, rider=None, cols=None, o_dtype=F32, pad_rows=0):
    T, K = a.shape
    Q, _, Ns = w.shape
    nbs = Ns // tn
    tm = min(tm, T)
    j0, j1 = cols or (0, Q * nbs)
    lead = pad_rows // tm
    return _mm(a, w, grid=(lead + T // tm, j1 - j0, K // tk),
               a_spec=pl.BlockSpec((tm, tk), lambda i, j, k: (jnp.maximum(i - lead, 0), k)),
               b_spec=pl.BlockSpec((None, tk, tn), lambda i, j, k: ((j + j0) // nbs, k, (j + j0) % nbs)),
               o_spec=pl.BlockSpec((tm, tn), lambda i, j, k: (i, j)),
               o_shape=(pad_rows + T, (j1 - j0) * tn), o_dtype=o_dtype, dims=NN, acc_shape=(tm, tn), name=name,
               rider=rider, zero_rows=lead)


def _mm_nt(g, w, *, tm, to, tn, name, rider=None):
    T = g.shape[0]
    Q, K, Ns = w.shape
    nbs = Ns // tn
    tm = min(tm, T)
    return _mm(g, w, grid=(T // tm, K // to, Q * nbs),
               a_spec=pl.BlockSpec((tm, tn), lambda i, j, n: (i, n)),
               b_spec=pl.BlockSpec((None, to, tn), lambda i, j, n: (n // nbs, j, n % nbs)),
               o_spec=pl.BlockSpec((tm, to), lambda i, j, n: (i, j)),
               o_shape=(T, K), o_dtype=F32, dims=NT, acc_shape=(tm, to), name=name, rider=rider)


def _mm_tn(a, g, *, q, tk, tn, tt, name, rider=None):
    T, K = a.shape
    Ns = g.shape[1] // q
    nbs = Ns // tn
    return _mm(a, g, grid=(K // tk, q * nbs, T // tt),
               a_spec=pl.BlockSpec((tt, tk), lambda i, j, t: (t, i)),
               b_spec=pl.BlockSpec((tt, tn), lambda i, j, t: (t, j)),
               o_spec=pl.BlockSpec((None, tk, tn), lambda i, j, t: (j // nbs, i, j % nbs)),
               o_shape=(q, K, Ns), o_dtype=BF16, dims=TN, acc_shape=(tk, tn), name=name, rider=rider)


def _mm_tn_add(a, g, part, *, tk, tn, name):
    T, K = a.shape
    Q, _, Ns = part.shape
    nbs = Ns // tn

    def body(a_ref, g_ref, p_ref, o_ref):
        o_ref[...] = (_dot(a_ref[...], g_ref[...], TN) + p_ref[...].astype(F32)).astype(o_ref.dtype)

    blk = pl.BlockSpec((None, tk, tn), lambda i, j: (j // nbs, i, j % nbs))
    return pl.pallas_call(
        body, name=name, grid=(K // tk, Q * nbs),
        in_specs=[pl.BlockSpec((T, tk), lambda i, j: (0, i)), pl.BlockSpec((T, tn), lambda i, j: (0, j)), blk],
        out_specs=blk, out_shape=jax.ShapeDtypeStruct((Q, K, Ns), BF16),
        compiler_params=_params(("parallel", "parallel"), 48),
    )(a, g, part)


def _ln(u):
    mu = jnp.mean(u, axis=-1, keepdims=True)
    d = u - mu
    r = lax.rsqrt(jnp.mean(d * d, axis=-1, keepdims=True) + EPS)
    return d * r, r


def _ln_bwd(dy, un, r):
    return r * (dy - jnp.mean(dy, axis=-1, keepdims=True) - un * jnp.mean(dy * un, axis=-1, keepdims=True))


def _colsum(v):
    return jnp.sum(v, axis=0, keepdims=True)


def _rowwise(name, fn, bigs, vecs, out_dtypes, n_acc, tm=128, rider=None):
    T, D = bigs[0].shape
    nb, nv, no = len(bigs), len(vecs), len(out_dtypes)

    def body(*refs):
        outs, accs = fn([r[...] for r in refs[:nb]], [r[...] for r in refs[nb:nb + nv]])
        for r, o in zip(refs[nb + nv:nb + nv + no], outs):
            r[...] = o.astype(r.dtype)
        if n_acc:
            acc_ref = refs[nb + nv + no]

            @pl.when(pl.program_id(0) == 0)
            def _():
                acc_ref[...] = jnp.zeros_like(acc_ref)

            for row, a in enumerate(accs):
                acc_ref[row:row + 1, :] += a

    big_spec = pl.BlockSpec((tm, D), lambda i: (i, 0))
    vec_spec = pl.BlockSpec((1, D), lambda i: (0, 0))
    out_shape = [jax.ShapeDtypeStruct((T, D), dt) for dt in out_dtypes]
    out_specs = [big_spec] * no
    if n_acc:
        out_shape.append(jax.ShapeDtypeStruct((8, D), F32))
        out_specs.append(pl.BlockSpec((8, D), lambda i: (0, 0)))
    outs, rode = _call(
        body, rider, name=name, grid=(T // tm,), in_specs=[big_spec] * nb + [vec_spec] * nv,
        out_specs=out_specs, out_shape=out_shape, scratch_shapes=[],
        compiler_params=_params(("arbitrary",), 48), operands=(*bigs, *vecs))
    return outs if rider is None else (outs, rode)


def _to_bf16(w, name, rider=None):
    R, C = w.shape
    tr = _row_tile(R, C)

    def body(w_ref, o_ref):
        o_ref[...] = w_ref[...].astype(o_ref.dtype)

    blk = pl.BlockSpec((tr, C), lambda i: (i, 0))
    return _call(body, rider, name=name, grid=(R // tr,), in_specs=[blk], out_specs=[blk],
                 out_shape=[jax.ShapeDtypeStruct((R, C), BF16)], scratch_shapes=[],
                 compiler_params=_params(("arbitrary",)), operands=(w,))


def _pre_mixer(x, scale1, shift1):
    def fn(b, v):
        xn, _ = _ln(b[0])
        return [xn * (1.0 + v[0]) + v[1]], []
    return _rowwise("pre_mixer", fn, [x], [scale1, shift1], [BF16], 0)[0]


def _post_mixer(mix, x, gate1, g1, b1, scale2, shift2, rider=None):
    def fn(b, v):
        un1, _ = _ln(ALPHA * b[1] + v[0] * b[0])
        x1 = un1 * v[1] + v[2]
        xn1, _ = _ln(x1)
        return [x1, xn1 * (1.0 + v[3]) + v[4]], []
    return _rowwise("post_mixer", fn, [mix, x], [gate1, g1, b1, scale2, shift2], [F32, BF16], 0, rider=rider)


def _loss_head(f, x1, tgt, gate2, g2, b2):
    def fn(b, v):
        ff, xx, tt = b
        d_model = ff.shape[-1]
        un2, r2 = _ln(ALPHA * xx + v[0] * ff)
        err = un2 * v[1] + v[2] - tt
        dy = err * (1.0 / d_model)
        du2 = _ln_bwd(dy * v[1], un2, r2)
        return [du2, du2 * v[0]], [_colsum(dy * un2), _colsum(dy), _colsum(du2 * ff), _colsum(err * err)]
    return _rowwise("loss_head", fn, [f, x1, tgt], [gate2, g2, b2], [F32, BF16], 4)


def _mid_bwd(dh2, du2, x1, mix, x, gate1, g1, scale2, rider=None):
    def fn(b, v):
        dh, du, xx1, mm, xx = b
        xn1, r1n = _ln(xx1)
        dx1 = ALPHA * du + _ln_bwd(dh * (1.0 + v[2]), xn1, r1n)
        un1, r1 = _ln(ALPHA * xx + v[0] * mm)
        du1 = _ln_bwd(dx1 * v[1], un1, r1)
        return [du1, du1 * v[0]], [_colsum(dh * xn1), _colsum(dh), _colsum(dx1 * un1), _colsum(dx1),
                                   _colsum(du1 * mm)]
    return _rowwise("mid_bwd", fn, [dh2, du2, x1, mix, x], [gate1, g1, scale2], [F32, BF16], 5, rider=rider)


def _first_bwd(dh1, du1, x, scale1, rider=None):
    def fn(b, v):
        dh, du, xx = b
        xn, r0 = _ln(xx)
        return [ALPHA * du + _ln_bwd(dh * (1.0 + v[0]), xn, r0)], [_colsum(dh * xn), _colsum(dh)]
    return _rowwise("first_bwd", fn, [dh1, du1, x], [scale1], [F32], 2, rider=rider)


def _ffn_in_swiglu(h2, w, *, tm, tn, rider=None):
    T, K = h2.shape
    Q, _, Ns = w.shape
    nbs = Ns // tn
    half = Q * nbs // 2
    tm = min(tm, T)

    def body(a_ref, wg_ref, wu_ref, g_ref, u_ref, act_ref):
        a = a_ref[...]
        g, u = _dot(a, wg_ref[...]), _dot(a, wu_ref[...])
        g_ref[...] = g.astype(g_ref.dtype)
        u_ref[...] = u.astype(u_ref.dtype)
        act_ref[...] = (g * jax.nn.sigmoid(g) * u).astype(act_ref.dtype)

    cols = lambda first: pl.BlockSpec((None, K, tn), lambda i, j: ((j + first) // nbs, 0, (j + first) % nbs))
    blk = pl.BlockSpec((tm, tn), lambda i, j: (i, j))
    return _call(
        body, rider, name="ffn_in", grid=(T // tm, half),
        in_specs=[pl.BlockSpec((tm, K), lambda i, j: (i, 0)), cols(0), cols(half)], out_specs=[blk] * 3,
        out_shape=[jax.ShapeDtypeStruct((T, half * tn), BF16)] * 3, scratch_shapes=[],
        compiler_params=_params(("arbitrary", "arbitrary"), 48), operands=(h2, w, w))


def _d_act_swiglu(df, w, gate, up, *, tm, to, rider=None):
    T, N = df.shape
    F = w.shape[1]
    tm = min(tm, T)

    def body(df_ref, w_ref, g_ref, u_ref, dg_ref, du_ref):
        d = _dot(df_ref[...], w_ref[...], NT)
        g = g_ref[...].astype(F32)
        s = jax.nn.sigmoid(g)
        du_ref[...] = (d * g * s).astype(du_ref.dtype)
        dg_ref[...] = (d * u_ref[...].astype(F32) * s * (1.0 + g * (1.0 - s))).astype(dg_ref.dtype)

    blk = pl.BlockSpec((tm, to), lambda i, j: (i, j))
    return _call(
        body, rider, name="d_act", grid=(T // tm, F // to),
        in_specs=[pl.BlockSpec((tm, N), lambda i, j: (i, 0)), pl.BlockSpec((None, to, N), lambda i, j: (0, j, 0)), blk, blk],
        out_specs=[blk, blk], out_shape=[jax.ShapeDtypeStruct((T, F), BF16)] * 2, scratch_shapes=[],
        compiler_params=_params(("arbitrary", "arbitrary"), 48), operands=(df, w, gate, up))


PAIR = 2


def _fill_table(table_ref, band_ref):
    table_ref[...] = jnp.full(table_ref.shape, NEG, F32)
    for e in range(PAIR):
        for g in range(QG):
            table_ref[e, g * CHUNK:(g + 1) * CHUNK, g * CHUNK:g * CHUNK + BAND] = band_ref[e]


def _attn_probs(q_ref, k_ref, bias_ref, e, step):
    start = pl.multiple_of(step * QROWS, QROWS)
    lanes = pl.ds(e * HD_A, HD_A)
    s = _dot(q_ref[:, lanes], k_ref[pl.ds(start + ZPAD - KPAD, UNION), lanes], NT) * (HD_A ** -0.5) + bias_ref[e]
    col = lax.broadcasted_iota(jnp.int32, s.shape, 1)
    s = jnp.where(col + start >= KPAD, s, NEG)
    p = jnp.exp(s - jnp.max(s, axis=-1, keepdims=True))
    return p / jnp.sum(p, axis=-1, keepdims=True), start


def _attn_specs(T, n_pairs):
    wide = PAIR * HD_A
    per_step = pl.BlockSpec((QROWS, wide), lambda hp, n: (n, hp))
    queries = pl.BlockSpec((QROWS, wide), lambda hp, n: (n + ZPAD // QROWS, hp))
    keys = pl.BlockSpec((ZPAD + T, wide), lambda hp, n: (0, n_pairs + hp))
    values = pl.BlockSpec((ZPAD + T, wide), lambda hp, n: (0, 2 * n_pairs + hp))
    grads = pl.BlockSpec((KPAD + T, wide), lambda hp, n: (0, hp))
    table = pl.BlockSpec((PAIR, CHUNK, BAND), lambda hp, n: (hp, 0, 0))
    vec = pl.BlockSpec((1, wide), lambda hp, n: (0, hp))
    return per_step, queries, keys, values, grads, table, vec


def _probs_spec():
    return pl.BlockSpec((PAIR, QROWS, UNION), lambda hp, n: (hp, n, 0))


def _attn_fwd(qkv, bias, gain, rider=None):
    T = qkv.shape[0] - ZPAD
    W = gain.shape[1]
    n_pairs = W // (PAIR * HD_A)

    def body(q_ref, k_ref, v_ref, band_ref, gain_ref, o_ref, p_ref, table_ref):
        @pl.when(pl.program_id(1) == 0)
        def _():
            _fill_table(table_ref, band_ref)

        for e in range(PAIR):
            lanes = pl.ds(e * HD_A, HD_A)
            p, start = _attn_probs(q_ref, k_ref, table_ref, e, pl.program_id(1))
            p_ref[e] = p.astype(p_ref.dtype)
            o = _dot(p_ref[e], v_ref[pl.ds(start + ZPAD - KPAD, UNION), lanes])
            rr = lax.rsqrt(jnp.mean(o * o, axis=-1, keepdims=True) + EPS)
            o_ref[:, lanes] = (o * rr * gain_ref[:, lanes]).astype(o_ref.dtype)

    per_step, queries, keys, values, _, table, vec = _attn_specs(T, n_pairs)
    return _call(
        body, rider, name="attn_fwd", grid=(n_pairs, T // QROWS), in_specs=[queries, keys, values, table, vec],
        out_specs=[per_step, _probs_spec()],
        out_shape=[jax.ShapeDtypeStruct((T, W), BF16), jax.ShapeDtypeStruct((n_pairs * PAIR, T, UNION), BF16)],
        scratch_shapes=[pltpu.VMEM((PAIR, QROWS, UNION), F32)],
        compiler_params=_params(("arbitrary", "arbitrary"), 40), operands=(qkv, qkv, qkv, bias, gain))


def _attn_bwd(qkv, probs, gain, dmixin, rider=None):
    T = qkv.shape[0] - ZPAD
    W = gain.shape[1]
    n_pairs = W // (PAIR * HD_A)
    scale = HD_A ** -0.5

    def body(q_ref, k_ref, v_ref, p_ref, gain_ref, don_ref, dq_ref, dk_ref, dv_ref, dband_ref, dgain_ref, dtable_ref):
        n = pl.program_id(1)

        @pl.when(n == 0)
        def _():
            dk_ref[...] = jnp.zeros_like(dk_ref)
            dv_ref[...] = jnp.zeros_like(dv_ref)
            dtable_ref[...] = jnp.zeros_like(dtable_ref)
            dgain_ref[...] = jnp.zeros_like(dgain_ref)

        for e in range(PAIR):
            lanes = pl.ds(e * HD_A, HD_A)
            start = pl.multiple_of(n * QROWS, QROWS)
            keys, in_qkv = pl.ds(start, UNION), pl.ds(start + ZPAD - KPAD, UNION)
            pb = p_ref[e]
            p = pb.astype(F32)
            vb = v_ref[in_qkv, lanes]
            o = _dot(pb, vb)
            rr = lax.rsqrt(jnp.mean(o * o, axis=-1, keepdims=True) + EPS)
            on = o * rr
            d_on = don_ref[:, lanes]
            dgain_ref[:, lanes] += _colsum(d_on * on)
            dyo = d_on * gain_ref[:, lanes]
            do = rr * (dyo - on * jnp.mean(dyo * on, axis=-1, keepdims=True))
            dob = do.astype(BF16)
            dp = _dot(dob, vb, NT)
            ds = p * (dp - jnp.sum(do * o, axis=-1, keepdims=True))
            dtable_ref[e] += ds
            dsb = ds.astype(BF16)
            dq_ref[:, lanes] = (_dot(dsb, k_ref[in_qkv, lanes]) * scale).astype(dq_ref.dtype)
            dk_ref[keys, lanes] += _dot(dsb, q_ref[:, lanes], TN) * scale
            dv_ref[keys, lanes] += _dot(pb, dob, TN)

        @pl.when(n == T // QROWS - 1)
        def _():
            for e in range(PAIR):
                dband_ref[e] = sum(dtable_ref[e, g * CHUNK:(g + 1) * CHUNK, g * CHUNK:g * CHUNK + BAND]
                                   for g in range(QG))

    per_step, queries, keys, values, grads, table, vec = _attn_specs(T, n_pairs)
    H = n_pairs * PAIR
    return _call(
        body, rider, name="attn_bwd", grid=(n_pairs, T // QROWS),
        in_specs=[queries, keys, values, _probs_spec(), vec, per_step],
        out_specs=[per_step, grads, grads, table, vec],
        out_shape=[jax.ShapeDtypeStruct((T, W), BF16), jax.ShapeDtypeStruct((KPAD + T, W), F32),
                   jax.ShapeDtypeStruct((KPAD + T, W), F32), jax.ShapeDtypeStruct((H, CHUNK, BAND), F32),
                   jax.ShapeDtypeStruct((1, W), F32)],
        scratch_shapes=[pltpu.VMEM((PAIR, QROWS, UNION), F32)],
        compiler_params=_params(("arbitrary", "arbitrary"), 40),
        operands=(qkv, qkv, qkv, probs, gain, dmixin))


N_DIAG = CHUNK + BAND - 1


def _bias_band(rel_bias):
    H = rel_bias.shape[0]
    idx = np.clip(BAND - 1 - np.arange(N_DIAG), -MAX_REL, MAX_REL) + MAX_REL
    rolled = rel_bias[:, idx[(np.arange(N_DIAG) + CHUNK - 1) % N_DIAG]]
    flat = jnp.broadcast_to(rolled[:, None, :], (H, CHUNK, N_DIAG)).reshape(H, CHUNK * N_DIAG)
    return flat[:, :CHUNK * (N_DIAG - 1)].reshape(H, CHUNK, N_DIAG - 1)[:, :, :BAND]


def _bias_band_grad(dband):
    H = dband.shape[0]
    skew = jnp.pad(dband, ((0, 0), (0, 0), (CHUNK - 1, 0))).reshape(H, CHUNK * N_DIAG)
    skew = jnp.pad(skew, ((0, 0), (0, CHUNK))).reshape(H, CHUNK, N_DIAG + 1)
    diag = jnp.sum(skew, axis=1)[:, :N_DIAG]
    n_far = BAND - MAX_REL
    far = jnp.sum(diag[:, :n_far], axis=1, keepdims=True)
    near = diag[:, n_far:][:, ::-1]
    zeros = jnp.zeros((H, MAX_REL - (CHUNK - 1)), F32)
    return jnp.concatenate([zeros, near, far], axis=1)


def _tri(n, lower):
    r = lax.broadcasted_iota(jnp.int32, (n, n), 0)
    c = lax.broadcasted_iota(jnp.int32, (n, n), 1)
    return jnp.where((c <= r) if lower else (c >= r), 1.0, 0.0).astype(F32)


def _hgrn_gates(zq_ref, zf_ref, lbl_ref, q_s, k_s, b_s):
    lb = jax.nn.sigmoid(lbl_ref[0:1, :] - lbl_ref[1:2, :])
    zq = zq_ref[...]
    sig = jax.nn.sigmoid(zf_ref[...])
    f = lb + (1.0 - lb) * sig
    sq = jax.nn.sigmoid(zq)
    q_s[...] = zq * sq
    k_s[...] = 1.0 - f
    b_s[...] = _dot(_tri(CHUNK, True), jnp.log(f), precision=HIGHEST)
    return lb, sig, f, sq


def _sub_rows(i):
    return pl.ds(i * SUB, SUB)


def _row_mask(s):
    return lax.broadcasted_iota(jnp.int32, (SUB, HD_B), 0) >= s


def _decay_from(b_sub, b_row, s):
    return jnp.where(_row_mask(s), jnp.exp(jnp.minimum(b_sub - b_row, 0.0)), 0.0)


def _hgrn_fwd(proj, lb_logits, gnorm_g, rider=None):
    T = proj.shape[0]
    nC = T // CHUNK
    W = lb_logits.shape[1]
    G = W // HD_B // HGRN_HEADS
    col0 = (proj.shape[1] - 4 * W) // (HD_B * HGRN_HEADS)
    wide = HGRN_HEADS * HD_B

    def body(*refs):
        @pl.when(pl.program_id(1) == 0)
        def _():
            refs[9][...] = jnp.zeros_like(refs[9])

        for h in range(HGRN_HEADS):
            lanes = pl.ds(h * HD_B, HD_B)
            one_head(*[r.at[:, lanes] for r in refs[:5]], refs[5], *[r.at[:, lanes] for r in refs[6:8]],
                     *[r.at[h] for r in refs[8:]])

    def one_head(zq_ref, zf_ref, xi_ref, zg_ref, lbl_ref, gn_ref, mix_ref, o_ref, stall_ref, st_ref, q_s, k_s, b_s, acc_s):
        _hgrn_gates(zq_ref, zf_ref, lbl_ref, q_s, k_s, b_s)
        q, k, b = q_s[...], k_s[...], b_s[...]
        st = st_ref[...]
        stall_ref[...] = st
        b_last = b_s[CHUNK - 1:CHUNK, :]
        acc_s[...] = _dot((q * jnp.exp(b)).astype(BF16), st.astype(BF16), NT)
        for i in range(CHUNK // SUB):
            rows = _sub_rows(i)
            q_i, b_i = q_s[rows, :], b_s[rows, :]
            acc = jnp.zeros((SUB, HD_B), F32)
            if i:
                past = pl.ds(0, i * SUB)
                b_ref = b_s[i * SUB - 1:i * SUB, :]
                qs = (q_i * jnp.exp(b_i - b_ref)).astype(BF16)
                ks = (k_s[past, :] * jnp.exp(b_ref - b_s[past, :])).astype(BF16)
                acc += _dot(_dot(qs, ks, NT).astype(BF16), xi_ref[past, :].astype(BF16))
            for s in range(SUB):
                row = pl.ds(i * SUB + s, 1)
                w = q_i * _decay_from(b_i, b_s[row, :], s)
                acc += jnp.sum(w * k_s[row, :], axis=-1, keepdims=True) * xi_ref[row, :]
            acc_s[rows, :] += acc
        o = acc_s[...]
        kd = (k * jnp.exp(b_last - b)).astype(BF16)
        st_ref[...] = st * jnp.exp(b_last) + _dot(xi_ref[...].astype(BF16), kd, TN)
        o_ref[...] = o
        zg = zg_ref[...]
        rr = lax.rsqrt(jnp.mean(o * o, axis=-1, keepdims=True) + EPS)
        mix_ref[...] = (o * rr * gn_ref[...] * (zg * jax.nn.sigmoid(zg))).astype(mix_ref.dtype)

    col = lambda part: pl.BlockSpec((CHUNK, wide), lambda g, n: (n, col0 + part * G + g))
    out_blk = pl.BlockSpec((CHUNK, wide), lambda g, n: (n, g))
    tile = pltpu.VMEM((HGRN_HEADS, CHUNK, HD_B), F32)
    return _call(
        body, rider, name="hgrn_fwd", grid=(G, nC),
        in_specs=[col(0), col(1), col(2), col(3), pl.BlockSpec((2, wide), lambda g, n: (0, g)),
                  pl.BlockSpec((1, HD_B), lambda g, n: (0, 0))],
        out_specs=[out_blk, out_blk, pl.BlockSpec((HGRN_HEADS, None, HD_B, HD_B), lambda g, n: (g, n, 0, 0))],
        out_shape=[jax.ShapeDtypeStruct((T, W), BF16), jax.ShapeDtypeStruct((T, W), F32),
                   jax.ShapeDtypeStruct((G * HGRN_HEADS, nC, HD_B, HD_B), F32)],
        scratch_shapes=[pltpu.VMEM((HGRN_HEADS, HD_B, HD_B), F32), tile, tile, tile, tile],
        compiler_params=_params(("arbitrary", "arbitrary")),
        operands=(proj, proj, proj, proj, lb_logits, gnorm_g))


def _hgrn_bwd(proj, lb_logits, gnorm_g, o_b, st_all, dmixin, rider=None):
    T = proj.shape[0]
    nC = T // CHUNK
    W = lb_logits.shape[1]
    G = W // HD_B // HGRN_HEADS
    wide = HGRN_HEADS * HD_B
    col0 = (proj.shape[1] - 4 * W) // wide
    dcol0 = (dmixin.shape[1] - W) // wide

    def body(*refs):
        g, n = pl.program_id(0), pl.program_id(1)
        dl0_ref, dgn_ref, dst_ref = refs[13:16]

        @pl.when(n == 0)
        def _():
            dst_ref[...] = jnp.zeros_like(dst_ref)
            dl0_ref[...] = jnp.zeros_like(dl0_ref)

        @pl.when((n == 0) & (g == 0))
        def _():
            dgn_ref[...] = jnp.zeros_like(dgn_ref)

        for h in range(HGRN_HEADS):
            lanes = pl.ds(h * HD_B, HD_B)
            cut = lambda r: r.at[:, lanes]
            one_head(*[cut(r) for r in refs[:5]], refs[5], cut(refs[6]), refs[7].at[h], cut(refs[8]),
                     *[cut(r) for r in refs[9:14]], dgn_ref, *[r.at[h] for r in refs[15:]])

    def one_head(zq_ref, zf_ref, xi_ref, zg_ref, lbl_ref, gn_ref, o_ref, st_ref, dout_ref,
                 dzq_ref, dzf_ref, dxi_ref, dzg_ref, dl0_ref, dgn_ref, dst_ref, q_s, k_s, b_s, do_s, dq_s, dk_s, di_s):
        lb, sig, f, sq = _hgrn_gates(zq_ref, zf_ref, lbl_ref, q_s, k_s, b_s)
        q, k, b = q_s[...], k_s[...], b_s[...]
        zg, o, dout = zg_ref[...], o_ref[...], dout_ref[...]
        sg = jax.nn.sigmoid(zg)
        rr = lax.rsqrt(jnp.mean(o * o, axis=-1, keepdims=True) + EPS)
        on = o * rr
        gn = gn_ref[...]
        dzg_ref[...] = (dout * on * gn * sg * (1.0 + zg * (1.0 - sg))).astype(dzg_ref.dtype)
        d_on = dout * zg * sg
        dgn_ref[...] += _colsum(d_on * on)
        d_on = d_on * gn
        do = rr * (d_on - on * jnp.mean(d_on * on, axis=-1, keepdims=True))
        do_s[...] = do
        dob = do.astype(BF16)
        st, dst = st_ref[...], dst_ref[...]
        b_last = b_s[CHUNK - 1:CHUNK, :]
        eb, e_last, k_dec = jnp.exp(b), jnp.exp(b_last), jnp.exp(b_last - b)
        qt, kd = q * eb, k * k_dec
        dstb = dst.astype(BF16)
        xib = xi_ref[...].astype(BF16)
        d_kd = _dot(xib, dstb)
        dq_s[...] = _dot(dob, st.astype(BF16)) * eb
        dk_s[...] = d_kd * k_dec
        di_s[...] = _dot(kd.astype(BF16), dstb, NT)
        d_b_last = e_last * _colsum(st * dst) + _colsum(d_kd * kd)
        dst_ref[...] = _dot(dob, qt.astype(BF16), TN) + dst * e_last
        for i in range(CHUNK // SUB):
            rows = _sub_rows(i)
            q_i, b_i, do_i = q_s[rows, :], b_s[rows, :], do_s[rows, :]
            dq_i = jnp.zeros((SUB, HD_B), F32)
            if i:
                past = pl.ds(0, i * SUB)
                b_ref = b_s[i * SUB - 1:i * SUB, :]
                e_q, e_k = jnp.exp(b_i - b_ref), jnp.exp(b_ref - b_s[past, :])
                qs, ks = (q_i * e_q).astype(BF16), (k_s[past, :] * e_k).astype(BF16)
                xi_p, do_b = xi_ref[past, :].astype(BF16), do_i.astype(BF16)
                di_s[past, :] += _dot(_dot(ks, qs, NT).astype(BF16), do_b)
                dq_i += _dot(_dot(do_b, xi_p, NT).astype(BF16), ks) * e_q
                dk_s[past, :] += _dot(_dot(xi_p, do_b, NT).astype(BF16), qs) * e_k
            for s in range(SUB):
                row = pl.ds(i * SUB + s, 1)
                k_row, i_row = k_s[row, :], xi_ref[row, :]
                e = _decay_from(b_i, b_s[row, :], s)
                w = q_i * e
                a_col = jnp.sum(w * k_row, axis=-1, keepdims=True)
                da_col = jnp.sum(do_i * i_row, axis=-1, keepdims=True)
                di_s[row, :] += _colsum(a_col * do_i)
                dq_i += da_col * e * k_row
                dk_s[row, :] += _colsum(da_col * w)
            dq_s[rows, :] += dq_i
        dq, dk = dq_s[...], dk_s[...]
        db = q * dq - k * dk
        is_last = lax.broadcasted_iota(jnp.int32, (CHUNK, HD_B), 0) == CHUNK - 1
        db = db + jnp.where(is_last, d_b_last, 0.0)
        df = _dot(_tri(CHUNK, False), db, precision=HIGHEST) / f - dk
        dzf_ref[...] = (df * (1.0 - lb) * sig * (1.0 - sig)).astype(dzf_ref.dtype)
        dl0_ref[...] += _colsum(df * (1.0 - sig)) * (lb * (1.0 - lb))
        zq = zq_ref[...]
        dzq_ref[...] = (dq * sq * (1.0 + zq * (1.0 - sq))).astype(dzq_ref.dtype)
        dxi_ref[...] = di_s[...].astype(dxi_ref.dtype)

    rev = lambda n: nC - 1 - n
    col = lambda part: pl.BlockSpec((CHUNK, wide), lambda g, n: (rev(n), col0 + part * G + g))
    blk = pl.BlockSpec((CHUNK, wide), lambda g, n: (rev(n), g))
    tile = pltpu.VMEM((HGRN_HEADS, CHUNK, HD_B), F32)
    out_big = jax.ShapeDtypeStruct((T, W), BF16)
    return _call(
        body, rider, name="hgrn_bwd", grid=(G, nC),
        in_specs=[col(0), col(1), col(2), col(3), pl.BlockSpec((2, wide), lambda g, n: (0, g)),
                  pl.BlockSpec((1, HD_B), lambda g, n: (0, 0)), blk,
                  pl.BlockSpec((HGRN_HEADS, None, HD_B, HD_B), lambda g, n: (g, rev(n), 0, 0)),
                  pl.BlockSpec((CHUNK, wide), lambda g, n: (rev(n), dcol0 + g))],
        out_specs=[blk, blk, blk, blk, pl.BlockSpec((1, wide), lambda g, n: (0, g)),
                   pl.BlockSpec((1, HD_B), lambda g, n: (0, 0))],
        out_shape=[out_big, out_big, out_big, out_big, jax.ShapeDtypeStruct((1, W), F32),
                   jax.ShapeDtypeStruct((1, HD_B), F32)],
        scratch_shapes=[pltpu.VMEM((HGRN_HEADS, HD_B, HD_B), F32)] + [tile] * 7,
        compiler_params=_params(("arbitrary", "arbitrary")),
        operands=(proj, proj, proj, proj, lb_logits, gnorm_g, o_b, st_all, dmixin))


def _adamw_math(g, w, m, v):
    m = B1 * m + (1.0 - B1) * g
    v = B2 * v + (1.0 - B2) * (g * g)
    m_hat = m / (1.0 - B1 ** STEP)
    v_hat = v / (1.0 - B2 ** STEP)
    return -LR * (m_hat / (jnp.sqrt(v_hat) + ADAM_EPS) + WD * w), m, v


def _adamw(g, w, m, v, name):
    R, C = g.shape
    tr = _row_tile(R, C)

    def body(g_ref, w_ref, m_ref, v_ref, go_ref, d_ref, mo_ref, vo_ref):
        g = g_ref[...]
        go_ref[...] = g
        d_ref[...], mo_ref[...], vo_ref[...] = _adamw_math(g, w_ref[...], m_ref[...], v_ref[...])

    blk = pl.BlockSpec((tr, C), lambda i: (i, 0))
    return pl.pallas_call(
        body, name=name, grid=(R // tr,), in_specs=[blk] * 4, out_specs=[blk] * 4,
        out_shape=[jax.ShapeDtypeStruct((R, C), F32)] * 4, compiler_params=_params(("parallel",), 40),
    )(g, w, m, v)


def _sum_pair(g_full, from_sibling, sel, name):
    Q, K, Ns = g_full.shape
    kh = K // 2
    tr = _row_tile(kh, Ns)
    nh = kh // tr

    def body(sel_ref, a_ref, b_ref, o_ref):
        o_ref[...] = (a_ref[...].astype(F32) + b_ref[...].astype(F32)).astype(o_ref.dtype)

    return pl.pallas_call(
        body, name=name,
        grid_spec=pltpu.PrefetchScalarGridSpec(
            num_scalar_prefetch=1, grid=(Q, nh),
            in_specs=[pl.BlockSpec((None, tr, Ns), lambda q, i, sel: (q, sel[1] * nh + i, 0)),
                      pl.BlockSpec((None, tr, Ns), lambda q, i, sel: (q, i, 0))],
            out_specs=pl.BlockSpec((None, tr, Ns), lambda q, i, sel: (q, i, 0))),
        out_shape=jax.ShapeDtypeStruct((Q, kh, Ns), BF16), compiler_params=_params(("parallel", "parallel")),
    )(sel, g_full, from_sibling)


def _sum_chips(pair_sum, from_chips, sel, name):
    Q, kh, Ns = pair_sum.shape
    tr = _row_tile(kh, Ns)
    nh = kh // tr

    def body(sel_ref, a_ref, b0_ref, b1_ref, b2_ref, o_ref):
        up = lambda r: r[...].astype(F32)
        o_ref[...] = ((up(a_ref) + up(b0_ref)) + up(b1_ref)) + up(b2_ref)

    recv = lambda k: pl.BlockSpec((None, tr, Ns), lambda i, sel: (k, i, 0))
    return pl.pallas_call(
        body, name=name,
        grid_spec=pltpu.PrefetchScalarGridSpec(
            num_scalar_prefetch=1, grid=(nh,),
            in_specs=[pl.BlockSpec((None, tr, Ns), lambda i, sel: (sel[0], i, 0)), recv(0), recv(1), recv(2)],
            out_specs=pl.BlockSpec((tr, Ns), lambda i, sel: (sel[1] * nh + i, 0))),
        out_shape=jax.ShapeDtypeStruct((2 * kh, Ns), F32), compiler_params=_params(("parallel",)),
    )(sel, pair_sum, from_chips, from_chips, from_chips)


def _gather_small(v, name):
    R, L = v.shape

    def body(v_ref, out_ref, send_sems, recv_sems):
        x, y, c = _place()
        me = 4 * x + 2 * y + c
        out_ref[me] = v_ref[...]
        peers = [(_flip(x, k >> 2 & 1), _flip(y, k >> 1 & 1), _flip(c, k & 1)) for k in range(1, N_DEV)]

        def copy(k, row, to):
            return pltpu.make_async_remote_copy(src_ref=v_ref, dst_ref=out_ref.at[row], send_sem=send_sems.at[k],
                                                recv_sem=recv_sems.at[k], device_id=to, device_id_type=MESH)

        sends = [copy(k, me, peer) for k, peer in enumerate(peers)]
        for cp in sends:
            cp.start()
        for k, (px, py, pc) in enumerate(peers):
            copy(k, 4 * px + 2 * py + pc, (x, y, c)).wait_recv()
        for cp in sends:
            cp.wait_send()

    vmem = pl.BlockSpec(memory_space=pltpu.VMEM)
    return pl.pallas_call(
        body, name=name, in_specs=[vmem], out_specs=vmem, out_shape=jax.ShapeDtypeStruct((N_DEV, R, L), F32),
        scratch_shapes=[pltpu.SemaphoreType.DMA((N_DEV - 1,)), pltpu.SemaphoreType.DMA((N_DEV - 1,))],
    )(v)


def _silu(v):
    return v * jax.nn.sigmoid(v)


def _ada_fwd(c_all, w_ada, tn=512):
    M, D = c_all.shape
    Ns = w_ada.shape[1]

    def body(c_ref, w_ref, o_ref):
        o_ref[...] = _dot(_silu(c_ref[...]).astype(BF16), w_ref[...].astype(BF16))

    return pl.pallas_call(
        body, name="ada_fwd", grid=(Ns // tn,),
        in_specs=[pl.BlockSpec((M, D), lambda j: (0, 0)), pl.BlockSpec((D, tn), lambda j: (0, j))],
        out_specs=pl.BlockSpec((M, tn), lambda j: (0, j)), out_shape=jax.ShapeDtypeStruct((M, Ns), F32),
        compiler_params=_params(("parallel",)),
    )(c_all, w_ada)


def _ada_bwd(c_all, dmod, w, m, v, tk=256, tn=1536):
    M, D = c_all.shape
    Ns = dmod.shape[1]

    def body(c_ref, d_ref, w_ref, m_ref, v_ref, g_ref, dl_ref, mo_ref, vo_ref):
        g = _dot(_silu(c_ref[...]).astype(BF16), d_ref[...].astype(BF16), TN)
        g_ref[...] = g
        dl_ref[...], mo_ref[...], vo_ref[...] = _adamw_math(g, w_ref[...], m_ref[...], v_ref[...])

    blk = pl.BlockSpec((tk, tn), lambda i, j: (i, j))
    return pl.pallas_call(
        body, name="ada_bwd", grid=(D // tk, Ns // tn),
        in_specs=[pl.BlockSpec((M, tk), lambda i, j: (0, i)), pl.BlockSpec((M, tn), lambda i, j: (0, j)), blk, blk, blk],
        out_specs=[blk] * 4, out_shape=[jax.ShapeDtypeStruct((D, Ns), F32)] * 4,
        compiler_params=_params(("parallel", "parallel"), 40),
    )(c_all, dmod, w, m, v)


def _small_update(g_all, w, m, v):
    R, L = w.shape

    def body(g_ref, w_ref, m_ref, v_ref, go_ref, d_ref, mo_ref, vo_ref):
        g = g_ref[0]
        for d in range(1, N_DEV):
            g = g + g_ref[d]
        go_ref[...] = g
        d_ref[...], mo_ref[...], vo_ref[...] = _adamw_math(g, w_ref[...], m_ref[...], v_ref[...])

    return pl.pallas_call(body, name="small_update", out_shape=[jax.ShapeDtypeStruct((R, L), F32)] * 4)(g_all, w, m, v)


def _pack(parts, rows):
    flat = jnp.concatenate([p.reshape(-1) for p in parts])
    return jnp.pad(flat, (0, rows * 128 - flat.shape[0])).reshape(rows, 128)


def _unpack(packed, shapes):
    flat, out, at = packed.reshape(-1), [], 0
    for shp in shapes:
        size = 1
        for d in shp:
            size *= d
        out.append(flat[at:at + size].reshape(shp))
        at += size
    return out


def _layer(x, tgt, mod, wts, rel_bias, attn_norm_g, lb_logits, gnorm_g, ln1_g, ln1_b, ln2_g, ln2_b, place=None):
    T, D = x.shape
    aw = attn_norm_g.shape[1]
    shift1, scale1, gate1, shift2, scale2, gate2 = [mod[i:i + 1] for i in range(6)]

    def gather(n, rows=None, into=None, before=None, last=True):
        return None if place is None else _gather_rider(wts[n], rows, None if into is None else into[0], before, last)

    def gathered(n, rode):
        return wts[n] if place is None else lax.dynamic_update_index_in_dim(rode[0], wts[n], place[0], 0)

    def blocks(g):
        return g.reshape(N_CHIPS, -1, g.shape[2])

    def to_sibling(g):
        return None if place is None else _pair_rider(g)

    def pair_sum(n, g, rode=None):
        if place is None:
            return g
        rode = _alone(_pair_rider(g), n + "_send_pair") if rode is None else rode
        return _sum_pair(g, rode[0], place[1], n + "_sum_pair")

    def to_chips(p, rows=None, into=None):
        return None if place is None else _chips_rider(p, rows, None if into is None else into[0])

    def summed(n, p, rode):
        return p if place is None else _sum_chips(p, rode[0], place[1], n + "_sum_chips")

    def to_both(block):
        return None if place is None else _share_rider(block)

    def carrying(mm, *args, rider, **kw):
        return mm(*args, rider=rider, **kw) if rider is not None else (mm(*args, **kw), None)

    def to_sibling_acts(a, b):
        return None if place is None else _acts_rider(a, b)

    def pair_grad(name, a, b, tn, rider, arrived=None):
        if place is None:
            return _mm_tn(a, b, q=N_CHIPS, tk=512, tn=tn, tt=T, name=name), None
        kh = a.shape[1] // 2
        mine = lax.dynamic_slice_in_dim(a, place[1][1] * kh, kh, axis=1)
        part, rode = carrying(_mm_tn, mine, b, q=N_CHIPS, tk=512, tn=tn, tt=T, name=name + "_own",
                              rider=_join(None if arrived else _acts_rider(a, b), rider))
        (a_sib, b_sib), rode = arrived or rode[:2], rode if arrived else rode[2:]
        return _mm_tn_add(a_sib, b_sib, part, tk=512, tn=tn, name=name + "_sib"), rode

    w_in = gathered("w_in", None if place is None else [wts["w_in_gathered"]])
    h1 = _pre_mixer(x, scale1, shift1)
    n_qkv = 3 * aw // 256
    kh_o, kh_f = wts["w_o"].shape[-2] // 2, wts["w_ffn_in"].shape[-2] // 2
    o_cut, f_cuts = 3 * kh_o // 8, (7 * kh_f // 16, 7 * kh_f // 8)
    qkv, rode = carrying(_mm_nn, h1, w_in, tm=ZPAD, tn=256, tk=D, name="proj_qkv", cols=(0, n_qkv), o_dtype=BF16,
                         pad_rows=ZPAD, rider=gather("w_o", (0, o_cut), last=False))
    proj, rode = carrying(_mm_nn, h1, w_in, tm=2048, tn=256, tk=D, name="proj_rec",
                          cols=(n_qkv, N_CHIPS * w_in.shape[2] // 256),
                          rider=gather("w_o", (o_cut, kh_o - o_cut), rode, before=(0, o_cut)))
    w_o3 = gathered("w_o", rode).reshape(1, D, D)
    bias = _bias_band(rel_bias)
    (mix_a, probs), rode = _attn_fwd(qkv, bias, attn_norm_g, rider=gather("w_ffn_in", (0, f_cuts[0]), last=False))
    (mix_b, o_b, st_all), rode = _hgrn_fwd(
        proj, lb_logits, gnorm_g,
        rider=gather("w_ffn_in", (f_cuts[0], f_cuts[1] - f_cuts[0]), rode, before=(0, f_cuts[0]), last=False))
    mixin = jnp.concatenate([mix_a, mix_b], axis=1)
    mix = _mm_nn(mixin, w_o3, tm=1024, tn=512, tk=D, name="mix_out")
    if place is None:
        x1, h2 = _post_mixer(mix, x, gate1, ln1_g, ln1_b, scale2, shift2)
    else:
        (x1, h2), rode = _post_mixer(mix, x, gate1, ln1_g, ln1_b, scale2, shift2,
                                     rider=gather("w_ffn_in", (f_cuts[1], kh_f - f_cuts[1]), rode,
                                                  before=(f_cuts[0], f_cuts[1] - f_cuts[0])))
    w_ffn_in = gathered("w_ffn_in", rode)
    (gate, up, act), rode = _ffn_in_swiglu(h2, w_ffn_in, tm=2048, tn=256, rider=gather("w_ffn_out"))
    w_out3 = gathered("w_ffn_out", rode)
    w_out3 = w_out3.reshape(1, -1, w_out3.shape[2])
    d_ff = w_out3.shape[1]
    f = _mm_nn(act, w_out3, tm=1024, tn=512, tk=d_ff, name="ffn_out")
    du2, df, acc2 = _loss_head(f, x1, tgt, gate2, ln2_g, ln2_b)
    loss = (0.5 / D) * jnp.sum(acc2[3])
    g = blocks(_mm_tn(act, df, q=1, tk=512, tn=1024, tt=T, name="g_ffn_out"))
    d_gate_up, rode = _d_act_swiglu(df, w_out3, gate, up, tm=1024, to=512, rider=to_sibling(g))
    p_out = pair_sum("w_ffn_out", g, rode)
    dff = jnp.concatenate(d_gate_up, axis=1)
    cut = 21 * p_out.shape[1] // 44
    dh2, rode = carrying(_mm_nt, dff, w_ffn_in, tm=1024, to=1024, tn=w_ffn_in.shape[2], name="d_h2",
                         rider=_join(to_chips(p_out, (0, cut)), to_sibling_acts(h2, dff)))
    p_fin, rode = pair_grad("g_ffn_in", h2, dff, w_ffn_in.shape[2] // 2,
                            to_chips(p_out, (cut, p_out.shape[1] - cut), rode), arrived=rode and rode[1:])
    g_ffn_out = summed("w_ffn_out", p_out, rode)
    if place is None:
        du1, dmix, acc1 = _mid_bwd(dh2, du2, x1, mix, x, gate1, ln1_g, scale2)
    else:
        (du1, dmix, acc1), (g_ffn_out,) = _mid_bwd(dh2, du2, x1, mix, x, gate1, ln1_g, scale2, rider=to_both(g_ffn_out))
    g = blocks(_mm_tn(mixin, dmix, q=1, tk=512, tn=1024, tt=T, name="g_o"))
    dmixin, rode = carrying(_mm_nt, dmix, w_o3, tm=1024, to=512, tn=D, name="d_mixin", rider=to_sibling(g))
    p_o = pair_sum("w_o", g, rode)
    cut = p_fin.shape[1] // 2
    (dq, dk, dv, dbias, dgain), rode = _attn_bwd(qkv, probs, attn_norm_g, dmixin, rider=to_chips(p_fin, (0, cut)))
    (dzq, dzf, dxi, dzg, dl0, dgn), rode = _hgrn_bwd(
        proj, lb_logits, gnorm_g, o_b, st_all, dmixin,
        rider=_join(to_chips(p_fin, (cut, p_fin.shape[1] - cut), rode), to_chips(p_o)))
    g_ffn_in, g_o = summed("w_ffn_in", p_fin, rode[:1]), summed("w_o", p_o, rode[1:])
    dproj = jnp.concatenate([dq, dk[KPAD:].astype(BF16), dv[KPAD:].astype(BF16), dzq, dzf, dxi, dzg], axis=1)
    p_in, rode = pair_grad("g_in", h1, dproj, w_in.shape[2] // 2, _join(to_both(g_ffn_in), to_both(g_o)))
    if place is not None:
        g_ffn_in, g_o = rode
    cut = 3 * p_in.shape[1] // 4
    dh1, rode = carrying(_mm_nt, dproj, w_in, tm=1024, to=1024, tn=w_in.shape[2], name="d_h1",
                         rider=to_chips(p_in, (0, cut)))
    if place is None:
        (grad_x, acc0), g_in = _first_bwd(dh1, du1, x, scale1), p_in
    else:
        (grad_x, acc0), rode = _first_bwd(dh1, du1, x, scale1, rider=to_chips(p_in, (cut, p_in.shape[1] - cut), rode))
        g_in, = _alone(to_both(summed("w_in", p_in, rode)), "w_in_share")
    dmod = jnp.concatenate([acc0[1:2], acc0[0:1], acc1[4:5], acc1[1:2], acc1[0:1], acc2[2:3]], axis=0)
    small = dict(rel_bias=_bias_band_grad(dbias), attn_norm_g=dgain,
                 lb_logits=jnp.concatenate([dl0, -dl0], axis=0), gnorm_g=dgn,
                 ln1_g=acc1[2:3], ln1_b=acc1[3:4], ln2_g=acc2[0:1], ln2_b=acc2[1:2])
    return loss, grad_x, dict(w_in=g_in, w_o=g_o, w_ffn_in=g_ffn_in, w_ffn_out=g_ffn_out), dmod, small


SMALL = ("rel_bias", "attn_norm_g", "lb_logits", "gnorm_g", "ln1_g", "ln1_b", "ln2_g", "ln2_b")
SMALL_ROWS = 256


def kernel(x, c, w_ada, b_ada, w_in, rel_bias, attn_norm_g, lb_logits, gnorm_g, w_o, ln1_g, ln1_b, w_ffn_in, w_ffn_out, ln2_g, ln2_b, loss_target, m_w_ada, m_b_ada, m_w_in, m_rel_bias, m_attn_norm_g, m_lb_logits, m_gnorm_g, m_w_o, m_ln1_g, m_ln1_b, m_w_ffn_in, m_w_ffn_out, m_ln2_g, m_ln2_b, v_w_ada, v_b_ada, v_w_in, v_rel_bias, v_attn_norm_g, v_lb_logits, v_gnorm_g, v_w_o, v_ln1_g, v_ln1_b, v_w_ffn_in, v_w_ffn_out, v_ln2_g, v_ln2_b):
    mx, my, mc = _place()
    me = 4 * mx + 2 * my + mc
    chip = 2 * mx + my
    sel = jnp.stack([chip, mc]).astype(jnp.int32)
    D = x.shape[2]
    ns_ada = w_ada.shape[2]

    big = dict(w_in=(w_in, m_w_in, v_w_in), w_o=(w_o, m_w_o, v_w_o), w_ffn_in=(w_ffn_in, m_w_ffn_in, v_w_ffn_in),
               w_ffn_out=(w_ffn_out, m_w_ffn_out, v_w_ffn_out))
    shards = dict(w_in=w_in[0].astype(BF16))
    kh, rode, rows = shards["w_in"].shape[0] // 2, None, None
    for n, part in (("w_ffn_in", 19), ("w_ffn_out", 9), ("w_o", 4)):
        before, rows = rows, (rows[0] + rows[1] if rows else 0, part * kh // 32)
        (shards[n],), rode = _to_bf16(big[n][0][0], "cast_" + n,
                                      _gather_rider(shards["w_in"], rows, rode and rode[0], before, last=n == "w_o"))
    shards["w_in_gathered"] = rode[0]

    c_all = _gather_small(c.reshape(D // 128, 128), "gather_c").reshape(N_DEV, D)
    c_all = jnp.pad(c_all, ((0, 16 - N_DEV), (0, 0)))
    mod_cols = _ada_fwd(c_all, w_ada[0])[:N_DEV]
    mod_all = _gather_small(mod_cols.reshape(-1, 128), "gather_mod").reshape(N_DEV, N_DEV, ns_ada)
    mod = lax.dynamic_index_in_dim(mod_all[::2], me, axis=1, keepdims=False)
    mod = (mod.reshape(1, -1) + b_ada).reshape(6, D)

    loss, grad_x, g_big, dmod, g_small = _layer(
        x[0], loss_target[0], mod, shards, rel_bias[0], attn_norm_g, lb_logits, gnorm_g, ln1_g, ln1_b, ln2_g, ln2_b,
        place=(chip, sel))

    grads, deltas, new_m, new_v = {}, {}, {}, {}
    for n, (w, m, v) in big.items():
        g, d, mo, vo = _adamw(g_big[n], w[0], m[0], v[0], "adamw_" + n)
        grads[n], deltas[n], new_m[n], new_v[n] = g[None], d[None], mo[None], vo[None]

    small_in = dict(rel_bias=(rel_bias, m_rel_bias, v_rel_bias), attn_norm_g=(attn_norm_g, m_attn_norm_g, v_attn_norm_g),
                    lb_logits=(lb_logits, m_lb_logits, v_lb_logits), gnorm_g=(gnorm_g, m_gnorm_g, v_gnorm_g),
                    ln1_g=(ln1_g, m_ln1_g, v_ln1_g), ln1_b=(ln1_b, m_ln1_b, v_ln1_b), ln2_g=(ln2_g, m_ln2_g, v_ln2_g),
                    ln2_b=(ln2_b, m_ln2_b, v_ln2_b))
    g_all = _gather_small(_pack([dmod] + [g_small[n] for n in SMALL] + [loss], SMALL_ROWS), "gather_small")
    packed = [_pack([t] + [small_in[n][i] for n in SMALL] + [jnp.zeros((), F32)], SMALL_ROWS)
              for i, t in enumerate((b_ada, m_b_ada, v_b_ada))]
    shapes = [b_ada.shape] + [small_in[n][0].shape for n in SMALL] + [()]
    outs = [_unpack(o, shapes) for o in _small_update(g_all, *packed)]
    loss = outs[0][-1]
    for i, n in enumerate(("b_ada",) + SMALL):
        grads[n], deltas[n], new_m[n], new_v[n] = outs[0][i], outs[1][i], outs[2][i], outs[3][i]

    dmod_all = g_all[:, :6 * D // 128].reshape(N_DEV, 6 * D)
    dmod_cols = lax.dynamic_slice_in_dim(dmod_all, chip * ns_ada, ns_ada, axis=1)
    dmod_cols = jnp.pad(dmod_cols, ((0, 16 - N_DEV), (0, 0)))
    g, d, mo, vo = _ada_bwd(c_all, dmod_cols, w_ada[0], m_w_ada[0], v_w_ada[0])
    grads["w_ada"], deltas["w_ada"], new_m["w_ada"], new_v["w_ada"] = g[None], d[None], mo[None], vo[None]

    order = ("w_ada", "b_ada", "w_in", "rel_bias", "attn_norm_g", "lb_logits", "gnorm_g", "w_o", "ln1_g", "ln1_b",
             "w_ffn_in", "w_ffn_out", "ln2_g", "ln2_b")
    return (loss, grad_x[None], *[grads[n] for n in order], *[deltas[n] for n in order],
            *[new_m[n] for n in order], *[new_v[n] for n in order])
```

```python
import numpy as np
import jax
import jax.numpy as jnp
from jax import lax
from jax.experimental import pallas as pl
from jax.experimental.pallas import tpu as pltpu

F32 = jnp.float32
BF16 = jnp.bfloat16
MESH = pl.DeviceIdType.MESH
HIGHEST = lax.Precision.HIGHEST

CHUNK = 64
N_PAST = 8
QG = 4
QROWS = QG * CHUNK
KPAD = N_PAST * CHUNK
ZPAD = 2 * KPAD
UNION = (QG + N_PAST) * CHUNK
BAND = (N_PAST + 1) * CHUNK
HD_A = 64
HD_B = 128
SUB = 16
HGRN_HEADS = 8
MAX_REL = 256
EPS = 1e-5
ALPHA = 2.0 ** 0.25
LR, B1, B2, ADAM_EPS, WD, STEP = 1e-3, 0.9, 0.999, 1e-8, 0.01, 10
N_CHIPS = 4
N_DEV = 8
NEG = -1e30
TILE_BYTES = 3 << 19

NN = ((1,), (0,))
NT = ((1,), (1,))
TN = ((0,), (0,))


def _dot(a, b, dims=NN, precision=None):
    return lax.dot_general(a, b, (dims, ((), ())), preferred_element_type=F32, precision=precision)


def _params(sem=None, vmem_mb=None, **kw):
    return pltpu.CompilerParams(dimension_semantics=sem,
                                vmem_limit_bytes=None if vmem_mb is None else vmem_mb << 20, **kw)


def _row_tile(rows, cols):
    for cand in (512, 256, 128, 64, 32, 16, 8):
        if rows % cand == 0 and cand * cols * 4 <= TILE_BYTES:
            return cand
    raise ValueError((rows, cols))


def _place():
    return lax.axis_index("x"), lax.axis_index("y"), lax.axis_index("c")


def _flip(v, bit):
    return 1 - v if bit else v


ANY = pl.BlockSpec(memory_space=pl.ANY)
CHIP_FLIPS = ((1, 0), (0, 1), (1, 1))


class _Rider:
    def __init__(self, operands, out_shape, n_sems, start, finish, aliases=None):
        self.operands, self.out_shape, self.n_sems, self.start, self.finish = operands, out_shape, n_sems, start, finish
        self.aliases = aliases or {}


def _call(body, rider, *, name, grid, in_specs, out_specs, out_shape, scratch_shapes, compiler_params, operands):
    if rider is None:
        outs = pl.pallas_call(body, name=name, grid=grid, in_specs=in_specs, out_specs=out_specs, out_shape=out_shape,
                              scratch_shapes=scratch_shapes, compiler_params=compiler_params)(*operands)
        return list(outs), []
    n_in, n_out, n_sc = len(in_specs), len(out_specs), len(scratch_shapes)
    r_in, r_out = len(rider.operands), len(rider.out_shape)

    def carried(*refs):
        refs = list(refs)
        cuts = [n_in, r_in, n_out, r_out, n_sc]
        ins, r_ins, outs, r_outs, scratch = [[refs.pop(0) for _ in range(n)] for n in cuts]
        first, last = None, None
        for axis, size in enumerate(grid):
            i = pl.program_id(axis)
            first = (i == 0) if first is None else first & (i == 0)
            last = (i == size - 1) if last is None else last & (i == size - 1)

        @pl.when(first)
        def _():
            rider.start(r_ins, r_outs, *refs)

        body(*ins, *outs, *scratch)

        @pl.when(last)
        def _():
            rider.finish(r_ins, r_outs, *refs)

    sems = [pltpu.SemaphoreType.DMA((rider.n_sems,)), pltpu.SemaphoreType.DMA((rider.n_sems,))]
    outs = pl.pallas_call(carried, name=name, grid=grid, in_specs=list(in_specs) + [ANY] * r_in,
                          out_specs=list(out_specs) + [ANY] * r_out, out_shape=list(out_shape) + rider.out_shape,
                          scratch_shapes=list(scratch_shapes) + sems, compiler_params=compiler_params,
                          input_output_aliases={n_in + i: n_out + o for i, o in rider.aliases.items()},
                          )(*operands, *rider.operands)
    return list(outs[:n_out]), list(outs[n_out:])


def _alone(rider, name):
    def body(*refs):
        ins, outs, sems = refs[:len(rider.operands)], refs[len(rider.operands):-2], refs[-2:]
        rider.start(ins, outs, *sems)
        rider.finish(ins, outs, *sems)

    return pl.pallas_call(
        body, name=name, in_specs=[ANY] * len(rider.operands), out_specs=[ANY] * len(rider.out_shape),
        out_shape=rider.out_shape, input_output_aliases=rider.aliases,
        scratch_shapes=[pltpu.SemaphoreType.DMA((rider.n_sems,)), pltpu.SemaphoreType.DMA((rider.n_sems,))],
    )(*rider.operands)


class _Sems:
    def __init__(self, sems, base):
        self.sems, self.base = sems, base

    @property
    def at(self):
        return self

    def __getitem__(self, k):
        return self.sems.at[self.base + k]


def _join(*riders):
    riders = [r for r in riders if r is not None]
    if len(riders) < 2:
        return riders[0] if riders else None

    def parts(ins, outs, send_sems, recv_sems):
        i = o = s = 0
        for r in riders:
            ni, no = len(r.operands), len(r.out_shape)
            yield r, ins[i:i + ni], outs[o:o + no], _Sems(send_sems, s), _Sems(recv_sems, s)
            i, o, s = i + ni, o + no, s + r.n_sems

    def start(*refs):
        for r, *args in parts(*refs):
            r.start(*args)

    def finish(*refs):
        for r, *args in parts(*refs):
            r.finish(*args)

    aliases, i, o = {}, 0, 0
    for r in riders:
        aliases.update({i + a: o + b for a, b in r.aliases.items()})
        i, o = i + len(r.operands), o + len(r.out_shape)
    return _Rider([a for r in riders for a in r.operands], [s for r in riders for s in r.out_shape],
                  sum(r.n_sems for r in riders), start, finish, aliases)


def _gather_rider(shard, rows=None, into=None, before=None, last=True):
    K, Ns = shard.shape
    kh = K // 2
    rows = rows or (0, kh)

    def copies(w_ref, out_ref, send_sems, recv_sems):
        x, y, c = _place()
        chips = [(_flip(x, fx), _flip(y, fy)) for fx, fy in CHIP_FLIPS]

        def half(chip, which, part):
            return out_ref.at[2 * chip[0] + chip[1], pl.ds(which * kh + part[0], part[1]), :]

        def copy(k, dst, to, src=None):
            return pltpu.make_async_remote_copy(src_ref=dst if src is None else src, dst_ref=dst,
                                                send_sem=send_sems.at[k], recv_sem=recv_sems.at[k],
                                                device_id=to, device_id_type=MESH)

        def first():
            return [copy(j, half((x, y), c, rows), (*chip, c), src=w_ref.at[pl.ds(c * kh + rows[0], rows[1]), :])
                    for j, chip in enumerate(chips)]

        def onward(base, part):
            return [copy(base + j, half(chip, c, part), (x, y, 1 - c)) for j, chip in enumerate(chips)]

        def arriving(base, which, part):
            return [copy(base + j, half(chip, which, part), (x, y, c)) for j, chip in enumerate(chips)]

        return c, first, onward, arriving

    def start(ins, outs, send_sems, recv_sems):
        _, first, onward, _ = copies(ins[0], outs[0], send_sems, recv_sems)
        for cp in first() + (onward(3, before) if before else []):
            cp.start()

    def finish(ins, outs, send_sems, recv_sems):
        c, first, onward, arriving = copies(ins[0], outs[0], send_sems, recv_sems)
        sent = first() + (onward(3, before) if before else [])
        passed = onward(6, rows) if last else [None] * 3
        for arrived, cp in zip(arriving(0, c, rows), passed):
            arrived.wait_recv()
            if last:
                cp.start()
        for arrived in (arriving(3, 1 - c, before) if before else []) + (arriving(6, 1 - c, rows) if last else []):
            arrived.wait_recv()
        for cp in sent + (passed if last else []):
            cp.wait_send()

    full = jax.ShapeDtypeStruct((N_CHIPS, K, Ns), shard.dtype)
    if into is None:
        return _Rider([shard], [full], 9, start, finish)
    return _Rider([shard, into], [full], 9, start, finish, aliases={1: 0})


def _pair_rider(g_full):
    Q, K, Ns = g_full.shape
    kh = K // 2

    def copy(g_ref, got_ref, send_sems, recv_sems):
        x, y, c = _place()
        return pltpu.make_async_remote_copy(src_ref=g_ref.at[:, pl.ds((1 - c) * kh, kh), :], dst_ref=got_ref,
                                            send_sem=send_sems.at[0], recv_sem=recv_sems.at[0],
                                            device_id=(x, y, 1 - c), device_id_type=MESH)

    def start(ins, outs, send_sems, recv_sems):
        copy(ins[0], outs[0], send_sems, recv_sems).start()

    def finish(ins, outs, send_sems, recv_sems):
        copy(ins[0], outs[0], send_sems, recv_sems).wait()

    return _Rider([g_full], [jax.ShapeDtypeStruct((Q, kh, Ns), g_full.dtype)], 1, start, finish)


def _acts_rider(a, b):
    T, K = a.shape
    kh = K // 2

    def copies(ins, outs, send_sems, recv_sems):
        x, y, c = _place()
        pair = [(ins[0].at[:, pl.ds((1 - c) * kh, kh)], outs[0]), (ins[1], outs[1])]
        return [pltpu.make_async_remote_copy(src_ref=src, dst_ref=dst, send_sem=send_sems.at[k], recv_sem=recv_sems.at[k],
                                             device_id=(x, y, 1 - c), device_id_type=MESH)
                for k, (src, dst) in enumerate(pair)]

    def start(*refs):
        for cp in copies(*refs):
            cp.start()

    def finish(*refs):
        for cp in copies(*refs):
            cp.wait()

    return _Rider([a, b], [jax.ShapeDtypeStruct((T, kh), a.dtype), jax.ShapeDtypeStruct(b.shape, b.dtype)], 2,
                  start, finish)


def _share_rider(block):
    K, Ns = block.shape
    kh = K // 2

    def halves(out_ref):
        x, y, c = _place()
        return out_ref.at[pl.ds(c * kh, kh), :], out_ref.at[pl.ds((1 - c) * kh, kh), :], (x, y, 1 - c)

    def start(ins, outs, send_sems, recv_sems):
        mine, _, sibling = halves(outs[0])
        pltpu.make_async_remote_copy(src_ref=mine, dst_ref=mine, send_sem=send_sems.at[0], recv_sem=recv_sems.at[0],
                                     device_id=sibling, device_id_type=MESH).start()

    def finish(ins, outs, send_sems, recv_sems):
        mine, theirs, sibling = halves(outs[0])
        pltpu.make_async_remote_copy(src_ref=theirs, dst_ref=theirs, send_sem=send_sems.at[0], recv_sem=recv_sems.at[0],
                                     device_id=sibling, device_id_type=MESH).wait_recv()
        pltpu.make_async_remote_copy(src_ref=mine, dst_ref=mine, send_sem=send_sems.at[0], recv_sem=recv_sems.at[0],
                                     device_id=sibling, device_id_type=MESH).wait_send()

    return _Rider([block], [jax.ShapeDtypeStruct((K, Ns), block.dtype)], 1, start, finish, aliases={0: 0})


def _chips_rider(pair_sum, rows=None, into=None):
    Q, kh, Ns = pair_sum.shape
    first_row, n_rows = rows or (0, kh)

    def copies(p_ref, got_ref, send_sems, recv_sems):
        x, y, c = _place()
        part = pl.ds(first_row, n_rows)
        out = []
        for j, (fx, fy) in enumerate(CHIP_FLIPS):
            px, py = _flip(x, fx), _flip(y, fy)
            out.append(pltpu.make_async_remote_copy(
                src_ref=p_ref.at[2 * px + py, part, :], dst_ref=got_ref.at[j, part, :], send_sem=send_sems.at[j],
                recv_sem=recv_sems.at[j], device_id=(px, py, c), device_id_type=MESH))
        return out

    def start(ins, outs, send_sems, recv_sems):
        for cp in copies(ins[0], outs[0], send_sems, recv_sems):
            cp.start()

    def finish(ins, outs, send_sems, recv_sems):
        sends = copies(ins[0], outs[0], send_sems, recv_sems)
        for cp in sends:
            cp.wait_recv()
        for cp in sends:
            cp.wait_send()

    got = jax.ShapeDtypeStruct((Q - 1, kh, Ns), pair_sum.dtype)
    if into is None:
        return _Rider([pair_sum], [got], 3, start, finish)
    return _Rider([pair_sum, into], [got], 3, start, finish, aliases={1: 0})


def _mm(a, b, *, grid, a_spec, b_spec, o_spec, o_shape, o_dtype, dims, acc_shape, name, rider=None, zero_rows=0,
        vmem_mb=48):
    nk = grid[2]

    def body(a_ref, b_ref, o_ref, *scratch):
        if zero_rows:
            @pl.when(pl.program_id(0) < zero_rows)
            def _():
                o_ref[...] = jnp.zeros_like(o_ref)

            @pl.when(pl.program_id(0) >= zero_rows)
            def _():
                o_ref[...] = _dot(a_ref[...], b_ref[...], dims).astype(o_ref.dtype)
            return
        part = _dot(a_ref[...], b_ref[...], dims)
        if nk == 1:
            o_ref[...] = part.astype(o_ref.dtype)
            return
        acc_ref, = scratch
        k = pl.program_id(2)

        @pl.when(k == 0)
        def _():
            acc_ref[...] = part

        @pl.when(k > 0)
        def _():
            acc_ref[...] += part

        @pl.when(k == nk - 1)
        def _():
            o_ref[...] = acc_ref[...].astype(o_ref.dtype)

    (out,), rode = _call(
        body, rider, name=name, grid=grid, in_specs=[a_spec, b_spec], out_specs=[o_spec],
        out_shape=[jax.ShapeDtypeStruct(o_shape, o_dtype)],
        scratch_shapes=[] if nk == 1 else [pltpu.VMEM(acc_shape, F32)],
        compiler_params=_params(("parallel", "parallel", "arbitrary") if rider is None else ("arbitrary",) * 3, vmem_mb),
        operands=(a, b))
    return out if rider is None else (out, rode)


def _mm_nn(a, w, *, tm, tn, tk, name, rider=None, cols=None, o_dtype=F32, pad_rows=0):
    T, K = a.shape
    Q, _, Ns = w.shape
    nbs = Ns // tn
    tm = min(tm, T)
    j0, j1 = cols or (0, Q * nbs)
    lead = pad_rows // tm
    return _mm(a, w, grid=(lead + T // tm, j1 - j0, K // tk),
               a_spec=pl.BlockSpec((tm, tk), lambda i, j, k: (jnp.maximum(i - lead, 0), k)),
               b_spec=pl.BlockSpec((None, tk, tn), lambda i, j, k: ((j + j0) // nbs, k, (j + j0) % nbs)),
               o_spec=pl.BlockSpec((tm, tn), lambda i, j, k: (i, j)),
               o_shape=(pad_rows + T, (j1 - j0) * tn), o_dtype=o_dtype, dims=NN, acc_shape=(tm, tn), name=name,
               rider=rider, zero_rows=lead)


def _mm_nt(g, w, *, tm, to, tn, name, rider=None):
    T = g.shape[0]
    Q, K, Ns = w.shape
    nbs = Ns // tn
    tm = min(tm, T)
    return _mm(g, w, grid=(T // tm, K // to, Q * nbs),
               a_spec=pl.BlockSpec((tm, tn), lambda i, j, n: (i, n)),
               b_spec=pl.BlockSpec((None, to, tn), lambda i, j, n: (n // nbs, j, n % nbs)),
               o_spec=pl.BlockSpec((tm, to), lambda i, j, n: (i, j)),
               o_shape=(T, K), o_dtype=F32, dims=NT, acc_shape=(tm, to), name=name, rider=rider)


def _mm_tn(a, g, *, q, tk, tn, tt, name, rider=None):
    T, K = a.shape
    Ns = g.shape[1] // q
    nbs = Ns // tn
    return _mm(a, g, grid=(K // tk, q * nbs, T // tt),
               a_spec=pl.BlockSpec((tt, tk), lambda i, j, t: (t, i)),
               b_spec=pl.BlockSpec((tt, tn), lambda i, j, t: (t, j)),
               o_spec=pl.BlockSpec((None, tk, tn), lambda i, j, t: (j // nbs, i, j % nbs)),
               o_shape=(q, K, Ns), o_dtype=BF16, dims=TN, acc_shape=(tk, tn), name=name, rider=rider)


def _mm_tn_add(a, g, part, *, tk, tn, name, rider=None):
    T, K = a.shape
    Q, _, Ns = part.shape
    nbs = Ns // tn

    def body(a_ref, g_ref, p_ref, o_ref):
        o_ref[...] = (_dot(a_ref[...], g_ref[...], TN) + p_ref[...].astype(F32)).astype(o_ref.dtype)

    blk = pl.BlockSpec((None, tk, tn), lambda i, j: (j // nbs, i, j % nbs))
    (out,), rode = _call(
        body, rider, name=name, grid=(K // tk, Q * nbs),
        in_specs=[pl.BlockSpec((T, tk), lambda i, j: (0, i)), pl.BlockSpec((T, tn), lambda i, j: (0, j)), blk],
        out_specs=[blk], out_shape=[jax.ShapeDtypeStruct((Q, K, Ns), BF16)], scratch_shapes=[],
        compiler_params=_params(("arbitrary", "arbitrary"), 48), operands=(a, g, part))
    return out, rode


def _ln(u):
    mu = jnp.mean(u, axis=-1, keepdims=True)
    d = u - mu
    r = lax.rsqrt(jnp.mean(d * d, axis=-1, keepdims=True) + EPS)
    return d * r, r


def _ln_bwd(dy, un, r):
    return r * (dy - jnp.mean(dy, axis=-1, keepdims=True) - un * jnp.mean(dy * un, axis=-1, keepdims=True))


def _colsum(v):
    return jnp.sum(v, axis=0, keepdims=True)


def _rowwise(name, fn, bigs, vecs, out_dtypes, n_acc, tm=128, rider=None):
    T, D = bigs[0].shape
    nb, nv, no = len(bigs), len(vecs), len(out_dtypes)

    def body(*refs):
        outs, accs = fn([r[...] for r in refs[:nb]], [r[...] for r in refs[nb:nb + nv]])
        for r, o in zip(refs[nb + nv:nb + nv + no], outs):
            r[...] = o.astype(r.dtype)
        if n_acc:
            acc_ref = refs[nb + nv + no]

            @pl.when(pl.program_id(0) == 0)
            def _():
                acc_ref[...] = jnp.zeros_like(acc_ref)

            for row, a in enumerate(accs):
                acc_ref[row:row + 1, :] += a

    big_spec = pl.BlockSpec((tm, D), lambda i: (i, 0))
    vec_spec = pl.BlockSpec((1, D), lambda i: (0, 0))
    out_shape = [jax.ShapeDtypeStruct((T, D), dt) for dt in out_dtypes]
    out_specs = [big_spec] * no
    if n_acc:
        out_shape.append(jax.ShapeDtypeStruct((8, D), F32))
        out_specs.append(pl.BlockSpec((8, D), lambda i: (0, 0)))
    outs, rode = _call(
        body, rider, name=name, grid=(T // tm,), in_specs=[big_spec] * nb + [vec_spec] * nv,
        out_specs=out_specs, out_shape=out_shape, scratch_shapes=[],
        compiler_params=_params(("arbitrary",), 48), operands=(*bigs, *vecs))
    return outs if rider is None else (outs, rode)


def _to_bf16(w, name, rider=None):
    R, C = w.shape
    tr = _row_tile(R, C)

    def body(w_ref, o_ref):
        o_ref[...] = w_ref[...].astype(o_ref.dtype)

    blk = pl.BlockSpec((tr, C), lambda i: (i, 0))
    return _call(body, rider, name=name, grid=(R // tr,), in_specs=[blk], out_specs=[blk],
                 out_shape=[jax.ShapeDtypeStruct((R, C), BF16)], scratch_shapes=[],
                 compiler_params=_params(("arbitrary",)), operands=(w,))


def _pre_mixer(x, scale1, shift1):
    def fn(b, v):
        xn, _ = _ln(b[0])
        return [xn * (1.0 + v[0]) + v[1]], []
    return _rowwise("pre_mixer", fn, [x], [scale1, shift1], [BF16], 0)[0]


def _post_mixer(mix, x, gate1, g1, b1, scale2, shift2, rider=None):
    def fn(b, v):
        un1, _ = _ln(ALPHA * b[1] + v[0] * b[0])
        x1 = un1 * v[1] + v[2]
        xn1, _ = _ln(x1)
        return [x1, xn1 * (1.0 + v[3]) + v[4]], []
    return _rowwise("post_mixer", fn, [mix, x], [gate1, g1, b1, scale2, shift2], [F32, BF16], 0, rider=rider)


def _loss_head(f, x1, tgt, gate2, g2, b2):
    def fn(b, v):
        ff, xx, tt = b
        d_model = ff.shape[-1]
        un2, r2 = _ln(ALPHA * xx + v[0] * ff)
        err = un2 * v[1] + v[2] - tt
        dy = err * (1.0 / d_model)
        du2 = _ln_bwd(dy * v[1], un2, r2)
        return [du2, du2 * v[0]], [_colsum(dy * un2), _colsum(dy), _colsum(du2 * ff), _colsum(err * err)]
    return _rowwise("loss_head", fn, [f, x1, tgt], [gate2, g2, b2], [F32, BF16], 4)


def _mid_bwd(dh2, du2, x1, mix, x, gate1, g1, scale2, rider=None):
    def fn(b, v):
        dh, du, xx1, mm, xx = b
        xn1, r1n = _ln(xx1)
        dx1 = ALPHA * du + _ln_bwd(dh * (1.0 + v[2]), xn1, r1n)
        un1, r1 = _ln(ALPHA * xx + v[0] * mm)
        du1 = _ln_bwd(dx1 * v[1], un1, r1)
        return [du1, du1 * v[0]], [_colsum(dh * xn1), _colsum(dh), _colsum(dx1 * un1), _colsum(dx1),
                                   _colsum(du1 * mm)]
    return _rowwise("mid_bwd", fn, [dh2, du2, x1, mix, x], [gate1, g1, scale2], [F32, BF16], 5, rider=rider)


def _first_bwd(dh1, du1, x, scale1, rider=None):
    def fn(b, v):
        dh, du, xx = b
        xn, r0 = _ln(xx)
        return [ALPHA * du + _ln_bwd(dh * (1.0 + v[0]), xn, r0)], [_colsum(dh * xn), _colsum(dh)]
    return _rowwise("first_bwd", fn, [dh1, du1, x], [scale1], [F32], 2, rider=rider)


def _ffn_in_swiglu(h2, w, *, tm, tn, rider=None):
    T, K = h2.shape
    Q, _, Ns = w.shape
    nbs = Ns // tn
    half = Q * nbs // 2
    tm = min(tm, T)

    def body(a_ref, wg_ref, wu_ref, g_ref, u_ref, act_ref):
        a = a_ref[...]
        g, u = _dot(a, wg_ref[...]), _dot(a, wu_ref[...])
        g_ref[...] = g.astype(g_ref.dtype)
        u_ref[...] = u.astype(u_ref.dtype)
        act_ref[...] = (g * jax.nn.sigmoid(g) * u).astype(act_ref.dtype)

    cols = lambda first: pl.BlockSpec((None, K, tn), lambda i, j: ((j + first) // nbs, 0, (j + first) % nbs))
    blk = pl.BlockSpec((tm, tn), lambda i, j: (i, j))
    return _call(
        body, rider, name="ffn_in", grid=(T // tm, half),
        in_specs=[pl.BlockSpec((tm, K), lambda i, j: (i, 0)), cols(0), cols(half)], out_specs=[blk] * 3,
        out_shape=[jax.ShapeDtypeStruct((T, half * tn), BF16)] * 3, scratch_shapes=[],
        compiler_params=_params(("arbitrary", "arbitrary"), 48), operands=(h2, w, w))


def _d_act_swiglu(df, w, gate, up, *, tm, to, rider=None):
    T, N = df.shape
    F = w.shape[1]
    tm = min(tm, T)

    def body(df_ref, w_ref, g_ref, u_ref, dg_ref, du_ref):
        d = _dot(df_ref[...], w_ref[...], NT)
        g = g_ref[...].astype(F32)
        s = jax.nn.sigmoid(g)
        du_ref[...] = (d * g * s).astype(du_ref.dtype)
        dg_ref[...] = (d * u_ref[...].astype(F32) * s * (1.0 + g * (1.0 - s))).astype(dg_ref.dtype)

    blk = pl.BlockSpec((tm, to), lambda i, j: (i, j))
    return _call(
        body, rider, name="d_act", grid=(T // tm, F // to),
        in_specs=[pl.BlockSpec((tm, N), lambda i, j: (i, 0)), pl.BlockSpec((None, to, N), lambda i, j: (0, j, 0)), blk, blk],
        out_specs=[blk, blk], out_shape=[jax.ShapeDtypeStruct((T, F), BF16)] * 2, scratch_shapes=[],
        compiler_params=_params(("arbitrary", "arbitrary"), 48), operands=(df, w, gate, up))


PAIR = 2


def _fill_table(table_ref, band_ref):
    table_ref[...] = jnp.full(table_ref.shape, NEG, F32)
    for e in range(PAIR):
        for g in range(QG):
            table_ref[e, g * CHUNK:(g + 1) * CHUNK, g * CHUNK:g * CHUNK + BAND] = band_ref[e]


def _attn_probs(q_ref, k_ref, bias_ref, e, step):
    start = pl.multiple_of(step * QROWS, QROWS)
    lanes = pl.ds(e * HD_A, HD_A)
    s = _dot(q_ref[:, lanes], k_ref[pl.ds(start + ZPAD - KPAD, UNION), lanes], NT) * (HD_A ** -0.5) + bias_ref[e]
    col = lax.broadcasted_iota(jnp.int32, s.shape, 1)
    s = jnp.where(col + start >= KPAD, s, NEG)
    p = jnp.exp(s - jnp.max(s, axis=-1, keepdims=True))
    return p / jnp.sum(p, axis=-1, keepdims=True), start


def _attn_specs(T, n_pairs):
    wide = PAIR * HD_A
    per_step = pl.BlockSpec((QROWS, wide), lambda hp, n: (n, hp))
    queries = pl.BlockSpec((QROWS, wide), lambda hp, n: (n + ZPAD // QROWS, hp))
    keys = pl.BlockSpec((ZPAD + T, wide), lambda hp, n: (0, n_pairs + hp))
    values = pl.BlockSpec((ZPAD + T, wide), lambda hp, n: (0, 2 * n_pairs + hp))
    grads = pl.BlockSpec((KPAD + T, wide), lambda hp, n: (0, hp))
    table = pl.BlockSpec((PAIR, CHUNK, BAND), lambda hp, n: (hp, 0, 0))
    vec = pl.BlockSpec((1, wide), lambda hp, n: (0, hp))
    return per_step, queries, keys, values, grads, table, vec


def _probs_spec():
    return pl.BlockSpec((PAIR, QROWS, UNION), lambda hp, n: (hp, n, 0))


def _attn_fwd(qkv, bias, gain, rider=None):
    T = qkv.shape[0] - ZPAD
    W = gain.shape[1]
    n_pairs = W // (PAIR * HD_A)

    def body(q_ref, k_ref, v_ref, band_ref, gain_ref, o_ref, p_ref, table_ref):
        @pl.when(pl.program_id(1) == 0)
        def _():
            _fill_table(table_ref, band_ref)

        for e in range(PAIR):
            lanes = pl.ds(e * HD_A, HD_A)
            p, start = _attn_probs(q_ref, k_ref, table_ref, e, pl.program_id(1))
            p_ref[e] = p.astype(p_ref.dtype)
            o = _dot(p_ref[e], v_ref[pl.ds(start + ZPAD - KPAD, UNION), lanes])
            rr = lax.rsqrt(jnp.mean(o * o, axis=-1, keepdims=True) + EPS)
            o_ref[:, lanes] = (o * rr * gain_ref[:, lanes]).astype(o_ref.dtype)

    per_step, queries, keys, values, _, table, vec = _attn_specs(T, n_pairs)
    return _call(
        body, rider, name="attn_fwd", grid=(n_pairs, T // QROWS), in_specs=[queries, keys, values, table, vec],
        out_specs=[per_step, _probs_spec()],
        out_shape=[jax.ShapeDtypeStruct((T, W), BF16), jax.ShapeDtypeStruct((n_pairs * PAIR, T, UNION), BF16)],
        scratch_shapes=[pltpu.VMEM((PAIR, QROWS, UNION), F32)],
        compiler_params=_params(("arbitrary", "arbitrary"), 40), operands=(qkv, qkv, qkv, bias, gain))


def _attn_bwd(qkv, probs, gain, dmixin, rider=None):
    T = qkv.shape[0] - ZPAD
    W = gain.shape[1]
    n_pairs = W // (PAIR * HD_A)
    scale = HD_A ** -0.5

    def body(q_ref, k_ref, v_ref, p_ref, gain_ref, don_ref, dq_ref, dk_ref, dv_ref, dband_ref, dgain_ref, dtable_ref):
        n = pl.program_id(1)

        @pl.when(n == 0)
        def _():
            dk_ref[...] = jnp.zeros_like(dk_ref)
            dv_ref[...] = jnp.zeros_like(dv_ref)
            dtable_ref[...] = jnp.zeros_like(dtable_ref)
            dgain_ref[...] = jnp.zeros_like(dgain_ref)

        for e in range(PAIR):
            lanes = pl.ds(e * HD_A, HD_A)
            start = pl.multiple_of(n * QROWS, QROWS)
            keys, in_qkv = pl.ds(start, UNION), pl.ds(start + ZPAD - KPAD, UNION)
            pb = p_ref[e]
            p = pb.astype(F32)
            vb = v_ref[in_qkv, lanes]
            o = _dot(pb, vb)
            rr = lax.rsqrt(jnp.mean(o * o, axis=-1, keepdims=True) + EPS)
            on = o * rr
            d_on = don_ref[:, lanes]
            dgain_ref[:, lanes] += _colsum(d_on * on)
            dyo = d_on * gain_ref[:, lanes]
            do = rr * (dyo - on * jnp.mean(dyo * on, axis=-1, keepdims=True))
            dob = do.astype(BF16)
            dp = _dot(dob, vb, NT)
            ds = p * (dp - jnp.sum(do * o, axis=-1, keepdims=True))
            dtable_ref[e] += ds
            dsb = ds.astype(BF16)
            dq_ref[:, lanes] = (_dot(dsb, k_ref[in_qkv, lanes]) * scale).astype(dq_ref.dtype)
            dk_ref[keys, lanes] += _dot(dsb, q_ref[:, lanes], TN) * scale
            dv_ref[keys, lanes] += _dot(pb, dob, TN)

        @pl.when(n == T // QROWS - 1)
        def _():
            for e in range(PAIR):
                dband_ref[e] = sum(dtable_ref[e, g * CHUNK:(g + 1) * CHUNK, g * CHUNK:g * CHUNK + BAND]
                                   for g in range(QG))

    per_step, queries, keys, values, grads, table, vec = _attn_specs(T, n_pairs)
    H = n_pairs * PAIR
    return _call(
        body, rider, name="attn_bwd", grid=(n_pairs, T // QROWS),
        in_specs=[queries, keys, values, _probs_spec(), vec, per_step],
        out_specs=[per_step, grads, grads, table, vec],
        out_shape=[jax.ShapeDtypeStruct((T, W), BF16), jax.ShapeDtypeStruct((KPAD + T, W), F32),
                   jax.ShapeDtypeStruct((KPAD + T, W), F32), jax.ShapeDtypeStruct((H, CHUNK, BAND), F32),
                   jax.ShapeDtypeStruct((1, W), F32)],
        scratch_shapes=[pltpu.VMEM((PAIR, QROWS, UNION), F32)],
        compiler_params=_params(("arbitrary", "arbitrary"), 40),
        operands=(qkv, qkv, qkv, probs, gain, dmixin))


N_DIAG = CHUNK + BAND - 1


def _bias_band(rel_bias):
    H = rel_bias.shape[0]
    idx = np.clip(BAND - 1 - np.arange(N_DIAG), -MAX_REL, MAX_REL) + MAX_REL
    rolled = rel_bias[:, idx[(np.arange(N_DIAG) + CHUNK - 1) % N_DIAG]]
    flat = jnp.broadcast_to(rolled[:, None, :], (H, CHUNK, N_DIAG)).reshape(H, CHUNK * N_DIAG)
    return flat[:, :CHUNK * (N_DIAG - 1)].reshape(H, CHUNK, N_DIAG - 1)[:, :, :BAND]


def _bias_band_grad(dband):
    H = dband.shape[0]
    skew = jnp.pad(dband, ((0, 0), (0, 0), (CHUNK - 1, 0))).reshape(H, CHUNK * N_DIAG)
    skew = jnp.pad(skew, ((0, 0), (0, CHUNK))).reshape(H, CHUNK, N_DIAG + 1)
    diag = jnp.sum(skew, axis=1)[:, :N_DIAG]
    n_far = BAND - MAX_REL
    far = jnp.sum(diag[:, :n_far], axis=1, keepdims=True)
    near = diag[:, n_far:][:, ::-1]
    zeros = jnp.zeros((H, MAX_REL - (CHUNK - 1)), F32)
    return jnp.concatenate([zeros, near, far], axis=1)


def _tri(n, lower):
    r = lax.broadcasted_iota(jnp.int32, (n, n), 0)
    c = lax.broadcasted_iota(jnp.int32, (n, n), 1)
    return jnp.where((c <= r) if lower else (c >= r), 1.0, 0.0).astype(F32)


def _hgrn_gates(zq_ref, zf_ref, lbl_ref, q_s, k_s, b_s):
    lb = jax.nn.sigmoid(lbl_ref[0:1, :] - lbl_ref[1:2, :])
    zq = zq_ref[...]
    sig = jax.nn.sigmoid(zf_ref[...])
    f = lb + (1.0 - lb) * sig
    sq = jax.nn.sigmoid(zq)
    q_s[...] = zq * sq
    k_s[...] = 1.0 - f
    b_s[...] = _dot(_tri(CHUNK, True), jnp.log(f), precision=HIGHEST)
    return lb, sig, f, sq


def _sub_rows(i):
    return pl.ds(i * SUB, SUB)


def _row_mask(s):
    return lax.broadcasted_iota(jnp.int32, (SUB, HD_B), 0) >= s


def _decay_from(b_sub, b_row, s):
    return jnp.where(_row_mask(s), jnp.exp(jnp.minimum(b_sub - b_row, 0.0)), 0.0)


def _hgrn_fwd(proj, lb_logits, gnorm_g, rider=None):
    T = proj.shape[0]
    nC = T // CHUNK
    W = lb_logits.shape[1]
    G = W // HD_B // HGRN_HEADS
    col0 = (proj.shape[1] - 4 * W) // (HD_B * HGRN_HEADS)
    wide = HGRN_HEADS * HD_B

    def body(*refs):
        @pl.when(pl.program_id(1) == 0)
        def _():
            refs[9][...] = jnp.zeros_like(refs[9])

        for h in range(HGRN_HEADS):
            lanes = pl.ds(h * HD_B, HD_B)
            one_head(*[r.at[:, lanes] for r in refs[:5]], refs[5], *[r.at[:, lanes] for r in refs[6:8]],
                     *[r.at[h] for r in refs[8:]])

    def one_head(zq_ref, zf_ref, xi_ref, zg_ref, lbl_ref, gn_ref, mix_ref, o_ref, stall_ref, st_ref, q_s, k_s, b_s, acc_s):
        _hgrn_gates(zq_ref, zf_ref, lbl_ref, q_s, k_s, b_s)
        q, k, b = q_s[...], k_s[...], b_s[...]
        st = st_ref[...]
        stall_ref[...] = st
        b_last = b_s[CHUNK - 1:CHUNK, :]
        acc_s[...] = _dot((q * jnp.exp(b)).astype(BF16), st.astype(BF16), NT)
        for i in range(CHUNK // SUB):
            rows = _sub_rows(i)
            q_i, b_i = q_s[rows, :], b_s[rows, :]
            acc = jnp.zeros((SUB, HD_B), F32)
            if i:
                past = pl.ds(0, i * SUB)
                b_ref = b_s[i * SUB - 1:i * SUB, :]
                qs = (q_i * jnp.exp(b_i - b_ref)).astype(BF16)
                ks = (k_s[past, :] * jnp.exp(b_ref - b_s[past, :])).astype(BF16)
                acc += _dot(_dot(qs, ks, NT).astype(BF16), xi_ref[past, :].astype(BF16))
            for s in range(SUB):
                row = pl.ds(i * SUB + s, 1)
                w = q_i * _decay_from(b_i, b_s[row, :], s)
                acc += jnp.sum(w * k_s[row, :], axis=-1, keepdims=True) * xi_ref[row, :]
            acc_s[rows, :] += acc
        o = acc_s[...]
        kd = (k * jnp.exp(b_last - b)).astype(BF16)
        st_ref[...] = st * jnp.exp(b_last) + _dot(xi_ref[...].astype(BF16), kd, TN)
        o_ref[...] = o
        zg = zg_ref[...]
        rr = lax.rsqrt(jnp.mean(o * o, axis=-1, keepdims=True) + EPS)
        mix_ref[...] = (o * rr * gn_ref[...] * (zg * jax.nn.sigmoid(zg))).astype(mix_ref.dtype)

    col = lambda part: pl.BlockSpec((CHUNK, wide), lambda g, n: (n, col0 + part * G + g))
    out_blk = pl.BlockSpec((CHUNK, wide), lambda g, n: (n, g))
    tile = pltpu.VMEM((HGRN_HEADS, CHUNK, HD_B), F32)
    return _call(
        body, rider, name="hgrn_fwd", grid=(G, nC),
        in_specs=[col(0), col(1), col(2), col(3), pl.BlockSpec((2, wide), lambda g, n: (0, g)),
                  pl.BlockSpec((1, HD_B), lambda g, n: (0, 0))],
        out_specs=[out_blk, out_blk, pl.BlockSpec((HGRN_HEADS, None, HD_B, HD_B), lambda g, n: (g, n, 0, 0))],
        out_shape=[jax.ShapeDtypeStruct((T, W), BF16), jax.ShapeDtypeStruct((T, W), F32),
                   jax.ShapeDtypeStruct((G * HGRN_HEADS, nC, HD_B, HD_B), F32)],
        scratch_shapes=[pltpu.VMEM((HGRN_HEADS, HD_B, HD_B), F32), tile, tile, tile, tile],
        compiler_params=_params(("arbitrary", "arbitrary")),
        operands=(proj, proj, proj, proj, lb_logits, gnorm_g))


def _hgrn_bwd(proj, lb_logits, gnorm_g, o_b, st_all, dmixin, rider=None):
    T = proj.shape[0]
    nC = T // CHUNK
    W = lb_logits.shape[1]
    G = W // HD_B // HGRN_HEADS
    wide = HGRN_HEADS * HD_B
    col0 = (proj.shape[1] - 4 * W) // wide
    dcol0 = (dmixin.shape[1] - W) // wide

    def body(*refs):
        g, n = pl.program_id(0), pl.program_id(1)
        dl0_ref, dgn_ref, dst_ref = refs[13:16]

        @pl.when(n == 0)
        def _():
            dst_ref[...] = jnp.zeros_like(dst_ref)
            dl0_ref[...] = jnp.zeros_like(dl0_ref)

        @pl.when((n == 0) & (g == 0))
        def _():
            dgn_ref[...] = jnp.zeros_like(dgn_ref)

        for h in range(HGRN_HEADS):
            lanes = pl.ds(h * HD_B, HD_B)
            cut = lambda r: r.at[:, lanes]
            one_head(*[cut(r) for r in refs[:5]], refs[5], cut(refs[6]), refs[7].at[h], cut(refs[8]),
                     *[cut(r) for r in refs[9:14]], dgn_ref, *[r.at[h] for r in refs[15:]])

    def one_head(zq_ref, zf_ref, xi_ref, zg_ref, lbl_ref, gn_ref, o_ref, st_ref, dout_ref,
                 dzq_ref, dzf_ref, dxi_ref, dzg_ref, dl0_ref, dgn_ref, dst_ref, q_s, k_s, b_s, do_s, dq_s, dk_s, di_s):
        lb, sig, f, sq = _hgrn_gates(zq_ref, zf_ref, lbl_ref, q_s, k_s, b_s)
        q, k, b = q_s[...], k_s[...], b_s[...]
        zg, o, dout = zg_ref[...], o_ref[...], dout_ref[...]
        sg = jax.nn.sigmoid(zg)
        rr = lax.rsqrt(jnp.mean(o * o, axis=-1, keepdims=True) + EPS)
        on = o * rr
        gn = gn_ref[...]
        dzg_ref[...] = (dout * on * gn * sg * (1.0 + zg * (1.0 - sg))).astype(dzg_ref.dtype)
        d_on = dout * zg * sg
        dgn_ref[...] += _colsum(d_on * on)
        d_on = d_on * gn
        do = rr * (d_on - on * jnp.mean(d_on * on, axis=-1, keepdims=True))
        do_s[...] = do
        dob = do.astype(BF16)
        st, dst = st_ref[...], dst_ref[...]
        b_last = b_s[CHUNK - 1:CHUNK, :]
        eb, e_last, k_dec = jnp.exp(b), jnp.exp(b_last), jnp.exp(b_last - b)
        qt, kd = q * eb, k * k_dec
        dstb = dst.astype(BF16)
        xib = xi_ref[...].astype(BF16)
        d_kd = _dot(xib, dstb)
        dq_s[...] = _dot(dob, st.astype(BF16)) * eb
        dk_s[...] = d_kd * k_dec
        di_s[...] = _dot(kd.astype(BF16), dstb, NT)
        d_b_last = e_last * _colsum(st * dst) + _colsum(d_kd * kd)
        dst_ref[...] = _dot(dob, qt.astype(BF16), TN) + dst * e_last
        for i in range(CHUNK // SUB):
            rows = _sub_rows(i)
            q_i, b_i, do_i = q_s[rows, :], b_s[rows, :], do_s[rows, :]
            dq_i = jnp.zeros((SUB, HD_B), F32)
            if i:
                past = pl.ds(0, i * SUB)
                b_ref = b_s[i * SUB - 1:i * SUB, :]
                e_q, e_k = jnp.exp(b_i - b_ref), jnp.exp(b_ref - b_s[past, :])
                qs, ks = (q_i * e_q).astype(BF16), (k_s[past, :] * e_k).astype(BF16)
                xi_p, do_b = xi_ref[past, :].astype(BF16), do_i.astype(BF16)
                di_s[past, :] += _dot(_dot(ks, qs, NT).astype(BF16), do_b)
                dq_i += _dot(_dot(do_b, xi_p, NT).astype(BF16), ks) * e_q
                dk_s[past, :] += _dot(_dot(xi_p, do_b, NT).astype(BF16), qs) * e_k
            for s in range(SUB):
                row = pl.ds(i * SUB + s, 1)
                k_row, i_row = k_s[row, :], xi_ref[row, :]
                e = _decay_from(b_i, b_s[row, :], s)
                w = q_i * e
                a_col = jnp.sum(w * k_row, axis=-1, keepdims=True)
                da_col = jnp.sum(do_i * i_row, axis=-1, keepdims=True)
                di_s[row, :] += _colsum(a_col * do_i)
                dq_i += da_col * e * k_row
                dk_s[row, :] += _colsum(da_col * w)
            dq_s[rows, :] += dq_i
        dq, dk = dq_s[...], dk_s[...]
        db = q * dq - k * dk
        is_last = lax.broadcasted_iota(jnp.int32, (CHUNK, HD_B), 0) == CHUNK - 1
        db = db + jnp.where(is_last, d_b_last, 0.0)
        df = _dot(_tri(CHUNK, False), db, precision=HIGHEST) / f - dk
        dzf_ref[...] = (df * (1.0 - lb) * sig * (1.0 - sig)).astype(dzf_ref.dtype)
        dl0_ref[...] += _colsum(df * (1.0 - sig)) * (lb * (1.0 - lb))
        zq = zq_ref[...]
        dzq_ref[...] = (dq * sq * (1.0 + zq * (1.0 - sq))).astype(dzq_ref.dtype)
        dxi_ref[...] = di_s[...].astype(dxi_ref.dtype)

    rev = lambda n: nC - 1 - n
    col = lambda part: pl.BlockSpec((CHUNK, wide), lambda g, n: (rev(n), col0 + part * G + g))
    blk = pl.BlockSpec((CHUNK, wide), lambda g, n: (rev(n), g))
    tile = pltpu.VMEM((HGRN_HEADS, CHUNK, HD_B), F32)
    out_big = jax.ShapeDtypeStruct((T, W), BF16)
    return _call(
        body, rider, name="hgrn_bwd", grid=(G, nC),
        in_specs=[col(0), col(1), col(2), col(3), pl.BlockSpec((2, wide), lambda g, n: (0, g)),
                  pl.BlockSpec((1, HD_B), lambda g, n: (0, 0)), blk,
                  pl.BlockSpec((HGRN_HEADS, None, HD_B, HD_B), lambda g, n: (g, rev(n), 0, 0)),
                  pl.BlockSpec((CHUNK, wide), lambda g, n: (rev(n), dcol0 + g))],
        out_specs=[blk, blk, blk, blk, pl.BlockSpec((1, wide), lambda g, n: (0, g)),
                   pl.BlockSpec((1, HD_B), lambda g, n: (0, 0))],
        out_shape=[out_big, out_big, out_big, out_big, jax.ShapeDtypeStruct((1, W), F32),
                   jax.ShapeDtypeStruct((1, HD_B), F32)],
        scratch_shapes=[pltpu.VMEM((HGRN_HEADS, HD_B, HD_B), F32)] + [tile] * 7,
        compiler_params=_params(("arbitrary", "arbitrary")),
        operands=(proj, proj, proj, proj, lb_logits, gnorm_g, o_b, st_all, dmixin))


def _adamw_math(g, w, m, v):
    m = B1 * m + (1.0 - B1) * g
    v = B2 * v + (1.0 - B2) * (g * g)
    m_hat = m / (1.0 - B1 ** STEP)
    v_hat = v / (1.0 - B2 ** STEP)
    return -LR * (m_hat / (jnp.sqrt(v_hat) + ADAM_EPS) + WD * w), m, v


def _adamw(g, w, m, v, name):
    R, C = g.shape
    tr = _row_tile(R, C)

    def body(g_ref, w_ref, m_ref, v_ref, go_ref, d_ref, mo_ref, vo_ref):
        g = g_ref[...]
        go_ref[...] = g
        d_ref[...], mo_ref[...], vo_ref[...] = _adamw_math(g, w_ref[...], m_ref[...], v_ref[...])

    blk = pl.BlockSpec((tr, C), lambda i: (i, 0))
    return pl.pallas_call(
        body, name=name, grid=(R // tr,), in_specs=[blk] * 4, out_specs=[blk] * 4,
        out_shape=[jax.ShapeDtypeStruct((R, C), F32)] * 4, compiler_params=_params(("parallel",), 40),
    )(g, w, m, v)


def _sum_pair(g_full, from_sibling, sel, name):
    Q, K, Ns = g_full.shape
    kh = K // 2
    tr = _row_tile(kh, Ns)
    nh = kh // tr

    def body(sel_ref, a_ref, b_ref, o_ref):
        o_ref[...] = (a_ref[...].astype(F32) + b_ref[...].astype(F32)).astype(o_ref.dtype)

    return pl.pallas_call(
        body, name=name,
        grid_spec=pltpu.PrefetchScalarGridSpec(
            num_scalar_prefetch=1, grid=(Q, nh),
            in_specs=[pl.BlockSpec((None, tr, Ns), lambda q, i, sel: (q, sel[1] * nh + i, 0)),
                      pl.BlockSpec((None, tr, Ns), lambda q, i, sel: (q, i, 0))],
            out_specs=pl.BlockSpec((None, tr, Ns), lambda q, i, sel: (q, i, 0))),
        out_shape=jax.ShapeDtypeStruct((Q, kh, Ns), BF16), compiler_params=_params(("parallel", "parallel")),
    )(sel, g_full, from_sibling)


def _sum_chips(pair_sum, from_chips, sel, name):
    Q, kh, Ns = pair_sum.shape
    tr = _row_tile(kh, Ns)
    nh = kh // tr

    def body(sel_ref, a_ref, b0_ref, b1_ref, b2_ref, o_ref):
        up = lambda r: r[...].astype(F32)
        o_ref[...] = ((up(a_ref) + up(b0_ref)) + up(b1_ref)) + up(b2_ref)

    recv = lambda k: pl.BlockSpec((None, tr, Ns), lambda i, sel: (k, i, 0))
    return pl.pallas_call(
        body, name=name,
        grid_spec=pltpu.PrefetchScalarGridSpec(
            num_scalar_prefetch=1, grid=(nh,),
            in_specs=[pl.BlockSpec((None, tr, Ns), lambda i, sel: (sel[0], i, 0)), recv(0), recv(1), recv(2)],
            out_specs=pl.BlockSpec((tr, Ns), lambda i, sel: (sel[1] * nh + i, 0))),
        out_shape=jax.ShapeDtypeStruct((2 * kh, Ns), F32), compiler_params=_params(("parallel",)),
    )(sel, pair_sum, from_chips, from_chips, from_chips)


def _gather_small(v, name):
    R, L = v.shape

    def body(v_ref, out_ref, send_sems, recv_sems):
        x, y, c = _place()
        me = 4 * x + 2 * y + c
        out_ref[me] = v_ref[...]
        peers = [(_flip(x, k >> 2 & 1), _flip(y, k >> 1 & 1), _flip(c, k & 1)) for k in range(1, N_DEV)]

        def copy(k, row, to):
            return pltpu.make_async_remote_copy(src_ref=v_ref, dst_ref=out_ref.at[row], send_sem=send_sems.at[k],
                                                recv_sem=recv_sems.at[k], device_id=to, device_id_type=MESH)

        sends = [copy(k, me, peer) for k, peer in enumerate(peers)]
        for cp in sends:
            cp.start()
        for k, (px, py, pc) in enumerate(peers):
            copy(k, 4 * px + 2 * py + pc, (x, y, c)).wait_recv()
        for cp in sends:
            cp.wait_send()

    vmem = pl.BlockSpec(memory_space=pltpu.VMEM)
    return pl.pallas_call(
        body, name=name, in_specs=[vmem], out_specs=vmem, out_shape=jax.ShapeDtypeStruct((N_DEV, R, L), F32),
        scratch_shapes=[pltpu.SemaphoreType.DMA((N_DEV - 1,)), pltpu.SemaphoreType.DMA((N_DEV - 1,))],
    )(v)


def _silu(v):
    return v * jax.nn.sigmoid(v)


def _ada_fwd(c_all, w_ada, tn=512):
    M, D = c_all.shape
    Ns = w_ada.shape[1]

    def body(c_ref, w_ref, o_ref):
        o_ref[...] = _dot(_silu(c_ref[...]).astype(BF16), w_ref[...].astype(BF16))

    return pl.pallas_call(
        body, name="ada_fwd", grid=(Ns // tn,),
        in_specs=[pl.BlockSpec((M, D), lambda j: (0, 0)), pl.BlockSpec((D, tn), lambda j: (0, j))],
        out_specs=pl.BlockSpec((M, tn), lambda j: (0, j)), out_shape=jax.ShapeDtypeStruct((M, Ns), F32),
        compiler_params=_params(("parallel",)),
    )(c_all, w_ada)


def _ada_bwd(c_all, dmod, w, m, v, tk=256, tn=1536):
    M, D = c_all.shape
    Ns = dmod.shape[1]

    def body(c_ref, d_ref, w_ref, m_ref, v_ref, g_ref, dl_ref, mo_ref, vo_ref):
        g = _dot(_silu(c_ref[...]).astype(BF16), d_ref[...].astype(BF16), TN)
        g_ref[...] = g
        dl_ref[...], mo_ref[...], vo_ref[...] = _adamw_math(g, w_ref[...], m_ref[...], v_ref[...])

    blk = pl.BlockSpec((tk, tn), lambda i, j: (i, j))
    return pl.pallas_call(
        body, name="ada_bwd", grid=(D // tk, Ns // tn),
        in_specs=[pl.BlockSpec((M, tk), lambda i, j: (0, i)), pl.BlockSpec((M, tn), lambda i, j: (0, j)), blk, blk, blk],
        out_specs=[blk] * 4, out_shape=[jax.ShapeDtypeStruct((D, Ns), F32)] * 4,
        compiler_params=_params(("parallel", "parallel"), 40),
    )(c_all, dmod, w, m, v)


def _small_update(g_all, w, m, v):
    R, L = w.shape

    def body(g_ref, w_ref, m_ref, v_ref, go_ref, d_ref, mo_ref, vo_ref):
        g = g_ref[0]
        for d in range(1, N_DEV):
            g = g + g_ref[d]
        go_ref[...] = g
        d_ref[...], mo_ref[...], vo_ref[...] = _adamw_math(g, w_ref[...], m_ref[...], v_ref[...])

    return pl.pallas_call(body, name="small_update", out_shape=[jax.ShapeDtypeStruct((R, L), F32)] * 4)(g_all, w, m, v)


def _pack(parts, rows):
    flat = jnp.concatenate([p.reshape(-1) for p in parts])
    return jnp.pad(flat, (0, rows * 128 - flat.shape[0])).reshape(rows, 128)


def _unpack(packed, shapes):
    flat, out, at = packed.reshape(-1), [], 0
    for shp in shapes:
        size = 1
        for d in shp:
            size *= d
        out.append(flat[at:at + size].reshape(shp))
        at += size
    return out


def _layer(x, tgt, mod, wts, rel_bias, attn_norm_g, lb_logits, gnorm_g, ln1_g, ln1_b, ln2_g, ln2_b, place=None):
    T, D = x.shape
    aw = attn_norm_g.shape[1]
    shift1, scale1, gate1, shift2, scale2, gate2 = [mod[i:i + 1] for i in range(6)]

    def gather(n, rows=None, into=None, before=None, last=True):
        return None if place is None else _gather_rider(wts[n], rows, None if into is None else into[0], before, last)

    def gathered(n, rode):
        return wts[n] if place is None else lax.dynamic_update_index_in_dim(rode[0], wts[n], place[0], 0)

    def blocks(g):
        return g.reshape(N_CHIPS, -1, g.shape[2])

    def to_sibling(g):
        return None if place is None else _pair_rider(g)

    def pair_sum(n, g, rode=None):
        if place is None:
            return g
        rode = _alone(_pair_rider(g), n + "_send_pair") if rode is None else rode
        return _sum_pair(g, rode[0], place[1], n + "_sum_pair")

    def to_chips(p, rows=None, into=None):
        return None if place is None else _chips_rider(p, rows, None if into is None else into[0])

    def summed(n, p, rode):
        return p if place is None else _sum_chips(p, rode[0], place[1], n + "_sum_chips")

    def to_both(block):
        return None if place is None else _share_rider(block)

    def carrying(mm, *args, rider, **kw):
        return mm(*args, rider=rider, **kw) if rider is not None else (mm(*args, **kw), None)

    def to_sibling_acts(a, b):
        return None if place is None else _acts_rider(a, b)

    def pair_grad(name, a, b, tn, rider, arrived=None, late_rider=None):
        if place is None:
            return _mm_tn(a, b, q=N_CHIPS, tk=512, tn=tn, tt=T, name=name), None
        kh = a.shape[1] // 2
        mine = lax.dynamic_slice_in_dim(a, place[1][1] * kh, kh, axis=1)
        part, rode = carrying(_mm_tn, mine, b, q=N_CHIPS, tk=512, tn=tn, tt=T, name=name + "_own",
                              rider=_join(None if arrived else _acts_rider(a, b), rider))
        (a_sib, b_sib), rode = arrived or rode[:2], rode if arrived else rode[2:]
        out, late = _mm_tn_add(a_sib, b_sib, part, tk=512, tn=tn, name=name + "_sib", rider=late_rider)
        return out, (rode or []) + late

    w_in = gathered("w_in", None if place is None else [wts["w_in_gathered"]])
    h1 = _pre_mixer(x, scale1, shift1)
    n_qkv = 3 * aw // 256
    kh_o, kh_f, kh_out = [wts[n].shape[-2] // 2 for n in ("w_o", "w_ffn_in", "w_ffn_out")]
    o_cut, f_cuts, out_cut = 3 * kh_o // 8, (7 * kh_f // 16, 7 * kh_f // 8), kh_out // 11
    qkv, rode = carrying(_mm_nn, h1, w_in, tm=ZPAD, tn=256, tk=D, name="proj_qkv", cols=(0, n_qkv), o_dtype=BF16,
                         pad_rows=ZPAD, rider=gather("w_o", (0, o_cut), last=False))
    proj, rode = carrying(_mm_nn, h1, w_in, tm=2048, tn=256, tk=D, name="proj_rec",
                          cols=(n_qkv, N_CHIPS * w_in.shape[2] // 256),
                          rider=gather("w_o", (o_cut, kh_o - o_cut), rode, before=(0, o_cut)))
    w_o3 = gathered("w_o", rode).reshape(1, D, D)
    bias = _bias_band(rel_bias)
    (mix_a, probs), rode = _attn_fwd(qkv, bias, attn_norm_g, rider=gather("w_ffn_in", (0, f_cuts[0]), last=False))
    (mix_b, o_b, st_all), rode = _hgrn_fwd(
        proj, lb_logits, gnorm_g,
        rider=gather("w_ffn_in", (f_cuts[0], f_cuts[1] - f_cuts[0]), rode, before=(0, f_cuts[0]), last=False))
    mixin = jnp.concatenate([mix_a, mix_b], axis=1)
    mix = _mm_nn(mixin, w_o3, tm=1024, tn=512, tk=D, name="mix_out")
    if place is None:
        x1, h2 = _post_mixer(mix, x, gate1, ln1_g, ln1_b, scale2, shift2)
    else:
        (x1, h2), rode = _post_mixer(mix, x, gate1, ln1_g, ln1_b, scale2, shift2, rider=_join(
            gather("w_ffn_in", (f_cuts[1], kh_f - f_cuts[1]), rode, before=(f_cuts[0], f_cuts[1] - f_cuts[0])),
            gather("w_ffn_out", (0, out_cut), last=False)))
    w_ffn_in = gathered("w_ffn_in", rode)
    (gate, up, act), rode = _ffn_in_swiglu(
        h2, w_ffn_in, tm=2048, tn=256,
        rider=gather("w_ffn_out", (out_cut, kh_out - out_cut), rode and rode[1:], before=(0, out_cut)))
    w_out3 = gathered("w_ffn_out", rode)
    w_out3 = w_out3.reshape(1, -1, w_out3.shape[2])
    d_ff = w_out3.shape[1]
    f = _mm_nn(act, w_out3, tm=1024, tn=512, tk=d_ff, name="ffn_out")
    du2, df, acc2 = _loss_head(f, x1, tgt, gate2, ln2_g, ln2_b)
    loss = (0.5 / D) * jnp.sum(acc2[3])
    g = blocks(_mm_tn(act, df, q=1, tk=512, tn=1024, tt=T, name="g_ffn_out"))
    d_gate_up, rode = _d_act_swiglu(df, w_out3, gate, up, tm=1024, to=512, rider=to_sibling(g))
    p_out = pair_sum("w_ffn_out", g, rode)
    dff = jnp.concatenate(d_gate_up, axis=1)
    cut = 21 * p_out.shape[1] // 44
    dh2, rode = carrying(_mm_nt, dff, w_ffn_in, tm=1024, to=1024, tn=w_ffn_in.shape[2], name="d_h2",
                         rider=_join(to_chips(p_out, (0, cut)), to_sibling_acts(h2, dff)))
    p_fin, rode = pair_grad("g_ffn_in", h2, dff, w_ffn_in.shape[2] // 2,
                            to_chips(p_out, (cut, p_out.shape[1] - cut), rode), arrived=rode and rode[1:])
    g_ffn_out = summed("w_ffn_out", p_out, rode)
    if place is None:
        du1, dmix, acc1 = _mid_bwd(dh2, du2, x1, mix, x, gate1, ln1_g, scale2)
    else:
        (du1, dmix, acc1), (g_ffn_out,) = _mid_bwd(dh2, du2, x1, mix, x, gate1, ln1_g, scale2, rider=to_both(g_ffn_out))
    g = blocks(_mm_tn(mixin, dmix, q=1, tk=512, tn=1024, tt=T, name="g_o"))
    dmixin, rode = carrying(_mm_nt, dmix, w_o3, tm=1024, to=512, tn=D, name="d_mixin", rider=to_sibling(g))
    p_o = pair_sum("w_o", g, rode)
    cut = p_fin.shape[1] // 2
    (dq, dk, dv, dbias, dgain), rode = _attn_bwd(qkv, probs, attn_norm_g, dmixin, rider=to_chips(p_fin, (0, cut)))
    (dzq, dzf, dxi, dzg, dl0, dgn), rode = _hgrn_bwd(
        proj, lb_logits, gnorm_g, o_b, st_all, dmixin,
        rider=_join(to_chips(p_fin, (cut, p_fin.shape[1] - cut), rode), to_chips(p_o)))
    g_ffn_in, g_o = summed("w_ffn_in", p_fin, rode[:1]), summed("w_o", p_o, rode[1:])
    dproj = jnp.concatenate([dq, dk[KPAD:].astype(BF16), dv[KPAD:].astype(BF16), dzq, dzf, dxi, dzg], axis=1)
    p_in, rode = pair_grad("g_in", h1, dproj, w_in.shape[2] // 2, None,
                           late_rider=_join(to_both(g_ffn_in), to_both(g_o)))
    if place is not None:
        g_ffn_in, g_o = rode
    cut = 3 * p_in.shape[1] // 4
    dh1, rode = carrying(_mm_nt, dproj, w_in, tm=1024, to=1024, tn=w_in.shape[2], name="d_h1",
                         rider=to_chips(p_in, (0, cut)))
    if place is None:
        (grad_x, acc0), g_in = _first_bwd(dh1, du1, x, scale1), p_in
    else:
        (grad_x, acc0), rode = _first_bwd(dh1, du1, x, scale1, rider=to_chips(p_in, (cut, p_in.shape[1] - cut), rode))
        g_in, = _alone(to_both(summed("w_in", p_in, rode)), "w_in_share")
    dmod = jnp.concatenate([acc0[1:2], acc0[0:1], acc1[4:5], acc1[1:2], acc1[0:1], acc2[2:3]], axis=0)
    small = dict(rel_bias=_bias_band_grad(dbias), attn_norm_g=dgain,
                 lb_logits=jnp.concatenate([dl0, -dl0], axis=0), gnorm_g=dgn,
                 ln1_g=acc1[2:3], ln1_b=acc1[3:4], ln2_g=acc2[0:1], ln2_b=acc2[1:2])
    return loss, grad_x, dict(w_in=g_in, w_o=g_o, w_ffn_in=g_ffn_in, w_ffn_out=g_ffn_out), dmod, small


SMALL = ("rel_bias", "attn_norm_g", "lb_logits", "gnorm_g", "ln1_g", "ln1_b", "ln2_g", "ln2_b")
SMALL_ROWS = 256


def kernel(x, c, w_ada, b_ada, w_in, rel_bias, attn_norm_g, lb_logits, gnorm_g, w_o, ln1_g, ln1_b, w_ffn_in, w_ffn_out, ln2_g, ln2_b, loss_target, m_w_ada, m_b_ada, m_w_in, m_rel_bias, m_attn_norm_g, m_lb_logits, m_gnorm_g, m_w_o, m_ln1_g, m_ln1_b, m_w_ffn_in, m_w_ffn_out, m_ln2_g, m_ln2_b, v_w_ada, v_b_ada, v_w_in, v_rel_bias, v_attn_norm_g, v_lb_logits, v_gnorm_g, v_w_o, v_ln1_g, v_ln1_b, v_w_ffn_in, v_w_ffn_out, v_ln2_g, v_ln2_b):
    mx, my, mc = _place()
    me = 4 * mx + 2 * my + mc
    chip = 2 * mx + my
    sel = jnp.stack([chip, mc]).astype(jnp.int32)
    D = x.shape[2]
    ns_ada = w_ada.shape[2]

    big = dict(w_in=(w_in, m_w_in, v_w_in), w_o=(w_o, m_w_o, v_w_o), w_ffn_in=(w_ffn_in, m_w_ffn_in, v_w_ffn_in),
               w_ffn_out=(w_ffn_out, m_w_ffn_out, v_w_ffn_out))
    shards = dict(w_in=w_in[0].astype(BF16))
    kh, rode, rows = shards["w_in"].shape[0] // 2, None, None
    for n, part in (("w_ffn_in", 19), ("w_ffn_out", 9), ("w_o", 4)):
        before, rows = rows, (rows[0] + rows[1] if rows else 0, part * kh // 32)
        (shards[n],), rode = _to_bf16(big[n][0][0], "cast_" + n,
                                      _gather_rider(shards["w_in"], rows, rode and rode[0], before, last=n == "w_o"))
    shards["w_in_gathered"] = rode[0]

    c_all = _gather_small(c.reshape(D // 128, 128), "gather_c").reshape(N_DEV, D)
    c_all = jnp.pad(c_all, ((0, 16 - N_DEV), (0, 0)))
    mod_cols = _ada_fwd(c_all, w_ada[0])[:N_DEV]
    mod_all = _gather_small(mod_cols.reshape(-1, 128), "gather_mod").reshape(N_DEV, N_DEV, ns_ada)
    mod = lax.dynamic_index_in_dim(mod_all[::2], me, axis=1, keepdims=False)
    mod = (mod.reshape(1, -1) + b_ada).reshape(6, D)

    loss, grad_x, g_big, dmod, g_small = _layer(
        x[0], loss_target[0], mod, shards, rel_bias[0], attn_norm_g, lb_logits, gnorm_g, ln1_g, ln1_b, ln2_g, ln2_b,
        place=(chip, sel))

    grads, deltas, new_m, new_v = {}, {}, {}, {}
    for n, (w, m, v) in big.items():
        g, d, mo, vo = _adamw(g_big[n], w[0], m[0], v[0], "adamw_" + n)
        grads[n], deltas[n], new_m[n], new_v[n] = g[None], d[None], mo[None], vo[None]

    small_in = dict(rel_bias=(rel_bias, m_rel_bias, v_rel_bias), attn_norm_g=(attn_norm_g, m_attn_norm_g, v_attn_norm_g),
                    lb_logits=(lb_logits, m_lb_logits, v_lb_logits), gnorm_g=(gnorm_g, m_gnorm_g, v_gnorm_g),
                    ln1_g=(ln1_g, m_ln1_g, v_ln1_g), ln1_b=(ln1_b, m_ln1_b, v_ln1_b), ln2_g=(ln2_g, m_ln2_g, v_ln2_g),
                    ln2_b=(ln2_b, m_ln2_b, v_ln2_b))
    g_all = _gather_small(_pack([dmod] + [g_small[n] for n in SMALL] + [loss], SMALL_ROWS), "gather_small")
    packed = [_pack([t] + [small_in[n][i] for n in SMALL] + [jnp.zeros((), F32)], SMALL_ROWS)
              for i, t in enumerate((b_ada, m_b_ada, v_b_ada))]
    shapes = [b_ada.shape] + [small_in[n][0].shape for n in SMALL] + [()]
    outs = [_unpack(o, shapes) for o in _small_update(g_all, *packed)]
    loss = outs[0][-1]
    for i, n in enumerate(("b_ada",) + SMALL):
        grads[n], deltas[n], new_m[n], new_v[n] = outs[0][i], outs[1][i], outs[2][i], outs[3][i]

    dmod_all = g_all[:, :6 * D // 128].reshape(N_DEV, 6 * D)
    dmod_cols = lax.dynamic_slice_in_dim(dmod_all, chip * ns_ada, ns_ada, axis=1)
    dmod_cols = jnp.pad(dmod_cols, ((0, 16 - N_DEV), (0, 0)))
    g, d, mo, vo = _ada_bwd(c_all, dmod_cols, w_ada[0], m_w_ada[0], v_w_ada[0])
    grads["w_ada"], deltas["w_ada"], new_m["w_ada"], new_v["w_ada"] = g[None], d[None], mo[None], vo[None]

    order = ("w_ada", "b_ada", "w_in", "rel_bias", "attn_norm_g", "lb_logits", "gnorm_g", "w_o", "ln1_g", "ln1_b",
             "w_ffn_in", "w_ffn_out", "ln2_g", "ln2_b")
    return (loss, grad_x[None], *[grads[n] for n in order], *[deltas[n] for n in order],
            *[new_m[n] for n in order], *[new_v[n] for n in order])
```

```python
import numpy as np
import jax
import jax.numpy as jnp
from jax import lax
from jax.experimental import pallas as pl
from jax.experimental.pallas import tpu as pltpu

F32 = jnp.float32
BF16 = jnp.bfloat16
MESH = pl.DeviceIdType.MESH
HIGHEST = lax.Precision.HIGHEST

CHUNK = 64
N_PAST = 8
QG = 4
QROWS = QG * CHUNK
KPAD = N_PAST * CHUNK
ZPAD = 2 * KPAD
UNION = (QG + N_PAST) * CHUNK
BAND = (N_PAST + 1) * CHUNK
HD_A = 64
HD_B = 128
SUB = 16
HGRN_HEADS = 8
MAX_REL = 256
EPS = 1e-5
ALPHA = 2.0 ** 0.25
LR, B1, B2, ADAM_EPS, WD, STEP = 1e-3, 0.9, 0.999, 1e-8, 0.01, 10
N_CHIPS = 4
N_DEV = 8
NEG = -1e30
TILE_BYTES = 3 << 19

NN = ((1,), (0,))
NT = ((1,), (1,))
TN = ((0,), (0,))


def _dot(a, b, dims=NN, precision=None):
    return lax.dot_general(a, b, (dims, ((), ())), preferred_element_type=F32, precision=precision)


def _params(sem=None, vmem_mb=None, **kw):
    return pltpu.CompilerParams(dimension_semantics=sem,
                                vmem_limit_bytes=None if vmem_mb is None else vmem_mb << 20, **kw)


def _row_tile(rows, cols):
    for cand in (512, 256, 128, 64, 32, 16, 8):
        if rows % cand == 0 and cand * cols * 4 <= TILE_BYTES:
            return cand
    raise ValueError((rows, cols))


def _place():
    return lax.axis_index("x"), lax.axis_index("y"), lax.axis_index("c")


def _flip(v, bit):
    return 1 - v if bit else v


ANY = pl.BlockSpec(memory_space=pl.ANY)
CHIP_FLIPS = ((1, 0), (0, 1), (1, 1))


class _Rider:
    def __init__(self, operands, out_shape, n_sems, start, finish, aliases=None):
        self.operands, self.out_shape, self.n_sems, self.start, self.finish = operands, out_shape, n_sems, start, finish
        self.aliases = aliases or {}


def _call(body, rider, *, name, grid, in_specs, out_specs, out_shape, scratch_shapes, compiler_params, operands):
    if rider is None:
        outs = pl.pallas_call(body, name=name, grid=grid, in_specs=in_specs, out_specs=out_specs, out_shape=out_shape,
                              scratch_shapes=scratch_shapes, compiler_params=compiler_params)(*operands)
        return list(outs), []
    n_in, n_out, n_sc = len(in_specs), len(out_specs), len(scratch_shapes)
    r_in, r_out = len(rider.operands), len(rider.out_shape)

    def carried(*refs):
        refs = list(refs)
        cuts = [n_in, r_in, n_out, r_out, n_sc]
        ins, r_ins, outs, r_outs, scratch = [[refs.pop(0) for _ in range(n)] for n in cuts]
        first, last = None, None
        for axis, size in enumerate(grid):
            i = pl.program_id(axis)
            first = (i == 0) if first is None else first & (i == 0)
            last = (i == size - 1) if last is None else last & (i == size - 1)

        @pl.when(first)
        def _():
            rider.start(r_ins, r_outs, *refs)

        body(*ins, *outs, *scratch)

        @pl.when(last)
        def _():
            rider.finish(r_ins, r_outs, *refs)

    sems = [pltpu.SemaphoreType.DMA((rider.n_sems,)), pltpu.SemaphoreType.DMA((rider.n_sems,))]
    outs = pl.pallas_call(carried, name=name, grid=grid, in_specs=list(in_specs) + [ANY] * r_in,
                          out_specs=list(out_specs) + [ANY] * r_out, out_shape=list(out_shape) + rider.out_shape,
                          scratch_shapes=list(scratch_shapes) + sems, compiler_params=compiler_params,
                          input_output_aliases={n_in + i: n_out + o for i, o in rider.aliases.items()},
                          )(*operands, *rider.operands)
    return list(outs[:n_out]), list(outs[n_out:])


def _alone(rider, name):
    def body(*refs):
        ins, outs, sems = refs[:len(rider.operands)], refs[len(rider.operands):-2], refs[-2:]
        rider.start(ins, outs, *sems)
        rider.finish(ins, outs, *sems)

    return pl.pallas_call(
        body, name=name, in_specs=[ANY] * len(rider.operands), out_specs=[ANY] * len(rider.out_shape),
        out_shape=rider.out_shape, input_output_aliases=rider.aliases,
        scratch_shapes=[pltpu.SemaphoreType.DMA((rider.n_sems,)), pltpu.SemaphoreType.DMA((rider.n_sems,))],
    )(*rider.operands)


class _Sems:
    def __init__(self, sems, base):
        self.sems, self.base = sems, base

    @property
    def at(self):
        return self

    def __getitem__(self, k):
        return self.sems.at[self.base + k]


def _join(*riders):
    riders = [r for r in riders if r is not None]
    if len(riders) < 2:
        return riders[0] if riders else None

    def parts(ins, outs, send_sems, recv_sems):
        i = o = s = 0
        for r in riders:
            ni, no = len(r.operands), len(r.out_shape)
            yield r, ins[i:i + ni], outs[o:o + no], _Sems(send_sems, s), _Sems(recv_sems, s)
            i, o, s = i + ni, o + no, s + r.n_sems

    def start(*refs):
        for r, *args in parts(*refs):
            r.start(*args)

    def finish(*refs):
        for r, *args in parts(*refs):
            r.finish(*args)

    aliases, i, o = {}, 0, 0
    for r in riders:
        aliases.update({i + a: o + b for a, b in r.aliases.items()})
        i, o = i + len(r.operands), o + len(r.out_shape)
    return _Rider([a for r in riders for a in r.operands], [s for r in riders for s in r.out_shape],
                  sum(r.n_sems for r in riders), start, finish, aliases)


def _gather_rider(shard, rows=None, into=None, before=None, last=True):
    K, Ns = shard.shape
    kh = K // 2
    rows = rows or (0, kh)

    def copies(w_ref, out_ref, send_sems, recv_sems):
        x, y, c = _place()
        chips = [(_flip(x, fx), _flip(y, fy)) for fx, fy in CHIP_FLIPS]

        def half(chip, which, part):
            return out_ref.at[2 * chip[0] + chip[1], pl.ds(which * kh + part[0], part[1]), :]

        def copy(k, dst, to, src=None):
            return pltpu.make_async_remote_copy(src_ref=dst if src is None else src, dst_ref=dst,
                                                send_sem=send_sems.at[k], recv_sem=recv_sems.at[k],
                                                device_id=to, device_id_type=MESH)

        def first():
            return [copy(j, half((x, y), c, rows), (*chip, c), src=w_ref.at[pl.ds(c * kh + rows[0], rows[1]), :])
                    for j, chip in enumerate(chips)]

        def onward(base, part):
            return [copy(base + j, half(chip, c, part), (x, y, 1 - c)) for j, chip in enumerate(chips)]

        def arriving(base, which, part):
            return [copy(base + j, half(chip, which, part), (x, y, c)) for j, chip in enumerate(chips)]

        return c, first, onward, arriving

    def start(ins, outs, send_sems, recv_sems):
        _, first, onward, _ = copies(ins[0], outs[0], send_sems, recv_sems)
        for cp in first() + (onward(3, before) if before else []):
            cp.start()

    def finish(ins, outs, send_sems, recv_sems):
        c, first, onward, arriving = copies(ins[0], outs[0], send_sems, recv_sems)
        sent = first() + (onward(3, before) if before else [])
        passed = onward(6, rows) if last else [None] * 3
        for arrived, cp in zip(arriving(0, c, rows), passed):
            arrived.wait_recv()
            if last:
                cp.start()
        for arrived in (arriving(3, 1 - c, before) if before else []) + (arriving(6, 1 - c, rows) if last else []):
            arrived.wait_recv()
        for cp in sent + (passed if last else []):
            cp.wait_send()

    full = jax.ShapeDtypeStruct((N_CHIPS, K, Ns), shard.dtype)
    if into is None:
        return _Rider([shard], [full], 9, start, finish)
    return _Rider([shard, into], [full], 9, start, finish, aliases={1: 0})


def _pair_rider(g_full):
    Q, K, Ns = g_full.shape
    kh = K // 2

    def copy(g_ref, got_ref, send_sems, recv_sems):
        x, y, c = _place()
        return pltpu.make_async_remote_copy(src_ref=g_ref.at[:, pl.ds((1 - c) * kh, kh), :], dst_ref=got_ref,
                                            send_sem=send_sems.at[0], recv_sem=recv_sems.at[0],
                                            device_id=(x, y, 1 - c), device_id_type=MESH)

    def start(ins, outs, send_sems, recv_sems):
        copy(ins[0], outs[0], send_sems, recv_sems).start()

    def finish(ins, outs, send_sems, recv_sems):
        copy(ins[0], outs[0], send_sems, recv_sems).wait()

    return _Rider([g_full], [jax.ShapeDtypeStruct((Q, kh, Ns), g_full.dtype)], 1, start, finish)


def _acts_rider(a, b):
    T, K = a.shape
    kh = K // 2

    def copies(ins, outs, send_sems, recv_sems):
        x, y, c = _place()
        pair = [(ins[0].at[:, pl.ds((1 - c) * kh, kh)], outs[0]), (ins[1], outs[1])]
        return [pltpu.make_async_remote_copy(src_ref=src, dst_ref=dst, send_sem=send_sems.at[k], recv_sem=recv_sems.at[k],
                                             device_id=(x, y, 1 - c), device_id_type=MESH)
                for k, (src, dst) in enumerate(pair)]

    def start(*refs):
        for cp in copies(*refs):
            cp.start()

    def finish(*refs):
        for cp in copies(*refs):
            cp.wait()

    return _Rider([a, b], [jax.ShapeDtypeStruct((T, kh), a.dtype), jax.ShapeDtypeStruct(b.shape, b.dtype)], 2,
                  start, finish)


def _share_rider(block):
    K, Ns = block.shape
    kh = K // 2

    def halves(out_ref):
        x, y, c = _place()
        return out_ref.at[pl.ds(c * kh, kh), :], out_ref.at[pl.ds((1 - c) * kh, kh), :], (x, y, 1 - c)

    def start(ins, outs, send_sems, recv_sems):
        mine, _, sibling = halves(outs[0])
        pltpu.make_async_remote_copy(src_ref=mine, dst_ref=mine, send_sem=send_sems.at[0], recv_sem=recv_sems.at[0],
                                     device_id=sibling, device_id_type=MESH).start()

    def finish(ins, outs, send_sems, recv_sems):
        mine, theirs, sibling = halves(outs[0])
        pltpu.make_async_remote_copy(src_ref=theirs, dst_ref=theirs, send_sem=send_sems.at[0], recv_sem=recv_sems.at[0],
                                     device_id=sibling, device_id_type=MESH).wait_recv()
        pltpu.make_async_remote_copy(src_ref=mine, dst_ref=mine, send_sem=send_sems.at[0], recv_sem=recv_sems.at[0],
                                     device_id=sibling, device_id_type=MESH).wait_send()

    return _Rider([block], [jax.ShapeDtypeStruct((K, Ns), block.dtype)], 1, start, finish, aliases={0: 0})


def _chips_rider(pair_sum, rows=None, into=None):
    Q, kh, Ns = pair_sum.shape
    first_row, n_rows = rows or (0, kh)

    def copies(p_ref, got_ref, send_sems, recv_sems):
        x, y, c = _place()
        part = pl.ds(first_row, n_rows)
        out = []
        for j, (fx, fy) in enumerate(CHIP_FLIPS):
            px, py = _flip(x, fx), _flip(y, fy)
            out.append(pltpu.make_async_remote_copy(
                src_ref=p_ref.at[2 * px + py, part, :], dst_ref=got_ref.at[j, part, :], send_sem=send_sems.at[j],
                recv_sem=recv_sems.at[j], device_id=(px, py, c), device_id_type=MESH))
        return out

    def start(ins, outs, send_sems, recv_sems):
        for cp in copies(ins[0], outs[0], send_sems, recv_sems):
            cp.start()

    def finish(ins, outs, send_sems, recv_sems):
        sends = copies(ins[0], outs[0], send_sems, recv_sems)
        for cp in sends:
            cp.wait_recv()
        for cp in sends:
            cp.wait_send()

    got = jax.ShapeDtypeStruct((Q - 1, kh, Ns), pair_sum.dtype)
    if into is None:
        return _Rider([pair_sum], [got], 3, start, finish)
    return _Rider([pair_sum, into], [got], 3, start, finish, aliases={1: 0})


def _mm(a, b, *, grid, a_spec, b_spec, o_spec, o_shape, o_dtype, dims, acc_shape, name, rider=None, zero_rows=0,
        vmem_mb=48):
    nk = grid[2]

    def body(a_ref, b_ref, o_ref, *scratch):
        if zero_rows:
            @pl.when(pl.program_id(0) < zero_rows)
            def _():
                o_ref[...] = jnp.zeros_like(o_ref)

            @pl.when(pl.program_id(0) >= zero_rows)
            def _():
                o_ref[...] = _dot(a_ref[...], b_ref[...], dims).astype(o_ref.dtype)
            return
        part = _dot(a_ref[...], b_ref[...], dims)
        if nk == 1:
            o_ref[...] = part.astype(o_ref.dtype)
            return
        acc_ref, = scratch
        k = pl.program_id(2)

        @pl.when(k == 0)
        def _():
            acc_ref[...] = part

        @pl.when(k > 0)
        def _():
            acc_ref[...] += part

        @pl.when(k == nk - 1)
        def _():
            o_ref[...] = acc_ref[...].astype(o_ref.dtype)

    (out,), rode = _call(
        body, rider, name=name, grid=grid, in_specs=[a_spec, b_spec], out_specs=[o_spec],
        out_shape=[jax.ShapeDtypeStruct(o_shape, o_dtype)],
        scratch_shapes=[] if nk == 1 else [pltpu.VMEM(acc_shape, F32)],
        compiler_params=_params(("parallel", "parallel", "arbitrary") if rider is None else ("arbitrary",) * 3, vmem_mb),
        operands=(a, b))
    return out if rider is None else (out, rode)


def _mm_nn(a, w, *, tm, tn, tk, name, rider=None, cols=None, o_dtype=F32, pad_rows=0):
    T, K = a.shape
    Q, _, Ns = w.shape
    nbs = Ns // tn
    tm = min(tm, T)
    j0, j1 = cols or (0, Q * nbs)
    lead = pad_rows // tm
    return _mm(a, w, grid=(lead + T // tm, j1 - j0, K // tk),
               a_spec=pl.BlockSpec((tm, tk), lambda i, j, k: (jnp.maximum(i - lead, 0), k)),
               b_spec=pl.BlockSpec((None, tk, tn), lambda i, j, k: ((j + j0) // nbs, k, (j + j0) % nbs)),
               o_spec=pl.BlockSpec((tm, tn), lambda i, j, k: (i, j)),
               o_shape=(pad_rows + T, (j1 - j0) * tn), o_dtype=o_dtype, dims=NN, acc_shape=(tm, tn), name=name,
               rider=rider, zero_rows=lead)


def _col_blocks(g, rows, tn, at):
    if g.ndim == 2:
        return pl.BlockSpec((rows, tn), at)
    per = g.shape[2] // tn

    def stacked(*idx):
        r, c = at(*idx)
        return c // per, r, c % per

    return pl.BlockSpec((None, rows, tn), stacked)


def _mm_nt(g, w, *, tm, to, tn, name, rider=None):
    T = g.shape[-2]
    Q, K, Ns = w.shape
    nbs = Ns // tn
    tm = min(tm, T)
    return _mm(g, w, grid=(T // tm, K // to, Q * nbs),
               a_spec=_col_blocks(g, tm, tn, lambda i, j, n: (i, n)),
               b_spec=pl.BlockSpec((None, to, tn), lambda i, j, n: (n // nbs, j, n % nbs)),
               o_spec=pl.BlockSpec((tm, to), lambda i, j, n: (i, j)),
               o_shape=(T, K), o_dtype=F32, dims=NT, acc_shape=(tm, to), name=name, rider=rider)


def _mm_tn(a, g, *, q, tk, tn, tt, name, rider=None):
    T, K = a.shape
    Ns = g.shape[-1] * (g.ndim - 1) // q
    nbs = Ns // tn
    return _mm(a, g, grid=(K // tk, q * nbs, T // tt),
               a_spec=pl.BlockSpec((tt, tk), lambda i, j, t: (t, i)),
               b_spec=_col_blocks(g, tt, tn, lambda i, j, t: (t, j)),
               o_spec=pl.BlockSpec((None, tk, tn), lambda i, j, t: (j // nbs, i, j % nbs)),
               o_shape=(q, K, Ns), o_dtype=BF16, dims=TN, acc_shape=(tk, tn), name=name, rider=rider)


def _mm_tn_add(a, g, part, *, tk, tn, name, rider=None):
    T, K = a.shape
    Q, _, Ns = part.shape
    nbs = Ns // tn

    def body(a_ref, g_ref, p_ref, o_ref):
        o_ref[...] = (_dot(a_ref[...], g_ref[...], TN) + p_ref[...].astype(F32)).astype(o_ref.dtype)

    blk = pl.BlockSpec((None, tk, tn), lambda i, j: (j // nbs, i, j % nbs))
    (out,), rode = _call(
        body, rider, name=name, grid=(K // tk, Q * nbs),
        in_specs=[pl.BlockSpec((T, tk), lambda i, j: (0, i)), _col_blocks(g, T, tn, lambda i, j: (0, j)), blk],
        out_specs=[blk], out_shape=[jax.ShapeDtypeStruct((Q, K, Ns), BF16)], scratch_shapes=[],
        compiler_params=_params(("arbitrary", "arbitrary"), 48), operands=(a, g, part))
    return out, rode


def _ln(u):
    mu = jnp.mean(u, axis=-1, keepdims=True)
    d = u - mu
    r = lax.rsqrt(jnp.mean(d * d, axis=-1, keepdims=True) + EPS)
    return d * r, r


def _ln_bwd(dy, un, r):
    return r * (dy - jnp.mean(dy, axis=-1, keepdims=True) - un * jnp.mean(dy * un, axis=-1, keepdims=True))


def _colsum(v):
    return jnp.sum(v, axis=0, keepdims=True)


def _rowwise(name, fn, bigs, vecs, out_dtypes, n_acc, tm=128, rider=None):
    T, D = bigs[0].shape
    nb, nv, no = len(bigs), len(vecs), len(out_dtypes)

    def body(*refs):
        outs, accs = fn([r[...] for r in refs[:nb]], [r[...] for r in refs[nb:nb + nv]])
        for r, o in zip(refs[nb + nv:nb + nv + no], outs):
            r[...] = o.astype(r.dtype)
        if n_acc:
            acc_ref = refs[nb + nv + no]

            @pl.when(pl.program_id(0) == 0)
            def _():
                acc_ref[...] = jnp.zeros_like(acc_ref)

            for row, a in enumerate(accs):
                acc_ref[row:row + 1, :] += a

    big_spec = pl.BlockSpec((tm, D), lambda i: (i, 0))
    vec_spec = pl.BlockSpec((1, D), lambda i: (0, 0))
    out_shape = [jax.ShapeDtypeStruct((T, D), dt) for dt in out_dtypes]
    out_specs = [big_spec] * no
    if n_acc:
        out_shape.append(jax.ShapeDtypeStruct((8, D), F32))
        out_specs.append(pl.BlockSpec((8, D), lambda i: (0, 0)))
    outs, rode = _call(
        body, rider, name=name, grid=(T // tm,), in_specs=[big_spec] * nb + [vec_spec] * nv,
        out_specs=out_specs, out_shape=out_shape, scratch_shapes=[],
        compiler_params=_params(("arbitrary",), 48), operands=(*bigs, *vecs))
    return outs if rider is None else (outs, rode)


def _to_bf16(w, name, rider=None):
    R, C = w.shape
    tr = _row_tile(R, C)

    def body(w_ref, o_ref):
        o_ref[...] = w_ref[...].astype(o_ref.dtype)

    blk = pl.BlockSpec((tr, C), lambda i: (i, 0))
    return _call(body, rider, name=name, grid=(R // tr,), in_specs=[blk], out_specs=[blk],
                 out_shape=[jax.ShapeDtypeStruct((R, C), BF16)], scratch_shapes=[],
                 compiler_params=_params(("arbitrary",)), operands=(w,))


def _pre_mixer(x, scale1, shift1):
    def fn(b, v):
        xn, _ = _ln(b[0])
        return [xn * (1.0 + v[0]) + v[1]], []
    return _rowwise("pre_mixer", fn, [x], [scale1, shift1], [BF16], 0)[0]


def _post_mixer(mix, x, gate1, g1, b1, scale2, shift2, rider=None):
    def fn(b, v):
        un1, _ = _ln(ALPHA * b[1] + v[0] * b[0])
        x1 = un1 * v[1] + v[2]
        xn1, _ = _ln(x1)
        return [x1, xn1 * (1.0 + v[3]) + v[4]], []
    return _rowwise("post_mixer", fn, [mix, x], [gate1, g1, b1, scale2, shift2], [F32, BF16], 0, rider=rider)


def _loss_head(f, x1, tgt, gate2, g2, b2):
    def fn(b, v):
        ff, xx, tt = b
        d_model = ff.shape[-1]
        un2, r2 = _ln(ALPHA * xx + v[0] * ff)
        err = un2 * v[1] + v[2] - tt
        dy = err * (1.0 / d_model)
        du2 = _ln_bwd(dy * v[1], un2, r2)
        return [du2, du2 * v[0]], [_colsum(dy * un2), _colsum(dy), _colsum(du2 * ff), _colsum(err * err)]
    return _rowwise("loss_head", fn, [f, x1, tgt], [gate2, g2, b2], [F32, BF16], 4)


def _mid_bwd(dh2, du2, x1, mix, x, gate1, g1, scale2, rider=None):
    def fn(b, v):
        dh, du, xx1, mm, xx = b
        xn1, r1n = _ln(xx1)
        dx1 = ALPHA * du + _ln_bwd(dh * (1.0 + v[2]), xn1, r1n)
        un1, r1 = _ln(ALPHA * xx + v[0] * mm)
        du1 = _ln_bwd(dx1 * v[1], un1, r1)
        return [du1, du1 * v[0]], [_colsum(dh * xn1), _colsum(dh), _colsum(dx1 * un1), _colsum(dx1),
                                   _colsum(du1 * mm)]
    return _rowwise("mid_bwd", fn, [dh2, du2, x1, mix, x], [gate1, g1, scale2], [F32, BF16], 5, rider=rider)


def _first_bwd(dh1, du1, x, scale1, rider=None):
    def fn(b, v):
        dh, du, xx = b
        xn, r0 = _ln(xx)
        return [ALPHA * du + _ln_bwd(dh * (1.0 + v[0]), xn, r0)], [_colsum(dh * xn), _colsum(dh)]
    return _rowwise("first_bwd", fn, [dh1, du1, x], [scale1], [F32], 2, rider=rider)


def _ffn_in_swiglu(h2, w, *, tm, tn, rider=None):
    T, K = h2.shape
    Q, _, Ns = w.shape
    nbs = Ns // tn
    half = Q * nbs // 2
    tm = min(tm, T)

    def body(a_ref, wg_ref, wu_ref, g_ref, u_ref, act_ref):
        a = a_ref[...]
        g, u = _dot(a, wg_ref[...]), _dot(a, wu_ref[...])
        g_ref[...] = g.astype(g_ref.dtype)
        u_ref[...] = u.astype(u_ref.dtype)
        act_ref[...] = (g * jax.nn.sigmoid(g) * u).astype(act_ref.dtype)

    cols = lambda first: pl.BlockSpec((None, K, tn), lambda i, j: ((j + first) // nbs, 0, (j + first) % nbs))
    blk = pl.BlockSpec((tm, tn), lambda i, j: (i, j))
    return _call(
        body, rider, name="ffn_in", grid=(T // tm, half),
        in_specs=[pl.BlockSpec((tm, K), lambda i, j: (i, 0)), cols(0), cols(half)], out_specs=[blk] * 3,
        out_shape=[jax.ShapeDtypeStruct((T, half * tn), BF16)] * 3, scratch_shapes=[],
        compiler_params=_params(("arbitrary", "arbitrary"), 48), operands=(h2, w, w))


def _d_act_swiglu(df, w, gate, up, *, tm, to, rider=None):
    T, N = df.shape
    F = w.shape[1]
    tm = min(tm, T)

    def body(df_ref, w_ref, g_ref, u_ref, o_ref):
        d = _dot(df_ref[...], w_ref[...], NT)
        g = g_ref[...].astype(F32)
        s = jax.nn.sigmoid(g)
        o_ref[0] = (d * u_ref[...].astype(F32) * s * (1.0 + g * (1.0 - s))).astype(o_ref.dtype)
        o_ref[1] = (d * g * s).astype(o_ref.dtype)

    blk = pl.BlockSpec((tm, to), lambda i, j: (i, j))
    return _call(
        body, rider, name="d_act", grid=(T // tm, F // to),
        in_specs=[pl.BlockSpec((tm, N), lambda i, j: (i, 0)), pl.BlockSpec((None, to, N), lambda i, j: (0, j, 0)), blk, blk],
        out_specs=[pl.BlockSpec((2, tm, to), lambda i, j: (0, i, j))],
        out_shape=[jax.ShapeDtypeStruct((2, T, F), BF16)], scratch_shapes=[],
        compiler_params=_params(("arbitrary", "arbitrary"), 48), operands=(df, w, gate, up))


PAIR = 2


def _fill_table(table_ref, band_ref):
    table_ref[...] = jnp.full(table_ref.shape, NEG, F32)
    for e in range(PAIR):
        for g in range(QG):
            table_ref[e, g * CHUNK:(g + 1) * CHUNK, g * CHUNK:g * CHUNK + BAND] = band_ref[e]


def _attn_probs(q_ref, k_ref, bias_ref, e, step):
    start = pl.multiple_of(step * QROWS, QROWS)
    lanes = pl.ds(e * HD_A, HD_A)
    s = _dot(q_ref[:, lanes], k_ref[pl.ds(start + ZPAD - KPAD, UNION), lanes], NT) * (HD_A ** -0.5) + bias_ref[e]
    col = lax.broadcasted_iota(jnp.int32, s.shape, 1)
    s = jnp.where(col + start >= KPAD, s, NEG)
    p = jnp.exp(s - jnp.max(s, axis=-1, keepdims=True))
    return p / jnp.sum(p, axis=-1, keepdims=True), start


def _attn_specs(T, n_pairs):
    wide = PAIR * HD_A
    per_step = pl.BlockSpec((QROWS, wide), lambda hp, n: (n, hp))
    queries = pl.BlockSpec((QROWS, wide), lambda hp, n: (n + ZPAD // QROWS, hp))
    keys = pl.BlockSpec((ZPAD + T, wide), lambda hp, n: (0, n_pairs + hp))
    values = pl.BlockSpec((ZPAD + T, wide), lambda hp, n: (0, 2 * n_pairs + hp))
    grads = pl.BlockSpec((KPAD + T, wide), lambda hp, n: (0, hp))
    table = pl.BlockSpec((PAIR, CHUNK, BAND), lambda hp, n: (hp, 0, 0))
    vec = pl.BlockSpec((1, wide), lambda hp, n: (0, hp))
    return per_step, queries, keys, values, grads, table, vec


def _probs_spec():
    return pl.BlockSpec((PAIR, QROWS, UNION), lambda hp, n: (hp, n, 0))


def _attn_fwd(qkv, bias, gain, rider=None):
    T = qkv.shape[0] - ZPAD
    W = gain.shape[1]
    n_pairs = W // (PAIR * HD_A)

    def body(q_ref, k_ref, v_ref, band_ref, gain_ref, o_ref, p_ref, table_ref):
        @pl.when(pl.program_id(1) == 0)
        def _():
            _fill_table(table_ref, band_ref)

        for e in range(PAIR):
            lanes = pl.ds(e * HD_A, HD_A)
            p, start = _attn_probs(q_ref, k_ref, table_ref, e, pl.program_id(1))
            p_ref[e] = p.astype(p_ref.dtype)
            o = _dot(p_ref[e], v_ref[pl.ds(start + ZPAD - KPAD, UNION), lanes])
            rr = lax.rsqrt(jnp.mean(o * o, axis=-1, keepdims=True) + EPS)
            o_ref[:, lanes] = (o * rr * gain_ref[:, lanes]).astype(o_ref.dtype)

    per_step, queries, keys, values, _, table, vec = _attn_specs(T, n_pairs)
    return _call(
        body, rider, name="attn_fwd", grid=(n_pairs, T // QROWS), in_specs=[queries, keys, values, table, vec],
        out_specs=[per_step, _probs_spec()],
        out_shape=[jax.ShapeDtypeStruct((T, W), BF16), jax.ShapeDtypeStruct((n_pairs * PAIR, T, UNION), BF16)],
        scratch_shapes=[pltpu.VMEM((PAIR, QROWS, UNION), F32)],
        compiler_params=_params(("arbitrary", "arbitrary"), 40), operands=(qkv, qkv, qkv, bias, gain))


def _attn_bwd(qkv, probs, gain, dmixin, rider=None):
    T = qkv.shape[0] - ZPAD
    W = gain.shape[1]
    n_pairs = W // (PAIR * HD_A)
    scale = HD_A ** -0.5

    def body(q_ref, k_ref, v_ref, p_ref, gain_ref, don_ref, dq_ref, dk_ref, dv_ref, dband_ref, dgain_ref, dtable_ref):
        n = pl.program_id(1)

        @pl.when(n == 0)
        def _():
            dk_ref[...] = jnp.zeros_like(dk_ref)
            dv_ref[...] = jnp.zeros_like(dv_ref)
            dtable_ref[...] = jnp.zeros_like(dtable_ref)
            dgain_ref[...] = jnp.zeros_like(dgain_ref)

        for e in range(PAIR):
            lanes = pl.ds(e * HD_A, HD_A)
            start = pl.multiple_of(n * QROWS, QROWS)
            keys, in_qkv = pl.ds(start, UNION), pl.ds(start + ZPAD - KPAD, UNION)
            pb = p_ref[e]
            p = pb.astype(F32)
            vb = v_ref[in_qkv, lanes]
            o = _dot(pb, vb)
            rr = lax.rsqrt(jnp.mean(o * o, axis=-1, keepdims=True) + EPS)
            on = o * rr
            d_on = don_ref[:, lanes]
            dgain_ref[:, lanes] += _colsum(d_on * on)
            dyo = d_on * gain_ref[:, lanes]
            do = rr * (dyo - on * jnp.mean(dyo * on, axis=-1, keepdims=True))
            dob = do.astype(BF16)
            dp = _dot(dob, vb, NT)
            ds = p * (dp - jnp.sum(do * o, axis=-1, keepdims=True))
            dtable_ref[e] += ds
            dsb = ds.astype(BF16)
            dq_ref[:, lanes] = (_dot(dsb, k_ref[in_qkv, lanes]) * scale).astype(dq_ref.dtype)
            dk_ref[keys, lanes] += _dot(dsb, q_ref[:, lanes], TN) * scale
            dv_ref[keys, lanes] += _dot(pb, dob, TN)

        @pl.when(n == T // QROWS - 1)
        def _():
            for e in range(PAIR):
                dband_ref[e] = sum(dtable_ref[e, g * CHUNK:(g + 1) * CHUNK, g * CHUNK:g * CHUNK + BAND]
                                   for g in range(QG))

    per_step, queries, keys, values, grads, table, vec = _attn_specs(T, n_pairs)
    H = n_pairs * PAIR
    return _call(
        body, rider, name="attn_bwd", grid=(n_pairs, T // QROWS),
        in_specs=[queries, keys, values, _probs_spec(), vec, per_step],
        out_specs=[per_step, grads, grads, table, vec],
        out_shape=[jax.ShapeDtypeStruct((T, W), BF16), jax.ShapeDtypeStruct((KPAD + T, W), F32),
                   jax.ShapeDtypeStruct((KPAD + T, W), F32), jax.ShapeDtypeStruct((H, CHUNK, BAND), F32),
                   jax.ShapeDtypeStruct((1, W), F32)],
        scratch_shapes=[pltpu.VMEM((PAIR, QROWS, UNION), F32)],
        compiler_params=_params(("arbitrary", "arbitrary"), 40),
        operands=(qkv, qkv, qkv, probs, gain, dmixin))


N_DIAG = CHUNK + BAND - 1


def _bias_band(rel_bias):
    H = rel_bias.shape[0]
    idx = np.clip(BAND - 1 - np.arange(N_DIAG), -MAX_REL, MAX_REL) + MAX_REL
    rolled = rel_bias[:, idx[(np.arange(N_DIAG) + CHUNK - 1) % N_DIAG]]
    flat = jnp.broadcast_to(rolled[:, None, :], (H, CHUNK, N_DIAG)).reshape(H, CHUNK * N_DIAG)
    return flat[:, :CHUNK * (N_DIAG - 1)].reshape(H, CHUNK, N_DIAG - 1)[:, :, :BAND]


def _bias_band_grad(dband):
    H = dband.shape[0]
    skew = jnp.pad(dband, ((0, 0), (0, 0), (CHUNK - 1, 0))).reshape(H, CHUNK * N_DIAG)
    skew = jnp.pad(skew, ((0, 0), (0, CHUNK))).reshape(H, CHUNK, N_DIAG + 1)
    diag = jnp.sum(skew, axis=1)[:, :N_DIAG]
    n_far = BAND - MAX_REL
    far = jnp.sum(diag[:, :n_far], axis=1, keepdims=True)
    near = diag[:, n_far:][:, ::-1]
    zeros = jnp.zeros((H, MAX_REL - (CHUNK - 1)), F32)
    return jnp.concatenate([zeros, near, far], axis=1)


def _tri(n, lower):
    r = lax.broadcasted_iota(jnp.int32, (n, n), 0)
    c = lax.broadcasted_iota(jnp.int32, (n, n), 1)
    return jnp.where((c <= r) if lower else (c >= r), 1.0, 0.0).astype(F32)


def _hgrn_gates(zq_ref, zf_ref, lbl_ref, q_s, k_s, b_s):
    lb = jax.nn.sigmoid(lbl_ref[0:1, :] - lbl_ref[1:2, :])
    zq = zq_ref[...]
    sig = jax.nn.sigmoid(zf_ref[...])
    f = lb + (1.0 - lb) * sig
    sq = jax.nn.sigmoid(zq)
    q_s[...] = zq * sq
    k_s[...] = 1.0 - f
    b_s[...] = _dot(_tri(CHUNK, True), jnp.log(f), precision=HIGHEST)
    return lb, sig, f, sq


def _sub_rows(i):
    return pl.ds(i * SUB, SUB)


def _row_mask(s):
    return lax.broadcasted_iota(jnp.int32, (SUB, HD_B), 0) >= s


def _decay_from(b_sub, b_row, s):
    return jnp.where(_row_mask(s), jnp.exp(jnp.minimum(b_sub - b_row, 0.0)), 0.0)


def _hgrn_fwd(proj, lb_logits, gnorm_g, rider=None):
    T = proj.shape[0]
    nC = T // CHUNK
    W = lb_logits.shape[1]
    G = W // HD_B // HGRN_HEADS
    col0 = (proj.shape[1] - 4 * W) // (HD_B * HGRN_HEADS)
    wide = HGRN_HEADS * HD_B

    def body(*refs):
        @pl.when(pl.program_id(1) == 0)
        def _():
            refs[9][...] = jnp.zeros_like(refs[9])

        for h in range(HGRN_HEADS):
            lanes = pl.ds(h * HD_B, HD_B)
            one_head(*[r.at[:, lanes] for r in refs[:5]], refs[5], *[r.at[:, lanes] for r in refs[6:8]],
                     *[r.at[h] for r in refs[8:]])

    def one_head(zq_ref, zf_ref, xi_ref, zg_ref, lbl_ref, gn_ref, mix_ref, o_ref, stall_ref, st_ref, q_s, k_s, b_s, acc_s):
        _hgrn_gates(zq_ref, zf_ref, lbl_ref, q_s, k_s, b_s)
        q, k, b = q_s[...], k_s[...], b_s[...]
        st = st_ref[...]
        stall_ref[...] = st
        b_last = b_s[CHUNK - 1:CHUNK, :]
        acc_s[...] = _dot((q * jnp.exp(b)).astype(BF16), st.astype(BF16), NT)
        for i in range(CHUNK // SUB):
            rows = _sub_rows(i)
            q_i, b_i = q_s[rows, :], b_s[rows, :]
            acc = jnp.zeros((SUB, HD_B), F32)
            if i:
                past = pl.ds(0, i * SUB)
                b_ref = b_s[i * SUB - 1:i * SUB, :]
                qs = (q_i * jnp.exp(b_i - b_ref)).astype(BF16)
                ks = (k_s[past, :] * jnp.exp(b_ref - b_s[past, :])).astype(BF16)
                acc += _dot(_dot(qs, ks, NT).astype(BF16), xi_ref[past, :].astype(BF16))
            for s in range(SUB):
                row = pl.ds(i * SUB + s, 1)
                w = q_i * _decay_from(b_i, b_s[row, :], s)
                acc += jnp.sum(w * k_s[row, :], axis=-1, keepdims=True) * xi_ref[row, :]
            acc_s[rows, :] += acc
        o = acc_s[...]
        kd = (k * jnp.exp(b_last - b)).astype(BF16)
        st_ref[...] = st * jnp.exp(b_last) + _dot(xi_ref[...].astype(BF16), kd, TN)
        o_ref[...] = o
        zg = zg_ref[...]
        rr = lax.rsqrt(jnp.mean(o * o, axis=-1, keepdims=True) + EPS)
        mix_ref[...] = (o * rr * gn_ref[...] * (zg * jax.nn.sigmoid(zg))).astype(mix_ref.dtype)

    col = lambda part: pl.BlockSpec((CHUNK, wide), lambda g, n: (n, col0 + part * G + g))
    out_blk = pl.BlockSpec((CHUNK, wide), lambda g, n: (n, g))
    tile = pltpu.VMEM((HGRN_HEADS, CHUNK, HD_B), F32)
    return _call(
        body, rider, name="hgrn_fwd", grid=(G, nC),
        in_specs=[col(0), col(1), col(2), col(3), pl.BlockSpec((2, wide), lambda g, n: (0, g)),
                  pl.BlockSpec((1, HD_B), lambda g, n: (0, 0))],
        out_specs=[out_blk, out_blk, pl.BlockSpec((HGRN_HEADS, None, HD_B, HD_B), lambda g, n: (g, n, 0, 0))],
        out_shape=[jax.ShapeDtypeStruct((T, W), BF16), jax.ShapeDtypeStruct((T, W), F32),
                   jax.ShapeDtypeStruct((G * HGRN_HEADS, nC, HD_B, HD_B), F32)],
        scratch_shapes=[pltpu.VMEM((HGRN_HEADS, HD_B, HD_B), F32), tile, tile, tile, tile],
        compiler_params=_params(("arbitrary", "arbitrary")),
        operands=(proj, proj, proj, proj, lb_logits, gnorm_g))


def _hgrn_bwd(proj, lb_logits, gnorm_g, o_b, st_all, dmixin, rider=None):
    T = proj.shape[0]
    nC = T // CHUNK
    W = lb_logits.shape[1]
    G = W // HD_B // HGRN_HEADS
    wide = HGRN_HEADS * HD_B
    col0 = (proj.shape[1] - 4 * W) // wide
    dcol0 = (dmixin.shape[1] - W) // wide

    def body(*refs):
        g, n = pl.program_id(0), pl.program_id(1)
        dl0_ref, dgn_ref, dst_ref = refs[13:16]

        @pl.when(n == 0)
        def _():
            dst_ref[...] = jnp.zeros_like(dst_ref)
            dl0_ref[...] = jnp.zeros_like(dl0_ref)

        @pl.when((n == 0) & (g == 0))
        def _():
            dgn_ref[...] = jnp.zeros_like(dgn_ref)

        for h in range(HGRN_HEADS):
            lanes = pl.ds(h * HD_B, HD_B)
            cut = lambda r: r.at[:, lanes]
            one_head(*[cut(r) for r in refs[:5]], refs[5], cut(refs[6]), refs[7].at[h], cut(refs[8]),
                     *[cut(r) for r in refs[9:14]], dgn_ref, *[r.at[h] for r in refs[15:]])

    def one_head(zq_ref, zf_ref, xi_ref, zg_ref, lbl_ref, gn_ref, o_ref, st_ref, dout_ref,
                 dzq_ref, dzf_ref, dxi_ref, dzg_ref, dl0_ref, dgn_ref, dst_ref, q_s, k_s, b_s, do_s, dq_s, dk_s, di_s):
        lb, sig, f, sq = _hgrn_gates(zq_ref, zf_ref, lbl_ref, q_s, k_s, b_s)
        q, k, b = q_s[...], k_s[...], b_s[...]
        zg, o, dout = zg_ref[...], o_ref[...], dout_ref[...]
        sg = jax.nn.sigmoid(zg)
        rr = lax.rsqrt(jnp.mean(o * o, axis=-1, keepdims=True) + EPS)
        on = o * rr
        gn = gn_ref[...]
        dzg_ref[...] = (dout * on * gn * sg * (1.0 + zg * (1.0 - sg))).astype(dzg_ref.dtype)
        d_on = dout * zg * sg
        dgn_ref[...] += _colsum(d_on * on)
        d_on = d_on * gn
        do = rr * (d_on - on * jnp.mean(d_on * on, axis=-1, keepdims=True))
        do_s[...] = do
        dob = do.astype(BF16)
        st, dst = st_ref[...], dst_ref[...]
        b_last = b_s[CHUNK - 1:CHUNK, :]
        eb, e_last, k_dec = jnp.exp(b), jnp.exp(b_last), jnp.exp(b_last - b)
        qt, kd = q * eb, k * k_dec
        dstb = dst.astype(BF16)
        xib = xi_ref[...].astype(BF16)
        d_kd = _dot(xib, dstb)
        dq_s[...] = _dot(dob, st.astype(BF16)) * eb
        dk_s[...] = d_kd * k_dec
        di_s[...] = _dot(kd.astype(BF16), dstb, NT)
        d_b_last = e_last * _colsum(st * dst) + _colsum(d_kd * kd)
        dst_ref[...] = _dot(dob, qt.astype(BF16), TN) + dst * e_last
        for i in range(CHUNK // SUB):
            rows = _sub_rows(i)
            q_i, b_i, do_i = q_s[rows, :], b_s[rows, :], do_s[rows, :]
            dq_i = jnp.zeros((SUB, HD_B), F32)
            if i:
                past = pl.ds(0, i * SUB)
                b_ref = b_s[i * SUB - 1:i * SUB, :]
                e_q, e_k = jnp.exp(b_i - b_ref), jnp.exp(b_ref - b_s[past, :])
                qs, ks = (q_i * e_q).astype(BF16), (k_s[past, :] * e_k).astype(BF16)
                xi_p, do_b = xi_ref[past, :].astype(BF16), do_i.astype(BF16)
                di_s[past, :] += _dot(_dot(ks, qs, NT).astype(BF16), do_b)
                dq_i += _dot(_dot(do_b, xi_p, NT).astype(BF16), ks) * e_q
                dk_s[past, :] += _dot(_dot(xi_p, do_b, NT).astype(BF16), qs) * e_k
            for s in range(SUB):
                row = pl.ds(i * SUB + s, 1)
                k_row, i_row = k_s[row, :], xi_ref[row, :]
                e = _decay_from(b_i, b_s[row, :], s)
                w = q_i * e
                a_col = jnp.sum(w * k_row, axis=-1, keepdims=True)
                da_col = jnp.sum(do_i * i_row, axis=-1, keepdims=True)
                di_s[row, :] += _colsum(a_col * do_i)
                dq_i += da_col * e * k_row
                dk_s[row, :] += _colsum(da_col * w)
            dq_s[rows, :] += dq_i
        dq, dk = dq_s[...], dk_s[...]
        db = q * dq - k * dk
        is_last = lax.broadcasted_iota(jnp.int32, (CHUNK, HD_B), 0) == CHUNK - 1
        db = db + jnp.where(is_last, d_b_last, 0.0)
        df = _dot(_tri(CHUNK, False), db, precision=HIGHEST) / f - dk
        dzf_ref[...] = (df * (1.0 - lb) * sig * (1.0 - sig)).astype(dzf_ref.dtype)
        dl0_ref[...] += _colsum(df * (1.0 - sig)) * (lb * (1.0 - lb))
        zq = zq_ref[...]
        dzq_ref[...] = (dq * sq * (1.0 + zq * (1.0 - sq))).astype(dzq_ref.dtype)
        dxi_ref[...] = di_s[...].astype(dxi_ref.dtype)

    rev = lambda n: nC - 1 - n
    col = lambda part: pl.BlockSpec((CHUNK, wide), lambda g, n: (rev(n), col0 + part * G + g))
    blk = pl.BlockSpec((CHUNK, wide), lambda g, n: (rev(n), g))
    tile = pltpu.VMEM((HGRN_HEADS, CHUNK, HD_B), F32)
    out_big = jax.ShapeDtypeStruct((T, W), BF16)
    return _call(
        body, rider, name="hgrn_bwd", grid=(G, nC),
        in_specs=[col(0), col(1), col(2), col(3), pl.BlockSpec((2, wide), lambda g, n: (0, g)),
                  pl.BlockSpec((1, HD_B), lambda g, n: (0, 0)), blk,
                  pl.BlockSpec((HGRN_HEADS, None, HD_B, HD_B), lambda g, n: (g, rev(n), 0, 0)),
                  pl.BlockSpec((CHUNK, wide), lambda g, n: (rev(n), dcol0 + g))],
        out_specs=[blk, blk, blk, blk, pl.BlockSpec((1, wide), lambda g, n: (0, g)),
                   pl.BlockSpec((1, HD_B), lambda g, n: (0, 0))],
        out_shape=[out_big, out_big, out_big, out_big, jax.ShapeDtypeStruct((1, W), F32),
                   jax.ShapeDtypeStruct((1, HD_B), F32)],
        scratch_shapes=[pltpu.VMEM((HGRN_HEADS, HD_B, HD_B), F32)] + [tile] * 7,
        compiler_params=_params(("arbitrary", "arbitrary")),
        operands=(proj, proj, proj, proj, lb_logits, gnorm_g, o_b, st_all, dmixin))


def _adamw_math(g, w, m, v):
    m = B1 * m + (1.0 - B1) * g
    v = B2 * v + (1.0 - B2) * (g * g)
    m_hat = m / (1.0 - B1 ** STEP)
    v_hat = v / (1.0 - B2 ** STEP)
    return -LR * (m_hat / (jnp.sqrt(v_hat) + ADAM_EPS) + WD * w), m, v


def _adamw(g, w, m, v, name):
    R, C = g.shape
    tr = _row_tile(R, C)

    def body(g_ref, w_ref, m_ref, v_ref, go_ref, d_ref, mo_ref, vo_ref):
        g = g_ref[...]
        go_ref[...] = g
        d_ref[...], mo_ref[...], vo_ref[...] = _adamw_math(g, w_ref[...], m_ref[...], v_ref[...])

    blk = pl.BlockSpec((tr, C), lambda i: (i, 0))
    return pl.pallas_call(
        body, name=name, grid=(R // tr,), in_specs=[blk] * 4, out_specs=[blk] * 4,
        out_shape=[jax.ShapeDtypeStruct((R, C), F32)] * 4, compiler_params=_params(("parallel",), 40),
    )(g, w, m, v)


def _sum_pair(g_full, from_sibling, sel, name):
    Q, K, Ns = g_full.shape
    kh = K // 2
    tr = _row_tile(kh, Ns)
    nh = kh // tr

    def body(sel_ref, a_ref, b_ref, o_ref):
        o_ref[...] = (a_ref[...].astype(F32) + b_ref[...].astype(F32)).astype(o_ref.dtype)

    return pl.pallas_call(
        body, name=name,
        grid_spec=pltpu.PrefetchScalarGridSpec(
            num_scalar_prefetch=1, grid=(Q, nh),
            in_specs=[pl.BlockSpec((None, tr, Ns), lambda q, i, sel: (q, sel[1] * nh + i, 0)),
                      pl.BlockSpec((None, tr, Ns), lambda q, i, sel: (q, i, 0))],
            out_specs=pl.BlockSpec((None, tr, Ns), lambda q, i, sel: (q, i, 0))),
        out_shape=jax.ShapeDtypeStruct((Q, kh, Ns), BF16), compiler_params=_params(("parallel", "parallel")),
    )(sel, g_full, from_sibling)


def _sum_chips(pair_sum, from_chips, sel, name):
    Q, kh, Ns = pair_sum.shape
    tr = _row_tile(kh, Ns)
    nh = kh // tr

    def body(sel_ref, a_ref, b0_ref, b1_ref, b2_ref, o_ref):
        up = lambda r: r[...].astype(F32)
        o_ref[...] = ((up(a_ref) + up(b0_ref)) + up(b1_ref)) + up(b2_ref)

    recv = lambda k: pl.BlockSpec((None, tr, Ns), lambda i, sel: (k, i, 0))
    return pl.pallas_call(
        body, name=name,
        grid_spec=pltpu.PrefetchScalarGridSpec(
            num_scalar_prefetch=1, grid=(nh,),
            in_specs=[pl.BlockSpec((None, tr, Ns), lambda i, sel: (sel[0], i, 0)), recv(0), recv(1), recv(2)],
            out_specs=pl.BlockSpec((tr, Ns), lambda i, sel: (sel[1] * nh + i, 0))),
        out_shape=jax.ShapeDtypeStruct((2 * kh, Ns), F32), compiler_params=_params(("parallel",)),
    )(sel, pair_sum, from_chips, from_chips, from_chips)


def _gather_small(v, name):
    R, L = v.shape

    def body(v_ref, out_ref, send_sems, recv_sems):
        x, y, c = _place()
        me = 4 * x + 2 * y + c
        out_ref[me] = v_ref[...]
        peers = [(_flip(x, k >> 2 & 1), _flip(y, k >> 1 & 1), _flip(c, k & 1)) for k in range(1, N_DEV)]

        def copy(k, row, to):
            return pltpu.make_async_remote_copy(src_ref=v_ref, dst_ref=out_ref.at[row], send_sem=send_sems.at[k],
                                                recv_sem=recv_sems.at[k], device_id=to, device_id_type=MESH)

        sends = [copy(k, me, peer) for k, peer in enumerate(peers)]
        for cp in sends:
            cp.start()
        for k, (px, py, pc) in enumerate(peers):
            copy(k, 4 * px + 2 * py + pc, (x, y, c)).wait_recv()
        for cp in sends:
            cp.wait_send()

    vmem = pl.BlockSpec(memory_space=pltpu.VMEM)
    return pl.pallas_call(
        body, name=name, in_specs=[vmem], out_specs=vmem, out_shape=jax.ShapeDtypeStruct((N_DEV, R, L), F32),
        scratch_shapes=[pltpu.SemaphoreType.DMA((N_DEV - 1,)), pltpu.SemaphoreType.DMA((N_DEV - 1,))],
    )(v)


def _silu(v):
    return v * jax.nn.sigmoid(v)


def _ada_fwd(c_all, w_ada, tn=512):
    M, D = c_all.shape
    Ns = w_ada.shape[1]

    def body(c_ref, w_ref, o_ref):
        o_ref[...] = _dot(_silu(c_ref[...]).astype(BF16), w_ref[...].astype(BF16))

    return pl.pallas_call(
        body, name="ada_fwd", grid=(Ns // tn,),
        in_specs=[pl.BlockSpec((M, D), lambda j: (0, 0)), pl.BlockSpec((D, tn), lambda j: (0, j))],
        out_specs=pl.BlockSpec((M, tn), lambda j: (0, j)), out_shape=jax.ShapeDtypeStruct((M, Ns), F32),
        compiler_params=_params(("parallel",)),
    )(c_all, w_ada)


def _ada_bwd(c_all, dmod, w, m, v, tk=256, tn=1536):
    M, D = c_all.shape
    Ns = dmod.shape[1]

    def body(c_ref, d_ref, w_ref, m_ref, v_ref, g_ref, dl_ref, mo_ref, vo_ref):
        g = _dot(_silu(c_ref[...]).astype(BF16), d_ref[...].astype(BF16), TN)
        g_ref[...] = g
        dl_ref[...], mo_ref[...], vo_ref[...] = _adamw_math(g, w_ref[...], m_ref[...], v_ref[...])

    blk = pl.BlockSpec((tk, tn), lambda i, j: (i, j))
    return pl.pallas_call(
        body, name="ada_bwd", grid=(D // tk, Ns // tn),
        in_specs=[pl.BlockSpec((M, tk), lambda i, j: (0, i)), pl.BlockSpec((M, tn), lambda i, j: (0, j)), blk, blk, blk],
        out_specs=[blk] * 4, out_shape=[jax.ShapeDtypeStruct((D, Ns), F32)] * 4,
        compiler_params=_params(("parallel", "parallel"), 40),
    )(c_all, dmod, w, m, v)


def _small_update(g_all, w, m, v):
    R, L = w.shape

    def body(g_ref, w_ref, m_ref, v_ref, go_ref, d_ref, mo_ref, vo_ref):
        g = g_ref[0]
        for d in range(1, N_DEV):
            g = g + g_ref[d]
        go_ref[...] = g
        d_ref[...], mo_ref[...], vo_ref[...] = _adamw_math(g, w_ref[...], m_ref[...], v_ref[...])

    return pl.pallas_call(body, name="small_update", out_shape=[jax.ShapeDtypeStruct((R, L), F32)] * 4)(g_all, w, m, v)


def _pack(parts, rows):
    flat = jnp.concatenate([p.reshape(-1) for p in parts])
    return jnp.pad(flat, (0, rows * 128 - flat.shape[0])).reshape(rows, 128)


def _unpack(packed, shapes):
    flat, out, at = packed.reshape(-1), [], 0
    for shp in shapes:
        size = 1
        for d in shp:
            size *= d
        out.append(flat[at:at + size].reshape(shp))
        at += size
    return out


def _layer(x, tgt, mod, wts, rel_bias, attn_norm_g, lb_logits, gnorm_g, ln1_g, ln1_b, ln2_g, ln2_b, place=None):
    T, D = x.shape
    aw = attn_norm_g.shape[1]
    shift1, scale1, gate1, shift2, scale2, gate2 = [mod[i:i + 1] for i in range(6)]

    def gather(n, rows=None, into=None, before=None, last=True):
        return None if place is None else _gather_rider(wts[n], rows, None if into is None else into[0], before, last)

    def gathered(n, rode):
        return wts[n] if place is None else lax.dynamic_update_index_in_dim(rode[0], wts[n], place[0], 0)

    def blocks(g):
        return g.reshape(N_CHIPS, -1, g.shape[2])

    def to_sibling(g):
        return None if place is None else _pair_rider(g)

    def pair_sum(n, g, rode=None):
        if place is None:
            return g
        rode = _alone(_pair_rider(g), n + "_send_pair") if rode is None else rode
        return _sum_pair(g, rode[0], place[1], n + "_sum_pair")

    def to_chips(p, rows=None, into=None):
        return None if place is None else _chips_rider(p, rows, None if into is None else into[0])

    def summed(n, p, rode):
        return p if place is None else _sum_chips(p, rode[0], place[1], n + "_sum_chips")

    def to_both(block):
        return None if place is None else _share_rider(block)

    def carrying(mm, *args, rider, **kw):
        return mm(*args, rider=rider, **kw) if rider is not None else (mm(*args, **kw), None)

    def to_sibling_acts(a, b):
        return None if place is None else _acts_rider(a, b)

    def pair_grad(name, a, b, tn, rider, arrived=None, late_rider=None):
        if place is None:
            return _mm_tn(a, b, q=N_CHIPS, tk=512, tn=tn, tt=T, name=name), None
        kh = a.shape[1] // 2
        mine = lax.dynamic_slice_in_dim(a, place[1][1] * kh, kh, axis=1)
        part, rode = carrying(_mm_tn, mine, b, q=N_CHIPS, tk=512, tn=tn, tt=T, name=name + "_own",
                              rider=_join(None if arrived else _acts_rider(a, b), rider))
        (a_sib, b_sib), rode = arrived or rode[:2], rode if arrived else rode[2:]
        out, late = _mm_tn_add(a_sib, b_sib, part, tk=512, tn=tn, name=name + "_sib", rider=late_rider)
        return out, (rode or []) + late

    w_in = gathered("w_in", None if place is None else [wts["w_in_gathered"]])
    h1 = _pre_mixer(x, scale1, shift1)
    n_qkv = 3 * aw // 256
    kh_o, kh_f, kh_out = [wts[n].shape[-2] // 2 for n in ("w_o", "w_ffn_in", "w_ffn_out")]
    o_cut, f_cuts, out_cut = 3 * kh_o // 8, (7 * kh_f // 16, 7 * kh_f // 8), kh_out // 11
    qkv, rode = carrying(_mm_nn, h1, w_in, tm=ZPAD, tn=256, tk=D, name="proj_qkv", cols=(0, n_qkv), o_dtype=BF16,
                         pad_rows=ZPAD, rider=gather("w_o", (0, o_cut), last=False))
    proj, rode = carrying(_mm_nn, h1, w_in, tm=2048, tn=256, tk=D, name="proj_rec",
                          cols=(n_qkv, N_CHIPS * w_in.shape[2] // 256),
                          rider=gather("w_o", (o_cut, kh_o - o_cut), rode, before=(0, o_cut)))
    w_o3 = gathered("w_o", rode).reshape(1, D, D)
    bias = _bias_band(rel_bias)
    (mix_a, probs), rode = _attn_fwd(qkv, bias, attn_norm_g, rider=gather("w_ffn_in", (0, f_cuts[0]), last=False))
    (mix_b, o_b, st_all), rode = _hgrn_fwd(
        proj, lb_logits, gnorm_g,
        rider=gather("w_ffn_in", (f_cuts[0], f_cuts[1] - f_cuts[0]), rode, before=(0, f_cuts[0]), last=False))
    mixin = jnp.concatenate([mix_a, mix_b], axis=1)
    mix = _mm_nn(mixin, w_o3, tm=1024, tn=512, tk=D, name="mix_out")
    if place is None:
        x1, h2 = _post_mixer(mix, x, gate1, ln1_g, ln1_b, scale2, shift2)
    else:
        (x1, h2), rode = _post_mixer(mix, x, gate1, ln1_g, ln1_b, scale2, shift2, rider=_join(
            gather("w_ffn_in", (f_cuts[1], kh_f - f_cuts[1]), rode, before=(f_cuts[0], f_cuts[1] - f_cuts[0])),
            gather("w_ffn_out", (0, out_cut), last=False)))
    w_ffn_in = gathered("w_ffn_in", rode)
    (gate, up, act), rode = _ffn_in_swiglu(
        h2, w_ffn_in, tm=2048, tn=256,
        rider=gather("w_ffn_out", (out_cut, kh_out - out_cut), rode and rode[1:], before=(0, out_cut)))
    w_out3 = gathered("w_ffn_out", rode)
    w_out3 = w_out3.reshape(1, -1, w_out3.shape[2])
    d_ff = w_out3.shape[1]
    f = _mm_nn(act, w_out3, tm=1024, tn=512, tk=d_ff, name="ffn_out")
    du2, df, acc2 = _loss_head(f, x1, tgt, gate2, ln2_g, ln2_b)
    loss = (0.5 / D) * jnp.sum(acc2[3])
    g = blocks(_mm_tn(act, df, q=1, tk=512, tn=1024, tt=T, name="g_ffn_out"))
    (dff,), rode = _d_act_swiglu(df, w_out3, gate, up, tm=1024, to=512, rider=to_sibling(g))
    p_out = pair_sum("w_ffn_out", g, rode)
    cut = 21 * p_out.shape[1] // 44
    dh2, rode = carrying(_mm_nt, dff, w_ffn_in, tm=1024, to=1024, tn=w_ffn_in.shape[2], name="d_h2",
                         rider=_join(to_chips(p_out, (0, cut)), to_sibling_acts(h2, dff)))
    p_fin, rode = pair_grad("g_ffn_in", h2, dff, w_ffn_in.shape[2] // 2,
                            to_chips(p_out, (cut, p_out.shape[1] - cut), rode), arrived=rode and rode[1:])
    g_ffn_out = summed("w_ffn_out", p_out, rode)
    if place is None:
        du1, dmix, acc1 = _mid_bwd(dh2, du2, x1, mix, x, gate1, ln1_g, scale2)
    else:
        (du1, dmix, acc1), (g_ffn_out,) = _mid_bwd(dh2, du2, x1, mix, x, gate1, ln1_g, scale2, rider=to_both(g_ffn_out))
    g = blocks(_mm_tn(mixin, dmix, q=1, tk=512, tn=1024, tt=T, name="g_o"))
    dmixin, rode = carrying(_mm_nt, dmix, w_o3, tm=1024, to=512, tn=D, name="d_mixin", rider=to_sibling(g))
    p_o = pair_sum("w_o", g, rode)
    cut = p_fin.shape[1] // 2
    (dq, dk, dv, dbias, dgain), rode = _attn_bwd(qkv, probs, attn_norm_g, dmixin, rider=to_chips(p_fin, (0, cut)))
    (dzq, dzf, dxi, dzg, dl0, dgn), rode = _hgrn_bwd(
        proj, lb_logits, gnorm_g, o_b, st_all, dmixin,
        rider=_join(to_chips(p_fin, (cut, p_fin.shape[1] - cut), rode), to_chips(p_o)))
    g_ffn_in, g_o = summed("w_ffn_in", p_fin, rode[:1]), summed("w_o", p_o, rode[1:])
    dproj = jnp.concatenate([dq, dk[KPAD:].astype(BF16), dv[KPAD:].astype(BF16), dzq, dzf, dxi, dzg], axis=1)
    p_in, rode = pair_grad("g_in", h1, dproj, w_in.shape[2] // 2, None,
                           late_rider=_join(to_both(g_ffn_in), to_both(g_o)))
    if place is not None:
        g_ffn_in, g_o = rode
    cut = 3 * p_in.shape[1] // 4
    dh1, rode = carrying(_mm_nt, dproj, w_in, tm=1024, to=1024, tn=w_in.shape[2], name="d_h1",
                         rider=to_chips(p_in, (0, cut)))
    if place is None:
        (grad_x, acc0), g_in = _first_bwd(dh1, du1, x, scale1), p_in
    else:
        (grad_x, acc0), rode = _first_bwd(dh1, du1, x, scale1, rider=to_chips(p_in, (cut, p_in.shape[1] - cut), rode))
        g_in, = _alone(to_both(summed("w_in", p_in, rode)), "w_in_share")
    dmod = jnp.concatenate([acc0[1:2], acc0[0:1], acc1[4:5], acc1[1:2], acc1[0:1], acc2[2:3]], axis=0)
    small = dict(rel_bias=_bias_band_grad(dbias), attn_norm_g=dgain,
                 lb_logits=jnp.concatenate([dl0, -dl0], axis=0), gnorm_g=dgn,
                 ln1_g=acc1[2:3], ln1_b=acc1[3:4], ln2_g=acc2[0:1], ln2_b=acc2[1:2])
    return loss, grad_x, dict(w_in=g_in, w_o=g_o, w_ffn_in=g_ffn_in, w_ffn_out=g_ffn_out), dmod, small


SMALL = ("rel_bias", "attn_norm_g", "lb_logits", "gnorm_g", "ln1_g", "ln1_b", "ln2_g", "ln2_b")
SMALL_ROWS = 256


def kernel(x, c, w_ada, b_ada, w_in, rel_bias, attn_norm_g, lb_logits, gnorm_g, w_o, ln1_g, ln1_b, w_ffn_in, w_ffn_out, ln2_g, ln2_b, loss_target, m_w_ada, m_b_ada, m_w_in, m_rel_bias, m_attn_norm_g, m_lb_logits, m_gnorm_g, m_w_o, m_ln1_g, m_ln1_b, m_w_ffn_in, m_w_ffn_out, m_ln2_g, m_ln2_b, v_w_ada, v_b_ada, v_w_in, v_rel_bias, v_attn_norm_g, v_lb_logits, v_gnorm_g, v_w_o, v_ln1_g, v_ln1_b, v_w_ffn_in, v_w_ffn_out, v_ln2_g, v_ln2_b):
    mx, my, mc = _place()
    me = 4 * mx + 2 * my + mc
    chip = 2 * mx + my
    sel = jnp.stack([chip, mc]).astype(jnp.int32)
    D = x.shape[2]
    ns_ada = w_ada.shape[2]

    big = dict(w_in=(w_in, m_w_in, v_w_in), w_o=(w_o, m_w_o, v_w_o), w_ffn_in=(w_ffn_in, m_w_ffn_in, v_w_ffn_in),
               w_ffn_out=(w_ffn_out, m_w_ffn_out, v_w_ffn_out))
    shards = dict(w_in=w_in[0].astype(BF16))
    kh, rode, rows = shards["w_in"].shape[0] // 2, None, None
    for n, part in (("w_ffn_in", 19), ("w_ffn_out", 9), ("w_o", 4)):
        before, rows = rows, (rows[0] + rows[1] if rows else 0, part * kh // 32)
        (shards[n],), rode = _to_bf16(big[n][0][0], "cast_" + n,
                                      _gather_rider(shards["w_in"], rows, rode and rode[0], before, last=n == "w_o"))
    shards["w_in_gathered"] = rode[0]

    c_all = _gather_small(c.reshape(D // 128, 128), "gather_c").reshape(N_DEV, D)
    c_all = jnp.pad(c_all, ((0, 16 - N_DEV), (0, 0)))
    mod_cols = _ada_fwd(c_all, w_ada[0])[:N_DEV]
    mod_all = _gather_small(mod_cols.reshape(-1, 128), "gather_mod").reshape(N_DEV, N_DEV, ns_ada)
    mod = lax.dynamic_index_in_dim(mod_all[::2], me, axis=1, keepdims=False)
    mod = (mod.reshape(1, -1) + b_ada).reshape(6, D)

    loss, grad_x, g_big, dmod, g_small = _layer(
        x[0], loss_target[0], mod, shards, rel_bias[0], attn_norm_g, lb_logits, gnorm_g, ln1_g, ln1_b, ln2_g, ln2_b,
        place=(chip, sel))

    grads, deltas, new_m, new_v = {}, {}, {}, {}
    for n, (w, m, v) in big.items():
        g, d, mo, vo = _adamw(g_big[n], w[0], m[0], v[0], "adamw_" + n)
        grads[n], deltas[n], new_m[n], new_v[n] = g[None], d[None], mo[None], vo[None]

    small_in = dict(rel_bias=(rel_bias, m_rel_bias, v_rel_bias), attn_norm_g=(attn_norm_g, m_attn_norm_g, v_attn_norm_g),
                    lb_logits=(lb_logits, m_lb_logits, v_lb_logits), gnorm_g=(gnorm_g, m_gnorm_g, v_gnorm_g),
                    ln1_g=(ln1_g, m_ln1_g, v_ln1_g), ln1_b=(ln1_b, m_ln1_b, v_ln1_b), ln2_g=(ln2_g, m_ln2_g, v_ln2_g),
                    ln2_b=(ln2_b, m_ln2_b, v_ln2_b))
    g_all = _gather_small(_pack([dmod] + [g_small[n] for n in SMALL] + [loss], SMALL_ROWS), "gather_small")
    packed = [_pack([t] + [small_in[n][i] for n in SMALL] + [jnp.zeros((), F32)], SMALL_ROWS)
              for i, t in enumerate((b_ada, m_b_ada, v_b_ada))]
    shapes = [b_ada.shape] + [small_in[n][0].shape for n in SMALL] + [()]
    outs = [_unpack(o, shapes) for o in _small_update(g_all, *packed)]
    loss = outs[0][-1]
    for i, n in enumerate(("b_ada",) + SMALL):
        grads[n], deltas[n], new_m[n], new_v[n] = outs[0][i], outs[1][i], outs[2][i], outs[3][i]

    dmod_all = g_all[:, :6 * D // 128].reshape(N_DEV, 6 * D)
    dmod_cols = lax.dynamic_slice_in_dim(dmod_all, chip * ns_ada, ns_ada, axis=1)
    dmod_cols = jnp.pad(dmod_cols, ((0, 16 - N_DEV), (0, 0)))
    g, d, mo, vo = _ada_bwd(c_all, dmod_cols, w_ada[0], m_w_ada[0], v_w_ada[0])
    grads["w_ada"], deltas["w_ada"], new_m["w_ada"], new_v["w_ada"] = g[None], d[None], mo[None], vo[None]

    order = ("w_ada", "b_ada", "w_in", "rel_bias", "attn_norm_g", "lb_logits", "gnorm_g", "w_o", "ln1_g", "ln1_b",
             "w_ffn_in", "w_ffn_out", "ln2_g", "ln2_b")
    return (loss, grad_x[None], *[grads[n] for n in order], *[deltas[n] for n in order],
            *[new_m[n] for n in order], *[new_v[n] for n in order])
```

```python
import numpy as np
import jax
import jax.numpy as jnp
from jax import lax
from jax.experimental import pallas as pl
from jax.experimental.pallas import tpu as pltpu

F32 = jnp.float32
BF16 = jnp.bfloat16
MESH = pl.DeviceIdType.MESH
HIGHEST = lax.Precision.HIGHEST

CHUNK = 64
N_PAST = 8
QG = 4
QROWS = QG * CHUNK
KPAD = N_PAST * CHUNK
ZPAD = 2 * KPAD
UNION = (QG + N_PAST) * CHUNK
BAND = (N_PAST + 1) * CHUNK
HD_A = 64
HD_B = 128
SUB = 16
HGRN_HEADS = 8
MAX_REL = 256
EPS = 1e-5
ALPHA = 2.0 ** 0.25
LR, B1, B2, ADAM_EPS, WD, STEP = 1e-3, 0.9, 0.999, 1e-8, 0.01, 10
N_CHIPS = 4
N_DEV = 8
NEG = -1e30
TILE_BYTES = 3 << 19

NN = ((1,), (0,))
NT = ((1,), (1,))
TN = ((0,), (0,))


def _dot(a, b, dims=NN, precision=None):
    return lax.dot_general(a, b, (dims, ((), ())), preferred_element_type=F32, precision=precision)


def _params(sem=None, vmem_mb=None, **kw):
    return pltpu.CompilerParams(dimension_semantics=sem,
                                vmem_limit_bytes=None if vmem_mb is None else vmem_mb << 20, **kw)


def _row_tile(rows, cols):
    for cand in (512, 256, 128, 64, 32, 16, 8):
        if rows % cand == 0 and cand * cols * 4 <= TILE_BYTES:
            return cand
    raise ValueError((rows, cols))


def _place():
    return lax.axis_index("x"), lax.axis_index("y"), lax.axis_index("c")


def _flip(v, bit):
    return 1 - v if bit else v


ANY = pl.BlockSpec(memory_space=pl.ANY)
CHIP_FLIPS = ((1, 0), (0, 1), (1, 1))


class _Rider:
    def __init__(self, operands, out_shape, n_sems, start, finish, aliases=None):
        self.operands, self.out_shape, self.n_sems, self.start, self.finish = operands, out_shape, n_sems, start, finish
        self.aliases = aliases or {}


def _call(body, rider, *, name, grid, in_specs, out_specs, out_shape, scratch_shapes, compiler_params, operands):
    if rider is None:
        outs = pl.pallas_call(body, name=name, grid=grid, in_specs=in_specs, out_specs=out_specs, out_shape=out_shape,
                              scratch_shapes=scratch_shapes, compiler_params=compiler_params)(*operands)
        return list(outs), []
    n_in, n_out, n_sc = len(in_specs), len(out_specs), len(scratch_shapes)
    r_in, r_out = len(rider.operands), len(rider.out_shape)

    def carried(*refs):
        refs = list(refs)
        cuts = [n_in, r_in, n_out, r_out, n_sc]
        ins, r_ins, outs, r_outs, scratch = [[refs.pop(0) for _ in range(n)] for n in cuts]
        first, last = None, None
        for axis, size in enumerate(grid):
            i = pl.program_id(axis)
            first = (i == 0) if first is None else first & (i == 0)
            last = (i == size - 1) if last is None else last & (i == size - 1)

        @pl.when(first)
        def _():
            rider.start(r_ins, r_outs, *refs)

        body(*ins, *outs, *scratch)

        @pl.when(last)
        def _():
            rider.finish(r_ins, r_outs, *refs)

    sems = [pltpu.SemaphoreType.DMA((rider.n_sems,)), pltpu.SemaphoreType.DMA((rider.n_sems,))]
    outs = pl.pallas_call(carried, name=name, grid=grid, in_specs=list(in_specs) + [ANY] * r_in,
                          out_specs=list(out_specs) + [ANY] * r_out, out_shape=list(out_shape) + rider.out_shape,
                          scratch_shapes=list(scratch_shapes) + sems, compiler_params=compiler_params,
                          input_output_aliases={n_in + i: n_out + o for i, o in rider.aliases.items()},
                          )(*operands, *rider.operands)
    return list(outs[:n_out]), list(outs[n_out:])


def _alone(rider, name):
    def body(*refs):
        ins, outs, sems = refs[:len(rider.operands)], refs[len(rider.operands):-2], refs[-2:]
        rider.start(ins, outs, *sems)
        rider.finish(ins, outs, *sems)

    return pl.pallas_call(
        body, name=name, in_specs=[ANY] * len(rider.operands), out_specs=[ANY] * len(rider.out_shape),
        out_shape=rider.out_shape, input_output_aliases=rider.aliases,
        scratch_shapes=[pltpu.SemaphoreType.DMA((rider.n_sems,)), pltpu.SemaphoreType.DMA((rider.n_sems,))],
    )(*rider.operands)


class _Sems:
    def __init__(self, sems, base):
        self.sems, self.base = sems, base

    @property
    def at(self):
        return self

    def __getitem__(self, k):
        return self.sems.at[self.base + k]


def _join(*riders):
    riders = [r for r in riders if r is not None]
    if len(riders) < 2:
        return riders[0] if riders else None

    def parts(ins, outs, send_sems, recv_sems):
        i = o = s = 0
        for r in riders:
            ni, no = len(r.operands), len(r.out_shape)
            yield r, ins[i:i + ni], outs[o:o + no], _Sems(send_sems, s), _Sems(recv_sems, s)
            i, o, s = i + ni, o + no, s + r.n_sems

    def start(*refs):
        for r, *args in parts(*refs):
            r.start(*args)

    def finish(*refs):
        for r, *args in parts(*refs):
            r.finish(*args)

    aliases, i, o = {}, 0, 0
    for r in riders:
        aliases.update({i + a: o + b for a, b in r.aliases.items()})
        i, o = i + len(r.operands), o + len(r.out_shape)
    return _Rider([a for r in riders for a in r.operands], [s for r in riders for s in r.out_shape],
                  sum(r.n_sems for r in riders), start, finish, aliases)


def _gather_rider(shard, rows=None, into=None, before=None, last=True):
    K, Ns = shard.shape
    kh = K // 2
    rows = rows or (0, kh)

    def copies(w_ref, out_ref, send_sems, recv_sems):
        x, y, c = _place()
        chips = [(_flip(x, fx), _flip(y, fy)) for fx, fy in CHIP_FLIPS]

        def half(chip, which, part):
            return out_ref.at[2 * chip[0] + chip[1], pl.ds(which * kh + part[0], part[1]), :]

        def copy(k, dst, to, src=None):
            return pltpu.make_async_remote_copy(src_ref=dst if src is None else src, dst_ref=dst,
                                                send_sem=send_sems.at[k], recv_sem=recv_sems.at[k],
                                                device_id=to, device_id_type=MESH)

        def first():
            return [copy(j, half((x, y), c, rows), (*chip, c), src=w_ref.at[pl.ds(c * kh + rows[0], rows[1]), :])
                    for j, chip in enumerate(chips)]

        def onward(base, part):
            return [copy(base + j, half(chip, c, part), (x, y, 1 - c)) for j, chip in enumerate(chips)]

        def arriving(base, which, part):
            return [copy(base + j, half(chip, which, part), (x, y, c)) for j, chip in enumerate(chips)]

        return c, first, onward, arriving

    def start(ins, outs, send_sems, recv_sems):
        _, first, onward, _ = copies(ins[0], outs[0], send_sems, recv_sems)
        for cp in first() + (onward(3, before) if before else []):
            cp.start()

    def finish(ins, outs, send_sems, recv_sems):
        c, first, onward, arriving = copies(ins[0], outs[0], send_sems, recv_sems)
        sent = first() + (onward(3, before) if before else [])
        passed = onward(6, rows) if last else [None] * 3
        for arrived, cp in zip(arriving(0, c, rows), passed):
            arrived.wait_recv()
            if last:
                cp.start()
        for arrived in (arriving(3, 1 - c, before) if before else []) + (arriving(6, 1 - c, rows) if last else []):
            arrived.wait_recv()
        for cp in sent + (passed if last else []):
            cp.wait_send()

    full = jax.ShapeDtypeStruct((N_CHIPS, K, Ns), shard.dtype)
    if into is None:
        return _Rider([shard], [full], 9, start, finish)
    return _Rider([shard, into], [full], 9, start, finish, aliases={1: 0})


def _pair_rider(g_full):
    Q, K, Ns = g_full.shape
    kh = K // 2

    def copy(g_ref, got_ref, send_sems, recv_sems):
        x, y, c = _place()
        return pltpu.make_async_remote_copy(src_ref=g_ref.at[:, pl.ds((1 - c) * kh, kh), :], dst_ref=got_ref,
                                            send_sem=send_sems.at[0], recv_sem=recv_sems.at[0],
                                            device_id=(x, y, 1 - c), device_id_type=MESH)

    def start(ins, outs, send_sems, recv_sems):
        copy(ins[0], outs[0], send_sems, recv_sems).start()

    def finish(ins, outs, send_sems, recv_sems):
        copy(ins[0], outs[0], send_sems, recv_sems).wait()

    return _Rider([g_full], [jax.ShapeDtypeStruct((Q, kh, Ns), g_full.dtype)], 1, start, finish)


def _acts_rider(a, b):
    T, K = a.shape
    kh = K // 2

    def copies(ins, outs, send_sems, recv_sems):
        x, y, c = _place()
        pair = [(ins[0].at[:, pl.ds((1 - c) * kh, kh)], outs[0]), (ins[1], outs[1])]
        return [pltpu.make_async_remote_copy(src_ref=src, dst_ref=dst, send_sem=send_sems.at[k], recv_sem=recv_sems.at[k],
                                             device_id=(x, y, 1 - c), device_id_type=MESH)
                for k, (src, dst) in enumerate(pair)]

    def start(*refs):
        for cp in copies(*refs):
            cp.start()

    def finish(*refs):
        for cp in copies(*refs):
            cp.wait()

    return _Rider([a, b], [jax.ShapeDtypeStruct((T, kh), a.dtype), jax.ShapeDtypeStruct(b.shape, b.dtype)], 2,
                  start, finish)


def _share_rider(block):
    K, Ns = block.shape
    kh = K // 2

    def halves(out_ref):
        x, y, c = _place()
        return out_ref.at[pl.ds(c * kh, kh), :], out_ref.at[pl.ds((1 - c) * kh, kh), :], (x, y, 1 - c)

    def start(ins, outs, send_sems, recv_sems):
        mine, _, sibling = halves(outs[0])
        pltpu.make_async_remote_copy(src_ref=mine, dst_ref=mine, send_sem=send_sems.at[0], recv_sem=recv_sems.at[0],
                                     device_id=sibling, device_id_type=MESH).start()

    def finish(ins, outs, send_sems, recv_sems):
        mine, theirs, sibling = halves(outs[0])
        pltpu.make_async_remote_copy(src_ref=theirs, dst_ref=theirs, send_sem=send_sems.at[0], recv_sem=recv_sems.at[0],
                                     device_id=sibling, device_id_type=MESH).wait_recv()
        pltpu.make_async_remote_copy(src_ref=mine, dst_ref=mine, send_sem=send_sems.at[0], recv_sem=recv_sems.at[0],
                                     device_id=sibling, device_id_type=MESH).wait_send()

    return _Rider([block], [jax.ShapeDtypeStruct((K, Ns), block.dtype)], 1, start, finish, aliases={0: 0})


def _chips_rider(pair_sum, rows=None, into=None):
    Q, kh, Ns = pair_sum.shape
    first_row, n_rows = rows or (0, kh)

    def copies(p_ref, got_ref, send_sems, recv_sems):
        x, y, c = _place()
        part = pl.ds(first_row, n_rows)
        out = []
        for j, (fx, fy) in enumerate(CHIP_FLIPS):
            px, py = _flip(x, fx), _flip(y, fy)
            out.append(pltpu.make_async_remote_copy(
                src_ref=p_ref.at[2 * px + py, part, :], dst_ref=got_ref.at[j, part, :], send_sem=send_sems.at[j],
                recv_sem=recv_sems.at[j], device_id=(px, py, c), device_id_type=MESH))
        return out

    def start(ins, outs, send_sems, recv_sems):
        for cp in copies(ins[0], outs[0], send_sems, recv_sems):
            cp.start()

    def finish(ins, outs, send_sems, recv_sems):
        sends = copies(ins[0], outs[0], send_sems, recv_sems)
        for cp in sends:
            cp.wait_recv()
        for cp in sends:
            cp.wait_send()

    got = jax.ShapeDtypeStruct((Q - 1, kh, Ns), pair_sum.dtype)
    if into is None:
        return _Rider([pair_sum], [got], 3, start, finish)
    return _Rider([pair_sum, into], [got], 3, start, finish, aliases={1: 0})


def _mm(a, b, *, grid, a_spec, b_spec, o_spec, o_shape, o_dtype, dims, acc_shape, name, rider=None, zero_rows=0,
        vmem_mb=48):
    nk = grid[2]

    def body(a_ref, b_ref, o_ref, *scratch):
        if zero_rows:
            @pl.when(pl.program_id(0) < zero_rows)
            def _():
                o_ref[...] = jnp.zeros_like(o_ref)

            @pl.when(pl.program_id(0) >= zero_rows)
            def _():
                o_ref[...] = _dot(a_ref[...], b_ref[...], dims).astype(o_ref.dtype)
            return
        part = _dot(a_ref[...], b_ref[...], dims)
        if nk == 1:
            o_ref[...] = part.astype(o_ref.dtype)
            return
        acc_ref, = scratch
        k = pl.program_id(2)

        @pl.when(k == 0)
        def _():
            acc_ref[...] = part

        @pl.when(k > 0)
        def _():
            acc_ref[...] += part

        @pl.when(k == nk - 1)
        def _():
            o_ref[...] = acc_ref[...].astype(o_ref.dtype)

    (out,), rode = _call(
        body, rider, name=name, grid=grid, in_specs=[a_spec, b_spec], out_specs=[o_spec],
        out_shape=[jax.ShapeDtypeStruct(o_shape, o_dtype)],
        scratch_shapes=[] if nk == 1 else [pltpu.VMEM(acc_shape, F32)],
        compiler_params=_params(("parallel", "parallel", "arbitrary") if rider is None else ("arbitrary",) * 3, vmem_mb),
        operands=(a, b))
    return out if rider is None else (out, rode)


def _mm_nn(a, w, *, tm, tn, tk, name, rider=None, cols=None, o_dtype=F32, pad_rows=0):
    T, K = a.shape
    Q, _, Ns = w.shape
    nbs = Ns // tn
    tm = min(tm, T)
    j0, j1 = cols or (0, Q * nbs)
    lead = pad_rows // tm
    return _mm(a, w, grid=(lead + T // tm, j1 - j0, K // tk),
               a_spec=pl.BlockSpec((tm, tk), lambda i, j, k: (jnp.maximum(i - lead, 0), k)),
               b_spec=pl.BlockSpec((None, tk, tn), lambda i, j, k: ((j + j0) // nbs, k, (j + j0) % nbs)),
               o_spec=pl.BlockSpec((tm, tn), lambda i, j, k: (i, j)),
               o_shape=(pad_rows + T, (j1 - j0) * tn), o_dtype=o_dtype, dims=NN, acc_shape=(tm, tn), name=name,
               rider=rider, zero_rows=lead)


def _col_blocks(g, rows, tn, at):
    if g.ndim == 2:
        return pl.BlockSpec((rows, tn), at)
    per = g.shape[2] // tn

    def stacked(*idx):
        r, c = at(*idx)
        return c // per, r, c % per

    return pl.BlockSpec((None, rows, tn), stacked)


def _mm_nt(g, w, *, tm, to, tn, name, rider=None):
    T = g.shape[-2]
    Q, K, Ns = w.shape
    nbs = Ns // tn
    tm = min(tm, T)
    return _mm(g, w, grid=(T // tm, K // to, Q * nbs),
               a_spec=_col_blocks(g, tm, tn, lambda i, j, n: (i, n)),
               b_spec=pl.BlockSpec((None, to, tn), lambda i, j, n: (n // nbs, j, n % nbs)),
               o_spec=pl.BlockSpec((tm, to), lambda i, j, n: (i, j)),
               o_shape=(T, K), o_dtype=F32, dims=NT, acc_shape=(tm, to), name=name, rider=rider)


def _mm_tn(a, g, *, q, tk, tn, tt, name, rider=None):
    T, K = a.shape
    Ns = g.shape[-1] * (g.ndim - 1) // q
    nbs = Ns // tn
    return _mm(a, g, grid=(K // tk, q * nbs, T // tt),
               a_spec=pl.BlockSpec((tt, tk), lambda i, j, t: (t, i)),
               b_spec=_col_blocks(g, tt, tn, lambda i, j, t: (t, j)),
               o_spec=pl.BlockSpec((None, tk, tn), lambda i, j, t: (j // nbs, i, j % nbs)),
               o_shape=(q, K, Ns), o_dtype=BF16, dims=TN, acc_shape=(tk, tn), name=name, rider=rider)


def _mm_tn_add(a, g, part, *, tk, tn, name, rider=None):
    T, K = a.shape
    Q, _, Ns = part.shape
    nbs = Ns // tn

    def body(a_ref, g_ref, p_ref, o_ref):
        o_ref[...] = (_dot(a_ref[...], g_ref[...], TN) + p_ref[...].astype(F32)).astype(o_ref.dtype)

    blk = pl.BlockSpec((None, tk, tn), lambda i, j: (j // nbs, i, j % nbs))
    (out,), rode = _call(
        body, rider, name=name, grid=(K // tk, Q * nbs),
        in_specs=[pl.BlockSpec((T, tk), lambda i, j: (0, i)), _col_blocks(g, T, tn, lambda i, j: (0, j)), blk],
        out_specs=[blk], out_shape=[jax.ShapeDtypeStruct((Q, K, Ns), BF16)], scratch_shapes=[],
        compiler_params=_params(("arbitrary", "arbitrary"), 48), operands=(a, g, part))
    return out, rode


def _ln(u):
    mu = jnp.mean(u, axis=-1, keepdims=True)
    d = u - mu
    r = lax.rsqrt(jnp.mean(d * d, axis=-1, keepdims=True) + EPS)
    return d * r, r


def _ln_bwd(dy, un, r):
    return r * (dy - jnp.mean(dy, axis=-1, keepdims=True) - un * jnp.mean(dy * un, axis=-1, keepdims=True))


def _colsum(v):
    return jnp.sum(v, axis=0, keepdims=True)


def _rowwise(name, fn, bigs, vecs, out_dtypes, n_acc, tm=128, rider=None):
    T, D = bigs[0].shape
    nb, nv, no = len(bigs), len(vecs), len(out_dtypes)

    def body(*refs):
        outs, accs = fn([r[...] for r in refs[:nb]], [r[...] for r in refs[nb:nb + nv]])
        for r, o in zip(refs[nb + nv:nb + nv + no], outs):
            r[...] = o.astype(r.dtype)
        if n_acc:
            acc_ref = refs[nb + nv + no]

            @pl.when(pl.program_id(0) == 0)
            def _():
                acc_ref[...] = jnp.zeros_like(acc_ref)

            for row, a in enumerate(accs):
                acc_ref[row:row + 1, :] += a

    big_spec = pl.BlockSpec((tm, D), lambda i: (i, 0))
    vec_spec = pl.BlockSpec((1, D), lambda i: (0, 0))
    out_shape = [jax.ShapeDtypeStruct((T, D), dt) for dt in out_dtypes]
    out_specs = [big_spec] * no
    if n_acc:
        out_shape.append(jax.ShapeDtypeStruct((8, D), F32))
        out_specs.append(pl.BlockSpec((8, D), lambda i: (0, 0)))
    outs, rode = _call(
        body, rider, name=name, grid=(T // tm,), in_specs=[big_spec] * nb + [vec_spec] * nv,
        out_specs=out_specs, out_shape=out_shape, scratch_shapes=[],
        compiler_params=_params(("arbitrary",), 48), operands=(*bigs, *vecs))
    return outs if rider is None else (outs, rode)


def _to_bf16(w, name, rider=None):
    R, C = w.shape
    tr = _row_tile(R, C)

    def body(w_ref, o_ref):
        o_ref[...] = w_ref[...].astype(o_ref.dtype)

    blk = pl.BlockSpec((tr, C), lambda i: (i, 0))
    return _call(body, rider, name=name, grid=(R // tr,), in_specs=[blk], out_specs=[blk],
                 out_shape=[jax.ShapeDtypeStruct((R, C), BF16)], scratch_shapes=[],
                 compiler_params=_params(("arbitrary",)), operands=(w,))


def _pre_mixer(x, scale1, shift1):
    def fn(b, v):
        xn, _ = _ln(b[0])
        return [xn * (1.0 + v[0]) + v[1]], []
    return _rowwise("pre_mixer", fn, [x], [scale1, shift1], [BF16], 0)[0]


def _post_mixer(mix, x, gate1, g1, b1, scale2, shift2, rider=None):
    def fn(b, v):
        un1, _ = _ln(ALPHA * b[1] + v[0] * b[0])
        x1 = un1 * v[1] + v[2]
        xn1, _ = _ln(x1)
        return [x1, xn1 * (1.0 + v[3]) + v[4]], []
    return _rowwise("post_mixer", fn, [mix, x], [gate1, g1, b1, scale2, shift2], [F32, BF16], 0, rider=rider)


def _loss_head(f, x1, tgt, gate2, g2, b2):
    def fn(b, v):
        ff, xx, tt = b
        d_model = ff.shape[-1]
        un2, r2 = _ln(ALPHA * xx + v[0] * ff)
        err = un2 * v[1] + v[2] - tt
        dy = err * (1.0 / d_model)
        du2 = _ln_bwd(dy * v[1], un2, r2)
        return [du2, du2 * v[0]], [_colsum(dy * un2), _colsum(dy), _colsum(du2 * ff), _colsum(err * err)]
    return _rowwise("loss_head", fn, [f, x1, tgt], [gate2, g2, b2], [F32, BF16], 4)


def _mid_bwd(dh2, du2, x1, mix, x, gate1, g1, scale2, rider=None):
    def fn(b, v):
        dh, du, xx1, mm, xx = b
        xn1, r1n = _ln(xx1)
        dx1 = ALPHA * du + _ln_bwd(dh * (1.0 + v[2]), xn1, r1n)
        un1, r1 = _ln(ALPHA * xx + v[0] * mm)
        du1 = _ln_bwd(dx1 * v[1], un1, r1)
        return [du1, du1 * v[0]], [_colsum(dh * xn1), _colsum(dh), _colsum(dx1 * un1), _colsum(dx1),
                                   _colsum(du1 * mm)]
    return _rowwise("mid_bwd", fn, [dh2, du2, x1, mix, x], [gate1, g1, scale2], [F32, BF16], 5, rider=rider)


def _first_bwd(dh1, du1, x, scale1, rider=None):
    def fn(b, v):
        dh, du, xx = b
        xn, r0 = _ln(xx)
        return [ALPHA * du + _ln_bwd(dh * (1.0 + v[0]), xn, r0)], [_colsum(dh * xn), _colsum(dh)]
    return _rowwise("first_bwd", fn, [dh1, du1, x], [scale1], [F32], 2, rider=rider)


def _ffn_in_swiglu(h2, w, *, tm, tn, rider=None):
    T, K = h2.shape
    Q, _, Ns = w.shape
    nbs = Ns // tn
    half = Q * nbs // 2
    tm = min(tm, T)

    def body(a_ref, wg_ref, wu_ref, g_ref, u_ref, act_ref):
        a = a_ref[...]
        g, u = _dot(a, wg_ref[...]), _dot(a, wu_ref[...])
        g_ref[...] = g.astype(g_ref.dtype)
        u_ref[...] = u.astype(u_ref.dtype)
        act_ref[...] = (g * jax.nn.sigmoid(g) * u).astype(act_ref.dtype)

    cols = lambda first: pl.BlockSpec((None, K, tn), lambda i, j: ((j + first) // nbs, 0, (j + first) % nbs))
    blk = pl.BlockSpec((tm, tn), lambda i, j: (i, j))
    return _call(
        body, rider, name="ffn_in", grid=(T // tm, half),
        in_specs=[pl.BlockSpec((tm, K), lambda i, j: (i, 0)), cols(0), cols(half)], out_specs=[blk] * 3,
        out_shape=[jax.ShapeDtypeStruct((T, half * tn), BF16)] * 3, scratch_shapes=[],
        compiler_params=_params(("arbitrary", "arbitrary"), 48), operands=(h2, w, w))


def _d_act_swiglu(df, w, gate, up, *, tm, to, rider=None):
    T, N = df.shape
    F = w.shape[1]
    tm = min(tm, T)

    def body(df_ref, w_ref, g_ref, u_ref, o_ref):
        d = _dot(df_ref[...], w_ref[...], NT)
        g = g_ref[...].astype(F32)
        s = jax.nn.sigmoid(g)
        o_ref[0] = (d * u_ref[...].astype(F32) * s * (1.0 + g * (1.0 - s))).astype(o_ref.dtype)
        o_ref[1] = (d * g * s).astype(o_ref.dtype)

    blk = pl.BlockSpec((tm, to), lambda i, j: (i, j))
    return _call(
        body, rider, name="d_act", grid=(T // tm, F // to),
        in_specs=[pl.BlockSpec((tm, N), lambda i, j: (i, 0)), pl.BlockSpec((None, to, N), lambda i, j: (0, j, 0)), blk, blk],
        out_specs=[pl.BlockSpec((2, tm, to), lambda i, j: (0, i, j))],
        out_shape=[jax.ShapeDtypeStruct((2, T, F), BF16)], scratch_shapes=[],
        compiler_params=_params(("arbitrary", "arbitrary"), 48), operands=(df, w, gate, up))


PAIR = 2


def _fill_table(table_ref, band_ref):
    table_ref[...] = jnp.full(table_ref.shape, NEG, F32)
    for e in range(PAIR):
        for g in range(QG):
            table_ref[e, g * CHUNK:(g + 1) * CHUNK, g * CHUNK:g * CHUNK + BAND] = band_ref[e]


def _attn_probs(q_ref, k_ref, bias_ref, e, step):
    start = pl.multiple_of(step * QROWS, QROWS)
    lanes = pl.ds(e * HD_A, HD_A)
    s = _dot(q_ref[:, lanes], k_ref[pl.ds(start + ZPAD - KPAD, UNION), lanes], NT) * (HD_A ** -0.5) + bias_ref[e]
    col = lax.broadcasted_iota(jnp.int32, s.shape, 1)
    s = jnp.where(col + start >= KPAD, s, NEG)
    p = jnp.exp(s - jnp.max(s, axis=-1, keepdims=True))
    return p / jnp.sum(p, axis=-1, keepdims=True), start


def _attn_specs(T, n_pairs):
    wide = PAIR * HD_A
    per_step = pl.BlockSpec((QROWS, wide), lambda hp, n: (n, hp))
    queries = pl.BlockSpec((QROWS, wide), lambda hp, n: (n + ZPAD // QROWS, hp))
    keys = pl.BlockSpec((ZPAD + T, wide), lambda hp, n: (0, n_pairs + hp))
    values = pl.BlockSpec((ZPAD + T, wide), lambda hp, n: (0, 2 * n_pairs + hp))
    grads = pl.BlockSpec((T, wide), lambda hp, n: (0, hp))
    table = pl.BlockSpec((PAIR, CHUNK, BAND), lambda hp, n: (hp, 0, 0))
    vec = pl.BlockSpec((1, wide), lambda hp, n: (0, hp))
    return per_step, queries, keys, values, grads, table, vec


def _probs_spec():
    return pl.BlockSpec((PAIR, QROWS, UNION), lambda hp, n: (hp, n, 0))


def _attn_fwd(qkv, bias, gain, rider=None):
    T = qkv.shape[0] - ZPAD
    W = gain.shape[1]
    n_pairs = W // (PAIR * HD_A)

    def body(q_ref, k_ref, v_ref, band_ref, gain_ref, o_ref, p_ref, table_ref):
        @pl.when(pl.program_id(1) == 0)
        def _():
            _fill_table(table_ref, band_ref)

        for e in range(PAIR):
            lanes = pl.ds(e * HD_A, HD_A)
            p, start = _attn_probs(q_ref, k_ref, table_ref, e, pl.program_id(1))
            p_ref[e] = p.astype(p_ref.dtype)
            o = _dot(p_ref[e], v_ref[pl.ds(start + ZPAD - KPAD, UNION), lanes])
            rr = lax.rsqrt(jnp.mean(o * o, axis=-1, keepdims=True) + EPS)
            o_ref[:, lanes] = (o * rr * gain_ref[:, lanes]).astype(o_ref.dtype)

    per_step, queries, keys, values, _, table, vec = _attn_specs(T, n_pairs)
    return _call(
        body, rider, name="attn_fwd", grid=(n_pairs, T // QROWS), in_specs=[queries, keys, values, table, vec],
        out_specs=[per_step, _probs_spec()],
        out_shape=[jax.ShapeDtypeStruct((T, W), BF16), jax.ShapeDtypeStruct((n_pairs * PAIR, T, UNION), BF16)],
        scratch_shapes=[pltpu.VMEM((PAIR, QROWS, UNION), F32)],
        compiler_params=_params(("arbitrary", "arbitrary"), 40), operands=(qkv, qkv, qkv, bias, gain))


def _attn_bwd(qkv, probs, gain, dmixin, rider=None):
    T = qkv.shape[0] - ZPAD
    W = gain.shape[1]
    n_pairs = W // (PAIR * HD_A)
    scale = HD_A ** -0.5

    def body(q_ref, k_ref, v_ref, p_ref, gain_ref, don_ref, dq_ref, dkb_ref, dvb_ref, dband_ref, dgain_ref,
             dtable_ref, dk_ref, dv_ref):
        n = pl.program_id(1)

        @pl.when(n == 0)
        def _():
            dk_ref[...] = jnp.zeros_like(dk_ref)
            dv_ref[...] = jnp.zeros_like(dv_ref)
            dtable_ref[...] = jnp.zeros_like(dtable_ref)
            dgain_ref[...] = jnp.zeros_like(dgain_ref)

        for e in range(PAIR):
            lanes = pl.ds(e * HD_A, HD_A)
            start = pl.multiple_of(n * QROWS, QROWS)
            keys, in_qkv = pl.ds(start, UNION), pl.ds(start + ZPAD - KPAD, UNION)
            pb = p_ref[e]
            p = pb.astype(F32)
            vb = v_ref[in_qkv, lanes]
            o = _dot(pb, vb)
            rr = lax.rsqrt(jnp.mean(o * o, axis=-1, keepdims=True) + EPS)
            on = o * rr
            d_on = don_ref[:, lanes]
            dgain_ref[:, lanes] += _colsum(d_on * on)
            dyo = d_on * gain_ref[:, lanes]
            do = rr * (dyo - on * jnp.mean(dyo * on, axis=-1, keepdims=True))
            dob = do.astype(BF16)
            dp = _dot(dob, vb, NT)
            ds = p * (dp - jnp.sum(do * o, axis=-1, keepdims=True))
            dtable_ref[e] += ds
            dsb = ds.astype(BF16)
            dq_ref[:, lanes] = (_dot(dsb, k_ref[in_qkv, lanes]) * scale).astype(dq_ref.dtype)
            dk_ref[keys, lanes] += _dot(dsb, q_ref[:, lanes], TN) * scale
            dv_ref[keys, lanes] += _dot(pb, dob, TN)

        @pl.when(n == T // QROWS - 1)
        def _():
            for e in range(PAIR):
                dband_ref[e] = sum(dtable_ref[e, g * CHUNK:(g + 1) * CHUNK, g * CHUNK:g * CHUNK + BAND]
                                   for g in range(QG))
            dkb_ref[...] = dk_ref[KPAD:, :].astype(dkb_ref.dtype)
            dvb_ref[...] = dv_ref[KPAD:, :].astype(dvb_ref.dtype)

    per_step, queries, keys, values, grads, table, vec = _attn_specs(T, n_pairs)
    H = n_pairs * PAIR
    return _call(
        body, rider, name="attn_bwd", grid=(n_pairs, T // QROWS),
        in_specs=[queries, keys, values, _probs_spec(), vec, per_step],
        out_specs=[per_step, grads, grads, table, vec],
        out_shape=[jax.ShapeDtypeStruct((T, W), BF16)] * 3 + [jax.ShapeDtypeStruct((H, CHUNK, BAND), F32),
                                                              jax.ShapeDtypeStruct((1, W), F32)],
        scratch_shapes=[pltpu.VMEM((PAIR, QROWS, UNION), F32)] + [pltpu.VMEM((KPAD + T, PAIR * HD_A), F32)] * 2,
        compiler_params=_params(("arbitrary", "arbitrary"), 40),
        operands=(qkv, qkv, qkv, probs, gain, dmixin))


N_DIAG = CHUNK + BAND - 1


def _bias_band(rel_bias):
    H = rel_bias.shape[0]
    idx = np.clip(BAND - 1 - np.arange(N_DIAG), -MAX_REL, MAX_REL) + MAX_REL
    rolled = rel_bias[:, idx[(np.arange(N_DIAG) + CHUNK - 1) % N_DIAG]]
    flat = jnp.broadcast_to(rolled[:, None, :], (H, CHUNK, N_DIAG)).reshape(H, CHUNK * N_DIAG)
    return flat[:, :CHUNK * (N_DIAG - 1)].reshape(H, CHUNK, N_DIAG - 1)[:, :, :BAND]


def _bias_band_grad(dband):
    H = dband.shape[0]
    skew = jnp.pad(dband, ((0, 0), (0, 0), (CHUNK - 1, 0))).reshape(H, CHUNK * N_DIAG)
    skew = jnp.pad(skew, ((0, 0), (0, CHUNK))).reshape(H, CHUNK, N_DIAG + 1)
    diag = jnp.sum(skew, axis=1)[:, :N_DIAG]
    n_far = BAND - MAX_REL
    far = jnp.sum(diag[:, :n_far], axis=1, keepdims=True)
    near = diag[:, n_far:][:, ::-1]
    zeros = jnp.zeros((H, MAX_REL - (CHUNK - 1)), F32)
    return jnp.concatenate([zeros, near, far], axis=1)


def _tri(n, lower):
    r = lax.broadcasted_iota(jnp.int32, (n, n), 0)
    c = lax.broadcasted_iota(jnp.int32, (n, n), 1)
    return jnp.where((c <= r) if lower else (c >= r), 1.0, 0.0).astype(F32)


def _hgrn_gates(zq_ref, zf_ref, lbl_ref, q_s, k_s, b_s):
    lb = jax.nn.sigmoid(lbl_ref[0:1, :] - lbl_ref[1:2, :])
    zq = zq_ref[...]
    sig = jax.nn.sigmoid(zf_ref[...])
    f = lb + (1.0 - lb) * sig
    sq = jax.nn.sigmoid(zq)
    q_s[...] = zq * sq
    k_s[...] = 1.0 - f
    b_s[...] = _dot(_tri(CHUNK, True), jnp.log(f), precision=HIGHEST)
    return lb, sig, f, sq


def _sub_rows(i):
    return pl.ds(i * SUB, SUB)


def _row_mask(s):
    return lax.broadcasted_iota(jnp.int32, (SUB, HD_B), 0) >= s


def _decay_from(b_sub, b_row, s):
    return jnp.where(_row_mask(s), jnp.exp(jnp.minimum(b_sub - b_row, 0.0)), 0.0)


def _hgrn_fwd(proj, lb_logits, gnorm_g, rider=None):
    T = proj.shape[0]
    nC = T // CHUNK
    W = lb_logits.shape[1]
    G = W // HD_B // HGRN_HEADS
    col0 = (proj.shape[1] - 4 * W) // (HD_B * HGRN_HEADS)
    wide = HGRN_HEADS * HD_B

    def body(*refs):
        @pl.when(pl.program_id(1) == 0)
        def _():
            refs[9][...] = jnp.zeros_like(refs[9])

        for h in range(HGRN_HEADS):
            lanes = pl.ds(h * HD_B, HD_B)
            one_head(*[r.at[:, lanes] for r in refs[:5]], refs[5], *[r.at[:, lanes] for r in refs[6:8]],
                     *[r.at[h] for r in refs[8:]])

    def one_head(zq_ref, zf_ref, xi_ref, zg_ref, lbl_ref, gn_ref, mix_ref, o_ref, stall_ref, st_ref, q_s, k_s, b_s, acc_s):
        _hgrn_gates(zq_ref, zf_ref, lbl_ref, q_s, k_s, b_s)
        q, k, b = q_s[...], k_s[...], b_s[...]
        st = st_ref[...]
        stall_ref[...] = st
        b_last = b_s[CHUNK - 1:CHUNK, :]
        acc_s[...] = _dot((q * jnp.exp(b)).astype(BF16), st.astype(BF16), NT)
        for i in range(CHUNK // SUB):
            rows = _sub_rows(i)
            q_i, b_i = q_s[rows, :], b_s[rows, :]
            acc = jnp.zeros((SUB, HD_B), F32)
            if i:
                past = pl.ds(0, i * SUB)
                b_ref = b_s[i * SUB - 1:i * SUB, :]
                qs = (q_i * jnp.exp(b_i - b_ref)).astype(BF16)
                ks = (k_s[past, :] * jnp.exp(b_ref - b_s[past, :])).astype(BF16)
                acc += _dot(_dot(qs, ks, NT).astype(BF16), xi_ref[past, :].astype(BF16))
            for s in range(SUB):
                row = pl.ds(i * SUB + s, 1)
                w = q_i * _decay_from(b_i, b_s[row, :], s)
                acc += jnp.sum(w * k_s[row, :], axis=-1, keepdims=True) * xi_ref[row, :]
            acc_s[rows, :] += acc
        o = acc_s[...]
        kd = (k * jnp.exp(b_last - b)).astype(BF16)
        st_ref[...] = st * jnp.exp(b_last) + _dot(xi_ref[...].astype(BF16), kd, TN)
        o_ref[...] = o
        zg = zg_ref[...]
        rr = lax.rsqrt(jnp.mean(o * o, axis=-1, keepdims=True) + EPS)
        mix_ref[...] = (o * rr * gn_ref[...] * (zg * jax.nn.sigmoid(zg))).astype(mix_ref.dtype)

    col = lambda part: pl.BlockSpec((CHUNK, wide), lambda g, n: (n, col0 + part * G + g))
    out_blk = pl.BlockSpec((CHUNK, wide), lambda g, n: (n, g))
    tile = pltpu.VMEM((HGRN_HEADS, CHUNK, HD_B), F32)
    return _call(
        body, rider, name="hgrn_fwd", grid=(G, nC),
        in_specs=[col(0), col(1), col(2), col(3), pl.BlockSpec((2, wide), lambda g, n: (0, g)),
                  pl.BlockSpec((1, HD_B), lambda g, n: (0, 0))],
        out_specs=[out_blk, out_blk, pl.BlockSpec((HGRN_HEADS, None, HD_B, HD_B), lambda g, n: (g, n, 0, 0))],
        out_shape=[jax.ShapeDtypeStruct((T, W), BF16), jax.ShapeDtypeStruct((T, W), F32),
                   jax.ShapeDtypeStruct((G * HGRN_HEADS, nC, HD_B, HD_B), F32)],
        scratch_shapes=[pltpu.VMEM((HGRN_HEADS, HD_B, HD_B), F32), tile, tile, tile, tile],
        compiler_params=_params(("arbitrary", "arbitrary")),
        operands=(proj, proj, proj, proj, lb_logits, gnorm_g))


def _hgrn_bwd(proj, lb_logits, gnorm_g, o_b, st_all, dmixin, d_attn, rider=None):
    T = proj.shape[0]
    nC = T // CHUNK
    W = lb_logits.shape[1]
    wa = d_attn[0].shape[1]
    assert W == HGRN_HEADS * HD_B, "one grid step takes every head: it writes whole rows of d proj"
    G = W // HD_B // HGRN_HEADS
    wide = HGRN_HEADS * HD_B
    col0 = (proj.shape[1] - 4 * W) // wide
    dcol0 = (dmixin.shape[1] - W) // wide

    def body(*refs):
        g, n = pl.program_id(0), pl.program_id(1)
        dproj_ref, dl0_ref, dgn_ref, dst_ref = refs[12:16]
        for i in range(3):
            dproj_ref[:, i * wa:(i + 1) * wa] = refs[9 + i][...]

        @pl.when(n == 0)
        def _():
            dst_ref[...] = jnp.zeros_like(dst_ref)
            dl0_ref[...] = jnp.zeros_like(dl0_ref)

        @pl.when((n == 0) & (g == 0))
        def _():
            dgn_ref[...] = jnp.zeros_like(dgn_ref)

        for h in range(HGRN_HEADS):
            lanes = pl.ds(h * HD_B, HD_B)
            cut = lambda r: r.at[:, lanes]
            parts = [dproj_ref.at[:, pl.ds(3 * wa + part * W + h * HD_B, HD_B)] for part in range(4)]
            one_head(*[cut(r) for r in refs[:5]], refs[5], cut(refs[6]), refs[7].at[h], cut(refs[8]),
                     *parts, cut(dl0_ref), dgn_ref, *[r.at[h] for r in refs[15:]])

    def one_head(zq_ref, zf_ref, xi_ref, zg_ref, lbl_ref, gn_ref, o_ref, st_ref, dout_ref,
                 dzq_ref, dzf_ref, dxi_ref, dzg_ref, dl0_ref, dgn_ref, dst_ref, q_s, k_s, b_s, do_s, dq_s, dk_s, di_s):
        lb, sig, f, sq = _hgrn_gates(zq_ref, zf_ref, lbl_ref, q_s, k_s, b_s)
        q, k, b = q_s[...], k_s[...], b_s[...]
        zg, o, dout = zg_ref[...], o_ref[...], dout_ref[...]
        sg = jax.nn.sigmoid(zg)
        rr = lax.rsqrt(jnp.mean(o * o, axis=-1, keepdims=True) + EPS)
        on = o * rr
        gn = gn_ref[...]
        dzg_ref[...] = (dout * on * gn * sg * (1.0 + zg * (1.0 - sg))).astype(dzg_ref.dtype)
        d_on = dout * zg * sg
        dgn_ref[...] += _colsum(d_on * on)
        d_on = d_on * gn
        do = rr * (d_on - on * jnp.mean(d_on * on, axis=-1, keepdims=True))
        do_s[...] = do
        dob = do.astype(BF16)
        st, dst = st_ref[...], dst_ref[...]
        b_last = b_s[CHUNK - 1:CHUNK, :]
        eb, e_last, k_dec = jnp.exp(b), jnp.exp(b_last), jnp.exp(b_last - b)
        qt, kd = q * eb, k * k_dec
        dstb = dst.astype(BF16)
        xib = xi_ref[...].astype(BF16)
        d_kd = _dot(xib, dstb)
        dq_s[...] = _dot(dob, st.astype(BF16)) * eb
        dk_s[...] = d_kd * k_dec
        di_s[...] = _dot(kd.astype(BF16), dstb, NT)
        d_b_last = e_last * _colsum(st * dst) + _colsum(d_kd * kd)
        dst_ref[...] = _dot(dob, qt.astype(BF16), TN) + dst * e_last
        for i in range(CHUNK // SUB):
            rows = _sub_rows(i)
            q_i, b_i, do_i = q_s[rows, :], b_s[rows, :], do_s[rows, :]
            dq_i = jnp.zeros((SUB, HD_B), F32)
            if i:
                past = pl.ds(0, i * SUB)
                b_ref = b_s[i * SUB - 1:i * SUB, :]
                e_q, e_k = jnp.exp(b_i - b_ref), jnp.exp(b_ref - b_s[past, :])
                qs, ks = (q_i * e_q).astype(BF16), (k_s[past, :] * e_k).astype(BF16)
                xi_p, do_b = xi_ref[past, :].astype(BF16), do_i.astype(BF16)
                di_s[past, :] += _dot(_dot(ks, qs, NT).astype(BF16), do_b)
                dq_i += _dot(_dot(do_b, xi_p, NT).astype(BF16), ks) * e_q
                dk_s[past, :] += _dot(_dot(xi_p, do_b, NT).astype(BF16), qs) * e_k
            for s in range(SUB):
                row = pl.ds(i * SUB + s, 1)
                k_row, i_row = k_s[row, :], xi_ref[row, :]
                e = _decay_from(b_i, b_s[row, :], s)
                w = q_i * e
                a_col = jnp.sum(w * k_row, axis=-1, keepdims=True)
                da_col = jnp.sum(do_i * i_row, axis=-1, keepdims=True)
                di_s[row, :] += _colsum(a_col * do_i)
                dq_i += da_col * e * k_row
                dk_s[row, :] += _colsum(da_col * w)
            dq_s[rows, :] += dq_i
        dq, dk = dq_s[...], dk_s[...]
        db = q * dq - k * dk
        is_last = lax.broadcasted_iota(jnp.int32, (CHUNK, HD_B), 0) == CHUNK - 1
        db = db + jnp.where(is_last, d_b_last, 0.0)
        df = _dot(_tri(CHUNK, False), db, precision=HIGHEST) / f - dk
        dzf_ref[...] = (df * (1.0 - lb) * sig * (1.0 - sig)).astype(dzf_ref.dtype)
        dl0_ref[...] += _colsum(df * (1.0 - sig)) * (lb * (1.0 - lb))
        zq = zq_ref[...]
        dzq_ref[...] = (dq * sq * (1.0 + zq * (1.0 - sq))).astype(dzq_ref.dtype)
        dxi_ref[...] = di_s[...].astype(dxi_ref.dtype)

    rev = lambda n: nC - 1 - n
    col = lambda part: pl.BlockSpec((CHUNK, wide), lambda g, n: (rev(n), col0 + part * G + g))
    blk = pl.BlockSpec((CHUNK, wide), lambda g, n: (rev(n), g))
    tile = pltpu.VMEM((HGRN_HEADS, CHUNK, HD_B), F32)
    rows = lambda width: pl.BlockSpec((CHUNK, width), lambda g, n: (rev(n), 0))
    return _call(
        body, rider, name="hgrn_bwd", grid=(G, nC),
        in_specs=[col(0), col(1), col(2), col(3), pl.BlockSpec((2, wide), lambda g, n: (0, g)),
                  pl.BlockSpec((1, HD_B), lambda g, n: (0, 0)), blk,
                  pl.BlockSpec((HGRN_HEADS, None, HD_B, HD_B), lambda g, n: (g, rev(n), 0, 0)),
                  pl.BlockSpec((CHUNK, wide), lambda g, n: (rev(n), dcol0 + g)), rows(wa), rows(wa), rows(wa)],
        out_specs=[rows(3 * wa + 4 * W), pl.BlockSpec((1, wide), lambda g, n: (0, g)),
                   pl.BlockSpec((1, HD_B), lambda g, n: (0, 0))],
        out_shape=[jax.ShapeDtypeStruct((T, 3 * wa + 4 * W), BF16), jax.ShapeDtypeStruct((1, W), F32),
                   jax.ShapeDtypeStruct((1, HD_B), F32)],
        scratch_shapes=[pltpu.VMEM((HGRN_HEADS, HD_B, HD_B), F32)] + [tile] * 7,
        compiler_params=_params(("arbitrary", "arbitrary")),
        operands=(proj, proj, proj, proj, lb_logits, gnorm_g, o_b, st_all, dmixin, *d_attn))


def _adamw_math(g, w, m, v):
    m = B1 * m + (1.0 - B1) * g
    v = B2 * v + (1.0 - B2) * (g * g)
    m_hat = m / (1.0 - B1 ** STEP)
    v_hat = v / (1.0 - B2 ** STEP)
    return -LR * (m_hat / (jnp.sqrt(v_hat) + ADAM_EPS) + WD * w), m, v


def _adamw(g, w, m, v, name):
    R, C = g.shape
    tr = _row_tile(R, C)

    def body(g_ref, w_ref, m_ref, v_ref, go_ref, d_ref, mo_ref, vo_ref):
        g = g_ref[...]
        go_ref[...] = g
        d_ref[...], mo_ref[...], vo_ref[...] = _adamw_math(g, w_ref[...], m_ref[...], v_ref[...])

    blk = pl.BlockSpec((tr, C), lambda i: (i, 0))
    return pl.pallas_call(
        body, name=name, grid=(R // tr,), in_specs=[blk] * 4, out_specs=[blk] * 4,
        out_shape=[jax.ShapeDtypeStruct((R, C), F32)] * 4, compiler_params=_params(("parallel",), 40),
    )(g, w, m, v)


def _sum_pair(g_full, from_sibling, sel, name):
    Q, K, Ns = g_full.shape
    kh = K // 2
    tr = _row_tile(kh, Ns)
    nh = kh // tr

    def body(sel_ref, a_ref, b_ref, o_ref):
        o_ref[...] = (a_ref[...].astype(F32) + b_ref[...].astype(F32)).astype(o_ref.dtype)

    return pl.pallas_call(
        body, name=name,
        grid_spec=pltpu.PrefetchScalarGridSpec(
            num_scalar_prefetch=1, grid=(Q, nh),
            in_specs=[pl.BlockSpec((None, tr, Ns), lambda q, i, sel: (q, sel[1] * nh + i, 0)),
                      pl.BlockSpec((None, tr, Ns), lambda q, i, sel: (q, i, 0))],
            out_specs=pl.BlockSpec((None, tr, Ns), lambda q, i, sel: (q, i, 0))),
        out_shape=jax.ShapeDtypeStruct((Q, kh, Ns), BF16), compiler_params=_params(("parallel", "parallel")),
    )(sel, g_full, from_sibling)


def _sum_chips(pair_sum, from_chips, sel, name):
    Q, kh, Ns = pair_sum.shape
    tr = _row_tile(kh, Ns)
    nh = kh // tr

    def body(sel_ref, a_ref, b0_ref, b1_ref, b2_ref, o_ref):
        up = lambda r: r[...].astype(F32)
        o_ref[...] = ((up(a_ref) + up(b0_ref)) + up(b1_ref)) + up(b2_ref)

    recv = lambda k: pl.BlockSpec((None, tr, Ns), lambda i, sel: (k, i, 0))
    return pl.pallas_call(
        body, name=name,
        grid_spec=pltpu.PrefetchScalarGridSpec(
            num_scalar_prefetch=1, grid=(nh,),
            in_specs=[pl.BlockSpec((None, tr, Ns), lambda i, sel: (sel[0], i, 0)), recv(0), recv(1), recv(2)],
            out_specs=pl.BlockSpec((tr, Ns), lambda i, sel: (sel[1] * nh + i, 0))),
        out_shape=jax.ShapeDtypeStruct((2 * kh, Ns), F32), compiler_params=_params(("parallel",)),
    )(sel, pair_sum, from_chips, from_chips, from_chips)


def _gather_small(v, name):
    R, L = v.shape

    def body(v_ref, out_ref, send_sems, recv_sems):
        x, y, c = _place()
        me = 4 * x + 2 * y + c
        out_ref[me] = v_ref[...]
        peers = [(_flip(x, k >> 2 & 1), _flip(y, k >> 1 & 1), _flip(c, k & 1)) for k in range(1, N_DEV)]

        def copy(k, row, to):
            return pltpu.make_async_remote_copy(src_ref=v_ref, dst_ref=out_ref.at[row], send_sem=send_sems.at[k],
                                                recv_sem=recv_sems.at[k], device_id=to, device_id_type=MESH)

        sends = [copy(k, me, peer) for k, peer in enumerate(peers)]
        for cp in sends:
            cp.start()
        for k, (px, py, pc) in enumerate(peers):
            copy(k, 4 * px + 2 * py + pc, (x, y, c)).wait_recv()
        for cp in sends:
            cp.wait_send()

    vmem = pl.BlockSpec(memory_space=pltpu.VMEM)
    return pl.pallas_call(
        body, name=name, in_specs=[vmem], out_specs=vmem, out_shape=jax.ShapeDtypeStruct((N_DEV, R, L), F32),
        scratch_shapes=[pltpu.SemaphoreType.DMA((N_DEV - 1,)), pltpu.SemaphoreType.DMA((N_DEV - 1,))],
    )(v)


def _silu(v):
    return v * jax.nn.sigmoid(v)


def _ada_fwd(c_all, w_ada, tn=512):
    M, D = c_all.shape
    Ns = w_ada.shape[1]

    def body(c_ref, w_ref, o_ref):
        o_ref[...] = _dot(_silu(c_ref[...]).astype(BF16), w_ref[...].astype(BF16))

    return pl.pallas_call(
        body, name="ada_fwd", grid=(Ns // tn,),
        in_specs=[pl.BlockSpec((M, D), lambda j: (0, 0)), pl.BlockSpec((D, tn), lambda j: (0, j))],
        out_specs=pl.BlockSpec((M, tn), lambda j: (0, j)), out_shape=jax.ShapeDtypeStruct((M, Ns), F32),
        compiler_params=_params(("parallel",)),
    )(c_all, w_ada)


def _ada_bwd(c_all, dmod, w, m, v, tk=256, tn=1536):
    M, D = c_all.shape
    Ns = dmod.shape[1]

    def body(c_ref, d_ref, w_ref, m_ref, v_ref, g_ref, dl_ref, mo_ref, vo_ref):
        g = _dot(_silu(c_ref[...]).astype(BF16), d_ref[...].astype(BF16), TN)
        g_ref[...] = g
        dl_ref[...], mo_ref[...], vo_ref[...] = _adamw_math(g, w_ref[...], m_ref[...], v_ref[...])

    blk = pl.BlockSpec((tk, tn), lambda i, j: (i, j))
    return pl.pallas_call(
        body, name="ada_bwd", grid=(D // tk, Ns // tn),
        in_specs=[pl.BlockSpec((M, tk), lambda i, j: (0, i)), pl.BlockSpec((M, tn), lambda i, j: (0, j)), blk, blk, blk],
        out_specs=[blk] * 4, out_shape=[jax.ShapeDtypeStruct((D, Ns), F32)] * 4,
        compiler_params=_params(("parallel", "parallel"), 40),
    )(c_all, dmod, w, m, v)


def _small_update(g_all, w, m, v):
    R, L = w.shape

    def body(g_ref, w_ref, m_ref, v_ref, go_ref, d_ref, mo_ref, vo_ref):
        g = g_ref[0]
        for d in range(1, N_DEV):
            g = g + g_ref[d]
        go_ref[...] = g
        d_ref[...], mo_ref[...], vo_ref[...] = _adamw_math(g, w_ref[...], m_ref[...], v_ref[...])

    return pl.pallas_call(body, name="small_update", out_shape=[jax.ShapeDtypeStruct((R, L), F32)] * 4)(g_all, w, m, v)


def _pack(parts, rows):
    flat = jnp.concatenate([p.reshape(-1) for p in parts])
    return jnp.pad(flat, (0, rows * 128 - flat.shape[0])).reshape(rows, 128)


def _unpack(packed, shapes):
    flat, out, at = packed.reshape(-1), [], 0
    for shp in shapes:
        size = 1
        for d in shp:
            size *= d
        out.append(flat[at:at + size].reshape(shp))
        at += size
    return out


def _layer(x, tgt, mod, wts, rel_bias, attn_norm_g, lb_logits, gnorm_g, ln1_g, ln1_b, ln2_g, ln2_b, place=None):
    T, D = x.shape
    aw = attn_norm_g.shape[1]
    shift1, scale1, gate1, shift2, scale2, gate2 = [mod[i:i + 1] for i in range(6)]

    def gather(n, rows=None, into=None, before=None, last=True):
        return None if place is None else _gather_rider(wts[n], rows, None if into is None else into[0], before, last)

    def gathered(n, rode):
        return wts[n] if place is None else lax.dynamic_update_index_in_dim(rode[0], wts[n], place[0], 0)

    def blocks(g):
        return g.reshape(N_CHIPS, -1, g.shape[2])

    def to_sibling(g):
        return None if place is None else _pair_rider(g)

    def pair_sum(n, g, rode=None):
        if place is None:
            return g
        rode = _alone(_pair_rider(g), n + "_send_pair") if rode is None else rode
        return _sum_pair(g, rode[0], place[1], n + "_sum_pair")

    def to_chips(p, rows=None, into=None):
        return None if place is None else _chips_rider(p, rows, None if into is None else into[0])

    def summed(n, p, rode):
        return p if place is None else _sum_chips(p, rode[0], place[1], n + "_sum_chips")

    def to_both(block):
        return None if place is None else _share_rider(block)

    def carrying(mm, *args, rider, **kw):
        return mm(*args, rider=rider, **kw) if rider is not None else (mm(*args, **kw), None)

    def to_sibling_acts(a, b):
        return None if place is None else _acts_rider(a, b)

    def pair_grad(name, a, b, tn, rider, arrived=None, late_rider=None):
        if place is None:
            return _mm_tn(a, b, q=N_CHIPS, tk=512, tn=tn, tt=T, name=name), None
        kh = a.shape[1] // 2
        mine = lax.dynamic_slice_in_dim(a, place[1][1] * kh, kh, axis=1)
        part, rode = carrying(_mm_tn, mine, b, q=N_CHIPS, tk=512, tn=tn, tt=T, name=name + "_own",
                              rider=_join(None if arrived else _acts_rider(a, b), rider))
        (a_sib, b_sib), rode = arrived or rode[:2], rode if arrived else rode[2:]
        out, late = _mm_tn_add(a_sib, b_sib, part, tk=512, tn=tn, name=name + "_sib", rider=late_rider)
        return out, (rode or []) + late

    w_in = gathered("w_in", None if place is None else [wts["w_in_gathered"]])
    h1 = _pre_mixer(x, scale1, shift1)
    n_qkv = 3 * aw // 256
    kh_o, kh_f, kh_out = [wts[n].shape[-2] // 2 for n in ("w_o", "w_ffn_in", "w_ffn_out")]
    o_cut, f_cuts, out_cut = 3 * kh_o // 8, (7 * kh_f // 16, 7 * kh_f // 8), kh_out // 11
    qkv, rode = carrying(_mm_nn, h1, w_in, tm=ZPAD, tn=256, tk=D, name="proj_qkv", cols=(0, n_qkv), o_dtype=BF16,
                         pad_rows=ZPAD, rider=gather("w_o", (0, o_cut), last=False))
    proj, rode = carrying(_mm_nn, h1, w_in, tm=2048, tn=256, tk=D, name="proj_rec",
                          cols=(n_qkv, N_CHIPS * w_in.shape[2] // 256),
                          rider=gather("w_o", (o_cut, kh_o - o_cut), rode, before=(0, o_cut)))
    w_o3 = gathered("w_o", rode).reshape(1, D, D)
    bias = _bias_band(rel_bias)
    (mix_a, probs), rode = _attn_fwd(qkv, bias, attn_norm_g, rider=gather("w_ffn_in", (0, f_cuts[0]), last=False))
    (mix_b, o_b, st_all), rode = _hgrn_fwd(
        proj, lb_logits, gnorm_g,
        rider=gather("w_ffn_in", (f_cuts[0], f_cuts[1] - f_cuts[0]), rode, before=(0, f_cuts[0]), last=False))
    mixin = jnp.concatenate([mix_a, mix_b], axis=1)
    mix = _mm_nn(mixin, w_o3, tm=1024, tn=512, tk=D, name="mix_out")
    if place is None:
        x1, h2 = _post_mixer(mix, x, gate1, ln1_g, ln1_b, scale2, shift2)
    else:
        (x1, h2), rode = _post_mixer(mix, x, gate1, ln1_g, ln1_b, scale2, shift2, rider=_join(
            gather("w_ffn_in", (f_cuts[1], kh_f - f_cuts[1]), rode, before=(f_cuts[0], f_cuts[1] - f_cuts[0])),
            gather("w_ffn_out", (0, out_cut), last=False)))
    w_ffn_in = gathered("w_ffn_in", rode)
    (gate, up, act), rode = _ffn_in_swiglu(
        h2, w_ffn_in, tm=2048, tn=256,
        rider=gather("w_ffn_out", (out_cut, kh_out - out_cut), rode and rode[1:], before=(0, out_cut)))
    w_out3 = gathered("w_ffn_out", rode)
    w_out3 = w_out3.reshape(1, -1, w_out3.shape[2])
    d_ff = w_out3.shape[1]
    f = _mm_nn(act, w_out3, tm=1024, tn=512, tk=d_ff, name="ffn_out")
    du2, df, acc2 = _loss_head(f, x1, tgt, gate2, ln2_g, ln2_b)
    loss = (0.5 / D) * jnp.sum(acc2[3])
    g = blocks(_mm_tn(act, df, q=1, tk=512, tn=1024, tt=T, name="g_ffn_out"))
    (dff,), rode = _d_act_swiglu(df, w_out3, gate, up, tm=1024, to=512, rider=to_sibling(g))
    p_out = pair_sum("w_ffn_out", g, rode)
    cut = 21 * p_out.shape[1] // 44
    dh2, rode = carrying(_mm_nt, dff, w_ffn_in, tm=1024, to=1024, tn=w_ffn_in.shape[2], name="d_h2",
                         rider=_join(to_chips(p_out, (0, cut)), to_sibling_acts(h2, dff)))
    p_fin, rode = pair_grad("g_ffn_in", h2, dff, w_ffn_in.shape[2] // 2,
                            to_chips(p_out, (cut, p_out.shape[1] - cut), rode), arrived=rode and rode[1:])
    g_ffn_out = summed("w_ffn_out", p_out, rode)
    if place is None:
        du1, dmix, acc1 = _mid_bwd(dh2, du2, x1, mix, x, gate1, ln1_g, scale2)
    else:
        (du1, dmix, acc1), (g_ffn_out,) = _mid_bwd(dh2, du2, x1, mix, x, gate1, ln1_g, scale2, rider=to_both(g_ffn_out))
    g = blocks(_mm_tn(mixin, dmix, q=1, tk=512, tn=1024, tt=T, name="g_o"))
    dmixin, rode = carrying(_mm_nt, dmix, w_o3, tm=1024, to=512, tn=D, name="d_mixin", rider=to_sibling(g))
    p_o = pair_sum("w_o", g, rode)
    cut = p_fin.shape[1] // 2
    (dq, dk, dv, dbias, dgain), rode = _attn_bwd(qkv, probs, attn_norm_g, dmixin, rider=to_chips(p_fin, (0, cut)))
    (dproj, dl0, dgn), rode = _hgrn_bwd(
        proj, lb_logits, gnorm_g, o_b, st_all, dmixin, (dq, dk, dv),
        rider=_join(to_chips(p_fin, (cut, p_fin.shape[1] - cut), rode), to_chips(p_o)))
    g_ffn_in, g_o = summed("w_ffn_in", p_fin, rode[:1]), summed("w_o", p_o, rode[1:])
    p_in, rode = pair_grad("g_in", h1, dproj, w_in.shape[2] // 2, None,
                           late_rider=_join(to_both(g_ffn_in), to_both(g_o)))
    if place is not None:
        g_ffn_in, g_o = rode
    cut = 3 * p_in.shape[1] // 4
    dh1, rode = carrying(_mm_nt, dproj, w_in, tm=1024, to=1024, tn=w_in.shape[2], name="d_h1",
                         rider=to_chips(p_in, (0, cut)))
    if place is None:
        (grad_x, acc0), g_in = _first_bwd(dh1, du1, x, scale1), p_in
    else:
        (grad_x, acc0), rode = _first_bwd(dh1, du1, x, scale1, rider=to_chips(p_in, (cut, p_in.shape[1] - cut), rode))
        g_in, = _alone(to_both(summed("w_in", p_in, rode)), "w_in_share")
    dmod = jnp.concatenate([acc0[1:2], acc0[0:1], acc1[4:5], acc1[1:2], acc1[0:1], acc2[2:3]], axis=0)
    small = dict(rel_bias=_bias_band_grad(dbias), attn_norm_g=dgain,
                 lb_logits=jnp.concatenate([dl0, -dl0], axis=0), gnorm_g=dgn,
                 ln1_g=acc1[2:3], ln1_b=acc1[3:4], ln2_g=acc2[0:1], ln2_b=acc2[1:2])
    return loss, grad_x, dict(w_in=g_in, w_o=g_o, w_ffn_in=g_ffn_in, w_ffn_out=g_ffn_out), dmod, small


SMALL = ("rel_bias", "attn_norm_g", "lb_logits", "gnorm_g", "ln1_g", "ln1_b", "ln2_g", "ln2_b")
SMALL_ROWS = 256


def kernel(x, c, w_ada, b_ada, w_in, rel_bias, attn_norm_g, lb_logits, gnorm_g, w_o, ln1_g, ln1_b, w_ffn_in, w_ffn_out, ln2_g, ln2_b, loss_target, m_w_ada, m_b_ada, m_w_in, m_rel_bias, m_attn_norm_g, m_lb_logits, m_gnorm_g, m_w_o, m_ln1_g, m_ln1_b, m_w_ffn_in, m_w_ffn_out, m_ln2_g, m_ln2_b, v_w_ada, v_b_ada, v_w_in, v_rel_bias, v_attn_norm_g, v_lb_logits, v_gnorm_g, v_w_o, v_ln1_g, v_ln1_b, v_w_ffn_in, v_w_ffn_out, v_ln2_g, v_ln2_b):
    mx, my, mc = _place()
    me = 4 * mx + 2 * my + mc
    chip = 2 * mx + my
    sel = jnp.stack([chip, mc]).astype(jnp.int32)
    D = x.shape[2]
    ns_ada = w_ada.shape[2]

    big = dict(w_in=(w_in, m_w_in, v_w_in), w_o=(w_o, m_w_o, v_w_o), w_ffn_in=(w_ffn_in, m_w_ffn_in, v_w_ffn_in),
               w_ffn_out=(w_ffn_out, m_w_ffn_out, v_w_ffn_out))
    shards = dict(w_in=w_in[0].astype(BF16))
    kh, rode, rows = shards["w_in"].shape[0] // 2, None, None
    for n, part in (("w_ffn_in", 19), ("w_ffn_out", 9), ("w_o", 4)):
        before, rows = rows, (rows[0] + rows[1] if rows else 0, part * kh // 32)
        (shards[n],), rode = _to_bf16(big[n][0][0], "cast_" + n,
                                      _gather_rider(shards["w_in"], rows, rode and rode[0], before, last=n == "w_o"))
    shards["w_in_gathered"] = rode[0]

    c_all = _gather_small(c.reshape(D // 128, 128), "gather_c").reshape(N_DEV, D)
    c_all = jnp.pad(c_all, ((0, 16 - N_DEV), (0, 0)))
    mod_cols = _ada_fwd(c_all, w_ada[0])[:N_DEV]
    mod_all = _gather_small(mod_cols.reshape(-1, 128), "gather_mod").reshape(N_DEV, N_DEV, ns_ada)
    mod = lax.dynamic_index_in_dim(mod_all[::2], me, axis=1, keepdims=False)
    mod = (mod.reshape(1, -1) + b_ada).reshape(6, D)

    loss, grad_x, g_big, dmod, g_small = _layer(
        x[0], loss_target[0], mod, shards, rel_bias[0], attn_norm_g, lb_logits, gnorm_g, ln1_g, ln1_b, ln2_g, ln2_b,
        place=(chip, sel))

    grads, deltas, new_m, new_v = {}, {}, {}, {}
    for n, (w, m, v) in big.items():
        g, d, mo, vo = _adamw(g_big[n], w[0], m[0], v[0], "adamw_" + n)
        grads[n], deltas[n], new_m[n], new_v[n] = g[None], d[None], mo[None], vo[None]

    small_in = dict(rel_bias=(rel_bias, m_rel_bias, v_rel_bias), attn_norm_g=(attn_norm_g, m_attn_norm_g, v_attn_norm_g),
                    lb_logits=(lb_logits, m_lb_logits, v_lb_logits), gnorm_g=(gnorm_g, m_gnorm_g, v_gnorm_g),
                    ln1_g=(ln1_g, m_ln1_g, v_ln1_g), ln1_b=(ln1_b, m_ln1_b, v_ln1_b), ln2_g=(ln2_g, m_ln2_g, v_ln2_g),
                    ln2_b=(ln2_b, m_ln2_b, v_ln2_b))
    g_all = _gather_small(_pack([dmod] + [g_small[n] for n in SMALL] + [loss], SMALL_ROWS), "gather_small")
    packed = [_pack([t] + [small_in[n][i] for n in SMALL] + [jnp.zeros((), F32)], SMALL_ROWS)
              for i, t in enumerate((b_ada, m_b_ada, v_b_ada))]
    shapes = [b_ada.shape] + [small_in[n][0].shape for n in SMALL] + [()]
    outs = [_unpack(o, shapes) for o in _small_update(g_all, *packed)]
    loss = outs[0][-1]
    for i, n in enumerate(("b_ada",) + SMALL):
        grads[n], deltas[n], new_m[n], new_v[n] = outs[0][i], outs[1][i], outs[2][i], outs[3][i]

    dmod_all = g_all[:, :6 * D // 128].reshape(N_DEV, 6 * D)
    dmod_cols = lax.dynamic_slice_in_dim(dmod_all, chip * ns_ada, ns_ada, axis=1)
    dmod_cols = jnp.pad(dmod_cols, ((0, 16 - N_DEV), (0, 0)))
    g, d, mo, vo = _ada_bwd(c_all, dmod_cols, w_ada[0], m_w_ada[0], v_w_ada[0])
    grads["w_ada"], deltas["w_ada"], new_m["w_ada"], new_v["w_ada"] = g[None], d[None], mo[None], vo[None]

    order = ("w_ada", "b_ada", "w_in", "rel_bias", "attn_norm_g", "lb_logits", "gnorm_g", "w_o", "ln1_g", "ln1_b",
             "w_ffn_in", "w_ffn_out", "ln2_g", "ln2_b")
    return (loss, grad_x[None], *[grads[n] for n in order], *[deltas[n] for n in order],
            *[new_m[n] for n in order], *[new_v[n] for n in order])
```

```python
import numpy as np
import jax
import jax.numpy as jnp
from jax import lax
from jax.experimental import pallas as pl
from jax.experimental.pallas import tpu as pltpu

F32 = jnp.float32
BF16 = jnp.bfloat16
MESH = pl.DeviceIdType.MESH
HIGHEST = lax.Precision.HIGHEST

CHUNK = 64
N_PAST = 8
QG = 4
QROWS = QG * CHUNK
KPAD = N_PAST * CHUNK
ZPAD = 2 * KPAD
UNION = (QG + N_PAST) * CHUNK
BAND = (N_PAST + 1) * CHUNK
HD_A = 64
HD_B = 128
SUB = 16
HGRN_HEADS = 8
MAX_REL = 256
EPS = 1e-5
ALPHA = 2.0 ** 0.25
LR, B1, B2, ADAM_EPS, WD, STEP = 1e-3, 0.9, 0.999, 1e-8, 0.01, 10
N_CHIPS = 4
N_DEV = 8
NEG = -1e30
TILE_BYTES = 3 << 19

NN = ((1,), (0,))
NT = ((1,), (1,))
TN = ((0,), (0,))


def _dot(a, b, dims=NN, precision=None):
    return lax.dot_general(a, b, (dims, ((), ())), preferred_element_type=F32, precision=precision)


def _params(sem=None, vmem_mb=None, **kw):
    return pltpu.CompilerParams(dimension_semantics=sem,
                                vmem_limit_bytes=None if vmem_mb is None else vmem_mb << 20, **kw)


def _row_tile(rows, cols):
    for cand in (512, 256, 128, 64, 32, 16, 8):
        if rows % cand == 0 and cand * cols * 4 <= TILE_BYTES:
            return cand
    raise ValueError((rows, cols))


def _place():
    return lax.axis_index("x"), lax.axis_index("y"), lax.axis_index("c")


def _flip(v, bit):
    return 1 - v if bit else v


ANY = pl.BlockSpec(memory_space=pl.ANY)
CHIP_FLIPS = ((1, 0), (0, 1), (1, 1))


class _Rider:
    def __init__(self, operands, out_shape, n_sems, start, finish, aliases=None):
        self.operands, self.out_shape, self.n_sems, self.start, self.finish = operands, out_shape, n_sems, start, finish
        self.aliases = aliases or {}


def _call(body, rider, *, name, grid, in_specs, out_specs, out_shape, scratch_shapes, compiler_params, operands):
    if rider is None:
        outs = pl.pallas_call(body, name=name, grid=grid, in_specs=in_specs, out_specs=out_specs, out_shape=out_shape,
                              scratch_shapes=scratch_shapes, compiler_params=compiler_params)(*operands)
        return list(outs), []
    n_in, n_out, n_sc = len(in_specs), len(out_specs), len(scratch_shapes)
    r_in, r_out = len(rider.operands), len(rider.out_shape)

    def carried(*refs):
        refs = list(refs)
        cuts = [n_in, r_in, n_out, r_out, n_sc]
        ins, r_ins, outs, r_outs, scratch = [[refs.pop(0) for _ in range(n)] for n in cuts]
        first, last = None, None
        for axis, size in enumerate(grid):
            i = pl.program_id(axis)
            first = (i == 0) if first is None else first & (i == 0)
            last = (i == size - 1) if last is None else last & (i == size - 1)

        @pl.when(first)
        def _():
            rider.start(r_ins, r_outs, *refs)

        body(*ins, *outs, *scratch)

        @pl.when(last)
        def _():
            rider.finish(r_ins, r_outs, *refs)

    sems = [pltpu.SemaphoreType.DMA((rider.n_sems,)), pltpu.SemaphoreType.DMA((rider.n_sems,))]
    outs = pl.pallas_call(carried, name=name, grid=grid, in_specs=list(in_specs) + [ANY] * r_in,
                          out_specs=list(out_specs) + [ANY] * r_out, out_shape=list(out_shape) + rider.out_shape,
                          scratch_shapes=list(scratch_shapes) + sems, compiler_params=compiler_params,
                          input_output_aliases={n_in + i: n_out + o for i, o in rider.aliases.items()},
                          )(*operands, *rider.operands)
    return list(outs[:n_out]), list(outs[n_out:])


def _alone(rider, name):
    def body(*refs):
        ins, outs, sems = refs[:len(rider.operands)], refs[len(rider.operands):-2], refs[-2:]
        rider.start(ins, outs, *sems)
        rider.finish(ins, outs, *sems)

    return pl.pallas_call(
        body, name=name, in_specs=[ANY] * len(rider.operands), out_specs=[ANY] * len(rider.out_shape),
        out_shape=rider.out_shape, input_output_aliases=rider.aliases,
        scratch_shapes=[pltpu.SemaphoreType.DMA((rider.n_sems,)), pltpu.SemaphoreType.DMA((rider.n_sems,))],
    )(*rider.operands)


class _Sems:
    def __init__(self, sems, base):
        self.sems, self.base = sems, base

    @property
    def at(self):
        return self

    def __getitem__(self, k):
        return self.sems.at[self.base + k]


def _join(*riders):
    riders = [r for r in riders if r is not None]
    if len(riders) < 2:
        return riders[0] if riders else None

    def parts(ins, outs, send_sems, recv_sems):
        i = o = s = 0
        for r in riders:
            ni, no = len(r.operands), len(r.out_shape)
            yield r, ins[i:i + ni], outs[o:o + no], _Sems(send_sems, s), _Sems(recv_sems, s)
            i, o, s = i + ni, o + no, s + r.n_sems

    def start(*refs):
        for r, *args in parts(*refs):
            r.start(*args)

    def finish(*refs):
        for r, *args in parts(*refs):
            r.finish(*args)

    aliases, i, o = {}, 0, 0
    for r in riders:
        aliases.update({i + a: o + b for a, b in r.aliases.items()})
        i, o = i + len(r.operands), o + len(r.out_shape)
    return _Rider([a for r in riders for a in r.operands], [s for r in riders for s in r.out_shape],
                  sum(r.n_sems for r in riders), start, finish, aliases)


def _gather_rider(shard, rows=None, into=None, before=None, last=True):
    K, Ns = shard.shape
    kh = K // 2
    rows = rows or (0, kh)

    def copies(w_ref, out_ref, send_sems, recv_sems):
        x, y, c = _place()
        chips = [(_flip(x, fx), _flip(y, fy)) for fx, fy in CHIP_FLIPS]

        def half(chip, which, part):
            return out_ref.at[2 * chip[0] + chip[1], pl.ds(which * kh + part[0], part[1]), :]

        def copy(k, dst, to, src=None):
            return pltpu.make_async_remote_copy(src_ref=dst if src is None else src, dst_ref=dst,
                                                send_sem=send_sems.at[k], recv_sem=recv_sems.at[k],
                                                device_id=to, device_id_type=MESH)

        def first():
            return [copy(j, half((x, y), c, rows), (*chip, c), src=w_ref.at[pl.ds(c * kh + rows[0], rows[1]), :])
                    for j, chip in enumerate(chips)]

        def onward(base, part):
            return [copy(base + j, half(chip, c, part), (x, y, 1 - c)) for j, chip in enumerate(chips)]

        def arriving(base, which, part):
            return [copy(base + j, half(chip, which, part), (x, y, c)) for j, chip in enumerate(chips)]

        return c, first, onward, arriving

    def start(ins, outs, send_sems, recv_sems):
        _, first, onward, _ = copies(ins[0], outs[0], send_sems, recv_sems)
        for cp in first() + (onward(3, before) if before else []):
            cp.start()

    def finish(ins, outs, send_sems, recv_sems):
        c, first, onward, arriving = copies(ins[0], outs[0], send_sems, recv_sems)
        sent = first() + (onward(3, before) if before else [])
        passed = onward(6, rows) if last else [None] * 3
        for arrived, cp in zip(arriving(0, c, rows), passed):
            arrived.wait_recv()
            if last:
                cp.start()
        for arrived in (arriving(3, 1 - c, before) if before else []) + (arriving(6, 1 - c, rows) if last else []):
            arrived.wait_recv()
        for cp in sent + (passed if last else []):
            cp.wait_send()

    full = jax.ShapeDtypeStruct((N_CHIPS, K, Ns), shard.dtype)
    if into is None:
        return _Rider([shard], [full], 9, start, finish)
    return _Rider([shard, into], [full], 9, start, finish, aliases={1: 0})


def _pair_rider(g_full):
    Q, K, Ns = g_full.shape
    kh = K // 2

    def copy(g_ref, got_ref, send_sems, recv_sems):
        x, y, c = _place()
        return pltpu.make_async_remote_copy(src_ref=g_ref.at[:, pl.ds((1 - c) * kh, kh), :], dst_ref=got_ref,
                                            send_sem=send_sems.at[0], recv_sem=recv_sems.at[0],
                                            device_id=(x, y, 1 - c), device_id_type=MESH)

    def start(ins, outs, send_sems, recv_sems):
        copy(ins[0], outs[0], send_sems, recv_sems).start()

    def finish(ins, outs, send_sems, recv_sems):
        copy(ins[0], outs[0], send_sems, recv_sems).wait()

    return _Rider([g_full], [jax.ShapeDtypeStruct((Q, kh, Ns), g_full.dtype)], 1, start, finish)


def _acts_rider(a, b):
    T, K = a.shape
    kh = K // 2

    def copies(ins, outs, send_sems, recv_sems):
        x, y, c = _place()
        pair = [(ins[0].at[:, pl.ds((1 - c) * kh, kh)], outs[0]), (ins[1], outs[1])]
        return [pltpu.make_async_remote_copy(src_ref=src, dst_ref=dst, send_sem=send_sems.at[k], recv_sem=recv_sems.at[k],
                                             device_id=(x, y, 1 - c), device_id_type=MESH)
                for k, (src, dst) in enumerate(pair)]

    def start(*refs):
        for cp in copies(*refs):
            cp.start()

    def finish(*refs):
        for cp in copies(*refs):
            cp.wait()

    return _Rider([a, b], [jax.ShapeDtypeStruct((T, kh), a.dtype), jax.ShapeDtypeStruct(b.shape, b.dtype)], 2,
                  start, finish)


def _share_rider(block):
    K, Ns = block.shape
    kh = K // 2

    def halves(out_ref):
        x, y, c = _place()
        return out_ref.at[pl.ds(c * kh, kh), :], out_ref.at[pl.ds((1 - c) * kh, kh), :], (x, y, 1 - c)

    def start(ins, outs, send_sems, recv_sems):
        mine, _, sibling = halves(outs[0])
        pltpu.make_async_remote_copy(src_ref=mine, dst_ref=mine, send_sem=send_sems.at[0], recv_sem=recv_sems.at[0],
                                     device_id=sibling, device_id_type=MESH).start()

    def finish(ins, outs, send_sems, recv_sems):
        mine, theirs, sibling = halves(outs[0])
        pltpu.make_async_remote_copy(src_ref=theirs, dst_ref=theirs, send_sem=send_sems.at[0], recv_sem=recv_sems.at[0],
                                     device_id=sibling, device_id_type=MESH).wait_recv()
        pltpu.make_async_remote_copy(src_ref=mine, dst_ref=mine, send_sem=send_sems.at[0], recv_sem=recv_sems.at[0],
                                     device_id=sibling, device_id_type=MESH).wait_send()

    return _Rider([block], [jax.ShapeDtypeStruct((K, Ns), block.dtype)], 1, start, finish, aliases={0: 0})


def _chips_rider(pair_sum, rows=None, into=None):
    Q, kh, Ns = pair_sum.shape
    first_row, n_rows = rows or (0, kh)

    def copies(p_ref, got_ref, send_sems, recv_sems):
        x, y, c = _place()
        part = pl.ds(first_row, n_rows)
        out = []
        for j, (fx, fy) in enumerate(CHIP_FLIPS):
            px, py = _flip(x, fx), _flip(y, fy)
            out.append(pltpu.make_async_remote_copy(
                src_ref=p_ref.at[2 * px + py, part, :], dst_ref=got_ref.at[j, part, :], send_sem=send_sems.at[j],
                recv_sem=recv_sems.at[j], device_id=(px, py, c), device_id_type=MESH))
        return out

    def start(ins, outs, send_sems, recv_sems):
        for cp in copies(ins[0], outs[0], send_sems, recv_sems):
            cp.start()

    def finish(ins, outs, send_sems, recv_sems):
        sends = copies(ins[0], outs[0], send_sems, recv_sems)
        for cp in sends:
            cp.wait_recv()
        for cp in sends:
            cp.wait_send()

    got = jax.ShapeDtypeStruct((Q - 1, kh, Ns), pair_sum.dtype)
    if into is None:
        return _Rider([pair_sum], [got], 3, start, finish)
    return _Rider([pair_sum, into], [got], 3, start, finish, aliases={1: 0})


def _mm(a, b, *, grid, a_spec, b_spec, o_spec, o_shape, o_dtype, dims, acc_shape, name, rider=None, zero_rows=0,
        vmem_mb=48):
    nk = grid[2]

    def body(a_ref, b_ref, o_ref, *scratch):
        if zero_rows:
            @pl.when(pl.program_id(0) < zero_rows)
            def _():
                o_ref[...] = jnp.zeros_like(o_ref)

            @pl.when(pl.program_id(0) >= zero_rows)
            def _():
                o_ref[...] = _dot(a_ref[...], b_ref[...], dims).astype(o_ref.dtype)
            return
        part = _dot(a_ref[...], b_ref[...], dims)
        if nk == 1:
            o_ref[...] = part.astype(o_ref.dtype)
            return
        acc_ref, = scratch
        k = pl.program_id(2)

        @pl.when(k == 0)
        def _():
            acc_ref[...] = part

        @pl.when(k > 0)
        def _():
            acc_ref[...] += part

        @pl.when(k == nk - 1)
        def _():
            o_ref[...] = acc_ref[...].astype(o_ref.dtype)

    (out,), rode = _call(
        body, rider, name=name, grid=grid, in_specs=[a_spec, b_spec], out_specs=[o_spec],
        out_shape=[jax.ShapeDtypeStruct(o_shape, o_dtype)],
        scratch_shapes=[] if nk == 1 else [pltpu.VMEM(acc_shape, F32)],
        compiler_params=_params(("parallel", "parallel", "arbitrary") if rider is None else ("arbitrary",) * 3, vmem_mb),
        operands=(a, b))
    return out if rider is None else (out, rode)


def _mm_nn(a, w, *, tm, tn, tk, name, rider=None, cols=None, o_dtype=F32, pad_rows=0):
    T, K = a.shape
    Q, _, Ns = w.shape
    nbs = Ns // tn
    tm = min(tm, T)
    j0, j1 = cols or (0, Q * nbs)
    lead = pad_rows // tm
    return _mm(a, w, grid=(lead + T // tm, j1 - j0, K // tk),
               a_spec=pl.BlockSpec((tm, tk), lambda i, j, k: (jnp.maximum(i - lead, 0), k)),
               b_spec=pl.BlockSpec((None, tk, tn), lambda i, j, k: ((j + j0) // nbs, k, (j + j0) % nbs)),
               o_spec=pl.BlockSpec((tm, tn), lambda i, j, k: (i, j)),
               o_shape=(pad_rows + T, (j1 - j0) * tn), o_dtype=o_dtype, dims=NN, acc_shape=(tm, tn), name=name,
               rider=rider, zero_rows=lead)


def _col_blocks(g, rows, tn, at):
    if g.ndim == 2:
        return pl.BlockSpec((rows, tn), at)
    per = g.shape[2] // tn

    def stacked(*idx):
        r, c = at(*idx)
        return c // per, r, c % per

    return pl.BlockSpec((None, rows, tn), stacked)


def _mm_nt(g, w, *, tm, to, tn, name, rider=None):
    T = g.shape[-2]
    Q, K, Ns = w.shape
    nbs = Ns // tn
    tm = min(tm, T)
    return _mm(g, w, grid=(T // tm, K // to, Q * nbs),
               a_spec=_col_blocks(g, tm, tn, lambda i, j, n: (i, n)),
               b_spec=pl.BlockSpec((None, to, tn), lambda i, j, n: (n // nbs, j, n % nbs)),
               o_spec=pl.BlockSpec((tm, to), lambda i, j, n: (i, j)),
               o_shape=(T, K), o_dtype=F32, dims=NT, acc_shape=(tm, to), name=name, rider=rider)


def _mm_tn(a, g, *, q, tk, tn, tt, name, rider=None):
    T, K = a.shape
    Ns = g.shape[-1] * (g.ndim - 1) // q
    nbs = Ns // tn
    return _mm(a, g, grid=(K // tk, q * nbs, T // tt),
               a_spec=pl.BlockSpec((tt, tk), lambda i, j, t: (t, i)),
               b_spec=_col_blocks(g, tt, tn, lambda i, j, t: (t, j)),
               o_spec=pl.BlockSpec((None, tk, tn), lambda i, j, t: (j // nbs, i, j % nbs)),
               o_shape=(q, K, Ns), o_dtype=BF16, dims=TN, acc_shape=(tk, tn), name=name, rider=rider)


def _mm_tn_add(a, g, part, *, tk, tn, name, rider=None):
    T, K = a.shape
    Q, _, Ns = part.shape
    nbs = Ns // tn

    def body(a_ref, g_ref, p_ref, o_ref):
        o_ref[...] = (_dot(a_ref[...], g_ref[...], TN) + p_ref[...].astype(F32)).astype(o_ref.dtype)

    blk = pl.BlockSpec((None, tk, tn), lambda i, j: (j // nbs, i, j % nbs))
    (out,), rode = _call(
        body, rider, name=name, grid=(K // tk, Q * nbs),
        in_specs=[pl.BlockSpec((T, tk), lambda i, j: (0, i)), _col_blocks(g, T, tn, lambda i, j: (0, j)), blk],
        out_specs=[blk], out_shape=[jax.ShapeDtypeStruct((Q, K, Ns), BF16)], scratch_shapes=[],
        compiler_params=_params(("arbitrary", "arbitrary"), 48), operands=(a, g, part))
    return out, rode


def _ln(u):
    mu = jnp.mean(u, axis=-1, keepdims=True)
    d = u - mu
    r = lax.rsqrt(jnp.mean(d * d, axis=-1, keepdims=True) + EPS)
    return d * r, r


def _ln_bwd(dy, un, r):
    return r * (dy - jnp.mean(dy, axis=-1, keepdims=True) - un * jnp.mean(dy * un, axis=-1, keepdims=True))


def _colsum(v):
    return jnp.sum(v, axis=0, keepdims=True)


def _rowwise(name, fn, bigs, vecs, out_dtypes, n_acc, tm=128, rider=None):
    T, D = bigs[0].shape
    nb, nv, no = len(bigs), len(vecs), len(out_dtypes)

    def body(*refs):
        outs, accs = fn([r[...] for r in refs[:nb]], [r[...] for r in refs[nb:nb + nv]])
        for r, o in zip(refs[nb + nv:nb + nv + no], outs):
            r[...] = o.astype(r.dtype)
        if n_acc:
            acc_ref = refs[nb + nv + no]

            @pl.when(pl.program_id(0) == 0)
            def _():
                acc_ref[...] = jnp.zeros_like(acc_ref)

            for row, a in enumerate(accs):
                acc_ref[row:row + 1, :] += a

    big_spec = pl.BlockSpec((tm, D), lambda i: (i, 0))
    vec_spec = pl.BlockSpec((1, D), lambda i: (0, 0))
    out_shape = [jax.ShapeDtypeStruct((T, D), dt) for dt in out_dtypes]
    out_specs = [big_spec] * no
    if n_acc:
        out_shape.append(jax.ShapeDtypeStruct((8, D), F32))
        out_specs.append(pl.BlockSpec((8, D), lambda i: (0, 0)))
    outs, rode = _call(
        body, rider, name=name, grid=(T // tm,), in_specs=[big_spec] * nb + [vec_spec] * nv,
        out_specs=out_specs, out_shape=out_shape, scratch_shapes=[],
        compiler_params=_params(("arbitrary",), 48), operands=(*bigs, *vecs))
    return outs if rider is None else (outs, rode)


def _to_bf16(w, name, rider=None):
    R, C = w.shape
    tr = _row_tile(R, C)

    def body(w_ref, o_ref):
        o_ref[...] = w_ref[...].astype(o_ref.dtype)

    blk = pl.BlockSpec((tr, C), lambda i: (i, 0))
    return _call(body, rider, name=name, grid=(R // tr,), in_specs=[blk], out_specs=[blk],
                 out_shape=[jax.ShapeDtypeStruct((R, C), BF16)], scratch_shapes=[],
                 compiler_params=_params(("arbitrary",)), operands=(w,))


def _pre_mixer(x, scale1, shift1):
    def fn(b, v):
        xn, _ = _ln(b[0])
        return [xn * (1.0 + v[0]) + v[1]], []
    return _rowwise("pre_mixer", fn, [x], [scale1, shift1], [BF16], 0)[0]


def _post_mixer(mix, x, gate1, g1, b1, scale2, shift2, rider=None):
    def fn(b, v):
        un1, _ = _ln(ALPHA * b[1] + v[0] * b[0])
        x1 = un1 * v[1] + v[2]
        xn1, _ = _ln(x1)
        return [x1, xn1 * (1.0 + v[3]) + v[4]], []
    return _rowwise("post_mixer", fn, [mix, x], [gate1, g1, b1, scale2, shift2], [F32, BF16], 0, rider=rider)


def _loss_head(f, x1, tgt, gate2, g2, b2):
    def fn(b, v):
        ff, xx, tt = b
        d_model = ff.shape[-1]
        un2, r2 = _ln(ALPHA * xx + v[0] * ff)
        err = un2 * v[1] + v[2] - tt
        dy = err * (1.0 / d_model)
        du2 = _ln_bwd(dy * v[1], un2, r2)
        return [du2, du2 * v[0]], [_colsum(dy * un2), _colsum(dy), _colsum(du2 * ff), _colsum(err * err)]
    return _rowwise("loss_head", fn, [f, x1, tgt], [gate2, g2, b2], [F32, BF16], 4)


def _mid_bwd(dh2, du2, x1, mix, x, gate1, g1, scale2, rider=None):
    def fn(b, v):
        dh, du, xx1, mm, xx = b
        xn1, r1n = _ln(xx1)
        dx1 = ALPHA * du + _ln_bwd(dh * (1.0 + v[2]), xn1, r1n)
        un1, r1 = _ln(ALPHA * xx + v[0] * mm)
        du1 = _ln_bwd(dx1 * v[1], un1, r1)
        return [du1, du1 * v[0]], [_colsum(dh * xn1), _colsum(dh), _colsum(dx1 * un1), _colsum(dx1),
                                   _colsum(du1 * mm)]
    return _rowwise("mid_bwd", fn, [dh2, du2, x1, mix, x], [gate1, g1, scale2], [F32, BF16], 5, rider=rider)


def _first_bwd(dh1, du1, x, scale1, rider=None):
    def fn(b, v):
        dh, du, xx = b
        xn, r0 = _ln(xx)
        return [ALPHA * du + _ln_bwd(dh * (1.0 + v[0]), xn, r0)], [_colsum(dh * xn), _colsum(dh)]
    return _rowwise("first_bwd", fn, [dh1, du1, x], [scale1], [F32], 2, rider=rider)


def _ffn_in_swiglu(h2, w, *, tm, tn, rider=None):
    T, K = h2.shape
    Q, _, Ns = w.shape
    nbs = Ns // tn
    half = Q * nbs // 2
    tm = min(tm, T)

    def body(a_ref, wg_ref, wu_ref, g_ref, u_ref, act_ref):
        a = a_ref[...]
        g, u = _dot(a, wg_ref[...]), _dot(a, wu_ref[...])
        g_ref[...] = g.astype(g_ref.dtype)
        u_ref[...] = u.astype(u_ref.dtype)
        act_ref[...] = (g * jax.nn.sigmoid(g) * u).astype(act_ref.dtype)

    cols = lambda first: pl.BlockSpec((None, K, tn), lambda i, j: ((j + first) // nbs, 0, (j + first) % nbs))
    blk = pl.BlockSpec((tm, tn), lambda i, j: (i, j))
    return _call(
        body, rider, name="ffn_in", grid=(T // tm, half),
        in_specs=[pl.BlockSpec((tm, K), lambda i, j: (i, 0)), cols(0), cols(half)], out_specs=[blk] * 3,
        out_shape=[jax.ShapeDtypeStruct((T, half * tn), BF16)] * 3, scratch_shapes=[],
        compiler_params=_params(("arbitrary", "arbitrary"), 48), operands=(h2, w, w))


def _d_act_swiglu(df, w, gate, up, *, tm, to, rider=None):
    T, N = df.shape
    F = w.shape[1]
    tm = min(tm, T)

    def body(df_ref, w_ref, g_ref, u_ref, o_ref):
        d = _dot(df_ref[...], w_ref[...], NT)
        g = g_ref[...].astype(F32)
        s = jax.nn.sigmoid(g)
        o_ref[0] = (d * u_ref[...].astype(F32) * s * (1.0 + g * (1.0 - s))).astype(o_ref.dtype)
        o_ref[1] = (d * g * s).astype(o_ref.dtype)

    blk = pl.BlockSpec((tm, to), lambda i, j: (i, j))
    return _call(
        body, rider, name="d_act", grid=(T // tm, F // to),
        in_specs=[pl.BlockSpec((tm, N), lambda i, j: (i, 0)), pl.BlockSpec((None, to, N), lambda i, j: (0, j, 0)), blk, blk],
        out_specs=[pl.BlockSpec((2, tm, to), lambda i, j: (0, i, j))],
        out_shape=[jax.ShapeDtypeStruct((2, T, F), BF16)], scratch_shapes=[],
        compiler_params=_params(("arbitrary", "arbitrary"), 48), operands=(df, w, gate, up))


PAIR = 2


def _fill_table(table_ref, band_ref):
    table_ref[...] = jnp.full(table_ref.shape, NEG, F32)
    for e in range(PAIR):
        for g in range(QG):
            table_ref[e, g * CHUNK:(g + 1) * CHUNK, g * CHUNK:g * CHUNK + BAND] = band_ref[e]


def _attn_probs(q_ref, k_ref, bias_ref, e, step):
    start = pl.multiple_of(step * QROWS, QROWS)
    lanes = pl.ds(e * HD_A, HD_A)
    s = _dot(q_ref[:, lanes], k_ref[pl.ds(start + ZPAD - KPAD, UNION), lanes], NT) * (HD_A ** -0.5) + bias_ref[e]
    col = lax.broadcasted_iota(jnp.int32, s.shape, 1)
    s = jnp.where(col + start >= KPAD, s, NEG)
    p = jnp.exp(s - jnp.max(s, axis=-1, keepdims=True))
    return p / jnp.sum(p, axis=-1, keepdims=True), start


def _attn_specs(T, n_pairs):
    wide = PAIR * HD_A
    per_step = pl.BlockSpec((QROWS, wide), lambda hp, n: (n, hp))
    queries = pl.BlockSpec((QROWS, wide), lambda hp, n: (n + ZPAD // QROWS, hp))
    keys = pl.BlockSpec((ZPAD + T, wide), lambda hp, n: (0, n_pairs + hp))
    values = pl.BlockSpec((ZPAD + T, wide), lambda hp, n: (0, 2 * n_pairs + hp))
    grads = pl.BlockSpec((T, wide), lambda hp, n: (0, hp))
    table = pl.BlockSpec((PAIR, CHUNK, BAND), lambda hp, n: (hp, 0, 0))
    vec = pl.BlockSpec((1, wide), lambda hp, n: (0, hp))
    return per_step, queries, keys, values, grads, table, vec


def _probs_spec():
    return pl.BlockSpec((PAIR, QROWS, UNION), lambda hp, n: (hp, n, 0))


def _attn_fwd(qkv, bias, gain, rider=None):
    T = qkv.shape[0] - ZPAD
    W = gain.shape[1]
    n_pairs = W // (PAIR * HD_A)

    def body(q_ref, k_ref, v_ref, band_ref, gain_ref, o_ref, p_ref, table_ref):
        @pl.when(pl.program_id(1) == 0)
        def _():
            _fill_table(table_ref, band_ref)

        for e in range(PAIR):
            lanes = pl.ds(e * HD_A, HD_A)
            p, start = _attn_probs(q_ref, k_ref, table_ref, e, pl.program_id(1))
            p_ref[e] = p.astype(p_ref.dtype)
            o = _dot(p_ref[e], v_ref[pl.ds(start + ZPAD - KPAD, UNION), lanes])
            rr = lax.rsqrt(jnp.mean(o * o, axis=-1, keepdims=True) + EPS)
            o_ref[:, lanes] = (o * rr * gain_ref[:, lanes]).astype(o_ref.dtype)

    per_step, queries, keys, values, _, table, vec = _attn_specs(T, n_pairs)
    return _call(
        body, rider, name="attn_fwd", grid=(n_pairs, T // QROWS), in_specs=[queries, keys, values, table, vec],
        out_specs=[per_step, _probs_spec()],
        out_shape=[jax.ShapeDtypeStruct((T, W), BF16), jax.ShapeDtypeStruct((n_pairs * PAIR, T, UNION), BF16)],
        scratch_shapes=[pltpu.VMEM((PAIR, QROWS, UNION), F32)],
        compiler_params=_params(("arbitrary", "arbitrary"), 40), operands=(qkv, qkv, qkv, bias, gain))


def _attn_bwd(qkv, probs, gain, dmixin, rider=None):
    T = qkv.shape[0] - ZPAD
    W = gain.shape[1]
    n_pairs = W // (PAIR * HD_A)
    scale = HD_A ** -0.5

    def body(q_ref, k_ref, v_ref, p_ref, gain_ref, don_ref, dq_ref, dkb_ref, dvb_ref, dband_ref, dgain_ref,
             dtable_ref, dk_ref, dv_ref):
        n = pl.program_id(1)

        @pl.when(n == 0)
        def _():
            dk_ref[...] = jnp.zeros_like(dk_ref)
            dv_ref[...] = jnp.zeros_like(dv_ref)
            dtable_ref[...] = jnp.zeros_like(dtable_ref)
            dgain_ref[...] = jnp.zeros_like(dgain_ref)

        for e in range(PAIR):
            lanes = pl.ds(e * HD_A, HD_A)
            start = pl.multiple_of(n * QROWS, QROWS)
            keys, in_qkv = pl.ds(start, UNION), pl.ds(start + ZPAD - KPAD, UNION)
            pb = p_ref[e]
            p = pb.astype(F32)
            vb = v_ref[in_qkv, lanes]
            o = _dot(pb, vb)
            rr = lax.rsqrt(jnp.mean(o * o, axis=-1, keepdims=True) + EPS)
            on = o * rr
            d_on = don_ref[:, lanes]
            dgain_ref[:, lanes] += _colsum(d_on * on)
            dyo = d_on * gain_ref[:, lanes]
            do = rr * (dyo - on * jnp.mean(dyo * on, axis=-1, keepdims=True))
            dob = do.astype(BF16)
            dp = _dot(dob, vb, NT)
            ds = p * (dp - jnp.sum(do * o, axis=-1, keepdims=True))
            dtable_ref[e] += ds
            dsb = ds.astype(BF16)
            dq_ref[:, lanes] = (_dot(dsb, k_ref[in_qkv, lanes]) * scale).astype(dq_ref.dtype)
            dk_ref[keys, lanes] += _dot(dsb, q_ref[:, lanes], TN) * scale
            dv_ref[keys, lanes] += _dot(pb, dob, TN)

        @pl.when(n == T // QROWS - 1)
        def _():
            for e in range(PAIR):
                dband_ref[e] = sum(dtable_ref[e, g * CHUNK:(g + 1) * CHUNK, g * CHUNK:g * CHUNK + BAND]
                                   for g in range(QG))
            dkb_ref[...] = dk_ref[KPAD:, :].astype(dkb_ref.dtype)
            dvb_ref[...] = dv_ref[KPAD:, :].astype(dvb_ref.dtype)

    per_step, queries, keys, values, grads, table, vec = _attn_specs(T, n_pairs)
    H = n_pairs * PAIR
    return _call(
        body, rider, name="attn_bwd", grid=(n_pairs, T // QROWS),
        in_specs=[queries, keys, values, _probs_spec(), vec, per_step],
        out_specs=[per_step, grads, grads, table, vec],
        out_shape=[jax.ShapeDtypeStruct((T, W), BF16)] * 3 + [jax.ShapeDtypeStruct((H, CHUNK, BAND), F32),
                                                              jax.ShapeDtypeStruct((1, W), F32)],
        scratch_shapes=[pltpu.VMEM((PAIR, QROWS, UNION), F32)] + [pltpu.VMEM((KPAD + T, PAIR * HD_A), F32)] * 2,
        compiler_params=_params(("arbitrary", "arbitrary"), 40),
        operands=(qkv, qkv, qkv, probs, gain, dmixin))


N_DIAG = CHUNK + BAND - 1


def _bias_band(rel_bias):
    H = rel_bias.shape[0]
    idx = np.clip(BAND - 1 - np.arange(N_DIAG), -MAX_REL, MAX_REL) + MAX_REL
    rolled = rel_bias[:, idx[(np.arange(N_DIAG) + CHUNK - 1) % N_DIAG]]
    flat = jnp.broadcast_to(rolled[:, None, :], (H, CHUNK, N_DIAG)).reshape(H, CHUNK * N_DIAG)
    return flat[:, :CHUNK * (N_DIAG - 1)].reshape(H, CHUNK, N_DIAG - 1)[:, :, :BAND]


def _bias_band_grad(dband):
    H = dband.shape[0]
    skew = jnp.pad(dband, ((0, 0), (0, 0), (CHUNK - 1, 0))).reshape(H, CHUNK * N_DIAG)
    skew = jnp.pad(skew, ((0, 0), (0, CHUNK))).reshape(H, CHUNK, N_DIAG + 1)
    diag = jnp.sum(skew, axis=1)[:, :N_DIAG]
    n_far = BAND - MAX_REL
    far = jnp.sum(diag[:, :n_far], axis=1, keepdims=True)
    near = diag[:, n_far:][:, ::-1]
    zeros = jnp.zeros((H, MAX_REL - (CHUNK - 1)), F32)
    return jnp.concatenate([zeros, near, far], axis=1)


def _tri(n, lower):
    r = lax.broadcasted_iota(jnp.int32, (n, n), 0)
    c = lax.broadcasted_iota(jnp.int32, (n, n), 1)
    return jnp.where((c <= r) if lower else (c >= r), 1.0, 0.0).astype(F32)


def _hgrn_gates(zq_ref, zf_ref, lbl_ref, q_s, k_s, b_s):
    lb = jax.nn.sigmoid(lbl_ref[0:1, :] - lbl_ref[1:2, :])
    zq = zq_ref[...]
    sig = jax.nn.sigmoid(zf_ref[...])
    f = lb + (1.0 - lb) * sig
    sq = jax.nn.sigmoid(zq)
    q_s[...] = zq * sq
    k_s[...] = 1.0 - f
    b_s[...] = _dot(_tri(CHUNK, True), jnp.log(f), precision=HIGHEST)
    return lb, sig, f, sq


def _sub_rows(i):
    return pl.ds(i * SUB, SUB)


def _row_mask(s):
    return lax.broadcasted_iota(jnp.int32, (SUB, HD_B), 0) >= s


def _decay_from(b_sub, b_row, s):
    return jnp.where(_row_mask(s), jnp.exp(jnp.minimum(b_sub - b_row, 0.0)), 0.0)


def _hgrn_fwd(proj, lb_logits, gnorm_g, rider=None):
    T = proj.shape[0]
    nC = T // CHUNK
    W = lb_logits.shape[1]
    G = W // HD_B // HGRN_HEADS
    col0 = (proj.shape[1] - 4 * W) // (HD_B * HGRN_HEADS)
    wide = HGRN_HEADS * HD_B

    def body(*refs):
        @pl.when(pl.program_id(1) == 0)
        def _():
            refs[9][...] = jnp.zeros_like(refs[9])

        for h in range(HGRN_HEADS):
            lanes = pl.ds(h * HD_B, HD_B)
            one_head(*[r.at[:, lanes] for r in refs[:5]], refs[5], *[r.at[:, lanes] for r in refs[6:8]],
                     *[r.at[h] for r in refs[8:]])

    def one_head(zq_ref, zf_ref, xi_ref, zg_ref, lbl_ref, gn_ref, mix_ref, o_ref, stall_ref, st_ref, q_s, k_s, b_s, acc_s):
        _hgrn_gates(zq_ref, zf_ref, lbl_ref, q_s, k_s, b_s)
        q, k, b = q_s[...], k_s[...], b_s[...]
        st = st_ref[...]
        stall_ref[...] = st
        b_last = b_s[CHUNK - 1:CHUNK, :]
        acc_s[...] = _dot((q * jnp.exp(b)).astype(BF16), st.astype(BF16), NT)
        for i in range(CHUNK // SUB):
            rows = _sub_rows(i)
            q_i, b_i = q_s[rows, :], b_s[rows, :]
            acc = jnp.zeros((SUB, HD_B), F32)
            if i:
                past = pl.ds(0, i * SUB)
                b_ref = b_s[i * SUB - 1:i * SUB, :]
                qs = (q_i * jnp.exp(b_i - b_ref)).astype(BF16)
                ks = (k_s[past, :] * jnp.exp(b_ref - b_s[past, :])).astype(BF16)
                acc += _dot(_dot(qs, ks, NT).astype(BF16), xi_ref[past, :].astype(BF16))
            for s in range(SUB):
                row = pl.ds(i * SUB + s, 1)
                w = q_i * _decay_from(b_i, b_s[row, :], s)
                acc += jnp.sum(w * k_s[row, :], axis=-1, keepdims=True) * xi_ref[row, :]
            acc_s[rows, :] += acc
        o = acc_s[...]
        kd = (k * jnp.exp(b_last - b)).astype(BF16)
        st_ref[...] = st * jnp.exp(b_last) + _dot(xi_ref[...].astype(BF16), kd, TN)
        o_ref[...] = o
        zg = zg_ref[...]
        rr = lax.rsqrt(jnp.mean(o * o, axis=-1, keepdims=True) + EPS)
        mix_ref[...] = (o * rr * gn_ref[...] * (zg * jax.nn.sigmoid(zg))).astype(mix_ref.dtype)

    col = lambda part: pl.BlockSpec((CHUNK, wide), lambda g, n: (n, col0 + part * G + g))
    out_blk = pl.BlockSpec((CHUNK, wide), lambda g, n: (n, g))
    tile = pltpu.VMEM((HGRN_HEADS, CHUNK, HD_B), F32)
    return _call(
        body, rider, name="hgrn_fwd", grid=(G, nC),
        in_specs=[col(0), col(1), col(2), col(3), pl.BlockSpec((2, wide), lambda g, n: (0, g)),
                  pl.BlockSpec((1, HD_B), lambda g, n: (0, 0))],
        out_specs=[out_blk, out_blk, pl.BlockSpec((HGRN_HEADS, None, HD_B, HD_B), lambda g, n: (g, n, 0, 0))],
        out_shape=[jax.ShapeDtypeStruct((T, W), BF16), jax.ShapeDtypeStruct((T, W), F32),
                   jax.ShapeDtypeStruct((G * HGRN_HEADS, nC, HD_B, HD_B), F32)],
        scratch_shapes=[pltpu.VMEM((HGRN_HEADS, HD_B, HD_B), F32), tile, tile, tile, tile],
        compiler_params=_params(("arbitrary", "arbitrary")),
        operands=(proj, proj, proj, proj, lb_logits, gnorm_g))


def _hgrn_bwd(proj, lb_logits, gnorm_g, o_b, st_all, dmixin, d_attn, rider=None):
    T = proj.shape[0]
    nC = T // CHUNK
    W = lb_logits.shape[1]
    wa = d_attn[0].shape[1]
    assert W == HGRN_HEADS * HD_B, "one grid step takes every head: it writes whole rows of d proj"
    G = W // HD_B // HGRN_HEADS
    wide = HGRN_HEADS * HD_B
    col0 = (proj.shape[1] - 4 * W) // wide
    dcol0 = (dmixin.shape[1] - W) // wide

    def body(*refs):
        g, n = pl.program_id(0), pl.program_id(1)
        dproj_ref, dl0_ref, dgn_ref, dst_ref = refs[12:16]
        for i in range(3):
            dproj_ref[:, i * wa:(i + 1) * wa] = refs[9 + i][...]

        @pl.when(n == 0)
        def _():
            dst_ref[...] = jnp.zeros_like(dst_ref)
            dl0_ref[...] = jnp.zeros_like(dl0_ref)

        @pl.when((n == 0) & (g == 0))
        def _():
            dgn_ref[...] = jnp.zeros_like(dgn_ref)

        for h in range(HGRN_HEADS):
            lanes = pl.ds(h * HD_B, HD_B)
            cut = lambda r: r.at[:, lanes]
            parts = [dproj_ref.at[:, pl.ds(3 * wa + part * W + h * HD_B, HD_B)] for part in range(4)]
            one_head(*[cut(r) for r in refs[:5]], refs[5], cut(refs[6]), refs[7].at[h], cut(refs[8]),
                     *parts, cut(dl0_ref), dgn_ref, *[r.at[h] for r in refs[15:]])

    def one_head(zq_ref, zf_ref, xi_ref, zg_ref, lbl_ref, gn_ref, o_ref, st_ref, dout_ref,
                 dzq_ref, dzf_ref, dxi_ref, dzg_ref, dl0_ref, dgn_ref, dst_ref, q_s, k_s, b_s, do_s, dq_s, dk_s, di_s):
        lb, sig, f, sq = _hgrn_gates(zq_ref, zf_ref, lbl_ref, q_s, k_s, b_s)
        q, k, b = q_s[...], k_s[...], b_s[...]
        zg, o, dout = zg_ref[...], o_ref[...], dout_ref[...]
        sg = jax.nn.sigmoid(zg)
        rr = lax.rsqrt(jnp.mean(o * o, axis=-1, keepdims=True) + EPS)
        on = o * rr
        gn = gn_ref[...]
        dzg_ref[...] = (dout * on * gn * sg * (1.0 + zg * (1.0 - sg))).astype(dzg_ref.dtype)
        d_on = dout * zg * sg
        dgn_ref[...] += _colsum(d_on * on)
        d_on = d_on * gn
        do = rr * (d_on - on * jnp.mean(d_on * on, axis=-1, keepdims=True))
        do_s[...] = do
        dob = do.astype(BF16)
        st, dst = st_ref[...], dst_ref[...]
        b_last = b_s[CHUNK - 1:CHUNK, :]
        eb, e_last, k_dec = jnp.exp(b), jnp.exp(b_last), jnp.exp(b_last - b)
        qt, kd = q * eb, k * k_dec
        dstb = dst.astype(BF16)
        xib = xi_ref[...].astype(BF16)
        d_kd = _dot(xib, dstb)
        dq_s[...] = _dot(dob, st.astype(BF16)) * eb
        dk_s[...] = d_kd * k_dec
        di_s[...] = _dot(kd.astype(BF16), dstb, NT)
        d_b_last = e_last * _colsum(st * dst) + _colsum(d_kd * kd)
        dst_ref[...] = _dot(dob, qt.astype(BF16), TN) + dst * e_last
        for i in range(CHUNK // SUB):
            rows = _sub_rows(i)
            q_i, b_i, do_i = q_s[rows, :], b_s[rows, :], do_s[rows, :]
            dq_i = jnp.zeros((SUB, HD_B), F32)
            if i:
                past = pl.ds(0, i * SUB)
                b_ref = b_s[i * SUB - 1:i * SUB, :]
                e_q, e_k = jnp.exp(b_i - b_ref), jnp.exp(b_ref - b_s[past, :])
                qs, ks = (q_i * e_q).astype(BF16), (k_s[past, :] * e_k).astype(BF16)
                xi_p, do_b = xi_ref[past, :].astype(BF16), do_i.astype(BF16)
                di_s[past, :] += _dot(_dot(ks, qs, NT).astype(BF16), do_b)
                dq_i += _dot(_dot(do_b, xi_p, NT).astype(BF16), ks) * e_q
                dk_s[past, :] += _dot(_dot(xi_p, do_b, NT).astype(BF16), qs) * e_k
            for s in range(SUB):
                row = pl.ds(i * SUB + s, 1)
                k_row, i_row = k_s[row, :], xi_ref[row, :]
                e = _decay_from(b_i, b_s[row, :], s)
                w = q_i * e
                a_col = jnp.sum(w * k_row, axis=-1, keepdims=True)
                da_col = jnp.sum(do_i * i_row, axis=-1, keepdims=True)
                di_s[row, :] += _colsum(a_col * do_i)
                dq_i += da_col * e * k_row
                dk_s[row, :] += _colsum(da_col * w)
            dq_s[rows, :] += dq_i
        dq, dk = dq_s[...], dk_s[...]
        db = q * dq - k * dk
        is_last = lax.broadcasted_iota(jnp.int32, (CHUNK, HD_B), 0) == CHUNK - 1
        db = db + jnp.where(is_last, d_b_last, 0.0)
        df = _dot(_tri(CHUNK, False), db, precision=HIGHEST) / f - dk
        dzf_ref[...] = (df * (1.0 - lb) * sig * (1.0 - sig)).astype(dzf_ref.dtype)
        dl0_ref[...] += _colsum(df * (1.0 - sig)) * (lb * (1.0 - lb))
        zq = zq_ref[...]
        dzq_ref[...] = (dq * sq * (1.0 + zq * (1.0 - sq))).astype(dzq_ref.dtype)
        dxi_ref[...] = di_s[...].astype(dxi_ref.dtype)

    rev = lambda n: nC - 1 - n
    col = lambda part: pl.BlockSpec((CHUNK, wide), lambda g, n: (rev(n), col0 + part * G + g))
    blk = pl.BlockSpec((CHUNK, wide), lambda g, n: (rev(n), g))
    tile = pltpu.VMEM((HGRN_HEADS, CHUNK, HD_B), F32)
    rows = lambda width: pl.BlockSpec((CHUNK, width), lambda g, n: (rev(n), 0))
    return _call(
        body, rider, name="hgrn_bwd", grid=(G, nC),
        in_specs=[col(0), col(1), col(2), col(3), pl.BlockSpec((2, wide), lambda g, n: (0, g)),
                  pl.BlockSpec((1, HD_B), lambda g, n: (0, 0)), blk,
                  pl.BlockSpec((HGRN_HEADS, None, HD_B, HD_B), lambda g, n: (g, rev(n), 0, 0)),
                  pl.BlockSpec((CHUNK, wide), lambda g, n: (rev(n), dcol0 + g)), rows(wa), rows(wa), rows(wa)],
        out_specs=[rows(3 * wa + 4 * W), pl.BlockSpec((1, wide), lambda g, n: (0, g)),
                   pl.BlockSpec((1, HD_B), lambda g, n: (0, 0))],
        out_shape=[jax.ShapeDtypeStruct((T, 3 * wa + 4 * W), BF16), jax.ShapeDtypeStruct((1, W), F32),
                   jax.ShapeDtypeStruct((1, HD_B), F32)],
        scratch_shapes=[pltpu.VMEM((HGRN_HEADS, HD_B, HD_B), F32)] + [tile] * 7,
        compiler_params=_params(("arbitrary", "arbitrary")),
        operands=(proj, proj, proj, proj, lb_logits, gnorm_g, o_b, st_all, dmixin, *d_attn))


def _adamw_math(g, w, m, v):
    m = B1 * m + (1.0 - B1) * g
    v = B2 * v + (1.0 - B2) * (g * g)
    m_hat = m / (1.0 - B1 ** STEP)
    v_hat = v / (1.0 - B2 ** STEP)
    return -LR * (m_hat / (jnp.sqrt(v_hat) + ADAM_EPS) + WD * w), m, v


def _adamw(g, w, m, v, name):
    R, C = g.shape
    tr = _row_tile(R, C)

    def body(g_ref, w_ref, m_ref, v_ref, go_ref, d_ref, mo_ref, vo_ref):
        g = g_ref[...]
        go_ref[...] = g
        d_ref[...], mo_ref[...], vo_ref[...] = _adamw_math(g, w_ref[...], m_ref[...], v_ref[...])

    blk = pl.BlockSpec((tr, C), lambda i: (i, 0))
    return pl.pallas_call(
        body, name=name, grid=(R // tr,), in_specs=[blk] * 4, out_specs=[blk] * 4,
        out_shape=[jax.ShapeDtypeStruct((R, C), F32)] * 4, compiler_params=_params(("parallel",), 40),
    )(g, w, m, v)


def _sum_pair(g_full, from_sibling, sel, name):
    Q, K, Ns = g_full.shape
    kh = K // 2
    tr = _row_tile(kh, Ns)
    nh = kh // tr

    def body(sel_ref, a_ref, b_ref, o_ref):
        o_ref[...] = (a_ref[...].astype(F32) + b_ref[...].astype(F32)).astype(o_ref.dtype)

    return pl.pallas_call(
        body, name=name,
        grid_spec=pltpu.PrefetchScalarGridSpec(
            num_scalar_prefetch=1, grid=(Q, nh),
            in_specs=[pl.BlockSpec((None, tr, Ns), lambda q, i, sel: (q, sel[1] * nh + i, 0)),
                      pl.BlockSpec((None, tr, Ns), lambda q, i, sel: (q, i, 0))],
            out_specs=pl.BlockSpec((None, tr, Ns), lambda q, i, sel: (q, i, 0))),
        out_shape=jax.ShapeDtypeStruct((Q, kh, Ns), BF16), compiler_params=_params(("parallel", "parallel")),
    )(sel, g_full, from_sibling)


def _sum_chips(pair_sum, from_chips, sel, name):
    Q, kh, Ns = pair_sum.shape
    tr = _row_tile(kh, Ns)
    nh = kh // tr

    def body(sel_ref, a_ref, b0_ref, b1_ref, b2_ref, o_ref):
        up = lambda r: r[...].astype(F32)
        o_ref[...] = ((up(a_ref) + up(b0_ref)) + up(b1_ref)) + up(b2_ref)

    recv = lambda k: pl.BlockSpec((None, tr, Ns), lambda i, sel: (k, i, 0))
    return pl.pallas_call(
        body, name=name,
        grid_spec=pltpu.PrefetchScalarGridSpec(
            num_scalar_prefetch=1, grid=(nh,),
            in_specs=[pl.BlockSpec((None, tr, Ns), lambda i, sel: (sel[0], i, 0)), recv(0), recv(1), recv(2)],
            out_specs=pl.BlockSpec((tr, Ns), lambda i, sel: (sel[1] * nh + i, 0))),
        out_shape=jax.ShapeDtypeStruct((2 * kh, Ns), F32), compiler_params=_params(("parallel",)),
    )(sel, pair_sum, from_chips, from_chips, from_chips)


def _gather_small(v, name):
    R, L = v.shape

    def body(v_ref, out_ref, send_sems, recv_sems):
        x, y, c = _place()
        me = 4 * x + 2 * y + c
        out_ref[me] = v_ref[...]
        peers = [(_flip(x, k >> 2 & 1), _flip(y, k >> 1 & 1), _flip(c, k & 1)) for k in range(1, N_DEV)]

        def copy(k, row, to):
            return pltpu.make_async_remote_copy(src_ref=v_ref, dst_ref=out_ref.at[row], send_sem=send_sems.at[k],
                                                recv_sem=recv_sems.at[k], device_id=to, device_id_type=MESH)

        sends = [copy(k, me, peer) for k, peer in enumerate(peers)]
        for cp in sends:
            cp.start()
        for k, (px, py, pc) in enumerate(peers):
            copy(k, 4 * px + 2 * py + pc, (x, y, c)).wait_recv()
        for cp in sends:
            cp.wait_send()

    vmem = pl.BlockSpec(memory_space=pltpu.VMEM)
    return pl.pallas_call(
        body, name=name, in_specs=[vmem], out_specs=vmem, out_shape=jax.ShapeDtypeStruct((N_DEV, R, L), F32),
        scratch_shapes=[pltpu.SemaphoreType.DMA((N_DEV - 1,)), pltpu.SemaphoreType.DMA((N_DEV - 1,))],
    )(v)


def _silu(v):
    return v * jax.nn.sigmoid(v)


def _ada_fwd(c_all, w_ada, tn=512):
    M, D = c_all.shape
    Ns = w_ada.shape[1]

    def body(c_ref, w_ref, o_ref):
        o_ref[...] = _dot(_silu(c_ref[...]).astype(BF16), w_ref[...].astype(BF16))

    return pl.pallas_call(
        body, name="ada_fwd", grid=(Ns // tn,),
        in_specs=[pl.BlockSpec((M, D), lambda j: (0, 0)), pl.BlockSpec((D, tn), lambda j: (0, j))],
        out_specs=pl.BlockSpec((M, tn), lambda j: (0, j)), out_shape=jax.ShapeDtypeStruct((M, Ns), F32),
        compiler_params=_params(("parallel",)),
    )(c_all, w_ada)


def _ada_bwd(c_all, dmod, w, m, v, tk=256, tn=1536):
    M, D = c_all.shape
    Ns = dmod.shape[1]

    def body(c_ref, d_ref, w_ref, m_ref, v_ref, g_ref, dl_ref, mo_ref, vo_ref):
        g = _dot(_silu(c_ref[...]).astype(BF16), d_ref[...].astype(BF16), TN)
        g_ref[...] = g
        dl_ref[...], mo_ref[...], vo_ref[...] = _adamw_math(g, w_ref[...], m_ref[...], v_ref[...])

    blk = pl.BlockSpec((tk, tn), lambda i, j: (i, j))
    return pl.pallas_call(
        body, name="ada_bwd", grid=(D // tk, Ns // tn),
        in_specs=[pl.BlockSpec((M, tk), lambda i, j: (0, i)), pl.BlockSpec((M, tn), lambda i, j: (0, j)), blk, blk, blk],
        out_specs=[blk] * 4, out_shape=[jax.ShapeDtypeStruct((D, Ns), F32)] * 4,
        compiler_params=_params(("parallel", "parallel"), 40),
    )(c_all, dmod, w, m, v)


def _small_update(g_all, w, m, v):
    R, L = w.shape

    def body(g_ref, w_ref, m_ref, v_ref, go_ref, d_ref, mo_ref, vo_ref):
        g = g_ref[0]
        for d in range(1, N_DEV):
            g = g + g_ref[d]
        go_ref[...] = g
        d_ref[...], mo_ref[...], vo_ref[...] = _adamw_math(g, w_ref[...], m_ref[...], v_ref[...])

    return pl.pallas_call(body, name="small_update", out_shape=[jax.ShapeDtypeStruct((R, L), F32)] * 4)(g_all, w, m, v)


def _pack(parts, rows):
    flat = jnp.concatenate([p.reshape(-1) for p in parts])
    return jnp.pad(flat, (0, rows * 128 - flat.shape[0])).reshape(rows, 128)


def _unpack(packed, shapes):
    flat, out, at = packed.reshape(-1), [], 0
    for shp in shapes:
        size = 1
        for d in shp:
            size *= d
        out.append(flat[at:at + size].reshape(shp))
        at += size
    return out


def _layer(x, tgt, mod, wts, rel_bias, attn_norm_g, lb_logits, gnorm_g, ln1_g, ln1_b, ln2_g, ln2_b, place=None):
    T, D = x.shape
    aw = attn_norm_g.shape[1]
    shift1, scale1, gate1, shift2, scale2, gate2 = [mod[i:i + 1] for i in range(6)]

    def gather(n, rows=None, into=None, before=None, last=True):
        return None if place is None else _gather_rider(wts[n], rows, None if into is None else into[0], before, last)

    def gathered(n, rode):
        return wts[n] if place is None else lax.dynamic_update_index_in_dim(rode[0], wts[n], place[0], 0)

    def blocks(g):
        return g.reshape(N_CHIPS, -1, g.shape[2])

    def to_sibling(g):
        return None if place is None else _pair_rider(g)

    def pair_sum(n, g, rode=None):
        if place is None:
            return g
        rode = _alone(_pair_rider(g), n + "_send_pair") if rode is None else rode
        return _sum_pair(g, rode[0], place[1], n + "_sum_pair")

    def to_chips(p, rows=None, into=None):
        return None if place is None else _chips_rider(p, rows, None if into is None else into[0])

    def summed(n, p, rode):
        return p if place is None else _sum_chips(p, rode[0], place[1], n + "_sum_chips")

    def to_both(block):
        return None if place is None else _share_rider(block)

    def carrying(mm, *args, rider, **kw):
        return mm(*args, rider=rider, **kw) if rider is not None else (mm(*args, **kw), None)

    def to_sibling_acts(a, b):
        return None if place is None else _acts_rider(a, b)

    def pair_grad(name, a, b, tn, rider, arrived=None, late_rider=None):
        if place is None:
            return _mm_tn(a, b, q=N_CHIPS, tk=512, tn=tn, tt=T, name=name), None
        kh = a.shape[1] // 2
        mine = lax.dynamic_slice_in_dim(a, place[1][1] * kh, kh, axis=1)
        part, rode = carrying(_mm_tn, mine, b, q=N_CHIPS, tk=512, tn=tn, tt=T, name=name + "_own",
                              rider=_join(None if arrived else _acts_rider(a, b), rider))
        (a_sib, b_sib), rode = arrived or rode[:2], rode if arrived else rode[2:]
        out, late = _mm_tn_add(a_sib, b_sib, part, tk=512, tn=tn, name=name + "_sib", rider=late_rider)
        return out, (rode or []) + late

    w_in = gathered("w_in", None if place is None else [wts["w_in_gathered"]])
    h1 = _pre_mixer(x, scale1, shift1)
    n_qkv = 3 * aw // 256
    kh_o, kh_f, kh_out = [wts[n].shape[-2] // 2 for n in ("w_o", "w_ffn_in", "w_ffn_out")]
    o_cut, f_cuts, out_cut = 3 * kh_o // 8, (7 * kh_f // 16, 7 * kh_f // 8), 2 * kh_out // 11
    qkv, rode = carrying(_mm_nn, h1, w_in, tm=ZPAD, tn=256, tk=D, name="proj_qkv", cols=(0, n_qkv), o_dtype=BF16,
                         pad_rows=ZPAD, rider=gather("w_o", (0, o_cut), last=False))
    proj, rode = carrying(_mm_nn, h1, w_in, tm=2048, tn=256, tk=D, name="proj_rec",
                          cols=(n_qkv, N_CHIPS * w_in.shape[2] // 256),
                          rider=gather("w_o", (o_cut, kh_o - o_cut), rode, before=(0, o_cut)))
    w_o3 = gathered("w_o", rode).reshape(1, D, D)
    bias = _bias_band(rel_bias)
    (mix_a, probs), rode = _attn_fwd(qkv, bias, attn_norm_g, rider=gather("w_ffn_in", (0, f_cuts[0]), last=False))
    (mix_b, o_b, st_all), rode = _hgrn_fwd(
        proj, lb_logits, gnorm_g,
        rider=gather("w_ffn_in", (f_cuts[0], f_cuts[1] - f_cuts[0]), rode, before=(0, f_cuts[0]), last=False))
    mixin = jnp.concatenate([mix_a, mix_b], axis=1)
    mix, out_part = carrying(_mm_nn, mixin, w_o3, tm=1024, tn=512, tk=D, name="mix_out",
                             rider=gather("w_ffn_out", (0, out_cut), last=False))
    if place is None:
        x1, h2 = _post_mixer(mix, x, gate1, ln1_g, ln1_b, scale2, shift2)
    else:
        (x1, h2), rode = _post_mixer(
            mix, x, gate1, ln1_g, ln1_b, scale2, shift2,
            rider=gather("w_ffn_in", (f_cuts[1], kh_f - f_cuts[1]), rode, before=(f_cuts[0], f_cuts[1] - f_cuts[0])))
    w_ffn_in = gathered("w_ffn_in", rode)
    (gate, up, act), rode = _ffn_in_swiglu(
        h2, w_ffn_in, tm=2048, tn=256,
        rider=gather("w_ffn_out", (out_cut, kh_out - out_cut), out_part, before=(0, out_cut)))
    w_out3 = gathered("w_ffn_out", rode)
    w_out3 = w_out3.reshape(1, -1, w_out3.shape[2])
    d_ff = w_out3.shape[1]
    f = _mm_nn(act, w_out3, tm=1024, tn=512, tk=d_ff, name="ffn_out")
    du2, df, acc2 = _loss_head(f, x1, tgt, gate2, ln2_g, ln2_b)
    loss = (0.5 / D) * jnp.sum(acc2[3])
    g = blocks(_mm_tn(act, df, q=1, tk=512, tn=1024, tt=T, name="g_ffn_out"))
    (dff,), rode = _d_act_swiglu(df, w_out3, gate, up, tm=1024, to=512, rider=to_sibling(g))
    p_out = pair_sum("w_ffn_out", g, rode)
    cut = 21 * p_out.shape[1] // 44
    dh2, rode = carrying(_mm_nt, dff, w_ffn_in, tm=1024, to=1024, tn=w_ffn_in.shape[2], name="d_h2",
                         rider=_join(to_chips(p_out, (0, cut)), to_sibling_acts(h2, dff)))
    p_fin, rode = pair_grad("g_ffn_in", h2, dff, w_ffn_in.shape[2] // 2,
                            to_chips(p_out, (cut, p_out.shape[1] - cut), rode), arrived=rode and rode[1:])
    g_ffn_out = summed("w_ffn_out", p_out, rode)
    if place is None:
        du1, dmix, acc1 = _mid_bwd(dh2, du2, x1, mix, x, gate1, ln1_g, scale2)
    else:
        (du1, dmix, acc1), (g_ffn_out,) = _mid_bwd(dh2, du2, x1, mix, x, gate1, ln1_g, scale2, rider=to_both(g_ffn_out))
    g = blocks(_mm_tn(mixin, dmix, q=1, tk=512, tn=1024, tt=T, name="g_o"))
    dmixin, rode = carrying(_mm_nt, dmix, w_o3, tm=1024, to=512, tn=D, name="d_mixin", rider=to_sibling(g))
    p_o = pair_sum("w_o", g, rode)
    cut = p_fin.shape[1] // 2
    (dq, dk, dv, dbias, dgain), rode = _attn_bwd(qkv, probs, attn_norm_g, dmixin, rider=to_chips(p_fin, (0, cut)))
    (dproj, dl0, dgn), rode = _hgrn_bwd(
        proj, lb_logits, gnorm_g, o_b, st_all, dmixin, (dq, dk, dv),
        rider=_join(to_chips(p_fin, (cut, p_fin.shape[1] - cut), rode), to_chips(p_o)))
    g_ffn_in, g_o = summed("w_ffn_in", p_fin, rode[:1]), summed("w_o", p_o, rode[1:])
    p_in, rode = pair_grad("g_in", h1, dproj, w_in.shape[2] // 2, None,
                           late_rider=_join(to_both(g_ffn_in), to_both(g_o)))
    if place is not None:
        g_ffn_in, g_o = rode
    cut = 3 * p_in.shape[1] // 4
    dh1, rode = carrying(_mm_nt, dproj, w_in, tm=1024, to=1024, tn=w_in.shape[2], name="d_h1",
                         rider=to_chips(p_in, (0, cut)))
    if place is None:
        (grad_x, acc0), g_in = _first_bwd(dh1, du1, x, scale1), p_in
    else:
        (grad_x, acc0), rode = _first_bwd(dh1, du1, x, scale1, rider=to_chips(p_in, (cut, p_in.shape[1] - cut), rode))
        g_in, = _alone(to_both(summed("w_in", p_in, rode)), "w_in_share")
    dmod = jnp.concatenate([acc0[1:2], acc0[0:1], acc1[4:5], acc1[1:2], acc1[0:1], acc2[2:3]], axis=0)
    small = dict(rel_bias=_bias_band_grad(dbias), attn_norm_g=dgain,
                 lb_logits=jnp.concatenate([dl0, -dl0], axis=0), gnorm_g=dgn,
                 ln1_g=acc1[2:3], ln1_b=acc1[3:4], ln2_g=acc2[0:1], ln2_b=acc2[1:2])
    return loss, grad_x, dict(w_in=g_in, w_o=g_o, w_ffn_in=g_ffn_in, w_ffn_out=g_ffn_out), dmod, small


SMALL = ("rel_bias", "attn_norm_g", "lb_logits", "gnorm_g", "ln1_g", "ln1_b", "ln2_g", "ln2_b")
SMALL_ROWS = 256


def kernel(x, c, w_ada, b_ada, w_in, rel_bias, attn_norm_g, lb_logits, gnorm_g, w_o, ln1_g, ln1_b, w_ffn_in, w_ffn_out, ln2_g, ln2_b, loss_target, m_w_ada, m_b_ada, m_w_in, m_rel_bias, m_attn_norm_g, m_lb_logits, m_gnorm_g, m_w_o, m_ln1_g, m_ln1_b, m_w_ffn_in, m_w_ffn_out, m_ln2_g, m_ln2_b, v_w_ada, v_b_ada, v_w_in, v_rel_bias, v_attn_norm_g, v_lb_logits, v_gnorm_g, v_w_o, v_ln1_g, v_ln1_b, v_w_ffn_in, v_w_ffn_out, v_ln2_g, v_ln2_b):
    mx, my, mc = _place()
    me = 4 * mx + 2 * my + mc
    chip = 2 * mx + my
    sel = jnp.stack([chip, mc]).astype(jnp.int32)
    D = x.shape[2]
    ns_ada = w_ada.shape[2]

    big = dict(w_in=(w_in, m_w_in, v_w_in), w_o=(w_o, m_w_o, v_w_o), w_ffn_in=(w_ffn_in, m_w_ffn_in, v_w_ffn_in),
               w_ffn_out=(w_ffn_out, m_w_ffn_out, v_w_ffn_out))
    shards = dict(w_in=w_in[0].astype(BF16))
    kh, rode, rows = shards["w_in"].shape[0] // 2, None, None
    for n, part in (("w_ffn_in", 19), ("w_ffn_out", 9), ("w_o", 4)):
        before, rows = rows, (rows[0] + rows[1] if rows else 0, part * kh // 32)
        (shards[n],), rode = _to_bf16(big[n][0][0], "cast_" + n,
                                      _gather_rider(shards["w_in"], rows, rode and rode[0], before, last=n == "w_o"))
    shards["w_in_gathered"] = rode[0]

    c_all = _gather_small(c.reshape(D // 128, 128), "gather_c").reshape(N_DEV, D)
    c_all = jnp.pad(c_all, ((0, 16 - N_DEV), (0, 0)))
    mod_cols = _ada_fwd(c_all, w_ada[0])[:N_DEV]
    mod_all = _gather_small(mod_cols.reshape(-1, 128), "gather_mod").reshape(N_DEV, N_DEV, ns_ada)
    mod = lax.dynamic_index_in_dim(mod_all[::2], me, axis=1, keepdims=False)
    mod = (mod.reshape(1, -1) + b_ada).reshape(6, D)

    loss, grad_x, g_big, dmod, g_small = _layer(
        x[0], loss_target[0], mod, shards, rel_bias[0], attn_norm_g, lb_logits, gnorm_g, ln1_g, ln1_b, ln2_g, ln2_b,
        place=(chip, sel))

    grads, deltas, new_m, new_v = {}, {}, {}, {}
    for n, (w, m, v) in big.items():
        g, d, mo, vo = _adamw(g_big[n], w[0], m[0], v[0], "adamw_" + n)
        grads[n], deltas[n], new_m[n], new_v[n] = g[None], d[None], mo[None], vo[None]

    small_in = dict(rel_bias=(rel_bias, m_rel_bias, v_rel_bias), attn_norm_g=(attn_norm_g, m_attn_norm_g, v_attn_norm_g),
                    lb_logits=(lb_logits, m_lb_logits, v_lb_logits), gnorm_g=(gnorm_g, m_gnorm_g, v_gnorm_g),
                    ln1_g=(ln1_g, m_ln1_g, v_ln1_g), ln1_b=(ln1_b, m_ln1_b, v_ln1_b), ln2_g=(ln2_g, m_ln2_g, v_ln2_g),
                    ln2_b=(ln2_b, m_ln2_b, v_ln2_b))
    g_all = _gather_small(_pack([dmod] + [g_small[n] for n in SMALL] + [loss], SMALL_ROWS), "gather_small")
    packed = [_pack([t] + [small_in[n][i] for n in SMALL] + [jnp.zeros((), F32)], SMALL_ROWS)
              for i, t in enumerate((b_ada, m_b_ada, v_b_ada))]
    shapes = [b_ada.shape] + [small_in[n][0].shape for n in SMALL] + [()]
    outs = [_unpack(o, shapes) for o in _small_update(g_all, *packed)]
    loss = outs[0][-1]
    for i, n in enumerate(("b_ada",) + SMALL):
        grads[n], deltas[n], new_m[n], new_v[n] = outs[0][i], outs[1][i], outs[2][i], outs[3][i]

    dmod_all = g_all[:, :6 * D // 128].reshape(N_DEV, 6 * D)
    dmod_cols = lax.dynamic_slice_in_dim(dmod_all, chip * ns_ada, ns_ada, axis=1)
    dmod_cols = jnp.pad(dmod_cols, ((0, 16 - N_DEV), (0, 0)))
    g, d, mo, vo = _ada_bwd(c_all, dmod_cols, w_ada[0], m_w_ada[0], v_w_ada[0])
    grads["w_ada"], deltas["w_ada"], new_m["w_ada"], new_v["w_ada"] = g[None], d[None], mo[None], vo[None]

    order = ("w_ada", "b_ada", "w_in", "rel_bias", "attn_norm_g", "lb_logits", "gnorm_g", "w_o", "ln1_g", "ln1_b",
             "w_ffn_in", "w_ffn_out", "ln2_g", "ln2_b")
    return (loss, grad_x[None], *[grads[n] for n in order], *[deltas[n] for n in order],
            *[new_m[n] for n in order], *[new_v[n] for n in order])
```

```python
import numpy as np
import jax
import jax.numpy as jnp
from jax import lax
from jax.experimental import pallas as pl
from jax.experimental.pallas import tpu as pltpu

F32 = jnp.float32
BF16 = jnp.bfloat16
MESH = pl.DeviceIdType.MESH
HIGHEST = lax.Precision.HIGHEST

CHUNK = 64
N_PAST = 8
QG = 4
QROWS = QG * CHUNK
KPAD = N_PAST * CHUNK
ZPAD = 2 * KPAD
UNION = (QG + N_PAST) * CHUNK
BAND = (N_PAST + 1) * CHUNK
HD_A = 64
HD_B = 128
SUB = 16
HGRN_HEADS = 8
MAX_REL = 256
EPS = 1e-5
ALPHA = 2.0 ** 0.25
LR, B1, B2, ADAM_EPS, WD, STEP = 1e-3, 0.9, 0.999, 1e-8, 0.01, 10
N_CHIPS = 4
N_DEV = 8
NEG = -1e30
TILE_BYTES = 3 << 19

NN = ((1,), (0,))
NT = ((1,), (1,))
TN = ((0,), (0,))


def _dot(a, b, dims=NN, precision=None):
    return lax.dot_general(a, b, (dims, ((), ())), preferred_element_type=F32, precision=precision)


def _params(sem=None, vmem_mb=None, **kw):
    return pltpu.CompilerParams(dimension_semantics=sem,
                                vmem_limit_bytes=None if vmem_mb is None else vmem_mb << 20, **kw)


def _row_tile(rows, cols):
    for cand in (512, 256, 128, 64, 32, 16, 8):
        if rows % cand == 0 and cand * cols * 4 <= TILE_BYTES:
            return cand
    raise ValueError((rows, cols))


def _place():
    return lax.axis_index("x"), lax.axis_index("y"), lax.axis_index("c")


def _flip(v, bit):
    return 1 - v if bit else v


ANY = pl.BlockSpec(memory_space=pl.ANY)
CHIP_FLIPS = ((1, 0), (0, 1), (1, 1))


class _Rider:
    def __init__(self, operands, out_shape, n_sems, start, finish, aliases=None):
        self.operands, self.out_shape, self.n_sems, self.start, self.finish = operands, out_shape, n_sems, start, finish
        self.aliases = aliases or {}


def _call(body, rider, *, name, grid, in_specs, out_specs, out_shape, scratch_shapes, compiler_params, operands):
    if rider is None:
        outs = pl.pallas_call(body, name=name, grid=grid, in_specs=in_specs, out_specs=out_specs, out_shape=out_shape,
                              scratch_shapes=scratch_shapes, compiler_params=compiler_params)(*operands)
        return list(outs), []
    n_in, n_out, n_sc = len(in_specs), len(out_specs), len(scratch_shapes)
    r_in, r_out = len(rider.operands), len(rider.out_shape)

    def carried(*refs):
        refs = list(refs)
        cuts = [n_in, r_in, n_out, r_out, n_sc]
        ins, r_ins, outs, r_outs, scratch = [[refs.pop(0) for _ in range(n)] for n in cuts]
        first, last = None, None
        for axis, size in enumerate(grid):
            i = pl.program_id(axis)
            first = (i == 0) if first is None else first & (i == 0)
            last = (i == size - 1) if last is None else last & (i == size - 1)

        @pl.when(first)
        def _():
            rider.start(r_ins, r_outs, *refs)

        body(*ins, *outs, *scratch)

        @pl.when(last)
        def _():
            rider.finish(r_ins, r_outs, *refs)

    sems = [pltpu.SemaphoreType.DMA((rider.n_sems,)), pltpu.SemaphoreType.DMA((rider.n_sems,))]
    outs = pl.pallas_call(carried, name=name, grid=grid, in_specs=list(in_specs) + [ANY] * r_in,
                          out_specs=list(out_specs) + [ANY] * r_out, out_shape=list(out_shape) + rider.out_shape,
                          scratch_shapes=list(scratch_shapes) + sems, compiler_params=compiler_params,
                          input_output_aliases={n_in + i: n_out + o for i, o in rider.aliases.items()},
                          )(*operands, *rider.operands)
    return list(outs[:n_out]), list(outs[n_out:])


def _alone(rider, name):
    def body(*refs):
        ins, outs, sems = refs[:len(rider.operands)], refs[len(rider.operands):-2], refs[-2:]
        rider.start(ins, outs, *sems)
        rider.finish(ins, outs, *sems)

    return pl.pallas_call(
        body, name=name, in_specs=[ANY] * len(rider.operands), out_specs=[ANY] * len(rider.out_shape),
        out_shape=rider.out_shape, input_output_aliases=rider.aliases,
        scratch_shapes=[pltpu.SemaphoreType.DMA((rider.n_sems,)), pltpu.SemaphoreType.DMA((rider.n_sems,))],
    )(*rider.operands)


class _Sems:
    def __init__(self, sems, base):
        self.sems, self.base = sems, base

    @property
    def at(self):
        return self

    def __getitem__(self, k):
        return self.sems.at[self.base + k]


def _join(*riders):
    riders = [r for r in riders if r is not None]
    if len(riders) < 2:
        return riders[0] if riders else None

    def parts(ins, outs, send_sems, recv_sems):
        i = o = s = 0
        for r in riders:
            ni, no = len(r.operands), len(r.out_shape)
            yield r, ins[i:i + ni], outs[o:o + no], _Sems(send_sems, s), _Sems(recv_sems, s)
            i, o, s = i + ni, o + no, s + r.n_sems

    def start(*refs):
        for r, *args in parts(*refs):
            r.start(*args)

    def finish(*refs):
        for r, *args in parts(*refs):
            r.finish(*args)

    aliases, i, o = {}, 0, 0
    for r in riders:
        aliases.update({i + a: o + b for a, b in r.aliases.items()})
        i, o = i + len(r.operands), o + len(r.out_shape)
    return _Rider([a for r in riders for a in r.operands], [s for r in riders for s in r.out_shape],
                  sum(r.n_sems for r in riders), start, finish, aliases)


def _gather_rider(shard, rows=None, into=None, before=None, last=True):
    K, Ns = shard.shape
    kh = K // 2
    rows = rows or (0, kh)

    def copies(w_ref, out_ref, send_sems, recv_sems):
        x, y, c = _place()
        chips = [(_flip(x, fx), _flip(y, fy)) for fx, fy in CHIP_FLIPS]

        def half(chip, which, part):
            return out_ref.at[2 * chip[0] + chip[1], pl.ds(which * kh + part[0], part[1]), :]

        def copy(k, dst, to, src=None):
            return pltpu.make_async_remote_copy(src_ref=dst if src is None else src, dst_ref=dst,
                                                send_sem=send_sems.at[k], recv_sem=recv_sems.at[k],
                                                device_id=to, device_id_type=MESH)

        def first():
            return [copy(j, half((x, y), c, rows), (*chip, c), src=w_ref.at[pl.ds(c * kh + rows[0], rows[1]), :])
                    for j, chip in enumerate(chips)]

        def onward(base, part):
            return [copy(base + j, half(chip, c, part), (x, y, 1 - c)) for j, chip in enumerate(chips)]

        def arriving(base, which, part):
            return [copy(base + j, half(chip, which, part), (x, y, c)) for j, chip in enumerate(chips)]

        return c, first, onward, arriving

    def start(ins, outs, send_sems, recv_sems):
        _, first, onward, _ = copies(ins[0], outs[0], send_sems, recv_sems)
        for cp in first() + (onward(3, before) if before else []):
            cp.start()

    def finish(ins, outs, send_sems, recv_sems):
        c, first, onward, arriving = copies(ins[0], outs[0], send_sems, recv_sems)
        sent = first() + (onward(3, before) if before else [])
        passed = onward(6, rows) if last else [None] * 3
        for arrived, cp in zip(arriving(0, c, rows), passed):
            arrived.wait_recv()
            if last:
                cp.start()
        for arrived in (arriving(3, 1 - c, before) if before else []) + (arriving(6, 1 - c, rows) if last else []):
            arrived.wait_recv()
        for cp in sent + (passed if last else []):
            cp.wait_send()

    full = jax.ShapeDtypeStruct((N_CHIPS, K, Ns), shard.dtype)
    if into is None:
        return _Rider([shard], [full], 9, start, finish)
    return _Rider([shard, into], [full], 9, start, finish, aliases={1: 0})


def _pair_rider(g_full):
    Q, K, Ns = g_full.shape
    kh = K // 2

    def copy(g_ref, got_ref, send_sems, recv_sems):
        x, y, c = _place()
        return pltpu.make_async_remote_copy(src_ref=g_ref.at[:, pl.ds((1 - c) * kh, kh), :], dst_ref=got_ref,
                                            send_sem=send_sems.at[0], recv_sem=recv_sems.at[0],
                                            device_id=(x, y, 1 - c), device_id_type=MESH)

    def start(ins, outs, send_sems, recv_sems):
        copy(ins[0], outs[0], send_sems, recv_sems).start()

    def finish(ins, outs, send_sems, recv_sems):
        copy(ins[0], outs[0], send_sems, recv_sems).wait()

    return _Rider([g_full], [jax.ShapeDtypeStruct((Q, kh, Ns), g_full.dtype)], 1, start, finish)


def _acts_rider(a, b):
    T, K = a.shape
    kh = K // 2

    def copies(ins, outs, send_sems, recv_sems):
        x, y, c = _place()
        pair = [(ins[0].at[:, pl.ds((1 - c) * kh, kh)], outs[0]), (ins[1], outs[1])]
        return [pltpu.make_async_remote_copy(src_ref=src, dst_ref=dst, send_sem=send_sems.at[k], recv_sem=recv_sems.at[k],
                                             device_id=(x, y, 1 - c), device_id_type=MESH)
                for k, (src, dst) in enumerate(pair)]

    def start(*refs):
        for cp in copies(*refs):
            cp.start()

    def finish(*refs):
        for cp in copies(*refs):
            cp.wait()

    return _Rider([a, b], [jax.ShapeDtypeStruct((T, kh), a.dtype), jax.ShapeDtypeStruct(b.shape, b.dtype)], 2,
                  start, finish)


def _share_rider(block):
    K, Ns = block.shape
    kh = K // 2

    def halves(out_ref):
        x, y, c = _place()
        return out_ref.at[pl.ds(c * kh, kh), :], out_ref.at[pl.ds((1 - c) * kh, kh), :], (x, y, 1 - c)

    def start(ins, outs, send_sems, recv_sems):
        mine, _, sibling = halves(outs[0])
        pltpu.make_async_remote_copy(src_ref=mine, dst_ref=mine, send_sem=send_sems.at[0], recv_sem=recv_sems.at[0],
                                     device_id=sibling, device_id_type=MESH).start()

    def finish(ins, outs, send_sems, recv_sems):
        mine, theirs, sibling = halves(outs[0])
        pltpu.make_async_remote_copy(src_ref=theirs, dst_ref=theirs, send_sem=send_sems.at[0], recv_sem=recv_sems.at[0],
                                     device_id=sibling, device_id_type=MESH).wait_recv()
        pltpu.make_async_remote_copy(src_ref=mine, dst_ref=mine, send_sem=send_sems.at[0], recv_sem=recv_sems.at[0],
                                     device_id=sibling, device_id_type=MESH).wait_send()

    return _Rider([block], [jax.ShapeDtypeStruct((K, Ns), block.dtype)], 1, start, finish, aliases={0: 0})


def _chips_rider(pair_sum, rows=None, into=None):
    Q, kh, Ns = pair_sum.shape
    first_row, n_rows = rows or (0, kh)

    def copies(p_ref, got_ref, send_sems, recv_sems):
        x, y, c = _place()
        part = pl.ds(first_row, n_rows)
        out = []
        for j, (fx, fy) in enumerate(CHIP_FLIPS):
            px, py = _flip(x, fx), _flip(y, fy)
            out.append(pltpu.make_async_remote_copy(
                src_ref=p_ref.at[2 * px + py, part, :], dst_ref=got_ref.at[j, part, :], send_sem=send_sems.at[j],
                recv_sem=recv_sems.at[j], device_id=(px, py, c), device_id_type=MESH))
        return out

    def start(ins, outs, send_sems, recv_sems):
        for cp in copies(ins[0], outs[0], send_sems, recv_sems):
            cp.start()

    def finish(ins, outs, send_sems, recv_sems):
        sends = copies(ins[0], outs[0], send_sems, recv_sems)
        for cp in sends:
            cp.wait_recv()
        for cp in sends:
            cp.wait_send()

    got = jax.ShapeDtypeStruct((Q - 1, kh, Ns), pair_sum.dtype)
    if into is None:
        return _Rider([pair_sum], [got], 3, start, finish)
    return _Rider([pair_sum, into], [got], 3, start, finish, aliases={1: 0})


def _mm(a, b, *, grid, a_spec, b_spec, o_spec, o_shape, o_dtype, dims, acc_shape, name, rider=None, zero_rows=0,
        vmem_mb=48):
    nk = grid[2]

    def body(a_ref, b_ref, o_ref, *scratch):
        if zero_rows:
            @pl.when(pl.program_id(0) < zero_rows)
            def _():
                o_ref[...] = jnp.zeros_like(o_ref)

            @pl.when(pl.program_id(0) >= zero_rows)
            def _():
                o_ref[...] = _dot(a_ref[...], b_ref[...], dims).astype(o_ref.dtype)
            return
        part = _dot(a_ref[...], b_ref[...], dims)
        if nk == 1:
            o_ref[...] = part.astype(o_ref.dtype)
            return
        acc_ref, = scratch
        k = pl.program_id(2)

        @pl.when(k == 0)
        def _():
            acc_ref[...] = part

        @pl.when(k > 0)
        def _():
            acc_ref[...] += part

        @pl.when(k == nk - 1)
        def _():
            o_ref[...] = acc_ref[...].astype(o_ref.dtype)

    (out,), rode = _call(
        body, rider, name=name, grid=grid, in_specs=[a_spec, b_spec], out_specs=[o_spec],
        out_shape=[jax.ShapeDtypeStruct(o_shape, o_dtype)],
        scratch_shapes=[] if nk == 1 else [pltpu.VMEM(acc_shape, F32)],
        compiler_params=_params(("parallel", "parallel", "arbitrary") if rider is None else ("arbitrary",) * 3, vmem_mb),
        operands=(a, b))
    return out if rider is None else (out, rode)


def _mm_nn(a, w, *, tm, tn, tk, name, rider=None, cols=None, o_dtype=F32, pad_rows=0):
    T, K = a.shape
    Q, _, Ns = w.shape
    nbs = Ns // tn
    tm = min(tm, T)
    j0, j1 = cols or (0, Q * nbs)
    lead = pad_rows // tm
    return _mm(a, w, grid=(lead + T // tm, j1 - j0, K // tk),
               a_spec=pl.BlockSpec((tm, tk), lambda i, j, k: (jnp.maximum(i - lead, 0), k)),
               b_spec=pl.BlockSpec((None, tk, tn), lambda i, j, k: ((j + j0) // nbs, k, (j + j0) % nbs)),
               o_spec=pl.BlockSpec((tm, tn), lambda i, j, k: (i, j)),
               o_shape=(pad_rows + T, (j1 - j0) * tn), o_dtype=o_dtype, dims=NN, acc_shape=(tm, tn), name=name,
               rider=rider, zero_rows=lead)


def _col_blocks(g, rows, tn, at):
    if g.ndim == 2:
        return pl.BlockSpec((rows, tn), at)
    per = g.shape[2] // tn

    def stacked(*idx):
        r, c = at(*idx)
        return c // per, r, c % per

    return pl.BlockSpec((None, rows, tn), stacked)


def _mm_nt(g, w, *, tm, to, tn, name, rider=None):
    T = g.shape[-2]
    Q, K, Ns = w.shape
    nbs = Ns // tn
    tm = min(tm, T)
    return _mm(g, w, grid=(T // tm, K // to, Q * nbs),
               a_spec=_col_blocks(g, tm, tn, lambda i, j, n: (i, n)),
               b_spec=pl.BlockSpec((None, to, tn), lambda i, j, n: (n // nbs, j, n % nbs)),
               o_spec=pl.BlockSpec((tm, to), lambda i, j, n: (i, j)),
               o_shape=(T, K), o_dtype=F32, dims=NT, acc_shape=(tm, to), name=name, rider=rider)


def _mm_tn(a, g, *, q, tk, tn, tt, name, rider=None):
    T, K = a.shape
    Ns = g.shape[-1] * (g.ndim - 1) // q
    nbs = Ns // tn
    return _mm(a, g, grid=(K // tk, q * nbs, T // tt),
               a_spec=pl.BlockSpec((tt, tk), lambda i, j, t: (t, i)),
               b_spec=_col_blocks(g, tt, tn, lambda i, j, t: (t, j)),
               o_spec=pl.BlockSpec((None, tk, tn), lambda i, j, t: (j // nbs, i, j % nbs)),
               o_shape=(q, K, Ns), o_dtype=BF16, dims=TN, acc_shape=(tk, tn), name=name, rider=rider)


def _mm_tn_add(a, g, part, *, tk, tn, name, rider=None):
    T, K = a.shape
    Q, _, Ns = part.shape
    nbs = Ns // tn

    def body(a_ref, g_ref, p_ref, o_ref):
        o_ref[...] = (_dot(a_ref[...], g_ref[...], TN) + p_ref[...].astype(F32)).astype(o_ref.dtype)

    blk = pl.BlockSpec((None, tk, tn), lambda i, j: (j // nbs, i, j % nbs))
    (out,), rode = _call(
        body, rider, name=name, grid=(K // tk, Q * nbs),
        in_specs=[pl.BlockSpec((T, tk), lambda i, j: (0, i)), _col_blocks(g, T, tn, lambda i, j: (0, j)), blk],
        out_specs=[blk], out_shape=[jax.ShapeDtypeStruct((Q, K, Ns), BF16)], scratch_shapes=[],
        compiler_params=_params(("arbitrary", "arbitrary"), 48), operands=(a, g, part))
    return out, rode


def _ln(u):
    mu = jnp.mean(u, axis=-1, keepdims=True)
    d = u - mu
    r = lax.rsqrt(jnp.mean(d * d, axis=-1, keepdims=True) + EPS)
    return d * r, r


def _ln_bwd(dy, un, r):
    return r * (dy - jnp.mean(dy, axis=-1, keepdims=True) - un * jnp.mean(dy * un, axis=-1, keepdims=True))


def _colsum(v):
    return jnp.sum(v, axis=0, keepdims=True)


def _rowwise(name, fn, bigs, vecs, out_dtypes, n_acc, tm=128, rider=None):
    T, D = bigs[0].shape
    nb, nv, no = len(bigs), len(vecs), len(out_dtypes)

    def body(*refs):
        outs, accs = fn([r[...] for r in refs[:nb]], [r[...] for r in refs[nb:nb + nv]])
        for r, o in zip(refs[nb + nv:nb + nv + no], outs):
            r[...] = o.astype(r.dtype)
        if n_acc:
            acc_ref = refs[nb + nv + no]

            @pl.when(pl.program_id(0) == 0)
            def _():
                acc_ref[...] = jnp.zeros_like(acc_ref)

            for row, a in enumerate(accs):
                acc_ref[row:row + 1, :] += a

    big_spec = pl.BlockSpec((tm, D), lambda i: (i, 0))
    vec_spec = pl.BlockSpec((1, D), lambda i: (0, 0))
    out_shape = [jax.ShapeDtypeStruct((T, D), dt) for dt in out_dtypes]
    out_specs = [big_spec] * no
    if n_acc:
        out_shape.append(jax.ShapeDtypeStruct((8, D), F32))
        out_specs.append(pl.BlockSpec((8, D), lambda i: (0, 0)))
    outs, rode = _call(
        body, rider, name=name, grid=(T // tm,), in_specs=[big_spec] * nb + [vec_spec] * nv,
        out_specs=out_specs, out_shape=out_shape, scratch_shapes=[],
        compiler_params=_params(("arbitrary",), 48), operands=(*bigs, *vecs))
    return outs if rider is None else (outs, rode)


def _to_bf16(w, name, rider=None):
    R, C = w.shape
    tr = _row_tile(R, C)

    def body(w_ref, o_ref):
        o_ref[...] = w_ref[...].astype(o_ref.dtype)

    blk = pl.BlockSpec((tr, C), lambda i: (i, 0))
    return _call(body, rider, name=name, grid=(R // tr,), in_specs=[blk], out_specs=[blk],
                 out_shape=[jax.ShapeDtypeStruct((R, C), BF16)], scratch_shapes=[],
                 compiler_params=_params(("arbitrary",)), operands=(w,))


def _pre_mixer(x, scale1, shift1):
    def fn(b, v):
        xn, _ = _ln(b[0])
        return [xn * (1.0 + v[0]) + v[1]], []
    return _rowwise("pre_mixer", fn, [x], [scale1, shift1], [BF16], 0)[0]


def _post_mixer(mix, x, gate1, g1, b1, scale2, shift2, rider=None):
    def fn(b, v):
        un1, _ = _ln(ALPHA * b[1] + v[0] * b[0])
        x1 = un1 * v[1] + v[2]
        xn1, _ = _ln(x1)
        return [x1, xn1 * (1.0 + v[3]) + v[4]], []
    return _rowwise("post_mixer", fn, [mix, x], [gate1, g1, b1, scale2, shift2], [F32, BF16], 0, rider=rider)


def _loss_head(f, x1, tgt, gate2, g2, b2):
    def fn(b, v):
        ff, xx, tt = b
        d_model = ff.shape[-1]
        un2, r2 = _ln(ALPHA * xx + v[0] * ff)
        err = un2 * v[1] + v[2] - tt
        dy = err * (1.0 / d_model)
        du2 = _ln_bwd(dy * v[1], un2, r2)
        return [du2, du2 * v[0]], [_colsum(dy * un2), _colsum(dy), _colsum(du2 * ff), _colsum(err * err)]
    return _rowwise("loss_head", fn, [f, x1, tgt], [gate2, g2, b2], [F32, BF16], 4)


def _mid_bwd(dh2, du2, x1, mix, x, gate1, g1, scale2, rider=None):
    def fn(b, v):
        dh, du, xx1, mm, xx = b
        xn1, r1n = _ln(xx1)
        dx1 = ALPHA * du + _ln_bwd(dh * (1.0 + v[2]), xn1, r1n)
        un1, r1 = _ln(ALPHA * xx + v[0] * mm)
        du1 = _ln_bwd(dx1 * v[1], un1, r1)
        return [du1, du1 * v[0]], [_colsum(dh * xn1), _colsum(dh), _colsum(dx1 * un1), _colsum(dx1),
                                   _colsum(du1 * mm)]
    return _rowwise("mid_bwd", fn, [dh2, du2, x1, mix, x], [gate1, g1, scale2], [F32, BF16], 5, rider=rider)


def _first_bwd(dh1, du1, x, scale1, rider=None):
    def fn(b, v):
        dh, du, xx = b
        xn, r0 = _ln(xx)
        return [ALPHA * du + _ln_bwd(dh * (1.0 + v[0]), xn, r0)], [_colsum(dh * xn), _colsum(dh)]
    return _rowwise("first_bwd", fn, [dh1, du1, x], [scale1], [F32], 2, rider=rider)


def _ffn_in_swiglu(h2, w, *, tm, tn, rider=None):
    T, K = h2.shape
    Q, _, Ns = w.shape
    nbs = Ns // tn
    half = Q * nbs // 2
    tm = min(tm, T)

    def body(a_ref, wg_ref, wu_ref, g_ref, u_ref, act_ref):
        a = a_ref[...]
        g, u = _dot(a, wg_ref[...]), _dot(a, wu_ref[...])
        g_ref[...] = g.astype(g_ref.dtype)
        u_ref[...] = u.astype(u_ref.dtype)
        act_ref[...] = (g * jax.nn.sigmoid(g) * u).astype(act_ref.dtype)

    cols = lambda first: pl.BlockSpec((None, K, tn), lambda i, j: ((j + first) // nbs, 0, (j + first) % nbs))
    blk = pl.BlockSpec((tm, tn), lambda i, j: (i, j))
    return _call(
        body, rider, name="ffn_in", grid=(T // tm, half),
        in_specs=[pl.BlockSpec((tm, K), lambda i, j: (i, 0)), cols(0), cols(half)], out_specs=[blk] * 3,
        out_shape=[jax.ShapeDtypeStruct((T, half * tn), BF16)] * 3, scratch_shapes=[],
        compiler_params=_params(("arbitrary", "arbitrary"), 48), operands=(h2, w, w))


def _d_act_swiglu(df, w, gate, up, *, tm, to, rider=None):
    T, N = df.shape
    F = w.shape[1]
    tm = min(tm, T)

    def body(df_ref, w_ref, g_ref, u_ref, o_ref):
        d = _dot(df_ref[...], w_ref[...], NT)
        g = g_ref[...].astype(F32)
        s = jax.nn.sigmoid(g)
        o_ref[0] = (d * u_ref[...].astype(F32) * s * (1.0 + g * (1.0 - s))).astype(o_ref.dtype)
        o_ref[1] = (d * g * s).astype(o_ref.dtype)

    blk = pl.BlockSpec((tm, to), lambda i, j: (i, j))
    return _call(
        body, rider, name="d_act", grid=(T // tm, F // to),
        in_specs=[pl.BlockSpec((tm, N), lambda i, j: (i, 0)), pl.BlockSpec((None, to, N), lambda i, j: (0, j, 0)), blk, blk],
        out_specs=[pl.BlockSpec((2, tm, to), lambda i, j: (0, i, j))],
        out_shape=[jax.ShapeDtypeStruct((2, T, F), BF16)], scratch_shapes=[],
        compiler_params=_params(("arbitrary", "arbitrary"), 48), operands=(df, w, gate, up))


PAIR = 2


def _fill_table(table_ref, band_ref):
    table_ref[...] = jnp.full(table_ref.shape, NEG, F32)
    for e in range(PAIR):
        for g in range(QG):
            table_ref[e, g * CHUNK:(g + 1) * CHUNK, g * CHUNK:g * CHUNK + BAND] = band_ref[e]


def _attn_probs(q_ref, k_ref, bias_ref, e, step):
    start = pl.multiple_of(step * QROWS, QROWS)
    lanes = pl.ds(e * HD_A, HD_A)
    s = _dot(q_ref[:, lanes], k_ref[pl.ds(start + ZPAD - KPAD, UNION), lanes], NT) * (HD_A ** -0.5) + bias_ref[e]
    col = lax.broadcasted_iota(jnp.int32, s.shape, 1)
    s = jnp.where(col + start >= KPAD, s, NEG)
    p = jnp.exp(s - jnp.max(s, axis=-1, keepdims=True))
    return p / jnp.sum(p, axis=-1, keepdims=True), start


def _attn_specs(T, n_pairs):
    wide = PAIR * HD_A
    per_step = pl.BlockSpec((QROWS, wide), lambda hp, n: (n, hp))
    queries = pl.BlockSpec((QROWS, wide), lambda hp, n: (n + ZPAD // QROWS, hp))
    keys = pl.BlockSpec((ZPAD + T, wide), lambda hp, n: (0, n_pairs + hp))
    values = pl.BlockSpec((ZPAD + T, wide), lambda hp, n: (0, 2 * n_pairs + hp))
    grads = pl.BlockSpec((T, wide), lambda hp, n: (0, hp))
    table = pl.BlockSpec((PAIR, CHUNK, BAND), lambda hp, n: (hp, 0, 0))
    vec = pl.BlockSpec((1, wide), lambda hp, n: (0, hp))
    return per_step, queries, keys, values, grads, table, vec


def _probs_spec():
    return pl.BlockSpec((PAIR, QROWS, UNION), lambda hp, n: (hp, n, 0))


def _attn_fwd(qkv, bias, gain, rider=None):
    T = qkv.shape[0] - ZPAD
    W = gain.shape[1]
    n_pairs = W // (PAIR * HD_A)

    def body(q_ref, k_ref, v_ref, band_ref, gain_ref, o_ref, p_ref, table_ref):
        @pl.when(pl.program_id(1) == 0)
        def _():
            _fill_table(table_ref, band_ref)

        for e in range(PAIR):
            lanes = pl.ds(e * HD_A, HD_A)
            p, start = _attn_probs(q_ref, k_ref, table_ref, e, pl.program_id(1))
            p_ref[e] = p.astype(p_ref.dtype)
            o = _dot(p_ref[e], v_ref[pl.ds(start + ZPAD - KPAD, UNION), lanes])
            rr = lax.rsqrt(jnp.mean(o * o, axis=-1, keepdims=True) + EPS)
            o_ref[:, lanes] = (o * rr * gain_ref[:, lanes]).astype(o_ref.dtype)

    per_step, queries, keys, values, _, table, vec = _attn_specs(T, n_pairs)
    return _call(
        body, rider, name="attn_fwd", grid=(n_pairs, T // QROWS), in_specs=[queries, keys, values, table, vec],
        out_specs=[per_step, _probs_spec()],
        out_shape=[jax.ShapeDtypeStruct((T, W), BF16), jax.ShapeDtypeStruct((n_pairs * PAIR, T, UNION), BF16)],
        scratch_shapes=[pltpu.VMEM((PAIR, QROWS, UNION), F32)],
        compiler_params=_params(("arbitrary", "arbitrary"), 40), operands=(qkv, qkv, qkv, bias, gain))


def _attn_bwd(qkv, probs, gain, dmixin, rider=None):
    T = qkv.shape[0] - ZPAD
    W = gain.shape[1]
    n_pairs = W // (PAIR * HD_A)
    scale = HD_A ** -0.5

    def body(q_ref, k_ref, v_ref, p_ref, gain_ref, don_ref, dq_ref, dkb_ref, dvb_ref, dband_ref, dgain_ref,
             dtable_ref, dk_ref, dv_ref):
        n = pl.program_id(1)

        @pl.when(n == 0)
        def _():
            dk_ref[...] = jnp.zeros_like(dk_ref)
            dv_ref[...] = jnp.zeros_like(dv_ref)
            dtable_ref[...] = jnp.zeros_like(dtable_ref)
            dgain_ref[...] = jnp.zeros_like(dgain_ref)

        for e in range(PAIR):
            lanes = pl.ds(e * HD_A, HD_A)
            start = pl.multiple_of(n * QROWS, QROWS)
            keys, in_qkv = pl.ds(start, UNION), pl.ds(start + ZPAD - KPAD, UNION)
            pb = p_ref[e]
            p = pb.astype(F32)
            vb = v_ref[in_qkv, lanes]
            o = _dot(pb, vb)
            rr = lax.rsqrt(jnp.mean(o * o, axis=-1, keepdims=True) + EPS)
            on = o * rr
            d_on = don_ref[:, lanes]
            dgain_ref[:, lanes] += _colsum(d_on * on)
            dyo = d_on * gain_ref[:, lanes]
            do = rr * (dyo - on * jnp.mean(dyo * on, axis=-1, keepdims=True))
            dob = do.astype(BF16)
            dp = _dot(dob, vb, NT)
            ds = p * (dp - jnp.sum(do * o, axis=-1, keepdims=True))
            dtable_ref[e] += ds
            dsb = ds.astype(BF16)
            dq_ref[:, lanes] = (_dot(dsb, k_ref[in_qkv, lanes]) * scale).astype(dq_ref.dtype)
            dk_ref[keys, lanes] += _dot(dsb, q_ref[:, lanes], TN) * scale
            dv_ref[keys, lanes] += _dot(pb, dob, TN)

        @pl.when(n == T // QROWS - 1)
        def _():
            for e in range(PAIR):
                dband_ref[e] = sum(dtable_ref[e, g * CHUNK:(g + 1) * CHUNK, g * CHUNK:g * CHUNK + BAND]
                                   for g in range(QG))
            dkb_ref[...] = dk_ref[KPAD:, :].astype(dkb_ref.dtype)
            dvb_ref[...] = dv_ref[KPAD:, :].astype(dvb_ref.dtype)

    per_step, queries, keys, values, grads, table, vec = _attn_specs(T, n_pairs)
    H = n_pairs * PAIR
    return _call(
        body, rider, name="attn_bwd", grid=(n_pairs, T // QROWS),
        in_specs=[queries, keys, values, _probs_spec(), vec, per_step],
        out_specs=[per_step, grads, grads, table, vec],
        out_shape=[jax.ShapeDtypeStruct((T, W), BF16)] * 3 + [jax.ShapeDtypeStruct((H, CHUNK, BAND), F32),
                                                              jax.ShapeDtypeStruct((1, W), F32)],
        scratch_shapes=[pltpu.VMEM((PAIR, QROWS, UNION), F32)] + [pltpu.VMEM((KPAD + T, PAIR * HD_A), F32)] * 2,
        compiler_params=_params(("arbitrary", "arbitrary"), 40),
        operands=(qkv, qkv, qkv, probs, gain, dmixin))


N_DIAG = CHUNK + BAND - 1


def _bias_band(rel_bias):
    H = rel_bias.shape[0]
    idx = np.clip(BAND - 1 - np.arange(N_DIAG), -MAX_REL, MAX_REL) + MAX_REL
    rolled = rel_bias[:, idx[(np.arange(N_DIAG) + CHUNK - 1) % N_DIAG]]
    flat = jnp.broadcast_to(rolled[:, None, :], (H, CHUNK, N_DIAG)).reshape(H, CHUNK * N_DIAG)
    return flat[:, :CHUNK * (N_DIAG - 1)].reshape(H, CHUNK, N_DIAG - 1)[:, :, :BAND]


def _bias_band_grad(dband):
    H = dband.shape[0]
    skew = jnp.pad(dband, ((0, 0), (0, 0), (CHUNK - 1, 0))).reshape(H, CHUNK * N_DIAG)
    skew = jnp.pad(skew, ((0, 0), (0, CHUNK))).reshape(H, CHUNK, N_DIAG + 1)
    diag = jnp.sum(skew, axis=1)[:, :N_DIAG]
    n_far = BAND - MAX_REL
    far = jnp.sum(diag[:, :n_far], axis=1, keepdims=True)
    near = diag[:, n_far:][:, ::-1]
    zeros = jnp.zeros((H, MAX_REL - (CHUNK - 1)), F32)
    return jnp.concatenate([zeros, near, far], axis=1)


def _tri(n, lower):
    r = lax.broadcasted_iota(jnp.int32, (n, n), 0)
    c = lax.broadcasted_iota(jnp.int32, (n, n), 1)
    return jnp.where((c <= r) if lower else (c >= r), 1.0, 0.0).astype(F32)


def _hgrn_gates(zq_ref, zf_ref, lbl_ref, q_s, k_s, b_s):
    lb = jax.nn.sigmoid(lbl_ref[0:1, :] - lbl_ref[1:2, :])
    zq = zq_ref[...]
    sig = jax.nn.sigmoid(zf_ref[...])
    f = lb + (1.0 - lb) * sig
    sq = jax.nn.sigmoid(zq)
    q_s[...] = zq * sq
    k_s[...] = 1.0 - f
    b_s[...] = _dot(_tri(CHUNK, True), jnp.log(f), precision=HIGHEST)
    return lb, sig, f, sq


def _sub_rows(i):
    return pl.ds(i * SUB, SUB)


def _row_mask(s):
    return lax.broadcasted_iota(jnp.int32, (SUB, HD_B), 0) >= s


def _decay_from(b_sub, b_row, s):
    return jnp.where(_row_mask(s), jnp.exp(jnp.minimum(b_sub - b_row, 0.0)), 0.0)


def _hgrn_fwd(proj, lb_logits, gnorm_g, rider=None):
    T = proj.shape[0]
    nC = T // CHUNK
    W = lb_logits.shape[1]
    G = W // HD_B // HGRN_HEADS
    col0 = (proj.shape[1] - 4 * W) // (HD_B * HGRN_HEADS)
    wide = HGRN_HEADS * HD_B

    def body(*refs):
        @pl.when(pl.program_id(1) == 0)
        def _():
            refs[9][...] = jnp.zeros_like(refs[9])

        for h in range(HGRN_HEADS):
            lanes = pl.ds(h * HD_B, HD_B)
            one_head(*[r.at[:, lanes] for r in refs[:5]], refs[5], *[r.at[:, lanes] for r in refs[6:8]],
                     *[r.at[h] for r in refs[8:]])

    def one_head(zq_ref, zf_ref, xi_ref, zg_ref, lbl_ref, gn_ref, mix_ref, o_ref, stall_ref, st_ref, q_s, k_s, b_s, acc_s):
        _hgrn_gates(zq_ref, zf_ref, lbl_ref, q_s, k_s, b_s)
        q, k, b = q_s[...], k_s[...], b_s[...]
        st = st_ref[...]
        stall_ref[...] = st
        b_last = b_s[CHUNK - 1:CHUNK, :]
        acc_s[...] = _dot((q * jnp.exp(b)).astype(BF16), st.astype(BF16), NT)
        for i in range(CHUNK // SUB):
            rows = _sub_rows(i)
            q_i, b_i = q_s[rows, :], b_s[rows, :]
            acc = jnp.zeros((SUB, HD_B), F32)
            if i:
                past = pl.ds(0, i * SUB)
                b_ref = b_s[i * SUB - 1:i * SUB, :]
                qs = (q_i * jnp.exp(b_i - b_ref)).astype(BF16)
                ks = (k_s[past, :] * jnp.exp(b_ref - b_s[past, :])).astype(BF16)
                acc += _dot(_dot(qs, ks, NT).astype(BF16), xi_ref[past, :].astype(BF16))
            for s in range(SUB):
                row = pl.ds(i * SUB + s, 1)
                w = q_i * _decay_from(b_i, b_s[row, :], s)
                acc += jnp.sum(w * k_s[row, :], axis=-1, keepdims=True) * xi_ref[row, :]
            acc_s[rows, :] += acc
        o = acc_s[...]
        kd = (k * jnp.exp(b_last - b)).astype(BF16)
        st_ref[...] = st * jnp.exp(b_last) + _dot(xi_ref[...].astype(BF16), kd, TN)
        o_ref[...] = o
        zg = zg_ref[...]
        rr = lax.rsqrt(jnp.mean(o * o, axis=-1, keepdims=True) + EPS)
        mix_ref[...] = (o * rr * gn_ref[...] * (zg * jax.nn.sigmoid(zg))).astype(mix_ref.dtype)

    col = lambda part: pl.BlockSpec((CHUNK, wide), lambda g, n: (n, col0 + part * G + g))
    out_blk = pl.BlockSpec((CHUNK, wide), lambda g, n: (n, g))
    tile = pltpu.VMEM((HGRN_HEADS, CHUNK, HD_B), F32)
    return _call(
        body, rider, name="hgrn_fwd", grid=(G, nC),
        in_specs=[col(0), col(1), col(2), col(3), pl.BlockSpec((2, wide), lambda g, n: (0, g)),
                  pl.BlockSpec((1, HD_B), lambda g, n: (0, 0))],
        out_specs=[out_blk, out_blk, pl.BlockSpec((HGRN_HEADS, None, HD_B, HD_B), lambda g, n: (g, n, 0, 0))],
        out_shape=[jax.ShapeDtypeStruct((T, W), BF16), jax.ShapeDtypeStruct((T, W), F32),
                   jax.ShapeDtypeStruct((G * HGRN_HEADS, nC, HD_B, HD_B), F32)],
        scratch_shapes=[pltpu.VMEM((HGRN_HEADS, HD_B, HD_B), F32), tile, tile, tile, tile],
        compiler_params=_params(("arbitrary", "arbitrary")),
        operands=(proj, proj, proj, proj, lb_logits, gnorm_g))


def _hgrn_bwd(proj, lb_logits, gnorm_g, o_b, st_all, dmixin, d_attn, rider=None):
    T = proj.shape[0]
    nC = T // CHUNK
    W = lb_logits.shape[1]
    wa = d_attn[0].shape[1]
    assert W == HGRN_HEADS * HD_B, "one grid step takes every head: it writes whole rows of d proj"
    G = W // HD_B // HGRN_HEADS
    wide = HGRN_HEADS * HD_B
    col0 = (proj.shape[1] - 4 * W) // wide
    dcol0 = (dmixin.shape[1] - W) // wide

    def body(*refs):
        g, n = pl.program_id(0), pl.program_id(1)
        dproj_ref, dl0_ref, dgn_ref, dst_ref = refs[12:16]
        for i in range(3):
            dproj_ref[:, i * wa:(i + 1) * wa] = refs[9 + i][...]

        @pl.when(n == 0)
        def _():
            dst_ref[...] = jnp.zeros_like(dst_ref)
            dl0_ref[...] = jnp.zeros_like(dl0_ref)

        @pl.when((n == 0) & (g == 0))
        def _():
            dgn_ref[...] = jnp.zeros_like(dgn_ref)

        for h in range(HGRN_HEADS):
            lanes = pl.ds(h * HD_B, HD_B)
            cut = lambda r: r.at[:, lanes]
            parts = [dproj_ref.at[:, pl.ds(3 * wa + part * W + h * HD_B, HD_B)] for part in range(4)]
            one_head(*[cut(r) for r in refs[:5]], refs[5], cut(refs[6]), refs[7].at[h], cut(refs[8]),
                     *parts, cut(dl0_ref), dgn_ref, *[r.at[h] for r in refs[15:]])

    def one_head(zq_ref, zf_ref, xi_ref, zg_ref, lbl_ref, gn_ref, o_ref, st_ref, dout_ref,
                 dzq_ref, dzf_ref, dxi_ref, dzg_ref, dl0_ref, dgn_ref, dst_ref, q_s, k_s, b_s, do_s, dq_s, dk_s, di_s):
        lb, sig, f, sq = _hgrn_gates(zq_ref, zf_ref, lbl_ref, q_s, k_s, b_s)
        q, k, b = q_s[...], k_s[...], b_s[...]
        zg, o, dout = zg_ref[...], o_ref[...], dout_ref[...]
        sg = jax.nn.sigmoid(zg)
        rr = lax.rsqrt(jnp.mean(o * o, axis=-1, keepdims=True) + EPS)
        on = o * rr
        gn = gn_ref[...]
        dzg_ref[...] = (dout * on * gn * sg * (1.0 + zg * (1.0 - sg))).astype(dzg_ref.dtype)
        d_on = dout * zg * sg
        dgn_ref[...] += _colsum(d_on * on)
        d_on = d_on * gn
        do = rr * (d_on - on * jnp.mean(d_on * on, axis=-1, keepdims=True))
        do_s[...] = do
        dob = do.astype(BF16)
        st, dst = st_ref[...], dst_ref[...]
        b_last = b_s[CHUNK - 1:CHUNK, :]
        eb, e_last, k_dec = jnp.exp(b), jnp.exp(b_last), jnp.exp(b_last - b)
        qt, kd = q * eb, k * k_dec
        dstb = dst.astype(BF16)
        xib = xi_ref[...].astype(BF16)
        d_kd = _dot(xib, dstb)
        dq_s[...] = _dot(dob, st.astype(BF16)) * eb
        dk_s[...] = d_kd * k_dec
        di_s[...] = _dot(kd.astype(BF16), dstb, NT)
        d_b_last = e_last * _colsum(st * dst) + _colsum(d_kd * kd)
        dst_ref[...] = _dot(dob, qt.astype(BF16), TN) + dst * e_last
        for i in range(CHUNK // SUB):
            rows = _sub_rows(i)
            q_i, b_i, do_i = q_s[rows, :], b_s[rows, :], do_s[rows, :]
            dq_i = jnp.zeros((SUB, HD_B), F32)
            if i:
                past = pl.ds(0, i * SUB)
                b_ref = b_s[i * SUB - 1:i * SUB, :]
                e_q, e_k = jnp.exp(b_i - b_ref), jnp.exp(b_ref - b_s[past, :])
                qs, ks = (q_i * e_q).astype(BF16), (k_s[past, :] * e_k).astype(BF16)
                xi_p, do_b = xi_ref[past, :].astype(BF16), do_i.astype(BF16)
                di_s[past, :] += _dot(_dot(ks, qs, NT).astype(BF16), do_b)
                dq_i += _dot(_dot(do_b, xi_p, NT).astype(BF16), ks) * e_q
                dk_s[past, :] += _dot(_dot(xi_p, do_b, NT).astype(BF16), qs) * e_k
            for s in range(SUB):
                row = pl.ds(i * SUB + s, 1)
                k_row, i_row = k_s[row, :], xi_ref[row, :]
                e = _decay_from(b_i, b_s[row, :], s)
                w = q_i * e
                a_col = jnp.sum(w * k_row, axis=-1, keepdims=True)
                da_col = jnp.sum(do_i * i_row, axis=-1, keepdims=True)
                di_s[row, :] += _colsum(a_col * do_i)
                dq_i += da_col * e * k_row
                dk_s[row, :] += _colsum(da_col * w)
            dq_s[rows, :] += dq_i
        dq, dk = dq_s[...], dk_s[...]
        db = q * dq - k * dk
        is_last = lax.broadcasted_iota(jnp.int32, (CHUNK, HD_B), 0) == CHUNK - 1
        db = db + jnp.where(is_last, d_b_last, 0.0)
        df = _dot(_tri(CHUNK, False), db, precision=HIGHEST) / f - dk
        dzf_ref[...] = (df * (1.0 - lb) * sig * (1.0 - sig)).astype(dzf_ref.dtype)
        dl0_ref[...] += _colsum(df * (1.0 - sig)) * (lb * (1.0 - lb))
        zq = zq_ref[...]
        dzq_ref[...] = (dq * sq * (1.0 + zq * (1.0 - sq))).astype(dzq_ref.dtype)
        dxi_ref[...] = di_s[...].astype(dxi_ref.dtype)

    rev = lambda n: nC - 1 - n
    col = lambda part: pl.BlockSpec((CHUNK, wide), lambda g, n: (rev(n), col0 + part * G + g))
    blk = pl.BlockSpec((CHUNK, wide), lambda g, n: (rev(n), g))
    tile = pltpu.VMEM((HGRN_HEADS, CHUNK, HD_B), F32)
    rows = lambda width: pl.BlockSpec((CHUNK, width), lambda g, n: (rev(n), 0))
    return _call(
        body, rider, name="hgrn_bwd", grid=(G, nC),
        in_specs=[col(0), col(1), col(2), col(3), pl.BlockSpec((2, wide), lambda g, n: (0, g)),
                  pl.BlockSpec((1, HD_B), lambda g, n: (0, 0)), blk,
                  pl.BlockSpec((HGRN_HEADS, None, HD_B, HD_B), lambda g, n: (g, rev(n), 0, 0)),
                  pl.BlockSpec((CHUNK, wide), lambda g, n: (rev(n), dcol0 + g)), rows(wa), rows(wa), rows(wa)],
        out_specs=[rows(3 * wa + 4 * W), pl.BlockSpec((1, wide), lambda g, n: (0, g)),
                   pl.BlockSpec((1, HD_B), lambda g, n: (0, 0))],
        out_shape=[jax.ShapeDtypeStruct((T, 3 * wa + 4 * W), BF16), jax.ShapeDtypeStruct((1, W), F32),
                   jax.ShapeDtypeStruct((1, HD_B), F32)],
        scratch_shapes=[pltpu.VMEM((HGRN_HEADS, HD_B, HD_B), F32)] + [tile] * 7,
        compiler_params=_params(("arbitrary", "arbitrary")),
        operands=(proj, proj, proj, proj, lb_logits, gnorm_g, o_b, st_all, dmixin, *d_attn))


def _adamw_math(g, w, m, v):
    m = B1 * m + (1.0 - B1) * g
    v = B2 * v + (1.0 - B2) * (g * g)
    m_hat = m / (1.0 - B1 ** STEP)
    v_hat = v / (1.0 - B2 ** STEP)
    return -LR * (m_hat / (jnp.sqrt(v_hat) + ADAM_EPS) + WD * w), m, v


def _adamw(g, w, m, v, name):
    R, C = g.shape
    tr = _row_tile(R, C)

    def body(g_ref, w_ref, m_ref, v_ref, go_ref, d_ref, mo_ref, vo_ref):
        g = g_ref[...]
        go_ref[...] = g
        d_ref[...], mo_ref[...], vo_ref[...] = _adamw_math(g, w_ref[...], m_ref[...], v_ref[...])

    blk = pl.BlockSpec((tr, C), lambda i: (i, 0))
    return pl.pallas_call(
        body, name=name, grid=(R // tr,), in_specs=[blk] * 4, out_specs=[blk] * 4,
        out_shape=[jax.ShapeDtypeStruct((R, C), F32)] * 4, compiler_params=_params(("parallel",), 40),
    )(g, w, m, v)


def _sum_pair(g_full, from_sibling, sel, name):
    Q, K, Ns = g_full.shape
    kh = K // 2
    tr = _row_tile(kh, Ns)
    nh = kh // tr

    def body(sel_ref, a_ref, b_ref, o_ref):
        o_ref[...] = (a_ref[...].astype(F32) + b_ref[...].astype(F32)).astype(o_ref.dtype)

    return pl.pallas_call(
        body, name=name,
        grid_spec=pltpu.PrefetchScalarGridSpec(
            num_scalar_prefetch=1, grid=(Q, nh),
            in_specs=[pl.BlockSpec((None, tr, Ns), lambda q, i, sel: (q, sel[1] * nh + i, 0)),
                      pl.BlockSpec((None, tr, Ns), lambda q, i, sel: (q, i, 0))],
            out_specs=pl.BlockSpec((None, tr, Ns), lambda q, i, sel: (q, i, 0))),
        out_shape=jax.ShapeDtypeStruct((Q, kh, Ns), BF16), compiler_params=_params(("parallel", "parallel")),
    )(sel, g_full, from_sibling)


def _sum_chips(pair_sum, from_chips, sel, name):
    Q, kh, Ns = pair_sum.shape
    tr = _row_tile(kh, Ns)
    nh = kh // tr

    def body(sel_ref, a_ref, b0_ref, b1_ref, b2_ref, o_ref):
        up = lambda r: r[...].astype(F32)
        o_ref[...] = ((up(a_ref) + up(b0_ref)) + up(b1_ref)) + up(b2_ref)

    recv = lambda k: pl.BlockSpec((None, tr, Ns), lambda i, sel: (k, i, 0))
    return pl.pallas_call(
        body, name=name,
        grid_spec=pltpu.PrefetchScalarGridSpec(
            num_scalar_prefetch=1, grid=(nh,),
            in_specs=[pl.BlockSpec((None, tr, Ns), lambda i, sel: (sel[0], i, 0)), recv(0), recv(1), recv(2)],
            out_specs=pl.BlockSpec((tr, Ns), lambda i, sel: (sel[1] * nh + i, 0))),
        out_shape=jax.ShapeDtypeStruct((2 * kh, Ns), F32), compiler_params=_params(("parallel",)),
    )(sel, pair_sum, from_chips, from_chips, from_chips)


def _gather_small(v, name):
    R, L = v.shape

    def body(v_ref, out_ref, send_sems, recv_sems):
        x, y, c = _place()
        me = 4 * x + 2 * y + c
        out_ref[me] = v_ref[...]
        peers = [(_flip(x, k >> 2 & 1), _flip(y, k >> 1 & 1), _flip(c, k & 1)) for k in range(1, N_DEV)]

        def copy(k, row, to):
            return pltpu.make_async_remote_copy(src_ref=v_ref, dst_ref=out_ref.at[row], send_sem=send_sems.at[k],
                                                recv_sem=recv_sems.at[k], device_id=to, device_id_type=MESH)

        sends = [copy(k, me, peer) for k, peer in enumerate(peers)]
        for cp in sends:
            cp.start()
        for k, (px, py, pc) in enumerate(peers):
            copy(k, 4 * px + 2 * py + pc, (x, y, c)).wait_recv()
        for cp in sends:
            cp.wait_send()

    vmem = pl.BlockSpec(memory_space=pltpu.VMEM)
    return pl.pallas_call(
        body, name=name, in_specs=[vmem], out_specs=vmem, out_shape=jax.ShapeDtypeStruct((N_DEV, R, L), F32),
        scratch_shapes=[pltpu.SemaphoreType.DMA((N_DEV - 1,)), pltpu.SemaphoreType.DMA((N_DEV - 1,))],
    )(v)


def _silu(v):
    return v * jax.nn.sigmoid(v)


def _ada_fwd(c_all, w_ada, tn=512):
    M, D = c_all.shape
    Ns = w_ada.shape[1]

    def body(c_ref, w_ref, o_ref):
        o_ref[...] = _dot(_silu(c_ref[...]).astype(BF16), w_ref[...].astype(BF16))

    return pl.pallas_call(
        body, name="ada_fwd", grid=(Ns // tn,),
        in_specs=[pl.BlockSpec((M, D), lambda j: (0, 0)), pl.BlockSpec((D, tn), lambda j: (0, j))],
        out_specs=pl.BlockSpec((M, tn), lambda j: (0, j)), out_shape=jax.ShapeDtypeStruct((M, Ns), F32),
        compiler_params=_params(("parallel",)),
    )(c_all, w_ada)


def _ada_bwd(c_all, dmod, w, m, v, tk=256, tn=1536):
    M, D = c_all.shape
    Ns = dmod.shape[1]

    def body(c_ref, d_ref, w_ref, m_ref, v_ref, g_ref, dl_ref, mo_ref, vo_ref):
        g = _dot(_silu(c_ref[...]).astype(BF16), d_ref[...].astype(BF16), TN)
        g_ref[...] = g
        dl_ref[...], mo_ref[...], vo_ref[...] = _adamw_math(g, w_ref[...], m_ref[...], v_ref[...])

    blk = pl.BlockSpec((tk, tn), lambda i, j: (i, j))
    return pl.pallas_call(
        body, name="ada_bwd", grid=(D // tk, Ns // tn),
        in_specs=[pl.BlockSpec((M, tk), lambda i, j: (0, i)), pl.BlockSpec((M, tn), lambda i, j: (0, j)), blk, blk, blk],
        out_specs=[blk] * 4, out_shape=[jax.ShapeDtypeStruct((D, Ns), F32)] * 4,
        compiler_params=_params(("parallel", "parallel"), 40),
    )(c_all, dmod, w, m, v)


def _small_update(g_all, w, m, v):
    R, L = w.shape

    def body(g_ref, w_ref, m_ref, v_ref, go_ref, d_ref, mo_ref, vo_ref):
        g = g_ref[0]
        for d in range(1, N_DEV):
            g = g + g_ref[d]
        go_ref[...] = g
        d_ref[...], mo_ref[...], vo_ref[...] = _adamw_math(g, w_ref[...], m_ref[...], v_ref[...])

    return pl.pallas_call(body, name="small_update", out_shape=[jax.ShapeDtypeStruct((R, L), F32)] * 4)(g_all, w, m, v)


def _pack(parts, rows):
    flat = jnp.concatenate([p.reshape(-1) for p in parts])
    return jnp.pad(flat, (0, rows * 128 - flat.shape[0])).reshape(rows, 128)


def _unpack(packed, shapes):
    flat, out, at = packed.reshape(-1), [], 0
    for shp in shapes:
        size = 1
        for d in shp:
            size *= d
        out.append(flat[at:at + size].reshape(shp))
        at += size
    return out


def _layer(x, tgt, mod, wts, rel_bias, attn_norm_g, lb_logits, gnorm_g, ln1_g, ln1_b, ln2_g, ln2_b, place=None):
    T, D = x.shape
    aw = attn_norm_g.shape[1]
    shift1, scale1, gate1, shift2, scale2, gate2 = [mod[i:i + 1] for i in range(6)]

    def gather(n, rows=None, into=None, before=None, last=True):
        return None if place is None else _gather_rider(wts[n], rows, None if into is None else into[0], before, last)

    def gathered(n, rode):
        return wts[n] if place is None else lax.dynamic_update_index_in_dim(rode[0], wts[n], place[0], 0)

    def blocks(g):
        return g.reshape(N_CHIPS, -1, g.shape[2])

    def to_sibling(g):
        return None if place is None else _pair_rider(g)

    def pair_sum(n, g, rode=None):
        if place is None:
            return g
        rode = _alone(_pair_rider(g), n + "_send_pair") if rode is None else rode
        return _sum_pair(g, rode[0], place[1], n + "_sum_pair")

    def to_chips(p, rows=None, into=None):
        return None if place is None else _chips_rider(p, rows, None if into is None else into[0])

    def summed(n, p, rode):
        return p if place is None else _sum_chips(p, rode[0], place[1], n + "_sum_chips")

    def to_both(block):
        return None if place is None else _share_rider(block)

    def carrying(mm, *args, rider, **kw):
        return mm(*args, rider=rider, **kw) if rider is not None else (mm(*args, **kw), None)

    def to_sibling_acts(a, b):
        return None if place is None else _acts_rider(a, b)

    def pair_grad(name, a, b, tn, rider, arrived=None, late_rider=None):
        if place is None:
            return _mm_tn(a, b, q=N_CHIPS, tk=512, tn=tn, tt=T, name=name), None
        kh = a.shape[1] // 2
        mine = lax.dynamic_slice_in_dim(a, place[1][1] * kh, kh, axis=1)
        part, rode = carrying(_mm_tn, mine, b, q=N_CHIPS, tk=512, tn=tn, tt=T, name=name + "_own",
                              rider=_join(None if arrived else _acts_rider(a, b), rider))
        (a_sib, b_sib), rode = arrived or rode[:2], rode if arrived else rode[2:]
        out, late = _mm_tn_add(a_sib, b_sib, part, tk=512, tn=tn, name=name + "_sib", rider=late_rider)
        return out, (rode or []) + late

    w_in = gathered("w_in", None if place is None else [wts["w_in_gathered"]])
    h1 = _pre_mixer(x, scale1, shift1)
    n_qkv = 3 * aw // 256
    kh_o, kh_f, kh_out = [wts[n].shape[-2] // 2 for n in ("w_o", "w_ffn_in", "w_ffn_out")]
    o_cut, f_cuts, out_cut = 3 * kh_o // 8, (7 * kh_f // 16, 7 * kh_f // 8), kh_out // 11
    qkv, rode = carrying(_mm_nn, h1, w_in, tm=ZPAD, tn=256, tk=D, name="proj_qkv", cols=(0, n_qkv), o_dtype=BF16,
                         pad_rows=ZPAD, rider=gather("w_o", (0, o_cut), last=False))
    proj, rode = carrying(_mm_nn, h1, w_in, tm=2048, tn=256, tk=D, name="proj_rec",
                          cols=(n_qkv, N_CHIPS * w_in.shape[2] // 256),
                          rider=gather("w_o", (o_cut, kh_o - o_cut), rode, before=(0, o_cut)))
    w_o3 = gathered("w_o", rode).reshape(1, D, D)
    bias = _bias_band(rel_bias)
    (mix_a, probs), rode = _attn_fwd(qkv, bias, attn_norm_g, rider=gather("w_ffn_in", (0, f_cuts[0]), last=False))
    (mix_b, o_b, st_all), rode = _hgrn_fwd(
        proj, lb_logits, gnorm_g,
        rider=gather("w_ffn_in", (f_cuts[0], f_cuts[1] - f_cuts[0]), rode, before=(0, f_cuts[0]), last=False))
    mixin = jnp.concatenate([mix_a, mix_b], axis=1)
    mix = _mm_nn(mixin, w_o3, tm=1024, tn=512, tk=D, name="mix_out")
    if place is None:
        x1, h2 = _post_mixer(mix, x, gate1, ln1_g, ln1_b, scale2, shift2)
    else:
        (x1, h2), rode = _post_mixer(mix, x, gate1, ln1_g, ln1_b, scale2, shift2, rider=_join(
            gather("w_ffn_in", (f_cuts[1], kh_f - f_cuts[1]), rode, before=(f_cuts[0], f_cuts[1] - f_cuts[0])),
            gather("w_ffn_out", (0, out_cut), last=False)))
    w_ffn_in = gathered("w_ffn_in", rode)
    (gate, up, act), rode = _ffn_in_swiglu(
        h2, w_ffn_in, tm=2048, tn=256,
        rider=gather("w_ffn_out", (out_cut, kh_out - out_cut), rode and rode[1:], before=(0, out_cut)))
    w_out3 = gathered("w_ffn_out", rode)
    w_out3 = w_out3.reshape(1, -1, w_out3.shape[2])
    d_ff = w_out3.shape[1]
    f = _mm_nn(act, w_out3, tm=1024, tn=512, tk=d_ff, name="ffn_out")
    du2, df, acc2 = _loss_head(f, x1, tgt, gate2, ln2_g, ln2_b)
    loss = (0.5 / D) * jnp.sum(acc2[3])
    g = blocks(_mm_tn(act, df, q=1, tk=512, tn=1024, tt=T, name="g_ffn_out"))
    (dff,), rode = _d_act_swiglu(df, w_out3, gate, up, tm=1024, to=512, rider=to_sibling(g))
    p_out = pair_sum("w_ffn_out", g, rode)
    cut = 25 * p_out.shape[1] // 44
    dh2, rode = carrying(_mm_nt, dff, w_ffn_in, tm=1024, to=1024, tn=w_ffn_in.shape[2], name="d_h2",
                         rider=_join(to_chips(p_out, (0, cut)), to_sibling_acts(h2, dff)))
    p_fin, rode = pair_grad("g_ffn_in", h2, dff, w_ffn_in.shape[2] // 2,
                            to_chips(p_out, (cut, p_out.shape[1] - cut), rode), arrived=rode and rode[1:])
    g_ffn_out = summed("w_ffn_out", p_out, rode)
    if place is None:
        du1, dmix, acc1 = _mid_bwd(dh2, du2, x1, mix, x, gate1, ln1_g, scale2)
    else:
        (du1, dmix, acc1), (g_ffn_out,) = _mid_bwd(dh2, du2, x1, mix, x, gate1, ln1_g, scale2, rider=to_both(g_ffn_out))
    g = blocks(_mm_tn(mixin, dmix, q=1, tk=512, tn=1024, tt=T, name="g_o"))
    dmixin, rode = carrying(_mm_nt, dmix, w_o3, tm=1024, to=512, tn=D, name="d_mixin", rider=to_sibling(g))
    p_o = pair_sum("w_o", g, rode)
    cut = p_fin.shape[1] // 2
    (dq, dk, dv, dbias, dgain), rode = _attn_bwd(qkv, probs, attn_norm_g, dmixin, rider=to_chips(p_fin, (0, cut)))
    (dproj, dl0, dgn), rode = _hgrn_bwd(
        proj, lb_logits, gnorm_g, o_b, st_all, dmixin, (dq, dk, dv),
        rider=_join(to_chips(p_fin, (cut, p_fin.shape[1] - cut), rode), to_chips(p_o)))
    g_ffn_in, g_o = summed("w_ffn_in", p_fin, rode[:1]), summed("w_o", p_o, rode[1:])
    p_in, rode = pair_grad("g_in", h1, dproj, w_in.shape[2] // 2, None,
                           late_rider=_join(to_both(g_ffn_in), to_both(g_o)))
    if place is not None:
        g_ffn_in, g_o = rode
    cut = 3 * p_in.shape[1] // 4
    dh1, rode = carrying(_mm_nt, dproj, w_in, tm=1024, to=1024, tn=w_in.shape[2], name="d_h1",
                         rider=to_chips(p_in, (0, cut)))
    if place is None:
        (grad_x, acc0), g_in = _first_bwd(dh1, du1, x, scale1), p_in
    else:
        (grad_x, acc0), rode = _first_bwd(dh1, du1, x, scale1, rider=to_chips(p_in, (cut, p_in.shape[1] - cut), rode))
        g_in, = _alone(to_both(summed("w_in", p_in, rode)), "w_in_share")
    dmod = jnp.concatenate([acc0[1:2], acc0[0:1], acc1[4:5], acc1[1:2], acc1[0:1], acc2[2:3]], axis=0)
    small = dict(rel_bias=_bias_band_grad(dbias), attn_norm_g=dgain,
                 lb_logits=jnp.concatenate([dl0, -dl0], axis=0), gnorm_g=dgn,
                 ln1_g=acc1[2:3], ln1_b=acc1[3:4], ln2_g=acc2[0:1], ln2_b=acc2[1:2])
    return loss, grad_x, dict(w_in=g_in, w_o=g_o, w_ffn_in=g_ffn_in, w_ffn_out=g_ffn_out), dmod, small


SMALL = ("rel_bias", "attn_norm_g", "lb_logits", "gnorm_g", "ln1_g", "ln1_b", "ln2_g", "ln2_b")
SMALL_ROWS = 256


def kernel(x, c, w_ada, b_ada, w_in, rel_bias, attn_norm_g, lb_logits, gnorm_g, w_o, ln1_g, ln1_b, w_ffn_in, w_ffn_out, ln2_g, ln2_b, loss_target, m_w_ada, m_b_ada, m_w_in, m_rel_bias, m_attn_norm_g, m_lb_logits, m_gnorm_g, m_w_o, m_ln1_g, m_ln1_b, m_w_ffn_in, m_w_ffn_out, m_ln2_g, m_ln2_b, v_w_ada, v_b_ada, v_w_in, v_rel_bias, v_attn_norm_g, v_lb_logits, v_gnorm_g, v_w_o, v_ln1_g, v_ln1_b, v_w_ffn_in, v_w_ffn_out, v_ln2_g, v_ln2_b):
    mx, my, mc = _place()
    me = 4 * mx + 2 * my + mc
    chip = 2 * mx + my
    sel = jnp.stack([chip, mc]).astype(jnp.int32)
    D = x.shape[2]
    ns_ada = w_ada.shape[2]

    big = dict(w_in=(w_in, m_w_in, v_w_in), w_o=(w_o, m_w_o, v_w_o), w_ffn_in=(w_ffn_in, m_w_ffn_in, v_w_ffn_in),
               w_ffn_out=(w_ffn_out, m_w_ffn_out, v_w_ffn_out))
    shards = dict(w_in=w_in[0].astype(BF16))
    kh, rode, rows = shards["w_in"].shape[0] // 2, None, None
    for n, part in (("w_ffn_in", 19), ("w_ffn_out", 9), ("w_o", 4)):
        before, rows = rows, (rows[0] + rows[1] if rows else 0, part * kh // 32)
        (shards[n],), rode = _to_bf16(big[n][0][0], "cast_" + n,
                                      _gather_rider(shards["w_in"], rows, rode and rode[0], before, last=n == "w_o"))
    shards["w_in_gathered"] = rode[0]

    c_all = _gather_small(c.reshape(D // 128, 128), "gather_c").reshape(N_DEV, D)
    c_all = jnp.pad(c_all, ((0, 16 - N_DEV), (0, 0)))
    mod_cols = _ada_fwd(c_all, w_ada[0])[:N_DEV]
    mod_all = _gather_small(mod_cols.reshape(-1, 128), "gather_mod").reshape(N_DEV, N_DEV, ns_ada)
    mod = lax.dynamic_index_in_dim(mod_all[::2], me, axis=1, keepdims=False)
    mod = (mod.reshape(1, -1) + b_ada).reshape(6, D)

    loss, grad_x, g_big, dmod, g_small = _layer(
        x[0], loss_target[0], mod, shards, rel_bias[0], attn_norm_g, lb_logits, gnorm_g, ln1_g, ln1_b, ln2_g, ln2_b,
        place=(chip, sel))

    grads, deltas, new_m, new_v = {}, {}, {}, {}
    for n, (w, m, v) in big.items():
        g, d, mo, vo = _adamw(g_big[n], w[0], m[0], v[0], "adamw_" + n)
        grads[n], deltas[n], new_m[n], new_v[n] = g[None], d[None], mo[None], vo[None]

    small_in = dict(rel_bias=(rel_bias, m_rel_bias, v_rel_bias), attn_norm_g=(attn_norm_g, m_attn_norm_g, v_attn_norm_g),
                    lb_logits=(lb_logits, m_lb_logits, v_lb_logits), gnorm_g=(gnorm_g, m_gnorm_g, v_gnorm_g),
                    ln1_g=(ln1_g, m_ln1_g, v_ln1_g), ln1_b=(ln1_b, m_ln1_b, v_ln1_b), ln2_g=(ln2_g, m_ln2_g, v_ln2_g),
                    ln2_b=(ln2_b, m_ln2_b, v_ln2_b))
    g_all = _gather_small(_pack([dmod] + [g_small[n] for n in SMALL] + [loss], SMALL_ROWS), "gather_small")
    packed = [_pack([t] + [small_in[n][i] for n in SMALL] + [jnp.zeros((), F32)], SMALL_ROWS)
              for i, t in enumerate((b_ada, m_b_ada, v_b_ada))]
    shapes = [b_ada.shape] + [small_in[n][0].shape for n in SMALL] + [()]
    outs = [_unpack(o, shapes) for o in _small_update(g_all, *packed)]
    loss = outs[0][-1]
    for i, n in enumerate(("b_ada",) + SMALL):
        grads[n], deltas[n], new_m[n], new_v[n] = outs[0][i], outs[1][i], outs[2][i], outs[3][i]

    dmod_all = g_all[:, :6 * D // 128].reshape(N_DEV, 6 * D)
    dmod_cols = lax.dynamic_slice_in_dim(dmod_all, chip * ns_ada, ns_ada, axis=1)
    dmod_cols = jnp.pad(dmod_cols, ((0, 16 - N_DEV), (0, 0)))
    g, d, mo, vo = _ada_bwd(c_all, dmod_cols, w_ada[0], m_w_ada[0], v_w_ada[0])
    grads["w_ada"], deltas["w_ada"], new_m["w_ada"], new_v["w_ada"] = g[None], d[None], mo[None], vo[None]

    order = ("w_ada", "b_ada", "w_in", "rel_bias", "attn_norm_g", "lb_logits", "gnorm_g", "w_o", "ln1_g", "ln1_b",
             "w_ffn_in", "w_ffn_out", "ln2_g", "ln2_b")
    return (loss, grad_x[None], *[grads[n] for n in order], *[deltas[n] for n in order],
            *[new_m[n] for n in order], *[new_v[n] for n in order])
```

```python
import numpy as np
import jax
import jax.numpy as jnp
from jax import lax
from jax.experimental import pallas as pl
from jax.experimental.pallas import tpu as pltpu

F32 = jnp.float32
BF16 = jnp.bfloat16
MESH = pl.DeviceIdType.MESH
HIGHEST = lax.Precision.HIGHEST

CHUNK = 64
N_PAST = 8
QG = 4
QROWS = QG * CHUNK
KPAD = N_PAST * CHUNK
ZPAD = 2 * KPAD
UNION = (QG + N_PAST) * CHUNK
BAND = (N_PAST + 1) * CHUNK
HD_A = 64
HD_B = 128
SUB = 16
HGRN_HEADS = 8
MAX_REL = 256
EPS = 1e-5
ALPHA = 2.0 ** 0.25
LR, B1, B2, ADAM_EPS, WD, STEP = 1e-3, 0.9, 0.999, 1e-8, 0.01, 10
N_CHIPS = 4
N_DEV = 8
NEG = -1e30
TILE_BYTES = 3 << 19

NN = ((1,), (0,))
NT = ((1,), (1,))
TN = ((0,), (0,))


def _dot(a, b, dims=NN, precision=None):
    return lax.dot_general(a, b, (dims, ((), ())), preferred_element_type=F32, precision=precision)


def _params(sem=None, vmem_mb=None, **kw):
    return pltpu.CompilerParams(dimension_semantics=sem,
                                vmem_limit_bytes=None if vmem_mb is None else vmem_mb << 20, **kw)


def _row_tile(rows, cols):
    for cand in (512, 256, 128, 64, 32, 16, 8):
        if rows % cand == 0 and cand * cols * 4 <= TILE_BYTES:
            return cand
    raise ValueError((rows, cols))


def _place():
    return lax.axis_index("x"), lax.axis_index("y"), lax.axis_index("c")


def _flip(v, bit):
    return 1 - v if bit else v


ANY = pl.BlockSpec(memory_space=pl.ANY)
CHIP_FLIPS = ((1, 0), (0, 1), (1, 1))


class _Rider:
    def __init__(self, operands, out_shape, n_sems, start, finish, aliases=None):
        self.operands, self.out_shape, self.n_sems, self.start, self.finish = operands, out_shape, n_sems, start, finish
        self.aliases = aliases or {}


def _call(body, rider, *, name, grid, in_specs, out_specs, out_shape, scratch_shapes, compiler_params, operands):
    if rider is None:
        outs = pl.pallas_call(body, name=name, grid=grid, in_specs=in_specs, out_specs=out_specs, out_shape=out_shape,
                              scratch_shapes=scratch_shapes, compiler_params=compiler_params)(*operands)
        return list(outs), []
    n_in, n_out, n_sc = len(in_specs), len(out_specs), len(scratch_shapes)
    r_in, r_out = len(rider.operands), len(rider.out_shape)

    def carried(*refs):
        refs = list(refs)
        cuts = [n_in, r_in, n_out, r_out, n_sc]
        ins, r_ins, outs, r_outs, scratch = [[refs.pop(0) for _ in range(n)] for n in cuts]
        first, last = None, None
        for axis, size in enumerate(grid):
            i = pl.program_id(axis)
            first = (i == 0) if first is None else first & (i == 0)
            last = (i == size - 1) if last is None else last & (i == size - 1)

        @pl.when(first)
        def _():
            rider.start(r_ins, r_outs, *refs)

        body(*ins, *outs, *scratch)

        @pl.when(last)
        def _():
            rider.finish(r_ins, r_outs, *refs)

    sems = [pltpu.SemaphoreType.DMA((rider.n_sems,)), pltpu.SemaphoreType.DMA((rider.n_sems,))]
    outs = pl.pallas_call(carried, name=name, grid=grid, in_specs=list(in_specs) + [ANY] * r_in,
                          out_specs=list(out_specs) + [ANY] * r_out, out_shape=list(out_shape) + rider.out_shape,
                          scratch_shapes=list(scratch_shapes) + sems, compiler_params=compiler_params,
                          input_output_aliases={n_in + i: n_out + o for i, o in rider.aliases.items()},
                          )(*operands, *rider.operands)
    return list(outs[:n_out]), list(outs[n_out:])


def _alone(rider, name):
    def body(*refs):
        ins, outs, sems = refs[:len(rider.operands)], refs[len(rider.operands):-2], refs[-2:]
        rider.start(ins, outs, *sems)
        rider.finish(ins, outs, *sems)

    return pl.pallas_call(
        body, name=name, in_specs=[ANY] * len(rider.operands), out_specs=[ANY] * len(rider.out_shape),
        out_shape=rider.out_shape, input_output_aliases=rider.aliases,
        scratch_shapes=[pltpu.SemaphoreType.DMA((rider.n_sems,)), pltpu.SemaphoreType.DMA((rider.n_sems,))],
    )(*rider.operands)


class _Sems:
    def __init__(self, sems, base):
        self.sems, self.base = sems, base

    @property
    def at(self):
        return self

    def __getitem__(self, k):
        return self.sems.at[self.base + k]


def _join(*riders):
    riders = [r for r in riders if r is not None]
    if len(riders) < 2:
        return riders[0] if riders else None

    def parts(ins, outs, send_sems, recv_sems):
        i = o = s = 0
        for r in riders:
            ni, no = len(r.operands), len(r.out_shape)
            yield r, ins[i:i + ni], outs[o:o + no], _Sems(send_sems, s), _Sems(recv_sems, s)
            i, o, s = i + ni, o + no, s + r.n_sems

    def start(*refs):
        for r, *args in parts(*refs):
            r.start(*args)

    def finish(*refs):
        for r, *args in parts(*refs):
            r.finish(*args)

    aliases, i, o = {}, 0, 0
    for r in riders:
        aliases.update({i + a: o + b for a, b in r.aliases.items()})
        i, o = i + len(r.operands), o + len(r.out_shape)
    return _Rider([a for r in riders for a in r.operands], [s for r in riders for s in r.out_shape],
                  sum(r.n_sems for r in riders), start, finish, aliases)


def _gather_rider(shard, rows=None, into=None, before=None, last=True):
    K, Ns = shard.shape
    kh = K // 2
    rows = rows or (0, kh)

    def copies(w_ref, out_ref, send_sems, recv_sems):
        x, y, c = _place()
        chips = [(_flip(x, fx), _flip(y, fy)) for fx, fy in CHIP_FLIPS]

        def half(chip, which, part):
            return out_ref.at[2 * chip[0] + chip[1], pl.ds(which * kh + part[0], part[1]), :]

        def copy(k, dst, to, src=None):
            return pltpu.make_async_remote_copy(src_ref=dst if src is None else src, dst_ref=dst,
                                                send_sem=send_sems.at[k], recv_sem=recv_sems.at[k],
                                                device_id=to, device_id_type=MESH)

        def first():
            return [copy(j, half((x, y), c, rows), (*chip, c), src=w_ref.at[pl.ds(c * kh + rows[0], rows[1]), :])
                    for j, chip in enumerate(chips)]

        def onward(base, part):
            return [copy(base + j, half(chip, c, part), (x, y, 1 - c)) for j, chip in enumerate(chips)]

        def arriving(base, which, part):
            return [copy(base + j, half(chip, which, part), (x, y, c)) for j, chip in enumerate(chips)]

        return c, first, onward, arriving

    def start(ins, outs, send_sems, recv_sems):
        _, first, onward, _ = copies(ins[0], outs[0], send_sems, recv_sems)
        for cp in first() + (onward(3, before) if before else []):
            cp.start()

    def finish(ins, outs, send_sems, recv_sems):
        c, first, onward, arriving = copies(ins[0], outs[0], send_sems, recv_sems)
        sent = first() + (onward(3, before) if before else [])
        passed = onward(6, rows) if last else [None] * 3
        for arrived, cp in zip(arriving(0, c, rows), passed):
            arrived.wait_recv()
            if last:
                cp.start()
        for arrived in (arriving(3, 1 - c, before) if before else []) + (arriving(6, 1 - c, rows) if last else []):
            arrived.wait_recv()
        for cp in sent + (passed if last else []):
            cp.wait_send()

    full = jax.ShapeDtypeStruct((N_CHIPS, K, Ns), shard.dtype)
    if into is None:
        return _Rider([shard], [full], 9, start, finish)
    return _Rider([shard, into], [full], 9, start, finish, aliases={1: 0})


def _pair_rider(g_full):
    Q, K, Ns = g_full.shape
    kh = K // 2

    def copy(g_ref, got_ref, send_sems, recv_sems):
        x, y, c = _place()
        return pltpu.make_async_remote_copy(src_ref=g_ref.at[:, pl.ds((1 - c) * kh, kh), :], dst_ref=got_ref,
                                            send_sem=send_sems.at[0], recv_sem=recv_sems.at[0],
                                            device_id=(x, y, 1 - c), device_id_type=MESH)

    def start(ins, outs, send_sems, recv_sems):
        copy(ins[0], outs[0], send_sems, recv_sems).start()

    def finish(ins, outs, send_sems, recv_sems):
        copy(ins[0], outs[0], send_sems, recv_sems).wait()

    return _Rider([g_full], [jax.ShapeDtypeStruct((Q, kh, Ns), g_full.dtype)], 1, start, finish)


def _acts_rider(a, b):
    T, K = a.shape
    kh = K // 2

    def copies(ins, outs, send_sems, recv_sems):
        x, y, c = _place()
        pair = [(ins[0].at[:, pl.ds((1 - c) * kh, kh)], outs[0]), (ins[1], outs[1])]
        return [pltpu.make_async_remote_copy(src_ref=src, dst_ref=dst, send_sem=send_sems.at[k], recv_sem=recv_sems.at[k],
                                             device_id=(x, y, 1 - c), device_id_type=MESH)
                for k, (src, dst) in enumerate(pair)]

    def start(*refs):
        for cp in copies(*refs):
            cp.start()

    def finish(*refs):
        for cp in copies(*refs):
            cp.wait()

    return _Rider([a, b], [jax.ShapeDtypeStruct((T, kh), a.dtype), jax.ShapeDtypeStruct(b.shape, b.dtype)], 2,
                  start, finish)


def _share_rider(block):
    K, Ns = block.shape
    kh = K // 2

    def halves(out_ref):
        x, y, c = _place()
        return out_ref.at[pl.ds(c * kh, kh), :], out_ref.at[pl.ds((1 - c) * kh, kh), :], (x, y, 1 - c)

    def start(ins, outs, send_sems, recv_sems):
        mine, _, sibling = halves(outs[0])
        pltpu.make_async_remote_copy(src_ref=mine, dst_ref=mine, send_sem=send_sems.at[0], recv_sem=recv_sems.at[0],
                                     device_id=sibling, device_id_type=MESH).start()

    def finish(ins, outs, send_sems, recv_sems):
        mine, theirs, sibling = halves(outs[0])
        pltpu.make_async_remote_copy(src_ref=theirs, dst_ref=theirs, send_sem=send_sems.at[0], recv_sem=recv_sems.at[0],
                                     device_id=sibling, device_id_type=MESH).wait_recv()
        pltpu.make_async_remote_copy(src_ref=mine, dst_ref=mine, send_sem=send_sems.at[0], recv_sem=recv_sems.at[0],
                                     device_id=sibling, device_id_type=MESH).wait_send()

    return _Rider([block], [jax.ShapeDtypeStruct((K, Ns), block.dtype)], 1, start, finish, aliases={0: 0})


def _chips_rider(pair_sum, rows=None, into=None):
    Q, kh, Ns = pair_sum.shape
    first_row, n_rows = rows or (0, kh)

    def copies(p_ref, got_ref, send_sems, recv_sems):
        x, y, c = _place()
        part = pl.ds(first_row, n_rows)
        out = []
        for j, (fx, fy) in enumerate(CHIP_FLIPS):
            px, py = _flip(x, fx), _flip(y, fy)
            out.append(pltpu.make_async_remote_copy(
                src_ref=p_ref.at[2 * px + py, part, :], dst_ref=got_ref.at[j, part, :], send_sem=send_sems.at[j],
                recv_sem=recv_sems.at[j], device_id=(px, py, c), device_id_type=MESH))
        return out

    def start(ins, outs, send_sems, recv_sems):
        for cp in copies(ins[0], outs[0], send_sems, recv_sems):
            cp.start()

    def finish(ins, outs, send_sems, recv_sems):
        sends = copies(ins[0], outs[0], send_sems, recv_sems)
        for cp in sends:
            cp.wait_recv()
        for cp in sends:
            cp.wait_send()

    got = jax.ShapeDtypeStruct((Q - 1, kh, Ns), pair_sum.dtype)
    if into is None:
        return _Rider([pair_sum], [got], 3, start, finish)
    return _Rider([pair_sum, into], [got], 3, start, finish, aliases={1: 0})


def _mm(a, b, *, grid, a_spec, b_spec, o_spec, o_shape, o_dtype, dims, acc_shape, name, rider=None, zero_rows=0,
        vmem_mb=48):
    nk = grid[2]

    def body(a_ref, b_ref, o_ref, *scratch):
        if zero_rows:
            @pl.when(pl.program_id(0) < zero_rows)
            def _():
                o_ref[...] = jnp.zeros_like(o_ref)

            @pl.when(pl.program_id(0) >= zero_rows)
            def _():
                o_ref[...] = _dot(a_ref[...], b_ref[...], dims).astype(o_ref.dtype)
            return
        part = _dot(a_ref[...], b_ref[...], dims)
        if nk == 1:
            o_ref[...] = part.astype(o_ref.dtype)
            return
        acc_ref, = scratch
        k = pl.program_id(2)

        @pl.when(k == 0)
        def _():
            acc_ref[...] = part

        @pl.when(k > 0)
        def _():
            acc_ref[...] += part

        @pl.when(k == nk - 1)
        def _():
            o_ref[...] = acc_ref[...].astype(o_ref.dtype)

    (out,), rode = _call(
        body, rider, name=name, grid=grid, in_specs=[a_spec, b_spec], out_specs=[o_spec],
        out_shape=[jax.ShapeDtypeStruct(o_shape, o_dtype)],
        scratch_shapes=[] if nk == 1 else [pltpu.VMEM(acc_shape, F32)],
        compiler_params=_params(("parallel", "parallel", "arbitrary") if rider is None else ("arbitrary",) * 3, vmem_mb),
        operands=(a, b))
    return out if rider is None else (out, rode)


def _mm_nn(a, w, *, tm, tn, tk, name, rider=None, cols=None, o_dtype=F32, pad_rows=0):
    T, K = a.shape
    Q, _, Ns = w.shape
    nbs = Ns // tn
    tm = min(tm, T)
    j0, j1 = cols or (0, Q * nbs)
    lead = pad_rows // tm
    return _mm(a, w, grid=(lead + T // tm, j1 - j0, K // tk),
               a_spec=pl.BlockSpec((tm, tk), lambda i, j, k: (jnp.maximum(i - lead, 0), k)),
               b_spec=pl.BlockSpec((None, tk, tn), lambda i, j, k: ((j + j0) // nbs, k, (j + j0) % nbs)),
               o_spec=pl.BlockSpec((tm, tn), lambda i, j, k: (i, j)),
               o_shape=(pad_rows + T, (j1 - j0) * tn), o_dtype=o_dtype, dims=NN, acc_shape=(tm, tn), name=name,
               rider=rider, zero_rows=lead)


def _col_blocks(g, rows, tn, at):
    if g.ndim == 2:
        return pl.BlockSpec((rows, tn), at)
    per = g.shape[2] // tn

    def stacked(*idx):
        r, c = at(*idx)
        return c // per, r, c % per

    return pl.BlockSpec((None, rows, tn), stacked)


def _mm_nt(g, w, *, tm, to, tn, name, rider=None):
    T = g.shape[-2]
    Q, K, Ns = w.shape
    nbs = Ns // tn
    tm = min(tm, T)
    return _mm(g, w, grid=(T // tm, K // to, Q * nbs),
               a_spec=_col_blocks(g, tm, tn, lambda i, j, n: (i, n)),
               b_spec=pl.BlockSpec((None, to, tn), lambda i, j, n: (n // nbs, j, n % nbs)),
               o_spec=pl.BlockSpec((tm, to), lambda i, j, n: (i, j)),
               o_shape=(T, K), o_dtype=F32, dims=NT, acc_shape=(tm, to), name=name, rider=rider)


def _mm_tn(a, g, *, q, tk, tn, tt, name, rider=None):
    T, K = a.shape
    Ns = g.shape[-1] * (g.ndim - 1) // q
    nbs = Ns // tn
    return _mm(a, g, grid=(K // tk, q * nbs, T // tt),
               a_spec=pl.BlockSpec((tt, tk), lambda i, j, t: (t, i)),
               b_spec=_col_blocks(g, tt, tn, lambda i, j, t: (t, j)),
               o_spec=pl.BlockSpec((None, tk, tn), lambda i, j, t: (j // nbs, i, j % nbs)),
               o_shape=(q, K, Ns), o_dtype=BF16, dims=TN, acc_shape=(tk, tn), name=name, rider=rider)


def _mm_tn_add(a, g, part, *, tk, tn, name, rider=None):
    T, K = a.shape
    Q, _, Ns = part.shape
    nbs = Ns // tn

    def body(a_ref, g_ref, p_ref, o_ref):
        o_ref[...] = (_dot(a_ref[...], g_ref[...], TN) + p_ref[...].astype(F32)).astype(o_ref.dtype)

    blk = pl.BlockSpec((None, tk, tn), lambda i, j: (j // nbs, i, j % nbs))
    (out,), rode = _call(
        body, rider, name=name, grid=(K // tk, Q * nbs),
        in_specs=[pl.BlockSpec((T, tk), lambda i, j: (0, i)), _col_blocks(g, T, tn, lambda i, j: (0, j)), blk],
        out_specs=[blk], out_shape=[jax.ShapeDtypeStruct((Q, K, Ns), BF16)], scratch_shapes=[],
        compiler_params=_params(("arbitrary", "arbitrary"), 48), operands=(a, g, part))
    return out, rode


def _ln(u):
    mu = jnp.mean(u, axis=-1, keepdims=True)
    d = u - mu
    r = lax.rsqrt(jnp.mean(d * d, axis=-1, keepdims=True) + EPS)
    return d * r, r


def _ln_bwd(dy, un, r):
    return r * (dy - jnp.mean(dy, axis=-1, keepdims=True) - un * jnp.mean(dy * un, axis=-1, keepdims=True))


def _colsum(v):
    return jnp.sum(v, axis=0, keepdims=True)


def _rowwise(name, fn, bigs, vecs, out_dtypes, n_acc, tm=128, rider=None):
    T, D = bigs[0].shape
    nb, nv, no = len(bigs), len(vecs), len(out_dtypes)

    def body(*refs):
        outs, accs = fn([r[...] for r in refs[:nb]], [r[...] for r in refs[nb:nb + nv]])
        for r, o in zip(refs[nb + nv:nb + nv + no], outs):
            r[...] = o.astype(r.dtype)
        if n_acc:
            acc_ref = refs[nb + nv + no]

            @pl.when(pl.program_id(0) == 0)
            def _():
                acc_ref[...] = jnp.zeros_like(acc_ref)

            for row, a in enumerate(accs):
                acc_ref[row:row + 1, :] += a

    big_spec = pl.BlockSpec((tm, D), lambda i: (i, 0))
    vec_spec = pl.BlockSpec((1, D), lambda i: (0, 0))
    out_shape = [jax.ShapeDtypeStruct((T, D), dt) for dt in out_dtypes]
    out_specs = [big_spec] * no
    if n_acc:
        out_shape.append(jax.ShapeDtypeStruct((8, D), F32))
        out_specs.append(pl.BlockSpec((8, D), lambda i: (0, 0)))
    outs, rode = _call(
        body, rider, name=name, grid=(T // tm,), in_specs=[big_spec] * nb + [vec_spec] * nv,
        out_specs=out_specs, out_shape=out_shape, scratch_shapes=[],
        compiler_params=_params(("arbitrary",), 48), operands=(*bigs, *vecs))
    return outs if rider is None else (outs, rode)


def _to_bf16(w, name, rider=None):
    R, C = w.shape
    tr = _row_tile(R, C)

    def body(w_ref, o_ref):
        o_ref[...] = w_ref[...].astype(o_ref.dtype)

    blk = pl.BlockSpec((tr, C), lambda i: (i, 0))
    return _call(body, rider, name=name, grid=(R // tr,), in_specs=[blk], out_specs=[blk],
                 out_shape=[jax.ShapeDtypeStruct((R, C), BF16)], scratch_shapes=[],
                 compiler_params=_params(("arbitrary",)), operands=(w,))


def _pre_mixer(x, scale1, shift1, rider=None):
    def fn(b, v):
        xn, _ = _ln(b[0])
        return [xn * (1.0 + v[0]) + v[1]], []
    outs = _rowwise("pre_mixer", fn, [x], [scale1, shift1], [BF16], 0, rider=rider)
    return outs[0] if rider is None else (outs[0][0], outs[1])


def _post_mixer(mix, x, gate1, g1, b1, scale2, shift2, rider=None):
    def fn(b, v):
        un1, _ = _ln(ALPHA * b[1] + v[0] * b[0])
        x1 = un1 * v[1] + v[2]
        xn1, _ = _ln(x1)
        return [x1, xn1 * (1.0 + v[3]) + v[4]], []
    return _rowwise("post_mixer", fn, [mix, x], [gate1, g1, b1, scale2, shift2], [F32, BF16], 0, rider=rider)


def _loss_head(f, x1, tgt, gate2, g2, b2):
    def fn(b, v):
        ff, xx, tt = b
        d_model = ff.shape[-1]
        un2, r2 = _ln(ALPHA * xx + v[0] * ff)
        err = un2 * v[1] + v[2] - tt
        dy = err * (1.0 / d_model)
        du2 = _ln_bwd(dy * v[1], un2, r2)
        return [du2, du2 * v[0]], [_colsum(dy * un2), _colsum(dy), _colsum(du2 * ff), _colsum(err * err)]
    return _rowwise("loss_head", fn, [f, x1, tgt], [gate2, g2, b2], [F32, BF16], 4)


def _mid_bwd(dh2, du2, x1, mix, x, gate1, g1, scale2, rider=None):
    def fn(b, v):
        dh, du, xx1, mm, xx = b
        xn1, r1n = _ln(xx1)
        dx1 = ALPHA * du + _ln_bwd(dh * (1.0 + v[2]), xn1, r1n)
        un1, r1 = _ln(ALPHA * xx + v[0] * mm)
        du1 = _ln_bwd(dx1 * v[1], un1, r1)
        return [du1, du1 * v[0]], [_colsum(dh * xn1), _colsum(dh), _colsum(dx1 * un1), _colsum(dx1),
                                   _colsum(du1 * mm)]
    return _rowwise("mid_bwd", fn, [dh2, du2, x1, mix, x], [gate1, g1, scale2], [F32, BF16], 5, rider=rider)


def _first_bwd(dh1, du1, x, scale1, rider=None):
    def fn(b, v):
        dh, du, xx = b
        xn, r0 = _ln(xx)
        return [ALPHA * du + _ln_bwd(dh * (1.0 + v[0]), xn, r0)], [_colsum(dh * xn), _colsum(dh)]
    return _rowwise("first_bwd", fn, [dh1, du1, x], [scale1], [F32], 2, rider=rider)


def _ffn_in_swiglu(h2, w, *, tm, tn, rider=None):
    T, K = h2.shape
    Q, _, Ns = w.shape
    nbs = Ns // tn
    half = Q * nbs // 2
    tm = min(tm, T)

    def body(a_ref, wg_ref, wu_ref, g_ref, u_ref, act_ref):
        a = a_ref[...]
        g, u = _dot(a, wg_ref[...]), _dot(a, wu_ref[...])
        g_ref[...] = g.astype(g_ref.dtype)
        u_ref[...] = u.astype(u_ref.dtype)
        act_ref[...] = (g * jax.nn.sigmoid(g) * u).astype(act_ref.dtype)

    cols = lambda first: pl.BlockSpec((None, K, tn), lambda i, j: ((j + first) // nbs, 0, (j + first) % nbs))
    blk = pl.BlockSpec((tm, tn), lambda i, j: (i, j))
    return _call(
        body, rider, name="ffn_in", grid=(T // tm, half),
        in_specs=[pl.BlockSpec((tm, K), lambda i, j: (i, 0)), cols(0), cols(half)], out_specs=[blk] * 3,
        out_shape=[jax.ShapeDtypeStruct((T, half * tn), BF16)] * 3, scratch_shapes=[],
        compiler_params=_params(("arbitrary", "arbitrary"), 48), operands=(h2, w, w))


def _d_act_swiglu(df, w, gate, up, *, tm, to, rider=None):
    T, N = df.shape
    F = w.shape[1]
    tm = min(tm, T)

    def body(df_ref, w_ref, g_ref, u_ref, o_ref):
        d = _dot(df_ref[...], w_ref[...], NT)
        g = g_ref[...].astype(F32)
        s = jax.nn.sigmoid(g)
        o_ref[0] = (d * u_ref[...].astype(F32) * s * (1.0 + g * (1.0 - s))).astype(o_ref.dtype)
        o_ref[1] = (d * g * s).astype(o_ref.dtype)

    blk = pl.BlockSpec((tm, to), lambda i, j: (i, j))
    return _call(
        body, rider, name="d_act", grid=(T // tm, F // to),
        in_specs=[pl.BlockSpec((tm, N), lambda i, j: (i, 0)), pl.BlockSpec((None, to, N), lambda i, j: (0, j, 0)), blk, blk],
        out_specs=[pl.BlockSpec((2, tm, to), lambda i, j: (0, i, j))],
        out_shape=[jax.ShapeDtypeStruct((2, T, F), BF16)], scratch_shapes=[],
        compiler_params=_params(("arbitrary", "arbitrary"), 48), operands=(df, w, gate, up))


PAIR = 2


def _fill_table(table_ref, band_ref):
    table_ref[...] = jnp.full(table_ref.shape, NEG, F32)
    for e in range(PAIR):
        for g in range(QG):
            table_ref[e, g * CHUNK:(g + 1) * CHUNK, g * CHUNK:g * CHUNK + BAND] = band_ref[e]


def _attn_probs(q_ref, k_ref, bias_ref, e, step):
    start = pl.multiple_of(step * QROWS, QROWS)
    lanes = pl.ds(e * HD_A, HD_A)
    s = _dot(q_ref[:, lanes], k_ref[pl.ds(start + ZPAD - KPAD, UNION), lanes], NT) * (HD_A ** -0.5) + bias_ref[e]
    col = lax.broadcasted_iota(jnp.int32, s.shape, 1)
    s = jnp.where(col + start >= KPAD, s, NEG)
    p = jnp.exp(s - jnp.max(s, axis=-1, keepdims=True))
    return p / jnp.sum(p, axis=-1, keepdims=True), start


def _attn_specs(T, n_pairs):
    wide = PAIR * HD_A
    per_step = pl.BlockSpec((QROWS, wide), lambda hp, n: (n, hp))
    queries = pl.BlockSpec((QROWS, wide), lambda hp, n: (n + ZPAD // QROWS, hp))
    keys = pl.BlockSpec((ZPAD + T, wide), lambda hp, n: (0, n_pairs + hp))
    values = pl.BlockSpec((ZPAD + T, wide), lambda hp, n: (0, 2 * n_pairs + hp))
    grads = pl.BlockSpec((T, wide), lambda hp, n: (0, hp))
    table = pl.BlockSpec((PAIR, CHUNK, BAND), lambda hp, n: (hp, 0, 0))
    vec = pl.BlockSpec((1, wide), lambda hp, n: (0, hp))
    return per_step, queries, keys, values, grads, table, vec


def _probs_spec():
    return pl.BlockSpec((PAIR, QROWS, UNION), lambda hp, n: (hp, n, 0))


def _attn_fwd(qkv, bias, gain, rider=None):
    T = qkv.shape[0] - ZPAD
    W = gain.shape[1]
    n_pairs = W // (PAIR * HD_A)

    def body(q_ref, k_ref, v_ref, band_ref, gain_ref, o_ref, p_ref, table_ref):
        @pl.when(pl.program_id(1) == 0)
        def _():
            _fill_table(table_ref, band_ref)

        for e in range(PAIR):
            lanes = pl.ds(e * HD_A, HD_A)
            p, start = _attn_probs(q_ref, k_ref, table_ref, e, pl.program_id(1))
            p_ref[e] = p.astype(p_ref.dtype)
            o = _dot(p_ref[e], v_ref[pl.ds(start + ZPAD - KPAD, UNION), lanes])
            rr = lax.rsqrt(jnp.mean(o * o, axis=-1, keepdims=True) + EPS)
            o_ref[:, lanes] = (o * rr * gain_ref[:, lanes]).astype(o_ref.dtype)

    per_step, queries, keys, values, _, table, vec = _attn_specs(T, n_pairs)
    return _call(
        body, rider, name="attn_fwd", grid=(n_pairs, T // QROWS), in_specs=[queries, keys, values, table, vec],
        out_specs=[per_step, _probs_spec()],
        out_shape=[jax.ShapeDtypeStruct((T, W), BF16), jax.ShapeDtypeStruct((n_pairs * PAIR, T, UNION), BF16)],
        scratch_shapes=[pltpu.VMEM((PAIR, QROWS, UNION), F32)],
        compiler_params=_params(("arbitrary", "arbitrary"), 40), operands=(qkv, qkv, qkv, bias, gain))


def _attn_bwd(qkv, probs, gain, dmixin, rider=None):
    T = qkv.shape[0] - ZPAD
    W = gain.shape[1]
    n_pairs = W // (PAIR * HD_A)
    scale = HD_A ** -0.5

    def body(q_ref, k_ref, v_ref, p_ref, gain_ref, don_ref, dq_ref, dkb_ref, dvb_ref, dband_ref, dgain_ref,
             dtable_ref, dk_ref, dv_ref):
        n = pl.program_id(1)

        @pl.when(n == 0)
        def _():
            dk_ref[...] = jnp.zeros_like(dk_ref)
            dv_ref[...] = jnp.zeros_like(dv_ref)
            dtable_ref[...] = jnp.zeros_like(dtable_ref)
            dgain_ref[...] = jnp.zeros_like(dgain_ref)

        for e in range(PAIR):
            lanes = pl.ds(e * HD_A, HD_A)
            start = pl.multiple_of(n * QROWS, QROWS)
            keys, in_qkv = pl.ds(start, UNION), pl.ds(start + ZPAD - KPAD, UNION)
            pb = p_ref[e]
            p = pb.astype(F32)
            vb = v_ref[in_qkv, lanes]
            o = _dot(pb, vb)
            rr = lax.rsqrt(jnp.mean(o * o, axis=-1, keepdims=True) + EPS)
            on = o * rr
            d_on = don_ref[:, lanes]
            dgain_ref[:, lanes] += _colsum(d_on * on)
            dyo = d_on * gain_ref[:, lanes]
            do = rr * (dyo - on * jnp.mean(dyo * on, axis=-1, keepdims=True))
            dob = do.astype(BF16)
            dp = _dot(dob, vb, NT)
            ds = p * (dp - jnp.sum(do * o, axis=-1, keepdims=True))
            dtable_ref[e] += ds
            dsb = ds.astype(BF16)
            dq_ref[:, lanes] = (_dot(dsb, k_ref[in_qkv, lanes]) * scale).astype(dq_ref.dtype)
            dk_ref[keys, lanes] += _dot(dsb, q_ref[:, lanes], TN) * scale
            dv_ref[keys, lanes] += _dot(pb, dob, TN)

        @pl.when(n == T // QROWS - 1)
        def _():
            for e in range(PAIR):
                dband_ref[e] = sum(dtable_ref[e, g * CHUNK:(g + 1) * CHUNK, g * CHUNK:g * CHUNK + BAND]
                                   for g in range(QG))
            dkb_ref[...] = dk_ref[KPAD:, :].astype(dkb_ref.dtype)
            dvb_ref[...] = dv_ref[KPAD:, :].astype(dvb_ref.dtype)

    per_step, queries, keys, values, grads, table, vec = _attn_specs(T, n_pairs)
    H = n_pairs * PAIR
    return _call(
        body, rider, name="attn_bwd", grid=(n_pairs, T // QROWS),
        in_specs=[queries, keys, values, _probs_spec(), vec, per_step],
        out_specs=[per_step, grads, grads, table, vec],
        out_shape=[jax.ShapeDtypeStruct((T, W), BF16)] * 3 + [jax.ShapeDtypeStruct((H, CHUNK, BAND), F32),
                                                              jax.ShapeDtypeStruct((1, W), F32)],
        scratch_shapes=[pltpu.VMEM((PAIR, QROWS, UNION), F32)] + [pltpu.VMEM((KPAD + T, PAIR * HD_A), F32)] * 2,
        compiler_params=_params(("arbitrary", "arbitrary"), 40),
        operands=(qkv, qkv, qkv, probs, gain, dmixin))


N_DIAG = CHUNK + BAND - 1


def _bias_band(rel_bias):
    H = rel_bias.shape[0]
    idx = np.clip(BAND - 1 - np.arange(N_DIAG), -MAX_REL, MAX_REL) + MAX_REL
    rolled = rel_bias[:, idx[(np.arange(N_DIAG) + CHUNK - 1) % N_DIAG]]
    flat = jnp.broadcast_to(rolled[:, None, :], (H, CHUNK, N_DIAG)).reshape(H, CHUNK * N_DIAG)
    return flat[:, :CHUNK * (N_DIAG - 1)].reshape(H, CHUNK, N_DIAG - 1)[:, :, :BAND]


def _bias_band_grad(dband):
    H = dband.shape[0]
    skew = jnp.pad(dband, ((0, 0), (0, 0), (CHUNK - 1, 0))).reshape(H, CHUNK * N_DIAG)
    skew = jnp.pad(skew, ((0, 0), (0, CHUNK))).reshape(H, CHUNK, N_DIAG + 1)
    diag = jnp.sum(skew, axis=1)[:, :N_DIAG]
    n_far = BAND - MAX_REL
    far = jnp.sum(diag[:, :n_far], axis=1, keepdims=True)
    near = diag[:, n_far:][:, ::-1]
    zeros = jnp.zeros((H, MAX_REL - (CHUNK - 1)), F32)
    return jnp.concatenate([zeros, near, far], axis=1)


def _tri(n, lower):
    r = lax.broadcasted_iota(jnp.int32, (n, n), 0)
    c = lax.broadcasted_iota(jnp.int32, (n, n), 1)
    return jnp.where((c <= r) if lower else (c >= r), 1.0, 0.0).astype(F32)


def _hgrn_gates(zq_ref, zf_ref, lbl_ref, q_s, k_s, b_s):
    lb = jax.nn.sigmoid(lbl_ref[0:1, :] - lbl_ref[1:2, :])
    zq = zq_ref[...]
    sig = jax.nn.sigmoid(zf_ref[...])
    f = lb + (1.0 - lb) * sig
    sq = jax.nn.sigmoid(zq)
    q_s[...] = zq * sq
    k_s[...] = 1.0 - f
    b_s[...] = _dot(_tri(CHUNK, True), jnp.log(f), precision=HIGHEST)
    return lb, sig, f, sq


def _sub_rows(i):
    return pl.ds(i * SUB, SUB)


def _row_mask(s):
    return lax.broadcasted_iota(jnp.int32, (SUB, HD_B), 0) >= s


def _decay_from(b_sub, b_row, s):
    return jnp.where(_row_mask(s), jnp.exp(jnp.minimum(b_sub - b_row, 0.0)), 0.0)


def _hgrn_fwd(proj, lb_logits, gnorm_g, rider=None):
    T = proj.shape[0]
    nC = T // CHUNK
    W = lb_logits.shape[1]
    G = W // HD_B // HGRN_HEADS
    col0 = (proj.shape[1] - 4 * W) // (HD_B * HGRN_HEADS)
    wide = HGRN_HEADS * HD_B

    def body(*refs):
        @pl.when(pl.program_id(1) == 0)
        def _():
            refs[9][...] = jnp.zeros_like(refs[9])

        for h in range(HGRN_HEADS):
            lanes = pl.ds(h * HD_B, HD_B)
            one_head(*[r.at[:, lanes] for r in refs[:5]], refs[5], *[r.at[:, lanes] for r in refs[6:8]],
                     *[r.at[h] for r in refs[8:]])

    def one_head(zq_ref, zf_ref, xi_ref, zg_ref, lbl_ref, gn_ref, mix_ref, o_ref, stall_ref, st_ref, q_s, k_s, b_s, acc_s):
        _hgrn_gates(zq_ref, zf_ref, lbl_ref, q_s, k_s, b_s)
        q, k, b = q_s[...], k_s[...], b_s[...]
        st = st_ref[...]
        stall_ref[...] = st
        b_last = b_s[CHUNK - 1:CHUNK, :]
        acc_s[...] = _dot((q * jnp.exp(b)).astype(BF16), st.astype(BF16), NT)
        for i in range(CHUNK // SUB):
            rows = _sub_rows(i)
            q_i, b_i = q_s[rows, :], b_s[rows, :]
            acc = jnp.zeros((SUB, HD_B), F32)
            if i:
                past = pl.ds(0, i * SUB)
                b_ref = b_s[i * SUB - 1:i * SUB, :]
                qs = (q_i * jnp.exp(b_i - b_ref)).astype(BF16)
                ks = (k_s[past, :] * jnp.exp(b_ref - b_s[past, :])).astype(BF16)
                acc += _dot(_dot(qs, ks, NT).astype(BF16), xi_ref[past, :].astype(BF16))
            for s in range(SUB):
                row = pl.ds(i * SUB + s, 1)
                w = q_i * _decay_from(b_i, b_s[row, :], s)
                acc += jnp.sum(w * k_s[row, :], axis=-1, keepdims=True) * xi_ref[row, :]
            acc_s[rows, :] += acc
        o = acc_s[...]
        kd = (k * jnp.exp(b_last - b)).astype(BF16)
        st_ref[...] = st * jnp.exp(b_last) + _dot(xi_ref[...].astype(BF16), kd, TN)
        o_ref[...] = o
        zg = zg_ref[...]
        rr = lax.rsqrt(jnp.mean(o * o, axis=-1, keepdims=True) + EPS)
        mix_ref[...] = (o * rr * gn_ref[...] * (zg * jax.nn.sigmoid(zg))).astype(mix_ref.dtype)

    col = lambda part: pl.BlockSpec((CHUNK, wide), lambda g, n: (n, col0 + part * G + g))
    out_blk = pl.BlockSpec((CHUNK, wide), lambda g, n: (n, g))
    tile = pltpu.VMEM((HGRN_HEADS, CHUNK, HD_B), F32)
    return _call(
        body, rider, name="hgrn_fwd", grid=(G, nC),
        in_specs=[col(0), col(1), col(2), col(3), pl.BlockSpec((2, wide), lambda g, n: (0, g)),
                  pl.BlockSpec((1, HD_B), lambda g, n: (0, 0))],
        out_specs=[out_blk, out_blk, pl.BlockSpec((HGRN_HEADS, None, HD_B, HD_B), lambda g, n: (g, n, 0, 0))],
        out_shape=[jax.ShapeDtypeStruct((T, W), BF16), jax.ShapeDtypeStruct((T, W), F32),
                   jax.ShapeDtypeStruct((G * HGRN_HEADS, nC, HD_B, HD_B), F32)],
        scratch_shapes=[pltpu.VMEM((HGRN_HEADS, HD_B, HD_B), F32), tile, tile, tile, tile],
        compiler_params=_params(("arbitrary", "arbitrary")),
        operands=(proj, proj, proj, proj, lb_logits, gnorm_g))


def _hgrn_bwd(proj, lb_logits, gnorm_g, o_b, st_all, dmixin, d_attn, rider=None):
    T = proj.shape[0]
    nC = T // CHUNK
    W = lb_logits.shape[1]
    wa = d_attn[0].shape[1]
    assert W == HGRN_HEADS * HD_B, "one grid step takes every head: it writes whole rows of d proj"
    G = W // HD_B // HGRN_HEADS
    wide = HGRN_HEADS * HD_B
    col0 = (proj.shape[1] - 4 * W) // wide
    dcol0 = (dmixin.shape[1] - W) // wide

    def body(*refs):
        g, n = pl.program_id(0), pl.program_id(1)
        dproj_ref, dl0_ref, dgn_ref, dst_ref = refs[12:16]
        for i in range(3):
            dproj_ref[:, i * wa:(i + 1) * wa] = refs[9 + i][...]

        @pl.when(n == 0)
        def _():
            dst_ref[...] = jnp.zeros_like(dst_ref)
            dl0_ref[...] = jnp.zeros_like(dl0_ref)

        @pl.when((n == 0) & (g == 0))
        def _():
            dgn_ref[...] = jnp.zeros_like(dgn_ref)

        for h in range(HGRN_HEADS):
            lanes = pl.ds(h * HD_B, HD_B)
            cut = lambda r: r.at[:, lanes]
            parts = [dproj_ref.at[:, pl.ds(3 * wa + part * W + h * HD_B, HD_B)] for part in range(4)]
            one_head(*[cut(r) for r in refs[:5]], refs[5], cut(refs[6]), refs[7].at[h], cut(refs[8]),
                     *parts, cut(dl0_ref), dgn_ref, *[r.at[h] for r in refs[15:]])

    def one_head(zq_ref, zf_ref, xi_ref, zg_ref, lbl_ref, gn_ref, o_ref, st_ref, dout_ref,
                 dzq_ref, dzf_ref, dxi_ref, dzg_ref, dl0_ref, dgn_ref, dst_ref, q_s, k_s, b_s, do_s, dq_s, dk_s, di_s):
        lb, sig, f, sq = _hgrn_gates(zq_ref, zf_ref, lbl_ref, q_s, k_s, b_s)
        q, k, b = q_s[...], k_s[...], b_s[...]
        zg, o, dout = zg_ref[...], o_ref[...], dout_ref[...]
        sg = jax.nn.sigmoid(zg)
        rr = lax.rsqrt(jnp.mean(o * o, axis=-1, keepdims=True) + EPS)
        on = o * rr
        gn = gn_ref[...]
        dzg_ref[...] = (dout * on * gn * sg * (1.0 + zg * (1.0 - sg))).astype(dzg_ref.dtype)
        d_on = dout * zg * sg
        dgn_ref[...] += _colsum(d_on * on)
        d_on = d_on * gn
        do = rr * (d_on - on * jnp.mean(d_on * on, axis=-1, keepdims=True))
        do_s[...] = do
        dob = do.astype(BF16)
        st, dst = st_ref[...], dst_ref[...]
        b_last = b_s[CHUNK - 1:CHUNK, :]
        eb, e_last, k_dec = jnp.exp(b), jnp.exp(b_last), jnp.exp(b_last - b)
        qt, kd = q * eb, k * k_dec
        dstb = dst.astype(BF16)
        xib = xi_ref[...].astype(BF16)
        d_kd = _dot(xib, dstb)
        dq_s[...] = _dot(dob, st.astype(BF16)) * eb
        dk_s[...] = d_kd * k_dec
        di_s[...] = _dot(kd.astype(BF16), dstb, NT)
        d_b_last = e_last * _colsum(st * dst) + _colsum(d_kd * kd)
        dst_ref[...] = _dot(dob, qt.astype(BF16), TN) + dst * e_last
        for i in range(CHUNK // SUB):
            rows = _sub_rows(i)
            q_i, b_i, do_i = q_s[rows, :], b_s[rows, :], do_s[rows, :]
            dq_i = jnp.zeros((SUB, HD_B), F32)
            if i:
                past = pl.ds(0, i * SUB)
                b_ref = b_s[i * SUB - 1:i * SUB, :]
                e_q, e_k = jnp.exp(b_i - b_ref), jnp.exp(b_ref - b_s[past, :])
                qs, ks = (q_i * e_q).astype(BF16), (k_s[past, :] * e_k).astype(BF16)
                xi_p, do_b = xi_ref[past, :].astype(BF16), do_i.astype(BF16)
                di_s[past, :] += _dot(_dot(ks, qs, NT).astype(BF16), do_b)
                dq_i += _dot(_dot(do_b, xi_p, NT).astype(BF16), ks) * e_q
                dk_s[past, :] += _dot(_dot(xi_p, do_b, NT).astype(BF16), qs) * e_k
            for s in range(SUB):
                row = pl.ds(i * SUB + s, 1)
                k_row, i_row = k_s[row, :], xi_ref[row, :]
                e = _decay_from(b_i, b_s[row, :], s)
                w = q_i * e
                a_col = jnp.sum(w * k_row, axis=-1, keepdims=True)
                da_col = jnp.sum(do_i * i_row, axis=-1, keepdims=True)
                di_s[row, :] += _colsum(a_col * do_i)
                dq_i += da_col * e * k_row
                dk_s[row, :] += _colsum(da_col * w)
            dq_s[rows, :] += dq_i
        dq, dk = dq_s[...], dk_s[...]
        db = q * dq - k * dk
        is_last = lax.broadcasted_iota(jnp.int32, (CHUNK, HD_B), 0) == CHUNK - 1
        db = db + jnp.where(is_last, d_b_last, 0.0)
        df = _dot(_tri(CHUNK, False), db, precision=HIGHEST) / f - dk
        dzf_ref[...] = (df * (1.0 - lb) * sig * (1.0 - sig)).astype(dzf_ref.dtype)
        dl0_ref[...] += _colsum(df * (1.0 - sig)) * (lb * (1.0 - lb))
        zq = zq_ref[...]
        dzq_ref[...] = (dq * sq * (1.0 + zq * (1.0 - sq))).astype(dzq_ref.dtype)
        dxi_ref[...] = di_s[...].astype(dxi_ref.dtype)

    rev = lambda n: nC - 1 - n
    col = lambda part: pl.BlockSpec((CHUNK, wide), lambda g, n: (rev(n), col0 + part * G + g))
    blk = pl.BlockSpec((CHUNK, wide), lambda g, n: (rev(n), g))
    tile = pltpu.VMEM((HGRN_HEADS, CHUNK, HD_B), F32)
    rows = lambda width: pl.BlockSpec((CHUNK, width), lambda g, n: (rev(n), 0))
    return _call(
        body, rider, name="hgrn_bwd", grid=(G, nC),
        in_specs=[col(0), col(1), col(2), col(3), pl.BlockSpec((2, wide), lambda g, n: (0, g)),
                  pl.BlockSpec((1, HD_B), lambda g, n: (0, 0)), blk,
                  pl.BlockSpec((HGRN_HEADS, None, HD_B, HD_B), lambda g, n: (g, rev(n), 0, 0)),
                  pl.BlockSpec((CHUNK, wide), lambda g, n: (rev(n), dcol0 + g)), rows(wa), rows(wa), rows(wa)],
        out_specs=[rows(3 * wa + 4 * W), pl.BlockSpec((1, wide), lambda g, n: (0, g)),
                   pl.BlockSpec((1, HD_B), lambda g, n: (0, 0))],
        out_shape=[jax.ShapeDtypeStruct((T, 3 * wa + 4 * W), BF16), jax.ShapeDtypeStruct((1, W), F32),
                   jax.ShapeDtypeStruct((1, HD_B), F32)],
        scratch_shapes=[pltpu.VMEM((HGRN_HEADS, HD_B, HD_B), F32)] + [tile] * 7,
        compiler_params=_params(("arbitrary", "arbitrary")),
        operands=(proj, proj, proj, proj, lb_logits, gnorm_g, o_b, st_all, dmixin, *d_attn))


def _adamw_math(g, w, m, v):
    m = B1 * m + (1.0 - B1) * g
    v = B2 * v + (1.0 - B2) * (g * g)
    m_hat = m / (1.0 - B1 ** STEP)
    v_hat = v / (1.0 - B2 ** STEP)
    return -LR * (m_hat / (jnp.sqrt(v_hat) + ADAM_EPS) + WD * w), m, v


def _adamw(g, w, m, v, name):
    R, C = g.shape
    tr = _row_tile(R, C)

    def body(g_ref, w_ref, m_ref, v_ref, go_ref, d_ref, mo_ref, vo_ref):
        g = g_ref[...]
        go_ref[...] = g
        d_ref[...], mo_ref[...], vo_ref[...] = _adamw_math(g, w_ref[...], m_ref[...], v_ref[...])

    blk = pl.BlockSpec((tr, C), lambda i: (i, 0))
    return pl.pallas_call(
        body, name=name, grid=(R // tr,), in_specs=[blk] * 4, out_specs=[blk] * 4,
        out_shape=[jax.ShapeDtypeStruct((R, C), F32)] * 4, compiler_params=_params(("parallel",), 40),
    )(g, w, m, v)


def _sum_pair(g_full, from_sibling, sel, name):
    Q, K, Ns = g_full.shape
    kh = K // 2
    tr = _row_tile(kh, Ns)
    nh = kh // tr

    def body(sel_ref, a_ref, b_ref, o_ref):
        o_ref[...] = (a_ref[...].astype(F32) + b_ref[...].astype(F32)).astype(o_ref.dtype)

    return pl.pallas_call(
        body, name=name,
        grid_spec=pltpu.PrefetchScalarGridSpec(
            num_scalar_prefetch=1, grid=(Q, nh),
            in_specs=[pl.BlockSpec((None, tr, Ns), lambda q, i, sel: (q, sel[1] * nh + i, 0)),
                      pl.BlockSpec((None, tr, Ns), lambda q, i, sel: (q, i, 0))],
            out_specs=pl.BlockSpec((None, tr, Ns), lambda q, i, sel: (q, i, 0))),
        out_shape=jax.ShapeDtypeStruct((Q, kh, Ns), BF16), compiler_params=_params(("parallel", "parallel")),
    )(sel, g_full, from_sibling)


def _sum_chips(pair_sum, from_chips, sel, name):
    Q, kh, Ns = pair_sum.shape
    tr = _row_tile(kh, Ns)
    nh = kh // tr

    def body(sel_ref, a_ref, b0_ref, b1_ref, b2_ref, o_ref):
        up = lambda r: r[...].astype(F32)
        o_ref[...] = ((up(a_ref) + up(b0_ref)) + up(b1_ref)) + up(b2_ref)

    recv = lambda k: pl.BlockSpec((None, tr, Ns), lambda i, sel: (k, i, 0))
    return pl.pallas_call(
        body, name=name,
        grid_spec=pltpu.PrefetchScalarGridSpec(
            num_scalar_prefetch=1, grid=(nh,),
            in_specs=[pl.BlockSpec((None, tr, Ns), lambda i, sel: (sel[0], i, 0)), recv(0), recv(1), recv(2)],
            out_specs=pl.BlockSpec((tr, Ns), lambda i, sel: (sel[1] * nh + i, 0))),
        out_shape=jax.ShapeDtypeStruct((2 * kh, Ns), F32), compiler_params=_params(("parallel",)),
    )(sel, pair_sum, from_chips, from_chips, from_chips)


def _gather_small(v, name):
    R, L = v.shape

    def body(v_ref, out_ref, send_sems, recv_sems):
        x, y, c = _place()
        me = 4 * x + 2 * y + c
        out_ref[me] = v_ref[...]
        peers = [(_flip(x, k >> 2 & 1), _flip(y, k >> 1 & 1), _flip(c, k & 1)) for k in range(1, N_DEV)]

        def copy(k, row, to):
            return pltpu.make_async_remote_copy(src_ref=v_ref, dst_ref=out_ref.at[row], send_sem=send_sems.at[k],
                                                recv_sem=recv_sems.at[k], device_id=to, device_id_type=MESH)

        sends = [copy(k, me, peer) for k, peer in enumerate(peers)]
        for cp in sends:
            cp.start()
        for k, (px, py, pc) in enumerate(peers):
            copy(k, 4 * px + 2 * py + pc, (x, y, c)).wait_recv()
        for cp in sends:
            cp.wait_send()

    vmem = pl.BlockSpec(memory_space=pltpu.VMEM)
    return pl.pallas_call(
        body, name=name, in_specs=[vmem], out_specs=vmem, out_shape=jax.ShapeDtypeStruct((N_DEV, R, L), F32),
        scratch_shapes=[pltpu.SemaphoreType.DMA((N_DEV - 1,)), pltpu.SemaphoreType.DMA((N_DEV - 1,))],
    )(v)


def _silu(v):
    return v * jax.nn.sigmoid(v)


def _ada_fwd(c_all, w_ada, rider, tn=512):
    M, D = c_all.shape
    Ns = w_ada.shape[1]

    def body(c_ref, w_ref, o_ref):
        o_ref[...] = _dot(_silu(c_ref[...]).astype(BF16), w_ref[...].astype(BF16))

    (out,), rode = _call(
        body, rider, name="ada_fwd", grid=(Ns // tn,),
        in_specs=[pl.BlockSpec((M, D), lambda j: (0, 0)), pl.BlockSpec((D, tn), lambda j: (0, j))],
        out_specs=[pl.BlockSpec((M, tn), lambda j: (0, j))], out_shape=[jax.ShapeDtypeStruct((M, Ns), F32)],
        scratch_shapes=[], compiler_params=_params(("arbitrary",)), operands=(c_all, w_ada))
    return out, rode


def _ada_bwd(c_all, dmod, w, m, v, tk=256, tn=1536):
    M, D = c_all.shape
    Ns = dmod.shape[1]

    def body(c_ref, d_ref, w_ref, m_ref, v_ref, g_ref, dl_ref, mo_ref, vo_ref):
        g = _dot(_silu(c_ref[...]).astype(BF16), d_ref[...].astype(BF16), TN)
        g_ref[...] = g
        dl_ref[...], mo_ref[...], vo_ref[...] = _adamw_math(g, w_ref[...], m_ref[...], v_ref[...])

    blk = pl.BlockSpec((tk, tn), lambda i, j: (i, j))
    return pl.pallas_call(
        body, name="ada_bwd", grid=(D // tk, Ns // tn),
        in_specs=[pl.BlockSpec((M, tk), lambda i, j: (0, i)), pl.BlockSpec((M, tn), lambda i, j: (0, j)), blk, blk, blk],
        out_specs=[blk] * 4, out_shape=[jax.ShapeDtypeStruct((D, Ns), F32)] * 4,
        compiler_params=_params(("parallel", "parallel"), 40),
    )(c_all, dmod, w, m, v)


def _small_update(g_all, w, m, v):
    R, L = w.shape

    def body(g_ref, w_ref, m_ref, v_ref, go_ref, d_ref, mo_ref, vo_ref):
        g = g_ref[0]
        for d in range(1, N_DEV):
            g = g + g_ref[d]
        go_ref[...] = g
        d_ref[...], mo_ref[...], vo_ref[...] = _adamw_math(g, w_ref[...], m_ref[...], v_ref[...])

    return pl.pallas_call(body, name="small_update", out_shape=[jax.ShapeDtypeStruct((R, L), F32)] * 4)(g_all, w, m, v)


def _pack(parts, rows):
    flat = jnp.concatenate([p.reshape(-1) for p in parts])
    return jnp.pad(flat, (0, rows * 128 - flat.shape[0])).reshape(rows, 128)


def _unpack(packed, shapes):
    flat, out, at = packed.reshape(-1), [], 0
    for shp in shapes:
        size = 1
        for d in shp:
            size *= d
        out.append(flat[at:at + size].reshape(shp))
        at += size
    return out


def _layer(x, tgt, mod, wts, rel_bias, attn_norm_g, lb_logits, gnorm_g, ln1_g, ln1_b, ln2_g, ln2_b, place=None):
    T, D = x.shape
    aw = attn_norm_g.shape[1]
    shift1, scale1, gate1, shift2, scale2, gate2 = [mod[i:i + 1] for i in range(6)]

    def gather(n, rows=None, into=None, before=None, last=True):
        return None if place is None else _gather_rider(wts[n], rows, None if into is None else into[0], before, last)

    def gathered(n, rode):
        return wts[n] if place is None else lax.dynamic_update_index_in_dim(rode[0], wts[n], place[0], 0)

    def blocks(g):
        return g.reshape(N_CHIPS, -1, g.shape[2])

    def to_sibling(g):
        return None if place is None else _pair_rider(g)

    def pair_sum(n, g, rode=None):
        if place is None:
            return g
        rode = _alone(_pair_rider(g), n + "_send_pair") if rode is None else rode
        return _sum_pair(g, rode[0], place[1], n + "_sum_pair")

    def to_chips(p, rows=None, into=None):
        return None if place is None else _chips_rider(p, rows, None if into is None else into[0])

    def summed(n, p, rode):
        return p if place is None else _sum_chips(p, rode[0], place[1], n + "_sum_chips")

    def to_both(block):
        return None if place is None else _share_rider(block)

    def carrying(mm, *args, rider, **kw):
        return mm(*args, rider=rider, **kw) if rider is not None else (mm(*args, **kw), None)

    def to_sibling_acts(a, b):
        return None if place is None else _acts_rider(a, b)

    def pair_grad(name, a, b, tn, rider, arrived=None, late_rider=None):
        if place is None:
            return _mm_tn(a, b, q=N_CHIPS, tk=512, tn=tn, tt=T, name=name), None
        kh = a.shape[1] // 2
        mine = lax.dynamic_slice_in_dim(a, place[1][1] * kh, kh, axis=1)
        part, rode = carrying(_mm_tn, mine, b, q=N_CHIPS, tk=512, tn=tn, tt=T, name=name + "_own",
                              rider=_join(None if arrived else _acts_rider(a, b), rider))
        (a_sib, b_sib), rode = arrived or rode[:2], rode if arrived else rode[2:]
        out, late = _mm_tn_add(a_sib, b_sib, part, tk=512, tn=tn, name=name + "_sib", rider=late_rider)
        return out, (rode or []) + late

    if place is None:
        h1, rode = _pre_mixer(x, scale1, shift1), None
    else:
        h1, rode = _pre_mixer(x, scale1, shift1, wts["w_in_last_part"])
    w_in = gathered("w_in", rode)
    n_qkv = 3 * aw // 256
    kh_o, kh_f, kh_out = [wts[n].shape[-2] // 2 for n in ("w_o", "w_ffn_in", "w_ffn_out")]
    o_cut, f_cuts, out_cut = 3 * kh_o // 8, (7 * kh_f // 16, 7 * kh_f // 8), kh_out // 11
    qkv, rode = carrying(_mm_nn, h1, w_in, tm=ZPAD, tn=256, tk=D, name="proj_qkv", cols=(0, n_qkv), o_dtype=BF16,
                         pad_rows=ZPAD, rider=gather("w_o", (0, o_cut), last=False))
    proj, rode = carrying(_mm_nn, h1, w_in, tm=2048, tn=256, tk=D, name="proj_rec",
                          cols=(n_qkv, N_CHIPS * w_in.shape[2] // 256),
                          rider=gather("w_o", (o_cut, kh_o - o_cut), rode, before=(0, o_cut)))
    w_o3 = gathered("w_o", rode).reshape(1, D, D)
    bias = _bias_band(rel_bias)
    (mix_a, probs), rode = _attn_fwd(qkv, bias, attn_norm_g, rider=gather("w_ffn_in", (0, f_cuts[0]), last=False))
    (mix_b, o_b, st_all), rode = _hgrn_fwd(
        proj, lb_logits, gnorm_g,
        rider=gather("w_ffn_in", (f_cuts[0], f_cuts[1] - f_cuts[0]), rode, before=(0, f_cuts[0]), last=False))
    mixin = jnp.concatenate([mix_a, mix_b], axis=1)
    mix = _mm_nn(mixin, w_o3, tm=1024, tn=512, tk=D, name="mix_out")
    if place is None:
        x1, h2 = _post_mixer(mix, x, gate1, ln1_g, ln1_b, scale2, shift2)
    else:
        (x1, h2), rode = _post_mixer(mix, x, gate1, ln1_g, ln1_b, scale2, shift2, rider=_join(
            gather("w_ffn_in", (f_cuts[1], kh_f - f_cuts[1]), rode, before=(f_cuts[0], f_cuts[1] - f_cuts[0])),
            gather("w_ffn_out", (0, out_cut), last=False)))
    w_ffn_in = gathered("w_ffn_in", rode)
    (gate, up, act), rode = _ffn_in_swiglu(
        h2, w_ffn_in, tm=2048, tn=256,
        rider=gather("w_ffn_out", (out_cut, kh_out - out_cut), rode and rode[1:], before=(0, out_cut)))
    w_out3 = gathered("w_ffn_out", rode)
    w_out3 = w_out3.reshape(1, -1, w_out3.shape[2])
    d_ff = w_out3.shape[1]
    f = _mm_nn(act, w_out3, tm=1024, tn=512, tk=d_ff, name="ffn_out")
    du2, df, acc2 = _loss_head(f, x1, tgt, gate2, ln2_g, ln2_b)
    loss = (0.5 / D) * jnp.sum(acc2[3])
    g = blocks(_mm_tn(act, df, q=1, tk=512, tn=1024, tt=T, name="g_ffn_out"))
    (dff,), rode = _d_act_swiglu(df, w_out3, gate, up, tm=1024, to=512, rider=to_sibling(g))
    p_out = pair_sum("w_ffn_out", g, rode)
    cut = 25 * p_out.shape[1] // 44
    dh2, rode = carrying(_mm_nt, dff, w_ffn_in, tm=1024, to=1024, tn=w_ffn_in.shape[2], name="d_h2",
                         rider=_join(to_chips(p_out, (0, cut)), to_sibling_acts(h2, dff)))
    p_fin, rode = pair_grad("g_ffn_in", h2, dff, w_ffn_in.shape[2] // 2,
                            to_chips(p_out, (cut, p_out.shape[1] - cut), rode), arrived=rode and rode[1:])
    g_ffn_out = summed("w_ffn_out", p_out, rode)
    if place is None:
        du1, dmix, acc1 = _mid_bwd(dh2, du2, x1, mix, x, gate1, ln1_g, scale2)
    else:
        (du1, dmix, acc1), (g_ffn_out,) = _mid_bwd(dh2, du2, x1, mix, x, gate1, ln1_g, scale2, rider=to_both(g_ffn_out))
    g = blocks(_mm_tn(mixin, dmix, q=1, tk=512, tn=1024, tt=T, name="g_o"))
    dmixin, rode = carrying(_mm_nt, dmix, w_o3, tm=1024, to=512, tn=D, name="d_mixin", rider=to_sibling(g))
    p_o = pair_sum("w_o", g, rode)
    cut = p_fin.shape[1] // 2
    (dq, dk, dv, dbias, dgain), rode = _attn_bwd(qkv, probs, attn_norm_g, dmixin, rider=to_chips(p_fin, (0, cut)))
    (dproj, dl0, dgn), rode = _hgrn_bwd(
        proj, lb_logits, gnorm_g, o_b, st_all, dmixin, (dq, dk, dv),
        rider=_join(to_chips(p_fin, (cut, p_fin.shape[1] - cut), rode), to_chips(p_o)))
    g_ffn_in, g_o = summed("w_ffn_in", p_fin, rode[:1]), summed("w_o", p_o, rode[1:])
    p_in, rode = pair_grad("g_in", h1, dproj, w_in.shape[2] // 2, None,
                           late_rider=_join(to_both(g_ffn_in), to_both(g_o)))
    if place is not None:
        g_ffn_in, g_o = rode
    cut = 3 * p_in.shape[1] // 4
    dh1, rode = carrying(_mm_nt, dproj, w_in, tm=1024, to=1024, tn=w_in.shape[2], name="d_h1",
                         rider=to_chips(p_in, (0, cut)))
    if place is None:
        (grad_x, acc0), g_in = _first_bwd(dh1, du1, x, scale1), p_in
    else:
        (grad_x, acc0), rode = _first_bwd(dh1, du1, x, scale1, rider=to_chips(p_in, (cut, p_in.shape[1] - cut), rode))
        g_in, = _alone(to_both(summed("w_in", p_in, rode)), "w_in_share")
    dmod = jnp.concatenate([acc0[1:2], acc0[0:1], acc1[4:5], acc1[1:2], acc1[0:1], acc2[2:3]], axis=0)
    small = dict(rel_bias=_bias_band_grad(dbias), attn_norm_g=dgain,
                 lb_logits=jnp.concatenate([dl0, -dl0], axis=0), gnorm_g=dgn,
                 ln1_g=acc1[2:3], ln1_b=acc1[3:4], ln2_g=acc2[0:1], ln2_b=acc2[1:2])
    return loss, grad_x, dict(w_in=g_in, w_o=g_o, w_ffn_in=g_ffn_in, w_ffn_out=g_ffn_out), dmod, small


SMALL = ("rel_bias", "attn_norm_g", "lb_logits", "gnorm_g", "ln1_g", "ln1_b", "ln2_g", "ln2_b")
SMALL_ROWS = 256


def kernel(x, c, w_ada, b_ada, w_in, rel_bias, attn_norm_g, lb_logits, gnorm_g, w_o, ln1_g, ln1_b, w_ffn_in, w_ffn_out, ln2_g, ln2_b, loss_target, m_w_ada, m_b_ada, m_w_in, m_rel_bias, m_attn_norm_g, m_lb_logits, m_gnorm_g, m_w_o, m_ln1_g, m_ln1_b, m_w_ffn_in, m_w_ffn_out, m_ln2_g, m_ln2_b, v_w_ada, v_b_ada, v_w_in, v_rel_bias, v_attn_norm_g, v_lb_logits, v_gnorm_g, v_w_o, v_ln1_g, v_ln1_b, v_w_ffn_in, v_w_ffn_out, v_ln2_g, v_ln2_b):
    mx, my, mc = _place()
    me = 4 * mx + 2 * my + mc
    chip = 2 * mx + my
    sel = jnp.stack([chip, mc]).astype(jnp.int32)
    D = x.shape[2]
    ns_ada = w_ada.shape[2]

    big = dict(w_in=(w_in, m_w_in, v_w_in), w_o=(w_o, m_w_o, v_w_o), w_ffn_in=(w_ffn_in, m_w_ffn_in, v_w_ffn_in),
               w_ffn_out=(w_ffn_out, m_w_ffn_out, v_w_ffn_out))
    shards = dict(w_in=w_in[0].astype(BF16))
    kh = shards["w_in"].shape[0] // 2
    cuts = [part * kh // 32 for part in (0, 12, 19, 22, 26, 32)]
    spans = [(a, b - a) for a, b in zip(cuts, cuts[1:])]

    def w_in_part(i, rode):
        return _gather_rider(shards["w_in"], spans[i], rode and rode[0], spans[i - 1] if i else None, last=i == 4)

    rode = None
    for i, n in enumerate(("w_ffn_in", "w_ffn_out", "w_o")):
        (shards[n],), rode = _to_bf16(big[n][0][0], "cast_" + n, w_in_part(i, rode))

    c_all = _gather_small(c.reshape(D // 128, 128), "gather_c").reshape(N_DEV, D)
    c_all = jnp.pad(c_all, ((0, 16 - N_DEV), (0, 0)))
    mod_cols, rode = _ada_fwd(c_all, w_ada[0], w_in_part(3, rode))
    mod_cols = mod_cols[:N_DEV]
    shards["w_in_last_part"] = w_in_part(4, rode)
    mod_all = _gather_small(mod_cols.reshape(-1, 128), "gather_mod").reshape(N_DEV, N_DEV, ns_ada)
    mod = lax.dynamic_index_in_dim(mod_all[::2], me, axis=1, keepdims=False)
    mod = (mod.reshape(1, -1) + b_ada).reshape(6, D)

    loss, grad_x, g_big, dmod, g_small = _layer(
        x[0], loss_target[0], mod, shards, rel_bias[0], attn_norm_g, lb_logits, gnorm_g, ln1_g, ln1_b, ln2_g, ln2_b,
        place=(chip, sel))

    grads, deltas, new_m, new_v = {}, {}, {}, {}
    for n, (w, m, v) in big.items():
        g, d, mo, vo = _adamw(g_big[n], w[0], m[0], v[0], "adamw_" + n)
        grads[n], deltas[n], new_m[n], new_v[n] = g[None], d[None], mo[None], vo[None]

    small_in = dict(rel_bias=(rel_bias, m_rel_bias, v_rel_bias), attn_norm_g=(attn_norm_g, m_attn_norm_g, v_attn_norm_g),
                    lb_logits=(lb_logits, m_lb_logits, v_lb_logits), gnorm_g=(gnorm_g, m_gnorm_g, v_gnorm_g),
                    ln1_g=(ln1_g, m_ln1_g, v_ln1_g), ln1_b=(ln1_b, m_ln1_b, v_ln1_b), ln2_g=(ln2_g, m_ln2_g, v_ln2_g),
                    ln2_b=(ln2_b, m_ln2_b, v_ln2_b))
    g_all = _gather_small(_pack([dmod] + [g_small[n] for n in SMALL] + [loss], SMALL_ROWS), "gather_small")
    packed = [_pack([t] + [small_in[n][i] for n in SMALL] + [jnp.zeros((), F32)], SMALL_ROWS)
              for i, t in enumerate((b_ada, m_b_ada, v_b_ada))]
    shapes = [b_ada.shape] + [small_in[n][0].shape for n in SMALL] + [()]
    outs = [_unpack(o, shapes) for o in _small_update(g_all, *packed)]
    loss = outs[0][-1]
    for i, n in enumerate(("b_ada",) + SMALL):
        grads[n], deltas[n], new_m[n], new_v[n] = outs[0][i], outs[1][i], outs[2][i], outs[3][i]

    dmod_all = g_all[:, :6 * D // 128].reshape(N_DEV, 6 * D)
    dmod_cols = lax.dynamic_slice_in_dim(dmod_all, chip * ns_ada, ns_ada, axis=1)
    dmod_cols = jnp.pad(dmod_cols, ((0, 16 - N_DEV), (0, 0)))
    g, d, mo, vo = _ada_bwd(c_all, dmod_cols, w_ada[0], m_w_ada[0], v_w_ada[0])
    grads["w_ada"], deltas["w_ada"], new_m["w_ada"], new_v["w_ada"] = g[None], d[None], mo[None], vo[None]

    order = ("w_ada", "b_ada", "w_in", "rel_bias", "attn_norm_g", "lb_logits", "gnorm_g", "w_o", "ln1_g", "ln1_b",
             "w_ffn_in", "w_ffn_out", "ln2_g", "ln2_b")
    return (loss, grad_x[None], *[grads[n] for n in order], *[deltas[n] for n in order],
            *[new_m[n] for n in order], *[new_v[n] for n in order])
```

```python
import numpy as np
import jax
import jax.numpy as jnp
from jax import lax
from jax.experimental import pallas as pl
from jax.experimental.pallas import tpu as pltpu

F32 = jnp.float32
BF16 = jnp.bfloat16
MESH = pl.DeviceIdType.MESH
HIGHEST = lax.Precision.HIGHEST

CHUNK = 64
N_PAST = 8
QG = 4
QROWS = QG * CHUNK
KPAD = N_PAST * CHUNK
ZPAD = 2 * KPAD
UNION = (QG + N_PAST) * CHUNK
BAND = (N_PAST + 1) * CHUNK
HD_A = 64
HD_B = 128
SUB = 16
HGRN_HEADS = 8
MAX_REL = 256
EPS = 1e-5
ALPHA = 2.0 ** 0.25
LR, B1, B2, ADAM_EPS, WD, STEP = 1e-3, 0.9, 0.999, 1e-8, 0.01, 10
N_CHIPS = 4
N_DEV = 8
NEG = -1e30
TILE_BYTES = 3 << 19

NN = ((1,), (0,))
NT = ((1,), (1,))
TN = ((0,), (0,))


def _dot(a, b, dims=NN, precision=None):
    return lax.dot_general(a, b, (dims, ((), ())), preferred_element_type=F32, precision=precision)


def _params(sem=None, vmem_mb=None, **kw):
    return pltpu.CompilerParams(dimension_semantics=sem,
                                vmem_limit_bytes=None if vmem_mb is None else vmem_mb << 20, **kw)


def _row_tile(rows, cols):
    for cand in (512, 256, 128, 64, 32, 16, 8):
        if rows % cand == 0 and cand * cols * 4 <= TILE_BYTES:
            return cand
    raise ValueError((rows, cols))


def _place():
    return lax.axis_index("x"), lax.axis_index("y"), lax.axis_index("c")


def _flip(v, bit):
    return 1 - v if bit else v


ANY = pl.BlockSpec(memory_space=pl.ANY)
CHIP_FLIPS = ((1, 0), (0, 1), (1, 1))


class _Rider:
    def __init__(self, operands, out_shape, n_sems, start, finish, aliases=None):
        self.operands, self.out_shape, self.n_sems, self.start, self.finish = operands, out_shape, n_sems, start, finish
        self.aliases = aliases or {}


def _call(body, rider, *, name, grid, in_specs, out_specs, out_shape, scratch_shapes, compiler_params, operands):
    if rider is None:
        outs = pl.pallas_call(body, name=name, grid=grid, in_specs=in_specs, out_specs=out_specs, out_shape=out_shape,
                              scratch_shapes=scratch_shapes, compiler_params=compiler_params)(*operands)
        return list(outs), []
    n_in, n_out, n_sc = len(in_specs), len(out_specs), len(scratch_shapes)
    r_in, r_out = len(rider.operands), len(rider.out_shape)

    def carried(*refs):
        refs = list(refs)
        cuts = [n_in, r_in, n_out, r_out, n_sc]
        ins, r_ins, outs, r_outs, scratch = [[refs.pop(0) for _ in range(n)] for n in cuts]
        first, last = None, None
        for axis, size in enumerate(grid):
            i = pl.program_id(axis)
            first = (i == 0) if first is None else first & (i == 0)
            last = (i == size - 1) if last is None else last & (i == size - 1)

        @pl.when(first)
        def _():
            rider.start(r_ins, r_outs, *refs)

        body(*ins, *outs, *scratch)

        @pl.when(last)
        def _():
            rider.finish(r_ins, r_outs, *refs)

    sems = [pltpu.SemaphoreType.DMA((rider.n_sems,)), pltpu.SemaphoreType.DMA((rider.n_sems,))]
    outs = pl.pallas_call(carried, name=name, grid=grid, in_specs=list(in_specs) + [ANY] * r_in,
                          out_specs=list(out_specs) + [ANY] * r_out, out_shape=list(out_shape) + rider.out_shape,
                          scratch_shapes=list(scratch_shapes) + sems, compiler_params=compiler_params,
                          input_output_aliases={n_in + i: n_out + o for i, o in rider.aliases.items()},
                          )(*operands, *rider.operands)
    return list(outs[:n_out]), list(outs[n_out:])


def _alone(rider, name):
    def body(*refs):
        ins, outs, sems = refs[:len(rider.operands)], refs[len(rider.operands):-2], refs[-2:]
        rider.start(ins, outs, *sems)
        rider.finish(ins, outs, *sems)

    return pl.pallas_call(
        body, name=name, in_specs=[ANY] * len(rider.operands), out_specs=[ANY] * len(rider.out_shape),
        out_shape=rider.out_shape, input_output_aliases=rider.aliases,
        scratch_shapes=[pltpu.SemaphoreType.DMA((rider.n_sems,)), pltpu.SemaphoreType.DMA((rider.n_sems,))],
    )(*rider.operands)


class _Sems:
    def __init__(self, sems, base):
        self.sems, self.base = sems, base

    @property
    def at(self):
        return self

    def __getitem__(self, k):
        return self.sems.at[self.base + k]


def _join(*riders):
    riders = [r for r in riders if r is not None]
    if len(riders) < 2:
        return riders[0] if riders else None

    def parts(ins, outs, send_sems, recv_sems):
        i = o = s = 0
        for r in riders:
            ni, no = len(r.operands), len(r.out_shape)
            yield r, ins[i:i + ni], outs[o:o + no], _Sems(send_sems, s), _Sems(recv_sems, s)
            i, o, s = i + ni, o + no, s + r.n_sems

    def start(*refs):
        for r, *args in parts(*refs):
            r.start(*args)

    def finish(*refs):
        for r, *args in parts(*refs):
            r.finish(*args)

    aliases, i, o = {}, 0, 0
    for r in riders:
        aliases.update({i + a: o + b for a, b in r.aliases.items()})
        i, o = i + len(r.operands), o + len(r.out_shape)
    return _Rider([a for r in riders for a in r.operands], [s for r in riders for s in r.out_shape],
                  sum(r.n_sems for r in riders), start, finish, aliases)


def _gather_rider(shard, rows=None, into=None, before=None, last=True):
    K, Ns = shard.shape
    kh = K // 2
    rows = rows or (0, kh)

    def copies(w_ref, out_ref, send_sems, recv_sems):
        x, y, c = _place()
        chips = [(_flip(x, fx), _flip(y, fy)) for fx, fy in CHIP_FLIPS]

        def half(chip, which, part):
            return out_ref.at[2 * chip[0] + chip[1], pl.ds(which * kh + part[0], part[1]), :]

        def copy(k, dst, to, src=None):
            return pltpu.make_async_remote_copy(src_ref=dst if src is None else src, dst_ref=dst,
                                                send_sem=send_sems.at[k], recv_sem=recv_sems.at[k],
                                                device_id=to, device_id_type=MESH)

        def first():
            return [copy(j, half((x, y), c, rows), (*chip, c), src=w_ref.at[pl.ds(c * kh + rows[0], rows[1]), :])
                    for j, chip in enumerate(chips)]

        def onward(base, part):
            return [copy(base + j, half(chip, c, part), (x, y, 1 - c)) for j, chip in enumerate(chips)]

        def arriving(base, which, part):
            return [copy(base + j, half(chip, which, part), (x, y, c)) for j, chip in enumerate(chips)]

        return c, first, onward, arriving

    def start(ins, outs, send_sems, recv_sems):
        _, first, onward, _ = copies(ins[0], outs[0], send_sems, recv_sems)
        for cp in first() + (onward(3, before) if before else []):
            cp.start()

    def finish(ins, outs, send_sems, recv_sems):
        c, first, onward, arriving = copies(ins[0], outs[0], send_sems, recv_sems)
        sent = first() + (onward(3, before) if before else [])
        passed = onward(6, rows) if last else [None] * 3
        for arrived, cp in zip(arriving(0, c, rows), passed):
            arrived.wait_recv()
            if last:
                cp.start()
        for arrived in (arriving(3, 1 - c, before) if before else []) + (arriving(6, 1 - c, rows) if last else []):
            arrived.wait_recv()
        for cp in sent + (passed if last else []):
            cp.wait_send()

    full = jax.ShapeDtypeStruct((N_CHIPS, K, Ns), shard.dtype)
    if into is None:
        return _Rider([shard], [full], 9, start, finish)
    return _Rider([shard, into], [full], 9, start, finish, aliases={1: 0})


def _pair_rider(g_full):
    Q, K, Ns = g_full.shape
    kh = K // 2

    def copy(g_ref, got_ref, send_sems, recv_sems):
        x, y, c = _place()
        return pltpu.make_async_remote_copy(src_ref=g_ref.at[:, pl.ds((1 - c) * kh, kh), :], dst_ref=got_ref,
                                            send_sem=send_sems.at[0], recv_sem=recv_sems.at[0],
                                            device_id=(x, y, 1 - c), device_id_type=MESH)

    def start(ins, outs, send_sems, recv_sems):
        copy(ins[0], outs[0], send_sems, recv_sems).start()

    def finish(ins, outs, send_sems, recv_sems):
        copy(ins[0], outs[0], send_sems, recv_sems).wait()

    return _Rider([g_full], [jax.ShapeDtypeStruct((Q, kh, Ns), g_full.dtype)], 1, start, finish)


def _acts_rider(a, b):
    T, K = a.shape
    kh = K // 2

    def copies(ins, outs, send_sems, recv_sems):
        x, y, c = _place()
        pair = [(ins[0].at[:, pl.ds((1 - c) * kh, kh)], outs[0]), (ins[1], outs[1])]
        return [pltpu.make_async_remote_copy(src_ref=src, dst_ref=dst, send_sem=send_sems.at[k], recv_sem=recv_sems.at[k],
                                             device_id=(x, y, 1 - c), device_id_type=MESH)
                for k, (src, dst) in enumerate(pair)]

    def start(*refs):
        for cp in copies(*refs):
            cp.start()

    def finish(*refs):
        for cp in copies(*refs):
            cp.wait()

    return _Rider([a, b], [jax.ShapeDtypeStruct((T, kh), a.dtype), jax.ShapeDtypeStruct(b.shape, b.dtype)], 2,
                  start, finish)


def _share_rider(block):
    K, Ns = block.shape
    kh = K // 2

    def halves(out_ref):
        x, y, c = _place()
        return out_ref.at[pl.ds(c * kh, kh), :], out_ref.at[pl.ds((1 - c) * kh, kh), :], (x, y, 1 - c)

    def start(ins, outs, send_sems, recv_sems):
        mine, _, sibling = halves(outs[0])
        pltpu.make_async_remote_copy(src_ref=mine, dst_ref=mine, send_sem=send_sems.at[0], recv_sem=recv_sems.at[0],
                                     device_id=sibling, device_id_type=MESH).start()

    def finish(ins, outs, send_sems, recv_sems):
        mine, theirs, sibling = halves(outs[0])
        pltpu.make_async_remote_copy(src_ref=theirs, dst_ref=theirs, send_sem=send_sems.at[0], recv_sem=recv_sems.at[0],
                                     device_id=sibling, device_id_type=MESH).wait_recv()
        pltpu.make_async_remote_copy(src_ref=mine, dst_ref=mine, send_sem=send_sems.at[0], recv_sem=recv_sems.at[0],
                                     device_id=sibling, device_id_type=MESH).wait_send()

    return _Rider([block], [jax.ShapeDtypeStruct((K, Ns), block.dtype)], 1, start, finish, aliases={0: 0})


def _chips_rider(pair_sum, rows=None, into=None):
    Q, kh, Ns = pair_sum.shape
    first_row, n_rows = rows or (0, kh)

    def copies(p_ref, got_ref, send_sems, recv_sems):
        x, y, c = _place()
        part = pl.ds(first_row, n_rows)
        out = []
        for j, (fx, fy) in enumerate(CHIP_FLIPS):
            px, py = _flip(x, fx), _flip(y, fy)
            out.append(pltpu.make_async_remote_copy(
                src_ref=p_ref.at[2 * px + py, part, :], dst_ref=got_ref.at[j, part, :], send_sem=send_sems.at[j],
                recv_sem=recv_sems.at[j], device_id=(px, py, c), device_id_type=MESH))
        return out

    def start(ins, outs, send_sems, recv_sems):
        for cp in copies(ins[0], outs[0], send_sems, recv_sems):
            cp.start()

    def finish(ins, outs, send_sems, recv_sems):
        sends = copies(ins[0], outs[0], send_sems, recv_sems)
        for cp in sends:
            cp.wait_recv()
        for cp in sends:
            cp.wait_send()

    got = jax.ShapeDtypeStruct((Q - 1, kh, Ns), pair_sum.dtype)
    if into is None:
        return _Rider([pair_sum], [got], 3, start, finish)
    return _Rider([pair_sum, into], [got], 3, start, finish, aliases={1: 0})


def _mm(a, b, *, grid, a_spec, b_spec, o_spec, o_shape, o_dtype, dims, acc_shape, name, rider=None, zero_rows=0,
        vmem_mb=48):
    nk = grid[2]

    def body(a_ref, b_ref, o_ref, *scratch):
        if zero_rows:
            @pl.when(pl.program_id(0) < zero_rows)
            def _():
                o_ref[...] = jnp.zeros_like(o_ref)

            @pl.when(pl.program_id(0) >= zero_rows)
            def _():
                o_ref[...] = _dot(a_ref[...], b_ref[...], dims).astype(o_ref.dtype)
            return
        part = _dot(a_ref[...], b_ref[...], dims)
        if nk == 1:
            o_ref[...] = part.astype(o_ref.dtype)
            return
        acc_ref, = scratch
        k = pl.program_id(2)

        @pl.when(k == 0)
        def _():
            acc_ref[...] = part

        @pl.when(k > 0)
        def _():
            acc_ref[...] += part

        @pl.when(k == nk - 1)
        def _():
            o_ref[...] = acc_ref[...].astype(o_ref.dtype)

    (out,), rode = _call(
        body, rider, name=name, grid=grid, in_specs=[a_spec, b_spec], out_specs=[o_spec],
        out_shape=[jax.ShapeDtypeStruct(o_shape, o_dtype)],
        scratch_shapes=[] if nk == 1 else [pltpu.VMEM(acc_shape, F32)],
        compiler_params=_params(("parallel", "parallel", "arbitrary") if rider is None else ("arbitrary",) * 3, vmem_mb),
        operands=(a, b))
    return out if rider is None else (out, rode)


def _mm_nn(a, w, *, tm, tn, tk, name, rider=None, cols=None, o_dtype=F32, pad_rows=0):
    T, K = a.shape
    Q, _, Ns = w.shape
    nbs = Ns // tn
    tm = min(tm, T)
    j0, j1 = cols or (0, Q * nbs)
    lead = pad_rows // tm
    return _mm(a, w, grid=(lead + T // tm, j1 - j0, K // tk),
               a_spec=pl.BlockSpec((tm, tk), lambda i, j, k: (jnp.maximum(i - lead, 0), k)),
               b_spec=pl.BlockSpec((None, tk, tn), lambda i, j, k: ((j + j0) // nbs, k, (j + j0) % nbs)),
               o_spec=pl.BlockSpec((tm, tn), lambda i, j, k: (i, j)),
               o_shape=(pad_rows + T, (j1 - j0) * tn), o_dtype=o_dtype, dims=NN, acc_shape=(tm, tn), name=name,
               rider=rider, zero_rows=lead)


def _col_blocks(g, rows, tn, at):
    if g.ndim == 2:
        return pl.BlockSpec((rows, tn), at)
    per = g.shape[2] // tn

    def stacked(*idx):
        r, c = at(*idx)
        return c // per, r, c % per

    return pl.BlockSpec((None, rows, tn), stacked)


def _mm_nt(g, w, *, tm, to, tn, name, rider=None):
    T = g.shape[-2]
    Q, K, Ns = w.shape
    nbs = Ns // tn
    tm = min(tm, T)
    return _mm(g, w, grid=(T // tm, K // to, Q * nbs),
               a_spec=_col_blocks(g, tm, tn, lambda i, j, n: (i, n)),
               b_spec=pl.BlockSpec((None, to, tn), lambda i, j, n: (n // nbs, j, n % nbs)),
               o_spec=pl.BlockSpec((tm, to), lambda i, j, n: (i, j)),
               o_shape=(T, K), o_dtype=F32, dims=NT, acc_shape=(tm, to), name=name, rider=rider)


def _mm_tn(a, g, *, q, tk, tn, tt, name, rider=None):
    T, K = a.shape
    Ns = g.shape[-1] * (g.ndim - 1) // q
    nbs = Ns // tn
    return _mm(a, g, grid=(K // tk, q * nbs, T // tt),
               a_spec=pl.BlockSpec((tt, tk), lambda i, j, t: (t, i)),
               b_spec=_col_blocks(g, tt, tn, lambda i, j, t: (t, j)),
               o_spec=pl.BlockSpec((None, tk, tn), lambda i, j, t: (j // nbs, i, j % nbs)),
               o_shape=(q, K, Ns), o_dtype=BF16, dims=TN, acc_shape=(tk, tn), name=name, rider=rider)


def _mm_tn_add(a, g, part, *, tk, tn, name, rider=None):
    T, K = a.shape
    Q, _, Ns = part.shape
    nbs = Ns // tn

    def body(a_ref, g_ref, p_ref, o_ref):
        o_ref[...] = (_dot(a_ref[...], g_ref[...], TN) + p_ref[...].astype(F32)).astype(o_ref.dtype)

    blk = pl.BlockSpec((None, tk, tn), lambda i, j: (j // nbs, i, j % nbs))
    (out,), rode = _call(
        body, rider, name=name, grid=(K // tk, Q * nbs),
        in_specs=[pl.BlockSpec((T, tk), lambda i, j: (0, i)), _col_blocks(g, T, tn, lambda i, j: (0, j)), blk],
        out_specs=[blk], out_shape=[jax.ShapeDtypeStruct((Q, K, Ns), BF16)], scratch_shapes=[],
        compiler_params=_params(("arbitrary", "arbitrary"), 48), operands=(a, g, part))
    return out, rode


def _ln(u):
    mu = jnp.mean(u, axis=-1, keepdims=True)
    d = u - mu
    r = lax.rsqrt(jnp.mean(d * d, axis=-1, keepdims=True) + EPS)
    return d * r, r


def _ln_bwd(dy, un, r):
    return r * (dy - jnp.mean(dy, axis=-1, keepdims=True) - un * jnp.mean(dy * un, axis=-1, keepdims=True))


def _colsum(v):
    return jnp.sum(v, axis=0, keepdims=True)


def _rowwise(name, fn, bigs, vecs, out_dtypes, n_acc, tm=128, rider=None):
    T, D = bigs[0].shape
    nb, nv, no = len(bigs), len(vecs), len(out_dtypes)

    def body(*refs):
        outs, accs = fn([r[...] for r in refs[:nb]], [r[...] for r in refs[nb:nb + nv]])
        for r, o in zip(refs[nb + nv:nb + nv + no], outs):
            r[...] = o.astype(r.dtype)
        if n_acc:
            acc_ref = refs[nb + nv + no]

            @pl.when(pl.program_id(0) == 0)
            def _():
                acc_ref[...] = jnp.zeros_like(acc_ref)

            for row, a in enumerate(accs):
                acc_ref[row:row + 1, :] += a

    big_spec = pl.BlockSpec((tm, D), lambda i: (i, 0))
    vec_spec = pl.BlockSpec((1, D), lambda i: (0, 0))
    out_shape = [jax.ShapeDtypeStruct((T, D), dt) for dt in out_dtypes]
    out_specs = [big_spec] * no
    if n_acc:
        out_shape.append(jax.ShapeDtypeStruct((8, D), F32))
        out_specs.append(pl.BlockSpec((8, D), lambda i: (0, 0)))
    outs, rode = _call(
        body, rider, name=name, grid=(T // tm,), in_specs=[big_spec] * nb + [vec_spec] * nv,
        out_specs=out_specs, out_shape=out_shape, scratch_shapes=[],
        compiler_params=_params(("arbitrary",), 48), operands=(*bigs, *vecs))
    return outs if rider is None else (outs, rode)


def _to_bf16(w, name, rider=None):
    R, C = w.shape
    tr = _row_tile(R, C)

    def body(w_ref, o_ref):
        o_ref[...] = w_ref[...].astype(o_ref.dtype)

    blk = pl.BlockSpec((tr, C), lambda i: (i, 0))
    return _call(body, rider, name=name, grid=(R // tr,), in_specs=[blk], out_specs=[blk],
                 out_shape=[jax.ShapeDtypeStruct((R, C), BF16)], scratch_shapes=[],
                 compiler_params=_params(("arbitrary",)), operands=(w,))


def _pre_mixer(x, scale1, shift1, rider=None):
    def fn(b, v):
        xn, _ = _ln(b[0])
        return [xn * (1.0 + v[0]) + v[1]], []
    outs = _rowwise("pre_mixer", fn, [x], [scale1, shift1], [BF16], 0, rider=rider)
    return outs[0] if rider is None else (outs[0][0], outs[1])


def _post_mixer(mix, x, gate1, g1, b1, scale2, shift2, rider=None):
    def fn(b, v):
        un1, _ = _ln(ALPHA * b[1] + v[0] * b[0])
        x1 = un1 * v[1] + v[2]
        xn1, _ = _ln(x1)
        return [x1, xn1 * (1.0 + v[3]) + v[4]], []
    return _rowwise("post_mixer", fn, [mix, x], [gate1, g1, b1, scale2, shift2], [F32, BF16], 0, rider=rider)


def _loss_head(f, x1, tgt, gate2, g2, b2):
    def fn(b, v):
        ff, xx, tt = b
        d_model = ff.shape[-1]
        un2, r2 = _ln(ALPHA * xx + v[0] * ff)
        err = un2 * v[1] + v[2] - tt
        dy = err * (1.0 / d_model)
        du2 = _ln_bwd(dy * v[1], un2, r2)
        return [du2, du2 * v[0]], [_colsum(dy * un2), _colsum(dy), _colsum(du2 * ff), _colsum(err * err)]
    return _rowwise("loss_head", fn, [f, x1, tgt], [gate2, g2, b2], [F32, BF16], 4)


def _mid_bwd(dh2, du2, x1, mix, x, gate1, g1, scale2, rider=None):
    def fn(b, v):
        dh, du, xx1, mm, xx = b
        xn1, r1n = _ln(xx1)
        dx1 = ALPHA * du + _ln_bwd(dh * (1.0 + v[2]), xn1, r1n)
        un1, r1 = _ln(ALPHA * xx + v[0] * mm)
        du1 = _ln_bwd(dx1 * v[1], un1, r1)
        return [du1, du1 * v[0]], [_colsum(dh * xn1), _colsum(dh), _colsum(dx1 * un1), _colsum(dx1),
                                   _colsum(du1 * mm)]
    return _rowwise("mid_bwd", fn, [dh2, du2, x1, mix, x], [gate1, g1, scale2], [F32, BF16], 5, rider=rider)


def _first_bwd(dh1, du1, x, scale1, rider=None):
    def fn(b, v):
        dh, du, xx = b
        xn, r0 = _ln(xx)
        return [ALPHA * du + _ln_bwd(dh * (1.0 + v[0]), xn, r0)], [_colsum(dh * xn), _colsum(dh)]
    return _rowwise("first_bwd", fn, [dh1, du1, x], [scale1], [F32], 2, rider=rider)


def _ffn_in_swiglu(h2, w, *, tm, tn, rider=None):
    T, K = h2.shape
    Q, _, Ns = w.shape
    nbs = Ns // tn
    half = Q * nbs // 2
    tm = min(tm, T)

    def body(a_ref, wg_ref, wu_ref, g_ref, u_ref, act_ref):
        a = a_ref[...]
        g, u = _dot(a, wg_ref[...]), _dot(a, wu_ref[...])
        g_ref[...] = g.astype(g_ref.dtype)
        u_ref[...] = u.astype(u_ref.dtype)
        act_ref[...] = (g * jax.nn.sigmoid(g) * u).astype(act_ref.dtype)

    cols = lambda first: pl.BlockSpec((None, K, tn), lambda i, j: ((j + first) // nbs, 0, (j + first) % nbs))
    blk = pl.BlockSpec((tm, tn), lambda i, j: (i, j))
    return _call(
        body, rider, name="ffn_in", grid=(T // tm, half),
        in_specs=[pl.BlockSpec((tm, K), lambda i, j: (i, 0)), cols(0), cols(half)], out_specs=[blk] * 3,
        out_shape=[jax.ShapeDtypeStruct((T, half * tn), BF16)] * 3, scratch_shapes=[],
        compiler_params=_params(("arbitrary", "arbitrary"), 48), operands=(h2, w, w))


def _d_act_swiglu(df, w, gate, up, *, tm, to, rider=None):
    T, N = df.shape
    F = w.shape[1]
    tm = min(tm, T)

    def body(df_ref, w_ref, g_ref, u_ref, o_ref):
        d = _dot(df_ref[...], w_ref[...], NT)
        g = g_ref[...].astype(F32)
        s = jax.nn.sigmoid(g)
        o_ref[0] = (d * u_ref[...].astype(F32) * s * (1.0 + g * (1.0 - s))).astype(o_ref.dtype)
        o_ref[1] = (d * g * s).astype(o_ref.dtype)

    blk = pl.BlockSpec((tm, to), lambda i, j: (i, j))
    return _call(
        body, rider, name="d_act", grid=(T // tm, F // to),
        in_specs=[pl.BlockSpec((tm, N), lambda i, j: (i, 0)), pl.BlockSpec((None, to, N), lambda i, j: (0, j, 0)), blk, blk],
        out_specs=[pl.BlockSpec((2, tm, to), lambda i, j: (0, i, j))],
        out_shape=[jax.ShapeDtypeStruct((2, T, F), BF16)], scratch_shapes=[],
        compiler_params=_params(("arbitrary", "arbitrary"), 48), operands=(df, w, gate, up))


PAIR = 2


def _fill_table(table_ref, band_ref):
    table_ref[...] = jnp.full(table_ref.shape, NEG, F32)
    for e in range(PAIR):
        for g in range(QG):
            table_ref[e, g * CHUNK:(g + 1) * CHUNK, g * CHUNK:g * CHUNK + BAND] = band_ref[e]


def _attn_probs(q_ref, k_ref, bias_ref, e, step):
    start = pl.multiple_of(step * QROWS, QROWS)
    lanes = pl.ds(e * HD_A, HD_A)
    s = _dot(q_ref[:, lanes], k_ref[pl.ds(start + ZPAD - KPAD, UNION), lanes], NT) * (HD_A ** -0.5) + bias_ref[e]
    col = lax.broadcasted_iota(jnp.int32, s.shape, 1)
    s = jnp.where(col + start >= KPAD, s, NEG)
    p = jnp.exp(s - jnp.max(s, axis=-1, keepdims=True))
    return p / jnp.sum(p, axis=-1, keepdims=True), start


def _attn_specs(T, n_pairs):
    wide = PAIR * HD_A
    per_step = pl.BlockSpec((QROWS, wide), lambda hp, n: (n, hp))
    queries = pl.BlockSpec((QROWS, wide), lambda hp, n: (n + ZPAD // QROWS, hp))
    keys = pl.BlockSpec((ZPAD + T, wide), lambda hp, n: (0, n_pairs + hp))
    values = pl.BlockSpec((ZPAD + T, wide), lambda hp, n: (0, 2 * n_pairs + hp))
    grads = pl.BlockSpec((T, wide), lambda hp, n: (0, hp))
    table = pl.BlockSpec((PAIR, CHUNK, BAND), lambda hp, n: (hp, 0, 0))
    vec = pl.BlockSpec((1, wide), lambda hp, n: (0, hp))
    return per_step, queries, keys, values, grads, table, vec


def _probs_spec():
    return pl.BlockSpec((PAIR, QROWS, UNION), lambda hp, n: (hp, n, 0))


def _attn_fwd(qkv, bias, gain, rider=None):
    T = qkv.shape[0] - ZPAD
    W = gain.shape[1]
    n_pairs = W // (PAIR * HD_A)

    def body(q_ref, k_ref, v_ref, band_ref, gain_ref, o_ref, p_ref, table_ref):
        @pl.when(pl.program_id(1) == 0)
        def _():
            _fill_table(table_ref, band_ref)

        for e in range(PAIR):
            lanes = pl.ds(e * HD_A, HD_A)
            p, start = _attn_probs(q_ref, k_ref, table_ref, e, pl.program_id(1))
            p_ref[e] = p.astype(p_ref.dtype)
            o = _dot(p_ref[e], v_ref[pl.ds(start + ZPAD - KPAD, UNION), lanes])
            rr = lax.rsqrt(jnp.mean(o * o, axis=-1, keepdims=True) + EPS)
            o_ref[:, lanes] = (o * rr * gain_ref[:, lanes]).astype(o_ref.dtype)

    per_step, queries, keys, values, _, table, vec = _attn_specs(T, n_pairs)
    return _call(
        body, rider, name="attn_fwd", grid=(n_pairs, T // QROWS), in_specs=[queries, keys, values, table, vec],
        out_specs=[per_step, _probs_spec()],
        out_shape=[jax.ShapeDtypeStruct((T, W), BF16), jax.ShapeDtypeStruct((n_pairs * PAIR, T, UNION), BF16)],
        scratch_shapes=[pltpu.VMEM((PAIR, QROWS, UNION), F32)],
        compiler_params=_params(("arbitrary", "arbitrary"), 40), operands=(qkv, qkv, qkv, bias, gain))


def _attn_bwd(qkv, probs, gain, dmixin, rider=None):
    T = qkv.shape[0] - ZPAD
    W = gain.shape[1]
    n_pairs = W // (PAIR * HD_A)
    scale = HD_A ** -0.5

    def body(q_ref, k_ref, v_ref, p_ref, gain_ref, don_ref, dq_ref, dkb_ref, dvb_ref, dband_ref, dgain_ref,
             dtable_ref, dk_ref, dv_ref):
        n = pl.program_id(1)

        @pl.when(n == 0)
        def _():
            dk_ref[...] = jnp.zeros_like(dk_ref)
            dv_ref[...] = jnp.zeros_like(dv_ref)
            dtable_ref[...] = jnp.zeros_like(dtable_ref)
            dgain_ref[...] = jnp.zeros_like(dgain_ref)

        for e in range(PAIR):
            lanes = pl.ds(e * HD_A, HD_A)
            start = pl.multiple_of(n * QROWS, QROWS)
            keys, in_qkv = pl.ds(start, UNION), pl.ds(start + ZPAD - KPAD, UNION)
            pb = p_ref[e]
            p = pb.astype(F32)
            vb = v_ref[in_qkv, lanes]
            o = _dot(pb, vb)
            rr = lax.rsqrt(jnp.mean(o * o, axis=-1, keepdims=True) + EPS)
            on = o * rr
            d_on = don_ref[:, lanes]
            dgain_ref[:, lanes] += _colsum(d_on * on)
            dyo = d_on * gain_ref[:, lanes]
            do = rr * (dyo - on * jnp.mean(dyo * on, axis=-1, keepdims=True))
            dob = do.astype(BF16)
            dp = _dot(dob, vb, NT)
            ds = p * (dp - jnp.sum(do * o, axis=-1, keepdims=True))
            dtable_ref[e] += ds
            dsb = ds.astype(BF16)
            dq_ref[:, lanes] = (_dot(dsb, k_ref[in_qkv, lanes]) * scale).astype(dq_ref.dtype)
            dk_ref[keys, lanes] += _dot(dsb, q_ref[:, lanes], TN) * scale
            dv_ref[keys, lanes] += _dot(pb, dob, TN)

        @pl.when(n == T // QROWS - 1)
        def _():
            for e in range(PAIR):
                dband_ref[e] = sum(dtable_ref[e, g * CHUNK:(g + 1) * CHUNK, g * CHUNK:g * CHUNK + BAND]
                                   for g in range(QG))
            dkb_ref[...] = dk_ref[KPAD:, :].astype(dkb_ref.dtype)
            dvb_ref[...] = dv_ref[KPAD:, :].astype(dvb_ref.dtype)

    per_step, queries, keys, values, grads, table, vec = _attn_specs(T, n_pairs)
    H = n_pairs * PAIR
    return _call(
        body, rider, name="attn_bwd", grid=(n_pairs, T // QROWS),
        in_specs=[queries, keys, values, _probs_spec(), vec, per_step],
        out_specs=[per_step, grads, grads, table, vec],
        out_shape=[jax.ShapeDtypeStruct((T, W), BF16)] * 3 + [jax.ShapeDtypeStruct((H, CHUNK, BAND), F32),
                                                              jax.ShapeDtypeStruct((1, W), F32)],
        scratch_shapes=[pltpu.VMEM((PAIR, QROWS, UNION), F32)] + [pltpu.VMEM((KPAD + T, PAIR * HD_A), F32)] * 2,
        compiler_params=_params(("arbitrary", "arbitrary"), 40),
        operands=(qkv, qkv, qkv, probs, gain, dmixin))


N_DIAG = CHUNK + BAND - 1


def _bias_band(rel_bias):
    H = rel_bias.shape[0]
    idx = np.clip(BAND - 1 - np.arange(N_DIAG), -MAX_REL, MAX_REL) + MAX_REL
    rolled = rel_bias[:, idx[(np.arange(N_DIAG) + CHUNK - 1) % N_DIAG]]
    flat = jnp.broadcast_to(rolled[:, None, :], (H, CHUNK, N_DIAG)).reshape(H, CHUNK * N_DIAG)
    return flat[:, :CHUNK * (N_DIAG - 1)].reshape(H, CHUNK, N_DIAG - 1)[:, :, :BAND]


def _bias_band_grad(dband):
    H = dband.shape[0]
    skew = jnp.pad(dband, ((0, 0), (0, 0), (CHUNK - 1, 0))).reshape(H, CHUNK * N_DIAG)
    skew = jnp.pad(skew, ((0, 0), (0, CHUNK))).reshape(H, CHUNK, N_DIAG + 1)
    diag = jnp.sum(skew, axis=1)[:, :N_DIAG]
    n_far = BAND - MAX_REL
    far = jnp.sum(diag[:, :n_far], axis=1, keepdims=True)
    near = diag[:, n_far:][:, ::-1]
    zeros = jnp.zeros((H, MAX_REL - (CHUNK - 1)), F32)
    return jnp.concatenate([zeros, near, far], axis=1)


def _tri(n, lower):
    r = lax.broadcasted_iota(jnp.int32, (n, n), 0)
    c = lax.broadcasted_iota(jnp.int32, (n, n), 1)
    return jnp.where((c <= r) if lower else (c >= r), 1.0, 0.0).astype(F32)


def _hgrn_gates(zq_ref, zf_ref, lbl_ref, q_s, k_s, b_s):
    lb = jax.nn.sigmoid(lbl_ref[0:1, :] - lbl_ref[1:2, :])
    zq = zq_ref[...]
    sig = jax.nn.sigmoid(zf_ref[...])
    f = lb + (1.0 - lb) * sig
    sq = jax.nn.sigmoid(zq)
    q_s[...] = zq * sq
    k_s[...] = 1.0 - f
    b_s[...] = _dot(_tri(CHUNK, True), jnp.log(f), precision=HIGHEST)
    return lb, sig, f, sq


def _sub_rows(i):
    return pl.ds(i * SUB, SUB)


def _row_mask(s):
    return lax.broadcasted_iota(jnp.int32, (SUB, HD_B), 0) >= s


def _decay_from(b_sub, b_row, s):
    return jnp.where(_row_mask(s), jnp.exp(jnp.minimum(b_sub - b_row, 0.0)), 0.0)


def _hgrn_fwd(proj, lb_logits, gnorm_g, rider=None):
    T = proj.shape[0]
    nC = T // CHUNK
    W = lb_logits.shape[1]
    G = W // HD_B // HGRN_HEADS
    col0 = (proj.shape[1] - 4 * W) // (HD_B * HGRN_HEADS)
    wide = HGRN_HEADS * HD_B

    def body(*refs):
        @pl.when(pl.program_id(1) == 0)
        def _():
            refs[9][...] = jnp.zeros_like(refs[9])

        for h in range(HGRN_HEADS):
            lanes = pl.ds(h * HD_B, HD_B)
            one_head(*[r.at[:, lanes] for r in refs[:5]], refs[5], *[r.at[:, lanes] for r in refs[6:8]],
                     *[r.at[h] for r in refs[8:]])

    def one_head(zq_ref, zf_ref, xi_ref, zg_ref, lbl_ref, gn_ref, mix_ref, o_ref, stall_ref, st_ref, q_s, k_s, b_s, acc_s):
        _hgrn_gates(zq_ref, zf_ref, lbl_ref, q_s, k_s, b_s)
        q, k, b = q_s[...], k_s[...], b_s[...]
        st = st_ref[...]
        stall_ref[...] = st
        b_last = b_s[CHUNK - 1:CHUNK, :]
        acc_s[...] = _dot((q * jnp.exp(b)).astype(BF16), st.astype(BF16), NT)
        for i in range(CHUNK // SUB):
            rows = _sub_rows(i)
            q_i, b_i = q_s[rows, :], b_s[rows, :]
            acc = jnp.zeros((SUB, HD_B), F32)
            if i:
                past = pl.ds(0, i * SUB)
                b_ref = b_s[i * SUB - 1:i * SUB, :]
                qs = (q_i * jnp.exp(b_i - b_ref)).astype(BF16)
                ks = (k_s[past, :] * jnp.exp(b_ref - b_s[past, :])).astype(BF16)
                acc += _dot(_dot(qs, ks, NT).astype(BF16), xi_ref[past, :].astype(BF16))
            for s in range(SUB):
                row = pl.ds(i * SUB + s, 1)
                w = q_i * _decay_from(b_i, b_s[row, :], s)
                acc += jnp.sum(w * k_s[row, :], axis=-1, keepdims=True) * xi_ref[row, :]
            acc_s[rows, :] += acc
        o = acc_s[...]
        kd = (k * jnp.exp(b_last - b)).astype(BF16)
        st_ref[...] = st * jnp.exp(b_last) + _dot(xi_ref[...].astype(BF16), kd, TN)
        o_ref[...] = o
        zg = zg_ref[...]
        rr = lax.rsqrt(jnp.mean(o * o, axis=-1, keepdims=True) + EPS)
        mix_ref[...] = (o * rr * gn_ref[...] * (zg * jax.nn.sigmoid(zg))).astype(mix_ref.dtype)

    col = lambda part: pl.BlockSpec((CHUNK, wide), lambda g, n: (n, col0 + part * G + g))
    out_blk = pl.BlockSpec((CHUNK, wide), lambda g, n: (n, g))
    tile = pltpu.VMEM((HGRN_HEADS, CHUNK, HD_B), F32)
    return _call(
        body, rider, name="hgrn_fwd", grid=(G, nC),
        in_specs=[col(0), col(1), col(2), col(3), pl.BlockSpec((2, wide), lambda g, n: (0, g)),
                  pl.BlockSpec((1, HD_B), lambda g, n: (0, 0))],
        out_specs=[out_blk, out_blk, pl.BlockSpec((HGRN_HEADS, None, HD_B, HD_B), lambda g, n: (g, n, 0, 0))],
        out_shape=[jax.ShapeDtypeStruct((T, W), BF16), jax.ShapeDtypeStruct((T, W), F32),
                   jax.ShapeDtypeStruct((G * HGRN_HEADS, nC, HD_B, HD_B), F32)],
        scratch_shapes=[pltpu.VMEM((HGRN_HEADS, HD_B, HD_B), F32), tile, tile, tile, tile],
        compiler_params=_params(("arbitrary", "arbitrary")),
        operands=(proj, proj, proj, proj, lb_logits, gnorm_g))


def _hgrn_bwd(proj, lb_logits, gnorm_g, o_b, st_all, dmixin, d_attn, rider=None):
    T = proj.shape[0]
    nC = T // CHUNK
    W = lb_logits.shape[1]
    wa = d_attn[0].shape[1]
    assert W == HGRN_HEADS * HD_B, "one grid step takes every head: it writes whole rows of d proj"
    G = W // HD_B // HGRN_HEADS
    wide = HGRN_HEADS * HD_B
    col0 = (proj.shape[1] - 4 * W) // wide
    dcol0 = (dmixin.shape[1] - W) // wide

    def body(*refs):
        g, n = pl.program_id(0), pl.program_id(1)
        dproj_ref, dl0_ref, dgn_ref, dst_ref = refs[12:16]
        for i in range(3):
            dproj_ref[:, i * wa:(i + 1) * wa] = refs[9 + i][...]

        @pl.when(n == 0)
        def _():
            dst_ref[...] = jnp.zeros_like(dst_ref)
            dl0_ref[...] = jnp.zeros_like(dl0_ref)

        @pl.when((n == 0) & (g == 0))
        def _():
            dgn_ref[...] = jnp.zeros_like(dgn_ref)

        for h in range(HGRN_HEADS):
            lanes = pl.ds(h * HD_B, HD_B)
            cut = lambda r: r.at[:, lanes]
            parts = [dproj_ref.at[:, pl.ds(3 * wa + part * W + h * HD_B, HD_B)] for part in range(4)]
            one_head(*[cut(r) for r in refs[:5]], refs[5], cut(refs[6]), refs[7].at[h], cut(refs[8]),
                     *parts, cut(dl0_ref), dgn_ref, *[r.at[h] for r in refs[15:]])

    def one_head(zq_ref, zf_ref, xi_ref, zg_ref, lbl_ref, gn_ref, o_ref, st_ref, dout_ref,
                 dzq_ref, dzf_ref, dxi_ref, dzg_ref, dl0_ref, dgn_ref, dst_ref, q_s, k_s, b_s, do_s, dq_s, dk_s, di_s):
        lb, sig, f, sq = _hgrn_gates(zq_ref, zf_ref, lbl_ref, q_s, k_s, b_s)
        q, k, b = q_s[...], k_s[...], b_s[...]
        zg, o, dout = zg_ref[...], o_ref[...], dout_ref[...]
        sg = jax.nn.sigmoid(zg)
        rr = lax.rsqrt(jnp.mean(o * o, axis=-1, keepdims=True) + EPS)
        on = o * rr
        gn = gn_ref[...]
        dzg_ref[...] = (dout * on * gn * sg * (1.0 + zg * (1.0 - sg))).astype(dzg_ref.dtype)
        d_on = dout * zg * sg
        dgn_ref[...] += _colsum(d_on * on)
        d_on = d_on * gn
        do = rr * (d_on - on * jnp.mean(d_on * on, axis=-1, keepdims=True))
        do_s[...] = do
        dob = do.astype(BF16)
        st, dst = st_ref[...], dst_ref[...]
        b_last = b_s[CHUNK - 1:CHUNK, :]
        eb, e_last, k_dec = jnp.exp(b), jnp.exp(b_last), jnp.exp(b_last - b)
        qt, kd = q * eb, k * k_dec
        dstb = dst.astype(BF16)
        xib = xi_ref[...].astype(BF16)
        d_kd = _dot(xib, dstb)
        dq_s[...] = _dot(dob, st.astype(BF16)) * eb
        dk_s[...] = d_kd * k_dec
        di_s[...] = _dot(kd.astype(BF16), dstb, NT)
        d_b_last = e_last * _colsum(st * dst) + _colsum(d_kd * kd)
        dst_ref[...] = _dot(dob, qt.astype(BF16), TN) + dst * e_last
        for i in range(CHUNK // SUB):
            rows = _sub_rows(i)
            q_i, b_i, do_i = q_s[rows, :], b_s[rows, :], do_s[rows, :]
            dq_i = jnp.zeros((SUB, HD_B), F32)
            if i:
                past = pl.ds(0, i * SUB)
                b_ref = b_s[i * SUB - 1:i * SUB, :]
                e_q, e_k = jnp.exp(b_i - b_ref), jnp.exp(b_ref - b_s[past, :])
                qs, ks = (q_i * e_q).astype(BF16), (k_s[past, :] * e_k).astype(BF16)
                xi_p, do_b = xi_ref[past, :].astype(BF16), do_i.astype(BF16)
                di_s[past, :] += _dot(_dot(ks, qs, NT).astype(BF16), do_b)
                dq_i += _dot(_dot(do_b, xi_p, NT).astype(BF16), ks) * e_q
                dk_s[past, :] += _dot(_dot(xi_p, do_b, NT).astype(BF16), qs) * e_k
            for s in range(SUB):
                row = pl.ds(i * SUB + s, 1)
                k_row, i_row = k_s[row, :], xi_ref[row, :]
                e = _decay_from(b_i, b_s[row, :], s)
                w = q_i * e
                a_col = jnp.sum(w * k_row, axis=-1, keepdims=True)
                da_col = jnp.sum(do_i * i_row, axis=-1, keepdims=True)
                di_s[row, :] += _colsum(a_col * do_i)
                dq_i += da_col * e * k_row
                dk_s[row, :] += _colsum(da_col * w)
            dq_s[rows, :] += dq_i
        dq, dk = dq_s[...], dk_s[...]
        db = q * dq - k * dk
        is_last = lax.broadcasted_iota(jnp.int32, (CHUNK, HD_B), 0) == CHUNK - 1
        db = db + jnp.where(is_last, d_b_last, 0.0)
        df = _dot(_tri(CHUNK, False), db, precision=HIGHEST) / f - dk
        dzf_ref[...] = (df * (1.0 - lb) * sig * (1.0 - sig)).astype(dzf_ref.dtype)
        dl0_ref[...] += _colsum(df * (1.0 - sig)) * (lb * (1.0 - lb))
        zq = zq_ref[...]
        dzq_ref[...] = (dq * sq * (1.0 + zq * (1.0 - sq))).astype(dzq_ref.dtype)
        dxi_ref[...] = di_s[...].astype(dxi_ref.dtype)

    rev = lambda n: nC - 1 - n
    col = lambda part: pl.BlockSpec((CHUNK, wide), lambda g, n: (rev(n), col0 + part * G + g))
    blk = pl.BlockSpec((CHUNK, wide), lambda g, n: (rev(n), g))
    tile = pltpu.VMEM((HGRN_HEADS, CHUNK, HD_B), F32)
    rows = lambda width: pl.BlockSpec((CHUNK, width), lambda g, n: (rev(n), 0))
    return _call(
        body, rider, name="hgrn_bwd", grid=(G, nC),
        in_specs=[col(0), col(1), col(2), col(3), pl.BlockSpec((2, wide), lambda g, n: (0, g)),
                  pl.BlockSpec((1, HD_B), lambda g, n: (0, 0)), blk,
                  pl.BlockSpec((HGRN_HEADS, None, HD_B, HD_B), lambda g, n: (g, rev(n), 0, 0)),
                  pl.BlockSpec((CHUNK, wide), lambda g, n: (rev(n), dcol0 + g)), rows(wa), rows(wa), rows(wa)],
        out_specs=[rows(3 * wa + 4 * W), pl.BlockSpec((1, wide), lambda g, n: (0, g)),
                   pl.BlockSpec((1, HD_B), lambda g, n: (0, 0))],
        out_shape=[jax.ShapeDtypeStruct((T, 3 * wa + 4 * W), BF16), jax.ShapeDtypeStruct((1, W), F32),
                   jax.ShapeDtypeStruct((1, HD_B), F32)],
        scratch_shapes=[pltpu.VMEM((HGRN_HEADS, HD_B, HD_B), F32)] + [tile] * 7,
        compiler_params=_params(("arbitrary", "arbitrary")),
        operands=(proj, proj, proj, proj, lb_logits, gnorm_g, o_b, st_all, dmixin, *d_attn))


def _adamw_math(g, w, m, v):
    m = B1 * m + (1.0 - B1) * g
    v = B2 * v + (1.0 - B2) * (g * g)
    m_hat = m / (1.0 - B1 ** STEP)
    v_hat = v / (1.0 - B2 ** STEP)
    return -LR * (m_hat / (jnp.sqrt(v_hat) + ADAM_EPS) + WD * w), m, v


def _adamw(g, w, m, v, name):
    R, C = g.shape
    tr = _row_tile(R, C)

    def body(g_ref, w_ref, m_ref, v_ref, go_ref, d_ref, mo_ref, vo_ref):
        g = g_ref[...]
        go_ref[...] = g
        d_ref[...], mo_ref[...], vo_ref[...] = _adamw_math(g, w_ref[...], m_ref[...], v_ref[...])

    blk = pl.BlockSpec((tr, C), lambda i: (i, 0))
    return pl.pallas_call(
        body, name=name, grid=(R // tr,), in_specs=[blk] * 4, out_specs=[blk] * 4,
        out_shape=[jax.ShapeDtypeStruct((R, C), F32)] * 4, compiler_params=_params(("parallel",), 40),
    )(g, w, m, v)


def _sum_pair(g_full, from_sibling, sel, name):
    Q, K, Ns = g_full.shape
    kh = K // 2
    tr = _row_tile(kh, Ns)
    nh = kh // tr

    def body(sel_ref, a_ref, b_ref, o_ref):
        o_ref[...] = (a_ref[...].astype(F32) + b_ref[...].astype(F32)).astype(o_ref.dtype)

    return pl.pallas_call(
        body, name=name,
        grid_spec=pltpu.PrefetchScalarGridSpec(
            num_scalar_prefetch=1, grid=(Q, nh),
            in_specs=[pl.BlockSpec((None, tr, Ns), lambda q, i, sel: (q, sel[1] * nh + i, 0)),
                      pl.BlockSpec((None, tr, Ns), lambda q, i, sel: (q, i, 0))],
            out_specs=pl.BlockSpec((None, tr, Ns), lambda q, i, sel: (q, i, 0))),
        out_shape=jax.ShapeDtypeStruct((Q, kh, Ns), BF16), compiler_params=_params(("parallel", "parallel")),
    )(sel, g_full, from_sibling)


def _sum_chips(pair_sum, from_chips, sel, name):
    Q, kh, Ns = pair_sum.shape
    tr = _row_tile(kh, Ns)
    nh = kh // tr

    def body(sel_ref, a_ref, b0_ref, b1_ref, b2_ref, o_ref):
        up = lambda r: r[...].astype(F32)
        o_ref[...] = ((up(a_ref) + up(b0_ref)) + up(b1_ref)) + up(b2_ref)

    recv = lambda k: pl.BlockSpec((None, tr, Ns), lambda i, sel: (k, i, 0))
    return pl.pallas_call(
        body, name=name,
        grid_spec=pltpu.PrefetchScalarGridSpec(
            num_scalar_prefetch=1, grid=(nh,),
            in_specs=[pl.BlockSpec((None, tr, Ns), lambda i, sel: (sel[0], i, 0)), recv(0), recv(1), recv(2)],
            out_specs=pl.BlockSpec((tr, Ns), lambda i, sel: (sel[1] * nh + i, 0))),
        out_shape=jax.ShapeDtypeStruct((2 * kh, Ns), F32), compiler_params=_params(("parallel",)),
    )(sel, pair_sum, from_chips, from_chips, from_chips)


def _gather_small(v, name):
    R, L = v.shape

    def body(v_ref, out_ref, send_sems, recv_sems):
        x, y, c = _place()
        me = 4 * x + 2 * y + c
        out_ref[me] = v_ref[...]
        peers = [(_flip(x, k >> 2 & 1), _flip(y, k >> 1 & 1), _flip(c, k & 1)) for k in range(1, N_DEV)]

        def copy(k, row, to):
            return pltpu.make_async_remote_copy(src_ref=v_ref, dst_ref=out_ref.at[row], send_sem=send_sems.at[k],
                                                recv_sem=recv_sems.at[k], device_id=to, device_id_type=MESH)

        sends = [copy(k, me, peer) for k, peer in enumerate(peers)]
        for cp in sends:
            cp.start()
        for k, (px, py, pc) in enumerate(peers):
            copy(k, 4 * px + 2 * py + pc, (x, y, c)).wait_recv()
        for cp in sends:
            cp.wait_send()

    vmem = pl.BlockSpec(memory_space=pltpu.VMEM)
    return pl.pallas_call(
        body, name=name, in_specs=[vmem], out_specs=vmem, out_shape=jax.ShapeDtypeStruct((N_DEV, R, L), F32),
        scratch_shapes=[pltpu.SemaphoreType.DMA((N_DEV - 1,)), pltpu.SemaphoreType.DMA((N_DEV - 1,))],
    )(v)


def _small_rider(v):
    def copies(ins, outs, send_sems, recv_sems):
        x, y, c = _place()
        peers = [(_flip(x, k >> 2 & 1), _flip(y, k >> 1 & 1), _flip(c, k & 1)) for k in range(1, N_DEV)]

        def copy(k, row, to):
            return pltpu.make_async_remote_copy(src_ref=ins[0], dst_ref=outs[0].at[row], send_sem=send_sems.at[k],
                                                recv_sem=recv_sems.at[k], device_id=to, device_id_type=MESH)

        sends = [copy(k, 4 * x + 2 * y + c, peer) for k, peer in enumerate(peers)]
        return sends, [copy(k, 4 * px + 2 * py + pc, (x, y, c)) for k, (px, py, pc) in enumerate(peers)]

    def start(*refs):
        for cp in copies(*refs)[0]:
            cp.start()

    def finish(*refs):
        sends, arrivals = copies(*refs)
        for cp in arrivals:
            cp.wait_recv()
        for cp in sends:
            cp.wait_send()

    return _Rider([v], [jax.ShapeDtypeStruct((N_DEV, *v.shape), F32)], N_DEV - 1, start, finish)


def _silu(v):
    return v * jax.nn.sigmoid(v)


def _ada_fwd(c_all, w_ada, rider, tn=512):
    M, D = c_all.shape
    Ns = w_ada.shape[1]

    def body(c_ref, w_ref, o_ref):
        o_ref[...] = _dot(_silu(c_ref[...]).astype(BF16), w_ref[...].astype(BF16))

    (out,), rode = _call(
        body, rider, name="ada_fwd", grid=(Ns // tn,),
        in_specs=[pl.BlockSpec((M, D), lambda j: (0, 0)), pl.BlockSpec((D, tn), lambda j: (0, j))],
        out_specs=[pl.BlockSpec((M, tn), lambda j: (0, j))], out_shape=[jax.ShapeDtypeStruct((M, Ns), F32)],
        scratch_shapes=[], compiler_params=_params(("arbitrary",)), operands=(c_all, w_ada))
    return out, rode


def _ada_bwd(c_all, dmod, w, m, v, tk=256, tn=1536):
    M, D = c_all.shape
    Ns = dmod.shape[1]

    def body(c_ref, d_ref, w_ref, m_ref, v_ref, g_ref, dl_ref, mo_ref, vo_ref):
        g = _dot(_silu(c_ref[...]).astype(BF16), d_ref[...].astype(BF16), TN)
        g_ref[...] = g
        dl_ref[...], mo_ref[...], vo_ref[...] = _adamw_math(g, w_ref[...], m_ref[...], v_ref[...])

    blk = pl.BlockSpec((tk, tn), lambda i, j: (i, j))
    return pl.pallas_call(
        body, name="ada_bwd", grid=(D // tk, Ns // tn),
        in_specs=[pl.BlockSpec((M, tk), lambda i, j: (0, i)), pl.BlockSpec((M, tn), lambda i, j: (0, j)), blk, blk, blk],
        out_specs=[blk] * 4, out_shape=[jax.ShapeDtypeStruct((D, Ns), F32)] * 4,
        compiler_params=_params(("parallel", "parallel"), 40),
    )(c_all, dmod, w, m, v)


def _small_update(g_all, w, m, v):
    R, L = w.shape

    def body(g_ref, w_ref, m_ref, v_ref, go_ref, d_ref, mo_ref, vo_ref):
        g = g_ref[0]
        for d in range(1, N_DEV):
            g = g + g_ref[d]
        go_ref[...] = g
        d_ref[...], mo_ref[...], vo_ref[...] = _adamw_math(g, w_ref[...], m_ref[...], v_ref[...])

    return pl.pallas_call(body, name="small_update", out_shape=[jax.ShapeDtypeStruct((R, L), F32)] * 4)(g_all, w, m, v)


def _pack(parts, rows):
    flat = jnp.concatenate([p.reshape(-1) for p in parts])
    return jnp.pad(flat, (0, rows * 128 - flat.shape[0])).reshape(rows, 128)


def _unpack(packed, shapes):
    flat, out, at = packed.reshape(-1), [], 0
    for shp in shapes:
        size = 1
        for d in shp:
            size *= d
        out.append(flat[at:at + size].reshape(shp))
        at += size
    return out


def _layer(x, tgt, mod, wts, rel_bias, attn_norm_g, lb_logits, gnorm_g, ln1_g, ln1_b, ln2_g, ln2_b, place=None):
    T, D = x.shape
    aw = attn_norm_g.shape[1]
    shift1, scale1, gate1, shift2, scale2, gate2 = [mod[i:i + 1] for i in range(6)]

    def gather(n, rows=None, into=None, before=None, last=True):
        return None if place is None else _gather_rider(wts[n], rows, None if into is None else into[0], before, last)

    def gathered(n, rode):
        return wts[n] if place is None else lax.dynamic_update_index_in_dim(rode[0], wts[n], place[0], 0)

    def blocks(g):
        return g.reshape(N_CHIPS, -1, g.shape[2])

    def to_sibling(g):
        return None if place is None else _pair_rider(g)

    def pair_sum(n, g, rode=None):
        if place is None:
            return g
        rode = _alone(_pair_rider(g), n + "_send_pair") if rode is None else rode
        return _sum_pair(g, rode[0], place[1], n + "_sum_pair")

    def to_chips(p, rows=None, into=None):
        return None if place is None else _chips_rider(p, rows, None if into is None else into[0])

    def summed(n, p, rode):
        return p if place is None else _sum_chips(p, rode[0], place[1], n + "_sum_chips")

    def to_both(block):
        return None if place is None else _share_rider(block)

    def carrying(mm, *args, rider, **kw):
        return mm(*args, rider=rider, **kw) if rider is not None else (mm(*args, **kw), None)

    def to_sibling_acts(a, b):
        return None if place is None else _acts_rider(a, b)

    def pair_grad(name, a, b, tn, rider, arrived=None, late_rider=None):
        if place is None:
            return _mm_tn(a, b, q=N_CHIPS, tk=512, tn=tn, tt=T, name=name), None
        kh = a.shape[1] // 2
        mine = lax.dynamic_slice_in_dim(a, place[1][1] * kh, kh, axis=1)
        part, rode = carrying(_mm_tn, mine, b, q=N_CHIPS, tk=512, tn=tn, tt=T, name=name + "_own",
                              rider=_join(None if arrived else _acts_rider(a, b), rider))
        (a_sib, b_sib), rode = arrived or rode[:2], rode if arrived else rode[2:]
        out, late = _mm_tn_add(a_sib, b_sib, part, tk=512, tn=tn, name=name + "_sib", rider=late_rider)
        return out, (rode or []) + late

    if place is None:
        h1, rode = _pre_mixer(x, scale1, shift1), None
    else:
        h1, rode = _pre_mixer(x, scale1, shift1, wts["w_in_last_part"])
    w_in = gathered("w_in", rode)
    n_qkv = 3 * aw // 256
    kh_o, kh_f, kh_out = [wts[n].shape[-2] // 2 for n in ("w_o", "w_ffn_in", "w_ffn_out")]
    o_cut, f_cuts, out_cut = 3 * kh_o // 8, (7 * kh_f // 16, 7 * kh_f // 8), kh_out // 11
    qkv, rode = carrying(_mm_nn, h1, w_in, tm=ZPAD, tn=256, tk=D, name="proj_qkv", cols=(0, n_qkv), o_dtype=BF16,
                         pad_rows=ZPAD, rider=gather("w_o", (0, o_cut), last=False))
    proj, rode = carrying(_mm_nn, h1, w_in, tm=2048, tn=256, tk=D, name="proj_rec",
                          cols=(n_qkv, N_CHIPS * w_in.shape[2] // 256),
                          rider=gather("w_o", (o_cut, kh_o - o_cut), rode, before=(0, o_cut)))
    w_o3 = gathered("w_o", rode).reshape(1, D, D)
    bias = _bias_band(rel_bias)
    (mix_a, probs), rode = _attn_fwd(qkv, bias, attn_norm_g, rider=gather("w_ffn_in", (0, f_cuts[0]), last=False))
    (mix_b, o_b, st_all), rode = _hgrn_fwd(
        proj, lb_logits, gnorm_g,
        rider=gather("w_ffn_in", (f_cuts[0], f_cuts[1] - f_cuts[0]), rode, before=(0, f_cuts[0]), last=False))
    mixin = jnp.concatenate([mix_a, mix_b], axis=1)
    mix = _mm_nn(mixin, w_o3, tm=1024, tn=512, tk=D, name="mix_out")
    if place is None:
        x1, h2 = _post_mixer(mix, x, gate1, ln1_g, ln1_b, scale2, shift2)
    else:
        (x1, h2), rode = _post_mixer(mix, x, gate1, ln1_g, ln1_b, scale2, shift2, rider=_join(
            gather("w_ffn_in", (f_cuts[1], kh_f - f_cuts[1]), rode, before=(f_cuts[0], f_cuts[1] - f_cuts[0])),
            gather("w_ffn_out", (0, out_cut), last=False)))
    w_ffn_in = gathered("w_ffn_in", rode)
    (gate, up, act), rode = _ffn_in_swiglu(
        h2, w_ffn_in, tm=2048, tn=256,
        rider=gather("w_ffn_out", (out_cut, kh_out - out_cut), rode and rode[1:], before=(0, out_cut)))
    w_out3 = gathered("w_ffn_out", rode)
    w_out3 = w_out3.reshape(1, -1, w_out3.shape[2])
    d_ff = w_out3.shape[1]
    f = _mm_nn(act, w_out3, tm=1024, tn=512, tk=d_ff, name="ffn_out")
    du2, df, acc2 = _loss_head(f, x1, tgt, gate2, ln2_g, ln2_b)
    loss = (0.5 / D) * jnp.sum(acc2[3])
    g = blocks(_mm_tn(act, df, q=1, tk=512, tn=1024, tt=T, name="g_ffn_out"))
    (dff,), rode = _d_act_swiglu(df, w_out3, gate, up, tm=1024, to=512, rider=to_sibling(g))
    p_out = pair_sum("w_ffn_out", g, rode)
    cut = 25 * p_out.shape[1] // 44
    dh2, rode = carrying(_mm_nt, dff, w_ffn_in, tm=1024, to=1024, tn=w_ffn_in.shape[2], name="d_h2",
                         rider=_join(to_chips(p_out, (0, cut)), to_sibling_acts(h2, dff)))
    p_fin, rode = pair_grad("g_ffn_in", h2, dff, w_ffn_in.shape[2] // 2,
                            to_chips(p_out, (cut, p_out.shape[1] - cut), rode), arrived=rode and rode[1:])
    g_ffn_out = summed("w_ffn_out", p_out, rode)
    if place is None:
        du1, dmix, acc1 = _mid_bwd(dh2, du2, x1, mix, x, gate1, ln1_g, scale2)
    else:
        (du1, dmix, acc1), (g_ffn_out,) = _mid_bwd(dh2, du2, x1, mix, x, gate1, ln1_g, scale2, rider=to_both(g_ffn_out))
    g = blocks(_mm_tn(mixin, dmix, q=1, tk=512, tn=1024, tt=T, name="g_o"))
    dmixin, rode = carrying(_mm_nt, dmix, w_o3, tm=1024, to=512, tn=D, name="d_mixin", rider=to_sibling(g))
    p_o = pair_sum("w_o", g, rode)
    cut = p_fin.shape[1] // 2
    (dq, dk, dv, dbias, dgain), rode = _attn_bwd(qkv, probs, attn_norm_g, dmixin, rider=to_chips(p_fin, (0, cut)))
    (dproj, dl0, dgn), rode = _hgrn_bwd(
        proj, lb_logits, gnorm_g, o_b, st_all, dmixin, (dq, dk, dv),
        rider=_join(to_chips(p_fin, (cut, p_fin.shape[1] - cut), rode), to_chips(p_o)))
    g_ffn_in, g_o = summed("w_ffn_in", p_fin, rode[:1]), summed("w_o", p_o, rode[1:])
    p_in, rode = pair_grad("g_in", h1, dproj, w_in.shape[2] // 2, None,
                           late_rider=_join(to_both(g_ffn_in), to_both(g_o)))
    if place is not None:
        g_ffn_in, g_o = rode
    cut = 3 * p_in.shape[1] // 4
    dh1, rode = carrying(_mm_nt, dproj, w_in, tm=1024, to=1024, tn=w_in.shape[2], name="d_h1",
                         rider=to_chips(p_in, (0, cut)))
    if place is None:
        (grad_x, acc0), g_in = _first_bwd(dh1, du1, x, scale1), p_in
    else:
        (grad_x, acc0), rode = _first_bwd(dh1, du1, x, scale1, rider=to_chips(p_in, (cut, p_in.shape[1] - cut), rode))
        g_in, = _alone(to_both(summed("w_in", p_in, rode)), "w_in_share")
    dmod = jnp.concatenate([acc0[1:2], acc0[0:1], acc1[4:5], acc1[1:2], acc1[0:1], acc2[2:3]], axis=0)
    small = dict(rel_bias=_bias_band_grad(dbias), attn_norm_g=dgain,
                 lb_logits=jnp.concatenate([dl0, -dl0], axis=0), gnorm_g=dgn,
                 ln1_g=acc1[2:3], ln1_b=acc1[3:4], ln2_g=acc2[0:1], ln2_b=acc2[1:2])
    return loss, grad_x, dict(w_in=g_in, w_o=g_o, w_ffn_in=g_ffn_in, w_ffn_out=g_ffn_out), dmod, small


SMALL = ("rel_bias", "attn_norm_g", "lb_logits", "gnorm_g", "ln1_g", "ln1_b", "ln2_g", "ln2_b")
SMALL_ROWS = 256


def kernel(x, c, w_ada, b_ada, w_in, rel_bias, attn_norm_g, lb_logits, gnorm_g, w_o, ln1_g, ln1_b, w_ffn_in, w_ffn_out, ln2_g, ln2_b, loss_target, m_w_ada, m_b_ada, m_w_in, m_rel_bias, m_attn_norm_g, m_lb_logits, m_gnorm_g, m_w_o, m_ln1_g, m_ln1_b, m_w_ffn_in, m_w_ffn_out, m_ln2_g, m_ln2_b, v_w_ada, v_b_ada, v_w_in, v_rel_bias, v_attn_norm_g, v_lb_logits, v_gnorm_g, v_w_o, v_ln1_g, v_ln1_b, v_w_ffn_in, v_w_ffn_out, v_ln2_g, v_ln2_b):
    mx, my, mc = _place()
    me = 4 * mx + 2 * my + mc
    chip = 2 * mx + my
    sel = jnp.stack([chip, mc]).astype(jnp.int32)
    D = x.shape[2]
    ns_ada = w_ada.shape[2]

    big = dict(w_in=(w_in, m_w_in, v_w_in), w_o=(w_o, m_w_o, v_w_o), w_ffn_in=(w_ffn_in, m_w_ffn_in, v_w_ffn_in),
               w_ffn_out=(w_ffn_out, m_w_ffn_out, v_w_ffn_out))
    shards = dict(w_in=w_in[0].astype(BF16))
    kh = shards["w_in"].shape[0] // 2
    cuts = [part * kh // 32 for part in (0, 12, 19, 22, 26, 32)]
    spans = [(a, b - a) for a, b in zip(cuts, cuts[1:])]

    def w_in_part(i, rode):
        return _gather_rider(shards["w_in"], spans[i], rode and rode[0], spans[i - 1] if i else None, last=i == 4)

    c_own = c.reshape(D // 128, 128)
    rode, c_all = None, None
    for i, n in enumerate(("w_ffn_in", "w_ffn_out", "w_o")):
        rider = _join(w_in_part(i, rode), None if i else _small_rider(c_own))
        (shards[n],), rode = _to_bf16(big[n][0][0], "cast_" + n, rider)
        c_all = c_all if i else rode[1]

    c_all = lax.dynamic_update_index_in_dim(c_all, c_own, me, 0).reshape(N_DEV, D)
    c_all = jnp.pad(c_all, ((0, 16 - N_DEV), (0, 0)))
    mod_cols, rode = _ada_fwd(c_all, w_ada[0], w_in_part(3, rode))
    mod_cols = mod_cols[:N_DEV]
    shards["w_in_last_part"] = w_in_part(4, rode)
    mod_all = _gather_small(mod_cols.reshape(-1, 128), "gather_mod").reshape(N_DEV, N_DEV, ns_ada)
    mod = lax.dynamic_index_in_dim(mod_all[::2], me, axis=1, keepdims=False)
    mod = (mod.reshape(1, -1) + b_ada).reshape(6, D)

    loss, grad_x, g_big, dmod, g_small = _layer(
        x[0], loss_target[0], mod, shards, rel_bias[0], attn_norm_g, lb_logits, gnorm_g, ln1_g, ln1_b, ln2_g, ln2_b,
        place=(chip, sel))

    grads, deltas, new_m, new_v = {}, {}, {}, {}
    for n, (w, m, v) in big.items():
        g, d, mo, vo = _adamw(g_big[n], w[0], m[0], v[0], "adamw_" + n)
        grads[n], deltas[n], new_m[n], new_v[n] = g[None], d[None], mo[None], vo[None]

    small_in = dict(rel_bias=(rel_bias, m_rel_bias, v_rel_bias), attn_norm_g=(attn_norm_g, m_attn_norm_g, v_attn_norm_g),
                    lb_logits=(lb_logits, m_lb_logits, v_lb_logits), gnorm_g=(gnorm_g, m_gnorm_g, v_gnorm_g),
                    ln1_g=(ln1_g, m_ln1_g, v_ln1_g), ln1_b=(ln1_b, m_ln1_b, v_ln1_b), ln2_g=(ln2_g, m_ln2_g, v_ln2_g),
                    ln2_b=(ln2_b, m_ln2_b, v_ln2_b))
    g_all = _gather_small(_pack([dmod] + [g_small[n] for n in SMALL] + [loss], SMALL_ROWS), "gather_small")
    packed = [_pack([t] + [small_in[n][i] for n in SMALL] + [jnp.zeros((), F32)], SMALL_ROWS)
              for i, t in enumerate((b_ada, m_b_ada, v_b_ada))]
    shapes = [b_ada.shape] + [small_in[n][0].shape for n in SMALL] + [()]
    outs = [_unpack(o, shapes) for o in _small_update(g_all, *packed)]
    loss = outs[0][-1]
    for i, n in enumerate(("b_ada",) + SMALL):
        grads[n], deltas[n], new_m[n], new_v[n] = outs[0][i], outs[1][i], outs[2][i], outs[3][i]

    dmod_all = g_all[:, :6 * D // 128].reshape(N_DEV, 6 * D)
    dmod_cols = lax.dynamic_slice_in_dim(dmod_all, chip * ns_ada, ns_ada, axis=1)
    dmod_cols = jnp.pad(dmod_cols, ((0, 16 - N_DEV), (0, 0)))
    g, d, mo, vo = _ada_bwd(c_all, dmod_cols, w_ada[0], m_w_ada[0], v_w_ada[0])
    grads["w_ada"], deltas["w_ada"], new_m["w_ada"], new_v["w_ada"] = g[None], d[None], mo[None], vo[None]

    order = ("w_ada", "b_ada", "w_in", "rel_bias", "attn_norm_g", "lb_logits", "gnorm_g", "w_o", "ln1_g", "ln1_b",
             "w_ffn_in", "w_ffn_out", "ln2_g", "ln2_b")
    return (loss, grad_x[None], *[grads[n] for n in order], *[deltas[n] for n in order],
            *[new_m[n] for n in order], *[new_v[n] for n in order])
```

```python
import numpy as np
import jax
import jax.numpy as jnp
from jax import lax
from jax.experimental import pallas as pl
from jax.experimental.pallas import tpu as pltpu

F32 = jnp.float32
BF16 = jnp.bfloat16
MESH = pl.DeviceIdType.MESH
HIGHEST = lax.Precision.HIGHEST

CHUNK = 64
N_PAST = 8
QG = 4
QROWS = QG * CHUNK
KPAD = N_PAST * CHUNK
ZPAD = 2 * KPAD
UNION = (QG + N_PAST) * CHUNK
BAND = (N_PAST + 1) * CHUNK
HD_A = 64
HD_B = 128
SUB = 16
HGRN_HEADS = 8
MAX_REL = 256
EPS = 1e-5
ALPHA = 2.0 ** 0.25
LR, B1, B2, ADAM_EPS, WD, STEP = 1e-3, 0.9, 0.999, 1e-8, 0.01, 10
N_CHIPS = 4
N_DEV = 8
NEG = -1e30
TILE_BYTES = 3 << 19

NN = ((1,), (0,))
NT = ((1,), (1,))
TN = ((0,), (0,))


def _dot(a, b, dims=NN, precision=None):
    return lax.dot_general(a, b, (dims, ((), ())), preferred_element_type=F32, precision=precision)


def _params(sem=None, vmem_mb=None, **kw):
    return pltpu.CompilerParams(dimension_semantics=sem,
                                vmem_limit_bytes=None if vmem_mb is None else vmem_mb << 20, **kw)


def _row_tile(rows, cols):
    for cand in (512, 256, 128, 64, 32, 16, 8):
        if rows % cand == 0 and cand * cols * 4 <= TILE_BYTES:
            return cand
    raise ValueError((rows, cols))


def _place():
    return lax.axis_index("x"), lax.axis_index("y"), lax.axis_index("c")


def _flip(v, bit):
    return 1 - v if bit else v


ANY = pl.BlockSpec(memory_space=pl.ANY)
CHIP_FLIPS = ((1, 0), (0, 1), (1, 1))


class _Rider:
    def __init__(self, operands, out_shape, n_sems, start, finish, aliases=None):
        self.operands, self.out_shape, self.n_sems, self.start, self.finish = operands, out_shape, n_sems, start, finish
        self.aliases = aliases or {}


def _call(body, rider, *, name, grid, in_specs, out_specs, out_shape, scratch_shapes, compiler_params, operands):
    if rider is None:
        outs = pl.pallas_call(body, name=name, grid=grid, in_specs=in_specs, out_specs=out_specs, out_shape=out_shape,
                              scratch_shapes=scratch_shapes, compiler_params=compiler_params)(*operands)
        return list(outs), []
    n_in, n_out, n_sc = len(in_specs), len(out_specs), len(scratch_shapes)
    r_in, r_out = len(rider.operands), len(rider.out_shape)

    def carried(*refs):
        refs = list(refs)
        cuts = [n_in, r_in, n_out, r_out, n_sc]
        ins, r_ins, outs, r_outs, scratch = [[refs.pop(0) for _ in range(n)] for n in cuts]
        first, last = None, None
        for axis, size in enumerate(grid):
            i = pl.program_id(axis)
            first = (i == 0) if first is None else first & (i == 0)
            last = (i == size - 1) if last is None else last & (i == size - 1)

        @pl.when(first)
        def _():
            rider.start(r_ins, r_outs, *refs)

        body(*ins, *outs, *scratch)

        @pl.when(last)
        def _():
            rider.finish(r_ins, r_outs, *refs)

    sems = [pltpu.SemaphoreType.DMA((rider.n_sems,)), pltpu.SemaphoreType.DMA((rider.n_sems,))]
    outs = pl.pallas_call(carried, name=name, grid=grid, in_specs=list(in_specs) + [ANY] * r_in,
                          out_specs=list(out_specs) + [ANY] * r_out, out_shape=list(out_shape) + rider.out_shape,
                          scratch_shapes=list(scratch_shapes) + sems, compiler_params=compiler_params,
                          input_output_aliases={n_in + i: n_out + o for i, o in rider.aliases.items()},
                          )(*operands, *rider.operands)
    return list(outs[:n_out]), list(outs[n_out:])


def _alone(rider, name):
    def body(*refs):
        ins, outs, sems = refs[:len(rider.operands)], refs[len(rider.operands):-2], refs[-2:]
        rider.start(ins, outs, *sems)
        rider.finish(ins, outs, *sems)

    return pl.pallas_call(
        body, name=name, in_specs=[ANY] * len(rider.operands), out_specs=[ANY] * len(rider.out_shape),
        out_shape=rider.out_shape, input_output_aliases=rider.aliases,
        scratch_shapes=[pltpu.SemaphoreType.DMA((rider.n_sems,)), pltpu.SemaphoreType.DMA((rider.n_sems,))],
    )(*rider.operands)


class _Sems:
    def __init__(self, sems, base):
        self.sems, self.base = sems, base

    @property
    def at(self):
        return self

    def __getitem__(self, k):
        return self.sems.at[self.base + k]


def _join(*riders):
    riders = [r for r in riders if r is not None]
    if len(riders) < 2:
        return riders[0] if riders else None

    def parts(ins, outs, send_sems, recv_sems):
        i = o = s = 0
        for r in riders:
            ni, no = len(r.operands), len(r.out_shape)
            yield r, ins[i:i + ni], outs[o:o + no], _Sems(send_sems, s), _Sems(recv_sems, s)
            i, o, s = i + ni, o + no, s + r.n_sems

    def start(*refs):
        for r, *args in parts(*refs):
            r.start(*args)

    def finish(*refs):
        for r, *args in parts(*refs):
            r.finish(*args)

    aliases, i, o = {}, 0, 0
    for r in riders:
        aliases.update({i + a: o + b for a, b in r.aliases.items()})
        i, o = i + len(r.operands), o + len(r.out_shape)
    return _Rider([a for r in riders for a in r.operands], [s for r in riders for s in r.out_shape],
                  sum(r.n_sems for r in riders), start, finish, aliases)


def _gather_rider(shard, rows=None, into=None, before=None, last=True):
    K, Ns = shard.shape
    kh = K // 2
    rows = rows or (0, kh)

    def copies(w_ref, out_ref, send_sems, recv_sems):
        x, y, c = _place()
        chips = [(_flip(x, fx), _flip(y, fy)) for fx, fy in CHIP_FLIPS]

        def half(chip, which, part):
            return out_ref.at[2 * chip[0] + chip[1], pl.ds(which * kh + part[0], part[1]), :]

        def copy(k, dst, to, src=None):
            return pltpu.make_async_remote_copy(src_ref=dst if src is None else src, dst_ref=dst,
                                                send_sem=send_sems.at[k], recv_sem=recv_sems.at[k],
                                                device_id=to, device_id_type=MESH)

        def first():
            return [copy(j, half((x, y), c, rows), (*chip, c), src=w_ref.at[pl.ds(c * kh + rows[0], rows[1]), :])
                    for j, chip in enumerate(chips)]

        def onward(base, part):
            return [copy(base + j, half(chip, c, part), (x, y, 1 - c)) for j, chip in enumerate(chips)]

        def arriving(base, which, part):
            return [copy(base + j, half(chip, which, part), (x, y, c)) for j, chip in enumerate(chips)]

        return c, first, onward, arriving

    def start(ins, outs, send_sems, recv_sems):
        _, first, onward, _ = copies(ins[0], outs[0], send_sems, recv_sems)
        for cp in first() + (onward(3, before) if before else []):
            cp.start()

    def finish(ins, outs, send_sems, recv_sems):
        c, first, onward, arriving = copies(ins[0], outs[0], send_sems, recv_sems)
        sent = first() + (onward(3, before) if before else [])
        passed = onward(6, rows) if last else [None] * 3
        for arrived, cp in zip(arriving(0, c, rows), passed):
            arrived.wait_recv()
            if last:
                cp.start()
        for arrived in (arriving(3, 1 - c, before) if before else []) + (arriving(6, 1 - c, rows) if last else []):
            arrived.wait_recv()
        for cp in sent + (passed if last else []):
            cp.wait_send()

    full = jax.ShapeDtypeStruct((N_CHIPS, K, Ns), shard.dtype)
    if into is None:
        return _Rider([shard], [full], 9, start, finish)
    return _Rider([shard, into], [full], 9, start, finish, aliases={1: 0})


def _pair_rider(g_full):
    Q, K, Ns = g_full.shape
    kh = K // 2

    def copy(g_ref, got_ref, send_sems, recv_sems):
        x, y, c = _place()
        return pltpu.make_async_remote_copy(src_ref=g_ref.at[:, pl.ds((1 - c) * kh, kh), :], dst_ref=got_ref,
                                            send_sem=send_sems.at[0], recv_sem=recv_sems.at[0],
                                            device_id=(x, y, 1 - c), device_id_type=MESH)

    def start(ins, outs, send_sems, recv_sems):
        copy(ins[0], outs[0], send_sems, recv_sems).start()

    def finish(ins, outs, send_sems, recv_sems):
        copy(ins[0], outs[0], send_sems, recv_sems).wait()

    return _Rider([g_full], [jax.ShapeDtypeStruct((Q, kh, Ns), g_full.dtype)], 1, start, finish)


def _acts_rider(a, b):
    T, K = a.shape
    kh = K // 2

    def copies(ins, outs, send_sems, recv_sems):
        x, y, c = _place()
        pair = [(ins[0].at[:, pl.ds((1 - c) * kh, kh)], outs[0]), (ins[1], outs[1])]
        return [pltpu.make_async_remote_copy(src_ref=src, dst_ref=dst, send_sem=send_sems.at[k], recv_sem=recv_sems.at[k],
                                             device_id=(x, y, 1 - c), device_id_type=MESH)
                for k, (src, dst) in enumerate(pair)]

    def start(*refs):
        for cp in copies(*refs):
            cp.start()

    def finish(*refs):
        for cp in copies(*refs):
            cp.wait()

    return _Rider([a, b], [jax.ShapeDtypeStruct((T, kh), a.dtype), jax.ShapeDtypeStruct(b.shape, b.dtype)], 2,
                  start, finish)


def _share_rider(block):
    K, Ns = block.shape
    kh = K // 2

    def halves(out_ref):
        x, y, c = _place()
        return out_ref.at[pl.ds(c * kh, kh), :], out_ref.at[pl.ds((1 - c) * kh, kh), :], (x, y, 1 - c)

    def start(ins, outs, send_sems, recv_sems):
        mine, _, sibling = halves(outs[0])
        pltpu.make_async_remote_copy(src_ref=mine, dst_ref=mine, send_sem=send_sems.at[0], recv_sem=recv_sems.at[0],
                                     device_id=sibling, device_id_type=MESH).start()

    def finish(ins, outs, send_sems, recv_sems):
        mine, theirs, sibling = halves(outs[0])
        pltpu.make_async_remote_copy(src_ref=theirs, dst_ref=theirs, send_sem=send_sems.at[0], recv_sem=recv_sems.at[0],
                                     device_id=sibling, device_id_type=MESH).wait_recv()
        pltpu.make_async_remote_copy(src_ref=mine, dst_ref=mine, send_sem=send_sems.at[0], recv_sem=recv_sems.at[0],
                                     device_id=sibling, device_id_type=MESH).wait_send()

    return _Rider([block], [jax.ShapeDtypeStruct((K, Ns), block.dtype)], 1, start, finish, aliases={0: 0})


def _chips_rider(pair_sum, rows=None, into=None):
    Q, kh, Ns = pair_sum.shape
    first_row, n_rows = rows or (0, kh)

    def copies(p_ref, got_ref, send_sems, recv_sems):
        x, y, c = _place()
        part = pl.ds(first_row, n_rows)
        out = []
        for j, (fx, fy) in enumerate(CHIP_FLIPS):
            px, py = _flip(x, fx), _flip(y, fy)
            out.append(pltpu.make_async_remote_copy(
                src_ref=p_ref.at[2 * px + py, part, :], dst_ref=got_ref.at[j, part, :], send_sem=send_sems.at[j],
                recv_sem=recv_sems.at[j], device_id=(px, py, c), device_id_type=MESH))
        return out

    def start(ins, outs, send_sems, recv_sems):
        for cp in copies(ins[0], outs[0], send_sems, recv_sems):
            cp.start()

    def finish(ins, outs, send_sems, recv_sems):
        sends = copies(ins[0], outs[0], send_sems, recv_sems)
        for cp in sends:
            cp.wait_recv()
        for cp in sends:
            cp.wait_send()

    got = jax.ShapeDtypeStruct((Q - 1, kh, Ns), pair_sum.dtype)
    if into is None:
        return _Rider([pair_sum], [got], 3, start, finish)
    return _Rider([pair_sum, into], [got], 3, start, finish, aliases={1: 0})


def _mm(a, b, *, grid, a_spec, b_spec, o_spec, o_shape, o_dtype, dims, acc_shape, name, rider=None, zero_rows=0,
        vmem_mb=48):
    nk = grid[2]

    def body(a_ref, b_ref, o_ref, *scratch):
        if zero_rows:
            @pl.when(pl.program_id(0) < zero_rows)
            def _():
                o_ref[...] = jnp.zeros_like(o_ref)

            @pl.when(pl.program_id(0) >= zero_rows)
            def _():
                o_ref[...] = _dot(a_ref[...], b_ref[...], dims).astype(o_ref.dtype)
            return
        part = _dot(a_ref[...], b_ref[...], dims)
        if nk == 1:
            o_ref[...] = part.astype(o_ref.dtype)
            return
        acc_ref, = scratch
        k = pl.program_id(2)

        @pl.when(k == 0)
        def _():
            acc_ref[...] = part

        @pl.when(k > 0)
        def _():
            acc_ref[...] += part

        @pl.when(k == nk - 1)
        def _():
            o_ref[...] = acc_ref[...].astype(o_ref.dtype)

    (out,), rode = _call(
        body, rider, name=name, grid=grid, in_specs=[a_spec, b_spec], out_specs=[o_spec],
        out_shape=[jax.ShapeDtypeStruct(o_shape, o_dtype)],
        scratch_shapes=[] if nk == 1 else [pltpu.VMEM(acc_shape, F32)],
        compiler_params=_params(("parallel", "parallel", "arbitrary") if rider is None else ("arbitrary",) * 3, vmem_mb),
        operands=(a, b))
    return out if rider is None else (out, rode)


def _mm_nn(a, w, *, tm, tn, tk, name, rider=None, cols=None, o_dtype=F32, pad_rows=0):
    T, K = a.shape
    Q, _, Ns = w.shape
    nbs = Ns // tn
    tm = min(tm, T)
    j0, j1 = cols or (0, Q * nbs)
    lead = pad_rows // tm
    return _mm(a, w, grid=(lead + T // tm, j1 - j0, K // tk),
               a_spec=pl.BlockSpec((tm, tk), lambda i, j, k: (jnp.maximum(i - lead, 0), k)),
               b_spec=pl.BlockSpec((None, tk, tn), lambda i, j, k: ((j + j0) // nbs, k, (j + j0) % nbs)),
               o_spec=pl.BlockSpec((tm, tn), lambda i, j, k: (i, j)),
               o_shape=(pad_rows + T, (j1 - j0) * tn), o_dtype=o_dtype, dims=NN, acc_shape=(tm, tn), name=name,
               rider=rider, zero_rows=lead)


def _col_blocks(g, rows, tn, at):
    if g.ndim == 2:
        return pl.BlockSpec((rows, tn), at)
    per = g.shape[2] // tn

    def stacked(*idx):
        r, c = at(*idx)
        return c // per, r, c % per

    return pl.BlockSpec((None, rows, tn), stacked)


def _mm_nt(g, w, *, tm, to, tn, name, rider=None):
    T = g.shape[-2]
    Q, K, Ns = w.shape
    nbs = Ns // tn
    tm = min(tm, T)
    return _mm(g, w, grid=(T // tm, K // to, Q * nbs),
               a_spec=_col_blocks(g, tm, tn, lambda i, j, n: (i, n)),
               b_spec=pl.BlockSpec((None, to, tn), lambda i, j, n: (n // nbs, j, n % nbs)),
               o_spec=pl.BlockSpec((tm, to), lambda i, j, n: (i, j)),
               o_shape=(T, K), o_dtype=F32, dims=NT, acc_shape=(tm, to), name=name, rider=rider)


def _mm_tn(a, g, *, q, tk, tn, tt, name, rider=None):
    T, K = a.shape
    Ns = g.shape[-1] * (g.ndim - 1) // q
    nbs = Ns // tn
    return _mm(a, g, grid=(K // tk, q * nbs, T // tt),
               a_spec=pl.BlockSpec((tt, tk), lambda i, j, t: (t, i)),
               b_spec=_col_blocks(g, tt, tn, lambda i, j, t: (t, j)),
               o_spec=pl.BlockSpec((None, tk, tn), lambda i, j, t: (j // nbs, i, j % nbs)),
               o_shape=(q, K, Ns), o_dtype=BF16, dims=TN, acc_shape=(tk, tn), name=name, rider=rider)


def _mm_tn_add(a, g, part, *, tk, tn, name, rider=None):
    T, K = a.shape
    Q, _, Ns = part.shape
    nbs = Ns // tn

    def body(a_ref, g_ref, p_ref, o_ref):
        o_ref[...] = (_dot(a_ref[...], g_ref[...], TN) + p_ref[...].astype(F32)).astype(o_ref.dtype)

    blk = pl.BlockSpec((None, tk, tn), lambda i, j: (j // nbs, i, j % nbs))
    (out,), rode = _call(
        body, rider, name=name, grid=(K // tk, Q * nbs),
        in_specs=[pl.BlockSpec((T, tk), lambda i, j: (0, i)), _col_blocks(g, T, tn, lambda i, j: (0, j)), blk],
        out_specs=[blk], out_shape=[jax.ShapeDtypeStruct((Q, K, Ns), BF16)], scratch_shapes=[],
        compiler_params=_params(("arbitrary", "arbitrary"), 48), operands=(a, g, part))
    return out, rode


def _ln(u):
    mu = jnp.mean(u, axis=-1, keepdims=True)
    d = u - mu
    r = lax.rsqrt(jnp.mean(d * d, axis=-1, keepdims=True) + EPS)
    return d * r, r


def _ln_bwd(dy, un, r):
    return r * (dy - jnp.mean(dy, axis=-1, keepdims=True) - un * jnp.mean(dy * un, axis=-1, keepdims=True))


def _colsum(v):
    return jnp.sum(v, axis=0, keepdims=True)


def _rowwise(name, fn, bigs, vecs, out_dtypes, n_acc, tm=128, rider=None):
    T, D = bigs[0].shape
    nb, nv, no = len(bigs), len(vecs), len(out_dtypes)

    def body(*refs):
        outs, accs = fn([r[...] for r in refs[:nb]], [r[...] for r in refs[nb:nb + nv]])
        for r, o in zip(refs[nb + nv:nb + nv + no], outs):
            r[...] = o.astype(r.dtype)
        if n_acc:
            acc_ref = refs[nb + nv + no]

            @pl.when(pl.program_id(0) == 0)
            def _():
                acc_ref[...] = jnp.zeros_like(acc_ref)

            for row, a in enumerate(accs):
                acc_ref[row:row + 1, :] += a

    big_spec = pl.BlockSpec((tm, D), lambda i: (i, 0))
    vec_spec = pl.BlockSpec((1, D), lambda i: (0, 0))
    out_shape = [jax.ShapeDtypeStruct((T, D), dt) for dt in out_dtypes]
    out_specs = [big_spec] * no
    if n_acc:
        out_shape.append(jax.ShapeDtypeStruct((8, D), F32))
        out_specs.append(pl.BlockSpec((8, D), lambda i: (0, 0)))
    outs, rode = _call(
        body, rider, name=name, grid=(T // tm,), in_specs=[big_spec] * nb + [vec_spec] * nv,
        out_specs=out_specs, out_shape=out_shape, scratch_shapes=[],
        compiler_params=_params(("arbitrary",), 48), operands=(*bigs, *vecs))
    return outs if rider is None else (outs, rode)


def _to_bf16(w, name, rider=None):
    R, C = w.shape
    tr = _row_tile(R, C)

    def body(w_ref, o_ref):
        o_ref[...] = w_ref[...].astype(o_ref.dtype)

    blk = pl.BlockSpec((tr, C), lambda i: (i, 0))
    return _call(body, rider, name=name, grid=(R // tr,), in_specs=[blk], out_specs=[blk],
                 out_shape=[jax.ShapeDtypeStruct((R, C), BF16)], scratch_shapes=[],
                 compiler_params=_params(("arbitrary",)), operands=(w,))


def _pre_mixer(x, scale1, shift1, rider=None):
    def fn(b, v):
        xn, _ = _ln(b[0])
        return [xn * (1.0 + v[0]) + v[1]], []
    outs = _rowwise("pre_mixer", fn, [x], [scale1, shift1], [BF16], 0, rider=rider)
    return outs[0] if rider is None else (outs[0][0], outs[1])


def _post_mixer(mix, x, gate1, g1, b1, scale2, shift2, rider=None):
    def fn(b, v):
        un1, _ = _ln(ALPHA * b[1] + v[0] * b[0])
        x1 = un1 * v[1] + v[2]
        xn1, _ = _ln(x1)
        return [x1, xn1 * (1.0 + v[3]) + v[4]], []
    return _rowwise("post_mixer", fn, [mix, x], [gate1, g1, b1, scale2, shift2], [F32, BF16], 0, rider=rider)


def _loss_head(f, x1, tgt, gate2, g2, b2):
    def fn(b, v):
        ff, xx, tt = b
        d_model = ff.shape[-1]
        un2, r2 = _ln(ALPHA * xx + v[0] * ff)
        err = un2 * v[1] + v[2] - tt
        dy = err * (1.0 / d_model)
        du2 = _ln_bwd(dy * v[1], un2, r2)
        return [du2, du2 * v[0]], [_colsum(dy * un2), _colsum(dy), _colsum(du2 * ff), _colsum(err * err)]
    return _rowwise("loss_head", fn, [f, x1, tgt], [gate2, g2, b2], [F32, BF16], 4)


def _mid_bwd(dh2, du2, x1, mix, x, gate1, g1, scale2, rider=None):
    def fn(b, v):
        dh, du, xx1, mm, xx = b
        xn1, r1n = _ln(xx1)
        dx1 = ALPHA * du + _ln_bwd(dh * (1.0 + v[2]), xn1, r1n)
        un1, r1 = _ln(ALPHA * xx + v[0] * mm)
        du1 = _ln_bwd(dx1 * v[1], un1, r1)
        return [du1, du1 * v[0]], [_colsum(dh * xn1), _colsum(dh), _colsum(dx1 * un1), _colsum(dx1),
                                   _colsum(du1 * mm)]
    return _rowwise("mid_bwd", fn, [dh2, du2, x1, mix, x], [gate1, g1, scale2], [F32, BF16], 5, rider=rider)


def _first_bwd(dh1, du1, x, scale1, rider=None):
    def fn(b, v):
        dh, du, xx = b
        xn, r0 = _ln(xx)
        return [ALPHA * du + _ln_bwd(dh * (1.0 + v[0]), xn, r0)], [_colsum(dh * xn), _colsum(dh)]
    return _rowwise("first_bwd", fn, [dh1, du1, x], [scale1], [F32], 2, rider=rider)


def _ffn_in_swiglu(h2, w, *, tm, tn, rider=None):
    T, K = h2.shape
    Q, _, Ns = w.shape
    nbs = Ns // tn
    half = Q * nbs // 2
    tm = min(tm, T)

    def body(a_ref, wg_ref, wu_ref, g_ref, u_ref, act_ref):
        a = a_ref[...]
        g, u = _dot(a, wg_ref[...]), _dot(a, wu_ref[...])
        g_ref[...] = g.astype(g_ref.dtype)
        u_ref[...] = u.astype(u_ref.dtype)
        act_ref[...] = (g * jax.nn.sigmoid(g) * u).astype(act_ref.dtype)

    cols = lambda first: pl.BlockSpec((None, K, tn), lambda i, j: ((j + first) // nbs, 0, (j + first) % nbs))
    blk = pl.BlockSpec((tm, tn), lambda i, j: (i, j))
    return _call(
        body, rider, name="ffn_in", grid=(T // tm, half),
        in_specs=[pl.BlockSpec((tm, K), lambda i, j: (i, 0)), cols(0), cols(half)], out_specs=[blk] * 3,
        out_shape=[jax.ShapeDtypeStruct((T, half * tn), BF16)] * 3, scratch_shapes=[],
        compiler_params=_params(("arbitrary", "arbitrary"), 48), operands=(h2, w, w))


def _d_act_swiglu(df, w, gate, up, *, tm, to, rider=None):
    T, N = df.shape
    F = w.shape[1]
    tm = min(tm, T)

    def body(df_ref, w_ref, g_ref, u_ref, o_ref):
        d = _dot(df_ref[...], w_ref[...], NT)
        g = g_ref[...].astype(F32)
        s = jax.nn.sigmoid(g)
        o_ref[0] = (d * u_ref[...].astype(F32) * s * (1.0 + g * (1.0 - s))).astype(o_ref.dtype)
        o_ref[1] = (d * g * s).astype(o_ref.dtype)

    blk = pl.BlockSpec((tm, to), lambda i, j: (i, j))
    return _call(
        body, rider, name="d_act", grid=(T // tm, F // to),
        in_specs=[pl.BlockSpec((tm, N), lambda i, j: (i, 0)), pl.BlockSpec((None, to, N), lambda i, j: (0, j, 0)), blk, blk],
        out_specs=[pl.BlockSpec((2, tm, to), lambda i, j: (0, i, j))],
        out_shape=[jax.ShapeDtypeStruct((2, T, F), BF16)], scratch_shapes=[],
        compiler_params=_params(("arbitrary", "arbitrary"), 48), operands=(df, w, gate, up))


PAIR = 2


def _fill_table(table_ref, band_ref):
    table_ref[...] = jnp.full(table_ref.shape, NEG, F32)
    for e in range(PAIR):
        for g in range(QG):
            table_ref[e, g * CHUNK:(g + 1) * CHUNK, g * CHUNK:g * CHUNK + BAND] = band_ref[e]


def _attn_probs(q_ref, k_ref, bias_ref, e, step):
    start = pl.multiple_of(step * QROWS, QROWS)
    lanes = pl.ds(e * HD_A, HD_A)
    s = _dot(q_ref[:, lanes], k_ref[pl.ds(start + ZPAD - KPAD, UNION), lanes], NT) * (HD_A ** -0.5) + bias_ref[e]
    col = lax.broadcasted_iota(jnp.int32, s.shape, 1)
    s = jnp.where(col + start >= KPAD, s, NEG)
    p = jnp.exp(s - jnp.max(s, axis=-1, keepdims=True))
    return p / jnp.sum(p, axis=-1, keepdims=True), start


def _attn_specs(T, n_pairs):
    wide = PAIR * HD_A
    per_step = pl.BlockSpec((QROWS, wide), lambda hp, n: (n, hp))
    queries = pl.BlockSpec((QROWS, wide), lambda hp, n: (n + ZPAD // QROWS, hp))
    keys = pl.BlockSpec((ZPAD + T, wide), lambda hp, n: (0, n_pairs + hp))
    values = pl.BlockSpec((ZPAD + T, wide), lambda hp, n: (0, 2 * n_pairs + hp))
    grads = pl.BlockSpec((T, wide), lambda hp, n: (0, hp))
    table = pl.BlockSpec((PAIR, CHUNK, BAND), lambda hp, n: (hp, 0, 0))
    vec = pl.BlockSpec((1, wide), lambda hp, n: (0, hp))
    return per_step, queries, keys, values, grads, table, vec


def _probs_spec():
    return pl.BlockSpec((PAIR, QROWS, UNION), lambda hp, n: (hp, n, 0))


def _attn_fwd(qkv, bias, gain, rider=None):
    T = qkv.shape[0] - ZPAD
    W = gain.shape[1]
    n_pairs = W // (PAIR * HD_A)

    def body(q_ref, k_ref, v_ref, band_ref, gain_ref, o_ref, p_ref, table_ref):
        @pl.when(pl.program_id(1) == 0)
        def _():
            _fill_table(table_ref, band_ref)

        for e in range(PAIR):
            lanes = pl.ds(e * HD_A, HD_A)
            p, start = _attn_probs(q_ref, k_ref, table_ref, e, pl.program_id(1))
            p_ref[e] = p.astype(p_ref.dtype)
            o = _dot(p_ref[e], v_ref[pl.ds(start + ZPAD - KPAD, UNION), lanes])
            rr = lax.rsqrt(jnp.mean(o * o, axis=-1, keepdims=True) + EPS)
            o_ref[:, lanes] = (o * rr * gain_ref[:, lanes]).astype(o_ref.dtype)

    per_step, queries, keys, values, _, table, vec = _attn_specs(T, n_pairs)
    return _call(
        body, rider, name="attn_fwd", grid=(n_pairs, T // QROWS), in_specs=[queries, keys, values, table, vec],
        out_specs=[per_step, _probs_spec()],
        out_shape=[jax.ShapeDtypeStruct((T, W), BF16), jax.ShapeDtypeStruct((n_pairs * PAIR, T, UNION), BF16)],
        scratch_shapes=[pltpu.VMEM((PAIR, QROWS, UNION), F32)],
        compiler_params=_params(("arbitrary", "arbitrary"), 40), operands=(qkv, qkv, qkv, bias, gain))


def _attn_bwd(qkv, probs, gain, dmixin, rider=None):
    T = qkv.shape[0] - ZPAD
    W = gain.shape[1]
    n_pairs = W // (PAIR * HD_A)
    scale = HD_A ** -0.5

    def body(q_ref, k_ref, v_ref, p_ref, gain_ref, don_ref, dq_ref, dkb_ref, dvb_ref, dband_ref, dgain_ref,
             dtable_ref, dk_ref, dv_ref):
        n = pl.program_id(1)

        @pl.when(n == 0)
        def _():
            dk_ref[...] = jnp.zeros_like(dk_ref)
            dv_ref[...] = jnp.zeros_like(dv_ref)
            dtable_ref[...] = jnp.zeros_like(dtable_ref)
            dgain_ref[...] = jnp.zeros_like(dgain_ref)

        for e in range(PAIR):
            lanes = pl.ds(e * HD_A, HD_A)
            start = pl.multiple_of(n * QROWS, QROWS)
            keys, in_qkv = pl.ds(start, UNION), pl.ds(start + ZPAD - KPAD, UNION)
            pb = p_ref[e]
            p = pb.astype(F32)
            vb = v_ref[in_qkv, lanes]
            o = _dot(pb, vb)
            rr = lax.rsqrt(jnp.mean(o * o, axis=-1, keepdims=True) + EPS)
            on = o * rr
            d_on = don_ref[:, lanes]
            dgain_ref[:, lanes] += _colsum(d_on * on)
            dyo = d_on * gain_ref[:, lanes]
            do = rr * (dyo - on * jnp.mean(dyo * on, axis=-1, keepdims=True))
            dob = do.astype(BF16)
            dp = _dot(dob, vb, NT)
            ds = p * (dp - jnp.sum(do * o, axis=-1, keepdims=True))
            dtable_ref[e] += ds
            dsb = ds.astype(BF16)
            dq_ref[:, lanes] = (_dot(dsb, k_ref[in_qkv, lanes]) * scale).astype(dq_ref.dtype)
            dk_ref[keys, lanes] += _dot(dsb, q_ref[:, lanes], TN) * scale
            dv_ref[keys, lanes] += _dot(pb, dob, TN)

        @pl.when(n == T // QROWS - 1)
        def _():
            for e in range(PAIR):
                dband_ref[e] = sum(dtable_ref[e, g * CHUNK:(g + 1) * CHUNK, g * CHUNK:g * CHUNK + BAND]
                                   for g in range(QG))
            dkb_ref[...] = dk_ref[KPAD:, :].astype(dkb_ref.dtype)
            dvb_ref[...] = dv_ref[KPAD:, :].astype(dvb_ref.dtype)

    per_step, queries, keys, values, grads, table, vec = _attn_specs(T, n_pairs)
    H = n_pairs * PAIR
    return _call(
        body, rider, name="attn_bwd", grid=(n_pairs, T // QROWS),
        in_specs=[queries, keys, values, _probs_spec(), vec, per_step],
        out_specs=[per_step, grads, grads, table, vec],
        out_shape=[jax.ShapeDtypeStruct((T, W), BF16)] * 3 + [jax.ShapeDtypeStruct((H, CHUNK, BAND), F32),
                                                              jax.ShapeDtypeStruct((1, W), F32)],
        scratch_shapes=[pltpu.VMEM((PAIR, QROWS, UNION), F32)] + [pltpu.VMEM((KPAD + T, PAIR * HD_A), F32)] * 2,
        compiler_params=_params(("arbitrary", "arbitrary"), 40),
        operands=(qkv, qkv, qkv, probs, gain, dmixin))


N_DIAG = CHUNK + BAND - 1


def _bias_band(rel_bias):
    H = rel_bias.shape[0]
    idx = np.clip(BAND - 1 - np.arange(N_DIAG), -MAX_REL, MAX_REL) + MAX_REL
    rolled = rel_bias[:, idx[(np.arange(N_DIAG) + CHUNK - 1) % N_DIAG]]
    flat = jnp.broadcast_to(rolled[:, None, :], (H, CHUNK, N_DIAG)).reshape(H, CHUNK * N_DIAG)
    return flat[:, :CHUNK * (N_DIAG - 1)].reshape(H, CHUNK, N_DIAG - 1)[:, :, :BAND]


def _bias_band_grad(dband):
    H = dband.shape[0]
    skew = jnp.pad(dband, ((0, 0), (0, 0), (CHUNK - 1, 0))).reshape(H, CHUNK * N_DIAG)
    skew = jnp.pad(skew, ((0, 0), (0, CHUNK))).reshape(H, CHUNK, N_DIAG + 1)
    diag = jnp.sum(skew, axis=1)[:, :N_DIAG]
    n_far = BAND - MAX_REL
    far = jnp.sum(diag[:, :n_far], axis=1, keepdims=True)
    near = diag[:, n_far:][:, ::-1]
    zeros = jnp.zeros((H, MAX_REL - (CHUNK - 1)), F32)
    return jnp.concatenate([zeros, near, far], axis=1)


def _tri(n, lower):
    r = lax.broadcasted_iota(jnp.int32, (n, n), 0)
    c = lax.broadcasted_iota(jnp.int32, (n, n), 1)
    return jnp.where((c <= r) if lower else (c >= r), 1.0, 0.0).astype(F32)


def _hgrn_gates(zq_ref, zf_ref, lbl_ref, q_s, k_s, b_s):
    lb = jax.nn.sigmoid(lbl_ref[0:1, :] - lbl_ref[1:2, :])
    zq = zq_ref[...]
    sig = jax.nn.sigmoid(zf_ref[...])
    f = lb + (1.0 - lb) * sig
    sq = jax.nn.sigmoid(zq)
    q_s[...] = zq * sq
    k_s[...] = 1.0 - f
    b_s[...] = _dot(_tri(CHUNK, True), jnp.log(f), precision=HIGHEST)
    return lb, sig, f, sq


def _sub_rows(i):
    return pl.ds(i * SUB, SUB)


def _row_mask(s):
    return lax.broadcasted_iota(jnp.int32, (SUB, HD_B), 0) >= s


def _decay_from(b_sub, b_row, s):
    return jnp.where(_row_mask(s), jnp.exp(jnp.minimum(b_sub - b_row, 0.0)), 0.0)


def _hgrn_fwd(proj, lb_logits, gnorm_g, rider=None):
    T = proj.shape[0]
    nC = T // CHUNK
    W = lb_logits.shape[1]
    G = W // HD_B // HGRN_HEADS
    col0 = (proj.shape[1] - 4 * W) // (HD_B * HGRN_HEADS)
    wide = HGRN_HEADS * HD_B

    def body(*refs):
        @pl.when(pl.program_id(1) == 0)
        def _():
            refs[9][...] = jnp.zeros_like(refs[9])

        for h in range(HGRN_HEADS):
            lanes = pl.ds(h * HD_B, HD_B)
            one_head(*[r.at[:, lanes] for r in refs[:5]], refs[5], *[r.at[:, lanes] for r in refs[6:8]],
                     *[r.at[h] for r in refs[8:]])

    def one_head(zq_ref, zf_ref, xi_ref, zg_ref, lbl_ref, gn_ref, mix_ref, o_ref, stall_ref, st_ref, q_s, k_s, b_s, acc_s):
        _hgrn_gates(zq_ref, zf_ref, lbl_ref, q_s, k_s, b_s)
        q, k, b = q_s[...], k_s[...], b_s[...]
        st = st_ref[...]
        stall_ref[...] = st
        b_last = b_s[CHUNK - 1:CHUNK, :]
        acc_s[...] = _dot((q * jnp.exp(b)).astype(BF16), st.astype(BF16), NT)
        for i in range(CHUNK // SUB):
            rows = _sub_rows(i)
            q_i, b_i = q_s[rows, :], b_s[rows, :]
            acc = jnp.zeros((SUB, HD_B), F32)
            if i:
                past = pl.ds(0, i * SUB)
                b_ref = b_s[i * SUB - 1:i * SUB, :]
                qs = (q_i * jnp.exp(b_i - b_ref)).astype(BF16)
                ks = (k_s[past, :] * jnp.exp(b_ref - b_s[past, :])).astype(BF16)
                acc += _dot(_dot(qs, ks, NT).astype(BF16), xi_ref[past, :].astype(BF16))
            for s in range(SUB):
                row = pl.ds(i * SUB + s, 1)
                w = q_i * _decay_from(b_i, b_s[row, :], s)
                acc += jnp.sum(w * k_s[row, :], axis=-1, keepdims=True) * xi_ref[row, :]
            acc_s[rows, :] += acc
        o = acc_s[...]
        kd = (k * jnp.exp(b_last - b)).astype(BF16)
        st_ref[...] = st * jnp.exp(b_last) + _dot(xi_ref[...].astype(BF16), kd, TN)
        o_ref[...] = o
        zg = zg_ref[...]
        rr = lax.rsqrt(jnp.mean(o * o, axis=-1, keepdims=True) + EPS)
        mix_ref[...] = (o * rr * gn_ref[...] * (zg * jax.nn.sigmoid(zg))).astype(mix_ref.dtype)

    col = lambda part: pl.BlockSpec((CHUNK, wide), lambda g, n: (n, col0 + part * G + g))
    out_blk = pl.BlockSpec((CHUNK, wide), lambda g, n: (n, g))
    tile = pltpu.VMEM((HGRN_HEADS, CHUNK, HD_B), F32)
    return _call(
        body, rider, name="hgrn_fwd", grid=(G, nC),
        in_specs=[col(0), col(1), col(2), col(3), pl.BlockSpec((2, wide), lambda g, n: (0, g)),
                  pl.BlockSpec((1, HD_B), lambda g, n: (0, 0))],
        out_specs=[out_blk, out_blk, pl.BlockSpec((HGRN_HEADS, None, HD_B, HD_B), lambda g, n: (g, n, 0, 0))],
        out_shape=[jax.ShapeDtypeStruct((T, W), BF16), jax.ShapeDtypeStruct((T, W), F32),
                   jax.ShapeDtypeStruct((G * HGRN_HEADS, nC, HD_B, HD_B), F32)],
        scratch_shapes=[pltpu.VMEM((HGRN_HEADS, HD_B, HD_B), F32), tile, tile, tile, tile],
        compiler_params=_params(("arbitrary", "arbitrary")),
        operands=(proj, proj, proj, proj, lb_logits, gnorm_g))


def _hgrn_bwd(proj, lb_logits, gnorm_g, o_b, st_all, dmixin, d_attn, rider=None):
    T = proj.shape[0]
    nC = T // CHUNK
    W = lb_logits.shape[1]
    wa = d_attn[0].shape[1]
    assert W == HGRN_HEADS * HD_B, "one grid step takes every head: it writes whole rows of d proj"
    G = W // HD_B // HGRN_HEADS
    wide = HGRN_HEADS * HD_B
    col0 = (proj.shape[1] - 4 * W) // wide
    dcol0 = (dmixin.shape[1] - W) // wide

    def body(*refs):
        g, n = pl.program_id(0), pl.program_id(1)
        dproj_ref, dl0_ref, dgn_ref, dst_ref = refs[12:16]
        for i in range(3):
            dproj_ref[:, i * wa:(i + 1) * wa] = refs[9 + i][...]

        @pl.when(n == 0)
        def _():
            dst_ref[...] = jnp.zeros_like(dst_ref)
            dl0_ref[...] = jnp.zeros_like(dl0_ref)

        @pl.when((n == 0) & (g == 0))
        def _():
            dgn_ref[...] = jnp.zeros_like(dgn_ref)

        for h in range(HGRN_HEADS):
            lanes = pl.ds(h * HD_B, HD_B)
            cut = lambda r: r.at[:, lanes]
            parts = [dproj_ref.at[:, pl.ds(3 * wa + part * W + h * HD_B, HD_B)] for part in range(4)]
            one_head(*[cut(r) for r in refs[:5]], refs[5], cut(refs[6]), refs[7].at[h], cut(refs[8]),
                     *parts, cut(dl0_ref), dgn_ref, *[r.at[h] for r in refs[15:]])

    def one_head(zq_ref, zf_ref, xi_ref, zg_ref, lbl_ref, gn_ref, o_ref, st_ref, dout_ref,
                 dzq_ref, dzf_ref, dxi_ref, dzg_ref, dl0_ref, dgn_ref, dst_ref, q_s, k_s, b_s, do_s, dq_s, dk_s, di_s):
        lb, sig, f, sq = _hgrn_gates(zq_ref, zf_ref, lbl_ref, q_s, k_s, b_s)
        q, k, b = q_s[...], k_s[...], b_s[...]
        zg, o, dout = zg_ref[...], o_ref[...], dout_ref[...]
        sg = jax.nn.sigmoid(zg)
        rr = lax.rsqrt(jnp.mean(o * o, axis=-1, keepdims=True) + EPS)
        on = o * rr
        gn = gn_ref[...]
        dzg_ref[...] = (dout * on * gn * sg * (1.0 + zg * (1.0 - sg))).astype(dzg_ref.dtype)
        d_on = dout * zg * sg
        dgn_ref[...] += _colsum(d_on * on)
        d_on = d_on * gn
        do = rr * (d_on - on * jnp.mean(d_on * on, axis=-1, keepdims=True))
        do_s[...] = do
        dob = do.astype(BF16)
        st, dst = st_ref[...], dst_ref[...]
        b_last = b_s[CHUNK - 1:CHUNK, :]
        eb, e_last, k_dec = jnp.exp(b), jnp.exp(b_last), jnp.exp(b_last - b)
        qt, kd = q * eb, k * k_dec
        dstb = dst.astype(BF16)
        xib = xi_ref[...].astype(BF16)
        d_kd = _dot(xib, dstb)
        dq_s[...] = _dot(dob, st.astype(BF16)) * eb
        dk_s[...] = d_kd * k_dec
        di_s[...] = _dot(kd.astype(BF16), dstb, NT)
        d_b_last = e_last * _colsum(st * dst) + _colsum(d_kd * kd)
        dst_ref[...] = _dot(dob, qt.astype(BF16), TN) + dst * e_last
        for i in range(CHUNK // SUB):
            rows = _sub_rows(i)
            q_i, b_i, do_i = q_s[rows, :], b_s[rows, :], do_s[rows, :]
            dq_i = jnp.zeros((SUB, HD_B), F32)
            if i:
                past = pl.ds(0, i * SUB)
                b_ref = b_s[i * SUB - 1:i * SUB, :]
                e_q, e_k = jnp.exp(b_i - b_ref), jnp.exp(b_ref - b_s[past, :])
                qs, ks = (q_i * e_q).astype(BF16), (k_s[past, :] * e_k).astype(BF16)
                xi_p, do_b = xi_ref[past, :].astype(BF16), do_i.astype(BF16)
                di_s[past, :] += _dot(_dot(ks, qs, NT).astype(BF16), do_b)
                dq_i += _dot(_dot(do_b, xi_p, NT).astype(BF16), ks) * e_q
                dk_s[past, :] += _dot(_dot(xi_p, do_b, NT).astype(BF16), qs) * e_k
            for s in range(SUB):
                row = pl.ds(i * SUB + s, 1)
                k_row, i_row = k_s[row, :], xi_ref[row, :]
                e = _decay_from(b_i, b_s[row, :], s)
                w = q_i * e
                a_col = jnp.sum(w * k_row, axis=-1, keepdims=True)
                da_col = jnp.sum(do_i * i_row, axis=-1, keepdims=True)
                di_s[row, :] += _colsum(a_col * do_i)
                dq_i += da_col * e * k_row
                dk_s[row, :] += _colsum(da_col * w)
            dq_s[rows, :] += dq_i
        dq, dk = dq_s[...], dk_s[...]
        db = q * dq - k * dk
        is_last = lax.broadcasted_iota(jnp.int32, (CHUNK, HD_B), 0) == CHUNK - 1
        db = db + jnp.where(is_last, d_b_last, 0.0)
        df = _dot(_tri(CHUNK, False), db, precision=HIGHEST) / f - dk
        dzf_ref[...] = (df * (1.0 - lb) * sig * (1.0 - sig)).astype(dzf_ref.dtype)
        dl0_ref[...] += _colsum(df * (1.0 - sig)) * (lb * (1.0 - lb))
        zq = zq_ref[...]
        dzq_ref[...] = (dq * sq * (1.0 + zq * (1.0 - sq))).astype(dzq_ref.dtype)
        dxi_ref[...] = di_s[...].astype(dxi_ref.dtype)

    rev = lambda n: nC - 1 - n
    col = lambda part: pl.BlockSpec((CHUNK, wide), lambda g, n: (rev(n), col0 + part * G + g))
    blk = pl.BlockSpec((CHUNK, wide), lambda g, n: (rev(n), g))
    tile = pltpu.VMEM((HGRN_HEADS, CHUNK, HD_B), F32)
    rows = lambda width: pl.BlockSpec((CHUNK, width), lambda g, n: (rev(n), 0))
    return _call(
        body, rider, name="hgrn_bwd", grid=(G, nC),
        in_specs=[col(0), col(1), col(2), col(3), pl.BlockSpec((2, wide), lambda g, n: (0, g)),
                  pl.BlockSpec((1, HD_B), lambda g, n: (0, 0)), blk,
                  pl.BlockSpec((HGRN_HEADS, None, HD_B, HD_B), lambda g, n: (g, rev(n), 0, 0)),
                  pl.BlockSpec((CHUNK, wide), lambda g, n: (rev(n), dcol0 + g)), rows(wa), rows(wa), rows(wa)],
        out_specs=[rows(3 * wa + 4 * W), pl.BlockSpec((1, wide), lambda g, n: (0, g)),
                   pl.BlockSpec((1, HD_B), lambda g, n: (0, 0))],
        out_shape=[jax.ShapeDtypeStruct((T, 3 * wa + 4 * W), BF16), jax.ShapeDtypeStruct((1, W), F32),
                   jax.ShapeDtypeStruct((1, HD_B), F32)],
        scratch_shapes=[pltpu.VMEM((HGRN_HEADS, HD_B, HD_B), F32)] + [tile] * 7,
        compiler_params=_params(("arbitrary", "arbitrary")),
        operands=(proj, proj, proj, proj, lb_logits, gnorm_g, o_b, st_all, dmixin, *d_attn))


def _adamw_math(g, w, m, v):
    m = B1 * m + (1.0 - B1) * g
    v = B2 * v + (1.0 - B2) * (g * g)
    m_hat = m / (1.0 - B1 ** STEP)
    v_hat = v / (1.0 - B2 ** STEP)
    return -LR * (m_hat / (jnp.sqrt(v_hat) + ADAM_EPS) + WD * w), m, v


def _adamw(g, w, m, v, name):
    R, C = g.shape
    tr = _row_tile(R, C)

    def body(g_ref, w_ref, m_ref, v_ref, go_ref, d_ref, mo_ref, vo_ref):
        g = g_ref[...]
        go_ref[...] = g
        d_ref[...], mo_ref[...], vo_ref[...] = _adamw_math(g, w_ref[...], m_ref[...], v_ref[...])

    blk = pl.BlockSpec((tr, C), lambda i: (i, 0))
    return pl.pallas_call(
        body, name=name, grid=(R // tr,), in_specs=[blk] * 4, out_specs=[blk] * 4,
        out_shape=[jax.ShapeDtypeStruct((R, C), F32)] * 4, compiler_params=_params(("parallel",), 40),
    )(g, w, m, v)


def _sum_pair(g_full, from_sibling, sel, name):
    Q, K, Ns = g_full.shape
    kh = K // 2
    tr = _row_tile(kh, Ns)
    nh = kh // tr

    def body(sel_ref, a_ref, b_ref, o_ref):
        o_ref[...] = (a_ref[...].astype(F32) + b_ref[...].astype(F32)).astype(o_ref.dtype)

    return pl.pallas_call(
        body, name=name,
        grid_spec=pltpu.PrefetchScalarGridSpec(
            num_scalar_prefetch=1, grid=(Q, nh),
            in_specs=[pl.BlockSpec((None, tr, Ns), lambda q, i, sel: (q, sel[1] * nh + i, 0)),
                      pl.BlockSpec((None, tr, Ns), lambda q, i, sel: (q, i, 0))],
            out_specs=pl.BlockSpec((None, tr, Ns), lambda q, i, sel: (q, i, 0))),
        out_shape=jax.ShapeDtypeStruct((Q, kh, Ns), BF16), compiler_params=_params(("parallel", "parallel")),
    )(sel, g_full, from_sibling)


def _sum_chips(pair_sum, from_chips, sel, name):
    Q, kh, Ns = pair_sum.shape
    tr = _row_tile(kh, Ns)
    nh = kh // tr

    def body(sel_ref, a_ref, b0_ref, b1_ref, b2_ref, o_ref):
        up = lambda r: r[...].astype(F32)
        o_ref[...] = ((up(a_ref) + up(b0_ref)) + up(b1_ref)) + up(b2_ref)

    recv = lambda k: pl.BlockSpec((None, tr, Ns), lambda i, sel: (k, i, 0))
    return pl.pallas_call(
        body, name=name,
        grid_spec=pltpu.PrefetchScalarGridSpec(
            num_scalar_prefetch=1, grid=(nh,),
            in_specs=[pl.BlockSpec((None, tr, Ns), lambda i, sel: (sel[0], i, 0)), recv(0), recv(1), recv(2)],
            out_specs=pl.BlockSpec((tr, Ns), lambda i, sel: (sel[1] * nh + i, 0))),
        out_shape=jax.ShapeDtypeStruct((2 * kh, Ns), F32), compiler_params=_params(("parallel",)),
    )(sel, pair_sum, from_chips, from_chips, from_chips)


def _gather_small(v, name):
    R, L = v.shape

    def body(v_ref, out_ref, send_sems, recv_sems):
        x, y, c = _place()
        me = 4 * x + 2 * y + c
        out_ref[me] = v_ref[...]
        peers = [(_flip(x, k >> 2 & 1), _flip(y, k >> 1 & 1), _flip(c, k & 1)) for k in range(1, N_DEV)]

        def copy(k, row, to):
            return pltpu.make_async_remote_copy(src_ref=v_ref, dst_ref=out_ref.at[row], send_sem=send_sems.at[k],
                                                recv_sem=recv_sems.at[k], device_id=to, device_id_type=MESH)

        sends = [copy(k, me, peer) for k, peer in enumerate(peers)]
        for cp in sends:
            cp.start()
        for k, (px, py, pc) in enumerate(peers):
            copy(k, 4 * px + 2 * py + pc, (x, y, c)).wait_recv()
        for cp in sends:
            cp.wait_send()

    vmem = pl.BlockSpec(memory_space=pltpu.VMEM)
    return pl.pallas_call(
        body, name=name, in_specs=[vmem], out_specs=vmem, out_shape=jax.ShapeDtypeStruct((N_DEV, R, L), F32),
        scratch_shapes=[pltpu.SemaphoreType.DMA((N_DEV - 1,)), pltpu.SemaphoreType.DMA((N_DEV - 1,))],
    )(v)


def _small_rider(v):
    def copies(ins, outs, send_sems, recv_sems):
        x, y, c = _place()
        peers = [(_flip(x, k >> 2 & 1), _flip(y, k >> 1 & 1), _flip(c, k & 1)) for k in range(1, N_DEV)]

        def copy(k, row, to):
            return pltpu.make_async_remote_copy(src_ref=ins[0], dst_ref=outs[0].at[row], send_sem=send_sems.at[k],
                                                recv_sem=recv_sems.at[k], device_id=to, device_id_type=MESH)

        sends = [copy(k, 4 * x + 2 * y + c, peer) for k, peer in enumerate(peers)]
        return sends, [copy(k, 4 * px + 2 * py + pc, (x, y, c)) for k, (px, py, pc) in enumerate(peers)]

    def start(*refs):
        for cp in copies(*refs)[0]:
            cp.start()

    def finish(*refs):
        sends, arrivals = copies(*refs)
        for cp in arrivals:
            cp.wait_recv()
        for cp in sends:
            cp.wait_send()

    return _Rider([v], [jax.ShapeDtypeStruct((N_DEV, *v.shape), F32)], N_DEV - 1, start, finish)


def _silu(v):
    return v * jax.nn.sigmoid(v)


def _ada_fwd(c_all, w_ada, rider, tn=512):
    M, D = c_all.shape
    Ns = w_ada.shape[1]
    steps = Ns // tn

    def body(c_ref, w_ref, o_ref, all_ref, send_sems, recv_sems):
        j = pl.program_id(0)
        o_ref[:, pl.ds(pl.multiple_of(j * tn, tn), tn)] = _dot(_silu(c_ref[...]).astype(BF16), w_ref[...].astype(BF16))

        @pl.when(j == steps - 1)
        def _():
            x, y, c = _place()
            peers = [(_flip(x, k >> 2 & 1), _flip(y, k >> 1 & 1), _flip(c, k & 1)) for k in range(1, N_DEV)]

            def copy(k, row, to):
                return pltpu.make_async_remote_copy(
                    src_ref=o_ref.at[pl.ds(0, N_DEV)], dst_ref=all_ref.at[row], send_sem=send_sems.at[k],
                    recv_sem=recv_sems.at[k], device_id=to, device_id_type=MESH)

            sends = [copy(k, 4 * x + 2 * y + c, peer) for k, peer in enumerate(peers)]
            for cp in sends:
                cp.start()
            for k, (px, py, pc) in enumerate(peers):
                copy(k, 4 * px + 2 * py + pc, (x, y, c)).wait_recv()
            for cp in sends:
                cp.wait_send()

    (out, out_all), rode = _call(
        body, rider, name="ada_fwd", grid=(steps,),
        in_specs=[pl.BlockSpec((M, D), lambda j: (0, 0)), pl.BlockSpec((D, tn), lambda j: (0, j))],
        out_specs=[pl.BlockSpec((M, Ns), lambda j: (0, 0)), ANY],
        out_shape=[jax.ShapeDtypeStruct((M, Ns), F32), jax.ShapeDtypeStruct((N_DEV, N_DEV, Ns), F32)],
        scratch_shapes=[pltpu.SemaphoreType.DMA((N_DEV - 1,)), pltpu.SemaphoreType.DMA((N_DEV - 1,))],
        compiler_params=_params(("arbitrary",)), operands=(c_all, w_ada))
    return out, out_all, rode


def _ada_bwd(c_all, dmod, w, m, v, tk=256, tn=1536):
    M, D = c_all.shape
    Ns = dmod.shape[1]

    def body(c_ref, d_ref, w_ref, m_ref, v_ref, g_ref, dl_ref, mo_ref, vo_ref):
        g = _dot(_silu(c_ref[...]).astype(BF16), d_ref[...].astype(BF16), TN)
        g_ref[...] = g
        dl_ref[...], mo_ref[...], vo_ref[...] = _adamw_math(g, w_ref[...], m_ref[...], v_ref[...])

    blk = pl.BlockSpec((tk, tn), lambda i, j: (i, j))
    return pl.pallas_call(
        body, name="ada_bwd", grid=(D // tk, Ns // tn),
        in_specs=[pl.BlockSpec((M, tk), lambda i, j: (0, i)), pl.BlockSpec((M, tn), lambda i, j: (0, j)), blk, blk, blk],
        out_specs=[blk] * 4, out_shape=[jax.ShapeDtypeStruct((D, Ns), F32)] * 4,
        compiler_params=_params(("parallel", "parallel"), 40),
    )(c_all, dmod, w, m, v)


def _small_update(g_all, w, m, v):
    R, L = w.shape

    def body(g_ref, w_ref, m_ref, v_ref, go_ref, d_ref, mo_ref, vo_ref):
        g = g_ref[0]
        for d in range(1, N_DEV):
            g = g + g_ref[d]
        go_ref[...] = g
        d_ref[...], mo_ref[...], vo_ref[...] = _adamw_math(g, w_ref[...], m_ref[...], v_ref[...])

    return pl.pallas_call(body, name="small_update", out_shape=[jax.ShapeDtypeStruct((R, L), F32)] * 4)(g_all, w, m, v)


def _pack(parts, rows):
    flat = jnp.concatenate([p.reshape(-1) for p in parts])
    return jnp.pad(flat, (0, rows * 128 - flat.shape[0])).reshape(rows, 128)


def _unpack(packed, shapes):
    flat, out, at = packed.reshape(-1), [], 0
    for shp in shapes:
        size = 1
        for d in shp:
            size *= d
        out.append(flat[at:at + size].reshape(shp))
        at += size
    return out


def _layer(x, tgt, mod, wts, rel_bias, attn_norm_g, lb_logits, gnorm_g, ln1_g, ln1_b, ln2_g, ln2_b, place=None):
    T, D = x.shape
    aw = attn_norm_g.shape[1]
    shift1, scale1, gate1, shift2, scale2, gate2 = [mod[i:i + 1] for i in range(6)]

    def gather(n, rows=None, into=None, before=None, last=True):
        return None if place is None else _gather_rider(wts[n], rows, None if into is None else into[0], before, last)

    def gathered(n, rode):
        return wts[n] if place is None else lax.dynamic_update_index_in_dim(rode[0], wts[n], place[0], 0)

    def blocks(g):
        return g.reshape(N_CHIPS, -1, g.shape[2])

    def to_sibling(g):
        return None if place is None else _pair_rider(g)

    def pair_sum(n, g, rode=None):
        if place is None:
            return g
        rode = _alone(_pair_rider(g), n + "_send_pair") if rode is None else rode
        return _sum_pair(g, rode[0], place[1], n + "_sum_pair")

    def to_chips(p, rows=None, into=None):
        return None if place is None else _chips_rider(p, rows, None if into is None else into[0])

    def summed(n, p, rode):
        return p if place is None else _sum_chips(p, rode[0], place[1], n + "_sum_chips")

    def to_both(block):
        return None if place is None else _share_rider(block)

    def carrying(mm, *args, rider, **kw):
        return mm(*args, rider=rider, **kw) if rider is not None else (mm(*args, **kw), None)

    def to_sibling_acts(a, b):
        return None if place is None else _acts_rider(a, b)

    def pair_grad(name, a, b, tn, rider, arrived=None, late_rider=None):
        if place is None:
            return _mm_tn(a, b, q=N_CHIPS, tk=512, tn=tn, tt=T, name=name), None
        kh = a.shape[1] // 2
        mine = lax.dynamic_slice_in_dim(a, place[1][1] * kh, kh, axis=1)
        part, rode = carrying(_mm_tn, mine, b, q=N_CHIPS, tk=512, tn=tn, tt=T, name=name + "_own",
                              rider=_join(None if arrived else _acts_rider(a, b), rider))
        (a_sib, b_sib), rode = arrived or rode[:2], rode if arrived else rode[2:]
        out, late = _mm_tn_add(a_sib, b_sib, part, tk=512, tn=tn, name=name + "_sib", rider=late_rider)
        return out, (rode or []) + late

    if place is None:
        h1, rode = _pre_mixer(x, scale1, shift1), None
    else:
        h1, rode = _pre_mixer(x, scale1, shift1, wts["w_in_last_part"])
    w_in = gathered("w_in", rode)
    n_qkv = 3 * aw // 256
    kh_o, kh_f, kh_out = [wts[n].shape[-2] // 2 for n in ("w_o", "w_ffn_in", "w_ffn_out")]
    o_cut, f_cuts, out_cut = 3 * kh_o // 8, (7 * kh_f // 16, 7 * kh_f // 8), kh_out // 11
    qkv, rode = carrying(_mm_nn, h1, w_in, tm=ZPAD, tn=256, tk=D, name="proj_qkv", cols=(0, n_qkv), o_dtype=BF16,
                         pad_rows=ZPAD, rider=gather("w_o", (0, o_cut), last=False))
    proj, rode = carrying(_mm_nn, h1, w_in, tm=2048, tn=256, tk=D, name="proj_rec",
                          cols=(n_qkv, N_CHIPS * w_in.shape[2] // 256),
                          rider=gather("w_o", (o_cut, kh_o - o_cut), rode, before=(0, o_cut)))
    w_o3 = gathered("w_o", rode).reshape(1, D, D)
    bias = _bias_band(rel_bias)
    (mix_a, probs), rode = _attn_fwd(qkv, bias, attn_norm_g, rider=gather("w_ffn_in", (0, f_cuts[0]), last=False))
    (mix_b, o_b, st_all), rode = _hgrn_fwd(
        proj, lb_logits, gnorm_g,
        rider=gather("w_ffn_in", (f_cuts[0], f_cuts[1] - f_cuts[0]), rode, before=(0, f_cuts[0]), last=False))
    mixin = jnp.concatenate([mix_a, mix_b], axis=1)
    mix = _mm_nn(mixin, w_o3, tm=1024, tn=512, tk=D, name="mix_out")
    if place is None:
        x1, h2 = _post_mixer(mix, x, gate1, ln1_g, ln1_b, scale2, shift2)
    else:
        (x1, h2), rode = _post_mixer(mix, x, gate1, ln1_g, ln1_b, scale2, shift2, rider=_join(
            gather("w_ffn_in", (f_cuts[1], kh_f - f_cuts[1]), rode, before=(f_cuts[0], f_cuts[1] - f_cuts[0])),
            gather("w_ffn_out", (0, out_cut), last=False)))
    w_ffn_in = gathered("w_ffn_in", rode)
    (gate, up, act), rode = _ffn_in_swiglu(
        h2, w_ffn_in, tm=2048, tn=256,
        rider=gather("w_ffn_out", (out_cut, kh_out - out_cut), rode and rode[1:], before=(0, out_cut)))
    w_out3 = gathered("w_ffn_out", rode)
    w_out3 = w_out3.reshape(1, -1, w_out3.shape[2])
    d_ff = w_out3.shape[1]
    f = _mm_nn(act, w_out3, tm=1024, tn=512, tk=d_ff, name="ffn_out")
    du2, df, acc2 = _loss_head(f, x1, tgt, gate2, ln2_g, ln2_b)
    loss = (0.5 / D) * jnp.sum(acc2[3])
    g = blocks(_mm_tn(act, df, q=1, tk=512, tn=1024, tt=T, name="g_ffn_out"))
    (dff,), rode = _d_act_swiglu(df, w_out3, gate, up, tm=1024, to=512, rider=to_sibling(g))
    p_out = pair_sum("w_ffn_out", g, rode)
    cut = 25 * p_out.shape[1] // 44
    dh2, rode = carrying(_mm_nt, dff, w_ffn_in, tm=1024, to=1024, tn=w_ffn_in.shape[2], name="d_h2",
                         rider=_join(to_chips(p_out, (0, cut)), to_sibling_acts(h2, dff)))
    p_fin, rode = pair_grad("g_ffn_in", h2, dff, w_ffn_in.shape[2] // 2,
                            to_chips(p_out, (cut, p_out.shape[1] - cut), rode), arrived=rode and rode[1:])
    g_ffn_out = summed("w_ffn_out", p_out, rode)
    if place is None:
        du1, dmix, acc1 = _mid_bwd(dh2, du2, x1, mix, x, gate1, ln1_g, scale2)
    else:
        (du1, dmix, acc1), (g_ffn_out,) = _mid_bwd(dh2, du2, x1, mix, x, gate1, ln1_g, scale2, rider=to_both(g_ffn_out))
    g = blocks(_mm_tn(mixin, dmix, q=1, tk=512, tn=1024, tt=T, name="g_o"))
    dmixin, rode = carrying(_mm_nt, dmix, w_o3, tm=1024, to=512, tn=D, name="d_mixin", rider=to_sibling(g))
    p_o = pair_sum("w_o", g, rode)
    cut = p_fin.shape[1] // 2
    (dq, dk, dv, dbias, dgain), rode = _attn_bwd(qkv, probs, attn_norm_g, dmixin, rider=to_chips(p_fin, (0, cut)))
    (dproj, dl0, dgn), rode = _hgrn_bwd(
        proj, lb_logits, gnorm_g, o_b, st_all, dmixin, (dq, dk, dv),
        rider=_join(to_chips(p_fin, (cut, p_fin.shape[1] - cut), rode), to_chips(p_o)))
    g_ffn_in, g_o = summed("w_ffn_in", p_fin, rode[:1]), summed("w_o", p_o, rode[1:])
    p_in, rode = pair_grad("g_in", h1, dproj, w_in.shape[2] // 2, None,
                           late_rider=_join(to_both(g_ffn_in), to_both(g_o)))
    if place is not None:
        g_ffn_in, g_o = rode
    cut = 3 * p_in.shape[1] // 4
    dh1, rode = carrying(_mm_nt, dproj, w_in, tm=1024, to=1024, tn=w_in.shape[2], name="d_h1",
                         rider=to_chips(p_in, (0, cut)))
    if place is None:
        (grad_x, acc0), g_in = _first_bwd(dh1, du1, x, scale1), p_in
    else:
        (grad_x, acc0), rode = _first_bwd(dh1, du1, x, scale1, rider=to_chips(p_in, (cut, p_in.shape[1] - cut), rode))
        g_in, = _alone(to_both(summed("w_in", p_in, rode)), "w_in_share")
    dmod = jnp.concatenate([acc0[1:2], acc0[0:1], acc1[4:5], acc1[1:2], acc1[0:1], acc2[2:3]], axis=0)
    small = dict(rel_bias=_bias_band_grad(dbias), attn_norm_g=dgain,
                 lb_logits=jnp.concatenate([dl0, -dl0], axis=0), gnorm_g=dgn,
                 ln1_g=acc1[2:3], ln1_b=acc1[3:4], ln2_g=acc2[0:1], ln2_b=acc2[1:2])
    return loss, grad_x, dict(w_in=g_in, w_o=g_o, w_ffn_in=g_ffn_in, w_ffn_out=g_ffn_out), dmod, small


SMALL = ("rel_bias", "attn_norm_g", "lb_logits", "gnorm_g", "ln1_g", "ln1_b", "ln2_g", "ln2_b")
SMALL_ROWS = 256


def kernel(x, c, w_ada, b_ada, w_in, rel_bias, attn_norm_g, lb_logits, gnorm_g, w_o, ln1_g, ln1_b, w_ffn_in, w_ffn_out, ln2_g, ln2_b, loss_target, m_w_ada, m_b_ada, m_w_in, m_rel_bias, m_attn_norm_g, m_lb_logits, m_gnorm_g, m_w_o, m_ln1_g, m_ln1_b, m_w_ffn_in, m_w_ffn_out, m_ln2_g, m_ln2_b, v_w_ada, v_b_ada, v_w_in, v_rel_bias, v_attn_norm_g, v_lb_logits, v_gnorm_g, v_w_o, v_ln1_g, v_ln1_b, v_w_ffn_in, v_w_ffn_out, v_ln2_g, v_ln2_b):
    mx, my, mc = _place()
    me = 4 * mx + 2 * my + mc
    chip = 2 * mx + my
    sel = jnp.stack([chip, mc]).astype(jnp.int32)
    D = x.shape[2]
    ns_ada = w_ada.shape[2]

    big = dict(w_in=(w_in, m_w_in, v_w_in), w_o=(w_o, m_w_o, v_w_o), w_ffn_in=(w_ffn_in, m_w_ffn_in, v_w_ffn_in),
               w_ffn_out=(w_ffn_out, m_w_ffn_out, v_w_ffn_out))
    shards = dict(w_in=w_in[0].astype(BF16))
    kh = shards["w_in"].shape[0] // 2
    cuts = [part * kh // 32 for part in (0, 12, 19, 22, 26, 32)]
    spans = [(a, b - a) for a, b in zip(cuts, cuts[1:])]

    def w_in_part(i, rode):
        return _gather_rider(shards["w_in"], spans[i], rode and rode[0], spans[i - 1] if i else None, last=i == 4)

    c_own = c.reshape(D // 128, 128)
    rode, c_all = None, None
    for i, n in enumerate(("w_ffn_in", "w_ffn_out", "w_o")):
        rider = _join(w_in_part(i, rode), None if i else _small_rider(c_own))
        (shards[n],), rode = _to_bf16(big[n][0][0], "cast_" + n, rider)
        c_all = c_all if i else rode[1]

    c_all = lax.dynamic_update_index_in_dim(c_all, c_own, me, 0).reshape(N_DEV, D)
    c_all = jnp.pad(c_all, ((0, 16 - N_DEV), (0, 0)))
    mod_cols, mod_all, rode = _ada_fwd(c_all, w_ada[0], w_in_part(3, rode))
    shards["w_in_last_part"] = w_in_part(4, rode)
    mod_all = lax.dynamic_update_index_in_dim(mod_all, mod_cols[:N_DEV], me, 0)
    mod = lax.dynamic_index_in_dim(mod_all[::2], me, axis=1, keepdims=False)
    mod = (mod.reshape(1, -1) + b_ada).reshape(6, D)

    loss, grad_x, g_big, dmod, g_small = _layer(
        x[0], loss_target[0], mod, shards, rel_bias[0], attn_norm_g, lb_logits, gnorm_g, ln1_g, ln1_b, ln2_g, ln2_b,
        place=(chip, sel))

    grads, deltas, new_m, new_v = {}, {}, {}, {}
    for n, (w, m, v) in big.items():
        g, d, mo, vo = _adamw(g_big[n], w[0], m[0], v[0], "adamw_" + n)
        grads[n], deltas[n], new_m[n], new_v[n] = g[None], d[None], mo[None], vo[None]

    small_in = dict(rel_bias=(rel_bias, m_rel_bias, v_rel_bias), attn_norm_g=(attn_norm_g, m_attn_norm_g, v_attn_norm_g),
                    lb_logits=(lb_logits, m_lb_logits, v_lb_logits), gnorm_g=(gnorm_g, m_gnorm_g, v_gnorm_g),
                    ln1_g=(ln1_g, m_ln1_g, v_ln1_g), ln1_b=(ln1_b, m_ln1_b, v_ln1_b), ln2_g=(ln2_g, m_ln2_g, v_ln2_g),
                    ln2_b=(ln2_b, m_ln2_b, v_ln2_b))
    g_all = _gather_small(_pack([dmod] + [g_small[n] for n in SMALL] + [loss], SMALL_ROWS), "gather_small")
    packed = [_pack([t] + [small_in[n][i] for n in SMALL] + [jnp.zeros((), F32)], SMALL_ROWS)
              for i, t in enumerate((b_ada, m_b_ada, v_b_ada))]
    shapes = [b_ada.shape] + [small_in[n][0].shape for n in SMALL] + [()]
    outs = [_unpack(o, shapes) for o in _small_update(g_all, *packed)]
    loss = outs[0][-1]
    for i, n in enumerate(("b_ada",) + SMALL):
        grads[n], deltas[n], new_m[n], new_v[n] = outs[0][i], outs[1][i], outs[2][i], outs[3][i]

    dmod_all = g_all[:, :6 * D // 128].reshape(N_DEV, 6 * D)
    dmod_cols = lax.dynamic_slice_in_dim(dmod_all, chip * ns_ada, ns_ada, axis=1)
    dmod_cols = jnp.pad(dmod_cols, ((0, 16 - N_DEV), (0, 0)))
    g, d, mo, vo = _ada_bwd(c_all, dmod_cols, w_ada[0], m_w_ada[0], v_w_ada[0])
    grads["w_ada"], deltas["w_ada"], new_m["w_ada"], new_v["w_ada"] = g[None], d[None], mo[None], vo[None]

    order = ("w_ada", "b_ada", "w_in", "rel_bias", "attn_norm_g", "lb_logits", "gnorm_g", "w_o", "ln1_g", "ln1_b",
             "w_ffn_in", "w_ffn_out", "ln2_g", "ln2_b")
    return (loss, grad_x[None], *[grads[n] for n in order], *[deltas[n] for n in order],
            *[new_m[n] for n in order], *[new_v[n] for n in order])
```

```python
import numpy as np
import jax
import jax.numpy as jnp
from jax import lax
from jax.experimental import pallas as pl
from jax.experimental.pallas import tpu as pltpu

F32 = jnp.float32
BF16 = jnp.bfloat16
MESH = pl.DeviceIdType.MESH
HIGHEST = lax.Precision.HIGHEST

CHUNK = 64
N_PAST = 8
QG = 4
QROWS = QG * CHUNK
KPAD = N_PAST * CHUNK
ZPAD = 2 * KPAD
UNION = (QG + N_PAST) * CHUNK
BAND = (N_PAST + 1) * CHUNK
HD_A = 64
HD_B = 128
SUB = 16
HGRN_HEADS = 8
MAX_REL = 256
EPS = 1e-5
ALPHA = 2.0 ** 0.25
LR, B1, B2, ADAM_EPS, WD, STEP = 1e-3, 0.9, 0.999, 1e-8, 0.01, 10
N_CHIPS = 4
N_DEV = 8
NEG = -1e30
TILE_BYTES = 3 << 19

NN = ((1,), (0,))
NT = ((1,), (1,))
TN = ((0,), (0,))


def _dot(a, b, dims=NN, precision=None):
    return lax.dot_general(a, b, (dims, ((), ())), preferred_element_type=F32, precision=precision)


def _params(sem=None, vmem_mb=None, **kw):
    return pltpu.CompilerParams(dimension_semantics=sem,
                                vmem_limit_bytes=None if vmem_mb is None else vmem_mb << 20, **kw)


def _row_tile(rows, cols):
    for cand in (512, 256, 128, 64, 32, 16, 8):
        if rows % cand == 0 and cand * cols * 4 <= TILE_BYTES:
            return cand
    raise ValueError((rows, cols))


def _place():
    return lax.axis_index("x"), lax.axis_index("y"), lax.axis_index("c")


def _flip(v, bit):
    return 1 - v if bit else v


ANY = pl.BlockSpec(memory_space=pl.ANY)
CHIP_FLIPS = ((1, 0), (0, 1), (1, 1))


class _Rider:
    def __init__(self, operands, out_shape, n_sems, start, finish, aliases=None):
        self.operands, self.out_shape, self.n_sems, self.start, self.finish = operands, out_shape, n_sems, start, finish
        self.aliases = aliases or {}


def _call(body, rider, *, name, grid, in_specs, out_specs, out_shape, scratch_shapes, compiler_params, operands):
    if rider is None:
        outs = pl.pallas_call(body, name=name, grid=grid, in_specs=in_specs, out_specs=out_specs, out_shape=out_shape,
                              scratch_shapes=scratch_shapes, compiler_params=compiler_params)(*operands)
        return list(outs), []
    n_in, n_out, n_sc = len(in_specs), len(out_specs), len(scratch_shapes)
    r_in, r_out = len(rider.operands), len(rider.out_shape)

    def carried(*refs):
        refs = list(refs)
        cuts = [n_in, r_in, n_out, r_out, n_sc]
        ins, r_ins, outs, r_outs, scratch = [[refs.pop(0) for _ in range(n)] for n in cuts]
        first, last = None, None
        for axis, size in enumerate(grid):
            i = pl.program_id(axis)
            first = (i == 0) if first is None else first & (i == 0)
            last = (i == size - 1) if last is None else last & (i == size - 1)

        @pl.when(first)
        def _():
            rider.start(r_ins, r_outs, *refs)

        body(*ins, *outs, *scratch)

        @pl.when(last)
        def _():
            rider.finish(r_ins, r_outs, *refs)

    sems = [pltpu.SemaphoreType.DMA((rider.n_sems,)), pltpu.SemaphoreType.DMA((rider.n_sems,))]
    outs = pl.pallas_call(carried, name=name, grid=grid, in_specs=list(in_specs) + [ANY] * r_in,
                          out_specs=list(out_specs) + [ANY] * r_out, out_shape=list(out_shape) + rider.out_shape,
                          scratch_shapes=list(scratch_shapes) + sems, compiler_params=compiler_params,
                          input_output_aliases={n_in + i: n_out + o for i, o in rider.aliases.items()},
                          )(*operands, *rider.operands)
    return list(outs[:n_out]), list(outs[n_out:])


def _alone(rider, name):
    def body(*refs):
        ins, outs, sems = refs[:len(rider.operands)], refs[len(rider.operands):-2], refs[-2:]
        rider.start(ins, outs, *sems)
        rider.finish(ins, outs, *sems)

    return pl.pallas_call(
        body, name=name, in_specs=[ANY] * len(rider.operands), out_specs=[ANY] * len(rider.out_shape),
        out_shape=rider.out_shape, input_output_aliases=rider.aliases,
        scratch_shapes=[pltpu.SemaphoreType.DMA((rider.n_sems,)), pltpu.SemaphoreType.DMA((rider.n_sems,))],
    )(*rider.operands)


class _Sems:
    def __init__(self, sems, base):
        self.sems, self.base = sems, base

    @property
    def at(self):
        return self

    def __getitem__(self, k):
        return self.sems.at[self.base + k]


def _join(*riders):
    riders = [r for r in riders if r is not None]
    if len(riders) < 2:
        return riders[0] if riders else None

    def parts(ins, outs, send_sems, recv_sems):
        i = o = s = 0
        for r in riders:
            ni, no = len(r.operands), len(r.out_shape)
            yield r, ins[i:i + ni], outs[o:o + no], _Sems(send_sems, s), _Sems(recv_sems, s)
            i, o, s = i + ni, o + no, s + r.n_sems

    def start(*refs):
        for r, *args in parts(*refs):
            r.start(*args)

    def finish(*refs):
        for r, *args in parts(*refs):
            r.finish(*args)

    aliases, i, o = {}, 0, 0
    for r in riders:
        aliases.update({i + a: o + b for a, b in r.aliases.items()})
        i, o = i + len(r.operands), o + len(r.out_shape)
    return _Rider([a for r in riders for a in r.operands], [s for r in riders for s in r.out_shape],
                  sum(r.n_sems for r in riders), start, finish, aliases)


def _gather_rider(shard, rows=None, into=None, before=None, last=True, relay=False):
    K, Ns = shard.shape
    kh = K // 2
    rows = rows or (0, kh)

    def copies(w_ref, out_ref, send_sems, recv_sems):
        x, y, c = _place()
        chips = [(_flip(x, fx), _flip(y, fy)) for fx, fy in CHIP_FLIPS]

        def half(chip, which, part):
            return out_ref.at[2 * chip[0] + chip[1], pl.ds(which * kh + part[0], part[1]), :]

        def copy(k, dst, to, src=None):
            return pltpu.make_async_remote_copy(src_ref=dst if src is None else src, dst_ref=dst,
                                                send_sem=send_sems.at[k], recv_sem=recv_sems.at[k],
                                                device_id=to, device_id_type=MESH)

        def first():
            return [copy(j, half((x, y), c, rows), (*chip, c), src=w_ref.at[pl.ds(c * kh + rows[0], rows[1]), :])
                    for j, chip in enumerate(chips[:2] if relay else chips)]

        def onward(base, part):
            return [copy(base + j, half(chip, c, part), (x, y, 1 - c)) for j, chip in enumerate(chips)]

        def arriving(base, which, part):
            return [copy(base + j, half(chip, which, part), (x, y, c)) for j, chip in enumerate(chips)]

        def relayed(to_me):
            lo, hi = (rows[0], rows[1] // 2), (rows[0] + rows[1] // 2, rows[1] - rows[1] // 2)
            if to_me:
                return [copy(2, half(chips[2], c, lo), (x, y, c)), copy(9, half(chips[2], c, hi), (x, y, c))]
            return [copy(2, half(chips[0], c, lo), (*chips[1], c)), copy(9, half(chips[1], c, hi), (*chips[0], c))]

        return c, first, onward, arriving, relayed

    def start(ins, outs, send_sems, recv_sems):
        _, first, onward, _, _ = copies(ins[0], outs[0], send_sems, recv_sems)
        for cp in first() + (onward(3, before) if before else []):
            cp.start()

    def finish(ins, outs, send_sems, recv_sems):
        c, first, onward, arriving, relayed = copies(ins[0], outs[0], send_sems, recv_sems)
        sent = first() + (onward(3, before) if before else [])
        passed = onward(6, rows) if last else [None] * 3
        for j, (arrived, cp) in enumerate(zip(arriving(0, c, rows), passed)):
            if relay and j == 2:
                sent += relayed(False)
                for relay_cp in sent[-2:]:
                    relay_cp.start()
                for relay_cp in relayed(True):
                    relay_cp.wait_recv()
            else:
                arrived.wait_recv()
            if last:
                cp.start()
        for arrived in (arriving(3, 1 - c, before) if before else []) + (arriving(6, 1 - c, rows) if last else []):
            arrived.wait_recv()
        for cp in sent + (passed if last else []):
            cp.wait_send()

    full = jax.ShapeDtypeStruct((N_CHIPS, K, Ns), shard.dtype)
    if into is None:
        return _Rider([shard], [full], 10, start, finish)
    return _Rider([shard, into], [full], 10, start, finish, aliases={1: 0})


def _pair_rider(g_full):
    Q, K, Ns = g_full.shape
    kh = K // 2

    def copy(g_ref, got_ref, send_sems, recv_sems):
        x, y, c = _place()
        return pltpu.make_async_remote_copy(src_ref=g_ref.at[:, pl.ds((1 - c) * kh, kh), :], dst_ref=got_ref,
                                            send_sem=send_sems.at[0], recv_sem=recv_sems.at[0],
                                            device_id=(x, y, 1 - c), device_id_type=MESH)

    def start(ins, outs, send_sems, recv_sems):
        copy(ins[0], outs[0], send_sems, recv_sems).start()

    def finish(ins, outs, send_sems, recv_sems):
        copy(ins[0], outs[0], send_sems, recv_sems).wait()

    return _Rider([g_full], [jax.ShapeDtypeStruct((Q, kh, Ns), g_full.dtype)], 1, start, finish)


def _acts_rider(a, b):
    T, K = a.shape
    kh = K // 2

    def copies(ins, outs, send_sems, recv_sems):
        x, y, c = _place()
        pair = [(ins[0].at[:, pl.ds((1 - c) * kh, kh)], outs[0]), (ins[1], outs[1])]
        return [pltpu.make_async_remote_copy(src_ref=src, dst_ref=dst, send_sem=send_sems.at[k], recv_sem=recv_sems.at[k],
                                             device_id=(x, y, 1 - c), device_id_type=MESH)
                for k, (src, dst) in enumerate(pair)]

    def start(*refs):
        for cp in copies(*refs):
            cp.start()

    def finish(*refs):
        for cp in copies(*refs):
            cp.wait()

    return _Rider([a, b], [jax.ShapeDtypeStruct((T, kh), a.dtype), jax.ShapeDtypeStruct(b.shape, b.dtype)], 2,
                  start, finish)


def _share_rider(block):
    K, Ns = block.shape
    kh = K // 2

    def halves(out_ref):
        x, y, c = _place()
        return out_ref.at[pl.ds(c * kh, kh), :], out_ref.at[pl.ds((1 - c) * kh, kh), :], (x, y, 1 - c)

    def start(ins, outs, send_sems, recv_sems):
        mine, _, sibling = halves(outs[0])
        pltpu.make_async_remote_copy(src_ref=mine, dst_ref=mine, send_sem=send_sems.at[0], recv_sem=recv_sems.at[0],
                                     device_id=sibling, device_id_type=MESH).start()

    def finish(ins, outs, send_sems, recv_sems):
        mine, theirs, sibling = halves(outs[0])
        pltpu.make_async_remote_copy(src_ref=theirs, dst_ref=theirs, send_sem=send_sems.at[0], recv_sem=recv_sems.at[0],
                                     device_id=sibling, device_id_type=MESH).wait_recv()
        pltpu.make_async_remote_copy(src_ref=mine, dst_ref=mine, send_sem=send_sems.at[0], recv_sem=recv_sems.at[0],
                                     device_id=sibling, device_id_type=MESH).wait_send()

    return _Rider([block], [jax.ShapeDtypeStruct((K, Ns), block.dtype)], 1, start, finish, aliases={0: 0})


def _chips_rider(pair_sum, rows=None, into=None):
    Q, kh, Ns = pair_sum.shape
    first_row, n_rows = rows or (0, kh)

    def copies(p_ref, got_ref, send_sems, recv_sems):
        x, y, c = _place()
        part = pl.ds(first_row, n_rows)
        out = []
        for j, (fx, fy) in enumerate(CHIP_FLIPS):
            px, py = _flip(x, fx), _flip(y, fy)
            out.append(pltpu.make_async_remote_copy(
                src_ref=p_ref.at[2 * px + py, part, :], dst_ref=got_ref.at[j, part, :], send_sem=send_sems.at[j],
                recv_sem=recv_sems.at[j], device_id=(px, py, c), device_id_type=MESH))
        return out

    def start(ins, outs, send_sems, recv_sems):
        for cp in copies(ins[0], outs[0], send_sems, recv_sems):
            cp.start()

    def finish(ins, outs, send_sems, recv_sems):
        sends = copies(ins[0], outs[0], send_sems, recv_sems)
        for cp in sends:
            cp.wait_recv()
        for cp in sends:
            cp.wait_send()

    got = jax.ShapeDtypeStruct((Q - 1, kh, Ns), pair_sum.dtype)
    if into is None:
        return _Rider([pair_sum], [got], 3, start, finish)
    return _Rider([pair_sum, into], [got], 3, start, finish, aliases={1: 0})


def _mm(a, b, *, grid, a_spec, b_spec, o_spec, o_shape, o_dtype, dims, acc_shape, name, rider=None, zero_rows=0,
        vmem_mb=48):
    nk = grid[2]

    def body(a_ref, b_ref, o_ref, *scratch):
        if zero_rows:
            @pl.when(pl.program_id(0) < zero_rows)
            def _():
                o_ref[...] = jnp.zeros_like(o_ref)

            @pl.when(pl.program_id(0) >= zero_rows)
            def _():
                o_ref[...] = _dot(a_ref[...], b_ref[...], dims).astype(o_ref.dtype)
            return
        part = _dot(a_ref[...], b_ref[...], dims)
        if nk == 1:
            o_ref[...] = part.astype(o_ref.dtype)
            return
        acc_ref, = scratch
        k = pl.program_id(2)

        @pl.when(k == 0)
        def _():
            acc_ref[...] = part

        @pl.when(k > 0)
        def _():
            acc_ref[...] += part

        @pl.when(k == nk - 1)
        def _():
            o_ref[...] = acc_ref[...].astype(o_ref.dtype)

    (out,), rode = _call(
        body, rider, name=name, grid=grid, in_specs=[a_spec, b_spec], out_specs=[o_spec],
        out_shape=[jax.ShapeDtypeStruct(o_shape, o_dtype)],
        scratch_shapes=[] if nk == 1 else [pltpu.VMEM(acc_shape, F32)],
        compiler_params=_params(("parallel", "parallel", "arbitrary") if rider is None else ("arbitrary",) * 3, vmem_mb),
        operands=(a, b))
    return out if rider is None else (out, rode)


def _mm_nn(a, w, *, tm, tn, tk, name, rider=None, cols=None, o_dtype=F32, pad_rows=0):
    T, K = a.shape
    Q, _, Ns = w.shape
    nbs = Ns // tn
    tm = min(tm, T)
    j0, j1 = cols or (0, Q * nbs)
    lead = pad_rows // tm
    return _mm(a, w, grid=(lead + T // tm, j1 - j0, K // tk),
               a_spec=pl.BlockSpec((tm, tk), lambda i, j, k: (jnp.maximum(i - lead, 0), k)),
               b_spec=pl.BlockSpec((None, tk, tn), lambda i, j, k: ((j + j0) // nbs, k, (j + j0) % nbs)),
               o_spec=pl.BlockSpec((tm, tn), lambda i, j, k: (i, j)),
               o_shape=(pad_rows + T, (j1 - j0) * tn), o_dtype=o_dtype, dims=NN, acc_shape=(tm, tn), name=name,
               rider=rider, zero_rows=lead)


def _col_blocks(g, rows, tn, at):
    if g.ndim == 2:
        return pl.BlockSpec((rows, tn), at)
    per = g.shape[2] // tn

    def stacked(*idx):
        r, c = at(*idx)
        return c // per, r, c % per

    return pl.BlockSpec((None, rows, tn), stacked)


def _mm_nt(g, w, *, tm, to, tn, name, rider=None):
    T = g.shape[-2]
    Q, K, Ns = w.shape
    nbs = Ns // tn
    tm = min(tm, T)
    return _mm(g, w, grid=(T // tm, K // to, Q * nbs),
               a_spec=_col_blocks(g, tm, tn, lambda i, j, n: (i, n)),
               b_spec=pl.BlockSpec((None, to, tn), lambda i, j, n: (n // nbs, j, n % nbs)),
               o_spec=pl.BlockSpec((tm, to), lambda i, j, n: (i, j)),
               o_shape=(T, K), o_dtype=F32, dims=NT, acc_shape=(tm, to), name=name, rider=rider)


def _mm_tn(a, g, *, q, tk, tn, tt, name, rider=None):
    T, K = a.shape
    Ns = g.shape[-1] * (g.ndim - 1) // q
    nbs = Ns // tn
    return _mm(a, g, grid=(K // tk, q * nbs, T // tt),
               a_spec=pl.BlockSpec((tt, tk), lambda i, j, t: (t, i)),
               b_spec=_col_blocks(g, tt, tn, lambda i, j, t: (t, j)),
               o_spec=pl.BlockSpec((None, tk, tn), lambda i, j, t: (j // nbs, i, j % nbs)),
               o_shape=(q, K, Ns), o_dtype=BF16, dims=TN, acc_shape=(tk, tn), name=name, rider=rider)


def _mm_tn_add(a, g, part, *, tk, tn, name, rider=None):
    T, K = a.shape
    Q, _, Ns = part.shape
    nbs = Ns // tn

    def body(a_ref, g_ref, p_ref, o_ref):
        o_ref[...] = (_dot(a_ref[...], g_ref[...], TN) + p_ref[...].astype(F32)).astype(o_ref.dtype)

    blk = pl.BlockSpec((None, tk, tn), lambda i, j: (j // nbs, i, j % nbs))
    (out,), rode = _call(
        body, rider, name=name, grid=(K // tk, Q * nbs),
        in_specs=[pl.BlockSpec((T, tk), lambda i, j: (0, i)), _col_blocks(g, T, tn, lambda i, j: (0, j)), blk],
        out_specs=[blk], out_shape=[jax.ShapeDtypeStruct((Q, K, Ns), BF16)], scratch_shapes=[],
        compiler_params=_params(("arbitrary", "arbitrary"), 48), operands=(a, g, part))
    return out, rode


def _ln(u):
    mu = jnp.mean(u, axis=-1, keepdims=True)
    d = u - mu
    r = lax.rsqrt(jnp.mean(d * d, axis=-1, keepdims=True) + EPS)
    return d * r, r


def _ln_bwd(dy, un, r):
    return r * (dy - jnp.mean(dy, axis=-1, keepdims=True) - un * jnp.mean(dy * un, axis=-1, keepdims=True))


def _colsum(v):
    return jnp.sum(v, axis=0, keepdims=True)


def _rowwise(name, fn, bigs, vecs, out_dtypes, n_acc, tm=128, rider=None):
    T, D = bigs[0].shape
    nb, nv, no = len(bigs), len(vecs), len(out_dtypes)

    def body(*refs):
        outs, accs = fn([r[...] for r in refs[:nb]], [r[...] for r in refs[nb:nb + nv]])
        for r, o in zip(refs[nb + nv:nb + nv + no], outs):
            r[...] = o.astype(r.dtype)
        if n_acc:
            acc_ref = refs[nb + nv + no]

            @pl.when(pl.program_id(0) == 0)
            def _():
                acc_ref[...] = jnp.zeros_like(acc_ref)

            for row, a in enumerate(accs):
                acc_ref[row:row + 1, :] += a

    big_spec = pl.BlockSpec((tm, D), lambda i: (i, 0))
    vec_spec = pl.BlockSpec((1, D), lambda i: (0, 0))
    out_shape = [jax.ShapeDtypeStruct((T, D), dt) for dt in out_dtypes]
    out_specs = [big_spec] * no
    if n_acc:
        out_shape.append(jax.ShapeDtypeStruct((8, D), F32))
        out_specs.append(pl.BlockSpec((8, D), lambda i: (0, 0)))
    outs, rode = _call(
        body, rider, name=name, grid=(T // tm,), in_specs=[big_spec] * nb + [vec_spec] * nv,
        out_specs=out_specs, out_shape=out_shape, scratch_shapes=[],
        compiler_params=_params(("arbitrary",), 48), operands=(*bigs, *vecs))
    return outs if rider is None else (outs, rode)


def _to_bf16(w, name, rider=None):
    R, C = w.shape
    tr = _row_tile(R, C)

    def body(w_ref, o_ref):
        o_ref[...] = w_ref[...].astype(o_ref.dtype)

    blk = pl.BlockSpec((tr, C), lambda i: (i, 0))
    return _call(body, rider, name=name, grid=(R // tr,), in_specs=[blk], out_specs=[blk],
                 out_shape=[jax.ShapeDtypeStruct((R, C), BF16)], scratch_shapes=[],
                 compiler_params=_params(("arbitrary",)), operands=(w,))


def _pre_mixer(x, scale1, shift1, rider=None):
    def fn(b, v):
        xn, _ = _ln(b[0])
        return [xn * (1.0 + v[0]) + v[1]], []
    outs = _rowwise("pre_mixer", fn, [x], [scale1, shift1], [BF16], 0, rider=rider)
    return outs[0] if rider is None else (outs[0][0], outs[1])


def _post_mixer(mix, x, gate1, g1, b1, scale2, shift2, rider=None):
    def fn(b, v):
        un1, _ = _ln(ALPHA * b[1] + v[0] * b[0])
        x1 = un1 * v[1] + v[2]
        xn1, _ = _ln(x1)
        return [x1, xn1 * (1.0 + v[3]) + v[4]], []
    return _rowwise("post_mixer", fn, [mix, x], [gate1, g1, b1, scale2, shift2], [F32, BF16], 0, rider=rider)


def _loss_head(f, x1, tgt, gate2, g2, b2):
    def fn(b, v):
        ff, xx, tt = b
        d_model = ff.shape[-1]
        un2, r2 = _ln(ALPHA * xx + v[0] * ff)
        err = un2 * v[1] + v[2] - tt
        dy = err * (1.0 / d_model)
        du2 = _ln_bwd(dy * v[1], un2, r2)
        return [du2, du2 * v[0]], [_colsum(dy * un2), _colsum(dy), _colsum(du2 * ff), _colsum(err * err)]
    return _rowwise("loss_head", fn, [f, x1, tgt], [gate2, g2, b2], [F32, BF16], 4)


def _mid_bwd(dh2, du2, x1, mix, x, gate1, g1, scale2, rider=None):
    def fn(b, v):
        dh, du, xx1, mm, xx = b
        xn1, r1n = _ln(xx1)
        dx1 = ALPHA * du + _ln_bwd(dh * (1.0 + v[2]), xn1, r1n)
        un1, r1 = _ln(ALPHA * xx + v[0] * mm)
        du1 = _ln_bwd(dx1 * v[1], un1, r1)
        return [du1, du1 * v[0]], [_colsum(dh * xn1), _colsum(dh), _colsum(dx1 * un1), _colsum(dx1),
                                   _colsum(du1 * mm)]
    return _rowwise("mid_bwd", fn, [dh2, du2, x1, mix, x], [gate1, g1, scale2], [F32, BF16], 5, rider=rider)


def _first_bwd(dh1, du1, x, scale1, rider=None):
    def fn(b, v):
        dh, du, xx = b
        xn, r0 = _ln(xx)
        return [ALPHA * du + _ln_bwd(dh * (1.0 + v[0]), xn, r0)], [_colsum(dh * xn), _colsum(dh)]
    return _rowwise("first_bwd", fn, [dh1, du1, x], [scale1], [F32], 2, rider=rider)


def _ffn_in_swiglu(h2, w, *, tm, tn, rider=None):
    T, K = h2.shape
    Q, _, Ns = w.shape
    nbs = Ns // tn
    half = Q * nbs // 2
    tm = min(tm, T)

    def body(a_ref, wg_ref, wu_ref, g_ref, u_ref, act_ref):
        a = a_ref[...]
        g, u = _dot(a, wg_ref[...]), _dot(a, wu_ref[...])
        g_ref[...] = g.astype(g_ref.dtype)
        u_ref[...] = u.astype(u_ref.dtype)
        act_ref[...] = (g * jax.nn.sigmoid(g) * u).astype(act_ref.dtype)

    cols = lambda first: pl.BlockSpec((None, K, tn), lambda i, j: ((j + first) // nbs, 0, (j + first) % nbs))
    blk = pl.BlockSpec((tm, tn), lambda i, j: (i, j))
    return _call(
        body, rider, name="ffn_in", grid=(T // tm, half),
        in_specs=[pl.BlockSpec((tm, K), lambda i, j: (i, 0)), cols(0), cols(half)], out_specs=[blk] * 3,
        out_shape=[jax.ShapeDtypeStruct((T, half * tn), BF16)] * 3, scratch_shapes=[],
        compiler_params=_params(("arbitrary", "arbitrary"), 48), operands=(h2, w, w))


def _d_act_swiglu(df, w, gate, up, *, tm, to, rider=None):
    T, N = df.shape
    F = w.shape[1]
    tm = min(tm, T)

    def body(df_ref, w_ref, g_ref, u_ref, o_ref):
        d = _dot(df_ref[...], w_ref[...], NT)
        g = g_ref[...].astype(F32)
        s = jax.nn.sigmoid(g)
        o_ref[0] = (d * u_ref[...].astype(F32) * s * (1.0 + g * (1.0 - s))).astype(o_ref.dtype)
        o_ref[1] = (d * g * s).astype(o_ref.dtype)

    blk = pl.BlockSpec((tm, to), lambda i, j: (i, j))
    return _call(
        body, rider, name="d_act", grid=(T // tm, F // to),
        in_specs=[pl.BlockSpec((tm, N), lambda i, j: (i, 0)), pl.BlockSpec((None, to, N), lambda i, j: (0, j, 0)), blk, blk],
        out_specs=[pl.BlockSpec((2, tm, to), lambda i, j: (0, i, j))],
        out_shape=[jax.ShapeDtypeStruct((2, T, F), BF16)], scratch_shapes=[],
        compiler_params=_params(("arbitrary", "arbitrary"), 48), operands=(df, w, gate, up))


PAIR = 2


def _fill_table(table_ref, band_ref):
    table_ref[...] = jnp.full(table_ref.shape, NEG, F32)
    for e in range(PAIR):
        for g in range(QG):
            table_ref[e, g * CHUNK:(g + 1) * CHUNK, g * CHUNK:g * CHUNK + BAND] = band_ref[e]


def _attn_probs(q_ref, k_ref, bias_ref, e, step):
    start = pl.multiple_of(step * QROWS, QROWS)
    lanes = pl.ds(e * HD_A, HD_A)
    s = _dot(q_ref[:, lanes], k_ref[pl.ds(start + ZPAD - KPAD, UNION), lanes], NT) * (HD_A ** -0.5) + bias_ref[e]
    col = lax.broadcasted_iota(jnp.int32, s.shape, 1)
    s = jnp.where(col + start >= KPAD, s, NEG)
    p = jnp.exp(s - jnp.max(s, axis=-1, keepdims=True))
    return p / jnp.sum(p, axis=-1, keepdims=True), start


def _attn_specs(T, n_pairs):
    wide = PAIR * HD_A
    per_step = pl.BlockSpec((QROWS, wide), lambda hp, n: (n, hp))
    queries = pl.BlockSpec((QROWS, wide), lambda hp, n: (n + ZPAD // QROWS, hp))
    keys = pl.BlockSpec((ZPAD + T, wide), lambda hp, n: (0, n_pairs + hp))
    values = pl.BlockSpec((ZPAD + T, wide), lambda hp, n: (0, 2 * n_pairs + hp))
    grads = pl.BlockSpec((T, wide), lambda hp, n: (0, hp))
    table = pl.BlockSpec((PAIR, CHUNK, BAND), lambda hp, n: (hp, 0, 0))
    vec = pl.BlockSpec((1, wide), lambda hp, n: (0, hp))
    return per_step, queries, keys, values, grads, table, vec


def _probs_spec():
    return pl.BlockSpec((PAIR, QROWS, UNION), lambda hp, n: (hp, n, 0))


def _attn_fwd(qkv, bias, gain, rider=None):
    T = qkv.shape[0] - ZPAD
    W = gain.shape[1]
    n_pairs = W // (PAIR * HD_A)

    def body(q_ref, k_ref, v_ref, band_ref, gain_ref, o_ref, p_ref, table_ref):
        @pl.when(pl.program_id(1) == 0)
        def _():
            _fill_table(table_ref, band_ref)

        for e in range(PAIR):
            lanes = pl.ds(e * HD_A, HD_A)
            p, start = _attn_probs(q_ref, k_ref, table_ref, e, pl.program_id(1))
            p_ref[e] = p.astype(p_ref.dtype)
            o = _dot(p_ref[e], v_ref[pl.ds(start + ZPAD - KPAD, UNION), lanes])
            rr = lax.rsqrt(jnp.mean(o * o, axis=-1, keepdims=True) + EPS)
            o_ref[:, lanes] = (o * rr * gain_ref[:, lanes]).astype(o_ref.dtype)

    per_step, queries, keys, values, _, table, vec = _attn_specs(T, n_pairs)
    return _call(
        body, rider, name="attn_fwd", grid=(n_pairs, T // QROWS), in_specs=[queries, keys, values, table, vec],
        out_specs=[per_step, _probs_spec()],
        out_shape=[jax.ShapeDtypeStruct((T, W), BF16), jax.ShapeDtypeStruct((n_pairs * PAIR, T, UNION), BF16)],
        scratch_shapes=[pltpu.VMEM((PAIR, QROWS, UNION), F32)],
        compiler_params=_params(("arbitrary", "arbitrary"), 40), operands=(qkv, qkv, qkv, bias, gain))


def _attn_bwd(qkv, probs, gain, dmixin, rider=None):
    T = qkv.shape[0] - ZPAD
    W = gain.shape[1]
    n_pairs = W // (PAIR * HD_A)
    scale = HD_A ** -0.5

    def body(q_ref, k_ref, v_ref, p_ref, gain_ref, don_ref, dq_ref, dkb_ref, dvb_ref, dband_ref, dgain_ref,
             dtable_ref, dk_ref, dv_ref):
        n = pl.program_id(1)

        @pl.when(n == 0)
        def _():
            dk_ref[...] = jnp.zeros_like(dk_ref)
            dv_ref[...] = jnp.zeros_like(dv_ref)
            dtable_ref[...] = jnp.zeros_like(dtable_ref)
            dgain_ref[...] = jnp.zeros_like(dgain_ref)

        for e in range(PAIR):
            lanes = pl.ds(e * HD_A, HD_A)
            start = pl.multiple_of(n * QROWS, QROWS)
            keys, in_qkv = pl.ds(start, UNION), pl.ds(start + ZPAD - KPAD, UNION)
            pb = p_ref[e]
            p = pb.astype(F32)
            vb = v_ref[in_qkv, lanes]
            o = _dot(pb, vb)
            rr = lax.rsqrt(jnp.mean(o * o, axis=-1, keepdims=True) + EPS)
            on = o * rr
            d_on = don_ref[:, lanes]
            dgain_ref[:, lanes] += _colsum(d_on * on)
            dyo = d_on * gain_ref[:, lanes]
            do = rr * (dyo - on * jnp.mean(dyo * on, axis=-1, keepdims=True))
            dob = do.astype(BF16)
            dp = _dot(dob, vb, NT)
            ds = p * (dp - jnp.sum(do * o, axis=-1, keepdims=True))
            dtable_ref[e] += ds
            dsb = ds.astype(BF16)
            dq_ref[:, lanes] = (_dot(dsb, k_ref[in_qkv, lanes]) * scale).astype(dq_ref.dtype)
            dk_ref[keys, lanes] += _dot(dsb, q_ref[:, lanes], TN) * scale
            dv_ref[keys, lanes] += _dot(pb, dob, TN)

        @pl.when(n == T // QROWS - 1)
        def _():
            for e in range(PAIR):
                dband_ref[e] = sum(dtable_ref[e, g * CHUNK:(g + 1) * CHUNK, g * CHUNK:g * CHUNK + BAND]
                                   for g in range(QG))
            dkb_ref[...] = dk_ref[KPAD:, :].astype(dkb_ref.dtype)
            dvb_ref[...] = dv_ref[KPAD:, :].astype(dvb_ref.dtype)

    per_step, queries, keys, values, grads, table, vec = _attn_specs(T, n_pairs)
    H = n_pairs * PAIR
    return _call(
        body, rider, name="attn_bwd", grid=(n_pairs, T // QROWS),
        in_specs=[queries, keys, values, _probs_spec(), vec, per_step],
        out_specs=[per_step, grads, grads, table, vec],
        out_shape=[jax.ShapeDtypeStruct((T, W), BF16)] * 3 + [jax.ShapeDtypeStruct((H, CHUNK, BAND), F32),
                                                              jax.ShapeDtypeStruct((1, W), F32)],
        scratch_shapes=[pltpu.VMEM((PAIR, QROWS, UNION), F32)] + [pltpu.VMEM((KPAD + T, PAIR * HD_A), F32)] * 2,
        compiler_params=_params(("arbitrary", "arbitrary"), 40),
        operands=(qkv, qkv, qkv, probs, gain, dmixin))


N_DIAG = CHUNK + BAND - 1


def _bias_band(rel_bias):
    H = rel_bias.shape[0]
    idx = np.clip(BAND - 1 - np.arange(N_DIAG), -MAX_REL, MAX_REL) + MAX_REL
    rolled = rel_bias[:, idx[(np.arange(N_DIAG) + CHUNK - 1) % N_DIAG]]
    flat = jnp.broadcast_to(rolled[:, None, :], (H, CHUNK, N_DIAG)).reshape(H, CHUNK * N_DIAG)
    return flat[:, :CHUNK * (N_DIAG - 1)].reshape(H, CHUNK, N_DIAG - 1)[:, :, :BAND]


def _bias_band_grad(dband):
    H = dband.shape[0]
    skew = jnp.pad(dband, ((0, 0), (0, 0), (CHUNK - 1, 0))).reshape(H, CHUNK * N_DIAG)
    skew = jnp.pad(skew, ((0, 0), (0, CHUNK))).reshape(H, CHUNK, N_DIAG + 1)
    diag = jnp.sum(skew, axis=1)[:, :N_DIAG]
    n_far = BAND - MAX_REL
    far = jnp.sum(diag[:, :n_far], axis=1, keepdims=True)
    near = diag[:, n_far:][:, ::-1]
    zeros = jnp.zeros((H, MAX_REL - (CHUNK - 1)), F32)
    return jnp.concatenate([zeros, near, far], axis=1)


def _tri(n, lower):
    r = lax.broadcasted_iota(jnp.int32, (n, n), 0)
    c = lax.broadcasted_iota(jnp.int32, (n, n), 1)
    return jnp.where((c <= r) if lower else (c >= r), 1.0, 0.0).astype(F32)


def _hgrn_gates(zq_ref, zf_ref, lbl_ref, q_s, k_s, b_s):
    lb = jax.nn.sigmoid(lbl_ref[0:1, :] - lbl_ref[1:2, :])
    zq = zq_ref[...]
    sig = jax.nn.sigmoid(zf_ref[...])
    f = lb + (1.0 - lb) * sig
    sq = jax.nn.sigmoid(zq)
    q_s[...] = zq * sq
    k_s[...] = 1.0 - f
    b_s[...] = _dot(_tri(CHUNK, True), jnp.log(f), precision=HIGHEST)
    return lb, sig, f, sq


def _sub_rows(i):
    return pl.ds(i * SUB, SUB)


def _row_mask(s):
    return lax.broadcasted_iota(jnp.int32, (SUB, HD_B), 0) >= s


def _decay_from(b_sub, b_row, s):
    return jnp.where(_row_mask(s), jnp.exp(jnp.minimum(b_sub - b_row, 0.0)), 0.0)


def _hgrn_fwd(proj, lb_logits, gnorm_g, rider=None):
    T = proj.shape[0]
    nC = T // CHUNK
    W = lb_logits.shape[1]
    G = W // HD_B // HGRN_HEADS
    col0 = (proj.shape[1] - 4 * W) // (HD_B * HGRN_HEADS)
    wide = HGRN_HEADS * HD_B

    def body(*refs):
        @pl.when(pl.program_id(1) == 0)
        def _():
            refs[9][...] = jnp.zeros_like(refs[9])

        for h in range(HGRN_HEADS):
            lanes = pl.ds(h * HD_B, HD_B)
            one_head(*[r.at[:, lanes] for r in refs[:5]], refs[5], *[r.at[:, lanes] for r in refs[6:8]],
                     *[r.at[h] for r in refs[8:]])

    def one_head(zq_ref, zf_ref, xi_ref, zg_ref, lbl_ref, gn_ref, mix_ref, o_ref, stall_ref, st_ref, q_s, k_s, b_s, acc_s):
        _hgrn_gates(zq_ref, zf_ref, lbl_ref, q_s, k_s, b_s)
        q, k, b = q_s[...], k_s[...], b_s[...]
        st = st_ref[...]
        stall_ref[...] = st
        b_last = b_s[CHUNK - 1:CHUNK, :]
        acc_s[...] = _dot((q * jnp.exp(b)).astype(BF16), st.astype(BF16), NT)
        for i in range(CHUNK // SUB):
            rows = _sub_rows(i)
            q_i, b_i = q_s[rows, :], b_s[rows, :]
            acc = jnp.zeros((SUB, HD_B), F32)
            if i:
                past = pl.ds(0, i * SUB)
                b_ref = b_s[i * SUB - 1:i * SUB, :]
                qs = (q_i * jnp.exp(b_i - b_ref)).astype(BF16)
                ks = (k_s[past, :] * jnp.exp(b_ref - b_s[past, :])).astype(BF16)
                acc += _dot(_dot(qs, ks, NT).astype(BF16), xi_ref[past, :].astype(BF16))
            for s in range(SUB):
                row = pl.ds(i * SUB + s, 1)
                w = q_i * _decay_from(b_i, b_s[row, :], s)
                acc += jnp.sum(w * k_s[row, :], axis=-1, keepdims=True) * xi_ref[row, :]
            acc_s[rows, :] += acc
        o = acc_s[...]
        kd = (k * jnp.exp(b_last - b)).astype(BF16)
        st_ref[...] = st * jnp.exp(b_last) + _dot(xi_ref[...].astype(BF16), kd, TN)
        o_ref[...] = o
        zg = zg_ref[...]
        rr = lax.rsqrt(jnp.mean(o * o, axis=-1, keepdims=True) + EPS)
        mix_ref[...] = (o * rr * gn_ref[...] * (zg * jax.nn.sigmoid(zg))).astype(mix_ref.dtype)

    col = lambda part: pl.BlockSpec((CHUNK, wide), lambda g, n: (n, col0 + part * G + g))
    out_blk = pl.BlockSpec((CHUNK, wide), lambda g, n: (n, g))
    tile = pltpu.VMEM((HGRN_HEADS, CHUNK, HD_B), F32)
    return _call(
        body, rider, name="hgrn_fwd", grid=(G, nC),
        in_specs=[col(0), col(1), col(2), col(3), pl.BlockSpec((2, wide), lambda g, n: (0, g)),
                  pl.BlockSpec((1, HD_B), lambda g, n: (0, 0))],
        out_specs=[out_blk, out_blk, pl.BlockSpec((HGRN_HEADS, None, HD_B, HD_B), lambda g, n: (g, n, 0, 0))],
        out_shape=[jax.ShapeDtypeStruct((T, W), BF16), jax.ShapeDtypeStruct((T, W), F32),
                   jax.ShapeDtypeStruct((G * HGRN_HEADS, nC, HD_B, HD_B), F32)],
        scratch_shapes=[pltpu.VMEM((HGRN_HEADS, HD_B, HD_B), F32), tile, tile, tile, tile],
        compiler_params=_params(("arbitrary", "arbitrary")),
        operands=(proj, proj, proj, proj, lb_logits, gnorm_g))


def _hgrn_bwd(proj, lb_logits, gnorm_g, o_b, st_all, dmixin, d_attn, rider=None):
    T = proj.shape[0]
    nC = T // CHUNK
    W = lb_logits.shape[1]
    wa = d_attn[0].shape[1]
    assert W == HGRN_HEADS * HD_B, "one grid step takes every head: it writes whole rows of d proj"
    G = W // HD_B // HGRN_HEADS
    wide = HGRN_HEADS * HD_B
    col0 = (proj.shape[1] - 4 * W) // wide
    dcol0 = (dmixin.shape[1] - W) // wide

    def body(*refs):
        g, n = pl.program_id(0), pl.program_id(1)
        dproj_ref, dl0_ref, dgn_ref, dst_ref = refs[12:16]
        for i in range(3):
            dproj_ref[:, i * wa:(i + 1) * wa] = refs[9 + i][...]

        @pl.when(n == 0)
        def _():
            dst_ref[...] = jnp.zeros_like(dst_ref)
            dl0_ref[...] = jnp.zeros_like(dl0_ref)

        @pl.when((n == 0) & (g == 0))
        def _():
            dgn_ref[...] = jnp.zeros_like(dgn_ref)

        for h in range(HGRN_HEADS):
            lanes = pl.ds(h * HD_B, HD_B)
            cut = lambda r: r.at[:, lanes]
            parts = [dproj_ref.at[:, pl.ds(3 * wa + part * W + h * HD_B, HD_B)] for part in range(4)]
            one_head(*[cut(r) for r in refs[:5]], refs[5], cut(refs[6]), refs[7].at[h], cut(refs[8]),
                     *parts, cut(dl0_ref), dgn_ref, *[r.at[h] for r in refs[15:]])

    def one_head(zq_ref, zf_ref, xi_ref, zg_ref, lbl_ref, gn_ref, o_ref, st_ref, dout_ref,
                 dzq_ref, dzf_ref, dxi_ref, dzg_ref, dl0_ref, dgn_ref, dst_ref, q_s, k_s, b_s, do_s, dq_s, dk_s, di_s):
        lb, sig, f, sq = _hgrn_gates(zq_ref, zf_ref, lbl_ref, q_s, k_s, b_s)
        q, k, b = q_s[...], k_s[...], b_s[...]
        zg, o, dout = zg_ref[...], o_ref[...], dout_ref[...]
        sg = jax.nn.sigmoid(zg)
        rr = lax.rsqrt(jnp.mean(o * o, axis=-1, keepdims=True) + EPS)
        on = o * rr
        gn = gn_ref[...]
        dzg_ref[...] = (dout * on * gn * sg * (1.0 + zg * (1.0 - sg))).astype(dzg_ref.dtype)
        d_on = dout * zg * sg
        dgn_ref[...] += _colsum(d_on * on)
        d_on = d_on * gn
        do = rr * (d_on - on * jnp.mean(d_on * on, axis=-1, keepdims=True))
        do_s[...] = do
        dob = do.astype(BF16)
        st, dst = st_ref[...], dst_ref[...]
        b_last = b_s[CHUNK - 1:CHUNK, :]
        eb, e_last, k_dec = jnp.exp(b), jnp.exp(b_last), jnp.exp(b_last - b)
        qt, kd = q * eb, k * k_dec
        dstb = dst.astype(BF16)
        xib = xi_ref[...].astype(BF16)
        d_kd = _dot(xib, dstb)
        dq_s[...] = _dot(dob, st.astype(BF16)) * eb
        dk_s[...] = d_kd * k_dec
        di_s[...] = _dot(kd.astype(BF16), dstb, NT)
        d_b_last = e_last * _colsum(st * dst) + _colsum(d_kd * kd)
        dst_ref[...] = _dot(dob, qt.astype(BF16), TN) + dst * e_last
        for i in range(CHUNK // SUB):
            rows = _sub_rows(i)
            q_i, b_i, do_i = q_s[rows, :], b_s[rows, :], do_s[rows, :]
            dq_i = jnp.zeros((SUB, HD_B), F32)
            if i:
                past = pl.ds(0, i * SUB)
                b_ref = b_s[i * SUB - 1:i * SUB, :]
                e_q, e_k = jnp.exp(b_i - b_ref), jnp.exp(b_ref - b_s[past, :])
                qs, ks = (q_i * e_q).astype(BF16), (k_s[past, :] * e_k).astype(BF16)
                xi_p, do_b = xi_ref[past, :].astype(BF16), do_i.astype(BF16)
                di_s[past, :] += _dot(_dot(ks, qs, NT).astype(BF16), do_b)
                dq_i += _dot(_dot(do_b, xi_p, NT).astype(BF16), ks) * e_q
                dk_s[past, :] += _dot(_dot(xi_p, do_b, NT).astype(BF16), qs) * e_k
            for s in range(SUB):
                row = pl.ds(i * SUB + s, 1)
                k_row, i_row = k_s[row, :], xi_ref[row, :]
                e = _decay_from(b_i, b_s[row, :], s)
                w = q_i * e
                a_col = jnp.sum(w * k_row, axis=-1, keepdims=True)
                da_col = jnp.sum(do_i * i_row, axis=-1, keepdims=True)
                di_s[row, :] += _colsum(a_col * do_i)
                dq_i += da_col * e * k_row
                dk_s[row, :] += _colsum(da_col * w)
            dq_s[rows, :] += dq_i
        dq, dk = dq_s[...], dk_s[...]
        db = q * dq - k * dk
        is_last = lax.broadcasted_iota(jnp.int32, (CHUNK, HD_B), 0) == CHUNK - 1
        db = db + jnp.where(is_last, d_b_last, 0.0)
        df = _dot(_tri(CHUNK, False), db, precision=HIGHEST) / f - dk
        dzf_ref[...] = (df * (1.0 - lb) * sig * (1.0 - sig)).astype(dzf_ref.dtype)
        dl0_ref[...] += _colsum(df * (1.0 - sig)) * (lb * (1.0 - lb))
        zq = zq_ref[...]
        dzq_ref[...] = (dq * sq * (1.0 + zq * (1.0 - sq))).astype(dzq_ref.dtype)
        dxi_ref[...] = di_s[...].astype(dxi_ref.dtype)

    rev = lambda n: nC - 1 - n
    col = lambda part: pl.BlockSpec((CHUNK, wide), lambda g, n: (rev(n), col0 + part * G + g))
    blk = pl.BlockSpec((CHUNK, wide), lambda g, n: (rev(n), g))
    tile = pltpu.VMEM((HGRN_HEADS, CHUNK, HD_B), F32)
    rows = lambda width: pl.BlockSpec((CHUNK, width), lambda g, n: (rev(n), 0))
    return _call(
        body, rider, name="hgrn_bwd", grid=(G, nC),
        in_specs=[col(0), col(1), col(2), col(3), pl.BlockSpec((2, wide), lambda g, n: (0, g)),
                  pl.BlockSpec((1, HD_B), lambda g, n: (0, 0)), blk,
                  pl.BlockSpec((HGRN_HEADS, None, HD_B, HD_B), lambda g, n: (g, rev(n), 0, 0)),
                  pl.BlockSpec((CHUNK, wide), lambda g, n: (rev(n), dcol0 + g)), rows(wa), rows(wa), rows(wa)],
        out_specs=[rows(3 * wa + 4 * W), pl.BlockSpec((1, wide), lambda g, n: (0, g)),
                   pl.BlockSpec((1, HD_B), lambda g, n: (0, 0))],
        out_shape=[jax.ShapeDtypeStruct((T, 3 * wa + 4 * W), BF16), jax.ShapeDtypeStruct((1, W), F32),
                   jax.ShapeDtypeStruct((1, HD_B), F32)],
        scratch_shapes=[pltpu.VMEM((HGRN_HEADS, HD_B, HD_B), F32)] + [tile] * 7,
        compiler_params=_params(("arbitrary", "arbitrary")),
        operands=(proj, proj, proj, proj, lb_logits, gnorm_g, o_b, st_all, dmixin, *d_attn))


def _adamw_math(g, w, m, v):
    m = B1 * m + (1.0 - B1) * g
    v = B2 * v + (1.0 - B2) * (g * g)
    m_hat = m / (1.0 - B1 ** STEP)
    v_hat = v / (1.0 - B2 ** STEP)
    return -LR * (m_hat / (jnp.sqrt(v_hat) + ADAM_EPS) + WD * w), m, v


def _adamw(g, w, m, v, name):
    R, C = g.shape
    tr = _row_tile(R, C)

    def body(g_ref, w_ref, m_ref, v_ref, go_ref, d_ref, mo_ref, vo_ref):
        g = g_ref[...]
        go_ref[...] = g
        d_ref[...], mo_ref[...], vo_ref[...] = _adamw_math(g, w_ref[...], m_ref[...], v_ref[...])

    blk = pl.BlockSpec((tr, C), lambda i: (i, 0))
    return pl.pallas_call(
        body, name=name, grid=(R // tr,), in_specs=[blk] * 4, out_specs=[blk] * 4,
        out_shape=[jax.ShapeDtypeStruct((R, C), F32)] * 4, compiler_params=_params(("parallel",), 40),
    )(g, w, m, v)


def _sum_pair(g_full, from_sibling, sel, name):
    Q, K, Ns = g_full.shape
    kh = K // 2
    tr = _row_tile(kh, Ns)
    nh = kh // tr

    def body(sel_ref, a_ref, b_ref, o_ref):
        o_ref[...] = (a_ref[...].astype(F32) + b_ref[...].astype(F32)).astype(o_ref.dtype)

    return pl.pallas_call(
        body, name=name,
        grid_spec=pltpu.PrefetchScalarGridSpec(
            num_scalar_prefetch=1, grid=(Q, nh),
            in_specs=[pl.BlockSpec((None, tr, Ns), lambda q, i, sel: (q, sel[1] * nh + i, 0)),
                      pl.BlockSpec((None, tr, Ns), lambda q, i, sel: (q, i, 0))],
            out_specs=pl.BlockSpec((None, tr, Ns), lambda q, i, sel: (q, i, 0))),
        out_shape=jax.ShapeDtypeStruct((Q, kh, Ns), BF16), compiler_params=_params(("parallel", "parallel")),
    )(sel, g_full, from_sibling)


def _sum_chips(pair_sum, from_chips, sel, name):
    Q, kh, Ns = pair_sum.shape
    tr = _row_tile(kh, Ns)
    nh = kh // tr

    def body(sel_ref, a_ref, b0_ref, b1_ref, b2_ref, o_ref):
        up = lambda r: r[...].astype(F32)
        o_ref[...] = ((up(a_ref) + up(b0_ref)) + up(b1_ref)) + up(b2_ref)

    recv = lambda k: pl.BlockSpec((None, tr, Ns), lambda i, sel: (k, i, 0))
    return pl.pallas_call(
        body, name=name,
        grid_spec=pltpu.PrefetchScalarGridSpec(
            num_scalar_prefetch=1, grid=(nh,),
            in_specs=[pl.BlockSpec((None, tr, Ns), lambda i, sel: (sel[0], i, 0)), recv(0), recv(1), recv(2)],
            out_specs=pl.BlockSpec((tr, Ns), lambda i, sel: (sel[1] * nh + i, 0))),
        out_shape=jax.ShapeDtypeStruct((2 * kh, Ns), F32), compiler_params=_params(("parallel",)),
    )(sel, pair_sum, from_chips, from_chips, from_chips)


def _gather_small(v, name):
    R, L = v.shape

    def body(v_ref, out_ref, send_sems, recv_sems):
        x, y, c = _place()
        me = 4 * x + 2 * y + c
        out_ref[me] = v_ref[...]
        peers = [(_flip(x, k >> 2 & 1), _flip(y, k >> 1 & 1), _flip(c, k & 1)) for k in range(1, N_DEV)]

        def copy(k, row, to):
            return pltpu.make_async_remote_copy(src_ref=v_ref, dst_ref=out_ref.at[row], send_sem=send_sems.at[k],
                                                recv_sem=recv_sems.at[k], device_id=to, device_id_type=MESH)

        sends = [copy(k, me, peer) for k, peer in enumerate(peers)]
        for cp in sends:
            cp.start()
        for k, (px, py, pc) in enumerate(peers):
            copy(k, 4 * px + 2 * py + pc, (x, y, c)).wait_recv()
        for cp in sends:
            cp.wait_send()

    vmem = pl.BlockSpec(memory_space=pltpu.VMEM)
    return pl.pallas_call(
        body, name=name, in_specs=[vmem], out_specs=vmem, out_shape=jax.ShapeDtypeStruct((N_DEV, R, L), F32),
        scratch_shapes=[pltpu.SemaphoreType.DMA((N_DEV - 1,)), pltpu.SemaphoreType.DMA((N_DEV - 1,))],
    )(v)


def _small_rider(v):
    def copies(ins, outs, send_sems, recv_sems):
        x, y, c = _place()
        peers = [(_flip(x, k >> 2 & 1), _flip(y, k >> 1 & 1), _flip(c, k & 1)) for k in range(1, N_DEV)]

        def copy(k, row, to):
            return pltpu.make_async_remote_copy(src_ref=ins[0], dst_ref=outs[0].at[row], send_sem=send_sems.at[k],
                                                recv_sem=recv_sems.at[k], device_id=to, device_id_type=MESH)

        sends = [copy(k, 4 * x + 2 * y + c, peer) for k, peer in enumerate(peers)]
        return sends, [copy(k, 4 * px + 2 * py + pc, (x, y, c)) for k, (px, py, pc) in enumerate(peers)]

    def start(*refs):
        for cp in copies(*refs)[0]:
            cp.start()

    def finish(*refs):
        sends, arrivals = copies(*refs)
        for cp in arrivals:
            cp.wait_recv()
        for cp in sends:
            cp.wait_send()

    return _Rider([v], [jax.ShapeDtypeStruct((N_DEV, *v.shape), F32)], N_DEV - 1, start, finish)


def _silu(v):
    return v * jax.nn.sigmoid(v)


def _ada_fwd(c_all, w_ada, rider, tn=512):
    M, D = c_all.shape
    Ns = w_ada.shape[1]
    steps = Ns // tn

    def body(c_ref, w_ref, o_ref, all_ref, send_sems, recv_sems):
        j = pl.program_id(0)
        o_ref[:, pl.ds(pl.multiple_of(j * tn, tn), tn)] = _dot(_silu(c_ref[...]).astype(BF16), w_ref[...].astype(BF16))

        @pl.when(j == steps - 1)
        def _():
            x, y, c = _place()
            peers = [(_flip(x, k >> 2 & 1), _flip(y, k >> 1 & 1), _flip(c, k & 1)) for k in range(1, N_DEV)]

            def copy(k, row, to):
                return pltpu.make_async_remote_copy(
                    src_ref=o_ref.at[pl.ds(0, N_DEV)], dst_ref=all_ref.at[row], send_sem=send_sems.at[k],
                    recv_sem=recv_sems.at[k], device_id=to, device_id_type=MESH)

            sends = [copy(k, 4 * x + 2 * y + c, peer) for k, peer in enumerate(peers)]
            for cp in sends:
                cp.start()
            for k, (px, py, pc) in enumerate(peers):
                copy(k, 4 * px + 2 * py + pc, (x, y, c)).wait_recv()
            for cp in sends:
                cp.wait_send()

    (out, out_all), rode = _call(
        body, rider, name="ada_fwd", grid=(steps,),
        in_specs=[pl.BlockSpec((M, D), lambda j: (0, 0)), pl.BlockSpec((D, tn), lambda j: (0, j))],
        out_specs=[pl.BlockSpec((M, Ns), lambda j: (0, 0)), ANY],
        out_shape=[jax.ShapeDtypeStruct((M, Ns), F32), jax.ShapeDtypeStruct((N_DEV, N_DEV, Ns), F32)],
        scratch_shapes=[pltpu.SemaphoreType.DMA((N_DEV - 1,)), pltpu.SemaphoreType.DMA((N_DEV - 1,))],
        compiler_params=_params(("arbitrary",)), operands=(c_all, w_ada))
    return out, out_all, rode


def _ada_bwd(c_all, dmod, w, m, v, tk=256, tn=1536):
    M, D = c_all.shape
    Ns = dmod.shape[1]

    def body(c_ref, d_ref, w_ref, m_ref, v_ref, g_ref, dl_ref, mo_ref, vo_ref):
        g = _dot(_silu(c_ref[...]).astype(BF16), d_ref[...].astype(BF16), TN)
        g_ref[...] = g
        dl_ref[...], mo_ref[...], vo_ref[...] = _adamw_math(g, w_ref[...], m_ref[...], v_ref[...])

    blk = pl.BlockSpec((tk, tn), lambda i, j: (i, j))
    return pl.pallas_call(
        body, name="ada_bwd", grid=(D // tk, Ns // tn),
        in_specs=[pl.BlockSpec((M, tk), lambda i, j: (0, i)), pl.BlockSpec((M, tn), lambda i, j: (0, j)), blk, blk, blk],
        out_specs=[blk] * 4, out_shape=[jax.ShapeDtypeStruct((D, Ns), F32)] * 4,
        compiler_params=_params(("parallel", "parallel"), 40),
    )(c_all, dmod, w, m, v)


def _small_update(g_all, w, m, v):
    R, L = w.shape

    def body(g_ref, w_ref, m_ref, v_ref, go_ref, d_ref, mo_ref, vo_ref):
        g = g_ref[0]
        for d in range(1, N_DEV):
            g = g + g_ref[d]
        go_ref[...] = g
        d_ref[...], mo_ref[...], vo_ref[...] = _adamw_math(g, w_ref[...], m_ref[...], v_ref[...])

    return pl.pallas_call(body, name="small_update", out_shape=[jax.ShapeDtypeStruct((R, L), F32)] * 4)(g_all, w, m, v)


def _pack(parts, rows):
    flat = jnp.concatenate([p.reshape(-1) for p in parts])
    return jnp.pad(flat, (0, rows * 128 - flat.shape[0])).reshape(rows, 128)


def _unpack(packed, shapes):
    flat, out, at = packed.reshape(-1), [], 0
    for shp in shapes:
        size = 1
        for d in shp:
            size *= d
        out.append(flat[at:at + size].reshape(shp))
        at += size
    return out


def _layer(x, tgt, mod, wts, rel_bias, attn_norm_g, lb_logits, gnorm_g, ln1_g, ln1_b, ln2_g, ln2_b, place=None):
    T, D = x.shape
    aw = attn_norm_g.shape[1]
    shift1, scale1, gate1, shift2, scale2, gate2 = [mod[i:i + 1] for i in range(6)]

    def gather(n, rows=None, into=None, before=None, last=True):
        return None if place is None else _gather_rider(wts[n], rows, None if into is None else into[0], before, last)

    def gathered(n, rode):
        return wts[n] if place is None else lax.dynamic_update_index_in_dim(rode[0], wts[n], place[0], 0)

    def blocks(g):
        return g.reshape(N_CHIPS, -1, g.shape[2])

    def to_sibling(g):
        return None if place is None else _pair_rider(g)

    def pair_sum(n, g, rode=None):
        if place is None:
            return g
        rode = _alone(_pair_rider(g), n + "_send_pair") if rode is None else rode
        return _sum_pair(g, rode[0], place[1], n + "_sum_pair")

    def to_chips(p, rows=None, into=None):
        return None if place is None else _chips_rider(p, rows, None if into is None else into[0])

    def summed(n, p, rode):
        return p if place is None else _sum_chips(p, rode[0], place[1], n + "_sum_chips")

    def to_both(block):
        return None if place is None else _share_rider(block)

    def carrying(mm, *args, rider, **kw):
        return mm(*args, rider=rider, **kw) if rider is not None else (mm(*args, **kw), None)

    def to_sibling_acts(a, b):
        return None if place is None else _acts_rider(a, b)

    def pair_grad(name, a, b, tn, rider, arrived=None, late_rider=None):
        if place is None:
            return _mm_tn(a, b, q=N_CHIPS, tk=512, tn=tn, tt=T, name=name), None
        kh = a.shape[1] // 2
        mine = lax.dynamic_slice_in_dim(a, place[1][1] * kh, kh, axis=1)
        part, rode = carrying(_mm_tn, mine, b, q=N_CHIPS, tk=512, tn=tn, tt=T, name=name + "_own",
                              rider=_join(None if arrived else _acts_rider(a, b), rider))
        (a_sib, b_sib), rode = arrived or rode[:2], rode if arrived else rode[2:]
        out, late = _mm_tn_add(a_sib, b_sib, part, tk=512, tn=tn, name=name + "_sib", rider=late_rider)
        return out, (rode or []) + late

    if place is None:
        h1, rode = _pre_mixer(x, scale1, shift1), None
    else:
        h1, rode = _pre_mixer(x, scale1, shift1, wts["w_in_last_part"])
    w_in = gathered("w_in", rode)
    n_qkv = 3 * aw // 256
    kh_o, kh_f, kh_out = [wts[n].shape[-2] // 2 for n in ("w_o", "w_ffn_in", "w_ffn_out")]
    o_cut, f_cuts, out_cut = 3 * kh_o // 8, (7 * kh_f // 16, 7 * kh_f // 8), kh_out // 11
    qkv, rode = carrying(_mm_nn, h1, w_in, tm=ZPAD, tn=256, tk=D, name="proj_qkv", cols=(0, n_qkv), o_dtype=BF16,
                         pad_rows=ZPAD, rider=gather("w_o", (0, o_cut), last=False))
    proj, rode = carrying(_mm_nn, h1, w_in, tm=2048, tn=256, tk=D, name="proj_rec",
                          cols=(n_qkv, N_CHIPS * w_in.shape[2] // 256),
                          rider=gather("w_o", (o_cut, kh_o - o_cut), rode, before=(0, o_cut)))
    w_o3 = gathered("w_o", rode).reshape(1, D, D)
    bias = _bias_band(rel_bias)
    (mix_a, probs), rode = _attn_fwd(qkv, bias, attn_norm_g, rider=gather("w_ffn_in", (0, f_cuts[0]), last=False))
    (mix_b, o_b, st_all), rode = _hgrn_fwd(
        proj, lb_logits, gnorm_g,
        rider=gather("w_ffn_in", (f_cuts[0], f_cuts[1] - f_cuts[0]), rode, before=(0, f_cuts[0]), last=False))
    mixin = jnp.concatenate([mix_a, mix_b], axis=1)
    mix = _mm_nn(mixin, w_o3, tm=1024, tn=512, tk=D, name="mix_out")
    if place is None:
        x1, h2 = _post_mixer(mix, x, gate1, ln1_g, ln1_b, scale2, shift2)
    else:
        (x1, h2), rode = _post_mixer(mix, x, gate1, ln1_g, ln1_b, scale2, shift2, rider=_join(
            gather("w_ffn_in", (f_cuts[1], kh_f - f_cuts[1]), rode, before=(f_cuts[0], f_cuts[1] - f_cuts[0])),
            gather("w_ffn_out", (0, out_cut), last=False)))
    w_ffn_in = gathered("w_ffn_in", rode)
    (gate, up, act), rode = _ffn_in_swiglu(
        h2, w_ffn_in, tm=2048, tn=256,
        rider=gather("w_ffn_out", (out_cut, kh_out - out_cut), rode and rode[1:], before=(0, out_cut)))
    w_out3 = gathered("w_ffn_out", rode)
    w_out3 = w_out3.reshape(1, -1, w_out3.shape[2])
    d_ff = w_out3.shape[1]
    f = _mm_nn(act, w_out3, tm=1024, tn=512, tk=d_ff, name="ffn_out")
    du2, df, acc2 = _loss_head(f, x1, tgt, gate2, ln2_g, ln2_b)
    loss = (0.5 / D) * jnp.sum(acc2[3])
    g = blocks(_mm_tn(act, df, q=1, tk=512, tn=1024, tt=T, name="g_ffn_out"))
    (dff,), rode = _d_act_swiglu(df, w_out3, gate, up, tm=1024, to=512, rider=to_sibling(g))
    p_out = pair_sum("w_ffn_out", g, rode)
    cut = 25 * p_out.shape[1] // 44
    dh2, rode = carrying(_mm_nt, dff, w_ffn_in, tm=1024, to=1024, tn=w_ffn_in.shape[2], name="d_h2",
                         rider=_join(to_chips(p_out, (0, cut)), to_sibling_acts(h2, dff)))
    p_fin, rode = pair_grad("g_ffn_in", h2, dff, w_ffn_in.shape[2] // 2,
                            to_chips(p_out, (cut, p_out.shape[1] - cut), rode), arrived=rode and rode[1:])
    g_ffn_out = summed("w_ffn_out", p_out, rode)
    if place is None:
        du1, dmix, acc1 = _mid_bwd(dh2, du2, x1, mix, x, gate1, ln1_g, scale2)
    else:
        (du1, dmix, acc1), (g_ffn_out,) = _mid_bwd(dh2, du2, x1, mix, x, gate1, ln1_g, scale2, rider=to_both(g_ffn_out))
    g = blocks(_mm_tn(mixin, dmix, q=1, tk=512, tn=1024, tt=T, name="g_o"))
    dmixin, rode = carrying(_mm_nt, dmix, w_o3, tm=1024, to=512, tn=D, name="d_mixin", rider=to_sibling(g))
    p_o = pair_sum("w_o", g, rode)
    cut = p_fin.shape[1] // 2
    (dq, dk, dv, dbias, dgain), rode = _attn_bwd(qkv, probs, attn_norm_g, dmixin, rider=to_chips(p_fin, (0, cut)))
    (dproj, dl0, dgn), rode = _hgrn_bwd(
        proj, lb_logits, gnorm_g, o_b, st_all, dmixin, (dq, dk, dv),
        rider=_join(to_chips(p_fin, (cut, p_fin.shape[1] - cut), rode), to_chips(p_o)))
    g_ffn_in, g_o = summed("w_ffn_in", p_fin, rode[:1]), summed("w_o", p_o, rode[1:])
    p_in, rode = pair_grad("g_in", h1, dproj, w_in.shape[2] // 2, None,
                           late_rider=_join(to_both(g_ffn_in), to_both(g_o)))
    if place is not None:
        g_ffn_in, g_o = rode
    cut = 3 * p_in.shape[1] // 4
    dh1, rode = carrying(_mm_nt, dproj, w_in, tm=1024, to=1024, tn=w_in.shape[2], name="d_h1",
                         rider=to_chips(p_in, (0, cut)))
    if place is None:
        (grad_x, acc0), g_in = _first_bwd(dh1, du1, x, scale1), p_in
    else:
        (grad_x, acc0), rode = _first_bwd(dh1, du1, x, scale1, rider=to_chips(p_in, (cut, p_in.shape[1] - cut), rode))
        g_in, = _alone(to_both(summed("w_in", p_in, rode)), "w_in_share")
    dmod = jnp.concatenate([acc0[1:2], acc0[0:1], acc1[4:5], acc1[1:2], acc1[0:1], acc2[2:3]], axis=0)
    small = dict(rel_bias=_bias_band_grad(dbias), attn_norm_g=dgain,
                 lb_logits=jnp.concatenate([dl0, -dl0], axis=0), gnorm_g=dgn,
                 ln1_g=acc1[2:3], ln1_b=acc1[3:4], ln2_g=acc2[0:1], ln2_b=acc2[1:2])
    return loss, grad_x, dict(w_in=g_in, w_o=g_o, w_ffn_in=g_ffn_in, w_ffn_out=g_ffn_out), dmod, small


SMALL = ("rel_bias", "attn_norm_g", "lb_logits", "gnorm_g", "ln1_g", "ln1_b", "ln2_g", "ln2_b")
SMALL_ROWS = 256


def kernel(x, c, w_ada, b_ada, w_in, rel_bias, attn_norm_g, lb_logits, gnorm_g, w_o, ln1_g, ln1_b, w_ffn_in, w_ffn_out, ln2_g, ln2_b, loss_target, m_w_ada, m_b_ada, m_w_in, m_rel_bias, m_attn_norm_g, m_lb_logits, m_gnorm_g, m_w_o, m_ln1_g, m_ln1_b, m_w_ffn_in, m_w_ffn_out, m_ln2_g, m_ln2_b, v_w_ada, v_b_ada, v_w_in, v_rel_bias, v_attn_norm_g, v_lb_logits, v_gnorm_g, v_w_o, v_ln1_g, v_ln1_b, v_w_ffn_in, v_w_ffn_out, v_ln2_g, v_ln2_b):
    mx, my, mc = _place()
    me = 4 * mx + 2 * my + mc
    chip = 2 * mx + my
    sel = jnp.stack([chip, mc]).astype(jnp.int32)
    D = x.shape[2]
    ns_ada = w_ada.shape[2]

    big = dict(w_in=(w_in, m_w_in, v_w_in), w_o=(w_o, m_w_o, v_w_o), w_ffn_in=(w_ffn_in, m_w_ffn_in, v_w_ffn_in),
               w_ffn_out=(w_ffn_out, m_w_ffn_out, v_w_ffn_out))
    shards = dict(w_in=w_in[0].astype(BF16))
    kh = shards["w_in"].shape[0] // 2
    cuts = [part * kh // 32 for part in (0, 12, 19, 22, 26, 32)]
    spans = [(a, b - a) for a, b in zip(cuts, cuts[1:])]

    def w_in_part(i, rode):
        return _gather_rider(shards["w_in"], spans[i], rode and rode[0], spans[i - 1] if i else None, last=i == 4,
                             relay=True)

    c_own = c.reshape(D // 128, 128)
    rode, c_all = None, None
    for i, n in enumerate(("w_ffn_in", "w_ffn_out", "w_o")):
        rider = _join(w_in_part(i, rode), None if i else _small_rider(c_own))
        (shards[n],), rode = _to_bf16(big[n][0][0], "cast_" + n, rider)
        c_all = c_all if i else rode[1]

    c_all = lax.dynamic_update_index_in_dim(c_all, c_own, me, 0).reshape(N_DEV, D)
    c_all = jnp.pad(c_all, ((0, 16 - N_DEV), (0, 0)))
    mod_cols, mod_all, rode = _ada_fwd(c_all, w_ada[0], w_in_part(3, rode))
    shards["w_in_last_part"] = w_in_part(4, rode)
    mod_all = lax.dynamic_update_index_in_dim(mod_all, mod_cols[:N_DEV], me, 0)
    mod = lax.dynamic_index_in_dim(mod_all[::2], me, axis=1, keepdims=False)
    mod = (mod.reshape(1, -1) + b_ada).reshape(6, D)

    loss, grad_x, g_big, dmod, g_small = _layer(
        x[0], loss_target[0], mod, shards, rel_bias[0], attn_norm_g, lb_logits, gnorm_g, ln1_g, ln1_b, ln2_g, ln2_b,
        place=(chip, sel))

    grads, deltas, new_m, new_v = {}, {}, {}, {}
    for n, (w, m, v) in big.items():
        g, d, mo, vo = _adamw(g_big[n], w[0], m[0], v[0], "adamw_" + n)
        grads[n], deltas[n], new_m[n], new_v[n] = g[None], d[None], mo[None], vo[None]

    small_in = dict(rel_bias=(rel_bias, m_rel_bias, v_rel_bias), attn_norm_g=(attn_norm_g, m_attn_norm_g, v_attn_norm_g),
                    lb_logits=(lb_logits, m_lb_logits, v_lb_logits), gnorm_g=(gnorm_g, m_gnorm_g, v_gnorm_g),
                    ln1_g=(ln1_g, m_ln1_g, v_ln1_g), ln1_b=(ln1_b, m_ln1_b, v_ln1_b), ln2_g=(ln2_g, m_ln2_g, v_ln2_g),
                    ln2_b=(ln2_b, m_ln2_b, v_ln2_b))
    g_all = _gather_small(_pack([dmod] + [g_small[n] for n in SMALL] + [loss], SMALL_ROWS), "gather_small")
    packed = [_pack([t] + [small_in[n][i] for n in SMALL] + [jnp.zeros((), F32)], SMALL_ROWS)
              for i, t in enumerate((b_ada, m_b_ada, v_b_ada))]
    shapes = [b_ada.shape] + [small_in[n][0].shape for n in SMALL] + [()]
    outs = [_unpack(o, shapes) for o in _small_update(g_all, *packed)]
    loss = outs[0][-1]
    for i, n in enumerate(("b_ada",) + SMALL):
        grads[n], deltas[n], new_m[n], new_v[n] = outs[0][i], outs[1][i], outs[2][i], outs[3][i]

    dmod_all = g_all[:, :6 * D // 128].reshape(N_DEV, 6 * D)
    dmod_cols = lax.dynamic_slice_in_dim(dmod_all, chip * ns_ada, ns_ada, axis=1)
    dmod_cols = jnp.pad(dmod_cols, ((0, 16 - N_DEV), (0, 0)))
    g, d, mo, vo = _ada_bwd(c_all, dmod_cols, w_ada[0], m_w_ada[0], v_w_ada[0])
    grads["w_ada"], deltas["w_ada"], new_m["w_ada"], new_v["w_ada"] = g[None], d[None], mo[None], vo[None]

    order = ("w_ada", "b_ada", "w_in", "rel_bias", "attn_norm_g", "lb_logits", "gnorm_g", "w_o", "ln1_g", "ln1_b",
             "w_ffn_in", "w_ffn_out", "ln2_g", "ln2_b")
    return (loss, grad_x[None], *[grads[n] for n in order], *[deltas[n] for n in order],
            *[new_m[n] for n in order], *[new_v[n] for n in order])
```

```python
import numpy as np
import jax
import jax.numpy as jnp
from jax import lax
from jax.experimental import pallas as pl
from jax.experimental.pallas import tpu as pltpu

F32 = jnp.float32
BF16 = jnp.bfloat16
MESH = pl.DeviceIdType.MESH
HIGHEST = lax.Precision.HIGHEST

CHUNK = 64
N_PAST = 8
QG = 4
QROWS = QG * CHUNK
KPAD = N_PAST * CHUNK
ZPAD = 2 * KPAD
UNION = (QG + N_PAST) * CHUNK
BAND = (N_PAST + 1) * CHUNK
HD_A = 64
HD_B = 128
SUB = 16
HGRN_HEADS = 8
MAX_REL = 256
EPS = 1e-5
ALPHA = 2.0 ** 0.25
LR, B1, B2, ADAM_EPS, WD, STEP = 1e-3, 0.9, 0.999, 1e-8, 0.01, 10
N_CHIPS = 4
N_DEV = 8
NEG = -1e30
TILE_BYTES = 3 << 19

NN = ((1,), (0,))
NT = ((1,), (1,))
TN = ((0,), (0,))


def _dot(a, b, dims=NN, precision=None):
    return lax.dot_general(a, b, (dims, ((), ())), preferred_element_type=F32, precision=precision)


def _params(sem=None, vmem_mb=None, **kw):
    return pltpu.CompilerParams(dimension_semantics=sem,
                                vmem_limit_bytes=None if vmem_mb is None else vmem_mb << 20, **kw)


def _row_tile(rows, cols):
    for cand in (512, 256, 128, 64, 32, 16, 8):
        if rows % cand == 0 and cand * cols * 4 <= TILE_BYTES:
            return cand
    raise ValueError((rows, cols))


def _place():
    return lax.axis_index("x"), lax.axis_index("y"), lax.axis_index("c")


def _flip(v, bit):
    return 1 - v if bit else v


ANY = pl.BlockSpec(memory_space=pl.ANY)
CHIP_FLIPS = ((1, 0), (0, 1), (1, 1))


class _Rider:
    def __init__(self, operands, out_shape, n_sems, start, finish, aliases=None):
        self.operands, self.out_shape, self.n_sems, self.start, self.finish = operands, out_shape, n_sems, start, finish
        self.aliases = aliases or {}


def _call(body, rider, *, name, grid, in_specs, out_specs, out_shape, scratch_shapes, compiler_params, operands):
    if rider is None:
        outs = pl.pallas_call(body, name=name, grid=grid, in_specs=in_specs, out_specs=out_specs, out_shape=out_shape,
                              scratch_shapes=scratch_shapes, compiler_params=compiler_params)(*operands)
        return list(outs), []
    n_in, n_out, n_sc = len(in_specs), len(out_specs), len(scratch_shapes)
    r_in, r_out = len(rider.operands), len(rider.out_shape)

    def carried(*refs):
        refs = list(refs)
        cuts = [n_in, r_in, n_out, r_out, n_sc]
        ins, r_ins, outs, r_outs, scratch = [[refs.pop(0) for _ in range(n)] for n in cuts]
        first, last = None, None
        for axis, size in enumerate(grid):
            i = pl.program_id(axis)
            first = (i == 0) if first is None else first & (i == 0)
            last = (i == size - 1) if last is None else last & (i == size - 1)

        @pl.when(first)
        def _():
            rider.start(r_ins, r_outs, *refs)

        body(*ins, *outs, *scratch)

        @pl.when(last)
        def _():
            rider.finish(r_ins, r_outs, *refs)

    sems = [pltpu.SemaphoreType.DMA((rider.n_sems,)), pltpu.SemaphoreType.DMA((rider.n_sems,))]
    outs = pl.pallas_call(carried, name=name, grid=grid, in_specs=list(in_specs) + [ANY] * r_in,
                          out_specs=list(out_specs) + [ANY] * r_out, out_shape=list(out_shape) + rider.out_shape,
                          scratch_shapes=list(scratch_shapes) + sems, compiler_params=compiler_params,
                          input_output_aliases={n_in + i: n_out + o for i, o in rider.aliases.items()},
                          )(*operands, *rider.operands)
    return list(outs[:n_out]), list(outs[n_out:])


def _alone(rider, name):
    def body(*refs):
        ins, outs, sems = refs[:len(rider.operands)], refs[len(rider.operands):-2], refs[-2:]
        rider.start(ins, outs, *sems)
        rider.finish(ins, outs, *sems)

    return pl.pallas_call(
        body, name=name, in_specs=[ANY] * len(rider.operands), out_specs=[ANY] * len(rider.out_shape),
        out_shape=rider.out_shape, input_output_aliases=rider.aliases,
        scratch_shapes=[pltpu.SemaphoreType.DMA((rider.n_sems,)), pltpu.SemaphoreType.DMA((rider.n_sems,))],
    )(*rider.operands)


class _Sems:
    def __init__(self, sems, base):
        self.sems, self.base = sems, base

    @property
    def at(self):
        return self

    def __getitem__(self, k):
        return self.sems.at[self.base + k]


def _join(*riders):
    riders = [r for r in riders if r is not None]
    if len(riders) < 2:
        return riders[0] if riders else None

    def parts(ins, outs, send_sems, recv_sems):
        i = o = s = 0
        for r in riders:
            ni, no = len(r.operands), len(r.out_shape)
            yield r, ins[i:i + ni], outs[o:o + no], _Sems(send_sems, s), _Sems(recv_sems, s)
            i, o, s = i + ni, o + no, s + r.n_sems

    def start(*refs):
        for r, *args in parts(*refs):
            r.start(*args)

    def finish(*refs):
        for r, *args in parts(*refs):
            r.finish(*args)

    aliases, i, o = {}, 0, 0
    for r in riders:
        aliases.update({i + a: o + b for a, b in r.aliases.items()})
        i, o = i + len(r.operands), o + len(r.out_shape)
    return _Rider([a for r in riders for a in r.operands], [s for r in riders for s in r.out_shape],
                  sum(r.n_sems for r in riders), start, finish, aliases)


def _gather_rider(shard, rows=None, into=None, before=None, last=True, relay=False):
    K, Ns = shard.shape
    kh = K // 2
    rows = rows or (0, kh)

    def copies(w_ref, out_ref, send_sems, recv_sems):
        x, y, c = _place()
        chips = [(_flip(x, fx), _flip(y, fy)) for fx, fy in CHIP_FLIPS]

        def half(chip, which, part):
            return out_ref.at[2 * chip[0] + chip[1], pl.ds(which * kh + part[0], part[1]), :]

        def copy(k, dst, to, src=None):
            return pltpu.make_async_remote_copy(src_ref=dst if src is None else src, dst_ref=dst,
                                                send_sem=send_sems.at[k], recv_sem=recv_sems.at[k],
                                                device_id=to, device_id_type=MESH)

        def first():
            return [copy(j, half((x, y), c, rows), (*chip, c), src=w_ref.at[pl.ds(c * kh + rows[0], rows[1]), :])
                    for j, chip in enumerate(chips[:2] if relay else chips)]

        def onward(base, part):
            return [copy(base + j, half(chip, c, part), (x, y, 1 - c)) for j, chip in enumerate(chips)]

        def arriving(base, which, part):
            return [copy(base + j, half(chip, which, part), (x, y, c)) for j, chip in enumerate(chips)]

        def relayed(to_me, part, k):
            lo, hi = (part[0], part[1] // 2), (part[0] + part[1] // 2, part[1] - part[1] // 2)
            if to_me:
                return [copy(k, half(chips[2], c, lo), (x, y, c)), copy(k + 1, half(chips[2], c, hi), (x, y, c))]
            return [copy(k, half(chips[0], c, lo), (*chips[1], c)), copy(k + 1, half(chips[1], c, hi), (*chips[0], c))]

        def started():
            out = first()
            if before:
                out += onward(3, before)[:2 if relay else 3] + (relayed(False, before, 9) if relay else [])
            return out

        return c, started, onward, arriving, relayed

    def start(ins, outs, send_sems, recv_sems):
        for cp in copies(ins[0], outs[0], send_sems, recv_sems)[1]():
            cp.start()

    def finish(ins, outs, send_sems, recv_sems):
        c, started, onward, arriving, relayed = copies(ins[0], outs[0], send_sems, recv_sems)
        sent = started()
        direct = arriving(0, c, rows)
        passed = onward(6, rows) if last else [None] * 3
        if relay and before:
            for cp in relayed(True, before, 9):
                cp.wait_recv()
            sent.append(onward(3, before)[2])
            sent[-1].start()
        for j in range(2 if relay else 3):
            direct[j].wait_recv()
            if last:
                passed[j].start()
        if relay and last:
            sent += relayed(False, rows, 11)
            for cp in sent[-2:]:
                cp.start()
            for cp in relayed(True, rows, 11):
                cp.wait_recv()
            passed[2].start()
        for arrived in (arriving(3, 1 - c, before) if before else []) + (arriving(6, 1 - c, rows) if last else []):
            arrived.wait_recv()
        for cp in sent + (passed if last else []):
            cp.wait_send()

    full = jax.ShapeDtypeStruct((N_CHIPS, K, Ns), shard.dtype)
    if into is None:
        return _Rider([shard], [full], 13, start, finish)
    return _Rider([shard, into], [full], 13, start, finish, aliases={1: 0})


def _pair_rider(g_full):
    Q, K, Ns = g_full.shape
    kh = K // 2

    def copy(g_ref, got_ref, send_sems, recv_sems):
        x, y, c = _place()
        return pltpu.make_async_remote_copy(src_ref=g_ref.at[:, pl.ds((1 - c) * kh, kh), :], dst_ref=got_ref,
                                            send_sem=send_sems.at[0], recv_sem=recv_sems.at[0],
                                            device_id=(x, y, 1 - c), device_id_type=MESH)

    def start(ins, outs, send_sems, recv_sems):
        copy(ins[0], outs[0], send_sems, recv_sems).start()

    def finish(ins, outs, send_sems, recv_sems):
        copy(ins[0], outs[0], send_sems, recv_sems).wait()

    return _Rider([g_full], [jax.ShapeDtypeStruct((Q, kh, Ns), g_full.dtype)], 1, start, finish)


def _acts_rider(a, b):
    T, K = a.shape
    kh = K // 2

    def copies(ins, outs, send_sems, recv_sems):
        x, y, c = _place()
        pair = [(ins[0].at[:, pl.ds((1 - c) * kh, kh)], outs[0]), (ins[1], outs[1])]
        return [pltpu.make_async_remote_copy(src_ref=src, dst_ref=dst, send_sem=send_sems.at[k], recv_sem=recv_sems.at[k],
                                             device_id=(x, y, 1 - c), device_id_type=MESH)
                for k, (src, dst) in enumerate(pair)]

    def start(*refs):
        for cp in copies(*refs):
            cp.start()

    def finish(*refs):
        for cp in copies(*refs):
            cp.wait()

    return _Rider([a, b], [jax.ShapeDtypeStruct((T, kh), a.dtype), jax.ShapeDtypeStruct(b.shape, b.dtype)], 2,
                  start, finish)


def _share_rider(block):
    K, Ns = block.shape
    kh = K // 2

    def halves(out_ref):
        x, y, c = _place()
        return out_ref.at[pl.ds(c * kh, kh), :], out_ref.at[pl.ds((1 - c) * kh, kh), :], (x, y, 1 - c)

    def start(ins, outs, send_sems, recv_sems):
        mine, _, sibling = halves(outs[0])
        pltpu.make_async_remote_copy(src_ref=mine, dst_ref=mine, send_sem=send_sems.at[0], recv_sem=recv_sems.at[0],
                                     device_id=sibling, device_id_type=MESH).start()

    def finish(ins, outs, send_sems, recv_sems):
        mine, theirs, sibling = halves(outs[0])
        pltpu.make_async_remote_copy(src_ref=theirs, dst_ref=theirs, send_sem=send_sems.at[0], recv_sem=recv_sems.at[0],
                                     device_id=sibling, device_id_type=MESH).wait_recv()
        pltpu.make_async_remote_copy(src_ref=mine, dst_ref=mine, send_sem=send_sems.at[0], recv_sem=recv_sems.at[0],
                                     device_id=sibling, device_id_type=MESH).wait_send()

    return _Rider([block], [jax.ShapeDtypeStruct((K, Ns), block.dtype)], 1, start, finish, aliases={0: 0})


def _chips_rider(pair_sum, rows=None, into=None):
    Q, kh, Ns = pair_sum.shape
    first_row, n_rows = rows or (0, kh)

    def copies(p_ref, got_ref, send_sems, recv_sems):
        x, y, c = _place()
        part = pl.ds(first_row, n_rows)
        out = []
        for j, (fx, fy) in enumerate(CHIP_FLIPS):
            px, py = _flip(x, fx), _flip(y, fy)
            out.append(pltpu.make_async_remote_copy(
                src_ref=p_ref.at[2 * px + py, part, :], dst_ref=got_ref.at[j, part, :], send_sem=send_sems.at[j],
                recv_sem=recv_sems.at[j], device_id=(px, py, c), device_id_type=MESH))
        return out

    def start(ins, outs, send_sems, recv_sems):
        for cp in copies(ins[0], outs[0], send_sems, recv_sems):
            cp.start()

    def finish(ins, outs, send_sems, recv_sems):
        sends = copies(ins[0], outs[0], send_sems, recv_sems)
        for cp in sends:
            cp.wait_recv()
        for cp in sends:
            cp.wait_send()

    got = jax.ShapeDtypeStruct((Q - 1, kh, Ns), pair_sum.dtype)
    if into is None:
        return _Rider([pair_sum], [got], 3, start, finish)
    return _Rider([pair_sum, into], [got], 3, start, finish, aliases={1: 0})


def _mm(a, b, *, grid, a_spec, b_spec, o_spec, o_shape, o_dtype, dims, acc_shape, name, rider=None, zero_rows=0,
        vmem_mb=48):
    nk = grid[2]

    def body(a_ref, b_ref, o_ref, *scratch):
        if zero_rows:
            @pl.when(pl.program_id(0) < zero_rows)
            def _():
                o_ref[...] = jnp.zeros_like(o_ref)

            @pl.when(pl.program_id(0) >= zero_rows)
            def _():
                o_ref[...] = _dot(a_ref[...], b_ref[...], dims).astype(o_ref.dtype)
            return
        part = _dot(a_ref[...], b_ref[...], dims)
        if nk == 1:
            o_ref[...] = part.astype(o_ref.dtype)
            return
        acc_ref, = scratch
        k = pl.program_id(2)

        @pl.when(k == 0)
        def _():
            acc_ref[...] = part

        @pl.when(k > 0)
        def _():
            acc_ref[...] += part

        @pl.when(k == nk - 1)
        def _():
            o_ref[...] = acc_ref[...].astype(o_ref.dtype)

    (out,), rode = _call(
        body, rider, name=name, grid=grid, in_specs=[a_spec, b_spec], out_specs=[o_spec],
        out_shape=[jax.ShapeDtypeStruct(o_shape, o_dtype)],
        scratch_shapes=[] if nk == 1 else [pltpu.VMEM(acc_shape, F32)],
        compiler_params=_params(("parallel", "parallel", "arbitrary") if rider is None else ("arbitrary",) * 3, vmem_mb),
        operands=(a, b))
    return out if rider is None else (out, rode)


def _mm_nn(a, w, *, tm, tn, tk, name, rider=None, cols=None, o_dtype=F32, pad_rows=0):
    T, K = a.shape
    Q, _, Ns = w.shape
    nbs = Ns // tn
    tm = min(tm, T)
    j0, j1 = cols or (0, Q * nbs)
    lead = pad_rows // tm
    return _mm(a, w, grid=(lead + T // tm, j1 - j0, K // tk),
               a_spec=pl.BlockSpec((tm, tk), lambda i, j, k: (jnp.maximum(i - lead, 0), k)),
               b_spec=pl.BlockSpec((None, tk, tn), lambda i, j, k: ((j + j0) // nbs, k, (j + j0) % nbs)),
               o_spec=pl.BlockSpec((tm, tn), lambda i, j, k: (i, j)),
               o_shape=(pad_rows + T, (j1 - j0) * tn), o_dtype=o_dtype, dims=NN, acc_shape=(tm, tn), name=name,
               rider=rider, zero_rows=lead)


def _col_blocks(g, rows, tn, at):
    if g.ndim == 2:
        return pl.BlockSpec((rows, tn), at)
    per = g.shape[2] // tn

    def stacked(*idx):
        r, c = at(*idx)
        return c // per, r, c % per

    return pl.BlockSpec((None, rows, tn), stacked)


def _mm_nt(g, w, *, tm, to, tn, name, rider=None):
    T = g.shape[-2]
    Q, K, Ns = w.shape
    nbs = Ns // tn
    tm = min(tm, T)
    return _mm(g, w, grid=(T // tm, K // to, Q * nbs),
               a_spec=_col_blocks(g, tm, tn, lambda i, j, n: (i, n)),
               b_spec=pl.BlockSpec((None, to, tn), lambda i, j, n: (n // nbs, j, n % nbs)),
               o_spec=pl.BlockSpec((tm, to), lambda i, j, n: (i, j)),
               o_shape=(T, K), o_dtype=F32, dims=NT, acc_shape=(tm, to), name=name, rider=rider)


def _mm_tn(a, g, *, q, tk, tn, tt, name, rider=None):
    T, K = a.shape
    Ns = g.shape[-1] * (g.ndim - 1) // q
    nbs = Ns // tn
    return _mm(a, g, grid=(K // tk, q * nbs, T // tt),
               a_spec=pl.BlockSpec((tt, tk), lambda i, j, t: (t, i)),
               b_spec=_col_blocks(g, tt, tn, lambda i, j, t: (t, j)),
               o_spec=pl.BlockSpec((None, tk, tn), lambda i, j, t: (j // nbs, i, j % nbs)),
               o_shape=(q, K, Ns), o_dtype=BF16, dims=TN, acc_shape=(tk, tn), name=name, rider=rider)


def _mm_tn_add(a, g, part, *, tk, tn, name, rider=None):
    T, K = a.shape
    Q, _, Ns = part.shape
    nbs = Ns // tn

    def body(a_ref, g_ref, p_ref, o_ref):
        o_ref[...] = (_dot(a_ref[...], g_ref[...], TN) + p_ref[...].astype(F32)).astype(o_ref.dtype)

    blk = pl.BlockSpec((None, tk, tn), lambda i, j: (j // nbs, i, j % nbs))
    (out,), rode = _call(
        body, rider, name=name, grid=(K // tk, Q * nbs),
        in_specs=[pl.BlockSpec((T, tk), lambda i, j: (0, i)), _col_blocks(g, T, tn, lambda i, j: (0, j)), blk],
        out_specs=[blk], out_shape=[jax.ShapeDtypeStruct((Q, K, Ns), BF16)], scratch_shapes=[],
        compiler_params=_params(("arbitrary", "arbitrary"), 48), operands=(a, g, part))
    return out, rode


def _ln(u):
    mu = jnp.mean(u, axis=-1, keepdims=True)
    d = u - mu
    r = lax.rsqrt(jnp.mean(d * d, axis=-1, keepdims=True) + EPS)
    return d * r, r


def _ln_bwd(dy, un, r):
    return r * (dy - jnp.mean(dy, axis=-1, keepdims=True) - un * jnp.mean(dy * un, axis=-1, keepdims=True))


def _colsum(v):
    return jnp.sum(v, axis=0, keepdims=True)


def _rowwise(name, fn, bigs, vecs, out_dtypes, n_acc, tm=128, rider=None):
    T, D = bigs[0].shape
    nb, nv, no = len(bigs), len(vecs), len(out_dtypes)

    def body(*refs):
        outs, accs = fn([r[...] for r in refs[:nb]], [r[...] for r in refs[nb:nb + nv]])
        for r, o in zip(refs[nb + nv:nb + nv + no], outs):
            r[...] = o.astype(r.dtype)
        if n_acc:
            acc_ref = refs[nb + nv + no]

            @pl.when(pl.program_id(0) == 0)
            def _():
                acc_ref[...] = jnp.zeros_like(acc_ref)

            for row, a in enumerate(accs):
                acc_ref[row:row + 1, :] += a

    big_spec = pl.BlockSpec((tm, D), lambda i: (i, 0))
    vec_spec = pl.BlockSpec((1, D), lambda i: (0, 0))
    out_shape = [jax.ShapeDtypeStruct((T, D), dt) for dt in out_dtypes]
    out_specs = [big_spec] * no
    if n_acc:
        out_shape.append(jax.ShapeDtypeStruct((8, D), F32))
        out_specs.append(pl.BlockSpec((8, D), lambda i: (0, 0)))
    outs, rode = _call(
        body, rider, name=name, grid=(T // tm,), in_specs=[big_spec] * nb + [vec_spec] * nv,
        out_specs=out_specs, out_shape=out_shape, scratch_shapes=[],
        compiler_params=_params(("arbitrary",), 48), operands=(*bigs, *vecs))
    return outs if rider is None else (outs, rode)


def _to_bf16(w, name, rider=None):
    R, C = w.shape
    tr = _row_tile(R, C)

    def body(w_ref, o_ref):
        o_ref[...] = w_ref[...].astype(o_ref.dtype)

    blk = pl.BlockSpec((tr, C), lambda i: (i, 0))
    return _call(body, rider, name=name, grid=(R // tr,), in_specs=[blk], out_specs=[blk],
                 out_shape=[jax.ShapeDtypeStruct((R, C), BF16)], scratch_shapes=[],
                 compiler_params=_params(("arbitrary",)), operands=(w,))


def _pre_mixer(x, scale1, shift1, rider=None):
    def fn(b, v):
        xn, _ = _ln(b[0])
        return [xn * (1.0 + v[0]) + v[1]], []
    outs = _rowwise("pre_mixer", fn, [x], [scale1, shift1], [BF16], 0, rider=rider)
    return outs[0] if rider is None else (outs[0][0], outs[1])


def _post_mixer(mix, x, gate1, g1, b1, scale2, shift2, rider=None):
    def fn(b, v):
        un1, _ = _ln(ALPHA * b[1] + v[0] * b[0])
        x1 = un1 * v[1] + v[2]
        xn1, _ = _ln(x1)
        return [x1, xn1 * (1.0 + v[3]) + v[4]], []
    return _rowwise("post_mixer", fn, [mix, x], [gate1, g1, b1, scale2, shift2], [F32, BF16], 0, rider=rider)


def _loss_head(f, x1, tgt, gate2, g2, b2):
    def fn(b, v):
        ff, xx, tt = b
        d_model = ff.shape[-1]
        un2, r2 = _ln(ALPHA * xx + v[0] * ff)
        err = un2 * v[1] + v[2] - tt
        dy = err * (1.0 / d_model)
        du2 = _ln_bwd(dy * v[1], un2, r2)
        return [du2, du2 * v[0]], [_colsum(dy * un2), _colsum(dy), _colsum(du2 * ff), _colsum(err * err)]
    return _rowwise("loss_head", fn, [f, x1, tgt], [gate2, g2, b2], [F32, BF16], 4)


def _mid_bwd(dh2, du2, x1, mix, x, gate1, g1, scale2, rider=None):
    def fn(b, v):
        dh, du, xx1, mm, xx = b
        xn1, r1n = _ln(xx1)
        dx1 = ALPHA * du + _ln_bwd(dh * (1.0 + v[2]), xn1, r1n)
        un1, r1 = _ln(ALPHA * xx + v[0] * mm)
        du1 = _ln_bwd(dx1 * v[1], un1, r1)
        return [du1, du1 * v[0]], [_colsum(dh * xn1), _colsum(dh), _colsum(dx1 * un1), _colsum(dx1),
                                   _colsum(du1 * mm)]
    return _rowwise("mid_bwd", fn, [dh2, du2, x1, mix, x], [gate1, g1, scale2], [F32, BF16], 5, rider=rider)


def _first_bwd(dh1, du1, x, scale1, rider=None):
    def fn(b, v):
        dh, du, xx = b
        xn, r0 = _ln(xx)
        return [ALPHA * du + _ln_bwd(dh * (1.0 + v[0]), xn, r0)], [_colsum(dh * xn), _colsum(dh)]
    return _rowwise("first_bwd", fn, [dh1, du1, x], [scale1], [F32], 2, rider=rider)


def _ffn_in_swiglu(h2, w, *, tm, tn, rider=None):
    T, K = h2.shape
    Q, _, Ns = w.shape
    nbs = Ns // tn
    half = Q * nbs // 2
    tm = min(tm, T)

    def body(a_ref, wg_ref, wu_ref, g_ref, u_ref, act_ref):
        a = a_ref[...]
        g, u = _dot(a, wg_ref[...]), _dot(a, wu_ref[...])
        g_ref[...] = g.astype(g_ref.dtype)
        u_ref[...] = u.astype(u_ref.dtype)
        act_ref[...] = (g * jax.nn.sigmoid(g) * u).astype(act_ref.dtype)

    cols = lambda first: pl.BlockSpec((None, K, tn), lambda i, j: ((j + first) // nbs, 0, (j + first) % nbs))
    blk = pl.BlockSpec((tm, tn), lambda i, j: (i, j))
    return _call(
        body, rider, name="ffn_in", grid=(T // tm, half),
        in_specs=[pl.BlockSpec((tm, K), lambda i, j: (i, 0)), cols(0), cols(half)], out_specs=[blk] * 3,
        out_shape=[jax.ShapeDtypeStruct((T, half * tn), BF16)] * 3, scratch_shapes=[],
        compiler_params=_params(("arbitrary", "arbitrary"), 48), operands=(h2, w, w))


def _d_act_swiglu(df, w, gate, up, *, tm, to, rider=None):
    T, N = df.shape
    F = w.shape[1]
    tm = min(tm, T)

    def body(df_ref, w_ref, g_ref, u_ref, o_ref):
        d = _dot(df_ref[...], w_ref[...], NT)
        g = g_ref[...].astype(F32)
        s = jax.nn.sigmoid(g)
        o_ref[0] = (d * u_ref[...].astype(F32) * s * (1.0 + g * (1.0 - s))).astype(o_ref.dtype)
        o_ref[1] = (d * g * s).astype(o_ref.dtype)

    blk = pl.BlockSpec((tm, to), lambda i, j: (i, j))
    return _call(
        body, rider, name="d_act", grid=(T // tm, F // to),
        in_specs=[pl.BlockSpec((tm, N), lambda i, j: (i, 0)), pl.BlockSpec((None, to, N), lambda i, j: (0, j, 0)), blk, blk],
        out_specs=[pl.BlockSpec((2, tm, to), lambda i, j: (0, i, j))],
        out_shape=[jax.ShapeDtypeStruct((2, T, F), BF16)], scratch_shapes=[],
        compiler_params=_params(("arbitrary", "arbitrary"), 48), operands=(df, w, gate, up))


PAIR = 2


def _fill_table(table_ref, band_ref):
    table_ref[...] = jnp.full(table_ref.shape, NEG, F32)
    for e in range(PAIR):
        for g in range(QG):
            table_ref[e, g * CHUNK:(g + 1) * CHUNK, g * CHUNK:g * CHUNK + BAND] = band_ref[e]


def _attn_probs(q_ref, k_ref, bias_ref, e, step):
    start = pl.multiple_of(step * QROWS, QROWS)
    lanes = pl.ds(e * HD_A, HD_A)
    s = _dot(q_ref[:, lanes], k_ref[pl.ds(start + ZPAD - KPAD, UNION), lanes], NT) * (HD_A ** -0.5) + bias_ref[e]
    col = lax.broadcasted_iota(jnp.int32, s.shape, 1)
    s = jnp.where(col + start >= KPAD, s, NEG)
    p = jnp.exp(s - jnp.max(s, axis=-1, keepdims=True))
    return p / jnp.sum(p, axis=-1, keepdims=True), start


def _attn_specs(T, n_pairs):
    wide = PAIR * HD_A
    per_step = pl.BlockSpec((QROWS, wide), lambda hp, n: (n, hp))
    queries = pl.BlockSpec((QROWS, wide), lambda hp, n: (n + ZPAD // QROWS, hp))
    keys = pl.BlockSpec((ZPAD + T, wide), lambda hp, n: (0, n_pairs + hp))
    values = pl.BlockSpec((ZPAD + T, wide), lambda hp, n: (0, 2 * n_pairs + hp))
    grads = pl.BlockSpec((T, wide), lambda hp, n: (0, hp))
    table = pl.BlockSpec((PAIR, CHUNK, BAND), lambda hp, n: (hp, 0, 0))
    vec = pl.BlockSpec((1, wide), lambda hp, n: (0, hp))
    return per_step, queries, keys, values, grads, table, vec


def _probs_spec():
    return pl.BlockSpec((PAIR, QROWS, UNION), lambda hp, n: (hp, n, 0))


def _attn_fwd(qkv, bias, gain, rider=None):
    T = qkv.shape[0] - ZPAD
    W = gain.shape[1]
    n_pairs = W // (PAIR * HD_A)

    def body(q_ref, k_ref, v_ref, band_ref, gain_ref, o_ref, p_ref, table_ref):
        @pl.when(pl.program_id(1) == 0)
        def _():
            _fill_table(table_ref, band_ref)

        for e in range(PAIR):
            lanes = pl.ds(e * HD_A, HD_A)
            p, start = _attn_probs(q_ref, k_ref, table_ref, e, pl.program_id(1))
            p_ref[e] = p.astype(p_ref.dtype)
            o = _dot(p_ref[e], v_ref[pl.ds(start + ZPAD - KPAD, UNION), lanes])
            rr = lax.rsqrt(jnp.mean(o * o, axis=-1, keepdims=True) + EPS)
            o_ref[:, lanes] = (o * rr * gain_ref[:, lanes]).astype(o_ref.dtype)

    per_step, queries, keys, values, _, table, vec = _attn_specs(T, n_pairs)
    return _call(
        body, rider, name="attn_fwd", grid=(n_pairs, T // QROWS), in_specs=[queries, keys, values, table, vec],
        out_specs=[per_step, _probs_spec()],
        out_shape=[jax.ShapeDtypeStruct((T, W), BF16), jax.ShapeDtypeStruct((n_pairs * PAIR, T, UNION), BF16)],
        scratch_shapes=[pltpu.VMEM((PAIR, QROWS, UNION), F32)],
        compiler_params=_params(("arbitrary", "arbitrary"), 40), operands=(qkv, qkv, qkv, bias, gain))


def _attn_bwd(qkv, probs, gain, dmixin, rider=None):
    T = qkv.shape[0] - ZPAD
    W = gain.shape[1]
    n_pairs = W // (PAIR * HD_A)
    scale = HD_A ** -0.5

    def body(q_ref, k_ref, v_ref, p_ref, gain_ref, don_ref, dq_ref, dkb_ref, dvb_ref, dband_ref, dgain_ref,
             dtable_ref, dk_ref, dv_ref):
        n = pl.program_id(1)

        @pl.when(n == 0)
        def _():
            dk_ref[...] = jnp.zeros_like(dk_ref)
            dv_ref[...] = jnp.zeros_like(dv_ref)
            dtable_ref[...] = jnp.zeros_like(dtable_ref)
            dgain_ref[...] = jnp.zeros_like(dgain_ref)

        for e in range(PAIR):
            lanes = pl.ds(e * HD_A, HD_A)
            start = pl.multiple_of(n * QROWS, QROWS)
            keys, in_qkv = pl.ds(start, UNION), pl.ds(start + ZPAD - KPAD, UNION)
            pb = p_ref[e]
            p = pb.astype(F32)
            vb = v_ref[in_qkv, lanes]
            o = _dot(pb, vb)
            rr = lax.rsqrt(jnp.mean(o * o, axis=-1, keepdims=True) + EPS)
            on = o * rr
            d_on = don_ref[:, lanes]
            dgain_ref[:, lanes] += _colsum(d_on * on)
            dyo = d_on * gain_ref[:, lanes]
            do = rr * (dyo - on * jnp.mean(dyo * on, axis=-1, keepdims=True))
            dob = do.astype(BF16)
            dp = _dot(dob, vb, NT)
            ds = p * (dp - jnp.sum(do * o, axis=-1, keepdims=True))
            dtable_ref[e] += ds
            dsb = ds.astype(BF16)
            dq_ref[:, lanes] = (_dot(dsb, k_ref[in_qkv, lanes]) * scale).astype(dq_ref.dtype)
            dk_ref[keys, lanes] += _dot(dsb, q_ref[:, lanes], TN) * scale
            dv_ref[keys, lanes] += _dot(pb, dob, TN)

        @pl.when(n == T // QROWS - 1)
        def _():
            for e in range(PAIR):
                dband_ref[e] = sum(dtable_ref[e, g * CHUNK:(g + 1) * CHUNK, g * CHUNK:g * CHUNK + BAND]
                                   for g in range(QG))
            dkb_ref[...] = dk_ref[KPAD:, :].astype(dkb_ref.dtype)
            dvb_ref[...] = dv_ref[KPAD:, :].astype(dvb_ref.dtype)

    per_step, queries, keys, values, grads, table, vec = _attn_specs(T, n_pairs)
    H = n_pairs * PAIR
    return _call(
        body, rider, name="attn_bwd", grid=(n_pairs, T // QROWS),
        in_specs=[queries, keys, values, _probs_spec(), vec, per_step],
        out_specs=[per_step, grads, grads, table, vec],
        out_shape=[jax.ShapeDtypeStruct((T, W), BF16)] * 3 + [jax.ShapeDtypeStruct((H, CHUNK, BAND), F32),
                                                              jax.ShapeDtypeStruct((1, W), F32)],
        scratch_shapes=[pltpu.VMEM((PAIR, QROWS, UNION), F32)] + [pltpu.VMEM((KPAD + T, PAIR * HD_A), F32)] * 2,
        compiler_params=_params(("arbitrary", "arbitrary"), 40),
        operands=(qkv, qkv, qkv, probs, gain, dmixin))


N_DIAG = CHUNK + BAND - 1


def _bias_band(rel_bias):
    H = rel_bias.shape[0]
    idx = np.clip(BAND - 1 - np.arange(N_DIAG), -MAX_REL, MAX_REL) + MAX_REL
    rolled = rel_bias[:, idx[(np.arange(N_DIAG) + CHUNK - 1) % N_DIAG]]
    flat = jnp.broadcast_to(rolled[:, None, :], (H, CHUNK, N_DIAG)).reshape(H, CHUNK * N_DIAG)
    return flat[:, :CHUNK * (N_DIAG - 1)].reshape(H, CHUNK, N_DIAG - 1)[:, :, :BAND]


def _bias_band_grad(dband):
    H = dband.shape[0]
    skew = jnp.pad(dband, ((0, 0), (0, 0), (CHUNK - 1, 0))).reshape(H, CHUNK * N_DIAG)
    skew = jnp.pad(skew, ((0, 0), (0, CHUNK))).reshape(H, CHUNK, N_DIAG + 1)
    diag = jnp.sum(skew, axis=1)[:, :N_DIAG]
    n_far = BAND - MAX_REL
    far = jnp.sum(diag[:, :n_far], axis=1, keepdims=True)
    near = diag[:, n_far:][:, ::-1]
    zeros = jnp.zeros((H, MAX_REL - (CHUNK - 1)), F32)
    return jnp.concatenate([zeros, near, far], axis=1)


def _tri(n, lower):
    r = lax.broadcasted_iota(jnp.int32, (n, n), 0)
    c = lax.broadcasted_iota(jnp.int32, (n, n), 1)
    return jnp.where((c <= r) if lower else (c >= r), 1.0, 0.0).astype(F32)


def _hgrn_gates(zq_ref, zf_ref, lbl_ref, q_s, k_s, b_s):
    lb = jax.nn.sigmoid(lbl_ref[0:1, :] - lbl_ref[1:2, :])
    zq = zq_ref[...]
    sig = jax.nn.sigmoid(zf_ref[...])
    f = lb + (1.0 - lb) * sig
    sq = jax.nn.sigmoid(zq)
    q_s[...] = zq * sq
    k_s[...] = 1.0 - f
    b_s[...] = _dot(_tri(CHUNK, True), jnp.log(f), precision=HIGHEST)
    return lb, sig, f, sq


def _sub_rows(i):
    return pl.ds(i * SUB, SUB)


def _row_mask(s):
    return lax.broadcasted_iota(jnp.int32, (SUB, HD_B), 0) >= s


def _decay_from(b_sub, b_row, s):
    return jnp.where(_row_mask(s), jnp.exp(jnp.minimum(b_sub - b_row, 0.0)), 0.0)


def _hgrn_fwd(proj, lb_logits, gnorm_g, rider=None):
    T = proj.shape[0]
    nC = T // CHUNK
    W = lb_logits.shape[1]
    G = W // HD_B // HGRN_HEADS
    col0 = (proj.shape[1] - 4 * W) // (HD_B * HGRN_HEADS)
    wide = HGRN_HEADS * HD_B

    def body(*refs):
        @pl.when(pl.program_id(1) == 0)
        def _():
            refs[9][...] = jnp.zeros_like(refs[9])

        for h in range(HGRN_HEADS):
            lanes = pl.ds(h * HD_B, HD_B)
            one_head(*[r.at[:, lanes] for r in refs[:5]], refs[5], *[r.at[:, lanes] for r in refs[6:8]],
                     *[r.at[h] for r in refs[8:]])

    def one_head(zq_ref, zf_ref, xi_ref, zg_ref, lbl_ref, gn_ref, mix_ref, o_ref, stall_ref, st_ref, q_s, k_s, b_s, acc_s):
        _hgrn_gates(zq_ref, zf_ref, lbl_ref, q_s, k_s, b_s)
        q, k, b = q_s[...], k_s[...], b_s[...]
        st = st_ref[...]
        stall_ref[...] = st
        b_last = b_s[CHUNK - 1:CHUNK, :]
        acc_s[...] = _dot((q * jnp.exp(b)).astype(BF16), st.astype(BF16), NT)
        for i in range(CHUNK // SUB):
            rows = _sub_rows(i)
            q_i, b_i = q_s[rows, :], b_s[rows, :]
            acc = jnp.zeros((SUB, HD_B), F32)
            if i:
                past = pl.ds(0, i * SUB)
                b_ref = b_s[i * SUB - 1:i * SUB, :]
                qs = (q_i * jnp.exp(b_i - b_ref)).astype(BF16)
                ks = (k_s[past, :] * jnp.exp(b_ref - b_s[past, :])).astype(BF16)
                acc += _dot(_dot(qs, ks, NT).astype(BF16), xi_ref[past, :].astype(BF16))
            for s in range(SUB):
                row = pl.ds(i * SUB + s, 1)
                w = q_i * _decay_from(b_i, b_s[row, :], s)
                acc += jnp.sum(w * k_s[row, :], axis=-1, keepdims=True) * xi_ref[row, :]
            acc_s[rows, :] += acc
        o = acc_s[...]
        kd = (k * jnp.exp(b_last - b)).astype(BF16)
        st_ref[...] = st * jnp.exp(b_last) + _dot(xi_ref[...].astype(BF16), kd, TN)
        o_ref[...] = o
        zg = zg_ref[...]
        rr = lax.rsqrt(jnp.mean(o * o, axis=-1, keepdims=True) + EPS)
        mix_ref[...] = (o * rr * gn_ref[...] * (zg * jax.nn.sigmoid(zg))).astype(mix_ref.dtype)

    col = lambda part: pl.BlockSpec((CHUNK, wide), lambda g, n: (n, col0 + part * G + g))
    out_blk = pl.BlockSpec((CHUNK, wide), lambda g, n: (n, g))
    tile = pltpu.VMEM((HGRN_HEADS, CHUNK, HD_B), F32)
    return _call(
        body, rider, name="hgrn_fwd", grid=(G, nC),
        in_specs=[col(0), col(1), col(2), col(3), pl.BlockSpec((2, wide), lambda g, n: (0, g)),
                  pl.BlockSpec((1, HD_B), lambda g, n: (0, 0))],
        out_specs=[out_blk, out_blk, pl.BlockSpec((HGRN_HEADS, None, HD_B, HD_B), lambda g, n: (g, n, 0, 0))],
        out_shape=[jax.ShapeDtypeStruct((T, W), BF16), jax.ShapeDtypeStruct((T, W), F32),
                   jax.ShapeDtypeStruct((G * HGRN_HEADS, nC, HD_B, HD_B), F32)],
        scratch_shapes=[pltpu.VMEM((HGRN_HEADS, HD_B, HD_B), F32), tile, tile, tile, tile],
        compiler_params=_params(("arbitrary", "arbitrary")),
        operands=(proj, proj, proj, proj, lb_logits, gnorm_g))


def _hgrn_bwd(proj, lb_logits, gnorm_g, o_b, st_all, dmixin, d_attn, rider=None):
    T = proj.shape[0]
    nC = T // CHUNK
    W = lb_logits.shape[1]
    wa = d_attn[0].shape[1]
    assert W == HGRN_HEADS * HD_B, "one grid step takes every head: it writes whole rows of d proj"
    G = W // HD_B // HGRN_HEADS
    wide = HGRN_HEADS * HD_B
    col0 = (proj.shape[1] - 4 * W) // wide
    dcol0 = (dmixin.shape[1] - W) // wide

    def body(*refs):
        g, n = pl.program_id(0), pl.program_id(1)
        dproj_ref, dl0_ref, dgn_ref, dst_ref = refs[12:16]
        for i in range(3):
            dproj_ref[:, i * wa:(i + 1) * wa] = refs[9 + i][...]

        @pl.when(n == 0)
        def _():
            dst_ref[...] = jnp.zeros_like(dst_ref)
            dl0_ref[...] = jnp.zeros_like(dl0_ref)

        @pl.when((n == 0) & (g == 0))
        def _():
            dgn_ref[...] = jnp.zeros_like(dgn_ref)

        for h in range(HGRN_HEADS):
            lanes = pl.ds(h * HD_B, HD_B)
            cut = lambda r: r.at[:, lanes]
            parts = [dproj_ref.at[:, pl.ds(3 * wa + part * W + h * HD_B, HD_B)] for part in range(4)]
            one_head(*[cut(r) for r in refs[:5]], refs[5], cut(refs[6]), refs[7].at[h], cut(refs[8]),
                     *parts, cut(dl0_ref), dgn_ref, *[r.at[h] for r in refs[15:]])

    def one_head(zq_ref, zf_ref, xi_ref, zg_ref, lbl_ref, gn_ref, o_ref, st_ref, dout_ref,
                 dzq_ref, dzf_ref, dxi_ref, dzg_ref, dl0_ref, dgn_ref, dst_ref, q_s, k_s, b_s, do_s, dq_s, dk_s, di_s):
        lb, sig, f, sq = _hgrn_gates(zq_ref, zf_ref, lbl_ref, q_s, k_s, b_s)
        q, k, b = q_s[...], k_s[...], b_s[...]
        zg, o, dout = zg_ref[...], o_ref[...], dout_ref[...]
        sg = jax.nn.sigmoid(zg)
        rr = lax.rsqrt(jnp.mean(o * o, axis=-1, keepdims=True) + EPS)
        on = o * rr
        gn = gn_ref[...]
        dzg_ref[...] = (dout * on * gn * sg * (1.0 + zg * (1.0 - sg))).astype(dzg_ref.dtype)
        d_on = dout * zg * sg
        dgn_ref[...] += _colsum(d_on * on)
        d_on = d_on * gn
        do = rr * (d_on - on * jnp.mean(d_on * on, axis=-1, keepdims=True))
        do_s[...] = do
        dob = do.astype(BF16)
        st, dst = st_ref[...], dst_ref[...]
        b_last = b_s[CHUNK - 1:CHUNK, :]
        eb, e_last, k_dec = jnp.exp(b), jnp.exp(b_last), jnp.exp(b_last - b)
        qt, kd = q * eb, k * k_dec
        dstb = dst.astype(BF16)
        xib = xi_ref[...].astype(BF16)
        d_kd = _dot(xib, dstb)
        dq_s[...] = _dot(dob, st.astype(BF16)) * eb
        dk_s[...] = d_kd * k_dec
        di_s[...] = _dot(kd.astype(BF16), dstb, NT)
        d_b_last = e_last * _colsum(st * dst) + _colsum(d_kd * kd)
        dst_ref[...] = _dot(dob, qt.astype(BF16), TN) + dst * e_last
        for i in range(CHUNK // SUB):
            rows = _sub_rows(i)
            q_i, b_i, do_i = q_s[rows, :], b_s[rows, :], do_s[rows, :]
            dq_i = jnp.zeros((SUB, HD_B), F32)
            if i:
                past = pl.ds(0, i * SUB)
                b_ref = b_s[i * SUB - 1:i * SUB, :]
                e_q, e_k = jnp.exp(b_i - b_ref), jnp.exp(b_ref - b_s[past, :])
                qs, ks = (q_i * e_q).astype(BF16), (k_s[past, :] * e_k).astype(BF16)
                xi_p, do_b = xi_ref[past, :].astype(BF16), do_i.astype(BF16)
                di_s[past, :] += _dot(_dot(ks, qs, NT).astype(BF16), do_b)
                dq_i += _dot(_dot(do_b, xi_p, NT).astype(BF16), ks) * e_q
                dk_s[past, :] += _dot(_dot(xi_p, do_b, NT).astype(BF16), qs) * e_k
            for s in range(SUB):
                row = pl.ds(i * SUB + s, 1)
                k_row, i_row = k_s[row, :], xi_ref[row, :]
                e = _decay_from(b_i, b_s[row, :], s)
                w = q_i * e
                a_col = jnp.sum(w * k_row, axis=-1, keepdims=True)
                da_col = jnp.sum(do_i * i_row, axis=-1, keepdims=True)
                di_s[row, :] += _colsum(a_col * do_i)
                dq_i += da_col * e * k_row
                dk_s[row, :] += _colsum(da_col * w)
            dq_s[rows, :] += dq_i
        dq, dk = dq_s[...], dk_s[...]
        db = q * dq - k * dk
        is_last = lax.broadcasted_iota(jnp.int32, (CHUNK, HD_B), 0) == CHUNK - 1
        db = db + jnp.where(is_last, d_b_last, 0.0)
        df = _dot(_tri(CHUNK, False), db, precision=HIGHEST) / f - dk
        dzf_ref[...] = (df * (1.0 - lb) * sig * (1.0 - sig)).astype(dzf_ref.dtype)
        dl0_ref[...] += _colsum(df * (1.0 - sig)) * (lb * (1.0 - lb))
        zq = zq_ref[...]
        dzq_ref[...] = (dq * sq * (1.0 + zq * (1.0 - sq))).astype(dzq_ref.dtype)
        dxi_ref[...] = di_s[...].astype(dxi_ref.dtype)

    rev = lambda n: nC - 1 - n
    col = lambda part: pl.BlockSpec((CHUNK, wide), lambda g, n: (rev(n), col0 + part * G + g))
    blk = pl.BlockSpec((CHUNK, wide), lambda g, n: (rev(n), g))
    tile = pltpu.VMEM((HGRN_HEADS, CHUNK, HD_B), F32)
    rows = lambda width: pl.BlockSpec((CHUNK, width), lambda g, n: (rev(n), 0))
    return _call(
        body, rider, name="hgrn_bwd", grid=(G, nC),
        in_specs=[col(0), col(1), col(2), col(3), pl.BlockSpec((2, wide), lambda g, n: (0, g)),
                  pl.BlockSpec((1, HD_B), lambda g, n: (0, 0)), blk,
                  pl.BlockSpec((HGRN_HEADS, None, HD_B, HD_B), lambda g, n: (g, rev(n), 0, 0)),
                  pl.BlockSpec((CHUNK, wide), lambda g, n: (rev(n), dcol0 + g)), rows(wa), rows(wa), rows(wa)],
        out_specs=[rows(3 * wa + 4 * W), pl.BlockSpec((1, wide), lambda g, n: (0, g)),
                   pl.BlockSpec((1, HD_B), lambda g, n: (0, 0))],
        out_shape=[jax.ShapeDtypeStruct((T, 3 * wa + 4 * W), BF16), jax.ShapeDtypeStruct((1, W), F32),
                   jax.ShapeDtypeStruct((1, HD_B), F32)],
        scratch_shapes=[pltpu.VMEM((HGRN_HEADS, HD_B, HD_B), F32)] + [tile] * 7,
        compiler_params=_params(("arbitrary", "arbitrary")),
        operands=(proj, proj, proj, proj, lb_logits, gnorm_g, o_b, st_all, dmixin, *d_attn))


def _adamw_math(g, w, m, v):
    m = B1 * m + (1.0 - B1) * g
    v = B2 * v + (1.0 - B2) * (g * g)
    m_hat = m / (1.0 - B1 ** STEP)
    v_hat = v / (1.0 - B2 ** STEP)
    return -LR * (m_hat / (jnp.sqrt(v_hat) + ADAM_EPS) + WD * w), m, v


def _adamw(g, w, m, v, name):
    R, C = g.shape
    tr = _row_tile(R, C)

    def body(g_ref, w_ref, m_ref, v_ref, go_ref, d_ref, mo_ref, vo_ref):
        g = g_ref[...]
        go_ref[...] = g
        d_ref[...], mo_ref[...], vo_ref[...] = _adamw_math(g, w_ref[...], m_ref[...], v_ref[...])

    blk = pl.BlockSpec((tr, C), lambda i: (i, 0))
    return pl.pallas_call(
        body, name=name, grid=(R // tr,), in_specs=[blk] * 4, out_specs=[blk] * 4,
        out_shape=[jax.ShapeDtypeStruct((R, C), F32)] * 4, compiler_params=_params(("parallel",), 40),
    )(g, w, m, v)


def _sum_pair(g_full, from_sibling, sel, name):
    Q, K, Ns = g_full.shape
    kh = K // 2
    tr = _row_tile(kh, Ns)
    nh = kh // tr

    def body(sel_ref, a_ref, b_ref, o_ref):
        o_ref[...] = (a_ref[...].astype(F32) + b_ref[...].astype(F32)).astype(o_ref.dtype)

    return pl.pallas_call(
        body, name=name,
        grid_spec=pltpu.PrefetchScalarGridSpec(
            num_scalar_prefetch=1, grid=(Q, nh),
            in_specs=[pl.BlockSpec((None, tr, Ns), lambda q, i, sel: (q, sel[1] * nh + i, 0)),
                      pl.BlockSpec((None, tr, Ns), lambda q, i, sel: (q, i, 0))],
            out_specs=pl.BlockSpec((None, tr, Ns), lambda q, i, sel: (q, i, 0))),
        out_shape=jax.ShapeDtypeStruct((Q, kh, Ns), BF16), compiler_params=_params(("parallel", "parallel")),
    )(sel, g_full, from_sibling)


def _sum_chips(pair_sum, from_chips, sel, name):
    Q, kh, Ns = pair_sum.shape
    tr = _row_tile(kh, Ns)
    nh = kh // tr

    def body(sel_ref, a_ref, b0_ref, b1_ref, b2_ref, o_ref):
        up = lambda r: r[...].astype(F32)
        o_ref[...] = ((up(a_ref) + up(b0_ref)) + up(b1_ref)) + up(b2_ref)

    recv = lambda k: pl.BlockSpec((None, tr, Ns), lambda i, sel: (k, i, 0))
    return pl.pallas_call(
        body, name=name,
        grid_spec=pltpu.PrefetchScalarGridSpec(
            num_scalar_prefetch=1, grid=(nh,),
            in_specs=[pl.BlockSpec((None, tr, Ns), lambda i, sel: (sel[0], i, 0)), recv(0), recv(1), recv(2)],
            out_specs=pl.BlockSpec((tr, Ns), lambda i, sel: (sel[1] * nh + i, 0))),
        out_shape=jax.ShapeDtypeStruct((2 * kh, Ns), F32), compiler_params=_params(("parallel",)),
    )(sel, pair_sum, from_chips, from_chips, from_chips)


def _gather_small(v, name):
    R, L = v.shape

    def body(v_ref, out_ref, send_sems, recv_sems):
        x, y, c = _place()
        me = 4 * x + 2 * y + c
        out_ref[me] = v_ref[...]
        peers = [(_flip(x, k >> 2 & 1), _flip(y, k >> 1 & 1), _flip(c, k & 1)) for k in range(1, N_DEV)]

        def copy(k, row, to):
            return pltpu.make_async_remote_copy(src_ref=v_ref, dst_ref=out_ref.at[row], send_sem=send_sems.at[k],
                                                recv_sem=recv_sems.at[k], device_id=to, device_id_type=MESH)

        sends = [copy(k, me, peer) for k, peer in enumerate(peers)]
        for cp in sends:
            cp.start()
        for k, (px, py, pc) in enumerate(peers):
            copy(k, 4 * px + 2 * py + pc, (x, y, c)).wait_recv()
        for cp in sends:
            cp.wait_send()

    vmem = pl.BlockSpec(memory_space=pltpu.VMEM)
    return pl.pallas_call(
        body, name=name, in_specs=[vmem], out_specs=vmem, out_shape=jax.ShapeDtypeStruct((N_DEV, R, L), F32),
        scratch_shapes=[pltpu.SemaphoreType.DMA((N_DEV - 1,)), pltpu.SemaphoreType.DMA((N_DEV - 1,))],
    )(v)


def _small_rider(v):
    def copies(ins, outs, send_sems, recv_sems):
        x, y, c = _place()
        peers = [(_flip(x, k >> 2 & 1), _flip(y, k >> 1 & 1), _flip(c, k & 1)) for k in range(1, N_DEV)]

        def copy(k, row, to):
            return pltpu.make_async_remote_copy(src_ref=ins[0], dst_ref=outs[0].at[row], send_sem=send_sems.at[k],
                                                recv_sem=recv_sems.at[k], device_id=to, device_id_type=MESH)

        sends = [copy(k, 4 * x + 2 * y + c, peer) for k, peer in enumerate(peers)]
        return sends, [copy(k, 4 * px + 2 * py + pc, (x, y, c)) for k, (px, py, pc) in enumerate(peers)]

    def start(*refs):
        for cp in copies(*refs)[0]:
            cp.start()

    def finish(*refs):
        sends, arrivals = copies(*refs)
        for cp in arrivals:
            cp.wait_recv()
        for cp in sends:
            cp.wait_send()

    return _Rider([v], [jax.ShapeDtypeStruct((N_DEV, *v.shape), F32)], N_DEV - 1, start, finish)


def _silu(v):
    return v * jax.nn.sigmoid(v)


def _ada_fwd(c_all, w_ada, rider, tn=512):
    M, D = c_all.shape
    Ns = w_ada.shape[1]
    steps = Ns // tn

    def body(c_ref, w_ref, o_ref, all_ref, send_sems, recv_sems):
        j = pl.program_id(0)
        o_ref[:, pl.ds(pl.multiple_of(j * tn, tn), tn)] = _dot(_silu(c_ref[...]).astype(BF16), w_ref[...].astype(BF16))

        @pl.when(j == steps - 1)
        def _():
            x, y, c = _place()
            peers = [(_flip(x, k >> 2 & 1), _flip(y, k >> 1 & 1), _flip(c, k & 1)) for k in range(1, N_DEV)]

            def copy(k, row, to):
                return pltpu.make_async_remote_copy(
                    src_ref=o_ref.at[pl.ds(0, N_DEV)], dst_ref=all_ref.at[row], send_sem=send_sems.at[k],
                    recv_sem=recv_sems.at[k], device_id=to, device_id_type=MESH)

            sends = [copy(k, 4 * x + 2 * y + c, peer) for k, peer in enumerate(peers)]
            for cp in sends:
                cp.start()
            for k, (px, py, pc) in enumerate(peers):
                copy(k, 4 * px + 2 * py + pc, (x, y, c)).wait_recv()
            for cp in sends:
                cp.wait_send()

    (out, out_all), rode = _call(
        body, rider, name="ada_fwd", grid=(steps,),
        in_specs=[pl.BlockSpec((M, D), lambda j: (0, 0)), pl.BlockSpec((D, tn), lambda j: (0, j))],
        out_specs=[pl.BlockSpec((M, Ns), lambda j: (0, 0)), ANY],
        out_shape=[jax.ShapeDtypeStruct((M, Ns), F32), jax.ShapeDtypeStruct((N_DEV, N_DEV, Ns), F32)],
        scratch_shapes=[pltpu.SemaphoreType.DMA((N_DEV - 1,)), pltpu.SemaphoreType.DMA((N_DEV - 1,))],
        compiler_params=_params(("arbitrary",)), operands=(c_all, w_ada))
    return out, out_all, rode


def _ada_bwd(c_all, dmod, w, m, v, tk=256, tn=1536):
    M, D = c_all.shape
    Ns = dmod.shape[1]

    def body(c_ref, d_ref, w_ref, m_ref, v_ref, g_ref, dl_ref, mo_ref, vo_ref):
        g = _dot(_silu(c_ref[...]).astype(BF16), d_ref[...].astype(BF16), TN)
        g_ref[...] = g
        dl_ref[...], mo_ref[...], vo_ref[...] = _adamw_math(g, w_ref[...], m_ref[...], v_ref[...])

    blk = pl.BlockSpec((tk, tn), lambda i, j: (i, j))
    return pl.pallas_call(
        body, name="ada_bwd", grid=(D // tk, Ns // tn),
        in_specs=[pl.BlockSpec((M, tk), lambda i, j: (0, i)), pl.BlockSpec((M, tn), lambda i, j: (0, j)), blk, blk, blk],
        out_specs=[blk] * 4, out_shape=[jax.ShapeDtypeStruct((D, Ns), F32)] * 4,
        compiler_params=_params(("parallel", "parallel"), 40),
    )(c_all, dmod, w, m, v)


def _small_update(g_all, w, m, v):
    R, L = w.shape

    def body(g_ref, w_ref, m_ref, v_ref, go_ref, d_ref, mo_ref, vo_ref):
        g = g_ref[0]
        for d in range(1, N_DEV):
            g = g + g_ref[d]
        go_ref[...] = g
        d_ref[...], mo_ref[...], vo_ref[...] = _adamw_math(g, w_ref[...], m_ref[...], v_ref[...])

    return pl.pallas_call(body, name="small_update", out_shape=[jax.ShapeDtypeStruct((R, L), F32)] * 4)(g_all, w, m, v)


def _pack(parts, rows):
    flat = jnp.concatenate([p.reshape(-1) for p in parts])
    return jnp.pad(flat, (0, rows * 128 - flat.shape[0])).reshape(rows, 128)


def _unpack(packed, shapes):
    flat, out, at = packed.reshape(-1), [], 0
    for shp in shapes:
        size = 1
        for d in shp:
            size *= d
        out.append(flat[at:at + size].reshape(shp))
        at += size
    return out


def _layer(x, tgt, mod, wts, rel_bias, attn_norm_g, lb_logits, gnorm_g, ln1_g, ln1_b, ln2_g, ln2_b, place=None):
    T, D = x.shape
    aw = attn_norm_g.shape[1]
    shift1, scale1, gate1, shift2, scale2, gate2 = [mod[i:i + 1] for i in range(6)]

    def gather(n, rows=None, into=None, before=None, last=True):
        return None if place is None else _gather_rider(wts[n], rows, None if into is None else into[0], before, last)

    def gathered(n, rode):
        return wts[n] if place is None else lax.dynamic_update_index_in_dim(rode[0], wts[n], place[0], 0)

    def blocks(g):
        return g.reshape(N_CHIPS, -1, g.shape[2])

    def to_sibling(g):
        return None if place is None else _pair_rider(g)

    def pair_sum(n, g, rode=None):
        if place is None:
            return g
        rode = _alone(_pair_rider(g), n + "_send_pair") if rode is None else rode
        return _sum_pair(g, rode[0], place[1], n + "_sum_pair")

    def to_chips(p, rows=None, into=None):
        return None if place is None else _chips_rider(p, rows, None if into is None else into[0])

    def summed(n, p, rode):
        return p if place is None else _sum_chips(p, rode[0], place[1], n + "_sum_chips")

    def to_both(block):
        return None if place is None else _share_rider(block)

    def carrying(mm, *args, rider, **kw):
        return mm(*args, rider=rider, **kw) if rider is not None else (mm(*args, **kw), None)

    def to_sibling_acts(a, b):
        return None if place is None else _acts_rider(a, b)

    def pair_grad(name, a, b, tn, rider, arrived=None, late_rider=None):
        if place is None:
            return _mm_tn(a, b, q=N_CHIPS, tk=512, tn=tn, tt=T, name=name), None
        kh = a.shape[1] // 2
        mine = lax.dynamic_slice_in_dim(a, place[1][1] * kh, kh, axis=1)
        part, rode = carrying(_mm_tn, mine, b, q=N_CHIPS, tk=512, tn=tn, tt=T, name=name + "_own",
                              rider=_join(None if arrived else _acts_rider(a, b), rider))
        (a_sib, b_sib), rode = arrived or rode[:2], rode if arrived else rode[2:]
        out, late = _mm_tn_add(a_sib, b_sib, part, tk=512, tn=tn, name=name + "_sib", rider=late_rider)
        return out, (rode or []) + late

    if place is None:
        h1, rode = _pre_mixer(x, scale1, shift1), None
    else:
        h1, rode = _pre_mixer(x, scale1, shift1, wts["w_in_last_part"])
    w_in = gathered("w_in", rode)
    n_qkv = 3 * aw // 256
    kh_o, kh_f, kh_out = [wts[n].shape[-2] // 2 for n in ("w_o", "w_ffn_in", "w_ffn_out")]
    o_cut, f_cuts, out_cut = 3 * kh_o // 8, (7 * kh_f // 16, 7 * kh_f // 8), kh_out // 11
    qkv, rode = carrying(_mm_nn, h1, w_in, tm=ZPAD, tn=256, tk=D, name="proj_qkv", cols=(0, n_qkv), o_dtype=BF16,
                         pad_rows=ZPAD, rider=gather("w_o", (0, o_cut), last=False))
    proj, rode = carrying(_mm_nn, h1, w_in, tm=2048, tn=256, tk=D, name="proj_rec",
                          cols=(n_qkv, N_CHIPS * w_in.shape[2] // 256),
                          rider=gather("w_o", (o_cut, kh_o - o_cut), rode, before=(0, o_cut)))
    w_o3 = gathered("w_o", rode).reshape(1, D, D)
    bias = _bias_band(rel_bias)
    (mix_a, probs), rode = _attn_fwd(qkv, bias, attn_norm_g, rider=gather("w_ffn_in", (0, f_cuts[0]), last=False))
    (mix_b, o_b, st_all), rode = _hgrn_fwd(
        proj, lb_logits, gnorm_g,
        rider=gather("w_ffn_in", (f_cuts[0], f_cuts[1] - f_cuts[0]), rode, before=(0, f_cuts[0]), last=False))
    mixin = jnp.concatenate([mix_a, mix_b], axis=1)
    mix = _mm_nn(mixin, w_o3, tm=1024, tn=512, tk=D, name="mix_out")
    if place is None:
        x1, h2 = _post_mixer(mix, x, gate1, ln1_g, ln1_b, scale2, shift2)
    else:
        (x1, h2), rode = _post_mixer(mix, x, gate1, ln1_g, ln1_b, scale2, shift2, rider=_join(
            gather("w_ffn_in", (f_cuts[1], kh_f - f_cuts[1]), rode, before=(f_cuts[0], f_cuts[1] - f_cuts[0])),
            gather("w_ffn_out", (0, out_cut), last=False)))
    w_ffn_in = gathered("w_ffn_in", rode)
    (gate, up, act), rode = _ffn_in_swiglu(
        h2, w_ffn_in, tm=2048, tn=256,
        rider=gather("w_ffn_out", (out_cut, kh_out - out_cut), rode and rode[1:], before=(0, out_cut)))
    w_out3 = gathered("w_ffn_out", rode)
    w_out3 = w_out3.reshape(1, -1, w_out3.shape[2])
    d_ff = w_out3.shape[1]
    f = _mm_nn(act, w_out3, tm=1024, tn=512, tk=d_ff, name="ffn_out")
    du2, df, acc2 = _loss_head(f, x1, tgt, gate2, ln2_g, ln2_b)
    loss = (0.5 / D) * jnp.sum(acc2[3])
    g = blocks(_mm_tn(act, df, q=1, tk=512, tn=1024, tt=T, name="g_ffn_out"))
    (dff,), rode = _d_act_swiglu(df, w_out3, gate, up, tm=1024, to=512, rider=to_sibling(g))
    p_out = pair_sum("w_ffn_out", g, rode)
    cut = 25 * p_out.shape[1] // 44
    dh2, rode = carrying(_mm_nt, dff, w_ffn_in, tm=1024, to=1024, tn=w_ffn_in.shape[2], name="d_h2",
                         rider=_join(to_chips(p_out, (0, cut)), to_sibling_acts(h2, dff)))
    p_fin, rode = pair_grad("g_ffn_in", h2, dff, w_ffn_in.shape[2] // 2,
                            to_chips(p_out, (cut, p_out.shape[1] - cut), rode), arrived=rode and rode[1:])
    g_ffn_out = summed("w_ffn_out", p_out, rode)
    if place is None:
        du1, dmix, acc1 = _mid_bwd(dh2, du2, x1, mix, x, gate1, ln1_g, scale2)
    else:
        (du1, dmix, acc1), (g_ffn_out,) = _mid_bwd(dh2, du2, x1, mix, x, gate1, ln1_g, scale2, rider=to_both(g_ffn_out))
    g = blocks(_mm_tn(mixin, dmix, q=1, tk=512, tn=1024, tt=T, name="g_o"))
    dmixin, rode = carrying(_mm_nt, dmix, w_o3, tm=1024, to=512, tn=D, name="d_mixin", rider=to_sibling(g))
    p_o = pair_sum("w_o", g, rode)
    cut = p_fin.shape[1] // 2
    (dq, dk, dv, dbias, dgain), rode = _attn_bwd(qkv, probs, attn_norm_g, dmixin, rider=to_chips(p_fin, (0, cut)))
    (dproj, dl0, dgn), rode = _hgrn_bwd(
        proj, lb_logits, gnorm_g, o_b, st_all, dmixin, (dq, dk, dv),
        rider=_join(to_chips(p_fin, (cut, p_fin.shape[1] - cut), rode), to_chips(p_o)))
    g_ffn_in, g_o = summed("w_ffn_in", p_fin, rode[:1]), summed("w_o", p_o, rode[1:])
    p_in, rode = pair_grad("g_in", h1, dproj, w_in.shape[2] // 2, None,
                           late_rider=_join(to_both(g_ffn_in), to_both(g_o)))
    if place is not None:
        g_ffn_in, g_o = rode
    cut = 3 * p_in.shape[1] // 4
    dh1, rode = carrying(_mm_nt, dproj, w_in, tm=1024, to=1024, tn=w_in.shape[2], name="d_h1",
                         rider=to_chips(p_in, (0, cut)))
    if place is None:
        (grad_x, acc0), g_in = _first_bwd(dh1, du1, x, scale1), p_in
    else:
        (grad_x, acc0), rode = _first_bwd(dh1, du1, x, scale1, rider=to_chips(p_in, (cut, p_in.shape[1] - cut), rode))
        g_in, = _alone(to_both(summed("w_in", p_in, rode)), "w_in_share")
    dmod = jnp.concatenate([acc0[1:2], acc0[0:1], acc1[4:5], acc1[1:2], acc1[0:1], acc2[2:3]], axis=0)
    small = dict(rel_bias=_bias_band_grad(dbias), attn_norm_g=dgain,
                 lb_logits=jnp.concatenate([dl0, -dl0], axis=0), gnorm_g=dgn,
                 ln1_g=acc1[2:3], ln1_b=acc1[3:4], ln2_g=acc2[0:1], ln2_b=acc2[1:2])
    return loss, grad_x, dict(w_in=g_in, w_o=g_o, w_ffn_in=g_ffn_in, w_ffn_out=g_ffn_out), dmod, small


SMALL = ("rel_bias", "attn_norm_g", "lb_logits", "gnorm_g", "ln1_g", "ln1_b", "ln2_g", "ln2_b")
SMALL_ROWS = 256


def kernel(x, c, w_ada, b_ada, w_in, rel_bias, attn_norm_g, lb_logits, gnorm_g, w_o, ln1_g, ln1_b, w_ffn_in, w_ffn_out, ln2_g, ln2_b, loss_target, m_w_ada, m_b_ada, m_w_in, m_rel_bias, m_attn_norm_g, m_lb_logits, m_gnorm_g, m_w_o, m_ln1_g, m_ln1_b, m_w_ffn_in, m_w_ffn_out, m_ln2_g, m_ln2_b, v_w_ada, v_b_ada, v_w_in, v_rel_bias, v_attn_norm_g, v_lb_logits, v_gnorm_g, v_w_o, v_ln1_g, v_ln1_b, v_w_ffn_in, v_w_ffn_out, v_ln2_g, v_ln2_b):
    mx, my, mc = _place()
    me = 4 * mx + 2 * my + mc
    chip = 2 * mx + my
    sel = jnp.stack([chip, mc]).astype(jnp.int32)
    D = x.shape[2]
    ns_ada = w_ada.shape[2]

    big = dict(w_in=(w_in, m_w_in, v_w_in), w_o=(w_o, m_w_o, v_w_o), w_ffn_in=(w_ffn_in, m_w_ffn_in, v_w_ffn_in),
               w_ffn_out=(w_ffn_out, m_w_ffn_out, v_w_ffn_out))
    shards = dict(w_in=w_in[0].astype(BF16))
    kh = shards["w_in"].shape[0] // 2
    cuts = [part * kh // 32 for part in (0, 12, 19, 22, 26, 32)]
    spans = [(a, b - a) for a, b in zip(cuts, cuts[1:])]

    def w_in_part(i, rode):
        return _gather_rider(shards["w_in"], spans[i], rode and rode[0], spans[i - 1] if i else None, last=i == 4,
                             relay=True)

    c_own = c.reshape(D // 128, 128)
    rode, c_all = None, None
    for i, n in enumerate(("w_ffn_in", "w_ffn_out", "w_o")):
        rider = _join(w_in_part(i, rode), None if i else _small_rider(c_own))
        (shards[n],), rode = _to_bf16(big[n][0][0], "cast_" + n, rider)
        c_all = c_all if i else rode[1]

    c_all = lax.dynamic_update_index_in_dim(c_all, c_own, me, 0).reshape(N_DEV, D)
    c_all = jnp.pad(c_all, ((0, 16 - N_DEV), (0, 0)))
    mod_cols, mod_all, rode = _ada_fwd(c_all, w_ada[0], w_in_part(3, rode))
    shards["w_in_last_part"] = w_in_part(4, rode)
    mod_all = lax.dynamic_update_index_in_dim(mod_all, mod_cols[:N_DEV], me, 0)
    mod = lax.dynamic_index_in_dim(mod_all[::2], me, axis=1, keepdims=False)
    mod = (mod.reshape(1, -1) + b_ada).reshape(6, D)

    loss, grad_x, g_big, dmod, g_small = _layer(
        x[0], loss_target[0], mod, shards, rel_bias[0], attn_norm_g, lb_logits, gnorm_g, ln1_g, ln1_b, ln2_g, ln2_b,
        place=(chip, sel))

    grads, deltas, new_m, new_v = {}, {}, {}, {}
    for n, (w, m, v) in big.items():
        g, d, mo, vo = _adamw(g_big[n], w[0], m[0], v[0], "adamw_" + n)
        grads[n], deltas[n], new_m[n], new_v[n] = g[None], d[None], mo[None], vo[None]

    small_in = dict(rel_bias=(rel_bias, m_rel_bias, v_rel_bias), attn_norm_g=(attn_norm_g, m_attn_norm_g, v_attn_norm_g),
                    lb_logits=(lb_logits, m_lb_logits, v_lb_logits), gnorm_g=(gnorm_g, m_gnorm_g, v_gnorm_g),
                    ln1_g=(ln1_g, m_ln1_g, v_ln1_g), ln1_b=(ln1_b, m_ln1_b, v_ln1_b), ln2_g=(ln2_g, m_ln2_g, v_ln2_g),
                    ln2_b=(ln2_b, m_ln2_b, v_ln2_b))
    g_all = _gather_small(_pack([dmod] + [g_small[n] for n in SMALL] + [loss], SMALL_ROWS), "gather_small")
    packed = [_pack([t] + [small_in[n][i] for n in SMALL] + [jnp.zeros((), F32)], SMALL_ROWS)
              for i, t in enumerate((b_ada, m_b_ada, v_b_ada))]
    shapes = [b_ada.shape] + [small_in[n][0].shape for n in SMALL] + [()]
    outs = [_unpack(o, shapes) for o in _small_update(g_all, *packed)]
    loss = outs[0][-1]
    for i, n in enumerate(("b_ada",) + SMALL):
        grads[n], deltas[n], new_m[n], new_v[n] = outs[0][i], outs[1][i], outs[2][i], outs[3][i]

    dmod_all = g_all[:, :6 * D // 128].reshape(N_DEV, 6 * D)
    dmod_cols = lax.dynamic_slice_in_dim(dmod_all, chip * ns_ada, ns_ada, axis=1)
    dmod_cols = jnp.pad(dmod_cols, ((0, 16 - N_DEV), (0, 0)))
    g, d, mo, vo = _ada_bwd(c_all, dmod_cols, w_ada[0], m_w_ada[0], v_w_ada[0])
    grads["w_ada"], deltas["w_ada"], new_m["w_ada"], new_v["w_ada"] = g[None], d[None], mo[None], vo[None]

    order = ("w_ada", "b_ada", "w_in", "rel_bias", "attn_norm_g", "lb_logits", "gnorm_g", "w_o", "ln1_g", "ln1_b",
             "w_ffn_in", "w_ffn_out", "ln2_g", "ln2_b")
    return (loss, grad_x[None], *[grads[n] for n in order], *[deltas[n] for n in order],
            *[new_m[n] for n in order], *[new_v[n] for n in order])
```

```python
import numpy as np
import jax
import jax.numpy as jnp
from jax import lax
from jax.experimental import pallas as pl
from jax.experimental.pallas import tpu as pltpu

F32 = jnp.float32
BF16 = jnp.bfloat16
MESH = pl.DeviceIdType.MESH
HIGHEST = lax.Precision.HIGHEST

CHUNK = 64
N_PAST = 8
QG = 4
QROWS = QG * CHUNK
KPAD = N_PAST * CHUNK
ZPAD = 2 * KPAD
UNION = (QG + N_PAST) * CHUNK
BAND = (N_PAST + 1) * CHUNK
HD_A = 64
HD_B = 128
SUB = 16
HGRN_HEADS = 8
MAX_REL = 256
EPS = 1e-5
ALPHA = 2.0 ** 0.25
LR, B1, B2, ADAM_EPS, WD, STEP = 1e-3, 0.9, 0.999, 1e-8, 0.01, 10
N_CHIPS = 4
N_DEV = 8
NEG = -1e30
TILE_BYTES = 3 << 19

NN = ((1,), (0,))
NT = ((1,), (1,))
TN = ((0,), (0,))


def _dot(a, b, dims=NN, precision=None):
    return lax.dot_general(a, b, (dims, ((), ())), preferred_element_type=F32, precision=precision)


def _params(sem=None, vmem_mb=None, **kw):
    return pltpu.CompilerParams(dimension_semantics=sem,
                                vmem_limit_bytes=None if vmem_mb is None else vmem_mb << 20, **kw)


def _row_tile(rows, cols):
    for cand in (512, 256, 128, 64, 32, 16, 8):
        if rows % cand == 0 and cand * cols * 4 <= TILE_BYTES:
            return cand
    raise ValueError((rows, cols))


def _place():
    return lax.axis_index("x"), lax.axis_index("y"), lax.axis_index("c")


def _flip(v, bit):
    return 1 - v if bit else v


ANY = pl.BlockSpec(memory_space=pl.ANY)
CHIP_FLIPS = ((1, 0), (0, 1), (1, 1))


class _Rider:
    def __init__(self, operands, out_shape, n_sems, start, finish, aliases=None):
        self.operands, self.out_shape, self.n_sems, self.start, self.finish = operands, out_shape, n_sems, start, finish
        self.aliases = aliases or {}


def _call(body, rider, *, name, grid, in_specs, out_specs, out_shape, scratch_shapes, compiler_params, operands):
    if rider is None:
        outs = pl.pallas_call(body, name=name, grid=grid, in_specs=in_specs, out_specs=out_specs, out_shape=out_shape,
                              scratch_shapes=scratch_shapes, compiler_params=compiler_params)(*operands)
        return list(outs), []
    n_in, n_out, n_sc = len(in_specs), len(out_specs), len(scratch_shapes)
    r_in, r_out = len(rider.operands), len(rider.out_shape)

    def carried(*refs):
        refs = list(refs)
        cuts = [n_in, r_in, n_out, r_out, n_sc]
        ins, r_ins, outs, r_outs, scratch = [[refs.pop(0) for _ in range(n)] for n in cuts]
        first, last = None, None
        for axis, size in enumerate(grid):
            i = pl.program_id(axis)
            first = (i == 0) if first is None else first & (i == 0)
            last = (i == size - 1) if last is None else last & (i == size - 1)

        @pl.when(first)
        def _():
            rider.start(r_ins, r_outs, *refs)

        body(*ins, *outs, *scratch)

        @pl.when(last)
        def _():
            rider.finish(r_ins, r_outs, *refs)

    sems = [pltpu.SemaphoreType.DMA((rider.n_sems,)), pltpu.SemaphoreType.DMA((rider.n_sems,))]
    outs = pl.pallas_call(carried, name=name, grid=grid, in_specs=list(in_specs) + [ANY] * r_in,
                          out_specs=list(out_specs) + [ANY] * r_out, out_shape=list(out_shape) + rider.out_shape,
                          scratch_shapes=list(scratch_shapes) + sems, compiler_params=compiler_params,
                          input_output_aliases={n_in + i: n_out + o for i, o in rider.aliases.items()},
                          )(*operands, *rider.operands)
    return list(outs[:n_out]), list(outs[n_out:])


def _alone(rider, name):
    def body(*refs):
        ins, outs, sems = refs[:len(rider.operands)], refs[len(rider.operands):-2], refs[-2:]
        rider.start(ins, outs, *sems)
        rider.finish(ins, outs, *sems)

    return pl.pallas_call(
        body, name=name, in_specs=[ANY] * len(rider.operands), out_specs=[ANY] * len(rider.out_shape),
        out_shape=rider.out_shape, input_output_aliases=rider.aliases,
        scratch_shapes=[pltpu.SemaphoreType.DMA((rider.n_sems,)), pltpu.SemaphoreType.DMA((rider.n_sems,))],
    )(*rider.operands)


class _Sems:
    def __init__(self, sems, base):
        self.sems, self.base = sems, base

    @property
    def at(self):
        return self

    def __getitem__(self, k):
        return self.sems.at[self.base + k]


def _join(*riders):
    riders = [r for r in riders if r is not None]
    if len(riders) < 2:
        return riders[0] if riders else None

    def parts(ins, outs, send_sems, recv_sems):
        i = o = s = 0
        for r in riders:
            ni, no = len(r.operands), len(r.out_shape)
            yield r, ins[i:i + ni], outs[o:o + no], _Sems(send_sems, s), _Sems(recv_sems, s)
            i, o, s = i + ni, o + no, s + r.n_sems

    def start(*refs):
        for r, *args in parts(*refs):
            r.start(*args)

    def finish(*refs):
        for r, *args in parts(*refs):
            r.finish(*args)

    aliases, i, o = {}, 0, 0
    for r in riders:
        aliases.update({i + a: o + b for a, b in r.aliases.items()})
        i, o = i + len(r.operands), o + len(r.out_shape)
    return _Rider([a for r in riders for a in r.operands], [s for r in riders for s in r.out_shape],
                  sum(r.n_sems for r in riders), start, finish, aliases)


def _gather_rider(shard, rows=None, into=None, before=None, last=True, relay=False):
    K, Ns = shard.shape
    kh = K // 2
    rows = rows or (0, kh)

    def copies(w_ref, out_ref, send_sems, recv_sems):
        x, y, c = _place()
        chips = [(_flip(x, fx), _flip(y, fy)) for fx, fy in CHIP_FLIPS]

        def half(chip, which, part):
            return out_ref.at[2 * chip[0] + chip[1], pl.ds(which * kh + part[0], part[1]), :]

        def copy(k, dst, to, src=None):
            return pltpu.make_async_remote_copy(src_ref=dst if src is None else src, dst_ref=dst,
                                                send_sem=send_sems.at[k], recv_sem=recv_sems.at[k],
                                                device_id=to, device_id_type=MESH)

        def first():
            return [copy(j, half((x, y), c, rows), (*chip, c), src=w_ref.at[pl.ds(c * kh + rows[0], rows[1]), :])
                    for j, chip in enumerate(chips[:2] if relay else chips)]

        def onward(base, part):
            return [copy(base + j, half(chip, c, part), (x, y, 1 - c)) for j, chip in enumerate(chips)]

        def arriving(base, which, part):
            return [copy(base + j, half(chip, which, part), (x, y, c)) for j, chip in enumerate(chips)]

        def relayed(to_me, part, k):
            lo, hi = (part[0], part[1] // 2), (part[0] + part[1] // 2, part[1] - part[1] // 2)
            if to_me:
                return [copy(k, half(chips[2], c, lo), (x, y, c)), copy(k + 1, half(chips[2], c, hi), (x, y, c))]
            return [copy(k, half(chips[0], c, lo), (*chips[1], c)), copy(k + 1, half(chips[1], c, hi), (*chips[0], c))]

        def started():
            out = first()
            if before:
                out += onward(3, before)[:2 if relay else 3] + (relayed(False, before, 9) if relay else [])
            return out

        return c, started, onward, arriving, relayed

    def start(ins, outs, send_sems, recv_sems):
        for cp in copies(ins[0], outs[0], send_sems, recv_sems)[1]():
            cp.start()

    def finish(ins, outs, send_sems, recv_sems):
        c, started, onward, arriving, relayed = copies(ins[0], outs[0], send_sems, recv_sems)
        sent = started()
        direct = arriving(0, c, rows)
        passed = onward(6, rows) if last else [None] * 3
        if relay and before:
            for cp in relayed(True, before, 9):
                cp.wait_recv()
            sent.append(onward(3, before)[2])
            sent[-1].start()
        for j in range(2 if relay else 3):
            direct[j].wait_recv()
            if last:
                passed[j].start()
        if relay and last:
            sent += relayed(False, rows, 11)
            for cp in sent[-2:]:
                cp.start()
            for cp in relayed(True, rows, 11):
                cp.wait_recv()
            passed[2].start()
        for arrived in (arriving(3, 1 - c, before) if before else []) + (arriving(6, 1 - c, rows) if last else []):
            arrived.wait_recv()
        for cp in sent + (passed if last else []):
            cp.wait_send()

    full = jax.ShapeDtypeStruct((N_CHIPS, K, Ns), shard.dtype)
    if into is None:
        return _Rider([shard], [full], 13, start, finish)
    return _Rider([shard, into], [full], 13, start, finish, aliases={1: 0})


def _pair_rider(g_full):
    Q, K, Ns = g_full.shape
    kh = K // 2

    def copy(g_ref, got_ref, send_sems, recv_sems):
        x, y, c = _place()
        return pltpu.make_async_remote_copy(src_ref=g_ref.at[:, pl.ds((1 - c) * kh, kh), :], dst_ref=got_ref,
                                            send_sem=send_sems.at[0], recv_sem=recv_sems.at[0],
                                            device_id=(x, y, 1 - c), device_id_type=MESH)

    def start(ins, outs, send_sems, recv_sems):
        copy(ins[0], outs[0], send_sems, recv_sems).start()

    def finish(ins, outs, send_sems, recv_sems):
        copy(ins[0], outs[0], send_sems, recv_sems).wait()

    return _Rider([g_full], [jax.ShapeDtypeStruct((Q, kh, Ns), g_full.dtype)], 1, start, finish)


def _acts_rider(a, b):
    T, K = a.shape
    kh = K // 2

    def copies(ins, outs, send_sems, recv_sems):
        x, y, c = _place()
        pair = [(ins[0].at[:, pl.ds((1 - c) * kh, kh)], outs[0]), (ins[1], outs[1])]
        return [pltpu.make_async_remote_copy(src_ref=src, dst_ref=dst, send_sem=send_sems.at[k], recv_sem=recv_sems.at[k],
                                             device_id=(x, y, 1 - c), device_id_type=MESH)
                for k, (src, dst) in enumerate(pair)]

    def start(*refs):
        for cp in copies(*refs):
            cp.start()

    def finish(*refs):
        for cp in copies(*refs):
            cp.wait()

    return _Rider([a, b], [jax.ShapeDtypeStruct((T, kh), a.dtype), jax.ShapeDtypeStruct(b.shape, b.dtype)], 2,
                  start, finish)


def _share_rider(block):
    K, Ns = block.shape
    kh = K // 2

    def halves(out_ref):
        x, y, c = _place()
        return out_ref.at[pl.ds(c * kh, kh), :], out_ref.at[pl.ds((1 - c) * kh, kh), :], (x, y, 1 - c)

    def start(ins, outs, send_sems, recv_sems):
        mine, _, sibling = halves(outs[0])
        pltpu.make_async_remote_copy(src_ref=mine, dst_ref=mine, send_sem=send_sems.at[0], recv_sem=recv_sems.at[0],
                                     device_id=sibling, device_id_type=MESH).start()

    def finish(ins, outs, send_sems, recv_sems):
        mine, theirs, sibling = halves(outs[0])
        pltpu.make_async_remote_copy(src_ref=theirs, dst_ref=theirs, send_sem=send_sems.at[0], recv_sem=recv_sems.at[0],
                                     device_id=sibling, device_id_type=MESH).wait_recv()
        pltpu.make_async_remote_copy(src_ref=mine, dst_ref=mine, send_sem=send_sems.at[0], recv_sem=recv_sems.at[0],
                                     device_id=sibling, device_id_type=MESH).wait_send()

    return _Rider([block], [jax.ShapeDtypeStruct((K, Ns), block.dtype)], 1, start, finish, aliases={0: 0})


def _chips_rider(pair_sum, rows=None, into=None):
    Q, kh, Ns = pair_sum.shape
    first_row, n_rows = rows or (0, kh)

    def copies(p_ref, got_ref, send_sems, recv_sems):
        x, y, c = _place()
        part = pl.ds(first_row, n_rows)
        out = []
        for j, (fx, fy) in enumerate(CHIP_FLIPS):
            px, py = _flip(x, fx), _flip(y, fy)
            out.append(pltpu.make_async_remote_copy(
                src_ref=p_ref.at[2 * px + py, part, :], dst_ref=got_ref.at[j, part, :], send_sem=send_sems.at[j],
                recv_sem=recv_sems.at[j], device_id=(px, py, c), device_id_type=MESH))
        return out

    def start(ins, outs, send_sems, recv_sems):
        for cp in copies(ins[0], outs[0], send_sems, recv_sems):
            cp.start()

    def finish(ins, outs, send_sems, recv_sems):
        sends = copies(ins[0], outs[0], send_sems, recv_sems)
        for cp in sends:
            cp.wait_recv()
        for cp in sends:
            cp.wait_send()

    got = jax.ShapeDtypeStruct((Q - 1, kh, Ns), pair_sum.dtype)
    if into is None:
        return _Rider([pair_sum], [got], 3, start, finish)
    return _Rider([pair_sum, into], [got], 3, start, finish, aliases={1: 0})


def _mm(a, b, *, grid, a_spec, b_spec, o_spec, o_shape, o_dtype, dims, acc_shape, name, rider=None, zero_rows=0,
        vmem_mb=48):
    nk = grid[2]

    def body(a_ref, b_ref, o_ref, *scratch):
        if zero_rows:
            @pl.when(pl.program_id(0) < zero_rows)
            def _():
                o_ref[...] = jnp.zeros_like(o_ref)

            @pl.when(pl.program_id(0) >= zero_rows)
            def _():
                o_ref[...] = _dot(a_ref[...], b_ref[...], dims).astype(o_ref.dtype)
            return
        part = _dot(a_ref[...], b_ref[...], dims)
        if nk == 1:
            o_ref[...] = part.astype(o_ref.dtype)
            return
        acc_ref, = scratch
        k = pl.program_id(2)

        @pl.when(k == 0)
        def _():
            acc_ref[...] = part

        @pl.when(k > 0)
        def _():
            acc_ref[...] += part

        @pl.when(k == nk - 1)
        def _():
            o_ref[...] = acc_ref[...].astype(o_ref.dtype)

    (out,), rode = _call(
        body, rider, name=name, grid=grid, in_specs=[a_spec, b_spec], out_specs=[o_spec],
        out_shape=[jax.ShapeDtypeStruct(o_shape, o_dtype)],
        scratch_shapes=[] if nk == 1 else [pltpu.VMEM(acc_shape, F32)],
        compiler_params=_params(("parallel", "parallel", "arbitrary") if rider is None else ("arbitrary",) * 3, vmem_mb),
        operands=(a, b))
    return out if rider is None else (out, rode)


def _mm_nn(a, w, *, tm, tn, tk, name, rider=None, cols=None, o_dtype=F32, pad_rows=0):
    T, K = a.shape
    Q, _, Ns = w.shape
    nbs = Ns // tn
    tm = min(tm, T)
    j0, j1 = cols or (0, Q * nbs)
    lead = pad_rows // tm
    return _mm(a, w, grid=(lead + T // tm, j1 - j0, K // tk),
               a_spec=pl.BlockSpec((tm, tk), lambda i, j, k: (jnp.maximum(i - lead, 0), k)),
               b_spec=pl.BlockSpec((None, tk, tn), lambda i, j, k: ((j + j0) // nbs, k, (j + j0) % nbs)),
               o_spec=pl.BlockSpec((tm, tn), lambda i, j, k: (i, j)),
               o_shape=(pad_rows + T, (j1 - j0) * tn), o_dtype=o_dtype, dims=NN, acc_shape=(tm, tn), name=name,
               rider=rider, zero_rows=lead)


def _col_blocks(g, rows, tn, at):
    if g.ndim == 2:
        return pl.BlockSpec((rows, tn), at)
    per = g.shape[2] // tn

    def stacked(*idx):
        r, c = at(*idx)
        return c // per, r, c % per

    return pl.BlockSpec((None, rows, tn), stacked)


def _mm_nt(g, w, *, tm, to, tn, name, rider=None):
    T = g.shape[-2]
    Q, K, Ns = w.shape
    nbs = Ns // tn
    tm = min(tm, T)
    return _mm(g, w, grid=(T // tm, K // to, Q * nbs),
               a_spec=_col_blocks(g, tm, tn, lambda i, j, n: (i, n)),
               b_spec=pl.BlockSpec((None, to, tn), lambda i, j, n: (n // nbs, j, n % nbs)),
               o_spec=pl.BlockSpec((tm, to), lambda i, j, n: (i, j)),
               o_shape=(T, K), o_dtype=F32, dims=NT, acc_shape=(tm, to), name=name, rider=rider)


def _mm_tn(a, g, *, q, tk, tn, tt, name, rider=None):
    T, K = a.shape
    Ns = g.shape[-1] * (g.ndim - 1) // q
    nbs = Ns // tn
    return _mm(a, g, grid=(K // tk, q * nbs, T // tt),
               a_spec=pl.BlockSpec((tt, tk), lambda i, j, t: (t, i)),
               b_spec=_col_blocks(g, tt, tn, lambda i, j, t: (t, j)),
               o_spec=pl.BlockSpec((None, tk, tn), lambda i, j, t: (j // nbs, i, j % nbs)),
               o_shape=(q, K, Ns), o_dtype=BF16, dims=TN, acc_shape=(tk, tn), name=name, rider=rider)


def _mm_tn_add(a, g, part, *, tk, tn, name, rider=None):
    T, K = a.shape
    Q, _, Ns = part.shape
    nbs = Ns // tn

    def body(a_ref, g_ref, p_ref, o_ref):
        o_ref[...] = (_dot(a_ref[...], g_ref[...], TN) + p_ref[...].astype(F32)).astype(o_ref.dtype)

    blk = pl.BlockSpec((None, tk, tn), lambda i, j: (j // nbs, i, j % nbs))
    (out,), rode = _call(
        body, rider, name=name, grid=(K // tk, Q * nbs),
        in_specs=[pl.BlockSpec((T, tk), lambda i, j: (0, i)), _col_blocks(g, T, tn, lambda i, j: (0, j)), blk],
        out_specs=[blk], out_shape=[jax.ShapeDtypeStruct((Q, K, Ns), BF16)], scratch_shapes=[],
        compiler_params=_params(("arbitrary", "arbitrary"), 48), operands=(a, g, part))
    return out, rode


def _ln(u):
    mu = jnp.mean(u, axis=-1, keepdims=True)
    d = u - mu
    r = lax.rsqrt(jnp.mean(d * d, axis=-1, keepdims=True) + EPS)
    return d * r, r


def _ln_bwd(dy, un, r):
    return r * (dy - jnp.mean(dy, axis=-1, keepdims=True) - un * jnp.mean(dy * un, axis=-1, keepdims=True))


def _colsum(v):
    return jnp.sum(v, axis=0, keepdims=True)


def _rowwise(name, fn, bigs, vecs, out_dtypes, n_acc, tm=128, rider=None):
    T, D = bigs[0].shape
    nb, nv, no = len(bigs), len(vecs), len(out_dtypes)

    def body(*refs):
        outs, accs = fn([r[...] for r in refs[:nb]], [r[...] for r in refs[nb:nb + nv]])
        for r, o in zip(refs[nb + nv:nb + nv + no], outs):
            r[...] = o.astype(r.dtype)
        if n_acc:
            acc_ref = refs[nb + nv + no]

            @pl.when(pl.program_id(0) == 0)
            def _():
                acc_ref[...] = jnp.zeros_like(acc_ref)

            for row, a in enumerate(accs):
                acc_ref[row:row + 1, :] += a

    big_spec = pl.BlockSpec((tm, D), lambda i: (i, 0))
    vec_spec = pl.BlockSpec((1, D), lambda i: (0, 0))
    out_shape = [jax.ShapeDtypeStruct((T, D), dt) for dt in out_dtypes]
    out_specs = [big_spec] * no
    if n_acc:
        out_shape.append(jax.ShapeDtypeStruct((8, D), F32))
        out_specs.append(pl.BlockSpec((8, D), lambda i: (0, 0)))
    outs, rode = _call(
        body, rider, name=name, grid=(T // tm,), in_specs=[big_spec] * nb + [vec_spec] * nv,
        out_specs=out_specs, out_shape=out_shape, scratch_shapes=[],
        compiler_params=_params(("arbitrary",), 48), operands=(*bigs, *vecs))
    return outs if rider is None else (outs, rode)


def _to_bf16(w, name, rider=None):
    R, C = w.shape
    tr = _row_tile(R, C)

    def body(w_ref, o_ref):
        o_ref[...] = w_ref[...].astype(o_ref.dtype)

    blk = pl.BlockSpec((tr, C), lambda i: (i, 0))
    return _call(body, rider, name=name, grid=(R // tr,), in_specs=[blk], out_specs=[blk],
                 out_shape=[jax.ShapeDtypeStruct((R, C), BF16)], scratch_shapes=[],
                 compiler_params=_params(("arbitrary",)), operands=(w,))


def _pre_mixer(x, scale1, shift1, rider=None):
    def fn(b, v):
        xn, _ = _ln(b[0])
        return [xn * (1.0 + v[0]) + v[1]], []
    outs = _rowwise("pre_mixer", fn, [x], [scale1, shift1], [BF16], 0, rider=rider)
    return outs[0] if rider is None else (outs[0][0], outs[1])


def _post_mixer(mix, x, gate1, g1, b1, scale2, shift2, rider=None):
    def fn(b, v):
        un1, _ = _ln(ALPHA * b[1] + v[0] * b[0])
        x1 = un1 * v[1] + v[2]
        xn1, _ = _ln(x1)
        return [x1, xn1 * (1.0 + v[3]) + v[4]], []
    return _rowwise("post_mixer", fn, [mix, x], [gate1, g1, b1, scale2, shift2], [F32, BF16], 0, rider=rider)


def _loss_head(f, x1, tgt, gate2, g2, b2):
    def fn(b, v):
        ff, xx, tt = b
        d_model = ff.shape[-1]
        un2, r2 = _ln(ALPHA * xx + v[0] * ff)
        err = un2 * v[1] + v[2] - tt
        dy = err * (1.0 / d_model)
        du2 = _ln_bwd(dy * v[1], un2, r2)
        return [du2, du2 * v[0]], [_colsum(dy * un2), _colsum(dy), _colsum(du2 * ff), _colsum(err * err)]
    return _rowwise("loss_head", fn, [f, x1, tgt], [gate2, g2, b2], [F32, BF16], 4)


def _mid_bwd(dh2, du2, x1, mix, x, gate1, g1, scale2, rider=None):
    def fn(b, v):
        dh, du, xx1, mm, xx = b
        xn1, r1n = _ln(xx1)
        dx1 = ALPHA * du + _ln_bwd(dh * (1.0 + v[2]), xn1, r1n)
        un1, r1 = _ln(ALPHA * xx + v[0] * mm)
        du1 = _ln_bwd(dx1 * v[1], un1, r1)
        return [du1, du1 * v[0]], [_colsum(dh * xn1), _colsum(dh), _colsum(dx1 * un1), _colsum(dx1),
                                   _colsum(du1 * mm)]
    return _rowwise("mid_bwd", fn, [dh2, du2, x1, mix, x], [gate1, g1, scale2], [F32, BF16], 5, rider=rider)


def _first_bwd(dh1, du1, x, scale1, rider=None):
    def fn(b, v):
        dh, du, xx = b
        xn, r0 = _ln(xx)
        return [ALPHA * du + _ln_bwd(dh * (1.0 + v[0]), xn, r0)], [_colsum(dh * xn), _colsum(dh)]
    return _rowwise("first_bwd", fn, [dh1, du1, x], [scale1], [F32], 2, rider=rider)


def _ffn_in_swiglu(h2, w, *, tm, tn, rider=None):
    T, K = h2.shape
    Q, _, Ns = w.shape
    nbs = Ns // tn
    half = Q * nbs // 2
    tm = min(tm, T)

    def body(a_ref, wg_ref, wu_ref, g_ref, u_ref, act_ref):
        a = a_ref[...]
        g, u = _dot(a, wg_ref[...]), _dot(a, wu_ref[...])
        g_ref[...] = g.astype(g_ref.dtype)
        u_ref[...] = u.astype(u_ref.dtype)
        act_ref[...] = (g * jax.nn.sigmoid(g) * u).astype(act_ref.dtype)

    cols = lambda first: pl.BlockSpec((None, K, tn), lambda i, j: ((j + first) // nbs, 0, (j + first) % nbs))
    blk = pl.BlockSpec((tm, tn), lambda i, j: (i, j))
    return _call(
        body, rider, name="ffn_in", grid=(T // tm, half),
        in_specs=[pl.BlockSpec((tm, K), lambda i, j: (i, 0)), cols(0), cols(half)], out_specs=[blk] * 3,
        out_shape=[jax.ShapeDtypeStruct((T, half * tn), BF16)] * 3, scratch_shapes=[],
        compiler_params=_params(("arbitrary", "arbitrary"), 48), operands=(h2, w, w))


def _d_act_swiglu(df, w, gate, up, *, tm, to, rider=None):
    T, N = df.shape
    F = w.shape[1]
    tm = min(tm, T)

    def body(df_ref, w_ref, g_ref, u_ref, o_ref):
        d = _dot(df_ref[...], w_ref[...], NT)
        g = g_ref[...].astype(F32)
        s = jax.nn.sigmoid(g)
        o_ref[0] = (d * u_ref[...].astype(F32) * s * (1.0 + g * (1.0 - s))).astype(o_ref.dtype)
        o_ref[1] = (d * g * s).astype(o_ref.dtype)

    blk = pl.BlockSpec((tm, to), lambda i, j: (i, j))
    return _call(
        body, rider, name="d_act", grid=(T // tm, F // to),
        in_specs=[pl.BlockSpec((tm, N), lambda i, j: (i, 0)), pl.BlockSpec((None, to, N), lambda i, j: (0, j, 0)), blk, blk],
        out_specs=[pl.BlockSpec((2, tm, to), lambda i, j: (0, i, j))],
        out_shape=[jax.ShapeDtypeStruct((2, T, F), BF16)], scratch_shapes=[],
        compiler_params=_params(("arbitrary", "arbitrary"), 48), operands=(df, w, gate, up))


PAIR = 2


def _fill_table(table_ref, band_ref):
    table_ref[...] = jnp.full(table_ref.shape, NEG, F32)
    for e in range(PAIR):
        for g in range(QG):
            table_ref[e, g * CHUNK:(g + 1) * CHUNK, g * CHUNK:g * CHUNK + BAND] = band_ref[e]


def _attn_probs(q_ref, k_ref, bias_ref, e, step):
    start = pl.multiple_of(step * QROWS, QROWS)
    lanes = pl.ds(e * HD_A, HD_A)
    s = _dot(q_ref[:, lanes], k_ref[pl.ds(start + ZPAD - KPAD, UNION), lanes], NT) * (HD_A ** -0.5) + bias_ref[e]
    col = lax.broadcasted_iota(jnp.int32, s.shape, 1)
    s = jnp.where(col + start >= KPAD, s, NEG)
    p = jnp.exp(s - jnp.max(s, axis=-1, keepdims=True))
    return p / jnp.sum(p, axis=-1, keepdims=True), start


def _attn_specs(T, n_pairs):
    wide = PAIR * HD_A
    per_step = pl.BlockSpec((QROWS, wide), lambda hp, n: (n, hp))
    queries = pl.BlockSpec((QROWS, wide), lambda hp, n: (n + ZPAD // QROWS, hp))
    keys = pl.BlockSpec((ZPAD + T, wide), lambda hp, n: (0, n_pairs + hp))
    values = pl.BlockSpec((ZPAD + T, wide), lambda hp, n: (0, 2 * n_pairs + hp))
    grads = pl.BlockSpec((T, wide), lambda hp, n: (0, hp))
    table = pl.BlockSpec((PAIR, CHUNK, BAND), lambda hp, n: (hp, 0, 0))
    vec = pl.BlockSpec((1, wide), lambda hp, n: (0, hp))
    return per_step, queries, keys, values, grads, table, vec


def _probs_spec():
    return pl.BlockSpec((PAIR, QROWS, UNION), lambda hp, n: (hp, n, 0))


def _attn_fwd(qkv, bias, gain, rider=None):
    T = qkv.shape[0] - ZPAD
    W = gain.shape[1]
    n_pairs = W // (PAIR * HD_A)

    def body(q_ref, k_ref, v_ref, band_ref, gain_ref, o_ref, p_ref, table_ref):
        @pl.when(pl.program_id(1) == 0)
        def _():
            _fill_table(table_ref, band_ref)

        for e in range(PAIR):
            lanes = pl.ds(e * HD_A, HD_A)
            p, start = _attn_probs(q_ref, k_ref, table_ref, e, pl.program_id(1))
            p_ref[e] = p.astype(p_ref.dtype)
            o = _dot(p_ref[e], v_ref[pl.ds(start + ZPAD - KPAD, UNION), lanes])
            rr = lax.rsqrt(jnp.mean(o * o, axis=-1, keepdims=True) + EPS)
            o_ref[:, lanes] = (o * rr * gain_ref[:, lanes]).astype(o_ref.dtype)

    per_step, queries, keys, values, _, table, vec = _attn_specs(T, n_pairs)
    return _call(
        body, rider, name="attn_fwd", grid=(n_pairs, T // QROWS), in_specs=[queries, keys, values, table, vec],
        out_specs=[per_step, _probs_spec()],
        out_shape=[jax.ShapeDtypeStruct((T, W), BF16), jax.ShapeDtypeStruct((n_pairs * PAIR, T, UNION), BF16)],
        scratch_shapes=[pltpu.VMEM((PAIR, QROWS, UNION), F32)],
        compiler_params=_params(("arbitrary", "arbitrary"), 40), operands=(qkv, qkv, qkv, bias, gain))


def _attn_bwd(qkv, probs, gain, dmixin, rider=None):
    T = qkv.shape[0] - ZPAD
    W = gain.shape[1]
    n_pairs = W // (PAIR * HD_A)
    scale = HD_A ** -0.5

    def body(q_ref, k_ref, v_ref, p_ref, gain_ref, don_ref, dq_ref, dkb_ref, dvb_ref, dband_ref, dgain_ref,
             dtable_ref, dk_ref, dv_ref):
        n = pl.program_id(1)

        @pl.when(n == 0)
        def _():
            dk_ref[...] = jnp.zeros_like(dk_ref)
            dv_ref[...] = jnp.zeros_like(dv_ref)
            dtable_ref[...] = jnp.zeros_like(dtable_ref)
            dgain_ref[...] = jnp.zeros_like(dgain_ref)

        for e in range(PAIR):
            lanes = pl.ds(e * HD_A, HD_A)
            start = pl.multiple_of(n * QROWS, QROWS)
            keys, in_qkv = pl.ds(start, UNION), pl.ds(start + ZPAD - KPAD, UNION)
            pb = p_ref[e]
            p = pb.astype(F32)
            vb = v_ref[in_qkv, lanes]
            o = _dot(pb, vb)
            rr = lax.rsqrt(jnp.mean(o * o, axis=-1, keepdims=True) + EPS)
            on = o * rr
            d_on = don_ref[:, lanes]
            dgain_ref[:, lanes] += _colsum(d_on * on)
            dyo = d_on * gain_ref[:, lanes]
            do = rr * (dyo - on * jnp.mean(dyo * on, axis=-1, keepdims=True))
            dob = do.astype(BF16)
            dp = _dot(dob, vb, NT)
            ds = p * (dp - jnp.sum(do * o, axis=-1, keepdims=True))
            dtable_ref[e] += ds
            dsb = ds.astype(BF16)
            dq_ref[:, lanes] = (_dot(dsb, k_ref[in_qkv, lanes]) * scale).astype(dq_ref.dtype)
            dk_ref[keys, lanes] += _dot(dsb, q_ref[:, lanes], TN) * scale
            dv_ref[keys, lanes] += _dot(pb, dob, TN)

        @pl.when(n == T // QROWS - 1)
        def _():
            for e in range(PAIR):
                dband_ref[e] = sum(dtable_ref[e, g * CHUNK:(g + 1) * CHUNK, g * CHUNK:g * CHUNK + BAND]
                                   for g in range(QG))
            dkb_ref[...] = dk_ref[KPAD:, :].astype(dkb_ref.dtype)
            dvb_ref[...] = dv_ref[KPAD:, :].astype(dvb_ref.dtype)

    per_step, queries, keys, values, grads, table, vec = _attn_specs(T, n_pairs)
    H = n_pairs * PAIR
    return _call(
        body, rider, name="attn_bwd", grid=(n_pairs, T // QROWS),
        in_specs=[queries, keys, values, _probs_spec(), vec, per_step],
        out_specs=[per_step, grads, grads, table, vec],
        out_shape=[jax.ShapeDtypeStruct((T, W), BF16)] * 3 + [jax.ShapeDtypeStruct((H, CHUNK, BAND), F32),
                                                              jax.ShapeDtypeStruct((1, W), F32)],
        scratch_shapes=[pltpu.VMEM((PAIR, QROWS, UNION), F32)] + [pltpu.VMEM((KPAD + T, PAIR * HD_A), F32)] * 2,
        compiler_params=_params(("arbitrary", "arbitrary"), 40),
        operands=(qkv, qkv, qkv, probs, gain, dmixin))


N_DIAG = CHUNK + BAND - 1


def _bias_band(rel_bias):
    H = rel_bias.shape[0]
    idx = np.clip(BAND - 1 - np.arange(N_DIAG), -MAX_REL, MAX_REL) + MAX_REL
    rolled = rel_bias[:, idx[(np.arange(N_DIAG) + CHUNK - 1) % N_DIAG]]
    flat = jnp.broadcast_to(rolled[:, None, :], (H, CHUNK, N_DIAG)).reshape(H, CHUNK * N_DIAG)
    return flat[:, :CHUNK * (N_DIAG - 1)].reshape(H, CHUNK, N_DIAG - 1)[:, :, :BAND]


def _bias_band_grad(dband):
    H = dband.shape[0]
    skew = jnp.pad(dband, ((0, 0), (0, 0), (CHUNK - 1, 0))).reshape(H, CHUNK * N_DIAG)
    skew = jnp.pad(skew, ((0, 0), (0, CHUNK))).reshape(H, CHUNK, N_DIAG + 1)
    diag = jnp.sum(skew, axis=1)[:, :N_DIAG]
    n_far = BAND - MAX_REL
    far = jnp.sum(diag[:, :n_far], axis=1, keepdims=True)
    near = diag[:, n_far:][:, ::-1]
    zeros = jnp.zeros((H, MAX_REL - (CHUNK - 1)), F32)
    return jnp.concatenate([zeros, near, far], axis=1)


def _tri(n, lower):
    r = lax.broadcasted_iota(jnp.int32, (n, n), 0)
    c = lax.broadcasted_iota(jnp.int32, (n, n), 1)
    return jnp.where((c <= r) if lower else (c >= r), 1.0, 0.0).astype(F32)


def _hgrn_gates(zq_ref, zf_ref, lbl_ref, q_s, k_s, b_s):
    lb = jax.nn.sigmoid(lbl_ref[0:1, :] - lbl_ref[1:2, :])
    zq = zq_ref[...]
    sig = jax.nn.sigmoid(zf_ref[...])
    f = lb + (1.0 - lb) * sig
    sq = jax.nn.sigmoid(zq)
    q_s[...] = zq * sq
    k_s[...] = 1.0 - f
    b_s[...] = _dot(_tri(CHUNK, True), jnp.log(f), precision=HIGHEST)
    return lb, sig, f, sq


def _sub_rows(i):
    return pl.ds(i * SUB, SUB)


def _row_mask(s):
    return lax.broadcasted_iota(jnp.int32, (SUB, HD_B), 0) >= s


def _decay_from(b_sub, b_row, s):
    return jnp.where(_row_mask(s), jnp.exp(jnp.minimum(b_sub - b_row, 0.0)), 0.0)


def _hgrn_fwd(proj, lb_logits, gnorm_g, rider=None):
    T = proj.shape[0]
    nC = T // CHUNK
    W = lb_logits.shape[1]
    G = W // HD_B // HGRN_HEADS
    col0 = (proj.shape[1] - 4 * W) // (HD_B * HGRN_HEADS)
    wide = HGRN_HEADS * HD_B

    def body(*refs):
        @pl.when(pl.program_id(1) == 0)
        def _():
            refs[9][...] = jnp.zeros_like(refs[9])

        for h in range(HGRN_HEADS):
            lanes = pl.ds(h * HD_B, HD_B)
            one_head(*[r.at[:, lanes] for r in refs[:5]], refs[5], *[r.at[:, lanes] for r in refs[6:8]],
                     *[r.at[h] for r in refs[8:]])

    def one_head(zq_ref, zf_ref, xi_ref, zg_ref, lbl_ref, gn_ref, mix_ref, o_ref, stall_ref, st_ref, q_s, k_s, b_s, acc_s):
        _hgrn_gates(zq_ref, zf_ref, lbl_ref, q_s, k_s, b_s)
        q, k, b = q_s[...], k_s[...], b_s[...]
        st = st_ref[...]
        stall_ref[...] = st
        b_last = b_s[CHUNK - 1:CHUNK, :]
        acc_s[...] = _dot((q * jnp.exp(b)).astype(BF16), st.astype(BF16), NT)
        for i in range(CHUNK // SUB):
            rows = _sub_rows(i)
            q_i, b_i = q_s[rows, :], b_s[rows, :]
            acc = jnp.zeros((SUB, HD_B), F32)
            if i:
                past = pl.ds(0, i * SUB)
                b_ref = b_s[i * SUB - 1:i * SUB, :]
                qs = (q_i * jnp.exp(b_i - b_ref)).astype(BF16)
                ks = (k_s[past, :] * jnp.exp(b_ref - b_s[past, :])).astype(BF16)
                acc += _dot(_dot(qs, ks, NT).astype(BF16), xi_ref[past, :].astype(BF16))
            for s in range(SUB):
                row = pl.ds(i * SUB + s, 1)
                w = q_i * _decay_from(b_i, b_s[row, :], s)
                acc += jnp.sum(w * k_s[row, :], axis=-1, keepdims=True) * xi_ref[row, :]
            acc_s[rows, :] += acc
        o = acc_s[...]
        kd = (k * jnp.exp(b_last - b)).astype(BF16)
        st_ref[...] = st * jnp.exp(b_last) + _dot(xi_ref[...].astype(BF16), kd, TN)
        o_ref[...] = o
        zg = zg_ref[...]
        rr = lax.rsqrt(jnp.mean(o * o, axis=-1, keepdims=True) + EPS)
        mix_ref[...] = (o * rr * gn_ref[...] * (zg * jax.nn.sigmoid(zg))).astype(mix_ref.dtype)

    col = lambda part: pl.BlockSpec((CHUNK, wide), lambda g, n: (n, col0 + part * G + g))
    out_blk = pl.BlockSpec((CHUNK, wide), lambda g, n: (n, g))
    tile = pltpu.VMEM((HGRN_HEADS, CHUNK, HD_B), F32)
    return _call(
        body, rider, name="hgrn_fwd", grid=(G, nC),
        in_specs=[col(0), col(1), col(2), col(3), pl.BlockSpec((2, wide), lambda g, n: (0, g)),
                  pl.BlockSpec((1, HD_B), lambda g, n: (0, 0))],
        out_specs=[out_blk, out_blk, pl.BlockSpec((HGRN_HEADS, None, HD_B, HD_B), lambda g, n: (g, n, 0, 0))],
        out_shape=[jax.ShapeDtypeStruct((T, W), BF16), jax.ShapeDtypeStruct((T, W), F32),
                   jax.ShapeDtypeStruct((G * HGRN_HEADS, nC, HD_B, HD_B), F32)],
        scratch_shapes=[pltpu.VMEM((HGRN_HEADS, HD_B, HD_B), F32), tile, tile, tile, tile],
        compiler_params=_params(("arbitrary", "arbitrary")),
        operands=(proj, proj, proj, proj, lb_logits, gnorm_g))


def _hgrn_bwd(proj, lb_logits, gnorm_g, o_b, st_all, dmixin, d_attn, rider=None):
    T = proj.shape[0]
    nC = T // CHUNK
    W = lb_logits.shape[1]
    wa = d_attn[0].shape[1]
    assert W == HGRN_HEADS * HD_B, "one grid step takes every head: it writes whole rows of d proj"
    G = W // HD_B // HGRN_HEADS
    wide = HGRN_HEADS * HD_B
    col0 = (proj.shape[1] - 4 * W) // wide
    dcol0 = (dmixin.shape[1] - W) // wide

    def body(*refs):
        g, n = pl.program_id(0), pl.program_id(1)
        dproj_ref, dl0_ref, dgn_ref, dst_ref = refs[12:16]
        for i in range(3):
            dproj_ref[:, i * wa:(i + 1) * wa] = refs[9 + i][...]

        @pl.when(n == 0)
        def _():
            dst_ref[...] = jnp.zeros_like(dst_ref)
            dl0_ref[...] = jnp.zeros_like(dl0_ref)

        @pl.when((n == 0) & (g == 0))
        def _():
            dgn_ref[...] = jnp.zeros_like(dgn_ref)

        for h in range(HGRN_HEADS):
            lanes = pl.ds(h * HD_B, HD_B)
            cut = lambda r: r.at[:, lanes]
            parts = [dproj_ref.at[:, pl.ds(3 * wa + part * W + h * HD_B, HD_B)] for part in range(4)]
            one_head(*[cut(r) for r in refs[:5]], refs[5], cut(refs[6]), refs[7].at[h], cut(refs[8]),
                     *parts, cut(dl0_ref), dgn_ref, *[r.at[h] for r in refs[15:]])

    def one_head(zq_ref, zf_ref, xi_ref, zg_ref, lbl_ref, gn_ref, o_ref, st_ref, dout_ref,
                 dzq_ref, dzf_ref, dxi_ref, dzg_ref, dl0_ref, dgn_ref, dst_ref, q_s, k_s, b_s, do_s, dq_s, dk_s, di_s):
        lb, sig, f, sq = _hgrn_gates(zq_ref, zf_ref, lbl_ref, q_s, k_s, b_s)
        q, k, b = q_s[...], k_s[...], b_s[...]
        zg, o, dout = zg_ref[...], o_ref[...], dout_ref[...]
        sg = jax.nn.sigmoid(zg)
        rr = lax.rsqrt(jnp.mean(o * o, axis=-1, keepdims=True) + EPS)
        on = o * rr
        gn = gn_ref[...]
        dzg_ref[...] = (dout * on * gn * sg * (1.0 + zg * (1.0 - sg))).astype(dzg_ref.dtype)
        d_on = dout * zg * sg
        dgn_ref[...] += _colsum(d_on * on)
        d_on = d_on * gn
        do = rr * (d_on - on * jnp.mean(d_on * on, axis=-1, keepdims=True))
        do_s[...] = do
        dob = do.astype(BF16)
        st, dst = st_ref[...], dst_ref[...]
        b_last = b_s[CHUNK - 1:CHUNK, :]
        eb, e_last, k_dec = jnp.exp(b), jnp.exp(b_last), jnp.exp(b_last - b)
        qt, kd = q * eb, k * k_dec
        dstb = dst.astype(BF16)
        xib = xi_ref[...].astype(BF16)
        d_kd = _dot(xib, dstb)
        dq_s[...] = _dot(dob, st.astype(BF16)) * eb
        dk_s[...] = d_kd * k_dec
        di_s[...] = _dot(kd.astype(BF16), dstb, NT)
        d_b_last = e_last * _colsum(st * dst) + _colsum(d_kd * kd)
        dst_ref[...] = _dot(dob, qt.astype(BF16), TN) + dst * e_last
        for i in range(CHUNK // SUB):
            rows = _sub_rows(i)
            q_i, b_i, do_i = q_s[rows, :], b_s[rows, :], do_s[rows, :]
            dq_i = jnp.zeros((SUB, HD_B), F32)
            if i:
                past = pl.ds(0, i * SUB)
                b_ref = b_s[i * SUB - 1:i * SUB, :]
                e_q, e_k = jnp.exp(b_i - b_ref), jnp.exp(b_ref - b_s[past, :])
                qs, ks = (q_i * e_q).astype(BF16), (k_s[past, :] * e_k).astype(BF16)
                xi_p, do_b = xi_ref[past, :].astype(BF16), do_i.astype(BF16)
                di_s[past, :] += _dot(_dot(ks, qs, NT).astype(BF16), do_b)
                dq_i += _dot(_dot(do_b, xi_p, NT).astype(BF16), ks) * e_q
                dk_s[past, :] += _dot(_dot(xi_p, do_b, NT).astype(BF16), qs) * e_k
            for s in range(SUB):
                row = pl.ds(i * SUB + s, 1)
                k_row, i_row = k_s[row, :], xi_ref[row, :]
                e = _decay_from(b_i, b_s[row, :], s)
                w = q_i * e
                a_col = jnp.sum(w * k_row, axis=-1, keepdims=True)
                da_col = jnp.sum(do_i * i_row, axis=-1, keepdims=True)
                di_s[row, :] += _colsum(a_col * do_i)
                dq_i += da_col * e * k_row
                dk_s[row, :] += _colsum(da_col * w)
            dq_s[rows, :] += dq_i
        dq, dk = dq_s[...], dk_s[...]
        db = q * dq - k * dk
        is_last = lax.broadcasted_iota(jnp.int32, (CHUNK, HD_B), 0) == CHUNK - 1
        db = db + jnp.where(is_last, d_b_last, 0.0)
        df = _dot(_tri(CHUNK, False), db, precision=HIGHEST) / f - dk
        dzf_ref[...] = (df * (1.0 - lb) * sig * (1.0 - sig)).astype(dzf_ref.dtype)
        dl0_ref[...] += _colsum(df * (1.0 - sig)) * (lb * (1.0 - lb))
        zq = zq_ref[...]
        dzq_ref[...] = (dq * sq * (1.0 + zq * (1.0 - sq))).astype(dzq_ref.dtype)
        dxi_ref[...] = di_s[...].astype(dxi_ref.dtype)

    rev = lambda n: nC - 1 - n
    col = lambda part: pl.BlockSpec((CHUNK, wide), lambda g, n: (rev(n), col0 + part * G + g))
    blk = pl.BlockSpec((CHUNK, wide), lambda g, n: (rev(n), g))
    tile = pltpu.VMEM((HGRN_HEADS, CHUNK, HD_B), F32)
    rows = lambda width: pl.BlockSpec((CHUNK, width), lambda g, n: (rev(n), 0))
    return _call(
        body, rider, name="hgrn_bwd", grid=(G, nC),
        in_specs=[col(0), col(1), col(2), col(3), pl.BlockSpec((2, wide), lambda g, n: (0, g)),
                  pl.BlockSpec((1, HD_B), lambda g, n: (0, 0)), blk,
                  pl.BlockSpec((HGRN_HEADS, None, HD_B, HD_B), lambda g, n: (g, rev(n), 0, 0)),
                  pl.BlockSpec((CHUNK, wide), lambda g, n: (rev(n), dcol0 + g)), rows(wa), rows(wa), rows(wa)],
        out_specs=[rows(3 * wa + 4 * W), pl.BlockSpec((1, wide), lambda g, n: (0, g)),
                   pl.BlockSpec((1, HD_B), lambda g, n: (0, 0))],
        out_shape=[jax.ShapeDtypeStruct((T, 3 * wa + 4 * W), BF16), jax.ShapeDtypeStruct((1, W), F32),
                   jax.ShapeDtypeStruct((1, HD_B), F32)],
        scratch_shapes=[pltpu.VMEM((HGRN_HEADS, HD_B, HD_B), F32)] + [tile] * 7,
        compiler_params=_params(("arbitrary", "arbitrary")),
        operands=(proj, proj, proj, proj, lb_logits, gnorm_g, o_b, st_all, dmixin, *d_attn))


def _adamw_math(g, w, m, v):
    m = B1 * m + (1.0 - B1) * g
    v = B2 * v + (1.0 - B2) * (g * g)
    m_hat = m / (1.0 - B1 ** STEP)
    v_hat = v / (1.0 - B2 ** STEP)
    return -LR * (m_hat / (jnp.sqrt(v_hat) + ADAM_EPS) + WD * w), m, v


def _adamw(g, w, m, v, name):
    R, C = g.shape
    tr = _row_tile(R, C)

    def body(g_ref, w_ref, m_ref, v_ref, go_ref, d_ref, mo_ref, vo_ref):
        g = g_ref[...]
        go_ref[...] = g
        d_ref[...], mo_ref[...], vo_ref[...] = _adamw_math(g, w_ref[...], m_ref[...], v_ref[...])

    blk = pl.BlockSpec((tr, C), lambda i: (i, 0))
    return pl.pallas_call(
        body, name=name, grid=(R // tr,), in_specs=[blk] * 4, out_specs=[blk] * 4,
        out_shape=[jax.ShapeDtypeStruct((R, C), F32)] * 4, compiler_params=_params(("parallel",), 40),
    )(g, w, m, v)


def _sum_pair(g_full, from_sibling, sel, name):
    Q, K, Ns = g_full.shape
    kh = K // 2
    tr = _row_tile(kh, Ns)
    nh = kh // tr

    def body(sel_ref, a_ref, b_ref, o_ref):
        o_ref[...] = (a_ref[...].astype(F32) + b_ref[...].astype(F32)).astype(o_ref.dtype)

    return pl.pallas_call(
        body, name=name,
        grid_spec=pltpu.PrefetchScalarGridSpec(
            num_scalar_prefetch=1, grid=(Q, nh),
            in_specs=[pl.BlockSpec((None, tr, Ns), lambda q, i, sel: (q, sel[1] * nh + i, 0)),
                      pl.BlockSpec((None, tr, Ns), lambda q, i, sel: (q, i, 0))],
            out_specs=pl.BlockSpec((None, tr, Ns), lambda q, i, sel: (q, i, 0))),
        out_shape=jax.ShapeDtypeStruct((Q, kh, Ns), BF16), compiler_params=_params(("parallel", "parallel")),
    )(sel, g_full, from_sibling)


def _sum_chips(pair_sum, from_chips, sel, name):
    Q, kh, Ns = pair_sum.shape
    tr = _row_tile(kh, Ns)
    nh = kh // tr

    def body(sel_ref, a_ref, b0_ref, b1_ref, b2_ref, o_ref):
        up = lambda r: r[...].astype(F32)
        o_ref[...] = ((up(a_ref) + up(b0_ref)) + up(b1_ref)) + up(b2_ref)

    recv = lambda k: pl.BlockSpec((None, tr, Ns), lambda i, sel: (k, i, 0))
    return pl.pallas_call(
        body, name=name,
        grid_spec=pltpu.PrefetchScalarGridSpec(
            num_scalar_prefetch=1, grid=(nh,),
            in_specs=[pl.BlockSpec((None, tr, Ns), lambda i, sel: (sel[0], i, 0)), recv(0), recv(1), recv(2)],
            out_specs=pl.BlockSpec((tr, Ns), lambda i, sel: (sel[1] * nh + i, 0))),
        out_shape=jax.ShapeDtypeStruct((2 * kh, Ns), F32), compiler_params=_params(("parallel",)),
    )(sel, pair_sum, from_chips, from_chips, from_chips)


def _gather_small(v, name):
    R, L = v.shape

    def body(v_ref, out_ref, send_sems, recv_sems):
        x, y, c = _place()
        me = 4 * x + 2 * y + c
        out_ref[me] = v_ref[...]
        peers = [(_flip(x, k >> 2 & 1), _flip(y, k >> 1 & 1), _flip(c, k & 1)) for k in range(1, N_DEV)]

        def copy(k, row, to):
            return pltpu.make_async_remote_copy(src_ref=v_ref, dst_ref=out_ref.at[row], send_sem=send_sems.at[k],
                                                recv_sem=recv_sems.at[k], device_id=to, device_id_type=MESH)

        sends = [copy(k, me, peer) for k, peer in enumerate(peers)]
        for cp in sends:
            cp.start()
        for k, (px, py, pc) in enumerate(peers):
            copy(k, 4 * px + 2 * py + pc, (x, y, c)).wait_recv()
        for cp in sends:
            cp.wait_send()

    vmem = pl.BlockSpec(memory_space=pltpu.VMEM)
    return pl.pallas_call(
        body, name=name, in_specs=[vmem], out_specs=vmem, out_shape=jax.ShapeDtypeStruct((N_DEV, R, L), F32),
        scratch_shapes=[pltpu.SemaphoreType.DMA((N_DEV - 1,)), pltpu.SemaphoreType.DMA((N_DEV - 1,))],
    )(v)


def _small_rider(v):
    def copies(ins, outs, send_sems, recv_sems):
        x, y, c = _place()
        peers = [(_flip(x, k >> 2 & 1), _flip(y, k >> 1 & 1), _flip(c, k & 1)) for k in range(1, N_DEV)]

        def copy(k, row, to):
            return pltpu.make_async_remote_copy(src_ref=ins[0], dst_ref=outs[0].at[row], send_sem=send_sems.at[k],
                                                recv_sem=recv_sems.at[k], device_id=to, device_id_type=MESH)

        sends = [copy(k, 4 * x + 2 * y + c, peer) for k, peer in enumerate(peers)]
        return sends, [copy(k, 4 * px + 2 * py + pc, (x, y, c)) for k, (px, py, pc) in enumerate(peers)]

    def start(*refs):
        for cp in copies(*refs)[0]:
            cp.start()

    def finish(*refs):
        sends, arrivals = copies(*refs)
        for cp in arrivals:
            cp.wait_recv()
        for cp in sends:
            cp.wait_send()

    return _Rider([v], [jax.ShapeDtypeStruct((N_DEV, *v.shape), F32)], N_DEV - 1, start, finish)


def _silu(v):
    return v * jax.nn.sigmoid(v)


def _ada_fwd(c_all, w_ada, rider, tn=512):
    M, D = c_all.shape
    Ns = w_ada.shape[1]
    steps = Ns // tn

    def body(c_ref, w_ref, o_ref, all_ref, send_sems, recv_sems):
        j = pl.program_id(0)
        o_ref[:, pl.ds(pl.multiple_of(j * tn, tn), tn)] = _dot(_silu(c_ref[...]).astype(BF16), w_ref[...].astype(BF16))

        @pl.when(j == steps - 1)
        def _():
            x, y, c = _place()
            peers = [(_flip(x, k >> 2 & 1), _flip(y, k >> 1 & 1), _flip(c, k & 1)) for k in range(1, N_DEV)]

            def copy(k, row, to):
                return pltpu.make_async_remote_copy(
                    src_ref=o_ref.at[pl.ds(0, N_DEV)], dst_ref=all_ref.at[row], send_sem=send_sems.at[k],
                    recv_sem=recv_sems.at[k], device_id=to, device_id_type=MESH)

            sends = [copy(k, 4 * x + 2 * y + c, peer) for k, peer in enumerate(peers)]
            for cp in sends:
                cp.start()
            for k, (px, py, pc) in enumerate(peers):
                copy(k, 4 * px + 2 * py + pc, (x, y, c)).wait_recv()
            for cp in sends:
                cp.wait_send()

    (out, out_all), rode = _call(
        body, rider, name="ada_fwd", grid=(steps,),
        in_specs=[pl.BlockSpec((M, D), lambda j: (0, 0)), pl.BlockSpec((D, tn), lambda j: (0, j))],
        out_specs=[pl.BlockSpec((M, Ns), lambda j: (0, 0)), ANY],
        out_shape=[jax.ShapeDtypeStruct((M, Ns), F32), jax.ShapeDtypeStruct((N_DEV, N_DEV, Ns), F32)],
        scratch_shapes=[pltpu.SemaphoreType.DMA((N_DEV - 1,)), pltpu.SemaphoreType.DMA((N_DEV - 1,))],
        compiler_params=_params(("arbitrary",)), operands=(c_all, w_ada))
    return out, out_all, rode


def _ada_bwd(c_all, dmod, w, m, v, tk=256, tn=1536):
    M, D = c_all.shape
    Ns = dmod.shape[1]

    def body(c_ref, d_ref, w_ref, m_ref, v_ref, g_ref, dl_ref, mo_ref, vo_ref):
        g = _dot(_silu(c_ref[...]).astype(BF16), d_ref[...].astype(BF16), TN)
        g_ref[...] = g
        dl_ref[...], mo_ref[...], vo_ref[...] = _adamw_math(g, w_ref[...], m_ref[...], v_ref[...])

    blk = pl.BlockSpec((tk, tn), lambda i, j: (i, j))
    return pl.pallas_call(
        body, name="ada_bwd", grid=(D // tk, Ns // tn),
        in_specs=[pl.BlockSpec((M, tk), lambda i, j: (0, i)), pl.BlockSpec((M, tn), lambda i, j: (0, j)), blk, blk, blk],
        out_specs=[blk] * 4, out_shape=[jax.ShapeDtypeStruct((D, Ns), F32)] * 4,
        compiler_params=_params(("parallel", "parallel"), 40),
    )(c_all, dmod, w, m, v)


def _small_update(g_all, w, m, v):
    R, L = w.shape

    def body(g_ref, w_ref, m_ref, v_ref, go_ref, d_ref, mo_ref, vo_ref):
        g = g_ref[0]
        for d in range(1, N_DEV):
            g = g + g_ref[d]
        go_ref[...] = g
        d_ref[...], mo_ref[...], vo_ref[...] = _adamw_math(g, w_ref[...], m_ref[...], v_ref[...])

    return pl.pallas_call(body, name="small_update", out_shape=[jax.ShapeDtypeStruct((R, L), F32)] * 4)(g_all, w, m, v)


def _pack(parts, rows):
    flat = jnp.concatenate([p.reshape(-1) for p in parts])
    return jnp.pad(flat, (0, rows * 128 - flat.shape[0])).reshape(rows, 128)


def _unpack(packed, shapes):
    flat, out, at = packed.reshape(-1), [], 0
    for shp in shapes:
        size = 1
        for d in shp:
            size *= d
        out.append(flat[at:at + size].reshape(shp))
        at += size
    return out


def _layer(x, tgt, mod, wts, rel_bias, attn_norm_g, lb_logits, gnorm_g, ln1_g, ln1_b, ln2_g, ln2_b, place=None):
    T, D = x.shape
    aw = attn_norm_g.shape[1]
    shift1, scale1, gate1, shift2, scale2, gate2 = [mod[i:i + 1] for i in range(6)]

    def gather(n, rows=None, into=None, before=None, last=True):
        return None if place is None else _gather_rider(wts[n], rows, None if into is None else into[0], before, last)

    def gathered(n, rode):
        return wts[n] if place is None else lax.dynamic_update_index_in_dim(rode[0], wts[n], place[0], 0)

    def blocks(g):
        return g.reshape(N_CHIPS, -1, g.shape[2])

    def to_sibling(g):
        return None if place is None else _pair_rider(g)

    def pair_sum(n, g, rode=None):
        if place is None:
            return g
        rode = _alone(_pair_rider(g), n + "_send_pair") if rode is None else rode
        return _sum_pair(g, rode[0], place[1], n + "_sum_pair")

    def to_chips(p, rows=None, into=None):
        return None if place is None else _chips_rider(p, rows, None if into is None else into[0])

    def summed(n, p, rode):
        return p if place is None else _sum_chips(p, rode[0], place[1], n + "_sum_chips")

    def to_both(block):
        return None if place is None else _share_rider(block)

    def carrying(mm, *args, rider, **kw):
        return mm(*args, rider=rider, **kw) if rider is not None else (mm(*args, **kw), None)

    def to_sibling_acts(a, b):
        return None if place is None else _acts_rider(a, b)

    def pair_grad(name, a, b, tn, rider, arrived=None, late_rider=None):
        if place is None:
            return _mm_tn(a, b, q=N_CHIPS, tk=512, tn=tn, tt=T, name=name), None
        kh = a.shape[1] // 2
        mine = lax.dynamic_slice_in_dim(a, place[1][1] * kh, kh, axis=1)
        part, rode = carrying(_mm_tn, mine, b, q=N_CHIPS, tk=512, tn=tn, tt=T, name=name + "_own",
                              rider=_join(None if arrived else _acts_rider(a, b), rider))
        (a_sib, b_sib), rode = arrived or rode[:2], rode if arrived else rode[2:]
        out, late = _mm_tn_add(a_sib, b_sib, part, tk=512, tn=tn, name=name + "_sib", rider=late_rider)
        return out, (rode or []) + late

    if place is None:
        h1, rode = _pre_mixer(x, scale1, shift1), None
    else:
        h1, rode = _pre_mixer(x, scale1, shift1, wts["w_in_last_part"])
    w_in = gathered("w_in", rode)
    n_qkv = 3 * aw // 256
    kh_o, kh_f, kh_out = [wts[n].shape[-2] // 2 for n in ("w_o", "w_ffn_in", "w_ffn_out")]
    o_cut, f_cuts, out_cut = 3 * kh_o // 8, (7 * kh_f // 16, 7 * kh_f // 8), kh_out // 11
    qkv, rode = carrying(_mm_nn, h1, w_in, tm=ZPAD, tn=256, tk=D, name="proj_qkv", cols=(0, n_qkv), o_dtype=BF16,
                         pad_rows=ZPAD, rider=gather("w_o", (0, o_cut), last=False))
    proj, rode = carrying(_mm_nn, h1, w_in, tm=2048, tn=256, tk=D, name="proj_rec",
                          cols=(n_qkv, N_CHIPS * w_in.shape[2] // 256),
                          rider=gather("w_o", (o_cut, kh_o - o_cut), rode, before=(0, o_cut)))
    w_o3 = gathered("w_o", rode).reshape(1, D, D)
    bias = _bias_band(rel_bias)
    (mix_a, probs), rode = _attn_fwd(qkv, bias, attn_norm_g, rider=gather("w_ffn_in", (0, f_cuts[0]), last=False))
    (mix_b, o_b, st_all), rode = _hgrn_fwd(
        proj, lb_logits, gnorm_g,
        rider=gather("w_ffn_in", (f_cuts[0], f_cuts[1] - f_cuts[0]), rode, before=(0, f_cuts[0]), last=False))
    mixin = jnp.concatenate([mix_a, mix_b], axis=1)
    mix = _mm_nn(mixin, w_o3, tm=1024, tn=512, tk=D, name="mix_out")
    if place is None:
        x1, h2 = _post_mixer(mix, x, gate1, ln1_g, ln1_b, scale2, shift2)
    else:
        (x1, h2), rode = _post_mixer(mix, x, gate1, ln1_g, ln1_b, scale2, shift2, rider=_join(
            gather("w_ffn_in", (f_cuts[1], kh_f - f_cuts[1]), rode, before=(f_cuts[0], f_cuts[1] - f_cuts[0])),
            gather("w_ffn_out", (0, out_cut), last=False)))
    w_ffn_in = gathered("w_ffn_in", rode)
    (gate, up, act), rode = _ffn_in_swiglu(
        h2, w_ffn_in, tm=2048, tn=256,
        rider=gather("w_ffn_out", (out_cut, kh_out - out_cut), rode and rode[1:], before=(0, out_cut)))
    w_out3 = gathered("w_ffn_out", rode)
    w_out3 = w_out3.reshape(1, -1, w_out3.shape[2])
    d_ff = w_out3.shape[1]
    f = _mm_nn(act, w_out3, tm=1024, tn=512, tk=d_ff, name="ffn_out")
    du2, df, acc2 = _loss_head(f, x1, tgt, gate2, ln2_g, ln2_b)
    loss = (0.5 / D) * jnp.sum(acc2[3])
    g = blocks(_mm_tn(act, df, q=1, tk=512, tn=1024, tt=T, name="g_ffn_out"))
    (dff,), rode = _d_act_swiglu(df, w_out3, gate, up, tm=1024, to=512, rider=to_sibling(g))
    p_out = pair_sum("w_ffn_out", g, rode)
    cut = 25 * p_out.shape[1] // 44
    dh2, rode = carrying(_mm_nt, dff, w_ffn_in, tm=1024, to=1024, tn=w_ffn_in.shape[2], name="d_h2",
                         rider=_join(to_chips(p_out, (0, cut)), to_sibling_acts(h2, dff)))
    p_fin, rode = pair_grad("g_ffn_in", h2, dff, w_ffn_in.shape[2] // 2,
                            to_chips(p_out, (cut, p_out.shape[1] - cut), rode), arrived=rode and rode[1:])
    g_ffn_out = summed("w_ffn_out", p_out, rode)
    if place is None:
        du1, dmix, acc1 = _mid_bwd(dh2, du2, x1, mix, x, gate1, ln1_g, scale2)
    else:
        (du1, dmix, acc1), (g_ffn_out,) = _mid_bwd(dh2, du2, x1, mix, x, gate1, ln1_g, scale2, rider=to_both(g_ffn_out))
    g = blocks(_mm_tn(mixin, dmix, q=1, tk=512, tn=1024, tt=T, name="g_o"))
    dmixin, rode = carrying(_mm_nt, dmix, w_o3, tm=1024, to=512, tn=D, name="d_mixin", rider=to_sibling(g))
    p_o = pair_sum("w_o", g, rode)
    cut = p_fin.shape[1] // 2
    (dq, dk, dv, dbias, dgain), rode = _attn_bwd(qkv, probs, attn_norm_g, dmixin, rider=to_chips(p_fin, (0, cut)))
    (dproj, dl0, dgn), rode = _hgrn_bwd(
        proj, lb_logits, gnorm_g, o_b, st_all, dmixin, (dq, dk, dv),
        rider=_join(to_chips(p_fin, (cut, p_fin.shape[1] - cut), rode), to_chips(p_o)))
    g_ffn_in, g_o = summed("w_ffn_in", p_fin, rode[:1]), summed("w_o", p_o, rode[1:])
    p_in, rode = pair_grad("g_in", h1, dproj, w_in.shape[2] // 2, None,
                           late_rider=_join(to_both(g_ffn_in), to_both(g_o)))
    if place is not None:
        g_ffn_in, g_o = rode
    cut = 3 * p_in.shape[1] // 4
    dh1, rode = carrying(_mm_nt, dproj, w_in, tm=1024, to=1024, tn=w_in.shape[2], name="d_h1",
                         rider=to_chips(p_in, (0, cut)))
    if place is None:
        (grad_x, acc0), g_in = _first_bwd(dh1, du1, x, scale1), p_in
    else:
        (grad_x, acc0), rode = _first_bwd(dh1, du1, x, scale1, rider=to_chips(p_in, (cut, p_in.shape[1] - cut), rode))
        g_in, = _alone(to_both(summed("w_in", p_in, rode)), "w_in_share")
    dmod = jnp.concatenate([acc0[1:2], acc0[0:1], acc1[4:5], acc1[1:2], acc1[0:1], acc2[2:3]], axis=0)
    small = dict(rel_bias=_bias_band_grad(dbias), attn_norm_g=dgain,
                 lb_logits=jnp.concatenate([dl0, -dl0], axis=0), gnorm_g=dgn,
                 ln1_g=acc1[2:3], ln1_b=acc1[3:4], ln2_g=acc2[0:1], ln2_b=acc2[1:2])
    return loss, grad_x, dict(w_in=g_in, w_o=g_o, w_ffn_in=g_ffn_in, w_ffn_out=g_ffn_out), dmod, small


SMALL = ("rel_bias", "attn_norm_g", "lb_logits", "gnorm_g", "ln1_g", "ln1_b", "ln2_g", "ln2_b")
SMALL_ROWS = 256


def kernel(x, c, w_ada, b_ada, w_in, rel_bias, attn_norm_g, lb_logits, gnorm_g, w_o, ln1_g, ln1_b, w_ffn_in, w_ffn_out, ln2_g, ln2_b, loss_target, m_w_ada, m_b_ada, m_w_in, m_rel_bias, m_attn_norm_g, m_lb_logits, m_gnorm_g, m_w_o, m_ln1_g, m_ln1_b, m_w_ffn_in, m_w_ffn_out, m_ln2_g, m_ln2_b, v_w_ada, v_b_ada, v_w_in, v_rel_bias, v_attn_norm_g, v_lb_logits, v_gnorm_g, v_w_o, v_ln1_g, v_ln1_b, v_w_ffn_in, v_w_ffn_out, v_ln2_g, v_ln2_b):
    mx, my, mc = _place()
    me = 4 * mx + 2 * my + mc
    chip = 2 * mx + my
    sel = jnp.stack([chip, mc]).astype(jnp.int32)
    D = x.shape[2]
    ns_ada = w_ada.shape[2]

    big = dict(w_in=(w_in, m_w_in, v_w_in), w_o=(w_o, m_w_o, v_w_o), w_ffn_in=(w_ffn_in, m_w_ffn_in, v_w_ffn_in),
               w_ffn_out=(w_ffn_out, m_w_ffn_out, v_w_ffn_out))
    shards = dict(w_in=w_in[0].astype(BF16))
    kh = shards["w_in"].shape[0] // 2
    cuts = [part * kh // 32 for part in (0, 14, 20, 23, 27, 32)]
    spans = [(a, b - a) for a, b in zip(cuts, cuts[1:])]

    def w_in_part(i, rode):
        return _gather_rider(shards["w_in"], spans[i], rode and rode[0], spans[i - 1] if i else None, last=i == 4,
                             relay=True)

    c_own = c.reshape(D // 128, 128)
    rode, c_all = None, None
    for i, n in enumerate(("w_ffn_in", "w_ffn_out", "w_o")):
        rider = _join(w_in_part(i, rode), None if i else _small_rider(c_own))
        (shards[n],), rode = _to_bf16(big[n][0][0], "cast_" + n, rider)
        c_all = c_all if i else rode[1]

    c_all = lax.dynamic_update_index_in_dim(c_all, c_own, me, 0).reshape(N_DEV, D)
    c_all = jnp.pad(c_all, ((0, 16 - N_DEV), (0, 0)))
    mod_cols, mod_all, rode = _ada_fwd(c_all, w_ada[0], w_in_part(3, rode))
    shards["w_in_last_part"] = w_in_part(4, rode)
    mod_all = lax.dynamic_update_index_in_dim(mod_all, mod_cols[:N_DEV], me, 0)
    mod = lax.dynamic_index_in_dim(mod_all[::2], me, axis=1, keepdims=False)
    mod = (mod.reshape(1, -1) + b_ada).reshape(6, D)

    loss, grad_x, g_big, dmod, g_small = _layer(
        x[0], loss_target[0], mod, shards, rel_bias[0], attn_norm_g, lb_logits, gnorm_g, ln1_g, ln1_b, ln2_g, ln2_b,
        place=(chip, sel))

    grads, deltas, new_m, new_v = {}, {}, {}, {}
    for n, (w, m, v) in big.items():
        g, d, mo, vo = _adamw(g_big[n], w[0], m[0], v[0], "adamw_" + n)
        grads[n], deltas[n], new_m[n], new_v[n] = g[None], d[None], mo[None], vo[None]

    small_in = dict(rel_bias=(rel_bias, m_rel_bias, v_rel_bias), attn_norm_g=(attn_norm_g, m_attn_norm_g, v_attn_norm_g),
                    lb_logits=(lb_logits, m_lb_logits, v_lb_logits), gnorm_g=(gnorm_g, m_gnorm_g, v_gnorm_g),
                    ln1_g=(ln1_g, m_ln1_g, v_ln1_g), ln1_b=(ln1_b, m_ln1_b, v_ln1_b), ln2_g=(ln2_g, m_ln2_g, v_ln2_g),
                    ln2_b=(ln2_b, m_ln2_b, v_ln2_b))
    g_all = _gather_small(_pack([dmod] + [g_small[n] for n in SMALL] + [loss], SMALL_ROWS), "gather_small")
    packed = [_pack([t] + [small_in[n][i] for n in SMALL] + [jnp.zeros((), F32)], SMALL_ROWS)
              for i, t in enumerate((b_ada, m_b_ada, v_b_ada))]
    shapes = [b_ada.shape] + [small_in[n][0].shape for n in SMALL] + [()]
    outs = [_unpack(o, shapes) for o in _small_update(g_all, *packed)]
    loss = outs[0][-1]
    for i, n in enumerate(("b_ada",) + SMALL):
        grads[n], deltas[n], new_m[n], new_v[n] = outs[0][i], outs[1][i], outs[2][i], outs[3][i]

    dmod_all = g_all[:, :6 * D // 128].reshape(N_DEV, 6 * D)
    dmod_cols = lax.dynamic_slice_in_dim(dmod_all, chip * ns_ada, ns_ada, axis=1)
    dmod_cols = jnp.pad(dmod_cols, ((0, 16 - N_DEV), (0, 0)))
    g, d, mo, vo = _ada_bwd(c_all, dmod_cols, w_ada[0], m_w_ada[0], v_w_ada[0])
    grads["w_ada"], deltas["w_ada"], new_m["w_ada"], new_v["w_ada"] = g[None], d[None], mo[None], vo[None]

    order = ("w_ada", "b_ada", "w_in", "rel_bias", "attn_norm_g", "lb_logits", "gnorm_g", "w_o", "ln1_g", "ln1_b",
             "w_ffn_in", "w_ffn_out", "ln2_g", "ln2_b")
    return (loss, grad_x[None], *[grads[n] for n in order], *[deltas[n] for n in order],
            *[new_m[n] for n in order], *[new_v[n] for n in order])
```
